```python
import math
import jax, jax.numpy as jnp
from jax import lax
import numpy as np

D_MODEL = 1024
BATCH = 8
SEQ = 2048
DEPTH = 2

D_MIX = D_MODEL
HEAD_DIM = 64
N_ATT_HEADS = 8
D_ATT = N_ATT_HEADS * HEAD_DIM
DILATED_PATTERNS = ((128, 1), (512, 4), (2048, 16))
ATT_BLOCK = 128
SSM_GROUP = 16
D_SSM = D_MIX // 4
N_SSM_GROUPS = D_SSM // SSM_GROUP
SSM_STATE = 64
POOL_WINDOWS = (2, 4, 8, 16)
D_POOL = D_MIX - D_ATT - D_SSM
POOL_GROUP = D_POOL // len(POOL_WINDOWS)
D_IN = 3 * D_ATT + D_SSM + D_POOL
IN_SPLITS = (D_ATT, 2 * D_ATT, 3 * D_ATT, 3 * D_ATT + D_SSM)
D_FF = 256 * int(math.ceil(8 * D_MODEL / 3 / 256))
N_BUCKETS = 32
MAX_DISTANCE = 2048
ALPHA = (2 * DEPTH) ** 0.25
BETA = (8 * DEPTH) ** -0.25
FFN_RES = 0.5
LN_EPS = 1e-5
NEG = -1e30

kernel_name = "hybrid_dilated_attn_s5_pool_macaron_deepnorm"


def _layernorm(x):
    xf = x.astype(jnp.float32)
    mu = xf.mean(-1, keepdims=True)
    var = jnp.square(xf - mu).mean(-1, keepdims=True)
    return ((xf - mu) * lax.rsqrt(var + LN_EPS)).astype(x.dtype)


def _layernorm_affine(x, gain, bias):
    return _layernorm(x) * gain + bias


def _modulate(x, shift, scale):
    return _layernorm(x) * (1.0 + scale) + shift


def _swiglu(h, w_gate, w_up, w_down):
    return (jax.nn.silu(h @ w_gate) * (h @ w_up)) @ w_down


def _t5_bucket(dist):
    max_exact = N_BUCKETS // 2
    d = np.maximum(dist, 1).astype(np.float32)
    large = max_exact + (np.log(d / max_exact) / math.log(MAX_DISTANCE / max_exact)
                         * (N_BUCKETS - max_exact)).astype(np.int32)
    large = np.minimum(large, N_BUCKETS - 1)
    return np.where(dist < max_exact, dist, large).astype(np.int32)


def _dilated_branch(q, k, v, rel_bias, window, dilation):
    B, S, H, E = q.shape
    Q = ATT_BLOCK
    n_keys = window // dilation
    L = S // dilation
    nb = -(-L // Q)
    Lp = nb * Q

    def by_residue(t):
        t = t.reshape(B, L, dilation, H, E).transpose(0, 2, 1, 3, 4)
        return jnp.pad(t, ((0, 0), (0, 0), (0, Lp - L), (0, 0), (0, 0)))

    def band(t):
        t = jnp.pad(t, ((0, 0), (0, 0), (Q, 0), (0, 0), (0, 0))).reshape(B, dilation, nb + 1, Q, H, E)
        return jnp.concatenate([t[:, :, :-1], t[:, :, 1:]], axis=3)

    qb = by_residue(q).reshape(B, dilation, nb, Q, H, E)
    kb = band(by_residue(k))
    vb = band(by_residue(v))

    i = np.arange(Q)[:, None]
    j = np.arange(2 * Q)[None, :]
    r = i + Q - j
    in_band = (r >= 0) & (r <= n_keys)
    k_abs = np.arange(nb)[:, None, None] * Q + j[None] - Q
    valid = (in_band[None] & (k_abs >= 0))[:, None]
    bucket = _t5_bucket(np.clip(r, 0, None) * dilation)
    bias = jnp.transpose(rel_bias[bucket], (2, 0, 1)).astype(jnp.float32)

    s = jnp.einsum('brnqhe,brnkhe->brnhqk', qb, kb, preferred_element_type=jnp.float32)
    s = jnp.where(valid, s + bias, NEG)
    m = s.max(-1, keepdims=True)
    p = jnp.exp(s - m)
    den = p.sum(-1, keepdims=True)
    o = jnp.einsum('brnhqk,brnkhe->brnqhe', p, vb.astype(jnp.float32))
    o = o / jnp.swapaxes(den, 3, 4)
    lse = jnp.swapaxes((m + jnp.log(den))[..., 0], 3, 4)

    o = o.reshape(B, dilation, Lp, H, E)[:, :, :L].transpose(0, 2, 1, 3, 4).reshape(B, S, H, E)
    lse = lse.reshape(B, dilation, Lp, H)[:, :, :L].transpose(0, 2, 1, 3).reshape(B, S, H)
    return o, lse


def _dilated_attention(q, k, v, rel_bias):
    outs, lses = [], []
    for window, dilation in DILATED_PATTERNS:
        o, lse = _dilated_branch(q, k, v, rel_bias, window, dilation)
        outs.append(o)
        lses.append(lse)
    w = jax.nn.softmax(jnp.stack(lses, 0), axis=0)
    return jnp.einsum('pbsh,pbshe->bshe', w, jnp.stack(outs, 0))


def _s5(u, a_re, a_im, log_dt, b_re, b_im, c_re, c_im, d_skip, glu_w, glu_b):
    Bsz, S, _ = u.shape
    f32 = jnp.float32
    lam = lax.complex(a_re.astype(f32), a_im.astype(f32))
    dt = jnp.exp(log_dt.astype(f32))[:, None]
    a_bar = jnp.exp(lam * dt)
    b_bar = ((a_bar - 1.0) / lam)[:, :, None] * lax.complex(b_re.astype(f32), b_im.astype(f32))
    uf = u.astype(f32)
    bu = jnp.einsum('bsgc,gpc->bsgp', uf.reshape(Bsz, S, N_SSM_GROUPS, SSM_GROUP), b_bar)
    a_full = jnp.broadcast_to(a_bar, bu.shape)

    def combine(e1, e2):
        a1, b1 = e1
        a2, b2 = e2
        return a2 * a1, a2 * b1 + b2

    _, states = lax.associative_scan(combine, (a_full, bu), axis=1)
    cm = lax.complex(c_re.astype(f32), c_im.astype(f32))
    y = jnp.einsum('gcp,bsgp->bsgc', cm, states).real.reshape(Bsz, S, D_SSM)
    y = y + d_skip.astype(f32) * uf
    return y * jax.nn.sigmoid(jax.nn.gelu(y) @ glu_w.astype(f32) + glu_b.astype(f32))


def _pool_mixer(u, pool_w, pool_scale):
    Bsz, S, _ = u.shape
    ug = u.astype(jnp.float32).reshape(Bsz, S, len(POOL_WINDOWS), POOL_GROUP)
    cs = jnp.cumsum(ug, axis=1)
    count = jnp.arange(1, S + 1, dtype=jnp.float32)
    means = []
    for g, w in enumerate(POOL_WINDOWS):
        c_g = cs[:, :, g]
        lagged = jnp.pad(c_g[:, :-w], ((0, 0), (w, 0), (0, 0)))
        means.append((c_g - lagged) / jnp.minimum(count, float(w))[None, :, None])
    pooled = jnp.stack(means, axis=2) - ug
    y = jnp.einsum('bsgc,gcd->bsgd', pooled, pool_w.astype(jnp.float32)).reshape(Bsz, S, D_POOL)
    return y * pool_scale.astype(jnp.float32)


def _hybrid_mixer(h, rel_bias, w_in, w_out, a_re, a_im, log_dt, b_re, b_im, c_re, c_im,
                  d_skip, glu_w, glu_b, pool_w, pool_scale):
    Bsz, S, _ = h.shape
    z = h @ w_in
    q, k, v, u_ssm, u_pool = jnp.split(z, IN_SPLITS, axis=-1)
    heads = lambda t: t.reshape(Bsz, S, N_ATT_HEADS, HEAD_DIM)
    y_att = _dilated_attention(heads(q) * HEAD_DIM ** -0.5, heads(k), heads(v), rel_bias)
    y_att = y_att.reshape(Bsz, S, D_ATT)
    y_ssm = _s5(u_ssm, a_re, a_im, log_dt, b_re, b_im, c_re, c_im, d_skip, glu_w, glu_b)
    y_pool = _pool_mixer(u_pool, pool_w, pool_scale)
    y = jnp.concatenate([y_att.astype(h.dtype), y_ssm.astype(h.dtype), y_pool.astype(h.dtype)], axis=-1)
    return y @ w_out


def _fwd_setup_inputs(seed: int = 0) -> dict:
    key = jax.random.key(seed)
    ks = jax.random.split(key, 26)
    f32 = jnp.float32
    nrm = lambda i, shape, std: std * jax.random.normal(ks[i], shape, f32)
    L, G, P = DEPTH, N_SSM_GROUPS, SSM_STATE
    n = jnp.arange(P, dtype=f32)
    return {
        "x": nrm(0, (BATCH, SEQ, D_MODEL), 1.0),
        "c": nrm(1, (BATCH, D_MODEL), 1.0),
        "rel_bias": nrm(2, (N_BUCKETS, N_ATT_HEADS), 0.1),
        "ada_w": nrm(3, (L, D_MODEL, 9 * D_MODEL), D_MODEL ** -0.5),
        "ada_b": nrm(4, (L, 9 * D_MODEL), 0.02),
        "ln_g": 1.0 + nrm(5, (L, 3, D_MODEL), 0.02),
        "ln_b": nrm(6, (L, 3, D_MODEL), 0.02),
        "ffn_w_gate": nrm(7, (L, 2, D_MODEL, D_FF), D_MODEL ** -0.5),
        "ffn_w_up": nrm(8, (L, 2, D_MODEL, D_FF), D_MODEL ** -0.5),
        "ffn_w_down": nrm(9, (L, 2, D_FF, D_MODEL), BETA * D_FF ** -0.5),
        "w_in": nrm(10, (L, D_MODEL, D_IN), D_MODEL ** -0.5),
        "w_out": nrm(11, (L, D_MIX, D_MODEL), BETA * D_MIX ** -0.5),
        "ssm_a_re": -0.5 + nrm(12, (L, G, P), 0.01),
        "ssm_a_im": math.pi * n + nrm(13, (L, G, P), 0.01),
        "ssm_log_dt": jax.random.uniform(ks[14], (L, G), f32, math.log(1e-3), math.log(1e-1)),
        "ssm_b_re": nrm(15, (L, G, P, SSM_GROUP), (2 * SSM_GROUP) ** -0.5),
        "ssm_b_im": nrm(16, (L, G, P, SSM_GROUP), (2 * SSM_GROUP) ** -0.5),
        "ssm_c_re": nrm(17, (L, G, SSM_GROUP, P), (2 * P) ** -0.5),
        "ssm_c_im": nrm(18, (L, G, SSM_GROUP, P), (2 * P) ** -0.5),
        "ssm_d": nrm(19, (L, D_SSM), 1.0),
        "glu_w": nrm(20, (L, D_SSM, D_SSM), D_SSM ** -0.5),
        "glu_b": nrm(21, (L, D_SSM), 0.02),
        "pool_w": nrm(22, (L, len(POOL_WINDOWS), POOL_GROUP, POOL_GROUP), POOL_GROUP ** -0.5),
        "pool_scale": 1.0 + nrm(23, (L, D_POOL), 0.02),
    }


def _fwd_reference(x, c, rel_bias, ada_w, ada_b, ln_g, ln_b, ffn_w_gate, ffn_w_up, ffn_w_down,
              w_in, w_out, ssm_a_re, ssm_a_im, ssm_log_dt, ssm_b_re, ssm_b_im, ssm_c_re,
              ssm_c_im, ssm_d, glu_w, glu_b, pool_w, pool_scale):
    Bsz = x.shape[0]
    cond = jax.nn.silu(c)
    for l in range(DEPTH):
        mod = (cond @ ada_w[l] + ada_b[l]).reshape(Bsz, 3, 3, 1, D_MODEL)
        h = _modulate(x, mod[:, 0, 0], mod[:, 0, 1])
        f = _swiglu(h, ffn_w_gate[l, 0], ffn_w_up[l, 0], ffn_w_down[l, 0])
        x = _layernorm_affine(ALPHA * x + FFN_RES * mod[:, 0, 2] * f, ln_g[l, 0], ln_b[l, 0])
        h = _modulate(x, mod[:, 1, 0], mod[:, 1, 1])
        y = _hybrid_mixer(h, rel_bias, w_in[l], w_out[l], ssm_a_re[l], ssm_a_im[l], ssm_log_dt[l],
                          ssm_b_re[l], ssm_b_im[l], ssm_c_re[l], ssm_c_im[l], ssm_d[l],
                          glu_w[l], glu_b[l], pool_w[l], pool_scale[l])
        x = _layernorm_affine(ALPHA * x + mod[:, 1, 2] * y, ln_g[l, 1], ln_b[l, 1])
        h = _modulate(x, mod[:, 2, 0], mod[:, 2, 1])
        f = _swiglu(h, ffn_w_gate[l, 1], ffn_w_up[l, 1], ffn_w_down[l, 1])
        x = _layernorm_affine(ALPHA * x + FFN_RES * mod[:, 2, 2] * f, ln_g[l, 2], ln_b[l, 2])
    return x


import jax as _jax
import jax.numpy as _jnp

TWIN_FORMAT = 'train_step'
FWD_PARAMS = ['x', 'c', 'rel_bias', 'ada_w', 'ada_b', 'ln_g', 'ln_b', 'ffn_w_gate', 'ffn_w_up', 'ffn_w_down', 'w_in', 'w_out', 'ssm_a_re', 'ssm_a_im', 'ssm_log_dt', 'ssm_b_re', 'ssm_b_im', 'ssm_c_re', 'ssm_c_im', 'ssm_d', 'glu_w', 'glu_b', 'pool_w', 'pool_scale']
TWIN_WEIGHTS = ['rel_bias', 'ada_w', 'ada_b', 'ln_g', 'ln_b', 'ffn_w_gate', 'ffn_w_up', 'ffn_w_down', 'w_in', 'w_out', 'ssm_a_re', 'ssm_a_im', 'ssm_log_dt', 'ssm_b_re', 'ssm_b_im', 'ssm_c_re', 'ssm_c_im', 'ssm_d', 'glu_w', 'glu_b', 'pool_w', 'pool_scale']
TWIN_DIFF_INPUT = 'x'
TWIN_INPUTS = ['x', 'c', 'rel_bias', 'ada_w', 'ada_b', 'ln_g', 'ln_b', 'ffn_w_gate', 'ffn_w_up', 'ffn_w_down', 'w_in', 'w_out', 'ssm_a_re', 'ssm_a_im', 'ssm_log_dt', 'ssm_b_re', 'ssm_b_im', 'ssm_c_re', 'ssm_c_im', 'ssm_d', 'glu_w', 'glu_b', 'pool_w', 'pool_scale', 'loss_target', 'm_rel_bias', 'm_ada_w', 'm_ada_b', 'm_ln_g', 'm_ln_b', 'm_ffn_w_gate', 'm_ffn_w_up', 'm_ffn_w_down', 'm_w_in', 'm_w_out', 'm_ssm_a_re', 'm_ssm_a_im', 'm_ssm_log_dt', 'm_ssm_b_re', 'm_ssm_b_im', 'm_ssm_c_re', 'm_ssm_c_im', 'm_ssm_d', 'm_glu_w', 'm_glu_b', 'm_pool_w', 'm_pool_scale', 'v_rel_bias', 'v_ada_w', 'v_ada_b', 'v_ln_g', 'v_ln_b', 'v_ffn_w_gate', 'v_ffn_w_up', 'v_ffn_w_down', 'v_w_in', 'v_w_out', 'v_ssm_a_re', 'v_ssm_a_im', 'v_ssm_log_dt', 'v_ssm_b_re', 'v_ssm_b_im', 'v_ssm_c_re', 'v_ssm_c_im', 'v_ssm_d', 'v_glu_w', 'v_glu_b', 'v_pool_w', 'v_pool_scale']
TWIN_OUTPUTS = ['loss', 'grad_x', 'grad_rel_bias', 'grad_ada_w', 'grad_ada_b', 'grad_ln_g', 'grad_ln_b', 'grad_ffn_w_gate', 'grad_ffn_w_up', 'grad_ffn_w_down', 'grad_w_in', 'grad_w_out', 'grad_ssm_a_re', 'grad_ssm_a_im', 'grad_ssm_log_dt', 'grad_ssm_b_re', 'grad_ssm_b_im', 'grad_ssm_c_re', 'grad_ssm_c_im', 'grad_ssm_d', 'grad_glu_w', 'grad_glu_b', 'grad_pool_w', 'grad_pool_scale', 'delta_rel_bias', 'delta_ada_w', 'delta_ada_b', 'delta_ln_g', 'delta_ln_b', 'delta_ffn_w_gate', 'delta_ffn_w_up', 'delta_ffn_w_down', 'delta_w_in', 'delta_w_out', 'delta_ssm_a_re', 'delta_ssm_a_im', 'delta_ssm_log_dt', 'delta_ssm_b_re', 'delta_ssm_b_im', 'delta_ssm_c_re', 'delta_ssm_c_im', 'delta_ssm_d', 'delta_glu_w', 'delta_glu_b', 'delta_pool_w', 'delta_pool_scale', 'new_m_rel_bias', 'new_m_ada_w', 'new_m_ada_b', 'new_m_ln_g', 'new_m_ln_b', 'new_m_ffn_w_gate', 'new_m_ffn_w_up', 'new_m_ffn_w_down', 'new_m_w_in', 'new_m_w_out', 'new_m_ssm_a_re', 'new_m_ssm_a_im', 'new_m_ssm_log_dt', 'new_m_ssm_b_re', 'new_m_ssm_b_im', 'new_m_ssm_c_re', 'new_m_ssm_c_im', 'new_m_ssm_d', 'new_m_glu_w', 'new_m_glu_b', 'new_m_pool_w', 'new_m_pool_scale', 'new_v_rel_bias', 'new_v_ada_w', 'new_v_ada_b', 'new_v_ln_g', 'new_v_ln_b', 'new_v_ffn_w_gate', 'new_v_ffn_w_up', 'new_v_ffn_w_down', 'new_v_w_in', 'new_v_w_out', 'new_v_ssm_a_re', 'new_v_ssm_a_im', 'new_v_ssm_log_dt', 'new_v_ssm_b_re', 'new_v_ssm_b_im', 'new_v_ssm_c_re', 'new_v_ssm_c_im', 'new_v_ssm_d', 'new_v_glu_w', 'new_v_glu_b', 'new_v_pool_w', 'new_v_pool_scale']
TWIN_LEAF_KINDS = {'loss': 'loss', 'grad_x': 'grad_x', 'grad_rel_bias': 'grad_w', 'grad_ada_w': 'grad_w', 'grad_ada_b': 'grad_w', 'grad_ln_g': 'grad_w', 'grad_ln_b': 'grad_w', 'grad_ffn_w_gate': 'grad_w', 'grad_ffn_w_up': 'grad_w', 'grad_ffn_w_down': 'grad_w', 'grad_w_in': 'grad_w', 'grad_w_out': 'grad_w', 'grad_ssm_a_re': 'grad_w', 'grad_ssm_a_im': 'grad_w', 'grad_ssm_log_dt': 'grad_w', 'grad_ssm_b_re': 'grad_w', 'grad_ssm_b_im': 'grad_w', 'grad_ssm_c_re': 'grad_w', 'grad_ssm_c_im': 'grad_w', 'grad_ssm_d': 'grad_w', 'grad_glu_w': 'grad_w', 'grad_glu_b': 'grad_w', 'grad_pool_w': 'grad_w', 'grad_pool_scale': 'grad_w', 'delta_rel_bias': 'delta_w', 'delta_ada_w': 'delta_w', 'delta_ada_b': 'delta_w', 'delta_ln_g': 'delta_w', 'delta_ln_b': 'delta_w', 'delta_ffn_w_gate': 'delta_w', 'delta_ffn_w_up': 'delta_w', 'delta_ffn_w_down': 'delta_w', 'delta_w_in': 'delta_w', 'delta_w_out': 'delta_w', 'delta_ssm_a_re': 'delta_w', 'delta_ssm_a_im': 'delta_w', 'delta_ssm_log_dt': 'delta_w', 'delta_ssm_b_re': 'delta_w', 'delta_ssm_b_im': 'delta_w', 'delta_ssm_c_re': 'delta_w', 'delta_ssm_c_im': 'delta_w', 'delta_ssm_d': 'delta_w', 'delta_glu_w': 'delta_w', 'delta_glu_b': 'delta_w', 'delta_pool_w': 'delta_w', 'delta_pool_scale': 'delta_w', 'new_m_rel_bias': 'new_m', 'new_m_ada_w': 'new_m', 'new_m_ada_b': 'new_m', 'new_m_ln_g': 'new_m', 'new_m_ln_b': 'new_m', 'new_m_ffn_w_gate': 'new_m', 'new_m_ffn_w_up': 'new_m', 'new_m_ffn_w_down': 'new_m', 'new_m_w_in': 'new_m', 'new_m_w_out': 'new_m', 'new_m_ssm_a_re': 'new_m', 'new_m_ssm_a_im': 'new_m', 'new_m_ssm_log_dt': 'new_m', 'new_m_ssm_b_re': 'new_m', 'new_m_ssm_b_im': 'new_m', 'new_m_ssm_c_re': 'new_m', 'new_m_ssm_c_im': 'new_m', 'new_m_ssm_d': 'new_m', 'new_m_glu_w': 'new_m', 'new_m_glu_b': 'new_m', 'new_m_pool_w': 'new_m', 'new_m_pool_scale': 'new_m', 'new_v_rel_bias': 'new_v', 'new_v_ada_w': 'new_v', 'new_v_ada_b': 'new_v', 'new_v_ln_g': 'new_v', 'new_v_ln_b': 'new_v', 'new_v_ffn_w_gate': 'new_v', 'new_v_ffn_w_up': 'new_v', 'new_v_ffn_w_down': 'new_v', 'new_v_w_in': 'new_v', 'new_v_w_out': 'new_v', 'new_v_ssm_a_re': 'new_v', 'new_v_ssm_a_im': 'new_v', 'new_v_ssm_log_dt': 'new_v', 'new_v_ssm_b_re': 'new_v', 'new_v_ssm_b_im': 'new_v', 'new_v_ssm_c_re': 'new_v', 'new_v_ssm_c_im': 'new_v', 'new_v_ssm_d': 'new_v', 'new_v_glu_w': 'new_v', 'new_v_glu_b': 'new_v', 'new_v_pool_w': 'new_v', 'new_v_pool_scale': 'new_v'}


def _forward(args):
    return _fwd_reference(*[args[k] for k in FWD_PARAMS])


def _output_shape():
    out = _jax.eval_shape(lambda: _forward(_fwd_setup_inputs(0)))
    return out.shape, out.dtype

N_MICROBATCH = 1
ADAM_LR = 0.001
ADAM_B1 = 0.9
ADAM_B2 = 0.999
ADAM_EPS = 1e-08
ADAM_WD = 0.01
ADAM_STEP = 10
PER_EXAMPLE_BATCH_AXIS = {'x': 0, 'c': 0, 'loss_target': 0}
SHARED_INPUTS = []
_WEIGHT_DTYPES = {'rel_bias': _jnp.float32, 'ada_w': _jnp.float32, 'ada_b': _jnp.float32, 'ln_g': _jnp.float32, 'ln_b': _jnp.float32, 'ffn_w_gate': _jnp.float32, 'ffn_w_up': _jnp.float32, 'ffn_w_down': _jnp.float32, 'w_in': _jnp.float32, 'w_out': _jnp.float32, 'ssm_a_re': _jnp.float32, 'ssm_a_im': _jnp.float32, 'ssm_log_dt': _jnp.float32, 'ssm_b_re': _jnp.float32, 'ssm_b_im': _jnp.float32, 'ssm_c_re': _jnp.float32, 'ssm_c_im': _jnp.float32, 'ssm_d': _jnp.float32, 'glu_w': _jnp.float32, 'glu_b': _jnp.float32, 'pool_w': _jnp.float32, 'pool_scale': _jnp.float32}
MOMENT_SCALE = {'rel_bias': 1.381380e-02, 'ada_w': 1.463139e-02, 'ada_b': 2.454437e-02, 'ln_g': 6.577738e+00, 'ln_b': 6.011635e-01, 'ffn_w_gate': 9.385929e-03, 'ffn_w_up': 9.181359e-03, 'ffn_w_down': 3.053349e-02, 'w_in': 1.810981e-02, 'w_out': 4.758228e-02, 'ssm_a_re': 1.601228e-03, 'ssm_a_im': 3.088864e-03, 'ssm_log_dt': 7.054890e-01, 'ssm_b_re': 1.242887e-03, 'ssm_b_im': 1.492667e-03, 'ssm_c_re': 3.076460e-03, 'ssm_c_im': 2.775869e-03, 'ssm_d': 2.253352e-02, 'glu_w': 7.358845e-03, 'glu_b': 9.317532e-03, 'pool_w': 2.808888e-02, 'pool_scale': 2.979926e-02}


def _to_microbatches(a, axis):
    t = _jnp.moveaxis(a, axis, 0)
    t = t.reshape((N_MICROBATCH, t.shape[0] // N_MICROBATCH) + t.shape[1:])
    return _jnp.moveaxis(t, 1, axis + 1)


def setup_inputs(seed: int = 0) -> dict:
    inp = _fwd_setup_inputs(seed)
    key = _jax.random.fold_in(_jax.random.key(seed), 7919)
    shape, _ = _output_shape()
    out = dict(inp)
    out["loss_target"] = _jax.random.normal(_jax.random.fold_in(key, 0), shape, _jnp.float32)
    for i, name in enumerate(TWIN_WEIGHTS):
        w = inp[name].astype(_jnp.float32)
        if MOMENT_SCALE is None:
            s = _jnp.sqrt(_jnp.mean(_jnp.square(w)) + 1e-30)
        else:
            s = MOMENT_SCALE[name]
        km, kv = _jax.random.split(_jax.random.fold_in(key, i + 1))
        out[name] = w
        out["m_" + name] = s * _jax.random.normal(km, w.shape, _jnp.float32)
        out["v_" + name] = (s * s) * _jax.random.uniform(kv, w.shape, _jnp.float32, 0.5, 1.5)
    if N_MICROBATCH > 1:
        for name, axis in PER_EXAMPLE_BATCH_AXIS.items():
            out[name] = _to_microbatches(out[name], axis)
    return {'x': out['x'], 'c': out['c'], 'rel_bias': out['rel_bias'], 'ada_w': out['ada_w'], 'ada_b': out['ada_b'], 'ln_g': out['ln_g'], 'ln_b': out['ln_b'], 'ffn_w_gate': out['ffn_w_gate'], 'ffn_w_up': out['ffn_w_up'], 'ffn_w_down': out['ffn_w_down'], 'w_in': out['w_in'], 'w_out': out['w_out'], 'ssm_a_re': out['ssm_a_re'], 'ssm_a_im': out['ssm_a_im'], 'ssm_log_dt': out['ssm_log_dt'], 'ssm_b_re': out['ssm_b_re'], 'ssm_b_im': out['ssm_b_im'], 'ssm_c_re': out['ssm_c_re'], 'ssm_c_im': out['ssm_c_im'], 'ssm_d': out['ssm_d'], 'glu_w': out['glu_w'], 'glu_b': out['glu_b'], 'pool_w': out['pool_w'], 'pool_scale': out['pool_scale'], 'loss_target': out['loss_target'], 'm_rel_bias': out['m_rel_bias'], 'm_ada_w': out['m_ada_w'], 'm_ada_b': out['m_ada_b'], 'm_ln_g': out['m_ln_g'], 'm_ln_b': out['m_ln_b'], 'm_ffn_w_gate': out['m_ffn_w_gate'], 'm_ffn_w_up': out['m_ffn_w_up'], 'm_ffn_w_down': out['m_ffn_w_down'], 'm_w_in': out['m_w_in'], 'm_w_out': out['m_w_out'], 'm_ssm_a_re': out['m_ssm_a_re'], 'm_ssm_a_im': out['m_ssm_a_im'], 'm_ssm_log_dt': out['m_ssm_log_dt'], 'm_ssm_b_re': out['m_ssm_b_re'], 'm_ssm_b_im': out['m_ssm_b_im'], 'm_ssm_c_re': out['m_ssm_c_re'], 'm_ssm_c_im': out['m_ssm_c_im'], 'm_ssm_d': out['m_ssm_d'], 'm_glu_w': out['m_glu_w'], 'm_glu_b': out['m_glu_b'], 'm_pool_w': out['m_pool_w'], 'm_pool_scale': out['m_pool_scale'], 'v_rel_bias': out['v_rel_bias'], 'v_ada_w': out['v_ada_w'], 'v_ada_b': out['v_ada_b'], 'v_ln_g': out['v_ln_g'], 'v_ln_b': out['v_ln_b'], 'v_ffn_w_gate': out['v_ffn_w_gate'], 'v_ffn_w_up': out['v_ffn_w_up'], 'v_ffn_w_down': out['v_ffn_w_down'], 'v_w_in': out['v_w_in'], 'v_w_out': out['v_w_out'], 'v_ssm_a_re': out['v_ssm_a_re'], 'v_ssm_a_im': out['v_ssm_a_im'], 'v_ssm_log_dt': out['v_ssm_log_dt'], 'v_ssm_b_re': out['v_ssm_b_re'], 'v_ssm_b_im': out['v_ssm_b_im'], 'v_ssm_c_re': out['v_ssm_c_re'], 'v_ssm_c_im': out['v_ssm_c_im'], 'v_ssm_d': out['v_ssm_d'], 'v_glu_w': out['v_glu_w'], 'v_glu_b': out['v_glu_b'], 'v_pool_w': out['v_pool_w'], 'v_pool_scale': out['v_pool_scale']}


def _loss(weights, diff, rest, loss_target):
    with _jax.named_scope("forward"):
        args = {**rest, TWIN_DIFF_INPUT: diff, **{k: w.astype(_WEIGHT_DTYPES[k]) for k, w in weights.items()}}
        y = _forward(args)
    with _jax.named_scope("loss_head"):
        err = _jnp.square(y.astype(_jnp.float32) - loss_target)
        return 0.5 * _jnp.sum(_jnp.mean(err, axis=-1)) if err.ndim else 0.5 * err


def _adamw(w, g, m, v):
    m = ADAM_B1 * m + (1.0 - ADAM_B1) * g
    v = ADAM_B2 * v + (1.0 - ADAM_B2) * _jnp.square(g)
    m_hat = m / (1.0 - ADAM_B1 ** ADAM_STEP)
    v_hat = v / (1.0 - ADAM_B2 ** ADAM_STEP)
    delta = -ADAM_LR * (m_hat / (_jnp.sqrt(v_hat) + ADAM_EPS) + ADAM_WD * w)
    return delta, m, v


def reference(x, c, rel_bias, ada_w, ada_b, ln_g, ln_b, ffn_w_gate, ffn_w_up, ffn_w_down, w_in, w_out, ssm_a_re, ssm_a_im, ssm_log_dt, ssm_b_re, ssm_b_im, ssm_c_re, ssm_c_im, ssm_d, glu_w, glu_b, pool_w, pool_scale, loss_target, m_rel_bias, m_ada_w, m_ada_b, m_ln_g, m_ln_b, m_ffn_w_gate, m_ffn_w_up, m_ffn_w_down, m_w_in, m_w_out, m_ssm_a_re, m_ssm_a_im, m_ssm_log_dt, m_ssm_b_re, m_ssm_b_im, m_ssm_c_re, m_ssm_c_im, m_ssm_d, m_glu_w, m_glu_b, m_pool_w, m_pool_scale, v_rel_bias, v_ada_w, v_ada_b, v_ln_g, v_ln_b, v_ffn_w_gate, v_ffn_w_up, v_ffn_w_down, v_w_in, v_w_out, v_ssm_a_re, v_ssm_a_im, v_ssm_log_dt, v_ssm_b_re, v_ssm_b_im, v_ssm_c_re, v_ssm_c_im, v_ssm_d, v_glu_w, v_glu_b, v_pool_w, v_pool_scale):
    given = dict(x=x, c=c, rel_bias=rel_bias, ada_w=ada_w, ada_b=ada_b, ln_g=ln_g, ln_b=ln_b, ffn_w_gate=ffn_w_gate, ffn_w_up=ffn_w_up, ffn_w_down=ffn_w_down, w_in=w_in, w_out=w_out, ssm_a_re=ssm_a_re, ssm_a_im=ssm_a_im, ssm_log_dt=ssm_log_dt, ssm_b_re=ssm_b_re, ssm_b_im=ssm_b_im, ssm_c_re=ssm_c_re, ssm_c_im=ssm_c_im, ssm_d=ssm_d, glu_w=glu_w, glu_b=glu_b, pool_w=pool_w, pool_scale=pool_scale, loss_target=loss_target, m_rel_bias=m_rel_bias, m_ada_w=m_ada_w, m_ada_b=m_ada_b, m_ln_g=m_ln_g, m_ln_b=m_ln_b, m_ffn_w_gate=m_ffn_w_gate, m_ffn_w_up=m_ffn_w_up, m_ffn_w_down=m_ffn_w_down, m_w_in=m_w_in, m_w_out=m_w_out, m_ssm_a_re=m_ssm_a_re, m_ssm_a_im=m_ssm_a_im, m_ssm_log_dt=m_ssm_log_dt, m_ssm_b_re=m_ssm_b_re, m_ssm_b_im=m_ssm_b_im, m_ssm_c_re=m_ssm_c_re, m_ssm_c_im=m_ssm_c_im, m_ssm_d=m_ssm_d, m_glu_w=m_glu_w, m_glu_b=m_glu_b, m_pool_w=m_pool_w, m_pool_scale=m_pool_scale, v_rel_bias=v_rel_bias, v_ada_w=v_ada_w, v_ada_b=v_ada_b, v_ln_g=v_ln_g, v_ln_b=v_ln_b, v_ffn_w_gate=v_ffn_w_gate, v_ffn_w_up=v_ffn_w_up, v_ffn_w_down=v_ffn_w_down, v_w_in=v_w_in, v_w_out=v_w_out, v_ssm_a_re=v_ssm_a_re, v_ssm_a_im=v_ssm_a_im, v_ssm_log_dt=v_ssm_log_dt, v_ssm_b_re=v_ssm_b_re, v_ssm_b_im=v_ssm_b_im, v_ssm_c_re=v_ssm_c_re, v_ssm_c_im=v_ssm_c_im, v_ssm_d=v_ssm_d, v_glu_w=v_glu_w, v_glu_b=v_glu_b, v_pool_w=v_pool_w, v_pool_scale=v_pool_scale)
    weights = {n: given[n] for n in TWIN_WEIGHTS}
    shared = {n: given[n] for n in SHARED_INPUTS}
    per_example = {n: given[n] for n in ['x', 'c']}
    grad_fn = _jax.value_and_grad(_loss, argnums=(0, 1))

    def one_microbatch(ex, loss_target):
        ex = dict(ex)
        diff = ex.pop(TWIN_DIFF_INPUT)
        return grad_fn(weights, diff, {**shared, **ex}, loss_target)

    if N_MICROBATCH == 1:
        loss, (grad_w, grad_x) = one_microbatch(per_example, given["loss_target"])
    else:
        def body(carry, xs):
            loss_sum, grad_sum = carry
            l_k, (gw_k, gx_k) = one_microbatch(xs[0], xs[1])
            with _jax.named_scope("update"):
                return (loss_sum + l_k, _jax.tree.map(_jnp.add, grad_sum, gw_k)), gx_k

        init = (_jnp.zeros((), _jnp.float32), _jax.tree.map(_jnp.zeros_like, weights))
        (loss, grad_w), grad_x = _jax.lax.scan(body, init, (per_example, given["loss_target"]))
    with _jax.named_scope("update"):
        delta_w, new_m, new_v = {}, {}, {}
        for n in TWIN_WEIGHTS:
            delta_w[n], new_m[n], new_v[n] = _adamw(weights[n], grad_w[n], given["m_" + n], given["v_" + n])
    return (loss, grad_x, *[grad_w[n] for n in TWIN_WEIGHTS], *[delta_w[n] for n in TWIN_WEIGHTS],
            *[new_m[n] for n in TWIN_WEIGHTS], *[new_v[n] for n in TWIN_WEIGHTS])
```

```python
import math

import numpy as np
import jax
import jax.numpy as jnp
from jax import lax
from jax.experimental import pallas as pl
from jax.experimental.pallas import tpu as pltpu

F32 = jnp.float32
BF16 = jnp.bfloat16
MXU = jnp.bfloat16

S = 2048
D = 1024
NDEV = 8
DEPTH = 2
D_ATT, D_SSM, D_POOL, D_IN = 512, 256, 256, 2048
N_HEADS = 8
FB = 352
FBP = 384
QB = 128
PATTERNS = ((128, 1), (512, 4), (2048, 16))
POOL_WINDOWS = (2, 4, 8, 16)
N_BUCKETS, MAX_DISTANCE = 32, 2048
ALPHA = (2 * DEPTH) ** 0.25
LN_EPS = 1e-5
NEG = -1e30
LR, B1, B2, EPS, WD, STEP = 0.001, 0.9, 0.999, 1e-08, 0.01, 10

TM = 256
TMM = 512
MIB = 1024 * 1024


def _cp(vmem_mib, sem=None):
    kw = dict(vmem_limit_bytes=vmem_mib * MIB)
    if sem is not None:
        kw["dimension_semantics"] = sem
    return pltpu.CompilerParams(**kw)


def _sds(shape, dtype):
    return jax.ShapeDtypeStruct(shape, dtype)


def _mm(a, b):
    return jnp.dot(a.astype(MXU), b.astype(MXU), preferred_element_type=F32)


def _mm_nt(a, b):
    return lax.dot_general(a.astype(MXU), b.astype(MXU), (((1,), (1,)), ((), ())), preferred_element_type=F32)


def _mm_tn(a, b):
    return lax.dot_general(a.astype(MXU), b.astype(MXU), (((0,), (0,)), ((), ())), preferred_element_type=F32)


def _ln_stats(x):
    mu = jnp.mean(x, axis=-1, keepdims=True)
    xc = x - mu
    var = jnp.mean(xc * xc, axis=-1, keepdims=True)
    rstd = lax.rsqrt(var + LN_EPS)
    return xc * rstd, rstd


def _ln_bwd(dn, n, rstd):
    return rstd * (dn - jnp.mean(dn, axis=-1, keepdims=True) - n * jnp.mean(dn * n, axis=-1, keepdims=True))


def _me():
    return 4 * lax.axis_index("x") + 2 * lax.axis_index("y") + lax.axis_index("c")


def _exchange(arrs, gather, name):
    n = len(arrs)

    def body(*refs):
        ins, outs = refs[:n], refs[n:2 * n]
        send_sems, recv_sems, loc_sems = refs[2 * n:]
        me = _me()

        def slab(a, t):
            return ins[a] if gather else ins[a].at[t]

        def remote(a, k, t, slot):
            return pltpu.make_async_remote_copy(
                src_ref=slab(a, t), dst_ref=outs[a].at[slot],
                send_sem=send_sems.at[a * 7 + k - 1], recv_sem=recv_sems.at[a * 7 + k - 1],
                device_id=(t // 4, (t // 2) % 2, t % 2), device_id_type=pl.DeviceIdType.MESH)

        local = [pltpu.make_async_copy(slab(a, me), outs[a].at[me], loc_sems.at[a]) for a in range(n)]
        for cp in local:
            cp.start()
        for k in range(1, NDEV):
            for a in range(n):
                remote(a, k, (me + k) % NDEV, me).start()
        for k in range(1, NDEV):
            for a in range(n):
                remote(a, k, (me + k) % NDEV, (me - k) % NDEV).wait()
        for cp in local:
            cp.wait()

    anyspec = pl.BlockSpec(memory_space=pl.ANY)
    outs = pl.pallas_call(
        body, name=name,
        out_shape=tuple(_sds((NDEV,) + (a.shape if gather else a.shape[1:]), a.dtype) for a in arrs),
        in_specs=[anyspec] * n, out_specs=tuple([anyspec] * n),
        scratch_shapes=[pltpu.SemaphoreType.DMA((7 * n,)), pltpu.SemaphoreType.DMA((7 * n,)),
                        pltpu.SemaphoreType.DMA((n,))],
    )(*arrs)
    return list(outs)


def _row_spec(cols, tm=TM):
    return pl.BlockSpec((tm, cols), lambda i: (i, 0))


def _full_spec(shape):
    nd = len(shape)
    return pl.BlockSpec(shape, lambda i: (0,) * nd)


def ln_mod_fwd(x, mod, sub, name):
    def body(x_ref, mod_ref, h_ref):
        n, _ = _ln_stats(x_ref[...])
        shift = mod_ref[3 * sub:3 * sub + 1, :]
        scale = mod_ref[3 * sub + 1:3 * sub + 2, :]
        h_ref[...] = (n * (1.0 + scale) + shift).astype(MXU)

    return pl.pallas_call(
        body, name=name, grid=(S // TM,),
        in_specs=[_row_spec(D), _full_spec((9, D))], out_specs=_row_spec(D),
        out_shape=_sds((S, D), MXU), compiler_params=_cp(32, ("arbitrary",)))(x, mod)


def res_ln_fwd(x, f, mod, sub, lng, lnb, w, name):
    def body(x_ref, f_ref, mod_ref, g_ref, b_ref, o_ref):
        gate = mod_ref[3 * sub + 2:3 * sub + 3, :]
        r = ALPHA * x_ref[...] + (w * gate) * f_ref[...]
        n, _ = _ln_stats(r)
        o_ref[...] = n * g_ref[sub:sub + 1, :] + b_ref[sub:sub + 1, :]

    return pl.pallas_call(
        body, name=name, grid=(S // TM,),
        in_specs=[_row_spec(D), _row_spec(D), _full_spec((9, D)), _full_spec((3, D)), _full_spec((3, D))],
        out_specs=_row_spec(D), out_shape=_sds((S, D), F32),
        compiler_params=_cp(32, ("arbitrary",)))(x, f, mod, lng, lnb)


def res_ln_bwd(x, f, mod, sub, lng, dxo, w, name):
    def body(x_ref, f_ref, mod_ref, g_ref, dxo_ref, dxa_ref, df_ref, sums_ref):
        i = pl.program_id(0)
        gate = mod_ref[3 * sub + 2:3 * sub + 3, :]
        fv = f_ref[...]
        r = ALPHA * x_ref[...] + (w * gate) * fv
        n, rstd = _ln_stats(r)
        dxo = dxo_ref[...]
        dr = _ln_bwd(dxo * g_ref[sub:sub + 1, :], n, rstd)
        dxa_ref[...] = ALPHA * dr
        df_ref[...] = ((w * gate) * dr).astype(MXU)
        part = jnp.concatenate([
            jnp.sum(dxo * n, axis=0, keepdims=True),
            jnp.sum(dxo, axis=0, keepdims=True),
            jnp.sum(dr * fv, axis=0, keepdims=True) * w,
            jnp.zeros((5, D), F32)], axis=0)

        @pl.when(i == 0)
        def _():
            sums_ref[...] = part

        @pl.when(i > 0)
        def _():
            sums_ref[...] += part

    return pl.pallas_call(
        body, name=name, grid=(S // TM,),
        in_specs=[_row_spec(D), _row_spec(D), _full_spec((9, D)), _full_spec((3, D)), _row_spec(D)],
        out_specs=(_row_spec(D), _row_spec(D), _full_spec((8, D))),
        out_shape=(_sds((S, D), F32), _sds((S, D), MXU), _sds((8, D), F32)),
        compiler_params=_cp(32, ("arbitrary",)))(x, f, mod, lng, dxo)


def ln_mod_bwd(x, dh, mod, sub, dxa, name):
    def body(x_ref, dh_ref, mod_ref, dxa_ref, dx_ref, sums_ref):
        i = pl.program_id(0)
        scale = mod_ref[3 * sub + 1:3 * sub + 2, :]
        n, rstd = _ln_stats(x_ref[...])
        dh = dh_ref[...]
        dx_ref[...] = dxa_ref[...] + _ln_bwd(dh * (1.0 + scale), n, rstd)
        part = jnp.concatenate([
            jnp.sum(dh, axis=0, keepdims=True),
            jnp.sum(dh * n, axis=0, keepdims=True),
            jnp.zeros((6, D), F32)], axis=0)

        @pl.when(i == 0)
        def _():
            sums_ref[...] = part

        @pl.when(i > 0)
        def _():
            sums_ref[...] += part

    return pl.pallas_call(
        body, name=name, grid=(S // TM,),
        in_specs=[_row_spec(D), _row_spec(D), _full_spec((9, D)), _row_spec(D)],
        out_specs=(_row_spec(D), _full_spec((8, D))),
        out_shape=(_sds((S, D), F32), _sds((8, D), F32)),
        compiler_params=_cp(32, ("arbitrary",)))(x, dh, mod, dxa)


def loss_fwd_bwd(y, target, name):
    def body(y_ref, t_ref, l_ref, dy_ref):
        i = pl.program_id(0)
        e = y_ref[...] - t_ref[...]
        dy_ref[...] = e * (1.0 / D)
        part = jnp.zeros((8, 128), F32) + (0.5 / D) * jnp.sum(e * e)

        @pl.when(i == 0)
        def _():
            l_ref[...] = part

        @pl.when(i > 0)
        def _():
            l_ref[...] += part

    return pl.pallas_call(
        body, name=name, grid=(S // TM,),
        in_specs=[_row_spec(D), _row_spec(D)], out_specs=(_full_spec((8, 128)), _row_spec(D)),
        out_shape=(_sds((8, 128), F32), _sds((S, D), F32)),
        compiler_params=_cp(32, ("arbitrary",)))(y, target)


def _w5(rows, cols, l, f):
    return pl.BlockSpec((None, None, None, rows, cols), lambda j, i: (j, l, f, 0, 0))


def ffn_fwd(h, wg, wu, wd, l, f, name):
    def body(h_ref, wg_ref, wu_ref, wd_ref, g_ref, u_ref, f_ref):
        j, i = pl.program_id(0), pl.program_id(1)
        hv = h_ref[...]
        g = _mm(hv, wg_ref[...])
        u = _mm(hv, wu_ref[...])
        g_ref[...] = g.astype(MXU)
        u_ref[...] = u.astype(MXU)
        a = g * jax.nn.sigmoid(g) * u
        part = _mm(a, wd_ref[...])
        rows = pl.ds(pl.multiple_of(i * TMM, TMM), TMM)

        @pl.when(j == 0)
        def _():
            f_ref[rows, :] = part

        @pl.when(j > 0)
        def _():
            f_ref[rows, :] += part

    gu = pl.BlockSpec((None, TMM, FBP), lambda j, i: (j, i, 0))
    return pl.pallas_call(
        body, name=name, grid=(NDEV, S // TMM),
        in_specs=[pl.BlockSpec((TMM, D), lambda j, i: (i, 0)), _w5(D, FBP, l, f), _w5(D, FBP, l, f), _w5(FBP, D, l, f)],
        out_specs=(gu, gu, pl.BlockSpec((S, D), lambda j, i: (0, 0))),
        out_shape=(_sds((NDEV, S, FBP), MXU), _sds((NDEV, S, FBP), MXU), _sds((S, D), F32)),
        compiler_params=_cp(48, ("arbitrary", "arbitrary")))(h, wg, wu, wd)


def ffn_bwd(df, h, g, u, wg, wu, wd, l, f, name):
    ni = S // TMM

    def body(df_ref, h_ref, g_ref, u_ref, wg_ref, wu_ref, wd_ref, dwg_ref, dwu_ref, dwd_ref, dh_ref,
             ag_ref, au_ref, ad_ref):
        j, i = pl.program_id(0), pl.program_id(1)
        dfv, hv = df_ref[...], h_ref[...]
        gv, uv = g_ref[...].astype(F32), u_ref[...].astype(F32)
        da = _mm_nt(dfv, wd_ref[...])
        sg = jax.nn.sigmoid(gv)
        silu = gv * sg
        du = da * silu
        dg = da * uv * (sg * (1.0 + gv * (1.0 - sg)))
        p_d = _mm_tn(silu * uv, dfv)
        p_g = _mm_tn(hv, dg)
        p_u = _mm_tn(hv, du)

        @pl.when(i == 0)
        def _():
            ad_ref[...] = p_d
            ag_ref[...] = p_g
            au_ref[...] = p_u

        @pl.when(i > 0)
        def _():
            ad_ref[...] += p_d
            ag_ref[...] += p_g
            au_ref[...] += p_u

        @pl.when(i == ni - 1)
        def _():
            dwd_ref[...] = ad_ref[...].astype(BF16)
            dwg_ref[...] = ag_ref[...].astype(BF16)
            dwu_ref[...] = au_ref[...].astype(BF16)

        part = _mm_nt(dg, wg_ref[...]) + _mm_nt(du, wu_ref[...])
        rows = pl.ds(pl.multiple_of(i * TMM, TMM), TMM)

        @pl.when(j == 0)
        def _():
            dh_ref[rows, :] = part

        @pl.when(j > 0)
        def _():
            dh_ref[rows, :] += part

    gu = pl.BlockSpec((None, TMM, FBP), lambda j, i: (j, i, 0))
    rowt = pl.BlockSpec((TMM, D), lambda j, i: (i, 0))
    return pl.pallas_call(
        body, name=name, grid=(NDEV, ni),
        in_specs=[rowt, rowt, gu, gu, _w5(D, FBP, l, f), _w5(D, FBP, l, f), _w5(FBP, D, l, f)],
        out_specs=(pl.BlockSpec((None, D, FBP), lambda j, i: (j, 0, 0)), pl.BlockSpec((None, D, FBP), lambda j, i: (j, 0, 0)),
                   pl.BlockSpec((None, FBP, D), lambda j, i: (j, 0, 0)), pl.BlockSpec((S, D), lambda j, i: (0, 0))),
        out_shape=(_sds((NDEV, D, FBP), BF16), _sds((NDEV, D, FBP), BF16), _sds((NDEV, FBP, D), BF16), _sds((S, D), F32)),
        scratch_shapes=[pltpu.VMEM((D, FBP), F32), pltpu.VMEM((D, FBP), F32), pltpu.VMEM((FBP, D), F32)],
        compiler_params=_cp(56, ("arbitrary", "arbitrary")))(df, h, g, u, wg, wu, wd)


def win_fwd(h, win, l, name):
    def body(h_ref, w_ref, z_ref):
        hv = h_ref[...]
        for j in range(NDEV):
            z_ref[:, 256 * j:256 * (j + 1)] = _mm(hv, w_ref[j])

    return pl.pallas_call(
        body, name=name, grid=(S // TMM,),
        in_specs=[_row_spec(D, TMM), pl.BlockSpec((NDEV, None, D, 256), lambda i: (0, l, 0, 0))],
        out_specs=_row_spec(D_IN, TMM), out_shape=_sds((S, D_IN), F32),
        compiler_params=_cp(40, ("arbitrary",)))(h, win)


def win_bwd(dparts, h, win, l, name):
    ni = S // TMM

    def body(dq_ref, dk_ref, dv_ref, dus_ref, dup_ref, h_ref, w_ref, dh_ref, dw_ref, acc_ref):
        i = pl.program_id(0)
        hv = h_ref[...]
        cols = [dq_ref[:, 0:256], dq_ref[:, 256:512], dk_ref[:, 0:256], dk_ref[:, 256:512],
                dv_ref[:, 0:256], dv_ref[:, 256:512], dus_ref[...], dup_ref[...]]
        dh = jnp.zeros((TMM, D), F32)
        for j in range(NDEV):
            dz = cols[j].astype(MXU)
            dh = dh + _mm_nt(dz, w_ref[j])
            p = _mm_tn(hv, dz)

            @pl.when(i == 0)
            def _():
                acc_ref[j] = p

            @pl.when(i > 0)
            def _():
                acc_ref[j] += p

        dh_ref[...] = dh

        @pl.when(i == ni - 1)
        def _():
            dw_ref[...] = acc_ref[...].astype(BF16)

    return pl.pallas_call(
        body, name=name, grid=(ni,),
        in_specs=[_row_spec(512, TMM), _row_spec(512, TMM), _row_spec(512, TMM), _row_spec(256, TMM), _row_spec(256, TMM),
                  _row_spec(D, TMM), pl.BlockSpec((NDEV, None, D, 256), lambda i: (0, l, 0, 0))],
        out_specs=(_row_spec(D, TMM), _full_spec((NDEV, D, 256))),
        out_shape=(_sds((S, D), F32), _sds((NDEV, D, 256), BF16)),
        scratch_shapes=[pltpu.VMEM((NDEV, D, 256), F32)],
        compiler_params=_cp(48, ("arbitrary",)))(*dparts, h, win)


def wout_fwd(ya, ys, yp, wout, l, name):
    def body(ya_ref, ys_ref, yp_ref, w_ref, o_ref):
        w = w_ref[...].reshape(D, D)
        o_ref[...] = _mm(ya_ref[...], w[0:512]) + _mm(ys_ref[...], w[512:768]) + _mm(yp_ref[...], w[768:1024])

    return pl.pallas_call(
        body, name=name, grid=(S // TMM,),
        in_specs=[_row_spec(512, TMM), _row_spec(256, TMM), _row_spec(256, TMM),
                  pl.BlockSpec((NDEV, None, 128, D), lambda i: (0, l, 0, 0))],
        out_specs=_row_spec(D, TMM), out_shape=_sds((S, D), F32),
        compiler_params=_cp(40, ("arbitrary",)))(ya, ys, yp, wout)


def wout_bwd(do, ya, ys, yp, wout, l, name):
    ni = S // TMM

    def body(do_ref, ya_ref, ys_ref, yp_ref, w_ref, dya_ref, dys_ref, dyp_ref, dw_ref, acc_ref):
        i = pl.program_id(0)
        w = w_ref[...].reshape(D, D)
        dov = do_ref[...]
        dya_ref[...] = _mm_nt(dov, w[0:512])
        dys_ref[...] = _mm_nt(dov, w[512:768])
        dyp_ref[...] = _mm_nt(dov, w[768:1024])
        parts = [(0, 512, _mm_tn(ya_ref[...], dov)), (512, 768, _mm_tn(ys_ref[...], dov)),
                 (768, 1024, _mm_tn(yp_ref[...], dov))]
        for lo, hi, p in parts:
            @pl.when(i == 0)
            def _():
                acc_ref[lo:hi, :] = p

            @pl.when(i > 0)
            def _():
                acc_ref[lo:hi, :] += p

        @pl.when(i == ni - 1)
        def _():
            dw_ref[...] = acc_ref[...].astype(BF16).reshape(NDEV, 128, D)

    return pl.pallas_call(
        body, name=name, grid=(ni,),
        in_specs=[_row_spec(D, TMM), _row_spec(512, TMM), _row_spec(256, TMM), _row_spec(256, TMM),
                  pl.BlockSpec((NDEV, None, 128, D), lambda i: (0, l, 0, 0))],
        out_specs=(_row_spec(512, TMM), _row_spec(256, TMM), _row_spec(256, TMM), _full_spec((NDEV, 128, D))),
        out_shape=(_sds((S, 512), F32), _sds((S, 256), F32), _sds((S, 256), F32), _sds((NDEV, 128, D), BF16)),
        scratch_shapes=[pltpu.VMEM((D, D), F32)],
        compiler_params=_cp(40, ("arbitrary",)))(do, ya, ys, yp, wout)


def _t5_bucket(dist):
    max_exact = N_BUCKETS // 2
    d = np.maximum(dist, 1).astype(np.float32)
    large = max_exact + (np.log(d / max_exact) / math.log(MAX_DISTANCE / max_exact)
                         * (N_BUCKETS - max_exact)).astype(np.int32)
    large = np.minimum(large, N_BUCKETS - 1)
    return np.where(dist < max_exact, dist, large).astype(np.int32)


def _att_static():
    i = np.arange(QB)[:, None]
    j = np.arange(2 * QB)[None, :]
    r = i + QB - j
    buckets, bands = [], []
    for window, dil in PATTERNS:
        bands.append((r >= 0) & (r <= window // dil))
        buckets.append(_t5_bucket(np.clip(r, 0, None) * dil))
    return np.stack(buckets), np.stack(bands), np.broadcast_to(j >= QB, (QB, 2 * QB))


def att_bias(rel_bias):
    bucket, band, cur = _att_static()
    b = jnp.transpose(rel_bias[bucket], (0, 3, 1, 2))
    full = jnp.where(band[:, None], b, NEG)
    first = jnp.where((band & cur[None])[:, None], b, NEG)
    return jnp.stack([full, first], axis=2)


def _permute_in(dst_ref, src_ref, d, scale=None, pad=QB):
    L = S // d
    for r in range(d):
        v = src_ref[pl.ds(r, L, stride=d), :] if d > 1 else src_ref[...]
        if scale is not None:
            v = v * scale
        dst_ref[pad + r * L:pad + (r + 1) * L, :] = v.astype(dst_ref.dtype)


def att_fwd(z, bias, name):
    def body(q_ref, k_ref, v_ref, b_ref, y_ref, l_ref, qs, ks, vs, o_perm, l_perm, o_nat, l_nat):
        zero_pad = jnp.zeros((QB, 128), MXU)
        ks[0:QB, :] = zero_pad
        vs[0:QB, :] = zero_pad
        lane = lax.broadcasted_iota(jnp.int32, (QB, 128), 1)
        for p, (_, d) in enumerate(PATTERNS):
            L = S // d
            nb = L // QB
            _permute_in(qs, q_ref, d, scale=0.125, pad=0)
            _permute_in(ks, k_ref, d)
            _permute_in(vs, v_ref, d)

            def blk(b, carry):
                r0 = pl.multiple_of(b * QB, QB)
                q = qs[pl.ds(r0, QB), :]
                kb = ks[pl.ds(r0, 2 * QB), :]
                vb = vs[pl.ds(r0, 2 * QB), :]
                first = (b % nb) == 0
                res = []
                for hh in range(2):
                    sel = (lane < 64) if hh == 0 else (lane >= 64)
                    qm = jnp.where(sel, q, jnp.zeros_like(q))
                    s = _mm_nt(qm, kb) + jnp.where(first, b_ref[p, hh, 1], b_ref[p, hh, 0])
                    m = jnp.max(s, axis=1, keepdims=True)
                    pe = jnp.exp(s - m)
                    den = jnp.sum(pe, axis=1, keepdims=True)
                    res.append((_mm(pe, vb) / den, m + jnp.log(den)))
                o_perm[pl.ds(r0, QB), :] = jnp.where(lane < 64, res[0][0], res[1][0])
                l_perm[pl.ds(r0, QB), :] = jnp.where(lane < 64, res[0][1], res[1][1])
                return carry

            lax.fori_loop(0, S // QB, blk, 0)
            for r in range(d):
                if d > 1:
                    o_nat[p, pl.ds(r, L, stride=d), :] = o_perm[r * L:(r + 1) * L, :]
                    l_nat[p, pl.ds(r, L, stride=d), :] = l_perm[r * L:(r + 1) * L, :]
                else:
                    o_nat[p] = o_perm[...]
                    l_nat[p] = l_perm[...]
        l0, l1, l2 = l_nat[0], l_nat[1], l_nat[2]
        m = jnp.maximum(jnp.maximum(l0, l1), l2)
        e0, e1, e2 = jnp.exp(l0 - m), jnp.exp(l1 - m), jnp.exp(l2 - m)
        den = e0 + e1 + e2
        y_ref[...] = (e0 * o_nat[0] + e1 * o_nat[1] + e2 * o_nat[2]) / den
        l_ref[...] = m + jnp.log(den)

    col = lambda c0: pl.BlockSpec((S, 128), lambda hp: (0, c0 + hp))
    return pl.pallas_call(
        body, name=name, grid=(N_HEADS // 2,),
        in_specs=[col(0), col(4), col(8), pl.BlockSpec((3, 2, 2, QB, 2 * QB), lambda hp: (0, hp, 0, 0, 0))],
        out_specs=(col(0), col(0)),
        out_shape=(_sds((S, D_ATT), F32), _sds((S, D_ATT), F32)),
        scratch_shapes=[pltpu.VMEM((S, 128), MXU), pltpu.VMEM((S + QB, 128), MXU), pltpu.VMEM((S + QB, 128), MXU),
                        pltpu.VMEM((S, 128), F32), pltpu.VMEM((S, 128), F32),
                        pltpu.VMEM((3, S, 128), F32), pltpu.VMEM((3, S, 128), F32)],
        compiler_params=_cp(40, ("arbitrary",)))(z, z, z, bias)


def att_bwd(z, bias, y, lse, dy, name):
    def body(q_ref, k_ref, v_ref, b_ref, y_ref, l_ref, dy_ref, dq_ref, dk_ref, dv_ref, db_ref,
             qs, ks, vs, dys, ls, dds, dn_nat, dq_perm, dk_perm, dv_perm):
        zero_pad = jnp.zeros((QB, 128), MXU)
        ks[0:QB, :] = zero_pad
        vs[0:QB, :] = zero_pad
        lane = lax.broadcasted_iota(jnp.int32, (QB, 128), 1)
        lane_s = lax.broadcasted_iota(jnp.int32, (S, 128), 1)
        t = dy_ref[...] * y_ref[...]
        sa = jnp.sum(jnp.where(lane_s < 64, t, 0.0), axis=1, keepdims=True)
        sb = jnp.sum(jnp.where(lane_s >= 64, t, 0.0), axis=1, keepdims=True)
        dn_nat[...] = jnp.where(lane_s < 64, sa, sb)
        dq_ref[...] = jnp.zeros((S, 128), F32)
        dk_ref[...] = jnp.zeros((S, 128), F32)
        dv_ref[...] = jnp.zeros((S, 128), F32)
        db_ref[...] = jnp.zeros((3, 2, QB, 2 * QB), F32)
        for p, (_, d) in enumerate(PATTERNS):
            L = S // d
            nb = L // QB
            _permute_in(qs, q_ref, d, scale=0.125, pad=0)
            _permute_in(ks, k_ref, d)
            _permute_in(vs, v_ref, d)
            _permute_in(dys, dy_ref, d, pad=0)
            _permute_in(ls, l_ref, d, pad=0)
            _permute_in(dds, dn_nat, d, pad=0)
            dk_perm[...] = jnp.zeros((S + QB, 128), F32)
            dv_perm[...] = jnp.zeros((S + QB, 128), F32)

            def blk(b, carry):
                r0 = pl.multiple_of(b * QB, QB)
                q = qs[pl.ds(r0, QB), :]
                kb = ks[pl.ds(r0, 2 * QB), :]
                vb = vs[pl.ds(r0, 2 * QB), :]
                dyb = dys[pl.ds(r0, QB), :]
                lb = ls[pl.ds(r0, QB), :]
                db = dds[pl.ds(r0, QB), :]
                first = (b % nb) == 0
                dqs = []
                dkb = jnp.zeros((2 * QB, 128), F32)
                dvb = jnp.zeros((2 * QB, 128), F32)
                for hh in range(2):
                    sel = (lane < 64) if hh == 0 else (lane >= 64)
                    c0 = 64 * hh
                    qm = jnp.where(sel, q, jnp.zeros_like(q))
                    dym = jnp.where(sel, dyb, jnp.zeros_like(dyb))
                    s = _mm_nt(qm, kb) + jnp.where(first, b_ref[p, hh, 1], b_ref[p, hh, 0])
                    pr = jnp.exp(s - lb[:, c0:c0 + 1])
                    dp = _mm_nt(dym, vb)
                    ds = pr * (dp - db[:, c0:c0 + 1])
                    db_ref[p, hh] += ds
                    dqs.append(_mm(ds, kb))
                    dkb = dkb + _mm_tn(ds, qm)
                    dvb = dvb + _mm_tn(pr, dym)
                dq_perm[pl.ds(r0, QB), :] = jnp.where(lane < 64, dqs[0], dqs[1])
                dk_perm[pl.ds(r0, 2 * QB), :] += dkb
                dv_perm[pl.ds(r0, 2 * QB), :] += dvb
                return carry

            lax.fori_loop(0, S // QB, blk, 0)
            for r in range(d):
                idx = pl.ds(r, L, stride=d) if d > 1 else pl.ds(0, S)
                dq_ref[idx, :] += dq_perm[r * L:(r + 1) * L, :] * 0.125
                dk_ref[idx, :] += dk_perm[QB + r * L:QB + (r + 1) * L, :]
                dv_ref[idx, :] += dv_perm[QB + r * L:QB + (r + 1) * L, :]

    col = lambda c0: pl.BlockSpec((S, 128), lambda hp: (0, c0 + hp))
    bspec = pl.BlockSpec((3, 2, 2, QB, 2 * QB), lambda hp: (0, hp, 0, 0, 0))
    return pl.pallas_call(
        body, name=name, grid=(N_HEADS // 2,),
        in_specs=[col(0), col(4), col(8), bspec, col(0), col(0), col(0)],
        out_specs=(col(0), col(0), col(0), pl.BlockSpec((3, 2, QB, 2 * QB), lambda hp: (0, hp, 0, 0))),
        out_shape=(_sds((S, D_ATT), F32), _sds((S, D_ATT), F32), _sds((S, D_ATT), F32),
                   _sds((3, N_HEADS, QB, 2 * QB), F32)),
        scratch_shapes=[pltpu.VMEM((S, 128), MXU), pltpu.VMEM((S + QB, 128), MXU), pltpu.VMEM((S + QB, 128), MXU),
                        pltpu.VMEM((S, 128), MXU), pltpu.VMEM((S, 128), F32), pltpu.VMEM((S, 128), F32),
                        pltpu.VMEM((S, 128), F32), pltpu.VMEM((S, 128), F32),
                        pltpu.VMEM((S + QB, 128), F32), pltpu.VMEM((S + QB, 128), F32)],
        compiler_params=_cp(48, ("arbitrary",)))(z, z, z, bias, y, lse, dy)


def relbias_grad(dbiases):
    bucket, band, _ = _att_static()
    onehot = (bucket[:, None] == np.arange(N_BUCKETS)[None, :, None, None]) & band[:, None]
    onehot = jnp.asarray(onehot.reshape(3, N_BUCKETS, QB * 2 * QB), BF16)

    def body(db0_ref, db1_ref, oh_ref, o_ref):
        acc = jnp.zeros((N_HEADS, N_BUCKETS), F32)
        for p in range(3):
            acc = acc + lax.dot_general(db0_ref[p] + db1_ref[p], oh_ref[p].astype(F32), (((1,), (1,)), ((), ())),
                                        preferred_element_type=F32, precision=lax.Precision.HIGHEST)
        o_ref[...] = acc

    vm = pl.BlockSpec(memory_space=pltpu.VMEM)
    out = pl.pallas_call(body, name="relbias_grad", in_specs=[vm, vm, vm], out_specs=vm,
                         out_shape=_sds((N_HEADS, N_BUCKETS), F32), compiler_params=_cp(40))(
        *[d.reshape(3, N_HEADS, QB * 2 * QB) for d in dbiases], onehot)
    return out.T


def _panel(t_ref, ri, j):
    return t_ref[ri, pl.ds(j, S, stride=8), :]


def _gelu(x):
    c = math.sqrt(2.0 / math.pi)
    th = jnp.tanh(c * (x + 0.044715 * x * x * x))
    return 0.5 * x * (1.0 + th), th


def ssm_fwd(z, a, bre, bim, cre, cim, dsk, gluw, glub, l, name):
    def body(u_ref, a_ref, bre_ref, bim_ref, cre_ref, cim_ref, d_ref, gw_ref, gb_ref, y_ref, yp_ref, st_hbm, st_ref):
        u = u_ref[...]
        for j in range(8):
            st_ref[0, pl.ds(j, S, stride=8), :] = _mm(u, bre_ref[:, 128 * j:128 * (j + 1)])
            st_ref[1, pl.ds(j, S, stride=8), :] = _mm(u, bim_ref[:, 128 * j:128 * (j + 1)])
        ar, ai = a_ref[0], a_ref[1]

        def step(t, c):
            re, im = c
            i = pl.multiple_of(t * 8, 8)
            nre = ar * re - ai * im + st_ref[0, pl.ds(i, 8), :]
            nim = ar * im + ai * re + st_ref[1, pl.ds(i, 8), :]
            st_ref[0, pl.ds(i, 8), :] = nre
            st_ref[1, pl.ds(i, 8), :] = nim
            return nre, nim

        zero = jnp.zeros((8, 128), F32)
        lax.fori_loop(0, S, step, (zero, zero), unroll=8)
        y = d_ref[...] * u
        for j in range(8):
            y = y + _mm(_panel(st_ref, 0, j), cre_ref[128 * j:128 * (j + 1), :])
            y = y - _mm(_panel(st_ref, 1, j), cim_ref[128 * j:128 * (j + 1), :])
        pltpu.sync_copy(st_ref, st_hbm)
        yp_ref[...] = y
        gl, _ = _gelu(y)
        tt = _mm(gl, gw_ref[...].reshape(D_SSM, D_SSM)) + gb_ref[...]
        y_ref[...] = y * jax.nn.sigmoid(tt)

    vm = lambda shape: pl.BlockSpec(shape, lambda i: (0,) * len(shape))
    return pl.pallas_call(
        body, name=name, grid=(1,),
        in_specs=[pl.BlockSpec((S, 256), lambda i: (0, 6)), vm((2, 8, 128)), vm((256, 1024)), vm((256, 1024)),
                  vm((1024, 256)), vm((1024, 256)), vm((1, 256)),
                  pl.BlockSpec((NDEV, None, 32, 256), lambda i: (0, l, 0, 0)), vm((1, 256))],
        out_specs=(vm((S, 256)), vm((S, 256)), pl.BlockSpec(memory_space=pl.ANY)),
        out_shape=(_sds((S, 256), F32), _sds((S, 256), F32), _sds((2, S * 8, 128), F32)),
        scratch_shapes=[pltpu.VMEM((2, S * 8, 128), F32)],
        compiler_params=_cp(40, ("arbitrary",)))(z, a, bre, bim, cre, cim, dsk, gluw, glub)


def ssm_bwd(dy, z, ypre, st, a, bre, bim, cre, cim, dsk, gluw, glub, l, name):
    def body(dy_ref, u_ref, yp_ref, st_hbm, a_ref, bre_ref, bim_ref, cre_ref, cim_ref, d_ref, gw_ref, gb_ref,
             du_ref, dbre_ref, dbim_ref, dcre_ref, dcim_ref, da_ref, dd_ref, dgw_ref, dgb_ref, g_ref, st_ref):
        pltpu.sync_copy(st_hbm, st_ref)
        u = u_ref[...]
        y = yp_ref[...]
        dout = dy_ref[...]
        gw = gw_ref[...].reshape(D_SSM, D_SSM)
        gl, th = _gelu(y)
        sig = jax.nn.sigmoid(_mm(gl, gw) + gb_ref[...])
        dt = dout * y * sig * (1.0 - sig)
        dgw_ref[...] = _mm_tn(gl, dt)
        dgb_ref[...] = jnp.sum(dt, axis=0, keepdims=True)
        c = math.sqrt(2.0 / math.pi)
        dgelu = 0.5 * (1.0 + th) + 0.5 * y * (1.0 - th * th) * c * (1.0 + 3.0 * 0.044715 * y * y)
        dyv = dout * sig + _mm_nt(dt, gw) * dgelu
        dd_ref[...] = jnp.sum(dyv * u, axis=0, keepdims=True)
        for j in range(8):
            rows = slice(128 * j, 128 * (j + 1))
            g_ref[0, pl.ds(j, S, stride=8), :] = _mm_nt(dyv, cre_ref[rows, :])
            g_ref[1, pl.ds(j, S, stride=8), :] = -_mm_nt(dyv, cim_ref[rows, :])
            dcre_ref[rows, :] = _mm_tn(_panel(st_ref, 0, j), dyv)
            dcim_ref[rows, :] = -_mm_tn(_panel(st_ref, 1, j), dyv)
        ar, ai = a_ref[0], a_ref[1]

        def step(k, c4):
            gre, gim, dar, dai = c4
            i = pl.multiple_of((S - 1 - k) * 8, 8)
            nre = g_ref[0, pl.ds(i, 8), :] + ar * gre + ai * gim
            nim = g_ref[1, pl.ds(i, 8), :] + ar * gim - ai * gre
            g_ref[0, pl.ds(i, 8), :] = nre
            g_ref[1, pl.ds(i, 8), :] = nim
            sre = st_ref[0, pl.ds(i - 8, 8), :]
            sim = st_ref[1, pl.ds(i - 8, 8), :]
            return nre, nim, dar + nre * sre + nim * sim, dai + nim * sre - nre * sim

        zero = jnp.zeros((8, 128), F32)
        gre, gim, dar, dai = lax.fori_loop(0, S - 1, step, (zero, zero, zero, zero), unroll=8)
        g_ref[0, 0:8, :] = g_ref[0, 0:8, :] + ar * gre + ai * gim
        g_ref[1, 0:8, :] = g_ref[1, 0:8, :] + ar * gim - ai * gre
        da_ref[0] = dar
        da_ref[1] = dai
        du = dyv * d_ref[...]
        for j in range(8):
            cols = slice(128 * j, 128 * (j + 1))
            gr, gi = _panel(g_ref, 0, j), _panel(g_ref, 1, j)
            dbre_ref[:, cols] = _mm_tn(u, gr)
            dbim_ref[:, cols] = _mm_tn(u, gi)
            du = du + _mm_nt(gr, bre_ref[:, cols]) + _mm_nt(gi, bim_ref[:, cols])
        du_ref[...] = du

    vm = lambda shape: pl.BlockSpec(shape, lambda i: (0,) * len(shape))
    return pl.pallas_call(
        body, name=name, grid=(1,),
        in_specs=[vm((S, 256)), pl.BlockSpec((S, 256), lambda i: (0, 6)), vm((S, 256)), pl.BlockSpec(memory_space=pl.ANY),
                  vm((2, 8, 128)), vm((256, 1024)), vm((256, 1024)), vm((1024, 256)), vm((1024, 256)), vm((1, 256)),
                  pl.BlockSpec((NDEV, None, 32, 256), lambda i: (0, l, 0, 0)), vm((1, 256))],
        out_specs=(vm((S, 256)), vm((256, 1024)), vm((256, 1024)), vm((1024, 256)), vm((1024, 256)),
                   vm((2, 8, 128)), vm((1, 256)), vm((256, 256)), vm((1, 256))),
        out_shape=(_sds((S, 256), F32), _sds((256, 1024), F32), _sds((256, 1024), F32), _sds((1024, 256), F32),
                   _sds((1024, 256), F32), _sds((2, 8, 128), F32), _sds((1, 256), F32), _sds((256, 256), F32),
                   _sds((1, 256), F32)),
        scratch_shapes=[pltpu.VMEM((2, S * 8, 128), F32), pltpu.VMEM((2, S * 8, 128), F32)],
        compiler_params=_cp(56, ("arbitrary",)))(dy, z, ypre, st, a, bre, bim, cre, cim, dsk, gluw, glub)


def _ssm_discretise(a_re, a_im, log_dt, b_re, b_im):
    dt = jnp.exp(log_dt)[:, None]
    er = jnp.exp(a_re * dt)
    abr, abi = er * jnp.cos(a_im * dt), er * jnp.sin(a_im * dt)
    den = a_re * a_re + a_im * a_im
    fr = ((abr - 1.0) * a_re + abi * a_im) / den
    fi = (abi * a_re - (abr - 1.0) * a_im) / den
    bbr = fr[:, :, None] * b_re - fi[:, :, None] * b_im
    bbi = fr[:, :, None] * b_im + fi[:, :, None] * b_re
    return abr, abi, bbr, bbi


def _blockdiag(t):
    g, r, c = t.shape
    eye = jnp.eye(g, dtype=t.dtype)
    return (t[:, :, None, :] * eye[:, None, :, None]).reshape(g * r, g * c)


def _blockdiag_take(m, r, c):
    g = m.shape[0] // r
    idx = jnp.arange(g)
    return m.reshape(g, r, g, c)[idx, :, idx, :]


PAD = 16


def _pool_lane_select(vals):
    lane = lax.broadcasted_iota(jnp.int32, vals[0].shape, 1)
    out = vals[3]
    for g in (2, 1, 0):
        out = jnp.where(lane < 64 * (g + 1), vals[g], out)
    return out


def _pool_counts():
    row = lax.broadcasted_iota(jnp.int32, (S, D_POOL), 0).astype(F32) + 1.0
    return _pool_lane_select([jnp.minimum(row, float(w)) for w in POOL_WINDOWS])


def _pooled(u, sa, sb):
    sums = []
    cur = u
    bufs = (sa, sb)
    for k, sh in enumerate((1, 2, 4, 8)):
        buf = bufs[k % 2]
        buf[PAD:PAD + S, :] = cur
        cur = cur + buf[PAD - sh:PAD - sh + S, :]
        sums.append(cur)
    return _pool_lane_select(sums) / _pool_counts() - u


def pool_fwd(z, pw, psc, name):
    def body(u_ref, w_ref, s_ref, y_ref, sa, sb):
        for buf in (sa, sb):
            buf[0:PAD, :] = jnp.zeros((PAD, D_POOL), F32)
        pooled = _pooled(u_ref[...], sa, sb)
        y_ref[...] = _mm(pooled, w_ref[...]) * s_ref[...]

    vm = lambda shape: pl.BlockSpec(shape, lambda i: (0,) * len(shape))
    return pl.pallas_call(
        body, name=name, grid=(1,),
        in_specs=[pl.BlockSpec((S, 256), lambda i: (0, 7)), vm((256, 256)), vm((1, 256))],
        out_specs=vm((S, 256)), out_shape=_sds((S, 256), F32),
        scratch_shapes=[pltpu.VMEM((S + 2 * PAD, D_POOL), F32)] * 2,
        compiler_params=_cp(40, ("arbitrary",)))(z, pw, psc)


def pool_bwd(dy, z, pw, psc, name):
    def body(dy_ref, u_ref, w_ref, s_ref, du_ref, dw_ref, ds_ref, sa, sb):
        for buf in (sa, sb):
            buf[0:PAD, :] = jnp.zeros((PAD, D_POOL), F32)
            buf[PAD + S:PAD + S + PAD, :] = jnp.zeros((PAD, D_POOL), F32)
        pooled = _pooled(u_ref[...], sa, sb)
        dyv = dy_ref[...]
        w = w_ref[...]
        ds_ref[...] = jnp.sum(dyv * _mm(pooled, w), axis=0, keepdims=True)
        dyl = dyv * s_ref[...]
        dw_ref[...] = _mm_tn(pooled, dyl)
        dpool = _mm_nt(dyl, w)
        cur = dpool / _pool_counts()
        sums = []
        bufs = (sa, sb)
        for k, sh in enumerate((1, 2, 4, 8)):
            buf = bufs[k % 2]
            buf[PAD:PAD + S, :] = cur
            cur = cur + buf[PAD + sh:PAD + sh + S, :]
            sums.append(cur)
        du_ref[...] = _pool_lane_select(sums) - dpool

    vm = lambda shape: pl.BlockSpec(shape, lambda i: (0,) * len(shape))
    return pl.pallas_call(
        body, name=name, grid=(1,),
        in_specs=[vm((S, 256)), pl.BlockSpec((S, 256), lambda i: (0, 7)), vm((256, 256)), vm((1, 256))],
        out_specs=(vm((S, 256)), vm((256, 256)), vm((1, 256))),
        out_shape=(_sds((S, 256), F32), _sds((256, 256), F32), _sds((1, 256), F32)),
        scratch_shapes=[pltpu.VMEM((S + 2 * PAD, D_POOL), F32)] * 2,
        compiler_params=_cp(40, ("arbitrary",)))(dy, z, pw, psc)


def ada_fwd(c_all, ada_w, ada_b_cols):
    def body(c_ref, w_ref, b_ref, o_ref):
        c = c_ref[...]
        cond = c * jax.nn.sigmoid(c)
        o_ref[...] = jnp.dot(cond, w_ref[...], preferred_element_type=F32, precision=lax.Precision.HIGHEST) + b_ref[...]

    return pl.pallas_call(
        body, name="ada_fwd", grid=(DEPTH,),
        in_specs=[pl.BlockSpec((NDEV, D), lambda l: (0, 0)), pl.BlockSpec((None, D, 1152), lambda l: (l, 0, 0)),
                  pl.BlockSpec((None, 1, 1152), lambda l: (l, 0, 0))],
        out_specs=pl.BlockSpec((None, NDEV, 1152), lambda l: (l, 0, 0)), out_shape=_sds((DEPTH, NDEV, 1152), F32),
        compiler_params=_cp(40, ("arbitrary",)))(c_all, ada_w, ada_b_cols)


def ada_bwd(c_all, dmod_cols):
    def body(c_ref, dm_ref, o_ref):
        c = c_ref[...]
        cond = c * jax.nn.sigmoid(c)
        o_ref[...] = lax.dot_general(cond, dm_ref[...], (((0,), (0,)), ((), ())), preferred_element_type=F32,
                                     precision=lax.Precision.HIGHEST)

    return pl.pallas_call(
        body, name="ada_bwd", grid=(DEPTH,),
        in_specs=[pl.BlockSpec((NDEV, D), lambda l: (0, 0)), pl.BlockSpec((None, NDEV, 1152), lambda l: (l, 0, 0))],
        out_specs=pl.BlockSpec((None, D, 1152), lambda l: (l, 0, 0)), out_shape=_sds((DEPTH, D, 1152), F32),
        compiler_params=_cp(40, ("arbitrary",)))(c_all, dmod_cols)


def _adamw(w, g, m, v):
    m2 = B1 * m + (1.0 - B1) * g
    v2 = B2 * v + (1.0 - B2) * (g * g)
    m_hat = m2 / (1.0 - B1 ** STEP)
    v_hat = v2 / (1.0 - B2 ** STEP)
    return -LR * (m_hat / (jnp.sqrt(v_hat) + EPS) + WD * w), m2, v2


def _sum8(ref):
    g = ref[0].astype(F32)
    for s in range(1, NDEV):
        g = g + ref[s].astype(F32)
    return g


def adam_rs(recv, w, m, v, tr, name):
    r, cdim = w.shape
    cp = recv.shape[2]

    def body(rc_ref, w_ref, m_ref, v_ref, g_ref, d_ref, m2_ref, v2_ref):
        g = _sum8(rc_ref)[:, :cdim]
        g_ref[...] = g
        d_ref[...], m2_ref[...], v2_ref[...] = _adamw(w_ref[...], g, m_ref[...], v_ref[...])

    rs = pl.BlockSpec((tr, cdim), lambda i: (i, 0))
    return pl.pallas_call(
        body, name=name, grid=(r // tr,),
        in_specs=[pl.BlockSpec((NDEV, tr, cp), lambda i: (0, i, 0)), rs, rs, rs],
        out_specs=(rs, rs, rs, rs), out_shape=tuple(_sds((r, cdim), F32) for _ in range(4)),
        compiler_params=_cp(48, ("arbitrary",)))(recv, w, m, v)


def adam_rs_rows(recv, w, m, v, name):
    half = FB // 2

    def body(rc_ref, w_ref, m_ref, v_ref, g_ref, d_ref, m2_ref, v2_ref):
        g = _sum8(rc_ref)
        g_ref[...] = g
        d_ref[...], m2_ref[...], v2_ref[...] = _adamw(w_ref[...], g, m_ref[...], v_ref[...])

    rs = pl.BlockSpec((None, half, D), lambda a, i: (a, i, 0))
    return pl.pallas_call(
        body, name=name, grid=(4, 2),
        in_specs=[pl.BlockSpec((NDEV, None, half, D), lambda a, i: (0, a, i, 0)), rs, rs, rs],
        out_specs=(rs, rs, rs, rs), out_shape=tuple(_sds((4, FB, D), F32) for _ in range(4)),
        compiler_params=_cp(48, ("arbitrary", "arbitrary")))(recv, w, m, v)


def adam_plain(g, w, m, v, tr, name):
    r, cdim = w.shape

    def body(g_ref, w_ref, m_ref, v_ref, d_ref, m2_ref, v2_ref):
        d_ref[...], m2_ref[...], v2_ref[...] = _adamw(w_ref[...], g_ref[...], m_ref[...], v_ref[...])

    rs = pl.BlockSpec((tr, cdim), lambda i: (i, 0))
    return pl.pallas_call(
        body, name=name, grid=(r // tr,), in_specs=[rs, rs, rs, rs], out_specs=(rs, rs, rs),
        out_shape=tuple(_sds((r, cdim), F32) for _ in range(3)),
        compiler_params=_cp(48, ("arbitrary",)))(g, w, m, v)


def sum_sources(recv, name):
    r = recv.shape[1]

    def body(rc_ref, o_ref):
        o_ref[...] = _sum8(rc_ref)

    vm = pl.BlockSpec(memory_space=pltpu.VMEM)
    return pl.pallas_call(body, name=name, in_specs=[vm], out_specs=vm, out_shape=_sds((r, 128), F32),
                          compiler_params=_cp(40))(recv)


def _pack(arrs):
    flat = jnp.concatenate([a.reshape(-1) for a in arrs])
    n = flat.shape[0]
    rows = -(-n // 1024) * 8
    return jnp.pad(flat, (0, rows * 128 - n)).reshape(rows, 128)


def _unpack(vec, shapes):
    flat = vec.reshape(-1)
    out, o = [], 0
    for sh in shapes:
        n = int(np.prod(sh))
        out.append(flat[o:o + n].reshape(sh))
        o += n
    return out


WEIGHTS = ['rel_bias', 'ada_w', 'ada_b', 'ln_g', 'ln_b', 'ffn_w_gate', 'ffn_w_up', 'ffn_w_down', 'w_in', 'w_out',
           'ssm_a_re', 'ssm_a_im', 'ssm_log_dt', 'ssm_b_re', 'ssm_b_im', 'ssm_c_re', 'ssm_c_im', 'ssm_d', 'glu_w',
           'glu_b', 'pool_w', 'pool_scale']
SMALL = ['rel_bias', 'ada_b', 'ln_g', 'ln_b', 'ssm_a_re', 'ssm_a_im', 'ssm_log_dt', 'ssm_b_re', 'ssm_b_im',
         'ssm_c_re', 'ssm_c_im', 'ssm_d', 'glu_b', 'pool_w', 'pool_scale']
SMALL_FULL_SHAPES = {'rel_bias': (32, 8), 'ada_b': (2, 9216), 'ln_g': (2, 3, 1024), 'ln_b': (2, 3, 1024),
                     'ssm_a_re': (2, 16, 64), 'ssm_a_im': (2, 16, 64), 'ssm_log_dt': (2, 16),
                     'ssm_b_re': (2, 16, 64, 16), 'ssm_b_im': (2, 16, 64, 16), 'ssm_c_re': (2, 16, 16, 64),
                     'ssm_c_im': (2, 16, 16, 64), 'ssm_d': (2, 256), 'glu_b': (2, 256), 'pool_w': (2, 4, 64, 64),
                     'pool_scale': (2, 256)}


def _step(P):
    me = _me()
    x0 = P['x'][0]
    target = P['loss_target'][0]

    padc = lambda a: jnp.pad(a.astype(BF16), ((0, 0), (0, 0), (0, 0), (0, FBP - FB)))
    padr = lambda a: jnp.pad(a.astype(BF16), ((0, 0), (0, 0), (0, FBP - FB), (0, 0)))
    c_all, lng_all, lnb_all, WG, WU, WD, WIN, WOUT, GLUW = _exchange(
        [P['c'], P['ln_g'], P['ln_b'], padc(P['ffn_w_gate']), padc(P['ffn_w_up']), padr(P['ffn_w_down']),
         P['w_in'].astype(BF16), P['w_out'].astype(BF16), P['glu_w'].astype(BF16)], True, "gather_weights")
    c_all = c_all.reshape(NDEV, D)
    ln_g = jnp.transpose(lng_all, (1, 2, 0, 3)).reshape(DEPTH, 3, D)
    ln_b = jnp.transpose(lnb_all, (1, 2, 0, 3)).reshape(DEPTH, 3, D)

    ada_b_cols = lax.dynamic_slice_in_dim(P['ada_b'], me * 1152, 1152, axis=1).reshape(DEPTH, 1, 1152)
    modc = ada_fwd(c_all, P['ada_w'], ada_b_cols)
    (mod_all,) = _exchange([modc], True, "gather_mod")
    mod_me = lax.dynamic_index_in_dim(mod_all, me, axis=2, keepdims=False)
    mod = jnp.transpose(mod_me, (1, 0, 2)).reshape(DEPTH, 9, D)

    bias = att_bias(P['rel_bias'])
    ssm = []
    for l in range(DEPTH):
        prm = (P['ssm_a_re'][l], P['ssm_a_im'][l], P['ssm_log_dt'][l], P['ssm_b_re'][l], P['ssm_b_im'][l])
        (abr, abi, bbr, bbi), disc_vjp = jax.vjp(_ssm_discretise, *prm)
        ssm.append(dict(
            vjp=disc_vjp, a=jnp.stack([abr.reshape(8, 128), abi.reshape(8, 128)]),
            bre=_blockdiag(jnp.transpose(bbr, (0, 2, 1))).astype(MXU), bim=_blockdiag(jnp.transpose(bbi, (0, 2, 1))).astype(MXU),
            cre=_blockdiag(jnp.transpose(P['ssm_c_re'][l], (0, 2, 1))).astype(MXU),
            cim=_blockdiag(jnp.transpose(P['ssm_c_im'][l], (0, 2, 1))).astype(MXU),
            d=P['ssm_d'][l].reshape(1, 256), gb=P['glu_b'][l].reshape(1, 256),
            pw=_blockdiag(P['pool_w'][l]).astype(MXU), psc=P['pool_scale'][l].reshape(1, 256)))

    saved = []
    x = x0
    for l in range(DEPTH):
        for sub in range(3):
            tag = f"l{l}s{sub}"
            h = ln_mod_fwd(x, mod[l], sub, "ln_mod_fwd_" + tag)
            if sub != 1:
                f = sub // 2
                G, U, fo = ffn_fwd(h, WG, WU, WD, l, f, "ffn_fwd_" + tag)
                saved.append(dict(x=x, h=h, G=G, U=U, f=fo))
                x = res_ln_fwd(x, fo, mod[l], sub, ln_g[l], ln_b[l], 0.5, "res_ln_fwd_" + tag)
            else:
                sp = ssm[l]
                z = win_fwd(h, WIN, l, "win_fwd_" + tag)
                ya, lse = att_fwd(z, bias, "att_fwd_" + tag)
                ys, ypre, st = ssm_fwd(z, sp['a'], sp['bre'], sp['bim'], sp['cre'], sp['cim'], sp['d'], GLUW, sp['gb'], l,
                                       "ssm_fwd_" + tag)
                yp = pool_fwd(z, sp['pw'], sp['psc'], "pool_fwd_" + tag)
                o = wout_fwd(ya, ys, yp, WOUT, l, "wout_fwd_" + tag)
                saved.append(dict(x=x, h=h, z=z, ya=ya, lse=lse, ys=ys, ypre=ypre, st=st, yp=yp, f=o))
                x = res_ln_fwd(x, o, mod[l], sub, ln_g[l], ln_b[l], 1.0, "res_ln_fwd_" + tag)

    loss_tile, dx = loss_fwd_bwd(x, target, "loss")
    loss = lax.psum(loss_tile[0, 0], ("x", "y", "c"))

    grads = {}
    dmod = [[None] * 9 for _ in range(DEPTH)]
    dlng = [[None] * 3 for _ in range(DEPTH)]
    dlnb = [[None] * 3 for _ in range(DEPTH)]
    dWG = [[None, None] for _ in range(DEPTH)]
    dWU = [[None, None] for _ in range(DEPTH)]
    dWD = [[None, None] for _ in range(DEPTH)]
    dWIN, dWOUT, dGLUW = [None] * DEPTH, [None] * DEPTH, [None] * DEPTH
    dbiases = [None] * DEPTH
    small_l = [dict() for _ in range(DEPTH)]
    for l in reversed(range(DEPTH)):
        for sub in reversed(range(3)):
            tag = f"l{l}s{sub}"
            sv = saved[3 * l + sub]
            w = 1.0 if sub == 1 else 0.5
            dxa, df, sums = res_ln_bwd(sv['x'], sv['f'], mod[l], sub, ln_g[l], dx, w, "res_ln_bwd_" + tag)
            dlng[l][sub], dlnb[l][sub], dmod[l][3 * sub + 2] = sums[0], sums[1], sums[2]
            if sub != 1:
                f = sub // 2
                dWG[l][f], dWU[l][f], dWD[l][f], dh = ffn_bwd(df, sv['h'], sv['G'], sv['U'], WG, WU, WD, l, f, "ffn_bwd_" + tag)
            else:
                sp = ssm[l]
                dya, dys, dyp, dWOUT[l] = wout_bwd(df, sv['ya'], sv['ys'], sv['yp'], WOUT, l, "wout_bwd_" + tag)
                dq, dk, dv, dbiases[l] = att_bwd(sv['z'], bias, sv['ya'], sv['lse'], dya, "att_bwd_" + tag)
                dus, dbre, dbim, dcre, dcim, da, dd, dgw, dgb = ssm_bwd(
                    dys, sv['z'], sv['ypre'], sv['st'], sp['a'], sp['bre'], sp['bim'], sp['cre'], sp['cim'], sp['d'],
                    GLUW, sp['gb'], l, "ssm_bwd_" + tag)
                dup, dpw, dpsc = pool_bwd(dyp, sv['z'], sp['pw'], sp['psc'], "pool_bwd_" + tag)
                dh, dWIN[l] = win_bwd((dq, dk, dv, dus, dup), sv['h'], WIN, l, "win_bwd_" + tag)
                dGLUW[l] = dgw.astype(BF16).reshape(NDEV, 32, 256)
                d_are, d_aim, d_ldt, d_bre, d_bim = sp['vjp']((
                    da[0].reshape(16, 64), da[1].reshape(16, 64),
                    jnp.transpose(_blockdiag_take(dbre, 16, 64), (0, 2, 1)), jnp.transpose(_blockdiag_take(dbim, 16, 64), (0, 2, 1))))
                small_l[l] = dict(
                    ssm_a_re=d_are, ssm_a_im=d_aim, ssm_log_dt=d_ldt, ssm_b_re=d_bre, ssm_b_im=d_bim,
                    ssm_c_re=jnp.transpose(_blockdiag_take(dcre, 64, 16), (0, 2, 1)),
                    ssm_c_im=jnp.transpose(_blockdiag_take(dcim, 64, 16), (0, 2, 1)),
                    ssm_d=dd.reshape(256), glu_b=dgb.reshape(256), pool_w=_blockdiag_take(dpw, 64, 64), pool_scale=dpsc.reshape(256))
            dx, sums = ln_mod_bwd(sv['x'], dh, mod[l], sub, dxa, "ln_mod_bwd_" + tag)
            dmod[l][3 * sub], dmod[l][3 * sub + 1] = sums[0], sums[1]
    grad_x = dx[None]

    small = {k: jnp.stack([small_l[l][k] for l in range(DEPTH)]) for k in small_l[0]}
    small['rel_bias'] = relbias_grad(dbiases)
    small['ada_b'] = jnp.stack([jnp.stack(dmod[l]).reshape(9 * D) for l in range(DEPTH)])
    small['ln_g'] = jnp.stack([jnp.stack(dlng[l]) for l in range(DEPTH)])
    small['ln_b'] = jnp.stack([jnp.stack(dlnb[l]) for l in range(DEPTH)])
    (small_all,) = _exchange([_pack([small[k] for k in SMALL])], True, "gather_small")
    gsum = dict(zip(SMALL, _unpack(sum_sources(small_all, "sum_small"), [SMALL_FULL_SHAPES[k] for k in SMALL])))
    off = 256
    dmod_all = small_all.reshape(NDEV, -1)[:, off:off + DEPTH * 9 * D].reshape(NDEV, DEPTH, 9 * D)
    dmod_cols = jnp.transpose(lax.dynamic_slice_in_dim(dmod_all, me * 1152, 1152, axis=2), (1, 0, 2))
    g_ada_w = ada_bwd(c_all, dmod_cols)

    stack4 = lambda t: jnp.stack([jnp.stack(t[l], axis=1) for l in range(DEPTH)], axis=1)
    rWG, rWU, rWD, rWIN, rWOUT, rGLUW = _exchange(
        [stack4(dWG), stack4(dWU), stack4(dWD), jnp.stack(dWIN, axis=1), jnp.stack(dWOUT, axis=1),
         jnp.stack(dGLUW, axis=1)], False, "scatter_grads")

    out = {}

    def put(name, g, d, m2, v2, shape):
        out['grad_' + name], out['delta_' + name] = g.reshape(shape), d.reshape(shape)
        out['new_m_' + name], out['new_v_' + name] = m2.reshape(shape), v2.reshape(shape)

    def flat2(name, cols):
        return [P[pre + name].reshape(-1, cols) for pre in ('', 'm_', 'v_')]

    for name, recv in (('ffn_w_gate', rWG), ('ffn_w_up', rWU)):
        put(name, *adam_rs(recv.reshape(NDEV, 4 * D, FBP), *flat2(name, FB), 256, "adam_" + name), P[name].shape)
    put('ffn_w_down', *adam_rs_rows(rWD.reshape(NDEV, 4, FBP, D), *[P[pre + 'ffn_w_down'].reshape(4, FB, D) for pre in ('', 'm_', 'v_')],
                                    "adam_ffn_w_down"), P['ffn_w_down'].shape)
    put('w_in', *adam_rs(rWIN.reshape(NDEV, 2 * D, 256), *flat2('w_in', 256), 512, "adam_w_in"), P['w_in'].shape)
    put('w_out', *adam_rs(rWOUT.reshape(NDEV, 256, D), *flat2('w_out', D), 128, "adam_w_out"), P['w_out'].shape)
    put('glu_w', *adam_rs(rGLUW.reshape(NDEV, 64, 256), *flat2('glu_w', 256), 64, "adam_glu_w"), P['glu_w'].shape)
    w_, m_, v_ = flat2('ada_w', 1152)
    put('ada_w', g_ada_w, *adam_plain(g_ada_w.reshape(-1, 1152), w_, m_, v_, 256, "adam_ada_w"), P['ada_w'].shape)

    for k in ('ln_g', 'ln_b'):
        gsum[k] = lax.dynamic_slice_in_dim(gsum[k], me * 128, 128, axis=2)
    gs, ws, ms, vs = (_pack([src(k) for k in SMALL]) for src in
                      (lambda k: gsum[k], lambda k: P[k], lambda k: P['m_' + k], lambda k: P['v_' + k]))
    ds_, m2s, v2s = adam_plain(gs, ws, ms, vs, gs.shape[0], "adam_small")
    shapes = [P[k].shape for k in SMALL]
    for k, d, m2, v2 in zip(SMALL, _unpack(ds_, shapes), _unpack(m2s, shapes), _unpack(v2s, shapes)):
        put(k, gsum[k], d, m2, v2, P[k].shape)

    res = [loss, grad_x]
    for pre in ('grad_', 'delta_', 'new_m_', 'new_v_'):
        res += [out[pre + k] for k in WEIGHTS]
    return tuple(res)


def kernel(x, c, rel_bias, ada_w, ada_b, ln_g, ln_b, ffn_w_gate, ffn_w_up, ffn_w_down, w_in, w_out, ssm_a_re, ssm_a_im, ssm_log_dt, ssm_b_re, ssm_b_im, ssm_c_re, ssm_c_im, ssm_d, glu_w, glu_b, pool_w, pool_scale, loss_target, m_rel_bias, m_ada_w, m_ada_b, m_ln_g, m_ln_b, m_ffn_w_gate, m_ffn_w_up, m_ffn_w_down, m_w_in, m_w_out, m_ssm_a_re, m_ssm_a_im, m_ssm_log_dt, m_ssm_b_re, m_ssm_b_im, m_ssm_c_re, m_ssm_c_im, m_ssm_d, m_glu_w, m_glu_b, m_pool_w, m_pool_scale, v_rel_bias, v_ada_w, v_ada_b, v_ln_g, v_ln_b, v_ffn_w_gate, v_ffn_w_up, v_ffn_w_down, v_w_in, v_w_out, v_ssm_a_re, v_ssm_a_im, v_ssm_log_dt, v_ssm_b_re, v_ssm_b_im, v_ssm_c_re, v_ssm_c_im, v_ssm_d, v_glu_w, v_glu_b, v_pool_w, v_pool_scale):
    return _step(dict(locals()))
```

```python
import functools
import math

import numpy as np
import jax
import jax.numpy as jnp
from jax import lax
from jax.experimental import pallas as pl
from jax.experimental.pallas import tpu as pltpu

F32 = jnp.float32
BF16 = jnp.bfloat16
MXU = jnp.bfloat16

S = 2048
D = 1024
NDEV = 8
DEPTH = 2
D_ATT, D_SSM, D_POOL, D_IN = 512, 256, 256, 2048
N_HEADS = 8
FB = 352
FBP = 384
QB = 128
PATTERNS = ((128, 1), (512, 4), (2048, 16))
POOL_WINDOWS = (2, 4, 8, 16)
N_BUCKETS, MAX_DISTANCE = 32, 2048
ALPHA = (2 * DEPTH) ** 0.25
LN_EPS = 1e-5
NEG = -1e30
LR, B1, B2, EPS, WD, STEP = 0.001, 0.9, 0.999, 1e-08, 0.01, 10

TM = 256
TMM = 512
MIB = 1024 * 1024


def _cp(vmem_mib, sem=None):
    kw = dict(vmem_limit_bytes=vmem_mib * MIB)
    if sem is not None:
        kw["dimension_semantics"] = sem
    return pltpu.CompilerParams(**kw)


def _sds(shape, dtype):
    return jax.ShapeDtypeStruct(shape, dtype)


def _mm(a, b):
    return jnp.dot(a.astype(MXU), b.astype(MXU), preferred_element_type=F32)


def _mm_nt(a, b):
    return lax.dot_general(a.astype(MXU), b.astype(MXU), (((1,), (1,)), ((), ())), preferred_element_type=F32)


def _mm_tn(a, b):
    return lax.dot_general(a.astype(MXU), b.astype(MXU), (((0,), (0,)), ((), ())), preferred_element_type=F32)


def _ln_stats(x):
    mu = jnp.mean(x, axis=-1, keepdims=True)
    xc = x - mu
    var = jnp.mean(xc * xc, axis=-1, keepdims=True)
    rstd = lax.rsqrt(var + LN_EPS)
    return xc * rstd, rstd


def _ln_bwd(dn, n, rstd):
    return rstd * (dn - jnp.mean(dn, axis=-1, keepdims=True) - n * jnp.mean(dn * n, axis=-1, keepdims=True))


def _me():
    return 4 * lax.axis_index("x") + 2 * lax.axis_index("y") + lax.axis_index("c")


ANY = pl.BlockSpec(memory_space=pl.ANY)


class Gather:
    def __init__(self, srcs):
        self.srcs = list(srcs)
        self.n = len(self.srcs)
        self.bufs = []
        self.out_shapes = [_sds((NDEV,) + a.shape, a.dtype) for a in self.srcs]
        self.sems = [pltpu.SemaphoreType.DMA((7 * self.n,)), pltpu.SemaphoreType.DMA((7 * self.n,)),
                     pltpu.SemaphoreType.DMA((self.n,))]

    def _parts(self, srcs, outs, sems):
        send_sems, recv_sems, loc_sems = sems
        x, y, c = lax.axis_index("x"), lax.axis_index("y"), lax.axis_index("c")
        me, sib = (x, y, c), (x, y, 1 - c)
        chips = [(1 - x, y), (x, 1 - y), (1 - x, 1 - y)]
        slot = lambda d: 4 * d[0] + 2 * d[1] + d[2]

        def copy(a, k, block, to, src=None):
            dst = outs[a].at[slot(block)]
            return pltpu.make_async_remote_copy(
                src_ref=dst if src is None else src, dst_ref=dst,
                send_sem=send_sems.at[7 * a + k], recv_sem=recv_sems.at[7 * a + k],
                device_id=to, device_id_type=pl.DeviceIdType.MESH)

        local = [pltpu.make_async_copy(srcs[a], outs[a].at[slot(me)], loc_sems.at[a]) for a in range(self.n)]
        return me, sib, chips, c, copy, local

    def start(self, srcs, bufs, outs, sems):
        me, sib, chips, c, copy, local = self._parts(srcs, outs, sems)
        for a in range(self.n):
            local[a].start()
            copy(a, 0, me, sib, src=srcs[a]).start()
            for j, chip in enumerate(chips):
                copy(a, 1 + j, me, (*chip, c), src=srcs[a]).start()

    def finish(self, srcs, bufs, outs, sems):
        me, sib, chips, c, copy, local = self._parts(srcs, outs, sems)
        for a in range(self.n):
            for j, chip in enumerate(chips):
                copy(a, 1 + j, (*chip, c), me).wait_recv()
                copy(a, 4 + j, (*chip, c), sib).start()
        for a in range(self.n):
            copy(a, 0, sib, me).wait_recv()
            copy(a, 0, me, sib, src=srcs[a]).wait_send()
            for j, chip in enumerate(chips):
                copy(a, 4 + j, (*chip, 1 - c), me).wait_recv()
                copy(a, 1 + j, me, (*chip, c), src=srcs[a]).wait_send()
                copy(a, 4 + j, (*chip, c), sib).wait_send()
            local[a].wait()


class Scatter:
    def __init__(self, srcs, bufs, index):
        self.srcs, self.bufs, self.index = list(srcs), list(bufs), [tuple(i) for i in index]
        self.n = len(self.srcs)
        self.out_shapes = [_sds(b.shape, b.dtype) for b in self.bufs]
        self.sems = [pltpu.SemaphoreType.DMA((7 * self.n,)), pltpu.SemaphoreType.DMA((7 * self.n,)),
                     pltpu.SemaphoreType.DMA((self.n,))]

    def _copies(self, srcs, outs, sems):
        send_sems, recv_sems, loc_sems = sems
        me = _me()

        def remote(a, k, slot):
            t = (me + k) % NDEV
            return pltpu.make_async_remote_copy(
                src_ref=srcs[a].at[t], dst_ref=outs[a].at[(slot,) + self.index[a]],
                send_sem=send_sems.at[7 * a + k - 1], recv_sem=recv_sems.at[7 * a + k - 1],
                device_id=(t // 4, (t // 2) % 2, t % 2), device_id_type=pl.DeviceIdType.MESH)

        local = [pltpu.make_async_copy(srcs[a].at[me], outs[a].at[(me,) + self.index[a]], loc_sems.at[a])
                 for a in range(self.n)]
        return me, remote, local

    def start(self, srcs, bufs, outs, sems):
        me, remote, local = self._copies(srcs, outs, sems)
        for a in range(self.n):
            local[a].start()
        for k in range(1, NDEV):
            for a in range(self.n):
                remote(a, k, me).start()

    def finish(self, srcs, bufs, outs, sems):
        me, remote, local = self._copies(srcs, outs, sems)
        for k in range(1, NDEV):
            for a in range(self.n):
                remote(a, k, (me - k) % NDEV).wait()
        for a in range(self.n):
            local[a].wait()


def _call(body, *, name, grid, in_specs, out_specs, out_shape, args, scratch=(), cp=None, ride=None):
    out_specs, out_shape, scratch = list(out_specs), list(out_shape), list(scratch)
    if ride is None:
        outs = pl.pallas_call(body, name=name, grid=grid, in_specs=list(in_specs), out_specs=tuple(out_specs),
                              out_shape=tuple(out_shape), scratch_shapes=scratch, compiler_params=cp)(*args)
        return list(outs), []
    nin, nout, nscr, n, nb = len(in_specs), len(out_specs), len(scratch), ride.n, len(ride.bufs)
    steps = list(grid)

    def wrapped(*refs):
        h_in, r_src, r_buf = refs[:nin], refs[nin:nin + n], refs[nin + n:nin + n + nb]
        o0 = nin + n + nb
        h_out, r_out = refs[o0:o0 + nout], refs[o0 + nout:o0 + nout + n]
        s0 = o0 + nout + n
        h_scr, sems = refs[s0:s0 + nscr], refs[s0 + nscr:]
        ids = [pl.program_id(a) for a in range(len(steps))]
        first = functools.reduce(jnp.logical_and, [i == 0 for i in ids])
        last = functools.reduce(jnp.logical_and, [i == s - 1 for i, s in zip(ids, steps)])

        @pl.when(first)
        def _():
            ride.start(r_src, r_buf, r_out, sems)

        body(*h_in, *h_out, *h_scr)

        @pl.when(last)
        def _():
            ride.finish(r_src, r_buf, r_out, sems)

    aliases = {nin + n + k: nout + k for k in range(nb)}
    outs = pl.pallas_call(
        wrapped, name=name, grid=grid, in_specs=list(in_specs) + [ANY] * (n + nb),
        out_specs=tuple(out_specs + [ANY] * n), out_shape=tuple(out_shape + ride.out_shapes),
        scratch_shapes=scratch + ride.sems, input_output_aliases=aliases, compiler_params=cp,
    )(*args, *ride.srcs, *ride.bufs)
    return list(outs[:nout]), list(outs[nout:])


def _exchange(ride, name):
    def body(dummy_ref, o_ref):
        o_ref[...] = dummy_ref[...]

    one = pl.BlockSpec((8, 128), lambda i: (0, 0))
    _, outs = _call(body, name=name, grid=(1,), in_specs=[one], out_specs=[one], out_shape=[_sds((8, 128), F32)],
                    args=(jnp.zeros((8, 128), F32),), ride=ride)
    return outs


def _row_spec(cols, tm=TM):
    return pl.BlockSpec((tm, cols), lambda i: (i, 0))


def _full_spec(shape):
    nd = len(shape)
    return pl.BlockSpec(shape, lambda i: (0,) * nd)


def ln_mod_fwd(x, mod, sub, name):
    def body(x_ref, mod_ref, h_ref):
        n, _ = _ln_stats(x_ref[...])
        shift = mod_ref[3 * sub:3 * sub + 1, :]
        scale = mod_ref[3 * sub + 1:3 * sub + 2, :]
        h_ref[...] = (n * (1.0 + scale) + shift).astype(MXU)

    return pl.pallas_call(
        body, name=name, grid=(S // TM,),
        in_specs=[_row_spec(D), _full_spec((9, D))], out_specs=_row_spec(D),
        out_shape=_sds((S, D), MXU), compiler_params=_cp(32, ("arbitrary",)))(x, mod)


def res_ln_fwd(x, f, mod, sub, lng, lnb, w, name):
    def body(x_ref, f_ref, mod_ref, g_ref, b_ref, o_ref):
        gate = mod_ref[3 * sub + 2:3 * sub + 3, :]
        r = ALPHA * x_ref[...] + (w * gate) * f_ref[...]
        n, _ = _ln_stats(r)
        o_ref[...] = n * g_ref[sub:sub + 1, :] + b_ref[sub:sub + 1, :]

    return pl.pallas_call(
        body, name=name, grid=(S // TM,),
        in_specs=[_row_spec(D), _row_spec(D), _full_spec((9, D)), _full_spec((3, D)), _full_spec((3, D))],
        out_specs=_row_spec(D), out_shape=_sds((S, D), F32),
        compiler_params=_cp(32, ("arbitrary",)))(x, f, mod, lng, lnb)


def res_ln_bwd(x, f, mod, sub, lng, dxo, w, name):
    def body(x_ref, f_ref, mod_ref, g_ref, dxo_ref, dxa_ref, df_ref, sums_ref):
        i = pl.program_id(0)
        gate = mod_ref[3 * sub + 2:3 * sub + 3, :]
        fv = f_ref[...]
        r = ALPHA * x_ref[...] + (w * gate) * fv
        n, rstd = _ln_stats(r)
        dxo = dxo_ref[...]
        dr = _ln_bwd(dxo * g_ref[sub:sub + 1, :], n, rstd)
        dxa_ref[...] = ALPHA * dr
        df_ref[...] = ((w * gate) * dr).astype(MXU)
        part = jnp.concatenate([
            jnp.sum(dxo * n, axis=0, keepdims=True),
            jnp.sum(dxo, axis=0, keepdims=True),
            jnp.sum(dr * fv, axis=0, keepdims=True) * w,
            jnp.zeros((5, D), F32)], axis=0)

        @pl.when(i == 0)
        def _():
            sums_ref[...] = part

        @pl.when(i > 0)
        def _():
            sums_ref[...] += part

    return pl.pallas_call(
        body, name=name, grid=(S // TM,),
        in_specs=[_row_spec(D), _row_spec(D), _full_spec((9, D)), _full_spec((3, D)), _row_spec(D)],
        out_specs=(_row_spec(D), _row_spec(D), _full_spec((8, D))),
        out_shape=(_sds((S, D), F32), _sds((S, D), MXU), _sds((8, D), F32)),
        compiler_params=_cp(32, ("arbitrary",)))(x, f, mod, lng, dxo)


def ln_mod_bwd(x, dh, mod, sub, dxa, name):
    def body(x_ref, dh_ref, mod_ref, dxa_ref, dx_ref, sums_ref):
        i = pl.program_id(0)
        scale = mod_ref[3 * sub + 1:3 * sub + 2, :]
        n, rstd = _ln_stats(x_ref[...])
        dh = dh_ref[...]
        dx_ref[...] = dxa_ref[...] + _ln_bwd(dh * (1.0 + scale), n, rstd)
        part = jnp.concatenate([
            jnp.sum(dh, axis=0, keepdims=True),
            jnp.sum(dh * n, axis=0, keepdims=True),
            jnp.zeros((6, D), F32)], axis=0)

        @pl.when(i == 0)
        def _():
            sums_ref[...] = part

        @pl.when(i > 0)
        def _():
            sums_ref[...] += part

    return pl.pallas_call(
        body, name=name, grid=(S // TM,),
        in_specs=[_row_spec(D), _row_spec(D), _full_spec((9, D)), _row_spec(D)],
        out_specs=(_row_spec(D), _full_spec((8, D))),
        out_shape=(_sds((S, D), F32), _sds((8, D), F32)),
        compiler_params=_cp(32, ("arbitrary",)))(x, dh, mod, dxa)


def loss_fwd_bwd(y, target, name):
    def body(y_ref, t_ref, l_ref, dy_ref):
        i = pl.program_id(0)
        e = y_ref[...] - t_ref[...]
        dy_ref[...] = e * (1.0 / D)
        part = jnp.zeros((8, 128), F32) + (0.5 / D) * jnp.sum(e * e)

        @pl.when(i == 0)
        def _():
            l_ref[...] = part

        @pl.when(i > 0)
        def _():
            l_ref[...] += part

    return pl.pallas_call(
        body, name=name, grid=(S // TM,),
        in_specs=[_row_spec(D), _row_spec(D)], out_specs=(_full_spec((8, 128)), _row_spec(D)),
        out_shape=(_sds((8, 128), F32), _sds((S, D), F32)),
        compiler_params=_cp(32, ("arbitrary",)))(y, target)


def _w3(rows, cols):
    return pl.BlockSpec((None, rows, cols), lambda j, i: (j, 0, 0))


def ffn_fwd(h, wg, wu, wd, name, ride=None):
    def body(h_ref, wg_ref, wu_ref, wd_ref, g_ref, u_ref, f_ref):
        j, i = pl.program_id(0), pl.program_id(1)
        hv = h_ref[...]
        g = _mm(hv, wg_ref[...])
        u = _mm(hv, wu_ref[...])
        g_ref[...] = g.astype(MXU)
        u_ref[...] = u.astype(MXU)
        a = g * jax.nn.sigmoid(g) * u
        part = _mm(a, wd_ref[...])
        rows = pl.ds(pl.multiple_of(i * TMM, TMM), TMM)

        @pl.when(j == 0)
        def _():
            f_ref[rows, :] = part

        @pl.when(j > 0)
        def _():
            f_ref[rows, :] += part

    gu = pl.BlockSpec((None, TMM, FBP), lambda j, i: (j, i, 0))
    return _call(
        body, name=name, grid=(NDEV, S // TMM),
        in_specs=[pl.BlockSpec((TMM, D), lambda j, i: (i, 0)), _w3(D, FBP), _w3(D, FBP), _w3(FBP, D)],
        out_specs=(gu, gu, pl.BlockSpec((S, D), lambda j, i: (0, 0))),
        out_shape=(_sds((NDEV, S, FBP), MXU), _sds((NDEV, S, FBP), MXU), _sds((S, D), F32)),
        cp=_cp(48, ("arbitrary", "arbitrary")), args=(h, wg, wu, wd), ride=ride)


def ffn_bwd(df, h, g, u, wg, wu, wd, name, ride=None):
    ni = S // TMM

    def body(df_ref, h_ref, g_ref, u_ref, wg_ref, wu_ref, wd_ref, dwg_ref, dwu_ref, dwd_ref, dh_ref,
             ag_ref, au_ref, ad_ref):
        j, i = pl.program_id(0), pl.program_id(1)
        dfv, hv = df_ref[...], h_ref[...]
        gv, uv = g_ref[...].astype(F32), u_ref[...].astype(F32)
        da = _mm_nt(dfv, wd_ref[...])
        sg = jax.nn.sigmoid(gv)
        silu = gv * sg
        du = da * silu
        dg = da * uv * (sg * (1.0 + gv * (1.0 - sg)))
        p_d = _mm_tn(silu * uv, dfv)
        p_g = _mm_tn(hv, dg)
        p_u = _mm_tn(hv, du)

        @pl.when(i == 0)
        def _():
            ad_ref[...] = p_d
            ag_ref[...] = p_g
            au_ref[...] = p_u

        @pl.when(i > 0)
        def _():
            ad_ref[...] += p_d
            ag_ref[...] += p_g
            au_ref[...] += p_u

        @pl.when(i == ni - 1)
        def _():
            dwd_ref[...] = ad_ref[...].astype(BF16)
            dwg_ref[...] = ag_ref[...].astype(BF16)
            dwu_ref[...] = au_ref[...].astype(BF16)

        part = _mm_nt(dg, wg_ref[...]) + _mm_nt(du, wu_ref[...])
        rows = pl.ds(pl.multiple_of(i * TMM, TMM), TMM)

        @pl.when(j == 0)
        def _():
            dh_ref[rows, :] = part

        @pl.when(j > 0)
        def _():
            dh_ref[rows, :] += part

    gu = pl.BlockSpec((None, TMM, FBP), lambda j, i: (j, i, 0))
    rowt = pl.BlockSpec((TMM, D), lambda j, i: (i, 0))
    return _call(
        body, name=name, grid=(NDEV, ni),
        in_specs=[rowt, rowt, gu, gu, _w3(D, FBP), _w3(D, FBP), _w3(FBP, D)],
        out_specs=(_w3(D, FBP), _w3(D, FBP), _w3(FBP, D), pl.BlockSpec((S, D), lambda j, i: (0, 0))),
        out_shape=(_sds((NDEV, D, FBP), BF16), _sds((NDEV, D, FBP), BF16), _sds((NDEV, FBP, D), BF16), _sds((S, D), F32)),
        scratch=[pltpu.VMEM((D, FBP), F32), pltpu.VMEM((D, FBP), F32), pltpu.VMEM((FBP, D), F32)],
        cp=_cp(56, ("arbitrary", "arbitrary")), args=(df, h, g, u, wg, wu, wd), ride=ride)


def win_fwd(h, win, name):
    def body(h_ref, w_ref, z_ref):
        hv = h_ref[...]
        for j in range(NDEV):
            z_ref[:, 256 * j:256 * (j + 1)] = _mm(hv, w_ref[j])

    return pl.pallas_call(
        body, name=name, grid=(S // TMM,),
        in_specs=[_row_spec(D, TMM), _full_spec((NDEV, D, 256))],
        out_specs=_row_spec(D_IN, TMM), out_shape=_sds((S, D_IN), F32),
        compiler_params=_cp(40, ("arbitrary",)))(h, win)


def win_bwd(dparts, h, win, name):
    ni = S // TMM

    def body(dq_ref, dk_ref, dv_ref, dus_ref, dup_ref, h_ref, w_ref, dh_ref, dw_ref, acc_ref):
        i = pl.program_id(0)
        hv = h_ref[...]
        cols = [dq_ref[:, 0:256], dq_ref[:, 256:512], dk_ref[:, 0:256], dk_ref[:, 256:512],
                dv_ref[:, 0:256], dv_ref[:, 256:512], dus_ref[...], dup_ref[...]]
        dh = jnp.zeros((TMM, D), F32)
        for j in range(NDEV):
            dz = cols[j].astype(MXU)
            dh = dh + _mm_nt(dz, w_ref[j])
            p = _mm_tn(hv, dz)

            @pl.when(i == 0)
            def _():
                acc_ref[j] = p

            @pl.when(i > 0)
            def _():
                acc_ref[j] += p

        dh_ref[...] = dh

        @pl.when(i == ni - 1)
        def _():
            dw_ref[...] = acc_ref[...].astype(BF16)

    return pl.pallas_call(
        body, name=name, grid=(ni,),
        in_specs=[_row_spec(512, TMM), _row_spec(512, TMM), _row_spec(512, TMM), _row_spec(256, TMM), _row_spec(256, TMM),
                  _row_spec(D, TMM), _full_spec((NDEV, D, 256))],
        out_specs=(_row_spec(D, TMM), _full_spec((NDEV, D, 256))),
        out_shape=(_sds((S, D), F32), _sds((NDEV, D, 256), BF16)),
        scratch_shapes=[pltpu.VMEM((NDEV, D, 256), F32)],
        compiler_params=_cp(48, ("arbitrary",)))(*dparts, h, win)


def wout_fwd(ya, ys, yp, wout, name):
    def body(ya_ref, ys_ref, yp_ref, w_ref, o_ref):
        w = w_ref[...].reshape(D, D)
        o_ref[...] = _mm(ya_ref[...], w[0:512]) + _mm(ys_ref[...], w[512:768]) + _mm(yp_ref[...], w[768:1024])

    return pl.pallas_call(
        body, name=name, grid=(S // TMM,),
        in_specs=[_row_spec(512, TMM), _row_spec(256, TMM), _row_spec(256, TMM), _full_spec((NDEV, 128, D))],
        out_specs=_row_spec(D, TMM), out_shape=_sds((S, D), F32),
        compiler_params=_cp(40, ("arbitrary",)))(ya, ys, yp, wout)


def wout_bwd(do, ya, ys, yp, wout, name):
    ni = S // TMM

    def body(do_ref, ya_ref, ys_ref, yp_ref, w_ref, dya_ref, dys_ref, dyp_ref, dw_ref, acc_ref):
        i = pl.program_id(0)
        w = w_ref[...].reshape(D, D)
        dov = do_ref[...]
        dya_ref[...] = _mm_nt(dov, w[0:512])
        dys_ref[...] = _mm_nt(dov, w[512:768])
        dyp_ref[...] = _mm_nt(dov, w[768:1024])
        parts = [(0, 512, _mm_tn(ya_ref[...], dov)), (512, 768, _mm_tn(ys_ref[...], dov)),
                 (768, 1024, _mm_tn(yp_ref[...], dov))]
        for lo, hi, p in parts:
            @pl.when(i == 0)
            def _():
                acc_ref[lo:hi, :] = p

            @pl.when(i > 0)
            def _():
                acc_ref[lo:hi, :] += p

        @pl.when(i == ni - 1)
        def _():
            dw_ref[...] = acc_ref[...].astype(BF16).reshape(NDEV, 128, D)

    return pl.pallas_call(
        body, name=name, grid=(ni,),
        in_specs=[_row_spec(D, TMM), _row_spec(512, TMM), _row_spec(256, TMM), _row_spec(256, TMM),
                  _full_spec((NDEV, 128, D))],
        out_specs=(_row_spec(512, TMM), _row_spec(256, TMM), _row_spec(256, TMM), _full_spec((NDEV, 128, D))),
        out_shape=(_sds((S, 512), F32), _sds((S, 256), F32), _sds((S, 256), F32), _sds((NDEV, 128, D), BF16)),
        scratch_shapes=[pltpu.VMEM((D, D), F32)],
        compiler_params=_cp(40, ("arbitrary",)))(do, ya, ys, yp, wout)


def _t5_bucket(dist):
    max_exact = N_BUCKETS // 2
    d = np.maximum(dist, 1).astype(np.float32)
    large = max_exact + (np.log(d / max_exact) / math.log(MAX_DISTANCE / max_exact)
                         * (N_BUCKETS - max_exact)).astype(np.int32)
    large = np.minimum(large, N_BUCKETS - 1)
    return np.where(dist < max_exact, dist, large).astype(np.int32)


def _att_static():
    i = np.arange(QB)[:, None]
    j = np.arange(2 * QB)[None, :]
    r = i + QB - j
    buckets, bands = [], []
    for window, dil in PATTERNS:
        bands.append((r >= 0) & (r <= window // dil))
        buckets.append(_t5_bucket(np.clip(r, 0, None) * dil))
    return np.stack(buckets), np.stack(bands), np.broadcast_to(j >= QB, (QB, 2 * QB))


def att_bias(rel_bias):
    m = np.arange(2 * QB)
    rows = []
    for window, dil in PATTERNS:
        r = QB - m
        ok = (r >= 0) & (r <= window // dil)
        b = rel_bias[_t5_bucket(np.clip(r, 0, None) * dil)]
        rows.append(jnp.where(ok[:, None], b, NEG).T)
    return jnp.broadcast_to(jnp.stack(rows)[:, :, None, :], (3, N_HEADS, 8, 2 * QB))


def _bias_tiles(t_ref, tiles):
    col = lax.broadcasted_iota(jnp.int32, (QB, 2 * QB), 1)
    for p in range(3):
        for hh in range(2):
            t = pltpu.roll(jnp.broadcast_to(t_ref[p, hh, 0:1, :], (QB, 2 * QB)), 0, 1, stride=1, stride_axis=0)
            tiles[p, hh, 0] = t
            tiles[p, hh, 1] = jnp.where(col >= QB, t, NEG)


def _permute_in(dst_ref, src_ref, d, scale=None, pad=QB):
    L = S // d
    for r in range(d):
        v = src_ref[pl.ds(r, L, stride=d), :] if d > 1 else src_ref[...]
        if scale is not None:
            v = v * scale
        dst_ref[pad + r * L:pad + (r + 1) * L, :] = v.astype(dst_ref.dtype)


def att_fwd(z, bias, name, ride=None):
    def body(q_ref, k_ref, v_ref, t_ref, y_ref, l_ref, qs, ks, vs, o_perm, l_perm, o_nat, l_nat, b_ref):
        _bias_tiles(t_ref, b_ref)
        zero_pad = jnp.zeros((QB, 128), MXU)
        ks[0:QB, :] = zero_pad
        vs[0:QB, :] = zero_pad
        lane = lax.broadcasted_iota(jnp.int32, (QB, 128), 1)
        for p, (_, d) in enumerate(PATTERNS):
            L = S // d
            nb = L // QB
            _permute_in(qs, q_ref, d, scale=0.125, pad=0)
            _permute_in(ks, k_ref, d)
            _permute_in(vs, v_ref, d)

            def blk(b, carry):
                r0 = pl.multiple_of(b * QB, QB)
                q = qs[pl.ds(r0, QB), :]
                kb = ks[pl.ds(r0, 2 * QB), :]
                vb = vs[pl.ds(r0, 2 * QB), :]
                first = (b % nb) == 0
                res = []
                for hh in range(2):
                    sel = (lane < 64) if hh == 0 else (lane >= 64)
                    qm = jnp.where(sel, q, jnp.zeros_like(q))
                    s = _mm_nt(qm, kb) + jnp.where(first, b_ref[p, hh, 1], b_ref[p, hh, 0])
                    m = jnp.max(s, axis=1, keepdims=True)
                    pe = jnp.exp(s - m)
                    den = jnp.sum(pe, axis=1, keepdims=True)
                    res.append((_mm(pe, vb) / den, m + jnp.log(den)))
                o_perm[pl.ds(r0, QB), :] = jnp.where(lane < 64, res[0][0], res[1][0])
                l_perm[pl.ds(r0, QB), :] = jnp.where(lane < 64, res[0][1], res[1][1])
                return carry

            lax.fori_loop(0, S // QB, blk, 0)
            for r in range(d):
                if d > 1:
                    o_nat[p, pl.ds(r, L, stride=d), :] = o_perm[r * L:(r + 1) * L, :]
                    l_nat[p, pl.ds(r, L, stride=d), :] = l_perm[r * L:(r + 1) * L, :]
                else:
                    o_nat[p] = o_perm[...]
                    l_nat[p] = l_perm[...]
        l0, l1, l2 = l_nat[0], l_nat[1], l_nat[2]
        m = jnp.maximum(jnp.maximum(l0, l1), l2)
        e0, e1, e2 = jnp.exp(l0 - m), jnp.exp(l1 - m), jnp.exp(l2 - m)
        den = e0 + e1 + e2
        y_ref[...] = (e0 * o_nat[0] + e1 * o_nat[1] + e2 * o_nat[2]) / den
        l_ref[...] = m + jnp.log(den)

    col = lambda c0: pl.BlockSpec((S, 128), lambda hp: (0, c0 + hp))
    return _call(
        body, name=name, grid=(N_HEADS // 2,),
        in_specs=[col(0), col(4), col(8), pl.BlockSpec((3, 2, 8, 2 * QB), lambda hp: (0, hp, 0, 0))],
        out_specs=(col(0), col(0)),
        out_shape=(_sds((S, D_ATT), F32), _sds((S, D_ATT), F32)),
        scratch=[pltpu.VMEM((S, 128), MXU), pltpu.VMEM((S + QB, 128), MXU), pltpu.VMEM((S + QB, 128), MXU),
                 pltpu.VMEM((S, 128), F32), pltpu.VMEM((S, 128), F32),
                 pltpu.VMEM((3, S, 128), F32), pltpu.VMEM((3, S, 128), F32),
                 pltpu.VMEM((3, 2, 2, QB, 2 * QB), F32)],
        cp=_cp(40, ("arbitrary",)), args=(z, z, z, bias), ride=ride)


def att_bwd(z, bias, y, lse, dy, name, ride=None):
    def body(q_ref, k_ref, v_ref, t_ref, y_ref, l_ref, dy_ref, dq_ref, dk_ref, dv_ref, db_ref,
             qs, ks, vs, dys, ls, dds, dn_nat, dq_perm, dk_perm, dv_perm, b_ref):
        _bias_tiles(t_ref, b_ref)
        zero_pad = jnp.zeros((QB, 128), MXU)
        ks[0:QB, :] = zero_pad
        vs[0:QB, :] = zero_pad
        lane = lax.broadcasted_iota(jnp.int32, (QB, 128), 1)
        lane_s = lax.broadcasted_iota(jnp.int32, (S, 128), 1)
        t = dy_ref[...] * y_ref[...]
        sa = jnp.sum(jnp.where(lane_s < 64, t, 0.0), axis=1, keepdims=True)
        sb = jnp.sum(jnp.where(lane_s >= 64, t, 0.0), axis=1, keepdims=True)
        dn_nat[...] = jnp.where(lane_s < 64, sa, sb)
        dq_ref[...] = jnp.zeros((S, 128), F32)
        dk_ref[...] = jnp.zeros((S, 128), F32)
        dv_ref[...] = jnp.zeros((S, 128), F32)
        db_ref[...] = jnp.zeros((3, 2, QB, 2 * QB), F32)
        for p, (_, d) in enumerate(PATTERNS):
            L = S // d
            nb = L // QB
            _permute_in(qs, q_ref, d, scale=0.125, pad=0)
            _permute_in(ks, k_ref, d)
            _permute_in(vs, v_ref, d)
            _permute_in(dys, dy_ref, d, pad=0)
            _permute_in(ls, l_ref, d, pad=0)
            _permute_in(dds, dn_nat, d, pad=0)
            dk_perm[...] = jnp.zeros((S + QB, 128), F32)
            dv_perm[...] = jnp.zeros((S + QB, 128), F32)

            def blk(b, carry):
                r0 = pl.multiple_of(b * QB, QB)
                q = qs[pl.ds(r0, QB), :]
                kb = ks[pl.ds(r0, 2 * QB), :]
                vb = vs[pl.ds(r0, 2 * QB), :]
                dyb = dys[pl.ds(r0, QB), :]
                lb = ls[pl.ds(r0, QB), :]
                db = dds[pl.ds(r0, QB), :]
                first = (b % nb) == 0
                dqs = []
                dkb = jnp.zeros((2 * QB, 128), F32)
                dvb = jnp.zeros((2 * QB, 128), F32)
                for hh in range(2):
                    sel = (lane < 64) if hh == 0 else (lane >= 64)
                    c0 = 64 * hh
                    qm = jnp.where(sel, q, jnp.zeros_like(q))
                    dym = jnp.where(sel, dyb, jnp.zeros_like(dyb))
                    s = _mm_nt(qm, kb) + jnp.where(first, b_ref[p, hh, 1], b_ref[p, hh, 0])
                    pr = jnp.exp(s - lb[:, c0:c0 + 1])
                    dp = _mm_nt(dym, vb)
                    ds = pr * (dp - db[:, c0:c0 + 1])
                    db_ref[p, hh] += ds
                    dqs.append(_mm(ds, kb))
                    dkb = dkb + _mm_tn(ds, qm)
                    dvb = dvb + _mm_tn(pr, dym)
                dq_perm[pl.ds(r0, QB), :] = jnp.where(lane < 64, dqs[0], dqs[1])
                dk_perm[pl.ds(r0, 2 * QB), :] += dkb
                dv_perm[pl.ds(r0, 2 * QB), :] += dvb
                return carry

            lax.fori_loop(0, S // QB, blk, 0)
            for r in range(d):
                idx = pl.ds(r, L, stride=d) if d > 1 else pl.ds(0, S)
                dq_ref[idx, :] += dq_perm[r * L:(r + 1) * L, :] * 0.125
                dk_ref[idx, :] += dk_perm[QB + r * L:QB + (r + 1) * L, :]
                dv_ref[idx, :] += dv_perm[QB + r * L:QB + (r + 1) * L, :]

    col = lambda c0: pl.BlockSpec((S, 128), lambda hp: (0, c0 + hp))
    bspec = pl.BlockSpec((3, 2, 8, 2 * QB), lambda hp: (0, hp, 0, 0))
    return _call(
        body, name=name, grid=(N_HEADS // 2,),
        in_specs=[col(0), col(4), col(8), bspec, col(0), col(0), col(0)],
        out_specs=(col(0), col(0), col(0), pl.BlockSpec((3, 2, QB, 2 * QB), lambda hp: (0, hp, 0, 0))),
        out_shape=(_sds((S, D_ATT), F32), _sds((S, D_ATT), F32), _sds((S, D_ATT), F32),
                   _sds((3, N_HEADS, QB, 2 * QB), F32)),
        scratch=[pltpu.VMEM((S, 128), MXU), pltpu.VMEM((S + QB, 128), MXU), pltpu.VMEM((S + QB, 128), MXU),
                 pltpu.VMEM((S, 128), MXU), pltpu.VMEM((S, 128), F32), pltpu.VMEM((S, 128), F32),
                 pltpu.VMEM((S, 128), F32), pltpu.VMEM((S, 128), F32),
                 pltpu.VMEM((S + QB, 128), F32), pltpu.VMEM((S + QB, 128), F32),
                 pltpu.VMEM((3, 2, 2, QB, 2 * QB), F32)],
        cp=_cp(48, ("arbitrary",)), args=(z, z, z, bias, y, lse, dy), ride=ride)


def relbias_grad(dbiases):
    bucket, band, _ = _att_static()
    onehot = (bucket[:, None] == np.arange(N_BUCKETS)[None, :, None, None]) & band[:, None]
    onehot = jnp.asarray(onehot.reshape(3, N_BUCKETS, QB * 2 * QB), BF16)

    def body(db0_ref, db1_ref, oh_ref, o_ref):
        acc = jnp.zeros((N_HEADS, N_BUCKETS), F32)
        for p in range(3):
            acc = acc + lax.dot_general(db0_ref[p] + db1_ref[p], oh_ref[p].astype(F32), (((1,), (1,)), ((), ())),
                                        preferred_element_type=F32, precision=lax.Precision.HIGHEST)
        o_ref[...] = acc

    vm = pl.BlockSpec(memory_space=pltpu.VMEM)
    out = pl.pallas_call(body, name="relbias_grad", in_specs=[vm, vm, vm], out_specs=vm,
                         out_shape=_sds((N_HEADS, N_BUCKETS), F32), compiler_params=_cp(40))(
        *[d.reshape(3, N_HEADS, QB * 2 * QB) for d in dbiases], onehot)
    return out.T


def _panel(t_ref, ri, j):
    return t_ref[ri, pl.ds(j, S, stride=8), :]


def _gelu(x):
    c = math.sqrt(2.0 / math.pi)
    th = jnp.tanh(c * (x + 0.044715 * x * x * x))
    return 0.5 * x * (1.0 + th), th


def ssm_fwd(z, a, bre, bim, cre, cim, dsk, gluw, glub, name):
    def body(u_ref, a_ref, bre_ref, bim_ref, cre_ref, cim_ref, d_ref, gw_ref, gb_ref, y_ref, yp_ref, st_hbm, st_ref):
        u = u_ref[...]
        for j in range(8):
            st_ref[0, pl.ds(j, S, stride=8), :] = _mm(u, bre_ref[:, 128 * j:128 * (j + 1)])
            st_ref[1, pl.ds(j, S, stride=8), :] = _mm(u, bim_ref[:, 128 * j:128 * (j + 1)])
        ar, ai = a_ref[0], a_ref[1]

        def step(t, c):
            re, im = c
            i = pl.multiple_of(t * 8, 8)
            nre = ar * re - ai * im + st_ref[0, pl.ds(i, 8), :]
            nim = ar * im + ai * re + st_ref[1, pl.ds(i, 8), :]
            st_ref[0, pl.ds(i, 8), :] = nre
            st_ref[1, pl.ds(i, 8), :] = nim
            return nre, nim

        zero = jnp.zeros((8, 128), F32)
        lax.fori_loop(0, S, step, (zero, zero), unroll=8)
        y = d_ref[...] * u
        for j in range(8):
            y = y + _mm(_panel(st_ref, 0, j), cre_ref[128 * j:128 * (j + 1), :])
            y = y - _mm(_panel(st_ref, 1, j), cim_ref[128 * j:128 * (j + 1), :])
        pltpu.sync_copy(st_ref, st_hbm)
        yp_ref[...] = y
        gl, _ = _gelu(y)
        tt = _mm(gl, gw_ref[...].reshape(D_SSM, D_SSM)) + gb_ref[...]
        y_ref[...] = y * jax.nn.sigmoid(tt)

    vm = lambda shape: pl.BlockSpec(shape, lambda i: (0,) * len(shape))
    return pl.pallas_call(
        body, name=name, grid=(1,),
        in_specs=[pl.BlockSpec((S, 256), lambda i: (0, 6)), vm((2, 8, 128)), vm((256, 1024)), vm((256, 1024)),
                  vm((1024, 256)), vm((1024, 256)), vm((1, 256)),
                  vm((NDEV, 32, 256)), vm((1, 256))],
        out_specs=(vm((S, 256)), vm((S, 256)), pl.BlockSpec(memory_space=pl.ANY)),
        out_shape=(_sds((S, 256), F32), _sds((S, 256), F32), _sds((2, S * 8, 128), F32)),
        scratch_shapes=[pltpu.VMEM((2, S * 8, 128), F32)],
        compiler_params=_cp(40, ("arbitrary",)))(z, a, bre, bim, cre, cim, dsk, gluw, glub)


def ssm_bwd(dy, z, ypre, st, a, bre, bim, cre, cim, dsk, gluw, glub, name, ride=None):
    def body(dy_ref, u_ref, yp_ref, st_hbm, a_ref, bre_ref, bim_ref, cre_ref, cim_ref, d_ref, gw_ref, gb_ref,
             du_ref, dbre_ref, dbim_ref, dcre_ref, dcim_ref, da_ref, dd_ref, dgw_ref, dgb_ref, g_ref, st_ref):
        pltpu.sync_copy(st_hbm, st_ref)
        u = u_ref[...]
        y = yp_ref[...]
        dout = dy_ref[...]
        gw = gw_ref[...].reshape(D_SSM, D_SSM)
        gl, th = _gelu(y)
        sig = jax.nn.sigmoid(_mm(gl, gw) + gb_ref[...])
        dt = dout * y * sig * (1.0 - sig)
        dgw_ref[...] = _mm_tn(gl, dt)
        dgb_ref[...] = jnp.sum(dt, axis=0, keepdims=True)
        c = math.sqrt(2.0 / math.pi)
        dgelu = 0.5 * (1.0 + th) + 0.5 * y * (1.0 - th * th) * c * (1.0 + 3.0 * 0.044715 * y * y)
        dyv = dout * sig + _mm_nt(dt, gw) * dgelu
        dd_ref[...] = jnp.sum(dyv * u, axis=0, keepdims=True)
        for j in range(8):
            rows = slice(128 * j, 128 * (j + 1))
            g_ref[0, pl.ds(j, S, stride=8), :] = _mm_nt(dyv, cre_ref[rows, :])
            g_ref[1, pl.ds(j, S, stride=8), :] = -_mm_nt(dyv, cim_ref[rows, :])
            dcre_ref[rows, :] = _mm_tn(_panel(st_ref, 0, j), dyv)
            dcim_ref[rows, :] = -_mm_tn(_panel(st_ref, 1, j), dyv)
        ar, ai = a_ref[0], a_ref[1]

        def step(k, c4):
            gre, gim, dar, dai = c4
            i = pl.multiple_of((S - 1 - k) * 8, 8)
            nre = g_ref[0, pl.ds(i, 8), :] + ar * gre + ai * gim
            nim = g_ref[1, pl.ds(i, 8), :] + ar * gim - ai * gre
            g_ref[0, pl.ds(i, 8), :] = nre
            g_ref[1, pl.ds(i, 8), :] = nim
            sre = st_ref[0, pl.ds(i - 8, 8), :]
            sim = st_ref[1, pl.ds(i - 8, 8), :]
            return nre, nim, dar + nre * sre + nim * sim, dai + nim * sre - nre * sim

        zero = jnp.zeros((8, 128), F32)
        gre, gim, dar, dai = lax.fori_loop(0, S - 1, step, (zero, zero, zero, zero), unroll=8)
        g_ref[0, 0:8, :] = g_ref[0, 0:8, :] + ar * gre + ai * gim
        g_ref[1, 0:8, :] = g_ref[1, 0:8, :] + ar * gim - ai * gre
        da_ref[0] = dar
        da_ref[1] = dai
        du = dyv * d_ref[...]
        for j in range(8):
            cols = slice(128 * j, 128 * (j + 1))
            gr, gi = _panel(g_ref, 0, j), _panel(g_ref, 1, j)
            dbre_ref[:, cols] = _mm_tn(u, gr)
            dbim_ref[:, cols] = _mm_tn(u, gi)
            du = du + _mm_nt(gr, bre_ref[:, cols]) + _mm_nt(gi, bim_ref[:, cols])
        du_ref[...] = du

    vm = lambda shape: pl.BlockSpec(shape, lambda i: (0,) * len(shape))
    return _call(
        body, name=name, grid=(1,),
        in_specs=[vm((S, 256)), pl.BlockSpec((S, 256), lambda i: (0, 6)), vm((S, 256)), pl.BlockSpec(memory_space=pl.ANY),
                  vm((2, 8, 128)), vm((256, 1024)), vm((256, 1024)), vm((1024, 256)), vm((1024, 256)), vm((1, 256)),
                  vm((NDEV, 32, 256)), vm((1, 256))],
        out_specs=(vm((S, 256)), vm((256, 1024)), vm((256, 1024)), vm((1024, 256)), vm((1024, 256)),
                   vm((2, 8, 128)), vm((1, 256)), vm((256, 256)), vm((1, 256))),
        out_shape=(_sds((S, 256), F32), _sds((256, 1024), F32), _sds((256, 1024), F32), _sds((1024, 256), F32),
                   _sds((1024, 256), F32), _sds((2, 8, 128), F32), _sds((1, 256), F32), _sds((256, 256), F32),
                   _sds((1, 256), F32)),
        scratch=[pltpu.VMEM((2, S * 8, 128), F32), pltpu.VMEM((2, S * 8, 128), F32)],
        cp=_cp(56, ("arbitrary",)), args=(dy, z, ypre, st, a, bre, bim, cre, cim, dsk, gluw, glub), ride=ride)


def _ssm_discretise(a_re, a_im, log_dt, b_re, b_im):
    dt = jnp.exp(log_dt)[:, None]
    er = jnp.exp(a_re * dt)
    abr, abi = er * jnp.cos(a_im * dt), er * jnp.sin(a_im * dt)
    den = a_re * a_re + a_im * a_im
    fr = ((abr - 1.0) * a_re + abi * a_im) / den
    fi = (abi * a_re - (abr - 1.0) * a_im) / den
    bbr = fr[:, :, None] * b_re - fi[:, :, None] * b_im
    bbi = fr[:, :, None] * b_im + fi[:, :, None] * b_re
    return abr, abi, bbr, bbi


def _blockdiag(t):
    g, r, c = t.shape
    eye = jnp.eye(g, dtype=t.dtype)
    return (t[:, :, None, :] * eye[:, None, :, None]).reshape(g * r, g * c)


def _blockdiag_take(m, r, c):
    g = m.shape[0] // r
    idx = jnp.arange(g)
    return m.reshape(g, r, g, c)[idx, :, idx, :]


PAD = 16


def _pool_lane_select(vals):
    lane = lax.broadcasted_iota(jnp.int32, vals[0].shape, 1)
    out = vals[3]
    for g in (2, 1, 0):
        out = jnp.where(lane < 64 * (g + 1), vals[g], out)
    return out


def _pool_counts():
    row = lax.broadcasted_iota(jnp.int32, (S, D_POOL), 0).astype(F32) + 1.0
    return _pool_lane_select([jnp.minimum(row, float(w)) for w in POOL_WINDOWS])


def _pooled(u, sa, sb):
    sums = []
    cur = u
    bufs = (sa, sb)
    for k, sh in enumerate((1, 2, 4, 8)):
        buf = bufs[k % 2]
        buf[PAD:PAD + S, :] = cur
        cur = cur + buf[PAD - sh:PAD - sh + S, :]
        sums.append(cur)
    return _pool_lane_select(sums) / _pool_counts() - u


def pool_fwd(z, pw, psc, name):
    def body(u_ref, w_ref, s_ref, y_ref, sa, sb):
        for buf in (sa, sb):
            buf[0:PAD, :] = jnp.zeros((PAD, D_POOL), F32)
        pooled = _pooled(u_ref[...], sa, sb)
        y_ref[...] = _mm(pooled, w_ref[...]) * s_ref[...]

    vm = lambda shape: pl.BlockSpec(shape, lambda i: (0,) * len(shape))
    return pl.pallas_call(
        body, name=name, grid=(1,),
        in_specs=[pl.BlockSpec((S, 256), lambda i: (0, 7)), vm((256, 256)), vm((1, 256))],
        out_specs=vm((S, 256)), out_shape=_sds((S, 256), F32),
        scratch_shapes=[pltpu.VMEM((S + 2 * PAD, D_POOL), F32)] * 2,
        compiler_params=_cp(40, ("arbitrary",)))(z, pw, psc)


def pool_bwd(dy, z, pw, psc, name):
    def body(dy_ref, u_ref, w_ref, s_ref, du_ref, dw_ref, ds_ref, sa, sb):
        for buf in (sa, sb):
            buf[0:PAD, :] = jnp.zeros((PAD, D_POOL), F32)
            buf[PAD + S:PAD + S + PAD, :] = jnp.zeros((PAD, D_POOL), F32)
        pooled = _pooled(u_ref[...], sa, sb)
        dyv = dy_ref[...]
        w = w_ref[...]
        ds_ref[...] = jnp.sum(dyv * _mm(pooled, w), axis=0, keepdims=True)
        dyl = dyv * s_ref[...]
        dw_ref[...] = _mm_tn(pooled, dyl)
        dpool = _mm_nt(dyl, w)
        cur = dpool / _pool_counts()
        sums = []
        bufs = (sa, sb)
        for k, sh in enumerate((1, 2, 4, 8)):
            buf = bufs[k % 2]
            buf[PAD:PAD + S, :] = cur
            cur = cur + buf[PAD + sh:PAD + sh + S, :]
            sums.append(cur)
        du_ref[...] = _pool_lane_select(sums) - dpool

    vm = lambda shape: pl.BlockSpec(shape, lambda i: (0,) * len(shape))
    return pl.pallas_call(
        body, name=name, grid=(1,),
        in_specs=[vm((S, 256)), pl.BlockSpec((S, 256), lambda i: (0, 7)), vm((256, 256)), vm((1, 256))],
        out_specs=(vm((S, 256)), vm((256, 256)), vm((1, 256))),
        out_shape=(_sds((S, 256), F32), _sds((256, 256), F32), _sds((1, 256), F32)),
        scratch_shapes=[pltpu.VMEM((S + 2 * PAD, D_POOL), F32)] * 2,
        compiler_params=_cp(40, ("arbitrary",)))(dy, z, pw, psc)


def ada_fwd(c_all, ada_w, ada_b_cols):
    def body(c_ref, w_ref, b_ref, o_ref):
        c = c_ref[...]
        cond = c * jax.nn.sigmoid(c)
        o_ref[...] = jnp.dot(cond, w_ref[...], preferred_element_type=F32, precision=lax.Precision.HIGHEST) + b_ref[...]

    return pl.pallas_call(
        body, name="ada_fwd", grid=(DEPTH,),
        in_specs=[pl.BlockSpec((NDEV, D), lambda l: (0, 0)), pl.BlockSpec((None, D, 1152), lambda l: (l, 0, 0)),
                  pl.BlockSpec((None, 1, 1152), lambda l: (l, 0, 0))],
        out_specs=pl.BlockSpec((None, NDEV, 1152), lambda l: (l, 0, 0)), out_shape=_sds((DEPTH, NDEV, 1152), F32),
        compiler_params=_cp(40, ("arbitrary",)))(c_all, ada_w, ada_b_cols)


def ada_bwd(c_all, dmod_cols):
    def body(c_ref, dm_ref, o_ref):
        c = c_ref[...]
        cond = c * jax.nn.sigmoid(c)
        o_ref[...] = lax.dot_general(cond, dm_ref[...], (((0,), (0,)), ((), ())), preferred_element_type=F32,
                                     precision=lax.Precision.HIGHEST)

    return pl.pallas_call(
        body, name="ada_bwd", grid=(DEPTH,),
        in_specs=[pl.BlockSpec((NDEV, D), lambda l: (0, 0)), pl.BlockSpec((None, NDEV, 1152), lambda l: (l, 0, 0))],
        out_specs=pl.BlockSpec((None, D, 1152), lambda l: (l, 0, 0)), out_shape=_sds((DEPTH, D, 1152), F32),
        compiler_params=_cp(40, ("arbitrary",)))(c_all, dmod_cols)


def _adamw(w, g, m, v):
    m2 = B1 * m + (1.0 - B1) * g
    v2 = B2 * v + (1.0 - B2) * (g * g)
    m_hat = m2 / (1.0 - B1 ** STEP)
    v_hat = v2 / (1.0 - B2 ** STEP)
    return -LR * (m_hat / (jnp.sqrt(v_hat) + EPS) + WD * w), m2, v2


def _sum8(ref):
    g = ref[0].astype(F32)
    for s in range(1, NDEV):
        g = g + ref[s].astype(F32)
    return g


def adam_rs(recv, w, m, v, tr, name):
    r, cdim = w.shape
    cp = recv.shape[2]

    def body(rc_ref, w_ref, m_ref, v_ref, g_ref, d_ref, m2_ref, v2_ref):
        g = _sum8(rc_ref)[:, :cdim]
        g_ref[...] = g
        d_ref[...], m2_ref[...], v2_ref[...] = _adamw(w_ref[...], g, m_ref[...], v_ref[...])

    rs = pl.BlockSpec((tr, cdim), lambda i: (i, 0))
    return pl.pallas_call(
        body, name=name, grid=(r // tr,),
        in_specs=[pl.BlockSpec((NDEV, tr, cp), lambda i: (0, i, 0)), rs, rs, rs],
        out_specs=(rs, rs, rs, rs), out_shape=tuple(_sds((r, cdim), F32) for _ in range(4)),
        compiler_params=_cp(48, ("arbitrary",)))(recv, w, m, v)


def adam_rs_rows(recv, w, m, v, name):
    half = FB // 2

    def body(rc_ref, w_ref, m_ref, v_ref, g_ref, d_ref, m2_ref, v2_ref):
        g = _sum8(rc_ref)
        g_ref[...] = g
        d_ref[...], m2_ref[...], v2_ref[...] = _adamw(w_ref[...], g, m_ref[...], v_ref[...])

    rs = pl.BlockSpec((None, half, D), lambda a, i: (a, i, 0))
    return pl.pallas_call(
        body, name=name, grid=(4, 2),
        in_specs=[pl.BlockSpec((NDEV, None, half, D), lambda a, i: (0, a, i, 0)), rs, rs, rs],
        out_specs=(rs, rs, rs, rs), out_shape=tuple(_sds((4, FB, D), F32) for _ in range(4)),
        compiler_params=_cp(48, ("arbitrary", "arbitrary")))(recv, w, m, v)


def adam_plain(g, w, m, v, tr, name):
    r, cdim = w.shape

    def body(g_ref, w_ref, m_ref, v_ref, d_ref, m2_ref, v2_ref):
        d_ref[...], m2_ref[...], v2_ref[...] = _adamw(w_ref[...], g_ref[...], m_ref[...], v_ref[...])

    rs = pl.BlockSpec((tr, cdim), lambda i: (i, 0))
    return pl.pallas_call(
        body, name=name, grid=(r // tr,), in_specs=[rs, rs, rs, rs], out_specs=(rs, rs, rs),
        out_shape=tuple(_sds((r, cdim), F32) for _ in range(3)),
        compiler_params=_cp(48, ("arbitrary",)))(g, w, m, v)


def sum_sources(recv, name):
    r = recv.shape[1]

    def body(rc_ref, o_ref):
        o_ref[...] = _sum8(rc_ref)

    vm = pl.BlockSpec(memory_space=pltpu.VMEM)
    return pl.pallas_call(body, name=name, in_specs=[vm], out_specs=vm, out_shape=_sds((r, 128), F32),
                          compiler_params=_cp(40))(recv)


def _pack(arrs):
    flat = jnp.concatenate([a.reshape(-1) for a in arrs])
    n = flat.shape[0]
    rows = -(-n // 1024) * 8
    return jnp.pad(flat, (0, rows * 128 - n)).reshape(rows, 128)


def _unpack(vec, shapes):
    flat = vec.reshape(-1)
    out, o = [], 0
    for sh in shapes:
        n = int(np.prod(sh))
        out.append(flat[o:o + n].reshape(sh))
        o += n
    return out


WEIGHTS = ['rel_bias', 'ada_w', 'ada_b', 'ln_g', 'ln_b', 'ffn_w_gate', 'ffn_w_up', 'ffn_w_down', 'w_in', 'w_out',
           'ssm_a_re', 'ssm_a_im', 'ssm_log_dt', 'ssm_b_re', 'ssm_b_im', 'ssm_c_re', 'ssm_c_im', 'ssm_d', 'glu_w',
           'glu_b', 'pool_w', 'pool_scale']
SMALL = ['rel_bias', 'ada_b', 'ln_g', 'ln_b', 'ssm_a_re', 'ssm_a_im', 'ssm_log_dt', 'ssm_b_re', 'ssm_b_im',
         'ssm_c_re', 'ssm_c_im', 'ssm_d', 'glu_b', 'pool_w', 'pool_scale']
SMALL_FULL_SHAPES = {'rel_bias': (32, 8), 'ada_b': (2, 9216), 'ln_g': (2, 3, 1024), 'ln_b': (2, 3, 1024),
                     'ssm_a_re': (2, 16, 64), 'ssm_a_im': (2, 16, 64), 'ssm_log_dt': (2, 16),
                     'ssm_b_re': (2, 16, 64, 16), 'ssm_b_im': (2, 16, 64, 16), 'ssm_c_re': (2, 16, 16, 64),
                     'ssm_c_im': (2, 16, 16, 64), 'ssm_d': (2, 256), 'glu_b': (2, 256), 'pool_w': (2, 4, 64, 64),
                     'pool_scale': (2, 256)}


def _step(P):
    me = _me()
    x0 = P['x'][0]
    target = P['loss_target'][0]

    def shards(l, sub):
        bf = lambda a: a.astype(BF16)
        if sub == 1:
            return [bf(P['w_in'][l]), bf(P['w_out'][l]), bf(P['glu_w'][l])]
        f = sub // 2
        padc = lambda a: jnp.pad(bf(a), ((0, 0), (0, FBP - FB)))
        return [padc(P['ffn_w_gate'][l, f]), padc(P['ffn_w_up'][l, f]), jnp.pad(bf(P['ffn_w_down'][l, f]), ((0, FBP - FB), (0, 0)))]

    order = [(l, sub) for l in range(DEPTH) for sub in range(3)]
    nxt = dict(zip(order[:-1], order[1:]))
    W = {}
    c_all, lng_all, lnb_all, *W[order[0]] = _exchange(Gather([P['c'], P['ln_g'], P['ln_b']] + shards(*order[0])), "gather_first")
    c_all = c_all.reshape(NDEV, D)
    ln_g = jnp.transpose(lng_all, (1, 2, 0, 3)).reshape(DEPTH, 3, D)
    ln_b = jnp.transpose(lnb_all, (1, 2, 0, 3)).reshape(DEPTH, 3, D)

    ada_b_cols = lax.dynamic_slice_in_dim(P['ada_b'], me * 1152, 1152, axis=1).reshape(DEPTH, 1, 1152)
    modc = ada_fwd(c_all, P['ada_w'], ada_b_cols)
    (mod_all,) = _exchange(Gather([modc]), "gather_mod")
    mod_me = lax.dynamic_index_in_dim(mod_all, me, axis=2, keepdims=False)
    mod = jnp.transpose(mod_me, (1, 0, 2)).reshape(DEPTH, 9, D)

    bias = att_bias(P['rel_bias'])
    ssm = []
    for l in range(DEPTH):
        prm = (P['ssm_a_re'][l], P['ssm_a_im'][l], P['ssm_log_dt'][l], P['ssm_b_re'][l], P['ssm_b_im'][l])
        (abr, abi, bbr, bbi), disc_vjp = jax.vjp(_ssm_discretise, *prm)
        ssm.append(dict(
            vjp=disc_vjp, a=jnp.stack([abr.reshape(8, 128), abi.reshape(8, 128)]),
            bre=_blockdiag(jnp.transpose(bbr, (0, 2, 1))).astype(MXU), bim=_blockdiag(jnp.transpose(bbi, (0, 2, 1))).astype(MXU),
            cre=_blockdiag(jnp.transpose(P['ssm_c_re'][l], (0, 2, 1))).astype(MXU),
            cim=_blockdiag(jnp.transpose(P['ssm_c_im'][l], (0, 2, 1))).astype(MXU),
            d=P['ssm_d'][l].reshape(1, 256), gb=P['glu_b'][l].reshape(1, 256),
            pw=_blockdiag(P['pool_w'][l]).astype(MXU), psc=P['pool_scale'][l].reshape(1, 256)))

    saved = []
    x = x0
    for l, sub in order:
        tag = f"l{l}s{sub}"
        ride = Gather(shards(*nxt[(l, sub)])) if (l, sub) in nxt else None
        h = ln_mod_fwd(x, mod[l], sub, "ln_mod_fwd_" + tag)
        if sub != 1:
            wg, wu, wd = W[(l, sub)]
            (G, U, fo), got = ffn_fwd(h, wg, wu, wd, "ffn_fwd_" + tag, ride)
            saved.append(dict(x=x, h=h, G=G, U=U, f=fo))
            x = res_ln_fwd(x, fo, mod[l], sub, ln_g[l], ln_b[l], 0.5, "res_ln_fwd_" + tag)
        else:
            sp = ssm[l]
            win, wout, gluw = W[(l, sub)]
            z = win_fwd(h, win, "win_fwd_" + tag)
            (ya, lse), got = att_fwd(z, bias, "att_fwd_" + tag, ride)
            ys, ypre, st = ssm_fwd(z, sp['a'], sp['bre'], sp['bim'], sp['cre'], sp['cim'], sp['d'], gluw, sp['gb'], "ssm_fwd_" + tag)
            yp = pool_fwd(z, sp['pw'], sp['psc'], "pool_fwd_" + tag)
            o = wout_fwd(ya, ys, yp, wout, "wout_fwd_" + tag)
            saved.append(dict(x=x, h=h, z=z, ya=ya, lse=lse, ys=ys, ypre=ypre, st=st, yp=yp, f=o))
            x = res_ln_fwd(x, o, mod[l], sub, ln_g[l], ln_b[l], 1.0, "res_ln_fwd_" + tag)
        if ride is not None:
            W[nxt[(l, sub)]] = got

    loss_tile, dx = loss_fwd_bwd(x, target, "loss")
    loss = lax.psum(loss_tile[0, 0], ("x", "y", "c"))

    recv = dict(g=lax.empty((NDEV, DEPTH, 2, D, FBP), BF16), u=lax.empty((NDEV, DEPTH, 2, D, FBP), BF16),
                d=lax.empty((NDEV, DEPTH, 2, FBP, D), BF16), win=lax.empty((NDEV, DEPTH, D, 256), BF16),
                wout=lax.empty((NDEV, DEPTH, 128, D), BF16), glu=lax.empty((NDEV, DEPTH, 32, 256), BF16))
    pending = []

    def take(k):
        items = pending[:k]
        del pending[:k]
        if not items:
            return None, []
        return Scatter([p for _, _, p in items], [recv[key] for key, _, _ in items], [idx for _, idx, _ in items]), items

    def landed(items, bufs):
        for (key, _, _), b in zip(items, bufs):
            recv[key] = b

    dmod = [[None] * 9 for _ in range(DEPTH)]
    dlng = [[None] * 3 for _ in range(DEPTH)]
    dlnb = [[None] * 3 for _ in range(DEPTH)]
    dbiases = [None] * DEPTH
    small_l = [dict() for _ in range(DEPTH)]
    for l, sub in reversed(order):
        tag = f"l{l}s{sub}"
        sv = saved[3 * l + sub]
        w = 1.0 if sub == 1 else 0.5
        dxa, df, sums = res_ln_bwd(sv['x'], sv['f'], mod[l], sub, ln_g[l], dx, w, "res_ln_bwd_" + tag)
        dlng[l][sub], dlnb[l][sub], dmod[l][3 * sub + 2] = sums[0], sums[1], sums[2]
        if sub != 1:
            f = sub // 2
            wg, wu, wd = W[(l, sub)]
            ride, items = take({(1, 0): 3, (0, 0): 4}.get((l, sub), 2))
            (dwg, dwu, dwd, dh), bufs = ffn_bwd(df, sv['h'], sv['G'], sv['U'], wg, wu, wd, "ffn_bwd_" + tag, ride)
            landed(items, bufs)
            pending += [('g', (l, f), dwg), ('u', (l, f), dwu), ('d', (l, f), dwd)]
        else:
            sp = ssm[l]
            win, wout, gluw = W[(l, sub)]
            dya, dys, dyp, dwout = wout_bwd(df, sv['ya'], sv['ys'], sv['yp'], wout, "wout_bwd_" + tag)
            ride, items = take(2)
            (dq, dk, dv, dbiases[l]), bufs = att_bwd(sv['z'], bias, sv['ya'], sv['lse'], dya, "att_bwd_" + tag, ride)
            landed(items, bufs)
            ride, items = take(1)
            (dus, dbre, dbim, dcre, dcim, da, dd, dgw, dgb), bufs = ssm_bwd(
                dys, sv['z'], sv['ypre'], sv['st'], sp['a'], sp['bre'], sp['bim'], sp['cre'], sp['cim'], sp['d'],
                gluw, sp['gb'], "ssm_bwd_" + tag, ride)
            landed(items, bufs)
            dup, dpw, dpsc = pool_bwd(dyp, sv['z'], sp['pw'], sp['psc'], "pool_bwd_" + tag)
            dh, dwin = win_bwd((dq, dk, dv, dus, dup), sv['h'], win, "win_bwd_" + tag)
            pending += [('win', (l,), dwin), ('wout', (l,), dwout), ('glu', (l,), dgw.astype(BF16).reshape(NDEV, 32, 256))]
            d_are, d_aim, d_ldt, d_bre, d_bim = sp['vjp']((
                da[0].reshape(16, 64), da[1].reshape(16, 64),
                jnp.transpose(_blockdiag_take(dbre, 16, 64), (0, 2, 1)), jnp.transpose(_blockdiag_take(dbim, 16, 64), (0, 2, 1))))
            small_l[l] = dict(
                ssm_a_re=d_are, ssm_a_im=d_aim, ssm_log_dt=d_ldt, ssm_b_re=d_bre, ssm_b_im=d_bim,
                ssm_c_re=jnp.transpose(_blockdiag_take(dcre, 64, 16), (0, 2, 1)),
                ssm_c_im=jnp.transpose(_blockdiag_take(dcim, 64, 16), (0, 2, 1)),
                ssm_d=dd.reshape(256), glu_b=dgb.reshape(256), pool_w=_blockdiag_take(dpw, 64, 64), pool_scale=dpsc.reshape(256))
        dx, sums = ln_mod_bwd(sv['x'], dh, mod[l], sub, dxa, "ln_mod_bwd_" + tag)
        dmod[l][3 * sub], dmod[l][3 * sub + 1] = sums[0], sums[1]
    grad_x = dx[None]
    ride, items = take(len(pending))
    landed(items, _exchange(ride, "scatter_last"))

    small = {k: jnp.stack([small_l[l][k] for l in range(DEPTH)]) for k in small_l[0]}
    small['rel_bias'] = relbias_grad(dbiases)
    small['ada_b'] = jnp.stack([jnp.stack(dmod[l]).reshape(9 * D) for l in range(DEPTH)])
    small['ln_g'] = jnp.stack([jnp.stack(dlng[l]) for l in range(DEPTH)])
    small['ln_b'] = jnp.stack([jnp.stack(dlnb[l]) for l in range(DEPTH)])
    (small_all,) = _exchange(Gather([_pack([small[k] for k in SMALL])]), "gather_small")
    gsum = dict(zip(SMALL, _unpack(sum_sources(small_all, "sum_small"), [SMALL_FULL_SHAPES[k] for k in SMALL])))
    off = 256
    dmod_all = small_all.reshape(NDEV, -1)[:, off:off + DEPTH * 9 * D].reshape(NDEV, DEPTH, 9 * D)
    dmod_cols = jnp.transpose(lax.dynamic_slice_in_dim(dmod_all, me * 1152, 1152, axis=2), (1, 0, 2))
    g_ada_w = ada_bwd(c_all, dmod_cols)

    out = {}

    def put(name, g, d, m2, v2, shape):
        out['grad_' + name], out['delta_' + name] = g.reshape(shape), d.reshape(shape)
        out['new_m_' + name], out['new_v_' + name] = m2.reshape(shape), v2.reshape(shape)

    def flat2(name, cols):
        return [P[pre + name].reshape(-1, cols) for pre in ('', 'm_', 'v_')]

    for name, key in (('ffn_w_gate', 'g'), ('ffn_w_up', 'u')):
        put(name, *adam_rs(recv[key].reshape(NDEV, 4 * D, FBP), *flat2(name, FB), 256, "adam_" + name), P[name].shape)
    put('ffn_w_down', *adam_rs_rows(recv['d'].reshape(NDEV, 4, FBP, D), *[P[pre + 'ffn_w_down'].reshape(4, FB, D) for pre in ('', 'm_', 'v_')],
                                    "adam_ffn_w_down"), P['ffn_w_down'].shape)
    put('w_in', *adam_rs(recv['win'].reshape(NDEV, 2 * D, 256), *flat2('w_in', 256), 512, "adam_w_in"), P['w_in'].shape)
    put('w_out', *adam_rs(recv['wout'].reshape(NDEV, 256, D), *flat2('w_out', D), 128, "adam_w_out"), P['w_out'].shape)
    put('glu_w', *adam_rs(recv['glu'].reshape(NDEV, 64, 256), *flat2('glu_w', 256), 64, "adam_glu_w"), P['glu_w'].shape)
    w_, m_, v_ = flat2('ada_w', 1152)
    put('ada_w', g_ada_w, *adam_plain(g_ada_w.reshape(-1, 1152), w_, m_, v_, 256, "adam_ada_w"), P['ada_w'].shape)

    for k in ('ln_g', 'ln_b'):
        gsum[k] = lax.dynamic_slice_in_dim(gsum[k], me * 128, 128, axis=2)
    gs, ws, ms, vs = (_pack([src(k) for k in SMALL]) for src in
                      (lambda k: gsum[k], lambda k: P[k], lambda k: P['m_' + k], lambda k: P['v_' + k]))
    ds_, m2s, v2s = adam_plain(gs, ws, ms, vs, gs.shape[0], "adam_small")
    shapes = [P[k].shape for k in SMALL]
    for k, d, m2, v2 in zip(SMALL, _unpack(ds_, shapes), _unpack(m2s, shapes), _unpack(v2s, shapes)):
        put(k, gsum[k], d, m2, v2, P[k].shape)

    res = [loss, grad_x]
    for pre in ('grad_', 'delta_', 'new_m_', 'new_v_'):
        res += [out[pre + k] for k in WEIGHTS]
    return tuple(res)


def kernel(x, c, rel_bias, ada_w, ada_b, ln_g, ln_b, ffn_w_gate, ffn_w_up, ffn_w_down, w_in, w_out, ssm_a_re, ssm_a_im, ssm_log_dt, ssm_b_re, ssm_b_im, ssm_c_re, ssm_c_im, ssm_d, glu_w, glu_b, pool_w, pool_scale, loss_target, m_rel_bias, m_ada_w, m_ada_b, m_ln_g, m_ln_b, m_ffn_w_gate, m_ffn_w_up, m_ffn_w_down, m_w_in, m_w_out, m_ssm_a_re, m_ssm_a_im, m_ssm_log_dt, m_ssm_b_re, m_ssm_b_im, m_ssm_c_re, m_ssm_c_im, m_ssm_d, m_glu_w, m_glu_b, m_pool_w, m_pool_scale, v_rel_bias, v_ada_w, v_ada_b, v_ln_g, v_ln_b, v_ffn_w_gate, v_ffn_w_up, v_ffn_w_down, v_w_in, v_w_out, v_ssm_a_re, v_ssm_a_im, v_ssm_log_dt, v_ssm_b_re, v_ssm_b_im, v_ssm_c_re, v_ssm_c_im, v_ssm_d, v_glu_w, v_glu_b, v_pool_w, v_pool_scale):
    return _step(dict(locals()))
```

```python
import functools
import math

import numpy as np
import jax
import jax.numpy as jnp
from jax import lax
from jax.experimental import pallas as pl
from jax.experimental.pallas import tpu as pltpu

F32 = jnp.float32
BF16 = jnp.bfloat16
MXU = jnp.bfloat16

S = 2048
D = 1024
NDEV = 8
DEPTH = 2
D_ATT, D_SSM, D_POOL, D_IN = 512, 256, 256, 2048
N_HEADS = 8
FB = 352
FBP = 384
QB = 128
PATTERNS = ((128, 1), (512, 4), (2048, 16))
POOL_WINDOWS = (2, 4, 8, 16)
N_BUCKETS, MAX_DISTANCE = 32, 2048
ALPHA = (2 * DEPTH) ** 0.25
LN_EPS = 1e-5
NEG = -1e30
LR, B1, B2, EPS, WD, STEP = 0.001, 0.9, 0.999, 1e-08, 0.01, 10

TM = 256
TMM = 512
MIB = 1024 * 1024


def _cp(vmem_mib, sem=None):
    kw = dict(vmem_limit_bytes=vmem_mib * MIB)
    if sem is not None:
        kw["dimension_semantics"] = sem
    return pltpu.CompilerParams(**kw)


def _sds(shape, dtype):
    return jax.ShapeDtypeStruct(shape, dtype)


def _mm(a, b):
    return jnp.dot(a.astype(MXU), b.astype(MXU), preferred_element_type=F32)


def _mm_nt(a, b):
    return lax.dot_general(a.astype(MXU), b.astype(MXU), (((1,), (1,)), ((), ())), preferred_element_type=F32)


def _mm_tn(a, b):
    return lax.dot_general(a.astype(MXU), b.astype(MXU), (((0,), (0,)), ((), ())), preferred_element_type=F32)


def _ln_stats(x):
    mu = jnp.mean(x, axis=-1, keepdims=True)
    xc = x - mu
    var = jnp.mean(xc * xc, axis=-1, keepdims=True)
    rstd = lax.rsqrt(var + LN_EPS)
    return xc * rstd, rstd


def _ln_bwd(dn, n, rstd):
    return rstd * (dn - jnp.mean(dn, axis=-1, keepdims=True) - n * jnp.mean(dn * n, axis=-1, keepdims=True))


def _me():
    return 4 * lax.axis_index("x") + 2 * lax.axis_index("y") + lax.axis_index("c")


ANY = pl.BlockSpec(memory_space=pl.ANY)
PIN_BYTES = 1 << 19


def _pallas_call(*a, **k):
    big = lambda o: math.prod(o.shape) * o.dtype.itemsize >= PIN_BYTES
    pin = lambda o: pltpu.HBM(o.shape, o.dtype) if isinstance(o, jax.ShapeDtypeStruct) and big(o) else o
    osh = k["out_shape"]
    k["out_shape"] = tuple(pin(o) for o in osh) if isinstance(osh, (tuple, list)) else pin(osh)
    fn = pl.pallas_call(*a, **k)

    def run(*args):
        return fn(*[pltpu.with_memory_space_constraint(x, pltpu.HBM) if big(x) else x for x in args])
    return run


class Gather:
    def __init__(self, srcs):
        self.srcs = list(srcs)
        self.n = len(self.srcs)
        self.bufs = []
        self.out_shapes = [_sds((NDEV,) + a.shape, a.dtype) for a in self.srcs]
        self.sems = [pltpu.SemaphoreType.DMA((7 * self.n,)), pltpu.SemaphoreType.DMA((7 * self.n,)),
                     pltpu.SemaphoreType.DMA((self.n,))]

    def _parts(self, srcs, outs, sems):
        send_sems, recv_sems, loc_sems = sems
        x, y, c = lax.axis_index("x"), lax.axis_index("y"), lax.axis_index("c")
        me, sib = (x, y, c), (x, y, 1 - c)
        chips = [(1 - x, y), (x, 1 - y), (1 - x, 1 - y)]
        slot = lambda d: 4 * d[0] + 2 * d[1] + d[2]

        def copy(a, k, block, to, src=None):
            dst = outs[a].at[slot(block)]
            return pltpu.make_async_remote_copy(
                src_ref=dst if src is None else src, dst_ref=dst,
                send_sem=send_sems.at[7 * a + k], recv_sem=recv_sems.at[7 * a + k],
                device_id=to, device_id_type=pl.DeviceIdType.MESH)

        local = [pltpu.make_async_copy(srcs[a], outs[a].at[slot(me)], loc_sems.at[a]) for a in range(self.n)]
        return me, sib, chips, c, copy, local

    def start(self, srcs, bufs, outs, sems):
        me, sib, chips, c, copy, local = self._parts(srcs, outs, sems)
        for a in range(self.n):
            local[a].start()
            copy(a, 0, me, sib, src=srcs[a]).start()
            for j, chip in enumerate(chips):
                copy(a, 1 + j, me, (*chip, c), src=srcs[a]).start()

    def finish(self, srcs, bufs, outs, sems):
        me, sib, chips, c, copy, local = self._parts(srcs, outs, sems)
        for a in range(self.n):
            for j, chip in enumerate(chips):
                copy(a, 1 + j, (*chip, c), me).wait_recv()
                copy(a, 4 + j, (*chip, c), sib).start()
        for a in range(self.n):
            copy(a, 0, sib, me).wait_recv()
            copy(a, 0, me, sib, src=srcs[a]).wait_send()
            for j, chip in enumerate(chips):
                copy(a, 4 + j, (*chip, 1 - c), me).wait_recv()
                copy(a, 1 + j, me, (*chip, c), src=srcs[a]).wait_send()
                copy(a, 4 + j, (*chip, c), sib).wait_send()
            local[a].wait()


class Scatter:
    def __init__(self, srcs, bufs, index):
        self.srcs, self.bufs, self.index = list(srcs), list(bufs), [tuple(i) for i in index]
        self.n = len(self.srcs)
        self.out_shapes = [_sds(b.shape, b.dtype) for b in self.bufs]
        self.sems = [pltpu.SemaphoreType.DMA((7 * self.n,)), pltpu.SemaphoreType.DMA((7 * self.n,)),
                     pltpu.SemaphoreType.DMA((self.n,))]

    def _copies(self, srcs, outs, sems):
        send_sems, recv_sems, loc_sems = sems
        me = _me()

        def remote(a, k, slot):
            t = (me + k) % NDEV
            return pltpu.make_async_remote_copy(
                src_ref=srcs[a].at[t], dst_ref=outs[a].at[(slot,) + self.index[a]],
                send_sem=send_sems.at[7 * a + k - 1], recv_sem=recv_sems.at[7 * a + k - 1],
                device_id=(t // 4, (t // 2) % 2, t % 2), device_id_type=pl.DeviceIdType.MESH)

        local = [pltpu.make_async_copy(srcs[a].at[me], outs[a].at[(me,) + self.index[a]], loc_sems.at[a])
                 for a in range(self.n)]
        return me, remote, local

    def start(self, srcs, bufs, outs, sems):
        me, remote, local = self._copies(srcs, outs, sems)
        for a in range(self.n):
            local[a].start()
        for k in range(1, NDEV):
            for a in range(self.n):
                remote(a, k, me).start()

    def finish(self, srcs, bufs, outs, sems):
        me, remote, local = self._copies(srcs, outs, sems)
        for k in range(1, NDEV):
            for a in range(self.n):
                remote(a, k, (me - k) % NDEV).wait()
        for a in range(self.n):
            local[a].wait()


def _call(body, *, name, grid, in_specs, out_specs, out_shape, args, scratch=(), cp=None, ride=None):
    out_specs, out_shape, scratch = list(out_specs), list(out_shape), list(scratch)
    if ride is None:
        outs = _pallas_call(body, name=name, grid=grid, in_specs=list(in_specs), out_specs=tuple(out_specs),
                              out_shape=tuple(out_shape), scratch_shapes=scratch, compiler_params=cp)(*args)
        return list(outs), []
    nin, nout, nscr, n, nb = len(in_specs), len(out_specs), len(scratch), ride.n, len(ride.bufs)
    steps = list(grid)

    def wrapped(*refs):
        h_in, r_src, r_buf = refs[:nin], refs[nin:nin + n], refs[nin + n:nin + n + nb]
        o0 = nin + n + nb
        h_out, r_out = refs[o0:o0 + nout], refs[o0 + nout:o0 + nout + n]
        s0 = o0 + nout + n
        h_scr, sems = refs[s0:s0 + nscr], refs[s0 + nscr:]
        ids = [pl.program_id(a) for a in range(len(steps))]
        first = functools.reduce(jnp.logical_and, [i == 0 for i in ids])
        last = functools.reduce(jnp.logical_and, [i == s - 1 for i, s in zip(ids, steps)])

        @pl.when(first)
        def _():
            ride.start(r_src, r_buf, r_out, sems)

        body(*h_in, *h_out, *h_scr)

        @pl.when(last)
        def _():
            ride.finish(r_src, r_buf, r_out, sems)

    aliases = {nin + n + k: nout + k for k in range(nb)}
    outs = _pallas_call(
        wrapped, name=name, grid=grid, in_specs=list(in_specs) + [ANY] * (n + nb),
        out_specs=tuple(out_specs + [ANY] * n), out_shape=tuple(out_shape + ride.out_shapes),
        scratch_shapes=scratch + ride.sems, input_output_aliases=aliases, compiler_params=cp,
    )(*args, *ride.srcs, *ride.bufs)
    return list(outs[:nout]), list(outs[nout:])


def _exchange(ride, name):
    def body(dummy_ref, o_ref):
        o_ref[...] = dummy_ref[...]

    one = pl.BlockSpec((8, 128), lambda i: (0, 0))
    _, outs = _call(body, name=name, grid=(1,), in_specs=[one], out_specs=[one], out_shape=[_sds((8, 128), F32)],
                    args=(jnp.zeros((8, 128), F32),), ride=ride)
    return outs


def _row_spec(cols, tm=TM):
    return pl.BlockSpec((tm, cols), lambda i: (i, 0))


def _full_spec(shape):
    nd = len(shape)
    return pl.BlockSpec(shape, lambda i: (0,) * nd)


def ln_mod_fwd(x, mod, sub, name):
    def body(x_ref, mod_ref, h_ref):
        n, _ = _ln_stats(x_ref[...])
        shift = mod_ref[3 * sub:3 * sub + 1, :]
        scale = mod_ref[3 * sub + 1:3 * sub + 2, :]
        h_ref[...] = (n * (1.0 + scale) + shift).astype(MXU)

    return _pallas_call(
        body, name=name, grid=(S // TM,),
        in_specs=[_row_spec(D), _full_spec((9, D))], out_specs=_row_spec(D),
        out_shape=_sds((S, D), MXU), compiler_params=_cp(32, ("arbitrary",)))(x, mod)


def res_ln_fwd(x, f, mod, sub, lng, lnb, w, name):
    def body(x_ref, f_ref, mod_ref, g_ref, b_ref, o_ref):
        gate = mod_ref[3 * sub + 2:3 * sub + 3, :]
        r = ALPHA * x_ref[...] + (w * gate) * f_ref[...]
        n, _ = _ln_stats(r)
        o_ref[...] = n * g_ref[sub:sub + 1, :] + b_ref[sub:sub + 1, :]

    return _pallas_call(
        body, name=name, grid=(S // TM,),
        in_specs=[_row_spec(D), _row_spec(D), _full_spec((9, D)), _full_spec((3, D)), _full_spec((3, D))],
        out_specs=_row_spec(D), out_shape=_sds((S, D), F32),
        compiler_params=_cp(32, ("arbitrary",)))(x, f, mod, lng, lnb)


def res_ln_bwd(x, f, mod, sub, lng, dxo, w, name):
    def body(x_ref, f_ref, mod_ref, g_ref, dxo_ref, dxa_ref, df_ref, sums_ref):
        i = pl.program_id(0)
        gate = mod_ref[3 * sub + 2:3 * sub + 3, :]
        fv = f_ref[...]
        r = ALPHA * x_ref[...] + (w * gate) * fv
        n, rstd = _ln_stats(r)
        dxo = dxo_ref[...]
        dr = _ln_bwd(dxo * g_ref[sub:sub + 1, :], n, rstd)
        dxa_ref[...] = ALPHA * dr
        df_ref[...] = ((w * gate) * dr).astype(MXU)
        part = jnp.concatenate([
            jnp.sum(dxo * n, axis=0, keepdims=True),
            jnp.sum(dxo, axis=0, keepdims=True),
            jnp.sum(dr * fv, axis=0, keepdims=True) * w,
            jnp.zeros((5, D), F32)], axis=0)

        @pl.when(i == 0)
        def _():
            sums_ref[...] = part

        @pl.when(i > 0)
        def _():
            sums_ref[...] += part

    return _pallas_call(
        body, name=name, grid=(S // TM,),
        in_specs=[_row_spec(D), _row_spec(D), _full_spec((9, D)), _full_spec((3, D)), _row_spec(D)],
        out_specs=(_row_spec(D), _row_spec(D), _full_spec((8, D))),
        out_shape=(_sds((S, D), F32), _sds((S, D), MXU), _sds((8, D), F32)),
        compiler_params=_cp(32, ("arbitrary",)))(x, f, mod, lng, dxo)


def ln_mod_bwd(x, dh, mod, sub, dxa, name):
    def body(x_ref, dh_ref, mod_ref, dxa_ref, dx_ref, sums_ref):
        i = pl.program_id(0)
        scale = mod_ref[3 * sub + 1:3 * sub + 2, :]
        n, rstd = _ln_stats(x_ref[...])
        dh = dh_ref[...]
        dx_ref[...] = dxa_ref[...] + _ln_bwd(dh * (1.0 + scale), n, rstd)
        part = jnp.concatenate([
            jnp.sum(dh, axis=0, keepdims=True),
            jnp.sum(dh * n, axis=0, keepdims=True),
            jnp.zeros((6, D), F32)], axis=0)

        @pl.when(i == 0)
        def _():
            sums_ref[...] = part

        @pl.when(i > 0)
        def _():
            sums_ref[...] += part

    return _pallas_call(
        body, name=name, grid=(S // TM,),
        in_specs=[_row_spec(D), _row_spec(D), _full_spec((9, D)), _row_spec(D)],
        out_specs=(_row_spec(D), _full_spec((8, D))),
        out_shape=(_sds((S, D), F32), _sds((8, D), F32)),
        compiler_params=_cp(32, ("arbitrary",)))(x, dh, mod, dxa)


def loss_fwd_bwd(y, target, name):
    def body(y_ref, t_ref, l_ref, dy_ref):
        i = pl.program_id(0)
        e = y_ref[...] - t_ref[...]
        dy_ref[...] = e * (1.0 / D)
        part = jnp.zeros((8, 128), F32) + (0.5 / D) * jnp.sum(e * e)

        @pl.when(i == 0)
        def _():
            l_ref[...] = part

        @pl.when(i > 0)
        def _():
            l_ref[...] += part

    return _pallas_call(
        body, name=name, grid=(S // TM,),
        in_specs=[_row_spec(D), _row_spec(D)], out_specs=(_full_spec((8, 128)), _row_spec(D)),
        out_shape=(_sds((8, 128), F32), _sds((S, D), F32)),
        compiler_params=_cp(32, ("arbitrary",)))(y, target)


def _w3(rows, cols):
    return pl.BlockSpec((None, rows, cols), lambda j, i: (j, 0, 0))


def ffn_fwd(h, wg, wu, wd, name, ride=None):
    def body(h_ref, wg_ref, wu_ref, wd_ref, g_ref, u_ref, f_ref):
        j, i = pl.program_id(0), pl.program_id(1)
        hv = h_ref[...]
        g = _mm(hv, wg_ref[...])
        u = _mm(hv, wu_ref[...])
        g_ref[...] = g.astype(MXU)
        u_ref[...] = u.astype(MXU)
        a = g * jax.nn.sigmoid(g) * u
        part = _mm(a, wd_ref[...])
        rows = pl.ds(pl.multiple_of(i * TMM, TMM), TMM)

        @pl.when(j == 0)
        def _():
            f_ref[rows, :] = part

        @pl.when(j > 0)
        def _():
            f_ref[rows, :] += part

    gu = pl.BlockSpec((None, TMM, FBP), lambda j, i: (j, i, 0))
    return _call(
        body, name=name, grid=(NDEV, S // TMM),
        in_specs=[pl.BlockSpec((TMM, D), lambda j, i: (i, 0)), _w3(D, FBP), _w3(D, FBP), _w3(FBP, D)],
        out_specs=(gu, gu, pl.BlockSpec((S, D), lambda j, i: (0, 0))),
        out_shape=(_sds((NDEV, S, FBP), MXU), _sds((NDEV, S, FBP), MXU), _sds((S, D), F32)),
        cp=_cp(48, ("arbitrary", "arbitrary")), args=(h, wg, wu, wd), ride=ride)


def ffn_bwd(df, h, g, u, wg, wu, wd, name, ride=None):
    ni = S // TMM

    def body(df_ref, h_ref, g_ref, u_ref, wg_ref, wu_ref, wd_ref, dwg_ref, dwu_ref, dwd_ref, dh_ref,
             ag_ref, au_ref, ad_ref):
        j, i = pl.program_id(0), pl.program_id(1)
        dfv, hv = df_ref[...], h_ref[...]
        gv, uv = g_ref[...].astype(F32), u_ref[...].astype(F32)
        da = _mm_nt(dfv, wd_ref[...])
        sg = jax.nn.sigmoid(gv)
        silu = gv * sg
        du = da * silu
        dg = da * uv * (sg * (1.0 + gv * (1.0 - sg)))
        p_d = _mm_tn(silu * uv, dfv)
        p_g = _mm_tn(hv, dg)
        p_u = _mm_tn(hv, du)

        @pl.when(i == 0)
        def _():
            ad_ref[...] = p_d
            ag_ref[...] = p_g
            au_ref[...] = p_u

        @pl.when(i > 0)
        def _():
            ad_ref[...] += p_d
            ag_ref[...] += p_g
            au_ref[...] += p_u

        @pl.when(i == ni - 1)
        def _():
            dwd_ref[...] = ad_ref[...].astype(BF16)
            dwg_ref[...] = ag_ref[...].astype(BF16)
            dwu_ref[...] = au_ref[...].astype(BF16)

        part = _mm_nt(dg, wg_ref[...]) + _mm_nt(du, wu_ref[...])
        rows = pl.ds(pl.multiple_of(i * TMM, TMM), TMM)

        @pl.when(j == 0)
        def _():
            dh_ref[rows, :] = part

        @pl.when(j > 0)
        def _():
            dh_ref[rows, :] += part

    gu = pl.BlockSpec((None, TMM, FBP), lambda j, i: (j, i, 0))
    rowt = pl.BlockSpec((TMM, D), lambda j, i: (i, 0))
    return _call(
        body, name=name, grid=(NDEV, ni),
        in_specs=[rowt, rowt, gu, gu, _w3(D, FBP), _w3(D, FBP), _w3(FBP, D)],
        out_specs=(_w3(D, FBP), _w3(D, FBP), _w3(FBP, D), pl.BlockSpec((S, D), lambda j, i: (0, 0))),
        out_shape=(_sds((NDEV, D, FBP), BF16), _sds((NDEV, D, FBP), BF16), _sds((NDEV, FBP, D), BF16), _sds((S, D), F32)),
        scratch=[pltpu.VMEM((D, FBP), F32), pltpu.VMEM((D, FBP), F32), pltpu.VMEM((FBP, D), F32)],
        cp=_cp(56, ("arbitrary", "arbitrary")), args=(df, h, g, u, wg, wu, wd), ride=ride)


def win_fwd(h, win, name):
    def body(h_ref, w_ref, z_ref):
        hv = h_ref[...]
        for j in range(NDEV):
            z_ref[:, 256 * j:256 * (j + 1)] = _mm(hv, w_ref[j])

    return _pallas_call(
        body, name=name, grid=(S // TMM,),
        in_specs=[_row_spec(D, TMM), _full_spec((NDEV, D, 256))],
        out_specs=_row_spec(D_IN, TMM), out_shape=_sds((S, D_IN), F32),
        compiler_params=_cp(40, ("arbitrary",)))(h, win)


def win_bwd(dparts, h, win, name):
    ni = S // TMM

    def body(dq_ref, dk_ref, dv_ref, dus_ref, dup_ref, h_ref, w_ref, dh_ref, dw_ref, acc_ref):
        i = pl.program_id(0)
        hv = h_ref[...]
        cols = [dq_ref[:, 0:256], dq_ref[:, 256:512], dk_ref[:, 0:256], dk_ref[:, 256:512],
                dv_ref[:, 0:256], dv_ref[:, 256:512], dus_ref[...], dup_ref[...]]
        dh = jnp.zeros((TMM, D), F32)
        for j in range(NDEV):
            dz = cols[j].astype(MXU)
            dh = dh + _mm_nt(dz, w_ref[j])
            p = _mm_tn(hv, dz)

            @pl.when(i == 0)
            def _():
                acc_ref[j] = p

            @pl.when(i > 0)
            def _():
                acc_ref[j] += p

        dh_ref[...] = dh

        @pl.when(i == ni - 1)
        def _():
            dw_ref[...] = acc_ref[...].astype(BF16)

    return _pallas_call(
        body, name=name, grid=(ni,),
        in_specs=[_row_spec(512, TMM), _row_spec(512, TMM), _row_spec(512, TMM), _row_spec(256, TMM), _row_spec(256, TMM),
                  _row_spec(D, TMM), _full_spec((NDEV, D, 256))],
        out_specs=(_row_spec(D, TMM), _full_spec((NDEV, D, 256))),
        out_shape=(_sds((S, D), F32), _sds((NDEV, D, 256), BF16)),
        scratch_shapes=[pltpu.VMEM((NDEV, D, 256), F32)],
        compiler_params=_cp(48, ("arbitrary",)))(*dparts, h, win)


def wout_fwd(ya, ys, yp, wout, name):
    def body(ya_ref, ys_ref, yp_ref, w_ref, o_ref):
        w = w_ref[...].reshape(D, D)
        o_ref[...] = _mm(ya_ref[...], w[0:512]) + _mm(ys_ref[...], w[512:768]) + _mm(yp_ref[...], w[768:1024])

    return _pallas_call(
        body, name=name, grid=(S // TMM,),
        in_specs=[_row_spec(512, TMM), _row_spec(256, TMM), _row_spec(256, TMM), _full_spec((NDEV, 128, D))],
        out_specs=_row_spec(D, TMM), out_shape=_sds((S, D), F32),
        compiler_params=_cp(40, ("arbitrary",)))(ya, ys, yp, wout)


def wout_bwd(do, ya, ys, yp, wout, name):
    ni = S // TMM

    def body(do_ref, ya_ref, ys_ref, yp_ref, w_ref, dya_ref, dys_ref, dyp_ref, dw_ref, acc_ref):
        i = pl.program_id(0)
        w = w_ref[...].reshape(D, D)
        dov = do_ref[...]
        dya_ref[...] = _mm_nt(dov, w[0:512])
        dys_ref[...] = _mm_nt(dov, w[512:768])
        dyp_ref[...] = _mm_nt(dov, w[768:1024])
        parts = [(0, 512, _mm_tn(ya_ref[...], dov)), (512, 768, _mm_tn(ys_ref[...], dov)),
                 (768, 1024, _mm_tn(yp_ref[...], dov))]
        for lo, hi, p in parts:
            @pl.when(i == 0)
            def _():
                acc_ref[lo:hi, :] = p

            @pl.when(i > 0)
            def _():
                acc_ref[lo:hi, :] += p

        @pl.when(i == ni - 1)
        def _():
            dw_ref[...] = acc_ref[...].astype(BF16).reshape(NDEV, 128, D)

    return _pallas_call(
        body, name=name, grid=(ni,),
        in_specs=[_row_spec(D, TMM), _row_spec(512, TMM), _row_spec(256, TMM), _row_spec(256, TMM),
                  _full_spec((NDEV, 128, D))],
        out_specs=(_row_spec(512, TMM), _row_spec(256, TMM), _row_spec(256, TMM), _full_spec((NDEV, 128, D))),
        out_shape=(_sds((S, 512), F32), _sds((S, 256), F32), _sds((S, 256), F32), _sds((NDEV, 128, D), BF16)),
        scratch_shapes=[pltpu.VMEM((D, D), F32)],
        compiler_params=_cp(40, ("arbitrary",)))(do, ya, ys, yp, wout)


def _t5_bucket(dist):
    max_exact = N_BUCKETS // 2
    d = np.maximum(dist, 1).astype(np.float32)
    large = max_exact + (np.log(d / max_exact) / math.log(MAX_DISTANCE / max_exact)
                         * (N_BUCKETS - max_exact)).astype(np.int32)
    large = np.minimum(large, N_BUCKETS - 1)
    return np.where(dist < max_exact, dist, large).astype(np.int32)


def _att_static():
    i = np.arange(QB)[:, None]
    j = np.arange(2 * QB)[None, :]
    r = i + QB - j
    buckets, bands = [], []
    for window, dil in PATTERNS:
        bands.append((r >= 0) & (r <= window // dil))
        buckets.append(_t5_bucket(np.clip(r, 0, None) * dil))
    return np.stack(buckets), np.stack(bands), np.broadcast_to(j >= QB, (QB, 2 * QB))


def att_bias(rel_bias):
    m = np.arange(2 * QB)
    rows = []
    for window, dil in PATTERNS:
        r = QB - m
        ok = (r >= 0) & (r <= window // dil)
        b = rel_bias[_t5_bucket(np.clip(r, 0, None) * dil)]
        rows.append(jnp.where(ok[:, None], b, NEG).T)
    return jnp.broadcast_to(jnp.stack(rows)[:, :, None, :], (3, N_HEADS, 8, 2 * QB))


def _bias_tiles(t_ref, tiles):
    col = lax.broadcasted_iota(jnp.int32, (QB, 2 * QB), 1)
    for p in range(3):
        for hh in range(2):
            t = pltpu.roll(jnp.broadcast_to(t_ref[p, hh, 0:1, :], (QB, 2 * QB)), 0, 1, stride=1, stride_axis=0)
            tiles[p, hh, 0] = t
            tiles[p, hh, 1] = jnp.where(col >= QB, t, NEG)


def _permute_in(dst_ref, src_ref, d, scale=None, pad=QB):
    L = S // d
    for r in range(d):
        v = src_ref[pl.ds(r, L, stride=d), :] if d > 1 else src_ref[...]
        if scale is not None:
            v = v * scale
        dst_ref[pad + r * L:pad + (r + 1) * L, :] = v.astype(dst_ref.dtype)


def att_fwd(z, bias, name, ride=None):
    def body(q_ref, k_ref, v_ref, t_ref, y_ref, l_ref, qs, ks, vs, o_perm, l_perm, o_nat, l_nat, b_ref):
        _bias_tiles(t_ref, b_ref)
        zero_pad = jnp.zeros((QB, 128), MXU)
        ks[0:QB, :] = zero_pad
        vs[0:QB, :] = zero_pad
        lane = lax.broadcasted_iota(jnp.int32, (QB, 128), 1)
        for p, (_, d) in enumerate(PATTERNS):
            L = S // d
            nb = L // QB
            _permute_in(qs, q_ref, d, scale=0.125, pad=0)
            _permute_in(ks, k_ref, d)
            _permute_in(vs, v_ref, d)

            def blk(b, carry):
                r0 = pl.multiple_of(b * QB, QB)
                q = qs[pl.ds(r0, QB), :]
                kb = ks[pl.ds(r0, 2 * QB), :]
                vb = vs[pl.ds(r0, 2 * QB), :]
                first = ((b % nb) == 0).astype(jnp.int32)
                res = []
                for hh in range(2):
                    sel = (lane < 64) if hh == 0 else (lane >= 64)
                    qm = jnp.where(sel, q, jnp.zeros_like(q))
                    s = _mm_nt(qm, kb) + b_ref[p, hh, first]
                    m = jnp.max(s, axis=1, keepdims=True)
                    pe = jnp.exp(s - m)
                    den = jnp.sum(pe, axis=1, keepdims=True)
                    res.append((_mm(pe, vb) / den, m + jnp.log(den)))
                o_perm[pl.ds(r0, QB), :] = jnp.where(lane < 64, res[0][0], res[1][0])
                l_perm[pl.ds(r0, QB), :] = jnp.where(lane < 64, res[0][1], res[1][1])
                return carry

            lax.fori_loop(0, S // QB, blk, 0, unroll=8)
            for r in range(d):
                if d > 1:
                    o_nat[p, pl.ds(r, L, stride=d), :] = o_perm[r * L:(r + 1) * L, :]
                    l_nat[p, pl.ds(r, L, stride=d), :] = l_perm[r * L:(r + 1) * L, :]
                else:
                    o_nat[p] = o_perm[...]
                    l_nat[p] = l_perm[...]
        l0, l1, l2 = l_nat[0], l_nat[1], l_nat[2]
        m = jnp.maximum(jnp.maximum(l0, l1), l2)
        e0, e1, e2 = jnp.exp(l0 - m), jnp.exp(l1 - m), jnp.exp(l2 - m)
        den = e0 + e1 + e2
        y_ref[...] = (e0 * o_nat[0] + e1 * o_nat[1] + e2 * o_nat[2]) / den
        l_ref[...] = m + jnp.log(den)

    col = lambda c0: pl.BlockSpec((S, 128), lambda hp: (0, c0 + hp))
    return _call(
        body, name=name, grid=(N_HEADS // 2,),
        in_specs=[col(0), col(4), col(8), pl.BlockSpec((3, 2, 8, 2 * QB), lambda hp: (0, hp, 0, 0))],
        out_specs=(col(0), col(0)),
        out_shape=(_sds((S, D_ATT), F32), _sds((S, D_ATT), F32)),
        scratch=[pltpu.VMEM((S, 128), MXU), pltpu.VMEM((S + QB, 128), MXU), pltpu.VMEM((S + QB, 128), MXU),
                 pltpu.VMEM((S, 128), F32), pltpu.VMEM((S, 128), F32),
                 pltpu.VMEM((3, S, 128), F32), pltpu.VMEM((3, S, 128), F32),
                 pltpu.VMEM((3, 2, 2, QB, 2 * QB), F32)],
        cp=_cp(40, ("arbitrary",)), args=(z, z, z, bias), ride=ride)


def att_bwd(z, bias, y, lse, dy, name, ride=None):
    def body(q_ref, k_ref, v_ref, t_ref, y_ref, l_ref, dy_ref, dq_ref, dk_ref, dv_ref, db_ref,
             qs, ks, vs, dys, ls, dds, dn_nat, dq_perm, dk_perm, dv_perm, b_ref):
        _bias_tiles(t_ref, b_ref)
        zero_pad = jnp.zeros((QB, 128), MXU)
        ks[0:QB, :] = zero_pad
        vs[0:QB, :] = zero_pad
        lane = lax.broadcasted_iota(jnp.int32, (QB, 128), 1)
        lane_s = lax.broadcasted_iota(jnp.int32, (S, 128), 1)
        t = dy_ref[...] * y_ref[...]
        sa = jnp.sum(jnp.where(lane_s < 64, t, 0.0), axis=1, keepdims=True)
        sb = jnp.sum(jnp.where(lane_s >= 64, t, 0.0), axis=1, keepdims=True)
        dn_nat[...] = jnp.where(lane_s < 64, sa, sb)
        dq_ref[...] = jnp.zeros((S, 128), F32)
        dk_ref[...] = jnp.zeros((S, 128), F32)
        dv_ref[...] = jnp.zeros((S, 128), F32)
        db_ref[...] = jnp.zeros((3, 2, QB, 2 * QB), F32)
        for p, (_, d) in enumerate(PATTERNS):
            L = S // d
            nb = L // QB
            _permute_in(qs, q_ref, d, scale=0.125, pad=0)
            _permute_in(ks, k_ref, d)
            _permute_in(vs, v_ref, d)
            _permute_in(dys, dy_ref, d, pad=0)
            _permute_in(ls, l_ref, d, pad=0)
            _permute_in(dds, dn_nat, d, pad=0)
            dk_perm[...] = jnp.zeros((S + QB, 128), F32)
            dv_perm[...] = jnp.zeros((S + QB, 128), F32)

            def blk(b, carry):
                r0 = pl.multiple_of(b * QB, QB)
                q = qs[pl.ds(r0, QB), :]
                kb = ks[pl.ds(r0, 2 * QB), :]
                vb = vs[pl.ds(r0, 2 * QB), :]
                dyb = dys[pl.ds(r0, QB), :]
                lb = ls[pl.ds(r0, QB), :]
                db = dds[pl.ds(r0, QB), :]
                first = ((b % nb) == 0).astype(jnp.int32)
                dqs = []
                dkb = jnp.zeros((2 * QB, 128), F32)
                dvb = jnp.zeros((2 * QB, 128), F32)
                for hh in range(2):
                    sel = (lane < 64) if hh == 0 else (lane >= 64)
                    c0 = 64 * hh
                    qm = jnp.where(sel, q, jnp.zeros_like(q))
                    dym = jnp.where(sel, dyb, jnp.zeros_like(dyb))
                    s = _mm_nt(qm, kb) + b_ref[p, hh, first]
                    pr = jnp.exp(s - lb[:, c0:c0 + 1])
                    dp = _mm_nt(dym, vb)
                    ds = pr * (dp - db[:, c0:c0 + 1])
                    db_ref[p, hh] += ds
                    dqs.append(_mm(ds, kb))
                    dkb = dkb + _mm_tn(ds, qm)
                    dvb = dvb + _mm_tn(pr, dym)
                dq_perm[pl.ds(r0, QB), :] = jnp.where(lane < 64, dqs[0], dqs[1])
                dk_perm[pl.ds(r0, 2 * QB), :] += dkb
                dv_perm[pl.ds(r0, 2 * QB), :] += dvb
                return carry

            lax.fori_loop(0, S // QB, blk, 0, unroll=4)
            for r in range(d):
                idx = pl.ds(r, L, stride=d) if d > 1 else pl.ds(0, S)
                dq_ref[idx, :] += dq_perm[r * L:(r + 1) * L, :] * 0.125
                dk_ref[idx, :] += dk_perm[QB + r * L:QB + (r + 1) * L, :]
                dv_ref[idx, :] += dv_perm[QB + r * L:QB + (r + 1) * L, :]

    col = lambda c0: pl.BlockSpec((S, 128), lambda hp: (0, c0 + hp))
    bspec = pl.BlockSpec((3, 2, 8, 2 * QB), lambda hp: (0, hp, 0, 0))
    return _call(
        body, name=name, grid=(N_HEADS // 2,),
        in_specs=[col(0), col(4), col(8), bspec, col(0), col(0), col(0)],
        out_specs=(col(0), col(0), col(0), pl.BlockSpec((3, 2, QB, 2 * QB), lambda hp: (0, hp, 0, 0))),
        out_shape=(_sds((S, D_ATT), F32), _sds((S, D_ATT), F32), _sds((S, D_ATT), F32),
                   _sds((3, N_HEADS, QB, 2 * QB), F32)),
        scratch=[pltpu.VMEM((S, 128), MXU), pltpu.VMEM((S + QB, 128), MXU), pltpu.VMEM((S + QB, 128), MXU),
                 pltpu.VMEM((S, 128), MXU), pltpu.VMEM((S, 128), F32), pltpu.VMEM((S, 128), F32),
                 pltpu.VMEM((S, 128), F32), pltpu.VMEM((S, 128), F32),
                 pltpu.VMEM((S + QB, 128), F32), pltpu.VMEM((S + QB, 128), F32),
                 pltpu.VMEM((3, 2, 2, QB, 2 * QB), F32)],
        cp=_cp(48, ("arbitrary",)), args=(z, z, z, bias, y, lse, dy), ride=ride)


def relbias_grad(dbiases):
    bucket, band, _ = _att_static()
    onehot = (bucket[:, None] == np.arange(N_BUCKETS)[None, :, None, None]) & band[:, None]
    onehot = jnp.asarray(onehot.reshape(3, N_BUCKETS, QB * 2 * QB), BF16)

    def body(db0_ref, db1_ref, oh_ref, o_ref):
        acc = jnp.zeros((N_HEADS, N_BUCKETS), F32)
        for p in range(3):
            acc = acc + lax.dot_general(db0_ref[p] + db1_ref[p], oh_ref[p].astype(F32), (((1,), (1,)), ((), ())),
                                        preferred_element_type=F32, precision=lax.Precision.HIGHEST)
        o_ref[...] = acc

    vm = pl.BlockSpec(memory_space=pltpu.VMEM)
    out = _pallas_call(body, name="relbias_grad", in_specs=[vm, vm, vm], out_specs=vm,
                         out_shape=_sds((N_HEADS, N_BUCKETS), F32), compiler_params=_cp(40))(
        *[d.reshape(3, N_HEADS, QB * 2 * QB) for d in dbiases], onehot)
    return out.T


def _panel(t_ref, ri, j):
    return t_ref[ri, pl.ds(j, S, stride=8), :]


def _gelu(x):
    c = math.sqrt(2.0 / math.pi)
    th = jnp.tanh(c * (x + 0.044715 * x * x * x))
    return 0.5 * x * (1.0 + th), th


def ssm_fwd(z, a, bre, bim, cre, cim, dsk, gluw, glub, name):
    def body(u_ref, a_ref, bre_ref, bim_ref, cre_ref, cim_ref, d_ref, gw_ref, gb_ref, y_ref, yp_ref, st_hbm, st_ref):
        u = u_ref[...]
        for j in range(8):
            st_ref[0, pl.ds(j, S, stride=8), :] = _mm(u, bre_ref[:, 128 * j:128 * (j + 1)])
            st_ref[1, pl.ds(j, S, stride=8), :] = _mm(u, bim_ref[:, 128 * j:128 * (j + 1)])
        ar, ai = a_ref[0], a_ref[1]

        def step(t, c):
            re, im = c
            i = pl.multiple_of(t * 8, 8)
            nre = ar * re - ai * im + st_ref[0, pl.ds(i, 8), :]
            nim = ar * im + ai * re + st_ref[1, pl.ds(i, 8), :]
            st_ref[0, pl.ds(i, 8), :] = nre
            st_ref[1, pl.ds(i, 8), :] = nim
            return nre, nim

        zero = jnp.zeros((8, 128), F32)
        lax.fori_loop(0, S, step, (zero, zero), unroll=8)
        y = d_ref[...] * u
        for j in range(8):
            y = y + _mm(_panel(st_ref, 0, j), cre_ref[128 * j:128 * (j + 1), :])
            y = y - _mm(_panel(st_ref, 1, j), cim_ref[128 * j:128 * (j + 1), :])
        pltpu.sync_copy(st_ref, st_hbm)
        yp_ref[...] = y
        gl, _ = _gelu(y)
        tt = _mm(gl, gw_ref[...].reshape(D_SSM, D_SSM)) + gb_ref[...]
        y_ref[...] = y * jax.nn.sigmoid(tt)

    vm = lambda shape: pl.BlockSpec(shape, lambda i: (0,) * len(shape))
    return _pallas_call(
        body, name=name, grid=(1,),
        in_specs=[pl.BlockSpec((S, 256), lambda i: (0, 6)), vm((2, 8, 128)), vm((256, 1024)), vm((256, 1024)),
                  vm((1024, 256)), vm((1024, 256)), vm((1, 256)),
                  vm((NDEV, 32, 256)), vm((1, 256))],
        out_specs=(vm((S, 256)), vm((S, 256)), pl.BlockSpec(memory_space=pl.ANY)),
        out_shape=(_sds((S, 256), F32), _sds((S, 256), F32), _sds((2, S * 8, 128), F32)),
        scratch_shapes=[pltpu.VMEM((2, S * 8, 128), F32)],
        compiler_params=_cp(40, ("arbitrary",)))(z, a, bre, bim, cre, cim, dsk, gluw, glub)


def ssm_bwd(dy, z, ypre, st, a, bre, bim, cre, cim, dsk, gluw, glub, name, ride=None):
    def body(dy_ref, u_ref, yp_ref, st_hbm, a_ref, bre_ref, bim_ref, cre_ref, cim_ref, d_ref, gw_ref, gb_ref,
             du_ref, dbre_ref, dbim_ref, dcre_ref, dcim_ref, da_ref, dd_ref, dgw_ref, dgb_ref, g_ref, st_ref):
        pltpu.sync_copy(st_hbm, st_ref)
        u = u_ref[...]
        y = yp_ref[...]
        dout = dy_ref[...]
        gw = gw_ref[...].reshape(D_SSM, D_SSM)
        gl, th = _gelu(y)
        sig = jax.nn.sigmoid(_mm(gl, gw) + gb_ref[...])
        dt = dout * y * sig * (1.0 - sig)
        dgw_ref[...] = _mm_tn(gl, dt)
        dgb_ref[...] = jnp.sum(dt, axis=0, keepdims=True)
        c = math.sqrt(2.0 / math.pi)
        dgelu = 0.5 * (1.0 + th) + 0.5 * y * (1.0 - th * th) * c * (1.0 + 3.0 * 0.044715 * y * y)
        dyv = dout * sig + _mm_nt(dt, gw) * dgelu
        dd_ref[...] = jnp.sum(dyv * u, axis=0, keepdims=True)
        for j in range(8):
            rows = slice(128 * j, 128 * (j + 1))
            g_ref[0, pl.ds(j, S, stride=8), :] = _mm_nt(dyv, cre_ref[rows, :])
            g_ref[1, pl.ds(j, S, stride=8), :] = -_mm_nt(dyv, cim_ref[rows, :])
            dcre_ref[rows, :] = _mm_tn(_panel(st_ref, 0, j), dyv)
            dcim_ref[rows, :] = -_mm_tn(_panel(st_ref, 1, j), dyv)
        ar, ai = a_ref[0], a_ref[1]

        def step(k, c4):
            gre, gim, dar, dai = c4
            i = pl.multiple_of((S - 1 - k) * 8, 8)
            nre = g_ref[0, pl.ds(i, 8), :] + ar * gre + ai * gim
            nim = g_ref[1, pl.ds(i, 8), :] + ar * gim - ai * gre
            g_ref[0, pl.ds(i, 8), :] = nre
            g_ref[1, pl.ds(i, 8), :] = nim
            sre = st_ref[0, pl.ds(i - 8, 8), :]
            sim = st_ref[1, pl.ds(i - 8, 8), :]
            return nre, nim, dar + nre * sre + nim * sim, dai + nim * sre - nre * sim

        zero = jnp.zeros((8, 128), F32)
        gre, gim, dar, dai = lax.fori_loop(0, S - 1, step, (zero, zero, zero, zero), unroll=8)
        g_ref[0, 0:8, :] = g_ref[0, 0:8, :] + ar * gre + ai * gim
        g_ref[1, 0:8, :] = g_ref[1, 0:8, :] + ar * gim - ai * gre
        da_ref[0] = dar
        da_ref[1] = dai
        du = dyv * d_ref[...]
        for j in range(8):
            cols = slice(128 * j, 128 * (j + 1))
            gr, gi = _panel(g_ref, 0, j), _panel(g_ref, 1, j)
            dbre_ref[:, cols] = _mm_tn(u, gr)
            dbim_ref[:, cols] = _mm_tn(u, gi)
            du = du + _mm_nt(gr, bre_ref[:, cols]) + _mm_nt(gi, bim_ref[:, cols])
        du_ref[...] = du

    vm = lambda shape: pl.BlockSpec(shape, lambda i: (0,) * len(shape))
    return _call(
        body, name=name, grid=(1,),
        in_specs=[vm((S, 256)), pl.BlockSpec((S, 256), lambda i: (0, 6)), vm((S, 256)), pl.BlockSpec(memory_space=pl.ANY),
                  vm((2, 8, 128)), vm((256, 1024)), vm((256, 1024)), vm((1024, 256)), vm((1024, 256)), vm((1, 256)),
                  vm((NDEV, 32, 256)), vm((1, 256))],
        out_specs=(vm((S, 256)), vm((256, 1024)), vm((256, 1024)), vm((1024, 256)), vm((1024, 256)),
                   vm((2, 8, 128)), vm((1, 256)), vm((256, 256)), vm((1, 256))),
        out_shape=(_sds((S, 256), F32), _sds((256, 1024), F32), _sds((256, 1024), F32), _sds((1024, 256), F32),
                   _sds((1024, 256), F32), _sds((2, 8, 128), F32), _sds((1, 256), F32), _sds((256, 256), F32),
                   _sds((1, 256), F32)),
        scratch=[pltpu.VMEM((2, S * 8, 128), F32), pltpu.VMEM((2, S * 8, 128), F32)],
        cp=_cp(56, ("arbitrary",)), args=(dy, z, ypre, st, a, bre, bim, cre, cim, dsk, gluw, glub), ride=ride)


def _ssm_discretise(a_re, a_im, log_dt, b_re, b_im):
    dt = jnp.exp(log_dt)[:, None]
    er = jnp.exp(a_re * dt)
    abr, abi = er * jnp.cos(a_im * dt), er * jnp.sin(a_im * dt)
    den = a_re * a_re + a_im * a_im
    fr = ((abr - 1.0) * a_re + abi * a_im) / den
    fi = (abi * a_re - (abr - 1.0) * a_im) / den
    bbr = fr[:, :, None] * b_re - fi[:, :, None] * b_im
    bbi = fr[:, :, None] * b_im + fi[:, :, None] * b_re
    return abr, abi, bbr, bbi


def _blockdiag(t):
    g, r, c = t.shape
    eye = jnp.eye(g, dtype=t.dtype)
    return (t[:, :, None, :] * eye[:, None, :, None]).reshape(g * r, g * c)


def _blockdiag_take(m, r, c):
    g = m.shape[0] // r
    idx = jnp.arange(g)
    return m.reshape(g, r, g, c)[idx, :, idx, :]


PAD = 16


def _pool_lane_select(vals):
    lane = lax.broadcasted_iota(jnp.int32, vals[0].shape, 1)
    out = vals[3]
    for g in (2, 1, 0):
        out = jnp.where(lane < 64 * (g + 1), vals[g], out)
    return out


def _pool_counts():
    row = lax.broadcasted_iota(jnp.int32, (S, D_POOL), 0).astype(F32) + 1.0
    return _pool_lane_select([jnp.minimum(row, float(w)) for w in POOL_WINDOWS])


def _pooled(u, sa, sb):
    sums = []
    cur = u
    bufs = (sa, sb)
    for k, sh in enumerate((1, 2, 4, 8)):
        buf = bufs[k % 2]
        buf[PAD:PAD + S, :] = cur
        cur = cur + buf[PAD - sh:PAD - sh + S, :]
        sums.append(cur)
    return _pool_lane_select(sums) / _pool_counts() - u


def pool_fwd(z, pw, psc, name):
    def body(u_ref, w_ref, s_ref, y_ref, sa, sb):
        for buf in (sa, sb):
            buf[0:PAD, :] = jnp.zeros((PAD, D_POOL), F32)
        pooled = _pooled(u_ref[...], sa, sb)
        y_ref[...] = _mm(pooled, w_ref[...]) * s_ref[...]

    vm = lambda shape: pl.BlockSpec(shape, lambda i: (0,) * len(shape))
    return _pallas_call(
        body, name=name, grid=(1,),
        in_specs=[pl.BlockSpec((S, 256), lambda i: (0, 7)), vm((256, 256)), vm((1, 256))],
        out_specs=vm((S, 256)), out_shape=_sds((S, 256), F32),
        scratch_shapes=[pltpu.VMEM((S + 2 * PAD, D_POOL), F32)] * 2,
        compiler_params=_cp(40, ("arbitrary",)))(z, pw, psc)


def pool_bwd(dy, z, pw, psc, name):
    def body(dy_ref, u_ref, w_ref, s_ref, du_ref, dw_ref, ds_ref, sa, sb):
        for buf in (sa, sb):
            buf[0:PAD, :] = jnp.zeros((PAD, D_POOL), F32)
            buf[PAD + S:PAD + S + PAD, :] = jnp.zeros((PAD, D_POOL), F32)
        pooled = _pooled(u_ref[...], sa, sb)
        dyv = dy_ref[...]
        w = w_ref[...]
        ds_ref[...] = jnp.sum(dyv * _mm(pooled, w), axis=0, keepdims=True)
        dyl = dyv * s_ref[...]
        dw_ref[...] = _mm_tn(pooled, dyl)
        dpool = _mm_nt(dyl, w)
        cur = dpool / _pool_counts()
        sums = []
        bufs = (sa, sb)
        for k, sh in enumerate((1, 2, 4, 8)):
            buf = bufs[k % 2]
            buf[PAD:PAD + S, :] = cur
            cur = cur + buf[PAD + sh:PAD + sh + S, :]
            sums.append(cur)
        du_ref[...] = _pool_lane_select(sums) - dpool

    vm = lambda shape: pl.BlockSpec(shape, lambda i: (0,) * len(shape))
    return _pallas_call(
        body, name=name, grid=(1,),
        in_specs=[vm((S, 256)), pl.BlockSpec((S, 256), lambda i: (0, 7)), vm((256, 256)), vm((1, 256))],
        out_specs=(vm((S, 256)), vm((256, 256)), vm((1, 256))),
        out_shape=(_sds((S, 256), F32), _sds((256, 256), F32), _sds((1, 256), F32)),
        scratch_shapes=[pltpu.VMEM((S + 2 * PAD, D_POOL), F32)] * 2,
        compiler_params=_cp(40, ("arbitrary",)))(dy, z, pw, psc)


def ada_fwd(c_all, ada_w, ada_b_cols):
    def body(c_ref, w_ref, b_ref, o_ref):
        c = c_ref[...]
        cond = c * jax.nn.sigmoid(c)
        o_ref[...] = jnp.dot(cond, w_ref[...], preferred_element_type=F32, precision=lax.Precision.HIGHEST) + b_ref[...]

    return _pallas_call(
        body, name="ada_fwd", grid=(DEPTH,),
        in_specs=[pl.BlockSpec((NDEV, D), lambda l: (0, 0)), pl.BlockSpec((None, D, 1152), lambda l: (l, 0, 0)),
                  pl.BlockSpec((None, 1, 1152), lambda l: (l, 0, 0))],
        out_specs=pl.BlockSpec((None, NDEV, 1152), lambda l: (l, 0, 0)), out_shape=_sds((DEPTH, NDEV, 1152), F32),
        compiler_params=_cp(40, ("arbitrary",)))(c_all, ada_w, ada_b_cols)


def ada_bwd(c_all, dmod_cols):
    def body(c_ref, dm_ref, o_ref):
        c = c_ref[...]
        cond = c * jax.nn.sigmoid(c)
        o_ref[...] = lax.dot_general(cond, dm_ref[...], (((0,), (0,)), ((), ())), preferred_element_type=F32,
                                     precision=lax.Precision.HIGHEST)

    return _pallas_call(
        body, name="ada_bwd", grid=(DEPTH,),
        in_specs=[pl.BlockSpec((NDEV, D), lambda l: (0, 0)), pl.BlockSpec((None, NDEV, 1152), lambda l: (l, 0, 0))],
        out_specs=pl.BlockSpec((None, D, 1152), lambda l: (l, 0, 0)), out_shape=_sds((DEPTH, D, 1152), F32),
        compiler_params=_cp(40, ("arbitrary",)))(c_all, dmod_cols)


def _adamw(w, g, m, v):
    m2 = B1 * m + (1.0 - B1) * g
    v2 = B2 * v + (1.0 - B2) * (g * g)
    m_hat = m2 / (1.0 - B1 ** STEP)
    v_hat = v2 / (1.0 - B2 ** STEP)
    return -LR * (m_hat / (jnp.sqrt(v_hat) + EPS) + WD * w), m2, v2


def _sum8(ref):
    g = ref[0].astype(F32)
    for s in range(1, NDEV):
        g = g + ref[s].astype(F32)
    return g


def adam_rs(recv, w, m, v, tr, name):
    lead, (r, cdim) = w.shape[:-2], w.shape[-2:]
    cp = recv.shape[-1]
    nl = len(lead)

    def body(rc_ref, w_ref, m_ref, v_ref, g_ref, d_ref, m2_ref, v2_ref):
        g = _sum8(rc_ref)[:, :cdim]
        g_ref[...] = g
        d_ref[...], m2_ref[...], v2_ref[...] = _adamw(w_ref[...], g, m_ref[...], v_ref[...])

    rs = pl.BlockSpec((None,) * nl + (tr, cdim), lambda *i: (*i, 0))
    return _pallas_call(
        body, name=name, grid=lead + (r // tr,),
        in_specs=[pl.BlockSpec((NDEV,) + (None,) * nl + (tr, cp), lambda *i: (0, *i, 0)), rs, rs, rs],
        out_specs=(rs, rs, rs, rs), out_shape=tuple(_sds(w.shape, F32) for _ in range(4)),
        compiler_params=_cp(48, ("arbitrary",) * (nl + 1)))(recv, w, m, v)


def adam_rs_rows(recv, w, m, v, name):
    half = FB // 2

    def body(rc_ref, w_ref, m_ref, v_ref, g_ref, d_ref, m2_ref, v2_ref):
        g = _sum8(rc_ref)
        g_ref[...] = g
        d_ref[...], m2_ref[...], v2_ref[...] = _adamw(w_ref[...], g, m_ref[...], v_ref[...])

    rs = pl.BlockSpec((None, None, half, D), lambda l, f, i: (l, f, i, 0))
    return _pallas_call(
        body, name=name, grid=(DEPTH, 2, 2),
        in_specs=[pl.BlockSpec((NDEV, None, None, half, D), lambda l, f, i: (0, l, f, i, 0)), rs, rs, rs],
        out_specs=(rs, rs, rs, rs), out_shape=tuple(_sds((DEPTH, 2, FB, D), F32) for _ in range(4)),
        compiler_params=_cp(48, ("arbitrary",) * 3))(recv, w, m, v)


def adam_plain(g, w, m, v, tr, name):
    lead, (r, cdim) = w.shape[:-2], w.shape[-2:]
    nl = len(lead)

    def body(g_ref, w_ref, m_ref, v_ref, d_ref, m2_ref, v2_ref):
        d_ref[...], m2_ref[...], v2_ref[...] = _adamw(w_ref[...], g_ref[...], m_ref[...], v_ref[...])

    rs = pl.BlockSpec((None,) * nl + (tr, cdim), lambda *i: (*i, 0))
    return _pallas_call(
        body, name=name, grid=lead + (r // tr,), in_specs=[rs, rs, rs, rs], out_specs=(rs, rs, rs),
        out_shape=tuple(_sds(w.shape, F32) for _ in range(3)),
        compiler_params=_cp(48, ("arbitrary",) * (nl + 1)))(g, w, m, v)


def adam_native(gs, ws, ms, vs, name):
    n = len(ws)

    def body(*refs):
        g_refs, w_refs, m_refs, v_refs = (refs[k * n:(k + 1) * n] for k in range(4))
        d_refs, m2_refs, v2_refs = (refs[(4 + k) * n:(5 + k) * n] for k in range(3))
        for a in range(n):
            d_refs[a][...], m2_refs[a][...], v2_refs[a][...] = _adamw(w_refs[a][...], g_refs[a][...], m_refs[a][...], v_refs[a][...])

    vm = pl.BlockSpec(memory_space=pltpu.VMEM)
    outs = _pallas_call(body, name=name, in_specs=[vm] * (4 * n), out_specs=tuple([vm] * (3 * n)),
                        out_shape=tuple(_sds(w.shape, F32) for w in ws) * 3, compiler_params=_cp(40))(*gs, *ws, *ms, *vs)
    return outs[:n], outs[n:2 * n], outs[2 * n:]


def sum_sources(recv, name):
    r = recv.shape[1]

    def body(rc_ref, o_ref):
        o_ref[...] = _sum8(rc_ref)

    vm = pl.BlockSpec(memory_space=pltpu.VMEM)
    return _pallas_call(body, name=name, in_specs=[vm], out_specs=vm, out_shape=_sds((r, 128), F32),
                          compiler_params=_cp(40))(recv)


def _pack(arrs):
    flat = jnp.concatenate([a.reshape(-1) for a in arrs])
    n = flat.shape[0]
    rows = -(-n // 1024) * 8
    return jnp.pad(flat, (0, rows * 128 - n)).reshape(rows, 128)


def _unpack(vec, shapes):
    flat = vec.reshape(-1)
    out, o = [], 0
    for sh in shapes:
        n = int(np.prod(sh))
        out.append(flat[o:o + n].reshape(sh))
        o += n
    return out


WEIGHTS = ['rel_bias', 'ada_w', 'ada_b', 'ln_g', 'ln_b', 'ffn_w_gate', 'ffn_w_up', 'ffn_w_down', 'w_in', 'w_out',
           'ssm_a_re', 'ssm_a_im', 'ssm_log_dt', 'ssm_b_re', 'ssm_b_im', 'ssm_c_re', 'ssm_c_im', 'ssm_d', 'glu_w',
           'glu_b', 'pool_w', 'pool_scale']
SMALL = ['rel_bias', 'ada_b', 'ln_g', 'ln_b', 'ssm_a_re', 'ssm_a_im', 'ssm_log_dt', 'ssm_b_re', 'ssm_b_im',
         'ssm_c_re', 'ssm_c_im', 'ssm_d', 'glu_b', 'pool_w', 'pool_scale']
SMALL_FULL_SHAPES = {'rel_bias': (32, 8), 'ada_b': (2, 9216), 'ln_g': (2, 3, 1024), 'ln_b': (2, 3, 1024),
                     'ssm_a_re': (2, 16, 64), 'ssm_a_im': (2, 16, 64), 'ssm_log_dt': (2, 16),
                     'ssm_b_re': (2, 16, 64, 16), 'ssm_b_im': (2, 16, 64, 16), 'ssm_c_re': (2, 16, 16, 64),
                     'ssm_c_im': (2, 16, 16, 64), 'ssm_d': (2, 256), 'glu_b': (2, 256), 'pool_w': (2, 4, 64, 64),
                     'pool_scale': (2, 256)}


def _step(P):
    me = _me()
    x0 = P['x'][0]
    target = P['loss_target'][0]

    def shards(l, sub):
        bf = lambda a: a.astype(BF16)
        if sub == 1:
            return [bf(P['w_in'][l]), bf(P['w_out'][l]), bf(P['glu_w'][l])]
        f = sub // 2
        padc = lambda a: jnp.pad(bf(a), ((0, 0), (0, FBP - FB)))
        return [padc(P['ffn_w_gate'][l, f]), padc(P['ffn_w_up'][l, f]), jnp.pad(bf(P['ffn_w_down'][l, f]), ((0, FBP - FB), (0, 0)))]

    order = [(l, sub) for l in range(DEPTH) for sub in range(3)]
    nxt = dict(zip(order[:-1], order[1:]))
    W = {}
    c_all, lng_all, lnb_all, *W[order[0]] = _exchange(Gather([P['c'], P['ln_g'], P['ln_b']] + shards(*order[0])), "gather_first")
    c_all = c_all.reshape(NDEV, D)
    ln_g = jnp.transpose(lng_all, (1, 2, 0, 3)).reshape(DEPTH, 3, D)
    ln_b = jnp.transpose(lnb_all, (1, 2, 0, 3)).reshape(DEPTH, 3, D)

    ada_b_cols = lax.dynamic_slice_in_dim(P['ada_b'], me * 1152, 1152, axis=1).reshape(DEPTH, 1, 1152)
    modc = ada_fwd(c_all, P['ada_w'], ada_b_cols)
    (mod_all,) = _exchange(Gather([modc]), "gather_mod")
    mod_me = lax.dynamic_index_in_dim(mod_all, me, axis=2, keepdims=False)
    mod = jnp.transpose(mod_me, (1, 0, 2)).reshape(DEPTH, 9, D)

    bias = att_bias(P['rel_bias'])
    ssm = []
    for l in range(DEPTH):
        prm = (P['ssm_a_re'][l], P['ssm_a_im'][l], P['ssm_log_dt'][l], P['ssm_b_re'][l], P['ssm_b_im'][l])
        (abr, abi, bbr, bbi), disc_vjp = jax.vjp(_ssm_discretise, *prm)
        ssm.append(dict(
            vjp=disc_vjp, a=jnp.stack([abr.reshape(8, 128), abi.reshape(8, 128)]),
            bre=_blockdiag(jnp.transpose(bbr, (0, 2, 1))).astype(MXU), bim=_blockdiag(jnp.transpose(bbi, (0, 2, 1))).astype(MXU),
            cre=_blockdiag(jnp.transpose(P['ssm_c_re'][l], (0, 2, 1))).astype(MXU),
            cim=_blockdiag(jnp.transpose(P['ssm_c_im'][l], (0, 2, 1))).astype(MXU),
            d=P['ssm_d'][l].reshape(1, 256), gb=P['glu_b'][l].reshape(1, 256),
            pw=_blockdiag(P['pool_w'][l]).astype(MXU), psc=P['pool_scale'][l].reshape(1, 256)))

    saved = []
    x = x0
    for l, sub in order:
        tag = f"l{l}s{sub}"
        ride = Gather(shards(*nxt[(l, sub)])) if (l, sub) in nxt else None
        h = ln_mod_fwd(x, mod[l], sub, "ln_mod_fwd_" + tag)
        if sub != 1:
            wg, wu, wd = W[(l, sub)]
            (G, U, fo), got = ffn_fwd(h, wg, wu, wd, "ffn_fwd_" + tag, ride)
            saved.append(dict(x=x, h=h, G=G, U=U, f=fo))
            x = res_ln_fwd(x, fo, mod[l], sub, ln_g[l], ln_b[l], 0.5, "res_ln_fwd_" + tag)
        else:
            sp = ssm[l]
            win, wout, gluw = W[(l, sub)]
            z = win_fwd(h, win, "win_fwd_" + tag)
            (ya, lse), got = att_fwd(z, bias, "att_fwd_" + tag, ride)
            ys, ypre, st = ssm_fwd(z, sp['a'], sp['bre'], sp['bim'], sp['cre'], sp['cim'], sp['d'], gluw, sp['gb'], "ssm_fwd_" + tag)
            yp = pool_fwd(z, sp['pw'], sp['psc'], "pool_fwd_" + tag)
            o = wout_fwd(ya, ys, yp, wout, "wout_fwd_" + tag)
            saved.append(dict(x=x, h=h, z=z, ya=ya, lse=lse, ys=ys, ypre=ypre, st=st, yp=yp, f=o))
            x = res_ln_fwd(x, o, mod[l], sub, ln_g[l], ln_b[l], 1.0, "res_ln_fwd_" + tag)
        if ride is not None:
            W[nxt[(l, sub)]] = got

    loss_tile, dx = loss_fwd_bwd(x, target, "loss")
    loss = lax.psum(loss_tile[0, 0], ("x", "y", "c"))

    recv = dict(g=lax.empty((NDEV, DEPTH, 2, D, FBP), BF16), u=lax.empty((NDEV, DEPTH, 2, D, FBP), BF16),
                d=lax.empty((NDEV, DEPTH, 2, FBP, D), BF16), win=lax.empty((NDEV, DEPTH, D, 256), BF16),
                wout=lax.empty((NDEV, DEPTH, 128, D), BF16), glu=lax.empty((NDEV, DEPTH, 32, 256), BF16))
    pending = []

    def take(k):
        items = pending[:k]
        del pending[:k]
        if not items:
            return None, []
        return Scatter([p for _, _, p in items], [recv[key] for key, _, _ in items], [idx for _, idx, _ in items]), items

    def landed(items, bufs):
        for (key, _, _), b in zip(items, bufs):
            recv[key] = b

    dmod = [[None] * 9 for _ in range(DEPTH)]
    dlng = [[None] * 3 for _ in range(DEPTH)]
    dlnb = [[None] * 3 for _ in range(DEPTH)]
    dbiases = [None] * DEPTH
    small_l = [dict() for _ in range(DEPTH)]
    for l, sub in reversed(order):
        tag = f"l{l}s{sub}"
        sv = saved[3 * l + sub]
        w = 1.0 if sub == 1 else 0.5
        dxa, df, sums = res_ln_bwd(sv['x'], sv['f'], mod[l], sub, ln_g[l], dx, w, "res_ln_bwd_" + tag)
        dlng[l][sub], dlnb[l][sub], dmod[l][3 * sub + 2] = sums[0], sums[1], sums[2]
        if sub != 1:
            f = sub // 2
            wg, wu, wd = W[(l, sub)]
            ride, items = take({(1, 0): 3, (0, 0): 4}.get((l, sub), 2))
            (dwg, dwu, dwd, dh), bufs = ffn_bwd(df, sv['h'], sv['G'], sv['U'], wg, wu, wd, "ffn_bwd_" + tag, ride)
            landed(items, bufs)
            pending += [('g', (l, f), dwg), ('u', (l, f), dwu), ('d', (l, f), dwd)]
        else:
            sp = ssm[l]
            win, wout, gluw = W[(l, sub)]
            dya, dys, dyp, dwout = wout_bwd(df, sv['ya'], sv['ys'], sv['yp'], wout, "wout_bwd_" + tag)
            ride, items = take(2)
            (dq, dk, dv, dbiases[l]), bufs = att_bwd(sv['z'], bias, sv['ya'], sv['lse'], dya, "att_bwd_" + tag, ride)
            landed(items, bufs)
            ride, items = take(1)
            (dus, dbre, dbim, dcre, dcim, da, dd, dgw, dgb), bufs = ssm_bwd(
                dys, sv['z'], sv['ypre'], sv['st'], sp['a'], sp['bre'], sp['bim'], sp['cre'], sp['cim'], sp['d'],
                gluw, sp['gb'], "ssm_bwd_" + tag, ride)
            landed(items, bufs)
            dup, dpw, dpsc = pool_bwd(dyp, sv['z'], sp['pw'], sp['psc'], "pool_bwd_" + tag)
            dh, dwin = win_bwd((dq, dk, dv, dus, dup), sv['h'], win, "win_bwd_" + tag)
            pending += [('win', (l,), dwin), ('wout', (l,), dwout), ('glu', (l,), dgw.astype(BF16).reshape(NDEV, 32, 256))]
            d_are, d_aim, d_ldt, d_bre, d_bim = sp['vjp']((
                da[0].reshape(16, 64), da[1].reshape(16, 64),
                jnp.transpose(_blockdiag_take(dbre, 16, 64), (0, 2, 1)), jnp.transpose(_blockdiag_take(dbim, 16, 64), (0, 2, 1))))
            small_l[l] = dict(
                ssm_a_re=d_are, ssm_a_im=d_aim, ssm_log_dt=d_ldt, ssm_b_re=d_bre, ssm_b_im=d_bim,
                ssm_c_re=jnp.transpose(_blockdiag_take(dcre, 64, 16), (0, 2, 1)),
                ssm_c_im=jnp.transpose(_blockdiag_take(dcim, 64, 16), (0, 2, 1)),
                ssm_d=dd.reshape(256), glu_b=dgb.reshape(256), pool_w=_blockdiag_take(dpw, 64, 64), pool_scale=dpsc.reshape(256))
        dx, sums = ln_mod_bwd(sv['x'], dh, mod[l], sub, dxa, "ln_mod_bwd_" + tag)
        dmod[l][3 * sub], dmod[l][3 * sub + 1] = sums[0], sums[1]
    grad_x = dx[None]
    ride, items = take(len(pending))
    landed(items, _exchange(ride, "scatter_last"))

    small = {k: jnp.stack([small_l[l][k] for l in range(DEPTH)]) for k in small_l[0]}
    small['rel_bias'] = relbias_grad(dbiases)
    small['ada_b'] = jnp.stack([jnp.stack(dmod[l]).reshape(9 * D) for l in range(DEPTH)])
    small['ln_g'] = jnp.stack([jnp.stack(dlng[l]) for l in range(DEPTH)])
    small['ln_b'] = jnp.stack([jnp.stack(dlnb[l]) for l in range(DEPTH)])
    (small_all,) = _exchange(Gather([_pack([small[k] for k in SMALL])]), "gather_small")
    gsum = dict(zip(SMALL, _unpack(sum_sources(small_all, "sum_small"), [SMALL_FULL_SHAPES[k] for k in SMALL])))
    off = 256
    dmod_all = small_all.reshape(NDEV, -1)[:, off:off + DEPTH * 9 * D].reshape(NDEV, DEPTH, 9 * D)
    dmod_cols = jnp.transpose(lax.dynamic_slice_in_dim(dmod_all, me * 1152, 1152, axis=2), (1, 0, 2))
    g_ada_w = ada_bwd(c_all, dmod_cols)

    out = {}

    def put(name, g, d, m2, v2, shape):
        out['grad_' + name], out['delta_' + name] = g.reshape(shape), d.reshape(shape)
        out['new_m_' + name], out['new_v_' + name] = m2.reshape(shape), v2.reshape(shape)

    def wmv(name):
        return [P[pre + name] for pre in ('', 'm_', 'v_')]

    for name, key, tr in (('ffn_w_gate', 'g', 256), ('ffn_w_up', 'u', 256), ('w_in', 'win', 512), ('w_out', 'wout', 128),
                          ('glu_w', 'glu', 32)):
        put(name, *adam_rs(recv[key], *wmv(name), tr, "adam_" + name), P[name].shape)
    put('ffn_w_down', *adam_rs_rows(recv['d'], *wmv('ffn_w_down'), "adam_ffn_w_down"), P['ffn_w_down'].shape)
    put('ada_w', g_ada_w, *adam_plain(g_ada_w, *wmv('ada_w'), 256, "adam_ada_w"), P['ada_w'].shape)

    for k in ('ln_g', 'ln_b'):
        gsum[k] = lax.dynamic_slice_in_dim(gsum[k], me * 128, 128, axis=2)
    ds_, m2s, v2s = adam_native([gsum[k] for k in SMALL], [P[k] for k in SMALL], [P['m_' + k] for k in SMALL],
                                [P['v_' + k] for k in SMALL], "adam_small")
    for k, d, m2, v2 in zip(SMALL, ds_, m2s, v2s):
        put(k, gsum[k], d, m2, v2, P[k].shape)

    res = [loss, grad_x]
    for pre in ('grad_', 'delta_', 'new_m_', 'new_v_'):
        res += [out[pre + k] for k in WEIGHTS]
    return tuple(res)


def kernel(x, c, rel_bias, ada_w, ada_b, ln_g, ln_b, ffn_w_gate, ffn_w_up, ffn_w_down, w_in, w_out, ssm_a_re, ssm_a_im, ssm_log_dt, ssm_b_re, ssm_b_im, ssm_c_re, ssm_c_im, ssm_d, glu_w, glu_b, pool_w, pool_scale, loss_target, m_rel_bias, m_ada_w, m_ada_b, m_ln_g, m_ln_b, m_ffn_w_gate, m_ffn_w_up, m_ffn_w_down, m_w_in, m_w_out, m_ssm_a_re, m_ssm_a_im, m_ssm_log_dt, m_ssm_b_re, m_ssm_b_im, m_ssm_c_re, m_ssm_c_im, m_ssm_d, m_glu_w, m_glu_b, m_pool_w, m_pool_scale, v_rel_bias, v_ada_w, v_ada_b, v_ln_g, v_ln_b, v_ffn_w_gate, v_ffn_w_up, v_ffn_w_down, v_w_in, v_w_out, v_ssm_a_re, v_ssm_a_im, v_ssm_log_dt, v_ssm_b_re, v_ssm_b_im, v_ssm_c_re, v_ssm_c_im, v_ssm_d, v_glu_w, v_glu_b, v_pool_w, v_pool_scale):
    return _step(dict(locals()))
```

```python
import functools
import math

import numpy as np
import jax
import jax.numpy as jnp
from jax import lax
from jax.experimental import pallas as pl
from jax.experimental.pallas import tpu as pltpu

F32 = jnp.float32
BF16 = jnp.bfloat16
MXU = jnp.bfloat16

S = 2048
D = 1024
NDEV = 8
DEPTH = 2
D_ATT, D_SSM, D_POOL, D_IN = 512, 256, 256, 2048
N_HEADS = 8
FB = 352
FBP = 384
QB = 128
PATTERNS = ((128, 1), (512, 4), (2048, 16))
POOL_WINDOWS = (2, 4, 8, 16)
N_BUCKETS, MAX_DISTANCE = 32, 2048
ALPHA = (2 * DEPTH) ** 0.25
LN_EPS = 1e-5
NEG = -1e30
LR, B1, B2, EPS, WD, STEP = 0.001, 0.9, 0.999, 1e-08, 0.01, 10

TM = 256
TMM = 512
MIB = 1024 * 1024


def _cp(vmem_mib, sem=None):
    kw = dict(vmem_limit_bytes=vmem_mib * MIB)
    if sem is not None:
        kw["dimension_semantics"] = sem
    return pltpu.CompilerParams(**kw)


def _sds(shape, dtype):
    return jax.ShapeDtypeStruct(shape, dtype)


def _mm(a, b):
    return jnp.dot(a.astype(MXU), b.astype(MXU), preferred_element_type=F32)


def _mm_nt(a, b):
    return lax.dot_general(a.astype(MXU), b.astype(MXU), (((1,), (1,)), ((), ())), preferred_element_type=F32)


def _mm_tn(a, b):
    return lax.dot_general(a.astype(MXU), b.astype(MXU), (((0,), (0,)), ((), ())), preferred_element_type=F32)


def _ln_stats(x):
    mu = jnp.mean(x, axis=-1, keepdims=True)
    xc = x - mu
    var = jnp.mean(xc * xc, axis=-1, keepdims=True)
    rstd = lax.rsqrt(var + LN_EPS)
    return xc * rstd, rstd


def _ln_bwd(dn, n, rstd):
    return rstd * (dn - jnp.mean(dn, axis=-1, keepdims=True) - n * jnp.mean(dn * n, axis=-1, keepdims=True))


def _me():
    return 4 * lax.axis_index("x") + 2 * lax.axis_index("y") + lax.axis_index("c")


ANY = pl.BlockSpec(memory_space=pl.ANY)
PIN_BYTES = 1 << 19


def _pallas_call(*a, **k):
    big = lambda o: math.prod(o.shape) * o.dtype.itemsize >= PIN_BYTES
    pin = lambda o: pltpu.HBM(o.shape, o.dtype) if isinstance(o, jax.ShapeDtypeStruct) and big(o) else o
    osh = k["out_shape"]
    k["out_shape"] = tuple(pin(o) for o in osh) if isinstance(osh, (tuple, list)) else pin(osh)
    fn = pl.pallas_call(*a, **k)

    def run(*args):
        return fn(*[pltpu.with_memory_space_constraint(x, pltpu.HBM) if big(x) else x for x in args])
    return run


class Gather:
    def __init__(self, srcs):
        self.srcs = list(srcs)
        self.n = len(self.srcs)
        self.bufs = []
        self.out_shapes = [_sds((NDEV,) + a.shape, a.dtype) for a in self.srcs]
        self.sems = [pltpu.SemaphoreType.DMA((7 * self.n,)), pltpu.SemaphoreType.DMA((7 * self.n,)),
                     pltpu.SemaphoreType.DMA((self.n,))]

    def _parts(self, srcs, outs, sems):
        send_sems, recv_sems, loc_sems = sems
        x, y, c = lax.axis_index("x"), lax.axis_index("y"), lax.axis_index("c")
        me, sib = (x, y, c), (x, y, 1 - c)
        chips = [(1 - x, y), (x, 1 - y), (1 - x, 1 - y)]
        slot = lambda d: 4 * d[0] + 2 * d[1] + d[2]

        def copy(a, k, block, to, src=None):
            dst = outs[a].at[slot(block)]
            return pltpu.make_async_remote_copy(
                src_ref=dst if src is None else src, dst_ref=dst,
                send_sem=send_sems.at[7 * a + k], recv_sem=recv_sems.at[7 * a + k],
                device_id=to, device_id_type=pl.DeviceIdType.MESH)

        local = [pltpu.make_async_copy(srcs[a], outs[a].at[slot(me)], loc_sems.at[a]) for a in range(self.n)]
        return me, sib, chips, c, copy, local

    def start(self, srcs, bufs, outs, sems):
        me, sib, chips, c, copy, local = self._parts(srcs, outs, sems)
        for a in range(self.n):
            local[a].start()
            copy(a, 0, me, sib, src=srcs[a]).start()
            for j, chip in enumerate(chips):
                copy(a, 1 + j, me, (*chip, c), src=srcs[a]).start()

    def finish(self, srcs, bufs, outs, sems):
        me, sib, chips, c, copy, local = self._parts(srcs, outs, sems)
        for a in range(self.n):
            for j, chip in enumerate(chips):
                copy(a, 1 + j, (*chip, c), me).wait_recv()
                copy(a, 4 + j, (*chip, c), sib).start()
        for a in range(self.n):
            copy(a, 0, sib, me).wait_recv()
            copy(a, 0, me, sib, src=srcs[a]).wait_send()
            for j, chip in enumerate(chips):
                copy(a, 4 + j, (*chip, 1 - c), me).wait_recv()
                copy(a, 1 + j, me, (*chip, c), src=srcs[a]).wait_send()
                copy(a, 4 + j, (*chip, c), sib).wait_send()
            local[a].wait()


class Scatter:
    def __init__(self, srcs, bufs, index):
        self.srcs, self.bufs, self.index = list(srcs), list(bufs), [tuple(i) for i in index]
        self.n = len(self.srcs)
        self.out_shapes = [_sds(b.shape, b.dtype) for b in self.bufs]
        self.sems = [pltpu.SemaphoreType.DMA((7 * self.n,)), pltpu.SemaphoreType.DMA((7 * self.n,)),
                     pltpu.SemaphoreType.DMA((self.n,))]

    def _copies(self, srcs, outs, sems):
        send_sems, recv_sems, loc_sems = sems
        me = _me()

        def remote(a, k, slot):
            t = (me + k) % NDEV
            return pltpu.make_async_remote_copy(
                src_ref=srcs[a].at[t], dst_ref=outs[a].at[(slot,) + self.index[a]],
                send_sem=send_sems.at[7 * a + k - 1], recv_sem=recv_sems.at[7 * a + k - 1],
                device_id=(t // 4, (t // 2) % 2, t % 2), device_id_type=pl.DeviceIdType.MESH)

        local = [pltpu.make_async_copy(srcs[a].at[me], outs[a].at[(me,) + self.index[a]], loc_sems.at[a])
                 for a in range(self.n)]
        return me, remote, local

    def start(self, srcs, bufs, outs, sems):
        me, remote, local = self._copies(srcs, outs, sems)
        for a in range(self.n):
            local[a].start()
        for k in range(1, NDEV):
            for a in range(self.n):
                remote(a, k, me).start()

    def finish(self, srcs, bufs, outs, sems):
        me, remote, local = self._copies(srcs, outs, sems)
        for k in range(1, NDEV):
            for a in range(self.n):
                remote(a, k, (me - k) % NDEV).wait()
        for a in range(self.n):
            local[a].wait()


def _call(body, *, name, grid, in_specs, out_specs, out_shape, args, scratch=(), cp=None, ride=None):
    out_specs, out_shape, scratch = list(out_specs), list(out_shape), list(scratch)
    if ride is None:
        outs = _pallas_call(body, name=name, grid=grid, in_specs=list(in_specs), out_specs=tuple(out_specs),
                              out_shape=tuple(out_shape), scratch_shapes=scratch, compiler_params=cp)(*args)
        return list(outs), []
    nin, nout, nscr, n, nb = len(in_specs), len(out_specs), len(scratch), ride.n, len(ride.bufs)
    steps = list(grid)

    def wrapped(*refs):
        h_in, r_src, r_buf = refs[:nin], refs[nin:nin + n], refs[nin + n:nin + n + nb]
        o0 = nin + n + nb
        h_out, r_out = refs[o0:o0 + nout], refs[o0 + nout:o0 + nout + n]
        s0 = o0 + nout + n
        h_scr, sems = refs[s0:s0 + nscr], refs[s0 + nscr:]
        ids = [pl.program_id(a) for a in range(len(steps))]
        first = functools.reduce(jnp.logical_and, [i == 0 for i in ids])
        last = functools.reduce(jnp.logical_and, [i == s - 1 for i, s in zip(ids, steps)])

        @pl.when(first)
        def _():
            ride.start(r_src, r_buf, r_out, sems)

        body(*h_in, *h_out, *h_scr)

        @pl.when(last)
        def _():
            ride.finish(r_src, r_buf, r_out, sems)

    aliases = {nin + n + k: nout + k for k in range(nb)}
    outs = _pallas_call(
        wrapped, name=name, grid=grid, in_specs=list(in_specs) + [ANY] * (n + nb),
        out_specs=tuple(out_specs + [ANY] * n), out_shape=tuple(out_shape + ride.out_shapes),
        scratch_shapes=scratch + ride.sems, input_output_aliases=aliases, compiler_params=cp,
    )(*args, *ride.srcs, *ride.bufs)
    return list(outs[:nout]), list(outs[nout:])


def _exchange(ride, name):
    def body(dummy_ref, o_ref):
        o_ref[...] = dummy_ref[...]

    one = pl.BlockSpec((8, 128), lambda i: (0, 0))
    _, outs = _call(body, name=name, grid=(1,), in_specs=[one], out_specs=[one], out_shape=[_sds((8, 128), F32)],
                    args=(jnp.zeros((8, 128), F32),), ride=ride)
    return outs


def _row_spec(cols, tm=TM):
    return pl.BlockSpec((tm, cols), lambda i: (i, 0))


def _full_spec(shape):
    nd = len(shape)
    return pl.BlockSpec(shape, lambda i: (0,) * nd)


def ln_mod_fwd(x, mod, sub, name):
    def body(x_ref, mod_ref, h_ref):
        n, _ = _ln_stats(x_ref[...])
        shift = mod_ref[3 * sub:3 * sub + 1, :]
        scale = mod_ref[3 * sub + 1:3 * sub + 2, :]
        h_ref[...] = (n * (1.0 + scale) + shift).astype(MXU)

    return _pallas_call(
        body, name=name, grid=(S // TM,),
        in_specs=[_row_spec(D), _full_spec((9, D))], out_specs=_row_spec(D),
        out_shape=_sds((S, D), MXU), compiler_params=_cp(32, ("arbitrary",)))(x, mod)


def res_ln_fwd(x, f, mod, sub, lng, lnb, w, name):
    def body(x_ref, f_ref, mod_ref, g_ref, b_ref, o_ref):
        gate = mod_ref[3 * sub + 2:3 * sub + 3, :]
        r = ALPHA * x_ref[...] + (w * gate) * f_ref[...]
        n, _ = _ln_stats(r)
        o_ref[...] = n * g_ref[sub:sub + 1, :] + b_ref[sub:sub + 1, :]

    return _pallas_call(
        body, name=name, grid=(S // TM,),
        in_specs=[_row_spec(D), _row_spec(D), _full_spec((9, D)), _full_spec((3, D)), _full_spec((3, D))],
        out_specs=_row_spec(D), out_shape=_sds((S, D), F32),
        compiler_params=_cp(32, ("arbitrary",)))(x, f, mod, lng, lnb)


def res_ln_bwd(x, f, mod, sub, lng, dxo, w, name):
    def body(x_ref, f_ref, mod_ref, g_ref, dxo_ref, dxa_ref, df_ref, sums_ref):
        i = pl.program_id(0)
        gate = mod_ref[3 * sub + 2:3 * sub + 3, :]
        fv = f_ref[...]
        r = ALPHA * x_ref[...] + (w * gate) * fv
        n, rstd = _ln_stats(r)
        dxo = dxo_ref[...]
        dr = _ln_bwd(dxo * g_ref[sub:sub + 1, :], n, rstd)
        dxa_ref[...] = ALPHA * dr
        df_ref[...] = ((w * gate) * dr).astype(MXU)
        part = jnp.concatenate([
            jnp.sum(dxo * n, axis=0, keepdims=True),
            jnp.sum(dxo, axis=0, keepdims=True),
            jnp.sum(dr * fv, axis=0, keepdims=True) * w,
            jnp.zeros((5, D), F32)], axis=0)

        @pl.when(i == 0)
        def _():
            sums_ref[...] = part

        @pl.when(i > 0)
        def _():
            sums_ref[...] += part

    return _pallas_call(
        body, name=name, grid=(S // TM,),
        in_specs=[_row_spec(D), _row_spec(D), _full_spec((9, D)), _full_spec((3, D)), _row_spec(D)],
        out_specs=(_row_spec(D), _row_spec(D), _full_spec((8, D))),
        out_shape=(_sds((S, D), F32), _sds((S, D), MXU), _sds((8, D), F32)),
        compiler_params=_cp(32, ("arbitrary",)))(x, f, mod, lng, dxo)


def ln_mod_bwd(x, dh, mod, sub, dxa, name):
    def body(x_ref, dh_ref, mod_ref, dxa_ref, dx_ref, sums_ref):
        i = pl.program_id(0)
        scale = mod_ref[3 * sub + 1:3 * sub + 2, :]
        n, rstd = _ln_stats(x_ref[...])
        dh = dh_ref[...]
        dx_ref[...] = dxa_ref[...] + _ln_bwd(dh * (1.0 + scale), n, rstd)
        part = jnp.concatenate([
            jnp.sum(dh, axis=0, keepdims=True),
            jnp.sum(dh * n, axis=0, keepdims=True),
            jnp.zeros((6, D), F32)], axis=0)

        @pl.when(i == 0)
        def _():
            sums_ref[...] = part

        @pl.when(i > 0)
        def _():
            sums_ref[...] += part

    return _pallas_call(
        body, name=name, grid=(S // TM,),
        in_specs=[_row_spec(D), _row_spec(D), _full_spec((9, D)), _row_spec(D)],
        out_specs=(_row_spec(D), _full_spec((8, D))),
        out_shape=(_sds((S, D), F32), _sds((8, D), F32)),
        compiler_params=_cp(32, ("arbitrary",)))(x, dh, mod, dxa)


def loss_fwd_bwd(y, target, name):
    def body(y_ref, t_ref, l_ref, dy_ref):
        i = pl.program_id(0)
        e = y_ref[...] - t_ref[...]
        dy_ref[...] = e * (1.0 / D)
        part = jnp.zeros((8, 128), F32) + (0.5 / D) * jnp.sum(e * e)

        @pl.when(i == 0)
        def _():
            l_ref[...] = part

        @pl.when(i > 0)
        def _():
            l_ref[...] += part

    return _pallas_call(
        body, name=name, grid=(S // TM,),
        in_specs=[_row_spec(D), _row_spec(D)], out_specs=(_full_spec((8, 128)), _row_spec(D)),
        out_shape=(_sds((8, 128), F32), _sds((S, D), F32)),
        compiler_params=_cp(32, ("arbitrary",)))(y, target)


HB = 2 * FBP
NHB = NDEV * FBP // HB
TMB = 1024


def _wrows(buffers=2):
    return pl.BlockSpec((HB, D), lambda j, i: (j, 0), pipeline_mode=pl.Buffered(buffers))


def _resident(shape):
    return pl.BlockSpec(shape, lambda j, i: (0, 0), pipeline_mode=pl.Buffered(1))


def ffn_fwd(h, wgt, wut, wd, name, ride=None):
    def body(h_ref, wg_ref, wu_ref, wd_ref, g_ref, u_ref, f_ref):
        j, i = pl.program_id(0), pl.program_id(1)
        hv = h_ref[...]
        g = _mm_nt(hv, wg_ref[...])
        u = _mm_nt(hv, wu_ref[...])
        g_ref[...] = g.astype(MXU)
        u_ref[...] = u.astype(MXU)
        a = g * jax.nn.sigmoid(g) * u
        part = _mm(a, wd_ref[...])
        rows = pl.ds(pl.multiple_of(i * TMB, TMB), TMB)

        @pl.when(j == 0)
        def _():
            f_ref[rows, :] = part

        @pl.when(j > 0)
        def _():
            f_ref[rows, :] += part

    gu = pl.BlockSpec((TMB, HB), lambda j, i: (i, j))
    return _call(
        body, name=name, grid=(NHB, S // TMB),
        in_specs=[pl.BlockSpec((TMB, D), lambda j, i: (i, 0)), _wrows(), _wrows(), _wrows()],
        out_specs=(gu, gu, _resident((S, D))),
        out_shape=(_sds((S, NDEV * FBP), MXU), _sds((S, NDEV * FBP), MXU), _sds((S, D), F32)),
        cp=_cp(52, ("arbitrary", "arbitrary")), args=(h, wgt, wut, wd), ride=ride)


def ffn_bwd(df, h, g, u, wgt, wut, wd, name, ride=None):
    ni = S // TMB

    def body(df_ref, h_ref, g_ref, u_ref, wg_ref, wu_ref, wd_ref, dwg_ref, dwu_ref, dwd_ref, dh_ref,
             ag_ref, au_ref, ad_ref):
        j, i = pl.program_id(0), pl.program_id(1)
        dfv, hv = df_ref[...], h_ref[...]
        gv, uv = g_ref[...].astype(F32), u_ref[...].astype(F32)
        da = _mm_nt(dfv, wd_ref[...])
        sg = jax.nn.sigmoid(gv)
        silu = gv * sg
        du = da * silu
        dg = da * uv * (sg * (1.0 + gv * (1.0 - sg)))
        p_d = _mm_tn(silu * uv, dfv)
        p_g = _mm_tn(dg, hv)
        p_u = _mm_tn(du, hv)

        @pl.when(i == 0)
        def _():
            ad_ref[...] = p_d
            ag_ref[...] = p_g
            au_ref[...] = p_u

        @pl.when(i > 0)
        def _():
            ad_ref[...] += p_d
            ag_ref[...] += p_g
            au_ref[...] += p_u

        @pl.when(i == ni - 1)
        def _():
            dwd_ref[...] = ad_ref[...].astype(BF16)
            dwg_ref[...] = ag_ref[...].astype(BF16)
            dwu_ref[...] = au_ref[...].astype(BF16)

        part = _mm(dg, wg_ref[...]) + _mm(du, wu_ref[...])
        rows = pl.ds(pl.multiple_of(i * TMB, TMB), TMB)

        @pl.when(j == 0)
        def _():
            dh_ref[rows, :] = part

        @pl.when(j > 0)
        def _():
            dh_ref[rows, :] += part

    gu = pl.BlockSpec((TMB, HB), lambda j, i: (i, j))
    rowt = pl.BlockSpec((TMB, D), lambda j, i: (i, 0))
    return _call(
        body, name=name, grid=(NHB, ni),
        in_specs=[rowt, rowt, gu, gu, _wrows(1), _wrows(1), _wrows(1)],
        out_specs=(_wrows(1), _wrows(1), _wrows(1), _resident((S, D))),
        out_shape=(_sds((NDEV * FBP, D), BF16), _sds((NDEV * FBP, D), BF16), _sds((NDEV * FBP, D), BF16), _sds((S, D), F32)),
        scratch=[pltpu.VMEM((HB, D), F32), pltpu.VMEM((HB, D), F32), pltpu.VMEM((HB, D), F32)],
        cp=_cp(60, ("arbitrary", "arbitrary")), args=(df, h, g, u, wgt, wut, wd), ride=ride)


def win_fwd(h, win, name):
    def body(h_ref, w_ref, z_ref):
        hv = h_ref[...]
        for j in range(NDEV):
            z_ref[:, 256 * j:256 * (j + 1)] = _mm(hv, w_ref[j])

    return _pallas_call(
        body, name=name, grid=(S // TMM,),
        in_specs=[_row_spec(D, TMM), _full_spec((NDEV, D, 256))],
        out_specs=_row_spec(D_IN, TMM), out_shape=_sds((S, D_IN), F32),
        compiler_params=_cp(40, ("arbitrary",)))(h, win)


def win_bwd(dparts, h, win, name):
    ni = S // TMM

    def body(dq_ref, dk_ref, dv_ref, dus_ref, dup_ref, h_ref, w_ref, dh_ref, dw_ref, acc_ref):
        i = pl.program_id(0)
        hv = h_ref[...]
        cols = [dq_ref[:, 0:256], dq_ref[:, 256:512], dk_ref[:, 0:256], dk_ref[:, 256:512],
                dv_ref[:, 0:256], dv_ref[:, 256:512], dus_ref[...], dup_ref[...]]
        dh = jnp.zeros((TMM, D), F32)
        for j in range(NDEV):
            dz = cols[j].astype(MXU)
            dh = dh + _mm_nt(dz, w_ref[j])
            p = _mm_tn(hv, dz)

            @pl.when(i == 0)
            def _():
                acc_ref[j] = p

            @pl.when(i > 0)
            def _():
                acc_ref[j] += p

        dh_ref[...] = dh

        @pl.when(i == ni - 1)
        def _():
            dw_ref[...] = acc_ref[...].astype(BF16)

    return _pallas_call(
        body, name=name, grid=(ni,),
        in_specs=[_row_spec(512, TMM), _row_spec(512, TMM), _row_spec(512, TMM), _row_spec(256, TMM), _row_spec(256, TMM),
                  _row_spec(D, TMM), _full_spec((NDEV, D, 256))],
        out_specs=(_row_spec(D, TMM), _full_spec((NDEV, D, 256))),
        out_shape=(_sds((S, D), F32), _sds((NDEV, D, 256), BF16)),
        scratch_shapes=[pltpu.VMEM((NDEV, D, 256), F32)],
        compiler_params=_cp(48, ("arbitrary",)))(*dparts, h, win)


def wout_fwd(ya, ys, yp, wout, name):
    def body(ya_ref, ys_ref, yp_ref, w_ref, o_ref):
        w = w_ref[...].reshape(D, D)
        o_ref[...] = _mm(ya_ref[...], w[0:512]) + _mm(ys_ref[...], w[512:768]) + _mm(yp_ref[...], w[768:1024])

    return _pallas_call(
        body, name=name, grid=(S // TMM,),
        in_specs=[_row_spec(512, TMM), _row_spec(256, TMM), _row_spec(256, TMM), _full_spec((NDEV, 128, D))],
        out_specs=_row_spec(D, TMM), out_shape=_sds((S, D), F32),
        compiler_params=_cp(40, ("arbitrary",)))(ya, ys, yp, wout)


def wout_bwd(do, ya, ys, yp, wout, name):
    ni = S // TMM

    def body(do_ref, ya_ref, ys_ref, yp_ref, w_ref, dya_ref, dys_ref, dyp_ref, dw_ref, acc_ref):
        i = pl.program_id(0)
        w = w_ref[...].reshape(D, D)
        dov = do_ref[...]
        dya_ref[...] = _mm_nt(dov, w[0:512])
        dys_ref[...] = _mm_nt(dov, w[512:768])
        dyp_ref[...] = _mm_nt(dov, w[768:1024])
        parts = [(0, 512, _mm_tn(ya_ref[...], dov)), (512, 768, _mm_tn(ys_ref[...], dov)),
                 (768, 1024, _mm_tn(yp_ref[...], dov))]
        for lo, hi, p in parts:
            @pl.when(i == 0)
            def _():
                acc_ref[lo:hi, :] = p

            @pl.when(i > 0)
            def _():
                acc_ref[lo:hi, :] += p

        @pl.when(i == ni - 1)
        def _():
            dw_ref[...] = acc_ref[...].astype(BF16).reshape(NDEV, 128, D)

    return _pallas_call(
        body, name=name, grid=(ni,),
        in_specs=[_row_spec(D, TMM), _row_spec(512, TMM), _row_spec(256, TMM), _row_spec(256, TMM),
                  _full_spec((NDEV, 128, D))],
        out_specs=(_row_spec(512, TMM), _row_spec(256, TMM), _row_spec(256, TMM), _full_spec((NDEV, 128, D))),
        out_shape=(_sds((S, 512), F32), _sds((S, 256), F32), _sds((S, 256), F32), _sds((NDEV, 128, D), BF16)),
        scratch_shapes=[pltpu.VMEM((D, D), F32)],
        compiler_params=_cp(40, ("arbitrary",)))(do, ya, ys, yp, wout)


def _t5_bucket(dist):
    max_exact = N_BUCKETS // 2
    d = np.maximum(dist, 1).astype(np.float32)
    large = max_exact + (np.log(d / max_exact) / math.log(MAX_DISTANCE / max_exact)
                         * (N_BUCKETS - max_exact)).astype(np.int32)
    large = np.minimum(large, N_BUCKETS - 1)
    return np.where(dist < max_exact, dist, large).astype(np.int32)


def _att_static():
    i = np.arange(QB)[:, None]
    j = np.arange(2 * QB)[None, :]
    r = i + QB - j
    buckets, bands = [], []
    for window, dil in PATTERNS:
        bands.append((r >= 0) & (r <= window // dil))
        buckets.append(_t5_bucket(np.clip(r, 0, None) * dil))
    return np.stack(buckets), np.stack(bands), np.broadcast_to(j >= QB, (QB, 2 * QB))


def att_bias(rel_bias):
    m = np.arange(2 * QB)
    rows = []
    for window, dil in PATTERNS:
        r = QB - m
        ok = (r >= 0) & (r <= window // dil)
        b = rel_bias[_t5_bucket(np.clip(r, 0, None) * dil)]
        rows.append(jnp.where(ok[:, None], b, NEG).T)
    return jnp.broadcast_to(jnp.stack(rows)[:, :, None, :], (3, N_HEADS, 8, 2 * QB))


def _bias_tiles(t_ref, tiles):
    col = lax.broadcasted_iota(jnp.int32, (QB, 2 * QB), 1)
    for p in range(3):
        for hh in range(2):
            t = pltpu.roll(jnp.broadcast_to(t_ref[p, hh, 0:1, :], (QB, 2 * QB)), 0, 1, stride=1, stride_axis=0)
            tiles[p, hh, 0] = t
            tiles[p, hh, 1] = jnp.where(col >= QB, t, NEG)


def _permute_in(dst_ref, src_ref, d, scale=None, pad=QB):
    L = S // d
    for r in range(d):
        v = src_ref[pl.ds(r, L, stride=d), :] if d > 1 else src_ref[...]
        if scale is not None:
            v = v * scale
        dst_ref[pad + r * L:pad + (r + 1) * L, :] = v.astype(dst_ref.dtype)


def att_fwd(z, bias, name, ride=None):
    def body(q_ref, k_ref, v_ref, t_ref, y_ref, l_ref, qs, ks, vs, o_perm, l_perm, o_nat, l_nat, b_ref):
        _bias_tiles(t_ref, b_ref)
        zero_pad = jnp.zeros((QB, 128), MXU)
        ks[0:QB, :] = zero_pad
        vs[0:QB, :] = zero_pad
        lane = lax.broadcasted_iota(jnp.int32, (QB, 128), 1)
        for p, (_, d) in enumerate(PATTERNS):
            L = S // d
            nb = L // QB
            _permute_in(qs, q_ref, d, scale=0.125, pad=0)
            _permute_in(ks, k_ref, d)
            _permute_in(vs, v_ref, d)

            def blk(b, carry):
                r0 = pl.multiple_of(b * QB, QB)
                q = qs[pl.ds(r0, QB), :]
                kb = ks[pl.ds(r0, 2 * QB), :]
                vb = vs[pl.ds(r0, 2 * QB), :]
                first = ((b % nb) == 0).astype(jnp.int32)
                res = []
                for hh in range(2):
                    sel = (lane < 64) if hh == 0 else (lane >= 64)
                    qm = jnp.where(sel, q, jnp.zeros_like(q))
                    s = _mm_nt(qm, kb) + b_ref[p, hh, first]
                    m = jnp.max(s, axis=1, keepdims=True)
                    pe = jnp.exp(s - m)
                    den = jnp.sum(pe, axis=1, keepdims=True)
                    res.append((_mm(pe, vb) / den, m + jnp.log(den)))
                o_perm[pl.ds(r0, QB), :] = jnp.where(lane < 64, res[0][0], res[1][0])
                l_perm[pl.ds(r0, QB), :] = jnp.where(lane < 64, res[0][1], res[1][1])
                return carry

            lax.fori_loop(0, S // QB, blk, 0, unroll=8)
            for r in range(d):
                if d > 1:
                    o_nat[p, pl.ds(r, L, stride=d), :] = o_perm[r * L:(r + 1) * L, :]
                    l_nat[p, pl.ds(r, L, stride=d), :] = l_perm[r * L:(r + 1) * L, :]
                else:
                    o_nat[p] = o_perm[...]
                    l_nat[p] = l_perm[...]
        l0, l1, l2 = l_nat[0], l_nat[1], l_nat[2]
        m = jnp.maximum(jnp.maximum(l0, l1), l2)
        e0, e1, e2 = jnp.exp(l0 - m), jnp.exp(l1 - m), jnp.exp(l2 - m)
        den = e0 + e1 + e2
        y_ref[...] = (e0 * o_nat[0] + e1 * o_nat[1] + e2 * o_nat[2]) / den
        l_ref[...] = m + jnp.log(den)

    col = lambda c0: pl.BlockSpec((S, 128), lambda hp: (0, c0 + hp))
    return _call(
        body, name=name, grid=(N_HEADS // 2,),
        in_specs=[col(0), col(4), col(8), pl.BlockSpec((3, 2, 8, 2 * QB), lambda hp: (0, hp, 0, 0))],
        out_specs=(col(0), col(0)),
        out_shape=(_sds((S, D_ATT), F32), _sds((S, D_ATT), F32)),
        scratch=[pltpu.VMEM((S, 128), MXU), pltpu.VMEM((S + QB, 128), MXU), pltpu.VMEM((S + QB, 128), MXU),
                 pltpu.VMEM((S, 128), F32), pltpu.VMEM((S, 128), F32),
                 pltpu.VMEM((3, S, 128), F32), pltpu.VMEM((3, S, 128), F32),
                 pltpu.VMEM((3, 2, 2, QB, 2 * QB), F32)],
        cp=_cp(40, ("arbitrary",)), args=(z, z, z, bias), ride=ride)


def att_bwd(z, bias, y, lse, dy, name, ride=None):
    def body(q_ref, k_ref, v_ref, t_ref, y_ref, l_ref, dy_ref, dq_ref, dk_ref, dv_ref, db_ref,
             qs, ks, vs, dys, ls, dds, dn_nat, dq_perm, dk_perm, dv_perm, b_ref):
        _bias_tiles(t_ref, b_ref)
        zero_pad = jnp.zeros((QB, 128), MXU)
        ks[0:QB, :] = zero_pad
        vs[0:QB, :] = zero_pad
        lane = lax.broadcasted_iota(jnp.int32, (QB, 128), 1)
        lane_s = lax.broadcasted_iota(jnp.int32, (S, 128), 1)
        t = dy_ref[...] * y_ref[...]
        sa = jnp.sum(jnp.where(lane_s < 64, t, 0.0), axis=1, keepdims=True)
        sb = jnp.sum(jnp.where(lane_s >= 64, t, 0.0), axis=1, keepdims=True)
        dn_nat[...] = jnp.where(lane_s < 64, sa, sb)
        dq_ref[...] = jnp.zeros((S, 128), F32)
        dk_ref[...] = jnp.zeros((S, 128), F32)
        dv_ref[...] = jnp.zeros((S, 128), F32)
        db_ref[...] = jnp.zeros((3, 2, QB, 2 * QB), F32)
        for p, (_, d) in enumerate(PATTERNS):
            L = S // d
            nb = L // QB
            _permute_in(qs, q_ref, d, scale=0.125, pad=0)
            _permute_in(ks, k_ref, d)
            _permute_in(vs, v_ref, d)
            _permute_in(dys, dy_ref, d, pad=0)
            _permute_in(ls, l_ref, d, pad=0)
            _permute_in(dds, dn_nat, d, pad=0)
            dk_perm[...] = jnp.zeros((S + QB, 128), F32)
            dv_perm[...] = jnp.zeros((S + QB, 128), F32)

            def blk(b, carry):
                r0 = pl.multiple_of(b * QB, QB)
                q = qs[pl.ds(r0, QB), :]
                kb = ks[pl.ds(r0, 2 * QB), :]
                vb = vs[pl.ds(r0, 2 * QB), :]
                dyb = dys[pl.ds(r0, QB), :]
                lb = ls[pl.ds(r0, QB), :]
                db = dds[pl.ds(r0, QB), :]
                first = ((b % nb) == 0).astype(jnp.int32)
                dqs = []
                dkb = jnp.zeros((2 * QB, 128), F32)
                dvb = jnp.zeros((2 * QB, 128), F32)
                for hh in range(2):
                    sel = (lane < 64) if hh == 0 else (lane >= 64)
                    c0 = 64 * hh
                    qm = jnp.where(sel, q, jnp.zeros_like(q))
                    dym = jnp.where(sel, dyb, jnp.zeros_like(dyb))
                    s = _mm_nt(qm, kb) + b_ref[p, hh, first]
                    pr = jnp.exp(s - lb[:, c0:c0 + 1])
                    dp = _mm_nt(dym, vb)
                    ds = pr * (dp - db[:, c0:c0 + 1])
                    db_ref[p, hh] += ds
                    dqs.append(_mm(ds, kb))
                    dkb = dkb + _mm_tn(ds, qm)
                    dvb = dvb + _mm_tn(pr, dym)
                dq_perm[pl.ds(r0, QB), :] = jnp.where(lane < 64, dqs[0], dqs[1])
                dk_perm[pl.ds(r0, 2 * QB), :] += dkb
                dv_perm[pl.ds(r0, 2 * QB), :] += dvb
                return carry

            lax.fori_loop(0, S // QB, blk, 0, unroll=4)
            for r in range(d):
                idx = pl.ds(r, L, stride=d) if d > 1 else pl.ds(0, S)
                dq_ref[idx, :] += dq_perm[r * L:(r + 1) * L, :] * 0.125
                dk_ref[idx, :] += dk_perm[QB + r * L:QB + (r + 1) * L, :]
                dv_ref[idx, :] += dv_perm[QB + r * L:QB + (r + 1) * L, :]

    col = lambda c0: pl.BlockSpec((S, 128), lambda hp: (0, c0 + hp))
    bspec = pl.BlockSpec((3, 2, 8, 2 * QB), lambda hp: (0, hp, 0, 0))
    return _call(
        body, name=name, grid=(N_HEADS // 2,),
        in_specs=[col(0), col(4), col(8), bspec, col(0), col(0), col(0)],
        out_specs=(col(0), col(0), col(0), pl.BlockSpec((3, 2, QB, 2 * QB), lambda hp: (0, hp, 0, 0))),
        out_shape=(_sds((S, D_ATT), F32), _sds((S, D_ATT), F32), _sds((S, D_ATT), F32),
                   _sds((3, N_HEADS, QB, 2 * QB), F32)),
        scratch=[pltpu.VMEM((S, 128), MXU), pltpu.VMEM((S + QB, 128), MXU), pltpu.VMEM((S + QB, 128), MXU),
                 pltpu.VMEM((S, 128), MXU), pltpu.VMEM((S, 128), F32), pltpu.VMEM((S, 128), F32),
                 pltpu.VMEM((S, 128), F32), pltpu.VMEM((S, 128), F32),
                 pltpu.VMEM((S + QB, 128), F32), pltpu.VMEM((S + QB, 128), F32),
                 pltpu.VMEM((3, 2, 2, QB, 2 * QB), F32)],
        cp=_cp(48, ("arbitrary",)), args=(z, z, z, bias, y, lse, dy), ride=ride)


def relbias_grad(dbiases):
    bucket, band, _ = _att_static()
    onehot = (bucket[:, None] == np.arange(N_BUCKETS)[None, :, None, None]) & band[:, None]
    onehot = jnp.asarray(onehot.reshape(3, N_BUCKETS, QB * 2 * QB), BF16)

    def body(db0_ref, db1_ref, oh_ref, o_ref):
        acc = jnp.zeros((N_HEADS, N_BUCKETS), F32)
        for p in range(3):
            acc = acc + lax.dot_general(db0_ref[p] + db1_ref[p], oh_ref[p].astype(F32), (((1,), (1,)), ((), ())),
                                        preferred_element_type=F32, precision=lax.Precision.HIGHEST)
        o_ref[...] = acc

    vm = pl.BlockSpec(memory_space=pltpu.VMEM)
    out = _pallas_call(body, name="relbias_grad", in_specs=[vm, vm, vm], out_specs=vm,
                         out_shape=_sds((N_HEADS, N_BUCKETS), F32), compiler_params=_cp(40))(
        *[d.reshape(3, N_HEADS, QB * 2 * QB) for d in dbiases], onehot)
    return out.T


def _panel(t_ref, ri, j):
    return t_ref[ri, pl.ds(j, S, stride=8), :]


def _gelu(x):
    c = math.sqrt(2.0 / math.pi)
    th = jnp.tanh(c * (x + 0.044715 * x * x * x))
    return 0.5 * x * (1.0 + th), th


def ssm_fwd(z, a, bre, bim, cre, cim, dsk, gluw, glub, name):
    def body(u_ref, a_ref, bre_ref, bim_ref, cre_ref, cim_ref, d_ref, gw_ref, gb_ref, y_ref, yp_ref, st_hbm, st_ref):
        u = u_ref[...]
        for j in range(8):
            st_ref[0, pl.ds(j, S, stride=8), :] = _mm(u, bre_ref[:, 128 * j:128 * (j + 1)])
            st_ref[1, pl.ds(j, S, stride=8), :] = _mm(u, bim_ref[:, 128 * j:128 * (j + 1)])
        ar, ai = a_ref[0], a_ref[1]

        def step(t, c):
            re, im = c
            i = pl.multiple_of(t * 8, 8)
            nre = ar * re - ai * im + st_ref[0, pl.ds(i, 8), :]
            nim = ar * im + ai * re + st_ref[1, pl.ds(i, 8), :]
            st_ref[0, pl.ds(i, 8), :] = nre
            st_ref[1, pl.ds(i, 8), :] = nim
            return nre, nim

        zero = jnp.zeros((8, 128), F32)
        lax.fori_loop(0, S, step, (zero, zero), unroll=8)
        y = d_ref[...] * u
        for j in range(8):
            y = y + _mm(_panel(st_ref, 0, j), cre_ref[128 * j:128 * (j + 1), :])
            y = y - _mm(_panel(st_ref, 1, j), cim_ref[128 * j:128 * (j + 1), :])
        pltpu.sync_copy(st_ref, st_hbm)
        yp_ref[...] = y
        gl, _ = _gelu(y)
        tt = _mm(gl, gw_ref[...].reshape(D_SSM, D_SSM)) + gb_ref[...]
        y_ref[...] = y * jax.nn.sigmoid(tt)

    vm = lambda shape: pl.BlockSpec(shape, lambda i: (0,) * len(shape))
    return _pallas_call(
        body, name=name, grid=(1,),
        in_specs=[pl.BlockSpec((S, 256), lambda i: (0, 6)), vm((2, 8, 128)), vm((256, 1024)), vm((256, 1024)),
                  vm((1024, 256)), vm((1024, 256)), vm((1, 256)),
                  vm((NDEV, 32, 256)), vm((1, 256))],
        out_specs=(vm((S, 256)), vm((S, 256)), pl.BlockSpec(memory_space=pl.ANY)),
        out_shape=(_sds((S, 256), F32), _sds((S, 256), F32), _sds((2, S * 8, 128), F32)),
        scratch_shapes=[pltpu.VMEM((2, S * 8, 128), F32)],
        compiler_params=_cp(40, ("arbitrary",)))(z, a, bre, bim, cre, cim, dsk, gluw, glub)


def ssm_bwd(dy, z, ypre, st, a, bre, bim, cre, cim, dsk, gluw, glub, name, ride=None):
    def body(dy_ref, u_ref, yp_ref, st_hbm, a_ref, bre_ref, bim_ref, cre_ref, cim_ref, d_ref, gw_ref, gb_ref,
             du_ref, dbre_ref, dbim_ref, dcre_ref, dcim_ref, da_ref, dd_ref, dgw_ref, dgb_ref, g_ref, st_ref):
        pltpu.sync_copy(st_hbm, st_ref)
        u = u_ref[...]
        y = yp_ref[...]
        dout = dy_ref[...]
        gw = gw_ref[...].reshape(D_SSM, D_SSM)
        gl, th = _gelu(y)
        sig = jax.nn.sigmoid(_mm(gl, gw) + gb_ref[...])
        dt = dout * y * sig * (1.0 - sig)
        dgw_ref[...] = _mm_tn(gl, dt)
        dgb_ref[...] = jnp.sum(dt, axis=0, keepdims=True)
        c = math.sqrt(2.0 / math.pi)
        dgelu = 0.5 * (1.0 + th) + 0.5 * y * (1.0 - th * th) * c * (1.0 + 3.0 * 0.044715 * y * y)
        dyv = dout * sig + _mm_nt(dt, gw) * dgelu
        dd_ref[...] = jnp.sum(dyv * u, axis=0, keepdims=True)
        for j in range(8):
            rows = slice(128 * j, 128 * (j + 1))
            g_ref[0, pl.ds(j, S, stride=8), :] = _mm_nt(dyv, cre_ref[rows, :])
            g_ref[1, pl.ds(j, S, stride=8), :] = -_mm_nt(dyv, cim_ref[rows, :])
            dcre_ref[rows, :] = _mm_tn(_panel(st_ref, 0, j), dyv)
            dcim_ref[rows, :] = -_mm_tn(_panel(st_ref, 1, j), dyv)
        ar, ai = a_ref[0], a_ref[1]

        def step(k, c4):
            gre, gim, dar, dai = c4
            i = pl.multiple_of((S - 1 - k) * 8, 8)
            nre = g_ref[0, pl.ds(i, 8), :] + ar * gre + ai * gim
            nim = g_ref[1, pl.ds(i, 8), :] + ar * gim - ai * gre
            g_ref[0, pl.ds(i, 8), :] = nre
            g_ref[1, pl.ds(i, 8), :] = nim
            sre = st_ref[0, pl.ds(i - 8, 8), :]
            sim = st_ref[1, pl.ds(i - 8, 8), :]
            return nre, nim, dar + nre * sre + nim * sim, dai + nim * sre - nre * sim

        zero = jnp.zeros((8, 128), F32)
        gre, gim, dar, dai = lax.fori_loop(0, S - 1, step, (zero, zero, zero, zero), unroll=8)
        g_ref[0, 0:8, :] = g_ref[0, 0:8, :] + ar * gre + ai * gim
        g_ref[1, 0:8, :] = g_ref[1, 0:8, :] + ar * gim - ai * gre
        da_ref[0] = dar
        da_ref[1] = dai
        du = dyv * d_ref[...]
        for j in range(8):
            cols = slice(128 * j, 128 * (j + 1))
            gr, gi = _panel(g_ref, 0, j), _panel(g_ref, 1, j)
            dbre_ref[:, cols] = _mm_tn(u, gr)
            dbim_ref[:, cols] = _mm_tn(u, gi)
            du = du + _mm_nt(gr, bre_ref[:, cols]) + _mm_nt(gi, bim_ref[:, cols])
        du_ref[...] = du

    vm = lambda shape: pl.BlockSpec(shape, lambda i: (0,) * len(shape))
    return _call(
        body, name=name, grid=(1,),
        in_specs=[vm((S, 256)), pl.BlockSpec((S, 256), lambda i: (0, 6)), vm((S, 256)), pl.BlockSpec(memory_space=pl.ANY),
                  vm((2, 8, 128)), vm((256, 1024)), vm((256, 1024)), vm((1024, 256)), vm((1024, 256)), vm((1, 256)),
                  vm((NDEV, 32, 256)), vm((1, 256))],
        out_specs=(vm((S, 256)), vm((256, 1024)), vm((256, 1024)), vm((1024, 256)), vm((1024, 256)),
                   vm((2, 8, 128)), vm((1, 256)), vm((256, 256)), vm((1, 256))),
        out_shape=(_sds((S, 256), F32), _sds((256, 1024), F32), _sds((256, 1024), F32), _sds((1024, 256), F32),
                   _sds((1024, 256), F32), _sds((2, 8, 128), F32), _sds((1, 256), F32), _sds((256, 256), F32),
                   _sds((1, 256), F32)),
        scratch=[pltpu.VMEM((2, S * 8, 128), F32), pltpu.VMEM((2, S * 8, 128), F32)],
        cp=_cp(56, ("arbitrary",)), args=(dy, z, ypre, st, a, bre, bim, cre, cim, dsk, gluw, glub), ride=ride)


def _ssm_discretise(a_re, a_im, log_dt, b_re, b_im):
    dt = jnp.exp(log_dt)[:, None]
    er = jnp.exp(a_re * dt)
    abr, abi = er * jnp.cos(a_im * dt), er * jnp.sin(a_im * dt)
    den = a_re * a_re + a_im * a_im
    fr = ((abr - 1.0) * a_re + abi * a_im) / den
    fi = (abi * a_re - (abr - 1.0) * a_im) / den
    bbr = fr[:, :, None] * b_re - fi[:, :, None] * b_im
    bbi = fr[:, :, None] * b_im + fi[:, :, None] * b_re
    return abr, abi, bbr, bbi


def _blockdiag(t):
    g, r, c = t.shape
    eye = jnp.eye(g, dtype=t.dtype)
    return (t[:, :, None, :] * eye[:, None, :, None]).reshape(g * r, g * c)


def _blockdiag_take(m, r, c):
    g = m.shape[0] // r
    idx = jnp.arange(g)
    return m.reshape(g, r, g, c)[idx, :, idx, :]


PAD = 16


def _pool_lane_select(vals):
    lane = lax.broadcasted_iota(jnp.int32, vals[0].shape, 1)
    out = vals[3]
    for g in (2, 1, 0):
        out = jnp.where(lane < 64 * (g + 1), vals[g], out)
    return out


def _pool_counts():
    row = lax.broadcasted_iota(jnp.int32, (S, D_POOL), 0).astype(F32) + 1.0
    return _pool_lane_select([jnp.minimum(row, float(w)) for w in POOL_WINDOWS])


def _pooled(u, sa, sb):
    sums = []
    cur = u
    bufs = (sa, sb)
    for k, sh in enumerate((1, 2, 4, 8)):
        buf = bufs[k % 2]
        buf[PAD:PAD + S, :] = cur
        cur = cur + buf[PAD - sh:PAD - sh + S, :]
        sums.append(cur)
    return _pool_lane_select(sums) / _pool_counts() - u


def pool_fwd(z, pw, psc, name):
    def body(u_ref, w_ref, s_ref, y_ref, sa, sb):
        for buf in (sa, sb):
            buf[0:PAD, :] = jnp.zeros((PAD, D_POOL), F32)
        pooled = _pooled(u_ref[...], sa, sb)
        y_ref[...] = _mm(pooled, w_ref[...]) * s_ref[...]

    vm = lambda shape: pl.BlockSpec(shape, lambda i: (0,) * len(shape))
    return _pallas_call(
        body, name=name, grid=(1,),
        in_specs=[pl.BlockSpec((S, 256), lambda i: (0, 7)), vm((256, 256)), vm((1, 256))],
        out_specs=vm((S, 256)), out_shape=_sds((S, 256), F32),
        scratch_shapes=[pltpu.VMEM((S + 2 * PAD, D_POOL), F32)] * 2,
        compiler_params=_cp(40, ("arbitrary",)))(z, pw, psc)


def pool_bwd(dy, z, pw, psc, name):
    def body(dy_ref, u_ref, w_ref, s_ref, du_ref, dw_ref, ds_ref, sa, sb):
        for buf in (sa, sb):
            buf[0:PAD, :] = jnp.zeros((PAD, D_POOL), F32)
            buf[PAD + S:PAD + S + PAD, :] = jnp.zeros((PAD, D_POOL), F32)
        pooled = _pooled(u_ref[...], sa, sb)
        dyv = dy_ref[...]
        w = w_ref[...]
        ds_ref[...] = jnp.sum(dyv * _mm(pooled, w), axis=0, keepdims=True)
        dyl = dyv * s_ref[...]
        dw_ref[...] = _mm_tn(pooled, dyl)
        dpool = _mm_nt(dyl, w)
        cur = dpool / _pool_counts()
        sums = []
        bufs = (sa, sb)
        for k, sh in enumerate((1, 2, 4, 8)):
            buf = bufs[k % 2]
            buf[PAD:PAD + S, :] = cur
            cur = cur + buf[PAD + sh:PAD + sh + S, :]
            sums.append(cur)
        du_ref[...] = _pool_lane_select(sums) - dpool

    vm = lambda shape: pl.BlockSpec(shape, lambda i: (0,) * len(shape))
    return _pallas_call(
        body, name=name, grid=(1,),
        in_specs=[vm((S, 256)), pl.BlockSpec((S, 256), lambda i: (0, 7)), vm((256, 256)), vm((1, 256))],
        out_specs=(vm((S, 256)), vm((256, 256)), vm((1, 256))),
        out_shape=(_sds((S, 256), F32), _sds((256, 256), F32), _sds((1, 256), F32)),
        scratch_shapes=[pltpu.VMEM((S + 2 * PAD, D_POOL), F32)] * 2,
        compiler_params=_cp(40, ("arbitrary",)))(dy, z, pw, psc)


def ada_fwd(c_all, ada_w, ada_b_cols):
    def body(c_ref, w_ref, b_ref, o_ref):
        c = c_ref[...]
        cond = c * jax.nn.sigmoid(c)
        o_ref[...] = jnp.dot(cond, w_ref[...], preferred_element_type=F32, precision=lax.Precision.HIGHEST) + b_ref[...]

    return _pallas_call(
        body, name="ada_fwd", grid=(DEPTH,),
        in_specs=[pl.BlockSpec((NDEV, D), lambda l: (0, 0)), pl.BlockSpec((None, D, 1152), lambda l: (l, 0, 0)),
                  pl.BlockSpec((None, 1, 1152), lambda l: (l, 0, 0))],
        out_specs=pl.BlockSpec((None, NDEV, 1152), lambda l: (l, 0, 0)), out_shape=_sds((DEPTH, NDEV, 1152), F32),
        compiler_params=_cp(40, ("arbitrary",)))(c_all, ada_w, ada_b_cols)


def ada_bwd(c_all, dmod_cols):
    def body(c_ref, dm_ref, o_ref):
        c = c_ref[...]
        cond = c * jax.nn.sigmoid(c)
        o_ref[...] = lax.dot_general(cond, dm_ref[...], (((0,), (0,)), ((), ())), preferred_element_type=F32,
                                     precision=lax.Precision.HIGHEST)

    return _pallas_call(
        body, name="ada_bwd", grid=(DEPTH,),
        in_specs=[pl.BlockSpec((NDEV, D), lambda l: (0, 0)), pl.BlockSpec((None, NDEV, 1152), lambda l: (l, 0, 0))],
        out_specs=pl.BlockSpec((None, D, 1152), lambda l: (l, 0, 0)), out_shape=_sds((DEPTH, D, 1152), F32),
        compiler_params=_cp(40, ("arbitrary",)))(c_all, dmod_cols)


def _adamw(w, g, m, v):
    m2 = B1 * m + (1.0 - B1) * g
    v2 = B2 * v + (1.0 - B2) * (g * g)
    m_hat = m2 / (1.0 - B1 ** STEP)
    v_hat = v2 / (1.0 - B2 ** STEP)
    return -LR * (m_hat / (jnp.sqrt(v_hat) + EPS) + WD * w), m2, v2


def _sum8(ref):
    g = ref[0].astype(F32)
    for s in range(1, NDEV):
        g = g + ref[s].astype(F32)
    return g


def adam_rs(recv, w, m, v, tr, name):
    lead, (r, cdim) = w.shape[:-2], w.shape[-2:]
    cp = recv.shape[-1]
    nl = len(lead)

    def body(rc_ref, w_ref, m_ref, v_ref, g_ref, d_ref, m2_ref, v2_ref):
        g = _sum8(rc_ref)[:, :cdim]
        g_ref[...] = g
        d_ref[...], m2_ref[...], v2_ref[...] = _adamw(w_ref[...], g, m_ref[...], v_ref[...])

    rs = pl.BlockSpec((None,) * nl + (tr, cdim), lambda *i: (*i, 0))
    return _pallas_call(
        body, name=name, grid=lead + (r // tr,),
        in_specs=[pl.BlockSpec((NDEV,) + (None,) * nl + (tr, cp), lambda *i: (0, *i, 0)), rs, rs, rs],
        out_specs=(rs, rs, rs, rs), out_shape=tuple(_sds(w.shape, F32) for _ in range(4)),
        compiler_params=_cp(48, ("arbitrary",) * (nl + 1)))(recv, w, m, v)


def adam_rs_rows(recv, w, m, v, name):
    half = FB // 2

    def body(rc_ref, w_ref, m_ref, v_ref, g_ref, d_ref, m2_ref, v2_ref):
        g = _sum8(rc_ref)
        g_ref[...] = g
        d_ref[...], m2_ref[...], v2_ref[...] = _adamw(w_ref[...], g, m_ref[...], v_ref[...])

    rs = pl.BlockSpec((None, None, half, D), lambda l, f, i: (l, f, i, 0))
    return _pallas_call(
        body, name=name, grid=(DEPTH, 2, 2),
        in_specs=[pl.BlockSpec((NDEV, None, None, half, D), lambda l, f, i: (0, l, f, i, 0)), rs, rs, rs],
        out_specs=(rs, rs, rs, rs), out_shape=tuple(_sds((DEPTH, 2, FB, D), F32) for _ in range(4)),
        compiler_params=_cp(48, ("arbitrary",) * 3))(recv, w, m, v)


def adam_plain(g, w, m, v, tr, name):
    lead, (r, cdim) = w.shape[:-2], w.shape[-2:]
    nl = len(lead)

    def body(g_ref, w_ref, m_ref, v_ref, d_ref, m2_ref, v2_ref):
        d_ref[...], m2_ref[...], v2_ref[...] = _adamw(w_ref[...], g_ref[...], m_ref[...], v_ref[...])

    rs = pl.BlockSpec((None,) * nl + (tr, cdim), lambda *i: (*i, 0))
    return _pallas_call(
        body, name=name, grid=lead + (r // tr,), in_specs=[rs, rs, rs, rs], out_specs=(rs, rs, rs),
        out_shape=tuple(_sds(w.shape, F32) for _ in range(3)),
        compiler_params=_cp(48, ("arbitrary",) * (nl + 1)))(g, w, m, v)


def adam_native(gs, ws, ms, vs, name):
    n = len(ws)

    def body(*refs):
        g_refs, w_refs, m_refs, v_refs = (refs[k * n:(k + 1) * n] for k in range(4))
        d_refs, m2_refs, v2_refs = (refs[(4 + k) * n:(5 + k) * n] for k in range(3))
        for a in range(n):
            d_refs[a][...], m2_refs[a][...], v2_refs[a][...] = _adamw(w_refs[a][...], g_refs[a][...], m_refs[a][...], v_refs[a][...])

    vm = pl.BlockSpec(memory_space=pltpu.VMEM)
    outs = _pallas_call(body, name=name, in_specs=[vm] * (4 * n), out_specs=tuple([vm] * (3 * n)),
                        out_shape=tuple(_sds(w.shape, F32) for w in ws) * 3, compiler_params=_cp(40))(*gs, *ws, *ms, *vs)
    return outs[:n], outs[n:2 * n], outs[2 * n:]


def sum_sources(recv, name):
    r = recv.shape[1]

    def body(rc_ref, o_ref):
        o_ref[...] = _sum8(rc_ref)

    vm = pl.BlockSpec(memory_space=pltpu.VMEM)
    return _pallas_call(body, name=name, in_specs=[vm], out_specs=vm, out_shape=_sds((r, 128), F32),
                          compiler_params=_cp(40))(recv)


def _pack(arrs):
    flat = jnp.concatenate([a.reshape(-1) for a in arrs])
    n = flat.shape[0]
    rows = -(-n // 1024) * 8
    return jnp.pad(flat, (0, rows * 128 - n)).reshape(rows, 128)


def _unpack(vec, shapes):
    flat = vec.reshape(-1)
    out, o = [], 0
    for sh in shapes:
        n = int(np.prod(sh))
        out.append(flat[o:o + n].reshape(sh))
        o += n
    return out


WEIGHTS = ['rel_bias', 'ada_w', 'ada_b', 'ln_g', 'ln_b', 'ffn_w_gate', 'ffn_w_up', 'ffn_w_down', 'w_in', 'w_out',
           'ssm_a_re', 'ssm_a_im', 'ssm_log_dt', 'ssm_b_re', 'ssm_b_im', 'ssm_c_re', 'ssm_c_im', 'ssm_d', 'glu_w',
           'glu_b', 'pool_w', 'pool_scale']
SMALL = ['rel_bias', 'ada_b', 'ln_g', 'ln_b', 'ssm_a_re', 'ssm_a_im', 'ssm_log_dt', 'ssm_b_re', 'ssm_b_im',
         'ssm_c_re', 'ssm_c_im', 'ssm_d', 'glu_b', 'pool_w', 'pool_scale']
SMALL_FULL_SHAPES = {'rel_bias': (32, 8), 'ada_b': (2, 9216), 'ln_g': (2, 3, 1024), 'ln_b': (2, 3, 1024),
                     'ssm_a_re': (2, 16, 64), 'ssm_a_im': (2, 16, 64), 'ssm_log_dt': (2, 16),
                     'ssm_b_re': (2, 16, 64, 16), 'ssm_b_im': (2, 16, 64, 16), 'ssm_c_re': (2, 16, 16, 64),
                     'ssm_c_im': (2, 16, 16, 64), 'ssm_d': (2, 256), 'glu_b': (2, 256), 'pool_w': (2, 4, 64, 64),
                     'pool_scale': (2, 256)}


def _step(P):
    me = _me()
    x0 = P['x'][0]
    target = P['loss_target'][0]

    def shards(l, sub):
        bf = lambda a: a.astype(BF16)
        if sub == 1:
            return [bf(P['w_in'][l]), bf(P['w_out'][l]), bf(P['glu_w'][l])]
        f = sub // 2
        padr = lambda a: jnp.pad(bf(a), ((0, FBP - FB), (0, 0)))
        return [padr(P['ffn_w_gate'][l, f].T), padr(P['ffn_w_up'][l, f].T), padr(P['ffn_w_down'][l, f])]

    order = [(l, sub) for l in range(DEPTH) for sub in range(3)]
    nxt = dict(zip(order[:-1], order[1:]))
    W = {}
    c_all, lng_all, lnb_all, *W[order[0]] = _exchange(Gather([P['c'], P['ln_g'], P['ln_b']] + shards(*order[0])), "gather_first")
    c_all = c_all.reshape(NDEV, D)
    ln_g = jnp.transpose(lng_all, (1, 2, 0, 3)).reshape(DEPTH, 3, D)
    ln_b = jnp.transpose(lnb_all, (1, 2, 0, 3)).reshape(DEPTH, 3, D)

    ada_b_cols = lax.dynamic_slice_in_dim(P['ada_b'], me * 1152, 1152, axis=1).reshape(DEPTH, 1, 1152)
    modc = ada_fwd(c_all, P['ada_w'], ada_b_cols)
    (mod_all,) = _exchange(Gather([modc]), "gather_mod")
    mod_me = lax.dynamic_index_in_dim(mod_all, me, axis=2, keepdims=False)
    mod = jnp.transpose(mod_me, (1, 0, 2)).reshape(DEPTH, 9, D)

    bias = att_bias(P['rel_bias'])
    ssm = []
    for l in range(DEPTH):
        prm = (P['ssm_a_re'][l], P['ssm_a_im'][l], P['ssm_log_dt'][l], P['ssm_b_re'][l], P['ssm_b_im'][l])
        (abr, abi, bbr, bbi), disc_vjp = jax.vjp(_ssm_discretise, *prm)
        ssm.append(dict(
            vjp=disc_vjp, a=jnp.stack([abr.reshape(8, 128), abi.reshape(8, 128)]),
            bre=_blockdiag(jnp.transpose(bbr, (0, 2, 1))).astype(MXU), bim=_blockdiag(jnp.transpose(bbi, (0, 2, 1))).astype(MXU),
            cre=_blockdiag(jnp.transpose(P['ssm_c_re'][l], (0, 2, 1))).astype(MXU),
            cim=_blockdiag(jnp.transpose(P['ssm_c_im'][l], (0, 2, 1))).astype(MXU),
            d=P['ssm_d'][l].reshape(1, 256), gb=P['glu_b'][l].reshape(1, 256),
            pw=_blockdiag(P['pool_w'][l]).astype(MXU), psc=P['pool_scale'][l].reshape(1, 256)))

    saved = []
    x = x0
    for l, sub in order:
        tag = f"l{l}s{sub}"
        ride = Gather(shards(*nxt[(l, sub)])) if (l, sub) in nxt else None
        h = ln_mod_fwd(x, mod[l], sub, "ln_mod_fwd_" + tag)
        if sub != 1:
            wg, wu, wd = (t.reshape(NDEV * FBP, D) for t in W[(l, sub)])
            (G, U, fo), got = ffn_fwd(h, wg, wu, wd, "ffn_fwd_" + tag, ride)
            saved.append(dict(x=x, h=h, G=G, U=U, f=fo))
            x = res_ln_fwd(x, fo, mod[l], sub, ln_g[l], ln_b[l], 0.5, "res_ln_fwd_" + tag)
        else:
            sp = ssm[l]
            win, wout, gluw = W[(l, sub)]
            z = win_fwd(h, win, "win_fwd_" + tag)
            (ya, lse), got = att_fwd(z, bias, "att_fwd_" + tag, ride)
            ys, ypre, st = ssm_fwd(z, sp['a'], sp['bre'], sp['bim'], sp['cre'], sp['cim'], sp['d'], gluw, sp['gb'], "ssm_fwd_" + tag)
            yp = pool_fwd(z, sp['pw'], sp['psc'], "pool_fwd_" + tag)
            o = wout_fwd(ya, ys, yp, wout, "wout_fwd_" + tag)
            saved.append(dict(x=x, h=h, z=z, ya=ya, lse=lse, ys=ys, ypre=ypre, st=st, yp=yp, f=o))
            x = res_ln_fwd(x, o, mod[l], sub, ln_g[l], ln_b[l], 1.0, "res_ln_fwd_" + tag)
        if ride is not None:
            W[nxt[(l, sub)]] = got

    loss_tile, dx = loss_fwd_bwd(x, target, "loss")
    loss = lax.psum(loss_tile[0, 0], ("x", "y", "c"))

    recv = dict(g=lax.empty((NDEV, DEPTH, 2, FBP, D), BF16), u=lax.empty((NDEV, DEPTH, 2, FBP, D), BF16),
                d=lax.empty((NDEV, DEPTH, 2, FBP, D), BF16), win=lax.empty((NDEV, DEPTH, D, 256), BF16),
                wout=lax.empty((NDEV, DEPTH, 128, D), BF16), glu=lax.empty((NDEV, DEPTH, 32, 256), BF16))
    pending = []

    def take(k):
        items = pending[:k]
        del pending[:k]
        if not items:
            return None, []
        return Scatter([p for _, _, p in items], [recv[key] for key, _, _ in items], [idx for _, idx, _ in items]), items

    def landed(items, bufs):
        for (key, _, _), b in zip(items, bufs):
            recv[key] = b

    dmod = [[None] * 9 for _ in range(DEPTH)]
    dlng = [[None] * 3 for _ in range(DEPTH)]
    dlnb = [[None] * 3 for _ in range(DEPTH)]
    dbiases = [None] * DEPTH
    small_l = [dict() for _ in range(DEPTH)]
    for l, sub in reversed(order):
        tag = f"l{l}s{sub}"
        sv = saved[3 * l + sub]
        w = 1.0 if sub == 1 else 0.5
        dxa, df, sums = res_ln_bwd(sv['x'], sv['f'], mod[l], sub, ln_g[l], dx, w, "res_ln_bwd_" + tag)
        dlng[l][sub], dlnb[l][sub], dmod[l][3 * sub + 2] = sums[0], sums[1], sums[2]
        if sub != 1:
            f = sub // 2
            wg, wu, wd = (t.reshape(NDEV * FBP, D) for t in W[(l, sub)])
            ride, items = take({(1, 0): 3, (0, 0): 4}.get((l, sub), 2))
            (dwg, dwu, dwd, dh), bufs = ffn_bwd(df, sv['h'], sv['G'], sv['U'], wg, wu, wd, "ffn_bwd_" + tag, ride)
            landed(items, bufs)
            pending += [(key, (l, f), t.reshape(NDEV, FBP, D)) for key, t in (('g', dwg), ('u', dwu), ('d', dwd))]
        else:
            sp = ssm[l]
            win, wout, gluw = W[(l, sub)]
            dya, dys, dyp, dwout = wout_bwd(df, sv['ya'], sv['ys'], sv['yp'], wout, "wout_bwd_" + tag)
            ride, items = take(2)
            (dq, dk, dv, dbiases[l]), bufs = att_bwd(sv['z'], bias, sv['ya'], sv['lse'], dya, "att_bwd_" + tag, ride)
            landed(items, bufs)
            ride, items = take(1)
            (dus, dbre, dbim, dcre, dcim, da, dd, dgw, dgb), bufs = ssm_bwd(
                dys, sv['z'], sv['ypre'], sv['st'], sp['a'], sp['bre'], sp['bim'], sp['cre'], sp['cim'], sp['d'],
                gluw, sp['gb'], "ssm_bwd_" + tag, ride)
            landed(items, bufs)
            dup, dpw, dpsc = pool_bwd(dyp, sv['z'], sp['pw'], sp['psc'], "pool_bwd_" + tag)
            dh, dwin = win_bwd((dq, dk, dv, dus, dup), sv['h'], win, "win_bwd_" + tag)
            pending += [('win', (l,), dwin), ('wout', (l,), dwout), ('glu', (l,), dgw.astype(BF16).reshape(NDEV, 32, 256))]
            d_are, d_aim, d_ldt, d_bre, d_bim = sp['vjp']((
                da[0].reshape(16, 64), da[1].reshape(16, 64),
                jnp.transpose(_blockdiag_take(dbre, 16, 64), (0, 2, 1)), jnp.transpose(_blockdiag_take(dbim, 16, 64), (0, 2, 1))))
            small_l[l] = dict(
                ssm_a_re=d_are, ssm_a_im=d_aim, ssm_log_dt=d_ldt, ssm_b_re=d_bre, ssm_b_im=d_bim,
                ssm_c_re=jnp.transpose(_blockdiag_take(dcre, 64, 16), (0, 2, 1)),
                ssm_c_im=jnp.transpose(_blockdiag_take(dcim, 64, 16), (0, 2, 1)),
                ssm_d=dd.reshape(256), glu_b=dgb.reshape(256), pool_w=_blockdiag_take(dpw, 64, 64), pool_scale=dpsc.reshape(256))
        dx, sums = ln_mod_bwd(sv['x'], dh, mod[l], sub, dxa, "ln_mod_bwd_" + tag)
        dmod[l][3 * sub], dmod[l][3 * sub + 1] = sums[0], sums[1]
    grad_x = dx[None]
    ride, items = take(len(pending))
    landed(items, _exchange(ride, "scatter_last"))

    small = {k: jnp.stack([small_l[l][k] for l in range(DEPTH)]) for k in small_l[0]}
    small['rel_bias'] = relbias_grad(dbiases)
    small['ada_b'] = jnp.stack([jnp.stack(dmod[l]).reshape(9 * D) for l in range(DEPTH)])
    small['ln_g'] = jnp.stack([jnp.stack(dlng[l]) for l in range(DEPTH)])
    small['ln_b'] = jnp.stack([jnp.stack(dlnb[l]) for l in range(DEPTH)])
    (small_all,) = _exchange(Gather([_pack([small[k] for k in SMALL])]), "gather_small")
    gsum = dict(zip(SMALL, _unpack(sum_sources(small_all, "sum_small"), [SMALL_FULL_SHAPES[k] for k in SMALL])))
    off = 256
    dmod_all = small_all.reshape(NDEV, -1)[:, off:off + DEPTH * 9 * D].reshape(NDEV, DEPTH, 9 * D)
    dmod_cols = jnp.transpose(lax.dynamic_slice_in_dim(dmod_all, me * 1152, 1152, axis=2), (1, 0, 2))
    g_ada_w = ada_bwd(c_all, dmod_cols)

    out = {}

    def put(name, g, d, m2, v2, shape):
        out['grad_' + name], out['delta_' + name] = g.reshape(shape), d.reshape(shape)
        out['new_m_' + name], out['new_v_' + name] = m2.reshape(shape), v2.reshape(shape)

    def wmv(name):
        return [P[pre + name] for pre in ('', 'm_', 'v_')]

    for name, key, tr in (('w_in', 'win', 512), ('w_out', 'wout', 128), ('glu_w', 'glu', 32)):
        put(name, *adam_rs(recv[key], *wmv(name), tr, "adam_" + name), P[name].shape)
    put('ffn_w_down', *adam_rs_rows(recv['d'], *wmv('ffn_w_down'), "adam_ffn_w_down"), P['ffn_w_down'].shape)
    for name, key in (('ffn_w_gate', 'g'), ('ffn_w_up', 'u')):
        res = adam_rs_rows(recv[key], *[jnp.swapaxes(t, 2, 3) for t in wmv(name)], "adam_" + name)
        put(name, *[jnp.swapaxes(t, 2, 3) for t in res], P[name].shape)
    put('ada_w', g_ada_w, *adam_plain(g_ada_w, *wmv('ada_w'), 256, "adam_ada_w"), P['ada_w'].shape)

    for k in ('ln_g', 'ln_b'):
        gsum[k] = lax.dynamic_slice_in_dim(gsum[k], me * 128, 128, axis=2)
    swaps = {'rel_bias': (0, 1), 'ln_g': (0, 1), 'ln_b': (0, 1), 'ssm_b_re': (2, 3), 'ssm_b_im': (2, 3)}
    view = lambda k, t: jnp.swapaxes(t, *swaps[k]) if k in swaps else t
    ds_, m2s, v2s = adam_native(*[[view(k, src(k)) for k in SMALL] for src in
                                  (lambda k: gsum[k], lambda k: P[k], lambda k: P['m_' + k], lambda k: P['v_' + k])],
                                "adam_small")
    for k, d, m2, v2 in zip(SMALL, ds_, m2s, v2s):
        put(k, gsum[k], view(k, d), view(k, m2), view(k, v2), P[k].shape)

    res = [loss, grad_x]
    for pre in ('grad_', 'delta_', 'new_m_', 'new_v_'):
        res += [out[pre + k] for k in WEIGHTS]
    return tuple(res)


def kernel(x, c, rel_bias, ada_w, ada_b, ln_g, ln_b, ffn_w_gate, ffn_w_up, ffn_w_down, w_in, w_out, ssm_a_re, ssm_a_im, ssm_log_dt, ssm_b_re, ssm_b_im, ssm_c_re, ssm_c_im, ssm_d, glu_w, glu_b, pool_w, pool_scale, loss_target, m_rel_bias, m_ada_w, m_ada_b, m_ln_g, m_ln_b, m_ffn_w_gate, m_ffn_w_up, m_ffn_w_down, m_w_in, m_w_out, m_ssm_a_re, m_ssm_a_im, m_ssm_log_dt, m_ssm_b_re, m_ssm_b_im, m_ssm_c_re, m_ssm_c_im, m_ssm_d, m_glu_w, m_glu_b, m_pool_w, m_pool_scale, v_rel_bias, v_ada_w, v_ada_b, v_ln_g, v_ln_b, v_ffn_w_gate, v_ffn_w_up, v_ffn_w_down, v_w_in, v_w_out, v_ssm_a_re, v_ssm_a_im, v_ssm_log_dt, v_ssm_b_re, v_ssm_b_im, v_ssm_c_re, v_ssm_c_im, v_ssm_d, v_glu_w, v_glu_b, v_pool_w, v_pool_scale):
    return _step(dict(locals()))
```

```python
import functools
import math

import numpy as np
import jax
import jax.numpy as jnp
from jax import lax
from jax.experimental import pallas as pl
from jax.experimental.pallas import tpu as pltpu

F32 = jnp.float32
BF16 = jnp.bfloat16
MXU = jnp.bfloat16

S = 2048
D = 1024
NDEV = 8
DEPTH = 2
D_ATT, D_SSM, D_POOL, D_IN = 512, 256, 256, 2048
N_HEADS = 8
FB = 352
FBP = 384
QB = 128
PATTERNS = ((128, 1), (512, 4), (2048, 16))
POOL_WINDOWS = (2, 4, 8, 16)
N_BUCKETS, MAX_DISTANCE = 32, 2048
ALPHA = (2 * DEPTH) ** 0.25
LN_EPS = 1e-5
NEG = -1e30
GATHER_US_PER_BYTE = 43e-6
SCATTER_US_PER_BYTE = 12.3e-6
LR, B1, B2, EPS, WD, STEP = 0.001, 0.9, 0.999, 1e-08, 0.01, 10

TM = 256
TMM = 512
MIB = 1024 * 1024


def _cp(vmem_mib, sem=None):
    kw = dict(vmem_limit_bytes=vmem_mib * MIB)
    if sem is not None:
        kw["dimension_semantics"] = sem
    return pltpu.CompilerParams(**kw)


def _sds(shape, dtype):
    return jax.ShapeDtypeStruct(shape, dtype)


def _mm(a, b):
    return jnp.dot(a.astype(MXU), b.astype(MXU), preferred_element_type=F32)


def _mm_nt(a, b):
    return lax.dot_general(a.astype(MXU), b.astype(MXU), (((1,), (1,)), ((), ())), preferred_element_type=F32)


def _mm_tn(a, b):
    return lax.dot_general(a.astype(MXU), b.astype(MXU), (((0,), (0,)), ((), ())), preferred_element_type=F32)


def _ln_stats(x):
    mu = jnp.mean(x, axis=-1, keepdims=True)
    xc = x - mu
    var = jnp.mean(xc * xc, axis=-1, keepdims=True)
    rstd = lax.rsqrt(var + LN_EPS)
    return xc * rstd, rstd


def _ln_bwd(dn, n, rstd):
    return rstd * (dn - jnp.mean(dn, axis=-1, keepdims=True) - n * jnp.mean(dn * n, axis=-1, keepdims=True))


def _me():
    return 4 * lax.axis_index("x") + 2 * lax.axis_index("y") + lax.axis_index("c")


ANY = pl.BlockSpec(memory_space=pl.ANY)
PIN_BYTES = 1 << 19


def _pallas_call(*a, **k):
    big = lambda o: math.prod(o.shape) * o.dtype.itemsize >= PIN_BYTES
    pin = lambda o: pltpu.HBM(o.shape, o.dtype) if isinstance(o, jax.ShapeDtypeStruct) and big(o) else o
    osh = k["out_shape"]
    k["out_shape"] = tuple(pin(o) for o in osh) if isinstance(osh, (tuple, list)) else pin(osh)
    fn = pl.pallas_call(*a, **k)

    def run(*args):
        return fn(*[pltpu.with_memory_space_constraint(x, pltpu.HBM) if big(x) else x for x in args])
    return run


class Gather:
    def __init__(self, srcs):
        self.srcs = list(srcs)
        self.n = len(self.srcs)
        self.bufs = []
        self.out_shapes = [_sds((NDEV,) + a.shape, a.dtype) for a in self.srcs]
        self.sems = [pltpu.SemaphoreType.DMA((7 * self.n,)), pltpu.SemaphoreType.DMA((7 * self.n,)),
                     pltpu.SemaphoreType.DMA((self.n,))]

    def _parts(self, srcs, outs, sems):
        send_sems, recv_sems, loc_sems = sems
        x, y, c = lax.axis_index("x"), lax.axis_index("y"), lax.axis_index("c")
        me, sib = (x, y, c), (x, y, 1 - c)
        chips = [(1 - x, y), (x, 1 - y), (1 - x, 1 - y)]
        slot = lambda d: 4 * d[0] + 2 * d[1] + d[2]

        def copy(a, k, block, to, src=None):
            dst = outs[a].at[slot(block)]
            return pltpu.make_async_remote_copy(
                src_ref=dst if src is None else src, dst_ref=dst,
                send_sem=send_sems.at[7 * a + k], recv_sem=recv_sems.at[7 * a + k],
                device_id=to, device_id_type=pl.DeviceIdType.MESH)

        local = [pltpu.make_async_copy(srcs[a], outs[a].at[slot(me)], loc_sems.at[a]) for a in range(self.n)]
        return me, sib, chips, c, copy, local

    def start(self, srcs, bufs, outs, sems):
        me, sib, chips, c, copy, local = self._parts(srcs, outs, sems)
        for a in range(self.n):
            local[a].start()
            copy(a, 0, me, sib, src=srcs[a]).start()
            for j, chip in enumerate(chips):
                copy(a, 1 + j, me, (*chip, c), src=srcs[a]).start()

    def finish(self, srcs, bufs, outs, sems):
        me, sib, chips, c, copy, local = self._parts(srcs, outs, sems)
        for a in range(self.n):
            for j, chip in enumerate(chips):
                copy(a, 1 + j, (*chip, c), me).wait_recv()
                copy(a, 4 + j, (*chip, c), sib).start()
        for a in range(self.n):
            copy(a, 0, sib, me).wait_recv()
            copy(a, 0, me, sib, src=srcs[a]).wait_send()
            for j, chip in enumerate(chips):
                copy(a, 4 + j, (*chip, 1 - c), me).wait_recv()
                copy(a, 1 + j, me, (*chip, c), src=srcs[a]).wait_send()
                copy(a, 4 + j, (*chip, c), sib).wait_send()
            local[a].wait()


class Scatter:
    def __init__(self, items, bufs):
        self.items = list(items)
        self.keys = list(dict.fromkeys(key for _, key, _, _ in self.items))
        self.srcs = [src for src, _, _, _ in self.items]
        self.bufs = [bufs[key] for key in self.keys]
        self.n = len(self.srcs)
        self.out_shapes = [_sds(b.shape, b.dtype) for b in self.bufs]
        pairs = [(a, k) for a, (_, _, _, ks) in enumerate(self.items) for k in ks]
        self.remote_pairs = [p for p in pairs if p[1] != 0]
        self.local_pairs = [p for p in pairs if p[1] == 0]
        self.sems = [pltpu.SemaphoreType.DMA((max(len(self.remote_pairs), 1),)),
                     pltpu.SemaphoreType.DMA((max(len(self.remote_pairs), 1),)),
                     pltpu.SemaphoreType.DMA((max(len(self.local_pairs), 1),))]

    def _copies(self, srcs, outs, sems):
        send_sems, recv_sems, loc_sems = sems
        me = _me()

        def dst(a, slot):
            _, key, index, _ = self.items[a]
            return outs[self.keys.index(key)].at[(slot,) + tuple(index)]

        def remote(n, slot):
            a, k = self.remote_pairs[n]
            t = (me + k) % NDEV
            return pltpu.make_async_remote_copy(
                src_ref=srcs[a].at[t], dst_ref=dst(a, slot), send_sem=send_sems.at[n], recv_sem=recv_sems.at[n],
                device_id=(t // 4, (t // 2) % 2, t % 2), device_id_type=pl.DeviceIdType.MESH)

        local = [pltpu.make_async_copy(srcs[a].at[me], dst(a, me), loc_sems.at[n])
                 for n, (a, _) in enumerate(self.local_pairs)]
        return me, remote, local

    def start(self, srcs, bufs, outs, sems):
        me, remote, local = self._copies(srcs, outs, sems)
        for cp in local:
            cp.start()
        for n in range(len(self.remote_pairs)):
            remote(n, me).start()

    def finish(self, srcs, bufs, outs, sems):
        me, remote, local = self._copies(srcs, outs, sems)
        for n, (_, k) in enumerate(self.remote_pairs):
            remote(n, (me - k) % NDEV).wait()
        for cp in local:
            cp.wait()


def _call(body, *, name, grid, in_specs, out_specs, out_shape, args, scratch=(), cp=None, ride=None):
    out_specs, out_shape, scratch = list(out_specs), list(out_shape), list(scratch)
    if ride is None:
        outs = _pallas_call(body, name=name, grid=grid, in_specs=list(in_specs), out_specs=tuple(out_specs),
                              out_shape=tuple(out_shape), scratch_shapes=scratch, compiler_params=cp)(*args)
        return list(outs), []
    nin, nout, nscr, n, nb, no = len(in_specs), len(out_specs), len(scratch), ride.n, len(ride.bufs), len(ride.out_shapes)
    steps = list(grid)

    def wrapped(*refs):
        h_in, r_src, r_buf = refs[:nin], refs[nin:nin + n], refs[nin + n:nin + n + nb]
        o0 = nin + n + nb
        h_out, r_out = refs[o0:o0 + nout], refs[o0 + nout:o0 + nout + no]
        s0 = o0 + nout + no
        h_scr, sems = refs[s0:s0 + nscr], refs[s0 + nscr:]
        ids = [pl.program_id(a) for a in range(len(steps))]
        first = functools.reduce(jnp.logical_and, [i == 0 for i in ids])
        last = functools.reduce(jnp.logical_and, [i == s - 1 for i, s in zip(ids, steps)])

        @pl.when(first)
        def _():
            ride.start(r_src, r_buf, r_out, sems)

        body(*h_in, *h_out, *h_scr)

        @pl.when(last)
        def _():
            ride.finish(r_src, r_buf, r_out, sems)

    aliases = {nin + n + k: nout + k for k in range(nb)}
    outs = _pallas_call(
        wrapped, name=name, grid=grid, in_specs=list(in_specs) + [ANY] * (n + nb),
        out_specs=tuple(out_specs + [ANY] * no), out_shape=tuple(out_shape + ride.out_shapes),
        scratch_shapes=scratch + ride.sems, input_output_aliases=aliases, compiler_params=cp,
    )(*args, *ride.srcs, *ride.bufs)
    return list(outs[:nout]), list(outs[nout:])


def _exchange(ride, name):
    def body(dummy_ref, o_ref):
        o_ref[...] = dummy_ref[...]

    one = pl.BlockSpec((8, 128), lambda i: (0, 0))
    _, outs = _call(body, name=name, grid=(1,), in_specs=[one], out_specs=[one], out_shape=[_sds((8, 128), F32)],
                    args=(jnp.zeros((8, 128), F32),), ride=ride)
    return outs


def _row_spec(cols, tm=TM):
    return pl.BlockSpec((tm, cols), lambda i: (i, 0))


def _full_spec(shape):
    nd = len(shape)
    return pl.BlockSpec(shape, lambda i: (0,) * nd)


def ln_mod_fwd(x, mod, sub, name):
    def body(x_ref, mod_ref, h_ref):
        n, _ = _ln_stats(x_ref[...])
        shift = mod_ref[3 * sub:3 * sub + 1, :]
        scale = mod_ref[3 * sub + 1:3 * sub + 2, :]
        h_ref[...] = (n * (1.0 + scale) + shift).astype(MXU)

    return _pallas_call(
        body, name=name, grid=(S // TM,),
        in_specs=[_row_spec(D), _full_spec((9, D))], out_specs=_row_spec(D),
        out_shape=_sds((S, D), MXU), compiler_params=_cp(32, ("arbitrary",)))(x, mod)


def res_ln_fwd(x, f, mod, sub, lng, lnb, w, name, nxt=None):
    def body(x_ref, f_ref, mod_ref, g_ref, b_ref, *rest):
        gate = mod_ref[3 * sub + 2:3 * sub + 3, :]
        r = ALPHA * x_ref[...] + (w * gate) * f_ref[...]
        n, _ = _ln_stats(r)
        xo = n * g_ref[sub:sub + 1, :] + b_ref[sub:sub + 1, :]
        rest[-1 if nxt is None else -2][...] = xo
        if nxt is not None:
            nmod_ref, h_ref = rest[0], rest[-1]
            n2, _ = _ln_stats(xo)
            s2 = nxt[1]
            h_ref[...] = (n2 * (1.0 + nmod_ref[3 * s2 + 1:3 * s2 + 2, :]) + nmod_ref[3 * s2:3 * s2 + 1, :]).astype(MXU)

    more = nxt is not None
    return _pallas_call(
        body, name=name, grid=(S // TM,),
        in_specs=[_row_spec(D), _row_spec(D), _full_spec((9, D)), _full_spec((3, D)), _full_spec((3, D))] + [_full_spec((9, D))] * more,
        out_specs=(_row_spec(D),) + (_row_spec(D),) * more, out_shape=(_sds((S, D), F32),) + (_sds((S, D), MXU),) * more,
        compiler_params=_cp(32, ("arbitrary",)))(x, f, mod, lng, lnb, *([nxt[0]] if more else []))


def res_ln_bwd(x, f, mod, sub, lng, dxo, w, name, ride=None):
    def body(x_ref, f_ref, mod_ref, g_ref, dxo_ref, dxa_ref, df_ref, sums_ref):
        i = pl.program_id(0)
        gate = mod_ref[3 * sub + 2:3 * sub + 3, :]
        fv = f_ref[...]
        r = ALPHA * x_ref[...] + (w * gate) * fv
        n, rstd = _ln_stats(r)
        dxo = dxo_ref[...]
        dr = _ln_bwd(dxo * g_ref[sub:sub + 1, :], n, rstd)
        dxa_ref[...] = ALPHA * dr
        df_ref[...] = ((w * gate) * dr).astype(MXU)
        part = jnp.concatenate([
            jnp.sum(dxo * n, axis=0, keepdims=True),
            jnp.sum(dxo, axis=0, keepdims=True),
            jnp.sum(dr * fv, axis=0, keepdims=True) * w,
            jnp.zeros((5, D), F32)], axis=0)

        @pl.when(i == 0)
        def _():
            sums_ref[...] = part

        @pl.when(i > 0)
        def _():
            sums_ref[...] += part

    return _call(
        body, name=name, grid=(S // TM,),
        in_specs=[_row_spec(D), _row_spec(D), _full_spec((9, D)), _full_spec((3, D)), _row_spec(D)],
        out_specs=(_row_spec(D), _row_spec(D), _full_spec((8, D))),
        out_shape=(_sds((S, D), F32), _sds((S, D), MXU), _sds((8, D), F32)),
        cp=_cp(32, ("arbitrary",)), args=(x, f, mod, lng, dxo), ride=ride)


def ln_mod_bwd(x, dh, mod, sub, dxa, name, ride=None):
    def body(x_ref, dh_ref, mod_ref, dxa_ref, dx_ref, sums_ref):
        i = pl.program_id(0)
        scale = mod_ref[3 * sub + 1:3 * sub + 2, :]
        n, rstd = _ln_stats(x_ref[...])
        dh = dh_ref[...]
        dx_ref[...] = dxa_ref[...] + _ln_bwd(dh * (1.0 + scale), n, rstd)
        part = jnp.concatenate([
            jnp.sum(dh, axis=0, keepdims=True),
            jnp.sum(dh * n, axis=0, keepdims=True),
            jnp.zeros((6, D), F32)], axis=0)

        @pl.when(i == 0)
        def _():
            sums_ref[...] = part

        @pl.when(i > 0)
        def _():
            sums_ref[...] += part

    return _call(
        body, name=name, grid=(S // TM,),
        in_specs=[_row_spec(D), _row_spec(D), _full_spec((9, D)), _row_spec(D)],
        out_specs=(_row_spec(D), _full_spec((8, D))),
        out_shape=(_sds((S, D), F32), _sds((8, D), F32)),
        cp=_cp(32, ("arbitrary",)), args=(x, dh, mod, dxa), ride=ride)


def loss_fwd_bwd(y, target, name):
    def body(y_ref, t_ref, l_ref, dy_ref):
        i = pl.program_id(0)
        e = y_ref[...] - t_ref[...]
        dy_ref[...] = e * (1.0 / D)
        part = jnp.zeros((8, 128), F32) + (0.5 / D) * jnp.sum(e * e)

        @pl.when(i == 0)
        def _():
            l_ref[...] = part

        @pl.when(i > 0)
        def _():
            l_ref[...] += part

    return _pallas_call(
        body, name=name, grid=(S // TM,),
        in_specs=[_row_spec(D), _row_spec(D)], out_specs=(_full_spec((8, 128)), _row_spec(D)),
        out_shape=(_sds((8, 128), F32), _sds((S, D), F32)),
        compiler_params=_cp(32, ("arbitrary",)))(y, target)


HB = 2 * FBP
NHB = NDEV * FBP // HB
TMB = 1024


def _wrows(buffers=2):
    return pl.BlockSpec((HB, D), lambda j, i: (j, 0), pipeline_mode=pl.Buffered(buffers))


def _resident(shape):
    return pl.BlockSpec(shape, lambda j, i: (0, 0), pipeline_mode=pl.Buffered(1))


def ffn_fwd(h, wgt, wut, wd, name, ride=None):
    def body(h_ref, wg_ref, wu_ref, wd_ref, g_ref, u_ref, f_ref):
        j, i = pl.program_id(0), pl.program_id(1)
        hv = h_ref[...]
        g = _mm_nt(hv, wg_ref[...])
        u = _mm_nt(hv, wu_ref[...])
        g_ref[...] = g.astype(MXU)
        u_ref[...] = u.astype(MXU)
        a = g * jax.nn.sigmoid(g) * u
        part = _mm(a, wd_ref[...])
        rows = pl.ds(pl.multiple_of(i * TMB, TMB), TMB)

        @pl.when(j == 0)
        def _():
            f_ref[rows, :] = part

        @pl.when(j > 0)
        def _():
            f_ref[rows, :] += part

    gu = pl.BlockSpec((TMB, HB), lambda j, i: (i, j))
    return _call(
        body, name=name, grid=(NHB, S // TMB),
        in_specs=[pl.BlockSpec((TMB, D), lambda j, i: (i, 0)), _wrows(), _wrows(), _wrows()],
        out_specs=(gu, gu, _resident((S, D))),
        out_shape=(_sds((S, NDEV * FBP), MXU), _sds((S, NDEV * FBP), MXU), _sds((S, D), F32)),
        cp=_cp(52, ("arbitrary", "arbitrary")), args=(h, wgt, wut, wd), ride=ride)


def ffn_bwd(df, h, g, u, wgt, wut, wd, name, ride=None):
    ni = S // TMB

    def body(df_ref, h_ref, g_ref, u_ref, wg_ref, wu_ref, wd_ref, dwg_ref, dwu_ref, dwd_ref, dh_ref,
             ag_ref, au_ref, ad_ref):
        j, i = pl.program_id(0), pl.program_id(1)
        dfv, hv = df_ref[...], h_ref[...]
        gv, uv = g_ref[...].astype(F32), u_ref[...].astype(F32)
        da = _mm_nt(dfv, wd_ref[...])
        sg = jax.nn.sigmoid(gv)
        silu = gv * sg
        du = da * silu
        dg = da * uv * (sg * (1.0 + gv * (1.0 - sg)))
        p_d = _mm_tn(silu * uv, dfv)
        p_g = _mm_tn(dg, hv)
        p_u = _mm_tn(du, hv)

        @pl.when(i == 0)
        def _():
            ad_ref[...] = p_d
            ag_ref[...] = p_g
            au_ref[...] = p_u

        @pl.when(i > 0)
        def _():
            ad_ref[...] += p_d
            ag_ref[...] += p_g
            au_ref[...] += p_u

        @pl.when(i == ni - 1)
        def _():
            dwd_ref[...] = ad_ref[...].astype(BF16)
            dwg_ref[...] = ag_ref[...].astype(BF16)
            dwu_ref[...] = au_ref[...].astype(BF16)

        part = _mm(dg, wg_ref[...]) + _mm(du, wu_ref[...])
        rows = pl.ds(pl.multiple_of(i * TMB, TMB), TMB)

        @pl.when(j == 0)
        def _():
            dh_ref[rows, :] = part

        @pl.when(j > 0)
        def _():
            dh_ref[rows, :] += part

    gu = pl.BlockSpec((TMB, HB), lambda j, i: (i, j))
    rowt = pl.BlockSpec((TMB, D), lambda j, i: (i, 0))
    return _call(
        body, name=name, grid=(NHB, ni),
        in_specs=[rowt, rowt, gu, gu, _wrows(1), _wrows(1), _wrows(1)],
        out_specs=(_wrows(1), _wrows(1), _wrows(1), _resident((S, D))),
        out_shape=(_sds((NDEV * FBP, D), BF16), _sds((NDEV * FBP, D), BF16), _sds((NDEV * FBP, D), BF16), _sds((S, D), F32)),
        scratch=[pltpu.VMEM((HB, D), F32), pltpu.VMEM((HB, D), F32), pltpu.VMEM((HB, D), F32)],
        cp=_cp(60, ("arbitrary", "arbitrary")), args=(df, h, g, u, wgt, wut, wd), ride=ride)


def win_fwd(h, win, name, ride=None):
    def body(h_ref, w_ref, z_ref):
        hv = h_ref[...]
        for j in range(NDEV):
            z_ref[:, 256 * j:256 * (j + 1)] = _mm(hv, w_ref[j])

    return _call(
        body, name=name, grid=(S // TMM,),
        in_specs=[_row_spec(D, TMM), _full_spec((NDEV, D, 256))],
        out_specs=[_row_spec(D_IN, TMM)], out_shape=[_sds((S, D_IN), F32)],
        cp=_cp(40, ("arbitrary",)), args=(h, win), ride=ride)


def win_bwd(dparts, h, win, name, ride=None):
    ni = S // TMM

    def body(dq_ref, dk_ref, dv_ref, dus_ref, dup_ref, h_ref, w_ref, dh_ref, dw_ref, acc_ref):
        i = pl.program_id(0)
        hv = h_ref[...]
        cols = [dq_ref[:, 0:256], dq_ref[:, 256:512], dk_ref[:, 0:256], dk_ref[:, 256:512],
                dv_ref[:, 0:256], dv_ref[:, 256:512], dus_ref[...], dup_ref[...]]
        dh = jnp.zeros((TMM, D), F32)
        for j in range(NDEV):
            dz = cols[j].astype(MXU)
            dh = dh + _mm_nt(dz, w_ref[j])
            p = _mm_tn(hv, dz)

            @pl.when(i == 0)
            def _():
                acc_ref[j] = p

            @pl.when(i > 0)
            def _():
                acc_ref[j] += p

        dh_ref[...] = dh

        @pl.when(i == ni - 1)
        def _():
            dw_ref[...] = acc_ref[...].astype(BF16)

    return _call(
        body, name=name, grid=(ni,),
        in_specs=[_row_spec(512, TMM), _row_spec(512, TMM), _row_spec(512, TMM), _row_spec(256, TMM), _row_spec(256, TMM),
                  _row_spec(D, TMM), _full_spec((NDEV, D, 256))],
        out_specs=(_row_spec(D, TMM), _full_spec((NDEV, D, 256))),
        out_shape=(_sds((S, D), F32), _sds((NDEV, D, 256), BF16)),
        scratch=[pltpu.VMEM((NDEV, D, 256), F32)],
        cp=_cp(48, ("arbitrary",)), args=(*dparts, h, win), ride=ride)


def wout_fwd(ya, ys, yp, wout, name, ride=None):
    def body(ya_ref, ys_ref, yp_ref, w_ref, o_ref):
        w = w_ref[...].reshape(D, D)
        o_ref[...] = _mm(ya_ref[...], w[0:512]) + _mm(ys_ref[...], w[512:768]) + _mm(yp_ref[...], w[768:1024])

    return _call(
        body, name=name, grid=(S // TMM,),
        in_specs=[_row_spec(512, TMM), _row_spec(256, TMM), _row_spec(256, TMM), _full_spec((NDEV, 128, D))],
        out_specs=[_row_spec(D, TMM)], out_shape=[_sds((S, D), F32)],
        cp=_cp(40, ("arbitrary",)), args=(ya, ys, yp, wout), ride=ride)


def wout_bwd(do, ya, ys, yp, wout, name, ride=None):
    ni = S // TMM

    def body(do_ref, ya_ref, ys_ref, yp_ref, w_ref, dya_ref, dys_ref, dyp_ref, dw_ref, acc_ref):
        i = pl.program_id(0)
        w = w_ref[...].reshape(D, D)
        dov = do_ref[...]
        dya_ref[...] = _mm_nt(dov, w[0:512])
        dys_ref[...] = _mm_nt(dov, w[512:768])
        dyp_ref[...] = _mm_nt(dov, w[768:1024])
        parts = [(0, 512, _mm_tn(ya_ref[...], dov)), (512, 768, _mm_tn(ys_ref[...], dov)),
                 (768, 1024, _mm_tn(yp_ref[...], dov))]
        for lo, hi, p in parts:
            @pl.when(i == 0)
            def _():
                acc_ref[lo:hi, :] = p

            @pl.when(i > 0)
            def _():
                acc_ref[lo:hi, :] += p

        @pl.when(i == ni - 1)
        def _():
            dw_ref[...] = acc_ref[...].astype(BF16).reshape(NDEV, 128, D)

    return _call(
        body, name=name, grid=(ni,),
        in_specs=[_row_spec(D, TMM), _row_spec(512, TMM), _row_spec(256, TMM), _row_spec(256, TMM),
                  _full_spec((NDEV, 128, D))],
        out_specs=(_row_spec(512, TMM), _row_spec(256, TMM), _row_spec(256, TMM), _full_spec((NDEV, 128, D))),
        out_shape=(_sds((S, 512), F32), _sds((S, 256), F32), _sds((S, 256), F32), _sds((NDEV, 128, D), BF16)),
        scratch=[pltpu.VMEM((D, D), F32)],
        cp=_cp(40, ("arbitrary",)), args=(do, ya, ys, yp, wout), ride=ride)


def _t5_bucket(dist):
    max_exact = N_BUCKETS // 2
    d = np.maximum(dist, 1).astype(np.float32)
    large = max_exact + (np.log(d / max_exact) / math.log(MAX_DISTANCE / max_exact)
                         * (N_BUCKETS - max_exact)).astype(np.int32)
    large = np.minimum(large, N_BUCKETS - 1)
    return np.where(dist < max_exact, dist, large).astype(np.int32)


def _att_static():
    i = np.arange(QB)[:, None]
    j = np.arange(2 * QB)[None, :]
    r = i + QB - j
    buckets, bands = [], []
    for window, dil in PATTERNS:
        bands.append((r >= 0) & (r <= window // dil))
        buckets.append(_t5_bucket(np.clip(r, 0, None) * dil))
    return np.stack(buckets), np.stack(bands), np.broadcast_to(j >= QB, (QB, 2 * QB))


def att_bias(rel_bias):
    m = np.arange(2 * QB)
    rows = []
    for window, dil in PATTERNS:
        r = QB - m
        ok = (r >= 0) & (r <= window // dil)
        b = rel_bias[_t5_bucket(np.clip(r, 0, None) * dil)]
        rows.append(jnp.where(ok[:, None], b, NEG).T)
    return jnp.broadcast_to(jnp.stack(rows)[:, :, None, :], (3, N_HEADS, 8, 2 * QB))


def _bias_tiles(t_ref, tiles):
    col = lax.broadcasted_iota(jnp.int32, (QB, 2 * QB), 1)
    for p in range(3):
        for hh in range(2):
            t = pltpu.roll(jnp.broadcast_to(t_ref[p, hh, 0:1, :], (QB, 2 * QB)), 0, 1, stride=1, stride_axis=0)
            tiles[p, hh, 0] = t
            tiles[p, hh, 1] = jnp.where(col >= QB, t, NEG)


def _permute_in(dst_ref, src_ref, d, scale=None, pad=QB):
    L = S // d
    for r in range(d):
        v = src_ref[pl.ds(r, L, stride=d), :] if d > 1 else src_ref[...]
        if scale is not None:
            v = v * scale
        dst_ref[pad + r * L:pad + (r + 1) * L, :] = v.astype(dst_ref.dtype)


def att_fwd(z, bias, name, ride=None):
    def body(q_ref, k_ref, v_ref, t_ref, y_ref, l_ref, qs, ks, vs, o_perm, l_perm, o_nat, l_nat, b_ref):
        _bias_tiles(t_ref, b_ref)
        zero_pad = jnp.zeros((QB, 128), MXU)
        ks[0:QB, :] = zero_pad
        vs[0:QB, :] = zero_pad
        lane = lax.broadcasted_iota(jnp.int32, (QB, 128), 1)
        for p, (_, d) in enumerate(PATTERNS):
            L = S // d
            nb = L // QB
            _permute_in(qs, q_ref, d, scale=0.125, pad=0)
            _permute_in(ks, k_ref, d)
            _permute_in(vs, v_ref, d)

            def blk(b, carry):
                r0 = pl.multiple_of(b * QB, QB)
                q = qs[pl.ds(r0, QB), :]
                kb = ks[pl.ds(r0, 2 * QB), :]
                vb = vs[pl.ds(r0, 2 * QB), :]
                first = ((b % nb) == 0).astype(jnp.int32)
                res = []
                for hh in range(2):
                    sel = (lane < 64) if hh == 0 else (lane >= 64)
                    qm = jnp.where(sel, q, jnp.zeros_like(q))
                    s = _mm_nt(qm, kb) + b_ref[p, hh, first]
                    m = jnp.max(s, axis=1, keepdims=True)
                    pe = jnp.exp(s - m)
                    den = jnp.sum(pe, axis=1, keepdims=True)
                    res.append((_mm(pe, vb) / den, m + jnp.log(den)))
                o_perm[pl.ds(r0, QB), :] = jnp.where(lane < 64, res[0][0], res[1][0])
                l_perm[pl.ds(r0, QB), :] = jnp.where(lane < 64, res[0][1], res[1][1])
                return carry

            lax.fori_loop(0, S // QB, blk, 0, unroll=8)
            for r in range(d):
                if d > 1:
                    o_nat[p, pl.ds(r, L, stride=d), :] = o_perm[r * L:(r + 1) * L, :]
                    l_nat[p, pl.ds(r, L, stride=d), :] = l_perm[r * L:(r + 1) * L, :]
                else:
                    o_nat[p] = o_perm[...]
                    l_nat[p] = l_perm[...]
        l0, l1, l2 = l_nat[0], l_nat[1], l_nat[2]
        m = jnp.maximum(jnp.maximum(l0, l1), l2)
        e0, e1, e2 = jnp.exp(l0 - m), jnp.exp(l1 - m), jnp.exp(l2 - m)
        den = e0 + e1 + e2
        y_ref[...] = (e0 * o_nat[0] + e1 * o_nat[1] + e2 * o_nat[2]) / den
        l_ref[...] = m + jnp.log(den)

    col = lambda c0: pl.BlockSpec((S, 128), lambda hp: (0, c0 + hp))
    return _call(
        body, name=name, grid=(N_HEADS // 2,),
        in_specs=[col(0), col(4), col(8), pl.BlockSpec((3, 2, 8, 2 * QB), lambda hp: (0, hp, 0, 0))],
        out_specs=(col(0), col(0)),
        out_shape=(_sds((S, D_ATT), F32), _sds((S, D_ATT), F32)),
        scratch=[pltpu.VMEM((S, 128), MXU), pltpu.VMEM((S + QB, 128), MXU), pltpu.VMEM((S + QB, 128), MXU),
                 pltpu.VMEM((S, 128), F32), pltpu.VMEM((S, 128), F32),
                 pltpu.VMEM((3, S, 128), F32), pltpu.VMEM((3, S, 128), F32),
                 pltpu.VMEM((3, 2, 2, QB, 2 * QB), F32)],
        cp=_cp(40, ("arbitrary",)), args=(z, z, z, bias), ride=ride)


def att_bwd(z, bias, y, lse, dy, name, ride=None):
    def body(q_ref, k_ref, v_ref, t_ref, y_ref, l_ref, dy_ref, dq_ref, dk_ref, dv_ref, db_ref,
             qs, ks, vs, dys, ls, dds, dn_nat, dq_perm, dk_perm, dv_perm, b_ref):
        _bias_tiles(t_ref, b_ref)
        zero_pad = jnp.zeros((QB, 128), MXU)
        ks[0:QB, :] = zero_pad
        vs[0:QB, :] = zero_pad
        lane = lax.broadcasted_iota(jnp.int32, (QB, 128), 1)
        lane_s = lax.broadcasted_iota(jnp.int32, (S, 128), 1)
        t = dy_ref[...] * y_ref[...]
        sa = jnp.sum(jnp.where(lane_s < 64, t, 0.0), axis=1, keepdims=True)
        sb = jnp.sum(jnp.where(lane_s >= 64, t, 0.0), axis=1, keepdims=True)
        dn_nat[...] = jnp.where(lane_s < 64, sa, sb)
        dq_ref[...] = jnp.zeros((S, 128), F32)
        dk_ref[...] = jnp.zeros((S, 128), F32)
        dv_ref[...] = jnp.zeros((S, 128), F32)
        db_ref[...] = jnp.zeros((3, 2, QB, 2 * QB), F32)
        for p, (_, d) in enumerate(PATTERNS):
            L = S // d
            nb = L // QB
            _permute_in(qs, q_ref, d, scale=0.125, pad=0)
            _permute_in(ks, k_ref, d)
            _permute_in(vs, v_ref, d)
            _permute_in(dys, dy_ref, d, pad=0)
            _permute_in(ls, l_ref, d, pad=0)
            _permute_in(dds, dn_nat, d, pad=0)
            dk_perm[...] = jnp.zeros((S + QB, 128), F32)
            dv_perm[...] = jnp.zeros((S + QB, 128), F32)

            def blk(b, carry):
                r0 = pl.multiple_of(b * QB, QB)
                q = qs[pl.ds(r0, QB), :]
                kb = ks[pl.ds(r0, 2 * QB), :]
                vb = vs[pl.ds(r0, 2 * QB), :]
                dyb = dys[pl.ds(r0, QB), :]
                lb = ls[pl.ds(r0, QB), :]
                db = dds[pl.ds(r0, QB), :]
                first = ((b % nb) == 0).astype(jnp.int32)
                dqs = []
                dkb = jnp.zeros((2 * QB, 128), F32)
                dvb = jnp.zeros((2 * QB, 128), F32)
                for hh in range(2):
                    sel = (lane < 64) if hh == 0 else (lane >= 64)
                    c0 = 64 * hh
                    qm = jnp.where(sel, q, jnp.zeros_like(q))
                    dym = jnp.where(sel, dyb, jnp.zeros_like(dyb))
                    s = _mm_nt(qm, kb) + b_ref[p, hh, first]
                    pr = jnp.exp(s - lb[:, c0:c0 + 1])
                    dp = _mm_nt(dym, vb)
                    ds = pr * (dp - db[:, c0:c0 + 1])
                    db_ref[p, hh] += ds
                    dqs.append(_mm(ds, kb))
                    dkb = dkb + _mm_tn(ds, qm)
                    dvb = dvb + _mm_tn(pr, dym)
                dq_perm[pl.ds(r0, QB), :] = jnp.where(lane < 64, dqs[0], dqs[1])
                dk_perm[pl.ds(r0, 2 * QB), :] += dkb
                dv_perm[pl.ds(r0, 2 * QB), :] += dvb
                return carry

            lax.fori_loop(0, S // QB, blk, 0, unroll=4)
            for r in range(d):
                idx = pl.ds(r, L, stride=d) if d > 1 else pl.ds(0, S)
                dq_ref[idx, :] += dq_perm[r * L:(r + 1) * L, :] * 0.125
                dk_ref[idx, :] += dk_perm[QB + r * L:QB + (r + 1) * L, :]
                dv_ref[idx, :] += dv_perm[QB + r * L:QB + (r + 1) * L, :]

    col = lambda c0: pl.BlockSpec((S, 128), lambda hp: (0, c0 + hp))
    bspec = pl.BlockSpec((3, 2, 8, 2 * QB), lambda hp: (0, hp, 0, 0))
    return _call(
        body, name=name, grid=(N_HEADS // 2,),
        in_specs=[col(0), col(4), col(8), bspec, col(0), col(0), col(0)],
        out_specs=(col(0), col(0), col(0), pl.BlockSpec((3, 2, QB, 2 * QB), lambda hp: (0, hp, 0, 0))),
        out_shape=(_sds((S, D_ATT), F32), _sds((S, D_ATT), F32), _sds((S, D_ATT), F32),
                   _sds((3, N_HEADS, QB, 2 * QB), F32)),
        scratch=[pltpu.VMEM((S, 128), MXU), pltpu.VMEM((S + QB, 128), MXU), pltpu.VMEM((S + QB, 128), MXU),
                 pltpu.VMEM((S, 128), MXU), pltpu.VMEM((S, 128), F32), pltpu.VMEM((S, 128), F32),
                 pltpu.VMEM((S, 128), F32), pltpu.VMEM((S, 128), F32),
                 pltpu.VMEM((S + QB, 128), F32), pltpu.VMEM((S + QB, 128), F32),
                 pltpu.VMEM((3, 2, 2, QB, 2 * QB), F32)],
        cp=_cp(48, ("arbitrary",)), args=(z, z, z, bias, y, lse, dy), ride=ride)


def relbias_grad(dbiases):
    bucket, band, _ = _att_static()
    onehot = (bucket[:, None] == np.arange(N_BUCKETS)[None, :, None, None]) & band[:, None]
    onehot = jnp.asarray(onehot.reshape(3, N_BUCKETS, QB * 2 * QB), BF16)

    def body(db0_ref, db1_ref, oh_ref, o_ref):
        acc = jnp.zeros((N_HEADS, N_BUCKETS), F32)
        for p in range(3):
            acc = acc + lax.dot_general(db0_ref[p] + db1_ref[p], oh_ref[p].astype(F32), (((1,), (1,)), ((), ())),
                                        preferred_element_type=F32, precision=lax.Precision.HIGHEST)
        o_ref[...] = acc

    vm = pl.BlockSpec(memory_space=pltpu.VMEM)
    out = _pallas_call(body, name="relbias_grad", in_specs=[vm, vm, vm], out_specs=vm,
                         out_shape=_sds((N_HEADS, N_BUCKETS), F32), compiler_params=_cp(40))(
        *[d.reshape(3, N_HEADS, QB * 2 * QB) for d in dbiases], onehot)
    return out.T


def _panel(t_ref, ri, j):
    return t_ref[ri, pl.ds(j, S, stride=8), :]


def _gelu(x):
    c = math.sqrt(2.0 / math.pi)
    th = jnp.tanh(c * (x + 0.044715 * x * x * x))
    return 0.5 * x * (1.0 + th), th


def ssm_fwd(z, a, bre, bim, cre, cim, dsk, gluw, glub, name, ride=None):
    def body(u_ref, a_ref, bre_ref, bim_ref, cre_ref, cim_ref, d_ref, gw_ref, gb_ref, y_ref, yp_ref, st_hbm, st_ref):
        u = u_ref[...]
        for j in range(8):
            st_ref[0, pl.ds(j, S, stride=8), :] = _mm(u, bre_ref[:, 128 * j:128 * (j + 1)])
            st_ref[1, pl.ds(j, S, stride=8), :] = _mm(u, bim_ref[:, 128 * j:128 * (j + 1)])
        ar, ai = a_ref[0], a_ref[1]

        def step(t, c):
            re, im = c
            i = pl.multiple_of(t * 8, 8)
            nre = ar * re - ai * im + st_ref[0, pl.ds(i, 8), :]
            nim = ar * im + ai * re + st_ref[1, pl.ds(i, 8), :]
            st_ref[0, pl.ds(i, 8), :] = nre
            st_ref[1, pl.ds(i, 8), :] = nim
            return nre, nim

        zero = jnp.zeros((8, 128), F32)
        lax.fori_loop(0, S, step, (zero, zero), unroll=8)
        y = d_ref[...] * u
        for j in range(8):
            y = y + _mm(_panel(st_ref, 0, j), cre_ref[128 * j:128 * (j + 1), :])
            y = y - _mm(_panel(st_ref, 1, j), cim_ref[128 * j:128 * (j + 1), :])
        pltpu.sync_copy(st_ref, st_hbm)
        yp_ref[...] = y
        gl, _ = _gelu(y)
        tt = _mm(gl, gw_ref[...].reshape(D_SSM, D_SSM)) + gb_ref[...]
        y_ref[...] = y * jax.nn.sigmoid(tt)

    vm = lambda shape: pl.BlockSpec(shape, lambda i: (0,) * len(shape))
    return _call(
        body, name=name, grid=(1,),
        in_specs=[pl.BlockSpec((S, 256), lambda i: (0, 6)), vm((2, 8, 128)), vm((256, 1024)), vm((256, 1024)),
                  vm((1024, 256)), vm((1024, 256)), vm((1, 256)),
                  vm((NDEV, 32, 256)), vm((1, 256))],
        out_specs=(vm((S, 256)), vm((S, 256)), pl.BlockSpec(memory_space=pl.ANY)),
        out_shape=(_sds((S, 256), F32), _sds((S, 256), F32), _sds((2, S * 8, 128), F32)),
        scratch=[pltpu.VMEM((2, S * 8, 128), F32)],
        cp=_cp(40, ("arbitrary",)), args=(z, a, bre, bim, cre, cim, dsk, gluw, glub), ride=ride)


def ssm_bwd(dy, z, ypre, st, a, bre, bim, cre, cim, dsk, gluw, glub, name, ride=None):
    def body(dy_ref, u_ref, yp_ref, st_hbm, a_ref, bre_ref, bim_ref, cre_ref, cim_ref, d_ref, gw_ref, gb_ref,
             du_ref, dbre_ref, dbim_ref, dcre_ref, dcim_ref, da_ref, dd_ref, dgw_ref, dgb_ref, g_ref, st_ref):
        pltpu.sync_copy(st_hbm, st_ref)
        u = u_ref[...]
        y = yp_ref[...]
        dout = dy_ref[...]
        gw = gw_ref[...].reshape(D_SSM, D_SSM)
        gl, th = _gelu(y)
        sig = jax.nn.sigmoid(_mm(gl, gw) + gb_ref[...])
        dt = dout * y * sig * (1.0 - sig)
        dgw_ref[...] = _mm_tn(gl, dt)
        dgb_ref[...] = jnp.sum(dt, axis=0, keepdims=True)
        c = math.sqrt(2.0 / math.pi)
        dgelu = 0.5 * (1.0 + th) + 0.5 * y * (1.0 - th * th) * c * (1.0 + 3.0 * 0.044715 * y * y)
        dyv = dout * sig + _mm_nt(dt, gw) * dgelu
        dd_ref[...] = jnp.sum(dyv * u, axis=0, keepdims=True)
        for j in range(8):
            rows = slice(128 * j, 128 * (j + 1))
            g_ref[0, pl.ds(j, S, stride=8), :] = _mm_nt(dyv, cre_ref[rows, :])
            g_ref[1, pl.ds(j, S, stride=8), :] = -_mm_nt(dyv, cim_ref[rows, :])
            dcre_ref[rows, :] = _mm_tn(_panel(st_ref, 0, j), dyv)
            dcim_ref[rows, :] = -_mm_tn(_panel(st_ref, 1, j), dyv)
        ar, ai = a_ref[0], a_ref[1]

        def step(k, c4):
            gre, gim, dar, dai = c4
            i = pl.multiple_of((S - 1 - k) * 8, 8)
            nre = g_ref[0, pl.ds(i, 8), :] + ar * gre + ai * gim
            nim = g_ref[1, pl.ds(i, 8), :] + ar * gim - ai * gre
            g_ref[0, pl.ds(i, 8), :] = nre
            g_ref[1, pl.ds(i, 8), :] = nim
            sre = st_ref[0, pl.ds(i - 8, 8), :]
            sim = st_ref[1, pl.ds(i - 8, 8), :]
            return nre, nim, dar + nre * sre + nim * sim, dai + nim * sre - nre * sim

        zero = jnp.zeros((8, 128), F32)
        gre, gim, dar, dai = lax.fori_loop(0, S - 1, step, (zero, zero, zero, zero), unroll=8)
        g_ref[0, 0:8, :] = g_ref[0, 0:8, :] + ar * gre + ai * gim
        g_ref[1, 0:8, :] = g_ref[1, 0:8, :] + ar * gim - ai * gre
        da_ref[0] = dar
        da_ref[1] = dai
        du = dyv * d_ref[...]
        for j in range(8):
            cols = slice(128 * j, 128 * (j + 1))
            gr, gi = _panel(g_ref, 0, j), _panel(g_ref, 1, j)
            dbre_ref[:, cols] = _mm_tn(u, gr)
            dbim_ref[:, cols] = _mm_tn(u, gi)
            du = du + _mm_nt(gr, bre_ref[:, cols]) + _mm_nt(gi, bim_ref[:, cols])
        du_ref[...] = du

    vm = lambda shape: pl.BlockSpec(shape, lambda i: (0,) * len(shape))
    return _call(
        body, name=name, grid=(1,),
        in_specs=[vm((S, 256)), pl.BlockSpec((S, 256), lambda i: (0, 6)), vm((S, 256)), pl.BlockSpec(memory_space=pl.ANY),
                  vm((2, 8, 128)), vm((256, 1024)), vm((256, 1024)), vm((1024, 256)), vm((1024, 256)), vm((1, 256)),
                  vm((NDEV, 32, 256)), vm((1, 256))],
        out_specs=(vm((S, 256)), vm((256, 1024)), vm((256, 1024)), vm((1024, 256)), vm((1024, 256)),
                   vm((2, 8, 128)), vm((1, 256)), vm((256, 256)), vm((1, 256))),
        out_shape=(_sds((S, 256), F32), _sds((256, 1024), F32), _sds((256, 1024), F32), _sds((1024, 256), F32),
                   _sds((1024, 256), F32), _sds((2, 8, 128), F32), _sds((1, 256), F32), _sds((256, 256), F32),
                   _sds((1, 256), F32)),
        scratch=[pltpu.VMEM((2, S * 8, 128), F32), pltpu.VMEM((2, S * 8, 128), F32)],
        cp=_cp(56, ("arbitrary",)), args=(dy, z, ypre, st, a, bre, bim, cre, cim, dsk, gluw, glub), ride=ride)


def _ssm_discretise(a_re, a_im, log_dt, b_re, b_im):
    dt = jnp.exp(log_dt)[:, None]
    er = jnp.exp(a_re * dt)
    abr, abi = er * jnp.cos(a_im * dt), er * jnp.sin(a_im * dt)
    den = a_re * a_re + a_im * a_im
    fr = ((abr - 1.0) * a_re + abi * a_im) / den
    fi = (abi * a_re - (abr - 1.0) * a_im) / den
    bbr = fr[:, :, None] * b_re - fi[:, :, None] * b_im
    bbi = fr[:, :, None] * b_im + fi[:, :, None] * b_re
    return abr, abi, bbr, bbi


def _blockdiag(t):
    g, r, c = t.shape
    eye = jnp.eye(g, dtype=t.dtype)
    return (t[:, :, None, :] * eye[:, None, :, None]).reshape(g * r, g * c)


def _blockdiag_take(m, r, c):
    g = m.shape[0] // r
    idx = jnp.arange(g)
    return m.reshape(g, r, g, c)[idx, :, idx, :]


PAD = 16


def _pool_lane_select(vals):
    lane = lax.broadcasted_iota(jnp.int32, vals[0].shape, 1)
    out = vals[3]
    for g in (2, 1, 0):
        out = jnp.where(lane < 64 * (g + 1), vals[g], out)
    return out


def _pool_counts():
    row = lax.broadcasted_iota(jnp.int32, (S, D_POOL), 0).astype(F32) + 1.0
    return _pool_lane_select([jnp.minimum(row, float(w)) for w in POOL_WINDOWS])


def _pooled(u, sa, sb):
    sums = []
    cur = u
    bufs = (sa, sb)
    for k, sh in enumerate((1, 2, 4, 8)):
        buf = bufs[k % 2]
        buf[PAD:PAD + S, :] = cur
        cur = cur + buf[PAD - sh:PAD - sh + S, :]
        sums.append(cur)
    return _pool_lane_select(sums) / _pool_counts() - u


def pool_fwd(z, pw, psc, name):
    def body(u_ref, w_ref, s_ref, y_ref, sa, sb):
        for buf in (sa, sb):
            buf[0:PAD, :] = jnp.zeros((PAD, D_POOL), F32)
        pooled = _pooled(u_ref[...], sa, sb)
        y_ref[...] = _mm(pooled, w_ref[...]) * s_ref[...]

    vm = lambda shape: pl.BlockSpec(shape, lambda i: (0,) * len(shape))
    return _pallas_call(
        body, name=name, grid=(1,),
        in_specs=[pl.BlockSpec((S, 256), lambda i: (0, 7)), vm((256, 256)), vm((1, 256))],
        out_specs=vm((S, 256)), out_shape=_sds((S, 256), F32),
        scratch_shapes=[pltpu.VMEM((S + 2 * PAD, D_POOL), F32)] * 2,
        compiler_params=_cp(40, ("arbitrary",)))(z, pw, psc)


def pool_bwd(dy, z, pw, psc, name):
    def body(dy_ref, u_ref, w_ref, s_ref, du_ref, dw_ref, ds_ref, sa, sb):
        for buf in (sa, sb):
            buf[0:PAD, :] = jnp.zeros((PAD, D_POOL), F32)
            buf[PAD + S:PAD + S + PAD, :] = jnp.zeros((PAD, D_POOL), F32)
        pooled = _pooled(u_ref[...], sa, sb)
        dyv = dy_ref[...]
        w = w_ref[...]
        ds_ref[...] = jnp.sum(dyv * _mm(pooled, w), axis=0, keepdims=True)
        dyl = dyv * s_ref[...]
        dw_ref[...] = _mm_tn(pooled, dyl)
        dpool = _mm_nt(dyl, w)
        cur = dpool / _pool_counts()
        sums = []
        bufs = (sa, sb)
        for k, sh in enumerate((1, 2, 4, 8)):
            buf = bufs[k % 2]
            buf[PAD:PAD + S, :] = cur
            cur = cur + buf[PAD + sh:PAD + sh + S, :]
            sums.append(cur)
        du_ref[...] = _pool_lane_select(sums) - dpool

    vm = lambda shape: pl.BlockSpec(shape, lambda i: (0,) * len(shape))
    return _pallas_call(
        body, name=name, grid=(1,),
        in_specs=[vm((S, 256)), pl.BlockSpec((S, 256), lambda i: (0, 7)), vm((256, 256)), vm((1, 256))],
        out_specs=(vm((S, 256)), vm((256, 256)), vm((1, 256))),
        out_shape=(_sds((S, 256), F32), _sds((256, 256), F32), _sds((1, 256), F32)),
        scratch_shapes=[pltpu.VMEM((S + 2 * PAD, D_POOL), F32)] * 2,
        compiler_params=_cp(40, ("arbitrary",)))(dy, z, pw, psc)


def ada_fwd(c_all, ada_w, ada_b_cols):
    def body(c_ref, w_ref, b_ref, o_ref):
        c = c_ref[...]
        cond = c * jax.nn.sigmoid(c)
        o_ref[...] = jnp.dot(cond, w_ref[...], preferred_element_type=F32, precision=lax.Precision.HIGHEST) + b_ref[...]

    return _pallas_call(
        body, name="ada_fwd", grid=(DEPTH,),
        in_specs=[pl.BlockSpec((NDEV, D), lambda l: (0, 0)), pl.BlockSpec((None, D, 1152), lambda l: (l, 0, 0)),
                  pl.BlockSpec((None, 1, 1152), lambda l: (l, 0, 0))],
        out_specs=pl.BlockSpec((None, NDEV, 1152), lambda l: (l, 0, 0)), out_shape=_sds((DEPTH, NDEV, 1152), F32),
        compiler_params=_cp(40, ("arbitrary",)))(c_all, ada_w, ada_b_cols)


def ada_bwd(c_all, dmod_cols):
    def body(c_ref, dm_ref, o_ref):
        c = c_ref[...]
        cond = c * jax.nn.sigmoid(c)
        o_ref[...] = lax.dot_general(cond, dm_ref[...], (((0,), (0,)), ((), ())), preferred_element_type=F32,
                                     precision=lax.Precision.HIGHEST)

    return _pallas_call(
        body, name="ada_bwd", grid=(DEPTH,),
        in_specs=[pl.BlockSpec((NDEV, D), lambda l: (0, 0)), pl.BlockSpec((None, NDEV, 1152), lambda l: (l, 0, 0))],
        out_specs=pl.BlockSpec((None, D, 1152), lambda l: (l, 0, 0)), out_shape=_sds((DEPTH, D, 1152), F32),
        compiler_params=_cp(40, ("arbitrary",)))(c_all, dmod_cols)


def _adamw(w, g, m, v):
    m2 = B1 * m + (1.0 - B1) * g
    v2 = B2 * v + (1.0 - B2) * (g * g)
    m_hat = m2 / (1.0 - B1 ** STEP)
    v_hat = v2 / (1.0 - B2 ** STEP)
    return -LR * (m_hat / (jnp.sqrt(v_hat) + EPS) + WD * w), m2, v2


def _sum8(ref):
    g = ref[0].astype(F32)
    for s in range(1, NDEV):
        g = g + ref[s].astype(F32)
    return g


def adam_rs(recv, w, m, v, tr, name, ride=None):
    lead, (r, cdim) = w.shape[:-2], w.shape[-2:]
    cp = recv.shape[-1]
    nl = len(lead)

    def body(rc_ref, w_ref, m_ref, v_ref, g_ref, d_ref, m2_ref, v2_ref):
        g = _sum8(rc_ref)[:, :cdim]
        g_ref[...] = g
        d_ref[...], m2_ref[...], v2_ref[...] = _adamw(w_ref[...], g, m_ref[...], v_ref[...])

    rs = pl.BlockSpec((None,) * nl + (tr, cdim), lambda *i: (*i, 0))
    return _call(
        body, name=name, grid=lead + (r // tr,),
        in_specs=[pl.BlockSpec((NDEV,) + (None,) * nl + (tr, cp), lambda *i: (0, *i, 0)), rs, rs, rs],
        out_specs=(rs, rs, rs, rs), out_shape=tuple(_sds(w.shape, F32) for _ in range(4)),
        cp=_cp(48, ("arbitrary",) * (nl + 1)), args=(recv, w, m, v), ride=ride)


def adam_rs_rows(recv, w, m, v, name, lf=(0, 2 * DEPTH), prev=None, ride=None):
    half = FB // 2
    lo, hi = lf

    def body(*refs):
        rc_ref, w_ref, m_ref, v_ref = refs[:4]
        g_ref, d_ref, m2_ref, v2_ref = refs[-4:]
        g = _sum8(rc_ref)
        g_ref[...] = g
        d_ref[...], m2_ref[...], v2_ref[...] = _adamw(w_ref[...], g, m_ref[...], v_ref[...])

    rs = pl.BlockSpec((None, None, half, D), lambda a, i: ((a + lo) // 2, (a + lo) % 2, i, 0))
    rc = pl.BlockSpec((NDEV, None, None, half, D), lambda a, i: (0, (a + lo) // 2, (a + lo) % 2, i, 0))
    prev = list(prev) if prev is not None else []
    call = functools.partial(
        _pallas_call, body, name=name, grid=(hi - lo, 2), in_specs=[rc, rs, rs, rs] + [ANY] * len(prev),
        out_specs=(rs, rs, rs, rs), out_shape=tuple(_sds((DEPTH, 2, FB, D), F32) for _ in range(4)),
        compiler_params=_cp(48, ("arbitrary",) * 2))
    if ride is None:
        return list(call(input_output_aliases={4 + k: k for k in range(len(prev))})(recv, w, m, v, *prev)), []
    assert not prev
    return _call(body, name=name, grid=(hi - lo, 2), in_specs=[rc, rs, rs, rs], out_specs=(rs, rs, rs, rs),
                 out_shape=tuple(_sds((DEPTH, 2, FB, D), F32) for _ in range(4)), cp=_cp(48, ("arbitrary",) * 2),
                 args=(recv, w, m, v), ride=ride)


def adam_plain(g, w, m, v, tr, name, ride=None):
    lead, (r, cdim) = w.shape[:-2], w.shape[-2:]
    nl = len(lead)

    def body(g_ref, w_ref, m_ref, v_ref, d_ref, m2_ref, v2_ref):
        d_ref[...], m2_ref[...], v2_ref[...] = _adamw(w_ref[...], g_ref[...], m_ref[...], v_ref[...])

    rs = pl.BlockSpec((None,) * nl + (tr, cdim), lambda *i: (*i, 0))
    return _call(
        body, name=name, grid=lead + (r // tr,), in_specs=[rs, rs, rs, rs], out_specs=(rs, rs, rs),
        out_shape=tuple(_sds(w.shape, F32) for _ in range(3)),
        cp=_cp(48, ("arbitrary",) * (nl + 1)), args=(g, w, m, v), ride=ride)


def adam_native(gs, ws, ms, vs, name):
    n = len(ws)

    def body(*refs):
        g_refs, w_refs, m_refs, v_refs = (refs[k * n:(k + 1) * n] for k in range(4))
        d_refs, m2_refs, v2_refs = (refs[(4 + k) * n:(5 + k) * n] for k in range(3))
        for a in range(n):
            d_refs[a][...], m2_refs[a][...], v2_refs[a][...] = _adamw(w_refs[a][...], g_refs[a][...], m_refs[a][...], v_refs[a][...])

    vm = pl.BlockSpec(memory_space=pltpu.VMEM)
    outs = _pallas_call(body, name=name, in_specs=[vm] * (4 * n), out_specs=tuple([vm] * (3 * n)),
                        out_shape=tuple(_sds(w.shape, F32) for w in ws) * 3, compiler_params=_cp(40))(*gs, *ws, *ms, *vs)
    return outs[:n], outs[n:2 * n], outs[2 * n:]


def sum_sources(recv, name):
    r = recv.shape[1]

    def body(rc_ref, o_ref):
        o_ref[...] = _sum8(rc_ref)

    vm = pl.BlockSpec(memory_space=pltpu.VMEM)
    return _pallas_call(body, name=name, in_specs=[vm], out_specs=vm, out_shape=_sds((r, 128), F32),
                          compiler_params=_cp(40))(recv)


def _pack(arrs):
    flat = jnp.concatenate([a.reshape(-1) for a in arrs])
    n = flat.shape[0]
    rows = -(-n // 1024) * 8
    return jnp.pad(flat, (0, rows * 128 - n)).reshape(rows, 128)


def _unpack(vec, shapes):
    flat = vec.reshape(-1)
    out, o = [], 0
    for sh in shapes:
        n = int(np.prod(sh))
        out.append(flat[o:o + n].reshape(sh))
        o += n
    return out


WEIGHTS = ['rel_bias', 'ada_w', 'ada_b', 'ln_g', 'ln_b', 'ffn_w_gate', 'ffn_w_up', 'ffn_w_down', 'w_in', 'w_out',
           'ssm_a_re', 'ssm_a_im', 'ssm_log_dt', 'ssm_b_re', 'ssm_b_im', 'ssm_c_re', 'ssm_c_im', 'ssm_d', 'glu_w',
           'glu_b', 'pool_w', 'pool_scale']
SMALL = ['rel_bias', 'ada_b', 'ln_g', 'ln_b', 'ssm_a_re', 'ssm_a_im', 'ssm_log_dt', 'ssm_b_re', 'ssm_b_im',
         'ssm_c_re', 'ssm_c_im', 'ssm_d', 'glu_b', 'pool_w', 'pool_scale']
SMALL_FULL_SHAPES = {'rel_bias': (32, 8), 'ada_b': (2, 9216), 'ln_g': (2, 3, 1024), 'ln_b': (2, 3, 1024),
                     'ssm_a_re': (2, 16, 64), 'ssm_a_im': (2, 16, 64), 'ssm_log_dt': (2, 16),
                     'ssm_b_re': (2, 16, 64, 16), 'ssm_b_im': (2, 16, 64, 16), 'ssm_c_re': (2, 16, 16, 64),
                     'ssm_c_im': (2, 16, 16, 64), 'ssm_d': (2, 256), 'glu_b': (2, 256), 'pool_w': (2, 4, 64, 64),
                     'pool_scale': (2, 256)}


def _step(P):
    me = _me()
    x0 = P['x'][0]
    target = P['loss_target'][0]

    def shards(l, sub):
        bf = lambda a: a.astype(BF16)
        if sub == 1:
            return [bf(P['w_in'][l]), bf(P['w_out'][l]), bf(P['glu_w'][l])]
        f = sub // 2
        padr = lambda a: jnp.pad(bf(a), ((0, FBP - FB), (0, 0)))
        return [padr(P['ffn_w_gate'][l, f].T), padr(P['ffn_w_up'][l, f].T), padr(P['ffn_w_down'][l, f])]

    order = [(l, sub) for l in range(DEPTH) for sub in range(3)]
    nxt = dict(zip(order[:-1], order[1:]))
    W = {key: [None] * 3 for key in order}
    c_all, lng_all, lnb_all, *W[order[0]] = _exchange(Gather([P['c'], P['ln_g'], P['ln_b']] + shards(*order[0])), "gather_first")
    gather_queue = [(key, pos, a) for key in order[1:] for pos, a in enumerate(shards(*key))]

    def gather_ride(cap_us, must=None):
        units, used = [], 0.0
        while gather_queue:
            key, _, a = gather_queue[0]
            cost = a.size * a.dtype.itemsize * GATHER_US_PER_BYTE
            if key != must and used + cost / 2 > cap_us:
                break
            units.append(gather_queue.pop(0))
            used += cost
        return (Gather([a for _, _, a in units]) if units else None), units

    def gathered(units, outs):
        for (key, pos, _), o in zip(units, outs):
            W[key][pos] = o

    c_all = c_all.reshape(NDEV, D)
    ln_g = jnp.transpose(lng_all, (1, 2, 0, 3)).reshape(DEPTH, 3, D)
    ln_b = jnp.transpose(lnb_all, (1, 2, 0, 3)).reshape(DEPTH, 3, D)

    ada_b_cols = lax.dynamic_slice_in_dim(P['ada_b'], me * 1152, 1152, axis=1).reshape(DEPTH, 1, 1152)
    modc = ada_fwd(c_all, P['ada_w'], ada_b_cols)
    (mod_all,) = _exchange(Gather([modc]), "gather_mod")
    mod_me = lax.dynamic_index_in_dim(mod_all, me, axis=2, keepdims=False)
    mod = jnp.transpose(mod_me, (1, 0, 2)).reshape(DEPTH, 9, D)

    bias = att_bias(P['rel_bias'])
    ssm = []
    for l in range(DEPTH):
        prm = (P['ssm_a_re'][l], P['ssm_a_im'][l], P['ssm_log_dt'][l], P['ssm_b_re'][l], P['ssm_b_im'][l])
        (abr, abi, bbr, bbi), disc_vjp = jax.vjp(_ssm_discretise, *prm)
        ssm.append(dict(
            vjp=disc_vjp, a=jnp.stack([abr.reshape(8, 128), abi.reshape(8, 128)]),
            bre=_blockdiag(jnp.transpose(bbr, (0, 2, 1))).astype(MXU), bim=_blockdiag(jnp.transpose(bbi, (0, 2, 1))).astype(MXU),
            cre=_blockdiag(jnp.transpose(P['ssm_c_re'][l], (0, 2, 1))).astype(MXU),
            cim=_blockdiag(jnp.transpose(P['ssm_c_im'][l], (0, 2, 1))).astype(MXU),
            d=P['ssm_d'][l].reshape(1, 256), gb=P['glu_b'][l].reshape(1, 256),
            pw=_blockdiag(P['pool_w'][l]).astype(MXU), psc=P['pool_scale'][l].reshape(1, 256)))

    saved = []
    x = x0
    h = ln_mod_fwd(x, mod[0], 0, "ln_mod_fwd_l0s0")
    for l, sub in order:
        tag = f"l{l}s{sub}"
        after = (mod[nxt[(l, sub)][0]], nxt[(l, sub)][1]) if (l, sub) in nxt else None
        if sub != 1:
            wg, wu, wd = (t.reshape(NDEV * FBP, D) for t in W[(l, sub)])
            ride, units = gather_ride(60, nxt.get((l, sub)))
            (G, U, fo), got = ffn_fwd(h, wg, wu, wd, "ffn_fwd_" + tag, ride)
            gathered(units, got)
            saved.append(dict(x=x, h=h, G=G, U=U, f=fo))
            x, *hn = res_ln_fwd(x, fo, mod[l], sub, ln_g[l], ln_b[l], 0.5, "res_ln_fwd_" + tag, after)
        else:
            sp = ssm[l]
            win, wout, gluw = W[(l, sub)]
            ride, units = gather_ride(15)
            (z,), got = win_fwd(h, win, "win_fwd_" + tag, ride)
            gathered(units, got)
            ride, units = gather_ride(55)
            (ya, lse), got = att_fwd(z, bias, "att_fwd_" + tag, ride)
            gathered(units, got)
            ride, units = gather_ride(35)
            (ys, ypre, st), got = ssm_fwd(z, sp['a'], sp['bre'], sp['bim'], sp['cre'], sp['cim'], sp['d'], gluw, sp['gb'],
                                          "ssm_fwd_" + tag, ride)
            gathered(units, got)
            yp = pool_fwd(z, sp['pw'], sp['psc'], "pool_fwd_" + tag)
            ride, units = gather_ride(12, nxt.get((l, sub)))
            (o,), got = wout_fwd(ya, ys, yp, wout, "wout_fwd_" + tag, ride)
            gathered(units, got)
            saved.append(dict(x=x, h=h, z=z, ya=ya, lse=lse, ys=ys, ypre=ypre, st=st, yp=yp, f=o))
            x, *hn = res_ln_fwd(x, o, mod[l], sub, ln_g[l], ln_b[l], 1.0, "res_ln_fwd_" + tag, after)
        h = hn[0] if hn else None
    assert not gather_queue

    loss_tile, dx = loss_fwd_bwd(x, target, "loss")
    loss = lax.psum(loss_tile[0, 0], ("x", "y", "c"))

    recv = dict(g=lax.empty((NDEV, DEPTH, 2, FBP, D), BF16), u=lax.empty((NDEV, DEPTH, 2, FBP, D), BF16),
                d=lax.empty((NDEV, DEPTH, 2, FBP, D), BF16), win=lax.empty((NDEV, DEPTH, D, 256), BF16),
                wout=lax.empty((NDEV, DEPTH, 128, D), BF16), glu=lax.empty((NDEV, DEPTH, 32, 256), BF16))
    pending = []

    def produced(key, index, payload):
        pending.extend((payload, key, index, k) for k in range(NDEV))

    def scatter_ride(cap_us):
        units, used = [], 0.0
        while pending:
            payload, _, _, k = pending[0]
            cost = payload[0].size * payload.dtype.itemsize * SCATTER_US_PER_BYTE if k else 1.0
            if used + cost / 2 > cap_us:
                break
            units.append(pending.pop(0))
            used += cost
        if not units:
            return None
        items = {}
        for payload, key, index, k in units:
            items.setdefault((id(payload), key, index), [payload, key, index, []])[3].append(k)
        return Scatter([tuple(v) for v in items.values()], recv)

    def landed(ride, bufs):
        if ride is not None:
            recv.update(zip(ride.keys, bufs))

    def hosted(cap_us, fn, *args):
        ride = scatter_ride(cap_us)
        outs, bufs = fn(*args, ride)
        landed(ride, bufs)
        return outs

    dmod = [[None] * 9 for _ in range(DEPTH)]
    dlng = [[None] * 3 for _ in range(DEPTH)]
    dlnb = [[None] * 3 for _ in range(DEPTH)]
    dbiases = [None] * DEPTH
    small_l = [dict() for _ in range(DEPTH)]
    for l, sub in reversed(order):
        tag = f"l{l}s{sub}"
        sv = saved[3 * l + sub]
        w = 1.0 if sub == 1 else 0.5
        dxa, df, sums = hosted(14, res_ln_bwd, sv['x'], sv['f'], mod[l], sub, ln_g[l], dx, w, "res_ln_bwd_" + tag)
        dlng[l][sub], dlnb[l][sub], dmod[l][3 * sub + 2] = sums[0], sums[1], sums[2]
        if sub != 1:
            f = sub // 2
            wg, wu, wd = (t.reshape(NDEV * FBP, D) for t in W[(l, sub)])
            dwg, dwu, dwd, dh = hosted(105, ffn_bwd, df, sv['h'], sv['G'], sv['U'], wg, wu, wd, "ffn_bwd_" + tag)
            for key, t in (('g', dwg), ('u', dwu), ('d', dwd)):
                produced(key, (l, f), t.reshape(NDEV, FBP, D))
        else:
            sp = ssm[l]
            win, wout, gluw = W[(l, sub)]
            dya, dys, dyp, dwout = hosted(15, wout_bwd, df, sv['ya'], sv['ys'], sv['yp'], wout, "wout_bwd_" + tag)
            dq, dk, dv, dbiases[l] = hosted(115, att_bwd, sv['z'], bias, sv['ya'], sv['lse'], dya, "att_bwd_" + tag)
            dus, dbre, dbim, dcre, dcim, da, dd, dgw, dgb = hosted(
                52, ssm_bwd, dys, sv['z'], sv['ypre'], sv['st'], sp['a'], sp['bre'], sp['bim'], sp['cre'], sp['cim'], sp['d'],
                gluw, sp['gb'], "ssm_bwd_" + tag)
            dup, dpw, dpsc = pool_bwd(dyp, sv['z'], sp['pw'], sp['psc'], "pool_bwd_" + tag)
            dh, dwin = hosted(40, win_bwd, (dq, dk, dv, dus, dup), sv['h'], win, "win_bwd_" + tag)
            produced('win', (l,), dwin)
            produced('wout', (l,), dwout)
            produced('glu', (l,), dgw.astype(BF16).reshape(NDEV, 32, 256))
            d_are, d_aim, d_ldt, d_bre, d_bim = sp['vjp']((
                da[0].reshape(16, 64), da[1].reshape(16, 64),
                jnp.transpose(_blockdiag_take(dbre, 16, 64), (0, 2, 1)), jnp.transpose(_blockdiag_take(dbim, 16, 64), (0, 2, 1))))
            small_l[l] = dict(
                ssm_a_re=d_are, ssm_a_im=d_aim, ssm_log_dt=d_ldt, ssm_b_re=d_bre, ssm_b_im=d_bim,
                ssm_c_re=jnp.transpose(_blockdiag_take(dcre, 64, 16), (0, 2, 1)),
                ssm_c_im=jnp.transpose(_blockdiag_take(dcim, 64, 16), (0, 2, 1)),
                ssm_d=dd.reshape(256), glu_b=dgb.reshape(256), pool_w=_blockdiag_take(dpw, 64, 64), pool_scale=dpsc.reshape(256))
        dx, sums = hosted(12, ln_mod_bwd, sv['x'], dh, mod[l], sub, dxa, "ln_mod_bwd_" + tag)
        dmod[l][3 * sub], dmod[l][3 * sub + 1] = sums[0], sums[1]
    grad_x = dx[None]

    out = {}

    def put(name, g, d, m2, v2, shape):
        out['grad_' + name], out['delta_' + name] = g.reshape(shape), d.reshape(shape)
        out['new_m_' + name], out['new_v_' + name] = m2.reshape(shape), v2.reshape(shape)

    def wmv(name):
        return [P[pre + name] for pre in ('', 'm_', 'v_')]

    def complete(key, index=None):
        return not any(k == key and (index is None or i == index) for _, k, i, _ in pending)

    for name, key, tr, cap in (('w_in', 'win', 512, 9), ('w_out', 'wout', 128, 5), ('glu_w', 'glu', 32, 2)):
        assert complete(key)
        put(name, *hosted(cap, adam_rs, recv[key], *wmv(name), tr, "adam_" + name), P[name].shape)
    ffn = {'ffn_w_down': ('d', wmv('ffn_w_down')), 'ffn_w_gate': ('g', [jnp.swapaxes(t, 2, 3) for t in wmv('ffn_w_gate')]),
           'ffn_w_up': ('u', [jnp.swapaxes(t, 2, 3) for t in wmv('ffn_w_up')])}
    part = {}
    for name, (key, ops) in ffn.items():
        assert all(complete(key, (l, f)) for l, f in ((0, 1), (1, 0), (1, 1)))
        part[name] = hosted(17, lambda *a: adam_rs_rows(*a[:-1], lf=(1, 4), ride=a[-1]), recv[key], *ops, "adam_" + name + "_a")
    ride = scatter_ride(1e9)
    landed(ride, _exchange(ride, "scatter_last"))
    for name, (key, ops) in ffn.items():
        res, _ = adam_rs_rows(recv[key], *ops, "adam_" + name + "_b", lf=(0, 1), prev=part[name])
        put(name, *([jnp.swapaxes(t, 2, 3) for t in res] if key != 'd' else res), P[name].shape)

    small = {k: jnp.stack([small_l[l][k] for l in range(DEPTH)]) for k in small_l[0]}
    small['rel_bias'] = relbias_grad(dbiases)
    small['ada_b'] = jnp.stack([jnp.stack(dmod[l]).reshape(9 * D) for l in range(DEPTH)])
    small['ln_g'] = jnp.stack([jnp.stack(dlng[l]) for l in range(DEPTH)])
    small['ln_b'] = jnp.stack([jnp.stack(dlnb[l]) for l in range(DEPTH)])
    (small_all,) = _exchange(Gather([_pack([small[k] for k in SMALL])]), "gather_small")
    gsum = dict(zip(SMALL, _unpack(sum_sources(small_all, "sum_small"), [SMALL_FULL_SHAPES[k] for k in SMALL])))
    off = 256
    dmod_all = small_all.reshape(NDEV, -1)[:, off:off + DEPTH * 9 * D].reshape(NDEV, DEPTH, 9 * D)
    dmod_cols = jnp.transpose(lax.dynamic_slice_in_dim(dmod_all, me * 1152, 1152, axis=2), (1, 0, 2))
    g_ada_w = ada_bwd(c_all, dmod_cols)

    put('ada_w', g_ada_w, *adam_plain(g_ada_w, *wmv('ada_w'), 256, "adam_ada_w")[0], P['ada_w'].shape)

    for k in ('ln_g', 'ln_b'):
        gsum[k] = lax.dynamic_slice_in_dim(gsum[k], me * 128, 128, axis=2)
    swaps = {'rel_bias': (0, 1), 'ln_g': (0, 1), 'ln_b': (0, 1), 'ssm_b_re': (2, 3), 'ssm_b_im': (2, 3)}
    view = lambda k, t: jnp.swapaxes(t, *swaps[k]) if k in swaps else t
    ds_, m2s, v2s = adam_native(*[[view(k, src(k)) for k in SMALL] for src in
                                  (lambda k: gsum[k], lambda k: P[k], lambda k: P['m_' + k], lambda k: P['v_' + k])],
                                "adam_small")
    for k, d, m2, v2 in zip(SMALL, ds_, m2s, v2s):
        put(k, gsum[k], view(k, d), view(k, m2), view(k, v2), P[k].shape)

    res = [loss, grad_x]
    for pre in ('grad_', 'delta_', 'new_m_', 'new_v_'):
        res += [out[pre + k] for k in WEIGHTS]
    return tuple(res)


def kernel(x, c, rel_bias, ada_w, ada_b, ln_g, ln_b, ffn_w_gate, ffn_w_up, ffn_w_down, w_in, w_out, ssm_a_re, ssm_a_im, ssm_log_dt, ssm_b_re, ssm_b_im, ssm_c_re, ssm_c_im, ssm_d, glu_w, glu_b, pool_w, pool_scale, loss_target, m_rel_bias, m_ada_w, m_ada_b, m_ln_g, m_ln_b, m_ffn_w_gate, m_ffn_w_up, m_ffn_w_down, m_w_in, m_w_out, m_ssm_a_re, m_ssm_a_im, m_ssm_log_dt, m_ssm_b_re, m_ssm_b_im, m_ssm_c_re, m_ssm_c_im, m_ssm_d, m_glu_w, m_glu_b, m_pool_w, m_pool_scale, v_rel_bias, v_ada_w, v_ada_b, v_ln_g, v_ln_b, v_ffn_w_gate, v_ffn_w_up, v_ffn_w_down, v_w_in, v_w_out, v_ssm_a_re, v_ssm_a_im, v_ssm_log_dt, v_ssm_b_re, v_ssm_b_im, v_ssm_c_re, v_ssm_c_im, v_ssm_d, v_glu_w, v_glu_b, v_pool_w, v_pool_scale):
    return _step(dict(locals()))
```

```python
import functools
import math

import numpy as np
import jax
import jax.numpy as jnp
from jax import lax
from jax.experimental import pallas as pl
from jax.experimental.pallas import tpu as pltpu

F32 = jnp.float32
BF16 = jnp.bfloat16
MXU = jnp.bfloat16

S = 2048
D = 1024
NDEV = 8
DEPTH = 2
D_ATT, D_SSM, D_POOL, D_IN = 512, 256, 256, 2048
N_HEADS = 8
FB = 352
FBP = 384
QB = 128
PATTERNS = ((128, 1), (512, 4), (2048, 16))
POOL_WINDOWS = (2, 4, 8, 16)
N_BUCKETS, MAX_DISTANCE = 32, 2048
ALPHA = (2 * DEPTH) ** 0.25
LN_EPS = 1e-5
NEG = -1e30
GATHER_US_PER_BYTE = 43e-6
SCATTER_US_PER_BYTE = 21.6e-6
LR, B1, B2, EPS, WD, STEP = 0.001, 0.9, 0.999, 1e-08, 0.01, 10

TM = 256
TMM = 512
MIB = 1024 * 1024


def _cp(vmem_mib, sem=None):
    kw = dict(vmem_limit_bytes=vmem_mib * MIB)
    if sem is not None:
        kw["dimension_semantics"] = sem
    return pltpu.CompilerParams(**kw)


def _sds(shape, dtype):
    return jax.ShapeDtypeStruct(shape, dtype)


def _mm(a, b):
    return jnp.dot(a.astype(MXU), b.astype(MXU), preferred_element_type=F32)


def _mm_nt(a, b):
    return lax.dot_general(a.astype(MXU), b.astype(MXU), (((1,), (1,)), ((), ())), preferred_element_type=F32)


def _mm_tn(a, b):
    return lax.dot_general(a.astype(MXU), b.astype(MXU), (((0,), (0,)), ((), ())), preferred_element_type=F32)


def _ln_stats(x):
    mu = jnp.mean(x, axis=-1, keepdims=True)
    xc = x - mu
    var = jnp.mean(xc * xc, axis=-1, keepdims=True)
    rstd = lax.rsqrt(var + LN_EPS)
    return xc * rstd, rstd


def _ln_bwd(dn, n, rstd):
    return rstd * (dn - jnp.mean(dn, axis=-1, keepdims=True) - n * jnp.mean(dn * n, axis=-1, keepdims=True))


def _me():
    return 4 * lax.axis_index("x") + 2 * lax.axis_index("y") + lax.axis_index("c")


ANY = pl.BlockSpec(memory_space=pl.ANY)
PIN_BYTES = 1 << 19


def _pallas_call(*a, **k):
    big = lambda o: math.prod(o.shape) * o.dtype.itemsize >= PIN_BYTES
    pin = lambda o: pltpu.HBM(o.shape, o.dtype) if isinstance(o, jax.ShapeDtypeStruct) and big(o) else o
    osh = k["out_shape"]
    k["out_shape"] = tuple(pin(o) for o in osh) if isinstance(osh, (tuple, list)) else pin(osh)
    fn = pl.pallas_call(*a, **k)

    def run(*args):
        return fn(*[pltpu.with_memory_space_constraint(x, pltpu.HBM) if big(x) else x for x in args])
    return run


class Gather:
    def __init__(self, srcs):
        self.srcs = list(srcs)
        self.n = len(self.srcs)
        self.bufs = []
        self.out_shapes = [_sds((NDEV,) + a.shape, a.dtype) for a in self.srcs]
        self.sems = [pltpu.SemaphoreType.DMA((7 * self.n,)), pltpu.SemaphoreType.DMA((7 * self.n,)),
                     pltpu.SemaphoreType.DMA((self.n,))]

    def _parts(self, srcs, outs, sems):
        send_sems, recv_sems, loc_sems = sems
        x, y, c = lax.axis_index("x"), lax.axis_index("y"), lax.axis_index("c")
        me, sib = (x, y, c), (x, y, 1 - c)
        chips = [(1 - x, y), (x, 1 - y), (1 - x, 1 - y)]
        slot = lambda d: 4 * d[0] + 2 * d[1] + d[2]

        def copy(a, k, block, to, src=None):
            dst = outs[a].at[slot(block)]
            return pltpu.make_async_remote_copy(
                src_ref=dst if src is None else src, dst_ref=dst,
                send_sem=send_sems.at[7 * a + k], recv_sem=recv_sems.at[7 * a + k],
                device_id=to, device_id_type=pl.DeviceIdType.MESH)

        local = [pltpu.make_async_copy(srcs[a], outs[a].at[slot(me)], loc_sems.at[a]) for a in range(self.n)]
        return me, sib, chips, c, copy, local

    def start(self, srcs, bufs, outs, sems):
        me, sib, chips, c, copy, local = self._parts(srcs, outs, sems)
        for a in range(self.n):
            local[a].start()
            copy(a, 0, me, sib, src=srcs[a]).start()
            for j, chip in enumerate(chips):
                copy(a, 1 + j, me, (*chip, c), src=srcs[a]).start()

    def finish(self, srcs, bufs, outs, sems):
        me, sib, chips, c, copy, local = self._parts(srcs, outs, sems)
        for a in range(self.n):
            for j, chip in enumerate(chips):
                copy(a, 1 + j, (*chip, c), me).wait_recv()
                copy(a, 4 + j, (*chip, c), sib).start()
        for a in range(self.n):
            copy(a, 0, sib, me).wait_recv()
            copy(a, 0, me, sib, src=srcs[a]).wait_send()
            for j, chip in enumerate(chips):
                copy(a, 4 + j, (*chip, 1 - c), me).wait_recv()
                copy(a, 1 + j, me, (*chip, c), src=srcs[a]).wait_send()
                copy(a, 4 + j, (*chip, c), sib).wait_send()
            local[a].wait()


class Scatter:
    def __init__(self, items, bufs):
        self.items = list(items)
        self.keys = list(dict.fromkeys(key for _, key, _, _ in self.items))
        self.srcs = [src for src, _, _, _ in self.items]
        self.bufs = [bufs[key] for key in self.keys]
        self.n = len(self.srcs)
        self.out_shapes = [_sds(b.shape, b.dtype) for b in self.bufs]
        pairs = [(a, k) for a, (_, _, _, ks) in enumerate(self.items) for k in ks]
        self.remote_pairs = [p for p in pairs if p[1] != 0]
        self.local_pairs = [p for p in pairs if p[1] == 0]
        self.sems = [pltpu.SemaphoreType.DMA((max(len(self.remote_pairs), 1),)),
                     pltpu.SemaphoreType.DMA((max(len(self.remote_pairs), 1),)),
                     pltpu.SemaphoreType.DMA((max(len(self.local_pairs), 1),))]

    def _copies(self, srcs, outs, sems):
        send_sems, recv_sems, loc_sems = sems
        me = _me()

        def dst(a, slot):
            _, key, index, _ = self.items[a]
            return outs[self.keys.index(key)].at[(slot,) + tuple(index)]

        def remote(n, slot):
            a, k = self.remote_pairs[n]
            t = me ^ k
            return pltpu.make_async_remote_copy(
                src_ref=srcs[a].at[t], dst_ref=dst(a, slot), send_sem=send_sems.at[n], recv_sem=recv_sems.at[n],
                device_id=(t // 4, (t // 2) % 2, t % 2), device_id_type=pl.DeviceIdType.MESH)

        local = [pltpu.make_async_copy(srcs[a].at[me], dst(a, me), loc_sems.at[n])
                 for n, (a, _) in enumerate(self.local_pairs)]
        return me, remote, local

    def start(self, srcs, bufs, outs, sems):
        me, remote, local = self._copies(srcs, outs, sems)
        for cp in local:
            cp.start()
        for n in range(len(self.remote_pairs)):
            remote(n, me).start()

    def finish(self, srcs, bufs, outs, sems):
        me, remote, local = self._copies(srcs, outs, sems)
        for n, (_, k) in enumerate(self.remote_pairs):
            remote(n, me ^ k).wait()
        for cp in local:
            cp.wait()


def _call(body, *, name, grid, in_specs, out_specs, out_shape, args, scratch=(), cp=None, ride=None):
    out_specs, out_shape, scratch = list(out_specs), list(out_shape), list(scratch)
    if ride is None:
        outs = _pallas_call(body, name=name, grid=grid, in_specs=list(in_specs), out_specs=tuple(out_specs),
                              out_shape=tuple(out_shape), scratch_shapes=scratch, compiler_params=cp)(*args)
        return list(outs), []
    nin, nout, nscr, n, nb, no = len(in_specs), len(out_specs), len(scratch), ride.n, len(ride.bufs), len(ride.out_shapes)
    steps = list(grid)

    def wrapped(*refs):
        h_in, r_src, r_buf = refs[:nin], refs[nin:nin + n], refs[nin + n:nin + n + nb]
        o0 = nin + n + nb
        h_out, r_out = refs[o0:o0 + nout], refs[o0 + nout:o0 + nout + no]
        s0 = o0 + nout + no
        h_scr, sems = refs[s0:s0 + nscr], refs[s0 + nscr:]
        ids = [pl.program_id(a) for a in range(len(steps))]
        first = functools.reduce(jnp.logical_and, [i == 0 for i in ids])
        last = functools.reduce(jnp.logical_and, [i == s - 1 for i, s in zip(ids, steps)])

        @pl.when(first)
        def _():
            ride.start(r_src, r_buf, r_out, sems)

        body(*h_in, *h_out, *h_scr)

        @pl.when(last)
        def _():
            ride.finish(r_src, r_buf, r_out, sems)

    aliases = {nin + n + k: nout + k for k in range(nb)}
    outs = _pallas_call(
        wrapped, name=name, grid=grid, in_specs=list(in_specs) + [ANY] * (n + nb),
        out_specs=tuple(out_specs + [ANY] * no), out_shape=tuple(out_shape + ride.out_shapes),
        scratch_shapes=scratch + ride.sems, input_output_aliases=aliases, compiler_params=cp,
    )(*args, *ride.srcs, *ride.bufs)
    return list(outs[:nout]), list(outs[nout:])


def _exchange(ride, name):
    def body(dummy_ref, o_ref):
        o_ref[...] = dummy_ref[...]

    one = pl.BlockSpec((8, 128), lambda i: (0, 0))
    _, outs = _call(body, name=name, grid=(1,), in_specs=[one], out_specs=[one], out_shape=[_sds((8, 128), F32)],
                    args=(jnp.zeros((8, 128), F32),), ride=ride)
    return outs


def _row_spec(cols, tm=TM):
    return pl.BlockSpec((tm, cols), lambda i: (i, 0))


def _full_spec(shape):
    nd = len(shape)
    return pl.BlockSpec(shape, lambda i: (0,) * nd)


def ln_mod_fwd(x, mod, sub, name):
    def body(x_ref, mod_ref, h_ref):
        n, _ = _ln_stats(x_ref[...])
        shift = mod_ref[3 * sub:3 * sub + 1, :]
        scale = mod_ref[3 * sub + 1:3 * sub + 2, :]
        h_ref[...] = (n * (1.0 + scale) + shift).astype(MXU)

    return _pallas_call(
        body, name=name, grid=(S // TM,),
        in_specs=[_row_spec(D), _full_spec((9, D))], out_specs=_row_spec(D),
        out_shape=_sds((S, D), MXU), compiler_params=_cp(32, ("arbitrary",)))(x, mod)


def res_ln_fwd(x, f, mod, sub, lng, lnb, w, name, nxt=None):
    def body(x_ref, f_ref, mod_ref, g_ref, b_ref, *rest):
        gate = mod_ref[3 * sub + 2:3 * sub + 3, :]
        r = ALPHA * x_ref[...] + (w * gate) * f_ref[...]
        n, _ = _ln_stats(r)
        xo = n * g_ref[sub:sub + 1, :] + b_ref[sub:sub + 1, :]
        rest[-1 if nxt is None else -2][...] = xo
        if nxt is not None:
            nmod_ref, h_ref = rest[0], rest[-1]
            n2, _ = _ln_stats(xo)
            s2 = nxt[1]
            h_ref[...] = (n2 * (1.0 + nmod_ref[3 * s2 + 1:3 * s2 + 2, :]) + nmod_ref[3 * s2:3 * s2 + 1, :]).astype(MXU)

    more = nxt is not None
    return _pallas_call(
        body, name=name, grid=(S // TM,),
        in_specs=[_row_spec(D), _row_spec(D), _full_spec((9, D)), _full_spec((3, D)), _full_spec((3, D))] + [_full_spec((9, D))] * more,
        out_specs=(_row_spec(D),) + (_row_spec(D),) * more, out_shape=(_sds((S, D), F32),) + (_sds((S, D), MXU),) * more,
        compiler_params=_cp(32, ("arbitrary",)))(x, f, mod, lng, lnb, *([nxt[0]] if more else []))


def res_ln_bwd(x, f, mod, sub, lng, dxo, w, name, ride=None):
    def body(x_ref, f_ref, mod_ref, g_ref, dxo_ref, dxa_ref, df_ref, sums_ref):
        i = pl.program_id(0)
        gate = mod_ref[3 * sub + 2:3 * sub + 3, :]
        fv = f_ref[...]
        r = ALPHA * x_ref[...] + (w * gate) * fv
        n, rstd = _ln_stats(r)
        dxo = dxo_ref[...]
        dr = _ln_bwd(dxo * g_ref[sub:sub + 1, :], n, rstd)
        dxa_ref[...] = ALPHA * dr
        df_ref[...] = ((w * gate) * dr).astype(MXU)
        part = jnp.concatenate([
            jnp.sum(dxo * n, axis=0, keepdims=True),
            jnp.sum(dxo, axis=0, keepdims=True),
            jnp.sum(dr * fv, axis=0, keepdims=True) * w,
            jnp.zeros((5, D), F32)], axis=0)

        @pl.when(i == 0)
        def _():
            sums_ref[...] = part

        @pl.when(i > 0)
        def _():
            sums_ref[...] += part

    return _call(
        body, name=name, grid=(S // TM,),
        in_specs=[_row_spec(D), _row_spec(D), _full_spec((9, D)), _full_spec((3, D)), _row_spec(D)],
        out_specs=(_row_spec(D), _row_spec(D), _full_spec((8, D))),
        out_shape=(_sds((S, D), F32), _sds((S, D), MXU), _sds((8, D), F32)),
        cp=_cp(32, ("arbitrary",)), args=(x, f, mod, lng, dxo), ride=ride)


def ln_mod_bwd(x, dh, mod, sub, dxa, name, ride=None):
    def body(x_ref, dh_ref, mod_ref, dxa_ref, dx_ref, sums_ref):
        i = pl.program_id(0)
        scale = mod_ref[3 * sub + 1:3 * sub + 2, :]
        n, rstd = _ln_stats(x_ref[...])
        dh = dh_ref[...]
        dx_ref[...] = dxa_ref[...] + _ln_bwd(dh * (1.0 + scale), n, rstd)
        part = jnp.concatenate([
            jnp.sum(dh, axis=0, keepdims=True),
            jnp.sum(dh * n, axis=0, keepdims=True),
            jnp.zeros((6, D), F32)], axis=0)

        @pl.when(i == 0)
        def _():
            sums_ref[...] = part

        @pl.when(i > 0)
        def _():
            sums_ref[...] += part

    return _call(
        body, name=name, grid=(S // TM,),
        in_specs=[_row_spec(D), _row_spec(D), _full_spec((9, D)), _row_spec(D)],
        out_specs=(_row_spec(D), _full_spec((8, D))),
        out_shape=(_sds((S, D), F32), _sds((8, D), F32)),
        cp=_cp(32, ("arbitrary",)), args=(x, dh, mod, dxa), ride=ride)


def loss_fwd_bwd(y, target, name):
    def body(y_ref, t_ref, l_ref, dy_ref):
        i = pl.program_id(0)
        e = y_ref[...] - t_ref[...]
        dy_ref[...] = e * (1.0 / D)
        part = jnp.zeros((8, 128), F32) + (0.5 / D) * jnp.sum(e * e)

        @pl.when(i == 0)
        def _():
            l_ref[...] = part

        @pl.when(i > 0)
        def _():
            l_ref[...] += part

    return _pallas_call(
        body, name=name, grid=(S // TM,),
        in_specs=[_row_spec(D), _row_spec(D)], out_specs=(_full_spec((8, 128)), _row_spec(D)),
        out_shape=(_sds((8, 128), F32), _sds((S, D), F32)),
        compiler_params=_cp(32, ("arbitrary",)))(y, target)


HB = 2 * FBP
NHB = NDEV * FBP // HB
TMB = 1024


def _wrows(buffers=2):
    return pl.BlockSpec((HB, D), lambda j, i: (j, 0), pipeline_mode=pl.Buffered(buffers))


def _resident(shape):
    return pl.BlockSpec(shape, lambda j, i: (0, 0), pipeline_mode=pl.Buffered(1))


def ffn_fwd(h, wgt, wut, wd, name, ride=None):
    def body(h_ref, wg_ref, wu_ref, wd_ref, g_ref, u_ref, f_ref):
        j, i = pl.program_id(0), pl.program_id(1)
        hv = h_ref[...]
        g = _mm_nt(hv, wg_ref[...])
        u = _mm_nt(hv, wu_ref[...])
        g_ref[...] = g.astype(MXU)
        u_ref[...] = u.astype(MXU)
        a = g * jax.nn.sigmoid(g) * u
        part = _mm(a, wd_ref[...])
        rows = pl.ds(pl.multiple_of(i * TMB, TMB), TMB)

        @pl.when(j == 0)
        def _():
            f_ref[rows, :] = part

        @pl.when(j > 0)
        def _():
            f_ref[rows, :] += part

    gu = pl.BlockSpec((TMB, HB), lambda j, i: (i, j))
    return _call(
        body, name=name, grid=(NHB, S // TMB),
        in_specs=[pl.BlockSpec((TMB, D), lambda j, i: (i, 0)), _wrows(), _wrows(), _wrows()],
        out_specs=(gu, gu, _resident((S, D))),
        out_shape=(_sds((S, NDEV * FBP), MXU), _sds((S, NDEV * FBP), MXU), _sds((S, D), F32)),
        cp=_cp(52, ("arbitrary", "arbitrary")), args=(h, wgt, wut, wd), ride=ride)


def ffn_bwd(df, h, g, u, wgt, wut, wd, name, ride=None):
    ni = S // TMB

    def body(df_ref, h_ref, g_ref, u_ref, wg_ref, wu_ref, wd_ref, dwg_ref, dwu_ref, dwd_ref, dh_ref,
             ag_ref, au_ref, ad_ref):
        j, i = pl.program_id(0), pl.program_id(1)
        dfv, hv = df_ref[...], h_ref[...]
        gv, uv = g_ref[...].astype(F32), u_ref[...].astype(F32)
        da = _mm_nt(dfv, wd_ref[...])
        sg = jax.nn.sigmoid(gv)
        silu = gv * sg
        du = da * silu
        dg = da * uv * (sg * (1.0 + gv * (1.0 - sg)))
        p_d = _mm_tn(silu * uv, dfv)
        p_g = _mm_tn(dg, hv)
        p_u = _mm_tn(du, hv)

        @pl.when(i == 0)
        def _():
            ad_ref[...] = p_d
            ag_ref[...] = p_g
            au_ref[...] = p_u

        @pl.when(i > 0)
        def _():
            ad_ref[...] += p_d
            ag_ref[...] += p_g
            au_ref[...] += p_u

        @pl.when(i == ni - 1)
        def _():
            dwd_ref[...] = ad_ref[...].astype(BF16)
            dwg_ref[...] = ag_ref[...].astype(BF16)
            dwu_ref[...] = au_ref[...].astype(BF16)

        part = _mm(dg, wg_ref[...]) + _mm(du, wu_ref[...])
        rows = pl.ds(pl.multiple_of(i * TMB, TMB), TMB)

        @pl.when(j == 0)
        def _():
            dh_ref[rows, :] = part

        @pl.when(j > 0)
        def _():
            dh_ref[rows, :] += part

    gu = pl.BlockSpec((TMB, HB), lambda j, i: (i, j))
    rowt = pl.BlockSpec((TMB, D), lambda j, i: (i, 0))
    return _call(
        body, name=name, grid=(NHB, ni),
        in_specs=[rowt, rowt, gu, gu, _wrows(1), _wrows(1), _wrows(1)],
        out_specs=(_wrows(1), _wrows(1), _wrows(1), _resident((S, D))),
        out_shape=(_sds((NDEV * FBP, D), BF16), _sds((NDEV * FBP, D), BF16), _sds((NDEV * FBP, D), BF16), _sds((S, D), F32)),
        scratch=[pltpu.VMEM((HB, D), F32), pltpu.VMEM((HB, D), F32), pltpu.VMEM((HB, D), F32)],
        cp=_cp(60, ("arbitrary", "arbitrary")), args=(df, h, g, u, wgt, wut, wd), ride=ride)


def win_fwd(h, win, name, ride=None):
    def body(h_ref, w_ref, z_ref):
        hv = h_ref[...]
        for j in range(NDEV):
            z_ref[:, 256 * j:256 * (j + 1)] = _mm(hv, w_ref[j])

    return _call(
        body, name=name, grid=(S // TMM,),
        in_specs=[_row_spec(D, TMM), _full_spec((NDEV, D, 256))],
        out_specs=[_row_spec(D_IN, TMM)], out_shape=[_sds((S, D_IN), F32)],
        cp=_cp(40, ("arbitrary",)), args=(h, win), ride=ride)


def win_bwd(dparts, h, win, name, ride=None):
    ni = S // TMM

    def body(dq_ref, dk_ref, dv_ref, dus_ref, dup_ref, h_ref, w_ref, dh_ref, dw_ref, acc_ref):
        i = pl.program_id(0)
        hv = h_ref[...]
        cols = [dq_ref[:, 0:256], dq_ref[:, 256:512], dk_ref[:, 0:256], dk_ref[:, 256:512],
                dv_ref[:, 0:256], dv_ref[:, 256:512], dus_ref[...], dup_ref[...]]
        dh = jnp.zeros((TMM, D), F32)
        for j in range(NDEV):
            dz = cols[j].astype(MXU)
            dh = dh + _mm_nt(dz, w_ref[j])
            p = _mm_tn(hv, dz)

            @pl.when(i == 0)
            def _():
                acc_ref[j] = p

            @pl.when(i > 0)
            def _():
                acc_ref[j] += p

        dh_ref[...] = dh

        @pl.when(i == ni - 1)
        def _():
            dw_ref[...] = acc_ref[...].astype(BF16)

    return _call(
        body, name=name, grid=(ni,),
        in_specs=[_row_spec(512, TMM), _row_spec(512, TMM), _row_spec(512, TMM), _row_spec(256, TMM), _row_spec(256, TMM),
                  _row_spec(D, TMM), _full_spec((NDEV, D, 256))],
        out_specs=(_row_spec(D, TMM), _full_spec((NDEV, D, 256))),
        out_shape=(_sds((S, D), F32), _sds((NDEV, D, 256), BF16)),
        scratch=[pltpu.VMEM((NDEV, D, 256), F32)],
        cp=_cp(48, ("arbitrary",)), args=(*dparts, h, win), ride=ride)


def wout_fwd(ya, ys, yp, wout, name, ride=None):
    def body(ya_ref, ys_ref, yp_ref, w_ref, o_ref):
        w = w_ref[...].reshape(D, D)
        o_ref[...] = _mm(ya_ref[...], w[0:512]) + _mm(ys_ref[...], w[512:768]) + _mm(yp_ref[...], w[768:1024])

    return _call(
        body, name=name, grid=(S // TMM,),
        in_specs=[_row_spec(512, TMM), _row_spec(256, TMM), _row_spec(256, TMM), _full_spec((NDEV, 128, D))],
        out_specs=[_row_spec(D, TMM)], out_shape=[_sds((S, D), F32)],
        cp=_cp(40, ("arbitrary",)), args=(ya, ys, yp, wout), ride=ride)


def wout_bwd(do, ya, ys, yp, wout, name, ride=None):
    ni = S // TMM

    def body(do_ref, ya_ref, ys_ref, yp_ref, w_ref, dya_ref, dys_ref, dyp_ref, dw_ref, acc_ref):
        i = pl.program_id(0)
        w = w_ref[...].reshape(D, D)
        dov = do_ref[...]
        dya_ref[...] = _mm_nt(dov, w[0:512])
        dys_ref[...] = _mm_nt(dov, w[512:768])
        dyp_ref[...] = _mm_nt(dov, w[768:1024])
        parts = [(0, 512, _mm_tn(ya_ref[...], dov)), (512, 768, _mm_tn(ys_ref[...], dov)),
                 (768, 1024, _mm_tn(yp_ref[...], dov))]
        for lo, hi, p in parts:
            @pl.when(i == 0)
            def _():
                acc_ref[lo:hi, :] = p

            @pl.when(i > 0)
            def _():
                acc_ref[lo:hi, :] += p

        @pl.when(i == ni - 1)
        def _():
            dw_ref[...] = acc_ref[...].astype(BF16).reshape(NDEV, 128, D)

    return _call(
        body, name=name, grid=(ni,),
        in_specs=[_row_spec(D, TMM), _row_spec(512, TMM), _row_spec(256, TMM), _row_spec(256, TMM),
                  _full_spec((NDEV, 128, D))],
        out_specs=(_row_spec(512, TMM), _row_spec(256, TMM), _row_spec(256, TMM), _full_spec((NDEV, 128, D))),
        out_shape=(_sds((S, 512), F32), _sds((S, 256), F32), _sds((S, 256), F32), _sds((NDEV, 128, D), BF16)),
        scratch=[pltpu.VMEM((D, D), F32)],
        cp=_cp(40, ("arbitrary",)), args=(do, ya, ys, yp, wout), ride=ride)


def _t5_bucket(dist):
    max_exact = N_BUCKETS // 2
    d = np.maximum(dist, 1).astype(np.float32)
    large = max_exact + (np.log(d / max_exact) / math.log(MAX_DISTANCE / max_exact)
                         * (N_BUCKETS - max_exact)).astype(np.int32)
    large = np.minimum(large, N_BUCKETS - 1)
    return np.where(dist < max_exact, dist, large).astype(np.int32)


def _att_static():
    i = np.arange(QB)[:, None]
    j = np.arange(2 * QB)[None, :]
    r = i + QB - j
    buckets, bands = [], []
    for window, dil in PATTERNS:
        bands.append((r >= 0) & (r <= window // dil))
        buckets.append(_t5_bucket(np.clip(r, 0, None) * dil))
    return np.stack(buckets), np.stack(bands), np.broadcast_to(j >= QB, (QB, 2 * QB))


def att_bias(rel_bias):
    m = np.arange(2 * QB)
    rows = []
    for window, dil in PATTERNS:
        r = QB - m
        ok = (r >= 0) & (r <= window // dil)
        b = rel_bias[_t5_bucket(np.clip(r, 0, None) * dil)]
        rows.append(jnp.where(ok[:, None], b, NEG).T)
    return jnp.broadcast_to(jnp.stack(rows)[:, :, None, :], (3, N_HEADS, 8, 2 * QB))


def _bias_tiles(t_ref, tiles):
    col = lax.broadcasted_iota(jnp.int32, (QB, 2 * QB), 1)
    for p in range(3):
        for hh in range(2):
            t = pltpu.roll(jnp.broadcast_to(t_ref[p, hh, 0:1, :], (QB, 2 * QB)), 0, 1, stride=1, stride_axis=0)
            tiles[p, hh, 0] = t
            tiles[p, hh, 1] = jnp.where(col >= QB, t, NEG)


def _permute_in(dst_ref, src_ref, d, scale=None, pad=QB):
    L = S // d
    for r in range(d):
        v = src_ref[pl.ds(r, L, stride=d), :] if d > 1 else src_ref[...]
        if scale is not None:
            v = v * scale
        dst_ref[pad + r * L:pad + (r + 1) * L, :] = v.astype(dst_ref.dtype)


def att_fwd(z, bias, name, ride=None):
    def body(q_ref, k_ref, v_ref, t_ref, y_ref, l_ref, qs, ks, vs, o_perm, l_perm, o_nat, l_nat, b_ref):
        _bias_tiles(t_ref, b_ref)
        zero_pad = jnp.zeros((QB, 128), MXU)
        ks[0:QB, :] = zero_pad
        vs[0:QB, :] = zero_pad
        lane = lax.broadcasted_iota(jnp.int32, (QB, 128), 1)
        for p, (_, d) in enumerate(PATTERNS):
            L = S // d
            nb = L // QB
            _permute_in(qs, q_ref, d, scale=0.125, pad=0)
            _permute_in(ks, k_ref, d)
            _permute_in(vs, v_ref, d)

            def blk(b, carry):
                r0 = pl.multiple_of(b * QB, QB)
                q = qs[pl.ds(r0, QB), :]
                kb = ks[pl.ds(r0, 2 * QB), :]
                vb = vs[pl.ds(r0, 2 * QB), :]
                first = ((b % nb) == 0).astype(jnp.int32)
                res = []
                for hh in range(2):
                    sel = (lane < 64) if hh == 0 else (lane >= 64)
                    qm = jnp.where(sel, q, jnp.zeros_like(q))
                    s = _mm_nt(qm, kb) + b_ref[p, hh, first]
                    m = jnp.max(s, axis=1, keepdims=True)
                    pe = jnp.exp(s - m)
                    den = jnp.sum(pe, axis=1, keepdims=True)
                    res.append((_mm(pe, vb) / den, m + jnp.log(den)))
                o_perm[pl.ds(r0, QB), :] = jnp.where(lane < 64, res[0][0], res[1][0])
                l_perm[pl.ds(r0, QB), :] = jnp.where(lane < 64, res[0][1], res[1][1])
                return carry

            lax.fori_loop(0, S // QB, blk, 0, unroll=8)
            for r in range(d):
                if d > 1:
                    o_nat[p, pl.ds(r, L, stride=d), :] = o_perm[r * L:(r + 1) * L, :]
                    l_nat[p, pl.ds(r, L, stride=d), :] = l_perm[r * L:(r + 1) * L, :]
                else:
                    o_nat[p] = o_perm[...]
                    l_nat[p] = l_perm[...]
        l0, l1, l2 = l_nat[0], l_nat[1], l_nat[2]
        m = jnp.maximum(jnp.maximum(l0, l1), l2)
        e0, e1, e2 = jnp.exp(l0 - m), jnp.exp(l1 - m), jnp.exp(l2 - m)
        den = e0 + e1 + e2
        y_ref[...] = (e0 * o_nat[0] + e1 * o_nat[1] + e2 * o_nat[2]) / den
        l_ref[...] = m + jnp.log(den)

    col = lambda c0: pl.BlockSpec((S, 128), lambda hp: (0, c0 + hp))
    return _call(
        body, name=name, grid=(N_HEADS // 2,),
        in_specs=[col(0), col(4), col(8), pl.BlockSpec((3, 2, 8, 2 * QB), lambda hp: (0, hp, 0, 0))],
        out_specs=(col(0), col(0)),
        out_shape=(_sds((S, D_ATT), F32), _sds((S, D_ATT), F32)),
        scratch=[pltpu.VMEM((S, 128), MXU), pltpu.VMEM((S + QB, 128), MXU), pltpu.VMEM((S + QB, 128), MXU),
                 pltpu.VMEM((S, 128), F32), pltpu.VMEM((S, 128), F32),
                 pltpu.VMEM((3, S, 128), F32), pltpu.VMEM((3, S, 128), F32),
                 pltpu.VMEM((3, 2, 2, QB, 2 * QB), F32)],
        cp=_cp(40, ("arbitrary",)), args=(z, z, z, bias), ride=ride)


def att_bwd(z, bias, y, lse, dy, name, ride=None):
    def body(q_ref, k_ref, v_ref, t_ref, y_ref, l_ref, dy_ref, dq_ref, dk_ref, dv_ref, db_ref,
             qs, ks, vs, dys, ls, dds, dn_nat, dq_perm, dk_perm, dv_perm, b_ref):
        _bias_tiles(t_ref, b_ref)
        zero_pad = jnp.zeros((QB, 128), MXU)
        ks[0:QB, :] = zero_pad
        vs[0:QB, :] = zero_pad
        lane = lax.broadcasted_iota(jnp.int32, (QB, 128), 1)
        lane_s = lax.broadcasted_iota(jnp.int32, (S, 128), 1)
        t = dy_ref[...] * y_ref[...]
        sa = jnp.sum(jnp.where(lane_s < 64, t, 0.0), axis=1, keepdims=True)
        sb = jnp.sum(jnp.where(lane_s >= 64, t, 0.0), axis=1, keepdims=True)
        dn_nat[...] = jnp.where(lane_s < 64, sa, sb)
        dq_ref[...] = jnp.zeros((S, 128), F32)
        dk_ref[...] = jnp.zeros((S, 128), F32)
        dv_ref[...] = jnp.zeros((S, 128), F32)
        db_ref[...] = jnp.zeros((3, 2, QB, 2 * QB), F32)
        for p, (_, d) in enumerate(PATTERNS):
            L = S // d
            nb = L // QB
            _permute_in(qs, q_ref, d, scale=0.125, pad=0)
            _permute_in(ks, k_ref, d)
            _permute_in(vs, v_ref, d)
            _permute_in(dys, dy_ref, d, pad=0)
            _permute_in(ls, l_ref, d, pad=0)
            _permute_in(dds, dn_nat, d, pad=0)
            dk_perm[...] = jnp.zeros((S + QB, 128), F32)
            dv_perm[...] = jnp.zeros((S + QB, 128), F32)

            def blk(b, carry):
                r0 = pl.multiple_of(b * QB, QB)
                q = qs[pl.ds(r0, QB), :]
                kb = ks[pl.ds(r0, 2 * QB), :]
                vb = vs[pl.ds(r0, 2 * QB), :]
                dyb = dys[pl.ds(r0, QB), :]
                lb = ls[pl.ds(r0, QB), :]
                db = dds[pl.ds(r0, QB), :]
                first = ((b % nb) == 0).astype(jnp.int32)
                dqs = []
                dkb = jnp.zeros((2 * QB, 128), F32)
                dvb = jnp.zeros((2 * QB, 128), F32)
                for hh in range(2):
                    sel = (lane < 64) if hh == 0 else (lane >= 64)
                    c0 = 64 * hh
                    qm = jnp.where(sel, q, jnp.zeros_like(q))
                    dym = jnp.where(sel, dyb, jnp.zeros_like(dyb))
                    s = _mm_nt(qm, kb) + b_ref[p, hh, first]
                    pr = jnp.exp(s - lb[:, c0:c0 + 1])
                    dp = _mm_nt(dym, vb)
                    ds = pr * (dp - db[:, c0:c0 + 1])
                    db_ref[p, hh] += ds
                    dqs.append(_mm(ds, kb))
                    dkb = dkb + _mm_tn(ds, qm)
                    dvb = dvb + _mm_tn(pr, dym)
                dq_perm[pl.ds(r0, QB), :] = jnp.where(lane < 64, dqs[0], dqs[1])
                dk_perm[pl.ds(r0, 2 * QB), :] += dkb
                dv_perm[pl.ds(r0, 2 * QB), :] += dvb
                return carry

            lax.fori_loop(0, S // QB, blk, 0, unroll=4)
            for r in range(d):
                idx = pl.ds(r, L, stride=d) if d > 1 else pl.ds(0, S)
                dq_ref[idx, :] += dq_perm[r * L:(r + 1) * L, :] * 0.125
                dk_ref[idx, :] += dk_perm[QB + r * L:QB + (r + 1) * L, :]
                dv_ref[idx, :] += dv_perm[QB + r * L:QB + (r + 1) * L, :]

    col = lambda c0: pl.BlockSpec((S, 128), lambda hp: (0, c0 + hp))
    bspec = pl.BlockSpec((3, 2, 8, 2 * QB), lambda hp: (0, hp, 0, 0))
    return _call(
        body, name=name, grid=(N_HEADS // 2,),
        in_specs=[col(0), col(4), col(8), bspec, col(0), col(0), col(0)],
        out_specs=(col(0), col(0), col(0), pl.BlockSpec((3, 2, QB, 2 * QB), lambda hp: (0, hp, 0, 0))),
        out_shape=(_sds((S, D_ATT), F32), _sds((S, D_ATT), F32), _sds((S, D_ATT), F32),
                   _sds((3, N_HEADS, QB, 2 * QB), F32)),
        scratch=[pltpu.VMEM((S, 128), MXU), pltpu.VMEM((S + QB, 128), MXU), pltpu.VMEM((S + QB, 128), MXU),
                 pltpu.VMEM((S, 128), MXU), pltpu.VMEM((S, 128), F32), pltpu.VMEM((S, 128), F32),
                 pltpu.VMEM((S, 128), F32), pltpu.VMEM((S, 128), F32),
                 pltpu.VMEM((S + QB, 128), F32), pltpu.VMEM((S + QB, 128), F32),
                 pltpu.VMEM((3, 2, 2, QB, 2 * QB), F32)],
        cp=_cp(48, ("arbitrary",)), args=(z, z, z, bias, y, lse, dy), ride=ride)


def relbias_grad(dbiases):
    bucket, band, _ = _att_static()
    onehot = (bucket[:, None] == np.arange(N_BUCKETS)[None, :, None, None]) & band[:, None]
    onehot = jnp.asarray(onehot.reshape(3, N_BUCKETS, QB * 2 * QB), BF16)

    def body(db0_ref, db1_ref, oh_ref, o_ref):
        acc = jnp.zeros((N_HEADS, N_BUCKETS), F32)
        for p in range(3):
            acc = acc + lax.dot_general(db0_ref[p] + db1_ref[p], oh_ref[p].astype(F32), (((1,), (1,)), ((), ())),
                                        preferred_element_type=F32, precision=lax.Precision.HIGHEST)
        o_ref[...] = acc

    vm = pl.BlockSpec(memory_space=pltpu.VMEM)
    out = _pallas_call(body, name="relbias_grad", in_specs=[vm, vm, vm], out_specs=vm,
                         out_shape=_sds((N_HEADS, N_BUCKETS), F32), compiler_params=_cp(40))(
        *[d.reshape(3, N_HEADS, QB * 2 * QB) for d in dbiases], onehot)
    return out.T


def _panel(t_ref, ri, j):
    return t_ref[ri, pl.ds(j, S, stride=8), :]


def _gelu(x):
    c = math.sqrt(2.0 / math.pi)
    th = jnp.tanh(c * (x + 0.044715 * x * x * x))
    return 0.5 * x * (1.0 + th), th


def ssm_fwd(z, a, bre, bim, cre, cim, dsk, gluw, glub, name, ride=None):
    def body(u_ref, a_ref, bre_ref, bim_ref, cre_ref, cim_ref, d_ref, gw_ref, gb_ref, y_ref, yp_ref, st_hbm, st_ref):
        u = u_ref[...]
        for j in range(8):
            st_ref[0, pl.ds(j, S, stride=8), :] = _mm(u, bre_ref[:, 128 * j:128 * (j + 1)])
            st_ref[1, pl.ds(j, S, stride=8), :] = _mm(u, bim_ref[:, 128 * j:128 * (j + 1)])
        ar, ai = a_ref[0], a_ref[1]

        def step(t, c):
            re, im = c
            i = pl.multiple_of(t * 8, 8)
            nre = ar * re - ai * im + st_ref[0, pl.ds(i, 8), :]
            nim = ar * im + ai * re + st_ref[1, pl.ds(i, 8), :]
            st_ref[0, pl.ds(i, 8), :] = nre
            st_ref[1, pl.ds(i, 8), :] = nim
            return nre, nim

        zero = jnp.zeros((8, 128), F32)
        lax.fori_loop(0, S, step, (zero, zero), unroll=8)
        y = d_ref[...] * u
        for j in range(8):
            y = y + _mm(_panel(st_ref, 0, j), cre_ref[128 * j:128 * (j + 1), :])
            y = y - _mm(_panel(st_ref, 1, j), cim_ref[128 * j:128 * (j + 1), :])
        pltpu.sync_copy(st_ref, st_hbm)
        yp_ref[...] = y
        gl, _ = _gelu(y)
        tt = _mm(gl, gw_ref[...].reshape(D_SSM, D_SSM)) + gb_ref[...]
        y_ref[...] = y * jax.nn.sigmoid(tt)

    vm = lambda shape: pl.BlockSpec(shape, lambda i: (0,) * len(shape))
    return _call(
        body, name=name, grid=(1,),
        in_specs=[pl.BlockSpec((S, 256), lambda i: (0, 6)), vm((2, 8, 128)), vm((256, 1024)), vm((256, 1024)),
                  vm((1024, 256)), vm((1024, 256)), vm((1, 256)),
                  vm((NDEV, 32, 256)), vm((1, 256))],
        out_specs=(vm((S, 256)), vm((S, 256)), pl.BlockSpec(memory_space=pl.ANY)),
        out_shape=(_sds((S, 256), F32), _sds((S, 256), F32), _sds((2, S * 8, 128), F32)),
        scratch=[pltpu.VMEM((2, S * 8, 128), F32)],
        cp=_cp(40, ("arbitrary",)), args=(z, a, bre, bim, cre, cim, dsk, gluw, glub), ride=ride)


def ssm_bwd(dy, z, ypre, st, a, bre, bim, cre, cim, dsk, gluw, glub, name, ride=None):
    def body(dy_ref, u_ref, yp_ref, st_hbm, a_ref, bre_ref, bim_ref, cre_ref, cim_ref, d_ref, gw_ref, gb_ref,
             du_ref, dbre_ref, dbim_ref, dcre_ref, dcim_ref, da_ref, dd_ref, dgw_ref, dgb_ref, g_ref, st_ref):
        pltpu.sync_copy(st_hbm, st_ref)
        u = u_ref[...]
        y = yp_ref[...]
        dout = dy_ref[...]
        gw = gw_ref[...].reshape(D_SSM, D_SSM)
        gl, th = _gelu(y)
        sig = jax.nn.sigmoid(_mm(gl, gw) + gb_ref[...])
        dt = dout * y * sig * (1.0 - sig)
        dgw_ref[...] = _mm_tn(gl, dt)
        dgb_ref[...] = jnp.sum(dt, axis=0, keepdims=True)
        c = math.sqrt(2.0 / math.pi)
        dgelu = 0.5 * (1.0 + th) + 0.5 * y * (1.0 - th * th) * c * (1.0 + 3.0 * 0.044715 * y * y)
        dyv = dout * sig + _mm_nt(dt, gw) * dgelu
        dd_ref[...] = jnp.sum(dyv * u, axis=0, keepdims=True)
        for j in range(8):
            rows = slice(128 * j, 128 * (j + 1))
            g_ref[0, pl.ds(j, S, stride=8), :] = _mm_nt(dyv, cre_ref[rows, :])
            g_ref[1, pl.ds(j, S, stride=8), :] = -_mm_nt(dyv, cim_ref[rows, :])
            dcre_ref[rows, :] = _mm_tn(_panel(st_ref, 0, j), dyv)
            dcim_ref[rows, :] = -_mm_tn(_panel(st_ref, 1, j), dyv)
        ar, ai = a_ref[0], a_ref[1]

        def step(k, c4):
            gre, gim, dar, dai = c4
            i = pl.multiple_of((S - 1 - k) * 8, 8)
            nre = g_ref[0, pl.ds(i, 8), :] + ar * gre + ai * gim
            nim = g_ref[1, pl.ds(i, 8), :] + ar * gim - ai * gre
            g_ref[0, pl.ds(i, 8), :] = nre
            g_ref[1, pl.ds(i, 8), :] = nim
            sre = st_ref[0, pl.ds(i - 8, 8), :]
            sim = st_ref[1, pl.ds(i - 8, 8), :]
            return nre, nim, dar + nre * sre + nim * sim, dai + nim * sre - nre * sim

        zero = jnp.zeros((8, 128), F32)
        gre, gim, dar, dai = lax.fori_loop(0, S - 1, step, (zero, zero, zero, zero), unroll=8)
        g_ref[0, 0:8, :] = g_ref[0, 0:8, :] + ar * gre + ai * gim
        g_ref[1, 0:8, :] = g_ref[1, 0:8, :] + ar * gim - ai * gre
        da_ref[0] = dar
        da_ref[1] = dai
        du = dyv * d_ref[...]
        for j in range(8):
            cols = slice(128 * j, 128 * (j + 1))
            gr, gi = _panel(g_ref, 0, j), _panel(g_ref, 1, j)
            dbre_ref[:, cols] = _mm_tn(u, gr)
            dbim_ref[:, cols] = _mm_tn(u, gi)
            du = du + _mm_nt(gr, bre_ref[:, cols]) + _mm_nt(gi, bim_ref[:, cols])
        du_ref[...] = du

    vm = lambda shape: pl.BlockSpec(shape, lambda i: (0,) * len(shape))
    return _call(
        body, name=name, grid=(1,),
        in_specs=[vm((S, 256)), pl.BlockSpec((S, 256), lambda i: (0, 6)), vm((S, 256)), pl.BlockSpec(memory_space=pl.ANY),
                  vm((2, 8, 128)), vm((256, 1024)), vm((256, 1024)), vm((1024, 256)), vm((1024, 256)), vm((1, 256)),
                  vm((NDEV, 32, 256)), vm((1, 256))],
        out_specs=(vm((S, 256)), vm((256, 1024)), vm((256, 1024)), vm((1024, 256)), vm((1024, 256)),
                   vm((2, 8, 128)), vm((1, 256)), vm((256, 256)), vm((1, 256))),
        out_shape=(_sds((S, 256), F32), _sds((256, 1024), F32), _sds((256, 1024), F32), _sds((1024, 256), F32),
                   _sds((1024, 256), F32), _sds((2, 8, 128), F32), _sds((1, 256), F32), _sds((256, 256), F32),
                   _sds((1, 256), F32)),
        scratch=[pltpu.VMEM((2, S * 8, 128), F32), pltpu.VMEM((2, S * 8, 128), F32)],
        cp=_cp(56, ("arbitrary",)), args=(dy, z, ypre, st, a, bre, bim, cre, cim, dsk, gluw, glub), ride=ride)


def _ssm_discretise(a_re, a_im, log_dt, b_re, b_im):
    dt = jnp.exp(log_dt)[:, None]
    er = jnp.exp(a_re * dt)
    abr, abi = er * jnp.cos(a_im * dt), er * jnp.sin(a_im * dt)
    den = a_re * a_re + a_im * a_im
    fr = ((abr - 1.0) * a_re + abi * a_im) / den
    fi = (abi * a_re - (abr - 1.0) * a_im) / den
    bbr = fr[:, :, None] * b_re - fi[:, :, None] * b_im
    bbi = fr[:, :, None] * b_im + fi[:, :, None] * b_re
    return abr, abi, bbr, bbi


def _blockdiag(t):
    g, r, c = t.shape
    eye = jnp.eye(g, dtype=t.dtype)
    return (t[:, :, None, :] * eye[:, None, :, None]).reshape(g * r, g * c)


def _blockdiag_take(m, r, c):
    g = m.shape[0] // r
    idx = jnp.arange(g)
    return m.reshape(g, r, g, c)[idx, :, idx, :]


PAD = 16


def _pool_lane_select(vals):
    lane = lax.broadcasted_iota(jnp.int32, vals[0].shape, 1)
    out = vals[3]
    for g in (2, 1, 0):
        out = jnp.where(lane < 64 * (g + 1), vals[g], out)
    return out


def _pool_counts():
    row = lax.broadcasted_iota(jnp.int32, (S, D_POOL), 0).astype(F32) + 1.0
    return _pool_lane_select([jnp.minimum(row, float(w)) for w in POOL_WINDOWS])


def _pooled(u, sa, sb):
    sums = []
    cur = u
    bufs = (sa, sb)
    for k, sh in enumerate((1, 2, 4, 8)):
        buf = bufs[k % 2]
        buf[PAD:PAD + S, :] = cur
        cur = cur + buf[PAD - sh:PAD - sh + S, :]
        sums.append(cur)
    return _pool_lane_select(sums) / _pool_counts() - u


def pool_fwd(z, pw, psc, name):
    def body(u_ref, w_ref, s_ref, y_ref, sa, sb):
        for buf in (sa, sb):
            buf[0:PAD, :] = jnp.zeros((PAD, D_POOL), F32)
        pooled = _pooled(u_ref[...], sa, sb)
        y_ref[...] = _mm(pooled, w_ref[...]) * s_ref[...]

    vm = lambda shape: pl.BlockSpec(shape, lambda i: (0,) * len(shape))
    return _pallas_call(
        body, name=name, grid=(1,),
        in_specs=[pl.BlockSpec((S, 256), lambda i: (0, 7)), vm((256, 256)), vm((1, 256))],
        out_specs=vm((S, 256)), out_shape=_sds((S, 256), F32),
        scratch_shapes=[pltpu.VMEM((S + 2 * PAD, D_POOL), F32)] * 2,
        compiler_params=_cp(40, ("arbitrary",)))(z, pw, psc)


def pool_bwd(dy, z, pw, psc, name):
    def body(dy_ref, u_ref, w_ref, s_ref, du_ref, dw_ref, ds_ref, sa, sb):
        for buf in (sa, sb):
            buf[0:PAD, :] = jnp.zeros((PAD, D_POOL), F32)
            buf[PAD + S:PAD + S + PAD, :] = jnp.zeros((PAD, D_POOL), F32)
        pooled = _pooled(u_ref[...], sa, sb)
        dyv = dy_ref[...]
        w = w_ref[...]
        ds_ref[...] = jnp.sum(dyv * _mm(pooled, w), axis=0, keepdims=True)
        dyl = dyv * s_ref[...]
        dw_ref[...] = _mm_tn(pooled, dyl)
        dpool = _mm_nt(dyl, w)
        cur = dpool / _pool_counts()
        sums = []
        bufs = (sa, sb)
        for k, sh in enumerate((1, 2, 4, 8)):
            buf = bufs[k % 2]
            buf[PAD:PAD + S, :] = cur
            cur = cur + buf[PAD + sh:PAD + sh + S, :]
            sums.append(cur)
        du_ref[...] = _pool_lane_select(sums) - dpool

    vm = lambda shape: pl.BlockSpec(shape, lambda i: (0,) * len(shape))
    return _pallas_call(
        body, name=name, grid=(1,),
        in_specs=[vm((S, 256)), pl.BlockSpec((S, 256), lambda i: (0, 7)), vm((256, 256)), vm((1, 256))],
        out_specs=(vm((S, 256)), vm((256, 256)), vm((1, 256))),
        out_shape=(_sds((S, 256), F32), _sds((256, 256), F32), _sds((1, 256), F32)),
        scratch_shapes=[pltpu.VMEM((S + 2 * PAD, D_POOL), F32)] * 2,
        compiler_params=_cp(40, ("arbitrary",)))(dy, z, pw, psc)


def ada_fwd(c_all, ada_w, ada_b_cols):
    def body(c_ref, w_ref, b_ref, o_ref):
        c = c_ref[...]
        cond = c * jax.nn.sigmoid(c)
        o_ref[...] = jnp.dot(cond, w_ref[...], preferred_element_type=F32, precision=lax.Precision.HIGHEST) + b_ref[...]

    return _pallas_call(
        body, name="ada_fwd", grid=(DEPTH,),
        in_specs=[pl.BlockSpec((NDEV, D), lambda l: (0, 0)), pl.BlockSpec((None, D, 1152), lambda l: (l, 0, 0)),
                  pl.BlockSpec((None, 1, 1152), lambda l: (l, 0, 0))],
        out_specs=pl.BlockSpec((None, NDEV, 1152), lambda l: (l, 0, 0)), out_shape=_sds((DEPTH, NDEV, 1152), F32),
        compiler_params=_cp(40, ("arbitrary",)))(c_all, ada_w, ada_b_cols)


def ada_bwd(c_all, dmod_cols):
    def body(c_ref, dm_ref, o_ref):
        c = c_ref[...]
        cond = c * jax.nn.sigmoid(c)
        o_ref[...] = lax.dot_general(cond, dm_ref[...], (((0,), (0,)), ((), ())), preferred_element_type=F32,
                                     precision=lax.Precision.HIGHEST)

    return _pallas_call(
        body, name="ada_bwd", grid=(DEPTH,),
        in_specs=[pl.BlockSpec((NDEV, D), lambda l: (0, 0)), pl.BlockSpec((None, NDEV, 1152), lambda l: (l, 0, 0))],
        out_specs=pl.BlockSpec((None, D, 1152), lambda l: (l, 0, 0)), out_shape=_sds((DEPTH, D, 1152), F32),
        compiler_params=_cp(40, ("arbitrary",)))(c_all, dmod_cols)


def _adamw(w, g, m, v):
    m2 = B1 * m + (1.0 - B1) * g
    v2 = B2 * v + (1.0 - B2) * (g * g)
    m_hat = m2 / (1.0 - B1 ** STEP)
    v_hat = v2 / (1.0 - B2 ** STEP)
    return -LR * (m_hat / (jnp.sqrt(v_hat) + EPS) + WD * w), m2, v2


def _sum8(ref):
    g = ref[0].astype(F32)
    for s in range(1, NDEV):
        g = g + ref[s].astype(F32)
    return g


def adam_rs(recv, w, m, v, tr, name, ride=None):
    lead, (r, cdim) = w.shape[:-2], w.shape[-2:]
    cp = recv.shape[-1]
    nl = len(lead)

    def body(rc_ref, w_ref, m_ref, v_ref, g_ref, d_ref, m2_ref, v2_ref):
        g = _sum8(rc_ref)[:, :cdim]
        g_ref[...] = g
        d_ref[...], m2_ref[...], v2_ref[...] = _adamw(w_ref[...], g, m_ref[...], v_ref[...])

    rs = pl.BlockSpec((None,) * nl + (tr, cdim), lambda *i: (*i, 0))
    return _call(
        body, name=name, grid=lead + (r // tr,),
        in_specs=[pl.BlockSpec((NDEV,) + (None,) * nl + (tr, cp), lambda *i: (0, *i, 0)), rs, rs, rs],
        out_specs=(rs, rs, rs, rs), out_shape=tuple(_sds(w.shape, F32) for _ in range(4)),
        cp=_cp(48, ("arbitrary",) * (nl + 1)), args=(recv, w, m, v), ride=ride)


def adam_rs_rows(recv, w, m, v, name, lf=(0, 2 * DEPTH), prev=None, ride=None):
    half = FB // 2
    lo, hi = lf

    def body(*refs):
        rc_ref, w_ref, m_ref, v_ref = refs[:4]
        g_ref, d_ref, m2_ref, v2_ref = refs[-4:]
        g = _sum8(rc_ref)
        g_ref[...] = g
        d_ref[...], m2_ref[...], v2_ref[...] = _adamw(w_ref[...], g, m_ref[...], v_ref[...])

    rs = pl.BlockSpec((None, None, half, D), lambda a, i: ((a + lo) // 2, (a + lo) % 2, i, 0))
    rc = pl.BlockSpec((NDEV, None, None, half, D), lambda a, i: (0, (a + lo) // 2, (a + lo) % 2, i, 0))
    prev = list(prev) if prev is not None else []
    call = functools.partial(
        _pallas_call, body, name=name, grid=(hi - lo, 2), in_specs=[rc, rs, rs, rs] + [ANY] * len(prev),
        out_specs=(rs, rs, rs, rs), out_shape=tuple(_sds((DEPTH, 2, FB, D), F32) for _ in range(4)),
        compiler_params=_cp(48, ("arbitrary",) * 2))
    if ride is None:
        return list(call(input_output_aliases={4 + k: k for k in range(len(prev))})(recv, w, m, v, *prev)), []
    assert not prev
    return _call(body, name=name, grid=(hi - lo, 2), in_specs=[rc, rs, rs, rs], out_specs=(rs, rs, rs, rs),
                 out_shape=tuple(_sds((DEPTH, 2, FB, D), F32) for _ in range(4)), cp=_cp(48, ("arbitrary",) * 2),
                 args=(recv, w, m, v), ride=ride)


def adam_plain(g, w, m, v, tr, name, ride=None):
    lead, (r, cdim) = w.shape[:-2], w.shape[-2:]
    nl = len(lead)

    def body(g_ref, w_ref, m_ref, v_ref, d_ref, m2_ref, v2_ref):
        d_ref[...], m2_ref[...], v2_ref[...] = _adamw(w_ref[...], g_ref[...], m_ref[...], v_ref[...])

    rs = pl.BlockSpec((None,) * nl + (tr, cdim), lambda *i: (*i, 0))
    return _call(
        body, name=name, grid=lead + (r // tr,), in_specs=[rs, rs, rs, rs], out_specs=(rs, rs, rs),
        out_shape=tuple(_sds(w.shape, F32) for _ in range(3)),
        cp=_cp(48, ("arbitrary",) * (nl + 1)), args=(g, w, m, v), ride=ride)


def adam_native(gs, ws, ms, vs, name):
    n = len(ws)

    def body(*refs):
        g_refs, w_refs, m_refs, v_refs = (refs[k * n:(k + 1) * n] for k in range(4))
        d_refs, m2_refs, v2_refs = (refs[(4 + k) * n:(5 + k) * n] for k in range(3))
        for a in range(n):
            d_refs[a][...], m2_refs[a][...], v2_refs[a][...] = _adamw(w_refs[a][...], g_refs[a][...], m_refs[a][...], v_refs[a][...])

    vm = pl.BlockSpec(memory_space=pltpu.VMEM)
    outs = _pallas_call(body, name=name, in_specs=[vm] * (4 * n), out_specs=tuple([vm] * (3 * n)),
                        out_shape=tuple(_sds(w.shape, F32) for w in ws) * 3, compiler_params=_cp(40))(*gs, *ws, *ms, *vs)
    return outs[:n], outs[n:2 * n], outs[2 * n:]


def sum_sources(recv, name):
    r = recv.shape[1]

    def body(rc_ref, o_ref):
        o_ref[...] = _sum8(rc_ref)

    vm = pl.BlockSpec(memory_space=pltpu.VMEM)
    return _pallas_call(body, name=name, in_specs=[vm], out_specs=vm, out_shape=_sds((r, 128), F32),
                          compiler_params=_cp(40))(recv)


def _pack(arrs):
    flat = jnp.concatenate([a.reshape(-1) for a in arrs])
    n = flat.shape[0]
    rows = -(-n // 1024) * 8
    return jnp.pad(flat, (0, rows * 128 - n)).reshape(rows, 128)


def _unpack(vec, shapes):
    flat = vec.reshape(-1)
    out, o = [], 0
    for sh in shapes:
        n = int(np.prod(sh))
        out.append(flat[o:o + n].reshape(sh))
        o += n
    return out


WEIGHTS = ['rel_bias', 'ada_w', 'ada_b', 'ln_g', 'ln_b', 'ffn_w_gate', 'ffn_w_up', 'ffn_w_down', 'w_in', 'w_out',
           'ssm_a_re', 'ssm_a_im', 'ssm_log_dt', 'ssm_b_re', 'ssm_b_im', 'ssm_c_re', 'ssm_c_im', 'ssm_d', 'glu_w',
           'glu_b', 'pool_w', 'pool_scale']
SMALL = ['rel_bias', 'ada_b', 'ln_g', 'ln_b', 'ssm_a_re', 'ssm_a_im', 'ssm_log_dt', 'ssm_b_re', 'ssm_b_im',
         'ssm_c_re', 'ssm_c_im', 'ssm_d', 'glu_b', 'pool_w', 'pool_scale']
SMALL_FULL_SHAPES = {'rel_bias': (32, 8), 'ada_b': (2, 9216), 'ln_g': (2, 3, 1024), 'ln_b': (2, 3, 1024),
                     'ssm_a_re': (2, 16, 64), 'ssm_a_im': (2, 16, 64), 'ssm_log_dt': (2, 16),
                     'ssm_b_re': (2, 16, 64, 16), 'ssm_b_im': (2, 16, 64, 16), 'ssm_c_re': (2, 16, 16, 64),
                     'ssm_c_im': (2, 16, 16, 64), 'ssm_d': (2, 256), 'glu_b': (2, 256), 'pool_w': (2, 4, 64, 64),
                     'pool_scale': (2, 256)}


def _step(P):
    me = _me()
    x0 = P['x'][0]
    target = P['loss_target'][0]

    def shards(l, sub):
        bf = lambda a: a.astype(BF16)
        if sub == 1:
            return [bf(P['w_in'][l]), bf(P['w_out'][l]), bf(P['glu_w'][l])]
        f = sub // 2
        padr = lambda a: jnp.pad(bf(a), ((0, FBP - FB), (0, 0)))
        return [padr(P['ffn_w_gate'][l, f].T), padr(P['ffn_w_up'][l, f].T), padr(P['ffn_w_down'][l, f])]

    order = [(l, sub) for l in range(DEPTH) for sub in range(3)]
    nxt = dict(zip(order[:-1], order[1:]))
    W = {key: [None] * 3 for key in order}
    c_all, lng_all, lnb_all, *W[order[0]] = _exchange(Gather([P['c'], P['ln_g'], P['ln_b']] + shards(*order[0])), "gather_first")
    gather_queue = [(key, pos, a) for key in order[1:] for pos, a in enumerate(shards(*key))]

    def gather_ride(cap_us, must=None):
        units, used = [], 0.0
        while gather_queue:
            key, _, a = gather_queue[0]
            cost = a.size * a.dtype.itemsize * GATHER_US_PER_BYTE
            if key != must and used + cost / 2 > cap_us:
                break
            units.append(gather_queue.pop(0))
            used += cost
        return (Gather([a for _, _, a in units]) if units else None), units

    def gathered(units, outs):
        for (key, pos, _), o in zip(units, outs):
            W[key][pos] = o

    c_all = c_all.reshape(NDEV, D)
    ln_g = jnp.transpose(lng_all, (1, 2, 0, 3)).reshape(DEPTH, 3, D)
    ln_b = jnp.transpose(lnb_all, (1, 2, 0, 3)).reshape(DEPTH, 3, D)

    ada_b_cols = lax.dynamic_slice_in_dim(P['ada_b'], me * 1152, 1152, axis=1).reshape(DEPTH, 1, 1152)
    modc = ada_fwd(c_all, P['ada_w'], ada_b_cols)
    (mod_all,) = _exchange(Gather([modc]), "gather_mod")
    mod_me = lax.dynamic_index_in_dim(mod_all, me, axis=2, keepdims=False)
    mod = jnp.transpose(mod_me, (1, 0, 2)).reshape(DEPTH, 9, D)

    bias = att_bias(P['rel_bias'])
    ssm = []
    for l in range(DEPTH):
        prm = (P['ssm_a_re'][l], P['ssm_a_im'][l], P['ssm_log_dt'][l], P['ssm_b_re'][l], P['ssm_b_im'][l])
        (abr, abi, bbr, bbi), disc_vjp = jax.vjp(_ssm_discretise, *prm)
        ssm.append(dict(
            vjp=disc_vjp, a=jnp.stack([abr.reshape(8, 128), abi.reshape(8, 128)]),
            bre=_blockdiag(jnp.transpose(bbr, (0, 2, 1))).astype(MXU), bim=_blockdiag(jnp.transpose(bbi, (0, 2, 1))).astype(MXU),
            cre=_blockdiag(jnp.transpose(P['ssm_c_re'][l], (0, 2, 1))).astype(MXU),
            cim=_blockdiag(jnp.transpose(P['ssm_c_im'][l], (0, 2, 1))).astype(MXU),
            d=P['ssm_d'][l].reshape(1, 256), gb=P['glu_b'][l].reshape(1, 256),
            pw=_blockdiag(P['pool_w'][l]).astype(MXU), psc=P['pool_scale'][l].reshape(1, 256)))

    saved = []
    x = x0
    h = ln_mod_fwd(x, mod[0], 0, "ln_mod_fwd_l0s0")
    for l, sub in order:
        tag = f"l{l}s{sub}"
        after = (mod[nxt[(l, sub)][0]], nxt[(l, sub)][1]) if (l, sub) in nxt else None
        if sub != 1:
            wg, wu, wd = (t.reshape(NDEV * FBP, D) for t in W[(l, sub)])
            ride, units = gather_ride(60, nxt.get((l, sub)))
            (G, U, fo), got = ffn_fwd(h, wg, wu, wd, "ffn_fwd_" + tag, ride)
            gathered(units, got)
            saved.append(dict(x=x, h=h, G=G, U=U, f=fo))
            x, *hn = res_ln_fwd(x, fo, mod[l], sub, ln_g[l], ln_b[l], 0.5, "res_ln_fwd_" + tag, after)
        else:
            sp = ssm[l]
            win, wout, gluw = W[(l, sub)]
            ride, units = gather_ride(15)
            (z,), got = win_fwd(h, win, "win_fwd_" + tag, ride)
            gathered(units, got)
            ride, units = gather_ride(55)
            (ya, lse), got = att_fwd(z, bias, "att_fwd_" + tag, ride)
            gathered(units, got)
            ride, units = gather_ride(35)
            (ys, ypre, st), got = ssm_fwd(z, sp['a'], sp['bre'], sp['bim'], sp['cre'], sp['cim'], sp['d'], gluw, sp['gb'],
                                          "ssm_fwd_" + tag, ride)
            gathered(units, got)
            yp = pool_fwd(z, sp['pw'], sp['psc'], "pool_fwd_" + tag)
            ride, units = gather_ride(12, nxt.get((l, sub)))
            (o,), got = wout_fwd(ya, ys, yp, wout, "wout_fwd_" + tag, ride)
            gathered(units, got)
            saved.append(dict(x=x, h=h, z=z, ya=ya, lse=lse, ys=ys, ypre=ypre, st=st, yp=yp, f=o))
            x, *hn = res_ln_fwd(x, o, mod[l], sub, ln_g[l], ln_b[l], 1.0, "res_ln_fwd_" + tag, after)
        h = hn[0] if hn else None
    assert not gather_queue

    loss_tile, dx = loss_fwd_bwd(x, target, "loss")
    loss = lax.psum(loss_tile[0, 0], ("x", "y", "c"))

    recv = dict(g=lax.empty((NDEV, DEPTH, 2, FBP, D), BF16), u=lax.empty((NDEV, DEPTH, 2, FBP, D), BF16),
                d=lax.empty((NDEV, DEPTH, 2, FBP, D), BF16), win=lax.empty((NDEV, DEPTH, D, 256), BF16),
                wout=lax.empty((NDEV, DEPTH, 128, D), BF16), glu=lax.empty((NDEV, DEPTH, 32, 256), BF16))
    pending = []

    def produced(key, index, payload):
        pending.extend((payload, key, index, ks) for ks in ((0, 1, 2, 4), (3, 5), (6,), (7,)))

    def scatter_ride(cap_us):
        units, used = [], 0.0
        while pending:
            payload = pending[0][0]
            cost = payload[0].size * payload.dtype.itemsize * SCATTER_US_PER_BYTE
            if used + cost / 2 > cap_us:
                break
            units.append(pending.pop(0))
            used += cost
        if not units:
            return None
        items = {}
        for payload, key, index, ks in units:
            items.setdefault((id(payload), key, index), [payload, key, index, []])[3].extend(ks)
        return Scatter([tuple(v) for v in items.values()], recv)

    def landed(ride, bufs):
        if ride is not None:
            recv.update(zip(ride.keys, bufs))

    def hosted(cap_us, fn, *args):
        ride = scatter_ride(cap_us)
        outs, bufs = fn(*args, ride)
        landed(ride, bufs)
        return outs

    dmod = [[None] * 9 for _ in range(DEPTH)]
    dlng = [[None] * 3 for _ in range(DEPTH)]
    dlnb = [[None] * 3 for _ in range(DEPTH)]
    dbiases = [None] * DEPTH
    small_l = [dict() for _ in range(DEPTH)]
    for l, sub in reversed(order):
        tag = f"l{l}s{sub}"
        sv = saved[3 * l + sub]
        w = 1.0 if sub == 1 else 0.5
        dxa, df, sums = hosted(0, res_ln_bwd, sv['x'], sv['f'], mod[l], sub, ln_g[l], dx, w, "res_ln_bwd_" + tag)
        dlng[l][sub], dlnb[l][sub], dmod[l][3 * sub + 2] = sums[0], sums[1], sums[2]
        if sub != 1:
            f = sub // 2
            wg, wu, wd = (t.reshape(NDEV * FBP, D) for t in W[(l, sub)])
            dwg, dwu, dwd, dh = hosted(105, ffn_bwd, df, sv['h'], sv['G'], sv['U'], wg, wu, wd, "ffn_bwd_" + tag)
            for key, t in (('g', dwg), ('u', dwu), ('d', dwd)):
                produced(key, (l, f), t.reshape(NDEV, FBP, D))
        else:
            sp = ssm[l]
            win, wout, gluw = W[(l, sub)]
            dya, dys, dyp, dwout = hosted(0, wout_bwd, df, sv['ya'], sv['ys'], sv['yp'], wout, "wout_bwd_" + tag)
            dq, dk, dv, dbiases[l] = hosted(115, att_bwd, sv['z'], bias, sv['ya'], sv['lse'], dya, "att_bwd_" + tag)
            dus, dbre, dbim, dcre, dcim, da, dd, dgw, dgb = hosted(
                52, ssm_bwd, dys, sv['z'], sv['ypre'], sv['st'], sp['a'], sp['bre'], sp['bim'], sp['cre'], sp['cim'], sp['d'],
                gluw, sp['gb'], "ssm_bwd_" + tag)
            dup, dpw, dpsc = pool_bwd(dyp, sv['z'], sp['pw'], sp['psc'], "pool_bwd_" + tag)
            dh, dwin = hosted(40, win_bwd, (dq, dk, dv, dus, dup), sv['h'], win, "win_bwd_" + tag)
            produced('win', (l,), dwin)
            produced('wout', (l,), dwout)
            produced('glu', (l,), dgw.astype(BF16).reshape(NDEV, 32, 256))
            d_are, d_aim, d_ldt, d_bre, d_bim = sp['vjp']((
                da[0].reshape(16, 64), da[1].reshape(16, 64),
                jnp.transpose(_blockdiag_take(dbre, 16, 64), (0, 2, 1)), jnp.transpose(_blockdiag_take(dbim, 16, 64), (0, 2, 1))))
            small_l[l] = dict(
                ssm_a_re=d_are, ssm_a_im=d_aim, ssm_log_dt=d_ldt, ssm_b_re=d_bre, ssm_b_im=d_bim,
                ssm_c_re=jnp.transpose(_blockdiag_take(dcre, 64, 16), (0, 2, 1)),
                ssm_c_im=jnp.transpose(_blockdiag_take(dcim, 64, 16), (0, 2, 1)),
                ssm_d=dd.reshape(256), glu_b=dgb.reshape(256), pool_w=_blockdiag_take(dpw, 64, 64), pool_scale=dpsc.reshape(256))
        dx, sums = hosted(0, ln_mod_bwd, sv['x'], dh, mod[l], sub, dxa, "ln_mod_bwd_" + tag)
        dmod[l][3 * sub], dmod[l][3 * sub + 1] = sums[0], sums[1]
    grad_x = dx[None]

    ride = scatter_ride(1e9)
    landed(ride, _exchange(ride, "scatter_last"))

    out = {}

    def put(name, g, d, m2, v2, shape):
        out['grad_' + name], out['delta_' + name] = g.reshape(shape), d.reshape(shape)
        out['new_m_' + name], out['new_v_' + name] = m2.reshape(shape), v2.reshape(shape)

    def wmv(name):
        return [P[pre + name] for pre in ('', 'm_', 'v_')]

    assert not pending
    for name, key, tr in (('w_in', 'win', 512), ('w_out', 'wout', 128), ('glu_w', 'glu', 32)):
        put(name, *adam_rs(recv[key], *wmv(name), tr, "adam_" + name)[0], P[name].shape)
    put('ffn_w_down', *adam_rs_rows(recv['d'], *wmv('ffn_w_down'), "adam_ffn_w_down")[0], P['ffn_w_down'].shape)
    for name, key in (('ffn_w_gate', 'g'), ('ffn_w_up', 'u')):
        res, _ = adam_rs_rows(recv[key], *[jnp.swapaxes(t, 2, 3) for t in wmv(name)], "adam_" + name)
        put(name, *[jnp.swapaxes(t, 2, 3) for t in res], P[name].shape)

    small = {k: jnp.stack([small_l[l][k] for l in range(DEPTH)]) for k in small_l[0]}
    small['rel_bias'] = relbias_grad(dbiases)
    small['ada_b'] = jnp.stack([jnp.stack(dmod[l]).reshape(9 * D) for l in range(DEPTH)])
    small['ln_g'] = jnp.stack([jnp.stack(dlng[l]) for l in range(DEPTH)])
    small['ln_b'] = jnp.stack([jnp.stack(dlnb[l]) for l in range(DEPTH)])
    (small_all,) = _exchange(Gather([_pack([small[k] for k in SMALL])]), "gather_small")
    gsum = dict(zip(SMALL, _unpack(sum_sources(small_all, "sum_small"), [SMALL_FULL_SHAPES[k] for k in SMALL])))
    off = 256
    dmod_all = small_all.reshape(NDEV, -1)[:, off:off + DEPTH * 9 * D].reshape(NDEV, DEPTH, 9 * D)
    dmod_cols = jnp.transpose(lax.dynamic_slice_in_dim(dmod_all, me * 1152, 1152, axis=2), (1, 0, 2))
    g_ada_w = ada_bwd(c_all, dmod_cols)

    put('ada_w', g_ada_w, *adam_plain(g_ada_w, *wmv('ada_w'), 256, "adam_ada_w")[0], P['ada_w'].shape)

    for k in ('ln_g', 'ln_b'):
        gsum[k] = lax.dynamic_slice_in_dim(gsum[k], me * 128, 128, axis=2)
    swaps = {'rel_bias': (0, 1), 'ln_g': (0, 1), 'ln_b': (0, 1), 'ssm_b_re': (2, 3), 'ssm_b_im': (2, 3)}
    view = lambda k, t: jnp.swapaxes(t, *swaps[k]) if k in swaps else t
    ds_, m2s, v2s = adam_native(*[[view(k, src(k)) for k in SMALL] for src in
                                  (lambda k: gsum[k], lambda k: P[k], lambda k: P['m_' + k], lambda k: P['v_' + k])],
                                "adam_small")
    for k, d, m2, v2 in zip(SMALL, ds_, m2s, v2s):
        put(k, gsum[k], view(k, d), view(k, m2), view(k, v2), P[k].shape)

    res = [loss, grad_x]
    for pre in ('grad_', 'delta_', 'new_m_', 'new_v_'):
        res += [out[pre + k] for k in WEIGHTS]
    return tuple(res)


def kernel(x, c, rel_bias, ada_w, ada_b, ln_g, ln_b, ffn_w_gate, ffn_w_up, ffn_w_down, w_in, w_out, ssm_a_re, ssm_a_im, ssm_log_dt, ssm_b_re, ssm_b_im, ssm_c_re, ssm_c_im, ssm_d, glu_w, glu_b, pool_w, pool_scale, loss_target, m_rel_bias, m_ada_w, m_ada_b, m_ln_g, m_ln_b, m_ffn_w_gate, m_ffn_w_up, m_ffn_w_down, m_w_in, m_w_out, m_ssm_a_re, m_ssm_a_im, m_ssm_log_dt, m_ssm_b_re, m_ssm_b_im, m_ssm_c_re, m_ssm_c_im, m_ssm_d, m_glu_w, m_glu_b, m_pool_w, m_pool_scale, v_rel_bias, v_ada_w, v_ada_b, v_ln_g, v_ln_b, v_ffn_w_gate, v_ffn_w_up, v_ffn_w_down, v_w_in, v_w_out, v_ssm_a_re, v_ssm_a_im, v_ssm_log_dt, v_ssm_b_re, v_ssm_b_im, v_ssm_c_re, v_ssm_c_im, v_ssm_d, v_glu_w, v_glu_b, v_pool_w, v_pool_scale):
    return _step(dict(locals()))
```

```python
import functools
import math

import numpy as np
import jax
import jax.numpy as jnp
from jax import lax
from jax.experimental import pallas as pl
from jax.experimental.pallas import tpu as pltpu

F32 = jnp.float32
BF16 = jnp.bfloat16
MXU = jnp.bfloat16

S = 2048
D = 1024
NDEV = 8
DEPTH = 2
D_ATT, D_SSM, D_POOL, D_IN = 512, 256, 256, 2048
N_HEADS = 8
FB = 352
FBP = 384
QB = 128
PATTERNS = ((128, 1), (512, 4), (2048, 16))
POOL_WINDOWS = (2, 4, 8, 16)
N_BUCKETS, MAX_DISTANCE = 32, 2048
ALPHA = (2 * DEPTH) ** 0.25
LN_EPS = 1e-5
NEG = -1e30
GATHER_US_PER_BYTE = 43e-6
SCATTER_US_PER_BYTE = 21.6e-6
LR, B1, B2, EPS, WD, STEP = 0.001, 0.9, 0.999, 1e-08, 0.01, 10

TM = 256
TMM = 512
MIB = 1024 * 1024


def _cp(vmem_mib, sem=None):
    kw = dict(vmem_limit_bytes=vmem_mib * MIB)
    if sem is not None:
        kw["dimension_semantics"] = sem
    return pltpu.CompilerParams(**kw)


def _sds(shape, dtype):
    return jax.ShapeDtypeStruct(shape, dtype)


def _mm(a, b):
    return jnp.dot(a.astype(MXU), b.astype(MXU), preferred_element_type=F32)


def _mm_nt(a, b):
    return lax.dot_general(a.astype(MXU), b.astype(MXU), (((1,), (1,)), ((), ())), preferred_element_type=F32)


def _mm_tn(a, b):
    return lax.dot_general(a.astype(MXU), b.astype(MXU), (((0,), (0,)), ((), ())), preferred_element_type=F32)


def _ln_stats(x):
    mu = jnp.mean(x, axis=-1, keepdims=True)
    xc = x - mu
    var = jnp.mean(xc * xc, axis=-1, keepdims=True)
    rstd = lax.rsqrt(var + LN_EPS)
    return xc * rstd, rstd


def _ln_bwd(dn, n, rstd):
    return rstd * (dn - jnp.mean(dn, axis=-1, keepdims=True) - n * jnp.mean(dn * n, axis=-1, keepdims=True))


def _me():
    return 4 * lax.axis_index("x") + 2 * lax.axis_index("y") + lax.axis_index("c")


ANY = pl.BlockSpec(memory_space=pl.ANY)
PIN_BYTES = 1 << 19


def _pallas_call(*a, **k):
    big = lambda o: math.prod(o.shape) * o.dtype.itemsize >= PIN_BYTES
    pin = lambda o: pltpu.HBM(o.shape, o.dtype) if isinstance(o, jax.ShapeDtypeStruct) and big(o) else o
    osh = k["out_shape"]
    k["out_shape"] = tuple(pin(o) for o in osh) if isinstance(osh, (tuple, list)) else pin(osh)
    fn = pl.pallas_call(*a, **k)

    def run(*args):
        return fn(*[pltpu.with_memory_space_constraint(x, pltpu.HBM) if big(x) else x for x in args])
    return run


class Gather:
    def __init__(self, srcs):
        self.srcs = list(srcs)
        self.n = len(self.srcs)
        self.bufs = []
        self.out_shapes = [_sds((NDEV,) + a.shape, a.dtype) for a in self.srcs]
        self.sems = [pltpu.SemaphoreType.DMA((7 * self.n,)), pltpu.SemaphoreType.DMA((7 * self.n,)),
                     pltpu.SemaphoreType.DMA((self.n,))]

    def _parts(self, srcs, outs, sems):
        send_sems, recv_sems, loc_sems = sems
        x, y, c = lax.axis_index("x"), lax.axis_index("y"), lax.axis_index("c")
        me, sib = (x, y, c), (x, y, 1 - c)
        chips = [(1 - x, y), (x, 1 - y), (1 - x, 1 - y)]
        slot = lambda d: 4 * d[0] + 2 * d[1] + d[2]

        def copy(a, k, block, to, src=None):
            dst = outs[a].at[slot(block)]
            return pltpu.make_async_remote_copy(
                src_ref=dst if src is None else src, dst_ref=dst,
                send_sem=send_sems.at[7 * a + k], recv_sem=recv_sems.at[7 * a + k],
                device_id=to, device_id_type=pl.DeviceIdType.MESH)

        local = [pltpu.make_async_copy(srcs[a], outs[a].at[slot(me)], loc_sems.at[a]) for a in range(self.n)]
        return me, sib, chips, c, copy, local

    def start(self, srcs, bufs, outs, sems):
        me, sib, chips, c, copy, local = self._parts(srcs, outs, sems)
        for a in range(self.n):
            local[a].start()
            copy(a, 0, me, sib, src=srcs[a]).start()
            for j, chip in enumerate(chips):
                copy(a, 1 + j, me, (*chip, c), src=srcs[a]).start()

    def finish(self, srcs, bufs, outs, sems):
        me, sib, chips, c, copy, local = self._parts(srcs, outs, sems)
        for a in range(self.n):
            for j, chip in enumerate(chips):
                copy(a, 1 + j, (*chip, c), me).wait_recv()
                copy(a, 4 + j, (*chip, c), sib).start()
        for a in range(self.n):
            copy(a, 0, sib, me).wait_recv()
            copy(a, 0, me, sib, src=srcs[a]).wait_send()
            for j, chip in enumerate(chips):
                copy(a, 4 + j, (*chip, 1 - c), me).wait_recv()
                copy(a, 1 + j, me, (*chip, c), src=srcs[a]).wait_send()
                copy(a, 4 + j, (*chip, c), sib).wait_send()
            local[a].wait()


class Scatter:
    def __init__(self, items, bufs):
        self.items = list(items)
        self.keys = list(dict.fromkeys(key for _, key, _, _ in self.items))
        self.srcs = [src for src, _, _, _ in self.items]
        self.bufs = [bufs[key] for key in self.keys]
        self.n = len(self.srcs)
        self.out_shapes = [_sds(b.shape, b.dtype) for b in self.bufs]
        pairs = [(a, k) for a, (_, _, _, ks) in enumerate(self.items) for k in ks]
        self.remote_pairs = [p for p in pairs if p[1] != 0]
        self.local_pairs = [p for p in pairs if p[1] == 0]
        self.sems = [pltpu.SemaphoreType.DMA((max(len(self.remote_pairs), 1),)),
                     pltpu.SemaphoreType.DMA((max(len(self.remote_pairs), 1),)),
                     pltpu.SemaphoreType.DMA((max(len(self.local_pairs), 1),))]

    def _copies(self, srcs, outs, sems):
        send_sems, recv_sems, loc_sems = sems
        me = _me()

        def dst(a, slot):
            _, key, index, _ = self.items[a]
            return outs[self.keys.index(key)].at[(slot,) + tuple(index)]

        def remote(n, slot):
            a, k = self.remote_pairs[n]
            t = me ^ k
            return pltpu.make_async_remote_copy(
                src_ref=srcs[a].at[t], dst_ref=dst(a, slot), send_sem=send_sems.at[n], recv_sem=recv_sems.at[n],
                device_id=(t // 4, (t // 2) % 2, t % 2), device_id_type=pl.DeviceIdType.MESH)

        local = [pltpu.make_async_copy(srcs[a].at[me], dst(a, me), loc_sems.at[n])
                 for n, (a, _) in enumerate(self.local_pairs)]
        return me, remote, local

    def start(self, srcs, bufs, outs, sems):
        me, remote, local = self._copies(srcs, outs, sems)
        for cp in local:
            cp.start()
        for n in range(len(self.remote_pairs)):
            remote(n, me).start()

    def finish(self, srcs, bufs, outs, sems):
        me, remote, local = self._copies(srcs, outs, sems)
        for n, (_, k) in enumerate(self.remote_pairs):
            remote(n, me ^ k).wait()
        for cp in local:
            cp.wait()


def _call(body, *, name, grid, in_specs, out_specs, out_shape, args, scratch=(), cp=None, ride=None):
    out_specs, out_shape, scratch = list(out_specs), list(out_shape), list(scratch)
    if ride is None:
        outs = _pallas_call(body, name=name, grid=grid, in_specs=list(in_specs), out_specs=tuple(out_specs),
                              out_shape=tuple(out_shape), scratch_shapes=scratch, compiler_params=cp)(*args)
        return list(outs), []
    nin, nout, nscr, n, nb, no = len(in_specs), len(out_specs), len(scratch), ride.n, len(ride.bufs), len(ride.out_shapes)
    steps = list(grid)

    def wrapped(*refs):
        h_in, r_src, r_buf = refs[:nin], refs[nin:nin + n], refs[nin + n:nin + n + nb]
        o0 = nin + n + nb
        h_out, r_out = refs[o0:o0 + nout], refs[o0 + nout:o0 + nout + no]
        s0 = o0 + nout + no
        h_scr, sems = refs[s0:s0 + nscr], refs[s0 + nscr:]
        ids = [pl.program_id(a) for a in range(len(steps))]
        first = functools.reduce(jnp.logical_and, [i == 0 for i in ids])
        last = functools.reduce(jnp.logical_and, [i == s - 1 for i, s in zip(ids, steps)])

        @pl.when(first)
        def _():
            ride.start(r_src, r_buf, r_out, sems)

        body(*h_in, *h_out, *h_scr)

        @pl.when(last)
        def _():
            ride.finish(r_src, r_buf, r_out, sems)

    aliases = {nin + n + k: nout + k for k in range(nb)}
    outs = _pallas_call(
        wrapped, name=name, grid=grid, in_specs=list(in_specs) + [ANY] * (n + nb),
        out_specs=tuple(out_specs + [ANY] * no), out_shape=tuple(out_shape + ride.out_shapes),
        scratch_shapes=scratch + ride.sems, input_output_aliases=aliases, compiler_params=cp,
    )(*args, *ride.srcs, *ride.bufs)
    return list(outs[:nout]), list(outs[nout:])


def _exchange(ride, name):
    def body(dummy_ref, o_ref):
        o_ref[...] = dummy_ref[...]

    one = pl.BlockSpec((8, 128), lambda i: (0, 0))
    _, outs = _call(body, name=name, grid=(1,), in_specs=[one], out_specs=[one], out_shape=[_sds((8, 128), F32)],
                    args=(jnp.zeros((8, 128), F32),), ride=ride)
    return outs


HBM = pl.BlockSpec(memory_space=pltpu.HBM)
SEM = pl.BlockSpec(memory_space=pltpu.SEMAPHORE)


def _scatter_copies(srcs, lands, sems):
    send_sems, recv_sems, loc_sems = sems
    me = _me()

    def remote(a, k, slot):
        t = me ^ k
        return pltpu.make_async_remote_copy(
            src_ref=srcs[a].at[t], dst_ref=lands[a].at[slot], send_sem=send_sems.at[7 * a + k - 1],
            recv_sem=recv_sems.at[7 * a + k - 1], device_id=(t // 4, (t // 2) % 2, t % 2), device_id_type=pl.DeviceIdType.MESH)

    local = [pltpu.make_async_copy(srcs[a].at[me], lands[a].at[me], loc_sems.at[a]) for a in range(len(srcs))]
    return me, remote, local


def scatter_start(payloads, name):
    n = len(payloads)

    def body(*refs):
        srcs, lands, sems = refs[:n], refs[n:2 * n], refs[2 * n:2 * n + 3]
        me, remote, local = _scatter_copies(srcs, lands, sems)
        for a in range(n):
            local[a].start()
            for k in range(1, NDEV):
                remote(a, k, me).start()
        refs[-1][...] = jnp.zeros((8, 128), F32)

    thru = [pltpu.HBM(p.shape, p.dtype) for p in payloads]
    outs = pl.pallas_call(
        body, name=name,
        out_shape=(pltpu.SemaphoreType.DMA((7 * n,)), pltpu.SemaphoreType.DMA((7 * n,)), pltpu.SemaphoreType.DMA((n,)),
                   *thru, *thru, _sds((8, 128), F32)),
        in_specs=[HBM] * (2 * n), out_specs=(SEM, SEM, SEM, *[HBM] * (2 * n), pl.BlockSpec(memory_space=pltpu.VMEM)),
        input_output_aliases={i: 3 + i for i in range(2 * n)},
        compiler_params=pltpu.CompilerParams(has_side_effects=pltpu.SideEffectType.DATAFLOW_SIDE_EFFECTING),
    )(*[pltpu.with_memory_space_constraint(p, pltpu.HBM) for p in payloads],
      *[pltpu.with_memory_space_constraint(lax.empty(p.shape, p.dtype), pltpu.HBM) for p in payloads])
    return outs[:3], outs[3:3 + n], outs[3 + n:3 + 2 * n]


def scatter_wait(handle, after, name):
    sems, srcs_thru, lands_thru = handle
    n = len(srcs_thru)

    def body(*refs):
        srcs, lands, sems_ = refs[:n], refs[n:2 * n], refs[2 * n:2 * n + 3]
        me, remote, local = _scatter_copies(srcs, lands, sems_)
        for a in range(n):
            for k in range(1, NDEV):
                cp = remote(a, k, me ^ k)
                cp.wait_send()
                cp.wait_recv()
            local[a].wait()

    outs = pl.pallas_call(
        body, name=name, out_shape=tuple(pltpu.HBM(p.shape, p.dtype) for p in (*srcs_thru, *lands_thru)),
        in_specs=[HBM] * (2 * n) + [SEM] * 3 + [ANY], out_specs=tuple([HBM] * (2 * n)),
        input_output_aliases={i: i for i in range(2 * n)},
        compiler_params=pltpu.CompilerParams(has_side_effects=pltpu.SideEffectType.DATAFLOW_SIDE_EFFECTING),
    )(*srcs_thru, *lands_thru, *sems, after)
    return list(outs[n:])


def _row_spec(cols, tm=TM):
    return pl.BlockSpec((tm, cols), lambda i: (i, 0))


def _full_spec(shape):
    nd = len(shape)
    return pl.BlockSpec(shape, lambda i: (0,) * nd)


def ln_mod_fwd(x, mod, sub, name):
    def body(x_ref, mod_ref, h_ref):
        n, _ = _ln_stats(x_ref[...])
        shift = mod_ref[3 * sub:3 * sub + 1, :]
        scale = mod_ref[3 * sub + 1:3 * sub + 2, :]
        h_ref[...] = (n * (1.0 + scale) + shift).astype(MXU)

    return _pallas_call(
        body, name=name, grid=(S // TM,),
        in_specs=[_row_spec(D), _full_spec((9, D))], out_specs=_row_spec(D),
        out_shape=_sds((S, D), MXU), compiler_params=_cp(32, ("arbitrary",)))(x, mod)


def res_ln_fwd(x, f, mod, sub, lng, lnb, w, name, nxt=None):
    def body(x_ref, f_ref, mod_ref, g_ref, b_ref, *rest):
        gate = mod_ref[3 * sub + 2:3 * sub + 3, :]
        r = ALPHA * x_ref[...] + (w * gate) * f_ref[...]
        n, _ = _ln_stats(r)
        xo = n * g_ref[sub:sub + 1, :] + b_ref[sub:sub + 1, :]
        rest[-1 if nxt is None else -2][...] = xo
        if nxt is not None:
            nmod_ref, h_ref = rest[0], rest[-1]
            n2, _ = _ln_stats(xo)
            s2 = nxt[1]
            h_ref[...] = (n2 * (1.0 + nmod_ref[3 * s2 + 1:3 * s2 + 2, :]) + nmod_ref[3 * s2:3 * s2 + 1, :]).astype(MXU)

    more = nxt is not None
    return _pallas_call(
        body, name=name, grid=(S // TM,),
        in_specs=[_row_spec(D), _row_spec(D), _full_spec((9, D)), _full_spec((3, D)), _full_spec((3, D))] + [_full_spec((9, D))] * more,
        out_specs=(_row_spec(D),) + (_row_spec(D),) * more, out_shape=(_sds((S, D), F32),) + (_sds((S, D), MXU),) * more,
        compiler_params=_cp(32, ("arbitrary",)))(x, f, mod, lng, lnb, *([nxt[0]] if more else []))


def res_ln_bwd(x, f, mod, sub, lng, dxo, w, name, ride=None):
    def body(x_ref, f_ref, mod_ref, g_ref, dxo_ref, dxa_ref, df_ref, sums_ref):
        i = pl.program_id(0)
        gate = mod_ref[3 * sub + 2:3 * sub + 3, :]
        fv = f_ref[...]
        r = ALPHA * x_ref[...] + (w * gate) * fv
        n, rstd = _ln_stats(r)
        dxo = dxo_ref[...]
        dr = _ln_bwd(dxo * g_ref[sub:sub + 1, :], n, rstd)
        dxa_ref[...] = ALPHA * dr
        df_ref[...] = ((w * gate) * dr).astype(MXU)
        part = jnp.concatenate([
            jnp.sum(dxo * n, axis=0, keepdims=True),
            jnp.sum(dxo, axis=0, keepdims=True),
            jnp.sum(dr * fv, axis=0, keepdims=True) * w,
            jnp.zeros((5, D), F32)], axis=0)

        @pl.when(i == 0)
        def _():
            sums_ref[...] = part

        @pl.when(i > 0)
        def _():
            sums_ref[...] += part

    return _call(
        body, name=name, grid=(S // TM,),
        in_specs=[_row_spec(D), _row_spec(D), _full_spec((9, D)), _full_spec((3, D)), _row_spec(D)],
        out_specs=(_row_spec(D), _row_spec(D), _full_spec((8, D))),
        out_shape=(_sds((S, D), F32), _sds((S, D), MXU), _sds((8, D), F32)),
        cp=_cp(32, ("arbitrary",)), args=(x, f, mod, lng, dxo), ride=ride)


def ln_mod_bwd(x, dh, mod, sub, dxa, name, ride=None):
    def body(x_ref, dh_ref, mod_ref, dxa_ref, dx_ref, sums_ref):
        i = pl.program_id(0)
        scale = mod_ref[3 * sub + 1:3 * sub + 2, :]
        n, rstd = _ln_stats(x_ref[...])
        dh = dh_ref[...]
        dx_ref[...] = dxa_ref[...] + _ln_bwd(dh * (1.0 + scale), n, rstd)
        part = jnp.concatenate([
            jnp.sum(dh, axis=0, keepdims=True),
            jnp.sum(dh * n, axis=0, keepdims=True),
            jnp.zeros((6, D), F32)], axis=0)

        @pl.when(i == 0)
        def _():
            sums_ref[...] = part

        @pl.when(i > 0)
        def _():
            sums_ref[...] += part

    return _call(
        body, name=name, grid=(S // TM,),
        in_specs=[_row_spec(D), _row_spec(D), _full_spec((9, D)), _row_spec(D)],
        out_specs=(_row_spec(D), _full_spec((8, D))),
        out_shape=(_sds((S, D), F32), _sds((8, D), F32)),
        cp=_cp(32, ("arbitrary",)), args=(x, dh, mod, dxa), ride=ride)


def loss_fwd_bwd(y, target, name):
    def body(y_ref, t_ref, l_ref, dy_ref):
        i = pl.program_id(0)
        e = y_ref[...] - t_ref[...]
        dy_ref[...] = e * (1.0 / D)
        part = jnp.zeros((8, 128), F32) + (0.5 / D) * jnp.sum(e * e)

        @pl.when(i == 0)
        def _():
            l_ref[...] = part

        @pl.when(i > 0)
        def _():
            l_ref[...] += part

    return _pallas_call(
        body, name=name, grid=(S // TM,),
        in_specs=[_row_spec(D), _row_spec(D)], out_specs=(_full_spec((8, 128)), _row_spec(D)),
        out_shape=(_sds((8, 128), F32), _sds((S, D), F32)),
        compiler_params=_cp(32, ("arbitrary",)))(y, target)


HB = 2 * FBP
NHB = NDEV * FBP // HB
TMB = 1024


def _wrows(buffers=2):
    return pl.BlockSpec((HB, D), lambda j, i: (j, 0), pipeline_mode=pl.Buffered(buffers))


def _resident(shape):
    return pl.BlockSpec(shape, lambda j, i: (0, 0), pipeline_mode=pl.Buffered(1))


def ffn_fwd(h, wgt, wut, wd, name, ride=None):
    def body(h_ref, wg_ref, wu_ref, wd_ref, g_ref, u_ref, f_ref):
        j, i = pl.program_id(0), pl.program_id(1)
        hv = h_ref[...]
        g = _mm_nt(hv, wg_ref[...])
        u = _mm_nt(hv, wu_ref[...])
        g_ref[...] = g.astype(MXU)
        u_ref[...] = u.astype(MXU)
        a = g * jax.nn.sigmoid(g) * u
        part = _mm(a, wd_ref[...])
        rows = pl.ds(pl.multiple_of(i * TMB, TMB), TMB)

        @pl.when(j == 0)
        def _():
            f_ref[rows, :] = part

        @pl.when(j > 0)
        def _():
            f_ref[rows, :] += part

    gu = pl.BlockSpec((TMB, HB), lambda j, i: (i, j))
    return _call(
        body, name=name, grid=(NHB, S // TMB),
        in_specs=[pl.BlockSpec((TMB, D), lambda j, i: (i, 0)), _wrows(), _wrows(), _wrows()],
        out_specs=(gu, gu, _resident((S, D))),
        out_shape=(_sds((S, NDEV * FBP), MXU), _sds((S, NDEV * FBP), MXU), _sds((S, D), F32)),
        cp=_cp(52, ("arbitrary", "arbitrary")), args=(h, wgt, wut, wd), ride=ride)


def ffn_bwd(df, h, g, u, wgt, wut, wd, name, ride=None):
    ni = S // TMB

    def body(df_ref, h_ref, g_ref, u_ref, wg_ref, wu_ref, wd_ref, dwg_ref, dwu_ref, dwd_ref, dh_ref,
             ag_ref, au_ref, ad_ref):
        j, i = pl.program_id(0), pl.program_id(1)
        dfv, hv = df_ref[...], h_ref[...]
        gv, uv = g_ref[...].astype(F32), u_ref[...].astype(F32)
        da = _mm_nt(dfv, wd_ref[...])
        sg = jax.nn.sigmoid(gv)
        silu = gv * sg
        du = da * silu
        dg = da * uv * (sg * (1.0 + gv * (1.0 - sg)))
        p_d = _mm_tn(silu * uv, dfv)
        p_g = _mm_tn(dg, hv)
        p_u = _mm_tn(du, hv)

        @pl.when(i == 0)
        def _():
            ad_ref[...] = p_d
            ag_ref[...] = p_g
            au_ref[...] = p_u

        @pl.when(i > 0)
        def _():
            ad_ref[...] += p_d
            ag_ref[...] += p_g
            au_ref[...] += p_u

        @pl.when(i == ni - 1)
        def _():
            dwd_ref[...] = ad_ref[...].astype(BF16)
            dwg_ref[...] = ag_ref[...].astype(BF16)
            dwu_ref[...] = au_ref[...].astype(BF16)

        part = _mm(dg, wg_ref[...]) + _mm(du, wu_ref[...])
        rows = pl.ds(pl.multiple_of(i * TMB, TMB), TMB)

        @pl.when(j == 0)
        def _():
            dh_ref[rows, :] = part

        @pl.when(j > 0)
        def _():
            dh_ref[rows, :] += part

    gu = pl.BlockSpec((TMB, HB), lambda j, i: (i, j))
    rowt = pl.BlockSpec((TMB, D), lambda j, i: (i, 0))
    return _call(
        body, name=name, grid=(NHB, ni),
        in_specs=[rowt, rowt, gu, gu, _wrows(1), _wrows(1), _wrows(1)],
        out_specs=(_wrows(1), _wrows(1), _wrows(1), _resident((S, D))),
        out_shape=(_sds((NDEV * FBP, D), BF16), _sds((NDEV * FBP, D), BF16), _sds((NDEV * FBP, D), BF16), _sds((S, D), F32)),
        scratch=[pltpu.VMEM((HB, D), F32), pltpu.VMEM((HB, D), F32), pltpu.VMEM((HB, D), F32)],
        cp=_cp(60, ("arbitrary", "arbitrary")), args=(df, h, g, u, wgt, wut, wd), ride=ride)


def win_fwd(h, win, name, ride=None):
    def body(h_ref, w_ref, z_ref):
        hv = h_ref[...]
        for j in range(NDEV):
            z_ref[:, 256 * j:256 * (j + 1)] = _mm(hv, w_ref[j])

    return _call(
        body, name=name, grid=(S // TMM,),
        in_specs=[_row_spec(D, TMM), _full_spec((NDEV, D, 256))],
        out_specs=[_row_spec(D_IN, TMM)], out_shape=[_sds((S, D_IN), F32)],
        cp=_cp(40, ("arbitrary",)), args=(h, win), ride=ride)


def win_bwd(dparts, h, win, name, ride=None):
    ni = S // TMM

    def body(dq_ref, dk_ref, dv_ref, dus_ref, dup_ref, h_ref, w_ref, dh_ref, dw_ref, acc_ref):
        i = pl.program_id(0)
        hv = h_ref[...]
        cols = [dq_ref[:, 0:256], dq_ref[:, 256:512], dk_ref[:, 0:256], dk_ref[:, 256:512],
                dv_ref[:, 0:256], dv_ref[:, 256:512], dus_ref[...], dup_ref[...]]
        dh = jnp.zeros((TMM, D), F32)
        for j in range(NDEV):
            dz = cols[j].astype(MXU)
            dh = dh + _mm_nt(dz, w_ref[j])
            p = _mm_tn(hv, dz)

            @pl.when(i == 0)
            def _():
                acc_ref[j] = p

            @pl.when(i > 0)
            def _():
                acc_ref[j] += p

        dh_ref[...] = dh

        @pl.when(i == ni - 1)
        def _():
            dw_ref[...] = acc_ref[...].astype(BF16)

    return _call(
        body, name=name, grid=(ni,),
        in_specs=[_row_spec(512, TMM), _row_spec(512, TMM), _row_spec(512, TMM), _row_spec(256, TMM), _row_spec(256, TMM),
                  _row_spec(D, TMM), _full_spec((NDEV, D, 256))],
        out_specs=(_row_spec(D, TMM), _full_spec((NDEV, D, 256))),
        out_shape=(_sds((S, D), F32), _sds((NDEV, D, 256), BF16)),
        scratch=[pltpu.VMEM((NDEV, D, 256), F32)],
        cp=_cp(48, ("arbitrary",)), args=(*dparts, h, win), ride=ride)


def wout_fwd(ya, ys, yp, wout, name, ride=None):
    def body(ya_ref, ys_ref, yp_ref, w_ref, o_ref):
        w = w_ref[...].reshape(D, D)
        o_ref[...] = _mm(ya_ref[...], w[0:512]) + _mm(ys_ref[...], w[512:768]) + _mm(yp_ref[...], w[768:1024])

    return _call(
        body, name=name, grid=(S // TMM,),
        in_specs=[_row_spec(512, TMM), _row_spec(256, TMM), _row_spec(256, TMM), _full_spec((NDEV, 128, D))],
        out_specs=[_row_spec(D, TMM)], out_shape=[_sds((S, D), F32)],
        cp=_cp(40, ("arbitrary",)), args=(ya, ys, yp, wout), ride=ride)


def wout_bwd(do, ya, ys, yp, wout, name, ride=None):
    ni = S // TMM

    def body(do_ref, ya_ref, ys_ref, yp_ref, w_ref, dya_ref, dys_ref, dyp_ref, dw_ref, acc_ref):
        i = pl.program_id(0)
        w = w_ref[...].reshape(D, D)
        dov = do_ref[...]
        dya_ref[...] = _mm_nt(dov, w[0:512])
        dys_ref[...] = _mm_nt(dov, w[512:768])
        dyp_ref[...] = _mm_nt(dov, w[768:1024])
        parts = [(0, 512, _mm_tn(ya_ref[...], dov)), (512, 768, _mm_tn(ys_ref[...], dov)),
                 (768, 1024, _mm_tn(yp_ref[...], dov))]
        for lo, hi, p in parts:
            @pl.when(i == 0)
            def _():
                acc_ref[lo:hi, :] = p

            @pl.when(i > 0)
            def _():
                acc_ref[lo:hi, :] += p

        @pl.when(i == ni - 1)
        def _():
            dw_ref[...] = acc_ref[...].astype(BF16).reshape(NDEV, 128, D)

    return _call(
        body, name=name, grid=(ni,),
        in_specs=[_row_spec(D, TMM), _row_spec(512, TMM), _row_spec(256, TMM), _row_spec(256, TMM),
                  _full_spec((NDEV, 128, D))],
        out_specs=(_row_spec(512, TMM), _row_spec(256, TMM), _row_spec(256, TMM), _full_spec((NDEV, 128, D))),
        out_shape=(_sds((S, 512), F32), _sds((S, 256), F32), _sds((S, 256), F32), _sds((NDEV, 128, D), BF16)),
        scratch=[pltpu.VMEM((D, D), F32)],
        cp=_cp(40, ("arbitrary",)), args=(do, ya, ys, yp, wout), ride=ride)


def _t5_bucket(dist):
    max_exact = N_BUCKETS // 2
    d = np.maximum(dist, 1).astype(np.float32)
    large = max_exact + (np.log(d / max_exact) / math.log(MAX_DISTANCE / max_exact)
                         * (N_BUCKETS - max_exact)).astype(np.int32)
    large = np.minimum(large, N_BUCKETS - 1)
    return np.where(dist < max_exact, dist, large).astype(np.int32)


def _att_static():
    i = np.arange(QB)[:, None]
    j = np.arange(2 * QB)[None, :]
    r = i + QB - j
    buckets, bands = [], []
    for window, dil in PATTERNS:
        bands.append((r >= 0) & (r <= window // dil))
        buckets.append(_t5_bucket(np.clip(r, 0, None) * dil))
    return np.stack(buckets), np.stack(bands), np.broadcast_to(j >= QB, (QB, 2 * QB))


def att_bias(rel_bias):
    m = np.arange(2 * QB)
    rows = []
    for window, dil in PATTERNS:
        r = QB - m
        ok = (r >= 0) & (r <= window // dil)
        b = rel_bias[_t5_bucket(np.clip(r, 0, None) * dil)]
        rows.append(jnp.where(ok[:, None], b, NEG).T)
    return jnp.broadcast_to(jnp.stack(rows)[:, :, None, :], (3, N_HEADS, 8, 2 * QB))


def _bias_tiles(t_ref, tiles):
    col = lax.broadcasted_iota(jnp.int32, (QB, 2 * QB), 1)
    for p in range(3):
        for hh in range(2):
            t = pltpu.roll(jnp.broadcast_to(t_ref[p, hh, 0:1, :], (QB, 2 * QB)), 0, 1, stride=1, stride_axis=0)
            tiles[p, hh, 0] = t
            tiles[p, hh, 1] = jnp.where(col >= QB, t, NEG)


def _permute_in(dst_ref, src_ref, d, scale=None, pad=QB):
    L = S // d
    for r in range(d):
        v = src_ref[pl.ds(r, L, stride=d), :] if d > 1 else src_ref[...]
        if scale is not None:
            v = v * scale
        dst_ref[pad + r * L:pad + (r + 1) * L, :] = v.astype(dst_ref.dtype)


def att_fwd(z, bias, name, ride=None):
    def body(q_ref, k_ref, v_ref, t_ref, y_ref, l_ref, qs, ks, vs, o_perm, l_perm, o_nat, l_nat, b_ref):
        _bias_tiles(t_ref, b_ref)
        zero_pad = jnp.zeros((QB, 128), MXU)
        ks[0:QB, :] = zero_pad
        vs[0:QB, :] = zero_pad
        lane = lax.broadcasted_iota(jnp.int32, (QB, 128), 1)
        for p, (_, d) in enumerate(PATTERNS):
            L = S // d
            nb = L // QB
            _permute_in(qs, q_ref, d, scale=0.125, pad=0)
            _permute_in(ks, k_ref, d)
            _permute_in(vs, v_ref, d)

            def blk(b, carry):
                r0 = pl.multiple_of(b * QB, QB)
                q = qs[pl.ds(r0, QB), :]
                kb = ks[pl.ds(r0, 2 * QB), :]
                vb = vs[pl.ds(r0, 2 * QB), :]
                first = ((b % nb) == 0).astype(jnp.int32)
                res = []
                for hh in range(2):
                    sel = (lane < 64) if hh == 0 else (lane >= 64)
                    qm = jnp.where(sel, q, jnp.zeros_like(q))
                    s = _mm_nt(qm, kb) + b_ref[p, hh, first]
                    m = jnp.max(s, axis=1, keepdims=True)
                    pe = jnp.exp(s - m)
                    den = jnp.sum(pe, axis=1, keepdims=True)
                    res.append((_mm(pe, vb) / den, m + jnp.log(den)))
                o_perm[pl.ds(r0, QB), :] = jnp.where(lane < 64, res[0][0], res[1][0])
                l_perm[pl.ds(r0, QB), :] = jnp.where(lane < 64, res[0][1], res[1][1])
                return carry

            lax.fori_loop(0, S // QB, blk, 0, unroll=8)
            for r in range(d):
                if d > 1:
                    o_nat[p, pl.ds(r, L, stride=d), :] = o_perm[r * L:(r + 1) * L, :]
                    l_nat[p, pl.ds(r, L, stride=d), :] = l_perm[r * L:(r + 1) * L, :]
                else:
                    o_nat[p] = o_perm[...]
                    l_nat[p] = l_perm[...]
        l0, l1, l2 = l_nat[0], l_nat[1], l_nat[2]
        m = jnp.maximum(jnp.maximum(l0, l1), l2)
        e0, e1, e2 = jnp.exp(l0 - m), jnp.exp(l1 - m), jnp.exp(l2 - m)
        den = e0 + e1 + e2
        y_ref[...] = (e0 * o_nat[0] + e1 * o_nat[1] + e2 * o_nat[2]) / den
        l_ref[...] = m + jnp.log(den)

    col = lambda c0: pl.BlockSpec((S, 128), lambda hp: (0, c0 + hp))
    return _call(
        body, name=name, grid=(N_HEADS // 2,),
        in_specs=[col(0), col(4), col(8), pl.BlockSpec((3, 2, 8, 2 * QB), lambda hp: (0, hp, 0, 0))],
        out_specs=(col(0), col(0)),
        out_shape=(_sds((S, D_ATT), F32), _sds((S, D_ATT), F32)),
        scratch=[pltpu.VMEM((S, 128), MXU), pltpu.VMEM((S + QB, 128), MXU), pltpu.VMEM((S + QB, 128), MXU),
                 pltpu.VMEM((S, 128), F32), pltpu.VMEM((S, 128), F32),
                 pltpu.VMEM((3, S, 128), F32), pltpu.VMEM((3, S, 128), F32),
                 pltpu.VMEM((3, 2, 2, QB, 2 * QB), F32)],
        cp=_cp(40, ("arbitrary",)), args=(z, z, z, bias), ride=ride)


def att_bwd(z, bias, y, lse, dy, name, ride=None):
    def body(q_ref, k_ref, v_ref, t_ref, y_ref, l_ref, dy_ref, dq_ref, dk_ref, dv_ref, db_ref,
             qs, ks, vs, dys, ls, dds, dn_nat, dq_perm, dk_perm, dv_perm, b_ref):
        _bias_tiles(t_ref, b_ref)
        zero_pad = jnp.zeros((QB, 128), MXU)
        ks[0:QB, :] = zero_pad
        vs[0:QB, :] = zero_pad
        lane = lax.broadcasted_iota(jnp.int32, (QB, 128), 1)
        lane_s = lax.broadcasted_iota(jnp.int32, (S, 128), 1)
        t = dy_ref[...] * y_ref[...]
        sa = jnp.sum(jnp.where(lane_s < 64, t, 0.0), axis=1, keepdims=True)
        sb = jnp.sum(jnp.where(lane_s >= 64, t, 0.0), axis=1, keepdims=True)
        dn_nat[...] = jnp.where(lane_s < 64, sa, sb)
        dq_ref[...] = jnp.zeros((S, 128), F32)
        dk_ref[...] = jnp.zeros((S, 128), F32)
        dv_ref[...] = jnp.zeros((S, 128), F32)
        db_ref[...] = jnp.zeros((3, 2, QB, 2 * QB), F32)
        for p, (_, d) in enumerate(PATTERNS):
            L = S // d
            nb = L // QB
            _permute_in(qs, q_ref, d, scale=0.125, pad=0)
            _permute_in(ks, k_ref, d)
            _permute_in(vs, v_ref, d)
            _permute_in(dys, dy_ref, d, pad=0)
            _permute_in(ls, l_ref, d, pad=0)
            _permute_in(dds, dn_nat, d, pad=0)
            dk_perm[...] = jnp.zeros((S + QB, 128), F32)
            dv_perm[...] = jnp.zeros((S + QB, 128), F32)

            def blk(b, carry):
                r0 = pl.multiple_of(b * QB, QB)
                q = qs[pl.ds(r0, QB), :]
                kb = ks[pl.ds(r0, 2 * QB), :]
                vb = vs[pl.ds(r0, 2 * QB), :]
                dyb = dys[pl.ds(r0, QB), :]
                lb = ls[pl.ds(r0, QB), :]
                db = dds[pl.ds(r0, QB), :]
                first = ((b % nb) == 0).astype(jnp.int32)
                dqs = []
                dkb = jnp.zeros((2 * QB, 128), F32)
                dvb = jnp.zeros((2 * QB, 128), F32)
                for hh in range(2):
                    sel = (lane < 64) if hh == 0 else (lane >= 64)
                    c0 = 64 * hh
                    qm = jnp.where(sel, q, jnp.zeros_like(q))
                    dym = jnp.where(sel, dyb, jnp.zeros_like(dyb))
                    s = _mm_nt(qm, kb) + b_ref[p, hh, first]
                    pr = jnp.exp(s - lb[:, c0:c0 + 1])
                    dp = _mm_nt(dym, vb)
                    ds = pr * (dp - db[:, c0:c0 + 1])
                    db_ref[p, hh] += ds
                    dqs.append(_mm(ds, kb))
                    dkb = dkb + _mm_tn(ds, qm)
                    dvb = dvb + _mm_tn(pr, dym)
                dq_perm[pl.ds(r0, QB), :] = jnp.where(lane < 64, dqs[0], dqs[1])
                dk_perm[pl.ds(r0, 2 * QB), :] += dkb
                dv_perm[pl.ds(r0, 2 * QB), :] += dvb
                return carry

            lax.fori_loop(0, S // QB, blk, 0, unroll=4)
            for r in range(d):
                idx = pl.ds(r, L, stride=d) if d > 1 else pl.ds(0, S)
                dq_ref[idx, :] += dq_perm[r * L:(r + 1) * L, :] * 0.125
                dk_ref[idx, :] += dk_perm[QB + r * L:QB + (r + 1) * L, :]
                dv_ref[idx, :] += dv_perm[QB + r * L:QB + (r + 1) * L, :]

    col = lambda c0: pl.BlockSpec((S, 128), lambda hp: (0, c0 + hp))
    bspec = pl.BlockSpec((3, 2, 8, 2 * QB), lambda hp: (0, hp, 0, 0))
    return _call(
        body, name=name, grid=(N_HEADS // 2,),
        in_specs=[col(0), col(4), col(8), bspec, col(0), col(0), col(0)],
        out_specs=(col(0), col(0), col(0), pl.BlockSpec((3, 2, QB, 2 * QB), lambda hp: (0, hp, 0, 0))),
        out_shape=(_sds((S, D_ATT), F32), _sds((S, D_ATT), F32), _sds((S, D_ATT), F32),
                   _sds((3, N_HEADS, QB, 2 * QB), F32)),
        scratch=[pltpu.VMEM((S, 128), MXU), pltpu.VMEM((S + QB, 128), MXU), pltpu.VMEM((S + QB, 128), MXU),
                 pltpu.VMEM((S, 128), MXU), pltpu.VMEM((S, 128), F32), pltpu.VMEM((S, 128), F32),
                 pltpu.VMEM((S, 128), F32), pltpu.VMEM((S, 128), F32),
                 pltpu.VMEM((S + QB, 128), F32), pltpu.VMEM((S + QB, 128), F32),
                 pltpu.VMEM((3, 2, 2, QB, 2 * QB), F32)],
        cp=_cp(48, ("arbitrary",)), args=(z, z, z, bias, y, lse, dy), ride=ride)


def relbias_grad(dbiases):
    bucket, band, _ = _att_static()
    onehot = (bucket[:, None] == np.arange(N_BUCKETS)[None, :, None, None]) & band[:, None]
    onehot = jnp.asarray(onehot.reshape(3, N_BUCKETS, QB * 2 * QB), BF16)

    def body(db0_ref, db1_ref, oh_ref, o_ref):
        acc = jnp.zeros((N_HEADS, N_BUCKETS), F32)
        for p in range(3):
            acc = acc + lax.dot_general(db0_ref[p] + db1_ref[p], oh_ref[p].astype(F32), (((1,), (1,)), ((), ())),
                                        preferred_element_type=F32, precision=lax.Precision.HIGHEST)
        o_ref[...] = acc

    vm = pl.BlockSpec(memory_space=pltpu.VMEM)
    out = _pallas_call(body, name="relbias_grad", in_specs=[vm, vm, vm], out_specs=vm,
                         out_shape=_sds((N_HEADS, N_BUCKETS), F32), compiler_params=_cp(40))(
        *[d.reshape(3, N_HEADS, QB * 2 * QB) for d in dbiases], onehot)
    return out.T


def _panel(t_ref, ri, j):
    return t_ref[ri, pl.ds(j, S, stride=8), :]


def _gelu(x):
    c = math.sqrt(2.0 / math.pi)
    th = jnp.tanh(c * (x + 0.044715 * x * x * x))
    return 0.5 * x * (1.0 + th), th


def ssm_fwd(z, a, bre, bim, cre, cim, dsk, gluw, glub, name, ride=None):
    def body(u_ref, a_ref, bre_ref, bim_ref, cre_ref, cim_ref, d_ref, gw_ref, gb_ref, y_ref, yp_ref, st_hbm, st_ref):
        u = u_ref[...]
        for j in range(8):
            st_ref[0, pl.ds(j, S, stride=8), :] = _mm(u, bre_ref[:, 128 * j:128 * (j + 1)])
            st_ref[1, pl.ds(j, S, stride=8), :] = _mm(u, bim_ref[:, 128 * j:128 * (j + 1)])
        ar, ai = a_ref[0], a_ref[1]

        def step(t, c):
            re, im = c
            i = pl.multiple_of(t * 8, 8)
            nre = ar * re - ai * im + st_ref[0, pl.ds(i, 8), :]
            nim = ar * im + ai * re + st_ref[1, pl.ds(i, 8), :]
            st_ref[0, pl.ds(i, 8), :] = nre
            st_ref[1, pl.ds(i, 8), :] = nim
            return nre, nim

        zero = jnp.zeros((8, 128), F32)
        lax.fori_loop(0, S, step, (zero, zero), unroll=8)
        y = d_ref[...] * u
        for j in range(8):
            y = y + _mm(_panel(st_ref, 0, j), cre_ref[128 * j:128 * (j + 1), :])
            y = y - _mm(_panel(st_ref, 1, j), cim_ref[128 * j:128 * (j + 1), :])
        pltpu.sync_copy(st_ref, st_hbm)
        yp_ref[...] = y
        gl, _ = _gelu(y)
        tt = _mm(gl, gw_ref[...].reshape(D_SSM, D_SSM)) + gb_ref[...]
        y_ref[...] = y * jax.nn.sigmoid(tt)

    vm = lambda shape: pl.BlockSpec(shape, lambda i: (0,) * len(shape))
    return _call(
        body, name=name, grid=(1,),
        in_specs=[pl.BlockSpec((S, 256), lambda i: (0, 6)), vm((2, 8, 128)), vm((256, 1024)), vm((256, 1024)),
                  vm((1024, 256)), vm((1024, 256)), vm((1, 256)),
                  vm((NDEV, 32, 256)), vm((1, 256))],
        out_specs=(vm((S, 256)), vm((S, 256)), pl.BlockSpec(memory_space=pl.ANY)),
        out_shape=(_sds((S, 256), F32), _sds((S, 256), F32), _sds((2, S * 8, 128), F32)),
        scratch=[pltpu.VMEM((2, S * 8, 128), F32)],
        cp=_cp(40, ("arbitrary",)), args=(z, a, bre, bim, cre, cim, dsk, gluw, glub), ride=ride)


def ssm_bwd(dy, z, ypre, st, a, bre, bim, cre, cim, dsk, gluw, glub, name, ride=None):
    def body(dy_ref, u_ref, yp_ref, st_hbm, a_ref, bre_ref, bim_ref, cre_ref, cim_ref, d_ref, gw_ref, gb_ref,
             du_ref, dbre_ref, dbim_ref, dcre_ref, dcim_ref, da_ref, dd_ref, dgw_ref, dgb_ref, g_ref, st_ref):
        pltpu.sync_copy(st_hbm, st_ref)
        u = u_ref[...]
        y = yp_ref[...]
        dout = dy_ref[...]
        gw = gw_ref[...].reshape(D_SSM, D_SSM)
        gl, th = _gelu(y)
        sig = jax.nn.sigmoid(_mm(gl, gw) + gb_ref[...])
        dt = dout * y * sig * (1.0 - sig)
        dgw_ref[...] = _mm_tn(gl, dt)
        dgb_ref[...] = jnp.sum(dt, axis=0, keepdims=True)
        c = math.sqrt(2.0 / math.pi)
        dgelu = 0.5 * (1.0 + th) + 0.5 * y * (1.0 - th * th) * c * (1.0 + 3.0 * 0.044715 * y * y)
        dyv = dout * sig + _mm_nt(dt, gw) * dgelu
        dd_ref[...] = jnp.sum(dyv * u, axis=0, keepdims=True)
        for j in range(8):
            rows = slice(128 * j, 128 * (j + 1))
            g_ref[0, pl.ds(j, S, stride=8), :] = _mm_nt(dyv, cre_ref[rows, :])
            g_ref[1, pl.ds(j, S, stride=8), :] = -_mm_nt(dyv, cim_ref[rows, :])
            dcre_ref[rows, :] = _mm_tn(_panel(st_ref, 0, j), dyv)
            dcim_ref[rows, :] = -_mm_tn(_panel(st_ref, 1, j), dyv)
        ar, ai = a_ref[0], a_ref[1]

        def step(k, c4):
            gre, gim, dar, dai = c4
            i = pl.multiple_of((S - 1 - k) * 8, 8)
            nre = g_ref[0, pl.ds(i, 8), :] + ar * gre + ai * gim
            nim = g_ref[1, pl.ds(i, 8), :] + ar * gim - ai * gre
            g_ref[0, pl.ds(i, 8), :] = nre
            g_ref[1, pl.ds(i, 8), :] = nim
            sre = st_ref[0, pl.ds(i - 8, 8), :]
            sim = st_ref[1, pl.ds(i - 8, 8), :]
            return nre, nim, dar + nre * sre + nim * sim, dai + nim * sre - nre * sim

        zero = jnp.zeros((8, 128), F32)
        gre, gim, dar, dai = lax.fori_loop(0, S - 1, step, (zero, zero, zero, zero), unroll=8)
        g_ref[0, 0:8, :] = g_ref[0, 0:8, :] + ar * gre + ai * gim
        g_ref[1, 0:8, :] = g_ref[1, 0:8, :] + ar * gim - ai * gre
        da_ref[0] = dar
        da_ref[1] = dai
        du = dyv * d_ref[...]
        for j in range(8):
            cols = slice(128 * j, 128 * (j + 1))
            gr, gi = _panel(g_ref, 0, j), _panel(g_ref, 1, j)
            dbre_ref[:, cols] = _mm_tn(u, gr)
            dbim_ref[:, cols] = _mm_tn(u, gi)
            du = du + _mm_nt(gr, bre_ref[:, cols]) + _mm_nt(gi, bim_ref[:, cols])
        du_ref[...] = du

    vm = lambda shape: pl.BlockSpec(shape, lambda i: (0,) * len(shape))
    return _call(
        body, name=name, grid=(1,),
        in_specs=[vm((S, 256)), pl.BlockSpec((S, 256), lambda i: (0, 6)), vm((S, 256)), pl.BlockSpec(memory_space=pl.ANY),
                  vm((2, 8, 128)), vm((256, 1024)), vm((256, 1024)), vm((1024, 256)), vm((1024, 256)), vm((1, 256)),
                  vm((NDEV, 32, 256)), vm((1, 256))],
        out_specs=(vm((S, 256)), vm((256, 1024)), vm((256, 1024)), vm((1024, 256)), vm((1024, 256)),
                   vm((2, 8, 128)), vm((1, 256)), vm((256, 256)), vm((1, 256))),
        out_shape=(_sds((S, 256), F32), _sds((256, 1024), F32), _sds((256, 1024), F32), _sds((1024, 256), F32),
                   _sds((1024, 256), F32), _sds((2, 8, 128), F32), _sds((1, 256), F32), _sds((256, 256), F32),
                   _sds((1, 256), F32)),
        scratch=[pltpu.VMEM((2, S * 8, 128), F32), pltpu.VMEM((2, S * 8, 128), F32)],
        cp=_cp(56, ("arbitrary",)), args=(dy, z, ypre, st, a, bre, bim, cre, cim, dsk, gluw, glub), ride=ride)


def _ssm_discretise(a_re, a_im, log_dt, b_re, b_im):
    dt = jnp.exp(log_dt)[:, None]
    er = jnp.exp(a_re * dt)
    abr, abi = er * jnp.cos(a_im * dt), er * jnp.sin(a_im * dt)
    den = a_re * a_re + a_im * a_im
    fr = ((abr - 1.0) * a_re + abi * a_im) / den
    fi = (abi * a_re - (abr - 1.0) * a_im) / den
    bbr = fr[:, :, None] * b_re - fi[:, :, None] * b_im
    bbi = fr[:, :, None] * b_im + fi[:, :, None] * b_re
    return abr, abi, bbr, bbi


def _blockdiag(t):
    g, r, c = t.shape
    eye = jnp.eye(g, dtype=t.dtype)
    return (t[:, :, None, :] * eye[:, None, :, None]).reshape(g * r, g * c)


def _blockdiag_take(m, r, c):
    g = m.shape[0] // r
    idx = jnp.arange(g)
    return m.reshape(g, r, g, c)[idx, :, idx, :]


PAD = 16


def _pool_lane_select(vals):
    lane = lax.broadcasted_iota(jnp.int32, vals[0].shape, 1)
    out = vals[3]
    for g in (2, 1, 0):
        out = jnp.where(lane < 64 * (g + 1), vals[g], out)
    return out


def _pool_counts():
    row = lax.broadcasted_iota(jnp.int32, (S, D_POOL), 0).astype(F32) + 1.0
    return _pool_lane_select([jnp.minimum(row, float(w)) for w in POOL_WINDOWS])


def _pooled(u, sa, sb):
    sums = []
    cur = u
    bufs = (sa, sb)
    for k, sh in enumerate((1, 2, 4, 8)):
        buf = bufs[k % 2]
        buf[PAD:PAD + S, :] = cur
        cur = cur + buf[PAD - sh:PAD - sh + S, :]
        sums.append(cur)
    return _pool_lane_select(sums) / _pool_counts() - u


def pool_fwd(z, pw, psc, name):
    def body(u_ref, w_ref, s_ref, y_ref, sa, sb):
        for buf in (sa, sb):
            buf[0:PAD, :] = jnp.zeros((PAD, D_POOL), F32)
        pooled = _pooled(u_ref[...], sa, sb)
        y_ref[...] = _mm(pooled, w_ref[...]) * s_ref[...]

    vm = lambda shape: pl.BlockSpec(shape, lambda i: (0,) * len(shape))
    return _pallas_call(
        body, name=name, grid=(1,),
        in_specs=[pl.BlockSpec((S, 256), lambda i: (0, 7)), vm((256, 256)), vm((1, 256))],
        out_specs=vm((S, 256)), out_shape=_sds((S, 256), F32),
        scratch_shapes=[pltpu.VMEM((S + 2 * PAD, D_POOL), F32)] * 2,
        compiler_params=_cp(40, ("arbitrary",)))(z, pw, psc)


def pool_bwd(dy, z, pw, psc, name):
    def body(dy_ref, u_ref, w_ref, s_ref, du_ref, dw_ref, ds_ref, sa, sb):
        for buf in (sa, sb):
            buf[0:PAD, :] = jnp.zeros((PAD, D_POOL), F32)
            buf[PAD + S:PAD + S + PAD, :] = jnp.zeros((PAD, D_POOL), F32)
        pooled = _pooled(u_ref[...], sa, sb)
        dyv = dy_ref[...]
        w = w_ref[...]
        ds_ref[...] = jnp.sum(dyv * _mm(pooled, w), axis=0, keepdims=True)
        dyl = dyv * s_ref[...]
        dw_ref[...] = _mm_tn(pooled, dyl)
        dpool = _mm_nt(dyl, w)
        cur = dpool / _pool_counts()
        sums = []
        bufs = (sa, sb)
        for k, sh in enumerate((1, 2, 4, 8)):
            buf = bufs[k % 2]
            buf[PAD:PAD + S, :] = cur
            cur = cur + buf[PAD + sh:PAD + sh + S, :]
            sums.append(cur)
        du_ref[...] = _pool_lane_select(sums) - dpool

    vm = lambda shape: pl.BlockSpec(shape, lambda i: (0,) * len(shape))
    return _pallas_call(
        body, name=name, grid=(1,),
        in_specs=[vm((S, 256)), pl.BlockSpec((S, 256), lambda i: (0, 7)), vm((256, 256)), vm((1, 256))],
        out_specs=(vm((S, 256)), vm((256, 256)), vm((1, 256))),
        out_shape=(_sds((S, 256), F32), _sds((256, 256), F32), _sds((1, 256), F32)),
        scratch_shapes=[pltpu.VMEM((S + 2 * PAD, D_POOL), F32)] * 2,
        compiler_params=_cp(40, ("arbitrary",)))(dy, z, pw, psc)


def ada_fwd(c_all, ada_w, ada_b_cols):
    def body(c_ref, w_ref, b_ref, o_ref):
        c = c_ref[...]
        cond = c * jax.nn.sigmoid(c)
        o_ref[...] = jnp.dot(cond, w_ref[...], preferred_element_type=F32, precision=lax.Precision.HIGHEST) + b_ref[...]

    return _pallas_call(
        body, name="ada_fwd", grid=(DEPTH,),
        in_specs=[pl.BlockSpec((NDEV, D), lambda l: (0, 0)), pl.BlockSpec((None, D, 1152), lambda l: (l, 0, 0)),
                  pl.BlockSpec((None, 1, 1152), lambda l: (l, 0, 0))],
        out_specs=pl.BlockSpec((None, NDEV, 1152), lambda l: (l, 0, 0)), out_shape=_sds((DEPTH, NDEV, 1152), F32),
        compiler_params=_cp(40, ("arbitrary",)))(c_all, ada_w, ada_b_cols)


def ada_bwd(c_all, dmod_cols):
    def body(c_ref, dm_ref, o_ref):
        c = c_ref[...]
        cond = c * jax.nn.sigmoid(c)
        o_ref[...] = lax.dot_general(cond, dm_ref[...], (((0,), (0,)), ((), ())), preferred_element_type=F32,
                                     precision=lax.Precision.HIGHEST)

    return _pallas_call(
        body, name="ada_bwd", grid=(DEPTH,),
        in_specs=[pl.BlockSpec((NDEV, D), lambda l: (0, 0)), pl.BlockSpec((None, NDEV, 1152), lambda l: (l, 0, 0))],
        out_specs=pl.BlockSpec((None, D, 1152), lambda l: (l, 0, 0)), out_shape=_sds((DEPTH, D, 1152), F32),
        compiler_params=_cp(40, ("arbitrary",)))(c_all, dmod_cols)


def _adamw(w, g, m, v):
    m2 = B1 * m + (1.0 - B1) * g
    v2 = B2 * v + (1.0 - B2) * (g * g)
    m_hat = m2 / (1.0 - B1 ** STEP)
    v_hat = v2 / (1.0 - B2 ** STEP)
    return -LR * (m_hat / (jnp.sqrt(v_hat) + EPS) + WD * w), m2, v2


def _sum8(ref):
    g = ref[0].astype(F32)
    for s in range(1, NDEV):
        g = g + ref[s].astype(F32)
    return g


def adam_rs(recv, w, m, v, tr, name, ride=None):
    lead, (r, cdim) = w.shape[:-2], w.shape[-2:]
    cp = recv.shape[-1]
    nl = len(lead)

    def body(rc_ref, w_ref, m_ref, v_ref, g_ref, d_ref, m2_ref, v2_ref):
        g = _sum8(rc_ref)[:, :cdim]
        g_ref[...] = g
        d_ref[...], m2_ref[...], v2_ref[...] = _adamw(w_ref[...], g, m_ref[...], v_ref[...])

    rs = pl.BlockSpec((None,) * nl + (tr, cdim), lambda *i: (*i, 0))
    return _call(
        body, name=name, grid=lead + (r // tr,),
        in_specs=[pl.BlockSpec((NDEV,) + (None,) * nl + (tr, cp), lambda *i: (0, *i, 0)), rs, rs, rs],
        out_specs=(rs, rs, rs, rs), out_shape=tuple(_sds(w.shape, F32) for _ in range(4)),
        cp=_cp(48, ("arbitrary",) * (nl + 1)), args=(recv, w, m, v), ride=ride)


def adam_block(recv, w, m, v, lf, prev, name):
    half = FB // 2

    def body(*refs):
        rc_ref, w_ref, m_ref, v_ref = refs[:4]
        g_ref, d_ref, m2_ref, v2_ref = refs[-4:]
        g = _sum8(rc_ref)
        g_ref[...] = g
        d_ref[...], m2_ref[...], v2_ref[...] = _adamw(w_ref[...], g, m_ref[...], v_ref[...])

    rs = pl.BlockSpec((None, None, half, D), lambda i: (lf // 2, lf % 2, i, 0))
    prev = list(prev) if prev is not None else []
    return list(_pallas_call(
        body, name=name, grid=(2,), in_specs=[pl.BlockSpec((NDEV, half, D), lambda i: (0, i, 0)), rs, rs, rs] + [ANY] * len(prev),
        out_specs=(rs, rs, rs, rs), out_shape=tuple(_sds((DEPTH, 2, FB, D), F32) for _ in range(4)),
        input_output_aliases={4 + k: k for k in range(len(prev))},
        compiler_params=_cp(48, ("arbitrary",)))(recv, w, m, v, *prev))


def adam_plain(g, w, m, v, tr, name, ride=None):
    lead, (r, cdim) = w.shape[:-2], w.shape[-2:]
    nl = len(lead)

    def body(g_ref, w_ref, m_ref, v_ref, d_ref, m2_ref, v2_ref):
        d_ref[...], m2_ref[...], v2_ref[...] = _adamw(w_ref[...], g_ref[...], m_ref[...], v_ref[...])

    rs = pl.BlockSpec((None,) * nl + (tr, cdim), lambda *i: (*i, 0))
    return _call(
        body, name=name, grid=lead + (r // tr,), in_specs=[rs, rs, rs, rs], out_specs=(rs, rs, rs),
        out_shape=tuple(_sds(w.shape, F32) for _ in range(3)),
        cp=_cp(48, ("arbitrary",) * (nl + 1)), args=(g, w, m, v), ride=ride)


def adam_native(gs, ws, ms, vs, name):
    n = len(ws)

    def body(*refs):
        g_refs, w_refs, m_refs, v_refs = (refs[k * n:(k + 1) * n] for k in range(4))
        d_refs, m2_refs, v2_refs = (refs[(4 + k) * n:(5 + k) * n] for k in range(3))
        for a in range(n):
            d_refs[a][...], m2_refs[a][...], v2_refs[a][...] = _adamw(w_refs[a][...], g_refs[a][...], m_refs[a][...], v_refs[a][...])

    vm = pl.BlockSpec(memory_space=pltpu.VMEM)
    outs = _pallas_call(body, name=name, in_specs=[vm] * (4 * n), out_specs=tuple([vm] * (3 * n)),
                        out_shape=tuple(_sds(w.shape, F32) for w in ws) * 3, compiler_params=_cp(40))(*gs, *ws, *ms, *vs)
    return outs[:n], outs[n:2 * n], outs[2 * n:]


def sum_sources(recv, name):
    r = recv.shape[1]

    def body(rc_ref, o_ref):
        o_ref[...] = _sum8(rc_ref)

    vm = pl.BlockSpec(memory_space=pltpu.VMEM)
    return _pallas_call(body, name=name, in_specs=[vm], out_specs=vm, out_shape=_sds((r, 128), F32),
                          compiler_params=_cp(40))(recv)


def _pack(arrs):
    flat = jnp.concatenate([a.reshape(-1) for a in arrs])
    n = flat.shape[0]
    rows = -(-n // 1024) * 8
    return jnp.pad(flat, (0, rows * 128 - n)).reshape(rows, 128)


def _unpack(vec, shapes):
    flat = vec.reshape(-1)
    out, o = [], 0
    for sh in shapes:
        n = int(np.prod(sh))
        out.append(flat[o:o + n].reshape(sh))
        o += n
    return out


WEIGHTS = ['rel_bias', 'ada_w', 'ada_b', 'ln_g', 'ln_b', 'ffn_w_gate', 'ffn_w_up', 'ffn_w_down', 'w_in', 'w_out',
           'ssm_a_re', 'ssm_a_im', 'ssm_log_dt', 'ssm_b_re', 'ssm_b_im', 'ssm_c_re', 'ssm_c_im', 'ssm_d', 'glu_w',
           'glu_b', 'pool_w', 'pool_scale']
SMALL = ['rel_bias', 'ada_b', 'ln_g', 'ln_b', 'ssm_a_re', 'ssm_a_im', 'ssm_log_dt', 'ssm_b_re', 'ssm_b_im',
         'ssm_c_re', 'ssm_c_im', 'ssm_d', 'glu_b', 'pool_w', 'pool_scale']
SMALL_FULL_SHAPES = {'rel_bias': (32, 8), 'ada_b': (2, 9216), 'ln_g': (2, 3, 1024), 'ln_b': (2, 3, 1024),
                     'ssm_a_re': (2, 16, 64), 'ssm_a_im': (2, 16, 64), 'ssm_log_dt': (2, 16),
                     'ssm_b_re': (2, 16, 64, 16), 'ssm_b_im': (2, 16, 64, 16), 'ssm_c_re': (2, 16, 16, 64),
                     'ssm_c_im': (2, 16, 16, 64), 'ssm_d': (2, 256), 'glu_b': (2, 256), 'pool_w': (2, 4, 64, 64),
                     'pool_scale': (2, 256)}


def _step(P):
    me = _me()
    x0 = P['x'][0]
    target = P['loss_target'][0]

    def shards(l, sub):
        bf = lambda a: a.astype(BF16)
        if sub == 1:
            return [bf(P['w_in'][l]), bf(P['w_out'][l]), bf(P['glu_w'][l])]
        f = sub // 2
        padr = lambda a: jnp.pad(bf(a), ((0, FBP - FB), (0, 0)))
        return [padr(P['ffn_w_gate'][l, f].T), padr(P['ffn_w_up'][l, f].T), padr(P['ffn_w_down'][l, f])]

    order = [(l, sub) for l in range(DEPTH) for sub in range(3)]
    nxt = dict(zip(order[:-1], order[1:]))
    W = {key: [None] * 3 for key in order}
    c_all, lng_all, lnb_all, *W[order[0]] = _exchange(Gather([P['c'], P['ln_g'], P['ln_b']] + shards(*order[0])), "gather_first")
    gather_queue = [(key, pos, a) for key in order[1:] for pos, a in enumerate(shards(*key))]

    def gather_ride(cap_us, must=None):
        units, used = [], 0.0
        while gather_queue:
            key, _, a = gather_queue[0]
            cost = a.size * a.dtype.itemsize * GATHER_US_PER_BYTE
            if key != must and used + cost / 2 > cap_us:
                break
            units.append(gather_queue.pop(0))
            used += cost
        return (Gather([a for _, _, a in units]) if units else None), units

    def gathered(units, outs):
        for (key, pos, _), o in zip(units, outs):
            W[key][pos] = o

    c_all = c_all.reshape(NDEV, D)
    ln_g = jnp.transpose(lng_all, (1, 2, 0, 3)).reshape(DEPTH, 3, D)
    ln_b = jnp.transpose(lnb_all, (1, 2, 0, 3)).reshape(DEPTH, 3, D)

    ada_b_cols = lax.dynamic_slice_in_dim(P['ada_b'], me * 1152, 1152, axis=1).reshape(DEPTH, 1, 1152)
    modc = ada_fwd(c_all, P['ada_w'], ada_b_cols)
    (mod_all,) = _exchange(Gather([modc]), "gather_mod")
    mod_me = lax.dynamic_index_in_dim(mod_all, me, axis=2, keepdims=False)
    mod = jnp.transpose(mod_me, (1, 0, 2)).reshape(DEPTH, 9, D)

    bias = att_bias(P['rel_bias'])
    ssm = []
    for l in range(DEPTH):
        prm = (P['ssm_a_re'][l], P['ssm_a_im'][l], P['ssm_log_dt'][l], P['ssm_b_re'][l], P['ssm_b_im'][l])
        (abr, abi, bbr, bbi), disc_vjp = jax.vjp(_ssm_discretise, *prm)
        ssm.append(dict(
            vjp=disc_vjp, a=jnp.stack([abr.reshape(8, 128), abi.reshape(8, 128)]),
            bre=_blockdiag(jnp.transpose(bbr, (0, 2, 1))).astype(MXU), bim=_blockdiag(jnp.transpose(bbi, (0, 2, 1))).astype(MXU),
            cre=_blockdiag(jnp.transpose(P['ssm_c_re'][l], (0, 2, 1))).astype(MXU),
            cim=_blockdiag(jnp.transpose(P['ssm_c_im'][l], (0, 2, 1))).astype(MXU),
            d=P['ssm_d'][l].reshape(1, 256), gb=P['glu_b'][l].reshape(1, 256),
            pw=_blockdiag(P['pool_w'][l]).astype(MXU), psc=P['pool_scale'][l].reshape(1, 256)))

    saved = []
    x = x0
    h = ln_mod_fwd(x, mod[0], 0, "ln_mod_fwd_l0s0")
    for l, sub in order:
        tag = f"l{l}s{sub}"
        after = (mod[nxt[(l, sub)][0]], nxt[(l, sub)][1]) if (l, sub) in nxt else None
        if sub != 1:
            wg, wu, wd = (t.reshape(NDEV * FBP, D) for t in W[(l, sub)])
            ride, units = gather_ride(60, nxt.get((l, sub)))
            (G, U, fo), got = ffn_fwd(h, wg, wu, wd, "ffn_fwd_" + tag, ride)
            gathered(units, got)
            saved.append(dict(x=x, h=h, G=G, U=U, f=fo))
            x, *hn = res_ln_fwd(x, fo, mod[l], sub, ln_g[l], ln_b[l], 0.5, "res_ln_fwd_" + tag, after)
        else:
            sp = ssm[l]
            win, wout, gluw = W[(l, sub)]
            ride, units = gather_ride(15)
            (z,), got = win_fwd(h, win, "win_fwd_" + tag, ride)
            gathered(units, got)
            ride, units = gather_ride(55)
            (ya, lse), got = att_fwd(z, bias, "att_fwd_" + tag, ride)
            gathered(units, got)
            ride, units = gather_ride(35)
            (ys, ypre, st), got = ssm_fwd(z, sp['a'], sp['bre'], sp['bim'], sp['cre'], sp['cim'], sp['d'], gluw, sp['gb'],
                                          "ssm_fwd_" + tag, ride)
            gathered(units, got)
            yp = pool_fwd(z, sp['pw'], sp['psc'], "pool_fwd_" + tag)
            ride, units = gather_ride(12, nxt.get((l, sub)))
            (o,), got = wout_fwd(ya, ys, yp, wout, "wout_fwd_" + tag, ride)
            gathered(units, got)
            saved.append(dict(x=x, h=h, z=z, ya=ya, lse=lse, ys=ys, ypre=ypre, st=st, yp=yp, f=o))
            x, *hn = res_ln_fwd(x, o, mod[l], sub, ln_g[l], ln_b[l], 1.0, "res_ln_fwd_" + tag, after)
        h = hn[0] if hn else None
    assert not gather_queue

    loss_tile, dx = loss_fwd_bwd(x, target, "loss")
    loss = lax.psum(loss_tile[0, 0], ("x", "y", "c"))

    flights = []

    def hosted(cap_us, fn, *args):
        return fn(*args, None)[0]

    dmod = [[None] * 9 for _ in range(DEPTH)]
    dlng = [[None] * 3 for _ in range(DEPTH)]
    dlnb = [[None] * 3 for _ in range(DEPTH)]
    dbiases = [None] * DEPTH
    small_l = [dict() for _ in range(DEPTH)]
    for l, sub in reversed(order):
        tag = f"l{l}s{sub}"
        sv = saved[3 * l + sub]
        w = 1.0 if sub == 1 else 0.5
        dxa, df, sums = hosted(0, res_ln_bwd, sv['x'], sv['f'], mod[l], sub, ln_g[l], dx, w, "res_ln_bwd_" + tag)
        dlng[l][sub], dlnb[l][sub], dmod[l][3 * sub + 2] = sums[0], sums[1], sums[2]
        if sub != 1:
            f = sub // 2
            wg, wu, wd = (t.reshape(NDEV * FBP, D) for t in W[(l, sub)])
            dwg, dwu, dwd, dh = hosted(105, ffn_bwd, df, sv['h'], sv['G'], sv['U'], wg, wu, wd, "ffn_bwd_" + tag)
            flights.append(((l, sub), scatter_start([t.reshape(NDEV, FBP, D) for t in (dwg, dwu, dwd)], "scatter_start_" + tag)))
        else:
            sp = ssm[l]
            win, wout, gluw = W[(l, sub)]
            dya, dys, dyp, dwout = hosted(0, wout_bwd, df, sv['ya'], sv['ys'], sv['yp'], wout, "wout_bwd_" + tag)
            dq, dk, dv, dbiases[l] = hosted(115, att_bwd, sv['z'], bias, sv['ya'], sv['lse'], dya, "att_bwd_" + tag)
            dus, dbre, dbim, dcre, dcim, da, dd, dgw, dgb = hosted(
                52, ssm_bwd, dys, sv['z'], sv['ypre'], sv['st'], sp['a'], sp['bre'], sp['bim'], sp['cre'], sp['cim'], sp['d'],
                gluw, sp['gb'], "ssm_bwd_" + tag)
            dup, dpw, dpsc = pool_bwd(dyp, sv['z'], sp['pw'], sp['psc'], "pool_bwd_" + tag)
            dh, dwin = hosted(40, win_bwd, (dq, dk, dv, dus, dup), sv['h'], win, "win_bwd_" + tag)
            flights.append(((l, sub), scatter_start([dwin, dwout, dgw.astype(BF16).reshape(NDEV, 32, 256)], "scatter_start_" + tag)))
            d_are, d_aim, d_ldt, d_bre, d_bim = sp['vjp']((
                da[0].reshape(16, 64), da[1].reshape(16, 64),
                jnp.transpose(_blockdiag_take(dbre, 16, 64), (0, 2, 1)), jnp.transpose(_blockdiag_take(dbim, 16, 64), (0, 2, 1))))
            small_l[l] = dict(
                ssm_a_re=d_are, ssm_a_im=d_aim, ssm_log_dt=d_ldt, ssm_b_re=d_bre, ssm_b_im=d_bim,
                ssm_c_re=jnp.transpose(_blockdiag_take(dcre, 64, 16), (0, 2, 1)),
                ssm_c_im=jnp.transpose(_blockdiag_take(dcim, 64, 16), (0, 2, 1)),
                ssm_d=dd.reshape(256), glu_b=dgb.reshape(256), pool_w=_blockdiag_take(dpw, 64, 64), pool_scale=dpsc.reshape(256))
        dx, sums = hosted(0, ln_mod_bwd, sv['x'], dh, mod[l], sub, dxa, "ln_mod_bwd_" + tag)
        dmod[l][3 * sub], dmod[l][3 * sub + 1] = sums[0], sums[1]
    grad_x = dx[None]

    out = {}

    def put(name, g, d, m2, v2, shape):
        out['grad_' + name], out['delta_' + name] = g.reshape(shape), d.reshape(shape)
        out['new_m_' + name], out['new_v_' + name] = m2.reshape(shape), v2.reshape(shape)

    def wmv(name):
        return [P[pre + name] for pre in ('', 'm_', 'v_')]

    recv = {}
    for key, handle in flights[:-1]:
        recv[key] = scatter_wait(handle, dx, "scatter_wait_l%ds%d" % key)
    for pos, (name, tr) in enumerate((('w_in', 512), ('w_out', 128), ('glu_w', 32))):
        both = jnp.stack([recv[(l, 1)][pos] for l in range(DEPTH)], axis=1)
        put(name, *adam_rs(both, *wmv(name), tr, "adam_" + name)[0], P[name].shape)
    ffn = (('ffn_w_gate', [jnp.swapaxes(t, 2, 3) for t in wmv('ffn_w_gate')]),
           ('ffn_w_up', [jnp.swapaxes(t, 2, 3) for t in wmv('ffn_w_up')]), ('ffn_w_down', wmv('ffn_w_down')))
    part = [None] * 3
    for l, sub in [key for key, _ in flights[:-1] if key[1] != 1]:
        for pos, (name, ops) in enumerate(ffn):
            part[pos] = adam_block(recv[(l, sub)][pos], *ops, 2 * l + sub // 2, part[pos], f"adam_{name}_l{l}s{sub}")
    (l, sub), handle = flights[-1]
    last = scatter_wait(handle, part[2][0], "scatter_wait_l%ds%d" % (l, sub))
    for pos, (name, ops) in enumerate(ffn):
        res = adam_block(last[pos], *ops, 2 * l + sub // 2, part[pos], f"adam_{name}_l{l}s{sub}")
        put(name, *([jnp.swapaxes(t, 2, 3) for t in res] if pos < 2 else res), P[name].shape)

    small = {k: jnp.stack([small_l[l][k] for l in range(DEPTH)]) for k in small_l[0]}
    small['rel_bias'] = relbias_grad(dbiases)
    small['ada_b'] = jnp.stack([jnp.stack(dmod[l]).reshape(9 * D) for l in range(DEPTH)])
    small['ln_g'] = jnp.stack([jnp.stack(dlng[l]) for l in range(DEPTH)])
    small['ln_b'] = jnp.stack([jnp.stack(dlnb[l]) for l in range(DEPTH)])
    (small_all,) = _exchange(Gather([_pack([small[k] for k in SMALL])]), "gather_small")
    gsum = dict(zip(SMALL, _unpack(sum_sources(small_all, "sum_small"), [SMALL_FULL_SHAPES[k] for k in SMALL])))
    off = 256
    dmod_all = small_all.reshape(NDEV, -1)[:, off:off + DEPTH * 9 * D].reshape(NDEV, DEPTH, 9 * D)
    dmod_cols = jnp.transpose(lax.dynamic_slice_in_dim(dmod_all, me * 1152, 1152, axis=2), (1, 0, 2))
    g_ada_w = ada_bwd(c_all, dmod_cols)

    put('ada_w', g_ada_w, *adam_plain(g_ada_w, *wmv('ada_w'), 256, "adam_ada_w")[0], P['ada_w'].shape)

    for k in ('ln_g', 'ln_b'):
        gsum[k] = lax.dynamic_slice_in_dim(gsum[k], me * 128, 128, axis=2)
    swaps = {'rel_bias': (0, 1), 'ln_g': (0, 1), 'ln_b': (0, 1), 'ssm_b_re': (2, 3), 'ssm_b_im': (2, 3)}
    view = lambda k, t: jnp.swapaxes(t, *swaps[k]) if k in swaps else t
    ds_, m2s, v2s = adam_native(*[[view(k, src(k)) for k in SMALL] for src in
                                  (lambda k: gsum[k], lambda k: P[k], lambda k: P['m_' + k], lambda k: P['v_' + k])],
                                "adam_small")
    for k, d, m2, v2 in zip(SMALL, ds_, m2s, v2s):
        put(k, gsum[k], view(k, d), view(k, m2), view(k, v2), P[k].shape)

    res = [loss, grad_x]
    for pre in ('grad_', 'delta_', 'new_m_', 'new_v_'):
        res += [out[pre + k] for k in WEIGHTS]
    return tuple(res)


def kernel(x, c, rel_bias, ada_w, ada_b, ln_g, ln_b, ffn_w_gate, ffn_w_up, ffn_w_down, w_in, w_out, ssm_a_re, ssm_a_im, ssm_log_dt, ssm_b_re, ssm_b_im, ssm_c_re, ssm_c_im, ssm_d, glu_w, glu_b, pool_w, pool_scale, loss_target, m_rel_bias, m_ada_w, m_ada_b, m_ln_g, m_ln_b, m_ffn_w_gate, m_ffn_w_up, m_ffn_w_down, m_w_in, m_w_out, m_ssm_a_re, m_ssm_a_im, m_ssm_log_dt, m_ssm_b_re, m_ssm_b_im, m_ssm_c_re, m_ssm_c_im, m_ssm_d, m_glu_w, m_glu_b, m_pool_w, m_pool_scale, v_rel_bias, v_ada_w, v_ada_b, v_ln_g, v_ln_b, v_ffn_w_gate, v_ffn_w_up, v_ffn_w_down, v_w_in, v_w_out, v_ssm_a_re, v_ssm_a_im, v_ssm_log_dt, v_ssm_b_re, v_ssm_b_im, v_ssm_c_re, v_ssm_c_im, v_ssm_d, v_glu_w, v_glu_b, v_pool_w, v_pool_scale):
    return _step(dict(locals()))
```

```python
import functools
import math

import numpy as np
import jax
import jax.numpy as jnp
from jax import lax
from jax.experimental import pallas as pl
from jax.experimental.pallas import tpu as pltpu

F32 = jnp.float32
BF16 = jnp.bfloat16
MXU = jnp.bfloat16

S = 2048
D = 1024
NDEV = 8
DEPTH = 2
D_ATT, D_SSM, D_POOL, D_IN = 512, 256, 256, 2048
N_HEADS = 8
FB = 352
FBP = 384
QB = 128
PATTERNS = ((128, 1), (512, 4), (2048, 16))
POOL_WINDOWS = (2, 4, 8, 16)
N_BUCKETS, MAX_DISTANCE = 32, 2048
ALPHA = (2 * DEPTH) ** 0.25
LN_EPS = 1e-5
NEG = -1e30
GATHER_US_PER_BYTE = 43e-6
SCATTER_US_PER_BYTE = 21.6e-6
LR, B1, B2, EPS, WD, STEP = 0.001, 0.9, 0.999, 1e-08, 0.01, 10

TM = 256
TMM = 512
MIB = 1024 * 1024


def _cp(vmem_mib, sem=None):
    kw = dict(vmem_limit_bytes=vmem_mib * MIB)
    if sem is not None:
        kw["dimension_semantics"] = sem
    return pltpu.CompilerParams(**kw)


def _sds(shape, dtype):
    return jax.ShapeDtypeStruct(shape, dtype)


def _mm(a, b):
    return jnp.dot(a.astype(MXU), b.astype(MXU), preferred_element_type=F32)


def _mm_nt(a, b):
    return lax.dot_general(a.astype(MXU), b.astype(MXU), (((1,), (1,)), ((), ())), preferred_element_type=F32)


def _mm_tn(a, b):
    return lax.dot_general(a.astype(MXU), b.astype(MXU), (((0,), (0,)), ((), ())), preferred_element_type=F32)


def _ln_stats(x):
    mu = jnp.mean(x, axis=-1, keepdims=True)
    xc = x - mu
    var = jnp.mean(xc * xc, axis=-1, keepdims=True)
    rstd = lax.rsqrt(var + LN_EPS)
    return xc * rstd, rstd


def _ln_bwd(dn, n, rstd):
    return rstd * (dn - jnp.mean(dn, axis=-1, keepdims=True) - n * jnp.mean(dn * n, axis=-1, keepdims=True))


def _me():
    return 4 * lax.axis_index("x") + 2 * lax.axis_index("y") + lax.axis_index("c")


ANY = pl.BlockSpec(memory_space=pl.ANY)
PIN_BYTES = 1 << 19


def _pallas_call(*a, **k):
    big = lambda o: math.prod(o.shape) * o.dtype.itemsize >= PIN_BYTES
    pin = lambda o: pltpu.HBM(o.shape, o.dtype) if isinstance(o, jax.ShapeDtypeStruct) and big(o) else o
    osh = k["out_shape"]
    k["out_shape"] = tuple(pin(o) for o in osh) if isinstance(osh, (tuple, list)) else pin(osh)
    fn = pl.pallas_call(*a, **k)

    def run(*args):
        return fn(*[pltpu.with_memory_space_constraint(x, pltpu.HBM) if big(x) else x for x in args])
    return run


class Gather:
    def __init__(self, srcs):
        self.srcs = list(srcs)
        self.n = len(self.srcs)
        self.bufs = []
        self.out_shapes = [_sds((NDEV,) + a.shape, a.dtype) for a in self.srcs]
        self.sems = [pltpu.SemaphoreType.DMA((7 * self.n,)), pltpu.SemaphoreType.DMA((7 * self.n,)),
                     pltpu.SemaphoreType.DMA((self.n,))]

    def _parts(self, srcs, outs, sems):
        send_sems, recv_sems, loc_sems = sems
        x, y, c = lax.axis_index("x"), lax.axis_index("y"), lax.axis_index("c")
        me, sib = (x, y, c), (x, y, 1 - c)
        chips = [(1 - x, y), (x, 1 - y), (1 - x, 1 - y)]
        slot = lambda d: 4 * d[0] + 2 * d[1] + d[2]

        def copy(a, k, block, to, src=None):
            dst = outs[a].at[slot(block)]
            return pltpu.make_async_remote_copy(
                src_ref=dst if src is None else src, dst_ref=dst,
                send_sem=send_sems.at[7 * a + k], recv_sem=recv_sems.at[7 * a + k],
                device_id=to, device_id_type=pl.DeviceIdType.MESH)

        local = [pltpu.make_async_copy(srcs[a], outs[a].at[slot(me)], loc_sems.at[a]) for a in range(self.n)]
        return me, sib, chips, c, copy, local

    def start(self, srcs, bufs, outs, sems):
        me, sib, chips, c, copy, local = self._parts(srcs, outs, sems)
        for a in range(self.n):
            local[a].start()
            copy(a, 0, me, sib, src=srcs[a]).start()
            for j, chip in enumerate(chips):
                copy(a, 1 + j, me, (*chip, c), src=srcs[a]).start()

    def finish(self, srcs, bufs, outs, sems):
        me, sib, chips, c, copy, local = self._parts(srcs, outs, sems)
        for a in range(self.n):
            for j, chip in enumerate(chips):
                copy(a, 1 + j, (*chip, c), me).wait_recv()
                copy(a, 4 + j, (*chip, c), sib).start()
        for a in range(self.n):
            copy(a, 0, sib, me).wait_recv()
            copy(a, 0, me, sib, src=srcs[a]).wait_send()
            for j, chip in enumerate(chips):
                copy(a, 4 + j, (*chip, 1 - c), me).wait_recv()
                copy(a, 1 + j, me, (*chip, c), src=srcs[a]).wait_send()
                copy(a, 4 + j, (*chip, c), sib).wait_send()
            local[a].wait()


class Scatter:
    def __init__(self, items, bufs):
        self.items = list(items)
        self.keys = list(dict.fromkeys(key for _, key, _, _ in self.items))
        self.srcs = [src for src, _, _, _ in self.items]
        self.bufs = [bufs[key] for key in self.keys]
        self.n = len(self.srcs)
        self.out_shapes = [_sds(b.shape, b.dtype) for b in self.bufs]
        pairs = [(a, k) for a, (_, _, _, ks) in enumerate(self.items) for k in ks]
        self.remote_pairs = [p for p in pairs if p[1] != 0]
        self.local_pairs = [p for p in pairs if p[1] == 0]
        self.sems = [pltpu.SemaphoreType.DMA((max(len(self.remote_pairs), 1),)),
                     pltpu.SemaphoreType.DMA((max(len(self.remote_pairs), 1),)),
                     pltpu.SemaphoreType.DMA((max(len(self.local_pairs), 1),))]

    def _copies(self, srcs, outs, sems):
        send_sems, recv_sems, loc_sems = sems
        me = _me()

        def dst(a, slot):
            _, key, index, _ = self.items[a]
            return outs[self.keys.index(key)].at[(slot,) + tuple(index)]

        def remote(n, slot):
            a, k = self.remote_pairs[n]
            t = me ^ k
            return pltpu.make_async_remote_copy(
                src_ref=srcs[a].at[t], dst_ref=dst(a, slot), send_sem=send_sems.at[n], recv_sem=recv_sems.at[n],
                device_id=(t // 4, (t // 2) % 2, t % 2), device_id_type=pl.DeviceIdType.MESH)

        local = [pltpu.make_async_copy(srcs[a].at[me], dst(a, me), loc_sems.at[n])
                 for n, (a, _) in enumerate(self.local_pairs)]
        return me, remote, local

    def start(self, srcs, bufs, outs, sems):
        me, remote, local = self._copies(srcs, outs, sems)
        for cp in local:
            cp.start()
        for n in range(len(self.remote_pairs)):
            remote(n, me).start()

    def finish(self, srcs, bufs, outs, sems):
        me, remote, local = self._copies(srcs, outs, sems)
        for n, (_, k) in enumerate(self.remote_pairs):
            remote(n, me ^ k).wait()
        for cp in local:
            cp.wait()


def _call(body, *, name, grid, in_specs, out_specs, out_shape, args, scratch=(), cp=None, ride=None):
    out_specs, out_shape, scratch = list(out_specs), list(out_shape), list(scratch)
    if ride is None:
        outs = _pallas_call(body, name=name, grid=grid, in_specs=list(in_specs), out_specs=tuple(out_specs),
                              out_shape=tuple(out_shape), scratch_shapes=scratch, compiler_params=cp)(*args)
        return list(outs), []
    nin, nout, nscr, n, nb, no = len(in_specs), len(out_specs), len(scratch), ride.n, len(ride.bufs), len(ride.out_shapes)
    steps = list(grid)

    def wrapped(*refs):
        h_in, r_src, r_buf = refs[:nin], refs[nin:nin + n], refs[nin + n:nin + n + nb]
        o0 = nin + n + nb
        h_out, r_out = refs[o0:o0 + nout], refs[o0 + nout:o0 + nout + no]
        s0 = o0 + nout + no
        h_scr, sems = refs[s0:s0 + nscr], refs[s0 + nscr:]
        ids = [pl.program_id(a) for a in range(len(steps))]
        first = functools.reduce(jnp.logical_and, [i == 0 for i in ids])
        last = functools.reduce(jnp.logical_and, [i == s - 1 for i, s in zip(ids, steps)])

        @pl.when(first)
        def _():
            ride.start(r_src, r_buf, r_out, sems)

        body(*h_in, *h_out, *h_scr)

        @pl.when(last)
        def _():
            ride.finish(r_src, r_buf, r_out, sems)

    aliases = {nin + n + k: nout + k for k in range(nb)}
    outs = _pallas_call(
        wrapped, name=name, grid=grid, in_specs=list(in_specs) + [ANY] * (n + nb),
        out_specs=tuple(out_specs + [ANY] * no), out_shape=tuple(out_shape + ride.out_shapes),
        scratch_shapes=scratch + ride.sems, input_output_aliases=aliases, compiler_params=cp,
    )(*args, *ride.srcs, *ride.bufs)
    return list(outs[:nout]), list(outs[nout:])


def _exchange(ride, name):
    def body(dummy_ref, o_ref):
        o_ref[...] = dummy_ref[...]

    one = pl.BlockSpec((8, 128), lambda i: (0, 0))
    _, outs = _call(body, name=name, grid=(1,), in_specs=[one], out_specs=[one], out_shape=[_sds((8, 128), F32)],
                    args=(jnp.zeros((8, 128), F32),), ride=ride)
    return outs


HBM = pl.BlockSpec(memory_space=pltpu.HBM)
SEM = pl.BlockSpec(memory_space=pltpu.SEMAPHORE)


def _scatter_copies(srcs, lands, sems):
    send_sems, recv_sems, loc_sems = sems
    me = _me()

    def remote(a, k, slot):
        t = me ^ k
        return pltpu.make_async_remote_copy(
            src_ref=srcs[a].at[t], dst_ref=lands[a].at[slot], send_sem=send_sems.at[7 * a + k - 1],
            recv_sem=recv_sems.at[7 * a + k - 1], device_id=(t // 4, (t // 2) % 2, t % 2), device_id_type=pl.DeviceIdType.MESH)

    local = [pltpu.make_async_copy(srcs[a].at[me], lands[a].at[me], loc_sems.at[a]) for a in range(len(srcs))]
    return me, remote, local


def scatter_start(payloads, name):
    n = len(payloads)

    def body(*refs):
        srcs, lands, sems = refs[:n], refs[n:2 * n], refs[2 * n:2 * n + 3]
        me, remote, local = _scatter_copies(srcs, lands, sems)
        for a in range(n):
            local[a].start()
            for k in range(1, NDEV):
                remote(a, k, me).start()
        refs[-1][...] = jnp.zeros((8, 128), F32)

    thru = [pltpu.HBM(p.shape, p.dtype) for p in payloads]
    outs = pl.pallas_call(
        body, name=name,
        out_shape=(pltpu.SemaphoreType.DMA((7 * n,)), pltpu.SemaphoreType.DMA((7 * n,)), pltpu.SemaphoreType.DMA((n,)),
                   *thru, *thru, _sds((8, 128), F32)),
        in_specs=[HBM] * (2 * n), out_specs=(SEM, SEM, SEM, *[HBM] * (2 * n), pl.BlockSpec(memory_space=pltpu.VMEM)),
        input_output_aliases={i: 3 + i for i in range(2 * n)},
        compiler_params=pltpu.CompilerParams(has_side_effects=pltpu.SideEffectType.DATAFLOW_SIDE_EFFECTING),
    )(*[pltpu.with_memory_space_constraint(p, pltpu.HBM) for p in payloads],
      *[pltpu.with_memory_space_constraint(lax.empty(p.shape, p.dtype), pltpu.HBM) for p in payloads])
    return (outs[:3], outs[3:3 + n], outs[3 + n:3 + 2 * n]), outs[-1][0, 0]


def scatter_wait(handle, after, name):
    sems, srcs_thru, lands_thru = handle
    n = len(srcs_thru)

    def body(*refs):
        srcs, lands, sems_ = refs[:n], refs[n:2 * n], refs[2 * n:2 * n + 3]
        me, remote, local = _scatter_copies(srcs, lands, sems_)
        for a in range(n):
            for k in range(1, NDEV):
                cp = remote(a, k, me ^ k)
                cp.wait_send()
                cp.wait_recv()
            local[a].wait()

    outs = pl.pallas_call(
        body, name=name, out_shape=tuple(pltpu.HBM(p.shape, p.dtype) for p in (*srcs_thru, *lands_thru)),
        in_specs=[HBM] * (2 * n) + [SEM] * 3 + [ANY], out_specs=tuple([HBM] * (2 * n)),
        input_output_aliases={i: i for i in range(2 * n)},
        compiler_params=pltpu.CompilerParams(has_side_effects=pltpu.SideEffectType.DATAFLOW_SIDE_EFFECTING),
    )(*srcs_thru, *lands_thru, *sems, after)
    return list(outs[n:])


def _row_spec(cols, tm=TM):
    return pl.BlockSpec((tm, cols), lambda i: (i, 0))


def _full_spec(shape):
    nd = len(shape)
    return pl.BlockSpec(shape, lambda i: (0,) * nd)


def ln_mod_fwd(x, mod, sub, name):
    def body(x_ref, mod_ref, h_ref):
        n, _ = _ln_stats(x_ref[...])
        shift = mod_ref[3 * sub:3 * sub + 1, :]
        scale = mod_ref[3 * sub + 1:3 * sub + 2, :]
        h_ref[...] = (n * (1.0 + scale) + shift).astype(MXU)

    return _pallas_call(
        body, name=name, grid=(S // TM,),
        in_specs=[_row_spec(D), _full_spec((9, D))], out_specs=_row_spec(D),
        out_shape=_sds((S, D), MXU), compiler_params=_cp(32, ("arbitrary",)))(x, mod)


def res_ln_fwd(x, f, mod, sub, lng, lnb, w, name, nxt=None):
    def body(x_ref, f_ref, mod_ref, g_ref, b_ref, *rest):
        gate = mod_ref[3 * sub + 2:3 * sub + 3, :]
        r = ALPHA * x_ref[...] + (w * gate) * f_ref[...]
        n, _ = _ln_stats(r)
        xo = n * g_ref[sub:sub + 1, :] + b_ref[sub:sub + 1, :]
        rest[-1 if nxt is None else -2][...] = xo
        if nxt is not None:
            nmod_ref, h_ref = rest[0], rest[-1]
            n2, _ = _ln_stats(xo)
            s2 = nxt[1]
            h_ref[...] = (n2 * (1.0 + nmod_ref[3 * s2 + 1:3 * s2 + 2, :]) + nmod_ref[3 * s2:3 * s2 + 1, :]).astype(MXU)

    more = nxt is not None
    return _pallas_call(
        body, name=name, grid=(S // TM,),
        in_specs=[_row_spec(D), _row_spec(D), _full_spec((9, D)), _full_spec((3, D)), _full_spec((3, D))] + [_full_spec((9, D))] * more,
        out_specs=(_row_spec(D),) + (_row_spec(D),) * more, out_shape=(_sds((S, D), F32),) + (_sds((S, D), MXU),) * more,
        compiler_params=_cp(32, ("arbitrary",)))(x, f, mod, lng, lnb, *([nxt[0]] if more else []))


def res_ln_bwd(x, f, mod, sub, lng, dxo, w, name, ride=None):
    def body(x_ref, f_ref, mod_ref, g_ref, dxo_ref, dxa_ref, df_ref, sums_ref):
        i = pl.program_id(0)
        gate = mod_ref[3 * sub + 2:3 * sub + 3, :]
        fv = f_ref[...]
        r = ALPHA * x_ref[...] + (w * gate) * fv
        n, rstd = _ln_stats(r)
        dxo = dxo_ref[...]
        dr = _ln_bwd(dxo * g_ref[sub:sub + 1, :], n, rstd)
        dxa_ref[...] = ALPHA * dr
        df_ref[...] = ((w * gate) * dr).astype(MXU)
        part = jnp.concatenate([
            jnp.sum(dxo * n, axis=0, keepdims=True),
            jnp.sum(dxo, axis=0, keepdims=True),
            jnp.sum(dr * fv, axis=0, keepdims=True) * w,
            jnp.zeros((5, D), F32)], axis=0)

        @pl.when(i == 0)
        def _():
            sums_ref[...] = part

        @pl.when(i > 0)
        def _():
            sums_ref[...] += part

    return _call(
        body, name=name, grid=(S // TM,),
        in_specs=[_row_spec(D), _row_spec(D), _full_spec((9, D)), _full_spec((3, D)), _row_spec(D)],
        out_specs=(_row_spec(D), _row_spec(D), _full_spec((8, D))),
        out_shape=(_sds((S, D), F32), _sds((S, D), MXU), _sds((8, D), F32)),
        cp=_cp(32, ("arbitrary",)), args=(x, f, mod, lng, dxo), ride=ride)


def ln_mod_bwd(x, dh, mod, sub, dxa, name, ride=None):
    def body(x_ref, dh_ref, mod_ref, dxa_ref, dx_ref, sums_ref):
        i = pl.program_id(0)
        scale = mod_ref[3 * sub + 1:3 * sub + 2, :]
        n, rstd = _ln_stats(x_ref[...])
        dh = dh_ref[...]
        dx_ref[...] = dxa_ref[...] + _ln_bwd(dh * (1.0 + scale), n, rstd)
        part = jnp.concatenate([
            jnp.sum(dh, axis=0, keepdims=True),
            jnp.sum(dh * n, axis=0, keepdims=True),
            jnp.zeros((6, D), F32)], axis=0)

        @pl.when(i == 0)
        def _():
            sums_ref[...] = part

        @pl.when(i > 0)
        def _():
            sums_ref[...] += part

    return _call(
        body, name=name, grid=(S // TM,),
        in_specs=[_row_spec(D), _row_spec(D), _full_spec((9, D)), _row_spec(D)],
        out_specs=(_row_spec(D), _full_spec((8, D))),
        out_shape=(_sds((S, D), F32), _sds((8, D), F32)),
        cp=_cp(32, ("arbitrary",)), args=(x, dh, mod, dxa), ride=ride)


def loss_fwd_bwd(y, target, name):
    def body(y_ref, t_ref, l_ref, dy_ref):
        i = pl.program_id(0)
        e = y_ref[...] - t_ref[...]
        dy_ref[...] = e * (1.0 / D)
        part = jnp.zeros((8, 128), F32) + (0.5 / D) * jnp.sum(e * e)

        @pl.when(i == 0)
        def _():
            l_ref[...] = part

        @pl.when(i > 0)
        def _():
            l_ref[...] += part

    return _pallas_call(
        body, name=name, grid=(S // TM,),
        in_specs=[_row_spec(D), _row_spec(D)], out_specs=(_full_spec((8, 128)), _row_spec(D)),
        out_shape=(_sds((8, 128), F32), _sds((S, D), F32)),
        compiler_params=_cp(32, ("arbitrary",)))(y, target)


HB = 2 * FBP
NHB = NDEV * FBP // HB
TMB = 1024


def _wrows(buffers=2):
    return pl.BlockSpec((HB, D), lambda j, i: (j, 0), pipeline_mode=pl.Buffered(buffers))


def _resident(shape):
    return pl.BlockSpec(shape, lambda j, i: (0, 0), pipeline_mode=pl.Buffered(1))


def ffn_fwd(h, wgt, wut, wd, name, ride=None):
    def body(h_ref, wg_ref, wu_ref, wd_ref, g_ref, u_ref, f_ref):
        j, i = pl.program_id(0), pl.program_id(1)
        hv = h_ref[...]
        g = _mm_nt(hv, wg_ref[...])
        u = _mm_nt(hv, wu_ref[...])
        g_ref[...] = g.astype(MXU)
        u_ref[...] = u.astype(MXU)
        a = g * jax.nn.sigmoid(g) * u
        part = _mm(a, wd_ref[...])
        rows = pl.ds(pl.multiple_of(i * TMB, TMB), TMB)

        @pl.when(j == 0)
        def _():
            f_ref[rows, :] = part

        @pl.when(j > 0)
        def _():
            f_ref[rows, :] += part

    gu = pl.BlockSpec((TMB, HB), lambda j, i: (i, j))
    return _call(
        body, name=name, grid=(NHB, S // TMB),
        in_specs=[pl.BlockSpec((TMB, D), lambda j, i: (i, 0)), _wrows(), _wrows(), _wrows()],
        out_specs=(gu, gu, _resident((S, D))),
        out_shape=(_sds((S, NDEV * FBP), MXU), _sds((S, NDEV * FBP), MXU), _sds((S, D), F32)),
        cp=_cp(52, ("arbitrary", "arbitrary")), args=(h, wgt, wut, wd), ride=ride)


def ffn_bwd(df, h, g, u, wgt, wut, wd, name, ride=None):
    ni = S // TMB

    def body(df_ref, h_ref, g_ref, u_ref, wg_ref, wu_ref, wd_ref, dwg_ref, dwu_ref, dwd_ref, dh_ref,
             ag_ref, au_ref, ad_ref):
        j, i = pl.program_id(0), pl.program_id(1)
        dfv, hv = df_ref[...], h_ref[...]
        gv, uv = g_ref[...].astype(F32), u_ref[...].astype(F32)
        da = _mm_nt(dfv, wd_ref[...])
        sg = jax.nn.sigmoid(gv)
        silu = gv * sg
        du = da * silu
        dg = da * uv * (sg * (1.0 + gv * (1.0 - sg)))
        p_d = _mm_tn(silu * uv, dfv)
        p_g = _mm_tn(dg, hv)
        p_u = _mm_tn(du, hv)

        @pl.when(i == 0)
        def _():
            ad_ref[...] = p_d
            ag_ref[...] = p_g
            au_ref[...] = p_u

        @pl.when(i > 0)
        def _():
            ad_ref[...] += p_d
            ag_ref[...] += p_g
            au_ref[...] += p_u

        @pl.when(i == ni - 1)
        def _():
            dwd_ref[...] = ad_ref[...].astype(BF16)
            dwg_ref[...] = ag_ref[...].astype(BF16)
            dwu_ref[...] = au_ref[...].astype(BF16)

        part = _mm(dg, wg_ref[...]) + _mm(du, wu_ref[...])
        rows = pl.ds(pl.multiple_of(i * TMB, TMB), TMB)

        @pl.when(j == 0)
        def _():
            dh_ref[rows, :] = part

        @pl.when(j > 0)
        def _():
            dh_ref[rows, :] += part

    gu = pl.BlockSpec((TMB, HB), lambda j, i: (i, j))
    rowt = pl.BlockSpec((TMB, D), lambda j, i: (i, 0))
    return _call(
        body, name=name, grid=(NHB, ni),
        in_specs=[rowt, rowt, gu, gu, _wrows(1), _wrows(1), _wrows(1)],
        out_specs=(_wrows(1), _wrows(1), _wrows(1), _resident((S, D))),
        out_shape=(_sds((NDEV * FBP, D), BF16), _sds((NDEV * FBP, D), BF16), _sds((NDEV * FBP, D), BF16), _sds((S, D), F32)),
        scratch=[pltpu.VMEM((HB, D), F32), pltpu.VMEM((HB, D), F32), pltpu.VMEM((HB, D), F32)],
        cp=_cp(60, ("arbitrary", "arbitrary")), args=(df, h, g, u, wgt, wut, wd), ride=ride)


def win_fwd(h, win, name, ride=None):
    def body(h_ref, w_ref, z_ref):
        hv = h_ref[...]
        for j in range(NDEV):
            z_ref[:, 256 * j:256 * (j + 1)] = _mm(hv, w_ref[j])

    return _call(
        body, name=name, grid=(S // TMM,),
        in_specs=[_row_spec(D, TMM), _full_spec((NDEV, D, 256))],
        out_specs=[_row_spec(D_IN, TMM)], out_shape=[_sds((S, D_IN), F32)],
        cp=_cp(40, ("arbitrary",)), args=(h, win), ride=ride)


def win_bwd(dparts, h, win, name, ride=None):
    ni = S // TMM

    def body(dq_ref, dk_ref, dv_ref, dus_ref, dup_ref, h_ref, w_ref, dh_ref, dw_ref, acc_ref):
        i = pl.program_id(0)
        hv = h_ref[...]
        cols = [dq_ref[:, 0:256], dq_ref[:, 256:512], dk_ref[:, 0:256], dk_ref[:, 256:512],
                dv_ref[:, 0:256], dv_ref[:, 256:512], dus_ref[...], dup_ref[...]]
        dh = jnp.zeros((TMM, D), F32)
        for j in range(NDEV):
            dz = cols[j].astype(MXU)
            dh = dh + _mm_nt(dz, w_ref[j])
            p = _mm_tn(hv, dz)

            @pl.when(i == 0)
            def _():
                acc_ref[j] = p

            @pl.when(i > 0)
            def _():
                acc_ref[j] += p

        dh_ref[...] = dh

        @pl.when(i == ni - 1)
        def _():
            dw_ref[...] = acc_ref[...].astype(BF16)

    return _call(
        body, name=name, grid=(ni,),
        in_specs=[_row_spec(512, TMM), _row_spec(512, TMM), _row_spec(512, TMM), _row_spec(256, TMM), _row_spec(256, TMM),
                  _row_spec(D, TMM), _full_spec((NDEV, D, 256))],
        out_specs=(_row_spec(D, TMM), _full_spec((NDEV, D, 256))),
        out_shape=(_sds((S, D), F32), _sds((NDEV, D, 256), BF16)),
        scratch=[pltpu.VMEM((NDEV, D, 256), F32)],
        cp=_cp(48, ("arbitrary",)), args=(*dparts, h, win), ride=ride)


def wout_fwd(ya, ys, yp, wout, name, ride=None):
    def body(ya_ref, ys_ref, yp_ref, w_ref, o_ref):
        w = w_ref[...].reshape(D, D)
        o_ref[...] = _mm(ya_ref[...], w[0:512]) + _mm(ys_ref[...], w[512:768]) + _mm(yp_ref[...], w[768:1024])

    return _call(
        body, name=name, grid=(S // TMM,),
        in_specs=[_row_spec(512, TMM), _row_spec(256, TMM), _row_spec(256, TMM), _full_spec((NDEV, 128, D))],
        out_specs=[_row_spec(D, TMM)], out_shape=[_sds((S, D), F32)],
        cp=_cp(40, ("arbitrary",)), args=(ya, ys, yp, wout), ride=ride)


def wout_bwd(do, ya, ys, yp, wout, name, ride=None):
    ni = S // TMM

    def body(do_ref, ya_ref, ys_ref, yp_ref, w_ref, dya_ref, dys_ref, dyp_ref, dw_ref, acc_ref):
        i = pl.program_id(0)
        w = w_ref[...].reshape(D, D)
        dov = do_ref[...]
        dya_ref[...] = _mm_nt(dov, w[0:512])
        dys_ref[...] = _mm_nt(dov, w[512:768])
        dyp_ref[...] = _mm_nt(dov, w[768:1024])
        parts = [(0, 512, _mm_tn(ya_ref[...], dov)), (512, 768, _mm_tn(ys_ref[...], dov)),
                 (768, 1024, _mm_tn(yp_ref[...], dov))]
        for lo, hi, p in parts:
            @pl.when(i == 0)
            def _():
                acc_ref[lo:hi, :] = p

            @pl.when(i > 0)
            def _():
                acc_ref[lo:hi, :] += p

        @pl.when(i == ni - 1)
        def _():
            dw_ref[...] = acc_ref[...].astype(BF16).reshape(NDEV, 128, D)

    return _call(
        body, name=name, grid=(ni,),
        in_specs=[_row_spec(D, TMM), _row_spec(512, TMM), _row_spec(256, TMM), _row_spec(256, TMM),
                  _full_spec((NDEV, 128, D))],
        out_specs=(_row_spec(512, TMM), _row_spec(256, TMM), _row_spec(256, TMM), _full_spec((NDEV, 128, D))),
        out_shape=(_sds((S, 512), F32), _sds((S, 256), F32), _sds((S, 256), F32), _sds((NDEV, 128, D), BF16)),
        scratch=[pltpu.VMEM((D, D), F32)],
        cp=_cp(40, ("arbitrary",)), args=(do, ya, ys, yp, wout), ride=ride)


def _t5_bucket(dist):
    max_exact = N_BUCKETS // 2
    d = np.maximum(dist, 1).astype(np.float32)
    large = max_exact + (np.log(d / max_exact) / math.log(MAX_DISTANCE / max_exact)
                         * (N_BUCKETS - max_exact)).astype(np.int32)
    large = np.minimum(large, N_BUCKETS - 1)
    return np.where(dist < max_exact, dist, large).astype(np.int32)


def _att_static():
    i = np.arange(QB)[:, None]
    j = np.arange(2 * QB)[None, :]
    r = i + QB - j
    buckets, bands = [], []
    for window, dil in PATTERNS:
        bands.append((r >= 0) & (r <= window // dil))
        buckets.append(_t5_bucket(np.clip(r, 0, None) * dil))
    return np.stack(buckets), np.stack(bands), np.broadcast_to(j >= QB, (QB, 2 * QB))


def att_bias(rel_bias):
    m = np.arange(2 * QB)
    rows = []
    for window, dil in PATTERNS:
        r = QB - m
        ok = (r >= 0) & (r <= window // dil)
        b = rel_bias[_t5_bucket(np.clip(r, 0, None) * dil)]
        rows.append(jnp.where(ok[:, None], b, NEG).T)
    return jnp.broadcast_to(jnp.stack(rows)[:, :, None, :], (3, N_HEADS, 8, 2 * QB))


def _bias_tiles(t_ref, tiles):
    col = lax.broadcasted_iota(jnp.int32, (QB, 2 * QB), 1)
    for p in range(3):
        for hh in range(2):
            t = pltpu.roll(jnp.broadcast_to(t_ref[p, hh, 0:1, :], (QB, 2 * QB)), 0, 1, stride=1, stride_axis=0)
            tiles[p, hh, 0] = t
            tiles[p, hh, 1] = jnp.where(col >= QB, t, NEG)


def _permute_in(dst_ref, src_ref, d, scale=None, pad=QB):
    L = S // d
    for r in range(d):
        v = src_ref[pl.ds(r, L, stride=d), :] if d > 1 else src_ref[...]
        if scale is not None:
            v = v * scale
        dst_ref[pad + r * L:pad + (r + 1) * L, :] = v.astype(dst_ref.dtype)


def att_fwd(z, bias, name, ride=None):
    def body(q_ref, k_ref, v_ref, t_ref, y_ref, l_ref, qs, ks, vs, o_perm, l_perm, o_nat, l_nat, b_ref):
        _bias_tiles(t_ref, b_ref)
        zero_pad = jnp.zeros((QB, 128), MXU)
        ks[0:QB, :] = zero_pad
        vs[0:QB, :] = zero_pad
        lane = lax.broadcasted_iota(jnp.int32, (QB, 128), 1)
        for p, (_, d) in enumerate(PATTERNS):
            L = S // d
            nb = L // QB
            _permute_in(qs, q_ref, d, scale=0.125, pad=0)
            _permute_in(ks, k_ref, d)
            _permute_in(vs, v_ref, d)

            def blk(b, carry):
                r0 = pl.multiple_of(b * QB, QB)
                q = qs[pl.ds(r0, QB), :]
                kb = ks[pl.ds(r0, 2 * QB), :]
                vb = vs[pl.ds(r0, 2 * QB), :]
                first = ((b % nb) == 0).astype(jnp.int32)
                res = []
                for hh in range(2):
                    sel = (lane < 64) if hh == 0 else (lane >= 64)
                    qm = jnp.where(sel, q, jnp.zeros_like(q))
                    s = _mm_nt(qm, kb) + b_ref[p, hh, first]
                    m = jnp.max(s, axis=1, keepdims=True)
                    pe = jnp.exp(s - m)
                    den = jnp.sum(pe, axis=1, keepdims=True)
                    res.append((_mm(pe, vb) / den, m + jnp.log(den)))
                o_perm[pl.ds(r0, QB), :] = jnp.where(lane < 64, res[0][0], res[1][0])
                l_perm[pl.ds(r0, QB), :] = jnp.where(lane < 64, res[0][1], res[1][1])
                return carry

            lax.fori_loop(0, S // QB, blk, 0, unroll=8)
            for r in range(d):
                if d > 1:
                    o_nat[p, pl.ds(r, L, stride=d), :] = o_perm[r * L:(r + 1) * L, :]
                    l_nat[p, pl.ds(r, L, stride=d), :] = l_perm[r * L:(r + 1) * L, :]
                else:
                    o_nat[p] = o_perm[...]
                    l_nat[p] = l_perm[...]
        l0, l1, l2 = l_nat[0], l_nat[1], l_nat[2]
        m = jnp.maximum(jnp.maximum(l0, l1), l2)
        e0, e1, e2 = jnp.exp(l0 - m), jnp.exp(l1 - m), jnp.exp(l2 - m)
        den = e0 + e1 + e2
        y_ref[...] = (e0 * o_nat[0] + e1 * o_nat[1] + e2 * o_nat[2]) / den
        l_ref[...] = m + jnp.log(den)

    col = lambda c0: pl.BlockSpec((S, 128), lambda hp: (0, c0 + hp))
    return _call(
        body, name=name, grid=(N_HEADS // 2,),
        in_specs=[col(0), col(4), col(8), pl.BlockSpec((3, 2, 8, 2 * QB), lambda hp: (0, hp, 0, 0))],
        out_specs=(col(0), col(0)),
        out_shape=(_sds((S, D_ATT), F32), _sds((S, D_ATT), F32)),
        scratch=[pltpu.VMEM((S, 128), MXU), pltpu.VMEM((S + QB, 128), MXU), pltpu.VMEM((S + QB, 128), MXU),
                 pltpu.VMEM((S, 128), F32), pltpu.VMEM((S, 128), F32),
                 pltpu.VMEM((3, S, 128), F32), pltpu.VMEM((3, S, 128), F32),
                 pltpu.VMEM((3, 2, 2, QB, 2 * QB), F32)],
        cp=_cp(40, ("arbitrary",)), args=(z, z, z, bias), ride=ride)


def att_bwd(z, bias, y, lse, dy, name, ride=None):
    def body(q_ref, k_ref, v_ref, t_ref, y_ref, l_ref, dy_ref, dq_ref, dk_ref, dv_ref, db_ref,
             qs, ks, vs, dys, ls, dds, dn_nat, dq_perm, dk_perm, dv_perm, b_ref):
        _bias_tiles(t_ref, b_ref)
        zero_pad = jnp.zeros((QB, 128), MXU)
        ks[0:QB, :] = zero_pad
        vs[0:QB, :] = zero_pad
        lane = lax.broadcasted_iota(jnp.int32, (QB, 128), 1)
        lane_s = lax.broadcasted_iota(jnp.int32, (S, 128), 1)
        t = dy_ref[...] * y_ref[...]
        sa = jnp.sum(jnp.where(lane_s < 64, t, 0.0), axis=1, keepdims=True)
        sb = jnp.sum(jnp.where(lane_s >= 64, t, 0.0), axis=1, keepdims=True)
        dn_nat[...] = jnp.where(lane_s < 64, sa, sb)
        dq_ref[...] = jnp.zeros((S, 128), F32)
        dk_ref[...] = jnp.zeros((S, 128), F32)
        dv_ref[...] = jnp.zeros((S, 128), F32)
        db_ref[...] = jnp.zeros((3, 2, QB, 2 * QB), F32)
        for p, (_, d) in enumerate(PATTERNS):
            L = S // d
            nb = L // QB
            _permute_in(qs, q_ref, d, scale=0.125, pad=0)
            _permute_in(ks, k_ref, d)
            _permute_in(vs, v_ref, d)
            _permute_in(dys, dy_ref, d, pad=0)
            _permute_in(ls, l_ref, d, pad=0)
            _permute_in(dds, dn_nat, d, pad=0)
            dk_perm[...] = jnp.zeros((S + QB, 128), F32)
            dv_perm[...] = jnp.zeros((S + QB, 128), F32)

            def blk(b, carry):
                r0 = pl.multiple_of(b * QB, QB)
                q = qs[pl.ds(r0, QB), :]
                kb = ks[pl.ds(r0, 2 * QB), :]
                vb = vs[pl.ds(r0, 2 * QB), :]
                dyb = dys[pl.ds(r0, QB), :]
                lb = ls[pl.ds(r0, QB), :]
                db = dds[pl.ds(r0, QB), :]
                first = ((b % nb) == 0).astype(jnp.int32)
                dqs = []
                dkb = jnp.zeros((2 * QB, 128), F32)
                dvb = jnp.zeros((2 * QB, 128), F32)
                for hh in range(2):
                    sel = (lane < 64) if hh == 0 else (lane >= 64)
                    c0 = 64 * hh
                    qm = jnp.where(sel, q, jnp.zeros_like(q))
                    dym = jnp.where(sel, dyb, jnp.zeros_like(dyb))
                    s = _mm_nt(qm, kb) + b_ref[p, hh, first]
                    pr = jnp.exp(s - lb[:, c0:c0 + 1])
                    dp = _mm_nt(dym, vb)
                    ds = pr * (dp - db[:, c0:c0 + 1])
                    db_ref[p, hh] += ds
                    dqs.append(_mm(ds, kb))
                    dkb = dkb + _mm_tn(ds, qm)
                    dvb = dvb + _mm_tn(pr, dym)
                dq_perm[pl.ds(r0, QB), :] = jnp.where(lane < 64, dqs[0], dqs[1])
                dk_perm[pl.ds(r0, 2 * QB), :] += dkb
                dv_perm[pl.ds(r0, 2 * QB), :] += dvb
                return carry

            lax.fori_loop(0, S // QB, blk, 0, unroll=4)
            for r in range(d):
                idx = pl.ds(r, L, stride=d) if d > 1 else pl.ds(0, S)
                dq_ref[idx, :] += dq_perm[r * L:(r + 1) * L, :] * 0.125
                dk_ref[idx, :] += dk_perm[QB + r * L:QB + (r + 1) * L, :]
                dv_ref[idx, :] += dv_perm[QB + r * L:QB + (r + 1) * L, :]

    col = lambda c0: pl.BlockSpec((S, 128), lambda hp: (0, c0 + hp))
    bspec = pl.BlockSpec((3, 2, 8, 2 * QB), lambda hp: (0, hp, 0, 0))
    return _call(
        body, name=name, grid=(N_HEADS // 2,),
        in_specs=[col(0), col(4), col(8), bspec, col(0), col(0), col(0)],
        out_specs=(col(0), col(0), col(0), pl.BlockSpec((3, 2, QB, 2 * QB), lambda hp: (0, hp, 0, 0))),
        out_shape=(_sds((S, D_ATT), F32), _sds((S, D_ATT), F32), _sds((S, D_ATT), F32),
                   _sds((3, N_HEADS, QB, 2 * QB), F32)),
        scratch=[pltpu.VMEM((S, 128), MXU), pltpu.VMEM((S + QB, 128), MXU), pltpu.VMEM((S + QB, 128), MXU),
                 pltpu.VMEM((S, 128), MXU), pltpu.VMEM((S, 128), F32), pltpu.VMEM((S, 128), F32),
                 pltpu.VMEM((S, 128), F32), pltpu.VMEM((S, 128), F32),
                 pltpu.VMEM((S + QB, 128), F32), pltpu.VMEM((S + QB, 128), F32),
                 pltpu.VMEM((3, 2, 2, QB, 2 * QB), F32)],
        cp=_cp(48, ("arbitrary",)), args=(z, z, z, bias, y, lse, dy), ride=ride)


def relbias_grad(dbiases):
    bucket, band, _ = _att_static()
    onehot = (bucket[:, None] == np.arange(N_BUCKETS)[None, :, None, None]) & band[:, None]
    onehot = jnp.asarray(onehot.reshape(3, N_BUCKETS, QB * 2 * QB), BF16)

    def body(db0_ref, db1_ref, oh_ref, o_ref):
        acc = jnp.zeros((N_HEADS, N_BUCKETS), F32)
        for p in range(3):
            acc = acc + lax.dot_general(db0_ref[p] + db1_ref[p], oh_ref[p].astype(F32), (((1,), (1,)), ((), ())),
                                        preferred_element_type=F32, precision=lax.Precision.HIGHEST)
        o_ref[...] = acc

    vm = pl.BlockSpec(memory_space=pltpu.VMEM)
    out = _pallas_call(body, name="relbias_grad", in_specs=[vm, vm, vm], out_specs=vm,
                         out_shape=_sds((N_HEADS, N_BUCKETS), F32), compiler_params=_cp(40))(
        *[d.reshape(3, N_HEADS, QB * 2 * QB) for d in dbiases], onehot)
    return out.T


def _panel(t_ref, ri, j):
    return t_ref[ri, pl.ds(j, S, stride=8), :]


def _gelu(x):
    c = math.sqrt(2.0 / math.pi)
    th = jnp.tanh(c * (x + 0.044715 * x * x * x))
    return 0.5 * x * (1.0 + th), th


def ssm_fwd(z, a, bre, bim, cre, cim, dsk, gluw, glub, name, ride=None):
    def body(u_ref, a_ref, bre_ref, bim_ref, cre_ref, cim_ref, d_ref, gw_ref, gb_ref, y_ref, yp_ref, st_hbm, st_ref):
        u = u_ref[...]
        for j in range(8):
            st_ref[0, pl.ds(j, S, stride=8), :] = _mm(u, bre_ref[:, 128 * j:128 * (j + 1)])
            st_ref[1, pl.ds(j, S, stride=8), :] = _mm(u, bim_ref[:, 128 * j:128 * (j + 1)])
        ar, ai = a_ref[0], a_ref[1]

        def step(t, c):
            re, im = c
            i = pl.multiple_of(t * 8, 8)
            nre = ar * re - ai * im + st_ref[0, pl.ds(i, 8), :]
            nim = ar * im + ai * re + st_ref[1, pl.ds(i, 8), :]
            st_ref[0, pl.ds(i, 8), :] = nre
            st_ref[1, pl.ds(i, 8), :] = nim
            return nre, nim

        zero = jnp.zeros((8, 128), F32)
        lax.fori_loop(0, S, step, (zero, zero), unroll=8)
        y = d_ref[...] * u
        for j in range(8):
            y = y + _mm(_panel(st_ref, 0, j), cre_ref[128 * j:128 * (j + 1), :])
            y = y - _mm(_panel(st_ref, 1, j), cim_ref[128 * j:128 * (j + 1), :])
        pltpu.sync_copy(st_ref, st_hbm)
        yp_ref[...] = y
        gl, _ = _gelu(y)
        tt = _mm(gl, gw_ref[...].reshape(D_SSM, D_SSM)) + gb_ref[...]
        y_ref[...] = y * jax.nn.sigmoid(tt)

    vm = lambda shape: pl.BlockSpec(shape, lambda i: (0,) * len(shape))
    return _call(
        body, name=name, grid=(1,),
        in_specs=[pl.BlockSpec((S, 256), lambda i: (0, 6)), vm((2, 8, 128)), vm((256, 1024)), vm((256, 1024)),
                  vm((1024, 256)), vm((1024, 256)), vm((1, 256)),
                  vm((NDEV, 32, 256)), vm((1, 256))],
        out_specs=(vm((S, 256)), vm((S, 256)), pl.BlockSpec(memory_space=pl.ANY)),
        out_shape=(_sds((S, 256), F32), _sds((S, 256), F32), _sds((2, S * 8, 128), F32)),
        scratch=[pltpu.VMEM((2, S * 8, 128), F32)],
        cp=_cp(40, ("arbitrary",)), args=(z, a, bre, bim, cre, cim, dsk, gluw, glub), ride=ride)


def ssm_bwd(dy, z, ypre, st, a, bre, bim, cre, cim, dsk, gluw, glub, name, ride=None):
    def body(dy_ref, u_ref, yp_ref, st_hbm, a_ref, bre_ref, bim_ref, cre_ref, cim_ref, d_ref, gw_ref, gb_ref,
             du_ref, dbre_ref, dbim_ref, dcre_ref, dcim_ref, da_ref, dd_ref, dgw_ref, dgb_ref, g_ref, st_ref):
        pltpu.sync_copy(st_hbm, st_ref)
        u = u_ref[...]
        y = yp_ref[...]
        dout = dy_ref[...]
        gw = gw_ref[...].reshape(D_SSM, D_SSM)
        gl, th = _gelu(y)
        sig = jax.nn.sigmoid(_mm(gl, gw) + gb_ref[...])
        dt = dout * y * sig * (1.0 - sig)
        dgw_ref[...] = _mm_tn(gl, dt)
        dgb_ref[...] = jnp.sum(dt, axis=0, keepdims=True)
        c = math.sqrt(2.0 / math.pi)
        dgelu = 0.5 * (1.0 + th) + 0.5 * y * (1.0 - th * th) * c * (1.0 + 3.0 * 0.044715 * y * y)
        dyv = dout * sig + _mm_nt(dt, gw) * dgelu
        dd_ref[...] = jnp.sum(dyv * u, axis=0, keepdims=True)
        for j in range(8):
            rows = slice(128 * j, 128 * (j + 1))
            g_ref[0, pl.ds(j, S, stride=8), :] = _mm_nt(dyv, cre_ref[rows, :])
            g_ref[1, pl.ds(j, S, stride=8), :] = -_mm_nt(dyv, cim_ref[rows, :])
            dcre_ref[rows, :] = _mm_tn(_panel(st_ref, 0, j), dyv)
            dcim_ref[rows, :] = -_mm_tn(_panel(st_ref, 1, j), dyv)
        ar, ai = a_ref[0], a_ref[1]

        def step(k, c4):
            gre, gim, dar, dai = c4
            i = pl.multiple_of((S - 1 - k) * 8, 8)
            nre = g_ref[0, pl.ds(i, 8), :] + ar * gre + ai * gim
            nim = g_ref[1, pl.ds(i, 8), :] + ar * gim - ai * gre
            g_ref[0, pl.ds(i, 8), :] = nre
            g_ref[1, pl.ds(i, 8), :] = nim
            sre = st_ref[0, pl.ds(i - 8, 8), :]
            sim = st_ref[1, pl.ds(i - 8, 8), :]
            return nre, nim, dar + nre * sre + nim * sim, dai + nim * sre - nre * sim

        zero = jnp.zeros((8, 128), F32)
        gre, gim, dar, dai = lax.fori_loop(0, S - 1, step, (zero, zero, zero, zero), unroll=8)
        g_ref[0, 0:8, :] = g_ref[0, 0:8, :] + ar * gre + ai * gim
        g_ref[1, 0:8, :] = g_ref[1, 0:8, :] + ar * gim - ai * gre
        da_ref[0] = dar
        da_ref[1] = dai
        du = dyv * d_ref[...]
        for j in range(8):
            cols = slice(128 * j, 128 * (j + 1))
            gr, gi = _panel(g_ref, 0, j), _panel(g_ref, 1, j)
            dbre_ref[:, cols] = _mm_tn(u, gr)
            dbim_ref[:, cols] = _mm_tn(u, gi)
            du = du + _mm_nt(gr, bre_ref[:, cols]) + _mm_nt(gi, bim_ref[:, cols])
        du_ref[...] = du

    vm = lambda shape: pl.BlockSpec(shape, lambda i: (0,) * len(shape))
    return _call(
        body, name=name, grid=(1,),
        in_specs=[vm((S, 256)), pl.BlockSpec((S, 256), lambda i: (0, 6)), vm((S, 256)), pl.BlockSpec(memory_space=pl.ANY),
                  vm((2, 8, 128)), vm((256, 1024)), vm((256, 1024)), vm((1024, 256)), vm((1024, 256)), vm((1, 256)),
                  vm((NDEV, 32, 256)), vm((1, 256))],
        out_specs=(vm((S, 256)), vm((256, 1024)), vm((256, 1024)), vm((1024, 256)), vm((1024, 256)),
                   vm((2, 8, 128)), vm((1, 256)), vm((256, 256)), vm((1, 256))),
        out_shape=(_sds((S, 256), F32), _sds((256, 1024), F32), _sds((256, 1024), F32), _sds((1024, 256), F32),
                   _sds((1024, 256), F32), _sds((2, 8, 128), F32), _sds((1, 256), F32), _sds((256, 256), F32),
                   _sds((1, 256), F32)),
        scratch=[pltpu.VMEM((2, S * 8, 128), F32), pltpu.VMEM((2, S * 8, 128), F32)],
        cp=_cp(56, ("arbitrary",)), args=(dy, z, ypre, st, a, bre, bim, cre, cim, dsk, gluw, glub), ride=ride)


def _ssm_discretise(a_re, a_im, log_dt, b_re, b_im):
    dt = jnp.exp(log_dt)[:, None]
    er = jnp.exp(a_re * dt)
    abr, abi = er * jnp.cos(a_im * dt), er * jnp.sin(a_im * dt)
    den = a_re * a_re + a_im * a_im
    fr = ((abr - 1.0) * a_re + abi * a_im) / den
    fi = (abi * a_re - (abr - 1.0) * a_im) / den
    bbr = fr[:, :, None] * b_re - fi[:, :, None] * b_im
    bbi = fr[:, :, None] * b_im + fi[:, :, None] * b_re
    return abr, abi, bbr, bbi


def _blockdiag(t):
    g, r, c = t.shape
    eye = jnp.eye(g, dtype=t.dtype)
    return (t[:, :, None, :] * eye[:, None, :, None]).reshape(g * r, g * c)


def _blockdiag_take(m, r, c):
    g = m.shape[0] // r
    idx = jnp.arange(g)
    return m.reshape(g, r, g, c)[idx, :, idx, :]


PAD = 16


def _pool_lane_select(vals):
    lane = lax.broadcasted_iota(jnp.int32, vals[0].shape, 1)
    out = vals[3]
    for g in (2, 1, 0):
        out = jnp.where(lane < 64 * (g + 1), vals[g], out)
    return out


def _pool_counts():
    row = lax.broadcasted_iota(jnp.int32, (S, D_POOL), 0).astype(F32) + 1.0
    return _pool_lane_select([jnp.minimum(row, float(w)) for w in POOL_WINDOWS])


def _pooled(u, sa, sb):
    sums = []
    cur = u
    bufs = (sa, sb)
    for k, sh in enumerate((1, 2, 4, 8)):
        buf = bufs[k % 2]
        buf[PAD:PAD + S, :] = cur
        cur = cur + buf[PAD - sh:PAD - sh + S, :]
        sums.append(cur)
    return _pool_lane_select(sums) / _pool_counts() - u


def pool_fwd(z, pw, psc, name):
    def body(u_ref, w_ref, s_ref, y_ref, sa, sb):
        for buf in (sa, sb):
            buf[0:PAD, :] = jnp.zeros((PAD, D_POOL), F32)
        pooled = _pooled(u_ref[...], sa, sb)
        y_ref[...] = _mm(pooled, w_ref[...]) * s_ref[...]

    vm = lambda shape: pl.BlockSpec(shape, lambda i: (0,) * len(shape))
    return _pallas_call(
        body, name=name, grid=(1,),
        in_specs=[pl.BlockSpec((S, 256), lambda i: (0, 7)), vm((256, 256)), vm((1, 256))],
        out_specs=vm((S, 256)), out_shape=_sds((S, 256), F32),
        scratch_shapes=[pltpu.VMEM((S + 2 * PAD, D_POOL), F32)] * 2,
        compiler_params=_cp(40, ("arbitrary",)))(z, pw, psc)


def pool_bwd(dy, z, pw, psc, name):
    def body(dy_ref, u_ref, w_ref, s_ref, du_ref, dw_ref, ds_ref, sa, sb):
        for buf in (sa, sb):
            buf[0:PAD, :] = jnp.zeros((PAD, D_POOL), F32)
            buf[PAD + S:PAD + S + PAD, :] = jnp.zeros((PAD, D_POOL), F32)
        pooled = _pooled(u_ref[...], sa, sb)
        dyv = dy_ref[...]
        w = w_ref[...]
        ds_ref[...] = jnp.sum(dyv * _mm(pooled, w), axis=0, keepdims=True)
        dyl = dyv * s_ref[...]
        dw_ref[...] = _mm_tn(pooled, dyl)
        dpool = _mm_nt(dyl, w)
        cur = dpool / _pool_counts()
        sums = []
        bufs = (sa, sb)
        for k, sh in enumerate((1, 2, 4, 8)):
            buf = bufs[k % 2]
            buf[PAD:PAD + S, :] = cur
            cur = cur + buf[PAD + sh:PAD + sh + S, :]
            sums.append(cur)
        du_ref[...] = _pool_lane_select(sums) - dpool

    vm = lambda shape: pl.BlockSpec(shape, lambda i: (0,) * len(shape))
    return _pallas_call(
        body, name=name, grid=(1,),
        in_specs=[vm((S, 256)), pl.BlockSpec((S, 256), lambda i: (0, 7)), vm((256, 256)), vm((1, 256))],
        out_specs=(vm((S, 256)), vm((256, 256)), vm((1, 256))),
        out_shape=(_sds((S, 256), F32), _sds((256, 256), F32), _sds((1, 256), F32)),
        scratch_shapes=[pltpu.VMEM((S + 2 * PAD, D_POOL), F32)] * 2,
        compiler_params=_cp(40, ("arbitrary",)))(dy, z, pw, psc)


def ada_fwd(c_all, ada_w, ada_b_cols):
    def body(c_ref, w_ref, b_ref, o_ref):
        c = c_ref[...]
        cond = c * jax.nn.sigmoid(c)
        o_ref[...] = jnp.dot(cond, w_ref[...], preferred_element_type=F32, precision=lax.Precision.HIGHEST) + b_ref[...]

    return _pallas_call(
        body, name="ada_fwd", grid=(DEPTH,),
        in_specs=[pl.BlockSpec((NDEV, D), lambda l: (0, 0)), pl.BlockSpec((None, D, 1152), lambda l: (l, 0, 0)),
                  pl.BlockSpec((None, 1, 1152), lambda l: (l, 0, 0))],
        out_specs=pl.BlockSpec((None, NDEV, 1152), lambda l: (l, 0, 0)), out_shape=_sds((DEPTH, NDEV, 1152), F32),
        compiler_params=_cp(40, ("arbitrary",)))(c_all, ada_w, ada_b_cols)


def ada_bwd(c_all, dmod_cols):
    def body(c_ref, dm_ref, o_ref):
        c = c_ref[...]
        cond = c * jax.nn.sigmoid(c)
        o_ref[...] = lax.dot_general(cond, dm_ref[...], (((0,), (0,)), ((), ())), preferred_element_type=F32,
                                     precision=lax.Precision.HIGHEST)

    return _pallas_call(
        body, name="ada_bwd", grid=(DEPTH,),
        in_specs=[pl.BlockSpec((NDEV, D), lambda l: (0, 0)), pl.BlockSpec((None, NDEV, 1152), lambda l: (l, 0, 0))],
        out_specs=pl.BlockSpec((None, D, 1152), lambda l: (l, 0, 0)), out_shape=_sds((DEPTH, D, 1152), F32),
        compiler_params=_cp(40, ("arbitrary",)))(c_all, dmod_cols)


def _adamw(w, g, m, v):
    m2 = B1 * m + (1.0 - B1) * g
    v2 = B2 * v + (1.0 - B2) * (g * g)
    m_hat = m2 / (1.0 - B1 ** STEP)
    v_hat = v2 / (1.0 - B2 ** STEP)
    return -LR * (m_hat / (jnp.sqrt(v_hat) + EPS) + WD * w), m2, v2


def _sum8(ref):
    g = ref[0].astype(F32)
    for s in range(1, NDEV):
        g = g + ref[s].astype(F32)
    return g


def adam_rs(recv, w, m, v, tr, name, ride=None):
    lead, (r, cdim) = w.shape[:-2], w.shape[-2:]
    cp = recv.shape[-1]
    nl = len(lead)

    def body(rc_ref, w_ref, m_ref, v_ref, g_ref, d_ref, m2_ref, v2_ref):
        g = _sum8(rc_ref)[:, :cdim]
        g_ref[...] = g
        d_ref[...], m2_ref[...], v2_ref[...] = _adamw(w_ref[...], g, m_ref[...], v_ref[...])

    rs = pl.BlockSpec((None,) * nl + (tr, cdim), lambda *i: (*i, 0))
    return _call(
        body, name=name, grid=lead + (r // tr,),
        in_specs=[pl.BlockSpec((NDEV,) + (None,) * nl + (tr, cp), lambda *i: (0, *i, 0)), rs, rs, rs],
        out_specs=(rs, rs, rs, rs), out_shape=tuple(_sds(w.shape, F32) for _ in range(4)),
        cp=_cp(48, ("arbitrary",) * (nl + 1)), args=(recv, w, m, v), ride=ride)


def adam_block(recv, w, m, v, lf, prev, name):
    half = FB // 2

    def body(*refs):
        rc_ref, w_ref, m_ref, v_ref = refs[:4]
        g_ref, d_ref, m2_ref, v2_ref = refs[-4:]
        g = _sum8(rc_ref)
        g_ref[...] = g
        d_ref[...], m2_ref[...], v2_ref[...] = _adamw(w_ref[...], g, m_ref[...], v_ref[...])

    rs = pl.BlockSpec((None, None, half, D), lambda i: (lf // 2, lf % 2, i, 0))
    prev = list(prev) if prev is not None else []
    return list(_pallas_call(
        body, name=name, grid=(2,), in_specs=[pl.BlockSpec((NDEV, half, D), lambda i: (0, i, 0)), rs, rs, rs] + [ANY] * len(prev),
        out_specs=(rs, rs, rs, rs), out_shape=tuple(_sds((DEPTH, 2, FB, D), F32) for _ in range(4)),
        input_output_aliases={4 + k: k for k in range(len(prev))},
        compiler_params=_cp(48, ("arbitrary",)))(recv, w, m, v, *prev))


def adam_plain(g, w, m, v, tr, name, ride=None):
    lead, (r, cdim) = w.shape[:-2], w.shape[-2:]
    nl = len(lead)

    def body(g_ref, w_ref, m_ref, v_ref, d_ref, m2_ref, v2_ref):
        d_ref[...], m2_ref[...], v2_ref[...] = _adamw(w_ref[...], g_ref[...], m_ref[...], v_ref[...])

    rs = pl.BlockSpec((None,) * nl + (tr, cdim), lambda *i: (*i, 0))
    return _call(
        body, name=name, grid=lead + (r // tr,), in_specs=[rs, rs, rs, rs], out_specs=(rs, rs, rs),
        out_shape=tuple(_sds(w.shape, F32) for _ in range(3)),
        cp=_cp(48, ("arbitrary",) * (nl + 1)), args=(g, w, m, v), ride=ride)


def adam_native(gs, ws, ms, vs, name):
    n = len(ws)

    def body(*refs):
        g_refs, w_refs, m_refs, v_refs = (refs[k * n:(k + 1) * n] for k in range(4))
        d_refs, m2_refs, v2_refs = (refs[(4 + k) * n:(5 + k) * n] for k in range(3))
        for a in range(n):
            d_refs[a][...], m2_refs[a][...], v2_refs[a][...] = _adamw(w_refs[a][...], g_refs[a][...], m_refs[a][...], v_refs[a][...])

    vm = pl.BlockSpec(memory_space=pltpu.VMEM)
    outs = _pallas_call(body, name=name, in_specs=[vm] * (4 * n), out_specs=tuple([vm] * (3 * n)),
                        out_shape=tuple(_sds(w.shape, F32) for w in ws) * 3, compiler_params=_cp(40))(*gs, *ws, *ms, *vs)
    return outs[:n], outs[n:2 * n], outs[2 * n:]


def sum_sources(recv, name):
    r = recv.shape[1]

    def body(rc_ref, o_ref):
        o_ref[...] = _sum8(rc_ref)

    vm = pl.BlockSpec(memory_space=pltpu.VMEM)
    return _pallas_call(body, name=name, in_specs=[vm], out_specs=vm, out_shape=_sds((r, 128), F32),
                          compiler_params=_cp(40))(recv)


def _pack(arrs):
    flat = jnp.concatenate([a.reshape(-1) for a in arrs])
    n = flat.shape[0]
    rows = -(-n // 1024) * 8
    return jnp.pad(flat, (0, rows * 128 - n)).reshape(rows, 128)


def _unpack(vec, shapes):
    flat = vec.reshape(-1)
    out, o = [], 0
    for sh in shapes:
        n = int(np.prod(sh))
        out.append(flat[o:o + n].reshape(sh))
        o += n
    return out


WEIGHTS = ['rel_bias', 'ada_w', 'ada_b', 'ln_g', 'ln_b', 'ffn_w_gate', 'ffn_w_up', 'ffn_w_down', 'w_in', 'w_out',
           'ssm_a_re', 'ssm_a_im', 'ssm_log_dt', 'ssm_b_re', 'ssm_b_im', 'ssm_c_re', 'ssm_c_im', 'ssm_d', 'glu_w',
           'glu_b', 'pool_w', 'pool_scale']
SMALL = ['rel_bias', 'ada_b', 'ln_g', 'ln_b', 'ssm_a_re', 'ssm_a_im', 'ssm_log_dt', 'ssm_b_re', 'ssm_b_im',
         'ssm_c_re', 'ssm_c_im', 'ssm_d', 'glu_b', 'pool_w', 'pool_scale']
SMALL_FULL_SHAPES = {'rel_bias': (32, 8), 'ada_b': (2, 9216), 'ln_g': (2, 3, 1024), 'ln_b': (2, 3, 1024),
                     'ssm_a_re': (2, 16, 64), 'ssm_a_im': (2, 16, 64), 'ssm_log_dt': (2, 16),
                     'ssm_b_re': (2, 16, 64, 16), 'ssm_b_im': (2, 16, 64, 16), 'ssm_c_re': (2, 16, 16, 64),
                     'ssm_c_im': (2, 16, 16, 64), 'ssm_d': (2, 256), 'glu_b': (2, 256), 'pool_w': (2, 4, 64, 64),
                     'pool_scale': (2, 256)}


def _step(P):
    me = _me()
    x0 = P['x'][0]
    target = P['loss_target'][0]

    def shards(l, sub):
        bf = lambda a: a.astype(BF16)
        if sub == 1:
            return [bf(P['w_in'][l]), bf(P['w_out'][l]), bf(P['glu_w'][l])]
        f = sub // 2
        padr = lambda a: jnp.pad(bf(a), ((0, FBP - FB), (0, 0)))
        return [padr(P['ffn_w_gate'][l, f].T), padr(P['ffn_w_up'][l, f].T), padr(P['ffn_w_down'][l, f])]

    order = [(l, sub) for l in range(DEPTH) for sub in range(3)]
    nxt = dict(zip(order[:-1], order[1:]))
    W = {key: [None] * 3 for key in order}
    c_all, lng_all, lnb_all, *W[order[0]] = _exchange(Gather([P['c'], P['ln_g'], P['ln_b']] + shards(*order[0])), "gather_first")
    gather_queue = [(key, pos, a) for key in order[1:] for pos, a in enumerate(shards(*key))]

    def gather_ride(cap_us, must=None):
        units, used = [], 0.0
        while gather_queue:
            key, _, a = gather_queue[0]
            cost = a.size * a.dtype.itemsize * GATHER_US_PER_BYTE
            if key != must and used + cost / 2 > cap_us:
                break
            units.append(gather_queue.pop(0))
            used += cost
        return (Gather([a for _, _, a in units]) if units else None), units

    def gathered(units, outs):
        for (key, pos, _), o in zip(units, outs):
            W[key][pos] = o

    c_all = c_all.reshape(NDEV, D)
    ln_g = jnp.transpose(lng_all, (1, 2, 0, 3)).reshape(DEPTH, 3, D)
    ln_b = jnp.transpose(lnb_all, (1, 2, 0, 3)).reshape(DEPTH, 3, D)

    ada_b_cols = lax.dynamic_slice_in_dim(P['ada_b'], me * 1152, 1152, axis=1).reshape(DEPTH, 1, 1152)
    modc = ada_fwd(c_all, P['ada_w'], ada_b_cols)
    (mod_all,) = _exchange(Gather([modc]), "gather_mod")
    mod_me = lax.dynamic_index_in_dim(mod_all, me, axis=2, keepdims=False)
    mod = jnp.transpose(mod_me, (1, 0, 2)).reshape(DEPTH, 9, D)

    bias = att_bias(P['rel_bias'])
    ssm = []
    for l in range(DEPTH):
        prm = (P['ssm_a_re'][l], P['ssm_a_im'][l], P['ssm_log_dt'][l], P['ssm_b_re'][l], P['ssm_b_im'][l])
        (abr, abi, bbr, bbi), disc_vjp = jax.vjp(_ssm_discretise, *prm)
        ssm.append(dict(
            vjp=disc_vjp, a=jnp.stack([abr.reshape(8, 128), abi.reshape(8, 128)]),
            bre=_blockdiag(jnp.transpose(bbr, (0, 2, 1))).astype(MXU), bim=_blockdiag(jnp.transpose(bbi, (0, 2, 1))).astype(MXU),
            cre=_blockdiag(jnp.transpose(P['ssm_c_re'][l], (0, 2, 1))).astype(MXU),
            cim=_blockdiag(jnp.transpose(P['ssm_c_im'][l], (0, 2, 1))).astype(MXU),
            d=P['ssm_d'][l].reshape(1, 256), gb=P['glu_b'][l].reshape(1, 256),
            pw=_blockdiag(P['pool_w'][l]).astype(MXU), psc=P['pool_scale'][l].reshape(1, 256)))

    saved = []
    x = x0
    h = ln_mod_fwd(x, mod[0], 0, "ln_mod_fwd_l0s0")
    for l, sub in order:
        tag = f"l{l}s{sub}"
        after = (mod[nxt[(l, sub)][0]], nxt[(l, sub)][1]) if (l, sub) in nxt else None
        if sub != 1:
            wg, wu, wd = (t.reshape(NDEV * FBP, D) for t in W[(l, sub)])
            ride, units = gather_ride(60, nxt.get((l, sub)))
            (G, U, fo), got = ffn_fwd(h, wg, wu, wd, "ffn_fwd_" + tag, ride)
            gathered(units, got)
            saved.append(dict(x=x, h=h, G=G, U=U, f=fo))
            x, *hn = res_ln_fwd(x, fo, mod[l], sub, ln_g[l], ln_b[l], 0.5, "res_ln_fwd_" + tag, after)
        else:
            sp = ssm[l]
            win, wout, gluw = W[(l, sub)]
            ride, units = gather_ride(15)
            (z,), got = win_fwd(h, win, "win_fwd_" + tag, ride)
            gathered(units, got)
            ride, units = gather_ride(55)
            (ya, lse), got = att_fwd(z, bias, "att_fwd_" + tag, ride)
            gathered(units, got)
            ride, units = gather_ride(35)
            (ys, ypre, st), got = ssm_fwd(z, sp['a'], sp['bre'], sp['bim'], sp['cre'], sp['cim'], sp['d'], gluw, sp['gb'],
                                          "ssm_fwd_" + tag, ride)
            gathered(units, got)
            yp = pool_fwd(z, sp['pw'], sp['psc'], "pool_fwd_" + tag)
            ride, units = gather_ride(12, nxt.get((l, sub)))
            (o,), got = wout_fwd(ya, ys, yp, wout, "wout_fwd_" + tag, ride)
            gathered(units, got)
            saved.append(dict(x=x, h=h, z=z, ya=ya, lse=lse, ys=ys, ypre=ypre, st=st, yp=yp, f=o))
            x, *hn = res_ln_fwd(x, o, mod[l], sub, ln_g[l], ln_b[l], 1.0, "res_ln_fwd_" + tag, after)
        h = hn[0] if hn else None
    assert not gather_queue

    loss_tile, dx = loss_fwd_bwd(x, target, "loss")
    loss = lax.psum(loss_tile[0, 0], ("x", "y", "c"))

    flights = []

    def hosted(cap_us, fn, *args):
        return fn(*args, None)[0]

    dmod = [[None] * 9 for _ in range(DEPTH)]
    dlng = [[None] * 3 for _ in range(DEPTH)]
    dlnb = [[None] * 3 for _ in range(DEPTH)]
    dbiases = [None] * DEPTH
    small_l = [dict() for _ in range(DEPTH)]
    for l, sub in reversed(order):
        tag = f"l{l}s{sub}"
        sv = saved[3 * l + sub]
        w = 1.0 if sub == 1 else 0.5
        dxa, df, sums = hosted(0, res_ln_bwd, sv['x'], sv['f'], mod[l], sub, ln_g[l], dx, w, "res_ln_bwd_" + tag)
        dlng[l][sub], dlnb[l][sub], dmod[l][3 * sub + 2] = sums[0], sums[1], sums[2]
        if sub != 1:
            f = sub // 2
            wg, wu, wd = (t.reshape(NDEV * FBP, D) for t in W[(l, sub)])
            dwg, dwu, dwd, dh = hosted(105, ffn_bwd, df, sv['h'], sv['G'], sv['U'], wg, wu, wd, "ffn_bwd_" + tag)
            handle, zero = scatter_start([t.reshape(NDEV, FBP, D) for t in (dwg, dwu, dwd)], "scatter_start_" + tag)
            flights.append(((l, sub), handle))
        else:
            sp = ssm[l]
            win, wout, gluw = W[(l, sub)]
            dya, dys, dyp, dwout = hosted(0, wout_bwd, df, sv['ya'], sv['ys'], sv['yp'], wout, "wout_bwd_" + tag)
            dq, dk, dv, dbiases[l] = hosted(115, att_bwd, sv['z'], bias, sv['ya'], sv['lse'], dya, "att_bwd_" + tag)
            dus, dbre, dbim, dcre, dcim, da, dd, dgw, dgb = hosted(
                52, ssm_bwd, dys, sv['z'], sv['ypre'], sv['st'], sp['a'], sp['bre'], sp['bim'], sp['cre'], sp['cim'], sp['d'],
                gluw, sp['gb'], "ssm_bwd_" + tag)
            dup, dpw, dpsc = pool_bwd(dyp, sv['z'], sp['pw'], sp['psc'], "pool_bwd_" + tag)
            dh, dwin = hosted(40, win_bwd, (dq, dk, dv, dus, dup), sv['h'], win, "win_bwd_" + tag)
            handle, zero = scatter_start([dwin, dwout, dgw.astype(BF16).reshape(NDEV, 32, 256)], "scatter_start_" + tag)
            flights.append(((l, sub), handle))
            d_are, d_aim, d_ldt, d_bre, d_bim = sp['vjp']((
                da[0].reshape(16, 64), da[1].reshape(16, 64),
                jnp.transpose(_blockdiag_take(dbre, 16, 64), (0, 2, 1)), jnp.transpose(_blockdiag_take(dbim, 16, 64), (0, 2, 1))))
            small_l[l] = dict(
                ssm_a_re=d_are, ssm_a_im=d_aim, ssm_log_dt=d_ldt, ssm_b_re=d_bre, ssm_b_im=d_bim,
                ssm_c_re=jnp.transpose(_blockdiag_take(dcre, 64, 16), (0, 2, 1)),
                ssm_c_im=jnp.transpose(_blockdiag_take(dcim, 64, 16), (0, 2, 1)),
                ssm_d=dd.reshape(256), glu_b=dgb.reshape(256), pool_w=_blockdiag_take(dpw, 64, 64), pool_scale=dpsc.reshape(256))
        dx, sums = hosted(0, ln_mod_bwd, sv['x'], dh, mod[l] + zero, sub, dxa, "ln_mod_bwd_" + tag)
        dmod[l][3 * sub], dmod[l][3 * sub + 1] = sums[0], sums[1]
    grad_x = dx[None]

    out = {}

    def put(name, g, d, m2, v2, shape):
        out['grad_' + name], out['delta_' + name] = g.reshape(shape), d.reshape(shape)
        out['new_m_' + name], out['new_v_' + name] = m2.reshape(shape), v2.reshape(shape)

    def wmv(name):
        return [P[pre + name] for pre in ('', 'm_', 'v_')]

    recv = {}
    for key, handle in flights[:-1]:
        recv[key] = scatter_wait(handle, dx, "scatter_wait_l%ds%d" % key)
    for pos, (name, tr) in enumerate((('w_in', 512), ('w_out', 128), ('glu_w', 32))):
        both = jnp.stack([recv[(l, 1)][pos] for l in range(DEPTH)], axis=1)
        put(name, *adam_rs(both, *wmv(name), tr, "adam_" + name)[0], P[name].shape)
    ffn = (('ffn_w_gate', [jnp.swapaxes(t, 2, 3) for t in wmv('ffn_w_gate')]),
           ('ffn_w_up', [jnp.swapaxes(t, 2, 3) for t in wmv('ffn_w_up')]), ('ffn_w_down', wmv('ffn_w_down')))
    part = [None] * 3
    for l, sub in [key for key, _ in flights[:-1] if key[1] != 1]:
        for pos, (name, ops) in enumerate(ffn):
            part[pos] = adam_block(recv[(l, sub)][pos], *ops, 2 * l + sub // 2, part[pos], f"adam_{name}_l{l}s{sub}")
    (l, sub), handle = flights[-1]
    last = scatter_wait(handle, part[2][0], "scatter_wait_l%ds%d" % (l, sub))
    for pos, (name, ops) in enumerate(ffn):
        res = adam_block(last[pos], *ops, 2 * l + sub // 2, part[pos], f"adam_{name}_l{l}s{sub}")
        put(name, *([jnp.swapaxes(t, 2, 3) for t in res] if pos < 2 else res), P[name].shape)

    small = {k: jnp.stack([small_l[l][k] for l in range(DEPTH)]) for k in small_l[0]}
    small['rel_bias'] = relbias_grad(dbiases)
    small['ada_b'] = jnp.stack([jnp.stack(dmod[l]).reshape(9 * D) for l in range(DEPTH)])
    small['ln_g'] = jnp.stack([jnp.stack(dlng[l]) for l in range(DEPTH)])
    small['ln_b'] = jnp.stack([jnp.stack(dlnb[l]) for l in range(DEPTH)])
    (small_all,) = _exchange(Gather([_pack([small[k] for k in SMALL])]), "gather_small")
    gsum = dict(zip(SMALL, _unpack(sum_sources(small_all, "sum_small"), [SMALL_FULL_SHAPES[k] for k in SMALL])))
    off = 256
    dmod_all = small_all.reshape(NDEV, -1)[:, off:off + DEPTH * 9 * D].reshape(NDEV, DEPTH, 9 * D)
    dmod_cols = jnp.transpose(lax.dynamic_slice_in_dim(dmod_all, me * 1152, 1152, axis=2), (1, 0, 2))
    g_ada_w = ada_bwd(c_all, dmod_cols)

    put('ada_w', g_ada_w, *adam_plain(g_ada_w, *wmv('ada_w'), 256, "adam_ada_w")[0], P['ada_w'].shape)

    for k in ('ln_g', 'ln_b'):
        gsum[k] = lax.dynamic_slice_in_dim(gsum[k], me * 128, 128, axis=2)
    swaps = {'rel_bias': (0, 1), 'ln_g': (0, 1), 'ln_b': (0, 1), 'ssm_b_re': (2, 3), 'ssm_b_im': (2, 3)}
    view = lambda k, t: jnp.swapaxes(t, *swaps[k]) if k in swaps else t
    ds_, m2s, v2s = adam_native(*[[view(k, src(k)) for k in SMALL] for src in
                                  (lambda k: gsum[k], lambda k: P[k], lambda k: P['m_' + k], lambda k: P['v_' + k])],
                                "adam_small")
    for k, d, m2, v2 in zip(SMALL, ds_, m2s, v2s):
        put(k, gsum[k], view(k, d), view(k, m2), view(k, v2), P[k].shape)

    res = [loss, grad_x]
    for pre in ('grad_', 'delta_', 'new_m_', 'new_v_'):
        res += [out[pre + k] for k in WEIGHTS]
    return tuple(res)


def kernel(x, c, rel_bias, ada_w, ada_b, ln_g, ln_b, ffn_w_gate, ffn_w_up, ffn_w_down, w_in, w_out, ssm_a_re, ssm_a_im, ssm_log_dt, ssm_b_re, ssm_b_im, ssm_c_re, ssm_c_im, ssm_d, glu_w, glu_b, pool_w, pool_scale, loss_target, m_rel_bias, m_ada_w, m_ada_b, m_ln_g, m_ln_b, m_ffn_w_gate, m_ffn_w_up, m_ffn_w_down, m_w_in, m_w_out, m_ssm_a_re, m_ssm_a_im, m_ssm_log_dt, m_ssm_b_re, m_ssm_b_im, m_ssm_c_re, m_ssm_c_im, m_ssm_d, m_glu_w, m_glu_b, m_pool_w, m_pool_scale, v_rel_bias, v_ada_w, v_ada_b, v_ln_g, v_ln_b, v_ffn_w_gate, v_ffn_w_up, v_ffn_w_down, v_w_in, v_w_out, v_ssm_a_re, v_ssm_a_im, v_ssm_log_dt, v_ssm_b_re, v_ssm_b_im, v_ssm_c_re, v_ssm_c_im, v_ssm_d, v_glu_w, v_glu_b, v_pool_w, v_pool_scale):
    return _step(dict(locals()))
```

```python
import functools
import math

import numpy as np
import jax
import jax.numpy as jnp
from jax import lax
from jax.experimental import pallas as pl
from jax.experimental.pallas import tpu as pltpu

F32 = jnp.float32
BF16 = jnp.bfloat16
MXU = jnp.bfloat16

S = 2048
D = 1024
NDEV = 8
DEPTH = 2
D_ATT, D_SSM, D_POOL, D_IN = 512, 256, 256, 2048
N_HEADS = 8
FB = 352
FBP = 384
QB = 128
PATTERNS = ((128, 1), (512, 4), (2048, 16))
POOL_WINDOWS = (2, 4, 8, 16)
N_BUCKETS, MAX_DISTANCE = 32, 2048
ALPHA = (2 * DEPTH) ** 0.25
LN_EPS = 1e-5
NEG = -1e30
GATHER_US_PER_BYTE = 43e-6
SCATTER_US_PER_BYTE = 21.6e-6
LR, B1, B2, EPS, WD, STEP = 0.001, 0.9, 0.999, 1e-08, 0.01, 10

TM = 256
TMM = 512
MIB = 1024 * 1024


def _cp(vmem_mib, sem=None):
    kw = dict(vmem_limit_bytes=vmem_mib * MIB)
    if sem is not None:
        kw["dimension_semantics"] = sem
    return pltpu.CompilerParams(**kw)


def _sds(shape, dtype):
    return jax.ShapeDtypeStruct(shape, dtype)


def _mm(a, b):
    return jnp.dot(a.astype(MXU), b.astype(MXU), preferred_element_type=F32)


def _mm_nt(a, b):
    return lax.dot_general(a.astype(MXU), b.astype(MXU), (((1,), (1,)), ((), ())), preferred_element_type=F32)


def _mm_tn(a, b):
    return lax.dot_general(a.astype(MXU), b.astype(MXU), (((0,), (0,)), ((), ())), preferred_element_type=F32)


def _ln_stats(x):
    mu = jnp.mean(x, axis=-1, keepdims=True)
    xc = x - mu
    var = jnp.mean(xc * xc, axis=-1, keepdims=True)
    rstd = lax.rsqrt(var + LN_EPS)
    return xc * rstd, rstd


def _ln_bwd(dn, n, rstd):
    return rstd * (dn - jnp.mean(dn, axis=-1, keepdims=True) - n * jnp.mean(dn * n, axis=-1, keepdims=True))


def _me():
    return 4 * lax.axis_index("x") + 2 * lax.axis_index("y") + lax.axis_index("c")


ANY = pl.BlockSpec(memory_space=pl.ANY)
PIN_BYTES = 1 << 19


def _pallas_call(*a, **k):
    big = lambda o: math.prod(o.shape) * o.dtype.itemsize >= PIN_BYTES
    pin = lambda o: pltpu.HBM(o.shape, o.dtype) if isinstance(o, jax.ShapeDtypeStruct) and big(o) else o
    osh = k["out_shape"]
    k["out_shape"] = tuple(pin(o) for o in osh) if isinstance(osh, (tuple, list)) else pin(osh)
    fn = pl.pallas_call(*a, **k)

    def run(*args):
        return fn(*[pltpu.with_memory_space_constraint(x, pltpu.HBM) if big(x) else x for x in args])
    return run


class Gather:
    def __init__(self, srcs):
        self.srcs = list(srcs)
        self.n = len(self.srcs)
        self.bufs = []
        self.out_shapes = [_sds((NDEV,) + a.shape, a.dtype) for a in self.srcs]
        self.sems = [pltpu.SemaphoreType.DMA((7 * self.n,)), pltpu.SemaphoreType.DMA((7 * self.n,)),
                     pltpu.SemaphoreType.DMA((self.n,))]

    def _parts(self, srcs, outs, sems):
        send_sems, recv_sems, loc_sems = sems
        x, y, c = lax.axis_index("x"), lax.axis_index("y"), lax.axis_index("c")
        me, sib = (x, y, c), (x, y, 1 - c)
        chips = [(1 - x, y), (x, 1 - y), (1 - x, 1 - y)]
        slot = lambda d: 4 * d[0] + 2 * d[1] + d[2]

        def copy(a, k, block, to, src=None):
            dst = outs[a].at[slot(block)]
            return pltpu.make_async_remote_copy(
                src_ref=dst if src is None else src, dst_ref=dst,
                send_sem=send_sems.at[7 * a + k], recv_sem=recv_sems.at[7 * a + k],
                device_id=to, device_id_type=pl.DeviceIdType.MESH)

        local = [pltpu.make_async_copy(srcs[a], outs[a].at[slot(me)], loc_sems.at[a]) for a in range(self.n)]
        return me, sib, chips, c, copy, local

    def start(self, srcs, bufs, outs, sems):
        me, sib, chips, c, copy, local = self._parts(srcs, outs, sems)
        for a in range(self.n):
            local[a].start()
            copy(a, 0, me, sib, src=srcs[a]).start()
            for j, chip in enumerate(chips):
                copy(a, 1 + j, me, (*chip, c), src=srcs[a]).start()

    def finish(self, srcs, bufs, outs, sems):
        me, sib, chips, c, copy, local = self._parts(srcs, outs, sems)
        for a in range(self.n):
            for j, chip in enumerate(chips):
                copy(a, 1 + j, (*chip, c), me).wait_recv()
                copy(a, 4 + j, (*chip, c), sib).start()
        for a in range(self.n):
            copy(a, 0, sib, me).wait_recv()
            copy(a, 0, me, sib, src=srcs[a]).wait_send()
            for j, chip in enumerate(chips):
                copy(a, 4 + j, (*chip, 1 - c), me).wait_recv()
                copy(a, 1 + j, me, (*chip, c), src=srcs[a]).wait_send()
                copy(a, 4 + j, (*chip, c), sib).wait_send()
            local[a].wait()


class Scatter:
    def __init__(self, items, bufs):
        self.items = list(items)
        self.keys = list(dict.fromkeys(key for _, key, _, _ in self.items))
        self.srcs = [src for src, _, _, _ in self.items]
        self.bufs = [bufs[key] for key in self.keys]
        self.n = len(self.srcs)
        self.out_shapes = [_sds(b.shape, b.dtype) for b in self.bufs]
        pairs = [(a, k) for a, (_, _, _, ks) in enumerate(self.items) for k in ks]
        self.remote_pairs = [p for p in pairs if p[1] != 0]
        self.local_pairs = [p for p in pairs if p[1] == 0]
        self.sems = [pltpu.SemaphoreType.DMA((max(len(self.remote_pairs), 1),)),
                     pltpu.SemaphoreType.DMA((max(len(self.remote_pairs), 1),)),
                     pltpu.SemaphoreType.DMA((max(len(self.local_pairs), 1),))]

    def _copies(self, srcs, outs, sems):
        send_sems, recv_sems, loc_sems = sems
        me = _me()

        def dst(a, slot):
            _, key, index, _ = self.items[a]
            return outs[self.keys.index(key)].at[(slot,) + tuple(index)]

        def remote(n, slot):
            a, k = self.remote_pairs[n]
            t = me ^ k
            return pltpu.make_async_remote_copy(
                src_ref=srcs[a].at[t], dst_ref=dst(a, slot), send_sem=send_sems.at[n], recv_sem=recv_sems.at[n],
                device_id=(t // 4, (t // 2) % 2, t % 2), device_id_type=pl.DeviceIdType.MESH)

        local = [pltpu.make_async_copy(srcs[a].at[me], dst(a, me), loc_sems.at[n])
                 for n, (a, _) in enumerate(self.local_pairs)]
        return me, remote, local

    def start(self, srcs, bufs, outs, sems):
        me, remote, local = self._copies(srcs, outs, sems)
        for cp in local:
            cp.start()
        for n in range(len(self.remote_pairs)):
            remote(n, me).start()

    def finish(self, srcs, bufs, outs, sems):
        me, remote, local = self._copies(srcs, outs, sems)
        for n, (_, k) in enumerate(self.remote_pairs):
            remote(n, me ^ k).wait()
        for cp in local:
            cp.wait()


def _call(body, *, name, grid, in_specs, out_specs, out_shape, args, scratch=(), cp=None, ride=None):
    out_specs, out_shape, scratch = list(out_specs), list(out_shape), list(scratch)
    if ride is None:
        outs = _pallas_call(body, name=name, grid=grid, in_specs=list(in_specs), out_specs=tuple(out_specs),
                              out_shape=tuple(out_shape), scratch_shapes=scratch, compiler_params=cp)(*args)
        return list(outs), []
    nin, nout, nscr, n, nb, no = len(in_specs), len(out_specs), len(scratch), ride.n, len(ride.bufs), len(ride.out_shapes)
    steps = list(grid)

    def wrapped(*refs):
        h_in, r_src, r_buf = refs[:nin], refs[nin:nin + n], refs[nin + n:nin + n + nb]
        o0 = nin + n + nb
        h_out, r_out = refs[o0:o0 + nout], refs[o0 + nout:o0 + nout + no]
        s0 = o0 + nout + no
        h_scr, sems = refs[s0:s0 + nscr], refs[s0 + nscr:]
        ids = [pl.program_id(a) for a in range(len(steps))]
        first = functools.reduce(jnp.logical_and, [i == 0 for i in ids])
        last = functools.reduce(jnp.logical_and, [i == s - 1 for i, s in zip(ids, steps)])

        @pl.when(first)
        def _():
            ride.start(r_src, r_buf, r_out, sems)

        body(*h_in, *h_out, *h_scr)

        @pl.when(last)
        def _():
            ride.finish(r_src, r_buf, r_out, sems)

    aliases = {nin + n + k: nout + k for k in range(nb)}
    outs = _pallas_call(
        wrapped, name=name, grid=grid, in_specs=list(in_specs) + [ANY] * (n + nb),
        out_specs=tuple(out_specs + [ANY] * no), out_shape=tuple(out_shape + ride.out_shapes),
        scratch_shapes=scratch + ride.sems, input_output_aliases=aliases, compiler_params=cp,
    )(*args, *ride.srcs, *ride.bufs)
    return list(outs[:nout]), list(outs[nout:])


def _exchange(ride, name):
    def body(dummy_ref, o_ref):
        o_ref[...] = dummy_ref[...]

    one = pl.BlockSpec((8, 128), lambda i: (0, 0))
    _, outs = _call(body, name=name, grid=(1,), in_specs=[one], out_specs=[one], out_shape=[_sds((8, 128), F32)],
                    args=(jnp.zeros((8, 128), F32),), ride=ride)
    return outs


HBM = pl.BlockSpec(memory_space=pltpu.HBM)
SEM = pl.BlockSpec(memory_space=pltpu.SEMAPHORE)


def _scatter_copies(srcs, lands, sems):
    send_sems, recv_sems, loc_sems = sems
    me = _me()

    def remote(a, k, slot):
        t = me ^ k
        return pltpu.make_async_remote_copy(
            src_ref=srcs[a].at[t], dst_ref=lands[a].at[slot], send_sem=send_sems.at[7 * a + k - 1],
            recv_sem=recv_sems.at[7 * a + k - 1], device_id=(t // 4, (t // 2) % 2, t % 2), device_id_type=pl.DeviceIdType.MESH)

    local = [pltpu.make_async_copy(srcs[a].at[me], lands[a].at[me], loc_sems.at[a]) for a in range(len(srcs))]
    return me, remote, local


def scatter_start(payloads, name):
    n = len(payloads)

    def body(*refs):
        srcs, lands, sems = refs[:n], refs[n:2 * n], refs[2 * n:2 * n + 3]
        me, remote, local = _scatter_copies(srcs, lands, sems)
        for a in range(n):
            local[a].start()
            for k in range(1, NDEV):
                remote(a, k, me).start()
        refs[-1][...] = jnp.zeros((8, 128), F32)

    thru = [pltpu.HBM(p.shape, p.dtype) for p in payloads]
    outs = pl.pallas_call(
        body, name=name,
        out_shape=(pltpu.SemaphoreType.DMA((7 * n,)), pltpu.SemaphoreType.DMA((7 * n,)), pltpu.SemaphoreType.DMA((n,)),
                   *thru, *thru, _sds((8, 128), F32)),
        in_specs=[HBM] * (2 * n), out_specs=(SEM, SEM, SEM, *[HBM] * (2 * n), pl.BlockSpec(memory_space=pltpu.VMEM)),
        input_output_aliases={i: 3 + i for i in range(2 * n)},
        compiler_params=pltpu.CompilerParams(has_side_effects=pltpu.SideEffectType.DATAFLOW_SIDE_EFFECTING),
    )(*[pltpu.with_memory_space_constraint(p, pltpu.HBM) for p in payloads],
      *[pltpu.with_memory_space_constraint(lax.empty(p.shape, p.dtype), pltpu.HBM) for p in payloads])
    return (outs[:3], outs[3:3 + n], outs[3 + n:3 + 2 * n]), outs[-1][0, 0]


def scatter_wait(handle, after, name):
    sems, srcs_thru, lands_thru = handle
    n = len(srcs_thru)

    def body(*refs):
        srcs, lands, sems_ = refs[:n], refs[n:2 * n], refs[2 * n:2 * n + 3]
        me, remote, local = _scatter_copies(srcs, lands, sems_)
        for a in range(n):
            for k in range(1, NDEV):
                cp = remote(a, k, me ^ k)
                cp.wait_send()
                cp.wait_recv()
            local[a].wait()

    outs = pl.pallas_call(
        body, name=name, out_shape=tuple(pltpu.HBM(p.shape, p.dtype) for p in (*srcs_thru, *lands_thru)),
        in_specs=[HBM] * (2 * n) + [SEM] * 3 + [ANY], out_specs=tuple([HBM] * (2 * n)),
        input_output_aliases={i: i for i in range(2 * n)},
        compiler_params=pltpu.CompilerParams(has_side_effects=pltpu.SideEffectType.DATAFLOW_SIDE_EFFECTING),
    )(*srcs_thru, *lands_thru, *sems, after)
    return list(outs[n:])


def _row_spec(cols, tm=TM):
    return pl.BlockSpec((tm, cols), lambda i: (i, 0))


def _full_spec(shape):
    nd = len(shape)
    return pl.BlockSpec(shape, lambda i: (0,) * nd)


def ln_mod_fwd(x, mod, sub, name):
    def body(x_ref, mod_ref, h_ref):
        n, _ = _ln_stats(x_ref[...])
        shift = mod_ref[3 * sub:3 * sub + 1, :]
        scale = mod_ref[3 * sub + 1:3 * sub + 2, :]
        h_ref[...] = (n * (1.0 + scale) + shift).astype(MXU)

    return _pallas_call(
        body, name=name, grid=(S // TM,),
        in_specs=[_row_spec(D), _full_spec((9, D))], out_specs=_row_spec(D),
        out_shape=_sds((S, D), MXU), compiler_params=_cp(32, ("arbitrary",)))(x, mod)


def res_ln_fwd(x, f, mod, sub, lng, lnb, w, name, nxt=None):
    def body(x_ref, f_ref, mod_ref, g_ref, b_ref, *rest):
        gate = mod_ref[3 * sub + 2:3 * sub + 3, :]
        r = ALPHA * x_ref[...] + (w * gate) * f_ref[...]
        n, _ = _ln_stats(r)
        xo = n * g_ref[sub:sub + 1, :] + b_ref[sub:sub + 1, :]
        rest[-1 if nxt is None else -2][...] = xo
        if nxt is not None:
            nmod_ref, h_ref = rest[0], rest[-1]
            n2, _ = _ln_stats(xo)
            s2 = nxt[1]
            h_ref[...] = (n2 * (1.0 + nmod_ref[3 * s2 + 1:3 * s2 + 2, :]) + nmod_ref[3 * s2:3 * s2 + 1, :]).astype(MXU)

    more = nxt is not None
    return _pallas_call(
        body, name=name, grid=(S // TM,),
        in_specs=[_row_spec(D), _row_spec(D), _full_spec((9, D)), _full_spec((3, D)), _full_spec((3, D))] + [_full_spec((9, D))] * more,
        out_specs=(_row_spec(D),) + (_row_spec(D),) * more, out_shape=(_sds((S, D), F32),) + (_sds((S, D), MXU),) * more,
        compiler_params=_cp(32, ("arbitrary",)))(x, f, mod, lng, lnb, *([nxt[0]] if more else []))


def res_ln_bwd(x, f, mod, sub, lng, dxo, w, name, ride=None):
    def body(x_ref, f_ref, mod_ref, g_ref, dxo_ref, dxa_ref, df_ref, sums_ref):
        i = pl.program_id(0)
        gate = mod_ref[3 * sub + 2:3 * sub + 3, :]
        fv = f_ref[...]
        r = ALPHA * x_ref[...] + (w * gate) * fv
        n, rstd = _ln_stats(r)
        dxo = dxo_ref[...]
        dr = _ln_bwd(dxo * g_ref[sub:sub + 1, :], n, rstd)
        dxa_ref[...] = ALPHA * dr
        df_ref[...] = ((w * gate) * dr).astype(MXU)
        part = jnp.concatenate([
            jnp.sum(dxo * n, axis=0, keepdims=True),
            jnp.sum(dxo, axis=0, keepdims=True),
            jnp.sum(dr * fv, axis=0, keepdims=True) * w,
            jnp.zeros((5, D), F32)], axis=0)

        @pl.when(i == 0)
        def _():
            sums_ref[...] = part

        @pl.when(i > 0)
        def _():
            sums_ref[...] += part

    return _call(
        body, name=name, grid=(S // TM,),
        in_specs=[_row_spec(D), _row_spec(D), _full_spec((9, D)), _full_spec((3, D)), _row_spec(D)],
        out_specs=(_row_spec(D), _row_spec(D), _full_spec((8, D))),
        out_shape=(_sds((S, D), F32), _sds((S, D), MXU), _sds((8, D), F32)),
        cp=_cp(32, ("arbitrary",)), args=(x, f, mod, lng, dxo), ride=ride)


def ln_mod_bwd(x, dh, mod, sub, dxa, name, ride=None):
    def body(x_ref, dh_ref, mod_ref, dxa_ref, dx_ref, sums_ref):
        i = pl.program_id(0)
        scale = mod_ref[3 * sub + 1:3 * sub + 2, :]
        n, rstd = _ln_stats(x_ref[...])
        dh = dh_ref[...]
        dx_ref[...] = dxa_ref[...] + _ln_bwd(dh * (1.0 + scale), n, rstd)
        part = jnp.concatenate([
            jnp.sum(dh, axis=0, keepdims=True),
            jnp.sum(dh * n, axis=0, keepdims=True),
            jnp.zeros((6, D), F32)], axis=0)

        @pl.when(i == 0)
        def _():
            sums_ref[...] = part

        @pl.when(i > 0)
        def _():
            sums_ref[...] += part

    return _call(
        body, name=name, grid=(S // TM,),
        in_specs=[_row_spec(D), _row_spec(D), _full_spec((9, D)), _row_spec(D)],
        out_specs=(_row_spec(D), _full_spec((8, D))),
        out_shape=(_sds((S, D), F32), _sds((8, D), F32)),
        cp=_cp(32, ("arbitrary",)), args=(x, dh, mod, dxa), ride=ride)


def ln_join_bwd(xp, fp, modp, subp, lngp, lnbp, wp, dh, mod, sub, dxa, name):
    def body(xp_ref, fp_ref, modp_ref, g_ref, b_ref, dh_ref, mod_ref, dxa_ref, dxap_ref, dfp_ref, sumsp_ref, sums_ref):
        i = pl.program_id(0)
        gate = modp_ref[3 * subp + 2:3 * subp + 3, :]
        fv = fp_ref[...]
        n, rstd = _ln_stats(ALPHA * xp_ref[...] + (wp * gate) * fv)
        gain = g_ref[subp:subp + 1, :]
        n2, rstd2 = _ln_stats(n * gain + b_ref[subp:subp + 1, :])
        dh = dh_ref[...]
        dx = dxa_ref[...] + _ln_bwd(dh * (1.0 + mod_ref[3 * sub + 1:3 * sub + 2, :]), n2, rstd2)
        dr = _ln_bwd(dx * gain, n, rstd)
        dxap_ref[...] = ALPHA * dr
        dfp_ref[...] = ((wp * gate) * dr).astype(MXU)
        partp = jnp.concatenate([
            jnp.sum(dx * n, axis=0, keepdims=True), jnp.sum(dx, axis=0, keepdims=True),
            jnp.sum(dr * fv, axis=0, keepdims=True) * wp, jnp.zeros((5, D), F32)], axis=0)
        part = jnp.concatenate([
            jnp.sum(dh, axis=0, keepdims=True), jnp.sum(dh * n2, axis=0, keepdims=True), jnp.zeros((6, D), F32)], axis=0)

        @pl.when(i == 0)
        def _():
            sumsp_ref[...] = partp
            sums_ref[...] = part

        @pl.when(i > 0)
        def _():
            sumsp_ref[...] += partp
            sums_ref[...] += part

    return _pallas_call(
        body, name=name, grid=(S // TM,),
        in_specs=[_row_spec(D), _row_spec(D), _full_spec((9, D)), _full_spec((3, D)), _full_spec((3, D)), _row_spec(D),
                  _full_spec((9, D)), _row_spec(D)],
        out_specs=(_row_spec(D), _row_spec(D), _full_spec((8, D)), _full_spec((8, D))),
        out_shape=(_sds((S, D), F32), _sds((S, D), MXU), _sds((8, D), F32), _sds((8, D), F32)),
        compiler_params=_cp(40, ("arbitrary",)))(xp, fp, modp, lngp, lnbp, dh, mod, dxa)


def loss_fwd_bwd(y, target, name):
    def body(y_ref, t_ref, l_ref, dy_ref):
        i = pl.program_id(0)
        e = y_ref[...] - t_ref[...]
        dy_ref[...] = e * (1.0 / D)
        part = jnp.zeros((8, 128), F32) + (0.5 / D) * jnp.sum(e * e)

        @pl.when(i == 0)
        def _():
            l_ref[...] = part

        @pl.when(i > 0)
        def _():
            l_ref[...] += part

    return _pallas_call(
        body, name=name, grid=(S // TM,),
        in_specs=[_row_spec(D), _row_spec(D)], out_specs=(_full_spec((8, 128)), _row_spec(D)),
        out_shape=(_sds((8, 128), F32), _sds((S, D), F32)),
        compiler_params=_cp(32, ("arbitrary",)))(y, target)


HB = 2 * FBP
NHB = NDEV * FBP // HB
TMB = 1024


def _wrows(buffers=2):
    return pl.BlockSpec((HB, D), lambda j, i: (j, 0), pipeline_mode=pl.Buffered(buffers))


def _resident(shape):
    return pl.BlockSpec(shape, lambda j, i: (0, 0), pipeline_mode=pl.Buffered(1))


def ffn_fwd(h, wgt, wut, wd, name, ride=None):
    def body(h_ref, wg_ref, wu_ref, wd_ref, g_ref, u_ref, f_ref):
        j, i = pl.program_id(0), pl.program_id(1)
        hv = h_ref[...]
        g = _mm_nt(hv, wg_ref[...])
        u = _mm_nt(hv, wu_ref[...])
        g_ref[...] = g.astype(MXU)
        u_ref[...] = u.astype(MXU)
        a = g * jax.nn.sigmoid(g) * u
        part = _mm(a, wd_ref[...])
        rows = pl.ds(pl.multiple_of(i * TMB, TMB), TMB)

        @pl.when(j == 0)
        def _():
            f_ref[rows, :] = part

        @pl.when(j > 0)
        def _():
            f_ref[rows, :] += part

    gu = pl.BlockSpec((TMB, HB), lambda j, i: (i, j))
    return _call(
        body, name=name, grid=(NHB, S // TMB),
        in_specs=[pl.BlockSpec((TMB, D), lambda j, i: (i, 0)), _wrows(), _wrows(), _wrows()],
        out_specs=(gu, gu, _resident((S, D))),
        out_shape=(_sds((S, NDEV * FBP), MXU), _sds((S, NDEV * FBP), MXU), _sds((S, D), F32)),
        cp=_cp(52, ("arbitrary", "arbitrary")), args=(h, wgt, wut, wd), ride=ride)


def ffn_bwd(df, h, g, u, wgt, wut, wd, name, ride=None):
    ni = S // TMB

    def body(df_ref, h_ref, g_ref, u_ref, wg_ref, wu_ref, wd_ref, dwg_ref, dwu_ref, dwd_ref, dh_ref,
             ag_ref, au_ref, ad_ref):
        j, i = pl.program_id(0), pl.program_id(1)
        dfv, hv = df_ref[...], h_ref[...]
        gv, uv = g_ref[...].astype(F32), u_ref[...].astype(F32)
        da = _mm_nt(dfv, wd_ref[...])
        sg = jax.nn.sigmoid(gv)
        silu = gv * sg
        du = da * silu
        dg = da * uv * (sg * (1.0 + gv * (1.0 - sg)))
        p_d = _mm_tn(silu * uv, dfv)
        p_g = _mm_tn(dg, hv)
        p_u = _mm_tn(du, hv)

        @pl.when(i == 0)
        def _():
            ad_ref[...] = p_d
            ag_ref[...] = p_g
            au_ref[...] = p_u

        @pl.when(i > 0)
        def _():
            ad_ref[...] += p_d
            ag_ref[...] += p_g
            au_ref[...] += p_u

        @pl.when(i == ni - 1)
        def _():
            dwd_ref[...] = ad_ref[...].astype(BF16)
            dwg_ref[...] = ag_ref[...].astype(BF16)
            dwu_ref[...] = au_ref[...].astype(BF16)

        part = _mm(dg, wg_ref[...]) + _mm(du, wu_ref[...])
        rows = pl.ds(pl.multiple_of(i * TMB, TMB), TMB)

        @pl.when(j == 0)
        def _():
            dh_ref[rows, :] = part

        @pl.when(j > 0)
        def _():
            dh_ref[rows, :] += part

    gu = pl.BlockSpec((TMB, HB), lambda j, i: (i, j))
    rowt = pl.BlockSpec((TMB, D), lambda j, i: (i, 0))
    return _call(
        body, name=name, grid=(NHB, ni),
        in_specs=[rowt, rowt, gu, gu, _wrows(1), _wrows(1), _wrows(1)],
        out_specs=(_wrows(1), _wrows(1), _wrows(1), _resident((S, D))),
        out_shape=(_sds((NDEV * FBP, D), BF16), _sds((NDEV * FBP, D), BF16), _sds((NDEV * FBP, D), BF16), _sds((S, D), F32)),
        scratch=[pltpu.VMEM((HB, D), F32), pltpu.VMEM((HB, D), F32), pltpu.VMEM((HB, D), F32)],
        cp=_cp(60, ("arbitrary", "arbitrary")), args=(df, h, g, u, wgt, wut, wd), ride=ride)


def win_fwd(h, win, name, ride=None):
    def body(h_ref, w_ref, z_ref):
        hv = h_ref[...]
        for j in range(NDEV):
            z_ref[:, 256 * j:256 * (j + 1)] = _mm(hv, w_ref[j])

    return _call(
        body, name=name, grid=(S // TMM,),
        in_specs=[_row_spec(D, TMM), _full_spec((NDEV, D, 256))],
        out_specs=[_row_spec(D_IN, TMM)], out_shape=[_sds((S, D_IN), F32)],
        cp=_cp(40, ("arbitrary",)), args=(h, win), ride=ride)


def win_bwd(dparts, h, win, name, ride=None):
    ni = S // TMM

    def body(dq_ref, dk_ref, dv_ref, dus_ref, dup_ref, h_ref, w_ref, dh_ref, dw_ref, acc_ref):
        i = pl.program_id(0)
        hv = h_ref[...]
        cols = [dq_ref[:, 0:256], dq_ref[:, 256:512], dk_ref[:, 0:256], dk_ref[:, 256:512],
                dv_ref[:, 0:256], dv_ref[:, 256:512], dus_ref[...], dup_ref[...]]
        dh = jnp.zeros((TMM, D), F32)
        for j in range(NDEV):
            dz = cols[j].astype(MXU)
            dh = dh + _mm_nt(dz, w_ref[j])
            p = _mm_tn(hv, dz)

            @pl.when(i == 0)
            def _():
                acc_ref[j] = p

            @pl.when(i > 0)
            def _():
                acc_ref[j] += p

        dh_ref[...] = dh

        @pl.when(i == ni - 1)
        def _():
            dw_ref[...] = acc_ref[...].astype(BF16)

    return _call(
        body, name=name, grid=(ni,),
        in_specs=[_row_spec(512, TMM), _row_spec(512, TMM), _row_spec(512, TMM), _row_spec(256, TMM), _row_spec(256, TMM),
                  _row_spec(D, TMM), _full_spec((NDEV, D, 256))],
        out_specs=(_row_spec(D, TMM), _full_spec((NDEV, D, 256))),
        out_shape=(_sds((S, D), F32), _sds((NDEV, D, 256), BF16)),
        scratch=[pltpu.VMEM((NDEV, D, 256), F32)],
        cp=_cp(48, ("arbitrary",)), args=(*dparts, h, win), ride=ride)


def wout_fwd(ya, ys, yp, wout, name, ride=None):
    def body(ya_ref, ys_ref, yp_ref, w_ref, o_ref):
        w = w_ref[...].reshape(D, D)
        o_ref[...] = _mm(ya_ref[...], w[0:512]) + _mm(ys_ref[...], w[512:768]) + _mm(yp_ref[...], w[768:1024])

    return _call(
        body, name=name, grid=(S // TMM,),
        in_specs=[_row_spec(512, TMM), _row_spec(256, TMM), _row_spec(256, TMM), _full_spec((NDEV, 128, D))],
        out_specs=[_row_spec(D, TMM)], out_shape=[_sds((S, D), F32)],
        cp=_cp(40, ("arbitrary",)), args=(ya, ys, yp, wout), ride=ride)


def wout_bwd(do, ya, ys, yp, wout, name, ride=None):
    ni = S // TMM

    def body(do_ref, ya_ref, ys_ref, yp_ref, w_ref, dya_ref, dys_ref, dyp_ref, dw_ref, acc_ref):
        i = pl.program_id(0)
        w = w_ref[...].reshape(D, D)
        dov = do_ref[...]
        dya_ref[...] = _mm_nt(dov, w[0:512])
        dys_ref[...] = _mm_nt(dov, w[512:768])
        dyp_ref[...] = _mm_nt(dov, w[768:1024])
        parts = [(0, 512, _mm_tn(ya_ref[...], dov)), (512, 768, _mm_tn(ys_ref[...], dov)),
                 (768, 1024, _mm_tn(yp_ref[...], dov))]
        for lo, hi, p in parts:
            @pl.when(i == 0)
            def _():
                acc_ref[lo:hi, :] = p

            @pl.when(i > 0)
            def _():
                acc_ref[lo:hi, :] += p

        @pl.when(i == ni - 1)
        def _():
            dw_ref[...] = acc_ref[...].astype(BF16).reshape(NDEV, 128, D)

    return _call(
        body, name=name, grid=(ni,),
        in_specs=[_row_spec(D, TMM), _row_spec(512, TMM), _row_spec(256, TMM), _row_spec(256, TMM),
                  _full_spec((NDEV, 128, D))],
        out_specs=(_row_spec(512, TMM), _row_spec(256, TMM), _row_spec(256, TMM), _full_spec((NDEV, 128, D))),
        out_shape=(_sds((S, 512), F32), _sds((S, 256), F32), _sds((S, 256), F32), _sds((NDEV, 128, D), BF16)),
        scratch=[pltpu.VMEM((D, D), F32)],
        cp=_cp(40, ("arbitrary",)), args=(do, ya, ys, yp, wout), ride=ride)


def _t5_bucket(dist):
    max_exact = N_BUCKETS // 2
    d = np.maximum(dist, 1).astype(np.float32)
    large = max_exact + (np.log(d / max_exact) / math.log(MAX_DISTANCE / max_exact)
                         * (N_BUCKETS - max_exact)).astype(np.int32)
    large = np.minimum(large, N_BUCKETS - 1)
    return np.where(dist < max_exact, dist, large).astype(np.int32)


def _att_static():
    i = np.arange(QB)[:, None]
    j = np.arange(2 * QB)[None, :]
    r = i + QB - j
    buckets, bands = [], []
    for window, dil in PATTERNS:
        bands.append((r >= 0) & (r <= window // dil))
        buckets.append(_t5_bucket(np.clip(r, 0, None) * dil))
    return np.stack(buckets), np.stack(bands), np.broadcast_to(j >= QB, (QB, 2 * QB))


def att_bias(rel_bias):
    m = np.arange(2 * QB)
    rows = []
    for window, dil in PATTERNS:
        r = QB - m
        ok = (r >= 0) & (r <= window // dil)
        b = rel_bias[_t5_bucket(np.clip(r, 0, None) * dil)]
        rows.append(jnp.where(ok[:, None], b, NEG).T)
    return jnp.broadcast_to(jnp.stack(rows)[:, :, None, :], (3, N_HEADS, 8, 2 * QB))


def _bias_tiles(t_ref, tiles):
    col = lax.broadcasted_iota(jnp.int32, (QB, 2 * QB), 1)
    for p in range(3):
        for hh in range(2):
            t = pltpu.roll(jnp.broadcast_to(t_ref[p, hh, 0:1, :], (QB, 2 * QB)), 0, 1, stride=1, stride_axis=0)
            tiles[p, hh, 0] = t
            tiles[p, hh, 1] = jnp.where(col >= QB, t, NEG)


def _permute_in(dst_ref, src_ref, d, scale=None, pad=QB):
    L = S // d
    for r in range(d):
        v = src_ref[pl.ds(r, L, stride=d), :] if d > 1 else src_ref[...]
        if scale is not None:
            v = v * scale
        dst_ref[pad + r * L:pad + (r + 1) * L, :] = v.astype(dst_ref.dtype)


def att_fwd(z, bias, name, ride=None):
    def body(q_ref, k_ref, v_ref, t_ref, y_ref, l_ref, qs, ks, vs, o_perm, l_perm, o_nat, l_nat, b_ref):
        _bias_tiles(t_ref, b_ref)
        zero_pad = jnp.zeros((QB, 128), MXU)
        ks[0:QB, :] = zero_pad
        vs[0:QB, :] = zero_pad
        lane = lax.broadcasted_iota(jnp.int32, (QB, 128), 1)
        for p, (_, d) in enumerate(PATTERNS):
            L = S // d
            nb = L // QB
            _permute_in(qs, q_ref, d, scale=0.125, pad=0)
            _permute_in(ks, k_ref, d)
            _permute_in(vs, v_ref, d)

            def blk(b, carry):
                r0 = pl.multiple_of(b * QB, QB)
                q = qs[pl.ds(r0, QB), :]
                kb = ks[pl.ds(r0, 2 * QB), :]
                vb = vs[pl.ds(r0, 2 * QB), :]
                first = ((b % nb) == 0).astype(jnp.int32)
                res = []
                for hh in range(2):
                    sel = (lane < 64) if hh == 0 else (lane >= 64)
                    qm = jnp.where(sel, q, jnp.zeros_like(q))
                    s = _mm_nt(qm, kb) + b_ref[p, hh, first]
                    m = jnp.max(s, axis=1, keepdims=True)
                    pe = jnp.exp(s - m)
                    den = jnp.sum(pe, axis=1, keepdims=True)
                    res.append((_mm(pe, vb) / den, m + jnp.log(den)))
                o_perm[pl.ds(r0, QB), :] = jnp.where(lane < 64, res[0][0], res[1][0])
                l_perm[pl.ds(r0, QB), :] = jnp.where(lane < 64, res[0][1], res[1][1])
                return carry

            lax.fori_loop(0, S // QB, blk, 0, unroll=8)
            for r in range(d):
                if d > 1:
                    o_nat[p, pl.ds(r, L, stride=d), :] = o_perm[r * L:(r + 1) * L, :]
                    l_nat[p, pl.ds(r, L, stride=d), :] = l_perm[r * L:(r + 1) * L, :]
                else:
                    o_nat[p] = o_perm[...]
                    l_nat[p] = l_perm[...]
        l0, l1, l2 = l_nat[0], l_nat[1], l_nat[2]
        m = jnp.maximum(jnp.maximum(l0, l1), l2)
        e0, e1, e2 = jnp.exp(l0 - m), jnp.exp(l1 - m), jnp.exp(l2 - m)
        den = e0 + e1 + e2
        y_ref[...] = (e0 * o_nat[0] + e1 * o_nat[1] + e2 * o_nat[2]) / den
        l_ref[...] = m + jnp.log(den)

    col = lambda c0: pl.BlockSpec((S, 128), lambda hp: (0, c0 + hp))
    return _call(
        body, name=name, grid=(N_HEADS // 2,),
        in_specs=[col(0), col(4), col(8), pl.BlockSpec((3, 2, 8, 2 * QB), lambda hp: (0, hp, 0, 0))],
        out_specs=(col(0), col(0)),
        out_shape=(_sds((S, D_ATT), F32), _sds((S, D_ATT), F32)),
        scratch=[pltpu.VMEM((S, 128), MXU), pltpu.VMEM((S + QB, 128), MXU), pltpu.VMEM((S + QB, 128), MXU),
                 pltpu.VMEM((S, 128), F32), pltpu.VMEM((S, 128), F32),
                 pltpu.VMEM((3, S, 128), F32), pltpu.VMEM((3, S, 128), F32),
                 pltpu.VMEM((3, 2, 2, QB, 2 * QB), F32)],
        cp=_cp(40, ("arbitrary",)), args=(z, z, z, bias), ride=ride)


def att_bwd(z, bias, y, lse, dy, name, ride=None):
    def body(q_ref, k_ref, v_ref, t_ref, y_ref, l_ref, dy_ref, dq_ref, dk_ref, dv_ref, db_ref,
             qs, ks, vs, dys, ls, dds, dn_nat, dq_perm, dk_perm, dv_perm, b_ref):
        _bias_tiles(t_ref, b_ref)
        zero_pad = jnp.zeros((QB, 128), MXU)
        ks[0:QB, :] = zero_pad
        vs[0:QB, :] = zero_pad
        lane = lax.broadcasted_iota(jnp.int32, (QB, 128), 1)
        lane_s = lax.broadcasted_iota(jnp.int32, (S, 128), 1)
        t = dy_ref[...] * y_ref[...]
        sa = jnp.sum(jnp.where(lane_s < 64, t, 0.0), axis=1, keepdims=True)
        sb = jnp.sum(jnp.where(lane_s >= 64, t, 0.0), axis=1, keepdims=True)
        dn_nat[...] = jnp.where(lane_s < 64, sa, sb)
        dq_ref[...] = jnp.zeros((S, 128), F32)
        dk_ref[...] = jnp.zeros((S, 128), F32)
        dv_ref[...] = jnp.zeros((S, 128), F32)
        db_ref[...] = jnp.zeros((3, 2, QB, 2 * QB), F32)
        for p, (_, d) in enumerate(PATTERNS):
            L = S // d
            nb = L // QB
            _permute_in(qs, q_ref, d, scale=0.125, pad=0)
            _permute_in(ks, k_ref, d)
            _permute_in(vs, v_ref, d)
            _permute_in(dys, dy_ref, d, pad=0)
            _permute_in(ls, l_ref, d, pad=0)
            _permute_in(dds, dn_nat, d, pad=0)
            dk_perm[...] = jnp.zeros((S + QB, 128), F32)
            dv_perm[...] = jnp.zeros((S + QB, 128), F32)

            def blk(b, carry):
                r0 = pl.multiple_of(b * QB, QB)
                q = qs[pl.ds(r0, QB), :]
                kb = ks[pl.ds(r0, 2 * QB), :]
                vb = vs[pl.ds(r0, 2 * QB), :]
                dyb = dys[pl.ds(r0, QB), :]
                lb = ls[pl.ds(r0, QB), :]
                db = dds[pl.ds(r0, QB), :]
                first = ((b % nb) == 0).astype(jnp.int32)
                lane2 = jnp.concatenate([lane, lane], axis=0)
                own = (lane2 >> 6) == (lax.broadcasted_iota(jnp.int32, (2 * QB, 128), 0) >> 7)
                qm = jnp.where(own, jnp.concatenate([q, q], axis=0), jnp.zeros((2 * QB, 128), q.dtype))
                dym = jnp.where(own, jnp.concatenate([dyb, dyb], axis=0), jnp.zeros((2 * QB, 128), dyb.dtype))
                wide = lambda t: jnp.concatenate([jnp.broadcast_to(t[:, 0:1], (QB, 2 * QB)), jnp.broadcast_to(t[:, 64:65], (QB, 2 * QB))], axis=0)
                lse2, dd2 = wide(lb), wide(db)
                bias2 = jnp.concatenate([b_ref[p, 0, first], b_ref[p, 1, first]], axis=0)
                pr = jnp.exp(_mm_nt(qm, kb) + bias2 - lse2)
                ds = pr * (_mm_nt(dym, vb) - dd2)
                db_ref[p, 0] += ds[0:QB]
                db_ref[p, 1] += ds[QB:2 * QB]
                dq2 = _mm(ds, kb)
                dqs = [dq2[0:QB], dq2[QB:2 * QB]]
                dkb = _mm_tn(ds, qm)
                dvb = _mm_tn(pr, dym)
                dq_perm[pl.ds(r0, QB), :] = jnp.where(lane < 64, dqs[0], dqs[1])
                dk_perm[pl.ds(r0, 2 * QB), :] += dkb
                dv_perm[pl.ds(r0, 2 * QB), :] += dvb
                return carry

            lax.fori_loop(0, S // QB, blk, 0, unroll=4)
            for r in range(d):
                idx = pl.ds(r, L, stride=d) if d > 1 else pl.ds(0, S)
                dq_ref[idx, :] += dq_perm[r * L:(r + 1) * L, :] * 0.125
                dk_ref[idx, :] += dk_perm[QB + r * L:QB + (r + 1) * L, :]
                dv_ref[idx, :] += dv_perm[QB + r * L:QB + (r + 1) * L, :]

    col = lambda c0: pl.BlockSpec((S, 128), lambda hp: (0, c0 + hp))
    bspec = pl.BlockSpec((3, 2, 8, 2 * QB), lambda hp: (0, hp, 0, 0))
    return _call(
        body, name=name, grid=(N_HEADS // 2,),
        in_specs=[col(0), col(4), col(8), bspec, col(0), col(0), col(0)],
        out_specs=(col(0), col(0), col(0), pl.BlockSpec((3, 2, QB, 2 * QB), lambda hp: (0, hp, 0, 0))),
        out_shape=(_sds((S, D_ATT), F32), _sds((S, D_ATT), F32), _sds((S, D_ATT), F32),
                   _sds((3, N_HEADS, QB, 2 * QB), F32)),
        scratch=[pltpu.VMEM((S, 128), MXU), pltpu.VMEM((S + QB, 128), MXU), pltpu.VMEM((S + QB, 128), MXU),
                 pltpu.VMEM((S, 128), MXU), pltpu.VMEM((S, 128), F32), pltpu.VMEM((S, 128), F32),
                 pltpu.VMEM((S, 128), F32), pltpu.VMEM((S, 128), F32),
                 pltpu.VMEM((S + QB, 128), F32), pltpu.VMEM((S + QB, 128), F32),
                 pltpu.VMEM((3, 2, 2, QB, 2 * QB), F32)],
        cp=_cp(48, ("arbitrary",)), args=(z, z, z, bias, y, lse, dy), ride=ride)


def relbias_grad(dbiases):
    bucket, band, _ = _att_static()
    onehot = (bucket[:, None] == np.arange(N_BUCKETS)[None, :, None, None]) & band[:, None]
    onehot = jnp.asarray(onehot.reshape(3, N_BUCKETS, QB * 2 * QB), BF16)

    def body(db0_ref, db1_ref, oh_ref, o_ref):
        acc = jnp.zeros((N_HEADS, N_BUCKETS), F32)
        for p in range(3):
            acc = acc + lax.dot_general(db0_ref[p] + db1_ref[p], oh_ref[p].astype(F32), (((1,), (1,)), ((), ())),
                                        preferred_element_type=F32, precision=lax.Precision.HIGHEST)
        o_ref[...] = acc

    vm = pl.BlockSpec(memory_space=pltpu.VMEM)
    out = _pallas_call(body, name="relbias_grad", in_specs=[vm, vm, vm], out_specs=vm,
                         out_shape=_sds((N_HEADS, N_BUCKETS), F32), compiler_params=_cp(40))(
        *[d.reshape(3, N_HEADS, QB * 2 * QB) for d in dbiases], onehot)
    return out.T


def _panel(t_ref, ri, j):
    return t_ref[ri, pl.ds(j, S, stride=8), :]


def _gelu(x):
    c = math.sqrt(2.0 / math.pi)
    th = jnp.tanh(c * (x + 0.044715 * x * x * x))
    return 0.5 * x * (1.0 + th), th


def ssm_fwd(z, a, bre, bim, cre, cim, dsk, gluw, glub, name, ride=None):
    def body(u_ref, a_ref, bre_ref, bim_ref, cre_ref, cim_ref, d_ref, gw_ref, gb_ref, y_ref, yp_ref, st_hbm, st_ref):
        u = u_ref[...]
        for j in range(8):
            st_ref[0, pl.ds(j, S, stride=8), :] = _mm(u, bre_ref[:, 128 * j:128 * (j + 1)])
            st_ref[1, pl.ds(j, S, stride=8), :] = _mm(u, bim_ref[:, 128 * j:128 * (j + 1)])
        ar, ai = a_ref[0], a_ref[1]

        def step(t, c):
            re, im = c
            i = pl.multiple_of(t * 8, 8)
            nre = ar * re - ai * im + st_ref[0, pl.ds(i, 8), :]
            nim = ar * im + ai * re + st_ref[1, pl.ds(i, 8), :]
            st_ref[0, pl.ds(i, 8), :] = nre
            st_ref[1, pl.ds(i, 8), :] = nim
            return nre, nim

        zero = jnp.zeros((8, 128), F32)
        lax.fori_loop(0, S, step, (zero, zero), unroll=8)
        y = d_ref[...] * u
        for j in range(8):
            y = y + _mm(_panel(st_ref, 0, j), cre_ref[128 * j:128 * (j + 1), :])
            y = y - _mm(_panel(st_ref, 1, j), cim_ref[128 * j:128 * (j + 1), :])
        pltpu.sync_copy(st_ref, st_hbm)
        yp_ref[...] = y
        gl, _ = _gelu(y)
        tt = _mm(gl, gw_ref[...].reshape(D_SSM, D_SSM)) + gb_ref[...]
        y_ref[...] = y * jax.nn.sigmoid(tt)

    vm = lambda shape: pl.BlockSpec(shape, lambda i: (0,) * len(shape))
    return _call(
        body, name=name, grid=(1,),
        in_specs=[pl.BlockSpec((S, 256), lambda i: (0, 6)), vm((2, 8, 128)), vm((256, 1024)), vm((256, 1024)),
                  vm((1024, 256)), vm((1024, 256)), vm((1, 256)),
                  vm((NDEV, 32, 256)), vm((1, 256))],
        out_specs=(vm((S, 256)), vm((S, 256)), pl.BlockSpec(memory_space=pl.ANY)),
        out_shape=(_sds((S, 256), F32), _sds((S, 256), F32), _sds((2, S * 8, 128), F32)),
        scratch=[pltpu.VMEM((2, S * 8, 128), F32)],
        cp=_cp(40, ("arbitrary",)), args=(z, a, bre, bim, cre, cim, dsk, gluw, glub), ride=ride)


def ssm_bwd(dy, z, ypre, st, a, bre, bim, cre, cim, dsk, gluw, glub, name, ride=None):
    def body(dy_ref, u_ref, yp_ref, st_hbm, a_ref, bre_ref, bim_ref, cre_ref, cim_ref, d_ref, gw_ref, gb_ref,
             du_ref, dbre_ref, dbim_ref, dcre_ref, dcim_ref, da_ref, dd_ref, dgw_ref, dgb_ref, g_ref, st_ref):
        pltpu.sync_copy(st_hbm, st_ref)
        u = u_ref[...]
        y = yp_ref[...]
        dout = dy_ref[...]
        gw = gw_ref[...].reshape(D_SSM, D_SSM)
        gl, th = _gelu(y)
        sig = jax.nn.sigmoid(_mm(gl, gw) + gb_ref[...])
        dt = dout * y * sig * (1.0 - sig)
        dgw_ref[...] = _mm_tn(gl, dt)
        dgb_ref[...] = jnp.sum(dt, axis=0, keepdims=True)
        c = math.sqrt(2.0 / math.pi)
        dgelu = 0.5 * (1.0 + th) + 0.5 * y * (1.0 - th * th) * c * (1.0 + 3.0 * 0.044715 * y * y)
        dyv = dout * sig + _mm_nt(dt, gw) * dgelu
        dd_ref[...] = jnp.sum(dyv * u, axis=0, keepdims=True)
        for j in range(8):
            rows = slice(128 * j, 128 * (j + 1))
            g_ref[0, pl.ds(j, S, stride=8), :] = _mm_nt(dyv, cre_ref[rows, :])
            g_ref[1, pl.ds(j, S, stride=8), :] = -_mm_nt(dyv, cim_ref[rows, :])
            dcre_ref[rows, :] = _mm_tn(_panel(st_ref, 0, j), dyv)
            dcim_ref[rows, :] = -_mm_tn(_panel(st_ref, 1, j), dyv)
        ar, ai = a_ref[0], a_ref[1]

        def step(k, c4):
            gre, gim, dar, dai = c4
            i = pl.multiple_of((S - 1 - k) * 8, 8)
            nre = g_ref[0, pl.ds(i, 8), :] + ar * gre + ai * gim
            nim = g_ref[1, pl.ds(i, 8), :] + ar * gim - ai * gre
            g_ref[0, pl.ds(i, 8), :] = nre
            g_ref[1, pl.ds(i, 8), :] = nim
            sre = st_ref[0, pl.ds(i - 8, 8), :]
            sim = st_ref[1, pl.ds(i - 8, 8), :]
            return nre, nim, dar + nre * sre + nim * sim, dai + nim * sre - nre * sim

        zero = jnp.zeros((8, 128), F32)
        gre, gim, dar, dai = lax.fori_loop(0, S - 1, step, (zero, zero, zero, zero), unroll=8)
        g_ref[0, 0:8, :] = g_ref[0, 0:8, :] + ar * gre + ai * gim
        g_ref[1, 0:8, :] = g_ref[1, 0:8, :] + ar * gim - ai * gre
        da_ref[0] = dar
        da_ref[1] = dai
        du = dyv * d_ref[...]
        for j in range(8):
            cols = slice(128 * j, 128 * (j + 1))
            gr, gi = _panel(g_ref, 0, j), _panel(g_ref, 1, j)
            dbre_ref[:, cols] = _mm_tn(u, gr)
            dbim_ref[:, cols] = _mm_tn(u, gi)
            du = du + _mm_nt(gr, bre_ref[:, cols]) + _mm_nt(gi, bim_ref[:, cols])
        du_ref[...] = du

    vm = lambda shape: pl.BlockSpec(shape, lambda i: (0,) * len(shape))
    return _call(
        body, name=name, grid=(1,),
        in_specs=[vm((S, 256)), pl.BlockSpec((S, 256), lambda i: (0, 6)), vm((S, 256)), pl.BlockSpec(memory_space=pl.ANY),
                  vm((2, 8, 128)), vm((256, 1024)), vm((256, 1024)), vm((1024, 256)), vm((1024, 256)), vm((1, 256)),
                  vm((NDEV, 32, 256)), vm((1, 256))],
        out_specs=(vm((S, 256)), vm((256, 1024)), vm((256, 1024)), vm((1024, 256)), vm((1024, 256)),
                   vm((2, 8, 128)), vm((1, 256)), vm((256, 256)), vm((1, 256))),
        out_shape=(_sds((S, 256), F32), _sds((256, 1024), F32), _sds((256, 1024), F32), _sds((1024, 256), F32),
                   _sds((1024, 256), F32), _sds((2, 8, 128), F32), _sds((1, 256), F32), _sds((256, 256), F32),
                   _sds((1, 256), F32)),
        scratch=[pltpu.VMEM((2, S * 8, 128), F32), pltpu.VMEM((2, S * 8, 128), F32)],
        cp=_cp(56, ("arbitrary",)), args=(dy, z, ypre, st, a, bre, bim, cre, cim, dsk, gluw, glub), ride=ride)


def _ssm_discretise(a_re, a_im, log_dt, b_re, b_im):
    dt = jnp.exp(log_dt)[:, None]
    er = jnp.exp(a_re * dt)
    abr, abi = er * jnp.cos(a_im * dt), er * jnp.sin(a_im * dt)
    den = a_re * a_re + a_im * a_im
    fr = ((abr - 1.0) * a_re + abi * a_im) / den
    fi = (abi * a_re - (abr - 1.0) * a_im) / den
    bbr = fr[:, :, None] * b_re - fi[:, :, None] * b_im
    bbi = fr[:, :, None] * b_im + fi[:, :, None] * b_re
    return abr, abi, bbr, bbi


def _blockdiag(t):
    g, r, c = t.shape
    eye = jnp.eye(g, dtype=t.dtype)
    return (t[:, :, None, :] * eye[:, None, :, None]).reshape(g * r, g * c)


def _blockdiag_take(m, r, c):
    g = m.shape[0] // r
    idx = jnp.arange(g)
    return m.reshape(g, r, g, c)[idx, :, idx, :]


PAD = 16


def _pool_lane_select(vals):
    lane = lax.broadcasted_iota(jnp.int32, vals[0].shape, 1)
    out = vals[3]
    for g in (2, 1, 0):
        out = jnp.where(lane < 64 * (g + 1), vals[g], out)
    return out


def _pool_counts():
    row = lax.broadcasted_iota(jnp.int32, (S, D_POOL), 0).astype(F32) + 1.0
    return _pool_lane_select([jnp.minimum(row, float(w)) for w in POOL_WINDOWS])


def _pooled(u, sa, sb):
    sums = []
    cur = u
    bufs = (sa, sb)
    for k, sh in enumerate((1, 2, 4, 8)):
        buf = bufs[k % 2]
        buf[PAD:PAD + S, :] = cur
        cur = cur + buf[PAD - sh:PAD - sh + S, :]
        sums.append(cur)
    return _pool_lane_select(sums) / _pool_counts() - u


def pool_fwd(z, pw, psc, name):
    def body(u_ref, w_ref, s_ref, y_ref, sa, sb):
        for buf in (sa, sb):
            buf[0:PAD, :] = jnp.zeros((PAD, D_POOL), F32)
        pooled = _pooled(u_ref[...], sa, sb)
        y_ref[...] = _mm(pooled, w_ref[...]) * s_ref[...]

    vm = lambda shape: pl.BlockSpec(shape, lambda i: (0,) * len(shape))
    return _pallas_call(
        body, name=name, grid=(1,),
        in_specs=[pl.BlockSpec((S, 256), lambda i: (0, 7)), vm((256, 256)), vm((1, 256))],
        out_specs=vm((S, 256)), out_shape=_sds((S, 256), F32),
        scratch_shapes=[pltpu.VMEM((S + 2 * PAD, D_POOL), F32)] * 2,
        compiler_params=_cp(40, ("arbitrary",)))(z, pw, psc)


def pool_bwd(dy, z, pw, psc, name):
    def body(dy_ref, u_ref, w_ref, s_ref, du_ref, dw_ref, ds_ref, sa, sb):
        for buf in (sa, sb):
            buf[0:PAD, :] = jnp.zeros((PAD, D_POOL), F32)
            buf[PAD + S:PAD + S + PAD, :] = jnp.zeros((PAD, D_POOL), F32)
        pooled = _pooled(u_ref[...], sa, sb)
        dyv = dy_ref[...]
        w = w_ref[...]
        ds_ref[...] = jnp.sum(dyv * _mm(pooled, w), axis=0, keepdims=True)
        dyl = dyv * s_ref[...]
        dw_ref[...] = _mm_tn(pooled, dyl)
        dpool = _mm_nt(dyl, w)
        cur = dpool / _pool_counts()
        sums = []
        bufs = (sa, sb)
        for k, sh in enumerate((1, 2, 4, 8)):
            buf = bufs[k % 2]
            buf[PAD:PAD + S, :] = cur
            cur = cur + buf[PAD + sh:PAD + sh + S, :]
            sums.append(cur)
        du_ref[...] = _pool_lane_select(sums) - dpool

    vm = lambda shape: pl.BlockSpec(shape, lambda i: (0,) * len(shape))
    return _pallas_call(
        body, name=name, grid=(1,),
        in_specs=[vm((S, 256)), pl.BlockSpec((S, 256), lambda i: (0, 7)), vm((256, 256)), vm((1, 256))],
        out_specs=(vm((S, 256)), vm((256, 256)), vm((1, 256))),
        out_shape=(_sds((S, 256), F32), _sds((256, 256), F32), _sds((1, 256), F32)),
        scratch_shapes=[pltpu.VMEM((S + 2 * PAD, D_POOL), F32)] * 2,
        compiler_params=_cp(40, ("arbitrary",)))(dy, z, pw, psc)


def ada_fwd(c_all, ada_w, ada_b_cols):
    def body(c_ref, w_ref, b_ref, o_ref):
        c = c_ref[...]
        cond = c * jax.nn.sigmoid(c)
        o_ref[...] = jnp.dot(cond, w_ref[...], preferred_element_type=F32, precision=lax.Precision.HIGHEST) + b_ref[...]

    return _pallas_call(
        body, name="ada_fwd", grid=(DEPTH,),
        in_specs=[pl.BlockSpec((NDEV, D), lambda l: (0, 0)), pl.BlockSpec((None, D, 1152), lambda l: (l, 0, 0)),
                  pl.BlockSpec((None, 1, 1152), lambda l: (l, 0, 0))],
        out_specs=pl.BlockSpec((None, NDEV, 1152), lambda l: (l, 0, 0)), out_shape=_sds((DEPTH, NDEV, 1152), F32),
        compiler_params=_cp(40, ("arbitrary",)))(c_all, ada_w, ada_b_cols)


def ada_bwd(c_all, dmod_cols):
    def body(c_ref, dm_ref, o_ref):
        c = c_ref[...]
        cond = c * jax.nn.sigmoid(c)
        o_ref[...] = lax.dot_general(cond, dm_ref[...], (((0,), (0,)), ((), ())), preferred_element_type=F32,
                                     precision=lax.Precision.HIGHEST)

    return _pallas_call(
        body, name="ada_bwd", grid=(DEPTH,),
        in_specs=[pl.BlockSpec((NDEV, D), lambda l: (0, 0)), pl.BlockSpec((None, NDEV, 1152), lambda l: (l, 0, 0))],
        out_specs=pl.BlockSpec((None, D, 1152), lambda l: (l, 0, 0)), out_shape=_sds((DEPTH, D, 1152), F32),
        compiler_params=_cp(40, ("arbitrary",)))(c_all, dmod_cols)


def _adamw(w, g, m, v):
    m2 = B1 * m + (1.0 - B1) * g
    v2 = B2 * v + (1.0 - B2) * (g * g)
    m_hat = m2 / (1.0 - B1 ** STEP)
    v_hat = v2 / (1.0 - B2 ** STEP)
    return -LR * (m_hat / (jnp.sqrt(v_hat) + EPS) + WD * w), m2, v2


def _sum8(ref):
    g = ref[0].astype(F32)
    for s in range(1, NDEV):
        g = g + ref[s].astype(F32)
    return g


def adam_rs(recv, w, m, v, tr, name, ride=None):
    lead, (r, cdim) = w.shape[:-2], w.shape[-2:]
    cp = recv.shape[-1]
    nl = len(lead)

    def body(rc_ref, w_ref, m_ref, v_ref, g_ref, d_ref, m2_ref, v2_ref):
        g = _sum8(rc_ref)[:, :cdim]
        g_ref[...] = g
        d_ref[...], m2_ref[...], v2_ref[...] = _adamw(w_ref[...], g, m_ref[...], v_ref[...])

    rs = pl.BlockSpec((None,) * nl + (tr, cdim), lambda *i: (*i, 0))
    return _call(
        body, name=name, grid=lead + (r // tr,),
        in_specs=[pl.BlockSpec((NDEV,) + (None,) * nl + (tr, cp), lambda *i: (0, *i, 0)), rs, rs, rs],
        out_specs=(rs, rs, rs, rs), out_shape=tuple(_sds(w.shape, F32) for _ in range(4)),
        cp=_cp(48, ("arbitrary",) * (nl + 1)), args=(recv, w, m, v), ride=ride)


def adam_block(recv, w, m, v, lf, prev, name):
    half = FB // 2

    def body(*refs):
        rc_ref, w_ref, m_ref, v_ref = refs[:4]
        g_ref, d_ref, m2_ref, v2_ref = refs[-4:]
        g = _sum8(rc_ref)
        g_ref[...] = g
        d_ref[...], m2_ref[...], v2_ref[...] = _adamw(w_ref[...], g, m_ref[...], v_ref[...])

    rs = pl.BlockSpec((None, None, half, D), lambda i: (lf // 2, lf % 2, i, 0))
    prev = list(prev) if prev is not None else []
    return list(_pallas_call(
        body, name=name, grid=(2,), in_specs=[pl.BlockSpec((NDEV, half, D), lambda i: (0, i, 0)), rs, rs, rs] + [ANY] * len(prev),
        out_specs=(rs, rs, rs, rs), out_shape=tuple(_sds((DEPTH, 2, FB, D), F32) for _ in range(4)),
        input_output_aliases={4 + k: k for k in range(len(prev))},
        compiler_params=_cp(48, ("arbitrary",)))(recv, w, m, v, *prev))


def adam_plain(g, w, m, v, tr, name, ride=None):
    lead, (r, cdim) = w.shape[:-2], w.shape[-2:]
    nl = len(lead)

    def body(g_ref, w_ref, m_ref, v_ref, d_ref, m2_ref, v2_ref):
        d_ref[...], m2_ref[...], v2_ref[...] = _adamw(w_ref[...], g_ref[...], m_ref[...], v_ref[...])

    rs = pl.BlockSpec((None,) * nl + (tr, cdim), lambda *i: (*i, 0))
    return _call(
        body, name=name, grid=lead + (r // tr,), in_specs=[rs, rs, rs, rs], out_specs=(rs, rs, rs),
        out_shape=tuple(_sds(w.shape, F32) for _ in range(3)),
        cp=_cp(48, ("arbitrary",) * (nl + 1)), args=(g, w, m, v), ride=ride)


def adam_native(gs, ws, ms, vs, name):
    n = len(ws)

    def body(*refs):
        g_refs, w_refs, m_refs, v_refs = (refs[k * n:(k + 1) * n] for k in range(4))
        d_refs, m2_refs, v2_refs = (refs[(4 + k) * n:(5 + k) * n] for k in range(3))
        for a in range(n):
            d_refs[a][...], m2_refs[a][...], v2_refs[a][...] = _adamw(w_refs[a][...], g_refs[a][...], m_refs[a][...], v_refs[a][...])

    vm = pl.BlockSpec(memory_space=pltpu.VMEM)
    outs = _pallas_call(body, name=name, in_specs=[vm] * (4 * n), out_specs=tuple([vm] * (3 * n)),
                        out_shape=tuple(_sds(w.shape, F32) for w in ws) * 3, compiler_params=_cp(40))(*gs, *ws, *ms, *vs)
    return outs[:n], outs[n:2 * n], outs[2 * n:]


def sum_sources(recv, name):
    r = recv.shape[1]

    def body(rc_ref, o_ref):
        o_ref[...] = _sum8(rc_ref)

    vm = pl.BlockSpec(memory_space=pltpu.VMEM)
    return _pallas_call(body, name=name, in_specs=[vm], out_specs=vm, out_shape=_sds((r, 128), F32),
                          compiler_params=_cp(40))(recv)


def _pack(arrs):
    flat = jnp.concatenate([a.reshape(-1) for a in arrs])
    n = flat.shape[0]
    rows = -(-n // 1024) * 8
    return jnp.pad(flat, (0, rows * 128 - n)).reshape(rows, 128)


def _unpack(vec, shapes):
    flat = vec.reshape(-1)
    out, o = [], 0
    for sh in shapes:
        n = int(np.prod(sh))
        out.append(flat[o:o + n].reshape(sh))
        o += n
    return out


WEIGHTS = ['rel_bias', 'ada_w', 'ada_b', 'ln_g', 'ln_b', 'ffn_w_gate', 'ffn_w_up', 'ffn_w_down', 'w_in', 'w_out',
           'ssm_a_re', 'ssm_a_im', 'ssm_log_dt', 'ssm_b_re', 'ssm_b_im', 'ssm_c_re', 'ssm_c_im', 'ssm_d', 'glu_w',
           'glu_b', 'pool_w', 'pool_scale']
SMALL = ['rel_bias', 'ada_b', 'ln_g', 'ln_b', 'ssm_a_re', 'ssm_a_im', 'ssm_log_dt', 'ssm_b_re', 'ssm_b_im',
         'ssm_c_re', 'ssm_c_im', 'ssm_d', 'glu_b', 'pool_w', 'pool_scale']
SMALL_FULL_SHAPES = {'rel_bias': (32, 8), 'ada_b': (2, 9216), 'ln_g': (2, 3, 1024), 'ln_b': (2, 3, 1024),
                     'ssm_a_re': (2, 16, 64), 'ssm_a_im': (2, 16, 64), 'ssm_log_dt': (2, 16),
                     'ssm_b_re': (2, 16, 64, 16), 'ssm_b_im': (2, 16, 64, 16), 'ssm_c_re': (2, 16, 16, 64),
                     'ssm_c_im': (2, 16, 16, 64), 'ssm_d': (2, 256), 'glu_b': (2, 256), 'pool_w': (2, 4, 64, 64),
                     'pool_scale': (2, 256)}


def _step(P):
    me = _me()
    x0 = P['x'][0]
    target = P['loss_target'][0]

    def shards(l, sub):
        bf = lambda a: a.astype(BF16)
        if sub == 1:
            return [bf(P['w_in'][l]), bf(P['w_out'][l]), bf(P['glu_w'][l])]
        f = sub // 2
        padr = lambda a: jnp.pad(bf(a), ((0, FBP - FB), (0, 0)))
        return [padr(P['ffn_w_gate'][l, f].T), padr(P['ffn_w_up'][l, f].T), padr(P['ffn_w_down'][l, f])]

    order = [(l, sub) for l in range(DEPTH) for sub in range(3)]
    nxt = dict(zip(order[:-1], order[1:]))
    W = {key: [None] * 3 for key in order}
    c_all, lng_all, lnb_all, *W[order[0]] = _exchange(Gather([P['c'], P['ln_g'], P['ln_b']] + shards(*order[0])), "gather_first")
    gather_queue = [(key, pos, a) for key in order[1:] for pos, a in enumerate(shards(*key))]

    def gather_ride(cap_us, must=None):
        units, used = [], 0.0
        while gather_queue:
            key, _, a = gather_queue[0]
            cost = a.size * a.dtype.itemsize * GATHER_US_PER_BYTE
            if key != must and used + cost / 2 > cap_us:
                break
            units.append(gather_queue.pop(0))
            used += cost
        return (Gather([a for _, _, a in units]) if units else None), units

    def gathered(units, outs):
        for (key, pos, _), o in zip(units, outs):
            W[key][pos] = o

    c_all = c_all.reshape(NDEV, D)
    ln_g = jnp.transpose(lng_all, (1, 2, 0, 3)).reshape(DEPTH, 3, D)
    ln_b = jnp.transpose(lnb_all, (1, 2, 0, 3)).reshape(DEPTH, 3, D)

    ada_b_cols = lax.dynamic_slice_in_dim(P['ada_b'], me * 1152, 1152, axis=1).reshape(DEPTH, 1, 1152)
    modc = ada_fwd(c_all, P['ada_w'], ada_b_cols)
    (mod_all,) = _exchange(Gather([modc]), "gather_mod")
    mod_me = lax.dynamic_index_in_dim(mod_all, me, axis=2, keepdims=False)
    mod = jnp.transpose(mod_me, (1, 0, 2)).reshape(DEPTH, 9, D)

    bias = att_bias(P['rel_bias'])
    ssm = []
    for l in range(DEPTH):
        prm = (P['ssm_a_re'][l], P['ssm_a_im'][l], P['ssm_log_dt'][l], P['ssm_b_re'][l], P['ssm_b_im'][l])
        (abr, abi, bbr, bbi), disc_vjp = jax.vjp(_ssm_discretise, *prm)
        ssm.append(dict(
            vjp=disc_vjp, a=jnp.stack([abr.reshape(8, 128), abi.reshape(8, 128)]),
            bre=_blockdiag(jnp.transpose(bbr, (0, 2, 1))).astype(MXU), bim=_blockdiag(jnp.transpose(bbi, (0, 2, 1))).astype(MXU),
            cre=_blockdiag(jnp.transpose(P['ssm_c_re'][l], (0, 2, 1))).astype(MXU),
            cim=_blockdiag(jnp.transpose(P['ssm_c_im'][l], (0, 2, 1))).astype(MXU),
            d=P['ssm_d'][l].reshape(1, 256), gb=P['glu_b'][l].reshape(1, 256),
            pw=_blockdiag(P['pool_w'][l]).astype(MXU), psc=P['pool_scale'][l].reshape(1, 256)))

    saved = []
    x = x0
    h = ln_mod_fwd(x, mod[0], 0, "ln_mod_fwd_l0s0")
    for l, sub in order:
        tag = f"l{l}s{sub}"
        after = (mod[nxt[(l, sub)][0]], nxt[(l, sub)][1]) if (l, sub) in nxt else None
        if sub != 1:
            wg, wu, wd = (t.reshape(NDEV * FBP, D) for t in W[(l, sub)])
            ride, units = gather_ride(60, nxt.get((l, sub)))
            (G, U, fo), got = ffn_fwd(h, wg, wu, wd, "ffn_fwd_" + tag, ride)
            gathered(units, got)
            saved.append(dict(x=x, h=h, G=G, U=U, f=fo))
            x, *hn = res_ln_fwd(x, fo, mod[l], sub, ln_g[l], ln_b[l], 0.5, "res_ln_fwd_" + tag, after)
        else:
            sp = ssm[l]
            win, wout, gluw = W[(l, sub)]
            ride, units = gather_ride(15)
            (z,), got = win_fwd(h, win, "win_fwd_" + tag, ride)
            gathered(units, got)
            ride, units = gather_ride(55)
            (ya, lse), got = att_fwd(z, bias, "att_fwd_" + tag, ride)
            gathered(units, got)
            ride, units = gather_ride(35)
            (ys, ypre, st), got = ssm_fwd(z, sp['a'], sp['bre'], sp['bim'], sp['cre'], sp['cim'], sp['d'], gluw, sp['gb'],
                                          "ssm_fwd_" + tag, ride)
            gathered(units, got)
            yp = pool_fwd(z, sp['pw'], sp['psc'], "pool_fwd_" + tag)
            ride, units = gather_ride(12, nxt.get((l, sub)))
            (o,), got = wout_fwd(ya, ys, yp, wout, "wout_fwd_" + tag, ride)
            gathered(units, got)
            saved.append(dict(x=x, h=h, z=z, ya=ya, lse=lse, ys=ys, ypre=ypre, st=st, yp=yp, f=o))
            x, *hn = res_ln_fwd(x, o, mod[l], sub, ln_g[l], ln_b[l], 1.0, "res_ln_fwd_" + tag, after)
        h = hn[0] if hn else None
    assert not gather_queue

    loss_tile, dx = loss_fwd_bwd(x, target, "loss")
    loss = lax.psum(loss_tile[0, 0], ("x", "y", "c"))

    flights = []

    def hosted(cap_us, fn, *args):
        return fn(*args, None)[0]

    dmod = [[None] * 9 for _ in range(DEPTH)]
    dlng = [[None] * 3 for _ in range(DEPTH)]
    dlnb = [[None] * 3 for _ in range(DEPTH)]
    dbiases = [None] * DEPTH
    small_l = [dict() for _ in range(DEPTH)]
    for l, sub in reversed(order):
        tag = f"l{l}s{sub}"
        sv = saved[3 * l + sub]
        if (l, sub) == order[-1]:
            dxa, df, sums = hosted(0, res_ln_bwd, sv['x'], sv['f'], mod[l], sub, ln_g[l], dx, 0.5, "res_ln_bwd_" + tag)
        dlng[l][sub], dlnb[l][sub], dmod[l][3 * sub + 2] = sums[0], sums[1], sums[2]
        if sub != 1:
            f = sub // 2
            wg, wu, wd = (t.reshape(NDEV * FBP, D) for t in W[(l, sub)])
            dwg, dwu, dwd, dh = hosted(105, ffn_bwd, df, sv['h'], sv['G'], sv['U'], wg, wu, wd, "ffn_bwd_" + tag)
            handle, zero = scatter_start([t.reshape(NDEV, FBP, D) for t in (dwg, dwu, dwd)], "scatter_start_" + tag)
            flights.append(((l, sub), handle))
        else:
            sp = ssm[l]
            win, wout, gluw = W[(l, sub)]
            dya, dys, dyp, dwout = hosted(0, wout_bwd, df, sv['ya'], sv['ys'], sv['yp'], wout, "wout_bwd_" + tag)
            dq, dk, dv, dbiases[l] = hosted(115, att_bwd, sv['z'], bias, sv['ya'], sv['lse'], dya, "att_bwd_" + tag)
            dus, dbre, dbim, dcre, dcim, da, dd, dgw, dgb = hosted(
                52, ssm_bwd, dys, sv['z'], sv['ypre'], sv['st'], sp['a'], sp['bre'], sp['bim'], sp['cre'], sp['cim'], sp['d'],
                gluw, sp['gb'], "ssm_bwd_" + tag)
            dup, dpw, dpsc = pool_bwd(dyp, sv['z'], sp['pw'], sp['psc'], "pool_bwd_" + tag)
            dh, dwin = hosted(40, win_bwd, (dq, dk, dv, dus, dup), sv['h'], win, "win_bwd_" + tag)
            handle, zero = scatter_start([dwin, dwout, dgw.astype(BF16).reshape(NDEV, 32, 256)], "scatter_start_" + tag)
            flights.append(((l, sub), handle))
            d_are, d_aim, d_ldt, d_bre, d_bim = sp['vjp']((
                da[0].reshape(16, 64), da[1].reshape(16, 64),
                jnp.transpose(_blockdiag_take(dbre, 16, 64), (0, 2, 1)), jnp.transpose(_blockdiag_take(dbim, 16, 64), (0, 2, 1))))
            small_l[l] = dict(
                ssm_a_re=d_are, ssm_a_im=d_aim, ssm_log_dt=d_ldt, ssm_b_re=d_bre, ssm_b_im=d_bim,
                ssm_c_re=jnp.transpose(_blockdiag_take(dcre, 64, 16), (0, 2, 1)),
                ssm_c_im=jnp.transpose(_blockdiag_take(dcim, 64, 16), (0, 2, 1)),
                ssm_d=dd.reshape(256), glu_b=dgb.reshape(256), pool_w=_blockdiag_take(dpw, 64, 64), pool_scale=dpsc.reshape(256))
        if (l, sub) == order[0]:
            dx, sums2 = hosted(0, ln_mod_bwd, sv['x'], dh, mod[l] + zero, sub, dxa, "ln_mod_bwd_" + tag)
        else:
            lp, sp_ = order[order.index((l, sub)) - 1]
            svp = saved[3 * lp + sp_]
            dxa, df, sums, sums2 = ln_join_bwd(svp['x'], svp['f'], mod[lp], sp_, ln_g[lp], ln_b[lp], 1.0 if sp_ == 1 else 0.5,
                                               dh, mod[l] + zero, sub, dxa, "ln_join_bwd_" + tag)
        dmod[l][3 * sub], dmod[l][3 * sub + 1] = sums2[0], sums2[1]
    grad_x = dx[None]

    out = {}

    def put(name, g, d, m2, v2, shape):
        out['grad_' + name], out['delta_' + name] = g.reshape(shape), d.reshape(shape)
        out['new_m_' + name], out['new_v_' + name] = m2.reshape(shape), v2.reshape(shape)

    def wmv(name):
        return [P[pre + name] for pre in ('', 'm_', 'v_')]

    recv = {}
    for key, handle in flights[:-1]:
        recv[key] = scatter_wait(handle, dx, "scatter_wait_l%ds%d" % key)
    for pos, (name, tr) in enumerate((('w_in', 512), ('w_out', 128), ('glu_w', 32))):
        both = jnp.stack([recv[(l, 1)][pos] for l in range(DEPTH)], axis=1)
        put(name, *adam_rs(both, *wmv(name), tr, "adam_" + name)[0], P[name].shape)
    ffn = (('ffn_w_gate', [jnp.swapaxes(t, 2, 3) for t in wmv('ffn_w_gate')]),
           ('ffn_w_up', [jnp.swapaxes(t, 2, 3) for t in wmv('ffn_w_up')]), ('ffn_w_down', wmv('ffn_w_down')))
    part = [None] * 3
    for l, sub in [key for key, _ in flights[:-1] if key[1] != 1]:
        for pos, (name, ops) in enumerate(ffn):
            part[pos] = adam_block(recv[(l, sub)][pos], *ops, 2 * l + sub // 2, part[pos], f"adam_{name}_l{l}s{sub}")
    (l, sub), handle = flights[-1]
    last = scatter_wait(handle, part[2][0], "scatter_wait_l%ds%d" % (l, sub))
    for pos, (name, ops) in enumerate(ffn):
        res = adam_block(last[pos], *ops, 2 * l + sub // 2, part[pos], f"adam_{name}_l{l}s{sub}")
        put(name, *([jnp.swapaxes(t, 2, 3) for t in res] if pos < 2 else res), P[name].shape)

    small = {k: jnp.stack([small_l[l][k] for l in range(DEPTH)]) for k in small_l[0]}
    small['rel_bias'] = relbias_grad(dbiases)
    small['ada_b'] = jnp.stack([jnp.stack(dmod[l]).reshape(9 * D) for l in range(DEPTH)])
    small['ln_g'] = jnp.stack([jnp.stack(dlng[l]) for l in range(DEPTH)])
    small['ln_b'] = jnp.stack([jnp.stack(dlnb[l]) for l in range(DEPTH)])
    (small_all,) = _exchange(Gather([_pack([small[k] for k in SMALL])]), "gather_small")
    gsum = dict(zip(SMALL, _unpack(sum_sources(small_all, "sum_small"), [SMALL_FULL_SHAPES[k] for k in SMALL])))
    off = 256
    dmod_all = small_all.reshape(NDEV, -1)[:, off:off + DEPTH * 9 * D].reshape(NDEV, DEPTH, 9 * D)
    dmod_cols = jnp.transpose(lax.dynamic_slice_in_dim(dmod_all, me * 1152, 1152, axis=2), (1, 0, 2))
    g_ada_w = ada_bwd(c_all, dmod_cols)

    put('ada_w', g_ada_w, *adam_plain(g_ada_w, *wmv('ada_w'), 256, "adam_ada_w")[0], P['ada_w'].shape)

    for k in ('ln_g', 'ln_b'):
        gsum[k] = lax.dynamic_slice_in_dim(gsum[k], me * 128, 128, axis=2)
    swaps = {'rel_bias': (0, 1), 'ln_g': (0, 1), 'ln_b': (0, 1), 'ssm_b_re': (2, 3), 'ssm_b_im': (2, 3)}
    view = lambda k, t: jnp.swapaxes(t, *swaps[k]) if k in swaps else t
    ds_, m2s, v2s = adam_native(*[[view(k, src(k)) for k in SMALL] for src in
                                  (lambda k: gsum[k], lambda k: P[k], lambda k: P['m_' + k], lambda k: P['v_' + k])],
                                "adam_small")
    for k, d, m2, v2 in zip(SMALL, ds_, m2s, v2s):
        put(k, gsum[k], view(k, d), view(k, m2), view(k, v2), P[k].shape)

    res = [loss, grad_x]
    for pre in ('grad_', 'delta_', 'new_m_', 'new_v_'):
        res += [out[pre + k] for k in WEIGHTS]
    return tuple(res)


def kernel(x, c, rel_bias, ada_w, ada_b, ln_g, ln_b, ffn_w_gate, ffn_w_up, ffn_w_down, w_in, w_out, ssm_a_re, ssm_a_im, ssm_log_dt, ssm_b_re, ssm_b_im, ssm_c_re, ssm_c_im, ssm_d, glu_w, glu_b, pool_w, pool_scale, loss_target, m_rel_bias, m_ada_w, m_ada_b, m_ln_g, m_ln_b, m_ffn_w_gate, m_ffn_w_up, m_ffn_w_down, m_w_in, m_w_out, m_ssm_a_re, m_ssm_a_im, m_ssm_log_dt, m_ssm_b_re, m_ssm_b_im, m_ssm_c_re, m_ssm_c_im, m_ssm_d, m_glu_w, m_glu_b, m_pool_w, m_pool_scale, v_rel_bias, v_ada_w, v_ada_b, v_ln_g, v_ln_b, v_ffn_w_gate, v_ffn_w_up, v_ffn_w_down, v_w_in, v_w_out, v_ssm_a_re, v_ssm_a_im, v_ssm_log_dt, v_ssm_b_re, v_ssm_b_im, v_ssm_c_re, v_ssm_c_im, v_ssm_d, v_glu_w, v_glu_b, v_pool_w, v_pool_scale):
    return _step(dict(locals()))
```

```python
import functools
import math

import numpy as np
import jax
import jax.numpy as jnp
from jax import lax
from jax.experimental import pallas as pl
from jax.experimental.pallas import tpu as pltpu

F32 = jnp.float32
BF16 = jnp.bfloat16
MXU = jnp.bfloat16

S = 2048
D = 1024
NDEV = 8
DEPTH = 2
D_ATT, D_SSM, D_POOL, D_IN = 512, 256, 256, 2048
N_HEADS = 8
FB = 352
FBP = 384
QB = 128
PATTERNS = ((128, 1), (512, 4), (2048, 16))
POOL_WINDOWS = (2, 4, 8, 16)
N_BUCKETS, MAX_DISTANCE = 32, 2048
ALPHA = (2 * DEPTH) ** 0.25
LN_EPS = 1e-5
NEG = -1e30
GATHER_US_PER_BYTE = 43e-6
SCATTER_US_PER_BYTE = 21.6e-6
LR, B1, B2, EPS, WD, STEP = 0.001, 0.9, 0.999, 1e-08, 0.01, 10

TM = 256
TMM = 512
MIB = 1024 * 1024


def _cp(vmem_mib, sem=None):
    kw = dict(vmem_limit_bytes=vmem_mib * MIB)
    if sem is not None:
        kw["dimension_semantics"] = sem
    return pltpu.CompilerParams(**kw)


def _sds(shape, dtype):
    return jax.ShapeDtypeStruct(shape, dtype)


def _mm(a, b):
    return jnp.dot(a.astype(MXU), b.astype(MXU), preferred_element_type=F32)


def _mm_nt(a, b):
    return lax.dot_general(a.astype(MXU), b.astype(MXU), (((1,), (1,)), ((), ())), preferred_element_type=F32)


def _mm_tn(a, b):
    return lax.dot_general(a.astype(MXU), b.astype(MXU), (((0,), (0,)), ((), ())), preferred_element_type=F32)


def _ln_stats(x):
    mu = jnp.mean(x, axis=-1, keepdims=True)
    xc = x - mu
    var = jnp.mean(xc * xc, axis=-1, keepdims=True)
    rstd = lax.rsqrt(var + LN_EPS)
    return xc * rstd, rstd


def _ln_bwd(dn, n, rstd):
    return rstd * (dn - jnp.mean(dn, axis=-1, keepdims=True) - n * jnp.mean(dn * n, axis=-1, keepdims=True))


def _me():
    return 4 * lax.axis_index("x") + 2 * lax.axis_index("y") + lax.axis_index("c")


ANY = pl.BlockSpec(memory_space=pl.ANY)
PIN_BYTES = 1 << 19


def _pallas_call(*a, **k):
    big = lambda o: math.prod(o.shape) * o.dtype.itemsize >= PIN_BYTES
    pin = lambda o: pltpu.HBM(o.shape, o.dtype) if isinstance(o, jax.ShapeDtypeStruct) and big(o) else o
    osh = k["out_shape"]
    k["out_shape"] = tuple(pin(o) for o in osh) if isinstance(osh, (tuple, list)) else pin(osh)
    fn = pl.pallas_call(*a, **k)

    def run(*args):
        return fn(*[pltpu.with_memory_space_constraint(x, pltpu.HBM) if big(x) else x for x in args])
    return run


class Gather:
    def __init__(self, srcs):
        self.srcs = list(srcs)
        self.n = len(self.srcs)
        self.bufs = []
        self.out_shapes = [_sds((NDEV,) + a.shape, a.dtype) for a in self.srcs]
        self.sems = [pltpu.SemaphoreType.DMA((7 * self.n,)), pltpu.SemaphoreType.DMA((7 * self.n,)),
                     pltpu.SemaphoreType.DMA((self.n,))]

    def _parts(self, srcs, outs, sems):
        send_sems, recv_sems, loc_sems = sems
        x, y, c = lax.axis_index("x"), lax.axis_index("y"), lax.axis_index("c")
        me, sib = (x, y, c), (x, y, 1 - c)
        chips = [(1 - x, y), (x, 1 - y), (1 - x, 1 - y)]
        slot = lambda d: 4 * d[0] + 2 * d[1] + d[2]

        def copy(a, k, block, to, src=None):
            dst = outs[a].at[slot(block)]
            return pltpu.make_async_remote_copy(
                src_ref=dst if src is None else src, dst_ref=dst,
                send_sem=send_sems.at[7 * a + k], recv_sem=recv_sems.at[7 * a + k],
                device_id=to, device_id_type=pl.DeviceIdType.MESH)

        local = [pltpu.make_async_copy(srcs[a], outs[a].at[slot(me)], loc_sems.at[a]) for a in range(self.n)]
        return me, sib, chips, c, copy, local

    def start(self, srcs, bufs, outs, sems):
        me, sib, chips, c, copy, local = self._parts(srcs, outs, sems)
        for a in range(self.n):
            local[a].start()
            copy(a, 0, me, sib, src=srcs[a]).start()
            for j, chip in enumerate(chips):
                copy(a, 1 + j, me, (*chip, c), src=srcs[a]).start()

    def finish(self, srcs, bufs, outs, sems):
        me, sib, chips, c, copy, local = self._parts(srcs, outs, sems)
        for a in range(self.n):
            for j, chip in enumerate(chips):
                copy(a, 1 + j, (*chip, c), me).wait_recv()
                copy(a, 4 + j, (*chip, c), sib).start()
        for a in range(self.n):
            copy(a, 0, sib, me).wait_recv()
            copy(a, 0, me, sib, src=srcs[a]).wait_send()
            for j, chip in enumerate(chips):
                copy(a, 4 + j, (*chip, 1 - c), me).wait_recv()
                copy(a, 1 + j, me, (*chip, c), src=srcs[a]).wait_send()
                copy(a, 4 + j, (*chip, c), sib).wait_send()
            local[a].wait()


class Scatter:
    def __init__(self, items, bufs):
        self.items = list(items)
        self.keys = list(dict.fromkeys(key for _, key, _, _ in self.items))
        self.srcs = [src for src, _, _, _ in self.items]
        self.bufs = [bufs[key] for key in self.keys]
        self.n = len(self.srcs)
        self.out_shapes = [_sds(b.shape, b.dtype) for b in self.bufs]
        pairs = [(a, k) for a, (_, _, _, ks) in enumerate(self.items) for k in ks]
        self.remote_pairs = [p for p in pairs if p[1] != 0]
        self.local_pairs = [p for p in pairs if p[1] == 0]
        self.sems = [pltpu.SemaphoreType.DMA((max(len(self.remote_pairs), 1),)),
                     pltpu.SemaphoreType.DMA((max(len(self.remote_pairs), 1),)),
                     pltpu.SemaphoreType.DMA((max(len(self.local_pairs), 1),))]

    def _copies(self, srcs, outs, sems):
        send_sems, recv_sems, loc_sems = sems
        me = _me()

        def dst(a, slot):
            _, key, index, _ = self.items[a]
            return outs[self.keys.index(key)].at[(slot,) + tuple(index)]

        def remote(n, slot):
            a, k = self.remote_pairs[n]
            t = me ^ k
            return pltpu.make_async_remote_copy(
                src_ref=srcs[a].at[t], dst_ref=dst(a, slot), send_sem=send_sems.at[n], recv_sem=recv_sems.at[n],
                device_id=(t // 4, (t // 2) % 2, t % 2), device_id_type=pl.DeviceIdType.MESH)

        local = [pltpu.make_async_copy(srcs[a].at[me], dst(a, me), loc_sems.at[n])
                 for n, (a, _) in enumerate(self.local_pairs)]
        return me, remote, local

    def start(self, srcs, bufs, outs, sems):
        me, remote, local = self._copies(srcs, outs, sems)
        for cp in local:
            cp.start()
        for n in range(len(self.remote_pairs)):
            remote(n, me).start()

    def finish(self, srcs, bufs, outs, sems):
        me, remote, local = self._copies(srcs, outs, sems)
        for n, (_, k) in enumerate(self.remote_pairs):
            remote(n, me ^ k).wait()
        for cp in local:
            cp.wait()


def _call(body, *, name, grid, in_specs, out_specs, out_shape, args, scratch=(), cp=None, ride=None):
    out_specs, out_shape, scratch = list(out_specs), list(out_shape), list(scratch)
    if ride is None:
        outs = _pallas_call(body, name=name, grid=grid, in_specs=list(in_specs), out_specs=tuple(out_specs),
                              out_shape=tuple(out_shape), scratch_shapes=scratch, compiler_params=cp)(*args)
        return list(outs), []
    nin, nout, nscr, n, nb, no = len(in_specs), len(out_specs), len(scratch), ride.n, len(ride.bufs), len(ride.out_shapes)
    steps = list(grid)

    def wrapped(*refs):
        h_in, r_src, r_buf = refs[:nin], refs[nin:nin + n], refs[nin + n:nin + n + nb]
        o0 = nin + n + nb
        h_out, r_out = refs[o0:o0 + nout], refs[o0 + nout:o0 + nout + no]
        s0 = o0 + nout + no
        h_scr, sems = refs[s0:s0 + nscr], refs[s0 + nscr:]
        ids = [pl.program_id(a) for a in range(len(steps))]
        first = functools.reduce(jnp.logical_and, [i == 0 for i in ids])
        last = functools.reduce(jnp.logical_and, [i == s - 1 for i, s in zip(ids, steps)])

        @pl.when(first)
        def _():
            ride.start(r_src, r_buf, r_out, sems)

        body(*h_in, *h_out, *h_scr)

        @pl.when(last)
        def _():
            ride.finish(r_src, r_buf, r_out, sems)

    aliases = {nin + n + k: nout + k for k in range(nb)}
    outs = _pallas_call(
        wrapped, name=name, grid=grid, in_specs=list(in_specs) + [ANY] * (n + nb),
        out_specs=tuple(out_specs + [ANY] * no), out_shape=tuple(out_shape + ride.out_shapes),
        scratch_shapes=scratch + ride.sems, input_output_aliases=aliases, compiler_params=cp,
    )(*args, *ride.srcs, *ride.bufs)
    return list(outs[:nout]), list(outs[nout:])


def _exchange(ride, name):
    def body(dummy_ref, o_ref):
        o_ref[...] = dummy_ref[...]

    one = pl.BlockSpec((8, 128), lambda i: (0, 0))
    _, outs = _call(body, name=name, grid=(1,), in_specs=[one], out_specs=[one], out_shape=[_sds((8, 128), F32)],
                    args=(jnp.zeros((8, 128), F32),), ride=ride)
    return outs


HBM = pl.BlockSpec(memory_space=pltpu.HBM)
SEM = pl.BlockSpec(memory_space=pltpu.SEMAPHORE)


def _scatter_copies(srcs, lands, sems):
    send_sems, recv_sems, loc_sems = sems
    me = _me()

    def remote(a, k, slot):
        t = me ^ k
        return pltpu.make_async_remote_copy(
            src_ref=srcs[a].at[t], dst_ref=lands[a].at[slot], send_sem=send_sems.at[7 * a + k - 1],
            recv_sem=recv_sems.at[7 * a + k - 1], device_id=(t // 4, (t // 2) % 2, t % 2), device_id_type=pl.DeviceIdType.MESH)

    local = [pltpu.make_async_copy(srcs[a].at[me], lands[a].at[me], loc_sems.at[a]) for a in range(len(srcs))]
    return me, remote, local


def scatter_start(payloads, name, after=()):
    n, na = len(payloads), len(after)

    def body(*refs):
        srcs, lands, sems = refs[:n], refs[n:2 * n], refs[2 * n + na:2 * n + na + 3]
        me, remote, local = _scatter_copies(srcs, lands, sems)
        for a in range(n):
            local[a].start()
            for k in range(1, NDEV):
                remote(a, k, me).start()
        refs[-1][...] = jnp.zeros((8, 128), F32)

    thru = [pltpu.HBM(p.shape, p.dtype) for p in payloads]
    outs = pl.pallas_call(
        body, name=name,
        out_shape=(pltpu.SemaphoreType.DMA((7 * n,)), pltpu.SemaphoreType.DMA((7 * n,)), pltpu.SemaphoreType.DMA((n,)),
                   *thru, *thru, _sds((8, 128), F32)),
        in_specs=[HBM] * (2 * n + na),
        out_specs=(SEM, SEM, SEM, *[HBM] * (2 * n), pl.BlockSpec(memory_space=pltpu.VMEM)),
        input_output_aliases={i: 3 + i for i in range(2 * n)},
        compiler_params=pltpu.CompilerParams(has_side_effects=pltpu.SideEffectType.DATAFLOW_SIDE_EFFECTING),
    )(*[pltpu.with_memory_space_constraint(p, pltpu.HBM) for p in payloads],
      *[pltpu.with_memory_space_constraint(lax.empty(p.shape, p.dtype), pltpu.HBM) for p in payloads],
      *[pltpu.with_memory_space_constraint(a, pltpu.HBM) for a in after])
    return (outs[:3], outs[3:3 + n], outs[3 + n:3 + 2 * n]), outs[-1][0, 0]


def scatter_wait(handle, after, name):
    sems, srcs_thru, lands_thru = handle
    n = len(srcs_thru)

    def body(*refs):
        srcs, lands, sems_ = refs[:n], refs[n:2 * n], refs[2 * n:2 * n + 3]
        me, remote, local = _scatter_copies(srcs, lands, sems_)
        for a in range(n):
            for k in range(1, NDEV):
                cp = remote(a, k, me ^ k)
                cp.wait_send()
                cp.wait_recv()
            local[a].wait()

    outs = pl.pallas_call(
        body, name=name, out_shape=tuple(pltpu.HBM(p.shape, p.dtype) for p in (*srcs_thru, *lands_thru)),
        in_specs=[HBM] * (2 * n) + [SEM] * 3 + [HBM], out_specs=tuple([HBM] * (2 * n)),
        input_output_aliases={i: i for i in range(2 * n)},
        compiler_params=pltpu.CompilerParams(has_side_effects=pltpu.SideEffectType.DATAFLOW_SIDE_EFFECTING),
    )(*srcs_thru, *lands_thru, *sems, pltpu.with_memory_space_constraint(after, pltpu.HBM))
    return list(outs[n:])


def _row_spec(cols, tm=TM):
    return pl.BlockSpec((tm, cols), lambda i: (i, 0))


def _full_spec(shape):
    nd = len(shape)
    return pl.BlockSpec(shape, lambda i: (0,) * nd)


def ln_mod_fwd(x, mod, sub, name):
    def body(x_ref, mod_ref, h_ref):
        n, _ = _ln_stats(x_ref[...])
        shift = mod_ref[3 * sub:3 * sub + 1, :]
        scale = mod_ref[3 * sub + 1:3 * sub + 2, :]
        h_ref[...] = (n * (1.0 + scale) + shift).astype(MXU)

    return _pallas_call(
        body, name=name, grid=(S // TM,),
        in_specs=[_row_spec(D), _full_spec((9, D))], out_specs=_row_spec(D),
        out_shape=_sds((S, D), MXU), compiler_params=_cp(32, ("arbitrary",)))(x, mod)


def res_ln_fwd(x, f, mod, sub, lng, lnb, w, name, nxt=None):
    def body(x_ref, f_ref, mod_ref, g_ref, b_ref, *rest):
        gate = mod_ref[3 * sub + 2:3 * sub + 3, :]
        r = ALPHA * x_ref[...] + (w * gate) * f_ref[...]
        n, _ = _ln_stats(r)
        xo = n * g_ref[sub:sub + 1, :] + b_ref[sub:sub + 1, :]
        rest[-1 if nxt is None else -2][...] = xo
        if nxt is not None:
            nmod_ref, h_ref = rest[0], rest[-1]
            n2, _ = _ln_stats(xo)
            s2 = nxt[1]
            h_ref[...] = (n2 * (1.0 + nmod_ref[3 * s2 + 1:3 * s2 + 2, :]) + nmod_ref[3 * s2:3 * s2 + 1, :]).astype(MXU)

    more = nxt is not None
    return _pallas_call(
        body, name=name, grid=(S // TM,),
        in_specs=[_row_spec(D), _row_spec(D), _full_spec((9, D)), _full_spec((3, D)), _full_spec((3, D))] + [_full_spec((9, D))] * more,
        out_specs=(_row_spec(D),) + (_row_spec(D),) * more, out_shape=(_sds((S, D), F32),) + (_sds((S, D), MXU),) * more,
        compiler_params=_cp(32, ("arbitrary",)))(x, f, mod, lng, lnb, *([nxt[0]] if more else []))


def res_ln_bwd(x, f, mod, sub, lng, dxo, w, name, ride=None):
    def body(x_ref, f_ref, mod_ref, g_ref, dxo_ref, dxa_ref, df_ref, sums_ref):
        i = pl.program_id(0)
        gate = mod_ref[3 * sub + 2:3 * sub + 3, :]
        fv = f_ref[...]
        r = ALPHA * x_ref[...] + (w * gate) * fv
        n, rstd = _ln_stats(r)
        dxo = dxo_ref[...]
        dr = _ln_bwd(dxo * g_ref[sub:sub + 1, :], n, rstd)
        dxa_ref[...] = ALPHA * dr
        df_ref[...] = ((w * gate) * dr).astype(MXU)
        part = jnp.concatenate([
            jnp.sum(dxo * n, axis=0, keepdims=True),
            jnp.sum(dxo, axis=0, keepdims=True),
            jnp.sum(dr * fv, axis=0, keepdims=True) * w,
            jnp.zeros((5, D), F32)], axis=0)

        @pl.when(i == 0)
        def _():
            sums_ref[...] = part

        @pl.when(i > 0)
        def _():
            sums_ref[...] += part

    return _call(
        body, name=name, grid=(S // TM,),
        in_specs=[_row_spec(D), _row_spec(D), _full_spec((9, D)), _full_spec((3, D)), _row_spec(D)],
        out_specs=(_row_spec(D), _row_spec(D), _full_spec((8, D))),
        out_shape=(_sds((S, D), F32), _sds((S, D), MXU), _sds((8, D), F32)),
        cp=_cp(32, ("arbitrary",)), args=(x, f, mod, lng, dxo), ride=ride)


def ln_mod_bwd(x, dh, mod, sub, dxa, name, ride=None):
    def body(x_ref, dh_ref, mod_ref, dxa_ref, dx_ref, sums_ref):
        i = pl.program_id(0)
        scale = mod_ref[3 * sub + 1:3 * sub + 2, :]
        n, rstd = _ln_stats(x_ref[...])
        dh = dh_ref[...]
        dx_ref[...] = dxa_ref[...] + _ln_bwd(dh * (1.0 + scale), n, rstd)
        part = jnp.concatenate([
            jnp.sum(dh, axis=0, keepdims=True),
            jnp.sum(dh * n, axis=0, keepdims=True),
            jnp.zeros((6, D), F32)], axis=0)

        @pl.when(i == 0)
        def _():
            sums_ref[...] = part

        @pl.when(i > 0)
        def _():
            sums_ref[...] += part

    return _call(
        body, name=name, grid=(S // TM,),
        in_specs=[_row_spec(D), _row_spec(D), _full_spec((9, D)), _row_spec(D)],
        out_specs=(_row_spec(D), _full_spec((8, D))),
        out_shape=(_sds((S, D), F32), _sds((8, D), F32)),
        cp=_cp(32, ("arbitrary",)), args=(x, dh, mod, dxa), ride=ride)


def ln_join_bwd(xp, fp, modp, subp, lngp, lnbp, wp, dh, mod, sub, dxa, name):
    def body(xp_ref, fp_ref, modp_ref, g_ref, b_ref, dh_ref, mod_ref, dxa_ref, dxap_ref, dfp_ref, sumsp_ref, sums_ref):
        i = pl.program_id(0)
        gate = modp_ref[3 * subp + 2:3 * subp + 3, :]
        fv = fp_ref[...]
        n, rstd = _ln_stats(ALPHA * xp_ref[...] + (wp * gate) * fv)
        gain = g_ref[subp:subp + 1, :]
        n2, rstd2 = _ln_stats(n * gain + b_ref[subp:subp + 1, :])
        dh = dh_ref[...]
        dx = dxa_ref[...] + _ln_bwd(dh * (1.0 + mod_ref[3 * sub + 1:3 * sub + 2, :]), n2, rstd2)
        dr = _ln_bwd(dx * gain, n, rstd)
        dxap_ref[...] = ALPHA * dr
        dfp_ref[...] = ((wp * gate) * dr).astype(MXU)
        partp = jnp.concatenate([
            jnp.sum(dx * n, axis=0, keepdims=True), jnp.sum(dx, axis=0, keepdims=True),
            jnp.sum(dr * fv, axis=0, keepdims=True) * wp, jnp.zeros((5, D), F32)], axis=0)
        part = jnp.concatenate([
            jnp.sum(dh, axis=0, keepdims=True), jnp.sum(dh * n2, axis=0, keepdims=True), jnp.zeros((6, D), F32)], axis=0)

        @pl.when(i == 0)
        def _():
            sumsp_ref[...] = partp
            sums_ref[...] = part

        @pl.when(i > 0)
        def _():
            sumsp_ref[...] += partp
            sums_ref[...] += part

    return _pallas_call(
        body, name=name, grid=(S // TM,),
        in_specs=[_row_spec(D), _row_spec(D), _full_spec((9, D)), _full_spec((3, D)), _full_spec((3, D)), _row_spec(D),
                  _full_spec((9, D)), _row_spec(D)],
        out_specs=(_row_spec(D), _row_spec(D), _full_spec((8, D)), _full_spec((8, D))),
        out_shape=(_sds((S, D), F32), _sds((S, D), MXU), _sds((8, D), F32), _sds((8, D), F32)),
        compiler_params=_cp(40, ("arbitrary",)))(xp, fp, modp, lngp, lnbp, dh, mod, dxa)


def loss_fwd_bwd(y, target, name):
    def body(y_ref, t_ref, l_ref, dy_ref):
        i = pl.program_id(0)
        e = y_ref[...] - t_ref[...]
        dy_ref[...] = e * (1.0 / D)
        part = jnp.zeros((8, 128), F32) + (0.5 / D) * jnp.sum(e * e)

        @pl.when(i == 0)
        def _():
            l_ref[...] = part

        @pl.when(i > 0)
        def _():
            l_ref[...] += part

    return _pallas_call(
        body, name=name, grid=(S // TM,),
        in_specs=[_row_spec(D), _row_spec(D)], out_specs=(_full_spec((8, 128)), _row_spec(D)),
        out_shape=(_sds((8, 128), F32), _sds((S, D), F32)),
        compiler_params=_cp(32, ("arbitrary",)))(y, target)


HB = 2 * FBP
NHB = NDEV * FBP // HB
TMB = 1024


def _wrows(buffers=2):
    return pl.BlockSpec((HB, D), lambda j, i: (j, 0), pipeline_mode=pl.Buffered(buffers))


def _resident(shape):
    return pl.BlockSpec(shape, lambda j, i: (0, 0), pipeline_mode=pl.Buffered(1))


def ffn_fwd(h, wgt, wut, wd, name, ride=None):
    def body(h_ref, wg_ref, wu_ref, wd_ref, g_ref, u_ref, f_ref):
        j, i = pl.program_id(0), pl.program_id(1)
        hv = h_ref[...]
        g = _mm_nt(hv, wg_ref[...])
        u = _mm_nt(hv, wu_ref[...])
        g_ref[...] = g.astype(MXU)
        u_ref[...] = u.astype(MXU)
        a = g * jax.nn.sigmoid(g) * u
        part = _mm(a, wd_ref[...])
        rows = pl.ds(pl.multiple_of(i * TMB, TMB), TMB)

        @pl.when(j == 0)
        def _():
            f_ref[rows, :] = part

        @pl.when(j > 0)
        def _():
            f_ref[rows, :] += part

    gu = pl.BlockSpec((TMB, HB), lambda j, i: (i, j))
    return _call(
        body, name=name, grid=(NHB, S // TMB),
        in_specs=[pl.BlockSpec((TMB, D), lambda j, i: (i, 0)), _wrows(), _wrows(), _wrows()],
        out_specs=(gu, gu, _resident((S, D))),
        out_shape=(_sds((S, NDEV * FBP), MXU), _sds((S, NDEV * FBP), MXU), _sds((S, D), F32)),
        cp=_cp(52, ("arbitrary", "arbitrary")), args=(h, wgt, wut, wd), ride=ride)


def ffn_bwd(df, h, g, u, wgt, wut, wd, name, ride=None):
    ni = S // TMB

    def body(df_ref, h_ref, g_ref, u_ref, wg_ref, wu_ref, wd_ref, dwg_ref, dwu_ref, dwd_ref, dh_ref,
             ag_ref, au_ref, ad_ref):
        j, i = pl.program_id(0), pl.program_id(1)
        dfv, hv = df_ref[...], h_ref[...]
        gv, uv = g_ref[...].astype(F32), u_ref[...].astype(F32)
        da = _mm_nt(dfv, wd_ref[...])
        sg = jax.nn.sigmoid(gv)
        silu = gv * sg
        du = da * silu
        dg = da * uv * (sg * (1.0 + gv * (1.0 - sg)))
        p_d = _mm_tn(silu * uv, dfv)
        p_g = _mm_tn(dg, hv)
        p_u = _mm_tn(du, hv)

        @pl.when(i == 0)
        def _():
            ad_ref[...] = p_d
            ag_ref[...] = p_g
            au_ref[...] = p_u

        @pl.when(i > 0)
        def _():
            ad_ref[...] += p_d
            ag_ref[...] += p_g
            au_ref[...] += p_u

        @pl.when(i == ni - 1)
        def _():
            dwd_ref[...] = ad_ref[...].astype(BF16)
            dwg_ref[...] = ag_ref[...].astype(BF16)
            dwu_ref[...] = au_ref[...].astype(BF16)

        part = _mm(dg, wg_ref[...]) + _mm(du, wu_ref[...])
        rows = pl.ds(pl.multiple_of(i * TMB, TMB), TMB)

        @pl.when(j == 0)
        def _():
            dh_ref[rows, :] = part

        @pl.when(j > 0)
        def _():
            dh_ref[rows, :] += part

    gu = pl.BlockSpec((TMB, HB), lambda j, i: (i, j))
    rowt = pl.BlockSpec((TMB, D), lambda j, i: (i, 0))
    return _call(
        body, name=name, grid=(NHB, ni),
        in_specs=[rowt, rowt, gu, gu, _wrows(1), _wrows(1), _wrows(1)],
        out_specs=(_wrows(1), _wrows(1), _wrows(1), _resident((S, D))),
        out_shape=(_sds((NDEV * FBP, D), BF16), _sds((NDEV * FBP, D), BF16), _sds((NDEV * FBP, D), BF16), _sds((S, D), F32)),
        scratch=[pltpu.VMEM((HB, D), F32), pltpu.VMEM((HB, D), F32), pltpu.VMEM((HB, D), F32)],
        cp=_cp(60, ("arbitrary", "arbitrary")), args=(df, h, g, u, wgt, wut, wd), ride=ride)


def win_fwd(h, win, name, ride=None):
    def body(h_ref, w_ref, z_ref):
        hv = h_ref[...]
        for j in range(NDEV):
            z_ref[:, 256 * j:256 * (j + 1)] = _mm(hv, w_ref[j])

    return _call(
        body, name=name, grid=(S // TMM,),
        in_specs=[_row_spec(D, TMM), _full_spec((NDEV, D, 256))],
        out_specs=[_row_spec(D_IN, TMM)], out_shape=[_sds((S, D_IN), F32)],
        cp=_cp(40, ("arbitrary",)), args=(h, win), ride=ride)


def win_bwd(dparts, h, win, name, ride=None):
    ni = S // TMM

    def body(dq_ref, dk_ref, dv_ref, dus_ref, dup_ref, h_ref, w_ref, dh_ref, dw_ref, acc_ref):
        i = pl.program_id(0)
        hv = h_ref[...]
        cols = [dq_ref[:, 0:256], dq_ref[:, 256:512], dk_ref[:, 0:256], dk_ref[:, 256:512],
                dv_ref[:, 0:256], dv_ref[:, 256:512], dus_ref[...], dup_ref[...]]
        dh = jnp.zeros((TMM, D), F32)
        for j in range(NDEV):
            dz = cols[j].astype(MXU)
            dh = dh + _mm_nt(dz, w_ref[j])
            p = _mm_tn(hv, dz)

            @pl.when(i == 0)
            def _():
                acc_ref[j] = p

            @pl.when(i > 0)
            def _():
                acc_ref[j] += p

        dh_ref[...] = dh

        @pl.when(i == ni - 1)
        def _():
            dw_ref[...] = acc_ref[...].astype(BF16)

    return _call(
        body, name=name, grid=(ni,),
        in_specs=[_row_spec(512, TMM), _row_spec(512, TMM), _row_spec(512, TMM), _row_spec(256, TMM), _row_spec(256, TMM),
                  _row_spec(D, TMM), _full_spec((NDEV, D, 256))],
        out_specs=(_row_spec(D, TMM), _full_spec((NDEV, D, 256))),
        out_shape=(_sds((S, D), F32), _sds((NDEV, D, 256), BF16)),
        scratch=[pltpu.VMEM((NDEV, D, 256), F32)],
        cp=_cp(48, ("arbitrary",)), args=(*dparts, h, win), ride=ride)


def wout_fwd(ya, ys, yp, wout, name, ride=None):
    def body(ya_ref, ys_ref, yp_ref, w_ref, o_ref):
        w = w_ref[...].reshape(D, D)
        o_ref[...] = _mm(ya_ref[...], w[0:512]) + _mm(ys_ref[...], w[512:768]) + _mm(yp_ref[...], w[768:1024])

    return _call(
        body, name=name, grid=(S // TMM,),
        in_specs=[_row_spec(512, TMM), _row_spec(256, TMM), _row_spec(256, TMM), _full_spec((NDEV, 128, D))],
        out_specs=[_row_spec(D, TMM)], out_shape=[_sds((S, D), F32)],
        cp=_cp(40, ("arbitrary",)), args=(ya, ys, yp, wout), ride=ride)


def wout_bwd(do, ya, ys, yp, wout, name, ride=None):
    ni = S // TMM

    def body(do_ref, ya_ref, ys_ref, yp_ref, w_ref, dya_ref, dys_ref, dyp_ref, dw_ref, acc_ref):
        i = pl.program_id(0)
        w = w_ref[...].reshape(D, D)
        dov = do_ref[...]
        dya_ref[...] = _mm_nt(dov, w[0:512])
        dys_ref[...] = _mm_nt(dov, w[512:768])
        dyp_ref[...] = _mm_nt(dov, w[768:1024])
        parts = [(0, 512, _mm_tn(ya_ref[...], dov)), (512, 768, _mm_tn(ys_ref[...], dov)),
                 (768, 1024, _mm_tn(yp_ref[...], dov))]
        for lo, hi, p in parts:
            @pl.when(i == 0)
            def _():
                acc_ref[lo:hi, :] = p

            @pl.when(i > 0)
            def _():
                acc_ref[lo:hi, :] += p

        @pl.when(i == ni - 1)
        def _():
            dw_ref[...] = acc_ref[...].astype(BF16).reshape(NDEV, 128, D)

    return _call(
        body, name=name, grid=(ni,),
        in_specs=[_row_spec(D, TMM), _row_spec(512, TMM), _row_spec(256, TMM), _row_spec(256, TMM),
                  _full_spec((NDEV, 128, D))],
        out_specs=(_row_spec(512, TMM), _row_spec(256, TMM), _row_spec(256, TMM), _full_spec((NDEV, 128, D))),
        out_shape=(_sds((S, 512), F32), _sds((S, 256), F32), _sds((S, 256), F32), _sds((NDEV, 128, D), BF16)),
        scratch=[pltpu.VMEM((D, D), F32)],
        cp=_cp(40, ("arbitrary",)), args=(do, ya, ys, yp, wout), ride=ride)


def _t5_bucket(dist):
    max_exact = N_BUCKETS // 2
    d = np.maximum(dist, 1).astype(np.float32)
    large = max_exact + (np.log(d / max_exact) / math.log(MAX_DISTANCE / max_exact)
                         * (N_BUCKETS - max_exact)).astype(np.int32)
    large = np.minimum(large, N_BUCKETS - 1)
    return np.where(dist < max_exact, dist, large).astype(np.int32)


def _att_static():
    i = np.arange(QB)[:, None]
    j = np.arange(2 * QB)[None, :]
    r = i + QB - j
    buckets, bands = [], []
    for window, dil in PATTERNS:
        bands.append((r >= 0) & (r <= window // dil))
        buckets.append(_t5_bucket(np.clip(r, 0, None) * dil))
    return np.stack(buckets), np.stack(bands), np.broadcast_to(j >= QB, (QB, 2 * QB))


def att_bias(rel_bias):
    m = np.arange(2 * QB)
    rows = []
    for window, dil in PATTERNS:
        r = QB - m
        ok = (r >= 0) & (r <= window // dil)
        b = rel_bias[_t5_bucket(np.clip(r, 0, None) * dil)]
        rows.append(jnp.where(ok[:, None], b, NEG).T)
    return jnp.broadcast_to(jnp.stack(rows)[:, :, None, :], (3, N_HEADS, 8, 2 * QB))


def _bias_tiles(t_ref, tiles):
    col = lax.broadcasted_iota(jnp.int32, (QB, 2 * QB), 1)
    for p in range(3):
        for hh in range(2):
            t = pltpu.roll(jnp.broadcast_to(t_ref[p, hh, 0:1, :], (QB, 2 * QB)), 0, 1, stride=1, stride_axis=0)
            tiles[p, hh, 0] = t
            tiles[p, hh, 1] = jnp.where(col >= QB, t, NEG)


def _permute_in(dst_ref, src_ref, d, scale=None, pad=QB):
    L = S // d
    for r in range(d):
        v = src_ref[pl.ds(r, L, stride=d), :] if d > 1 else src_ref[...]
        if scale is not None:
            v = v * scale
        dst_ref[pad + r * L:pad + (r + 1) * L, :] = v.astype(dst_ref.dtype)


def att_fwd(z, bias, name, ride=None):
    def body(q_ref, k_ref, v_ref, t_ref, y_ref, l_ref, qs, ks, vs, o_perm, l_perm, o_nat, l_nat, b_ref):
        _bias_tiles(t_ref, b_ref)
        zero_pad = jnp.zeros((QB, 128), MXU)
        ks[0:QB, :] = zero_pad
        vs[0:QB, :] = zero_pad
        lane = lax.broadcasted_iota(jnp.int32, (QB, 128), 1)
        for p, (_, d) in enumerate(PATTERNS):
            L = S // d
            nb = L // QB
            _permute_in(qs, q_ref, d, scale=0.125, pad=0)
            _permute_in(ks, k_ref, d)
            _permute_in(vs, v_ref, d)

            def blk(b, carry):
                r0 = pl.multiple_of(b * QB, QB)
                q = qs[pl.ds(r0, QB), :]
                kb = ks[pl.ds(r0, 2 * QB), :]
                vb = vs[pl.ds(r0, 2 * QB), :]
                first = ((b % nb) == 0).astype(jnp.int32)
                res = []
                for hh in range(2):
                    sel = (lane < 64) if hh == 0 else (lane >= 64)
                    qm = jnp.where(sel, q, jnp.zeros_like(q))
                    s = _mm_nt(qm, kb) + b_ref[p, hh, first]
                    m = jnp.max(s, axis=1, keepdims=True)
                    pe = jnp.exp(s - m)
                    den = jnp.sum(pe, axis=1, keepdims=True)
                    res.append((_mm(pe, vb) / den, m + jnp.log(den)))
                o_perm[pl.ds(r0, QB), :] = jnp.where(lane < 64, res[0][0], res[1][0])
                l_perm[pl.ds(r0, QB), :] = jnp.where(lane < 64, res[0][1], res[1][1])
                return carry

            lax.fori_loop(0, S // QB, blk, 0, unroll=8)
            for r in range(d):
                if d > 1:
                    o_nat[p, pl.ds(r, L, stride=d), :] = o_perm[r * L:(r + 1) * L, :]
                    l_nat[p, pl.ds(r, L, stride=d), :] = l_perm[r * L:(r + 1) * L, :]
                else:
                    o_nat[p] = o_perm[...]
                    l_nat[p] = l_perm[...]
        l0, l1, l2 = l_nat[0], l_nat[1], l_nat[2]
        m = jnp.maximum(jnp.maximum(l0, l1), l2)
        e0, e1, e2 = jnp.exp(l0 - m), jnp.exp(l1 - m), jnp.exp(l2 - m)
        den = e0 + e1 + e2
        y_ref[...] = (e0 * o_nat[0] + e1 * o_nat[1] + e2 * o_nat[2]) / den
        l_ref[...] = m + jnp.log(den)

    col = lambda c0: pl.BlockSpec((S, 128), lambda hp: (0, c0 + hp))
    return _call(
        body, name=name, grid=(N_HEADS // 2,),
        in_specs=[col(0), col(4), col(8), pl.BlockSpec((3, 2, 8, 2 * QB), lambda hp: (0, hp, 0, 0))],
        out_specs=(col(0), col(0)),
        out_shape=(_sds((S, D_ATT), F32), _sds((S, D_ATT), F32)),
        scratch=[pltpu.VMEM((S, 128), MXU), pltpu.VMEM((S + QB, 128), MXU), pltpu.VMEM((S + QB, 128), MXU),
                 pltpu.VMEM((S, 128), F32), pltpu.VMEM((S, 128), F32),
                 pltpu.VMEM((3, S, 128), F32), pltpu.VMEM((3, S, 128), F32),
                 pltpu.VMEM((3, 2, 2, QB, 2 * QB), F32)],
        cp=_cp(40, ("arbitrary",)), args=(z, z, z, bias), ride=ride)


def att_bwd(z, bias, y, lse, dy, name, ride=None):
    def body(q_ref, k_ref, v_ref, t_ref, y_ref, l_ref, dy_ref, dq_ref, dk_ref, dv_ref, db_ref,
             qs, ks, vs, dys, ls, dds, dn_nat, dq_perm, dk_perm, dv_perm, b_ref):
        _bias_tiles(t_ref, b_ref)
        zero_pad = jnp.zeros((QB, 128), MXU)
        ks[0:QB, :] = zero_pad
        vs[0:QB, :] = zero_pad
        lane = lax.broadcasted_iota(jnp.int32, (QB, 128), 1)
        lane_s = lax.broadcasted_iota(jnp.int32, (S, 128), 1)
        t = dy_ref[...] * y_ref[...]
        sa = jnp.sum(jnp.where(lane_s < 64, t, 0.0), axis=1, keepdims=True)
        sb = jnp.sum(jnp.where(lane_s >= 64, t, 0.0), axis=1, keepdims=True)
        dn_nat[...] = jnp.where(lane_s < 64, sa, sb)
        dq_ref[...] = jnp.zeros((S, 128), F32)
        dk_ref[...] = jnp.zeros((S, 128), F32)
        dv_ref[...] = jnp.zeros((S, 128), F32)
        db_ref[...] = jnp.zeros((3, 2, QB, 2 * QB), F32)
        for p, (_, d) in enumerate(PATTERNS):
            L = S // d
            nb = L // QB
            _permute_in(qs, q_ref, d, scale=0.125, pad=0)
            _permute_in(ks, k_ref, d)
            _permute_in(vs, v_ref, d)
            _permute_in(dys, dy_ref, d, pad=0)
            _permute_in(ls, l_ref, d, pad=0)
            _permute_in(dds, dn_nat, d, pad=0)
            dk_perm[...] = jnp.zeros((S + QB, 128), F32)
            dv_perm[...] = jnp.zeros((S + QB, 128), F32)

            def blk(b, carry):
                r0 = pl.multiple_of(b * QB, QB)
                q = qs[pl.ds(r0, QB), :]
                kb = ks[pl.ds(r0, 2 * QB), :]
                vb = vs[pl.ds(r0, 2 * QB), :]
                dyb = dys[pl.ds(r0, QB), :]
                lb = ls[pl.ds(r0, QB), :]
                db = dds[pl.ds(r0, QB), :]
                first = ((b % nb) == 0).astype(jnp.int32)
                lane2 = jnp.concatenate([lane, lane], axis=0)
                own = (lane2 >> 6) == (lax.broadcasted_iota(jnp.int32, (2 * QB, 128), 0) >> 7)
                qm = jnp.where(own, jnp.concatenate([q, q], axis=0), jnp.zeros((2 * QB, 128), q.dtype))
                dym = jnp.where(own, jnp.concatenate([dyb, dyb], axis=0), jnp.zeros((2 * QB, 128), dyb.dtype))
                wide = lambda t: jnp.concatenate([jnp.broadcast_to(t[:, 0:1], (QB, 2 * QB)), jnp.broadcast_to(t[:, 64:65], (QB, 2 * QB))], axis=0)
                lse2, dd2 = wide(lb), wide(db)
                bias2 = jnp.concatenate([b_ref[p, 0, first], b_ref[p, 1, first]], axis=0)
                pr = jnp.exp(_mm_nt(qm, kb) + bias2 - lse2)
                ds = pr * (_mm_nt(dym, vb) - dd2)
                db_ref[p, 0] += ds[0:QB]
                db_ref[p, 1] += ds[QB:2 * QB]
                dq2 = _mm(ds, kb)
                dqs = [dq2[0:QB], dq2[QB:2 * QB]]
                dkb = _mm_tn(ds, qm)
                dvb = _mm_tn(pr, dym)
                dq_perm[pl.ds(r0, QB), :] = jnp.where(lane < 64, dqs[0], dqs[1])
                dk_perm[pl.ds(r0, 2 * QB), :] += dkb
                dv_perm[pl.ds(r0, 2 * QB), :] += dvb
                return carry

            lax.fori_loop(0, S // QB, blk, 0, unroll=4)
            for r in range(d):
                idx = pl.ds(r, L, stride=d) if d > 1 else pl.ds(0, S)
                dq_ref[idx, :] += dq_perm[r * L:(r + 1) * L, :] * 0.125
                dk_ref[idx, :] += dk_perm[QB + r * L:QB + (r + 1) * L, :]
                dv_ref[idx, :] += dv_perm[QB + r * L:QB + (r + 1) * L, :]

    col = lambda c0: pl.BlockSpec((S, 128), lambda hp: (0, c0 + hp))
    bspec = pl.BlockSpec((3, 2, 8, 2 * QB), lambda hp: (0, hp, 0, 0))
    return _call(
        body, name=name, grid=(N_HEADS // 2,),
        in_specs=[col(0), col(4), col(8), bspec, col(0), col(0), col(0)],
        out_specs=(col(0), col(0), col(0), pl.BlockSpec((3, 2, QB, 2 * QB), lambda hp: (0, hp, 0, 0))),
        out_shape=(_sds((S, D_ATT), F32), _sds((S, D_ATT), F32), _sds((S, D_ATT), F32),
                   _sds((3, N_HEADS, QB, 2 * QB), F32)),
        scratch=[pltpu.VMEM((S, 128), MXU), pltpu.VMEM((S + QB, 128), MXU), pltpu.VMEM((S + QB, 128), MXU),
                 pltpu.VMEM((S, 128), MXU), pltpu.VMEM((S, 128), F32), pltpu.VMEM((S, 128), F32),
                 pltpu.VMEM((S, 128), F32), pltpu.VMEM((S, 128), F32),
                 pltpu.VMEM((S + QB, 128), F32), pltpu.VMEM((S + QB, 128), F32),
                 pltpu.VMEM((3, 2, 2, QB, 2 * QB), F32)],
        cp=_cp(48, ("arbitrary",)), args=(z, z, z, bias, y, lse, dy), ride=ride)


def relbias_grad(dbiases):
    bucket, band, _ = _att_static()
    onehot = (bucket[:, None] == np.arange(N_BUCKETS)[None, :, None, None]) & band[:, None]
    onehot = jnp.asarray(onehot.reshape(3, N_BUCKETS, QB * 2 * QB), BF16)

    def body(db0_ref, db1_ref, oh_ref, o_ref):
        acc = jnp.zeros((N_HEADS, N_BUCKETS), F32)
        for p in range(3):
            acc = acc + lax.dot_general(db0_ref[p] + db1_ref[p], oh_ref[p].astype(F32), (((1,), (1,)), ((), ())),
                                        preferred_element_type=F32, precision=lax.Precision.HIGHEST)
        o_ref[...] = acc

    vm = pl.BlockSpec(memory_space=pltpu.VMEM)
    out = _pallas_call(body, name="relbias_grad", in_specs=[vm, vm, vm], out_specs=vm,
                         out_shape=_sds((N_HEADS, N_BUCKETS), F32), compiler_params=_cp(40))(
        *[d.reshape(3, N_HEADS, QB * 2 * QB) for d in dbiases], onehot)
    return out.T


def _panel(t_ref, ri, j):
    return t_ref[ri, pl.ds(j, S, stride=8), :]


def _gelu(x):
    c = math.sqrt(2.0 / math.pi)
    th = jnp.tanh(c * (x + 0.044715 * x * x * x))
    return 0.5 * x * (1.0 + th), th


def ssm_fwd(z, a, bre, bim, cre, cim, dsk, gluw, glub, name, ride=None):
    def body(u_ref, a_ref, bre_ref, bim_ref, cre_ref, cim_ref, d_ref, gw_ref, gb_ref, y_ref, yp_ref, st_hbm, st_ref):
        u = u_ref[...]
        for j in range(8):
            st_ref[0, pl.ds(j, S, stride=8), :] = _mm(u, bre_ref[:, 128 * j:128 * (j + 1)])
            st_ref[1, pl.ds(j, S, stride=8), :] = _mm(u, bim_ref[:, 128 * j:128 * (j + 1)])
        ar, ai = a_ref[0], a_ref[1]

        def step(t, c):
            re, im = c
            i = pl.multiple_of(t * 8, 8)
            nre = ar * re - ai * im + st_ref[0, pl.ds(i, 8), :]
            nim = ar * im + ai * re + st_ref[1, pl.ds(i, 8), :]
            st_ref[0, pl.ds(i, 8), :] = nre
            st_ref[1, pl.ds(i, 8), :] = nim
            return nre, nim

        zero = jnp.zeros((8, 128), F32)
        lax.fori_loop(0, S, step, (zero, zero), unroll=8)
        y = d_ref[...] * u
        for j in range(8):
            y = y + _mm(_panel(st_ref, 0, j), cre_ref[128 * j:128 * (j + 1), :])
            y = y - _mm(_panel(st_ref, 1, j), cim_ref[128 * j:128 * (j + 1), :])
        pltpu.sync_copy(st_ref, st_hbm)
        yp_ref[...] = y
        gl, _ = _gelu(y)
        tt = _mm(gl, gw_ref[...].reshape(D_SSM, D_SSM)) + gb_ref[...]
        y_ref[...] = y * jax.nn.sigmoid(tt)

    vm = lambda shape: pl.BlockSpec(shape, lambda i: (0,) * len(shape))
    return _call(
        body, name=name, grid=(1,),
        in_specs=[pl.BlockSpec((S, 256), lambda i: (0, 6)), vm((2, 8, 128)), vm((256, 1024)), vm((256, 1024)),
                  vm((1024, 256)), vm((1024, 256)), vm((1, 256)),
                  vm((NDEV, 32, 256)), vm((1, 256))],
        out_specs=(vm((S, 256)), vm((S, 256)), pl.BlockSpec(memory_space=pl.ANY)),
        out_shape=(_sds((S, 256), F32), _sds((S, 256), F32), _sds((2, S * 8, 128), F32)),
        scratch=[pltpu.VMEM((2, S * 8, 128), F32)],
        cp=_cp(40, ("arbitrary",)), args=(z, a, bre, bim, cre, cim, dsk, gluw, glub), ride=ride)


def ssm_bwd(dy, z, ypre, st, a, bre, bim, cre, cim, dsk, gluw, glub, name, ride=None):
    def body(dy_ref, u_ref, yp_ref, st_hbm, a_ref, bre_ref, bim_ref, cre_ref, cim_ref, d_ref, gw_ref, gb_ref,
             du_ref, dbre_ref, dbim_ref, dcre_ref, dcim_ref, da_ref, dd_ref, dgw_ref, dgb_ref, g_ref, st_ref):
        pltpu.sync_copy(st_hbm, st_ref)
        u = u_ref[...]
        y = yp_ref[...]
        dout = dy_ref[...]
        gw = gw_ref[...].reshape(D_SSM, D_SSM)
        gl, th = _gelu(y)
        sig = jax.nn.sigmoid(_mm(gl, gw) + gb_ref[...])
        dt = dout * y * sig * (1.0 - sig)
        dgw_ref[...] = _mm_tn(gl, dt)
        dgb_ref[...] = jnp.sum(dt, axis=0, keepdims=True)
        c = math.sqrt(2.0 / math.pi)
        dgelu = 0.5 * (1.0 + th) + 0.5 * y * (1.0 - th * th) * c * (1.0 + 3.0 * 0.044715 * y * y)
        dyv = dout * sig + _mm_nt(dt, gw) * dgelu
        dd_ref[...] = jnp.sum(dyv * u, axis=0, keepdims=True)
        for j in range(8):
            rows = slice(128 * j, 128 * (j + 1))
            g_ref[0, pl.ds(j, S, stride=8), :] = _mm_nt(dyv, cre_ref[rows, :])
            g_ref[1, pl.ds(j, S, stride=8), :] = -_mm_nt(dyv, cim_ref[rows, :])
            dcre_ref[rows, :] = _mm_tn(_panel(st_ref, 0, j), dyv)
            dcim_ref[rows, :] = -_mm_tn(_panel(st_ref, 1, j), dyv)
        ar, ai = a_ref[0], a_ref[1]

        def step(k, c4):
            gre, gim, dar, dai = c4
            i = pl.multiple_of((S - 1 - k) * 8, 8)
            nre = g_ref[0, pl.ds(i, 8), :] + ar * gre + ai * gim
            nim = g_ref[1, pl.ds(i, 8), :] + ar * gim - ai * gre
            g_ref[0, pl.ds(i, 8), :] = nre
            g_ref[1, pl.ds(i, 8), :] = nim
            sre = st_ref[0, pl.ds(i - 8, 8), :]
            sim = st_ref[1, pl.ds(i - 8, 8), :]
            return nre, nim, dar + nre * sre + nim * sim, dai + nim * sre - nre * sim

        zero = jnp.zeros((8, 128), F32)
        gre, gim, dar, dai = lax.fori_loop(0, S - 1, step, (zero, zero, zero, zero), unroll=8)
        g_ref[0, 0:8, :] = g_ref[0, 0:8, :] + ar * gre + ai * gim
        g_ref[1, 0:8, :] = g_ref[1, 0:8, :] + ar * gim - ai * gre
        da_ref[0] = dar
        da_ref[1] = dai
        du = dyv * d_ref[...]
        for j in range(8):
            cols = slice(128 * j, 128 * (j + 1))
            gr, gi = _panel(g_ref, 0, j), _panel(g_ref, 1, j)
            dbre_ref[:, cols] = _mm_tn(u, gr)
            dbim_ref[:, cols] = _mm_tn(u, gi)
            du = du + _mm_nt(gr, bre_ref[:, cols]) + _mm_nt(gi, bim_ref[:, cols])
        du_ref[...] = du

    vm = lambda shape: pl.BlockSpec(shape, lambda i: (0,) * len(shape))
    return _call(
        body, name=name, grid=(1,),
        in_specs=[vm((S, 256)), pl.BlockSpec((S, 256), lambda i: (0, 6)), vm((S, 256)), pl.BlockSpec(memory_space=pl.ANY),
                  vm((2, 8, 128)), vm((256, 1024)), vm((256, 1024)), vm((1024, 256)), vm((1024, 256)), vm((1, 256)),
                  vm((NDEV, 32, 256)), vm((1, 256))],
        out_specs=(vm((S, 256)), vm((256, 1024)), vm((256, 1024)), vm((1024, 256)), vm((1024, 256)),
                   vm((2, 8, 128)), vm((1, 256)), vm((256, 256)), vm((1, 256))),
        out_shape=(_sds((S, 256), F32), _sds((256, 1024), F32), _sds((256, 1024), F32), _sds((1024, 256), F32),
                   _sds((1024, 256), F32), _sds((2, 8, 128), F32), _sds((1, 256), F32), _sds((256, 256), F32),
                   _sds((1, 256), F32)),
        scratch=[pltpu.VMEM((2, S * 8, 128), F32), pltpu.VMEM((2, S * 8, 128), F32)],
        cp=_cp(56, ("arbitrary",)), args=(dy, z, ypre, st, a, bre, bim, cre, cim, dsk, gluw, glub), ride=ride)


def _ssm_discretise(a_re, a_im, log_dt, b_re, b_im):
    dt = jnp.exp(log_dt)[:, None]
    er = jnp.exp(a_re * dt)
    abr, abi = er * jnp.cos(a_im * dt), er * jnp.sin(a_im * dt)
    den = a_re * a_re + a_im * a_im
    fr = ((abr - 1.0) * a_re + abi * a_im) / den
    fi = (abi * a_re - (abr - 1.0) * a_im) / den
    bbr = fr[:, :, None] * b_re - fi[:, :, None] * b_im
    bbi = fr[:, :, None] * b_im + fi[:, :, None] * b_re
    return abr, abi, bbr, bbi


def _blockdiag(t):
    g, r, c = t.shape
    eye = jnp.eye(g, dtype=t.dtype)
    return (t[:, :, None, :] * eye[:, None, :, None]).reshape(g * r, g * c)


def _blockdiag_take(m, r, c):
    g = m.shape[0] // r
    idx = jnp.arange(g)
    return m.reshape(g, r, g, c)[idx, :, idx, :]


PAD = 16


def _pool_lane_select(vals):
    lane = lax.broadcasted_iota(jnp.int32, vals[0].shape, 1)
    out = vals[3]
    for g in (2, 1, 0):
        out = jnp.where(lane < 64 * (g + 1), vals[g], out)
    return out


def _pool_counts():
    row = lax.broadcasted_iota(jnp.int32, (S, D_POOL), 0).astype(F32) + 1.0
    return _pool_lane_select([jnp.minimum(row, float(w)) for w in POOL_WINDOWS])


def _pooled(u, sa, sb):
    sums = []
    cur = u
    bufs = (sa, sb)
    for k, sh in enumerate((1, 2, 4, 8)):
        buf = bufs[k % 2]
        buf[PAD:PAD + S, :] = cur
        cur = cur + buf[PAD - sh:PAD - sh + S, :]
        sums.append(cur)
    return _pool_lane_select(sums) / _pool_counts() - u


def pool_fwd(z, pw, psc, name):
    def body(u_ref, w_ref, s_ref, y_ref, sa, sb):
        for buf in (sa, sb):
            buf[0:PAD, :] = jnp.zeros((PAD, D_POOL), F32)
        pooled = _pooled(u_ref[...], sa, sb)
        y_ref[...] = _mm(pooled, w_ref[...]) * s_ref[...]

    vm = lambda shape: pl.BlockSpec(shape, lambda i: (0,) * len(shape))
    return _pallas_call(
        body, name=name, grid=(1,),
        in_specs=[pl.BlockSpec((S, 256), lambda i: (0, 7)), vm((256, 256)), vm((1, 256))],
        out_specs=vm((S, 256)), out_shape=_sds((S, 256), F32),
        scratch_shapes=[pltpu.VMEM((S + 2 * PAD, D_POOL), F32)] * 2,
        compiler_params=_cp(40, ("arbitrary",)))(z, pw, psc)


def pool_bwd(dy, z, pw, psc, name):
    def body(dy_ref, u_ref, w_ref, s_ref, du_ref, dw_ref, ds_ref, sa, sb):
        for buf in (sa, sb):
            buf[0:PAD, :] = jnp.zeros((PAD, D_POOL), F32)
            buf[PAD + S:PAD + S + PAD, :] = jnp.zeros((PAD, D_POOL), F32)
        pooled = _pooled(u_ref[...], sa, sb)
        dyv = dy_ref[...]
        w = w_ref[...]
        ds_ref[...] = jnp.sum(dyv * _mm(pooled, w), axis=0, keepdims=True)
        dyl = dyv * s_ref[...]
        dw_ref[...] = _mm_tn(pooled, dyl)
        dpool = _mm_nt(dyl, w)
        cur = dpool / _pool_counts()
        sums = []
        bufs = (sa, sb)
        for k, sh in enumerate((1, 2, 4, 8)):
            buf = bufs[k % 2]
            buf[PAD:PAD + S, :] = cur
            cur = cur + buf[PAD + sh:PAD + sh + S, :]
            sums.append(cur)
        du_ref[...] = _pool_lane_select(sums) - dpool

    vm = lambda shape: pl.BlockSpec(shape, lambda i: (0,) * len(shape))
    return _pallas_call(
        body, name=name, grid=(1,),
        in_specs=[vm((S, 256)), pl.BlockSpec((S, 256), lambda i: (0, 7)), vm((256, 256)), vm((1, 256))],
        out_specs=(vm((S, 256)), vm((256, 256)), vm((1, 256))),
        out_shape=(_sds((S, 256), F32), _sds((256, 256), F32), _sds((1, 256), F32)),
        scratch_shapes=[pltpu.VMEM((S + 2 * PAD, D_POOL), F32)] * 2,
        compiler_params=_cp(40, ("arbitrary",)))(dy, z, pw, psc)


def ada_fwd(c_all, ada_w, ada_b_cols):
    def body(c_ref, w_ref, b_ref, o_ref):
        c = c_ref[...]
        cond = c * jax.nn.sigmoid(c)
        o_ref[...] = jnp.dot(cond, w_ref[...], preferred_element_type=F32, precision=lax.Precision.HIGHEST) + b_ref[...]

    return _pallas_call(
        body, name="ada_fwd", grid=(DEPTH,),
        in_specs=[pl.BlockSpec((NDEV, D), lambda l: (0, 0)), pl.BlockSpec((None, D, 1152), lambda l: (l, 0, 0)),
                  pl.BlockSpec((None, 1, 1152), lambda l: (l, 0, 0))],
        out_specs=pl.BlockSpec((None, NDEV, 1152), lambda l: (l, 0, 0)), out_shape=_sds((DEPTH, NDEV, 1152), F32),
        compiler_params=_cp(40, ("arbitrary",)))(c_all, ada_w, ada_b_cols)


def ada_bwd(c_all, dmod_cols):
    def body(c_ref, dm_ref, o_ref):
        c = c_ref[...]
        cond = c * jax.nn.sigmoid(c)
        o_ref[...] = lax.dot_general(cond, dm_ref[...], (((0,), (0,)), ((), ())), preferred_element_type=F32,
                                     precision=lax.Precision.HIGHEST)

    return _pallas_call(
        body, name="ada_bwd", grid=(DEPTH,),
        in_specs=[pl.BlockSpec((NDEV, D), lambda l: (0, 0)), pl.BlockSpec((None, NDEV, 1152), lambda l: (l, 0, 0))],
        out_specs=pl.BlockSpec((None, D, 1152), lambda l: (l, 0, 0)), out_shape=_sds((DEPTH, D, 1152), F32),
        compiler_params=_cp(40, ("arbitrary",)))(c_all, dmod_cols)


def _adamw(w, g, m, v):
    m2 = B1 * m + (1.0 - B1) * g
    v2 = B2 * v + (1.0 - B2) * (g * g)
    m_hat = m2 / (1.0 - B1 ** STEP)
    v_hat = v2 / (1.0 - B2 ** STEP)
    return -LR * (m_hat / (jnp.sqrt(v_hat) + EPS) + WD * w), m2, v2


def _sum8(ref):
    g = ref[0].astype(F32)
    for s in range(1, NDEV):
        g = g + ref[s].astype(F32)
    return g


def adam_rs(recv, w, m, v, tr, name, ride=None):
    lead, (r, cdim) = w.shape[:-2], w.shape[-2:]
    cp = recv.shape[-1]
    nl = len(lead)

    def body(rc_ref, w_ref, m_ref, v_ref, g_ref, d_ref, m2_ref, v2_ref):
        g = _sum8(rc_ref)[:, :cdim]
        g_ref[...] = g
        d_ref[...], m2_ref[...], v2_ref[...] = _adamw(w_ref[...], g, m_ref[...], v_ref[...])

    rs = pl.BlockSpec((None,) * nl + (tr, cdim), lambda *i: (*i, 0))
    return _call(
        body, name=name, grid=lead + (r // tr,),
        in_specs=[pl.BlockSpec((NDEV,) + (None,) * nl + (tr, cp), lambda *i: (0, *i, 0)), rs, rs, rs],
        out_specs=(rs, rs, rs, rs), out_shape=tuple(_sds(w.shape, F32) for _ in range(4)),
        cp=_cp(48, ("arbitrary",) * (nl + 1)), args=(recv, w, m, v), ride=ride)


def adam_block(recv, w, m, v, lf, prev, name):
    half = FB // 2

    def body(*refs):
        rc_ref, w_ref, m_ref, v_ref = refs[:4]
        g_ref, d_ref, m2_ref, v2_ref = refs[-4:]
        g = _sum8(rc_ref)
        g_ref[...] = g
        d_ref[...], m2_ref[...], v2_ref[...] = _adamw(w_ref[...], g, m_ref[...], v_ref[...])

    rs = pl.BlockSpec((None, None, half, D), lambda i: (lf // 2, lf % 2, i, 0))
    prev = list(prev) if prev is not None else []
    return list(_pallas_call(
        body, name=name, grid=(2,), in_specs=[pl.BlockSpec((NDEV, half, D), lambda i: (0, i, 0)), rs, rs, rs] + [ANY] * len(prev),
        out_specs=(rs, rs, rs, rs), out_shape=tuple(_sds((DEPTH, 2, FB, D), F32) for _ in range(4)),
        input_output_aliases={4 + k: k for k in range(len(prev))},
        compiler_params=_cp(48, ("arbitrary",)))(recv, w, m, v, *prev))


def adam_plain(g, w, m, v, tr, name, ride=None):
    lead, (r, cdim) = w.shape[:-2], w.shape[-2:]
    nl = len(lead)

    def body(g_ref, w_ref, m_ref, v_ref, d_ref, m2_ref, v2_ref):
        d_ref[...], m2_ref[...], v2_ref[...] = _adamw(w_ref[...], g_ref[...], m_ref[...], v_ref[...])

    rs = pl.BlockSpec((None,) * nl + (tr, cdim), lambda *i: (*i, 0))
    return _call(
        body, name=name, grid=lead + (r // tr,), in_specs=[rs, rs, rs, rs], out_specs=(rs, rs, rs),
        out_shape=tuple(_sds(w.shape, F32) for _ in range(3)),
        cp=_cp(48, ("arbitrary",) * (nl + 1)), args=(g, w, m, v), ride=ride)


def adam_native(gs, ws, ms, vs, name):
    n = len(ws)

    def body(*refs):
        g_refs, w_refs, m_refs, v_refs = (refs[k * n:(k + 1) * n] for k in range(4))
        d_refs, m2_refs, v2_refs = (refs[(4 + k) * n:(5 + k) * n] for k in range(3))
        for a in range(n):
            d_refs[a][...], m2_refs[a][...], v2_refs[a][...] = _adamw(w_refs[a][...], g_refs[a][...], m_refs[a][...], v_refs[a][...])

    vm = pl.BlockSpec(memory_space=pltpu.VMEM)
    outs = _pallas_call(body, name=name, in_specs=[vm] * (4 * n), out_specs=tuple([vm] * (3 * n)),
                        out_shape=tuple(_sds(w.shape, F32) for w in ws) * 3, compiler_params=_cp(40))(*gs, *ws, *ms, *vs)
    return outs[:n], outs[n:2 * n], outs[2 * n:]


def sum_sources(recv, name):
    r = recv.shape[1]

    def body(rc_ref, o_ref):
        o_ref[...] = _sum8(rc_ref)

    vm = pl.BlockSpec(memory_space=pltpu.VMEM)
    return _pallas_call(body, name=name, in_specs=[vm], out_specs=vm, out_shape=_sds((r, 128), F32),
                          compiler_params=_cp(40))(recv)


def _pack(arrs):
    flat = jnp.concatenate([a.reshape(-1) for a in arrs])
    n = flat.shape[0]
    rows = -(-n // 1024) * 8
    return jnp.pad(flat, (0, rows * 128 - n)).reshape(rows, 128)


def _unpack(vec, shapes):
    flat = vec.reshape(-1)
    out, o = [], 0
    for sh in shapes:
        n = int(np.prod(sh))
        out.append(flat[o:o + n].reshape(sh))
        o += n
    return out


WEIGHTS = ['rel_bias', 'ada_w', 'ada_b', 'ln_g', 'ln_b', 'ffn_w_gate', 'ffn_w_up', 'ffn_w_down', 'w_in', 'w_out',
           'ssm_a_re', 'ssm_a_im', 'ssm_log_dt', 'ssm_b_re', 'ssm_b_im', 'ssm_c_re', 'ssm_c_im', 'ssm_d', 'glu_w',
           'glu_b', 'pool_w', 'pool_scale']
SMALL = ['rel_bias', 'ada_b', 'ln_g', 'ln_b', 'ssm_a_re', 'ssm_a_im', 'ssm_log_dt', 'ssm_b_re', 'ssm_b_im',
         'ssm_c_re', 'ssm_c_im', 'ssm_d', 'glu_b', 'pool_w', 'pool_scale']
SMALL_FULL_SHAPES = {'rel_bias': (32, 8), 'ada_b': (2, 9216), 'ln_g': (2, 3, 1024), 'ln_b': (2, 3, 1024),
                     'ssm_a_re': (2, 16, 64), 'ssm_a_im': (2, 16, 64), 'ssm_log_dt': (2, 16),
                     'ssm_b_re': (2, 16, 64, 16), 'ssm_b_im': (2, 16, 64, 16), 'ssm_c_re': (2, 16, 16, 64),
                     'ssm_c_im': (2, 16, 16, 64), 'ssm_d': (2, 256), 'glu_b': (2, 256), 'pool_w': (2, 4, 64, 64),
                     'pool_scale': (2, 256)}


def _step(P):
    me = _me()
    x0 = P['x'][0]
    target = P['loss_target'][0]

    bf = lambda a: a.astype(BF16)
    padr = lambda a: jnp.pad(bf(a), ((0, 0), (0, 0), (0, FBP - FB), (0, 0)))
    ffn_b = [padr(jnp.swapaxes(P['ffn_w_gate'], 2, 3)), padr(jnp.swapaxes(P['ffn_w_up'], 2, 3)), padr(P['ffn_w_down'])]
    mix_b = [bf(P['w_in']), bf(P['w_out']), bf(P['glu_w'])]

    def shards(l, sub):
        return [t[l] for t in mix_b] if sub == 1 else [t[l, sub // 2] for t in ffn_b]

    order = [(l, sub) for l in range(DEPTH) for sub in range(3)]
    nxt = dict(zip(order[:-1], order[1:]))
    W = {key: [None] * 3 for key in order}
    c_all, lng_all, lnb_all, *W[order[0]] = _exchange(Gather([P['c'], P['ln_g'], P['ln_b']] + shards(*order[0])), "gather_first")
    gather_queue = [(key, pos, a) for key in order[1:] for pos, a in enumerate(shards(*key))]

    def gather_ride(cap_us, must=None):
        units, used = [], 0.0
        while gather_queue:
            key, _, a = gather_queue[0]
            cost = a.size * a.dtype.itemsize * GATHER_US_PER_BYTE
            if key != must and used + cost / 2 > cap_us:
                break
            units.append(gather_queue.pop(0))
            used += cost
        return (Gather([a for _, _, a in units]) if units else None), units

    def gathered(units, outs):
        for (key, pos, _), o in zip(units, outs):
            W[key][pos] = o

    c_all = c_all.reshape(NDEV, D)
    ln_g = jnp.transpose(lng_all, (1, 2, 0, 3)).reshape(DEPTH, 3, D)
    ln_b = jnp.transpose(lnb_all, (1, 2, 0, 3)).reshape(DEPTH, 3, D)

    ada_b_cols = lax.dynamic_slice_in_dim(P['ada_b'], me * 1152, 1152, axis=1).reshape(DEPTH, 1, 1152)
    modc = ada_fwd(c_all, P['ada_w'], ada_b_cols)
    (mod_all,) = _exchange(Gather([modc]), "gather_mod")
    mod_me = lax.dynamic_index_in_dim(mod_all, me, axis=2, keepdims=False)
    mod = jnp.transpose(mod_me, (1, 0, 2)).reshape(DEPTH, 9, D)

    bias = att_bias(P['rel_bias'])
    ssm = []
    for l in range(DEPTH):
        prm = (P['ssm_a_re'][l], P['ssm_a_im'][l], P['ssm_log_dt'][l], P['ssm_b_re'][l], P['ssm_b_im'][l])
        (abr, abi, bbr, bbi), disc_vjp = jax.vjp(_ssm_discretise, *prm)
        ssm.append(dict(
            vjp=disc_vjp, a=jnp.stack([abr.reshape(8, 128), abi.reshape(8, 128)]),
            bre=_blockdiag(jnp.transpose(bbr, (0, 2, 1))).astype(MXU), bim=_blockdiag(jnp.transpose(bbi, (0, 2, 1))).astype(MXU),
            cre=_blockdiag(jnp.transpose(P['ssm_c_re'][l], (0, 2, 1))).astype(MXU),
            cim=_blockdiag(jnp.transpose(P['ssm_c_im'][l], (0, 2, 1))).astype(MXU),
            d=P['ssm_d'][l].reshape(1, 256), gb=P['glu_b'][l].reshape(1, 256),
            pw=_blockdiag(P['pool_w'][l]).astype(MXU), psc=P['pool_scale'][l].reshape(1, 256)))

    saved = []
    x = x0
    h = ln_mod_fwd(x, mod[0], 0, "ln_mod_fwd_l0s0")
    for l, sub in order:
        tag = f"l{l}s{sub}"
        after = (mod[nxt[(l, sub)][0]], nxt[(l, sub)][1]) if (l, sub) in nxt else None
        if sub != 1:
            wg, wu, wd = (t.reshape(NDEV * FBP, D) for t in W[(l, sub)])
            ride, units = gather_ride(60, nxt.get((l, sub)))
            (G, U, fo), got = ffn_fwd(h, wg, wu, wd, "ffn_fwd_" + tag, ride)
            gathered(units, got)
            saved.append(dict(x=x, h=h, G=G, U=U, f=fo))
            x, *hn = res_ln_fwd(x, fo, mod[l], sub, ln_g[l], ln_b[l], 0.5, "res_ln_fwd_" + tag, after)
        else:
            sp = ssm[l]
            win, wout, gluw = W[(l, sub)]
            ride, units = gather_ride(15)
            (z,), got = win_fwd(h, win, "win_fwd_" + tag, ride)
            gathered(units, got)
            ride, units = gather_ride(55)
            (ya, lse), got = att_fwd(z, bias, "att_fwd_" + tag, ride)
            gathered(units, got)
            ride, units = gather_ride(35)
            (ys, ypre, st), got = ssm_fwd(z, sp['a'], sp['bre'], sp['bim'], sp['cre'], sp['cim'], sp['d'], gluw, sp['gb'],
                                          "ssm_fwd_" + tag, ride)
            gathered(units, got)
            yp = pool_fwd(z, sp['pw'], sp['psc'], "pool_fwd_" + tag)
            ride, units = gather_ride(12, nxt.get((l, sub)))
            (o,), got = wout_fwd(ya, ys, yp, wout, "wout_fwd_" + tag, ride)
            gathered(units, got)
            saved.append(dict(x=x, h=h, z=z, ya=ya, lse=lse, ys=ys, ypre=ypre, st=st, yp=yp, f=o))
            x, *hn = res_ln_fwd(x, o, mod[l], sub, ln_g[l], ln_b[l], 1.0, "res_ln_fwd_" + tag, after)
        h = hn[0] if hn else None
    assert not gather_queue

    loss_tile, dx = loss_fwd_bwd(x, target, "loss")
    loss = lax.psum(loss_tile[0, 0], ("x", "y", "c"))

    flights = []

    def hosted(cap_us, fn, *args):
        return fn(*args, None)[0]

    dmod = [[None] * 9 for _ in range(DEPTH)]
    dlng = [[None] * 3 for _ in range(DEPTH)]
    dlnb = [[None] * 3 for _ in range(DEPTH)]
    dbiases = [None] * DEPTH
    small_l = [dict() for _ in range(DEPTH)]
    for l, sub in reversed(order):
        tag = f"l{l}s{sub}"
        sv = saved[3 * l + sub]
        if (l, sub) == order[-1]:
            dxa, df, sums = hosted(0, res_ln_bwd, sv['x'], sv['f'], mod[l], sub, ln_g[l], dx, 0.5, "res_ln_bwd_" + tag)
        dlng[l][sub], dlnb[l][sub], dmod[l][3 * sub + 2] = sums[0], sums[1], sums[2]
        if sub != 1:
            f = sub // 2
            wg, wu, wd = (t.reshape(NDEV * FBP, D) for t in W[(l, sub)])
            dwg, dwu, dwd, dh = hosted(105, ffn_bwd, df, sv['h'], sv['G'], sv['U'], wg, wu, wd, "ffn_bwd_" + tag)
            payload = [t.reshape(NDEV, FBP, D) for t in (dwg, dwu, dwd)]
            if (l, sub) == order[0]:
                last_payload, zero = payload, 0.0
            else:
                handle, zero = scatter_start(payload, "scatter_start_" + tag)
                flights.append(((l, sub), handle))
        else:
            sp = ssm[l]
            win, wout, gluw = W[(l, sub)]
            dya, dys, dyp, dwout = hosted(0, wout_bwd, df, sv['ya'], sv['ys'], sv['yp'], wout, "wout_bwd_" + tag)
            dq, dk, dv, dbiases[l] = hosted(115, att_bwd, sv['z'], bias, sv['ya'], sv['lse'], dya, "att_bwd_" + tag)
            dus, dbre, dbim, dcre, dcim, da, dd, dgw, dgb = hosted(
                52, ssm_bwd, dys, sv['z'], sv['ypre'], sv['st'], sp['a'], sp['bre'], sp['bim'], sp['cre'], sp['cim'], sp['d'],
                gluw, sp['gb'], "ssm_bwd_" + tag)
            dup, dpw, dpsc = pool_bwd(dyp, sv['z'], sp['pw'], sp['psc'], "pool_bwd_" + tag)
            dh, dwin = hosted(40, win_bwd, (dq, dk, dv, dus, dup), sv['h'], win, "win_bwd_" + tag)
            handle, zero = scatter_start([dwin, dwout, dgw.astype(BF16).reshape(NDEV, 32, 256)], "scatter_start_" + tag)
            flights.append(((l, sub), handle))
            d_are, d_aim, d_ldt, d_bre, d_bim = sp['vjp']((
                da[0].reshape(16, 64), da[1].reshape(16, 64),
                jnp.transpose(_blockdiag_take(dbre, 16, 64), (0, 2, 1)), jnp.transpose(_blockdiag_take(dbim, 16, 64), (0, 2, 1))))
            small_l[l] = dict(
                ssm_a_re=d_are, ssm_a_im=d_aim, ssm_log_dt=d_ldt, ssm_b_re=d_bre, ssm_b_im=d_bim,
                ssm_c_re=jnp.transpose(_blockdiag_take(dcre, 64, 16), (0, 2, 1)),
                ssm_c_im=jnp.transpose(_blockdiag_take(dcim, 64, 16), (0, 2, 1)),
                ssm_d=dd.reshape(256), glu_b=dgb.reshape(256), pool_w=_blockdiag_take(dpw, 64, 64), pool_scale=dpsc.reshape(256))
        if (l, sub) == order[0]:
            dx, sums2 = hosted(0, ln_mod_bwd, sv['x'], dh, mod[l] + zero, sub, dxa, "ln_mod_bwd_" + tag)
        else:
            lp, sp_ = order[order.index((l, sub)) - 1]
            svp = saved[3 * lp + sp_]
            dxa, df, sums, sums2 = ln_join_bwd(svp['x'], svp['f'], mod[lp], sp_, ln_g[lp], ln_b[lp], 1.0 if sp_ == 1 else 0.5,
                                               dh, mod[l] + zero, sub, dxa, "ln_join_bwd_" + tag)
        dmod[l][3 * sub], dmod[l][3 * sub + 1] = sums2[0], sums2[1]
    grad_x = dx[None]

    small = {k: jnp.stack([small_l[l][k] for l in range(DEPTH)]) for k in small_l[0]}
    small['rel_bias'] = relbias_grad(dbiases)
    small['ada_b'] = jnp.stack([jnp.stack(dmod[l]).reshape(9 * D) for l in range(DEPTH)])
    small['ln_g'] = jnp.stack([jnp.stack(dlng[l]) for l in range(DEPTH)])
    small['ln_b'] = jnp.stack([jnp.stack(dlnb[l]) for l in range(DEPTH)])
    (small_all,) = _exchange(Gather([_pack([small[k] for k in SMALL])]), "gather_small")
    handle, zero = scatter_start(last_payload, "scatter_start_l%ds%d" % order[0], after=(small_all,))
    flights.append((order[0], handle))

    out = {}

    def put(name, g, d, m2, v2, shape):
        out['grad_' + name], out['delta_' + name] = g.reshape(shape), d.reshape(shape)
        out['new_m_' + name], out['new_v_' + name] = m2.reshape(shape), v2.reshape(shape)

    def wmv(name):
        return [P[pre + name] for pre in ('', 'm_', 'v_')]

    recv = {}
    started_last = flights[-1][1][1][0]
    for key, handle in flights[:-1]:
        recv[key] = scatter_wait(handle, started_last, "scatter_wait_l%ds%d" % key)
    for pos, (name, tr) in enumerate((('w_in', 512), ('w_out', 128), ('glu_w', 32))):
        both = jnp.stack([recv[(l, 1)][pos] for l in range(DEPTH)], axis=1)
        put(name, *adam_rs(both, *wmv(name), tr, "adam_" + name)[0], P[name].shape)
    ffn = (('ffn_w_gate', [jnp.swapaxes(t, 2, 3) for t in wmv('ffn_w_gate')]),
           ('ffn_w_up', [jnp.swapaxes(t, 2, 3) for t in wmv('ffn_w_up')]), ('ffn_w_down', wmv('ffn_w_down')))
    part = [None] * 3
    for l, sub in [key for key, _ in flights[:-1] if key[1] != 1]:
        for pos, (name, ops) in enumerate(ffn):
            part[pos] = adam_block(recv[(l, sub)][pos], *ops, 2 * l + sub // 2, part[pos], f"adam_{name}_l{l}s{sub}")
    (l, sub), handle = flights[-1]
    last = scatter_wait(handle, part[2][0], "scatter_wait_l%ds%d" % (l, sub))
    for pos, (name, ops) in enumerate(ffn):
        res = adam_block(last[pos], *ops, 2 * l + sub // 2, part[pos], f"adam_{name}_l{l}s{sub}")
        put(name, *([jnp.swapaxes(t, 2, 3) for t in res] if pos < 2 else res), P[name].shape)

    gsum = dict(zip(SMALL, _unpack(sum_sources(small_all, "sum_small"), [SMALL_FULL_SHAPES[k] for k in SMALL])))
    off = 256
    dmod_all = small_all.reshape(NDEV, -1)[:, off:off + DEPTH * 9 * D].reshape(NDEV, DEPTH, 9 * D)
    dmod_cols = jnp.transpose(lax.dynamic_slice_in_dim(dmod_all, me * 1152, 1152, axis=2), (1, 0, 2))
    g_ada_w = ada_bwd(c_all, dmod_cols)

    put('ada_w', g_ada_w, *adam_plain(g_ada_w, *wmv('ada_w'), 256, "adam_ada_w")[0], P['ada_w'].shape)

    for k in ('ln_g', 'ln_b'):
        gsum[k] = lax.dynamic_slice_in_dim(gsum[k], me * 128, 128, axis=2)
    swaps = {'rel_bias': (0, 1), 'ln_g': (0, 1), 'ln_b': (0, 1), 'ssm_b_re': (2, 3), 'ssm_b_im': (2, 3)}
    view = lambda k, t: jnp.swapaxes(t, *swaps[k]) if k in swaps else t
    ds_, m2s, v2s = adam_native(*[[view(k, src(k)) for k in SMALL] for src in
                                  (lambda k: gsum[k], lambda k: P[k], lambda k: P['m_' + k], lambda k: P['v_' + k])],
                                "adam_small")
    for k, d, m2, v2 in zip(SMALL, ds_, m2s, v2s):
        put(k, gsum[k], view(k, d), view(k, m2), view(k, v2), P[k].shape)

    res = [loss, grad_x]
    for pre in ('grad_', 'delta_', 'new_m_', 'new_v_'):
        res += [out[pre + k] for k in WEIGHTS]
    return tuple(res)


def kernel(x, c, rel_bias, ada_w, ada_b, ln_g, ln_b, ffn_w_gate, ffn_w_up, ffn_w_down, w_in, w_out, ssm_a_re, ssm_a_im, ssm_log_dt, ssm_b_re, ssm_b_im, ssm_c_re, ssm_c_im, ssm_d, glu_w, glu_b, pool_w, pool_scale, loss_target, m_rel_bias, m_ada_w, m_ada_b, m_ln_g, m_ln_b, m_ffn_w_gate, m_ffn_w_up, m_ffn_w_down, m_w_in, m_w_out, m_ssm_a_re, m_ssm_a_im, m_ssm_log_dt, m_ssm_b_re, m_ssm_b_im, m_ssm_c_re, m_ssm_c_im, m_ssm_d, m_glu_w, m_glu_b, m_pool_w, m_pool_scale, v_rel_bias, v_ada_w, v_ada_b, v_ln_g, v_ln_b, v_ffn_w_gate, v_ffn_w_up, v_ffn_w_down, v_w_in, v_w_out, v_ssm_a_re, v_ssm_a_im, v_ssm_log_dt, v_ssm_b_re, v_ssm_b_im, v_ssm_c_re, v_ssm_c_im, v_ssm_d, v_glu_w, v_glu_b, v_pool_w, v_pool_scale):
    return _step(dict(locals()))
```

```python
import functools
import math

import numpy as np
import jax
import jax.numpy as jnp
from jax import lax
from jax.experimental import pallas as pl
from jax.experimental.pallas import tpu as pltpu

F32 = jnp.float32
BF16 = jnp.bfloat16
MXU = jnp.bfloat16

S = 2048
D = 1024
NDEV = 8
DEPTH = 2
D_ATT, D_SSM, D_POOL, D_IN = 512, 256, 256, 2048
N_HEADS = 8
FB = 352
FBP = 384
QB = 128
PATTERNS = ((128, 1), (512, 4), (2048, 16))
POOL_WINDOWS = (2, 4, 8, 16)
N_BUCKETS, MAX_DISTANCE = 32, 2048
ALPHA = (2 * DEPTH) ** 0.25
LN_EPS = 1e-5
NEG = -1e30
GATHER_US_PER_BYTE = 43e-6
SCATTER_US_PER_BYTE = 21.6e-6
LR, B1, B2, EPS, WD, STEP = 0.001, 0.9, 0.999, 1e-08, 0.01, 10

TM = 256
TMM = 512
MIB = 1024 * 1024


def _cp(vmem_mib, sem=None):
    kw = dict(vmem_limit_bytes=vmem_mib * MIB)
    if sem is not None:
        kw["dimension_semantics"] = sem
    return pltpu.CompilerParams(**kw)


def _sds(shape, dtype):
    return jax.ShapeDtypeStruct(shape, dtype)


def _mm(a, b):
    return jnp.dot(a.astype(MXU), b.astype(MXU), preferred_element_type=F32)


def _mm_nt(a, b):
    return lax.dot_general(a.astype(MXU), b.astype(MXU), (((1,), (1,)), ((), ())), preferred_element_type=F32)


def _mm_tn(a, b):
    return lax.dot_general(a.astype(MXU), b.astype(MXU), (((0,), (0,)), ((), ())), preferred_element_type=F32)


def _ln_stats(x):
    mu = jnp.mean(x, axis=-1, keepdims=True)
    xc = x - mu
    var = jnp.mean(xc * xc, axis=-1, keepdims=True)
    rstd = lax.rsqrt(var + LN_EPS)
    return xc * rstd, rstd


def _ln_bwd(dn, n, rstd):
    return rstd * (dn - jnp.mean(dn, axis=-1, keepdims=True) - n * jnp.mean(dn * n, axis=-1, keepdims=True))


def _me():
    return 4 * lax.axis_index("x") + 2 * lax.axis_index("y") + lax.axis_index("c")


ANY = pl.BlockSpec(memory_space=pl.ANY)
PIN_BYTES = 1 << 19


def _pallas_call(*a, **k):
    big = lambda o: math.prod(o.shape) * o.dtype.itemsize >= PIN_BYTES
    pin = lambda o: pltpu.HBM(o.shape, o.dtype) if isinstance(o, jax.ShapeDtypeStruct) and big(o) else o
    osh = k["out_shape"]
    k["out_shape"] = tuple(pin(o) for o in osh) if isinstance(osh, (tuple, list)) else pin(osh)
    fn = pl.pallas_call(*a, **k)

    def run(*args):
        return fn(*[pltpu.with_memory_space_constraint(x, pltpu.HBM) if big(x) else x for x in args])
    return run


class Gather:
    def __init__(self, srcs):
        self.srcs = list(srcs)
        self.n = len(self.srcs)
        self.bufs = []
        self.out_shapes = [_sds((NDEV,) + a.shape, a.dtype) for a in self.srcs]
        self.sems = [pltpu.SemaphoreType.DMA((7 * self.n,)), pltpu.SemaphoreType.DMA((7 * self.n,)),
                     pltpu.SemaphoreType.DMA((self.n,))]

    def _parts(self, srcs, outs, sems):
        send_sems, recv_sems, loc_sems = sems
        x, y, c = lax.axis_index("x"), lax.axis_index("y"), lax.axis_index("c")
        me, sib = (x, y, c), (x, y, 1 - c)
        chips = [(1 - x, y), (x, 1 - y), (1 - x, 1 - y)]
        slot = lambda d: 4 * d[0] + 2 * d[1] + d[2]

        def copy(a, k, block, to, src=None):
            dst = outs[a].at[slot(block)]
            return pltpu.make_async_remote_copy(
                src_ref=dst if src is None else src, dst_ref=dst,
                send_sem=send_sems.at[7 * a + k], recv_sem=recv_sems.at[7 * a + k],
                device_id=to, device_id_type=pl.DeviceIdType.MESH)

        local = [pltpu.make_async_copy(srcs[a], outs[a].at[slot(me)], loc_sems.at[a]) for a in range(self.n)]
        return me, sib, chips, c, copy, local

    def start(self, srcs, bufs, outs, sems):
        me, sib, chips, c, copy, local = self._parts(srcs, outs, sems)
        for a in range(self.n):
            local[a].start()
            copy(a, 0, me, sib, src=srcs[a]).start()
            for j, chip in enumerate(chips):
                copy(a, 1 + j, me, (*chip, c), src=srcs[a]).start()

    def finish(self, srcs, bufs, outs, sems):
        me, sib, chips, c, copy, local = self._parts(srcs, outs, sems)
        for a in range(self.n):
            for j, chip in enumerate(chips):
                copy(a, 1 + j, (*chip, c), me).wait_recv()
                copy(a, 4 + j, (*chip, c), sib).start()
        for a in range(self.n):
            copy(a, 0, sib, me).wait_recv()
            copy(a, 0, me, sib, src=srcs[a]).wait_send()
            for j, chip in enumerate(chips):
                copy(a, 4 + j, (*chip, 1 - c), me).wait_recv()
                copy(a, 1 + j, me, (*chip, c), src=srcs[a]).wait_send()
                copy(a, 4 + j, (*chip, c), sib).wait_send()
            local[a].wait()


class Scatter:
    def __init__(self, items, bufs):
        self.items = list(items)
        self.keys = list(dict.fromkeys(key for _, key, _, _ in self.items))
        self.srcs = [src for src, _, _, _ in self.items]
        self.bufs = [bufs[key] for key in self.keys]
        self.n = len(self.srcs)
        self.out_shapes = [_sds(b.shape, b.dtype) for b in self.bufs]
        pairs = [(a, k) for a, (_, _, _, ks) in enumerate(self.items) for k in ks]
        self.remote_pairs = [p for p in pairs if p[1] != 0]
        self.local_pairs = [p for p in pairs if p[1] == 0]
        self.sems = [pltpu.SemaphoreType.DMA((max(len(self.remote_pairs), 1),)),
                     pltpu.SemaphoreType.DMA((max(len(self.remote_pairs), 1),)),
                     pltpu.SemaphoreType.DMA((max(len(self.local_pairs), 1),))]

    def _copies(self, srcs, outs, sems):
        send_sems, recv_sems, loc_sems = sems
        me = _me()

        def dst(a, slot):
            _, key, index, _ = self.items[a]
            return outs[self.keys.index(key)].at[(slot,) + tuple(index)]

        def remote(n, slot):
            a, k = self.remote_pairs[n]
            t = me ^ k
            return pltpu.make_async_remote_copy(
                src_ref=srcs[a].at[t], dst_ref=dst(a, slot), send_sem=send_sems.at[n], recv_sem=recv_sems.at[n],
                device_id=(t // 4, (t // 2) % 2, t % 2), device_id_type=pl.DeviceIdType.MESH)

        local = [pltpu.make_async_copy(srcs[a].at[me], dst(a, me), loc_sems.at[n])
                 for n, (a, _) in enumerate(self.local_pairs)]
        return me, remote, local

    def start(self, srcs, bufs, outs, sems):
        me, remote, local = self._copies(srcs, outs, sems)
        for cp in local:
            cp.start()
        for n in range(len(self.remote_pairs)):
            remote(n, me).start()

    def finish(self, srcs, bufs, outs, sems):
        me, remote, local = self._copies(srcs, outs, sems)
        for n, (_, k) in enumerate(self.remote_pairs):
            remote(n, me ^ k).wait()
        for cp in local:
            cp.wait()


def _call(body, *, name, grid, in_specs, out_specs, out_shape, args, scratch=(), cp=None, ride=None):
    out_specs, out_shape, scratch = list(out_specs), list(out_shape), list(scratch)
    if ride is None:
        outs = _pallas_call(body, name=name, grid=grid, in_specs=list(in_specs), out_specs=tuple(out_specs),
                              out_shape=tuple(out_shape), scratch_shapes=scratch, compiler_params=cp)(*args)
        return list(outs), []
    nin, nout, nscr, n, nb, no = len(in_specs), len(out_specs), len(scratch), ride.n, len(ride.bufs), len(ride.out_shapes)
    steps = list(grid)

    def wrapped(*refs):
        h_in, r_src, r_buf = refs[:nin], refs[nin:nin + n], refs[nin + n:nin + n + nb]
        o0 = nin + n + nb
        h_out, r_out = refs[o0:o0 + nout], refs[o0 + nout:o0 + nout + no]
        s0 = o0 + nout + no
        h_scr, sems = refs[s0:s0 + nscr], refs[s0 + nscr:]
        ids = [pl.program_id(a) for a in range(len(steps))]
        first = functools.reduce(jnp.logical_and, [i == 0 for i in ids])
        last = functools.reduce(jnp.logical_and, [i == s - 1 for i, s in zip(ids, steps)])

        @pl.when(first)
        def _():
            ride.start(r_src, r_buf, r_out, sems)

        body(*h_in, *h_out, *h_scr)

        @pl.when(last)
        def _():
            ride.finish(r_src, r_buf, r_out, sems)

    aliases = {nin + n + k: nout + k for k in range(nb)}
    outs = _pallas_call(
        wrapped, name=name, grid=grid, in_specs=list(in_specs) + [ANY] * (n + nb),
        out_specs=tuple(out_specs + [ANY] * no), out_shape=tuple(out_shape + ride.out_shapes),
        scratch_shapes=scratch + ride.sems, input_output_aliases=aliases, compiler_params=cp,
    )(*args, *ride.srcs, *ride.bufs)
    return list(outs[:nout]), list(outs[nout:])


def _exchange(ride, name):
    def body(dummy_ref, o_ref):
        o_ref[...] = dummy_ref[...]

    one = pl.BlockSpec((8, 128), lambda i: (0, 0))
    _, outs = _call(body, name=name, grid=(1,), in_specs=[one], out_specs=[one], out_shape=[_sds((8, 128), F32)],
                    args=(jnp.zeros((8, 128), F32),), ride=ride)
    return outs


HBM = pl.BlockSpec(memory_space=pltpu.HBM)
SEM = pl.BlockSpec(memory_space=pltpu.SEMAPHORE)


def routes_all(me):
    return [(k, me ^ k, me, me ^ k) for k in range(NDEV)]


def routes_sibling(me):
    return [(1, 2 * q + 1 - (me & 1), q, q) for q in range(NDEV // 2)]


def routes_chips(me):
    return [(2 * m, (me >> 1) ^ m, me >> 1, (me >> 1) ^ m) for m in range(NDEV // 2)]


def _scatter_copies(srcs, lands, sems, routes, sending):
    send_sems, recv_sems, loc_sems = sems
    me = _me()
    rts = routes(me)
    remote_ix = [r for r, (k, _, _, _) in enumerate(rts) if k != 0]
    local_ix = [r for r, (k, _, _, _) in enumerate(rts) if k == 0]
    remote, local = [], []
    for a in range(len(srcs)):
        for n, r in enumerate(remote_ix):
            k, slab, there, here = rts[r]
            t = me ^ k
            sem = len(remote_ix) * a + n
            remote.append(pltpu.make_async_remote_copy(
                src_ref=srcs[a].at[slab], dst_ref=lands[a].at[there if sending else here], send_sem=send_sems.at[sem],
                recv_sem=recv_sems.at[sem], device_id=(t // 4, (t // 2) % 2, t % 2), device_id_type=pl.DeviceIdType.MESH))
        for n, r in enumerate(local_ix):
            _, slab, there, _ = rts[r]
            local.append(pltpu.make_async_copy(srcs[a].at[slab], lands[a].at[there], loc_sems.at[len(local_ix) * a + n]))
    return remote, local


def scatter_start(payloads, name, after=(), routes=routes_all, slots=NDEV):
    n, na = len(payloads), len(after)
    rts = routes(0)
    nr, nl = sum(k != 0 for k, _, _, _ in rts), sum(k == 0 for k, _, _, _ in rts)

    def body(*refs):
        srcs, lands, sems = refs[:n], refs[n:2 * n], refs[2 * n + na:2 * n + na + 3]
        remote, local = _scatter_copies(srcs, lands, sems, routes, True)
        for cp in local + remote:
            cp.start()
        refs[-1][...] = jnp.zeros((8, 128), F32)

    land_shapes = [(slots,) + p.shape[1:] for p in payloads]
    outs = pl.pallas_call(
        body, name=name,
        out_shape=(pltpu.SemaphoreType.DMA((nr * n,)), pltpu.SemaphoreType.DMA((nr * n,)), pltpu.SemaphoreType.DMA((max(nl, 1) * n,)),
                   *[pltpu.HBM(p.shape, p.dtype) for p in payloads], *[pltpu.HBM(sh, p.dtype) for sh, p in zip(land_shapes, payloads)],
                   _sds((8, 128), F32)),
        in_specs=[HBM] * (2 * n + na),
        out_specs=(SEM, SEM, SEM, *[HBM] * (2 * n), pl.BlockSpec(memory_space=pltpu.VMEM)),
        input_output_aliases={i: 3 + i for i in range(2 * n)},
        compiler_params=pltpu.CompilerParams(has_side_effects=pltpu.SideEffectType.DATAFLOW_SIDE_EFFECTING),
    )(*[pltpu.with_memory_space_constraint(p, pltpu.HBM) for p in payloads],
      *[pltpu.with_memory_space_constraint(lax.empty(sh, p.dtype), pltpu.HBM) for sh, p in zip(land_shapes, payloads)],
      *[pltpu.with_memory_space_constraint(a, pltpu.HBM) for a in after])
    return (outs[:3], outs[3:3 + n], outs[3 + n:3 + 2 * n], routes), outs[-1][0, 0]


def scatter_wait(handle, after, name):
    sems, srcs_thru, lands_thru, routes = handle
    n = len(srcs_thru)

    def body(*refs):
        srcs, lands, sems_ = refs[:n], refs[n:2 * n], refs[2 * n:2 * n + 3]
        remote, local = _scatter_copies(srcs, lands, sems_, routes, False)
        for cp in remote:
            cp.wait_send()
            cp.wait_recv()
        for cp in local:
            cp.wait()

    outs = pl.pallas_call(
        body, name=name, out_shape=tuple(pltpu.HBM(p.shape, p.dtype) for p in (*srcs_thru, *lands_thru)),
        in_specs=[HBM] * (2 * n) + [SEM] * 3 + [HBM], out_specs=tuple([HBM] * (2 * n)),
        input_output_aliases={i: i for i in range(2 * n)},
        compiler_params=pltpu.CompilerParams(has_side_effects=pltpu.SideEffectType.DATAFLOW_SIDE_EFFECTING),
    )(*srcs_thru, *lands_thru, *sems, pltpu.with_memory_space_constraint(after, pltpu.HBM))
    return list(outs[n:]), list(outs[:n])


def pair_sum(payloads, theirs, name):
    n, (_, r, dd) = len(payloads), payloads[0].shape
    tr = 128

    def body(*refs):
        c = lax.axis_index("c")
        for a in range(n):
            p0, p1, th, o = refs[3 * a], refs[3 * a + 1], refs[3 * a + 2], refs[3 * n + a]
            mine = jnp.where(c == 0, p0[...], p1[...])
            o[...] = (mine.astype(F32) + th[...].astype(F32)).astype(BF16)

    mine = lambda cc: pl.BlockSpec((None, None, tr, dd), lambda q, i: (q, cc, i, 0))
    one = pl.BlockSpec((None, tr, dd), lambda q, i: (q, i, 0))
    args = []
    for p, t in zip(payloads, theirs):
        p4 = p.reshape(NDEV // 2, 2, r, dd)
        args += [p4, p4, t]
    return _pallas_call(
        body, name=name, grid=(NDEV // 2, r // tr), in_specs=[mine(0), mine(1), one] * n, out_specs=tuple([one] * n),
        out_shape=tuple(_sds((NDEV // 2, r, dd), BF16) for _ in range(n)),
        compiler_params=_cp(32, ("arbitrary", "arbitrary")))(*args)


def _row_spec(cols, tm=TM):
    return pl.BlockSpec((tm, cols), lambda i: (i, 0))


def _full_spec(shape):
    nd = len(shape)
    return pl.BlockSpec(shape, lambda i: (0,) * nd)


def ln_mod_fwd(x, mod, sub, name):
    def body(x_ref, mod_ref, h_ref):
        n, _ = _ln_stats(x_ref[...])
        shift = mod_ref[3 * sub:3 * sub + 1, :]
        scale = mod_ref[3 * sub + 1:3 * sub + 2, :]
        h_ref[...] = (n * (1.0 + scale) + shift).astype(MXU)

    return _pallas_call(
        body, name=name, grid=(S // TM,),
        in_specs=[_row_spec(D), _full_spec((9, D))], out_specs=_row_spec(D),
        out_shape=_sds((S, D), MXU), compiler_params=_cp(32, ("arbitrary",)))(x, mod)


def res_ln_fwd(x, f, mod, sub, lng, lnb, w, name, nxt=None):
    def body(x_ref, f_ref, mod_ref, g_ref, b_ref, *rest):
        gate = mod_ref[3 * sub + 2:3 * sub + 3, :]
        r = ALPHA * x_ref[...] + (w * gate) * f_ref[...]
        n, _ = _ln_stats(r)
        xo = n * g_ref[sub:sub + 1, :] + b_ref[sub:sub + 1, :]
        rest[-1 if nxt is None else -2][...] = xo
        if nxt is not None:
            nmod_ref, h_ref = rest[0], rest[-1]
            n2, _ = _ln_stats(xo)
            s2 = nxt[1]
            h_ref[...] = (n2 * (1.0 + nmod_ref[3 * s2 + 1:3 * s2 + 2, :]) + nmod_ref[3 * s2:3 * s2 + 1, :]).astype(MXU)

    more = nxt is not None
    return _pallas_call(
        body, name=name, grid=(S // TM,),
        in_specs=[_row_spec(D), _row_spec(D), _full_spec((9, D)), _full_spec((3, D)), _full_spec((3, D))] + [_full_spec((9, D))] * more,
        out_specs=(_row_spec(D),) + (_row_spec(D),) * more, out_shape=(_sds((S, D), F32),) + (_sds((S, D), MXU),) * more,
        compiler_params=_cp(32, ("arbitrary",)))(x, f, mod, lng, lnb, *([nxt[0]] if more else []))


def res_ln_bwd(x, f, mod, sub, lng, dxo, w, name, ride=None):
    def body(x_ref, f_ref, mod_ref, g_ref, dxo_ref, dxa_ref, df_ref, sums_ref):
        i = pl.program_id(0)
        gate = mod_ref[3 * sub + 2:3 * sub + 3, :]
        fv = f_ref[...]
        r = ALPHA * x_ref[...] + (w * gate) * fv
        n, rstd = _ln_stats(r)
        dxo = dxo_ref[...]
        dr = _ln_bwd(dxo * g_ref[sub:sub + 1, :], n, rstd)
        dxa_ref[...] = ALPHA * dr
        df_ref[...] = ((w * gate) * dr).astype(MXU)
        part = jnp.concatenate([
            jnp.sum(dxo * n, axis=0, keepdims=True),
            jnp.sum(dxo, axis=0, keepdims=True),
            jnp.sum(dr * fv, axis=0, keepdims=True) * w,
            jnp.zeros((5, D), F32)], axis=0)

        @pl.when(i == 0)
        def _():
            sums_ref[...] = part

        @pl.when(i > 0)
        def _():
            sums_ref[...] += part

    return _call(
        body, name=name, grid=(S // TM,),
        in_specs=[_row_spec(D), _row_spec(D), _full_spec((9, D)), _full_spec((3, D)), _row_spec(D)],
        out_specs=(_row_spec(D), _row_spec(D), _full_spec((8, D))),
        out_shape=(_sds((S, D), F32), _sds((S, D), MXU), _sds((8, D), F32)),
        cp=_cp(32, ("arbitrary",)), args=(x, f, mod, lng, dxo), ride=ride)


def ln_mod_bwd(x, dh, mod, sub, dxa, name, ride=None):
    def body(x_ref, dh_ref, mod_ref, dxa_ref, dx_ref, sums_ref):
        i = pl.program_id(0)
        scale = mod_ref[3 * sub + 1:3 * sub + 2, :]
        n, rstd = _ln_stats(x_ref[...])
        dh = dh_ref[...]
        dx_ref[...] = dxa_ref[...] + _ln_bwd(dh * (1.0 + scale), n, rstd)
        part = jnp.concatenate([
            jnp.sum(dh, axis=0, keepdims=True),
            jnp.sum(dh * n, axis=0, keepdims=True),
            jnp.zeros((6, D), F32)], axis=0)

        @pl.when(i == 0)
        def _():
            sums_ref[...] = part

        @pl.when(i > 0)
        def _():
            sums_ref[...] += part

    return _call(
        body, name=name, grid=(S // TM,),
        in_specs=[_row_spec(D), _row_spec(D), _full_spec((9, D)), _row_spec(D)],
        out_specs=(_row_spec(D), _full_spec((8, D))),
        out_shape=(_sds((S, D), F32), _sds((8, D), F32)),
        cp=_cp(32, ("arbitrary",)), args=(x, dh, mod, dxa), ride=ride)


def ln_join_bwd(xp, fp, modp, subp, lngp, lnbp, wp, dh, mod, sub, dxa, name):
    def body(xp_ref, fp_ref, modp_ref, g_ref, b_ref, dh_ref, mod_ref, dxa_ref, dxap_ref, dfp_ref, sumsp_ref, sums_ref):
        i = pl.program_id(0)
        gate = modp_ref[3 * subp + 2:3 * subp + 3, :]
        fv = fp_ref[...]
        n, rstd = _ln_stats(ALPHA * xp_ref[...] + (wp * gate) * fv)
        gain = g_ref[subp:subp + 1, :]
        n2, rstd2 = _ln_stats(n * gain + b_ref[subp:subp + 1, :])
        dh = dh_ref[...]
        dx = dxa_ref[...] + _ln_bwd(dh * (1.0 + mod_ref[3 * sub + 1:3 * sub + 2, :]), n2, rstd2)
        dr = _ln_bwd(dx * gain, n, rstd)
        dxap_ref[...] = ALPHA * dr
        dfp_ref[...] = ((wp * gate) * dr).astype(MXU)
        partp = jnp.concatenate([
            jnp.sum(dx * n, axis=0, keepdims=True), jnp.sum(dx, axis=0, keepdims=True),
            jnp.sum(dr * fv, axis=0, keepdims=True) * wp, jnp.zeros((5, D), F32)], axis=0)
        part = jnp.concatenate([
            jnp.sum(dh, axis=0, keepdims=True), jnp.sum(dh * n2, axis=0, keepdims=True), jnp.zeros((6, D), F32)], axis=0)

        @pl.when(i == 0)
        def _():
            sumsp_ref[...] = partp
            sums_ref[...] = part

        @pl.when(i > 0)
        def _():
            sumsp_ref[...] += partp
            sums_ref[...] += part

    return _pallas_call(
        body, name=name, grid=(S // TM,),
        in_specs=[_row_spec(D), _row_spec(D), _full_spec((9, D)), _full_spec((3, D)), _full_spec((3, D)), _row_spec(D),
                  _full_spec((9, D)), _row_spec(D)],
        out_specs=(_row_spec(D), _row_spec(D), _full_spec((8, D)), _full_spec((8, D))),
        out_shape=(_sds((S, D), F32), _sds((S, D), MXU), _sds((8, D), F32), _sds((8, D), F32)),
        compiler_params=_cp(40, ("arbitrary",)))(xp, fp, modp, lngp, lnbp, dh, mod, dxa)


def loss_fwd_bwd(y, target, name):
    def body(y_ref, t_ref, l_ref, dy_ref):
        i = pl.program_id(0)
        e = y_ref[...] - t_ref[...]
        dy_ref[...] = e * (1.0 / D)
        part = jnp.zeros((8, 128), F32) + (0.5 / D) * jnp.sum(e * e)

        @pl.when(i == 0)
        def _():
            l_ref[...] = part

        @pl.when(i > 0)
        def _():
            l_ref[...] += part

    return _pallas_call(
        body, name=name, grid=(S // TM,),
        in_specs=[_row_spec(D), _row_spec(D)], out_specs=(_full_spec((8, 128)), _row_spec(D)),
        out_shape=(_sds((8, 128), F32), _sds((S, D), F32)),
        compiler_params=_cp(32, ("arbitrary",)))(y, target)


HB = 2 * FBP
NHB = NDEV * FBP // HB
TMB = 1024


def _wrows(buffers=2):
    return pl.BlockSpec((HB, D), lambda j, i: (j, 0), pipeline_mode=pl.Buffered(buffers))


def _resident(shape):
    return pl.BlockSpec(shape, lambda j, i: (0, 0), pipeline_mode=pl.Buffered(1))


def ffn_fwd(h, wgt, wut, wd, name, ride=None):
    def body(h_ref, wg_ref, wu_ref, wd_ref, g_ref, u_ref, f_ref):
        j, i = pl.program_id(0), pl.program_id(1)
        hv = h_ref[...]
        g = _mm_nt(hv, wg_ref[...])
        u = _mm_nt(hv, wu_ref[...])
        g_ref[...] = g.astype(MXU)
        u_ref[...] = u.astype(MXU)
        a = g * jax.nn.sigmoid(g) * u
        part = _mm(a, wd_ref[...])
        rows = pl.ds(pl.multiple_of(i * TMB, TMB), TMB)

        @pl.when(j == 0)
        def _():
            f_ref[rows, :] = part

        @pl.when(j > 0)
        def _():
            f_ref[rows, :] += part

    gu = pl.BlockSpec((TMB, HB), lambda j, i: (i, j))
    return _call(
        body, name=name, grid=(NHB, S // TMB),
        in_specs=[pl.BlockSpec((TMB, D), lambda j, i: (i, 0)), _wrows(), _wrows(), _wrows()],
        out_specs=(gu, gu, _resident((S, D))),
        out_shape=(_sds((S, NDEV * FBP), MXU), _sds((S, NDEV * FBP), MXU), _sds((S, D), F32)),
        cp=_cp(52, ("arbitrary", "arbitrary")), args=(h, wgt, wut, wd), ride=ride)


def ffn_bwd(df, h, g, u, wgt, wut, wd, name, ride=None):
    ni = S // TMB

    def body(df_ref, h_ref, g_ref, u_ref, wg_ref, wu_ref, wd_ref, dwg_ref, dwu_ref, dwd_ref, dh_ref,
             ag_ref, au_ref, ad_ref):
        j, i = pl.program_id(0), pl.program_id(1)
        dfv, hv = df_ref[...], h_ref[...]
        gv, uv = g_ref[...].astype(F32), u_ref[...].astype(F32)
        da = _mm_nt(dfv, wd_ref[...])
        sg = jax.nn.sigmoid(gv)
        silu = gv * sg
        du = da * silu
        dg = da * uv * (sg * (1.0 + gv * (1.0 - sg)))
        p_d = _mm_tn(silu * uv, dfv)
        p_g = _mm_tn(dg, hv)
        p_u = _mm_tn(du, hv)

        @pl.when(i == 0)
        def _():
            ad_ref[...] = p_d
            ag_ref[...] = p_g
            au_ref[...] = p_u

        @pl.when(i > 0)
        def _():
            ad_ref[...] += p_d
            ag_ref[...] += p_g
            au_ref[...] += p_u

        @pl.when(i == ni - 1)
        def _():
            dwd_ref[...] = ad_ref[...].astype(BF16)
            dwg_ref[...] = ag_ref[...].astype(BF16)
            dwu_ref[...] = au_ref[...].astype(BF16)

        part = _mm(dg, wg_ref[...]) + _mm(du, wu_ref[...])
        rows = pl.ds(pl.multiple_of(i * TMB, TMB), TMB)

        @pl.when(j == 0)
        def _():
            dh_ref[rows, :] = part

        @pl.when(j > 0)
        def _():
            dh_ref[rows, :] += part

    gu = pl.BlockSpec((TMB, HB), lambda j, i: (i, j))
    rowt = pl.BlockSpec((TMB, D), lambda j, i: (i, 0))
    return _call(
        body, name=name, grid=(NHB, ni),
        in_specs=[rowt, rowt, gu, gu, _wrows(1), _wrows(1), _wrows(1)],
        out_specs=(_wrows(1), _wrows(1), _wrows(1), _resident((S, D))),
        out_shape=(_sds((NDEV * FBP, D), BF16), _sds((NDEV * FBP, D), BF16), _sds((NDEV * FBP, D), BF16), _sds((S, D), F32)),
        scratch=[pltpu.VMEM((HB, D), F32), pltpu.VMEM((HB, D), F32), pltpu.VMEM((HB, D), F32)],
        cp=_cp(60, ("arbitrary", "arbitrary")), args=(df, h, g, u, wgt, wut, wd), ride=ride)


def win_fwd(h, win, name, ride=None):
    def body(h_ref, w_ref, z_ref):
        hv = h_ref[...]
        for j in range(NDEV):
            z_ref[:, 256 * j:256 * (j + 1)] = _mm(hv, w_ref[j])

    return _call(
        body, name=name, grid=(S // TMM,),
        in_specs=[_row_spec(D, TMM), _full_spec((NDEV, D, 256))],
        out_specs=[_row_spec(D_IN, TMM)], out_shape=[_sds((S, D_IN), F32)],
        cp=_cp(40, ("arbitrary",)), args=(h, win), ride=ride)


def win_bwd(dparts, h, win, name, ride=None):
    ni = S // TMM

    def body(dq_ref, dk_ref, dv_ref, dus_ref, dup_ref, h_ref, w_ref, dh_ref, dw_ref, acc_ref):
        i = pl.program_id(0)
        hv = h_ref[...]
        cols = [dq_ref[:, 0:256], dq_ref[:, 256:512], dk_ref[:, 0:256], dk_ref[:, 256:512],
                dv_ref[:, 0:256], dv_ref[:, 256:512], dus_ref[...], dup_ref[...]]
        dh = jnp.zeros((TMM, D), F32)
        for j in range(NDEV):
            dz = cols[j].astype(MXU)
            dh = dh + _mm_nt(dz, w_ref[j])
            p = _mm_tn(hv, dz)

            @pl.when(i == 0)
            def _():
                acc_ref[j] = p

            @pl.when(i > 0)
            def _():
                acc_ref[j] += p

        dh_ref[...] = dh

        @pl.when(i == ni - 1)
        def _():
            dw_ref[...] = acc_ref[...].astype(BF16)

    return _call(
        body, name=name, grid=(ni,),
        in_specs=[_row_spec(512, TMM), _row_spec(512, TMM), _row_spec(512, TMM), _row_spec(256, TMM), _row_spec(256, TMM),
                  _row_spec(D, TMM), _full_spec((NDEV, D, 256))],
        out_specs=(_row_spec(D, TMM), _full_spec((NDEV, D, 256))),
        out_shape=(_sds((S, D), F32), _sds((NDEV, D, 256), BF16)),
        scratch=[pltpu.VMEM((NDEV, D, 256), F32)],
        cp=_cp(48, ("arbitrary",)), args=(*dparts, h, win), ride=ride)


def wout_fwd(ya, ys, yp, wout, name, ride=None):
    def body(ya_ref, ys_ref, yp_ref, w_ref, o_ref):
        w = w_ref[...].reshape(D, D)
        o_ref[...] = _mm(ya_ref[...], w[0:512]) + _mm(ys_ref[...], w[512:768]) + _mm(yp_ref[...], w[768:1024])

    return _call(
        body, name=name, grid=(S // TMM,),
        in_specs=[_row_spec(512, TMM), _row_spec(256, TMM), _row_spec(256, TMM), _full_spec((NDEV, 128, D))],
        out_specs=[_row_spec(D, TMM)], out_shape=[_sds((S, D), F32)],
        cp=_cp(40, ("arbitrary",)), args=(ya, ys, yp, wout), ride=ride)


def wout_bwd(do, ya, ys, yp, wout, name, ride=None):
    ni = S // TMM

    def body(do_ref, ya_ref, ys_ref, yp_ref, w_ref, dya_ref, dys_ref, dyp_ref, dw_ref, acc_ref):
        i = pl.program_id(0)
        w = w_ref[...].reshape(D, D)
        dov = do_ref[...]
        dya_ref[...] = _mm_nt(dov, w[0:512])
        dys_ref[...] = _mm_nt(dov, w[512:768])
        dyp_ref[...] = _mm_nt(dov, w[768:1024])
        parts = [(0, 512, _mm_tn(ya_ref[...], dov)), (512, 768, _mm_tn(ys_ref[...], dov)),
                 (768, 1024, _mm_tn(yp_ref[...], dov))]
        for lo, hi, p in parts:
            @pl.when(i == 0)
            def _():
                acc_ref[lo:hi, :] = p

            @pl.when(i > 0)
            def _():
                acc_ref[lo:hi, :] += p

        @pl.when(i == ni - 1)
        def _():
            dw_ref[...] = acc_ref[...].astype(BF16).reshape(NDEV, 128, D)

    return _call(
        body, name=name, grid=(ni,),
        in_specs=[_row_spec(D, TMM), _row_spec(512, TMM), _row_spec(256, TMM), _row_spec(256, TMM),
                  _full_spec((NDEV, 128, D))],
        out_specs=(_row_spec(512, TMM), _row_spec(256, TMM), _row_spec(256, TMM), _full_spec((NDEV, 128, D))),
        out_shape=(_sds((S, 512), F32), _sds((S, 256), F32), _sds((S, 256), F32), _sds((NDEV, 128, D), BF16)),
        scratch=[pltpu.VMEM((D, D), F32)],
        cp=_cp(40, ("arbitrary",)), args=(do, ya, ys, yp, wout), ride=ride)


def _t5_bucket(dist):
    max_exact = N_BUCKETS // 2
    d = np.maximum(dist, 1).astype(np.float32)
    large = max_exact + (np.log(d / max_exact) / math.log(MAX_DISTANCE / max_exact)
                         * (N_BUCKETS - max_exact)).astype(np.int32)
    large = np.minimum(large, N_BUCKETS - 1)
    return np.where(dist < max_exact, dist, large).astype(np.int32)


def _att_static():
    i = np.arange(QB)[:, None]
    j = np.arange(2 * QB)[None, :]
    r = i + QB - j
    buckets, bands = [], []
    for window, dil in PATTERNS:
        bands.append((r >= 0) & (r <= window // dil))
        buckets.append(_t5_bucket(np.clip(r, 0, None) * dil))
    return np.stack(buckets), np.stack(bands), np.broadcast_to(j >= QB, (QB, 2 * QB))


def att_bias(rel_bias):
    m = np.arange(2 * QB)
    rows = []
    for window, dil in PATTERNS:
        r = QB - m
        ok = (r >= 0) & (r <= window // dil)
        b = rel_bias[_t5_bucket(np.clip(r, 0, None) * dil)]
        rows.append(jnp.where(ok[:, None], b, NEG).T)
    return jnp.broadcast_to(jnp.stack(rows)[:, :, None, :], (3, N_HEADS, 8, 2 * QB))


def _bias_tiles(t_ref, tiles):
    col = lax.broadcasted_iota(jnp.int32, (QB, 2 * QB), 1)
    for p in range(3):
        for hh in range(2):
            t = pltpu.roll(jnp.broadcast_to(t_ref[p, hh, 0:1, :], (QB, 2 * QB)), 0, 1, stride=1, stride_axis=0)
            tiles[p, hh, 0] = t
            tiles[p, hh, 1] = jnp.where(col >= QB, t, NEG)


def _permute_in(dst_ref, src_ref, d, scale=None, pad=QB):
    L = S // d
    for r in range(d):
        v = src_ref[pl.ds(r, L, stride=d), :] if d > 1 else src_ref[...]
        if scale is not None:
            v = v * scale
        dst_ref[pad + r * L:pad + (r + 1) * L, :] = v.astype(dst_ref.dtype)


def att_fwd(z, bias, name, ride=None):
    def body(q_ref, k_ref, v_ref, t_ref, y_ref, l_ref, qs, ks, vs, o_perm, l_perm, o_nat, l_nat, b_ref):
        _bias_tiles(t_ref, b_ref)
        zero_pad = jnp.zeros((QB, 128), MXU)
        ks[0:QB, :] = zero_pad
        vs[0:QB, :] = zero_pad
        lane = lax.broadcasted_iota(jnp.int32, (QB, 128), 1)
        for p, (_, d) in enumerate(PATTERNS):
            L = S // d
            nb = L // QB
            _permute_in(qs, q_ref, d, scale=0.125, pad=0)
            _permute_in(ks, k_ref, d)
            _permute_in(vs, v_ref, d)

            def blk(b, carry):
                r0 = pl.multiple_of(b * QB, QB)
                q = qs[pl.ds(r0, QB), :]
                kb = ks[pl.ds(r0, 2 * QB), :]
                vb = vs[pl.ds(r0, 2 * QB), :]
                first = ((b % nb) == 0).astype(jnp.int32)
                res = []
                for hh in range(2):
                    sel = (lane < 64) if hh == 0 else (lane >= 64)
                    qm = jnp.where(sel, q, jnp.zeros_like(q))
                    s = _mm_nt(qm, kb) + b_ref[p, hh, first]
                    m = jnp.max(s, axis=1, keepdims=True)
                    pe = jnp.exp(s - m)
                    den = jnp.sum(pe, axis=1, keepdims=True)
                    res.append((_mm(pe, vb) / den, m + jnp.log(den)))
                o_perm[pl.ds(r0, QB), :] = jnp.where(lane < 64, res[0][0], res[1][0])
                l_perm[pl.ds(r0, QB), :] = jnp.where(lane < 64, res[0][1], res[1][1])
                return carry

            lax.fori_loop(0, S // QB, blk, 0, unroll=8)
            for r in range(d):
                if d > 1:
                    o_nat[p, pl.ds(r, L, stride=d), :] = o_perm[r * L:(r + 1) * L, :]
                    l_nat[p, pl.ds(r, L, stride=d), :] = l_perm[r * L:(r + 1) * L, :]
                else:
                    o_nat[p] = o_perm[...]
                    l_nat[p] = l_perm[...]
        l0, l1, l2 = l_nat[0], l_nat[1], l_nat[2]
        m = jnp.maximum(jnp.maximum(l0, l1), l2)
        e0, e1, e2 = jnp.exp(l0 - m), jnp.exp(l1 - m), jnp.exp(l2 - m)
        den = e0 + e1 + e2
        y_ref[...] = (e0 * o_nat[0] + e1 * o_nat[1] + e2 * o_nat[2]) / den
        l_ref[...] = m + jnp.log(den)

    col = lambda c0: pl.BlockSpec((S, 128), lambda hp: (0, c0 + hp))
    return _call(
        body, name=name, grid=(N_HEADS // 2,),
        in_specs=[col(0), col(4), col(8), pl.BlockSpec((3, 2, 8, 2 * QB), lambda hp: (0, hp, 0, 0))],
        out_specs=(col(0), col(0)),
        out_shape=(_sds((S, D_ATT), F32), _sds((S, D_ATT), F32)),
        scratch=[pltpu.VMEM((S, 128), MXU), pltpu.VMEM((S + QB, 128), MXU), pltpu.VMEM((S + QB, 128), MXU),
                 pltpu.VMEM((S, 128), F32), pltpu.VMEM((S, 128), F32),
                 pltpu.VMEM((3, S, 128), F32), pltpu.VMEM((3, S, 128), F32),
                 pltpu.VMEM((3, 2, 2, QB, 2 * QB), F32)],
        cp=_cp(40, ("arbitrary",)), args=(z, z, z, bias), ride=ride)


def att_bwd(z, bias, y, lse, dy, name, ride=None):
    def body(q_ref, k_ref, v_ref, t_ref, y_ref, l_ref, dy_ref, dq_ref, dk_ref, dv_ref, db_ref,
             qs, ks, vs, dys, ls, dds, dn_nat, dq_perm, dk_perm, dv_perm, b_ref):
        _bias_tiles(t_ref, b_ref)
        zero_pad = jnp.zeros((QB, 128), MXU)
        ks[0:QB, :] = zero_pad
        vs[0:QB, :] = zero_pad
        lane = lax.broadcasted_iota(jnp.int32, (QB, 128), 1)
        lane_s = lax.broadcasted_iota(jnp.int32, (S, 128), 1)
        t = dy_ref[...] * y_ref[...]
        sa = jnp.sum(jnp.where(lane_s < 64, t, 0.0), axis=1, keepdims=True)
        sb = jnp.sum(jnp.where(lane_s >= 64, t, 0.0), axis=1, keepdims=True)
        dn_nat[...] = jnp.where(lane_s < 64, sa, sb)
        dq_ref[...] = jnp.zeros((S, 128), F32)
        dk_ref[...] = jnp.zeros((S, 128), F32)
        dv_ref[...] = jnp.zeros((S, 128), F32)
        db_ref[...] = jnp.zeros((3, 2, QB, 2 * QB), F32)
        for p, (_, d) in enumerate(PATTERNS):
            L = S // d
            nb = L // QB
            _permute_in(qs, q_ref, d, scale=0.125, pad=0)
            _permute_in(ks, k_ref, d)
            _permute_in(vs, v_ref, d)
            _permute_in(dys, dy_ref, d, pad=0)
            _permute_in(ls, l_ref, d, pad=0)
            _permute_in(dds, dn_nat, d, pad=0)
            dk_perm[...] = jnp.zeros((S + QB, 128), F32)
            dv_perm[...] = jnp.zeros((S + QB, 128), F32)

            def blk(b, carry):
                r0 = pl.multiple_of(b * QB, QB)
                q = qs[pl.ds(r0, QB), :]
                kb = ks[pl.ds(r0, 2 * QB), :]
                vb = vs[pl.ds(r0, 2 * QB), :]
                dyb = dys[pl.ds(r0, QB), :]
                lb = ls[pl.ds(r0, QB), :]
                db = dds[pl.ds(r0, QB), :]
                first = ((b % nb) == 0).astype(jnp.int32)
                lane2 = jnp.concatenate([lane, lane], axis=0)
                own = (lane2 >> 6) == (lax.broadcasted_iota(jnp.int32, (2 * QB, 128), 0) >> 7)
                qm = jnp.where(own, jnp.concatenate([q, q], axis=0), jnp.zeros((2 * QB, 128), q.dtype))
                dym = jnp.where(own, jnp.concatenate([dyb, dyb], axis=0), jnp.zeros((2 * QB, 128), dyb.dtype))
                wide = lambda t: jnp.concatenate([jnp.broadcast_to(t[:, 0:1], (QB, 2 * QB)), jnp.broadcast_to(t[:, 64:65], (QB, 2 * QB))], axis=0)
                lse2, dd2 = wide(lb), wide(db)
                bias2 = jnp.concatenate([b_ref[p, 0, first], b_ref[p, 1, first]], axis=0)
                pr = jnp.exp(_mm_nt(qm, kb) + bias2 - lse2)
                ds = pr * (_mm_nt(dym, vb) - dd2)
                db_ref[p, 0] += ds[0:QB]
                db_ref[p, 1] += ds[QB:2 * QB]
                dq2 = _mm(ds, kb)
                dqs = [dq2[0:QB], dq2[QB:2 * QB]]
                dkb = _mm_tn(ds, qm)
                dvb = _mm_tn(pr, dym)
                dq_perm[pl.ds(r0, QB), :] = jnp.where(lane < 64, dqs[0], dqs[1])
                dk_perm[pl.ds(r0, 2 * QB), :] += dkb
                dv_perm[pl.ds(r0, 2 * QB), :] += dvb
                return carry

            lax.fori_loop(0, S // QB, blk, 0, unroll=4)
            for r in range(d):
                idx = pl.ds(r, L, stride=d) if d > 1 else pl.ds(0, S)
                dq_ref[idx, :] += dq_perm[r * L:(r + 1) * L, :] * 0.125
                dk_ref[idx, :] += dk_perm[QB + r * L:QB + (r + 1) * L, :]
                dv_ref[idx, :] += dv_perm[QB + r * L:QB + (r + 1) * L, :]

    col = lambda c0: pl.BlockSpec((S, 128), lambda hp: (0, c0 + hp))
    bspec = pl.BlockSpec((3, 2, 8, 2 * QB), lambda hp: (0, hp, 0, 0))
    return _call(
        body, name=name, grid=(N_HEADS // 2,),
        in_specs=[col(0), col(4), col(8), bspec, col(0), col(0), col(0)],
        out_specs=(col(0), col(0), col(0), pl.BlockSpec((3, 2, QB, 2 * QB), lambda hp: (0, hp, 0, 0))),
        out_shape=(_sds((S, D_ATT), F32), _sds((S, D_ATT), F32), _sds((S, D_ATT), F32),
                   _sds((3, N_HEADS, QB, 2 * QB), F32)),
        scratch=[pltpu.VMEM((S, 128), MXU), pltpu.VMEM((S + QB, 128), MXU), pltpu.VMEM((S + QB, 128), MXU),
                 pltpu.VMEM((S, 128), MXU), pltpu.VMEM((S, 128), F32), pltpu.VMEM((S, 128), F32),
                 pltpu.VMEM((S, 128), F32), pltpu.VMEM((S, 128), F32),
                 pltpu.VMEM((S + QB, 128), F32), pltpu.VMEM((S + QB, 128), F32),
                 pltpu.VMEM((3, 2, 2, QB, 2 * QB), F32)],
        cp=_cp(48, ("arbitrary",)), args=(z, z, z, bias, y, lse, dy), ride=ride)


def relbias_grad(dbiases):
    bucket, band, _ = _att_static()
    onehot = (bucket[:, None] == np.arange(N_BUCKETS)[None, :, None, None]) & band[:, None]
    onehot = jnp.asarray(onehot.reshape(3, N_BUCKETS, QB * 2 * QB), BF16)

    def body(db0_ref, db1_ref, oh_ref, o_ref):
        acc = jnp.zeros((N_HEADS, N_BUCKETS), F32)
        for p in range(3):
            acc = acc + lax.dot_general(db0_ref[p] + db1_ref[p], oh_ref[p].astype(F32), (((1,), (1,)), ((), ())),
                                        preferred_element_type=F32, precision=lax.Precision.HIGHEST)
        o_ref[...] = acc

    vm = pl.BlockSpec(memory_space=pltpu.VMEM)
    out = _pallas_call(body, name="relbias_grad", in_specs=[vm, vm, vm], out_specs=vm,
                         out_shape=_sds((N_HEADS, N_BUCKETS), F32), compiler_params=_cp(40))(
        *[d.reshape(3, N_HEADS, QB * 2 * QB) for d in dbiases], onehot)
    return out.T


def _panel(t_ref, ri, j):
    return t_ref[ri, pl.ds(j, S, stride=8), :]


def _gelu(x):
    c = math.sqrt(2.0 / math.pi)
    th = jnp.tanh(c * (x + 0.044715 * x * x * x))
    return 0.5 * x * (1.0 + th), th


def ssm_fwd(z, a, bre, bim, cre, cim, dsk, gluw, glub, name, ride=None):
    def body(u_ref, a_ref, bre_ref, bim_ref, cre_ref, cim_ref, d_ref, gw_ref, gb_ref, y_ref, yp_ref, st_hbm, st_ref):
        u = u_ref[...]
        for j in range(8):
            st_ref[0, pl.ds(j, S, stride=8), :] = _mm(u, bre_ref[:, 128 * j:128 * (j + 1)])
            st_ref[1, pl.ds(j, S, stride=8), :] = _mm(u, bim_ref[:, 128 * j:128 * (j + 1)])
        ar, ai = a_ref[0], a_ref[1]

        def step(t, c):
            re, im = c
            i = pl.multiple_of(t * 8, 8)
            nre = ar * re - ai * im + st_ref[0, pl.ds(i, 8), :]
            nim = ar * im + ai * re + st_ref[1, pl.ds(i, 8), :]
            st_ref[0, pl.ds(i, 8), :] = nre
            st_ref[1, pl.ds(i, 8), :] = nim
            return nre, nim

        zero = jnp.zeros((8, 128), F32)
        lax.fori_loop(0, S, step, (zero, zero), unroll=8)
        y = d_ref[...] * u
        for j in range(8):
            y = y + _mm(_panel(st_ref, 0, j), cre_ref[128 * j:128 * (j + 1), :])
            y = y - _mm(_panel(st_ref, 1, j), cim_ref[128 * j:128 * (j + 1), :])
        pltpu.sync_copy(st_ref, st_hbm)
        yp_ref[...] = y
        gl, _ = _gelu(y)
        tt = _mm(gl, gw_ref[...].reshape(D_SSM, D_SSM)) + gb_ref[...]
        y_ref[...] = y * jax.nn.sigmoid(tt)

    vm = lambda shape: pl.BlockSpec(shape, lambda i: (0,) * len(shape))
    return _call(
        body, name=name, grid=(1,),
        in_specs=[pl.BlockSpec((S, 256), lambda i: (0, 6)), vm((2, 8, 128)), vm((256, 1024)), vm((256, 1024)),
                  vm((1024, 256)), vm((1024, 256)), vm((1, 256)),
                  vm((NDEV, 32, 256)), vm((1, 256))],
        out_specs=(vm((S, 256)), vm((S, 256)), pl.BlockSpec(memory_space=pl.ANY)),
        out_shape=(_sds((S, 256), F32), _sds((S, 256), F32), _sds((2, S * 8, 128), F32)),
        scratch=[pltpu.VMEM((2, S * 8, 128), F32)],
        cp=_cp(40, ("arbitrary",)), args=(z, a, bre, bim, cre, cim, dsk, gluw, glub), ride=ride)


def ssm_bwd(dy, z, ypre, st, a, bre, bim, cre, cim, dsk, gluw, glub, name, ride=None):
    def body(dy_ref, u_ref, yp_ref, st_hbm, a_ref, bre_ref, bim_ref, cre_ref, cim_ref, d_ref, gw_ref, gb_ref,
             du_ref, dbre_ref, dbim_ref, dcre_ref, dcim_ref, da_ref, dd_ref, dgw_ref, dgb_ref, g_ref, st_ref):
        pltpu.sync_copy(st_hbm, st_ref)
        u = u_ref[...]
        y = yp_ref[...]
        dout = dy_ref[...]
        gw = gw_ref[...].reshape(D_SSM, D_SSM)
        gl, th = _gelu(y)
        sig = jax.nn.sigmoid(_mm(gl, gw) + gb_ref[...])
        dt = dout * y * sig * (1.0 - sig)
        dgw_ref[...] = _mm_tn(gl, dt)
        dgb_ref[...] = jnp.sum(dt, axis=0, keepdims=True)
        c = math.sqrt(2.0 / math.pi)
        dgelu = 0.5 * (1.0 + th) + 0.5 * y * (1.0 - th * th) * c * (1.0 + 3.0 * 0.044715 * y * y)
        dyv = dout * sig + _mm_nt(dt, gw) * dgelu
        dd_ref[...] = jnp.sum(dyv * u, axis=0, keepdims=True)
        for j in range(8):
            rows = slice(128 * j, 128 * (j + 1))
            g_ref[0, pl.ds(j, S, stride=8), :] = _mm_nt(dyv, cre_ref[rows, :])
            g_ref[1, pl.ds(j, S, stride=8), :] = -_mm_nt(dyv, cim_ref[rows, :])
            dcre_ref[rows, :] = _mm_tn(_panel(st_ref, 0, j), dyv)
            dcim_ref[rows, :] = -_mm_tn(_panel(st_ref, 1, j), dyv)
        ar, ai = a_ref[0], a_ref[1]

        def step(k, c4):
            gre, gim, dar, dai = c4
            i = pl.multiple_of((S - 1 - k) * 8, 8)
            nre = g_ref[0, pl.ds(i, 8), :] + ar * gre + ai * gim
            nim = g_ref[1, pl.ds(i, 8), :] + ar * gim - ai * gre
            g_ref[0, pl.ds(i, 8), :] = nre
            g_ref[1, pl.ds(i, 8), :] = nim
            sre = st_ref[0, pl.ds(i - 8, 8), :]
            sim = st_ref[1, pl.ds(i - 8, 8), :]
            return nre, nim, dar + nre * sre + nim * sim, dai + nim * sre - nre * sim

        zero = jnp.zeros((8, 128), F32)
        gre, gim, dar, dai = lax.fori_loop(0, S - 1, step, (zero, zero, zero, zero), unroll=8)
        g_ref[0, 0:8, :] = g_ref[0, 0:8, :] + ar * gre + ai * gim
        g_ref[1, 0:8, :] = g_ref[1, 0:8, :] + ar * gim - ai * gre
        da_ref[0] = dar
        da_ref[1] = dai
        du = dyv * d_ref[...]
        for j in range(8):
            cols = slice(128 * j, 128 * (j + 1))
            gr, gi = _panel(g_ref, 0, j), _panel(g_ref, 1, j)
            dbre_ref[:, cols] = _mm_tn(u, gr)
            dbim_ref[:, cols] = _mm_tn(u, gi)
            du = du + _mm_nt(gr, bre_ref[:, cols]) + _mm_nt(gi, bim_ref[:, cols])
        du_ref[...] = du

    vm = lambda shape: pl.BlockSpec(shape, lambda i: (0,) * len(shape))
    return _call(
        body, name=name, grid=(1,),
        in_specs=[vm((S, 256)), pl.BlockSpec((S, 256), lambda i: (0, 6)), vm((S, 256)), pl.BlockSpec(memory_space=pl.ANY),
                  vm((2, 8, 128)), vm((256, 1024)), vm((256, 1024)), vm((1024, 256)), vm((1024, 256)), vm((1, 256)),
                  vm((NDEV, 32, 256)), vm((1, 256))],
        out_specs=(vm((S, 256)), vm((256, 1024)), vm((256, 1024)), vm((1024, 256)), vm((1024, 256)),
                   vm((2, 8, 128)), vm((1, 256)), vm((256, 256)), vm((1, 256))),
        out_shape=(_sds((S, 256), F32), _sds((256, 1024), F32), _sds((256, 1024), F32), _sds((1024, 256), F32),
                   _sds((1024, 256), F32), _sds((2, 8, 128), F32), _sds((1, 256), F32), _sds((256, 256), F32),
                   _sds((1, 256), F32)),
        scratch=[pltpu.VMEM((2, S * 8, 128), F32), pltpu.VMEM((2, S * 8, 128), F32)],
        cp=_cp(56, ("arbitrary",)), args=(dy, z, ypre, st, a, bre, bim, cre, cim, dsk, gluw, glub), ride=ride)


def _ssm_discretise(a_re, a_im, log_dt, b_re, b_im):
    dt = jnp.exp(log_dt)[:, None]
    er = jnp.exp(a_re * dt)
    abr, abi = er * jnp.cos(a_im * dt), er * jnp.sin(a_im * dt)
    den = a_re * a_re + a_im * a_im
    fr = ((abr - 1.0) * a_re + abi * a_im) / den
    fi = (abi * a_re - (abr - 1.0) * a_im) / den
    bbr = fr[:, :, None] * b_re - fi[:, :, None] * b_im
    bbi = fr[:, :, None] * b_im + fi[:, :, None] * b_re
    return abr, abi, bbr, bbi


def _blockdiag(t):
    g, r, c = t.shape
    eye = jnp.eye(g, dtype=t.dtype)
    return (t[:, :, None, :] * eye[:, None, :, None]).reshape(g * r, g * c)


def _blockdiag_take(m, r, c):
    g = m.shape[0] // r
    idx = jnp.arange(g)
    return m.reshape(g, r, g, c)[idx, :, idx, :]


PAD = 16


def _pool_lane_select(vals):
    lane = lax.broadcasted_iota(jnp.int32, vals[0].shape, 1)
    out = vals[3]
    for g in (2, 1, 0):
        out = jnp.where(lane < 64 * (g + 1), vals[g], out)
    return out


def _pool_counts():
    row = lax.broadcasted_iota(jnp.int32, (S, D_POOL), 0).astype(F32) + 1.0
    return _pool_lane_select([jnp.minimum(row, float(w)) for w in POOL_WINDOWS])


def _pooled(u, sa, sb):
    sums = []
    cur = u
    bufs = (sa, sb)
    for k, sh in enumerate((1, 2, 4, 8)):
        buf = bufs[k % 2]
        buf[PAD:PAD + S, :] = cur
        cur = cur + buf[PAD - sh:PAD - sh + S, :]
        sums.append(cur)
    return _pool_lane_select(sums) / _pool_counts() - u


def pool_fwd(z, pw, psc, name):
    def body(u_ref, w_ref, s_ref, y_ref, sa, sb):
        for buf in (sa, sb):
            buf[0:PAD, :] = jnp.zeros((PAD, D_POOL), F32)
        pooled = _pooled(u_ref[...], sa, sb)
        y_ref[...] = _mm(pooled, w_ref[...]) * s_ref[...]

    vm = lambda shape: pl.BlockSpec(shape, lambda i: (0,) * len(shape))
    return _pallas_call(
        body, name=name, grid=(1,),
        in_specs=[pl.BlockSpec((S, 256), lambda i: (0, 7)), vm((256, 256)), vm((1, 256))],
        out_specs=vm((S, 256)), out_shape=_sds((S, 256), F32),
        scratch_shapes=[pltpu.VMEM((S + 2 * PAD, D_POOL), F32)] * 2,
        compiler_params=_cp(40, ("arbitrary",)))(z, pw, psc)


def pool_bwd(dy, z, pw, psc, name):
    def body(dy_ref, u_ref, w_ref, s_ref, du_ref, dw_ref, ds_ref, sa, sb):
        for buf in (sa, sb):
            buf[0:PAD, :] = jnp.zeros((PAD, D_POOL), F32)
            buf[PAD + S:PAD + S + PAD, :] = jnp.zeros((PAD, D_POOL), F32)
        pooled = _pooled(u_ref[...], sa, sb)
        dyv = dy_ref[...]
        w = w_ref[...]
        ds_ref[...] = jnp.sum(dyv * _mm(pooled, w), axis=0, keepdims=True)
        dyl = dyv * s_ref[...]
        dw_ref[...] = _mm_tn(pooled, dyl)
        dpool = _mm_nt(dyl, w)
        cur = dpool / _pool_counts()
        sums = []
        bufs = (sa, sb)
        for k, sh in enumerate((1, 2, 4, 8)):
            buf = bufs[k % 2]
            buf[PAD:PAD + S, :] = cur
            cur = cur + buf[PAD + sh:PAD + sh + S, :]
            sums.append(cur)
        du_ref[...] = _pool_lane_select(sums) - dpool

    vm = lambda shape: pl.BlockSpec(shape, lambda i: (0,) * len(shape))
    return _pallas_call(
        body, name=name, grid=(1,),
        in_specs=[vm((S, 256)), pl.BlockSpec((S, 256), lambda i: (0, 7)), vm((256, 256)), vm((1, 256))],
        out_specs=(vm((S, 256)), vm((256, 256)), vm((1, 256))),
        out_shape=(_sds((S, 256), F32), _sds((256, 256), F32), _sds((1, 256), F32)),
        scratch_shapes=[pltpu.VMEM((S + 2 * PAD, D_POOL), F32)] * 2,
        compiler_params=_cp(40, ("arbitrary",)))(dy, z, pw, psc)


def ada_fwd(c_all, ada_w, ada_b_cols):
    def body(c_ref, w_ref, b_ref, o_ref):
        c = c_ref[...]
        cond = c * jax.nn.sigmoid(c)
        o_ref[...] = jnp.dot(cond, w_ref[...], preferred_element_type=F32, precision=lax.Precision.HIGHEST) + b_ref[...]

    return _pallas_call(
        body, name="ada_fwd", grid=(DEPTH,),
        in_specs=[pl.BlockSpec((NDEV, D), lambda l: (0, 0)), pl.BlockSpec((None, D, 1152), lambda l: (l, 0, 0)),
                  pl.BlockSpec((None, 1, 1152), lambda l: (l, 0, 0))],
        out_specs=pl.BlockSpec((None, NDEV, 1152), lambda l: (l, 0, 0)), out_shape=_sds((DEPTH, NDEV, 1152), F32),
        compiler_params=_cp(40, ("arbitrary",)))(c_all, ada_w, ada_b_cols)


def ada_bwd(c_all, dmod_cols):
    def body(c_ref, dm_ref, o_ref):
        c = c_ref[...]
        cond = c * jax.nn.sigmoid(c)
        o_ref[...] = lax.dot_general(cond, dm_ref[...], (((0,), (0,)), ((), ())), preferred_element_type=F32,
                                     precision=lax.Precision.HIGHEST)

    return _pallas_call(
        body, name="ada_bwd", grid=(DEPTH,),
        in_specs=[pl.BlockSpec((NDEV, D), lambda l: (0, 0)), pl.BlockSpec((None, NDEV, 1152), lambda l: (l, 0, 0))],
        out_specs=pl.BlockSpec((None, D, 1152), lambda l: (l, 0, 0)), out_shape=_sds((DEPTH, D, 1152), F32),
        compiler_params=_cp(40, ("arbitrary",)))(c_all, dmod_cols)


def _adamw(w, g, m, v):
    m2 = B1 * m + (1.0 - B1) * g
    v2 = B2 * v + (1.0 - B2) * (g * g)
    m_hat = m2 / (1.0 - B1 ** STEP)
    v_hat = v2 / (1.0 - B2 ** STEP)
    return -LR * (m_hat / (jnp.sqrt(v_hat) + EPS) + WD * w), m2, v2


def _sum8(ref):
    g = ref[0].astype(F32)
    for s in range(1, ref.shape[0]):
        g = g + ref[s].astype(F32)
    return g


def adam_rs(recv, w, m, v, tr, name, ride=None):
    lead, (r, cdim) = w.shape[:-2], w.shape[-2:]
    cp = recv.shape[-1]
    nl = len(lead)

    def body(rc_ref, w_ref, m_ref, v_ref, g_ref, d_ref, m2_ref, v2_ref):
        g = _sum8(rc_ref)[:, :cdim]
        g_ref[...] = g
        d_ref[...], m2_ref[...], v2_ref[...] = _adamw(w_ref[...], g, m_ref[...], v_ref[...])

    rs = pl.BlockSpec((None,) * nl + (tr, cdim), lambda *i: (*i, 0))
    return _call(
        body, name=name, grid=lead + (r // tr,),
        in_specs=[pl.BlockSpec((NDEV,) + (None,) * nl + (tr, cp), lambda *i: (0, *i, 0)), rs, rs, rs],
        out_specs=(rs, rs, rs, rs), out_shape=tuple(_sds(w.shape, F32) for _ in range(4)),
        cp=_cp(48, ("arbitrary",) * (nl + 1)), args=(recv, w, m, v), ride=ride)


def adam_block(recv, w, m, v, lf, prev, name):
    half = FB // 2

    def body(*refs):
        rc_ref, w_ref, m_ref, v_ref = refs[:4]
        g_ref, d_ref, m2_ref, v2_ref = refs[-4:]
        g = _sum8(rc_ref)
        g_ref[...] = g
        d_ref[...], m2_ref[...], v2_ref[...] = _adamw(w_ref[...], g, m_ref[...], v_ref[...])

    rs = pl.BlockSpec((None, None, half, D), lambda i: (lf // 2, lf % 2, i, 0))
    prev = list(prev) if prev is not None else []
    return list(_pallas_call(
        body, name=name, grid=(2,), in_specs=[pl.BlockSpec((recv.shape[0], half, D), lambda i: (0, i, 0)), rs, rs, rs] + [ANY] * len(prev),
        out_specs=(rs, rs, rs, rs), out_shape=tuple(_sds((DEPTH, 2, FB, D), F32) for _ in range(4)),
        input_output_aliases={4 + k: k for k in range(len(prev))},
        compiler_params=_cp(48, ("arbitrary",)))(recv, w, m, v, *prev))


def adam_plain(g, w, m, v, tr, name, ride=None):
    lead, (r, cdim) = w.shape[:-2], w.shape[-2:]
    nl = len(lead)

    def body(g_ref, w_ref, m_ref, v_ref, d_ref, m2_ref, v2_ref):
        d_ref[...], m2_ref[...], v2_ref[...] = _adamw(w_ref[...], g_ref[...], m_ref[...], v_ref[...])

    rs = pl.BlockSpec((None,) * nl + (tr, cdim), lambda *i: (*i, 0))
    return _call(
        body, name=name, grid=lead + (r // tr,), in_specs=[rs, rs, rs, rs], out_specs=(rs, rs, rs),
        out_shape=tuple(_sds(w.shape, F32) for _ in range(3)),
        cp=_cp(48, ("arbitrary",) * (nl + 1)), args=(g, w, m, v), ride=ride)


def adam_native(gs, ws, ms, vs, name):
    n = len(ws)

    def body(*refs):
        g_refs, w_refs, m_refs, v_refs = (refs[k * n:(k + 1) * n] for k in range(4))
        d_refs, m2_refs, v2_refs = (refs[(4 + k) * n:(5 + k) * n] for k in range(3))
        for a in range(n):
            d_refs[a][...], m2_refs[a][...], v2_refs[a][...] = _adamw(w_refs[a][...], g_refs[a][...], m_refs[a][...], v_refs[a][...])

    vm = pl.BlockSpec(memory_space=pltpu.VMEM)
    outs = _pallas_call(body, name=name, in_specs=[vm] * (4 * n), out_specs=tuple([vm] * (3 * n)),
                        out_shape=tuple(_sds(w.shape, F32) for w in ws) * 3, compiler_params=_cp(40))(*gs, *ws, *ms, *vs)
    return outs[:n], outs[n:2 * n], outs[2 * n:]


def sum_sources(recv, name):
    r = recv.shape[1]

    def body(rc_ref, o_ref):
        o_ref[...] = _sum8(rc_ref)

    vm = pl.BlockSpec(memory_space=pltpu.VMEM)
    return _pallas_call(body, name=name, in_specs=[vm], out_specs=vm, out_shape=_sds((r, 128), F32),
                          compiler_params=_cp(40))(recv)


def _pack(arrs):
    flat = jnp.concatenate([a.reshape(-1) for a in arrs])
    n = flat.shape[0]
    rows = -(-n // 1024) * 8
    return jnp.pad(flat, (0, rows * 128 - n)).reshape(rows, 128)


def _unpack(vec, shapes):
    flat = vec.reshape(-1)
    out, o = [], 0
    for sh in shapes:
        n = int(np.prod(sh))
        out.append(flat[o:o + n].reshape(sh))
        o += n
    return out


WEIGHTS = ['rel_bias', 'ada_w', 'ada_b', 'ln_g', 'ln_b', 'ffn_w_gate', 'ffn_w_up', 'ffn_w_down', 'w_in', 'w_out',
           'ssm_a_re', 'ssm_a_im', 'ssm_log_dt', 'ssm_b_re', 'ssm_b_im', 'ssm_c_re', 'ssm_c_im', 'ssm_d', 'glu_w',
           'glu_b', 'pool_w', 'pool_scale']
SMALL = ['rel_bias', 'ada_b', 'ln_g', 'ln_b', 'ssm_a_re', 'ssm_a_im', 'ssm_log_dt', 'ssm_b_re', 'ssm_b_im',
         'ssm_c_re', 'ssm_c_im', 'ssm_d', 'glu_b', 'pool_w', 'pool_scale']
SMALL_FULL_SHAPES = {'rel_bias': (32, 8), 'ada_b': (2, 9216), 'ln_g': (2, 3, 1024), 'ln_b': (2, 3, 1024),
                     'ssm_a_re': (2, 16, 64), 'ssm_a_im': (2, 16, 64), 'ssm_log_dt': (2, 16),
                     'ssm_b_re': (2, 16, 64, 16), 'ssm_b_im': (2, 16, 64, 16), 'ssm_c_re': (2, 16, 16, 64),
                     'ssm_c_im': (2, 16, 16, 64), 'ssm_d': (2, 256), 'glu_b': (2, 256), 'pool_w': (2, 4, 64, 64),
                     'pool_scale': (2, 256)}


def _step(P):
    me = _me()
    x0 = P['x'][0]
    target = P['loss_target'][0]

    bf = lambda a: a.astype(BF16)
    padr = lambda a: jnp.pad(bf(a), ((0, 0), (0, 0), (0, FBP - FB), (0, 0)))
    ffn_b = [padr(jnp.swapaxes(P['ffn_w_gate'], 2, 3)), padr(jnp.swapaxes(P['ffn_w_up'], 2, 3)), padr(P['ffn_w_down'])]
    mix_b = [bf(P['w_in']), bf(P['w_out']), bf(P['glu_w'])]

    def shards(l, sub):
        return [t[l] for t in mix_b] if sub == 1 else [t[l, sub // 2] for t in ffn_b]

    order = [(l, sub) for l in range(DEPTH) for sub in range(3)]
    nxt = dict(zip(order[:-1], order[1:]))
    W = {key: [None] * 3 for key in order}
    c_all, lng_all, lnb_all, *W[order[0]] = _exchange(Gather([P['c'], P['ln_g'], P['ln_b']] + shards(*order[0])), "gather_first")
    gather_queue = [(key, pos, a) for key in order[1:] for pos, a in enumerate(shards(*key))]

    def gather_ride(cap_us, must=None):
        units, used = [], 0.0
        while gather_queue:
            key, _, a = gather_queue[0]
            cost = a.size * a.dtype.itemsize * GATHER_US_PER_BYTE
            if key != must and used + cost / 2 > cap_us:
                break
            units.append(gather_queue.pop(0))
            used += cost
        return (Gather([a for _, _, a in units]) if units else None), units

    def gathered(units, outs):
        for (key, pos, _), o in zip(units, outs):
            W[key][pos] = o

    c_all = c_all.reshape(NDEV, D)
    ln_g = jnp.transpose(lng_all, (1, 2, 0, 3)).reshape(DEPTH, 3, D)
    ln_b = jnp.transpose(lnb_all, (1, 2, 0, 3)).reshape(DEPTH, 3, D)

    ada_b_cols = lax.dynamic_slice_in_dim(P['ada_b'], me * 1152, 1152, axis=1).reshape(DEPTH, 1, 1152)
    modc = ada_fwd(c_all, P['ada_w'], ada_b_cols)
    (mod_all,) = _exchange(Gather([modc]), "gather_mod")
    mod_me = lax.dynamic_index_in_dim(mod_all, me, axis=2, keepdims=False)
    mod = jnp.transpose(mod_me, (1, 0, 2)).reshape(DEPTH, 9, D)

    bias = att_bias(P['rel_bias'])
    ssm = []
    for l in range(DEPTH):
        prm = (P['ssm_a_re'][l], P['ssm_a_im'][l], P['ssm_log_dt'][l], P['ssm_b_re'][l], P['ssm_b_im'][l])
        (abr, abi, bbr, bbi), disc_vjp = jax.vjp(_ssm_discretise, *prm)
        ssm.append(dict(
            vjp=disc_vjp, a=jnp.stack([abr.reshape(8, 128), abi.reshape(8, 128)]),
            bre=_blockdiag(jnp.transpose(bbr, (0, 2, 1))).astype(MXU), bim=_blockdiag(jnp.transpose(bbi, (0, 2, 1))).astype(MXU),
            cre=_blockdiag(jnp.transpose(P['ssm_c_re'][l], (0, 2, 1))).astype(MXU),
            cim=_blockdiag(jnp.transpose(P['ssm_c_im'][l], (0, 2, 1))).astype(MXU),
            d=P['ssm_d'][l].reshape(1, 256), gb=P['glu_b'][l].reshape(1, 256),
            pw=_blockdiag(P['pool_w'][l]).astype(MXU), psc=P['pool_scale'][l].reshape(1, 256)))

    saved = []
    x = x0
    h = ln_mod_fwd(x, mod[0], 0, "ln_mod_fwd_l0s0")
    for l, sub in order:
        tag = f"l{l}s{sub}"
        after = (mod[nxt[(l, sub)][0]], nxt[(l, sub)][1]) if (l, sub) in nxt else None
        if sub != 1:
            wg, wu, wd = (t.reshape(NDEV * FBP, D) for t in W[(l, sub)])
            ride, units = gather_ride(60, nxt.get((l, sub)))
            (G, U, fo), got = ffn_fwd(h, wg, wu, wd, "ffn_fwd_" + tag, ride)
            gathered(units, got)
            saved.append(dict(x=x, h=h, G=G, U=U, f=fo))
            x, *hn = res_ln_fwd(x, fo, mod[l], sub, ln_g[l], ln_b[l], 0.5, "res_ln_fwd_" + tag, after)
        else:
            sp = ssm[l]
            win, wout, gluw = W[(l, sub)]
            ride, units = gather_ride(15)
            (z,), got = win_fwd(h, win, "win_fwd_" + tag, ride)
            gathered(units, got)
            ride, units = gather_ride(55)
            (ya, lse), got = att_fwd(z, bias, "att_fwd_" + tag, ride)
            gathered(units, got)
            ride, units = gather_ride(35)
            (ys, ypre, st), got = ssm_fwd(z, sp['a'], sp['bre'], sp['bim'], sp['cre'], sp['cim'], sp['d'], gluw, sp['gb'],
                                          "ssm_fwd_" + tag, ride)
            gathered(units, got)
            yp = pool_fwd(z, sp['pw'], sp['psc'], "pool_fwd_" + tag)
            ride, units = gather_ride(12, nxt.get((l, sub)))
            (o,), got = wout_fwd(ya, ys, yp, wout, "wout_fwd_" + tag, ride)
            gathered(units, got)
            saved.append(dict(x=x, h=h, z=z, ya=ya, lse=lse, ys=ys, ypre=ypre, st=st, yp=yp, f=o))
            x, *hn = res_ln_fwd(x, o, mod[l], sub, ln_g[l], ln_b[l], 1.0, "res_ln_fwd_" + tag, after)
        h = hn[0] if hn else None
    assert not gather_queue

    loss_tile, dx = loss_fwd_bwd(x, target, "loss")
    loss = lax.psum(loss_tile[0, 0], ("x", "y", "c"))

    flights = []

    def hosted(cap_us, fn, *args):
        return fn(*args, None)[0]

    dmod = [[None] * 9 for _ in range(DEPTH)]
    dlng = [[None] * 3 for _ in range(DEPTH)]
    dlnb = [[None] * 3 for _ in range(DEPTH)]
    dbiases = [None] * DEPTH
    small_l = [dict() for _ in range(DEPTH)]
    for l, sub in reversed(order):
        tag = f"l{l}s{sub}"
        sv = saved[3 * l + sub]
        if (l, sub) == order[-1]:
            dxa, df, sums = hosted(0, res_ln_bwd, sv['x'], sv['f'], mod[l], sub, ln_g[l], dx, 0.5, "res_ln_bwd_" + tag)
        dlng[l][sub], dlnb[l][sub], dmod[l][3 * sub + 2] = sums[0], sums[1], sums[2]
        if sub != 1:
            f = sub // 2
            wg, wu, wd = (t.reshape(NDEV * FBP, D) for t in W[(l, sub)])
            dwg, dwu, dwd, dh = hosted(105, ffn_bwd, df, sv['h'], sv['G'], sv['U'], wg, wu, wd, "ffn_bwd_" + tag)
            payload = [t.reshape(NDEV, FBP, D) for t in (dwg, dwu, dwd)]
            if (l, sub) == order[0]:
                swap, zero = scatter_start(payload, "swap_start_" + tag, routes=routes_sibling, slots=NDEV // 2)
            else:
                handle, zero = scatter_start(payload, "scatter_start_" + tag)
                flights.append(((l, sub), handle))
        else:
            sp = ssm[l]
            win, wout, gluw = W[(l, sub)]
            dya, dys, dyp, dwout = hosted(0, wout_bwd, df, sv['ya'], sv['ys'], sv['yp'], wout, "wout_bwd_" + tag)
            dq, dk, dv, dbiases[l] = hosted(115, att_bwd, sv['z'], bias, sv['ya'], sv['lse'], dya, "att_bwd_" + tag)
            dus, dbre, dbim, dcre, dcim, da, dd, dgw, dgb = hosted(
                52, ssm_bwd, dys, sv['z'], sv['ypre'], sv['st'], sp['a'], sp['bre'], sp['bim'], sp['cre'], sp['cim'], sp['d'],
                gluw, sp['gb'], "ssm_bwd_" + tag)
            dup, dpw, dpsc = pool_bwd(dyp, sv['z'], sp['pw'], sp['psc'], "pool_bwd_" + tag)
            dh, dwin = hosted(40, win_bwd, (dq, dk, dv, dus, dup), sv['h'], win, "win_bwd_" + tag)
            handle, zero = scatter_start([dwin, dwout, dgw.astype(BF16).reshape(NDEV, 32, 256)], "scatter_start_" + tag)
            flights.append(((l, sub), handle))
            d_are, d_aim, d_ldt, d_bre, d_bim = sp['vjp']((
                da[0].reshape(16, 64), da[1].reshape(16, 64),
                jnp.transpose(_blockdiag_take(dbre, 16, 64), (0, 2, 1)), jnp.transpose(_blockdiag_take(dbim, 16, 64), (0, 2, 1))))
            small_l[l] = dict(
                ssm_a_re=d_are, ssm_a_im=d_aim, ssm_log_dt=d_ldt, ssm_b_re=d_bre, ssm_b_im=d_bim,
                ssm_c_re=jnp.transpose(_blockdiag_take(dcre, 64, 16), (0, 2, 1)),
                ssm_c_im=jnp.transpose(_blockdiag_take(dcim, 64, 16), (0, 2, 1)),
                ssm_d=dd.reshape(256), glu_b=dgb.reshape(256), pool_w=_blockdiag_take(dpw, 64, 64), pool_scale=dpsc.reshape(256))
        if (l, sub) == order[0]:
            dx, sums2 = hosted(0, ln_mod_bwd, sv['x'], dh, mod[l] + zero, sub, dxa, "ln_mod_bwd_" + tag)
        else:
            lp, sp_ = order[order.index((l, sub)) - 1]
            svp = saved[3 * lp + sp_]
            dxa, df, sums, sums2 = ln_join_bwd(svp['x'], svp['f'], mod[lp], sp_, ln_g[lp], ln_b[lp], 1.0 if sp_ == 1 else 0.5,
                                               dh, mod[l] + zero, sub, dxa, "ln_join_bwd_" + tag)
        dmod[l][3 * sub], dmod[l][3 * sub + 1] = sums2[0], sums2[1]
    grad_x = dx[None]

    theirs, mine = scatter_wait(swap, dx, "swap_wait_l%ds%d" % order[0])
    handle, zero = scatter_start(pair_sum(mine, theirs, "pair_sum_l%ds%d" % order[0]), "scatter_start_l%ds%d" % order[0],
                                 routes=routes_chips, slots=NDEV // 2)
    flights.append((order[0], handle))

    small = {k: jnp.stack([small_l[l][k] for l in range(DEPTH)]) for k in small_l[0]}
    small['rel_bias'] = relbias_grad(dbiases)
    small['ada_b'] = jnp.stack([jnp.stack(dmod[l]).reshape(9 * D) for l in range(DEPTH)])
    small['ln_g'] = jnp.stack([jnp.stack(dlng[l]) for l in range(DEPTH)])
    small['ln_b'] = jnp.stack([jnp.stack(dlnb[l]) for l in range(DEPTH)])
    (small_all,) = _exchange(Gather([_pack([small[k] for k in SMALL]) + zero]), "gather_small")

    out = {}

    def put(name, g, d, m2, v2, shape):
        out['grad_' + name], out['delta_' + name] = g.reshape(shape), d.reshape(shape)
        out['new_m_' + name], out['new_v_' + name] = m2.reshape(shape), v2.reshape(shape)

    def wmv(name):
        return [P[pre + name] for pre in ('', 'm_', 'v_')]

    recv = {}
    started_last = flights[-1][1][1][0]
    for key, handle in flights[:-1]:
        recv[key], _ = scatter_wait(handle, started_last, "scatter_wait_l%ds%d" % key)
    for pos, (name, tr) in enumerate((('w_in', 512), ('w_out', 128), ('glu_w', 32))):
        both = jnp.stack([recv[(l, 1)][pos] for l in range(DEPTH)], axis=1)
        put(name, *adam_rs(both, *wmv(name), tr, "adam_" + name)[0], P[name].shape)
    ffn = (('ffn_w_gate', [jnp.swapaxes(t, 2, 3) for t in wmv('ffn_w_gate')]),
           ('ffn_w_up', [jnp.swapaxes(t, 2, 3) for t in wmv('ffn_w_up')]), ('ffn_w_down', wmv('ffn_w_down')))
    part = [None] * 3
    for l, sub in [key for key, _ in flights[:-1] if key[1] != 1]:
        for pos, (name, ops) in enumerate(ffn):
            part[pos] = adam_block(recv[(l, sub)][pos], *ops, 2 * l + sub // 2, part[pos], f"adam_{name}_l{l}s{sub}")
    (l, sub), handle = flights[-1]
    last, _ = scatter_wait(handle, part[2][0], "scatter_wait_l%ds%d" % (l, sub))
    for pos, (name, ops) in enumerate(ffn):
        res = adam_block(last[pos], *ops, 2 * l + sub // 2, part[pos], f"adam_{name}_l{l}s{sub}")
        put(name, *([jnp.swapaxes(t, 2, 3) for t in res] if pos < 2 else res), P[name].shape)

    gsum = dict(zip(SMALL, _unpack(sum_sources(small_all, "sum_small"), [SMALL_FULL_SHAPES[k] for k in SMALL])))
    off = 256
    dmod_all = small_all.reshape(NDEV, -1)[:, off:off + DEPTH * 9 * D].reshape(NDEV, DEPTH, 9 * D)
    dmod_cols = jnp.transpose(lax.dynamic_slice_in_dim(dmod_all, me * 1152, 1152, axis=2), (1, 0, 2))
    g_ada_w = ada_bwd(c_all, dmod_cols)

    put('ada_w', g_ada_w, *adam_plain(g_ada_w, *wmv('ada_w'), 256, "adam_ada_w")[0], P['ada_w'].shape)

    for k in ('ln_g', 'ln_b'):
        gsum[k] = lax.dynamic_slice_in_dim(gsum[k], me * 128, 128, axis=2)
    swaps = {'rel_bias': (0, 1), 'ln_g': (0, 1), 'ln_b': (0, 1), 'ssm_b_re': (2, 3), 'ssm_b_im': (2, 3)}
    view = lambda k, t: jnp.swapaxes(t, *swaps[k]) if k in swaps else t
    ds_, m2s, v2s = adam_native(*[[view(k, src(k)) for k in SMALL] for src in
                                  (lambda k: gsum[k], lambda k: P[k], lambda k: P['m_' + k], lambda k: P['v_' + k])],
                                "adam_small")
    for k, d, m2, v2 in zip(SMALL, ds_, m2s, v2s):
        put(k, gsum[k], view(k, d), view(k, m2), view(k, v2), P[k].shape)

    res = [loss, grad_x]
    for pre in ('grad_', 'delta_', 'new_m_', 'new_v_'):
        res += [out[pre + k] for k in WEIGHTS]
    return tuple(res)


def kernel(x, c, rel_bias, ada_w, ada_b, ln_g, ln_b, ffn_w_gate, ffn_w_up, ffn_w_down, w_in, w_out, ssm_a_re, ssm_a_im, ssm_log_dt, ssm_b_re, ssm_b_im, ssm_c_re, ssm_c_im, ssm_d, glu_w, glu_b, pool_w, pool_scale, loss_target, m_rel_bias, m_ada_w, m_ada_b, m_ln_g, m_ln_b, m_ffn_w_gate, m_ffn_w_up, m_ffn_w_down, m_w_in, m_w_out, m_ssm_a_re, m_ssm_a_im, m_ssm_log_dt, m_ssm_b_re, m_ssm_b_im, m_ssm_c_re, m_ssm_c_im, m_ssm_d, m_glu_w, m_glu_b, m_pool_w, m_pool_scale, v_rel_bias, v_ada_w, v_ada_b, v_ln_g, v_ln_b, v_ffn_w_gate, v_ffn_w_up, v_ffn_w_down, v_w_in, v_w_out, v_ssm_a_re, v_ssm_a_im, v_ssm_log_dt, v_ssm_b_re, v_ssm_b_im, v_ssm_c_re, v_ssm_c_im, v_ssm_d, v_glu_w, v_glu_b, v_pool_w, v_pool_scale):
    return _step(dict(locals()))
```

```python
import functools
import math

import numpy as np
import jax
import jax.numpy as jnp
from jax import lax
from jax.experimental import pallas as pl
from jax.experimental.pallas import tpu as pltpu

F32 = jnp.float32
BF16 = jnp.bfloat16
MXU = jnp.bfloat16

S = 2048
D = 1024
NDEV = 8
DEPTH = 2
D_ATT, D_SSM, D_POOL, D_IN = 512, 256, 256, 2048
N_HEADS = 8
FB = 352
FBP = 384
QB = 128
PATTERNS = ((128, 1), (512, 4), (2048, 16))
POOL_WINDOWS = (2, 4, 8, 16)
N_BUCKETS, MAX_DISTANCE = 32, 2048
ALPHA = (2 * DEPTH) ** 0.25
LN_EPS = 1e-5
NEG = -1e30
GATHER_US_PER_BYTE = 43e-6
LR, B1, B2, EPS, WD, STEP = 0.001, 0.9, 0.999, 1e-08, 0.01, 10

TM = 256
TMM = 512
MIB = 1024 * 1024


def _cp(vmem_mib, sem=None):
    kw = dict(vmem_limit_bytes=vmem_mib * MIB)
    if sem is not None:
        kw["dimension_semantics"] = sem
    return pltpu.CompilerParams(**kw)


def _sds(shape, dtype):
    return jax.ShapeDtypeStruct(shape, dtype)


def _mm(a, b):
    return jnp.dot(a.astype(MXU), b.astype(MXU), preferred_element_type=F32)


def _mm_nt(a, b):
    return lax.dot_general(a.astype(MXU), b.astype(MXU), (((1,), (1,)), ((), ())), preferred_element_type=F32)


def _mm_tn(a, b):
    return lax.dot_general(a.astype(MXU), b.astype(MXU), (((0,), (0,)), ((), ())), preferred_element_type=F32)


def _ln_stats(x):
    mu = jnp.mean(x, axis=-1, keepdims=True)
    xc = x - mu
    var = jnp.mean(xc * xc, axis=-1, keepdims=True)
    rstd = lax.rsqrt(var + LN_EPS)
    return xc * rstd, rstd


def _ln_bwd(dn, n, rstd):
    return rstd * (dn - jnp.mean(dn, axis=-1, keepdims=True) - n * jnp.mean(dn * n, axis=-1, keepdims=True))


def _me():
    return 4 * lax.axis_index("x") + 2 * lax.axis_index("y") + lax.axis_index("c")


ANY = pl.BlockSpec(memory_space=pl.ANY)
PIN_BYTES = 1 << 19


def _pallas_call(*a, **k):
    big = lambda o: math.prod(o.shape) * o.dtype.itemsize >= PIN_BYTES
    pin = lambda o: pltpu.HBM(o.shape, o.dtype) if isinstance(o, jax.ShapeDtypeStruct) and big(o) else o
    osh = k["out_shape"]
    k["out_shape"] = tuple(pin(o) for o in osh) if isinstance(osh, (tuple, list)) else pin(osh)
    fn = pl.pallas_call(*a, **k)

    def run(*args):
        return fn(*[pltpu.with_memory_space_constraint(x, pltpu.HBM) if big(x) else x for x in args])
    return run


class Gather:
    def __init__(self, srcs):
        self.srcs = list(srcs)
        self.n = len(self.srcs)
        self.bufs = []
        self.out_shapes = [_sds((NDEV,) + a.shape, a.dtype) for a in self.srcs]
        self.sems = [pltpu.SemaphoreType.DMA((7 * self.n,)), pltpu.SemaphoreType.DMA((7 * self.n,)),
                     pltpu.SemaphoreType.DMA((self.n,))]

    def _parts(self, srcs, outs, sems):
        send_sems, recv_sems, loc_sems = sems
        x, y, c = lax.axis_index("x"), lax.axis_index("y"), lax.axis_index("c")
        me, sib = (x, y, c), (x, y, 1 - c)
        chips = [(1 - x, y), (x, 1 - y), (1 - x, 1 - y)]
        slot = lambda d: 4 * d[0] + 2 * d[1] + d[2]

        def copy(a, k, block, to, src=None):
            dst = outs[a].at[slot(block)]
            return pltpu.make_async_remote_copy(
                src_ref=dst if src is None else src, dst_ref=dst,
                send_sem=send_sems.at[7 * a + k], recv_sem=recv_sems.at[7 * a + k],
                device_id=to, device_id_type=pl.DeviceIdType.MESH)

        local = [pltpu.make_async_copy(srcs[a], outs[a].at[slot(me)], loc_sems.at[a]) for a in range(self.n)]
        return me, sib, chips, c, copy, local

    def start(self, srcs, bufs, outs, sems):
        me, sib, chips, c, copy, local = self._parts(srcs, outs, sems)
        for a in range(self.n):
            local[a].start()
            copy(a, 0, me, sib, src=srcs[a]).start()
            for j, chip in enumerate(chips):
                copy(a, 1 + j, me, (*chip, c), src=srcs[a]).start()

    def finish(self, srcs, bufs, outs, sems):
        me, sib, chips, c, copy, local = self._parts(srcs, outs, sems)
        for a in range(self.n):
            for j, chip in enumerate(chips):
                copy(a, 1 + j, (*chip, c), me).wait_recv()
                copy(a, 4 + j, (*chip, c), sib).start()
        for a in range(self.n):
            copy(a, 0, sib, me).wait_recv()
            copy(a, 0, me, sib, src=srcs[a]).wait_send()
            for j, chip in enumerate(chips):
                copy(a, 4 + j, (*chip, 1 - c), me).wait_recv()
                copy(a, 1 + j, me, (*chip, c), src=srcs[a]).wait_send()
                copy(a, 4 + j, (*chip, c), sib).wait_send()
            local[a].wait()


def _call(body, *, name, grid, in_specs, out_specs, out_shape, args, scratch=(), cp=None, ride=None):
    out_specs, out_shape, scratch = list(out_specs), list(out_shape), list(scratch)
    if ride is None:
        outs = _pallas_call(body, name=name, grid=grid, in_specs=list(in_specs), out_specs=tuple(out_specs),
                              out_shape=tuple(out_shape), scratch_shapes=scratch, compiler_params=cp)(*args)
        return list(outs), []
    nin, nout, nscr, n, nb, no = len(in_specs), len(out_specs), len(scratch), ride.n, len(ride.bufs), len(ride.out_shapes)
    steps = list(grid)

    def wrapped(*refs):
        h_in, r_src, r_buf = refs[:nin], refs[nin:nin + n], refs[nin + n:nin + n + nb]
        o0 = nin + n + nb
        h_out, r_out = refs[o0:o0 + nout], refs[o0 + nout:o0 + nout + no]
        s0 = o0 + nout + no
        h_scr, sems = refs[s0:s0 + nscr], refs[s0 + nscr:]
        ids = [pl.program_id(a) for a in range(len(steps))]
        first = functools.reduce(jnp.logical_and, [i == 0 for i in ids])
        last = functools.reduce(jnp.logical_and, [i == s - 1 for i, s in zip(ids, steps)])

        @pl.when(first)
        def _():
            ride.start(r_src, r_buf, r_out, sems)

        body(*h_in, *h_out, *h_scr)

        @pl.when(last)
        def _():
            ride.finish(r_src, r_buf, r_out, sems)

    aliases = {nin + n + k: nout + k for k in range(nb)}
    outs = _pallas_call(
        wrapped, name=name, grid=grid, in_specs=list(in_specs) + [ANY] * (n + nb),
        out_specs=tuple(out_specs + [ANY] * no), out_shape=tuple(out_shape + ride.out_shapes),
        scratch_shapes=scratch + ride.sems, input_output_aliases=aliases, compiler_params=cp,
    )(*args, *ride.srcs, *ride.bufs)
    return list(outs[:nout]), list(outs[nout:])


def _exchange(ride, name):
    def body(dummy_ref, o_ref):
        o_ref[...] = dummy_ref[...]

    one = pl.BlockSpec((8, 128), lambda i: (0, 0))
    _, outs = _call(body, name=name, grid=(1,), in_specs=[one], out_specs=[one], out_shape=[_sds((8, 128), F32)],
                    args=(jnp.zeros((8, 128), F32),), ride=ride)
    return outs


HBM = pl.BlockSpec(memory_space=pltpu.HBM)
SEM = pl.BlockSpec(memory_space=pltpu.SEMAPHORE)


def routes_all(me):
    return [(k, me ^ k, me, me ^ k) for k in range(NDEV)]


def _scatter_copies(srcs, lands, sems, sending):
    send_sems, recv_sems, loc_sems = sems
    me = _me()
    rts = routes_all(me)
    remote_ix = [r for r, (k, _, _, _) in enumerate(rts) if k != 0]
    local_ix = [r for r, (k, _, _, _) in enumerate(rts) if k == 0]
    remote, local = [], []
    for a in range(len(srcs)):
        for n, r in enumerate(remote_ix):
            k, slab, there, here = rts[r]
            t = me ^ k
            sem = len(remote_ix) * a + n
            remote.append(pltpu.make_async_remote_copy(
                src_ref=srcs[a].at[slab], dst_ref=lands[a].at[there if sending else here], send_sem=send_sems.at[sem],
                recv_sem=recv_sems.at[sem], device_id=(t // 4, (t // 2) % 2, t % 2), device_id_type=pl.DeviceIdType.MESH))
        for n, r in enumerate(local_ix):
            _, slab, there, _ = rts[r]
            local.append(pltpu.make_async_copy(srcs[a].at[slab], lands[a].at[there], loc_sems.at[len(local_ix) * a + n]))
    return remote, local


def scatter_start(payloads, name):
    n = len(payloads)
    nr = NDEV - 1

    def body(*refs):
        srcs, lands, sems = refs[:n], refs[n:2 * n], refs[2 * n:2 * n + 3]
        remote, local = _scatter_copies(srcs, lands, sems, True)
        for cp in local + remote:
            cp.start()
        refs[-1][...] = jnp.zeros((8, 128), F32)

    thru = [pltpu.HBM(p.shape, p.dtype) for p in payloads]
    outs = pl.pallas_call(
        body, name=name,
        out_shape=(pltpu.SemaphoreType.DMA((nr * n,)), pltpu.SemaphoreType.DMA((nr * n,)), pltpu.SemaphoreType.DMA((n,)),
                   *thru, *thru, _sds((8, 128), F32)),
        in_specs=[HBM] * (2 * n),
        out_specs=(SEM, SEM, SEM, *[HBM] * (2 * n), pl.BlockSpec(memory_space=pltpu.VMEM)),
        input_output_aliases={i: 3 + i for i in range(2 * n)},
        compiler_params=pltpu.CompilerParams(has_side_effects=pltpu.SideEffectType.DATAFLOW_SIDE_EFFECTING),
    )(*[pltpu.with_memory_space_constraint(p, pltpu.HBM) for p in payloads],
      *[pltpu.with_memory_space_constraint(lax.empty(p.shape, p.dtype), pltpu.HBM) for p in payloads])
    return (outs[:3], outs[3:3 + n], outs[3 + n:3 + 2 * n]), outs[-1][0, 0]


def scatter_wait(handle, after, name):
    sems, srcs_thru, lands_thru = handle
    n = len(srcs_thru)

    def body(*refs):
        srcs, lands, sems_ = refs[:n], refs[n:2 * n], refs[2 * n:2 * n + 3]
        remote, local = _scatter_copies(srcs, lands, sems_, False)
        for cp in remote:
            cp.wait_send()
            cp.wait_recv()
        for cp in local:
            cp.wait()

    outs = pl.pallas_call(
        body, name=name, out_shape=tuple(pltpu.HBM(p.shape, p.dtype) for p in (*srcs_thru, *lands_thru)),
        in_specs=[HBM] * (2 * n) + [SEM] * 3 + [HBM], out_specs=tuple([HBM] * (2 * n)),
        input_output_aliases={i: i for i in range(2 * n)},
        compiler_params=pltpu.CompilerParams(has_side_effects=pltpu.SideEffectType.DATAFLOW_SIDE_EFFECTING),
    )(*srcs_thru, *lands_thru, *sems, pltpu.with_memory_space_constraint(after, pltpu.HBM))
    return list(outs[n:])


def _row_spec(cols, tm=TM):
    return pl.BlockSpec((tm, cols), lambda i: (i, 0))


def _full_spec(shape):
    nd = len(shape)
    return pl.BlockSpec(shape, lambda i: (0,) * nd)


def ln_mod_fwd(x, mod, sub, name):
    def body(x_ref, mod_ref, h_ref):
        n, _ = _ln_stats(x_ref[...])
        shift = mod_ref[3 * sub:3 * sub + 1, :]
        scale = mod_ref[3 * sub + 1:3 * sub + 2, :]
        h_ref[...] = (n * (1.0 + scale) + shift).astype(MXU)

    return _pallas_call(
        body, name=name, grid=(S // TM,),
        in_specs=[_row_spec(D), _full_spec((9, D))], out_specs=_row_spec(D),
        out_shape=_sds((S, D), MXU), compiler_params=_cp(32, ("arbitrary",)))(x, mod)


def res_ln_fwd(x, f, mod, sub, lng, lnb, w, name, nxt=None):
    def body(x_ref, f_ref, mod_ref, g_ref, b_ref, *rest):
        gate = mod_ref[3 * sub + 2:3 * sub + 3, :]
        r = ALPHA * x_ref[...] + (w * gate) * f_ref[...]
        n, _ = _ln_stats(r)
        xo = n * g_ref[sub:sub + 1, :] + b_ref[sub:sub + 1, :]
        rest[-1 if nxt is None else -2][...] = xo
        if nxt is not None:
            nmod_ref, h_ref = rest[0], rest[-1]
            n2, _ = _ln_stats(xo)
            s2 = nxt[1]
            h_ref[...] = (n2 * (1.0 + nmod_ref[3 * s2 + 1:3 * s2 + 2, :]) + nmod_ref[3 * s2:3 * s2 + 1, :]).astype(MXU)

    more = nxt is not None
    return _pallas_call(
        body, name=name, grid=(S // TM,),
        in_specs=[_row_spec(D), _row_spec(D), _full_spec((9, D)), _full_spec((3, D)), _full_spec((3, D))] + [_full_spec((9, D))] * more,
        out_specs=(_row_spec(D),) + (_row_spec(D),) * more, out_shape=(_sds((S, D), F32),) + (_sds((S, D), MXU),) * more,
        compiler_params=_cp(32, ("arbitrary",)))(x, f, mod, lng, lnb, *([nxt[0]] if more else []))


def res_ln_bwd(x, f, mod, sub, lng, dxo, w, name):
    def body(x_ref, f_ref, mod_ref, g_ref, dxo_ref, dxa_ref, df_ref, sums_ref):
        i = pl.program_id(0)
        gate = mod_ref[3 * sub + 2:3 * sub + 3, :]
        fv = f_ref[...]
        r = ALPHA * x_ref[...] + (w * gate) * fv
        n, rstd = _ln_stats(r)
        dxo = dxo_ref[...]
        dr = _ln_bwd(dxo * g_ref[sub:sub + 1, :], n, rstd)
        dxa_ref[...] = ALPHA * dr
        df_ref[...] = ((w * gate) * dr).astype(MXU)
        part = jnp.concatenate([
            jnp.sum(dxo * n, axis=0, keepdims=True),
            jnp.sum(dxo, axis=0, keepdims=True),
            jnp.sum(dr * fv, axis=0, keepdims=True) * w,
            jnp.zeros((5, D), F32)], axis=0)

        @pl.when(i == 0)
        def _():
            sums_ref[...] = part

        @pl.when(i > 0)
        def _():
            sums_ref[...] += part

    return _call(
        body, name=name, grid=(S // TM,),
        in_specs=[_row_spec(D), _row_spec(D), _full_spec((9, D)), _full_spec((3, D)), _row_spec(D)],
        out_specs=(_row_spec(D), _row_spec(D), _full_spec((8, D))),
        out_shape=(_sds((S, D), F32), _sds((S, D), MXU), _sds((8, D), F32)),
        cp=_cp(32, ("arbitrary",)), args=(x, f, mod, lng, dxo))[0]


def ln_mod_bwd(x, dh, mod, sub, dxa, name):
    def body(x_ref, dh_ref, mod_ref, dxa_ref, dx_ref, sums_ref):
        i = pl.program_id(0)
        scale = mod_ref[3 * sub + 1:3 * sub + 2, :]
        n, rstd = _ln_stats(x_ref[...])
        dh = dh_ref[...]
        dx_ref[...] = dxa_ref[...] + _ln_bwd(dh * (1.0 + scale), n, rstd)
        part = jnp.concatenate([
            jnp.sum(dh, axis=0, keepdims=True),
            jnp.sum(dh * n, axis=0, keepdims=True),
            jnp.zeros((6, D), F32)], axis=0)

        @pl.when(i == 0)
        def _():
            sums_ref[...] = part

        @pl.when(i > 0)
        def _():
            sums_ref[...] += part

    return _call(
        body, name=name, grid=(S // TM,),
        in_specs=[_row_spec(D), _row_spec(D), _full_spec((9, D)), _row_spec(D)],
        out_specs=(_row_spec(D), _full_spec((8, D))),
        out_shape=(_sds((S, D), F32), _sds((8, D), F32)),
        cp=_cp(32, ("arbitrary",)), args=(x, dh, mod, dxa))[0]


def ln_join_bwd(xp, fp, modp, subp, lngp, lnbp, wp, dh, mod, sub, dxa, name):
    def body(xp_ref, fp_ref, modp_ref, g_ref, b_ref, dh_ref, mod_ref, dxa_ref, dxap_ref, dfp_ref, sumsp_ref, sums_ref):
        i = pl.program_id(0)
        gate = modp_ref[3 * subp + 2:3 * subp + 3, :]
        fv = fp_ref[...]
        n, rstd = _ln_stats(ALPHA * xp_ref[...] + (wp * gate) * fv)
        gain = g_ref[subp:subp + 1, :]
        n2, rstd2 = _ln_stats(n * gain + b_ref[subp:subp + 1, :])
        dh = dh_ref[...]
        dx = dxa_ref[...] + _ln_bwd(dh * (1.0 + mod_ref[3 * sub + 1:3 * sub + 2, :]), n2, rstd2)
        dr = _ln_bwd(dx * gain, n, rstd)
        dxap_ref[...] = ALPHA * dr
        dfp_ref[...] = ((wp * gate) * dr).astype(MXU)
        partp = jnp.concatenate([
            jnp.sum(dx * n, axis=0, keepdims=True), jnp.sum(dx, axis=0, keepdims=True),
            jnp.sum(dr * fv, axis=0, keepdims=True) * wp, jnp.zeros((5, D), F32)], axis=0)
        part = jnp.concatenate([
            jnp.sum(dh, axis=0, keepdims=True), jnp.sum(dh * n2, axis=0, keepdims=True), jnp.zeros((6, D), F32)], axis=0)

        @pl.when(i == 0)
        def _():
            sumsp_ref[...] = partp
            sums_ref[...] = part

        @pl.when(i > 0)
        def _():
            sumsp_ref[...] += partp
            sums_ref[...] += part

    return _pallas_call(
        body, name=name, grid=(S // TM,),
        in_specs=[_row_spec(D), _row_spec(D), _full_spec((9, D)), _full_spec((3, D)), _full_spec((3, D)), _row_spec(D),
                  _full_spec((9, D)), _row_spec(D)],
        out_specs=(_row_spec(D), _row_spec(D), _full_spec((8, D)), _full_spec((8, D))),
        out_shape=(_sds((S, D), F32), _sds((S, D), MXU), _sds((8, D), F32), _sds((8, D), F32)),
        compiler_params=_cp(40, ("arbitrary",)))(xp, fp, modp, lngp, lnbp, dh, mod, dxa)


def loss_fwd_bwd(y, target, name):
    def body(y_ref, t_ref, l_ref, dy_ref):
        i = pl.program_id(0)
        e = y_ref[...] - t_ref[...]
        dy_ref[...] = e * (1.0 / D)
        part = jnp.zeros((8, 128), F32) + (0.5 / D) * jnp.sum(e * e)

        @pl.when(i == 0)
        def _():
            l_ref[...] = part

        @pl.when(i > 0)
        def _():
            l_ref[...] += part

    return _pallas_call(
        body, name=name, grid=(S // TM,),
        in_specs=[_row_spec(D), _row_spec(D)], out_specs=(_full_spec((8, 128)), _row_spec(D)),
        out_shape=(_sds((8, 128), F32), _sds((S, D), F32)),
        compiler_params=_cp(32, ("arbitrary",)))(y, target)


HB = 2 * FBP
NHB = NDEV * FBP // HB
TMB = 1024


def _wrows(buffers=2):
    return pl.BlockSpec((HB, D), lambda j, i: (j, 0), pipeline_mode=pl.Buffered(buffers))


def _resident(shape):
    return pl.BlockSpec(shape, lambda j, i: (0, 0), pipeline_mode=pl.Buffered(1))


def ffn_fwd(h, wgt, wut, wd, name, ride=None):
    def body(h_ref, wg_ref, wu_ref, wd_ref, g_ref, u_ref, f_ref):
        j, i = pl.program_id(0), pl.program_id(1)
        hv = h_ref[...]
        g = _mm_nt(hv, wg_ref[...])
        u = _mm_nt(hv, wu_ref[...])
        g_ref[...] = g.astype(MXU)
        u_ref[...] = u.astype(MXU)
        a = g * jax.nn.sigmoid(g) * u
        part = _mm(a, wd_ref[...])
        rows = pl.ds(pl.multiple_of(i * TMB, TMB), TMB)

        @pl.when(j == 0)
        def _():
            f_ref[rows, :] = part

        @pl.when(j > 0)
        def _():
            f_ref[rows, :] += part

    gu = pl.BlockSpec((TMB, HB), lambda j, i: (i, j))
    return _call(
        body, name=name, grid=(NHB, S // TMB),
        in_specs=[pl.BlockSpec((TMB, D), lambda j, i: (i, 0)), _wrows(), _wrows(), _wrows()],
        out_specs=(gu, gu, _resident((S, D))),
        out_shape=(_sds((S, NDEV * FBP), MXU), _sds((S, NDEV * FBP), MXU), _sds((S, D), F32)),
        cp=_cp(52, ("arbitrary", "arbitrary")), args=(h, wgt, wut, wd), ride=ride)


def ffn_bwd(df, h, g, u, wgt, wut, wd, name):
    ni = S // TMB

    def body(df_ref, h_ref, g_ref, u_ref, wg_ref, wu_ref, wd_ref, dwg_ref, dwu_ref, dwd_ref, dh_ref,
             ag_ref, au_ref, ad_ref):
        j, i = pl.program_id(0), pl.program_id(1)
        dfv, hv = df_ref[...], h_ref[...]
        gv, uv = g_ref[...].astype(F32), u_ref[...].astype(F32)
        da = _mm_nt(dfv, wd_ref[...])
        sg = jax.nn.sigmoid(gv)
        silu = gv * sg
        du = da * silu
        dg = da * uv * (sg * (1.0 + gv * (1.0 - sg)))
        p_d = _mm_tn(silu * uv, dfv)
        p_g = _mm_tn(dg, hv)
        p_u = _mm_tn(du, hv)

        @pl.when(i == 0)
        def _():
            ad_ref[...] = p_d
            ag_ref[...] = p_g
            au_ref[...] = p_u

        @pl.when(i > 0)
        def _():
            ad_ref[...] += p_d
            ag_ref[...] += p_g
            au_ref[...] += p_u

        @pl.when(i == ni - 1)
        def _():
            dwd_ref[...] = ad_ref[...].astype(BF16)
            dwg_ref[...] = ag_ref[...].astype(BF16)
            dwu_ref[...] = au_ref[...].astype(BF16)

        part = _mm(dg, wg_ref[...]) + _mm(du, wu_ref[...])
        rows = pl.ds(pl.multiple_of(i * TMB, TMB), TMB)

        @pl.when(j == 0)
        def _():
            dh_ref[rows, :] = part

        @pl.when(j > 0)
        def _():
            dh_ref[rows, :] += part

    gu = pl.BlockSpec((TMB, HB), lambda j, i: (i, j))
    rowt = pl.BlockSpec((TMB, D), lambda j, i: (i, 0))
    return _call(
        body, name=name, grid=(NHB, ni),
        in_specs=[rowt, rowt, gu, gu, _wrows(1), _wrows(1), _wrows(1)],
        out_specs=(_wrows(1), _wrows(1), _wrows(1), _resident((S, D))),
        out_shape=(_sds((NDEV * FBP, D), BF16), _sds((NDEV * FBP, D), BF16), _sds((NDEV * FBP, D), BF16), _sds((S, D), F32)),
        scratch=[pltpu.VMEM((HB, D), F32), pltpu.VMEM((HB, D), F32), pltpu.VMEM((HB, D), F32)],
        cp=_cp(60, ("arbitrary", "arbitrary")), args=(df, h, g, u, wgt, wut, wd))[0]


def win_fwd(h, win, name, ride=None):
    def body(h_ref, w_ref, z_ref):
        hv = h_ref[...]
        for j in range(NDEV):
            z_ref[:, 256 * j:256 * (j + 1)] = _mm(hv, w_ref[j])

    return _call(
        body, name=name, grid=(S // TMM,),
        in_specs=[_row_spec(D, TMM), _full_spec((NDEV, D, 256))],
        out_specs=[_row_spec(D_IN, TMM)], out_shape=[_sds((S, D_IN), F32)],
        cp=_cp(40, ("arbitrary",)), args=(h, win), ride=ride)


def win_bwd(dparts, h, win, name):
    ni = S // TMM

    def body(dq_ref, dk_ref, dv_ref, dus_ref, dup_ref, h_ref, w_ref, dh_ref, dw_ref, acc_ref):
        i = pl.program_id(0)
        hv = h_ref[...]
        cols = [dq_ref[:, 0:256], dq_ref[:, 256:512], dk_ref[:, 0:256], dk_ref[:, 256:512],
                dv_ref[:, 0:256], dv_ref[:, 256:512], dus_ref[...], dup_ref[...]]
        dh = jnp.zeros((TMM, D), F32)
        for j in range(NDEV):
            dz = cols[j].astype(MXU)
            dh = dh + _mm_nt(dz, w_ref[j])
            p = _mm_tn(hv, dz)

            @pl.when(i == 0)
            def _():
                acc_ref[j] = p

            @pl.when(i > 0)
            def _():
                acc_ref[j] += p

        dh_ref[...] = dh

        @pl.when(i == ni - 1)
        def _():
            dw_ref[...] = acc_ref[...].astype(BF16)

    return _call(
        body, name=name, grid=(ni,),
        in_specs=[_row_spec(512, TMM), _row_spec(512, TMM), _row_spec(512, TMM), _row_spec(256, TMM), _row_spec(256, TMM),
                  _row_spec(D, TMM), _full_spec((NDEV, D, 256))],
        out_specs=(_row_spec(D, TMM), _full_spec((NDEV, D, 256))),
        out_shape=(_sds((S, D), F32), _sds((NDEV, D, 256), BF16)),
        scratch=[pltpu.VMEM((NDEV, D, 256), F32)],
        cp=_cp(48, ("arbitrary",)), args=(*dparts, h, win))[0]


def wout_fwd(ya, ys, yp, wout, name, ride=None):
    def body(ya_ref, ys_ref, yp_ref, w_ref, o_ref):
        w = w_ref[...].reshape(D, D)
        o_ref[...] = _mm(ya_ref[...], w[0:512]) + _mm(ys_ref[...], w[512:768]) + _mm(yp_ref[...], w[768:1024])

    return _call(
        body, name=name, grid=(S // TMM,),
        in_specs=[_row_spec(512, TMM), _row_spec(256, TMM), _row_spec(256, TMM), _full_spec((NDEV, 128, D))],
        out_specs=[_row_spec(D, TMM)], out_shape=[_sds((S, D), F32)],
        cp=_cp(40, ("arbitrary",)), args=(ya, ys, yp, wout), ride=ride)


def wout_bwd(do, ya, ys, yp, wout, name):
    ni = S // TMM

    def body(do_ref, ya_ref, ys_ref, yp_ref, w_ref, dya_ref, dys_ref, dyp_ref, dw_ref, acc_ref):
        i = pl.program_id(0)
        w = w_ref[...].reshape(D, D)
        dov = do_ref[...]
        dya_ref[...] = _mm_nt(dov, w[0:512])
        dys_ref[...] = _mm_nt(dov, w[512:768])
        dyp_ref[...] = _mm_nt(dov, w[768:1024])
        parts = [(0, 512, _mm_tn(ya_ref[...], dov)), (512, 768, _mm_tn(ys_ref[...], dov)),
                 (768, 1024, _mm_tn(yp_ref[...], dov))]
        for lo, hi, p in parts:
            @pl.when(i == 0)
            def _():
                acc_ref[lo:hi, :] = p

            @pl.when(i > 0)
            def _():
                acc_ref[lo:hi, :] += p

        @pl.when(i == ni - 1)
        def _():
            dw_ref[...] = acc_ref[...].astype(BF16).reshape(NDEV, 128, D)

    return _call(
        body, name=name, grid=(ni,),
        in_specs=[_row_spec(D, TMM), _row_spec(512, TMM), _row_spec(256, TMM), _row_spec(256, TMM),
                  _full_spec((NDEV, 128, D))],
        out_specs=(_row_spec(512, TMM), _row_spec(256, TMM), _row_spec(256, TMM), _full_spec((NDEV, 128, D))),
        out_shape=(_sds((S, 512), F32), _sds((S, 256), F32), _sds((S, 256), F32), _sds((NDEV, 128, D), BF16)),
        scratch=[pltpu.VMEM((D, D), F32)],
        cp=_cp(40, ("arbitrary",)), args=(do, ya, ys, yp, wout))[0]


def _t5_bucket(dist):
    max_exact = N_BUCKETS // 2
    d = np.maximum(dist, 1).astype(np.float32)
    large = max_exact + (np.log(d / max_exact) / math.log(MAX_DISTANCE / max_exact)
                         * (N_BUCKETS - max_exact)).astype(np.int32)
    large = np.minimum(large, N_BUCKETS - 1)
    return np.where(dist < max_exact, dist, large).astype(np.int32)


def _att_static():
    i = np.arange(QB)[:, None]
    j = np.arange(2 * QB)[None, :]
    r = i + QB - j
    buckets, bands = [], []
    for window, dil in PATTERNS:
        bands.append((r >= 0) & (r <= window // dil))
        buckets.append(_t5_bucket(np.clip(r, 0, None) * dil))
    return np.stack(buckets), np.stack(bands), np.broadcast_to(j >= QB, (QB, 2 * QB))


def att_bias(rel_bias):
    m = np.arange(2 * QB)
    rows = []
    for window, dil in PATTERNS:
        r = QB - m
        ok = (r >= 0) & (r <= window // dil)
        b = rel_bias[_t5_bucket(np.clip(r, 0, None) * dil)]
        rows.append(jnp.where(ok[:, None], b, NEG).T)
    return jnp.broadcast_to(jnp.stack(rows)[:, :, None, :], (3, N_HEADS, 8, 2 * QB))


def _bias_tiles(t_ref, tiles):
    col = lax.broadcasted_iota(jnp.int32, (QB, 2 * QB), 1)
    for p in range(3):
        for hh in range(2):
            t = pltpu.roll(jnp.broadcast_to(t_ref[p, hh, 0:1, :], (QB, 2 * QB)), 0, 1, stride=1, stride_axis=0)
            tiles[p, hh, 0] = t
            tiles[p, hh, 1] = jnp.where(col >= QB, t, NEG)


def _permute_in(dst_ref, src_ref, d, scale=None, pad=QB):
    L = S // d
    for r in range(d):
        v = src_ref[pl.ds(r, L, stride=d), :] if d > 1 else src_ref[...]
        if scale is not None:
            v = v * scale
        dst_ref[pad + r * L:pad + (r + 1) * L, :] = v.astype(dst_ref.dtype)


def att_fwd(z, bias, name, ride=None):
    def body(q_ref, k_ref, v_ref, t_ref, y_ref, l_ref, qs, ks, vs, o_perm, l_perm, o_nat, l_nat, b_ref):
        _bias_tiles(t_ref, b_ref)
        zero_pad = jnp.zeros((QB, 128), MXU)
        ks[0:QB, :] = zero_pad
        vs[0:QB, :] = zero_pad
        lane = lax.broadcasted_iota(jnp.int32, (QB, 128), 1)
        for p, (_, d) in enumerate(PATTERNS):
            L = S // d
            nb = L // QB
            _permute_in(qs, q_ref, d, scale=0.125, pad=0)
            _permute_in(ks, k_ref, d)
            _permute_in(vs, v_ref, d)

            def blk(b, carry):
                r0 = pl.multiple_of(b * QB, QB)
                q = qs[pl.ds(r0, QB), :]
                kb = ks[pl.ds(r0, 2 * QB), :]
                vb = vs[pl.ds(r0, 2 * QB), :]
                first = ((b % nb) == 0).astype(jnp.int32)
                res = []
                for hh in range(2):
                    sel = (lane < 64) if hh == 0 else (lane >= 64)
                    qm = jnp.where(sel, q, jnp.zeros_like(q))
                    s = _mm_nt(qm, kb) + b_ref[p, hh, first]
                    m = jnp.max(s, axis=1, keepdims=True)
                    pe = jnp.exp(s - m)
                    den = jnp.sum(pe, axis=1, keepdims=True)
                    res.append((_mm(pe, vb) / den, m + jnp.log(den)))
                o_perm[pl.ds(r0, QB), :] = jnp.where(lane < 64, res[0][0], res[1][0])
                l_perm[pl.ds(r0, QB), :] = jnp.where(lane < 64, res[0][1], res[1][1])
                return carry

            lax.fori_loop(0, S // QB, blk, 0, unroll=8)
            for r in range(d):
                if d > 1:
                    o_nat[p, pl.ds(r, L, stride=d), :] = o_perm[r * L:(r + 1) * L, :]
                    l_nat[p, pl.ds(r, L, stride=d), :] = l_perm[r * L:(r + 1) * L, :]
                else:
                    o_nat[p] = o_perm[...]
                    l_nat[p] = l_perm[...]
        l0, l1, l2 = l_nat[0], l_nat[1], l_nat[2]
        m = jnp.maximum(jnp.maximum(l0, l1), l2)
        e0, e1, e2 = jnp.exp(l0 - m), jnp.exp(l1 - m), jnp.exp(l2 - m)
        den = e0 + e1 + e2
        y_ref[...] = (e0 * o_nat[0] + e1 * o_nat[1] + e2 * o_nat[2]) / den
        l_ref[...] = m + jnp.log(den)

    col = lambda c0: pl.BlockSpec((S, 128), lambda hp: (0, c0 + hp))
    return _call(
        body, name=name, grid=(N_HEADS // 2,),
        in_specs=[col(0), col(4), col(8), pl.BlockSpec((3, 2, 8, 2 * QB), lambda hp: (0, hp, 0, 0))],
        out_specs=(col(0), col(0)),
        out_shape=(_sds((S, D_ATT), F32), _sds((S, D_ATT), F32)),
        scratch=[pltpu.VMEM((S, 128), MXU), pltpu.VMEM((S + QB, 128), MXU), pltpu.VMEM((S + QB, 128), MXU),
                 pltpu.VMEM((S, 128), F32), pltpu.VMEM((S, 128), F32),
                 pltpu.VMEM((3, S, 128), F32), pltpu.VMEM((3, S, 128), F32),
                 pltpu.VMEM((3, 2, 2, QB, 2 * QB), F32)],
        cp=_cp(40, ("arbitrary",)), args=(z, z, z, bias), ride=ride)


def att_bwd(z, bias, y, lse, dy, name):
    def body(q_ref, k_ref, v_ref, t_ref, y_ref, l_ref, dy_ref, dq_ref, dk_ref, dv_ref, db_ref,
             qs, ks, vs, dys, ls, dds, dn_nat, dq_perm, dk_perm, dv_perm, b_ref):
        _bias_tiles(t_ref, b_ref)
        zero_pad = jnp.zeros((QB, 128), MXU)
        ks[0:QB, :] = zero_pad
        vs[0:QB, :] = zero_pad
        lane = lax.broadcasted_iota(jnp.int32, (QB, 128), 1)
        lane_s = lax.broadcasted_iota(jnp.int32, (S, 128), 1)
        t = dy_ref[...] * y_ref[...]
        sa = jnp.sum(jnp.where(lane_s < 64, t, 0.0), axis=1, keepdims=True)
        sb = jnp.sum(jnp.where(lane_s >= 64, t, 0.0), axis=1, keepdims=True)
        dn_nat[...] = jnp.where(lane_s < 64, sa, sb)
        dq_ref[...] = jnp.zeros((S, 128), F32)
        dk_ref[...] = jnp.zeros((S, 128), F32)
        dv_ref[...] = jnp.zeros((S, 128), F32)
        db_ref[...] = jnp.zeros((3, 2, QB, 2 * QB), F32)
        for p, (_, d) in enumerate(PATTERNS):
            L = S // d
            nb = L // QB
            _permute_in(qs, q_ref, d, scale=0.125, pad=0)
            _permute_in(ks, k_ref, d)
            _permute_in(vs, v_ref, d)
            _permute_in(dys, dy_ref, d, pad=0)
            _permute_in(ls, l_ref, d, pad=0)
            _permute_in(dds, dn_nat, d, pad=0)
            dk_perm[...] = jnp.zeros((S + QB, 128), F32)
            dv_perm[...] = jnp.zeros((S + QB, 128), F32)

            def blk(b, carry):
                r0 = pl.multiple_of(b * QB, QB)
                q = qs[pl.ds(r0, QB), :]
                kb = ks[pl.ds(r0, 2 * QB), :]
                vb = vs[pl.ds(r0, 2 * QB), :]
                dyb = dys[pl.ds(r0, QB), :]
                lb = ls[pl.ds(r0, QB), :]
                db = dds[pl.ds(r0, QB), :]
                first = ((b % nb) == 0).astype(jnp.int32)
                lane2 = jnp.concatenate([lane, lane], axis=0)
                own = (lane2 >> 6) == (lax.broadcasted_iota(jnp.int32, (2 * QB, 128), 0) >> 7)
                qm = jnp.where(own, jnp.concatenate([q, q], axis=0), jnp.zeros((2 * QB, 128), q.dtype))
                dym = jnp.where(own, jnp.concatenate([dyb, dyb], axis=0), jnp.zeros((2 * QB, 128), dyb.dtype))
                wide = lambda t: jnp.concatenate([jnp.broadcast_to(t[:, 0:1], (QB, 2 * QB)), jnp.broadcast_to(t[:, 64:65], (QB, 2 * QB))], axis=0)
                lse2, dd2 = wide(lb), wide(db)
                bias2 = jnp.concatenate([b_ref[p, 0, first], b_ref[p, 1, first]], axis=0)
                pr = jnp.exp(_mm_nt(qm, kb) + bias2 - lse2)
                ds = pr * (_mm_nt(dym, vb) - dd2)
                db_ref[p, 0] += ds[0:QB]
                db_ref[p, 1] += ds[QB:2 * QB]
                dq2 = _mm(ds, kb)
                dqs = [dq2[0:QB], dq2[QB:2 * QB]]
                dkb = _mm_tn(ds, qm)
                dvb = _mm_tn(pr, dym)
                dq_perm[pl.ds(r0, QB), :] = jnp.where(lane < 64, dqs[0], dqs[1])
                dk_perm[pl.ds(r0, 2 * QB), :] += dkb
                dv_perm[pl.ds(r0, 2 * QB), :] += dvb
                return carry

            lax.fori_loop(0, S // QB, blk, 0, unroll=4)
            for r in range(d):
                idx = pl.ds(r, L, stride=d) if d > 1 else pl.ds(0, S)
                dq_ref[idx, :] += dq_perm[r * L:(r + 1) * L, :] * 0.125
                dk_ref[idx, :] += dk_perm[QB + r * L:QB + (r + 1) * L, :]
                dv_ref[idx, :] += dv_perm[QB + r * L:QB + (r + 1) * L, :]

    col = lambda c0: pl.BlockSpec((S, 128), lambda hp: (0, c0 + hp))
    bspec = pl.BlockSpec((3, 2, 8, 2 * QB), lambda hp: (0, hp, 0, 0))
    return _call(
        body, name=name, grid=(N_HEADS // 2,),
        in_specs=[col(0), col(4), col(8), bspec, col(0), col(0), col(0)],
        out_specs=(col(0), col(0), col(0), pl.BlockSpec((3, 2, QB, 2 * QB), lambda hp: (0, hp, 0, 0))),
        out_shape=(_sds((S, D_ATT), F32), _sds((S, D_ATT), F32), _sds((S, D_ATT), F32),
                   _sds((3, N_HEADS, QB, 2 * QB), F32)),
        scratch=[pltpu.VMEM((S, 128), MXU), pltpu.VMEM((S + QB, 128), MXU), pltpu.VMEM((S + QB, 128), MXU),
                 pltpu.VMEM((S, 128), MXU), pltpu.VMEM((S, 128), F32), pltpu.VMEM((S, 128), F32),
                 pltpu.VMEM((S, 128), F32), pltpu.VMEM((S, 128), F32),
                 pltpu.VMEM((S + QB, 128), F32), pltpu.VMEM((S + QB, 128), F32),
                 pltpu.VMEM((3, 2, 2, QB, 2 * QB), F32)],
        cp=_cp(48, ("arbitrary",)), args=(z, z, z, bias, y, lse, dy))[0]


def relbias_grad(dbiases):
    bucket, band, _ = _att_static()
    onehot = (bucket[:, None] == np.arange(N_BUCKETS)[None, :, None, None]) & band[:, None]
    onehot = jnp.asarray(onehot.reshape(3, N_BUCKETS, QB * 2 * QB), BF16)

    def body(db0_ref, db1_ref, oh_ref, o_ref):
        acc = jnp.zeros((N_HEADS, N_BUCKETS), F32)
        for p in range(3):
            acc = acc + lax.dot_general(db0_ref[p] + db1_ref[p], oh_ref[p].astype(F32), (((1,), (1,)), ((), ())),
                                        preferred_element_type=F32, precision=lax.Precision.HIGHEST)
        o_ref[...] = acc

    vm = pl.BlockSpec(memory_space=pltpu.VMEM)
    out = _pallas_call(body, name="relbias_grad", in_specs=[vm, vm, vm], out_specs=vm,
                         out_shape=_sds((N_HEADS, N_BUCKETS), F32), compiler_params=_cp(40))(
        *[d.reshape(3, N_HEADS, QB * 2 * QB) for d in dbiases], onehot)
    return out.T


def _panel(t_ref, ri, j):
    return t_ref[ri, pl.ds(j, S, stride=8), :]


def _gelu(x):
    c = math.sqrt(2.0 / math.pi)
    th = jnp.tanh(c * (x + 0.044715 * x * x * x))
    return 0.5 * x * (1.0 + th), th


def ssm_fwd(z, a, bre, bim, cre, cim, dsk, gluw, glub, name, ride=None):
    def body(u_ref, a_ref, bre_ref, bim_ref, cre_ref, cim_ref, d_ref, gw_ref, gb_ref, y_ref, yp_ref, st_hbm, st_ref):
        u = u_ref[...]
        for j in range(8):
            st_ref[0, pl.ds(j, S, stride=8), :] = _mm(u, bre_ref[:, 128 * j:128 * (j + 1)])
            st_ref[1, pl.ds(j, S, stride=8), :] = _mm(u, bim_ref[:, 128 * j:128 * (j + 1)])
        ar, ai = a_ref[0], a_ref[1]

        def step(t, c):
            re, im = c
            i = pl.multiple_of(t * 8, 8)
            nre = ar * re - ai * im + st_ref[0, pl.ds(i, 8), :]
            nim = ar * im + ai * re + st_ref[1, pl.ds(i, 8), :]
            st_ref[0, pl.ds(i, 8), :] = nre
            st_ref[1, pl.ds(i, 8), :] = nim
            return nre, nim

        zero = jnp.zeros((8, 128), F32)
        lax.fori_loop(0, S, step, (zero, zero), unroll=8)
        y = d_ref[...] * u
        for j in range(8):
            y = y + _mm(_panel(st_ref, 0, j), cre_ref[128 * j:128 * (j + 1), :])
            y = y - _mm(_panel(st_ref, 1, j), cim_ref[128 * j:128 * (j + 1), :])
        pltpu.sync_copy(st_ref, st_hbm)
        yp_ref[...] = y
        gl, _ = _gelu(y)
        tt = _mm(gl, gw_ref[...].reshape(D_SSM, D_SSM)) + gb_ref[...]
        y_ref[...] = y * jax.nn.sigmoid(tt)

    vm = lambda shape: pl.BlockSpec(shape, lambda i: (0,) * len(shape))
    return _call(
        body, name=name, grid=(1,),
        in_specs=[pl.BlockSpec((S, 256), lambda i: (0, 6)), vm((2, 8, 128)), vm((256, 1024)), vm((256, 1024)),
                  vm((1024, 256)), vm((1024, 256)), vm((1, 256)),
                  vm((NDEV, 32, 256)), vm((1, 256))],
        out_specs=(vm((S, 256)), vm((S, 256)), pl.BlockSpec(memory_space=pl.ANY)),
        out_shape=(_sds((S, 256), F32), _sds((S, 256), F32), _sds((2, S * 8, 128), F32)),
        scratch=[pltpu.VMEM((2, S * 8, 128), F32)],
        cp=_cp(40, ("arbitrary",)), args=(z, a, bre, bim, cre, cim, dsk, gluw, glub), ride=ride)


def ssm_bwd(dy, z, ypre, st, a, bre, bim, cre, cim, dsk, gluw, glub, name):
    def body(dy_ref, u_ref, yp_ref, st_hbm, a_ref, bre_ref, bim_ref, cre_ref, cim_ref, d_ref, gw_ref, gb_ref,
             du_ref, dbre_ref, dbim_ref, dcre_ref, dcim_ref, da_ref, dd_ref, dgw_ref, dgb_ref, g_ref, st_ref):
        pltpu.sync_copy(st_hbm, st_ref)
        u = u_ref[...]
        y = yp_ref[...]
        dout = dy_ref[...]
        gw = gw_ref[...].reshape(D_SSM, D_SSM)
        gl, th = _gelu(y)
        sig = jax.nn.sigmoid(_mm(gl, gw) + gb_ref[...])
        dt = dout * y * sig * (1.0 - sig)
        dgw_ref[...] = _mm_tn(gl, dt)
        dgb_ref[...] = jnp.sum(dt, axis=0, keepdims=True)
        c = math.sqrt(2.0 / math.pi)
        dgelu = 0.5 * (1.0 + th) + 0.5 * y * (1.0 - th * th) * c * (1.0 + 3.0 * 0.044715 * y * y)
        dyv = dout * sig + _mm_nt(dt, gw) * dgelu
        dd_ref[...] = jnp.sum(dyv * u, axis=0, keepdims=True)
        for j in range(8):
            rows = slice(128 * j, 128 * (j + 1))
            g_ref[0, pl.ds(j, S, stride=8), :] = _mm_nt(dyv, cre_ref[rows, :])
            g_ref[1, pl.ds(j, S, stride=8), :] = -_mm_nt(dyv, cim_ref[rows, :])
            dcre_ref[rows, :] = _mm_tn(_panel(st_ref, 0, j), dyv)
            dcim_ref[rows, :] = -_mm_tn(_panel(st_ref, 1, j), dyv)
        ar, ai = a_ref[0], a_ref[1]

        def step(k, c4):
            gre, gim, dar, dai = c4
            i = pl.multiple_of((S - 1 - k) * 8, 8)
            nre = g_ref[0, pl.ds(i, 8), :] + ar * gre + ai * gim
            nim = g_ref[1, pl.ds(i, 8), :] + ar * gim - ai * gre
            g_ref[0, pl.ds(i, 8), :] = nre
            g_ref[1, pl.ds(i, 8), :] = nim
            sre = st_ref[0, pl.ds(i - 8, 8), :]
            sim = st_ref[1, pl.ds(i - 8, 8), :]
            return nre, nim, dar + nre * sre + nim * sim, dai + nim * sre - nre * sim

        zero = jnp.zeros((8, 128), F32)
        gre, gim, dar, dai = lax.fori_loop(0, S - 1, step, (zero, zero, zero, zero), unroll=8)
        g_ref[0, 0:8, :] = g_ref[0, 0:8, :] + ar * gre + ai * gim
        g_ref[1, 0:8, :] = g_ref[1, 0:8, :] + ar * gim - ai * gre
        da_ref[0] = dar
        da_ref[1] = dai
        du = dyv * d_ref[...]
        for j in range(8):
            cols = slice(128 * j, 128 * (j + 1))
            gr, gi = _panel(g_ref, 0, j), _panel(g_ref, 1, j)
            dbre_ref[:, cols] = _mm_tn(u, gr)
            dbim_ref[:, cols] = _mm_tn(u, gi)
            du = du + _mm_nt(gr, bre_ref[:, cols]) + _mm_nt(gi, bim_ref[:, cols])
        du_ref[...] = du

    vm = lambda shape: pl.BlockSpec(shape, lambda i: (0,) * len(shape))
    return _call(
        body, name=name, grid=(1,),
        in_specs=[vm((S, 256)), pl.BlockSpec((S, 256), lambda i: (0, 6)), vm((S, 256)), pl.BlockSpec(memory_space=pl.ANY),
                  vm((2, 8, 128)), vm((256, 1024)), vm((256, 1024)), vm((1024, 256)), vm((1024, 256)), vm((1, 256)),
                  vm((NDEV, 32, 256)), vm((1, 256))],
        out_specs=(vm((S, 256)), vm((256, 1024)), vm((256, 1024)), vm((1024, 256)), vm((1024, 256)),
                   vm((2, 8, 128)), vm((1, 256)), vm((256, 256)), vm((1, 256))),
        out_shape=(_sds((S, 256), F32), _sds((256, 1024), F32), _sds((256, 1024), F32), _sds((1024, 256), F32),
                   _sds((1024, 256), F32), _sds((2, 8, 128), F32), _sds((1, 256), F32), _sds((256, 256), F32),
                   _sds((1, 256), F32)),
        scratch=[pltpu.VMEM((2, S * 8, 128), F32), pltpu.VMEM((2, S * 8, 128), F32)],
        cp=_cp(56, ("arbitrary",)), args=(dy, z, ypre, st, a, bre, bim, cre, cim, dsk, gluw, glub))[0]


def _ssm_discretise(a_re, a_im, log_dt, b_re, b_im):
    dt = jnp.exp(log_dt)[:, None]
    er = jnp.exp(a_re * dt)
    abr, abi = er * jnp.cos(a_im * dt), er * jnp.sin(a_im * dt)
    den = a_re * a_re + a_im * a_im
    fr = ((abr - 1.0) * a_re + abi * a_im) / den
    fi = (abi * a_re - (abr - 1.0) * a_im) / den
    bbr = fr[:, :, None] * b_re - fi[:, :, None] * b_im
    bbi = fr[:, :, None] * b_im + fi[:, :, None] * b_re
    return abr, abi, bbr, bbi


def _blockdiag(t):
    g, r, c = t.shape
    eye = jnp.eye(g, dtype=t.dtype)
    return (t[:, :, None, :] * eye[:, None, :, None]).reshape(g * r, g * c)


def _blockdiag_take(m, r, c):
    g = m.shape[0] // r
    idx = jnp.arange(g)
    return m.reshape(g, r, g, c)[idx, :, idx, :]


PAD = 16


def _pool_lane_select(vals):
    lane = lax.broadcasted_iota(jnp.int32, vals[0].shape, 1)
    out = vals[3]
    for g in (2, 1, 0):
        out = jnp.where(lane < 64 * (g + 1), vals[g], out)
    return out


def _pool_counts():
    row = lax.broadcasted_iota(jnp.int32, (S, D_POOL), 0).astype(F32) + 1.0
    return _pool_lane_select([jnp.minimum(row, float(w)) for w in POOL_WINDOWS])


def _pooled(u, sa, sb):
    sums = []
    cur = u
    bufs = (sa, sb)
    for k, sh in enumerate((1, 2, 4, 8)):
        buf = bufs[k % 2]
        buf[PAD:PAD + S, :] = cur
        cur = cur + buf[PAD - sh:PAD - sh + S, :]
        sums.append(cur)
    return _pool_lane_select(sums) / _pool_counts() - u


def pool_fwd(z, pw, psc, name):
    def body(u_ref, w_ref, s_ref, y_ref, sa, sb):
        for buf in (sa, sb):
            buf[0:PAD, :] = jnp.zeros((PAD, D_POOL), F32)
        pooled = _pooled(u_ref[...], sa, sb)
        y_ref[...] = _mm(pooled, w_ref[...]) * s_ref[...]

    vm = lambda shape: pl.BlockSpec(shape, lambda i: (0,) * len(shape))
    return _pallas_call(
        body, name=name, grid=(1,),
        in_specs=[pl.BlockSpec((S, 256), lambda i: (0, 7)), vm((256, 256)), vm((1, 256))],
        out_specs=vm((S, 256)), out_shape=_sds((S, 256), F32),
        scratch_shapes=[pltpu.VMEM((S + 2 * PAD, D_POOL), F32)] * 2,
        compiler_params=_cp(40, ("arbitrary",)))(z, pw, psc)


def pool_bwd(dy, z, pw, psc, name):
    def body(dy_ref, u_ref, w_ref, s_ref, du_ref, dw_ref, ds_ref, sa, sb):
        for buf in (sa, sb):
            buf[0:PAD, :] = jnp.zeros((PAD, D_POOL), F32)
            buf[PAD + S:PAD + S + PAD, :] = jnp.zeros((PAD, D_POOL), F32)
        pooled = _pooled(u_ref[...], sa, sb)
        dyv = dy_ref[...]
        w = w_ref[...]
        ds_ref[...] = jnp.sum(dyv * _mm(pooled, w), axis=0, keepdims=True)
        dyl = dyv * s_ref[...]
        dw_ref[...] = _mm_tn(pooled, dyl)
        dpool = _mm_nt(dyl, w)
        cur = dpool / _pool_counts()
        sums = []
        bufs = (sa, sb)
        for k, sh in enumerate((1, 2, 4, 8)):
            buf = bufs[k % 2]
            buf[PAD:PAD + S, :] = cur
            cur = cur + buf[PAD + sh:PAD + sh + S, :]
            sums.append(cur)
        du_ref[...] = _pool_lane_select(sums) - dpool

    vm = lambda shape: pl.BlockSpec(shape, lambda i: (0,) * len(shape))
    return _pallas_call(
        body, name=name, grid=(1,),
        in_specs=[vm((S, 256)), pl.BlockSpec((S, 256), lambda i: (0, 7)), vm((256, 256)), vm((1, 256))],
        out_specs=(vm((S, 256)), vm((256, 256)), vm((1, 256))),
        out_shape=(_sds((S, 256), F32), _sds((256, 256), F32), _sds((1, 256), F32)),
        scratch_shapes=[pltpu.VMEM((S + 2 * PAD, D_POOL), F32)] * 2,
        compiler_params=_cp(40, ("arbitrary",)))(dy, z, pw, psc)


def ada_fwd(c_all, ada_w, ada_b_cols):
    def body(c_ref, w_ref, b_ref, o_ref):
        c = c_ref[...]
        cond = c * jax.nn.sigmoid(c)
        o_ref[...] = jnp.dot(cond, w_ref[...], preferred_element_type=F32, precision=lax.Precision.HIGHEST) + b_ref[...]

    return _pallas_call(
        body, name="ada_fwd", grid=(DEPTH,),
        in_specs=[pl.BlockSpec((NDEV, D), lambda l: (0, 0)), pl.BlockSpec((None, D, 1152), lambda l: (l, 0, 0)),
                  pl.BlockSpec((None, 1, 1152), lambda l: (l, 0, 0))],
        out_specs=pl.BlockSpec((None, NDEV, 1152), lambda l: (l, 0, 0)), out_shape=_sds((DEPTH, NDEV, 1152), F32),
        compiler_params=_cp(40, ("arbitrary",)))(c_all, ada_w, ada_b_cols)


def ada_bwd(c_all, dmod_cols):
    def body(c_ref, dm_ref, o_ref):
        c = c_ref[...]
        cond = c * jax.nn.sigmoid(c)
        o_ref[...] = lax.dot_general(cond, dm_ref[...], (((0,), (0,)), ((), ())), preferred_element_type=F32,
                                     precision=lax.Precision.HIGHEST)

    return _pallas_call(
        body, name="ada_bwd", grid=(DEPTH,),
        in_specs=[pl.BlockSpec((NDEV, D), lambda l: (0, 0)), pl.BlockSpec((None, NDEV, 1152), lambda l: (l, 0, 0))],
        out_specs=pl.BlockSpec((None, D, 1152), lambda l: (l, 0, 0)), out_shape=_sds((DEPTH, D, 1152), F32),
        compiler_params=_cp(40, ("arbitrary",)))(c_all, dmod_cols)


def _adamw(w, g, m, v):
    m2 = B1 * m + (1.0 - B1) * g
    v2 = B2 * v + (1.0 - B2) * (g * g)
    m_hat = m2 / (1.0 - B1 ** STEP)
    v_hat = v2 / (1.0 - B2 ** STEP)
    return -LR * (m_hat / (jnp.sqrt(v_hat) + EPS) + WD * w), m2, v2


def _sum8(ref):
    g = ref[0].astype(F32)
    for s in range(1, ref.shape[0]):
        g = g + ref[s].astype(F32)
    return g


def adam_rs(recv, w, m, v, tr, name):
    lead, (r, cdim) = w.shape[:-2], w.shape[-2:]
    cp = recv.shape[-1]
    nl = len(lead)

    def body(rc_ref, w_ref, m_ref, v_ref, g_ref, d_ref, m2_ref, v2_ref):
        g = _sum8(rc_ref)[:, :cdim]
        g_ref[...] = g
        d_ref[...], m2_ref[...], v2_ref[...] = _adamw(w_ref[...], g, m_ref[...], v_ref[...])

    rs = pl.BlockSpec((None,) * nl + (tr, cdim), lambda *i: (*i, 0))
    return _call(
        body, name=name, grid=lead + (r // tr,),
        in_specs=[pl.BlockSpec((NDEV,) + (None,) * nl + (tr, cp), lambda *i: (0, *i, 0)), rs, rs, rs],
        out_specs=(rs, rs, rs, rs), out_shape=tuple(_sds(w.shape, F32) for _ in range(4)),
        cp=_cp(48, ("arbitrary",) * (nl + 1)), args=(recv, w, m, v))[0]


def adam_block(recv, w, m, v, lf, prev, name):
    half = FB // 2

    def body(*refs):
        rc_ref, w_ref, m_ref, v_ref = refs[:4]
        g_ref, d_ref, m2_ref, v2_ref = refs[-4:]
        g = _sum8(rc_ref)
        g_ref[...] = g
        d_ref[...], m2_ref[...], v2_ref[...] = _adamw(w_ref[...], g, m_ref[...], v_ref[...])

    rs = pl.BlockSpec((None, None, half, D), lambda i: (lf // 2, lf % 2, i, 0))
    prev = list(prev) if prev is not None else []
    return list(_pallas_call(
        body, name=name, grid=(2,), in_specs=[pl.BlockSpec((recv.shape[0], half, D), lambda i: (0, i, 0)), rs, rs, rs] + [ANY] * len(prev),
        out_specs=(rs, rs, rs, rs), out_shape=tuple(_sds((DEPTH, 2, FB, D), F32) for _ in range(4)),
        input_output_aliases={4 + k: k for k in range(len(prev))},
        compiler_params=_cp(48, ("arbitrary",)))(recv, w, m, v, *prev))


def adam_plain(g, w, m, v, tr, name):
    lead, (r, cdim) = w.shape[:-2], w.shape[-2:]
    nl = len(lead)

    def body(g_ref, w_ref, m_ref, v_ref, d_ref, m2_ref, v2_ref):
        d_ref[...], m2_ref[...], v2_ref[...] = _adamw(w_ref[...], g_ref[...], m_ref[...], v_ref[...])

    rs = pl.BlockSpec((None,) * nl + (tr, cdim), lambda *i: (*i, 0))
    return _call(
        body, name=name, grid=lead + (r // tr,), in_specs=[rs, rs, rs, rs], out_specs=(rs, rs, rs),
        out_shape=tuple(_sds(w.shape, F32) for _ in range(3)),
        cp=_cp(48, ("arbitrary",) * (nl + 1)), args=(g, w, m, v))[0]


def adam_native(gs, ws, ms, vs, name):
    n = len(ws)

    def body(*refs):
        g_refs, w_refs, m_refs, v_refs = (refs[k * n:(k + 1) * n] for k in range(4))
        d_refs, m2_refs, v2_refs = (refs[(4 + k) * n:(5 + k) * n] for k in range(3))
        for a in range(n):
            d_refs[a][...], m2_refs[a][...], v2_refs[a][...] = _adamw(w_refs[a][...], g_refs[a][...], m_refs[a][...], v_refs[a][...])

    vm = pl.BlockSpec(memory_space=pltpu.VMEM)
    outs = _pallas_call(body, name=name, in_specs=[vm] * (4 * n), out_specs=tuple([vm] * (3 * n)),
                        out_shape=tuple(_sds(w.shape, F32) for w in ws) * 3, compiler_params=_cp(40))(*gs, *ws, *ms, *vs)
    return outs[:n], outs[n:2 * n], outs[2 * n:]


def sum_sources(recv, name):
    r = recv.shape[1]

    def body(rc_ref, o_ref):
        o_ref[...] = _sum8(rc_ref)

    vm = pl.BlockSpec(memory_space=pltpu.VMEM)
    return _pallas_call(body, name=name, in_specs=[vm], out_specs=vm, out_shape=_sds((r, 128), F32),
                          compiler_params=_cp(40))(recv)


def _pack(arrs):
    flat = jnp.concatenate([a.reshape(-1) for a in arrs])
    n = flat.shape[0]
    rows = -(-n // 1024) * 8
    return jnp.pad(flat, (0, rows * 128 - n)).reshape(rows, 128)


def _unpack(vec, shapes):
    flat = vec.reshape(-1)
    out, o = [], 0
    for sh in shapes:
        n = int(np.prod(sh))
        out.append(flat[o:o + n].reshape(sh))
        o += n
    return out


WEIGHTS = ['rel_bias', 'ada_w', 'ada_b', 'ln_g', 'ln_b', 'ffn_w_gate', 'ffn_w_up', 'ffn_w_down', 'w_in', 'w_out',
           'ssm_a_re', 'ssm_a_im', 'ssm_log_dt', 'ssm_b_re', 'ssm_b_im', 'ssm_c_re', 'ssm_c_im', 'ssm_d', 'glu_w',
           'glu_b', 'pool_w', 'pool_scale']
SMALL = ['rel_bias', 'ada_b', 'ln_g', 'ln_b', 'ssm_a_re', 'ssm_a_im', 'ssm_log_dt', 'ssm_b_re', 'ssm_b_im',
         'ssm_c_re', 'ssm_c_im', 'ssm_d', 'glu_b', 'pool_w', 'pool_scale']
SMALL_FULL_SHAPES = {'rel_bias': (32, 8), 'ada_b': (2, 9216), 'ln_g': (2, 3, 1024), 'ln_b': (2, 3, 1024),
                     'ssm_a_re': (2, 16, 64), 'ssm_a_im': (2, 16, 64), 'ssm_log_dt': (2, 16),
                     'ssm_b_re': (2, 16, 64, 16), 'ssm_b_im': (2, 16, 64, 16), 'ssm_c_re': (2, 16, 16, 64),
                     'ssm_c_im': (2, 16, 16, 64), 'ssm_d': (2, 256), 'glu_b': (2, 256), 'pool_w': (2, 4, 64, 64),
                     'pool_scale': (2, 256)}


def _step(P):
    me = _me()
    x0 = P['x'][0]
    target = P['loss_target'][0]

    bf = lambda a: a.astype(BF16)
    padr = lambda a: jnp.pad(bf(a), ((0, 0), (0, 0), (0, FBP - FB), (0, 0)))
    ffn_b = [padr(jnp.swapaxes(P['ffn_w_gate'], 2, 3)), padr(jnp.swapaxes(P['ffn_w_up'], 2, 3)), padr(P['ffn_w_down'])]
    mix_b = [bf(P['w_in']), bf(P['w_out']), bf(P['glu_w'])]

    def shards(l, sub):
        return [t[l] for t in mix_b] if sub == 1 else [t[l, sub // 2] for t in ffn_b]

    order = [(l, sub) for l in range(DEPTH) for sub in range(3)]
    nxt = dict(zip(order[:-1], order[1:]))
    W = {key: [None] * 3 for key in order}
    c_all, lng_all, lnb_all, *W[order[0]] = _exchange(Gather([P['c'], P['ln_g'], P['ln_b']] + shards(*order[0])), "gather_first")
    gather_queue = [(key, pos, a) for key in order[1:] for pos, a in enumerate(shards(*key))]

    def gather_ride(cap_us, must=None):
        units, used = [], 0.0
        while gather_queue:
            key, _, a = gather_queue[0]
            cost = a.size * a.dtype.itemsize * GATHER_US_PER_BYTE
            if key != must and used + cost / 2 > cap_us:
                break
            units.append(gather_queue.pop(0))
            used += cost
        return (Gather([a for _, _, a in units]) if units else None), units

    def gathered(units, outs):
        for (key, pos, _), o in zip(units, outs):
            W[key][pos] = o

    c_all = c_all.reshape(NDEV, D)
    ln_g = jnp.transpose(lng_all, (1, 2, 0, 3)).reshape(DEPTH, 3, D)
    ln_b = jnp.transpose(lnb_all, (1, 2, 0, 3)).reshape(DEPTH, 3, D)

    ada_b_cols = lax.dynamic_slice_in_dim(P['ada_b'], me * 1152, 1152, axis=1).reshape(DEPTH, 1, 1152)
    modc = ada_fwd(c_all, P['ada_w'], ada_b_cols)
    (mod_all,) = _exchange(Gather([modc]), "gather_mod")
    mod_me = lax.dynamic_index_in_dim(mod_all, me, axis=2, keepdims=False)
    mod = jnp.transpose(mod_me, (1, 0, 2)).reshape(DEPTH, 9, D)

    bias = att_bias(P['rel_bias'])
    ssm = []
    for l in range(DEPTH):
        prm = (P['ssm_a_re'][l], P['ssm_a_im'][l], P['ssm_log_dt'][l], P['ssm_b_re'][l], P['ssm_b_im'][l])
        (abr, abi, bbr, bbi), disc_vjp = jax.vjp(_ssm_discretise, *prm)
        ssm.append(dict(
            vjp=disc_vjp, a=jnp.stack([abr.reshape(8, 128), abi.reshape(8, 128)]),
            bre=_blockdiag(jnp.transpose(bbr, (0, 2, 1))).astype(MXU), bim=_blockdiag(jnp.transpose(bbi, (0, 2, 1))).astype(MXU),
            cre=_blockdiag(jnp.transpose(P['ssm_c_re'][l], (0, 2, 1))).astype(MXU),
            cim=_blockdiag(jnp.transpose(P['ssm_c_im'][l], (0, 2, 1))).astype(MXU),
            d=P['ssm_d'][l].reshape(1, 256), gb=P['glu_b'][l].reshape(1, 256),
            pw=_blockdiag(P['pool_w'][l]).astype(MXU), psc=P['pool_scale'][l].reshape(1, 256)))

    saved = []
    x = x0
    h = ln_mod_fwd(x, mod[0], 0, "ln_mod_fwd_l0s0")
    for l, sub in order:
        tag = f"l{l}s{sub}"
        after = (mod[nxt[(l, sub)][0]], nxt[(l, sub)][1]) if (l, sub) in nxt else None
        if sub != 1:
            wg, wu, wd = (t.reshape(NDEV * FBP, D) for t in W[(l, sub)])
            ride, units = gather_ride(60, nxt.get((l, sub)))
            (G, U, fo), got = ffn_fwd(h, wg, wu, wd, "ffn_fwd_" + tag, ride)
            gathered(units, got)
            saved.append(dict(x=x, h=h, G=G, U=U, f=fo))
            x, *hn = res_ln_fwd(x, fo, mod[l], sub, ln_g[l], ln_b[l], 0.5, "res_ln_fwd_" + tag, after)
        else:
            sp = ssm[l]
            win, wout, gluw = W[(l, sub)]
            ride, units = gather_ride(15)
            (z,), got = win_fwd(h, win, "win_fwd_" + tag, ride)
            gathered(units, got)
            ride, units = gather_ride(55)
            (ya, lse), got = att_fwd(z, bias, "att_fwd_" + tag, ride)
            gathered(units, got)
            ride, units = gather_ride(35)
            (ys, ypre, st), got = ssm_fwd(z, sp['a'], sp['bre'], sp['bim'], sp['cre'], sp['cim'], sp['d'], gluw, sp['gb'],
                                          "ssm_fwd_" + tag, ride)
            gathered(units, got)
            yp = pool_fwd(z, sp['pw'], sp['psc'], "pool_fwd_" + tag)
            ride, units = gather_ride(12, nxt.get((l, sub)))
            (o,), got = wout_fwd(ya, ys, yp, wout, "wout_fwd_" + tag, ride)
            gathered(units, got)
            saved.append(dict(x=x, h=h, z=z, ya=ya, lse=lse, ys=ys, ypre=ypre, st=st, yp=yp, f=o))
            x, *hn = res_ln_fwd(x, o, mod[l], sub, ln_g[l], ln_b[l], 1.0, "res_ln_fwd_" + tag, after)
        h = hn[0] if hn else None
    assert not gather_queue

    loss_tile, dx = loss_fwd_bwd(x, target, "loss")
    loss = lax.psum(loss_tile[0, 0], ("x", "y", "c"))

    flights = []

    dmod = [[None] * 9 for _ in range(DEPTH)]
    dlng = [[None] * 3 for _ in range(DEPTH)]
    dlnb = [[None] * 3 for _ in range(DEPTH)]
    dbiases = [None] * DEPTH
    small_l = [dict() for _ in range(DEPTH)]
    for l, sub in reversed(order):
        tag = f"l{l}s{sub}"
        sv = saved[3 * l + sub]
        if (l, sub) == order[-1]:
            dxa, df, sums = res_ln_bwd(sv['x'], sv['f'], mod[l], sub, ln_g[l], dx, 0.5, "res_ln_bwd_" + tag)
        dlng[l][sub], dlnb[l][sub], dmod[l][3 * sub + 2] = sums[0], sums[1], sums[2]
        if sub != 1:
            f = sub // 2
            wg, wu, wd = (t.reshape(NDEV * FBP, D) for t in W[(l, sub)])
            dwg, dwu, dwd, dh = ffn_bwd(df, sv['h'], sv['G'], sv['U'], wg, wu, wd, "ffn_bwd_" + tag)
            handle, zero = scatter_start([t.reshape(NDEV, FBP, D) for t in (dwg, dwu, dwd)], "scatter_start_" + tag)
            flights.append(((l, sub), handle))
        else:
            sp = ssm[l]
            win, wout, gluw = W[(l, sub)]
            dya, dys, dyp, dwout = wout_bwd(df, sv['ya'], sv['ys'], sv['yp'], wout, "wout_bwd_" + tag)
            dq, dk, dv, dbiases[l] = att_bwd(sv['z'], bias, sv['ya'], sv['lse'], dya, "att_bwd_" + tag)
            dus, dbre, dbim, dcre, dcim, da, dd, dgw, dgb = ssm_bwd(
                dys, sv['z'], sv['ypre'], sv['st'], sp['a'], sp['bre'], sp['bim'], sp['cre'], sp['cim'], sp['d'],
                gluw, sp['gb'], "ssm_bwd_" + tag)
            dup, dpw, dpsc = pool_bwd(dyp, sv['z'], sp['pw'], sp['psc'], "pool_bwd_" + tag)
            dh, dwin = win_bwd((dq, dk, dv, dus, dup), sv['h'], win, "win_bwd_" + tag)
            handle, zero = scatter_start([dwin, dwout, dgw.astype(BF16).reshape(NDEV, 32, 256)], "scatter_start_" + tag)
            flights.append(((l, sub), handle))
            d_are, d_aim, d_ldt, d_bre, d_bim = sp['vjp']((
                da[0].reshape(16, 64), da[1].reshape(16, 64),
                jnp.transpose(_blockdiag_take(dbre, 16, 64), (0, 2, 1)), jnp.transpose(_blockdiag_take(dbim, 16, 64), (0, 2, 1))))
            small_l[l] = dict(
                ssm_a_re=d_are, ssm_a_im=d_aim, ssm_log_dt=d_ldt, ssm_b_re=d_bre, ssm_b_im=d_bim,
                ssm_c_re=jnp.transpose(_blockdiag_take(dcre, 64, 16), (0, 2, 1)),
                ssm_c_im=jnp.transpose(_blockdiag_take(dcim, 64, 16), (0, 2, 1)),
                ssm_d=dd.reshape(256), glu_b=dgb.reshape(256), pool_w=_blockdiag_take(dpw, 64, 64), pool_scale=dpsc.reshape(256))
        if (l, sub) == order[0]:
            dx, sums2 = ln_mod_bwd(sv['x'], dh, mod[l] + zero, sub, dxa, "ln_mod_bwd_" + tag)
        else:
            lp, sp_ = order[order.index((l, sub)) - 1]
            svp = saved[3 * lp + sp_]
            dxa, df, sums, sums2 = ln_join_bwd(svp['x'], svp['f'], mod[lp], sp_, ln_g[lp], ln_b[lp], 1.0 if sp_ == 1 else 0.5,
                                               dh, mod[l] + zero, sub, dxa, "ln_join_bwd_" + tag)
        dmod[l][3 * sub], dmod[l][3 * sub + 1] = sums2[0], sums2[1]
    grad_x = dx[None]

    small = {k: jnp.stack([small_l[l][k] for l in range(DEPTH)]) for k in small_l[0]}
    small['rel_bias'] = relbias_grad(dbiases)
    small['ada_b'] = jnp.stack([jnp.stack(dmod[l]).reshape(9 * D) for l in range(DEPTH)])
    small['ln_g'] = jnp.stack([jnp.stack(dlng[l]) for l in range(DEPTH)])
    small['ln_b'] = jnp.stack([jnp.stack(dlnb[l]) for l in range(DEPTH)])
    (small_all,) = _exchange(Gather([_pack([small[k] for k in SMALL])]), "gather_small")

    out = {}

    def put(name, g, d, m2, v2, shape):
        out['grad_' + name], out['delta_' + name] = g.reshape(shape), d.reshape(shape)
        out['new_m_' + name], out['new_v_' + name] = m2.reshape(shape), v2.reshape(shape)

    def wmv(name):
        return [P[pre + name] for pre in ('', 'm_', 'v_')]

    recv = {}
    started_last = flights[-1][1][1][0]
    for key, handle in flights[:-1]:
        recv[key] = scatter_wait(handle, started_last, "scatter_wait_l%ds%d" % key)
    for pos, (name, tr) in enumerate((('w_in', 512), ('w_out', 128), ('glu_w', 32))):
        both = jnp.stack([recv[(l, 1)][pos] for l in range(DEPTH)], axis=1)
        put(name, *adam_rs(both, *wmv(name), tr, "adam_" + name), P[name].shape)
    ffn = (('ffn_w_gate', [jnp.swapaxes(t, 2, 3) for t in wmv('ffn_w_gate')]),
           ('ffn_w_up', [jnp.swapaxes(t, 2, 3) for t in wmv('ffn_w_up')]), ('ffn_w_down', wmv('ffn_w_down')))
    part = [None] * 3
    for l, sub in [key for key, _ in flights[:-1] if key[1] != 1]:
        for pos, (name, ops) in enumerate(ffn):
            part[pos] = adam_block(recv[(l, sub)][pos], *ops, 2 * l + sub // 2, part[pos], f"adam_{name}_l{l}s{sub}")
    (l, sub), handle = flights[-1]
    last = scatter_wait(handle, part[2][0], "scatter_wait_l%ds%d" % (l, sub))
    for pos, (name, ops) in enumerate(ffn):
        res = adam_block(last[pos], *ops, 2 * l + sub // 2, part[pos], f"adam_{name}_l{l}s{sub}")
        put(name, *([jnp.swapaxes(t, 2, 3) for t in res] if pos < 2 else res), P[name].shape)

    gsum = dict(zip(SMALL, _unpack(sum_sources(small_all, "sum_small"), [SMALL_FULL_SHAPES[k] for k in SMALL])))
    off = 256
    dmod_all = small_all.reshape(NDEV, -1)[:, off:off + DEPTH * 9 * D].reshape(NDEV, DEPTH, 9 * D)
    dmod_cols = jnp.transpose(lax.dynamic_slice_in_dim(dmod_all, me * 1152, 1152, axis=2), (1, 0, 2))
    g_ada_w = ada_bwd(c_all, dmod_cols)

    put('ada_w', g_ada_w, *adam_plain(g_ada_w, *wmv('ada_w'), 256, "adam_ada_w"), P['ada_w'].shape)

    for k in ('ln_g', 'ln_b'):
        gsum[k] = lax.dynamic_slice_in_dim(gsum[k], me * 128, 128, axis=2)
    swaps = {'rel_bias': (0, 1), 'ln_g': (0, 1), 'ln_b': (0, 1), 'ssm_b_re': (2, 3), 'ssm_b_im': (2, 3)}
    view = lambda k, t: jnp.swapaxes(t, *swaps[k]) if k in swaps else t
    ds_, m2s, v2s = adam_native(*[[view(k, src(k)) for k in SMALL] for src in
                                  (lambda k: gsum[k], lambda k: P[k], lambda k: P['m_' + k], lambda k: P['v_' + k])],
                                "adam_small")
    for k, d, m2, v2 in zip(SMALL, ds_, m2s, v2s):
        put(k, gsum[k], view(k, d), view(k, m2), view(k, v2), P[k].shape)

    res = [loss, grad_x]
    for pre in ('grad_', 'delta_', 'new_m_', 'new_v_'):
        res += [out[pre + k] for k in WEIGHTS]
    return tuple(res)


def kernel(x, c, rel_bias, ada_w, ada_b, ln_g, ln_b, ffn_w_gate, ffn_w_up, ffn_w_down, w_in, w_out, ssm_a_re, ssm_a_im, ssm_log_dt, ssm_b_re, ssm_b_im, ssm_c_re, ssm_c_im, ssm_d, glu_w, glu_b, pool_w, pool_scale, loss_target, m_rel_bias, m_ada_w, m_ada_b, m_ln_g, m_ln_b, m_ffn_w_gate, m_ffn_w_up, m_ffn_w_down, m_w_in, m_w_out, m_ssm_a_re, m_ssm_a_im, m_ssm_log_dt, m_ssm_b_re, m_ssm_b_im, m_ssm_c_re, m_ssm_c_im, m_ssm_d, m_glu_w, m_glu_b, m_pool_w, m_pool_scale, v_rel_bias, v_ada_w, v_ada_b, v_ln_g, v_ln_b, v_ffn_w_gate, v_ffn_w_up, v_ffn_w_down, v_w_in, v_w_out, v_ssm_a_re, v_ssm_a_im, v_ssm_log_dt, v_ssm_b_re, v_ssm_b_im, v_ssm_c_re, v_ssm_c_im, v_ssm_d, v_glu_w, v_glu_b, v_pool_w, v_pool_scale):
    return _step(dict(locals()))
```

```python
import functools
import math

import numpy as np
import jax
import jax.numpy as jnp
from jax import lax
from jax.experimental import pallas as pl
from jax.experimental.pallas import tpu as pltpu

F32 = jnp.float32
BF16 = jnp.bfloat16
MXU = jnp.bfloat16

S = 2048
D = 1024
NDEV = 8
DEPTH = 2
D_ATT, D_SSM, D_POOL, D_IN = 512, 256, 256, 2048
N_HEADS = 8
FB = 352
FBP = 384
QB = 128
PATTERNS = ((128, 1), (512, 4), (2048, 16))
POOL_WINDOWS = (2, 4, 8, 16)
N_BUCKETS, MAX_DISTANCE = 32, 2048
ALPHA = (2 * DEPTH) ** 0.25
LN_EPS = 1e-5
NEG = -1e30
GATHER_US_PER_BYTE = 43e-6
LR, B1, B2, EPS, WD, STEP = 0.001, 0.9, 0.999, 1e-08, 0.01, 10

TM = 256
TMM = 512
MIB = 1024 * 1024


def _cp(vmem_mib, sem=None):
    kw = dict(vmem_limit_bytes=vmem_mib * MIB)
    if sem is not None:
        kw["dimension_semantics"] = sem
    return pltpu.CompilerParams(**kw)


def _sds(shape, dtype):
    return jax.ShapeDtypeStruct(shape, dtype)


def _mm(a, b):
    return jnp.dot(a.astype(MXU), b.astype(MXU), preferred_element_type=F32)


def _mm_nt(a, b):
    return lax.dot_general(a.astype(MXU), b.astype(MXU), (((1,), (1,)), ((), ())), preferred_element_type=F32)


def _mm_tn(a, b):
    return lax.dot_general(a.astype(MXU), b.astype(MXU), (((0,), (0,)), ((), ())), preferred_element_type=F32)


def _ln_stats(x):
    mu = jnp.mean(x, axis=-1, keepdims=True)
    xc = x - mu
    var = jnp.mean(xc * xc, axis=-1, keepdims=True)
    rstd = lax.rsqrt(var + LN_EPS)
    return xc * rstd, rstd


def _ln_bwd(dn, n, rstd):
    return rstd * (dn - jnp.mean(dn, axis=-1, keepdims=True) - n * jnp.mean(dn * n, axis=-1, keepdims=True))


def _me():
    return 4 * lax.axis_index("x") + 2 * lax.axis_index("y") + lax.axis_index("c")


ANY = pl.BlockSpec(memory_space=pl.ANY)
PIN_BYTES = 1 << 19


def _pallas_call(*a, **k):
    big = lambda o: math.prod(o.shape) * o.dtype.itemsize >= PIN_BYTES
    pin = lambda o: pltpu.HBM(o.shape, o.dtype) if isinstance(o, jax.ShapeDtypeStruct) and big(o) else o
    osh = k["out_shape"]
    k["out_shape"] = tuple(pin(o) for o in osh) if isinstance(osh, (tuple, list)) else pin(osh)
    fn = pl.pallas_call(*a, **k)

    def run(*args):
        return fn(*[pltpu.with_memory_space_constraint(x, pltpu.HBM) if big(x) else x for x in args])
    return run


class Gather:
    def __init__(self, srcs):
        self.srcs = list(srcs)
        self.n = len(self.srcs)
        self.bufs = []
        self.out_shapes = [_sds((NDEV,) + a.shape, a.dtype) for a in self.srcs]
        self.sems = [pltpu.SemaphoreType.DMA((7 * self.n,)), pltpu.SemaphoreType.DMA((7 * self.n,)),
                     pltpu.SemaphoreType.DMA((self.n,))]

    def _parts(self, srcs, outs, sems):
        send_sems, recv_sems, loc_sems = sems
        x, y, c = lax.axis_index("x"), lax.axis_index("y"), lax.axis_index("c")
        me, sib = (x, y, c), (x, y, 1 - c)
        chips = [(1 - x, y), (x, 1 - y), (1 - x, 1 - y)]
        slot = lambda d: 4 * d[0] + 2 * d[1] + d[2]

        def copy(a, k, block, to, src=None):
            dst = outs[a].at[slot(block)]
            return pltpu.make_async_remote_copy(
                src_ref=dst if src is None else src, dst_ref=dst,
                send_sem=send_sems.at[7 * a + k], recv_sem=recv_sems.at[7 * a + k],
                device_id=to, device_id_type=pl.DeviceIdType.MESH)

        local = [pltpu.make_async_copy(srcs[a], outs[a].at[slot(me)], loc_sems.at[a]) for a in range(self.n)]
        return me, sib, chips, c, copy, local

    def start(self, srcs, bufs, outs, sems):
        me, sib, chips, c, copy, local = self._parts(srcs, outs, sems)
        for a in range(self.n):
            local[a].start()
            copy(a, 0, me, sib, src=srcs[a]).start()
            for j, chip in enumerate(chips):
                copy(a, 1 + j, me, (*chip, c), src=srcs[a]).start()

    def finish(self, srcs, bufs, outs, sems):
        me, sib, chips, c, copy, local = self._parts(srcs, outs, sems)
        for a in range(self.n):
            for j, chip in enumerate(chips):
                copy(a, 1 + j, (*chip, c), me).wait_recv()
                copy(a, 4 + j, (*chip, c), sib).start()
        for a in range(self.n):
            copy(a, 0, sib, me).wait_recv()
            copy(a, 0, me, sib, src=srcs[a]).wait_send()
            for j, chip in enumerate(chips):
                copy(a, 4 + j, (*chip, 1 - c), me).wait_recv()
                copy(a, 1 + j, me, (*chip, c), src=srcs[a]).wait_send()
                copy(a, 4 + j, (*chip, c), sib).wait_send()
            local[a].wait()


def _call(body, *, name, grid, in_specs, out_specs, out_shape, args, scratch=(), cp=None, ride=None):
    out_specs, out_shape, scratch = list(out_specs), list(out_shape), list(scratch)
    if ride is None:
        outs = _pallas_call(body, name=name, grid=grid, in_specs=list(in_specs), out_specs=tuple(out_specs),
                              out_shape=tuple(out_shape), scratch_shapes=scratch, compiler_params=cp)(*args)
        return list(outs), []
    nin, nout, nscr, n, nb, no = len(in_specs), len(out_specs), len(scratch), ride.n, len(ride.bufs), len(ride.out_shapes)
    steps = list(grid)

    def wrapped(*refs):
        h_in, r_src, r_buf = refs[:nin], refs[nin:nin + n], refs[nin + n:nin + n + nb]
        o0 = nin + n + nb
        h_out, r_out = refs[o0:o0 + nout], refs[o0 + nout:o0 + nout + no]
        s0 = o0 + nout + no
        h_scr, sems = refs[s0:s0 + nscr], refs[s0 + nscr:]
        ids = [pl.program_id(a) for a in range(len(steps))]
        first = functools.reduce(jnp.logical_and, [i == 0 for i in ids])
        last = functools.reduce(jnp.logical_and, [i == s - 1 for i, s in zip(ids, steps)])

        @pl.when(first)
        def _():
            ride.start(r_src, r_buf, r_out, sems)

        body(*h_in, *h_out, *h_scr)

        @pl.when(last)
        def _():
            ride.finish(r_src, r_buf, r_out, sems)

    aliases = {nin + n + k: nout + k for k in range(nb)}
    outs = _pallas_call(
        wrapped, name=name, grid=grid, in_specs=list(in_specs) + [ANY] * (n + nb),
        out_specs=tuple(out_specs + [ANY] * no), out_shape=tuple(out_shape + ride.out_shapes),
        scratch_shapes=scratch + ride.sems, input_output_aliases=aliases, compiler_params=cp,
    )(*args, *ride.srcs, *ride.bufs)
    return list(outs[:nout]), list(outs[nout:])


def _exchange(ride, name):
    def body(dummy_ref, o_ref):
        o_ref[...] = dummy_ref[...]

    one = pl.BlockSpec((8, 128), lambda i: (0, 0))
    _, outs = _call(body, name=name, grid=(1,), in_specs=[one], out_specs=[one], out_shape=[_sds((8, 128), F32)],
                    args=(jnp.zeros((8, 128), F32),), ride=ride)
    return outs


HBM = pl.BlockSpec(memory_space=pltpu.HBM)
SEM = pl.BlockSpec(memory_space=pltpu.SEMAPHORE)


def routes_all(me):
    return [(k, me ^ k, me, me ^ k) for k in range(NDEV)]


def _scatter_copies(srcs, lands, sems, sending, whole):
    send_sems, recv_sems, loc_sems = sems
    me = _me()
    rts = routes_all(me)
    remote_ix = [r for r, (k, _, _, _) in enumerate(rts) if k != 0]
    local_ix = [r for r, (k, _, _, _) in enumerate(rts) if k == 0]
    remote, local = [], []
    for a in range(len(srcs)):
        for n, r in enumerate(remote_ix):
            k, slab, there, here = rts[r]
            t = me ^ k
            sem = len(remote_ix) * a + n
            remote.append(pltpu.make_async_remote_copy(
                src_ref=srcs[a] if whole else srcs[a].at[slab], dst_ref=lands[a].at[there if sending else here], send_sem=send_sems.at[sem],
                recv_sem=recv_sems.at[sem], device_id=(t // 4, (t // 2) % 2, t % 2), device_id_type=pl.DeviceIdType.MESH))
        for n, r in enumerate(local_ix):
            _, slab, there, _ = rts[r]
            local.append(pltpu.make_async_copy(srcs[a] if whole else srcs[a].at[slab], lands[a].at[there], loc_sems.at[len(local_ix) * a + n]))
    return remote, local


def scatter_start(payloads, name, whole=False):
    n = len(payloads)
    nr = NDEV - 1

    def body(*refs):
        srcs, lands, sems = refs[:n], refs[n:2 * n], refs[2 * n:2 * n + 3]
        remote, local = _scatter_copies(srcs, lands, sems, True, whole)
        for cp in local + remote:
            cp.start()
        refs[-1][...] = jnp.zeros((8, 128), F32)

    thru = [pltpu.HBM(p.shape, p.dtype) for p in payloads]
    land_shapes = [(NDEV,) + p.shape if whole else p.shape for p in payloads]
    outs = pl.pallas_call(
        body, name=name,
        out_shape=(pltpu.SemaphoreType.DMA((nr * n,)), pltpu.SemaphoreType.DMA((nr * n,)), pltpu.SemaphoreType.DMA((n,)),
                   *thru, *[pltpu.HBM(sh, p.dtype) for sh, p in zip(land_shapes, payloads)], _sds((8, 128), F32)),
        in_specs=[HBM] * (2 * n),
        out_specs=(SEM, SEM, SEM, *[HBM] * (2 * n), pl.BlockSpec(memory_space=pltpu.VMEM)),
        input_output_aliases={i: 3 + i for i in range(2 * n)},
        compiler_params=pltpu.CompilerParams(has_side_effects=pltpu.SideEffectType.DATAFLOW_SIDE_EFFECTING),
    )(*[pltpu.with_memory_space_constraint(p, pltpu.HBM) for p in payloads],
      *[pltpu.with_memory_space_constraint(lax.empty(sh, p.dtype), pltpu.HBM) for sh, p in zip(land_shapes, payloads)])
    return (outs[:3], outs[3:3 + n], outs[3 + n:3 + 2 * n], whole), outs[-1][0, 0]


def scatter_wait(handle, after, name):
    sems, srcs_thru, lands_thru, whole = handle
    n = len(srcs_thru)

    def body(*refs):
        srcs, lands, sems_ = refs[:n], refs[n:2 * n], refs[2 * n:2 * n + 3]
        remote, local = _scatter_copies(srcs, lands, sems_, False, whole)
        for cp in remote:
            cp.wait_send()
            cp.wait_recv()
        for cp in local:
            cp.wait()

    outs = pl.pallas_call(
        body, name=name, out_shape=tuple(pltpu.HBM(p.shape, p.dtype) for p in (*srcs_thru, *lands_thru)),
        in_specs=[HBM] * (2 * n) + [SEM] * 3 + [HBM], out_specs=tuple([HBM] * (2 * n)),
        input_output_aliases={i: i for i in range(2 * n)},
        compiler_params=pltpu.CompilerParams(has_side_effects=pltpu.SideEffectType.DATAFLOW_SIDE_EFFECTING),
    )(*srcs_thru, *lands_thru, *sems, pltpu.with_memory_space_constraint(after, pltpu.HBM))
    return list(outs[n:])


def _row_spec(cols, tm=TM):
    return pl.BlockSpec((tm, cols), lambda i: (i, 0))


def _full_spec(shape):
    nd = len(shape)
    return pl.BlockSpec(shape, lambda i: (0,) * nd)


def ln_mod_fwd(x, mod, sub, name):
    def body(x_ref, mod_ref, h_ref):
        n, _ = _ln_stats(x_ref[...])
        shift = mod_ref[3 * sub:3 * sub + 1, :]
        scale = mod_ref[3 * sub + 1:3 * sub + 2, :]
        h_ref[...] = (n * (1.0 + scale) + shift).astype(MXU)

    return _pallas_call(
        body, name=name, grid=(S // TM,),
        in_specs=[_row_spec(D), _full_spec((9, D))], out_specs=_row_spec(D),
        out_shape=_sds((S, D), MXU), compiler_params=_cp(32, ("arbitrary",)))(x, mod)


def res_ln_fwd(x, f, mod, sub, lng, lnb, w, name, nxt=None):
    def body(x_ref, f_ref, mod_ref, g_ref, b_ref, *rest):
        gate = mod_ref[3 * sub + 2:3 * sub + 3, :]
        r = ALPHA * x_ref[...] + (w * gate) * f_ref[...]
        n, _ = _ln_stats(r)
        xo = n * g_ref[sub:sub + 1, :] + b_ref[sub:sub + 1, :]
        rest[-1 if nxt is None else -2][...] = xo
        if nxt is not None:
            nmod_ref, h_ref = rest[0], rest[-1]
            n2, _ = _ln_stats(xo)
            s2 = nxt[1]
            h_ref[...] = (n2 * (1.0 + nmod_ref[3 * s2 + 1:3 * s2 + 2, :]) + nmod_ref[3 * s2:3 * s2 + 1, :]).astype(MXU)

    more = nxt is not None
    return _pallas_call(
        body, name=name, grid=(S // TM,),
        in_specs=[_row_spec(D), _row_spec(D), _full_spec((9, D)), _full_spec((3, D)), _full_spec((3, D))] + [_full_spec((9, D))] * more,
        out_specs=(_row_spec(D),) + (_row_spec(D),) * more, out_shape=(_sds((S, D), F32),) + (_sds((S, D), MXU),) * more,
        compiler_params=_cp(32, ("arbitrary",)))(x, f, mod, lng, lnb, *([nxt[0]] if more else []))


def res_ln_bwd(x, f, mod, sub, lng, dxo, w, name):
    def body(x_ref, f_ref, mod_ref, g_ref, dxo_ref, dxa_ref, df_ref, sums_ref):
        i = pl.program_id(0)
        gate = mod_ref[3 * sub + 2:3 * sub + 3, :]
        fv = f_ref[...]
        r = ALPHA * x_ref[...] + (w * gate) * fv
        n, rstd = _ln_stats(r)
        dxo = dxo_ref[...]
        dr = _ln_bwd(dxo * g_ref[sub:sub + 1, :], n, rstd)
        dxa_ref[...] = ALPHA * dr
        df_ref[...] = ((w * gate) * dr).astype(MXU)
        part = jnp.concatenate([
            jnp.sum(dxo * n, axis=0, keepdims=True),
            jnp.sum(dxo, axis=0, keepdims=True),
            jnp.sum(dr * fv, axis=0, keepdims=True) * w,
            jnp.zeros((5, D), F32)], axis=0)

        @pl.when(i == 0)
        def _():
            sums_ref[...] = part

        @pl.when(i > 0)
        def _():
            sums_ref[...] += part

    return _call(
        body, name=name, grid=(S // TM,),
        in_specs=[_row_spec(D), _row_spec(D), _full_spec((9, D)), _full_spec((3, D)), _row_spec(D)],
        out_specs=(_row_spec(D), _row_spec(D), _full_spec((8, D))),
        out_shape=(_sds((S, D), F32), _sds((S, D), MXU), _sds((8, D), F32)),
        cp=_cp(32, ("arbitrary",)), args=(x, f, mod, lng, dxo))[0]


def ln_mod_bwd(x, dh, mod, sub, dxa, name):
    def body(x_ref, dh_ref, mod_ref, dxa_ref, dx_ref, sums_ref):
        i = pl.program_id(0)
        scale = mod_ref[3 * sub + 1:3 * sub + 2, :]
        n, rstd = _ln_stats(x_ref[...])
        dh = dh_ref[...]
        dx_ref[...] = dxa_ref[...] + _ln_bwd(dh * (1.0 + scale), n, rstd)
        part = jnp.concatenate([
            jnp.sum(dh, axis=0, keepdims=True),
            jnp.sum(dh * n, axis=0, keepdims=True),
            jnp.zeros((6, D), F32)], axis=0)

        @pl.when(i == 0)
        def _():
            sums_ref[...] = part

        @pl.when(i > 0)
        def _():
            sums_ref[...] += part

    return _call(
        body, name=name, grid=(S // TM,),
        in_specs=[_row_spec(D), _row_spec(D), _full_spec((9, D)), _row_spec(D)],
        out_specs=(_row_spec(D), _full_spec((8, D))),
        out_shape=(_sds((S, D), F32), _sds((8, D), F32)),
        cp=_cp(32, ("arbitrary",)), args=(x, dh, mod, dxa))[0]


def ln_join_bwd(xp, fp, modp, subp, lngp, lnbp, wp, dh, mod, sub, dxa, name):
    def body(xp_ref, fp_ref, modp_ref, g_ref, b_ref, dh_ref, mod_ref, dxa_ref, dxap_ref, dfp_ref, sumsp_ref, sums_ref):
        i = pl.program_id(0)
        gate = modp_ref[3 * subp + 2:3 * subp + 3, :]
        fv = fp_ref[...]
        n, rstd = _ln_stats(ALPHA * xp_ref[...] + (wp * gate) * fv)
        gain = g_ref[subp:subp + 1, :]
        n2, rstd2 = _ln_stats(n * gain + b_ref[subp:subp + 1, :])
        dh = dh_ref[...]
        dx = dxa_ref[...] + _ln_bwd(dh * (1.0 + mod_ref[3 * sub + 1:3 * sub + 2, :]), n2, rstd2)
        dr = _ln_bwd(dx * gain, n, rstd)
        dxap_ref[...] = ALPHA * dr
        dfp_ref[...] = ((wp * gate) * dr).astype(MXU)
        partp = jnp.concatenate([
            jnp.sum(dx * n, axis=0, keepdims=True), jnp.sum(dx, axis=0, keepdims=True),
            jnp.sum(dr * fv, axis=0, keepdims=True) * wp, jnp.zeros((5, D), F32)], axis=0)
        part = jnp.concatenate([
            jnp.sum(dh, axis=0, keepdims=True), jnp.sum(dh * n2, axis=0, keepdims=True), jnp.zeros((6, D), F32)], axis=0)

        @pl.when(i == 0)
        def _():
            sumsp_ref[...] = partp
            sums_ref[...] = part

        @pl.when(i > 0)
        def _():
            sumsp_ref[...] += partp
            sums_ref[...] += part

    return _pallas_call(
        body, name=name, grid=(S // TM,),
        in_specs=[_row_spec(D), _row_spec(D), _full_spec((9, D)), _full_spec((3, D)), _full_spec((3, D)), _row_spec(D),
                  _full_spec((9, D)), _row_spec(D)],
        out_specs=(_row_spec(D), _row_spec(D), _full_spec((8, D)), _full_spec((8, D))),
        out_shape=(_sds((S, D), F32), _sds((S, D), MXU), _sds((8, D), F32), _sds((8, D), F32)),
        compiler_params=_cp(40, ("arbitrary",)))(xp, fp, modp, lngp, lnbp, dh, mod, dxa)


def loss_fwd_bwd(y, target, name):
    def body(y_ref, t_ref, l_ref, dy_ref):
        i = pl.program_id(0)
        e = y_ref[...] - t_ref[...]
        dy_ref[...] = e * (1.0 / D)
        part = jnp.zeros((8, 128), F32) + (0.5 / D) * jnp.sum(e * e)

        @pl.when(i == 0)
        def _():
            l_ref[...] = part

        @pl.when(i > 0)
        def _():
            l_ref[...] += part

    return _pallas_call(
        body, name=name, grid=(S // TM,),
        in_specs=[_row_spec(D), _row_spec(D)], out_specs=(_full_spec((8, 128)), _row_spec(D)),
        out_shape=(_sds((8, 128), F32), _sds((S, D), F32)),
        compiler_params=_cp(32, ("arbitrary",)))(y, target)


HB = 2 * FBP
NHB = NDEV * FBP // HB
TMB = 1024


def _wrows(buffers=2):
    return pl.BlockSpec((HB, D), lambda j, i: (j, 0), pipeline_mode=pl.Buffered(buffers))


def _resident(shape):
    return pl.BlockSpec(shape, lambda j, i: (0, 0), pipeline_mode=pl.Buffered(1))


def ffn_fwd(h, wgt, wut, wd, name, ride=None):
    def body(h_ref, wg_ref, wu_ref, wd_ref, g_ref, u_ref, f_ref):
        j, i = pl.program_id(0), pl.program_id(1)
        hv = h_ref[...]
        g = _mm_nt(hv, wg_ref[...])
        u = _mm_nt(hv, wu_ref[...])
        g_ref[...] = g.astype(MXU)
        u_ref[...] = u.astype(MXU)
        a = g * jax.nn.sigmoid(g) * u
        part = _mm(a, wd_ref[...])
        rows = pl.ds(pl.multiple_of(i * TMB, TMB), TMB)

        @pl.when(j == 0)
        def _():
            f_ref[rows, :] = part

        @pl.when(j > 0)
        def _():
            f_ref[rows, :] += part

    gu = pl.BlockSpec((TMB, HB), lambda j, i: (i, j))
    return _call(
        body, name=name, grid=(NHB, S // TMB),
        in_specs=[pl.BlockSpec((TMB, D), lambda j, i: (i, 0)), _wrows(), _wrows(), _wrows()],
        out_specs=(gu, gu, _resident((S, D))),
        out_shape=(_sds((S, NDEV * FBP), MXU), _sds((S, NDEV * FBP), MXU), _sds((S, D), F32)),
        cp=_cp(52, ("arbitrary", "arbitrary")), args=(h, wgt, wut, wd), ride=ride)


def ffn_bwd(df, h, g, u, wgt, wut, wd, name):
    ni = S // TMB

    def body(df_ref, h_ref, g_ref, u_ref, wg_ref, wu_ref, wd_ref, dwg_ref, dwu_ref, dwd_ref, dh_ref,
             ag_ref, au_ref, ad_ref):
        j, i = pl.program_id(0), pl.program_id(1)
        dfv, hv = df_ref[...], h_ref[...]
        gv, uv = g_ref[...].astype(F32), u_ref[...].astype(F32)
        da = _mm_nt(dfv, wd_ref[...])
        sg = jax.nn.sigmoid(gv)
        silu = gv * sg
        du = da * silu
        dg = da * uv * (sg * (1.0 + gv * (1.0 - sg)))
        p_d = _mm_tn(silu * uv, dfv)
        p_g = _mm_tn(dg, hv)
        p_u = _mm_tn(du, hv)

        @pl.when(i == 0)
        def _():
            ad_ref[...] = p_d
            ag_ref[...] = p_g
            au_ref[...] = p_u

        @pl.when(i > 0)
        def _():
            ad_ref[...] += p_d
            ag_ref[...] += p_g
            au_ref[...] += p_u

        @pl.when(i == ni - 1)
        def _():
            dwd_ref[...] = ad_ref[...].astype(BF16)
            dwg_ref[...] = ag_ref[...].astype(BF16)
            dwu_ref[...] = au_ref[...].astype(BF16)

        part = _mm(dg, wg_ref[...]) + _mm(du, wu_ref[...])
        rows = pl.ds(pl.multiple_of(i * TMB, TMB), TMB)

        @pl.when(j == 0)
        def _():
            dh_ref[rows, :] = part

        @pl.when(j > 0)
        def _():
            dh_ref[rows, :] += part

    gu = pl.BlockSpec((TMB, HB), lambda j, i: (i, j))
    rowt = pl.BlockSpec((TMB, D), lambda j, i: (i, 0))
    return _call(
        body, name=name, grid=(NHB, ni),
        in_specs=[rowt, rowt, gu, gu, _wrows(1), _wrows(1), _wrows(1)],
        out_specs=(_wrows(1), _wrows(1), _wrows(1), _resident((S, D))),
        out_shape=(_sds((NDEV * FBP, D), BF16), _sds((NDEV * FBP, D), BF16), _sds((NDEV * FBP, D), BF16), _sds((S, D), F32)),
        scratch=[pltpu.VMEM((HB, D), F32), pltpu.VMEM((HB, D), F32), pltpu.VMEM((HB, D), F32)],
        cp=_cp(60, ("arbitrary", "arbitrary")), args=(df, h, g, u, wgt, wut, wd))[0]


def win_fwd(h, win, name, ride=None):
    def body(h_ref, w_ref, z_ref):
        hv = h_ref[...]
        for j in range(NDEV):
            z_ref[:, 256 * j:256 * (j + 1)] = _mm(hv, w_ref[j])

    return _call(
        body, name=name, grid=(S // TMM,),
        in_specs=[_row_spec(D, TMM), _full_spec((NDEV, D, 256))],
        out_specs=[_row_spec(D_IN, TMM)], out_shape=[_sds((S, D_IN), F32)],
        cp=_cp(40, ("arbitrary",)), args=(h, win), ride=ride)


def win_bwd(dparts, h, win, name):
    ni = S // TMM

    def body(dq_ref, dk_ref, dv_ref, dus_ref, dup_ref, h_ref, w_ref, dh_ref, dw_ref, acc_ref):
        i = pl.program_id(0)
        hv = h_ref[...]
        cols = [dq_ref[:, 0:256], dq_ref[:, 256:512], dk_ref[:, 0:256], dk_ref[:, 256:512],
                dv_ref[:, 0:256], dv_ref[:, 256:512], dus_ref[...], dup_ref[...]]
        dh = jnp.zeros((TMM, D), F32)
        for j in range(NDEV):
            dz = cols[j].astype(MXU)
            dh = dh + _mm_nt(dz, w_ref[j])
            p = _mm_tn(hv, dz)

            @pl.when(i == 0)
            def _():
                acc_ref[j] = p

            @pl.when(i > 0)
            def _():
                acc_ref[j] += p

        dh_ref[...] = dh

        @pl.when(i == ni - 1)
        def _():
            dw_ref[...] = acc_ref[...].astype(BF16)

    return _call(
        body, name=name, grid=(ni,),
        in_specs=[_row_spec(512, TMM), _row_spec(512, TMM), _row_spec(512, TMM), _row_spec(256, TMM), _row_spec(256, TMM),
                  _row_spec(D, TMM), _full_spec((NDEV, D, 256))],
        out_specs=(_row_spec(D, TMM), _full_spec((NDEV, D, 256))),
        out_shape=(_sds((S, D), F32), _sds((NDEV, D, 256), BF16)),
        scratch=[pltpu.VMEM((NDEV, D, 256), F32)],
        cp=_cp(48, ("arbitrary",)), args=(*dparts, h, win))[0]


def wout_fwd(ya, ys, yp, wout, name, ride=None):
    def body(ya_ref, ys_ref, yp_ref, w_ref, o_ref):
        w = w_ref[...].reshape(D, D)
        o_ref[...] = _mm(ya_ref[...], w[0:512]) + _mm(ys_ref[...], w[512:768]) + _mm(yp_ref[...], w[768:1024])

    return _call(
        body, name=name, grid=(S // TMM,),
        in_specs=[_row_spec(512, TMM), _row_spec(256, TMM), _row_spec(256, TMM), _full_spec((NDEV, 128, D))],
        out_specs=[_row_spec(D, TMM)], out_shape=[_sds((S, D), F32)],
        cp=_cp(40, ("arbitrary",)), args=(ya, ys, yp, wout), ride=ride)


def wout_bwd(do, ya, ys, yp, wout, name):
    ni = S // TMM

    def body(do_ref, ya_ref, ys_ref, yp_ref, w_ref, dya_ref, dys_ref, dyp_ref, dw_ref, acc_ref):
        i = pl.program_id(0)
        w = w_ref[...].reshape(D, D)
        dov = do_ref[...]
        dya_ref[...] = _mm_nt(dov, w[0:512])
        dys_ref[...] = _mm_nt(dov, w[512:768])
        dyp_ref[...] = _mm_nt(dov, w[768:1024])
        parts = [(0, 512, _mm_tn(ya_ref[...], dov)), (512, 768, _mm_tn(ys_ref[...], dov)),
                 (768, 1024, _mm_tn(yp_ref[...], dov))]
        for lo, hi, p in parts:
            @pl.when(i == 0)
            def _():
                acc_ref[lo:hi, :] = p

            @pl.when(i > 0)
            def _():
                acc_ref[lo:hi, :] += p

        @pl.when(i == ni - 1)
        def _():
            dw_ref[...] = acc_ref[...].astype(BF16).reshape(NDEV, 128, D)

    return _call(
        body, name=name, grid=(ni,),
        in_specs=[_row_spec(D, TMM), _row_spec(512, TMM), _row_spec(256, TMM), _row_spec(256, TMM),
                  _full_spec((NDEV, 128, D))],
        out_specs=(_row_spec(512, TMM), _row_spec(256, TMM), _row_spec(256, TMM), _full_spec((NDEV, 128, D))),
        out_shape=(_sds((S, 512), F32), _sds((S, 256), F32), _sds((S, 256), F32), _sds((NDEV, 128, D), BF16)),
        scratch=[pltpu.VMEM((D, D), F32)],
        cp=_cp(40, ("arbitrary",)), args=(do, ya, ys, yp, wout))[0]


def _t5_bucket(dist):
    max_exact = N_BUCKETS // 2
    d = np.maximum(dist, 1).astype(np.float32)
    large = max_exact + (np.log(d / max_exact) / math.log(MAX_DISTANCE / max_exact)
                         * (N_BUCKETS - max_exact)).astype(np.int32)
    large = np.minimum(large, N_BUCKETS - 1)
    return np.where(dist < max_exact, dist, large).astype(np.int32)


def _att_static():
    i = np.arange(QB)[:, None]
    j = np.arange(2 * QB)[None, :]
    r = i + QB - j
    buckets, bands = [], []
    for window, dil in PATTERNS:
        bands.append((r >= 0) & (r <= window // dil))
        buckets.append(_t5_bucket(np.clip(r, 0, None) * dil))
    return np.stack(buckets), np.stack(bands), np.broadcast_to(j >= QB, (QB, 2 * QB))


def att_bias(rel_bias):
    m = np.arange(2 * QB)
    rows = []
    for window, dil in PATTERNS:
        r = QB - m
        ok = (r >= 0) & (r <= window // dil)
        b = rel_bias[_t5_bucket(np.clip(r, 0, None) * dil)]
        rows.append(jnp.where(ok[:, None], b, NEG).T)
    return jnp.broadcast_to(jnp.stack(rows)[:, :, None, :], (3, N_HEADS, 8, 2 * QB))


def _bias_tiles(t_ref, tiles):
    col = lax.broadcasted_iota(jnp.int32, (QB, 2 * QB), 1)
    for p in range(3):
        for hh in range(2):
            t = pltpu.roll(jnp.broadcast_to(t_ref[p, hh, 0:1, :], (QB, 2 * QB)), 0, 1, stride=1, stride_axis=0)
            tiles[p, hh, 0] = t
            tiles[p, hh, 1] = jnp.where(col >= QB, t, NEG)


def _permute_in(dst_ref, src_ref, d, scale=None, pad=QB):
    L = S // d
    for r in range(d):
        v = src_ref[pl.ds(r, L, stride=d), :] if d > 1 else src_ref[...]
        if scale is not None:
            v = v * scale
        dst_ref[pad + r * L:pad + (r + 1) * L, :] = v.astype(dst_ref.dtype)


def att_fwd(z, bias, name, ride=None):
    def body(q_ref, k_ref, v_ref, t_ref, y_ref, l_ref, qs, ks, vs, o_perm, l_perm, o_nat, l_nat, b_ref):
        _bias_tiles(t_ref, b_ref)
        zero_pad = jnp.zeros((QB, 128), MXU)
        ks[0:QB, :] = zero_pad
        vs[0:QB, :] = zero_pad
        lane = lax.broadcasted_iota(jnp.int32, (QB, 128), 1)
        for p, (_, d) in enumerate(PATTERNS):
            L = S // d
            nb = L // QB
            _permute_in(qs, q_ref, d, scale=0.125, pad=0)
            _permute_in(ks, k_ref, d)
            _permute_in(vs, v_ref, d)

            def blk(b, carry):
                r0 = pl.multiple_of(b * QB, QB)
                q = qs[pl.ds(r0, QB), :]
                kb = ks[pl.ds(r0, 2 * QB), :]
                vb = vs[pl.ds(r0, 2 * QB), :]
                first = ((b % nb) == 0).astype(jnp.int32)
                res = []
                for hh in range(2):
                    sel = (lane < 64) if hh == 0 else (lane >= 64)
                    qm = jnp.where(sel, q, jnp.zeros_like(q))
                    s = _mm_nt(qm, kb) + b_ref[p, hh, first]
                    m = jnp.max(s, axis=1, keepdims=True)
                    pe = jnp.exp(s - m)
                    den = jnp.sum(pe, axis=1, keepdims=True)
                    res.append((_mm(pe, vb) / den, m + jnp.log(den)))
                o_perm[pl.ds(r0, QB), :] = jnp.where(lane < 64, res[0][0], res[1][0])
                l_perm[pl.ds(r0, QB), :] = jnp.where(lane < 64, res[0][1], res[1][1])
                return carry

            lax.fori_loop(0, S // QB, blk, 0, unroll=8)
            for r in range(d):
                if d > 1:
                    o_nat[p, pl.ds(r, L, stride=d), :] = o_perm[r * L:(r + 1) * L, :]
                    l_nat[p, pl.ds(r, L, stride=d), :] = l_perm[r * L:(r + 1) * L, :]
                else:
                    o_nat[p] = o_perm[...]
                    l_nat[p] = l_perm[...]
        l0, l1, l2 = l_nat[0], l_nat[1], l_nat[2]
        m = jnp.maximum(jnp.maximum(l0, l1), l2)
        e0, e1, e2 = jnp.exp(l0 - m), jnp.exp(l1 - m), jnp.exp(l2 - m)
        den = e0 + e1 + e2
        y_ref[...] = (e0 * o_nat[0] + e1 * o_nat[1] + e2 * o_nat[2]) / den
        l_ref[...] = m + jnp.log(den)

    col = lambda c0: pl.BlockSpec((S, 128), lambda hp: (0, c0 + hp))
    return _call(
        body, name=name, grid=(N_HEADS // 2,),
        in_specs=[col(0), col(4), col(8), pl.BlockSpec((3, 2, 8, 2 * QB), lambda hp: (0, hp, 0, 0))],
        out_specs=(col(0), col(0)),
        out_shape=(_sds((S, D_ATT), F32), _sds((S, D_ATT), F32)),
        scratch=[pltpu.VMEM((S, 128), MXU), pltpu.VMEM((S + QB, 128), MXU), pltpu.VMEM((S + QB, 128), MXU),
                 pltpu.VMEM((S, 128), F32), pltpu.VMEM((S, 128), F32),
                 pltpu.VMEM((3, S, 128), F32), pltpu.VMEM((3, S, 128), F32),
                 pltpu.VMEM((3, 2, 2, QB, 2 * QB), F32)],
        cp=_cp(40, ("arbitrary",)), args=(z, z, z, bias), ride=ride)


def att_bwd(z, bias, y, lse, dy, name):
    def body(q_ref, k_ref, v_ref, t_ref, y_ref, l_ref, dy_ref, dq_ref, dk_ref, dv_ref, db_ref,
             qs, ks, vs, dys, ls, dds, dn_nat, dq_perm, dk_perm, dv_perm, b_ref):
        _bias_tiles(t_ref, b_ref)
        zero_pad = jnp.zeros((QB, 128), MXU)
        ks[0:QB, :] = zero_pad
        vs[0:QB, :] = zero_pad
        lane = lax.broadcasted_iota(jnp.int32, (QB, 128), 1)
        lane_s = lax.broadcasted_iota(jnp.int32, (S, 128), 1)
        t = dy_ref[...] * y_ref[...]
        sa = jnp.sum(jnp.where(lane_s < 64, t, 0.0), axis=1, keepdims=True)
        sb = jnp.sum(jnp.where(lane_s >= 64, t, 0.0), axis=1, keepdims=True)
        dn_nat[...] = jnp.where(lane_s < 64, sa, sb)
        dq_ref[...] = jnp.zeros((S, 128), F32)
        dk_ref[...] = jnp.zeros((S, 128), F32)
        dv_ref[...] = jnp.zeros((S, 128), F32)
        db_ref[...] = jnp.zeros((3, 2, QB, 2 * QB), F32)
        for p, (_, d) in enumerate(PATTERNS):
            L = S // d
            nb = L // QB
            _permute_in(qs, q_ref, d, scale=0.125, pad=0)
            _permute_in(ks, k_ref, d)
            _permute_in(vs, v_ref, d)
            _permute_in(dys, dy_ref, d, pad=0)
            _permute_in(ls, l_ref, d, pad=0)
            _permute_in(dds, dn_nat, d, pad=0)
            dk_perm[...] = jnp.zeros((S + QB, 128), F32)
            dv_perm[...] = jnp.zeros((S + QB, 128), F32)

            def blk(b, carry):
                r0 = pl.multiple_of(b * QB, QB)
                q = qs[pl.ds(r0, QB), :]
                kb = ks[pl.ds(r0, 2 * QB), :]
                vb = vs[pl.ds(r0, 2 * QB), :]
                dyb = dys[pl.ds(r0, QB), :]
                lb = ls[pl.ds(r0, QB), :]
                db = dds[pl.ds(r0, QB), :]
                first = ((b % nb) == 0).astype(jnp.int32)
                lane2 = jnp.concatenate([lane, lane], axis=0)
                own = (lane2 >> 6) == (lax.broadcasted_iota(jnp.int32, (2 * QB, 128), 0) >> 7)
                qm = jnp.where(own, jnp.concatenate([q, q], axis=0), jnp.zeros((2 * QB, 128), q.dtype))
                dym = jnp.where(own, jnp.concatenate([dyb, dyb], axis=0), jnp.zeros((2 * QB, 128), dyb.dtype))
                wide = lambda t: jnp.concatenate([jnp.broadcast_to(t[:, 0:1], (QB, 2 * QB)), jnp.broadcast_to(t[:, 64:65], (QB, 2 * QB))], axis=0)
                lse2, dd2 = wide(lb), wide(db)
                bias2 = jnp.concatenate([b_ref[p, 0, first], b_ref[p, 1, first]], axis=0)
                pr = jnp.exp(_mm_nt(qm, kb) + bias2 - lse2)
                ds = pr * (_mm_nt(dym, vb) - dd2)
                db_ref[p, 0] += ds[0:QB]
                db_ref[p, 1] += ds[QB:2 * QB]
                dq2 = _mm(ds, kb)
                dqs = [dq2[0:QB], dq2[QB:2 * QB]]
                dkb = _mm_tn(ds, qm)
                dvb = _mm_tn(pr, dym)
                dq_perm[pl.ds(r0, QB), :] = jnp.where(lane < 64, dqs[0], dqs[1])
                dk_perm[pl.ds(r0, 2 * QB), :] += dkb
                dv_perm[pl.ds(r0, 2 * QB), :] += dvb
                return carry

            lax.fori_loop(0, S // QB, blk, 0, unroll=4)
            for r in range(d):
                idx = pl.ds(r, L, stride=d) if d > 1 else pl.ds(0, S)
                dq_ref[idx, :] += dq_perm[r * L:(r + 1) * L, :] * 0.125
                dk_ref[idx, :] += dk_perm[QB + r * L:QB + (r + 1) * L, :]
                dv_ref[idx, :] += dv_perm[QB + r * L:QB + (r + 1) * L, :]

    col = lambda c0: pl.BlockSpec((S, 128), lambda hp: (0, c0 + hp))
    bspec = pl.BlockSpec((3, 2, 8, 2 * QB), lambda hp: (0, hp, 0, 0))
    return _call(
        body, name=name, grid=(N_HEADS // 2,),
        in_specs=[col(0), col(4), col(8), bspec, col(0), col(0), col(0)],
        out_specs=(col(0), col(0), col(0), pl.BlockSpec((3, 2, QB, 2 * QB), lambda hp: (0, hp, 0, 0))),
        out_shape=(_sds((S, D_ATT), F32), _sds((S, D_ATT), F32), _sds((S, D_ATT), F32),
                   _sds((3, N_HEADS, QB, 2 * QB), F32)),
        scratch=[pltpu.VMEM((S, 128), MXU), pltpu.VMEM((S + QB, 128), MXU), pltpu.VMEM((S + QB, 128), MXU),
                 pltpu.VMEM((S, 128), MXU), pltpu.VMEM((S, 128), F32), pltpu.VMEM((S, 128), F32),
                 pltpu.VMEM((S, 128), F32), pltpu.VMEM((S, 128), F32),
                 pltpu.VMEM((S + QB, 128), F32), pltpu.VMEM((S + QB, 128), F32),
                 pltpu.VMEM((3, 2, 2, QB, 2 * QB), F32)],
        cp=_cp(48, ("arbitrary",)), args=(z, z, z, bias, y, lse, dy))[0]


def relbias_grad(dbiases):
    bucket, band, _ = _att_static()
    onehot = (bucket[:, None] == np.arange(N_BUCKETS)[None, :, None, None]) & band[:, None]
    onehot = jnp.asarray(onehot.reshape(3, N_BUCKETS, QB * 2 * QB), BF16)

    def body(db0_ref, db1_ref, oh_ref, o_ref):
        acc = jnp.zeros((N_HEADS, N_BUCKETS), F32)
        for p in range(3):
            acc = acc + lax.dot_general(db0_ref[p] + db1_ref[p], oh_ref[p].astype(F32), (((1,), (1,)), ((), ())),
                                        preferred_element_type=F32, precision=lax.Precision.HIGHEST)
        o_ref[...] = acc

    vm = pl.BlockSpec(memory_space=pltpu.VMEM)
    out = _pallas_call(body, name="relbias_grad", in_specs=[vm, vm, vm], out_specs=vm,
                         out_shape=_sds((N_HEADS, N_BUCKETS), F32), compiler_params=_cp(40))(
        *[d.reshape(3, N_HEADS, QB * 2 * QB) for d in dbiases], onehot)
    return out.T


def _panel(t_ref, ri, j):
    return t_ref[ri, pl.ds(j, S, stride=8), :]


def _gelu(x):
    c = math.sqrt(2.0 / math.pi)
    th = jnp.tanh(c * (x + 0.044715 * x * x * x))
    return 0.5 * x * (1.0 + th), th


def ssm_fwd(z, a, bre, bim, cre, cim, dsk, gluw, glub, name, ride=None):
    def body(u_ref, a_ref, bre_ref, bim_ref, cre_ref, cim_ref, d_ref, gw_ref, gb_ref, y_ref, yp_ref, st_hbm, st_ref):
        u = u_ref[...]
        for j in range(8):
            st_ref[0, pl.ds(j, S, stride=8), :] = _mm(u, bre_ref[:, 128 * j:128 * (j + 1)])
            st_ref[1, pl.ds(j, S, stride=8), :] = _mm(u, bim_ref[:, 128 * j:128 * (j + 1)])
        ar, ai = a_ref[0], a_ref[1]

        def step(t, c):
            re, im = c
            i = pl.multiple_of(t * 8, 8)
            nre = ar * re - ai * im + st_ref[0, pl.ds(i, 8), :]
            nim = ar * im + ai * re + st_ref[1, pl.ds(i, 8), :]
            st_ref[0, pl.ds(i, 8), :] = nre
            st_ref[1, pl.ds(i, 8), :] = nim
            return nre, nim

        zero = jnp.zeros((8, 128), F32)
        lax.fori_loop(0, S, step, (zero, zero), unroll=8)
        y = d_ref[...] * u
        for j in range(8):
            y = y + _mm(_panel(st_ref, 0, j), cre_ref[128 * j:128 * (j + 1), :])
            y = y - _mm(_panel(st_ref, 1, j), cim_ref[128 * j:128 * (j + 1), :])
        pltpu.sync_copy(st_ref, st_hbm)
        yp_ref[...] = y
        gl, _ = _gelu(y)
        tt = _mm(gl, gw_ref[...].reshape(D_SSM, D_SSM)) + gb_ref[...]
        y_ref[...] = y * jax.nn.sigmoid(tt)

    vm = lambda shape: pl.BlockSpec(shape, lambda i: (0,) * len(shape))
    return _call(
        body, name=name, grid=(1,),
        in_specs=[pl.BlockSpec((S, 256), lambda i: (0, 6)), vm((2, 8, 128)), vm((256, 1024)), vm((256, 1024)),
                  vm((1024, 256)), vm((1024, 256)), vm((1, 256)),
                  vm((NDEV, 32, 256)), vm((1, 256))],
        out_specs=(vm((S, 256)), vm((S, 256)), pl.BlockSpec(memory_space=pl.ANY)),
        out_shape=(_sds((S, 256), F32), _sds((S, 256), F32), _sds((2, S * 8, 128), F32)),
        scratch=[pltpu.VMEM((2, S * 8, 128), F32)],
        cp=_cp(40, ("arbitrary",)), args=(z, a, bre, bim, cre, cim, dsk, gluw, glub), ride=ride)


def ssm_bwd(dy, z, ypre, st, a, bre, bim, cre, cim, dsk, gluw, glub, name):
    def body(dy_ref, u_ref, yp_ref, st_hbm, a_ref, bre_ref, bim_ref, cre_ref, cim_ref, d_ref, gw_ref, gb_ref,
             du_ref, dbre_ref, dbim_ref, dcre_ref, dcim_ref, da_ref, dd_ref, dgw_ref, dgb_ref, g_ref, st_ref):
        pltpu.sync_copy(st_hbm, st_ref)
        u = u_ref[...]
        y = yp_ref[...]
        dout = dy_ref[...]
        gw = gw_ref[...].reshape(D_SSM, D_SSM)
        gl, th = _gelu(y)
        sig = jax.nn.sigmoid(_mm(gl, gw) + gb_ref[...])
        dt = dout * y * sig * (1.0 - sig)
        dgw_ref[...] = _mm_tn(gl, dt)
        dgb_ref[...] = jnp.sum(dt, axis=0, keepdims=True)
        c = math.sqrt(2.0 / math.pi)
        dgelu = 0.5 * (1.0 + th) + 0.5 * y * (1.0 - th * th) * c * (1.0 + 3.0 * 0.044715 * y * y)
        dyv = dout * sig + _mm_nt(dt, gw) * dgelu
        dd_ref[...] = jnp.sum(dyv * u, axis=0, keepdims=True)
        for j in range(8):
            rows = slice(128 * j, 128 * (j + 1))
            g_ref[0, pl.ds(j, S, stride=8), :] = _mm_nt(dyv, cre_ref[rows, :])
            g_ref[1, pl.ds(j, S, stride=8), :] = -_mm_nt(dyv, cim_ref[rows, :])
            dcre_ref[rows, :] = _mm_tn(_panel(st_ref, 0, j), dyv)
            dcim_ref[rows, :] = -_mm_tn(_panel(st_ref, 1, j), dyv)
        ar, ai = a_ref[0], a_ref[1]

        def step(k, c4):
            gre, gim, dar, dai = c4
            i = pl.multiple_of((S - 1 - k) * 8, 8)
            nre = g_ref[0, pl.ds(i, 8), :] + ar * gre + ai * gim
            nim = g_ref[1, pl.ds(i, 8), :] + ar * gim - ai * gre
            g_ref[0, pl.ds(i, 8), :] = nre
            g_ref[1, pl.ds(i, 8), :] = nim
            sre = st_ref[0, pl.ds(i - 8, 8), :]
            sim = st_ref[1, pl.ds(i - 8, 8), :]
            return nre, nim, dar + nre * sre + nim * sim, dai + nim * sre - nre * sim

        zero = jnp.zeros((8, 128), F32)
        gre, gim, dar, dai = lax.fori_loop(0, S - 1, step, (zero, zero, zero, zero), unroll=8)
        g_ref[0, 0:8, :] = g_ref[0, 0:8, :] + ar * gre + ai * gim
        g_ref[1, 0:8, :] = g_ref[1, 0:8, :] + ar * gim - ai * gre
        da_ref[0] = dar
        da_ref[1] = dai
        du = dyv * d_ref[...]
        for j in range(8):
            cols = slice(128 * j, 128 * (j + 1))
            gr, gi = _panel(g_ref, 0, j), _panel(g_ref, 1, j)
            dbre_ref[:, cols] = _mm_tn(u, gr)
            dbim_ref[:, cols] = _mm_tn(u, gi)
            du = du + _mm_nt(gr, bre_ref[:, cols]) + _mm_nt(gi, bim_ref[:, cols])
        du_ref[...] = du

    vm = lambda shape: pl.BlockSpec(shape, lambda i: (0,) * len(shape))
    return _call(
        body, name=name, grid=(1,),
        in_specs=[vm((S, 256)), pl.BlockSpec((S, 256), lambda i: (0, 6)), vm((S, 256)), pl.BlockSpec(memory_space=pl.ANY),
                  vm((2, 8, 128)), vm((256, 1024)), vm((256, 1024)), vm((1024, 256)), vm((1024, 256)), vm((1, 256)),
                  vm((NDEV, 32, 256)), vm((1, 256))],
        out_specs=(vm((S, 256)), vm((256, 1024)), vm((256, 1024)), vm((1024, 256)), vm((1024, 256)),
                   vm((2, 8, 128)), vm((1, 256)), vm((256, 256)), vm((1, 256))),
        out_shape=(_sds((S, 256), F32), _sds((256, 1024), F32), _sds((256, 1024), F32), _sds((1024, 256), F32),
                   _sds((1024, 256), F32), _sds((2, 8, 128), F32), _sds((1, 256), F32), _sds((256, 256), F32),
                   _sds((1, 256), F32)),
        scratch=[pltpu.VMEM((2, S * 8, 128), F32), pltpu.VMEM((2, S * 8, 128), F32)],
        cp=_cp(56, ("arbitrary",)), args=(dy, z, ypre, st, a, bre, bim, cre, cim, dsk, gluw, glub))[0]


def _ssm_discretise(a_re, a_im, log_dt, b_re, b_im):
    dt = jnp.exp(log_dt)[:, None]
    er = jnp.exp(a_re * dt)
    abr, abi = er * jnp.cos(a_im * dt), er * jnp.sin(a_im * dt)
    den = a_re * a_re + a_im * a_im
    fr = ((abr - 1.0) * a_re + abi * a_im) / den
    fi = (abi * a_re - (abr - 1.0) * a_im) / den
    bbr = fr[:, :, None] * b_re - fi[:, :, None] * b_im
    bbi = fr[:, :, None] * b_im + fi[:, :, None] * b_re
    return abr, abi, bbr, bbi


def _blockdiag(t):
    g, r, c = t.shape
    eye = jnp.eye(g, dtype=t.dtype)
    return (t[:, :, None, :] * eye[:, None, :, None]).reshape(g * r, g * c)


def _blockdiag_take(m, r, c):
    g = m.shape[0] // r
    idx = jnp.arange(g)
    return m.reshape(g, r, g, c)[idx, :, idx, :]


PAD = 16


def _pool_lane_select(vals):
    lane = lax.broadcasted_iota(jnp.int32, vals[0].shape, 1)
    out = vals[3]
    for g in (2, 1, 0):
        out = jnp.where(lane < 64 * (g + 1), vals[g], out)
    return out


def _pool_counts():
    row = lax.broadcasted_iota(jnp.int32, (S, D_POOL), 0).astype(F32) + 1.0
    return _pool_lane_select([jnp.minimum(row, float(w)) for w in POOL_WINDOWS])


def _pooled(u, sa, sb):
    sums = []
    cur = u
    bufs = (sa, sb)
    for k, sh in enumerate((1, 2, 4, 8)):
        buf = bufs[k % 2]
        buf[PAD:PAD + S, :] = cur
        cur = cur + buf[PAD - sh:PAD - sh + S, :]
        sums.append(cur)
    return _pool_lane_select(sums) / _pool_counts() - u


def pool_fwd(z, pw, psc, name):
    def body(u_ref, w_ref, s_ref, y_ref, sa, sb):
        for buf in (sa, sb):
            buf[0:PAD, :] = jnp.zeros((PAD, D_POOL), F32)
        pooled = _pooled(u_ref[...], sa, sb)
        y_ref[...] = _mm(pooled, w_ref[...]) * s_ref[...]

    vm = lambda shape: pl.BlockSpec(shape, lambda i: (0,) * len(shape))
    return _pallas_call(
        body, name=name, grid=(1,),
        in_specs=[pl.BlockSpec((S, 256), lambda i: (0, 7)), vm((256, 256)), vm((1, 256))],
        out_specs=vm((S, 256)), out_shape=_sds((S, 256), F32),
        scratch_shapes=[pltpu.VMEM((S + 2 * PAD, D_POOL), F32)] * 2,
        compiler_params=_cp(40, ("arbitrary",)))(z, pw, psc)


def pool_bwd(dy, z, pw, psc, name):
    def body(dy_ref, u_ref, w_ref, s_ref, du_ref, dw_ref, ds_ref, sa, sb):
        for buf in (sa, sb):
            buf[0:PAD, :] = jnp.zeros((PAD, D_POOL), F32)
            buf[PAD + S:PAD + S + PAD, :] = jnp.zeros((PAD, D_POOL), F32)
        pooled = _pooled(u_ref[...], sa, sb)
        dyv = dy_ref[...]
        w = w_ref[...]
        ds_ref[...] = jnp.sum(dyv * _mm(pooled, w), axis=0, keepdims=True)
        dyl = dyv * s_ref[...]
        dw_ref[...] = _mm_tn(pooled, dyl)
        dpool = _mm_nt(dyl, w)
        cur = dpool / _pool_counts()
        sums = []
        bufs = (sa, sb)
        for k, sh in enumerate((1, 2, 4, 8)):
            buf = bufs[k % 2]
            buf[PAD:PAD + S, :] = cur
            cur = cur + buf[PAD + sh:PAD + sh + S, :]
            sums.append(cur)
        du_ref[...] = _pool_lane_select(sums) - dpool

    vm = lambda shape: pl.BlockSpec(shape, lambda i: (0,) * len(shape))
    return _pallas_call(
        body, name=name, grid=(1,),
        in_specs=[vm((S, 256)), pl.BlockSpec((S, 256), lambda i: (0, 7)), vm((256, 256)), vm((1, 256))],
        out_specs=(vm((S, 256)), vm((256, 256)), vm((1, 256))),
        out_shape=(_sds((S, 256), F32), _sds((256, 256), F32), _sds((1, 256), F32)),
        scratch_shapes=[pltpu.VMEM((S + 2 * PAD, D_POOL), F32)] * 2,
        compiler_params=_cp(40, ("arbitrary",)))(dy, z, pw, psc)


def ada_fwd(c_all, ada_w, ada_b_cols):
    def body(c_ref, w_ref, b_ref, o_ref):
        c = c_ref[...]
        cond = c * jax.nn.sigmoid(c)
        o_ref[...] = jnp.dot(cond, w_ref[...], preferred_element_type=F32, precision=lax.Precision.HIGHEST) + b_ref[...]

    return _pallas_call(
        body, name="ada_fwd", grid=(DEPTH,),
        in_specs=[pl.BlockSpec((NDEV, D), lambda l: (0, 0)), pl.BlockSpec((None, D, 1152), lambda l: (l, 0, 0)),
                  pl.BlockSpec((None, 1, 1152), lambda l: (l, 0, 0))],
        out_specs=pl.BlockSpec((None, NDEV, 1152), lambda l: (l, 0, 0)), out_shape=_sds((DEPTH, NDEV, 1152), F32),
        compiler_params=_cp(40, ("arbitrary",)))(c_all, ada_w, ada_b_cols)


def ada_bwd(c_all, dmod_cols):
    def body(c_ref, dm_ref, o_ref):
        c = c_ref[...]
        cond = c * jax.nn.sigmoid(c)
        o_ref[...] = lax.dot_general(cond, dm_ref[...], (((0,), (0,)), ((), ())), preferred_element_type=F32,
                                     precision=lax.Precision.HIGHEST)

    return _pallas_call(
        body, name="ada_bwd", grid=(DEPTH,),
        in_specs=[pl.BlockSpec((NDEV, D), lambda l: (0, 0)), pl.BlockSpec((None, NDEV, 1152), lambda l: (l, 0, 0))],
        out_specs=pl.BlockSpec((None, D, 1152), lambda l: (l, 0, 0)), out_shape=_sds((DEPTH, D, 1152), F32),
        compiler_params=_cp(40, ("arbitrary",)))(c_all, dmod_cols)


def _adamw(w, g, m, v):
    m2 = B1 * m + (1.0 - B1) * g
    v2 = B2 * v + (1.0 - B2) * (g * g)
    m_hat = m2 / (1.0 - B1 ** STEP)
    v_hat = v2 / (1.0 - B2 ** STEP)
    return -LR * (m_hat / (jnp.sqrt(v_hat) + EPS) + WD * w), m2, v2


def _sum8(ref):
    g = ref[0].astype(F32)
    for s in range(1, ref.shape[0]):
        g = g + ref[s].astype(F32)
    return g


def adam_rs(recv, w, m, v, tr, name):
    lead, (r, cdim) = w.shape[:-2], w.shape[-2:]
    cp = recv.shape[-1]
    nl = len(lead)

    def body(rc_ref, w_ref, m_ref, v_ref, g_ref, d_ref, m2_ref, v2_ref):
        g = _sum8(rc_ref)[:, :cdim]
        g_ref[...] = g
        d_ref[...], m2_ref[...], v2_ref[...] = _adamw(w_ref[...], g, m_ref[...], v_ref[...])

    rs = pl.BlockSpec((None,) * nl + (tr, cdim), lambda *i: (*i, 0))
    return _call(
        body, name=name, grid=lead + (r // tr,),
        in_specs=[pl.BlockSpec((NDEV,) + (None,) * nl + (tr, cp), lambda *i: (0, *i, 0)), rs, rs, rs],
        out_specs=(rs, rs, rs, rs), out_shape=tuple(_sds(w.shape, F32) for _ in range(4)),
        cp=_cp(48, ("arbitrary",) * (nl + 1)), args=(recv, w, m, v))[0]


def adam_block(recv, w, m, v, lf, prev, name):
    half = FB // 2

    def body(*refs):
        rc_ref, w_ref, m_ref, v_ref = refs[:4]
        g_ref, d_ref, m2_ref, v2_ref = refs[-4:]
        g = _sum8(rc_ref)
        g_ref[...] = g
        d_ref[...], m2_ref[...], v2_ref[...] = _adamw(w_ref[...], g, m_ref[...], v_ref[...])

    rs = pl.BlockSpec((None, None, half, D), lambda i: (lf // 2, lf % 2, i, 0))
    prev = list(prev) if prev is not None else []
    return list(_pallas_call(
        body, name=name, grid=(2,), in_specs=[pl.BlockSpec((recv.shape[0], half, D), lambda i: (0, i, 0)), rs, rs, rs] + [ANY] * len(prev),
        out_specs=(rs, rs, rs, rs), out_shape=tuple(_sds((DEPTH, 2, FB, D), F32) for _ in range(4)),
        input_output_aliases={4 + k: k for k in range(len(prev))},
        compiler_params=_cp(48, ("arbitrary",)))(recv, w, m, v, *prev))


def adam_plain(g, w, m, v, tr, name):
    lead, (r, cdim) = w.shape[:-2], w.shape[-2:]
    nl = len(lead)

    def body(g_ref, w_ref, m_ref, v_ref, d_ref, m2_ref, v2_ref):
        d_ref[...], m2_ref[...], v2_ref[...] = _adamw(w_ref[...], g_ref[...], m_ref[...], v_ref[...])

    rs = pl.BlockSpec((None,) * nl + (tr, cdim), lambda *i: (*i, 0))
    return _call(
        body, name=name, grid=lead + (r // tr,), in_specs=[rs, rs, rs, rs], out_specs=(rs, rs, rs),
        out_shape=tuple(_sds(w.shape, F32) for _ in range(3)),
        cp=_cp(48, ("arbitrary",) * (nl + 1)), args=(g, w, m, v))[0]


def adam_native(gs, ws, ms, vs, name):
    n = len(ws)

    def body(*refs):
        g_refs, w_refs, m_refs, v_refs = (refs[k * n:(k + 1) * n] for k in range(4))
        d_refs, m2_refs, v2_refs = (refs[(4 + k) * n:(5 + k) * n] for k in range(3))
        for a in range(n):
            d_refs[a][...], m2_refs[a][...], v2_refs[a][...] = _adamw(w_refs[a][...], g_refs[a][...], m_refs[a][...], v_refs[a][...])

    vm = pl.BlockSpec(memory_space=pltpu.VMEM)
    outs = _pallas_call(body, name=name, in_specs=[vm] * (4 * n), out_specs=tuple([vm] * (3 * n)),
                        out_shape=tuple(_sds(w.shape, F32) for w in ws) * 3, compiler_params=_cp(40))(*gs, *ws, *ms, *vs)
    return outs[:n], outs[n:2 * n], outs[2 * n:]


def sum_sources(recv, name):
    r = recv.shape[1]

    def body(rc_ref, o_ref):
        o_ref[...] = _sum8(rc_ref)

    vm = pl.BlockSpec(memory_space=pltpu.VMEM)
    return _pallas_call(body, name=name, in_specs=[vm], out_specs=vm, out_shape=_sds((r, 128), F32),
                          compiler_params=_cp(40))(recv)


def _pack(arrs):
    flat = jnp.concatenate([a.reshape(-1) for a in arrs])
    n = flat.shape[0]
    rows = -(-n // 1024) * 8
    return jnp.pad(flat, (0, rows * 128 - n)).reshape(rows, 128)


def _unpack(vec, shapes):
    flat = vec.reshape(-1)
    out, o = [], 0
    for sh in shapes:
        n = int(np.prod(sh))
        out.append(flat[o:o + n].reshape(sh))
        o += n
    return out


WEIGHTS = ['rel_bias', 'ada_w', 'ada_b', 'ln_g', 'ln_b', 'ffn_w_gate', 'ffn_w_up', 'ffn_w_down', 'w_in', 'w_out',
           'ssm_a_re', 'ssm_a_im', 'ssm_log_dt', 'ssm_b_re', 'ssm_b_im', 'ssm_c_re', 'ssm_c_im', 'ssm_d', 'glu_w',
           'glu_b', 'pool_w', 'pool_scale']
SMALL = ['rel_bias', 'ada_b', 'ln_g', 'ln_b', 'ssm_a_re', 'ssm_a_im', 'ssm_log_dt', 'ssm_b_re', 'ssm_b_im',
         'ssm_c_re', 'ssm_c_im', 'ssm_d', 'glu_b', 'pool_w', 'pool_scale']
SMALL_FULL_SHAPES = {'rel_bias': (32, 8), 'ada_b': (2, 9216), 'ln_g': (2, 3, 1024), 'ln_b': (2, 3, 1024),
                     'ssm_a_re': (2, 16, 64), 'ssm_a_im': (2, 16, 64), 'ssm_log_dt': (2, 16),
                     'ssm_b_re': (2, 16, 64, 16), 'ssm_b_im': (2, 16, 64, 16), 'ssm_c_re': (2, 16, 16, 64),
                     'ssm_c_im': (2, 16, 16, 64), 'ssm_d': (2, 256), 'glu_b': (2, 256), 'pool_w': (2, 4, 64, 64),
                     'pool_scale': (2, 256)}


def _step(P):
    me = _me()
    x0 = P['x'][0]
    target = P['loss_target'][0]

    bf = lambda a: a.astype(BF16)
    padr = lambda a: jnp.pad(bf(a), ((0, 0), (0, 0), (0, FBP - FB), (0, 0)))
    ffn_b = [padr(jnp.swapaxes(P['ffn_w_gate'], 2, 3)), padr(jnp.swapaxes(P['ffn_w_up'], 2, 3)), padr(P['ffn_w_down'])]
    mix_b = [bf(P['w_in']), bf(P['w_out']), bf(P['glu_w'])]

    def shards(l, sub):
        return [t[l] for t in mix_b] if sub == 1 else [t[l, sub // 2] for t in ffn_b]

    order = [(l, sub) for l in range(DEPTH) for sub in range(3)]
    nxt = dict(zip(order[:-1], order[1:]))
    W = {key: [None] * 3 for key in order}
    c_all, lng_all, lnb_all, *W[order[0]] = _exchange(Gather([P['c'], P['ln_g'], P['ln_b']] + shards(*order[0])), "gather_first")
    gather_queue = [(key, pos, a) for key in order[1:] for pos, a in enumerate(shards(*key))]

    def gather_ride(cap_us, must=None):
        units, used = [], 0.0
        while gather_queue:
            key, _, a = gather_queue[0]
            cost = a.size * a.dtype.itemsize * GATHER_US_PER_BYTE
            if key != must and used + cost / 2 > cap_us:
                break
            units.append(gather_queue.pop(0))
            used += cost
        return (Gather([a for _, _, a in units]) if units else None), units

    def gathered(units, outs):
        for (key, pos, _), o in zip(units, outs):
            W[key][pos] = o

    c_all = c_all.reshape(NDEV, D)
    ln_g = jnp.transpose(lng_all, (1, 2, 0, 3)).reshape(DEPTH, 3, D)
    ln_b = jnp.transpose(lnb_all, (1, 2, 0, 3)).reshape(DEPTH, 3, D)

    ada_b_cols = lax.dynamic_slice_in_dim(P['ada_b'], me * 1152, 1152, axis=1).reshape(DEPTH, 1, 1152)
    modc = ada_fwd(c_all, P['ada_w'], ada_b_cols)
    (mod_all,) = _exchange(Gather([modc]), "gather_mod")
    mod_me = lax.dynamic_index_in_dim(mod_all, me, axis=2, keepdims=False)
    mod = jnp.transpose(mod_me, (1, 0, 2)).reshape(DEPTH, 9, D)

    bias = att_bias(P['rel_bias'])
    ssm = []
    for l in range(DEPTH):
        prm = (P['ssm_a_re'][l], P['ssm_a_im'][l], P['ssm_log_dt'][l], P['ssm_b_re'][l], P['ssm_b_im'][l])
        (abr, abi, bbr, bbi), disc_vjp = jax.vjp(_ssm_discretise, *prm)
        ssm.append(dict(
            vjp=disc_vjp, a=jnp.stack([abr.reshape(8, 128), abi.reshape(8, 128)]),
            bre=_blockdiag(jnp.transpose(bbr, (0, 2, 1))).astype(MXU), bim=_blockdiag(jnp.transpose(bbi, (0, 2, 1))).astype(MXU),
            cre=_blockdiag(jnp.transpose(P['ssm_c_re'][l], (0, 2, 1))).astype(MXU),
            cim=_blockdiag(jnp.transpose(P['ssm_c_im'][l], (0, 2, 1))).astype(MXU),
            d=P['ssm_d'][l].reshape(1, 256), gb=P['glu_b'][l].reshape(1, 256),
            pw=_blockdiag(P['pool_w'][l]).astype(MXU), psc=P['pool_scale'][l].reshape(1, 256)))

    saved = []
    x = x0
    h = ln_mod_fwd(x, mod[0], 0, "ln_mod_fwd_l0s0")
    for l, sub in order:
        tag = f"l{l}s{sub}"
        after = (mod[nxt[(l, sub)][0]], nxt[(l, sub)][1]) if (l, sub) in nxt else None
        if sub != 1:
            wg, wu, wd = (t.reshape(NDEV * FBP, D) for t in W[(l, sub)])
            ride, units = gather_ride(60, nxt.get((l, sub)))
            (G, U, fo), got = ffn_fwd(h, wg, wu, wd, "ffn_fwd_" + tag, ride)
            gathered(units, got)
            saved.append(dict(x=x, h=h, G=G, U=U, f=fo))
            x, *hn = res_ln_fwd(x, fo, mod[l], sub, ln_g[l], ln_b[l], 0.5, "res_ln_fwd_" + tag, after)
        else:
            sp = ssm[l]
            win, wout, gluw = W[(l, sub)]
            ride, units = gather_ride(15)
            (z,), got = win_fwd(h, win, "win_fwd_" + tag, ride)
            gathered(units, got)
            ride, units = gather_ride(55)
            (ya, lse), got = att_fwd(z, bias, "att_fwd_" + tag, ride)
            gathered(units, got)
            ride, units = gather_ride(35)
            (ys, ypre, st), got = ssm_fwd(z, sp['a'], sp['bre'], sp['bim'], sp['cre'], sp['cim'], sp['d'], gluw, sp['gb'],
                                          "ssm_fwd_" + tag, ride)
            gathered(units, got)
            yp = pool_fwd(z, sp['pw'], sp['psc'], "pool_fwd_" + tag)
            ride, units = gather_ride(12, nxt.get((l, sub)))
            (o,), got = wout_fwd(ya, ys, yp, wout, "wout_fwd_" + tag, ride)
            gathered(units, got)
            saved.append(dict(x=x, h=h, z=z, ya=ya, lse=lse, ys=ys, ypre=ypre, st=st, yp=yp, f=o))
            x, *hn = res_ln_fwd(x, o, mod[l], sub, ln_g[l], ln_b[l], 1.0, "res_ln_fwd_" + tag, after)
        h = hn[0] if hn else None
    assert not gather_queue

    loss_tile, dx = loss_fwd_bwd(x, target, "loss")
    loss = lax.psum(loss_tile[0, 0], ("x", "y", "c"))

    flights = []

    dmod = [[None] * 9 for _ in range(DEPTH)]
    dlng = [[None] * 3 for _ in range(DEPTH)]
    dlnb = [[None] * 3 for _ in range(DEPTH)]
    dbiases = [None] * DEPTH
    small_l = [dict() for _ in range(DEPTH)]
    for l, sub in reversed(order):
        tag = f"l{l}s{sub}"
        sv = saved[3 * l + sub]
        if (l, sub) == order[-1]:
            dxa, df, sums = res_ln_bwd(sv['x'], sv['f'], mod[l], sub, ln_g[l], dx, 0.5, "res_ln_bwd_" + tag)
        dlng[l][sub], dlnb[l][sub], dmod[l][3 * sub + 2] = sums[0], sums[1], sums[2]
        if sub != 1:
            f = sub // 2
            wg, wu, wd = (t.reshape(NDEV * FBP, D) for t in W[(l, sub)])
            dwg, dwu, dwd, dh = ffn_bwd(df, sv['h'], sv['G'], sv['U'], wg, wu, wd, "ffn_bwd_" + tag)
            handle, zero = scatter_start([t.reshape(NDEV, FBP, D) for t in (dwg, dwu, dwd)], "scatter_start_" + tag)
            flights.append(((l, sub), handle))
        else:
            sp = ssm[l]
            win, wout, gluw = W[(l, sub)]
            dya, dys, dyp, dwout = wout_bwd(df, sv['ya'], sv['ys'], sv['yp'], wout, "wout_bwd_" + tag)
            dq, dk, dv, dbiases[l] = att_bwd(sv['z'], bias, sv['ya'], sv['lse'], dya, "att_bwd_" + tag)
            dus, dbre, dbim, dcre, dcim, da, dd, dgw, dgb = ssm_bwd(
                dys, sv['z'], sv['ypre'], sv['st'], sp['a'], sp['bre'], sp['bim'], sp['cre'], sp['cim'], sp['d'],
                gluw, sp['gb'], "ssm_bwd_" + tag)
            dup, dpw, dpsc = pool_bwd(dyp, sv['z'], sp['pw'], sp['psc'], "pool_bwd_" + tag)
            dh, dwin = win_bwd((dq, dk, dv, dus, dup), sv['h'], win, "win_bwd_" + tag)
            handle, zero = scatter_start([dwin, dwout, dgw.astype(BF16).reshape(NDEV, 32, 256)], "scatter_start_" + tag)
            flights.append(((l, sub), handle))
            d_are, d_aim, d_ldt, d_bre, d_bim = sp['vjp']((
                da[0].reshape(16, 64), da[1].reshape(16, 64),
                jnp.transpose(_blockdiag_take(dbre, 16, 64), (0, 2, 1)), jnp.transpose(_blockdiag_take(dbim, 16, 64), (0, 2, 1))))
            small_l[l] = dict(
                ssm_a_re=d_are, ssm_a_im=d_aim, ssm_log_dt=d_ldt, ssm_b_re=d_bre, ssm_b_im=d_bim,
                ssm_c_re=jnp.transpose(_blockdiag_take(dcre, 64, 16), (0, 2, 1)),
                ssm_c_im=jnp.transpose(_blockdiag_take(dcim, 64, 16), (0, 2, 1)),
                ssm_d=dd.reshape(256), glu_b=dgb.reshape(256), pool_w=_blockdiag_take(dpw, 64, 64), pool_scale=dpsc.reshape(256))
        if (l, sub) == order[0]:
            dx, sums2 = ln_mod_bwd(sv['x'], dh, mod[l] + zero, sub, dxa, "ln_mod_bwd_" + tag)
        else:
            lp, sp_ = order[order.index((l, sub)) - 1]
            svp = saved[3 * lp + sp_]
            dxa, df, sums, sums2 = ln_join_bwd(svp['x'], svp['f'], mod[lp], sp_, ln_g[lp], ln_b[lp], 1.0 if sp_ == 1 else 0.5,
                                               dh, mod[l] + zero, sub, dxa, "ln_join_bwd_" + tag)
        dmod[l][3 * sub], dmod[l][3 * sub + 1] = sums2[0], sums2[1]
    grad_x = dx[None]

    small = {k: jnp.stack([small_l[l][k] for l in range(DEPTH)]) for k in small_l[0]}
    small['rel_bias'] = relbias_grad(dbiases)
    small['ada_b'] = jnp.stack([jnp.stack(dmod[l]).reshape(9 * D) for l in range(DEPTH)])
    small['ln_g'] = jnp.stack([jnp.stack(dlng[l]) for l in range(DEPTH)])
    small['ln_b'] = jnp.stack([jnp.stack(dlnb[l]) for l in range(DEPTH)])
    small_flight, _ = scatter_start([_pack([small[k] for k in SMALL])], "gather_small_start", whole=True)

    out = {}

    def put(name, g, d, m2, v2, shape):
        out['grad_' + name], out['delta_' + name] = g.reshape(shape), d.reshape(shape)
        out['new_m_' + name], out['new_v_' + name] = m2.reshape(shape), v2.reshape(shape)

    def wmv(name):
        return [P[pre + name] for pre in ('', 'm_', 'v_')]

    recv = {}
    started_last = small_flight[1][0]
    for key, handle in flights[:-1]:
        recv[key] = scatter_wait(handle, started_last, "scatter_wait_l%ds%d" % key)
    for pos, (name, tr) in enumerate((('w_in', 512), ('w_out', 128), ('glu_w', 32))):
        both = jnp.stack([recv[(l, 1)][pos] for l in range(DEPTH)], axis=1)
        put(name, *adam_rs(both, *wmv(name), tr, "adam_" + name), P[name].shape)
    ffn = (('ffn_w_gate', [jnp.swapaxes(t, 2, 3) for t in wmv('ffn_w_gate')]),
           ('ffn_w_up', [jnp.swapaxes(t, 2, 3) for t in wmv('ffn_w_up')]), ('ffn_w_down', wmv('ffn_w_down')))
    part = [None] * 3
    for l, sub in [key for key, _ in flights[:-1] if key[1] != 1]:
        for pos, (name, ops) in enumerate(ffn):
            part[pos] = adam_block(recv[(l, sub)][pos], *ops, 2 * l + sub // 2, part[pos], f"adam_{name}_l{l}s{sub}")

    (small_all,) = scatter_wait(small_flight, part[2][0], "gather_small_wait")
    gsum = dict(zip(SMALL, _unpack(sum_sources(small_all, "sum_small"), [SMALL_FULL_SHAPES[k] for k in SMALL])))
    off = 256
    dmod_all = small_all.reshape(NDEV, -1)[:, off:off + DEPTH * 9 * D].reshape(NDEV, DEPTH, 9 * D)
    dmod_cols = jnp.transpose(lax.dynamic_slice_in_dim(dmod_all, me * 1152, 1152, axis=2), (1, 0, 2))
    g_ada_w = ada_bwd(c_all, dmod_cols)

    put('ada_w', g_ada_w, *adam_plain(g_ada_w, *wmv('ada_w'), 256, "adam_ada_w"), P['ada_w'].shape)

    for k in ('ln_g', 'ln_b'):
        gsum[k] = lax.dynamic_slice_in_dim(gsum[k], me * 128, 128, axis=2)
    swaps = {'rel_bias': (0, 1), 'ln_g': (0, 1), 'ln_b': (0, 1), 'ssm_b_re': (2, 3), 'ssm_b_im': (2, 3)}
    view = lambda k, t: jnp.swapaxes(t, *swaps[k]) if k in swaps else t
    ds_, m2s, v2s = adam_native(*[[view(k, src(k)) for k in SMALL] for src in
                                  (lambda k: gsum[k], lambda k: P[k], lambda k: P['m_' + k], lambda k: P['v_' + k])],
                                "adam_small")
    for k, d, m2, v2 in zip(SMALL, ds_, m2s, v2s):
        put(k, gsum[k], view(k, d), view(k, m2), view(k, v2), P[k].shape)

    (l, sub), handle = flights[-1]
    last = scatter_wait(handle, out['new_v_ada_w'], "scatter_wait_l%ds%d" % (l, sub))
    for pos, (name, ops) in enumerate(ffn):
        res = adam_block(last[pos], *ops, 2 * l + sub // 2, part[pos], f"adam_{name}_l{l}s{sub}")
        put(name, *([jnp.swapaxes(t, 2, 3) for t in res] if pos < 2 else res), P[name].shape)

    res = [loss, grad_x]
    for pre in ('grad_', 'delta_', 'new_m_', 'new_v_'):
        res += [out[pre + k] for k in WEIGHTS]
    return tuple(res)


def kernel(x, c, rel_bias, ada_w, ada_b, ln_g, ln_b, ffn_w_gate, ffn_w_up, ffn_w_down, w_in, w_out, ssm_a_re, ssm_a_im, ssm_log_dt, ssm_b_re, ssm_b_im, ssm_c_re, ssm_c_im, ssm_d, glu_w, glu_b, pool_w, pool_scale, loss_target, m_rel_bias, m_ada_w, m_ada_b, m_ln_g, m_ln_b, m_ffn_w_gate, m_ffn_w_up, m_ffn_w_down, m_w_in, m_w_out, m_ssm_a_re, m_ssm_a_im, m_ssm_log_dt, m_ssm_b_re, m_ssm_b_im, m_ssm_c_re, m_ssm_c_im, m_ssm_d, m_glu_w, m_glu_b, m_pool_w, m_pool_scale, v_rel_bias, v_ada_w, v_ada_b, v_ln_g, v_ln_b, v_ffn_w_gate, v_ffn_w_up, v_ffn_w_down, v_w_in, v_w_out, v_ssm_a_re, v_ssm_a_im, v_ssm_log_dt, v_ssm_b_re, v_ssm_b_im, v_ssm_c_re, v_ssm_c_im, v_ssm_d, v_glu_w, v_glu_b, v_pool_w, v_pool_scale):
    return _step(dict(locals()))
```

```python
import functools
import math

import numpy as np
import jax
import jax.numpy as jnp
from jax import lax
from jax.experimental import pallas as pl
from jax.experimental.pallas import tpu as pltpu

F32 = jnp.float32
BF16 = jnp.bfloat16
MXU = jnp.bfloat16

S = 2048
D = 1024
NDEV = 8
DEPTH = 2
D_ATT, D_SSM, D_POOL, D_IN = 512, 256, 256, 2048
N_HEADS = 8
FB = 352
FBP = 384
QB = 128
PATTERNS = ((128, 1), (512, 4), (2048, 16))
POOL_WINDOWS = (2, 4, 8, 16)
N_BUCKETS, MAX_DISTANCE = 32, 2048
ALPHA = (2 * DEPTH) ** 0.25
LN_EPS = 1e-5
NEG = -1e30
GATHER_US_PER_BYTE = 43e-6
LR, B1, B2, EPS, WD, STEP = 0.001, 0.9, 0.999, 1e-08, 0.01, 10

TM = 256
TMM = 512
MIB = 1024 * 1024


def _cp(vmem_mib, sem=None):
    kw = dict(vmem_limit_bytes=vmem_mib * MIB)
    if sem is not None:
        kw["dimension_semantics"] = sem
    return pltpu.CompilerParams(**kw)


def _sds(shape, dtype):
    return jax.ShapeDtypeStruct(shape, dtype)


def _mm(a, b):
    return jnp.dot(a.astype(MXU), b.astype(MXU), preferred_element_type=F32)


def _mm_nt(a, b):
    return lax.dot_general(a.astype(MXU), b.astype(MXU), (((1,), (1,)), ((), ())), preferred_element_type=F32)


def _mm_tn(a, b):
    return lax.dot_general(a.astype(MXU), b.astype(MXU), (((0,), (0,)), ((), ())), preferred_element_type=F32)


def _ln_stats(x):
    mu = jnp.mean(x, axis=-1, keepdims=True)
    xc = x - mu
    var = jnp.mean(xc * xc, axis=-1, keepdims=True)
    rstd = lax.rsqrt(var + LN_EPS)
    return xc * rstd, rstd


def _ln_bwd(dn, n, rstd):
    return rstd * (dn - jnp.mean(dn, axis=-1, keepdims=True) - n * jnp.mean(dn * n, axis=-1, keepdims=True))


def _me():
    return 4 * lax.axis_index("x") + 2 * lax.axis_index("y") + lax.axis_index("c")


ANY = pl.BlockSpec(memory_space=pl.ANY)
PIN_BYTES = 1 << 19


def _pallas_call(*a, **k):
    big = lambda o: math.prod(o.shape) * o.dtype.itemsize >= PIN_BYTES
    pin = lambda o: pltpu.HBM(o.shape, o.dtype) if isinstance(o, jax.ShapeDtypeStruct) and big(o) else o
    osh = k["out_shape"]
    k["out_shape"] = tuple(pin(o) for o in osh) if isinstance(osh, (tuple, list)) else pin(osh)
    fn = pl.pallas_call(*a, **k)

    def run(*args):
        return fn(*[pltpu.with_memory_space_constraint(x, pltpu.HBM) if big(x) else x for x in args])
    return run


class Gather:
    def __init__(self, srcs):
        self.srcs = list(srcs)
        self.n = len(self.srcs)
        self.bufs = []
        self.out_shapes = [_sds((NDEV,) + a.shape, a.dtype) for a in self.srcs]
        self.sems = [pltpu.SemaphoreType.DMA((7 * self.n,)), pltpu.SemaphoreType.DMA((7 * self.n,)),
                     pltpu.SemaphoreType.DMA((self.n,))]

    def _parts(self, srcs, outs, sems):
        send_sems, recv_sems, loc_sems = sems
        x, y, c = lax.axis_index("x"), lax.axis_index("y"), lax.axis_index("c")
        me, sib = (x, y, c), (x, y, 1 - c)
        chips = [(1 - x, y), (x, 1 - y), (1 - x, 1 - y)]
        slot = lambda d: 4 * d[0] + 2 * d[1] + d[2]

        def copy(a, k, block, to, src=None):
            dst = outs[a].at[slot(block)]
            return pltpu.make_async_remote_copy(
                src_ref=dst if src is None else src, dst_ref=dst,
                send_sem=send_sems.at[7 * a + k], recv_sem=recv_sems.at[7 * a + k],
                device_id=to, device_id_type=pl.DeviceIdType.MESH)

        local = [pltpu.make_async_copy(srcs[a], outs[a].at[slot(me)], loc_sems.at[a]) for a in range(self.n)]
        return me, sib, chips, c, copy, local

    def start(self, srcs, bufs, outs, sems):
        me, sib, chips, c, copy, local = self._parts(srcs, outs, sems)
        for a in range(self.n):
            local[a].start()
            copy(a, 0, me, sib, src=srcs[a]).start()
            for j, chip in enumerate(chips):
                copy(a, 1 + j, me, (*chip, c), src=srcs[a]).start()

    def finish(self, srcs, bufs, outs, sems):
        me, sib, chips, c, copy, local = self._parts(srcs, outs, sems)
        for a in range(self.n):
            for j, chip in enumerate(chips):
                copy(a, 1 + j, (*chip, c), me).wait_recv()
                copy(a, 4 + j, (*chip, c), sib).start()
        for a in range(self.n):
            copy(a, 0, sib, me).wait_recv()
            copy(a, 0, me, sib, src=srcs[a]).wait_send()
            for j, chip in enumerate(chips):
                copy(a, 4 + j, (*chip, 1 - c), me).wait_recv()
                copy(a, 1 + j, me, (*chip, c), src=srcs[a]).wait_send()
                copy(a, 4 + j, (*chip, c), sib).wait_send()
            local[a].wait()


def _call(body, *, name, grid, in_specs, out_specs, out_shape, args, scratch=(), cp=None, ride=None):
    out_specs, out_shape, scratch = list(out_specs), list(out_shape), list(scratch)
    if ride is None:
        outs = _pallas_call(body, name=name, grid=grid, in_specs=list(in_specs), out_specs=tuple(out_specs),
                              out_shape=tuple(out_shape), scratch_shapes=scratch, compiler_params=cp)(*args)
        return list(outs), []
    nin, nout, nscr, n, nb, no = len(in_specs), len(out_specs), len(scratch), ride.n, len(ride.bufs), len(ride.out_shapes)
    steps = list(grid)

    def wrapped(*refs):
        h_in, r_src, r_buf = refs[:nin], refs[nin:nin + n], refs[nin + n:nin + n + nb]
        o0 = nin + n + nb
        h_out, r_out = refs[o0:o0 + nout], refs[o0 + nout:o0 + nout + no]
        s0 = o0 + nout + no
        h_scr, sems = refs[s0:s0 + nscr], refs[s0 + nscr:]
        ids = [pl.program_id(a) for a in range(len(steps))]
        first = functools.reduce(jnp.logical_and, [i == 0 for i in ids])
        last = functools.reduce(jnp.logical_and, [i == s - 1 for i, s in zip(ids, steps)])

        @pl.when(first)
        def _():
            ride.start(r_src, r_buf, r_out, sems)

        body(*h_in, *h_out, *h_scr)

        @pl.when(last)
        def _():
            ride.finish(r_src, r_buf, r_out, sems)

    aliases = {nin + n + k: nout + k for k in range(nb)}
    outs = _pallas_call(
        wrapped, name=name, grid=grid, in_specs=list(in_specs) + [ANY] * (n + nb),
        out_specs=tuple(out_specs + [ANY] * no), out_shape=tuple(out_shape + ride.out_shapes),
        scratch_shapes=scratch + ride.sems, input_output_aliases=aliases, compiler_params=cp,
    )(*args, *ride.srcs, *ride.bufs)
    return list(outs[:nout]), list(outs[nout:])


def _exchange(ride, name):
    def body(dummy_ref, o_ref):
        o_ref[...] = dummy_ref[...]

    one = pl.BlockSpec((8, 128), lambda i: (0, 0))
    _, outs = _call(body, name=name, grid=(1,), in_specs=[one], out_specs=[one], out_shape=[_sds((8, 128), F32)],
                    args=(jnp.zeros((8, 128), F32),), ride=ride)
    return outs


HBM = pl.BlockSpec(memory_space=pltpu.HBM)
SEM = pl.BlockSpec(memory_space=pltpu.SEMAPHORE)


def routes_all(me):
    return [(k, me ^ k, me, me ^ k) for k in range(NDEV)]


def _scatter_copies(srcs, lands, sems, sending, whole, rows):
    send_sems, recv_sems, loc_sems = sems
    me = _me()
    rts = routes_all(me)
    part = lambda ref, slab: ref if whole else ref.at[slab] if rows is None else ref.at[slab, pl.ds(0, rows)]
    remote_ix = [r for r, (k, _, _, _) in enumerate(rts) if k != 0]
    local_ix = [r for r, (k, _, _, _) in enumerate(rts) if k == 0]
    remote, local = [], []
    for a in range(len(srcs)):
        for n, r in enumerate(remote_ix):
            k, slab, there, here = rts[r]
            t = me ^ k
            sem = len(remote_ix) * a + n
            remote.append(pltpu.make_async_remote_copy(
                src_ref=part(srcs[a], slab), dst_ref=lands[a].at[there if sending else here], send_sem=send_sems.at[sem],
                recv_sem=recv_sems.at[sem], device_id=(t // 4, (t // 2) % 2, t % 2), device_id_type=pl.DeviceIdType.MESH))
        for n, r in enumerate(local_ix):
            _, slab, there, _ = rts[r]
            local.append(pltpu.make_async_copy(part(srcs[a], slab), lands[a].at[there], loc_sems.at[len(local_ix) * a + n]))
    return remote, local


def scatter_start(payloads, name, whole=False, rows=None):
    n = len(payloads)
    nr = NDEV - 1

    def body(*refs):
        srcs, lands, sems = refs[:n], refs[n:2 * n], refs[2 * n:2 * n + 3]
        remote, local = _scatter_copies(srcs, lands, sems, True, whole, rows)
        for cp in local + remote:
            cp.start()
        refs[-1][...] = jnp.zeros((8, 128), F32)

    thru = [pltpu.HBM(p.shape, p.dtype) for p in payloads]
    land_shapes = [(NDEV,) + p.shape if whole else p.shape if rows is None else (NDEV, rows) + p.shape[2:] for p in payloads]
    outs = pl.pallas_call(
        body, name=name,
        out_shape=(pltpu.SemaphoreType.DMA((nr * n,)), pltpu.SemaphoreType.DMA((nr * n,)), pltpu.SemaphoreType.DMA((n,)),
                   *thru, *[pltpu.HBM(sh, p.dtype) for sh, p in zip(land_shapes, payloads)], _sds((8, 128), F32)),
        in_specs=[HBM] * (2 * n),
        out_specs=(SEM, SEM, SEM, *[HBM] * (2 * n), pl.BlockSpec(memory_space=pltpu.VMEM)),
        input_output_aliases={i: 3 + i for i in range(2 * n)},
        compiler_params=pltpu.CompilerParams(has_side_effects=pltpu.SideEffectType.DATAFLOW_SIDE_EFFECTING),
    )(*[pltpu.with_memory_space_constraint(p, pltpu.HBM) for p in payloads],
      *[pltpu.with_memory_space_constraint(lax.empty(sh, p.dtype), pltpu.HBM) for sh, p in zip(land_shapes, payloads)])
    return (outs[:3], outs[3:3 + n], outs[3 + n:3 + 2 * n], whole, rows), outs[-1][0, 0]


def scatter_wait(handle, after, name):
    sems, srcs_thru, lands_thru, whole, rows = handle
    n = len(srcs_thru)

    def body(*refs):
        srcs, lands, sems_ = refs[:n], refs[n:2 * n], refs[2 * n:2 * n + 3]
        remote, local = _scatter_copies(srcs, lands, sems_, False, whole, rows)
        for cp in remote:
            cp.wait_send()
            cp.wait_recv()
        for cp in local:
            cp.wait()

    outs = pl.pallas_call(
        body, name=name, out_shape=tuple(pltpu.HBM(p.shape, p.dtype) for p in (*srcs_thru, *lands_thru)),
        in_specs=[HBM] * (2 * n) + [SEM] * 3 + [HBM], out_specs=tuple([HBM] * (2 * n)),
        input_output_aliases={i: i for i in range(2 * n)},
        compiler_params=pltpu.CompilerParams(has_side_effects=pltpu.SideEffectType.DATAFLOW_SIDE_EFFECTING),
    )(*srcs_thru, *lands_thru, *sems, pltpu.with_memory_space_constraint(after, pltpu.HBM))
    return list(outs[n:])


def _row_spec(cols, tm=TM):
    return pl.BlockSpec((tm, cols), lambda i: (i, 0))


def _full_spec(shape):
    nd = len(shape)
    return pl.BlockSpec(shape, lambda i: (0,) * nd)


def ln_mod_fwd(x, mod, sub, name):
    def body(x_ref, mod_ref, h_ref):
        n, _ = _ln_stats(x_ref[...])
        shift = mod_ref[3 * sub:3 * sub + 1, :]
        scale = mod_ref[3 * sub + 1:3 * sub + 2, :]
        h_ref[...] = (n * (1.0 + scale) + shift).astype(MXU)

    return _pallas_call(
        body, name=name, grid=(S // TM,),
        in_specs=[_row_spec(D), _full_spec((9, D))], out_specs=_row_spec(D),
        out_shape=_sds((S, D), MXU), compiler_params=_cp(32, ("arbitrary",)))(x, mod)


def res_ln_fwd(x, f, mod, sub, lng, lnb, w, name, nxt=None):
    def body(x_ref, f_ref, mod_ref, g_ref, b_ref, *rest):
        gate = mod_ref[3 * sub + 2:3 * sub + 3, :]
        r = ALPHA * x_ref[...] + (w * gate) * f_ref[...]
        n, _ = _ln_stats(r)
        xo = n * g_ref[sub:sub + 1, :] + b_ref[sub:sub + 1, :]
        rest[-1 if nxt is None else -2][...] = xo
        if nxt is not None:
            nmod_ref, h_ref = rest[0], rest[-1]
            n2, _ = _ln_stats(xo)
            s2 = nxt[1]
            h_ref[...] = (n2 * (1.0 + nmod_ref[3 * s2 + 1:3 * s2 + 2, :]) + nmod_ref[3 * s2:3 * s2 + 1, :]).astype(MXU)

    more = nxt is not None
    return _pallas_call(
        body, name=name, grid=(S // TM,),
        in_specs=[_row_spec(D), _row_spec(D), _full_spec((9, D)), _full_spec((3, D)), _full_spec((3, D))] + [_full_spec((9, D))] * more,
        out_specs=(_row_spec(D),) + (_row_spec(D),) * more, out_shape=(_sds((S, D), F32),) + (_sds((S, D), MXU),) * more,
        compiler_params=_cp(32, ("arbitrary",)))(x, f, mod, lng, lnb, *([nxt[0]] if more else []))


def res_ln_bwd(x, f, mod, sub, lng, dxo, w, name):
    def body(x_ref, f_ref, mod_ref, g_ref, dxo_ref, dxa_ref, df_ref, sums_ref):
        i = pl.program_id(0)
        gate = mod_ref[3 * sub + 2:3 * sub + 3, :]
        fv = f_ref[...]
        r = ALPHA * x_ref[...] + (w * gate) * fv
        n, rstd = _ln_stats(r)
        dxo = dxo_ref[...]
        dr = _ln_bwd(dxo * g_ref[sub:sub + 1, :], n, rstd)
        dxa_ref[...] = ALPHA * dr
        df_ref[...] = ((w * gate) * dr).astype(MXU)
        part = jnp.concatenate([
            jnp.sum(dxo * n, axis=0, keepdims=True),
            jnp.sum(dxo, axis=0, keepdims=True),
            jnp.sum(dr * fv, axis=0, keepdims=True) * w,
            jnp.zeros((5, D), F32)], axis=0)

        @pl.when(i == 0)
        def _():
            sums_ref[...] = part

        @pl.when(i > 0)
        def _():
            sums_ref[...] += part

    return _call(
        body, name=name, grid=(S // TM,),
        in_specs=[_row_spec(D), _row_spec(D), _full_spec((9, D)), _full_spec((3, D)), _row_spec(D)],
        out_specs=(_row_spec(D), _row_spec(D), _full_spec((8, D))),
        out_shape=(_sds((S, D), F32), _sds((S, D), MXU), _sds((8, D), F32)),
        cp=_cp(32, ("arbitrary",)), args=(x, f, mod, lng, dxo))[0]


def ln_mod_bwd(x, dh, mod, sub, dxa, name):
    def body(x_ref, dh_ref, mod_ref, dxa_ref, dx_ref, sums_ref):
        i = pl.program_id(0)
        scale = mod_ref[3 * sub + 1:3 * sub + 2, :]
        n, rstd = _ln_stats(x_ref[...])
        dh = dh_ref[...]
        dx_ref[...] = dxa_ref[...] + _ln_bwd(dh * (1.0 + scale), n, rstd)
        part = jnp.concatenate([
            jnp.sum(dh, axis=0, keepdims=True),
            jnp.sum(dh * n, axis=0, keepdims=True),
            jnp.zeros((6, D), F32)], axis=0)

        @pl.when(i == 0)
        def _():
            sums_ref[...] = part

        @pl.when(i > 0)
        def _():
            sums_ref[...] += part

    return _call(
        body, name=name, grid=(S // TM,),
        in_specs=[_row_spec(D), _row_spec(D), _full_spec((9, D)), _row_spec(D)],
        out_specs=(_row_spec(D), _full_spec((8, D))),
        out_shape=(_sds((S, D), F32), _sds((8, D), F32)),
        cp=_cp(32, ("arbitrary",)), args=(x, dh, mod, dxa))[0]


def ln_join_bwd(xp, fp, modp, subp, lngp, lnbp, wp, dh, mod, sub, dxa, name):
    def body(xp_ref, fp_ref, modp_ref, g_ref, b_ref, dh_ref, mod_ref, dxa_ref, dxap_ref, dfp_ref, sumsp_ref, sums_ref):
        i = pl.program_id(0)
        gate = modp_ref[3 * subp + 2:3 * subp + 3, :]
        fv = fp_ref[...]
        n, rstd = _ln_stats(ALPHA * xp_ref[...] + (wp * gate) * fv)
        gain = g_ref[subp:subp + 1, :]
        n2, rstd2 = _ln_stats(n * gain + b_ref[subp:subp + 1, :])
        dh = dh_ref[...]
        dx = dxa_ref[...] + _ln_bwd(dh * (1.0 + mod_ref[3 * sub + 1:3 * sub + 2, :]), n2, rstd2)
        dr = _ln_bwd(dx * gain, n, rstd)
        dxap_ref[...] = ALPHA * dr
        dfp_ref[...] = ((wp * gate) * dr).astype(MXU)
        partp = jnp.concatenate([
            jnp.sum(dx * n, axis=0, keepdims=True), jnp.sum(dx, axis=0, keepdims=True),
            jnp.sum(dr * fv, axis=0, keepdims=True) * wp, jnp.zeros((5, D), F32)], axis=0)
        part = jnp.concatenate([
            jnp.sum(dh, axis=0, keepdims=True), jnp.sum(dh * n2, axis=0, keepdims=True), jnp.zeros((6, D), F32)], axis=0)

        @pl.when(i == 0)
        def _():
            sumsp_ref[...] = partp
            sums_ref[...] = part

        @pl.when(i > 0)
        def _():
            sumsp_ref[...] += partp
            sums_ref[...] += part

    return _pallas_call(
        body, name=name, grid=(S // TM,),
        in_specs=[_row_spec(D), _row_spec(D), _full_spec((9, D)), _full_spec((3, D)), _full_spec((3, D)), _row_spec(D),
                  _full_spec((9, D)), _row_spec(D)],
        out_specs=(_row_spec(D), _row_spec(D), _full_spec((8, D)), _full_spec((8, D))),
        out_shape=(_sds((S, D), F32), _sds((S, D), MXU), _sds((8, D), F32), _sds((8, D), F32)),
        compiler_params=_cp(40, ("arbitrary",)))(xp, fp, modp, lngp, lnbp, dh, mod, dxa)


def loss_fwd_bwd(y, target, name):
    def body(y_ref, t_ref, l_ref, dy_ref):
        i = pl.program_id(0)
        e = y_ref[...] - t_ref[...]
        dy_ref[...] = e * (1.0 / D)
        part = jnp.zeros((8, 128), F32) + (0.5 / D) * jnp.sum(e * e)

        @pl.when(i == 0)
        def _():
            l_ref[...] = part

        @pl.when(i > 0)
        def _():
            l_ref[...] += part

    return _pallas_call(
        body, name=name, grid=(S // TM,),
        in_specs=[_row_spec(D), _row_spec(D)], out_specs=(_full_spec((8, 128)), _row_spec(D)),
        out_shape=(_sds((8, 128), F32), _sds((S, D), F32)),
        compiler_params=_cp(32, ("arbitrary",)))(y, target)


HB = 2 * FBP
NHB = NDEV * FBP // HB
TMB = 1024


def _wrows(buffers=2):
    return pl.BlockSpec((HB, D), lambda j, i: (j, 0), pipeline_mode=pl.Buffered(buffers))


def _resident(shape):
    return pl.BlockSpec(shape, lambda j, i: (0, 0), pipeline_mode=pl.Buffered(1))


def ffn_fwd(h, wgt, wut, wd, name, ride=None):
    def body(h_ref, wg_ref, wu_ref, wd_ref, g_ref, u_ref, f_ref):
        j, i = pl.program_id(0), pl.program_id(1)
        hv = h_ref[...]
        g = _mm_nt(hv, wg_ref[...])
        u = _mm_nt(hv, wu_ref[...])
        g_ref[...] = g.astype(MXU)
        u_ref[...] = u.astype(MXU)
        a = g * jax.nn.sigmoid(g) * u
        part = _mm(a, wd_ref[...])
        rows = pl.ds(pl.multiple_of(i * TMB, TMB), TMB)

        @pl.when(j == 0)
        def _():
            f_ref[rows, :] = part

        @pl.when(j > 0)
        def _():
            f_ref[rows, :] += part

    gu = pl.BlockSpec((TMB, HB), lambda j, i: (i, j))
    return _call(
        body, name=name, grid=(NHB, S // TMB),
        in_specs=[pl.BlockSpec((TMB, D), lambda j, i: (i, 0)), _wrows(), _wrows(), _wrows()],
        out_specs=(gu, gu, _resident((S, D))),
        out_shape=(_sds((S, NDEV * FBP), MXU), _sds((S, NDEV * FBP), MXU), _sds((S, D), F32)),
        cp=_cp(52, ("arbitrary", "arbitrary")), args=(h, wgt, wut, wd), ride=ride)


def ffn_bwd(df, h, g, u, wgt, wut, wd, name):
    ni = S // TMB

    def body(df_ref, h_ref, g_ref, u_ref, wg_ref, wu_ref, wd_ref, dwg_ref, dwu_ref, dwd_ref, dh_ref,
             ag_ref, au_ref, ad_ref):
        j, i = pl.program_id(0), pl.program_id(1)
        dfv, hv = df_ref[...], h_ref[...]
        gv, uv = g_ref[...].astype(F32), u_ref[...].astype(F32)
        da = _mm_nt(dfv, wd_ref[...])
        sg = jax.nn.sigmoid(gv)
        silu = gv * sg
        du = da * silu
        dg = da * uv * (sg * (1.0 + gv * (1.0 - sg)))
        p_d = _mm_tn(silu * uv, dfv)
        p_g = _mm_tn(dg, hv)
        p_u = _mm_tn(du, hv)

        @pl.when(i == 0)
        def _():
            ad_ref[...] = p_d
            ag_ref[...] = p_g
            au_ref[...] = p_u

        @pl.when(i > 0)
        def _():
            ad_ref[...] += p_d
            ag_ref[...] += p_g
            au_ref[...] += p_u

        @pl.when(i == ni - 1)
        def _():
            dwd_ref[...] = ad_ref[...].astype(BF16)
            dwg_ref[...] = ag_ref[...].astype(BF16)
            dwu_ref[...] = au_ref[...].astype(BF16)

        part = _mm(dg, wg_ref[...]) + _mm(du, wu_ref[...])
        rows = pl.ds(pl.multiple_of(i * TMB, TMB), TMB)

        @pl.when(j == 0)
        def _():
            dh_ref[rows, :] = part

        @pl.when(j > 0)
        def _():
            dh_ref[rows, :] += part

    gu = pl.BlockSpec((TMB, HB), lambda j, i: (i, j))
    rowt = pl.BlockSpec((TMB, D), lambda j, i: (i, 0))
    return _call(
        body, name=name, grid=(NHB, ni),
        in_specs=[rowt, rowt, gu, gu, _wrows(1), _wrows(1), _wrows(1)],
        out_specs=(_wrows(1), _wrows(1), _wrows(1), _resident((S, D))),
        out_shape=(_sds((NDEV * FBP, D), BF16), _sds((NDEV * FBP, D), BF16), _sds((NDEV * FBP, D), BF16), _sds((S, D), F32)),
        scratch=[pltpu.VMEM((HB, D), F32), pltpu.VMEM((HB, D), F32), pltpu.VMEM((HB, D), F32)],
        cp=_cp(60, ("arbitrary", "arbitrary")), args=(df, h, g, u, wgt, wut, wd))[0]


def win_fwd(h, win, name, ride=None):
    def body(h_ref, w_ref, z_ref):
        hv = h_ref[...]
        for j in range(NDEV):
            z_ref[:, 256 * j:256 * (j + 1)] = _mm(hv, w_ref[j])

    return _call(
        body, name=name, grid=(S // TMM,),
        in_specs=[_row_spec(D, TMM), _full_spec((NDEV, D, 256))],
        out_specs=[_row_spec(D_IN, TMM)], out_shape=[_sds((S, D_IN), F32)],
        cp=_cp(40, ("arbitrary",)), args=(h, win), ride=ride)


def win_bwd(dparts, h, win, name):
    ni = S // TMM

    def body(dq_ref, dk_ref, dv_ref, dus_ref, dup_ref, h_ref, w_ref, dh_ref, dw_ref, acc_ref):
        i = pl.program_id(0)
        hv = h_ref[...]
        cols = [dq_ref[:, 0:256], dq_ref[:, 256:512], dk_ref[:, 0:256], dk_ref[:, 256:512],
                dv_ref[:, 0:256], dv_ref[:, 256:512], dus_ref[...], dup_ref[...]]
        dh = jnp.zeros((TMM, D), F32)
        for j in range(NDEV):
            dz = cols[j].astype(MXU)
            dh = dh + _mm_nt(dz, w_ref[j])
            p = _mm_tn(hv, dz)

            @pl.when(i == 0)
            def _():
                acc_ref[j] = p

            @pl.when(i > 0)
            def _():
                acc_ref[j] += p

        dh_ref[...] = dh

        @pl.when(i == ni - 1)
        def _():
            dw_ref[...] = acc_ref[...].astype(BF16)

    return _call(
        body, name=name, grid=(ni,),
        in_specs=[_row_spec(512, TMM), _row_spec(512, TMM), _row_spec(512, TMM), _row_spec(256, TMM), _row_spec(256, TMM),
                  _row_spec(D, TMM), _full_spec((NDEV, D, 256))],
        out_specs=(_row_spec(D, TMM), _full_spec((NDEV, D, 256))),
        out_shape=(_sds((S, D), F32), _sds((NDEV, D, 256), BF16)),
        scratch=[pltpu.VMEM((NDEV, D, 256), F32)],
        cp=_cp(48, ("arbitrary",)), args=(*dparts, h, win))[0]


def wout_fwd(ya, ys, yp, wout, name, ride=None):
    def body(ya_ref, ys_ref, yp_ref, w_ref, o_ref):
        w = w_ref[...].reshape(D, D)
        o_ref[...] = _mm(ya_ref[...], w[0:512]) + _mm(ys_ref[...], w[512:768]) + _mm(yp_ref[...], w[768:1024])

    return _call(
        body, name=name, grid=(S // TMM,),
        in_specs=[_row_spec(512, TMM), _row_spec(256, TMM), _row_spec(256, TMM), _full_spec((NDEV, 128, D))],
        out_specs=[_row_spec(D, TMM)], out_shape=[_sds((S, D), F32)],
        cp=_cp(40, ("arbitrary",)), args=(ya, ys, yp, wout), ride=ride)


def wout_bwd(do, ya, ys, yp, wout, name):
    ni = S // TMM

    def body(do_ref, ya_ref, ys_ref, yp_ref, w_ref, dya_ref, dys_ref, dyp_ref, dw_ref, acc_ref):
        i = pl.program_id(0)
        w = w_ref[...].reshape(D, D)
        dov = do_ref[...]
        dya_ref[...] = _mm_nt(dov, w[0:512])
        dys_ref[...] = _mm_nt(dov, w[512:768])
        dyp_ref[...] = _mm_nt(dov, w[768:1024])
        parts = [(0, 512, _mm_tn(ya_ref[...], dov)), (512, 768, _mm_tn(ys_ref[...], dov)),
                 (768, 1024, _mm_tn(yp_ref[...], dov))]
        for lo, hi, p in parts:
            @pl.when(i == 0)
            def _():
                acc_ref[lo:hi, :] = p

            @pl.when(i > 0)
            def _():
                acc_ref[lo:hi, :] += p

        @pl.when(i == ni - 1)
        def _():
            dw_ref[...] = acc_ref[...].astype(BF16).reshape(NDEV, 128, D)

    return _call(
        body, name=name, grid=(ni,),
        in_specs=[_row_spec(D, TMM), _row_spec(512, TMM), _row_spec(256, TMM), _row_spec(256, TMM),
                  _full_spec((NDEV, 128, D))],
        out_specs=(_row_spec(512, TMM), _row_spec(256, TMM), _row_spec(256, TMM), _full_spec((NDEV, 128, D))),
        out_shape=(_sds((S, 512), F32), _sds((S, 256), F32), _sds((S, 256), F32), _sds((NDEV, 128, D), BF16)),
        scratch=[pltpu.VMEM((D, D), F32)],
        cp=_cp(40, ("arbitrary",)), args=(do, ya, ys, yp, wout))[0]


def _t5_bucket(dist):
    max_exact = N_BUCKETS // 2
    d = np.maximum(dist, 1).astype(np.float32)
    large = max_exact + (np.log(d / max_exact) / math.log(MAX_DISTANCE / max_exact)
                         * (N_BUCKETS - max_exact)).astype(np.int32)
    large = np.minimum(large, N_BUCKETS - 1)
    return np.where(dist < max_exact, dist, large).astype(np.int32)


def _att_static():
    i = np.arange(QB)[:, None]
    j = np.arange(2 * QB)[None, :]
    r = i + QB - j
    buckets, bands = [], []
    for window, dil in PATTERNS:
        bands.append((r >= 0) & (r <= window // dil))
        buckets.append(_t5_bucket(np.clip(r, 0, None) * dil))
    return np.stack(buckets), np.stack(bands), np.broadcast_to(j >= QB, (QB, 2 * QB))


def att_bias(rel_bias):
    m = np.arange(2 * QB)
    rows = []
    for window, dil in PATTERNS:
        r = QB - m
        ok = (r >= 0) & (r <= window // dil)
        b = rel_bias[_t5_bucket(np.clip(r, 0, None) * dil)]
        rows.append(jnp.where(ok[:, None], b, NEG).T)
    return jnp.broadcast_to(jnp.stack(rows)[:, :, None, :], (3, N_HEADS, 8, 2 * QB))


def _bias_tiles(t_ref, tiles):
    col = lax.broadcasted_iota(jnp.int32, (QB, 2 * QB), 1)
    for p in range(3):
        for hh in range(2):
            t = pltpu.roll(jnp.broadcast_to(t_ref[p, hh, 0:1, :], (QB, 2 * QB)), 0, 1, stride=1, stride_axis=0)
            tiles[p, hh, 0] = t
            tiles[p, hh, 1] = jnp.where(col >= QB, t, NEG)


def _permute_in(dst_ref, src_ref, d, scale=None, pad=QB):
    L = S // d
    for r in range(d):
        v = src_ref[pl.ds(r, L, stride=d), :] if d > 1 else src_ref[...]
        if scale is not None:
            v = v * scale
        dst_ref[pad + r * L:pad + (r + 1) * L, :] = v.astype(dst_ref.dtype)


def att_fwd(z, bias, name, ride=None):
    def body(q_ref, k_ref, v_ref, t_ref, y_ref, l_ref, qs, ks, vs, o_perm, l_perm, o_nat, l_nat, b_ref):
        _bias_tiles(t_ref, b_ref)
        zero_pad = jnp.zeros((QB, 128), MXU)
        ks[0:QB, :] = zero_pad
        vs[0:QB, :] = zero_pad
        lane = lax.broadcasted_iota(jnp.int32, (QB, 128), 1)
        for p, (_, d) in enumerate(PATTERNS):
            L = S // d
            nb = L // QB
            _permute_in(qs, q_ref, d, scale=0.125, pad=0)
            _permute_in(ks, k_ref, d)
            _permute_in(vs, v_ref, d)

            def blk(b, carry):
                r0 = pl.multiple_of(b * QB, QB)
                q = qs[pl.ds(r0, QB), :]
                kb = ks[pl.ds(r0, 2 * QB), :]
                vb = vs[pl.ds(r0, 2 * QB), :]
                first = ((b % nb) == 0).astype(jnp.int32)
                res = []
                for hh in range(2):
                    sel = (lane < 64) if hh == 0 else (lane >= 64)
                    qm = jnp.where(sel, q, jnp.zeros_like(q))
                    s = _mm_nt(qm, kb) + b_ref[p, hh, first]
                    m = jnp.max(s, axis=1, keepdims=True)
                    pe = jnp.exp(s - m)
                    den = jnp.sum(pe, axis=1, keepdims=True)
                    res.append((_mm(pe, vb) / den, m + jnp.log(den)))
                o_perm[pl.ds(r0, QB), :] = jnp.where(lane < 64, res[0][0], res[1][0])
                l_perm[pl.ds(r0, QB), :] = jnp.where(lane < 64, res[0][1], res[1][1])
                return carry

            lax.fori_loop(0, S // QB, blk, 0, unroll=8)
            for r in range(d):
                if d > 1:
                    o_nat[p, pl.ds(r, L, stride=d), :] = o_perm[r * L:(r + 1) * L, :]
                    l_nat[p, pl.ds(r, L, stride=d), :] = l_perm[r * L:(r + 1) * L, :]
                else:
                    o_nat[p] = o_perm[...]
                    l_nat[p] = l_perm[...]
        l0, l1, l2 = l_nat[0], l_nat[1], l_nat[2]
        m = jnp.maximum(jnp.maximum(l0, l1), l2)
        e0, e1, e2 = jnp.exp(l0 - m), jnp.exp(l1 - m), jnp.exp(l2 - m)
        den = e0 + e1 + e2
        y_ref[...] = (e0 * o_nat[0] + e1 * o_nat[1] + e2 * o_nat[2]) / den
        l_ref[...] = m + jnp.log(den)

    col = lambda c0: pl.BlockSpec((S, 128), lambda hp: (0, c0 + hp))
    return _call(
        body, name=name, grid=(N_HEADS // 2,),
        in_specs=[col(0), col(4), col(8), pl.BlockSpec((3, 2, 8, 2 * QB), lambda hp: (0, hp, 0, 0))],
        out_specs=(col(0), col(0)),
        out_shape=(_sds((S, D_ATT), F32), _sds((S, D_ATT), F32)),
        scratch=[pltpu.VMEM((S, 128), MXU), pltpu.VMEM((S + QB, 128), MXU), pltpu.VMEM((S + QB, 128), MXU),
                 pltpu.VMEM((S, 128), F32), pltpu.VMEM((S, 128), F32),
                 pltpu.VMEM((3, S, 128), F32), pltpu.VMEM((3, S, 128), F32),
                 pltpu.VMEM((3, 2, 2, QB, 2 * QB), F32)],
        cp=_cp(40, ("arbitrary",)), args=(z, z, z, bias), ride=ride)


def att_bwd(z, bias, y, lse, dy, name):
    def body(q_ref, k_ref, v_ref, t_ref, y_ref, l_ref, dy_ref, dq_ref, dk_ref, dv_ref, db_ref,
             qs, ks, vs, dys, ls, dds, dn_nat, dq_perm, dk_perm, dv_perm, b_ref):
        _bias_tiles(t_ref, b_ref)
        zero_pad = jnp.zeros((QB, 128), MXU)
        ks[0:QB, :] = zero_pad
        vs[0:QB, :] = zero_pad
        lane = lax.broadcasted_iota(jnp.int32, (QB, 128), 1)
        lane_s = lax.broadcasted_iota(jnp.int32, (S, 128), 1)
        t = dy_ref[...] * y_ref[...]
        sa = jnp.sum(jnp.where(lane_s < 64, t, 0.0), axis=1, keepdims=True)
        sb = jnp.sum(jnp.where(lane_s >= 64, t, 0.0), axis=1, keepdims=True)
        dn_nat[...] = jnp.where(lane_s < 64, sa, sb)
        dq_ref[...] = jnp.zeros((S, 128), F32)
        dk_ref[...] = jnp.zeros((S, 128), F32)
        dv_ref[...] = jnp.zeros((S, 128), F32)
        db_ref[...] = jnp.zeros((3, 2, QB, 2 * QB), F32)
        for p, (_, d) in enumerate(PATTERNS):
            L = S // d
            nb = L // QB
            _permute_in(qs, q_ref, d, scale=0.125, pad=0)
            _permute_in(ks, k_ref, d)
            _permute_in(vs, v_ref, d)
            _permute_in(dys, dy_ref, d, pad=0)
            _permute_in(ls, l_ref, d, pad=0)
            _permute_in(dds, dn_nat, d, pad=0)
            dk_perm[...] = jnp.zeros((S + QB, 128), F32)
            dv_perm[...] = jnp.zeros((S + QB, 128), F32)

            def blk(b, carry):
                r0 = pl.multiple_of(b * QB, QB)
                q = qs[pl.ds(r0, QB), :]
                kb = ks[pl.ds(r0, 2 * QB), :]
                vb = vs[pl.ds(r0, 2 * QB), :]
                dyb = dys[pl.ds(r0, QB), :]
                lb = ls[pl.ds(r0, QB), :]
                db = dds[pl.ds(r0, QB), :]
                first = ((b % nb) == 0).astype(jnp.int32)
                lane2 = jnp.concatenate([lane, lane], axis=0)
                own = (lane2 >> 6) == (lax.broadcasted_iota(jnp.int32, (2 * QB, 128), 0) >> 7)
                qm = jnp.where(own, jnp.concatenate([q, q], axis=0), jnp.zeros((2 * QB, 128), q.dtype))
                dym = jnp.where(own, jnp.concatenate([dyb, dyb], axis=0), jnp.zeros((2 * QB, 128), dyb.dtype))
                wide = lambda t: jnp.concatenate([jnp.broadcast_to(t[:, 0:1], (QB, 2 * QB)), jnp.broadcast_to(t[:, 64:65], (QB, 2 * QB))], axis=0)
                lse2, dd2 = wide(lb), wide(db)
                bias2 = jnp.concatenate([b_ref[p, 0, first], b_ref[p, 1, first]], axis=0)
                pr = jnp.exp(_mm_nt(qm, kb) + bias2 - lse2)
                ds = pr * (_mm_nt(dym, vb) - dd2)
                db_ref[p, 0] += ds[0:QB]
                db_ref[p, 1] += ds[QB:2 * QB]
                dq2 = _mm(ds, kb)
                dqs = [dq2[0:QB], dq2[QB:2 * QB]]
                dkb = _mm_tn(ds, qm)
                dvb = _mm_tn(pr, dym)
                dq_perm[pl.ds(r0, QB), :] = jnp.where(lane < 64, dqs[0], dqs[1])
                dk_perm[pl.ds(r0, 2 * QB), :] += dkb
                dv_perm[pl.ds(r0, 2 * QB), :] += dvb
                return carry

            lax.fori_loop(0, S // QB, blk, 0, unroll=4)
            for r in range(d):
                idx = pl.ds(r, L, stride=d) if d > 1 else pl.ds(0, S)
                dq_ref[idx, :] += dq_perm[r * L:(r + 1) * L, :] * 0.125
                dk_ref[idx, :] += dk_perm[QB + r * L:QB + (r + 1) * L, :]
                dv_ref[idx, :] += dv_perm[QB + r * L:QB + (r + 1) * L, :]

    col = lambda c0: pl.BlockSpec((S, 128), lambda hp: (0, c0 + hp))
    bspec = pl.BlockSpec((3, 2, 8, 2 * QB), lambda hp: (0, hp, 0, 0))
    return _call(
        body, name=name, grid=(N_HEADS // 2,),
        in_specs=[col(0), col(4), col(8), bspec, col(0), col(0), col(0)],
        out_specs=(col(0), col(0), col(0), pl.BlockSpec((3, 2, QB, 2 * QB), lambda hp: (0, hp, 0, 0))),
        out_shape=(_sds((S, D_ATT), F32), _sds((S, D_ATT), F32), _sds((S, D_ATT), F32),
                   _sds((3, N_HEADS, QB, 2 * QB), F32)),
        scratch=[pltpu.VMEM((S, 128), MXU), pltpu.VMEM((S + QB, 128), MXU), pltpu.VMEM((S + QB, 128), MXU),
                 pltpu.VMEM((S, 128), MXU), pltpu.VMEM((S, 128), F32), pltpu.VMEM((S, 128), F32),
                 pltpu.VMEM((S, 128), F32), pltpu.VMEM((S, 128), F32),
                 pltpu.VMEM((S + QB, 128), F32), pltpu.VMEM((S + QB, 128), F32),
                 pltpu.VMEM((3, 2, 2, QB, 2 * QB), F32)],
        cp=_cp(48, ("arbitrary",)), args=(z, z, z, bias, y, lse, dy))[0]


def relbias_grad(dbiases):
    bucket, band, _ = _att_static()
    onehot = (bucket[:, None] == np.arange(N_BUCKETS)[None, :, None, None]) & band[:, None]
    onehot = jnp.asarray(onehot.reshape(3, N_BUCKETS, QB * 2 * QB), BF16)

    def body(db0_ref, db1_ref, oh_ref, o_ref):
        acc = jnp.zeros((N_HEADS, N_BUCKETS), F32)
        for p in range(3):
            acc = acc + lax.dot_general(db0_ref[p] + db1_ref[p], oh_ref[p].astype(F32), (((1,), (1,)), ((), ())),
                                        preferred_element_type=F32, precision=lax.Precision.HIGHEST)
        o_ref[...] = acc

    vm = pl.BlockSpec(memory_space=pltpu.VMEM)
    out = _pallas_call(body, name="relbias_grad", in_specs=[vm, vm, vm], out_specs=vm,
                         out_shape=_sds((N_HEADS, N_BUCKETS), F32), compiler_params=_cp(40))(
        *[d.reshape(3, N_HEADS, QB * 2 * QB) for d in dbiases], onehot)
    return out.T


def _panel(t_ref, ri, j):
    return t_ref[ri, pl.ds(j, S, stride=8), :]


def _gelu(x):
    c = math.sqrt(2.0 / math.pi)
    th = jnp.tanh(c * (x + 0.044715 * x * x * x))
    return 0.5 * x * (1.0 + th), th


def ssm_fwd(z, a, bre, bim, cre, cim, dsk, gluw, glub, name, ride=None):
    def body(u_ref, a_ref, bre_ref, bim_ref, cre_ref, cim_ref, d_ref, gw_ref, gb_ref, y_ref, yp_ref, st_hbm, st_ref):
        u = u_ref[...]
        for j in range(8):
            st_ref[0, pl.ds(j, S, stride=8), :] = _mm(u, bre_ref[:, 128 * j:128 * (j + 1)])
            st_ref[1, pl.ds(j, S, stride=8), :] = _mm(u, bim_ref[:, 128 * j:128 * (j + 1)])
        ar, ai = a_ref[0], a_ref[1]

        def step(t, c):
            re, im = c
            i = pl.multiple_of(t * 8, 8)
            nre = ar * re - ai * im + st_ref[0, pl.ds(i, 8), :]
            nim = ar * im + ai * re + st_ref[1, pl.ds(i, 8), :]
            st_ref[0, pl.ds(i, 8), :] = nre
            st_ref[1, pl.ds(i, 8), :] = nim
            return nre, nim

        zero = jnp.zeros((8, 128), F32)
        lax.fori_loop(0, S, step, (zero, zero), unroll=8)
        y = d_ref[...] * u
        for j in range(8):
            y = y + _mm(_panel(st_ref, 0, j), cre_ref[128 * j:128 * (j + 1), :])
            y = y - _mm(_panel(st_ref, 1, j), cim_ref[128 * j:128 * (j + 1), :])
        pltpu.sync_copy(st_ref, st_hbm)
        yp_ref[...] = y
        gl, _ = _gelu(y)
        tt = _mm(gl, gw_ref[...].reshape(D_SSM, D_SSM)) + gb_ref[...]
        y_ref[...] = y * jax.nn.sigmoid(tt)

    vm = lambda shape: pl.BlockSpec(shape, lambda i: (0,) * len(shape))
    return _call(
        body, name=name, grid=(1,),
        in_specs=[pl.BlockSpec((S, 256), lambda i: (0, 6)), vm((2, 8, 128)), vm((256, 1024)), vm((256, 1024)),
                  vm((1024, 256)), vm((1024, 256)), vm((1, 256)),
                  vm((NDEV, 32, 256)), vm((1, 256))],
        out_specs=(vm((S, 256)), vm((S, 256)), pl.BlockSpec(memory_space=pl.ANY)),
        out_shape=(_sds((S, 256), F32), _sds((S, 256), F32), _sds((2, S * 8, 128), F32)),
        scratch=[pltpu.VMEM((2, S * 8, 128), F32)],
        cp=_cp(40, ("arbitrary",)), args=(z, a, bre, bim, cre, cim, dsk, gluw, glub), ride=ride)


def ssm_bwd(dy, z, ypre, st, a, bre, bim, cre, cim, dsk, gluw, glub, name):
    def body(dy_ref, u_ref, yp_ref, st_hbm, a_ref, bre_ref, bim_ref, cre_ref, cim_ref, d_ref, gw_ref, gb_ref,
             du_ref, dbre_ref, dbim_ref, dcre_ref, dcim_ref, da_ref, dd_ref, dgw_ref, dgb_ref, g_ref, st_ref):
        pltpu.sync_copy(st_hbm, st_ref)
        u = u_ref[...]
        y = yp_ref[...]
        dout = dy_ref[...]
        gw = gw_ref[...].reshape(D_SSM, D_SSM)
        gl, th = _gelu(y)
        sig = jax.nn.sigmoid(_mm(gl, gw) + gb_ref[...])
        dt = dout * y * sig * (1.0 - sig)
        dgw_ref[...] = _mm_tn(gl, dt)
        dgb_ref[...] = jnp.sum(dt, axis=0, keepdims=True)
        c = math.sqrt(2.0 / math.pi)
        dgelu = 0.5 * (1.0 + th) + 0.5 * y * (1.0 - th * th) * c * (1.0 + 3.0 * 0.044715 * y * y)
        dyv = dout * sig + _mm_nt(dt, gw) * dgelu
        dd_ref[...] = jnp.sum(dyv * u, axis=0, keepdims=True)
        for j in range(8):
            rows = slice(128 * j, 128 * (j + 1))
            g_ref[0, pl.ds(j, S, stride=8), :] = _mm_nt(dyv, cre_ref[rows, :])
            g_ref[1, pl.ds(j, S, stride=8), :] = -_mm_nt(dyv, cim_ref[rows, :])
            dcre_ref[rows, :] = _mm_tn(_panel(st_ref, 0, j), dyv)
            dcim_ref[rows, :] = -_mm_tn(_panel(st_ref, 1, j), dyv)
        ar, ai = a_ref[0], a_ref[1]

        def step(k, c4):
            gre, gim, dar, dai = c4
            i = pl.multiple_of((S - 1 - k) * 8, 8)
            nre = g_ref[0, pl.ds(i, 8), :] + ar * gre + ai * gim
            nim = g_ref[1, pl.ds(i, 8), :] + ar * gim - ai * gre
            g_ref[0, pl.ds(i, 8), :] = nre
            g_ref[1, pl.ds(i, 8), :] = nim
            sre = st_ref[0, pl.ds(i - 8, 8), :]
            sim = st_ref[1, pl.ds(i - 8, 8), :]
            return nre, nim, dar + nre * sre + nim * sim, dai + nim * sre - nre * sim

        zero = jnp.zeros((8, 128), F32)
        gre, gim, dar, dai = lax.fori_loop(0, S - 1, step, (zero, zero, zero, zero), unroll=8)
        g_ref[0, 0:8, :] = g_ref[0, 0:8, :] + ar * gre + ai * gim
        g_ref[1, 0:8, :] = g_ref[1, 0:8, :] + ar * gim - ai * gre
        da_ref[0] = dar
        da_ref[1] = dai
        du = dyv * d_ref[...]
        for j in range(8):
            cols = slice(128 * j, 128 * (j + 1))
            gr, gi = _panel(g_ref, 0, j), _panel(g_ref, 1, j)
            dbre_ref[:, cols] = _mm_tn(u, gr)
            dbim_ref[:, cols] = _mm_tn(u, gi)
            du = du + _mm_nt(gr, bre_ref[:, cols]) + _mm_nt(gi, bim_ref[:, cols])
        du_ref[...] = du

    vm = lambda shape: pl.BlockSpec(shape, lambda i: (0,) * len(shape))
    return _call(
        body, name=name, grid=(1,),
        in_specs=[vm((S, 256)), pl.BlockSpec((S, 256), lambda i: (0, 6)), vm((S, 256)), pl.BlockSpec(memory_space=pl.ANY),
                  vm((2, 8, 128)), vm((256, 1024)), vm((256, 1024)), vm((1024, 256)), vm((1024, 256)), vm((1, 256)),
                  vm((NDEV, 32, 256)), vm((1, 256))],
        out_specs=(vm((S, 256)), vm((256, 1024)), vm((256, 1024)), vm((1024, 256)), vm((1024, 256)),
                   vm((2, 8, 128)), vm((1, 256)), vm((256, 256)), vm((1, 256))),
        out_shape=(_sds((S, 256), F32), _sds((256, 1024), F32), _sds((256, 1024), F32), _sds((1024, 256), F32),
                   _sds((1024, 256), F32), _sds((2, 8, 128), F32), _sds((1, 256), F32), _sds((256, 256), F32),
                   _sds((1, 256), F32)),
        scratch=[pltpu.VMEM((2, S * 8, 128), F32), pltpu.VMEM((2, S * 8, 128), F32)],
        cp=_cp(56, ("arbitrary",)), args=(dy, z, ypre, st, a, bre, bim, cre, cim, dsk, gluw, glub))[0]


def _ssm_discretise(a_re, a_im, log_dt, b_re, b_im):
    dt = jnp.exp(log_dt)[:, None]
    er = jnp.exp(a_re * dt)
    abr, abi = er * jnp.cos(a_im * dt), er * jnp.sin(a_im * dt)
    den = a_re * a_re + a_im * a_im
    fr = ((abr - 1.0) * a_re + abi * a_im) / den
    fi = (abi * a_re - (abr - 1.0) * a_im) / den
    bbr = fr[:, :, None] * b_re - fi[:, :, None] * b_im
    bbi = fr[:, :, None] * b_im + fi[:, :, None] * b_re
    return abr, abi, bbr, bbi


def _blockdiag(t):
    g, r, c = t.shape
    eye = jnp.eye(g, dtype=t.dtype)
    return (t[:, :, None, :] * eye[:, None, :, None]).reshape(g * r, g * c)


def _blockdiag_take(m, r, c):
    g = m.shape[0] // r
    idx = jnp.arange(g)
    return m.reshape(g, r, g, c)[idx, :, idx, :]


PAD = 16


def _pool_lane_select(vals):
    lane = lax.broadcasted_iota(jnp.int32, vals[0].shape, 1)
    out = vals[3]
    for g in (2, 1, 0):
        out = jnp.where(lane < 64 * (g + 1), vals[g], out)
    return out


def _pool_counts():
    row = lax.broadcasted_iota(jnp.int32, (S, D_POOL), 0).astype(F32) + 1.0
    return _pool_lane_select([jnp.minimum(row, float(w)) for w in POOL_WINDOWS])


def _pooled(u, sa, sb):
    sums = []
    cur = u
    bufs = (sa, sb)
    for k, sh in enumerate((1, 2, 4, 8)):
        buf = bufs[k % 2]
        buf[PAD:PAD + S, :] = cur
        cur = cur + buf[PAD - sh:PAD - sh + S, :]
        sums.append(cur)
    return _pool_lane_select(sums) / _pool_counts() - u


def pool_fwd(z, pw, psc, name):
    def body(u_ref, w_ref, s_ref, y_ref, sa, sb):
        for buf in (sa, sb):
            buf[0:PAD, :] = jnp.zeros((PAD, D_POOL), F32)
        pooled = _pooled(u_ref[...], sa, sb)
        y_ref[...] = _mm(pooled, w_ref[...]) * s_ref[...]

    vm = lambda shape: pl.BlockSpec(shape, lambda i: (0,) * len(shape))
    return _pallas_call(
        body, name=name, grid=(1,),
        in_specs=[pl.BlockSpec((S, 256), lambda i: (0, 7)), vm((256, 256)), vm((1, 256))],
        out_specs=vm((S, 256)), out_shape=_sds((S, 256), F32),
        scratch_shapes=[pltpu.VMEM((S + 2 * PAD, D_POOL), F32)] * 2,
        compiler_params=_cp(40, ("arbitrary",)))(z, pw, psc)


def pool_bwd(dy, z, pw, psc, name):
    def body(dy_ref, u_ref, w_ref, s_ref, du_ref, dw_ref, ds_ref, sa, sb):
        for buf in (sa, sb):
            buf[0:PAD, :] = jnp.zeros((PAD, D_POOL), F32)
            buf[PAD + S:PAD + S + PAD, :] = jnp.zeros((PAD, D_POOL), F32)
        pooled = _pooled(u_ref[...], sa, sb)
        dyv = dy_ref[...]
        w = w_ref[...]
        ds_ref[...] = jnp.sum(dyv * _mm(pooled, w), axis=0, keepdims=True)
        dyl = dyv * s_ref[...]
        dw_ref[...] = _mm_tn(pooled, dyl)
        dpool = _mm_nt(dyl, w)
        cur = dpool / _pool_counts()
        sums = []
        bufs = (sa, sb)
        for k, sh in enumerate((1, 2, 4, 8)):
            buf = bufs[k % 2]
            buf[PAD:PAD + S, :] = cur
            cur = cur + buf[PAD + sh:PAD + sh + S, :]
            sums.append(cur)
        du_ref[...] = _pool_lane_select(sums) - dpool

    vm = lambda shape: pl.BlockSpec(shape, lambda i: (0,) * len(shape))
    return _pallas_call(
        body, name=name, grid=(1,),
        in_specs=[vm((S, 256)), pl.BlockSpec((S, 256), lambda i: (0, 7)), vm((256, 256)), vm((1, 256))],
        out_specs=(vm((S, 256)), vm((256, 256)), vm((1, 256))),
        out_shape=(_sds((S, 256), F32), _sds((256, 256), F32), _sds((1, 256), F32)),
        scratch_shapes=[pltpu.VMEM((S + 2 * PAD, D_POOL), F32)] * 2,
        compiler_params=_cp(40, ("arbitrary",)))(dy, z, pw, psc)


def ada_fwd(c_all, ada_w, ada_b_cols):
    def body(c_ref, w_ref, b_ref, o_ref):
        c = c_ref[...]
        cond = c * jax.nn.sigmoid(c)
        o_ref[...] = jnp.dot(cond, w_ref[...], preferred_element_type=F32, precision=lax.Precision.HIGHEST) + b_ref[...]

    return _pallas_call(
        body, name="ada_fwd", grid=(DEPTH,),
        in_specs=[pl.BlockSpec((NDEV, D), lambda l: (0, 0)), pl.BlockSpec((None, D, 1152), lambda l: (l, 0, 0)),
                  pl.BlockSpec((None, 1, 1152), lambda l: (l, 0, 0))],
        out_specs=pl.BlockSpec((None, NDEV, 1152), lambda l: (l, 0, 0)), out_shape=_sds((DEPTH, NDEV, 1152), F32),
        compiler_params=_cp(40, ("arbitrary",)))(c_all, ada_w, ada_b_cols)


def ada_bwd(c_all, dmod_cols):
    def body(c_ref, dm_ref, o_ref):
        c = c_ref[...]
        cond = c * jax.nn.sigmoid(c)
        o_ref[...] = lax.dot_general(cond, dm_ref[...], (((0,), (0,)), ((), ())), preferred_element_type=F32,
                                     precision=lax.Precision.HIGHEST)

    return _pallas_call(
        body, name="ada_bwd", grid=(DEPTH,),
        in_specs=[pl.BlockSpec((NDEV, D), lambda l: (0, 0)), pl.BlockSpec((None, NDEV, 1152), lambda l: (l, 0, 0))],
        out_specs=pl.BlockSpec((None, D, 1152), lambda l: (l, 0, 0)), out_shape=_sds((DEPTH, D, 1152), F32),
        compiler_params=_cp(40, ("arbitrary",)))(c_all, dmod_cols)


def _adamw(w, g, m, v):
    m2 = B1 * m + (1.0 - B1) * g
    v2 = B2 * v + (1.0 - B2) * (g * g)
    m_hat = m2 / (1.0 - B1 ** STEP)
    v_hat = v2 / (1.0 - B2 ** STEP)
    return -LR * (m_hat / (jnp.sqrt(v_hat) + EPS) + WD * w), m2, v2


def _sum8(ref):
    g = ref[0].astype(F32)
    for s in range(1, ref.shape[0]):
        g = g + ref[s].astype(F32)
    return g


def adam_rs(recv, w, m, v, tr, name):
    lead, (r, cdim) = w.shape[:-2], w.shape[-2:]
    cp = recv.shape[-1]
    nl = len(lead)

    def body(rc_ref, w_ref, m_ref, v_ref, g_ref, d_ref, m2_ref, v2_ref):
        g = _sum8(rc_ref)[:, :cdim]
        g_ref[...] = g
        d_ref[...], m2_ref[...], v2_ref[...] = _adamw(w_ref[...], g, m_ref[...], v_ref[...])

    rs = pl.BlockSpec((None,) * nl + (tr, cdim), lambda *i: (*i, 0))
    return _call(
        body, name=name, grid=lead + (r // tr,),
        in_specs=[pl.BlockSpec((NDEV,) + (None,) * nl + (tr, cp), lambda *i: (0, *i, 0)), rs, rs, rs],
        out_specs=(rs, rs, rs, rs), out_shape=tuple(_sds(w.shape, F32) for _ in range(4)),
        cp=_cp(48, ("arbitrary",) * (nl + 1)), args=(recv, w, m, v))[0]


def adam_block(recv, w, m, v, lf, prev, name):
    half = FB // 2

    def body(*refs):
        rc_ref, w_ref, m_ref, v_ref = refs[:4]
        g_ref, d_ref, m2_ref, v2_ref = refs[-4:]
        g = _sum8(rc_ref)
        g_ref[...] = g
        d_ref[...], m2_ref[...], v2_ref[...] = _adamw(w_ref[...], g, m_ref[...], v_ref[...])

    rs = pl.BlockSpec((None, None, half, D), lambda i: (lf // 2, lf % 2, i, 0))
    prev = list(prev) if prev is not None else []
    return list(_pallas_call(
        body, name=name, grid=(2,), in_specs=[pl.BlockSpec((recv.shape[0], half, D), lambda i: (0, i, 0)), rs, rs, rs] + [ANY] * len(prev),
        out_specs=(rs, rs, rs, rs), out_shape=tuple(_sds((DEPTH, 2, FB, D), F32) for _ in range(4)),
        input_output_aliases={4 + k: k for k in range(len(prev))},
        compiler_params=_cp(48, ("arbitrary",)))(recv, w, m, v, *prev))


def adam_plain(g, w, m, v, tr, name):
    lead, (r, cdim) = w.shape[:-2], w.shape[-2:]
    nl = len(lead)

    def body(g_ref, w_ref, m_ref, v_ref, d_ref, m2_ref, v2_ref):
        d_ref[...], m2_ref[...], v2_ref[...] = _adamw(w_ref[...], g_ref[...], m_ref[...], v_ref[...])

    rs = pl.BlockSpec((None,) * nl + (tr, cdim), lambda *i: (*i, 0))
    return _call(
        body, name=name, grid=lead + (r // tr,), in_specs=[rs, rs, rs, rs], out_specs=(rs, rs, rs),
        out_shape=tuple(_sds(w.shape, F32) for _ in range(3)),
        cp=_cp(48, ("arbitrary",) * (nl + 1)), args=(g, w, m, v))[0]


def adam_native(gs, ws, ms, vs, name):
    n = len(ws)

    def body(*refs):
        g_refs, w_refs, m_refs, v_refs = (refs[k * n:(k + 1) * n] for k in range(4))
        d_refs, m2_refs, v2_refs = (refs[(4 + k) * n:(5 + k) * n] for k in range(3))
        for a in range(n):
            d_refs[a][...], m2_refs[a][...], v2_refs[a][...] = _adamw(w_refs[a][...], g_refs[a][...], m_refs[a][...], v_refs[a][...])

    vm = pl.BlockSpec(memory_space=pltpu.VMEM)
    outs = _pallas_call(body, name=name, in_specs=[vm] * (4 * n), out_specs=tuple([vm] * (3 * n)),
                        out_shape=tuple(_sds(w.shape, F32) for w in ws) * 3, compiler_params=_cp(40))(*gs, *ws, *ms, *vs)
    return outs[:n], outs[n:2 * n], outs[2 * n:]


def sum_sources(recv, name):
    r = recv.shape[1]

    def body(rc_ref, o_ref):
        o_ref[...] = _sum8(rc_ref)

    vm = pl.BlockSpec(memory_space=pltpu.VMEM)
    return _pallas_call(body, name=name, in_specs=[vm], out_specs=vm, out_shape=_sds((r, 128), F32),
                          compiler_params=_cp(40))(recv)


def _pack(arrs):
    flat = jnp.concatenate([a.reshape(-1) for a in arrs])
    n = flat.shape[0]
    rows = -(-n // 1024) * 8
    return jnp.pad(flat, (0, rows * 128 - n)).reshape(rows, 128)


def _unpack(vec, shapes):
    flat = vec.reshape(-1)
    out, o = [], 0
    for sh in shapes:
        n = int(np.prod(sh))
        out.append(flat[o:o + n].reshape(sh))
        o += n
    return out


WEIGHTS = ['rel_bias', 'ada_w', 'ada_b', 'ln_g', 'ln_b', 'ffn_w_gate', 'ffn_w_up', 'ffn_w_down', 'w_in', 'w_out',
           'ssm_a_re', 'ssm_a_im', 'ssm_log_dt', 'ssm_b_re', 'ssm_b_im', 'ssm_c_re', 'ssm_c_im', 'ssm_d', 'glu_w',
           'glu_b', 'pool_w', 'pool_scale']
SMALL = ['rel_bias', 'ada_b', 'ln_g', 'ln_b', 'ssm_a_re', 'ssm_a_im', 'ssm_log_dt', 'ssm_b_re', 'ssm_b_im',
         'ssm_c_re', 'ssm_c_im', 'ssm_d', 'glu_b', 'pool_w', 'pool_scale']
SMALL_FULL_SHAPES = {'rel_bias': (32, 8), 'ada_b': (2, 9216), 'ln_g': (2, 3, 1024), 'ln_b': (2, 3, 1024),
                     'ssm_a_re': (2, 16, 64), 'ssm_a_im': (2, 16, 64), 'ssm_log_dt': (2, 16),
                     'ssm_b_re': (2, 16, 64, 16), 'ssm_b_im': (2, 16, 64, 16), 'ssm_c_re': (2, 16, 16, 64),
                     'ssm_c_im': (2, 16, 16, 64), 'ssm_d': (2, 256), 'glu_b': (2, 256), 'pool_w': (2, 4, 64, 64),
                     'pool_scale': (2, 256)}


def _step(P):
    me = _me()
    x0 = P['x'][0]
    target = P['loss_target'][0]

    bf = lambda a: a.astype(BF16)
    padr = lambda a: jnp.pad(bf(a), ((0, 0), (0, 0), (0, FBP - FB), (0, 0)))
    ffn_b = [padr(jnp.swapaxes(P['ffn_w_gate'], 2, 3)), padr(jnp.swapaxes(P['ffn_w_up'], 2, 3)), padr(P['ffn_w_down'])]
    mix_b = [bf(P['w_in']), bf(P['w_out']), bf(P['glu_w'])]

    def shards(l, sub):
        return [t[l] for t in mix_b] if sub == 1 else [t[l, sub // 2] for t in ffn_b]

    order = [(l, sub) for l in range(DEPTH) for sub in range(3)]
    nxt = dict(zip(order[:-1], order[1:]))
    W = {key: [None] * 3 for key in order}
    c_all, lng_all, lnb_all, *W[order[0]] = _exchange(Gather([P['c'], P['ln_g'], P['ln_b']] + shards(*order[0])), "gather_first")
    gather_queue = [(key, pos, a) for key in order[1:] for pos, a in enumerate(shards(*key))]

    def gather_ride(cap_us, must=None):
        units, used = [], 0.0
        while gather_queue:
            key, _, a = gather_queue[0]
            cost = a.size * a.dtype.itemsize * GATHER_US_PER_BYTE
            if key != must and used + cost / 2 > cap_us:
                break
            units.append(gather_queue.pop(0))
            used += cost
        return (Gather([a for _, _, a in units]) if units else None), units

    def gathered(units, outs):
        for (key, pos, _), o in zip(units, outs):
            W[key][pos] = o

    c_all = c_all.reshape(NDEV, D)
    ln_g = jnp.transpose(lng_all, (1, 2, 0, 3)).reshape(DEPTH, 3, D)
    ln_b = jnp.transpose(lnb_all, (1, 2, 0, 3)).reshape(DEPTH, 3, D)

    ada_b_cols = lax.dynamic_slice_in_dim(P['ada_b'], me * 1152, 1152, axis=1).reshape(DEPTH, 1, 1152)
    modc = ada_fwd(c_all, P['ada_w'], ada_b_cols)
    (mod_all,) = _exchange(Gather([modc]), "gather_mod")
    mod_me = lax.dynamic_index_in_dim(mod_all, me, axis=2, keepdims=False)
    mod = jnp.transpose(mod_me, (1, 0, 2)).reshape(DEPTH, 9, D)

    bias = att_bias(P['rel_bias'])
    ssm = []
    for l in range(DEPTH):
        prm = (P['ssm_a_re'][l], P['ssm_a_im'][l], P['ssm_log_dt'][l], P['ssm_b_re'][l], P['ssm_b_im'][l])
        (abr, abi, bbr, bbi), disc_vjp = jax.vjp(_ssm_discretise, *prm)
        ssm.append(dict(
            vjp=disc_vjp, a=jnp.stack([abr.reshape(8, 128), abi.reshape(8, 128)]),
            bre=_blockdiag(jnp.transpose(bbr, (0, 2, 1))).astype(MXU), bim=_blockdiag(jnp.transpose(bbi, (0, 2, 1))).astype(MXU),
            cre=_blockdiag(jnp.transpose(P['ssm_c_re'][l], (0, 2, 1))).astype(MXU),
            cim=_blockdiag(jnp.transpose(P['ssm_c_im'][l], (0, 2, 1))).astype(MXU),
            d=P['ssm_d'][l].reshape(1, 256), gb=P['glu_b'][l].reshape(1, 256),
            pw=_blockdiag(P['pool_w'][l]).astype(MXU), psc=P['pool_scale'][l].reshape(1, 256)))

    saved = []
    x = x0
    h = ln_mod_fwd(x, mod[0], 0, "ln_mod_fwd_l0s0")
    for l, sub in order:
        tag = f"l{l}s{sub}"
        after = (mod[nxt[(l, sub)][0]], nxt[(l, sub)][1]) if (l, sub) in nxt else None
        if sub != 1:
            wg, wu, wd = (t.reshape(NDEV * FBP, D) for t in W[(l, sub)])
            ride, units = gather_ride(60, nxt.get((l, sub)))
            (G, U, fo), got = ffn_fwd(h, wg, wu, wd, "ffn_fwd_" + tag, ride)
            gathered(units, got)
            saved.append(dict(x=x, h=h, G=G, U=U, f=fo))
            x, *hn = res_ln_fwd(x, fo, mod[l], sub, ln_g[l], ln_b[l], 0.5, "res_ln_fwd_" + tag, after)
        else:
            sp = ssm[l]
            win, wout, gluw = W[(l, sub)]
            ride, units = gather_ride(15)
            (z,), got = win_fwd(h, win, "win_fwd_" + tag, ride)
            gathered(units, got)
            ride, units = gather_ride(55)
            (ya, lse), got = att_fwd(z, bias, "att_fwd_" + tag, ride)
            gathered(units, got)
            ride, units = gather_ride(35)
            (ys, ypre, st), got = ssm_fwd(z, sp['a'], sp['bre'], sp['bim'], sp['cre'], sp['cim'], sp['d'], gluw, sp['gb'],
                                          "ssm_fwd_" + tag, ride)
            gathered(units, got)
            yp = pool_fwd(z, sp['pw'], sp['psc'], "pool_fwd_" + tag)
            ride, units = gather_ride(12, nxt.get((l, sub)))
            (o,), got = wout_fwd(ya, ys, yp, wout, "wout_fwd_" + tag, ride)
            gathered(units, got)
            saved.append(dict(x=x, h=h, z=z, ya=ya, lse=lse, ys=ys, ypre=ypre, st=st, yp=yp, f=o))
            x, *hn = res_ln_fwd(x, o, mod[l], sub, ln_g[l], ln_b[l], 1.0, "res_ln_fwd_" + tag, after)
        h = hn[0] if hn else None
    assert not gather_queue

    loss_tile, dx = loss_fwd_bwd(x, target, "loss")
    loss = lax.psum(loss_tile[0, 0], ("x", "y", "c"))

    flights = []

    dmod = [[None] * 9 for _ in range(DEPTH)]
    dlng = [[None] * 3 for _ in range(DEPTH)]
    dlnb = [[None] * 3 for _ in range(DEPTH)]
    dbiases = [None] * DEPTH
    small_l = [dict() for _ in range(DEPTH)]
    for l, sub in reversed(order):
        tag = f"l{l}s{sub}"
        sv = saved[3 * l + sub]
        if (l, sub) == order[-1]:
            dxa, df, sums = res_ln_bwd(sv['x'], sv['f'], mod[l], sub, ln_g[l], dx, 0.5, "res_ln_bwd_" + tag)
        dlng[l][sub], dlnb[l][sub], dmod[l][3 * sub + 2] = sums[0], sums[1], sums[2]
        if sub != 1:
            f = sub // 2
            wg, wu, wd = (t.reshape(NDEV * FBP, D) for t in W[(l, sub)])
            dwg, dwu, dwd, dh = ffn_bwd(df, sv['h'], sv['G'], sv['U'], wg, wu, wd, "ffn_bwd_" + tag)
            handle, zero = scatter_start([t.reshape(NDEV, FBP, D) for t in (dwg, dwu, dwd)], "scatter_start_" + tag, rows=FB)
            flights.append(((l, sub), handle))
        else:
            sp = ssm[l]
            win, wout, gluw = W[(l, sub)]
            dya, dys, dyp, dwout = wout_bwd(df, sv['ya'], sv['ys'], sv['yp'], wout, "wout_bwd_" + tag)
            dq, dk, dv, dbiases[l] = att_bwd(sv['z'], bias, sv['ya'], sv['lse'], dya, "att_bwd_" + tag)
            dus, dbre, dbim, dcre, dcim, da, dd, dgw, dgb = ssm_bwd(
                dys, sv['z'], sv['ypre'], sv['st'], sp['a'], sp['bre'], sp['bim'], sp['cre'], sp['cim'], sp['d'],
                gluw, sp['gb'], "ssm_bwd_" + tag)
            dup, dpw, dpsc = pool_bwd(dyp, sv['z'], sp['pw'], sp['psc'], "pool_bwd_" + tag)
            dh, dwin = win_bwd((dq, dk, dv, dus, dup), sv['h'], win, "win_bwd_" + tag)
            handle, zero = scatter_start([dwin, dwout, dgw.astype(BF16).reshape(NDEV, 32, 256)], "scatter_start_" + tag)
            flights.append(((l, sub), handle))
            d_are, d_aim, d_ldt, d_bre, d_bim = sp['vjp']((
                da[0].reshape(16, 64), da[1].reshape(16, 64),
                jnp.transpose(_blockdiag_take(dbre, 16, 64), (0, 2, 1)), jnp.transpose(_blockdiag_take(dbim, 16, 64), (0, 2, 1))))
            small_l[l] = dict(
                ssm_a_re=d_are, ssm_a_im=d_aim, ssm_log_dt=d_ldt, ssm_b_re=d_bre, ssm_b_im=d_bim,
                ssm_c_re=jnp.transpose(_blockdiag_take(dcre, 64, 16), (0, 2, 1)),
                ssm_c_im=jnp.transpose(_blockdiag_take(dcim, 64, 16), (0, 2, 1)),
                ssm_d=dd.reshape(256), glu_b=dgb.reshape(256), pool_w=_blockdiag_take(dpw, 64, 64), pool_scale=dpsc.reshape(256))
        if (l, sub) == order[0]:
            dx, sums2 = ln_mod_bwd(sv['x'], dh, mod[l] + zero, sub, dxa, "ln_mod_bwd_" + tag)
        else:
            lp, sp_ = order[order.index((l, sub)) - 1]
            svp = saved[3 * lp + sp_]
            dxa, df, sums, sums2 = ln_join_bwd(svp['x'], svp['f'], mod[lp], sp_, ln_g[lp], ln_b[lp], 1.0 if sp_ == 1 else 0.5,
                                               dh, mod[l] + zero, sub, dxa, "ln_join_bwd_" + tag)
        dmod[l][3 * sub], dmod[l][3 * sub + 1] = sums2[0], sums2[1]
    grad_x = dx[None]

    small = {k: jnp.stack([small_l[l][k] for l in range(DEPTH)]) for k in small_l[0]}
    small['rel_bias'] = relbias_grad(dbiases)
    small['ada_b'] = jnp.stack([jnp.stack(dmod[l]).reshape(9 * D) for l in range(DEPTH)])
    small['ln_g'] = jnp.stack([jnp.stack(dlng[l]) for l in range(DEPTH)])
    small['ln_b'] = jnp.stack([jnp.stack(dlnb[l]) for l in range(DEPTH)])
    small_flight, _ = scatter_start([_pack([small[k] for k in SMALL])], "gather_small_start", whole=True)

    out = {}

    def put(name, g, d, m2, v2, shape):
        out['grad_' + name], out['delta_' + name] = g.reshape(shape), d.reshape(shape)
        out['new_m_' + name], out['new_v_' + name] = m2.reshape(shape), v2.reshape(shape)

    def wmv(name):
        return [P[pre + name] for pre in ('', 'm_', 'v_')]

    recv = {}
    started_last = small_flight[1][0]
    for key, handle in flights[:-1]:
        recv[key] = scatter_wait(handle, started_last, "scatter_wait_l%ds%d" % key)
    for pos, (name, tr) in enumerate((('w_in', 512), ('w_out', 128), ('glu_w', 32))):
        both = jnp.stack([recv[(l, 1)][pos] for l in range(DEPTH)], axis=1)
        put(name, *adam_rs(both, *wmv(name), tr, "adam_" + name), P[name].shape)
    ffn = (('ffn_w_gate', [jnp.swapaxes(t, 2, 3) for t in wmv('ffn_w_gate')]),
           ('ffn_w_up', [jnp.swapaxes(t, 2, 3) for t in wmv('ffn_w_up')]), ('ffn_w_down', wmv('ffn_w_down')))
    part = [None] * 3
    for l, sub in [key for key, _ in flights[:-1] if key[1] != 1]:
        for pos, (name, ops) in enumerate(ffn):
            part[pos] = adam_block(recv[(l, sub)][pos], *ops, 2 * l + sub // 2, part[pos], f"adam_{name}_l{l}s{sub}")

    (small_all,) = scatter_wait(small_flight, part[2][0], "gather_small_wait")
    gsum = dict(zip(SMALL, _unpack(sum_sources(small_all, "sum_small"), [SMALL_FULL_SHAPES[k] for k in SMALL])))
    off = 256
    dmod_all = small_all.reshape(NDEV, -1)[:, off:off + DEPTH * 9 * D].reshape(NDEV, DEPTH, 9 * D)
    dmod_cols = jnp.transpose(lax.dynamic_slice_in_dim(dmod_all, me * 1152, 1152, axis=2), (1, 0, 2))
    g_ada_w = ada_bwd(c_all, dmod_cols)

    put('ada_w', g_ada_w, *adam_plain(g_ada_w, *wmv('ada_w'), 256, "adam_ada_w"), P['ada_w'].shape)

    for k in ('ln_g', 'ln_b'):
        gsum[k] = lax.dynamic_slice_in_dim(gsum[k], me * 128, 128, axis=2)
    swaps = {'rel_bias': (0, 1), 'ln_g': (0, 1), 'ln_b': (0, 1), 'ssm_b_re': (2, 3), 'ssm_b_im': (2, 3)}
    view = lambda k, t: jnp.swapaxes(t, *swaps[k]) if k in swaps else t
    ds_, m2s, v2s = adam_native(*[[view(k, src(k)) for k in SMALL] for src in
                                  (lambda k: gsum[k], lambda k: P[k], lambda k: P['m_' + k], lambda k: P['v_' + k])],
                                "adam_small")
    for k, d, m2, v2 in zip(SMALL, ds_, m2s, v2s):
        put(k, gsum[k], view(k, d), view(k, m2), view(k, v2), P[k].shape)

    (l, sub), handle = flights[-1]
    last = scatter_wait(handle, out['new_v_ada_w'], "scatter_wait_l%ds%d" % (l, sub))
    for pos, (name, ops) in enumerate(ffn):
        res = adam_block(last[pos], *ops, 2 * l + sub // 2, part[pos], f"adam_{name}_l{l}s{sub}")
        put(name, *([jnp.swapaxes(t, 2, 3) for t in res] if pos < 2 else res), P[name].shape)

    res = [loss, grad_x]
    for pre in ('grad_', 'delta_', 'new_m_', 'new_v_'):
        res += [out[pre + k] for k in WEIGHTS]
    return tuple(res)


def kernel(x, c, rel_bias, ada_w, ada_b, ln_g, ln_b, ffn_w_gate, ffn_w_up, ffn_w_down, w_in, w_out, ssm_a_re, ssm_a_im, ssm_log_dt, ssm_b_re, ssm_b_im, ssm_c_re, ssm_c_im, ssm_d, glu_w, glu_b, pool_w, pool_scale, loss_target, m_rel_bias, m_ada_w, m_ada_b, m_ln_g, m_ln_b, m_ffn_w_gate, m_ffn_w_up, m_ffn_w_down, m_w_in, m_w_out, m_ssm_a_re, m_ssm_a_im, m_ssm_log_dt, m_ssm_b_re, m_ssm_b_im, m_ssm_c_re, m_ssm_c_im, m_ssm_d, m_glu_w, m_glu_b, m_pool_w, m_pool_scale, v_rel_bias, v_ada_w, v_ada_b, v_ln_g, v_ln_b, v_ffn_w_gate, v_ffn_w_up, v_ffn_w_down, v_w_in, v_w_out, v_ssm_a_re, v_ssm_a_im, v_ssm_log_dt, v_ssm_b_re, v_ssm_b_im, v_ssm_c_re, v_ssm_c_im, v_ssm_d, v_glu_w, v_glu_b, v_pool_w, v_pool_scale):
    return _step(dict(locals()))
```

```python
import functools
import math

import numpy as np
import jax
import jax.numpy as jnp
from jax import lax
from jax.experimental import pallas as pl
from jax.experimental.pallas import tpu as pltpu

F32 = jnp.float32
BF16 = jnp.bfloat16
MXU = jnp.bfloat16

S = 2048
D = 1024
NDEV = 8
DEPTH = 2
D_ATT, D_SSM, D_POOL, D_IN = 512, 256, 256, 2048
N_HEADS = 8
FB = 352
FBP = 384
QB = 128
PATTERNS = ((128, 1), (512, 4), (2048, 16))
POOL_WINDOWS = (2, 4, 8, 16)
N_BUCKETS, MAX_DISTANCE = 32, 2048
ALPHA = (2 * DEPTH) ** 0.25
LN_EPS = 1e-5
NEG = -1e30
GATHER_US_PER_BYTE = 43e-6
LR, B1, B2, EPS, WD, STEP = 0.001, 0.9, 0.999, 1e-08, 0.01, 10

TM = 256
TMM = 512
MIB = 1024 * 1024


def _cp(vmem_mib, sem=None):
    kw = dict(vmem_limit_bytes=vmem_mib * MIB)
    if sem is not None:
        kw["dimension_semantics"] = sem
    return pltpu.CompilerParams(**kw)


def _sds(shape, dtype):
    return jax.ShapeDtypeStruct(shape, dtype)


def _mm(a, b):
    return jnp.dot(a.astype(MXU), b.astype(MXU), preferred_element_type=F32)


def _mm_nt(a, b):
    return lax.dot_general(a.astype(MXU), b.astype(MXU), (((1,), (1,)), ((), ())), preferred_element_type=F32)


def _mm_tn(a, b):
    return lax.dot_general(a.astype(MXU), b.astype(MXU), (((0,), (0,)), ((), ())), preferred_element_type=F32)


def _ln_stats(x):
    mu = jnp.mean(x, axis=-1, keepdims=True)
    xc = x - mu
    var = jnp.mean(xc * xc, axis=-1, keepdims=True)
    rstd = lax.rsqrt(var + LN_EPS)
    return xc * rstd, rstd


def _ln_bwd(dn, n, rstd):
    return rstd * (dn - jnp.mean(dn, axis=-1, keepdims=True) - n * jnp.mean(dn * n, axis=-1, keepdims=True))


def _me():
    return 4 * lax.axis_index("x") + 2 * lax.axis_index("y") + lax.axis_index("c")


ANY = pl.BlockSpec(memory_space=pl.ANY)
PIN_BYTES = 1 << 19


def _pallas_call(*a, **k):
    big = lambda o: math.prod(o.shape) * o.dtype.itemsize >= PIN_BYTES
    pin = lambda o: pltpu.HBM(o.shape, o.dtype) if isinstance(o, jax.ShapeDtypeStruct) and big(o) else o
    osh = k["out_shape"]
    k["out_shape"] = tuple(pin(o) for o in osh) if isinstance(osh, (tuple, list)) else pin(osh)
    fn = pl.pallas_call(*a, **k)

    def run(*args):
        return fn(*[pltpu.with_memory_space_constraint(x, pltpu.HBM) if big(x) else x for x in args])
    return run


class Gather:
    def __init__(self, srcs):
        self.srcs = list(srcs)
        self.n = len(self.srcs)
        self.bufs = []
        self.out_shapes = [_sds((NDEV,) + a.shape, a.dtype) for a in self.srcs]
        self.sems = [pltpu.SemaphoreType.DMA((7 * self.n,)), pltpu.SemaphoreType.DMA((7 * self.n,)),
                     pltpu.SemaphoreType.DMA((self.n,))]

    def _parts(self, srcs, outs, sems):
        send_sems, recv_sems, loc_sems = sems
        x, y, c = lax.axis_index("x"), lax.axis_index("y"), lax.axis_index("c")
        me, sib = (x, y, c), (x, y, 1 - c)
        chips = [(1 - x, y), (x, 1 - y), (1 - x, 1 - y)]
        slot = lambda d: 4 * d[0] + 2 * d[1] + d[2]

        def copy(a, k, block, to, src=None):
            dst = outs[a].at[slot(block)]
            return pltpu.make_async_remote_copy(
                src_ref=dst if src is None else src, dst_ref=dst,
                send_sem=send_sems.at[7 * a + k], recv_sem=recv_sems.at[7 * a + k],
                device_id=to, device_id_type=pl.DeviceIdType.MESH)

        local = [pltpu.make_async_copy(srcs[a], outs[a].at[slot(me)], loc_sems.at[a]) for a in range(self.n)]
        return me, sib, chips, c, copy, local

    def start(self, srcs, bufs, outs, sems):
        me, sib, chips, c, copy, local = self._parts(srcs, outs, sems)
        for a in range(self.n):
            local[a].start()
            copy(a, 0, me, sib, src=srcs[a]).start()
            for j, chip in enumerate(chips):
                copy(a, 1 + j, me, (*chip, c), src=srcs[a]).start()

    def finish(self, srcs, bufs, outs, sems):
        me, sib, chips, c, copy, local = self._parts(srcs, outs, sems)
        for a in range(self.n):
            for j, chip in enumerate(chips):
                copy(a, 1 + j, (*chip, c), me).wait_recv()
                copy(a, 4 + j, (*chip, c), sib).start()
        for a in range(self.n):
            copy(a, 0, sib, me).wait_recv()
            copy(a, 0, me, sib, src=srcs[a]).wait_send()
            for j, chip in enumerate(chips):
                copy(a, 4 + j, (*chip, 1 - c), me).wait_recv()
                copy(a, 1 + j, me, (*chip, c), src=srcs[a]).wait_send()
                copy(a, 4 + j, (*chip, c), sib).wait_send()
            local[a].wait()


def _call(body, *, name, grid, in_specs, out_specs, out_shape, args, scratch=(), cp=None, ride=None):
    out_specs, out_shape, scratch = list(out_specs), list(out_shape), list(scratch)
    if ride is None:
        outs = _pallas_call(body, name=name, grid=grid, in_specs=list(in_specs), out_specs=tuple(out_specs),
                              out_shape=tuple(out_shape), scratch_shapes=scratch, compiler_params=cp)(*args)
        return list(outs), []
    nin, nout, nscr, n, nb, no = len(in_specs), len(out_specs), len(scratch), ride.n, len(ride.bufs), len(ride.out_shapes)
    steps = list(grid)

    def wrapped(*refs):
        h_in, r_src, r_buf = refs[:nin], refs[nin:nin + n], refs[nin + n:nin + n + nb]
        o0 = nin + n + nb
        h_out, r_out = refs[o0:o0 + nout], refs[o0 + nout:o0 + nout + no]
        s0 = o0 + nout + no
        h_scr, sems = refs[s0:s0 + nscr], refs[s0 + nscr:]
        ids = [pl.program_id(a) for a in range(len(steps))]
        first = functools.reduce(jnp.logical_and, [i == 0 for i in ids])
        last = functools.reduce(jnp.logical_and, [i == s - 1 for i, s in zip(ids, steps)])

        @pl.when(first)
        def _():
            ride.start(r_src, r_buf, r_out, sems)

        body(*h_in, *h_out, *h_scr)

        @pl.when(last)
        def _():
            ride.finish(r_src, r_buf, r_out, sems)

    aliases = {nin + n + k: nout + k for k in range(nb)}
    outs = _pallas_call(
        wrapped, name=name, grid=grid, in_specs=list(in_specs) + [ANY] * (n + nb),
        out_specs=tuple(out_specs + [ANY] * no), out_shape=tuple(out_shape + ride.out_shapes),
        scratch_shapes=scratch + ride.sems, input_output_aliases=aliases, compiler_params=cp,
    )(*args, *ride.srcs, *ride.bufs)
    return list(outs[:nout]), list(outs[nout:])


def _exchange(ride, name):
    def body(dummy_ref, o_ref):
        o_ref[...] = dummy_ref[...]

    one = pl.BlockSpec((8, 128), lambda i: (0, 0))
    _, outs = _call(body, name=name, grid=(1,), in_specs=[one], out_specs=[one], out_shape=[_sds((8, 128), F32)],
                    args=(jnp.zeros((8, 128), F32),), ride=ride)
    return outs


HBM = pl.BlockSpec(memory_space=pltpu.HBM)
SEM = pl.BlockSpec(memory_space=pltpu.SEMAPHORE)


def routes_all(me):
    return [(k, me ^ k, me, me ^ k) for k in range(NDEV)]


def _scatter_copies(srcs, lands, sems, sending, whole, rows):
    send_sems, recv_sems, loc_sems = sems
    me = _me()
    rts = routes_all(me)
    part = lambda ref, slab: ref if whole else ref.at[slab] if rows is None else ref.at[slab, pl.ds(0, rows)]
    remote_ix = [r for r, (k, _, _, _) in enumerate(rts) if k != 0]
    local_ix = [r for r, (k, _, _, _) in enumerate(rts) if k == 0]
    remote, local = [], []
    for a in range(len(srcs)):
        for n, r in enumerate(remote_ix):
            k, slab, there, here = rts[r]
            t = me ^ k
            sem = len(remote_ix) * a + n
            remote.append(pltpu.make_async_remote_copy(
                src_ref=part(srcs[a], slab), dst_ref=lands[a].at[there if sending else here], send_sem=send_sems.at[sem],
                recv_sem=recv_sems.at[sem], device_id=(t // 4, (t // 2) % 2, t % 2), device_id_type=pl.DeviceIdType.MESH))
        for n, r in enumerate(local_ix):
            _, slab, there, _ = rts[r]
            local.append(pltpu.make_async_copy(part(srcs[a], slab), lands[a].at[there], loc_sems.at[len(local_ix) * a + n]))
    return remote, local


def scatter_start(payloads, name, whole=False, rows=None):
    n = len(payloads)
    nr = NDEV - 1

    def body(*refs):
        srcs, lands, sems = refs[:n], refs[n:2 * n], refs[2 * n:2 * n + 3]
        remote, local = _scatter_copies(srcs, lands, sems, True, whole, rows)
        for cp in local + remote:
            cp.start()
        refs[-1][...] = jnp.zeros((8, 128), F32)

    thru = [pltpu.HBM(p.shape, p.dtype) for p in payloads]
    land_shapes = [(NDEV,) + p.shape if whole else p.shape if rows is None else (NDEV, rows) + p.shape[2:] for p in payloads]
    outs = pl.pallas_call(
        body, name=name,
        out_shape=(pltpu.SemaphoreType.DMA((nr * n,)), pltpu.SemaphoreType.DMA((nr * n,)), pltpu.SemaphoreType.DMA((n,)),
                   *thru, *[pltpu.HBM(sh, p.dtype) for sh, p in zip(land_shapes, payloads)], _sds((8, 128), F32)),
        in_specs=[HBM] * (2 * n),
        out_specs=(SEM, SEM, SEM, *[HBM] * (2 * n), pl.BlockSpec(memory_space=pltpu.VMEM)),
        input_output_aliases={i: 3 + i for i in range(2 * n)},
        compiler_params=pltpu.CompilerParams(has_side_effects=pltpu.SideEffectType.DATAFLOW_SIDE_EFFECTING),
    )(*[pltpu.with_memory_space_constraint(p, pltpu.HBM) for p in payloads],
      *[pltpu.with_memory_space_constraint(lax.empty(sh, p.dtype), pltpu.HBM) for sh, p in zip(land_shapes, payloads)])
    return (outs[:3], outs[3:3 + n], outs[3 + n:3 + 2 * n], whole, rows), outs[-1][0, 0]


def scatter_wait(handle, after, name):
    sems, srcs_thru, lands_thru, whole, rows = handle
    n = len(srcs_thru)
    after = list(after) if isinstance(after, (list, tuple)) else [after]

    def body(*refs):
        srcs, lands, sems_ = refs[:n], refs[n:2 * n], refs[2 * n:2 * n + 3]
        remote, local = _scatter_copies(srcs, lands, sems_, False, whole, rows)
        for cp in remote:
            cp.wait_send()
            cp.wait_recv()
        for cp in local:
            cp.wait()

    outs = pl.pallas_call(
        body, name=name, out_shape=tuple(pltpu.HBM(p.shape, p.dtype) for p in (*srcs_thru, *lands_thru)),
        in_specs=[HBM] * (2 * n) + [SEM] * 3 + [HBM] * len(after), out_specs=tuple([HBM] * (2 * n)),
        input_output_aliases={i: i for i in range(2 * n)},
        compiler_params=pltpu.CompilerParams(has_side_effects=pltpu.SideEffectType.DATAFLOW_SIDE_EFFECTING),
    )(*srcs_thru, *lands_thru, *sems, *[pltpu.with_memory_space_constraint(a, pltpu.HBM) for a in after])
    return list(outs[n:])


def _row_spec(cols, tm=TM):
    return pl.BlockSpec((tm, cols), lambda i: (i, 0))


def _full_spec(shape):
    nd = len(shape)
    return pl.BlockSpec(shape, lambda i: (0,) * nd)


def ln_mod_fwd(x, mod, sub, name):
    def body(x_ref, mod_ref, h_ref):
        n, _ = _ln_stats(x_ref[...])
        shift = mod_ref[3 * sub:3 * sub + 1, :]
        scale = mod_ref[3 * sub + 1:3 * sub + 2, :]
        h_ref[...] = (n * (1.0 + scale) + shift).astype(MXU)

    return _pallas_call(
        body, name=name, grid=(S // TM,),
        in_specs=[_row_spec(D), _full_spec((9, D))], out_specs=_row_spec(D),
        out_shape=_sds((S, D), MXU), compiler_params=_cp(32, ("arbitrary",)))(x, mod)


def res_ln_fwd(x, f, mod, sub, lng, lnb, w, name, nxt=None):
    def body(x_ref, f_ref, mod_ref, g_ref, b_ref, *rest):
        gate = mod_ref[3 * sub + 2:3 * sub + 3, :]
        r = ALPHA * x_ref[...] + (w * gate) * f_ref[...]
        n, _ = _ln_stats(r)
        xo = n * g_ref[sub:sub + 1, :] + b_ref[sub:sub + 1, :]
        rest[-1 if nxt is None else -2][...] = xo
        if nxt is not None:
            nmod_ref, h_ref = rest[0], rest[-1]
            n2, _ = _ln_stats(xo)
            s2 = nxt[1]
            h_ref[...] = (n2 * (1.0 + nmod_ref[3 * s2 + 1:3 * s2 + 2, :]) + nmod_ref[3 * s2:3 * s2 + 1, :]).astype(MXU)

    more = nxt is not None
    return _pallas_call(
        body, name=name, grid=(S // TM,),
        in_specs=[_row_spec(D), _row_spec(D), _full_spec((9, D)), _full_spec((3, D)), _full_spec((3, D))] + [_full_spec((9, D))] * more,
        out_specs=(_row_spec(D),) + (_row_spec(D),) * more, out_shape=(_sds((S, D), F32),) + (_sds((S, D), MXU),) * more,
        compiler_params=_cp(32, ("arbitrary",)))(x, f, mod, lng, lnb, *([nxt[0]] if more else []))


def res_ln_bwd(x, f, mod, sub, lng, dxo, w, name):
    def body(x_ref, f_ref, mod_ref, g_ref, dxo_ref, dxa_ref, df_ref, sums_ref):
        i = pl.program_id(0)
        gate = mod_ref[3 * sub + 2:3 * sub + 3, :]
        fv = f_ref[...]
        r = ALPHA * x_ref[...] + (w * gate) * fv
        n, rstd = _ln_stats(r)
        dxo = dxo_ref[...]
        dr = _ln_bwd(dxo * g_ref[sub:sub + 1, :], n, rstd)
        dxa_ref[...] = ALPHA * dr
        df_ref[...] = ((w * gate) * dr).astype(MXU)
        part = jnp.concatenate([
            jnp.sum(dxo * n, axis=0, keepdims=True),
            jnp.sum(dxo, axis=0, keepdims=True),
            jnp.sum(dr * fv, axis=0, keepdims=True) * w,
            jnp.zeros((5, D), F32)], axis=0)

        @pl.when(i == 0)
        def _():
            sums_ref[...] = part

        @pl.when(i > 0)
        def _():
            sums_ref[...] += part

    return _call(
        body, name=name, grid=(S // TM,),
        in_specs=[_row_spec(D), _row_spec(D), _full_spec((9, D)), _full_spec((3, D)), _row_spec(D)],
        out_specs=(_row_spec(D), _row_spec(D), _full_spec((8, D))),
        out_shape=(_sds((S, D), F32), _sds((S, D), MXU), _sds((8, D), F32)),
        cp=_cp(32, ("arbitrary",)), args=(x, f, mod, lng, dxo))[0]


def ln_mod_bwd(x, dh, mod, sub, dxa, name):
    def body(x_ref, dh_ref, mod_ref, dxa_ref, dx_ref, sums_ref):
        i = pl.program_id(0)
        scale = mod_ref[3 * sub + 1:3 * sub + 2, :]
        n, rstd = _ln_stats(x_ref[...])
        dh = dh_ref[...]
        dx_ref[...] = dxa_ref[...] + _ln_bwd(dh * (1.0 + scale), n, rstd)
        part = jnp.concatenate([
            jnp.sum(dh, axis=0, keepdims=True),
            jnp.sum(dh * n, axis=0, keepdims=True),
            jnp.zeros((6, D), F32)], axis=0)

        @pl.when(i == 0)
        def _():
            sums_ref[...] = part

        @pl.when(i > 0)
        def _():
            sums_ref[...] += part

    return _call(
        body, name=name, grid=(S // TM,),
        in_specs=[_row_spec(D), _row_spec(D), _full_spec((9, D)), _row_spec(D)],
        out_specs=(_row_spec(D), _full_spec((8, D))),
        out_shape=(_sds((S, D), F32), _sds((8, D), F32)),
        cp=_cp(32, ("arbitrary",)), args=(x, dh, mod, dxa))[0]


def ln_join_bwd(xp, fp, modp, subp, lngp, lnbp, wp, dh, mod, sub, dxa, name):
    def body(xp_ref, fp_ref, modp_ref, g_ref, b_ref, dh_ref, mod_ref, dxa_ref, dxap_ref, dfp_ref, sumsp_ref, sums_ref):
        i = pl.program_id(0)
        gate = modp_ref[3 * subp + 2:3 * subp + 3, :]
        fv = fp_ref[...]
        n, rstd = _ln_stats(ALPHA * xp_ref[...] + (wp * gate) * fv)
        gain = g_ref[subp:subp + 1, :]
        n2, rstd2 = _ln_stats(n * gain + b_ref[subp:subp + 1, :])
        dh = dh_ref[...]
        dx = dxa_ref[...] + _ln_bwd(dh * (1.0 + mod_ref[3 * sub + 1:3 * sub + 2, :]), n2, rstd2)
        dr = _ln_bwd(dx * gain, n, rstd)
        dxap_ref[...] = ALPHA * dr
        dfp_ref[...] = ((wp * gate) * dr).astype(MXU)
        partp = jnp.concatenate([
            jnp.sum(dx * n, axis=0, keepdims=True), jnp.sum(dx, axis=0, keepdims=True),
            jnp.sum(dr * fv, axis=0, keepdims=True) * wp, jnp.zeros((5, D), F32)], axis=0)
        part = jnp.concatenate([
            jnp.sum(dh, axis=0, keepdims=True), jnp.sum(dh * n2, axis=0, keepdims=True), jnp.zeros((6, D), F32)], axis=0)

        @pl.when(i == 0)
        def _():
            sumsp_ref[...] = partp
            sums_ref[...] = part

        @pl.when(i > 0)
        def _():
            sumsp_ref[...] += partp
            sums_ref[...] += part

    return _pallas_call(
        body, name=name, grid=(S // TM,),
        in_specs=[_row_spec(D), _row_spec(D), _full_spec((9, D)), _full_spec((3, D)), _full_spec((3, D)), _row_spec(D),
                  _full_spec((9, D)), _row_spec(D)],
        out_specs=(_row_spec(D), _row_spec(D), _full_spec((8, D)), _full_spec((8, D))),
        out_shape=(_sds((S, D), F32), _sds((S, D), MXU), _sds((8, D), F32), _sds((8, D), F32)),
        compiler_params=_cp(40, ("arbitrary",)))(xp, fp, modp, lngp, lnbp, dh, mod, dxa)


def loss_fwd_bwd(y, target, name):
    def body(y_ref, t_ref, l_ref, dy_ref):
        i = pl.program_id(0)
        e = y_ref[...] - t_ref[...]
        dy_ref[...] = e * (1.0 / D)
        part = jnp.zeros((8, 128), F32) + (0.5 / D) * jnp.sum(e * e)

        @pl.when(i == 0)
        def _():
            l_ref[...] = part

        @pl.when(i > 0)
        def _():
            l_ref[...] += part

    return _pallas_call(
        body, name=name, grid=(S // TM,),
        in_specs=[_row_spec(D), _row_spec(D)], out_specs=(_full_spec((8, 128)), _row_spec(D)),
        out_shape=(_sds((8, 128), F32), _sds((S, D), F32)),
        compiler_params=_cp(32, ("arbitrary",)))(y, target)


HB = 2 * FBP
NHB = NDEV * FBP // HB
TMB = 1024


def _wrows(buffers=2):
    return pl.BlockSpec((HB, D), lambda j, i: (j, 0), pipeline_mode=pl.Buffered(buffers))


def _resident(shape):
    return pl.BlockSpec(shape, lambda j, i: (0, 0), pipeline_mode=pl.Buffered(1))


def ffn_fwd(h, wgt, wut, wd, name, ride=None):
    def body(h_ref, wg_ref, wu_ref, wd_ref, g_ref, u_ref, f_ref):
        j, i = pl.program_id(0), pl.program_id(1)
        hv = h_ref[...]
        g = _mm_nt(hv, wg_ref[...])
        u = _mm_nt(hv, wu_ref[...])
        g_ref[...] = g.astype(MXU)
        u_ref[...] = u.astype(MXU)
        a = g * jax.nn.sigmoid(g) * u
        part = _mm(a, wd_ref[...])
        rows = pl.ds(pl.multiple_of(i * TMB, TMB), TMB)

        @pl.when(j == 0)
        def _():
            f_ref[rows, :] = part

        @pl.when(j > 0)
        def _():
            f_ref[rows, :] += part

    gu = pl.BlockSpec((TMB, HB), lambda j, i: (i, j))
    return _call(
        body, name=name, grid=(NHB, S // TMB),
        in_specs=[pl.BlockSpec((TMB, D), lambda j, i: (i, 0)), _wrows(), _wrows(), _wrows()],
        out_specs=(gu, gu, _resident((S, D))),
        out_shape=(_sds((S, NDEV * FBP), MXU), _sds((S, NDEV * FBP), MXU), _sds((S, D), F32)),
        cp=_cp(52, ("arbitrary", "arbitrary")), args=(h, wgt, wut, wd), ride=ride)


def ffn_bwd(df, h, g, u, wgt, wut, wd, name):
    ni = S // TMB

    def body(df_ref, h_ref, g_ref, u_ref, wg_ref, wu_ref, wd_ref, dwg_ref, dwu_ref, dwd_ref, dh_ref,
             ag_ref, au_ref, ad_ref):
        j, i = pl.program_id(0), pl.program_id(1)
        dfv, hv = df_ref[...], h_ref[...]
        gv, uv = g_ref[...].astype(F32), u_ref[...].astype(F32)
        da = _mm_nt(dfv, wd_ref[...])
        sg = jax.nn.sigmoid(gv)
        silu = gv * sg
        du = da * silu
        dg = da * uv * (sg * (1.0 + gv * (1.0 - sg)))
        p_d = _mm_tn(silu * uv, dfv)
        p_g = _mm_tn(dg, hv)
        p_u = _mm_tn(du, hv)

        @pl.when(i == 0)
        def _():
            ad_ref[...] = p_d
            ag_ref[...] = p_g
            au_ref[...] = p_u

        @pl.when(i > 0)
        def _():
            ad_ref[...] += p_d
            ag_ref[...] += p_g
            au_ref[...] += p_u

        @pl.when(i == ni - 1)
        def _():
            dwd_ref[...] = ad_ref[...].astype(BF16)
            dwg_ref[...] = ag_ref[...].astype(BF16)
            dwu_ref[...] = au_ref[...].astype(BF16)

        part = _mm(dg, wg_ref[...]) + _mm(du, wu_ref[...])
        rows = pl.ds(pl.multiple_of(i * TMB, TMB), TMB)

        @pl.when(j == 0)
        def _():
            dh_ref[rows, :] = part

        @pl.when(j > 0)
        def _():
            dh_ref[rows, :] += part

    gu = pl.BlockSpec((TMB, HB), lambda j, i: (i, j))
    rowt = pl.BlockSpec((TMB, D), lambda j, i: (i, 0))
    return _call(
        body, name=name, grid=(NHB, ni),
        in_specs=[rowt, rowt, gu, gu, _wrows(1), _wrows(1), _wrows(1)],
        out_specs=(_wrows(1), _wrows(1), _wrows(1), _resident((S, D))),
        out_shape=(_sds((NDEV * FBP, D), BF16), _sds((NDEV * FBP, D), BF16), _sds((NDEV * FBP, D), BF16), _sds((S, D), F32)),
        scratch=[pltpu.VMEM((HB, D), F32), pltpu.VMEM((HB, D), F32), pltpu.VMEM((HB, D), F32)],
        cp=_cp(60, ("arbitrary", "arbitrary")), args=(df, h, g, u, wgt, wut, wd))[0]


def win_fwd(h, win, name, ride=None):
    def body(h_ref, w_ref, z_ref):
        hv = h_ref[...]
        for j in range(NDEV):
            z_ref[:, 256 * j:256 * (j + 1)] = _mm(hv, w_ref[j])

    return _call(
        body, name=name, grid=(S // TMM,),
        in_specs=[_row_spec(D, TMM), _full_spec((NDEV, D, 256))],
        out_specs=[_row_spec(D_IN, TMM)], out_shape=[_sds((S, D_IN), F32)],
        cp=_cp(40, ("arbitrary",)), args=(h, win), ride=ride)


def win_bwd(dparts, h, win, name):
    ni = S // TMM

    def body(dq_ref, dk_ref, dv_ref, dus_ref, dup_ref, h_ref, w_ref, dh_ref, dw_ref, acc_ref):
        i = pl.program_id(0)
        hv = h_ref[...]
        cols = [dq_ref[:, 0:256], dq_ref[:, 256:512], dk_ref[:, 0:256], dk_ref[:, 256:512],
                dv_ref[:, 0:256], dv_ref[:, 256:512], dus_ref[...], dup_ref[...]]
        dh = jnp.zeros((TMM, D), F32)
        for j in range(NDEV):
            dz = cols[j].astype(MXU)
            dh = dh + _mm_nt(dz, w_ref[j])
            p = _mm_tn(hv, dz)

            @pl.when(i == 0)
            def _():
                acc_ref[j] = p

            @pl.when(i > 0)
            def _():
                acc_ref[j] += p

        dh_ref[...] = dh

        @pl.when(i == ni - 1)
        def _():
            dw_ref[...] = acc_ref[...].astype(BF16)

    return _call(
        body, name=name, grid=(ni,),
        in_specs=[_row_spec(512, TMM), _row_spec(512, TMM), _row_spec(512, TMM), _row_spec(256, TMM), _row_spec(256, TMM),
                  _row_spec(D, TMM), _full_spec((NDEV, D, 256))],
        out_specs=(_row_spec(D, TMM), _full_spec((NDEV, D, 256))),
        out_shape=(_sds((S, D), F32), _sds((NDEV, D, 256), BF16)),
        scratch=[pltpu.VMEM((NDEV, D, 256), F32)],
        cp=_cp(48, ("arbitrary",)), args=(*dparts, h, win))[0]


def wout_fwd(ya, ys, yp, wout, name, ride=None):
    def body(ya_ref, ys_ref, yp_ref, w_ref, o_ref):
        w = w_ref[...].reshape(D, D)
        o_ref[...] = _mm(ya_ref[...], w[0:512]) + _mm(ys_ref[...], w[512:768]) + _mm(yp_ref[...], w[768:1024])

    return _call(
        body, name=name, grid=(S // TMM,),
        in_specs=[_row_spec(512, TMM), _row_spec(256, TMM), _row_spec(256, TMM), _full_spec((NDEV, 128, D))],
        out_specs=[_row_spec(D, TMM)], out_shape=[_sds((S, D), F32)],
        cp=_cp(40, ("arbitrary",)), args=(ya, ys, yp, wout), ride=ride)


def wout_bwd(do, ya, ys, yp, wout, name):
    ni = S // TMM

    def body(do_ref, ya_ref, ys_ref, yp_ref, w_ref, dya_ref, dys_ref, dyp_ref, dw_ref, acc_ref):
        i = pl.program_id(0)
        w = w_ref[...].reshape(D, D)
        dov = do_ref[...]
        dya_ref[...] = _mm_nt(dov, w[0:512])
        dys_ref[...] = _mm_nt(dov, w[512:768])
        dyp_ref[...] = _mm_nt(dov, w[768:1024])
        parts = [(0, 512, _mm_tn(ya_ref[...], dov)), (512, 768, _mm_tn(ys_ref[...], dov)),
                 (768, 1024, _mm_tn(yp_ref[...], dov))]
        for lo, hi, p in parts:
            @pl.when(i == 0)
            def _():
                acc_ref[lo:hi, :] = p

            @pl.when(i > 0)
            def _():
                acc_ref[lo:hi, :] += p

        @pl.when(i == ni - 1)
        def _():
            dw_ref[...] = acc_ref[...].astype(BF16).reshape(NDEV, 128, D)

    return _call(
        body, name=name, grid=(ni,),
        in_specs=[_row_spec(D, TMM), _row_spec(512, TMM), _row_spec(256, TMM), _row_spec(256, TMM),
                  _full_spec((NDEV, 128, D))],
        out_specs=(_row_spec(512, TMM), _row_spec(256, TMM), _row_spec(256, TMM), _full_spec((NDEV, 128, D))),
        out_shape=(_sds((S, 512), F32), _sds((S, 256), F32), _sds((S, 256), F32), _sds((NDEV, 128, D), BF16)),
        scratch=[pltpu.VMEM((D, D), F32)],
        cp=_cp(40, ("arbitrary",)), args=(do, ya, ys, yp, wout))[0]


def _t5_bucket(dist):
    max_exact = N_BUCKETS // 2
    d = np.maximum(dist, 1).astype(np.float32)
    large = max_exact + (np.log(d / max_exact) / math.log(MAX_DISTANCE / max_exact)
                         * (N_BUCKETS - max_exact)).astype(np.int32)
    large = np.minimum(large, N_BUCKETS - 1)
    return np.where(dist < max_exact, dist, large).astype(np.int32)


def _att_static():
    i = np.arange(QB)[:, None]
    j = np.arange(2 * QB)[None, :]
    r = i + QB - j
    buckets, bands = [], []
    for window, dil in PATTERNS:
        bands.append((r >= 0) & (r <= window // dil))
        buckets.append(_t5_bucket(np.clip(r, 0, None) * dil))
    return np.stack(buckets), np.stack(bands), np.broadcast_to(j >= QB, (QB, 2 * QB))


def att_bias(rel_bias):
    m = np.arange(2 * QB)
    rows = []
    for window, dil in PATTERNS:
        r = QB - m
        ok = (r >= 0) & (r <= window // dil)
        b = rel_bias[_t5_bucket(np.clip(r, 0, None) * dil)]
        rows.append(jnp.where(ok[:, None], b, NEG).T)
    return jnp.broadcast_to(jnp.stack(rows)[:, :, None, :], (3, N_HEADS, 8, 2 * QB))


def _bias_tiles(t_ref, tiles):
    col = lax.broadcasted_iota(jnp.int32, (QB, 2 * QB), 1)
    for p in range(3):
        for hh in range(2):
            t = pltpu.roll(jnp.broadcast_to(t_ref[p, hh, 0:1, :], (QB, 2 * QB)), 0, 1, stride=1, stride_axis=0)
            tiles[p, hh, 0] = t
            tiles[p, hh, 1] = jnp.where(col >= QB, t, NEG)


def _permute_in(dst_ref, src_ref, d, scale=None, pad=QB):
    L = S // d
    for r in range(d):
        v = src_ref[pl.ds(r, L, stride=d), :] if d > 1 else src_ref[...]
        if scale is not None:
            v = v * scale
        dst_ref[pad + r * L:pad + (r + 1) * L, :] = v.astype(dst_ref.dtype)


def att_fwd(z, bias, name, ride=None):
    def body(q_ref, k_ref, v_ref, t_ref, y_ref, l_ref, qs, ks, vs, o_perm, l_perm, o_nat, l_nat, b_ref):
        _bias_tiles(t_ref, b_ref)
        zero_pad = jnp.zeros((QB, 128), MXU)
        ks[0:QB, :] = zero_pad
        vs[0:QB, :] = zero_pad
        lane = lax.broadcasted_iota(jnp.int32, (QB, 128), 1)
        for p, (_, d) in enumerate(PATTERNS):
            L = S // d
            nb = L // QB
            _permute_in(qs, q_ref, d, scale=0.125, pad=0)
            _permute_in(ks, k_ref, d)
            _permute_in(vs, v_ref, d)

            def blk(b, carry):
                r0 = pl.multiple_of(b * QB, QB)
                q = qs[pl.ds(r0, QB), :]
                kb = ks[pl.ds(r0, 2 * QB), :]
                vb = vs[pl.ds(r0, 2 * QB), :]
                first = ((b % nb) == 0).astype(jnp.int32)
                res = []
                for hh in range(2):
                    sel = (lane < 64) if hh == 0 else (lane >= 64)
                    qm = jnp.where(sel, q, jnp.zeros_like(q))
                    s = _mm_nt(qm, kb) + b_ref[p, hh, first]
                    m = jnp.max(s, axis=1, keepdims=True)
                    pe = jnp.exp(s - m)
                    den = jnp.sum(pe, axis=1, keepdims=True)
                    res.append((_mm(pe, vb) / den, m + jnp.log(den)))
                o_perm[pl.ds(r0, QB), :] = jnp.where(lane < 64, res[0][0], res[1][0])
                l_perm[pl.ds(r0, QB), :] = jnp.where(lane < 64, res[0][1], res[1][1])
                return carry

            lax.fori_loop(0, S // QB, blk, 0, unroll=8)
            for r in range(d):
                if d > 1:
                    o_nat[p, pl.ds(r, L, stride=d), :] = o_perm[r * L:(r + 1) * L, :]
                    l_nat[p, pl.ds(r, L, stride=d), :] = l_perm[r * L:(r + 1) * L, :]
                else:
                    o_nat[p] = o_perm[...]
                    l_nat[p] = l_perm[...]
        l0, l1, l2 = l_nat[0], l_nat[1], l_nat[2]
        m = jnp.maximum(jnp.maximum(l0, l1), l2)
        e0, e1, e2 = jnp.exp(l0 - m), jnp.exp(l1 - m), jnp.exp(l2 - m)
        den = e0 + e1 + e2
        y_ref[...] = (e0 * o_nat[0] + e1 * o_nat[1] + e2 * o_nat[2]) / den
        l_ref[...] = m + jnp.log(den)

    col = lambda c0: pl.BlockSpec((S, 128), lambda hp: (0, c0 + hp))
    return _call(
        body, name=name, grid=(N_HEADS // 2,),
        in_specs=[col(0), col(4), col(8), pl.BlockSpec((3, 2, 8, 2 * QB), lambda hp: (0, hp, 0, 0))],
        out_specs=(col(0), col(0)),
        out_shape=(_sds((S, D_ATT), F32), _sds((S, D_ATT), F32)),
        scratch=[pltpu.VMEM((S, 128), MXU), pltpu.VMEM((S + QB, 128), MXU), pltpu.VMEM((S + QB, 128), MXU),
                 pltpu.VMEM((S, 128), F32), pltpu.VMEM((S, 128), F32),
                 pltpu.VMEM((3, S, 128), F32), pltpu.VMEM((3, S, 128), F32),
                 pltpu.VMEM((3, 2, 2, QB, 2 * QB), F32)],
        cp=_cp(40, ("arbitrary",)), args=(z, z, z, bias), ride=ride)


def att_bwd(z, bias, y, lse, dy, name):
    def body(q_ref, k_ref, v_ref, t_ref, y_ref, l_ref, dy_ref, dq_ref, dk_ref, dv_ref, db_ref,
             qs, ks, vs, dys, ls, dds, dn_nat, dq_perm, dk_perm, dv_perm, b_ref):
        _bias_tiles(t_ref, b_ref)
        zero_pad = jnp.zeros((QB, 128), MXU)
        ks[0:QB, :] = zero_pad
        vs[0:QB, :] = zero_pad
        lane = lax.broadcasted_iota(jnp.int32, (QB, 128), 1)
        lane_s = lax.broadcasted_iota(jnp.int32, (S, 128), 1)
        t = dy_ref[...] * y_ref[...]
        sa = jnp.sum(jnp.where(lane_s < 64, t, 0.0), axis=1, keepdims=True)
        sb = jnp.sum(jnp.where(lane_s >= 64, t, 0.0), axis=1, keepdims=True)
        dn_nat[...] = jnp.where(lane_s < 64, sa, sb)
        dq_ref[...] = jnp.zeros((S, 128), F32)
        dk_ref[...] = jnp.zeros((S, 128), F32)
        dv_ref[...] = jnp.zeros((S, 128), F32)
        db_ref[...] = jnp.zeros((3, 2, QB, 2 * QB), F32)
        for p, (_, d) in enumerate(PATTERNS):
            L = S // d
            nb = L // QB
            _permute_in(qs, q_ref, d, scale=0.125, pad=0)
            _permute_in(ks, k_ref, d)
            _permute_in(vs, v_ref, d)
            _permute_in(dys, dy_ref, d, pad=0)
            _permute_in(ls, l_ref, d, pad=0)
            _permute_in(dds, dn_nat, d, pad=0)
            dk_perm[...] = jnp.zeros((S + QB, 128), F32)
            dv_perm[...] = jnp.zeros((S + QB, 128), F32)

            def blk(b, carry):
                r0 = pl.multiple_of(b * QB, QB)
                q = qs[pl.ds(r0, QB), :]
                kb = ks[pl.ds(r0, 2 * QB), :]
                vb = vs[pl.ds(r0, 2 * QB), :]
                dyb = dys[pl.ds(r0, QB), :]
                lb = ls[pl.ds(r0, QB), :]
                db = dds[pl.ds(r0, QB), :]
                first = ((b % nb) == 0).astype(jnp.int32)
                lane2 = jnp.concatenate([lane, lane], axis=0)
                own = (lane2 >> 6) == (lax.broadcasted_iota(jnp.int32, (2 * QB, 128), 0) >> 7)
                qm = jnp.where(own, jnp.concatenate([q, q], axis=0), jnp.zeros((2 * QB, 128), q.dtype))
                dym = jnp.where(own, jnp.concatenate([dyb, dyb], axis=0), jnp.zeros((2 * QB, 128), dyb.dtype))
                wide = lambda t: jnp.concatenate([jnp.broadcast_to(t[:, 0:1], (QB, 2 * QB)), jnp.broadcast_to(t[:, 64:65], (QB, 2 * QB))], axis=0)
                lse2, dd2 = wide(lb), wide(db)
                bias2 = jnp.concatenate([b_ref[p, 0, first], b_ref[p, 1, first]], axis=0)
                pr = jnp.exp(_mm_nt(qm, kb) + bias2 - lse2)
                ds = pr * (_mm_nt(dym, vb) - dd2)
                db_ref[p, 0] += ds[0:QB]
                db_ref[p, 1] += ds[QB:2 * QB]
                dq2 = _mm(ds, kb)
                dqs = [dq2[0:QB], dq2[QB:2 * QB]]
                dkb = _mm_tn(ds, qm)
                dvb = _mm_tn(pr, dym)
                dq_perm[pl.ds(r0, QB), :] = jnp.where(lane < 64, dqs[0], dqs[1])
                dk_perm[pl.ds(r0, 2 * QB), :] += dkb
                dv_perm[pl.ds(r0, 2 * QB), :] += dvb
                return carry

            lax.fori_loop(0, S // QB, blk, 0, unroll=4)
            for r in range(d):
                idx = pl.ds(r, L, stride=d) if d > 1 else pl.ds(0, S)
                dq_ref[idx, :] += dq_perm[r * L:(r + 1) * L, :] * 0.125
                dk_ref[idx, :] += dk_perm[QB + r * L:QB + (r + 1) * L, :]
                dv_ref[idx, :] += dv_perm[QB + r * L:QB + (r + 1) * L, :]

    col = lambda c0: pl.BlockSpec((S, 128), lambda hp: (0, c0 + hp))
    bspec = pl.BlockSpec((3, 2, 8, 2 * QB), lambda hp: (0, hp, 0, 0))
    return _call(
        body, name=name, grid=(N_HEADS // 2,),
        in_specs=[col(0), col(4), col(8), bspec, col(0), col(0), col(0)],
        out_specs=(col(0), col(0), col(0), pl.BlockSpec((3, 2, QB, 2 * QB), lambda hp: (0, hp, 0, 0))),
        out_shape=(_sds((S, D_ATT), F32), _sds((S, D_ATT), F32), _sds((S, D_ATT), F32),
                   _sds((3, N_HEADS, QB, 2 * QB), F32)),
        scratch=[pltpu.VMEM((S, 128), MXU), pltpu.VMEM((S + QB, 128), MXU), pltpu.VMEM((S + QB, 128), MXU),
                 pltpu.VMEM((S, 128), MXU), pltpu.VMEM((S, 128), F32), pltpu.VMEM((S, 128), F32),
                 pltpu.VMEM((S, 128), F32), pltpu.VMEM((S, 128), F32),
                 pltpu.VMEM((S + QB, 128), F32), pltpu.VMEM((S + QB, 128), F32),
                 pltpu.VMEM((3, 2, 2, QB, 2 * QB), F32)],
        cp=_cp(48, ("arbitrary",)), args=(z, z, z, bias, y, lse, dy))[0]


def relbias_grad(dbiases):
    bucket, band, _ = _att_static()
    onehot = (bucket[:, None] == np.arange(N_BUCKETS)[None, :, None, None]) & band[:, None]
    onehot = jnp.asarray(onehot.reshape(3, N_BUCKETS, QB * 2 * QB), BF16)

    def body(db0_ref, db1_ref, oh_ref, o_ref):
        acc = jnp.zeros((N_HEADS, N_BUCKETS), F32)
        for p in range(3):
            acc = acc + lax.dot_general(db0_ref[p] + db1_ref[p], oh_ref[p].astype(F32), (((1,), (1,)), ((), ())),
                                        preferred_element_type=F32, precision=lax.Precision.HIGHEST)
        o_ref[...] = acc

    vm = pl.BlockSpec(memory_space=pltpu.VMEM)
    out = _pallas_call(body, name="relbias_grad", in_specs=[vm, vm, vm], out_specs=vm,
                         out_shape=_sds((N_HEADS, N_BUCKETS), F32), compiler_params=_cp(40))(
        *[d.reshape(3, N_HEADS, QB * 2 * QB) for d in dbiases], onehot)
    return out.T


def _panel(t_ref, ri, j):
    return t_ref[ri, pl.ds(j, S, stride=8), :]


def _gelu(x):
    c = math.sqrt(2.0 / math.pi)
    th = jnp.tanh(c * (x + 0.044715 * x * x * x))
    return 0.5 * x * (1.0 + th), th


def ssm_fwd(z, a, bre, bim, cre, cim, dsk, gluw, glub, name, ride=None):
    def body(u_ref, a_ref, bre_ref, bim_ref, cre_ref, cim_ref, d_ref, gw_ref, gb_ref, y_ref, yp_ref, st_hbm, st_ref):
        u = u_ref[...]
        for j in range(8):
            st_ref[0, pl.ds(j, S, stride=8), :] = _mm(u, bre_ref[:, 128 * j:128 * (j + 1)])
            st_ref[1, pl.ds(j, S, stride=8), :] = _mm(u, bim_ref[:, 128 * j:128 * (j + 1)])
        ar, ai = a_ref[0], a_ref[1]

        def step(t, c):
            re, im = c
            i = pl.multiple_of(t * 8, 8)
            nre = ar * re - ai * im + st_ref[0, pl.ds(i, 8), :]
            nim = ar * im + ai * re + st_ref[1, pl.ds(i, 8), :]
            st_ref[0, pl.ds(i, 8), :] = nre
            st_ref[1, pl.ds(i, 8), :] = nim
            return nre, nim

        zero = jnp.zeros((8, 128), F32)
        lax.fori_loop(0, S, step, (zero, zero), unroll=8)
        y = d_ref[...] * u
        for j in range(8):
            y = y + _mm(_panel(st_ref, 0, j), cre_ref[128 * j:128 * (j + 1), :])
            y = y - _mm(_panel(st_ref, 1, j), cim_ref[128 * j:128 * (j + 1), :])
        pltpu.sync_copy(st_ref, st_hbm)
        yp_ref[...] = y
        gl, _ = _gelu(y)
        tt = _mm(gl, gw_ref[...].reshape(D_SSM, D_SSM)) + gb_ref[...]
        y_ref[...] = y * jax.nn.sigmoid(tt)

    vm = lambda shape: pl.BlockSpec(shape, lambda i: (0,) * len(shape))
    return _call(
        body, name=name, grid=(1,),
        in_specs=[pl.BlockSpec((S, 256), lambda i: (0, 6)), vm((2, 8, 128)), vm((256, 1024)), vm((256, 1024)),
                  vm((1024, 256)), vm((1024, 256)), vm((1, 256)),
                  vm((NDEV, 32, 256)), vm((1, 256))],
        out_specs=(vm((S, 256)), vm((S, 256)), pl.BlockSpec(memory_space=pl.ANY)),
        out_shape=(_sds((S, 256), F32), _sds((S, 256), F32), _sds((2, S * 8, 128), F32)),
        scratch=[pltpu.VMEM((2, S * 8, 128), F32)],
        cp=_cp(40, ("arbitrary",)), args=(z, a, bre, bim, cre, cim, dsk, gluw, glub), ride=ride)


def ssm_bwd(dy, z, ypre, st, a, bre, bim, cre, cim, dsk, gluw, glub, name):
    def body(dy_ref, u_ref, yp_ref, st_hbm, a_ref, bre_ref, bim_ref, cre_ref, cim_ref, d_ref, gw_ref, gb_ref,
             du_ref, dbre_ref, dbim_ref, dcre_ref, dcim_ref, da_ref, dd_ref, dgw_ref, dgb_ref, g_ref, st_ref):
        pltpu.sync_copy(st_hbm, st_ref)
        u = u_ref[...]
        y = yp_ref[...]
        dout = dy_ref[...]
        gw = gw_ref[...].reshape(D_SSM, D_SSM)
        gl, th = _gelu(y)
        sig = jax.nn.sigmoid(_mm(gl, gw) + gb_ref[...])
        dt = dout * y * sig * (1.0 - sig)
        dgw_ref[...] = _mm_tn(gl, dt)
        dgb_ref[...] = jnp.sum(dt, axis=0, keepdims=True)
        c = math.sqrt(2.0 / math.pi)
        dgelu = 0.5 * (1.0 + th) + 0.5 * y * (1.0 - th * th) * c * (1.0 + 3.0 * 0.044715 * y * y)
        dyv = dout * sig + _mm_nt(dt, gw) * dgelu
        dd_ref[...] = jnp.sum(dyv * u, axis=0, keepdims=True)
        for j in range(8):
            rows = slice(128 * j, 128 * (j + 1))
            g_ref[0, pl.ds(j, S, stride=8), :] = _mm_nt(dyv, cre_ref[rows, :])
            g_ref[1, pl.ds(j, S, stride=8), :] = -_mm_nt(dyv, cim_ref[rows, :])
            dcre_ref[rows, :] = _mm_tn(_panel(st_ref, 0, j), dyv)
            dcim_ref[rows, :] = -_mm_tn(_panel(st_ref, 1, j), dyv)
        ar, ai = a_ref[0], a_ref[1]

        def step(k, c4):
            gre, gim, dar, dai = c4
            i = pl.multiple_of((S - 1 - k) * 8, 8)
            nre = g_ref[0, pl.ds(i, 8), :] + ar * gre + ai * gim
            nim = g_ref[1, pl.ds(i, 8), :] + ar * gim - ai * gre
            g_ref[0, pl.ds(i, 8), :] = nre
            g_ref[1, pl.ds(i, 8), :] = nim
            sre = st_ref[0, pl.ds(i - 8, 8), :]
            sim = st_ref[1, pl.ds(i - 8, 8), :]
            return nre, nim, dar + nre * sre + nim * sim, dai + nim * sre - nre * sim

        zero = jnp.zeros((8, 128), F32)
        gre, gim, dar, dai = lax.fori_loop(0, S - 1, step, (zero, zero, zero, zero), unroll=8)
        g_ref[0, 0:8, :] = g_ref[0, 0:8, :] + ar * gre + ai * gim
        g_ref[1, 0:8, :] = g_ref[1, 0:8, :] + ar * gim - ai * gre
        da_ref[0] = dar
        da_ref[1] = dai
        du = dyv * d_ref[...]
        for j in range(8):
            cols = slice(128 * j, 128 * (j + 1))
            gr, gi = _panel(g_ref, 0, j), _panel(g_ref, 1, j)
            dbre_ref[:, cols] = _mm_tn(u, gr)
            dbim_ref[:, cols] = _mm_tn(u, gi)
            du = du + _mm_nt(gr, bre_ref[:, cols]) + _mm_nt(gi, bim_ref[:, cols])
        du_ref[...] = du

    vm = lambda shape: pl.BlockSpec(shape, lambda i: (0,) * len(shape))
    return _call(
        body, name=name, grid=(1,),
        in_specs=[vm((S, 256)), pl.BlockSpec((S, 256), lambda i: (0, 6)), vm((S, 256)), pl.BlockSpec(memory_space=pl.ANY),
                  vm((2, 8, 128)), vm((256, 1024)), vm((256, 1024)), vm((1024, 256)), vm((1024, 256)), vm((1, 256)),
                  vm((NDEV, 32, 256)), vm((1, 256))],
        out_specs=(vm((S, 256)), vm((256, 1024)), vm((256, 1024)), vm((1024, 256)), vm((1024, 256)),
                   vm((2, 8, 128)), vm((1, 256)), vm((256, 256)), vm((1, 256))),
        out_shape=(_sds((S, 256), F32), _sds((256, 1024), F32), _sds((256, 1024), F32), _sds((1024, 256), F32),
                   _sds((1024, 256), F32), _sds((2, 8, 128), F32), _sds((1, 256), F32), _sds((256, 256), F32),
                   _sds((1, 256), F32)),
        scratch=[pltpu.VMEM((2, S * 8, 128), F32), pltpu.VMEM((2, S * 8, 128), F32)],
        cp=_cp(56, ("arbitrary",)), args=(dy, z, ypre, st, a, bre, bim, cre, cim, dsk, gluw, glub))[0]


def _ssm_discretise(a_re, a_im, log_dt, b_re, b_im):
    dt = jnp.exp(log_dt)[:, None]
    er = jnp.exp(a_re * dt)
    abr, abi = er * jnp.cos(a_im * dt), er * jnp.sin(a_im * dt)
    den = a_re * a_re + a_im * a_im
    fr = ((abr - 1.0) * a_re + abi * a_im) / den
    fi = (abi * a_re - (abr - 1.0) * a_im) / den
    bbr = fr[:, :, None] * b_re - fi[:, :, None] * b_im
    bbi = fr[:, :, None] * b_im + fi[:, :, None] * b_re
    return abr, abi, bbr, bbi


def _blockdiag(t):
    g, r, c = t.shape
    eye = jnp.eye(g, dtype=t.dtype)
    return (t[:, :, None, :] * eye[:, None, :, None]).reshape(g * r, g * c)


def _blockdiag_take(m, r, c):
    g = m.shape[0] // r
    idx = jnp.arange(g)
    return m.reshape(g, r, g, c)[idx, :, idx, :]


PAD = 16


def _pool_lane_select(vals):
    lane = lax.broadcasted_iota(jnp.int32, vals[0].shape, 1)
    out = vals[3]
    for g in (2, 1, 0):
        out = jnp.where(lane < 64 * (g + 1), vals[g], out)
    return out


def _pool_counts():
    row = lax.broadcasted_iota(jnp.int32, (S, D_POOL), 0).astype(F32) + 1.0
    return _pool_lane_select([jnp.minimum(row, float(w)) for w in POOL_WINDOWS])


def _pooled(u, sa, sb):
    sums = []
    cur = u
    bufs = (sa, sb)
    for k, sh in enumerate((1, 2, 4, 8)):
        buf = bufs[k % 2]
        buf[PAD:PAD + S, :] = cur
        cur = cur + buf[PAD - sh:PAD - sh + S, :]
        sums.append(cur)
    return _pool_lane_select(sums) / _pool_counts() - u


def pool_fwd(z, pw, psc, name):
    def body(u_ref, w_ref, s_ref, y_ref, sa, sb):
        for buf in (sa, sb):
            buf[0:PAD, :] = jnp.zeros((PAD, D_POOL), F32)
        pooled = _pooled(u_ref[...], sa, sb)
        y_ref[...] = _mm(pooled, w_ref[...]) * s_ref[...]

    vm = lambda shape: pl.BlockSpec(shape, lambda i: (0,) * len(shape))
    return _pallas_call(
        body, name=name, grid=(1,),
        in_specs=[pl.BlockSpec((S, 256), lambda i: (0, 7)), vm((256, 256)), vm((1, 256))],
        out_specs=vm((S, 256)), out_shape=_sds((S, 256), F32),
        scratch_shapes=[pltpu.VMEM((S + 2 * PAD, D_POOL), F32)] * 2,
        compiler_params=_cp(40, ("arbitrary",)))(z, pw, psc)


def pool_bwd(dy, z, pw, psc, name):
    def body(dy_ref, u_ref, w_ref, s_ref, du_ref, dw_ref, ds_ref, sa, sb):
        for buf in (sa, sb):
            buf[0:PAD, :] = jnp.zeros((PAD, D_POOL), F32)
            buf[PAD + S:PAD + S + PAD, :] = jnp.zeros((PAD, D_POOL), F32)
        pooled = _pooled(u_ref[...], sa, sb)
        dyv = dy_ref[...]
        w = w_ref[...]
        ds_ref[...] = jnp.sum(dyv * _mm(pooled, w), axis=0, keepdims=True)
        dyl = dyv * s_ref[...]
        dw_ref[...] = _mm_tn(pooled, dyl)
        dpool = _mm_nt(dyl, w)
        cur = dpool / _pool_counts()
        sums = []
        bufs = (sa, sb)
        for k, sh in enumerate((1, 2, 4, 8)):
            buf = bufs[k % 2]
            buf[PAD:PAD + S, :] = cur
            cur = cur + buf[PAD + sh:PAD + sh + S, :]
            sums.append(cur)
        du_ref[...] = _pool_lane_select(sums) - dpool

    vm = lambda shape: pl.BlockSpec(shape, lambda i: (0,) * len(shape))
    return _pallas_call(
        body, name=name, grid=(1,),
        in_specs=[vm((S, 256)), pl.BlockSpec((S, 256), lambda i: (0, 7)), vm((256, 256)), vm((1, 256))],
        out_specs=(vm((S, 256)), vm((256, 256)), vm((1, 256))),
        out_shape=(_sds((S, 256), F32), _sds((256, 256), F32), _sds((1, 256), F32)),
        scratch_shapes=[pltpu.VMEM((S + 2 * PAD, D_POOL), F32)] * 2,
        compiler_params=_cp(40, ("arbitrary",)))(dy, z, pw, psc)


def ada_fwd(c_all, ada_w, ada_b_cols):
    def body(c_ref, w_ref, b_ref, o_ref):
        c = c_ref[...]
        cond = c * jax.nn.sigmoid(c)
        o_ref[...] = jnp.dot(cond, w_ref[...], preferred_element_type=F32, precision=lax.Precision.HIGHEST) + b_ref[...]

    return _pallas_call(
        body, name="ada_fwd", grid=(DEPTH,),
        in_specs=[pl.BlockSpec((NDEV, D), lambda l: (0, 0)), pl.BlockSpec((None, D, 1152), lambda l: (l, 0, 0)),
                  pl.BlockSpec((None, 1, 1152), lambda l: (l, 0, 0))],
        out_specs=pl.BlockSpec((None, NDEV, 1152), lambda l: (l, 0, 0)), out_shape=_sds((DEPTH, NDEV, 1152), F32),
        compiler_params=_cp(40, ("arbitrary",)))(c_all, ada_w, ada_b_cols)


def ada_bwd(c_all, dmod_cols):
    def body(c_ref, dm_ref, o_ref):
        c = c_ref[...]
        cond = c * jax.nn.sigmoid(c)
        o_ref[...] = lax.dot_general(cond, dm_ref[...], (((0,), (0,)), ((), ())), preferred_element_type=F32,
                                     precision=lax.Precision.HIGHEST)

    return _pallas_call(
        body, name="ada_bwd", grid=(DEPTH,),
        in_specs=[pl.BlockSpec((NDEV, D), lambda l: (0, 0)), pl.BlockSpec((None, NDEV, 1152), lambda l: (l, 0, 0))],
        out_specs=pl.BlockSpec((None, D, 1152), lambda l: (l, 0, 0)), out_shape=_sds((DEPTH, D, 1152), F32),
        compiler_params=_cp(40, ("arbitrary",)))(c_all, dmod_cols)


def _adamw(w, g, m, v):
    m2 = B1 * m + (1.0 - B1) * g
    v2 = B2 * v + (1.0 - B2) * (g * g)
    m_hat = m2 / (1.0 - B1 ** STEP)
    v_hat = v2 / (1.0 - B2 ** STEP)
    return -LR * (m_hat / (jnp.sqrt(v_hat) + EPS) + WD * w), m2, v2


def _sum8(ref):
    g = ref[0].astype(F32)
    for s in range(1, ref.shape[0]):
        g = g + ref[s].astype(F32)
    return g


def adam_rs(recv, w, m, v, tr, name):
    lead, (r, cdim) = w.shape[:-2], w.shape[-2:]
    cp = recv.shape[-1]
    nl = len(lead)

    def body(rc_ref, w_ref, m_ref, v_ref, g_ref, d_ref, m2_ref, v2_ref):
        g = _sum8(rc_ref)[:, :cdim]
        g_ref[...] = g
        d_ref[...], m2_ref[...], v2_ref[...] = _adamw(w_ref[...], g, m_ref[...], v_ref[...])

    rs = pl.BlockSpec((None,) * nl + (tr, cdim), lambda *i: (*i, 0))
    return _call(
        body, name=name, grid=lead + (r // tr,),
        in_specs=[pl.BlockSpec((NDEV,) + (None,) * nl + (tr, cp), lambda *i: (0, *i, 0)), rs, rs, rs],
        out_specs=(rs, rs, rs, rs), out_shape=tuple(_sds(w.shape, F32) for _ in range(4)),
        cp=_cp(48, ("arbitrary",) * (nl + 1)), args=(recv, w, m, v))[0]


def adam_block(recv, w, m, v, lf, prev, name):
    half = FB // 2

    def body(*refs):
        rc_ref, w_ref, m_ref, v_ref = refs[:4]
        g_ref, d_ref, m2_ref, v2_ref = refs[-4:]
        g = _sum8(rc_ref)
        g_ref[...] = g
        d_ref[...], m2_ref[...], v2_ref[...] = _adamw(w_ref[...], g, m_ref[...], v_ref[...])

    rs = pl.BlockSpec((None, None, half, D), lambda i: (lf // 2, lf % 2, i, 0))
    prev = list(prev) if prev is not None else []
    return list(_pallas_call(
        body, name=name, grid=(2,), in_specs=[pl.BlockSpec((recv.shape[0], half, D), lambda i: (0, i, 0)), rs, rs, rs] + [ANY] * len(prev),
        out_specs=(rs, rs, rs, rs), out_shape=tuple(_sds((DEPTH, 2, FB, D), F32) for _ in range(4)),
        input_output_aliases={4 + k: k for k in range(len(prev))},
        compiler_params=_cp(48, ("arbitrary",)))(recv, w, m, v, *prev))


def adam_plain(g, w, m, v, tr, name):
    lead, (r, cdim) = w.shape[:-2], w.shape[-2:]
    nl = len(lead)

    def body(g_ref, w_ref, m_ref, v_ref, d_ref, m2_ref, v2_ref):
        d_ref[...], m2_ref[...], v2_ref[...] = _adamw(w_ref[...], g_ref[...], m_ref[...], v_ref[...])

    rs = pl.BlockSpec((None,) * nl + (tr, cdim), lambda *i: (*i, 0))
    return _call(
        body, name=name, grid=lead + (r // tr,), in_specs=[rs, rs, rs, rs], out_specs=(rs, rs, rs),
        out_shape=tuple(_sds(w.shape, F32) for _ in range(3)),
        cp=_cp(48, ("arbitrary",) * (nl + 1)), args=(g, w, m, v))[0]


def adam_native(gs, ws, ms, vs, name):
    n = len(ws)

    def body(*refs):
        g_refs, w_refs, m_refs, v_refs = (refs[k * n:(k + 1) * n] for k in range(4))
        d_refs, m2_refs, v2_refs = (refs[(4 + k) * n:(5 + k) * n] for k in range(3))
        for a in range(n):
            d_refs[a][...], m2_refs[a][...], v2_refs[a][...] = _adamw(w_refs[a][...], g_refs[a][...], m_refs[a][...], v_refs[a][...])

    vm = pl.BlockSpec(memory_space=pltpu.VMEM)
    outs = _pallas_call(body, name=name, in_specs=[vm] * (4 * n), out_specs=tuple([vm] * (3 * n)),
                        out_shape=tuple(_sds(w.shape, F32) for w in ws) * 3, compiler_params=_cp(40))(*gs, *ws, *ms, *vs)
    return outs[:n], outs[n:2 * n], outs[2 * n:]


def sum_sources(recv, name):
    r = recv.shape[1]

    def body(rc_ref, o_ref):
        o_ref[...] = _sum8(rc_ref)

    vm = pl.BlockSpec(memory_space=pltpu.VMEM)
    return _pallas_call(body, name=name, in_specs=[vm], out_specs=vm, out_shape=_sds((r, 128), F32),
                          compiler_params=_cp(40))(recv)


def _pack(arrs, dtype=F32):
    flat = jnp.concatenate([a.reshape(-1) for a in arrs]).astype(dtype)
    n = flat.shape[0]
    tile = 128 * (32 // jnp.dtype(dtype).itemsize)
    rows = -(-n // tile) * (tile // 128)
    return jnp.pad(flat, (0, rows * 128 - n)).reshape(rows, 128)


def _unpack(vec, shapes):
    flat = vec.reshape(-1)
    out, o = [], 0
    for sh in shapes:
        n = int(np.prod(sh))
        out.append(flat[o:o + n].reshape(sh))
        o += n
    return out


WEIGHTS = ['rel_bias', 'ada_w', 'ada_b', 'ln_g', 'ln_b', 'ffn_w_gate', 'ffn_w_up', 'ffn_w_down', 'w_in', 'w_out',
           'ssm_a_re', 'ssm_a_im', 'ssm_log_dt', 'ssm_b_re', 'ssm_b_im', 'ssm_c_re', 'ssm_c_im', 'ssm_d', 'glu_w',
           'glu_b', 'pool_w', 'pool_scale']
SMALL = ['rel_bias', 'ada_b', 'ln_g', 'ln_b', 'ssm_a_re', 'ssm_a_im', 'ssm_log_dt', 'ssm_b_re', 'ssm_b_im',
         'ssm_c_re', 'ssm_c_im', 'ssm_d', 'glu_b', 'pool_w', 'pool_scale']
SMALL_EXACT = ['rel_bias', 'ln_g', 'ln_b', 'ssm_a_re', 'ssm_a_im', 'ssm_log_dt']
SMALL_ROUNDED = ['ada_b', 'ssm_b_re', 'ssm_b_im', 'ssm_c_re', 'ssm_c_im', 'ssm_d', 'glu_b', 'pool_w', 'pool_scale']
SMALL_FULL_SHAPES = {'rel_bias': (32, 8), 'ada_b': (2, 9216), 'ln_g': (2, 3, 1024), 'ln_b': (2, 3, 1024),
                     'ssm_a_re': (2, 16, 64), 'ssm_a_im': (2, 16, 64), 'ssm_log_dt': (2, 16),
                     'ssm_b_re': (2, 16, 64, 16), 'ssm_b_im': (2, 16, 64, 16), 'ssm_c_re': (2, 16, 16, 64),
                     'ssm_c_im': (2, 16, 16, 64), 'ssm_d': (2, 256), 'glu_b': (2, 256), 'pool_w': (2, 4, 64, 64),
                     'pool_scale': (2, 256)}


def _step(P):
    me = _me()
    x0 = P['x'][0]
    target = P['loss_target'][0]

    bf = lambda a: a.astype(BF16)
    padr = lambda a: jnp.pad(bf(a), ((0, 0), (0, 0), (0, FBP - FB), (0, 0)))
    ffn_b = [padr(jnp.swapaxes(P['ffn_w_gate'], 2, 3)), padr(jnp.swapaxes(P['ffn_w_up'], 2, 3)), padr(P['ffn_w_down'])]
    mix_b = [bf(P['w_in']), bf(P['w_out']), bf(P['glu_w'])]

    def shards(l, sub):
        return [t[l] for t in mix_b] if sub == 1 else [t[l, sub // 2] for t in ffn_b]

    order = [(l, sub) for l in range(DEPTH) for sub in range(3)]
    nxt = dict(zip(order[:-1], order[1:]))
    W = {key: [None] * 3 for key in order}
    c_all, lng_all, lnb_all, *W[order[0]] = _exchange(Gather([P['c'], P['ln_g'], P['ln_b']] + shards(*order[0])), "gather_first")
    gather_queue = [(key, pos, a) for key in order[1:] for pos, a in enumerate(shards(*key))]

    def gather_ride(cap_us, must=None):
        units, used = [], 0.0
        while gather_queue:
            key, _, a = gather_queue[0]
            cost = a.size * a.dtype.itemsize * GATHER_US_PER_BYTE
            if key != must and used + cost / 2 > cap_us:
                break
            units.append(gather_queue.pop(0))
            used += cost
        return (Gather([a for _, _, a in units]) if units else None), units

    def gathered(units, outs):
        for (key, pos, _), o in zip(units, outs):
            W[key][pos] = o

    c_all = c_all.reshape(NDEV, D)
    ln_g = jnp.transpose(lng_all, (1, 2, 0, 3)).reshape(DEPTH, 3, D)
    ln_b = jnp.transpose(lnb_all, (1, 2, 0, 3)).reshape(DEPTH, 3, D)

    ada_b_cols = lax.dynamic_slice_in_dim(P['ada_b'], me * 1152, 1152, axis=1).reshape(DEPTH, 1, 1152)
    modc = ada_fwd(c_all, P['ada_w'], ada_b_cols)
    (mod_all,) = _exchange(Gather([modc]), "gather_mod")
    mod_me = lax.dynamic_index_in_dim(mod_all, me, axis=2, keepdims=False)
    mod = jnp.transpose(mod_me, (1, 0, 2)).reshape(DEPTH, 9, D)

    bias = att_bias(P['rel_bias'])
    ssm = []
    for l in range(DEPTH):
        prm = (P['ssm_a_re'][l], P['ssm_a_im'][l], P['ssm_log_dt'][l], P['ssm_b_re'][l], P['ssm_b_im'][l])
        (abr, abi, bbr, bbi), disc_vjp = jax.vjp(_ssm_discretise, *prm)
        ssm.append(dict(
            vjp=disc_vjp, a=jnp.stack([abr.reshape(8, 128), abi.reshape(8, 128)]),
            bre=_blockdiag(jnp.transpose(bbr, (0, 2, 1))).astype(MXU), bim=_blockdiag(jnp.transpose(bbi, (0, 2, 1))).astype(MXU),
            cre=_blockdiag(jnp.transpose(P['ssm_c_re'][l], (0, 2, 1))).astype(MXU),
            cim=_blockdiag(jnp.transpose(P['ssm_c_im'][l], (0, 2, 1))).astype(MXU),
            d=P['ssm_d'][l].reshape(1, 256), gb=P['glu_b'][l].reshape(1, 256),
            pw=_blockdiag(P['pool_w'][l]).astype(MXU), psc=P['pool_scale'][l].reshape(1, 256)))

    saved = []
    x = x0
    h = ln_mod_fwd(x, mod[0], 0, "ln_mod_fwd_l0s0")
    for l, sub in order:
        tag = f"l{l}s{sub}"
        after = (mod[nxt[(l, sub)][0]], nxt[(l, sub)][1]) if (l, sub) in nxt else None
        if sub != 1:
            wg, wu, wd = (t.reshape(NDEV * FBP, D) for t in W[(l, sub)])
            ride, units = gather_ride(60, nxt.get((l, sub)))
            (G, U, fo), got = ffn_fwd(h, wg, wu, wd, "ffn_fwd_" + tag, ride)
            gathered(units, got)
            saved.append(dict(x=x, h=h, G=G, U=U, f=fo))
            x, *hn = res_ln_fwd(x, fo, mod[l], sub, ln_g[l], ln_b[l], 0.5, "res_ln_fwd_" + tag, after)
        else:
            sp = ssm[l]
            win, wout, gluw = W[(l, sub)]
            ride, units = gather_ride(15)
            (z,), got = win_fwd(h, win, "win_fwd_" + tag, ride)
            gathered(units, got)
            ride, units = gather_ride(55)
            (ya, lse), got = att_fwd(z, bias, "att_fwd_" + tag, ride)
            gathered(units, got)
            ride, units = gather_ride(35)
            (ys, ypre, st), got = ssm_fwd(z, sp['a'], sp['bre'], sp['bim'], sp['cre'], sp['cim'], sp['d'], gluw, sp['gb'],
                                          "ssm_fwd_" + tag, ride)
            gathered(units, got)
            yp = pool_fwd(z, sp['pw'], sp['psc'], "pool_fwd_" + tag)
            ride, units = gather_ride(12, nxt.get((l, sub)))
            (o,), got = wout_fwd(ya, ys, yp, wout, "wout_fwd_" + tag, ride)
            gathered(units, got)
            saved.append(dict(x=x, h=h, z=z, ya=ya, lse=lse, ys=ys, ypre=ypre, st=st, yp=yp, f=o))
            x, *hn = res_ln_fwd(x, o, mod[l], sub, ln_g[l], ln_b[l], 1.0, "res_ln_fwd_" + tag, after)
        h = hn[0] if hn else None
    assert not gather_queue

    loss_tile, dx = loss_fwd_bwd(x, target, "loss")
    loss = lax.psum(loss_tile[0, 0], ("x", "y", "c"))

    flights = []

    dmod = [[None] * 9 for _ in range(DEPTH)]
    dlng = [[None] * 3 for _ in range(DEPTH)]
    dlnb = [[None] * 3 for _ in range(DEPTH)]
    dbiases = [None] * DEPTH
    small_l = [dict() for _ in range(DEPTH)]
    for l, sub in reversed(order):
        tag = f"l{l}s{sub}"
        sv = saved[3 * l + sub]
        if (l, sub) == order[-1]:
            dxa, df, sums = res_ln_bwd(sv['x'], sv['f'], mod[l], sub, ln_g[l], dx, 0.5, "res_ln_bwd_" + tag)
        dlng[l][sub], dlnb[l][sub], dmod[l][3 * sub + 2] = sums[0], sums[1], sums[2]
        if sub != 1:
            f = sub // 2
            wg, wu, wd = (t.reshape(NDEV * FBP, D) for t in W[(l, sub)])
            dwg, dwu, dwd, dh = ffn_bwd(df, sv['h'], sv['G'], sv['U'], wg, wu, wd, "ffn_bwd_" + tag)
            handle, zero = scatter_start([t.reshape(NDEV, FBP, D) for t in (dwg, dwu, dwd)], "scatter_start_" + tag, rows=FB)
            flights.append(((l, sub), handle))
        else:
            sp = ssm[l]
            win, wout, gluw = W[(l, sub)]
            dya, dys, dyp, dwout = wout_bwd(df, sv['ya'], sv['ys'], sv['yp'], wout, "wout_bwd_" + tag)
            dq, dk, dv, dbiases[l] = att_bwd(sv['z'], bias, sv['ya'], sv['lse'], dya, "att_bwd_" + tag)
            dus, dbre, dbim, dcre, dcim, da, dd, dgw, dgb = ssm_bwd(
                dys, sv['z'], sv['ypre'], sv['st'], sp['a'], sp['bre'], sp['bim'], sp['cre'], sp['cim'], sp['d'],
                gluw, sp['gb'], "ssm_bwd_" + tag)
            dup, dpw, dpsc = pool_bwd(dyp, sv['z'], sp['pw'], sp['psc'], "pool_bwd_" + tag)
            dh, dwin = win_bwd((dq, dk, dv, dus, dup), sv['h'], win, "win_bwd_" + tag)
            handle, zero = scatter_start([dwin, dwout, dgw.astype(BF16).reshape(NDEV, 32, 256)], "scatter_start_" + tag)
            flights.append(((l, sub), handle))
            d_are, d_aim, d_ldt, d_bre, d_bim = sp['vjp']((
                da[0].reshape(16, 64), da[1].reshape(16, 64),
                jnp.transpose(_blockdiag_take(dbre, 16, 64), (0, 2, 1)), jnp.transpose(_blockdiag_take(dbim, 16, 64), (0, 2, 1))))
            small_l[l] = dict(
                ssm_a_re=d_are, ssm_a_im=d_aim, ssm_log_dt=d_ldt, ssm_b_re=d_bre, ssm_b_im=d_bim,
                ssm_c_re=jnp.transpose(_blockdiag_take(dcre, 64, 16), (0, 2, 1)),
                ssm_c_im=jnp.transpose(_blockdiag_take(dcim, 64, 16), (0, 2, 1)),
                ssm_d=dd.reshape(256), glu_b=dgb.reshape(256), pool_w=_blockdiag_take(dpw, 64, 64), pool_scale=dpsc.reshape(256))
        if (l, sub) == order[0]:
            dx, sums2 = ln_mod_bwd(sv['x'], dh, mod[l] + zero, sub, dxa, "ln_mod_bwd_" + tag)
        else:
            lp, sp_ = order[order.index((l, sub)) - 1]
            svp = saved[3 * lp + sp_]
            dxa, df, sums, sums2 = ln_join_bwd(svp['x'], svp['f'], mod[lp], sp_, ln_g[lp], ln_b[lp], 1.0 if sp_ == 1 else 0.5,
                                               dh, mod[l] + zero, sub, dxa, "ln_join_bwd_" + tag)
        dmod[l][3 * sub], dmod[l][3 * sub + 1] = sums2[0], sums2[1]
    grad_x = dx[None]

    small = {k: jnp.stack([small_l[l][k] for l in range(DEPTH)]) for k in small_l[0]}
    small['rel_bias'] = relbias_grad(dbiases)
    small['ada_b'] = jnp.stack([jnp.stack(dmod[l]).reshape(9 * D) for l in range(DEPTH)])
    small['ln_g'] = jnp.stack([jnp.stack(dlng[l]) for l in range(DEPTH)])
    small['ln_b'] = jnp.stack([jnp.stack(dlnb[l]) for l in range(DEPTH)])
    small_flight, _ = scatter_start([_pack([small[k] for k in SMALL_EXACT]), _pack([small[k] for k in SMALL_ROUNDED], BF16)],
                                    "gather_small_start", whole=True)

    out = {}

    def put(name, g, d, m2, v2, shape):
        out['grad_' + name], out['delta_' + name] = g.reshape(shape), d.reshape(shape)
        out['new_m_' + name], out['new_v_' + name] = m2.reshape(shape), v2.reshape(shape)

    def wmv(name):
        return [P[pre + name] for pre in ('', 'm_', 'v_')]

    recv = {}
    started_last = small_flight[1][0]
    for key, handle in flights[:-1]:
        recv[key] = scatter_wait(handle, started_last, "scatter_wait_l%ds%d" % key)
    for pos, (name, tr) in enumerate((('w_in', 512), ('w_out', 128), ('glu_w', 32))):
        both = jnp.stack([recv[(l, 1)][pos] for l in range(DEPTH)], axis=1)
        put(name, *adam_rs(both, *wmv(name), tr, "adam_" + name), P[name].shape)
    ffn = (('ffn_w_gate', [jnp.swapaxes(t, 2, 3) for t in wmv('ffn_w_gate')]),
           ('ffn_w_up', [jnp.swapaxes(t, 2, 3) for t in wmv('ffn_w_up')]), ('ffn_w_down', wmv('ffn_w_down')))
    part = [None] * 3
    for l, sub in [key for key, _ in flights[:-1] if key[1] != 1]:
        for pos, (name, ops) in enumerate(ffn):
            part[pos] = adam_block(recv[(l, sub)][pos], *ops, 2 * l + sub // 2, part[pos], f"adam_{name}_l{l}s{sub}")

    exact_all, rounded_all = scatter_wait(small_flight, [p[0] for p in part] + [out['new_v_w_in']], "gather_small_wait")
    gsum = dict(zip(SMALL_EXACT, _unpack(sum_sources(exact_all, "sum_small_exact"), [SMALL_FULL_SHAPES[k] for k in SMALL_EXACT])))
    gsum.update(zip(SMALL_ROUNDED, _unpack(sum_sources(rounded_all, "sum_small_rounded"), [SMALL_FULL_SHAPES[k] for k in SMALL_ROUNDED])))
    dmod_all = rounded_all.reshape(NDEV, -1)[:, :DEPTH * 9 * D].astype(F32).reshape(NDEV, DEPTH, 9 * D)
    dmod_cols = jnp.transpose(lax.dynamic_slice_in_dim(dmod_all, me * 1152, 1152, axis=2), (1, 0, 2))
    g_ada_w = ada_bwd(c_all, dmod_cols)

    put('ada_w', g_ada_w, *adam_plain(g_ada_w, *wmv('ada_w'), 256, "adam_ada_w"), P['ada_w'].shape)

    for k in ('ln_g', 'ln_b'):
        gsum[k] = lax.dynamic_slice_in_dim(gsum[k], me * 128, 128, axis=2)
    swaps = {'rel_bias': (0, 1), 'ln_g': (0, 1), 'ln_b': (0, 1), 'ssm_b_re': (2, 3), 'ssm_b_im': (2, 3)}
    view = lambda k, t: jnp.swapaxes(t, *swaps[k]) if k in swaps else t
    ds_, m2s, v2s = adam_native(*[[view(k, src(k)) for k in SMALL] for src in
                                  (lambda k: gsum[k], lambda k: P[k], lambda k: P['m_' + k], lambda k: P['v_' + k])],
                                "adam_small")
    for k, d, m2, v2 in zip(SMALL, ds_, m2s, v2s):
        put(k, gsum[k], view(k, d), view(k, m2), view(k, v2), P[k].shape)

    (l, sub), handle = flights[-1]
    last = scatter_wait(handle, out['new_v_ada_w'], "scatter_wait_l%ds%d" % (l, sub))
    for pos, (name, ops) in enumerate(ffn):
        res = adam_block(last[pos], *ops, 2 * l + sub // 2, part[pos], f"adam_{name}_l{l}s{sub}")
        put(name, *([jnp.swapaxes(t, 2, 3) for t in res] if pos < 2 else res), P[name].shape)

    res = [loss, grad_x]
    for pre in ('grad_', 'delta_', 'new_m_', 'new_v_'):
        res += [out[pre + k] for k in WEIGHTS]
    return tuple(res)


def kernel(x, c, rel_bias, ada_w, ada_b, ln_g, ln_b, ffn_w_gate, ffn_w_up, ffn_w_down, w_in, w_out, ssm_a_re, ssm_a_im, ssm_log_dt, ssm_b_re, ssm_b_im, ssm_c_re, ssm_c_im, ssm_d, glu_w, glu_b, pool_w, pool_scale, loss_target, m_rel_bias, m_ada_w, m_ada_b, m_ln_g, m_ln_b, m_ffn_w_gate, m_ffn_w_up, m_ffn_w_down, m_w_in, m_w_out, m_ssm_a_re, m_ssm_a_im, m_ssm_log_dt, m_ssm_b_re, m_ssm_b_im, m_ssm_c_re, m_ssm_c_im, m_ssm_d, m_glu_w, m_glu_b, m_pool_w, m_pool_scale, v_rel_bias, v_ada_w, v_ada_b, v_ln_g, v_ln_b, v_ffn_w_gate, v_ffn_w_up, v_ffn_w_down, v_w_in, v_w_out, v_ssm_a_re, v_ssm_a_im, v_ssm_log_dt, v_ssm_b_re, v_ssm_b_im, v_ssm_c_re, v_ssm_c_im, v_ssm_d, v_glu_w, v_glu_b, v_pool_w, v_pool_scale):
    return _step(dict(locals()))
```

```python
import functools
import math

import numpy as np
import jax
import jax.numpy as jnp
from jax import lax
from jax.experimental import pallas as pl
from jax.experimental.pallas import tpu as pltpu

F32 = jnp.float32
BF16 = jnp.bfloat16
MXU = jnp.bfloat16

S = 2048
D = 1024
NDEV = 8
DEPTH = 2
D_ATT, D_SSM, D_POOL, D_IN = 512, 256, 256, 2048
N_HEADS = 8
FB = 352
FBP = 384
QB = 128
PATTERNS = ((128, 1), (512, 4), (2048, 16))
POOL_WINDOWS = (2, 4, 8, 16)
N_BUCKETS, MAX_DISTANCE = 32, 2048
ALPHA = (2 * DEPTH) ** 0.25
LN_EPS = 1e-5
NEG = -1e30
GATHER_US_PER_BYTE = 43e-6
LR, B1, B2, EPS, WD, STEP = 0.001, 0.9, 0.999, 1e-08, 0.01, 10

TM = 256
TMM = 512
MIB = 1024 * 1024


def _cp(vmem_mib, sem=None):
    kw = dict(vmem_limit_bytes=vmem_mib * MIB)
    if sem is not None:
        kw["dimension_semantics"] = sem
    return pltpu.CompilerParams(**kw)


def _sds(shape, dtype):
    return jax.ShapeDtypeStruct(shape, dtype)


def _mm(a, b):
    return jnp.dot(a.astype(MXU), b.astype(MXU), preferred_element_type=F32)


def _mm_nt(a, b):
    return lax.dot_general(a.astype(MXU), b.astype(MXU), (((1,), (1,)), ((), ())), preferred_element_type=F32)


def _mm_tn(a, b):
    return lax.dot_general(a.astype(MXU), b.astype(MXU), (((0,), (0,)), ((), ())), preferred_element_type=F32)


def _ln_stats(x):
    mu = jnp.mean(x, axis=-1, keepdims=True)
    xc = x - mu
    var = jnp.mean(xc * xc, axis=-1, keepdims=True)
    rstd = lax.rsqrt(var + LN_EPS)
    return xc * rstd, rstd


def _ln_bwd(dn, n, rstd):
    return rstd * (dn - jnp.mean(dn, axis=-1, keepdims=True) - n * jnp.mean(dn * n, axis=-1, keepdims=True))


def _me():
    return 4 * lax.axis_index("x") + 2 * lax.axis_index("y") + lax.axis_index("c")


ANY = pl.BlockSpec(memory_space=pl.ANY)
PIN_BYTES = 1 << 19


def _pallas_call(*a, **k):
    big = lambda o: math.prod(o.shape) * o.dtype.itemsize >= PIN_BYTES
    pin = lambda o: pltpu.HBM(o.shape, o.dtype) if isinstance(o, jax.ShapeDtypeStruct) and big(o) else o
    osh = k["out_shape"]
    k["out_shape"] = tuple(pin(o) for o in osh) if isinstance(osh, (tuple, list)) else pin(osh)
    fn = pl.pallas_call(*a, **k)

    def run(*args):
        return fn(*[pltpu.with_memory_space_constraint(x, pltpu.HBM) if big(x) else x for x in args])
    return run


class Gather:
    def __init__(self, srcs):
        self.srcs = list(srcs)
        self.n = len(self.srcs)
        self.bufs = []
        self.out_shapes = [_sds((NDEV,) + a.shape, a.dtype) for a in self.srcs]
        self.sems = [pltpu.SemaphoreType.DMA((7 * self.n,)), pltpu.SemaphoreType.DMA((7 * self.n,)),
                     pltpu.SemaphoreType.DMA((self.n,))]

    def _parts(self, srcs, outs, sems):
        send_sems, recv_sems, loc_sems = sems
        x, y, c = lax.axis_index("x"), lax.axis_index("y"), lax.axis_index("c")
        me, sib = (x, y, c), (x, y, 1 - c)
        chips = [(1 - x, y), (x, 1 - y), (1 - x, 1 - y)]
        slot = lambda d: 4 * d[0] + 2 * d[1] + d[2]

        def copy(a, k, block, to, src=None):
            dst = outs[a].at[slot(block)]
            return pltpu.make_async_remote_copy(
                src_ref=dst if src is None else src, dst_ref=dst,
                send_sem=send_sems.at[7 * a + k], recv_sem=recv_sems.at[7 * a + k],
                device_id=to, device_id_type=pl.DeviceIdType.MESH)

        local = [pltpu.make_async_copy(srcs[a], outs[a].at[slot(me)], loc_sems.at[a]) for a in range(self.n)]
        return me, sib, chips, c, copy, local

    def start(self, srcs, bufs, outs, sems):
        me, sib, chips, c, copy, local = self._parts(srcs, outs, sems)
        for a in range(self.n):
            local[a].start()
            copy(a, 0, me, sib, src=srcs[a]).start()
            for j, chip in enumerate(chips):
                copy(a, 1 + j, me, (*chip, c), src=srcs[a]).start()

    def finish(self, srcs, bufs, outs, sems):
        me, sib, chips, c, copy, local = self._parts(srcs, outs, sems)
        for a in range(self.n):
            for j, chip in enumerate(chips):
                copy(a, 1 + j, (*chip, c), me).wait_recv()
                copy(a, 4 + j, (*chip, c), sib).start()
        for a in range(self.n):
            copy(a, 0, sib, me).wait_recv()
            copy(a, 0, me, sib, src=srcs[a]).wait_send()
            for j, chip in enumerate(chips):
                copy(a, 4 + j, (*chip, 1 - c), me).wait_recv()
                copy(a, 1 + j, me, (*chip, c), src=srcs[a]).wait_send()
                copy(a, 4 + j, (*chip, c), sib).wait_send()
            local[a].wait()


def _call(body, *, name, grid, in_specs, out_specs, out_shape, args, scratch=(), cp=None, ride=None):
    out_specs, out_shape, scratch = list(out_specs), list(out_shape), list(scratch)
    if ride is None:
        outs = _pallas_call(body, name=name, grid=grid, in_specs=list(in_specs), out_specs=tuple(out_specs),
                              out_shape=tuple(out_shape), scratch_shapes=scratch, compiler_params=cp)(*args)
        return list(outs), []
    nin, nout, nscr, n, nb, no = len(in_specs), len(out_specs), len(scratch), ride.n, len(ride.bufs), len(ride.out_shapes)
    steps = list(grid)

    def wrapped(*refs):
        h_in, r_src, r_buf = refs[:nin], refs[nin:nin + n], refs[nin + n:nin + n + nb]
        o0 = nin + n + nb
        h_out, r_out = refs[o0:o0 + nout], refs[o0 + nout:o0 + nout + no]
        s0 = o0 + nout + no
        h_scr, sems = refs[s0:s0 + nscr], refs[s0 + nscr:]
        ids = [pl.program_id(a) for a in range(len(steps))]
        first = functools.reduce(jnp.logical_and, [i == 0 for i in ids])
        last = functools.reduce(jnp.logical_and, [i == s - 1 for i, s in zip(ids, steps)])

        @pl.when(first)
        def _():
            ride.start(r_src, r_buf, r_out, sems)

        body(*h_in, *h_out, *h_scr)

        @pl.when(last)
        def _():
            ride.finish(r_src, r_buf, r_out, sems)

    aliases = {nin + n + k: nout + k for k in range(nb)}
    outs = _pallas_call(
        wrapped, name=name, grid=grid, in_specs=list(in_specs) + [ANY] * (n + nb),
        out_specs=tuple(out_specs + [ANY] * no), out_shape=tuple(out_shape + ride.out_shapes),
        scratch_shapes=scratch + ride.sems, input_output_aliases=aliases, compiler_params=cp,
    )(*args, *ride.srcs, *ride.bufs)
    return list(outs[:nout]), list(outs[nout:])


def _exchange(ride, name):
    def body(dummy_ref, o_ref):
        o_ref[...] = dummy_ref[...]

    one = pl.BlockSpec((8, 128), lambda i: (0, 0))
    _, outs = _call(body, name=name, grid=(1,), in_specs=[one], out_specs=[one], out_shape=[_sds((8, 128), F32)],
                    args=(jnp.zeros((8, 128), F32),), ride=ride)
    return outs


HBM = pl.BlockSpec(memory_space=pltpu.HBM)
SEM = pl.BlockSpec(memory_space=pltpu.SEMAPHORE)


def routes_all(me):
    return [(k, me ^ k, me, me ^ k) for k in range(NDEV)]


def _scatter_copies(srcs, lands, sems, sending, whole, rows):
    send_sems, recv_sems, loc_sems = sems
    me = _me()
    rts = routes_all(me)
    part = lambda ref, slab: ref if whole else ref.at[slab] if rows is None else ref.at[slab, pl.ds(0, rows)]
    remote_ix = [r for r, (k, _, _, _) in enumerate(rts) if k != 0]
    local_ix = [r for r, (k, _, _, _) in enumerate(rts) if k == 0]
    remote, local = [], []
    for a in range(len(srcs)):
        for n, r in enumerate(remote_ix):
            k, slab, there, here = rts[r]
            t = me ^ k
            sem = len(remote_ix) * a + n
            remote.append(pltpu.make_async_remote_copy(
                src_ref=part(srcs[a], slab), dst_ref=lands[a].at[there if sending else here], send_sem=send_sems.at[sem],
                recv_sem=recv_sems.at[sem], device_id=(t // 4, (t // 2) % 2, t % 2), device_id_type=pl.DeviceIdType.MESH))
        for n, r in enumerate(local_ix):
            _, slab, there, _ = rts[r]
            local.append(pltpu.make_async_copy(part(srcs[a], slab), lands[a].at[there], loc_sems.at[len(local_ix) * a + n]))
    return remote, local


def scatter_start(payloads, name, whole=False, rows=None):
    n = len(payloads)
    nr = NDEV - 1

    def body(*refs):
        srcs, lands, sems = refs[:n], refs[n:2 * n], refs[2 * n:2 * n + 3]
        remote, local = _scatter_copies(srcs, lands, sems, True, whole, rows)
        for cp in local + remote:
            cp.start()
        refs[-1][...] = jnp.zeros((8, 128), F32)

    thru = [pltpu.HBM(p.shape, p.dtype) for p in payloads]
    land_shapes = [(NDEV,) + p.shape if whole else p.shape if rows is None else (NDEV, rows) + p.shape[2:] for p in payloads]
    outs = pl.pallas_call(
        body, name=name,
        out_shape=(pltpu.SemaphoreType.DMA((nr * n,)), pltpu.SemaphoreType.DMA((nr * n,)), pltpu.SemaphoreType.DMA((n,)),
                   *thru, *[pltpu.HBM(sh, p.dtype) for sh, p in zip(land_shapes, payloads)], _sds((8, 128), F32)),
        in_specs=[HBM] * (2 * n),
        out_specs=(SEM, SEM, SEM, *[HBM] * (2 * n), pl.BlockSpec(memory_space=pltpu.VMEM)),
        input_output_aliases={i: 3 + i for i in range(2 * n)},
        compiler_params=pltpu.CompilerParams(has_side_effects=pltpu.SideEffectType.DATAFLOW_SIDE_EFFECTING),
    )(*[pltpu.with_memory_space_constraint(p, pltpu.HBM) for p in payloads],
      *[pltpu.with_memory_space_constraint(lax.empty(sh, p.dtype), pltpu.HBM) for sh, p in zip(land_shapes, payloads)])
    return (outs[:3], outs[3:3 + n], outs[3 + n:3 + 2 * n], whole, rows), outs[-1][0, 0]


def scatter_wait(handle, after, name):
    sems, srcs_thru, lands_thru, whole, rows = handle
    n = len(srcs_thru)
    after = list(after) if isinstance(after, (list, tuple)) else [after]

    def body(*refs):
        srcs, lands, sems_ = refs[:n], refs[n:2 * n], refs[2 * n:2 * n + 3]
        remote, local = _scatter_copies(srcs, lands, sems_, False, whole, rows)
        for cp in remote:
            cp.wait_send()
            cp.wait_recv()
        for cp in local:
            cp.wait()

    outs = pl.pallas_call(
        body, name=name, out_shape=tuple(pltpu.HBM(p.shape, p.dtype) for p in (*srcs_thru, *lands_thru)),
        in_specs=[HBM] * (2 * n) + [SEM] * 3 + [HBM] * len(after), out_specs=tuple([HBM] * (2 * n)),
        input_output_aliases={i: i for i in range(2 * n)},
        compiler_params=pltpu.CompilerParams(has_side_effects=pltpu.SideEffectType.DATAFLOW_SIDE_EFFECTING),
    )(*srcs_thru, *lands_thru, *sems, *[pltpu.with_memory_space_constraint(a, pltpu.HBM) for a in after])
    return list(outs[n:])


def _row_spec(cols, tm=TM):
    return pl.BlockSpec((tm, cols), lambda i: (i, 0))


def _full_spec(shape):
    nd = len(shape)
    return pl.BlockSpec(shape, lambda i: (0,) * nd)


def ln_mod_fwd(x, mod, sub, name):
    def body(x_ref, mod_ref, h_ref):
        n, _ = _ln_stats(x_ref[...])
        shift = mod_ref[3 * sub:3 * sub + 1, :]
        scale = mod_ref[3 * sub + 1:3 * sub + 2, :]
        h_ref[...] = (n * (1.0 + scale) + shift).astype(MXU)

    return _pallas_call(
        body, name=name, grid=(S // TM,),
        in_specs=[_row_spec(D), _full_spec((9, D))], out_specs=_row_spec(D),
        out_shape=_sds((S, D), MXU), compiler_params=_cp(32, ("arbitrary",)))(x, mod)


def res_ln_fwd(x, f, mod, sub, lng, lnb, w, name, nxt=None):
    def body(x_ref, f_ref, mod_ref, g_ref, b_ref, *rest):
        gate = mod_ref[3 * sub + 2:3 * sub + 3, :]
        r = ALPHA * x_ref[...] + (w * gate) * f_ref[...]
        n, _ = _ln_stats(r)
        xo = n * g_ref[sub:sub + 1, :] + b_ref[sub:sub + 1, :]
        rest[-1 if nxt is None else -2][...] = xo
        if nxt is not None:
            nmod_ref, h_ref = rest[0], rest[-1]
            n2, _ = _ln_stats(xo)
            s2 = nxt[1]
            h_ref[...] = (n2 * (1.0 + nmod_ref[3 * s2 + 1:3 * s2 + 2, :]) + nmod_ref[3 * s2:3 * s2 + 1, :]).astype(MXU)

    more = nxt is not None
    return _pallas_call(
        body, name=name, grid=(S // TM,),
        in_specs=[_row_spec(D), _row_spec(D), _full_spec((9, D)), _full_spec((3, D)), _full_spec((3, D))] + [_full_spec((9, D))] * more,
        out_specs=(_row_spec(D),) + (_row_spec(D),) * more, out_shape=(_sds((S, D), F32),) + (_sds((S, D), MXU),) * more,
        compiler_params=_cp(32, ("arbitrary",)))(x, f, mod, lng, lnb, *([nxt[0]] if more else []))


def res_ln_bwd(x, f, mod, sub, lng, dxo, w, name):
    def body(x_ref, f_ref, mod_ref, g_ref, dxo_ref, dxa_ref, df_ref, sums_ref):
        i = pl.program_id(0)
        gate = mod_ref[3 * sub + 2:3 * sub + 3, :]
        fv = f_ref[...]
        r = ALPHA * x_ref[...] + (w * gate) * fv
        n, rstd = _ln_stats(r)
        dxo = dxo_ref[...]
        dr = _ln_bwd(dxo * g_ref[sub:sub + 1, :], n, rstd)
        dxa_ref[...] = ALPHA * dr
        df_ref[...] = ((w * gate) * dr).astype(MXU)
        part = jnp.concatenate([
            jnp.sum(dxo * n, axis=0, keepdims=True),
            jnp.sum(dxo, axis=0, keepdims=True),
            jnp.sum(dr * fv, axis=0, keepdims=True) * w,
            jnp.zeros((5, D), F32)], axis=0)

        @pl.when(i == 0)
        def _():
            sums_ref[...] = part

        @pl.when(i > 0)
        def _():
            sums_ref[...] += part

    return _call(
        body, name=name, grid=(S // TM,),
        in_specs=[_row_spec(D), _row_spec(D), _full_spec((9, D)), _full_spec((3, D)), _row_spec(D)],
        out_specs=(_row_spec(D), _row_spec(D), _full_spec((8, D))),
        out_shape=(_sds((S, D), F32), _sds((S, D), MXU), _sds((8, D), F32)),
        cp=_cp(32, ("arbitrary",)), args=(x, f, mod, lng, dxo))[0]


def ln_mod_bwd(x, dh, mod, sub, dxa, name):
    def body(x_ref, dh_ref, mod_ref, dxa_ref, dx_ref, sums_ref):
        i = pl.program_id(0)
        scale = mod_ref[3 * sub + 1:3 * sub + 2, :]
        n, rstd = _ln_stats(x_ref[...])
        dh = dh_ref[...]
        dx_ref[...] = dxa_ref[...] + _ln_bwd(dh * (1.0 + scale), n, rstd)
        part = jnp.concatenate([
            jnp.sum(dh, axis=0, keepdims=True),
            jnp.sum(dh * n, axis=0, keepdims=True),
            jnp.zeros((6, D), F32)], axis=0)

        @pl.when(i == 0)
        def _():
            sums_ref[...] = part

        @pl.when(i > 0)
        def _():
            sums_ref[...] += part

    return _call(
        body, name=name, grid=(S // TM,),
        in_specs=[_row_spec(D), _row_spec(D), _full_spec((9, D)), _row_spec(D)],
        out_specs=(_row_spec(D), _full_spec((8, D))),
        out_shape=(_sds((S, D), F32), _sds((8, D), F32)),
        cp=_cp(32, ("arbitrary",)), args=(x, dh, mod, dxa))[0]


def ln_join_bwd(xp, fp, modp, subp, lngp, lnbp, wp, dh, mod, sub, dxa, name):
    def body(xp_ref, fp_ref, modp_ref, g_ref, b_ref, dh_ref, mod_ref, dxa_ref, dxap_ref, dfp_ref, sumsp_ref, sums_ref):
        i = pl.program_id(0)
        gate = modp_ref[3 * subp + 2:3 * subp + 3, :]
        fv = fp_ref[...]
        n, rstd = _ln_stats(ALPHA * xp_ref[...] + (wp * gate) * fv)
        gain = g_ref[subp:subp + 1, :]
        n2, rstd2 = _ln_stats(n * gain + b_ref[subp:subp + 1, :])
        dh = dh_ref[...]
        dx = dxa_ref[...] + _ln_bwd(dh * (1.0 + mod_ref[3 * sub + 1:3 * sub + 2, :]), n2, rstd2)
        dr = _ln_bwd(dx * gain, n, rstd)
        dxap_ref[...] = ALPHA * dr
        dfp_ref[...] = ((wp * gate) * dr).astype(MXU)
        partp = jnp.concatenate([
            jnp.sum(dx * n, axis=0, keepdims=True), jnp.sum(dx, axis=0, keepdims=True),
            jnp.sum(dr * fv, axis=0, keepdims=True) * wp, jnp.zeros((5, D), F32)], axis=0)
        part = jnp.concatenate([
            jnp.sum(dh, axis=0, keepdims=True), jnp.sum(dh * n2, axis=0, keepdims=True), jnp.zeros((6, D), F32)], axis=0)

        @pl.when(i == 0)
        def _():
            sumsp_ref[...] = partp
            sums_ref[...] = part

        @pl.when(i > 0)
        def _():
            sumsp_ref[...] += partp
            sums_ref[...] += part

    return _pallas_call(
        body, name=name, grid=(S // TM,),
        in_specs=[_row_spec(D), _row_spec(D), _full_spec((9, D)), _full_spec((3, D)), _full_spec((3, D)), _row_spec(D),
                  _full_spec((9, D)), _row_spec(D)],
        out_specs=(_row_spec(D), _row_spec(D), _full_spec((8, D)), _full_spec((8, D))),
        out_shape=(_sds((S, D), F32), _sds((S, D), MXU), _sds((8, D), F32), _sds((8, D), F32)),
        compiler_params=_cp(40, ("arbitrary",)))(xp, fp, modp, lngp, lnbp, dh, mod, dxa)


def loss_fwd_bwd(y, target, name):
    def body(y_ref, t_ref, l_ref, dy_ref):
        i = pl.program_id(0)
        e = y_ref[...] - t_ref[...]
        dy_ref[...] = e * (1.0 / D)
        part = jnp.zeros((8, 128), F32) + (0.5 / D) * jnp.sum(e * e)

        @pl.when(i == 0)
        def _():
            l_ref[...] = part

        @pl.when(i > 0)
        def _():
            l_ref[...] += part

    return _pallas_call(
        body, name=name, grid=(S // TM,),
        in_specs=[_row_spec(D), _row_spec(D)], out_specs=(_full_spec((8, 128)), _row_spec(D)),
        out_shape=(_sds((8, 128), F32), _sds((S, D), F32)),
        compiler_params=_cp(32, ("arbitrary",)))(y, target)


HB = 2 * FBP
NHB = NDEV * FBP // HB
TMB = 1024


def _wrows(buffers=2):
    return pl.BlockSpec((HB, D), lambda j, i: (j, 0), pipeline_mode=pl.Buffered(buffers))


def _resident(shape):
    return pl.BlockSpec(shape, lambda j, i: (0, 0), pipeline_mode=pl.Buffered(1))


def ffn_fwd(h, wgt, wut, wd, name, ride=None):
    def body(h_ref, wg_ref, wu_ref, wd_ref, g_ref, u_ref, f_ref):
        j, i = pl.program_id(0), pl.program_id(1)
        hv = h_ref[...]
        g = _mm_nt(hv, wg_ref[...])
        u = _mm_nt(hv, wu_ref[...])
        g_ref[...] = g.astype(MXU)
        u_ref[...] = u.astype(MXU)
        a = g * jax.nn.sigmoid(g) * u
        part = _mm(a, wd_ref[...])
        rows = pl.ds(pl.multiple_of(i * TMB, TMB), TMB)

        @pl.when(j == 0)
        def _():
            f_ref[rows, :] = part

        @pl.when(j > 0)
        def _():
            f_ref[rows, :] += part

    gu = pl.BlockSpec((TMB, HB), lambda j, i: (i, j))
    return _call(
        body, name=name, grid=(NHB, S // TMB),
        in_specs=[pl.BlockSpec((TMB, D), lambda j, i: (i, 0)), _wrows(), _wrows(), _wrows()],
        out_specs=(gu, gu, _resident((S, D))),
        out_shape=(_sds((S, NDEV * FBP), MXU), _sds((S, NDEV * FBP), MXU), _sds((S, D), F32)),
        cp=_cp(52, ("arbitrary", "arbitrary")), args=(h, wgt, wut, wd), ride=ride)


def ffn_bwd(df, h, g, u, wgt, wut, wd, name):
    ni = S // TMB

    def body(df_ref, h_ref, g_ref, u_ref, wg_ref, wu_ref, wd_ref, dwg_ref, dwu_ref, dwd_ref, dh_ref,
             ag_ref, au_ref, ad_ref):
        j, i = pl.program_id(0), pl.program_id(1)
        dfv, hv = df_ref[...], h_ref[...]
        gv, uv = g_ref[...].astype(F32), u_ref[...].astype(F32)
        da = _mm_nt(dfv, wd_ref[...])
        sg = jax.nn.sigmoid(gv)
        silu = gv * sg
        du = da * silu
        dg = da * uv * (sg * (1.0 + gv * (1.0 - sg)))
        p_d = _mm_tn(silu * uv, dfv)
        p_g = _mm_tn(dg, hv)
        p_u = _mm_tn(du, hv)

        @pl.when(i == 0)
        def _():
            ad_ref[...] = p_d
            ag_ref[...] = p_g
            au_ref[...] = p_u

        @pl.when(i > 0)
        def _():
            ad_ref[...] += p_d
            ag_ref[...] += p_g
            au_ref[...] += p_u

        @pl.when(i == ni - 1)
        def _():
            dwd_ref[...] = ad_ref[...].astype(BF16)
            dwg_ref[...] = ag_ref[...].astype(BF16)
            dwu_ref[...] = au_ref[...].astype(BF16)

        part = _mm(dg, wg_ref[...]) + _mm(du, wu_ref[...])
        rows = pl.ds(pl.multiple_of(i * TMB, TMB), TMB)

        @pl.when(j == 0)
        def _():
            dh_ref[rows, :] = part

        @pl.when(j > 0)
        def _():
            dh_ref[rows, :] += part

    gu = pl.BlockSpec((TMB, HB), lambda j, i: (i, j))
    rowt = pl.BlockSpec((TMB, D), lambda j, i: (i, 0))
    return _call(
        body, name=name, grid=(NHB, ni),
        in_specs=[rowt, rowt, gu, gu, _wrows(1), _wrows(1), _wrows(1)],
        out_specs=(_wrows(1), _wrows(1), _wrows(1), _resident((S, D))),
        out_shape=(_sds((NDEV * FBP, D), BF16), _sds((NDEV * FBP, D), BF16), _sds((NDEV * FBP, D), BF16), _sds((S, D), F32)),
        scratch=[pltpu.VMEM((HB, D), F32), pltpu.VMEM((HB, D), F32), pltpu.VMEM((HB, D), F32)],
        cp=_cp(60, ("arbitrary", "arbitrary")), args=(df, h, g, u, wgt, wut, wd))[0]


def win_fwd(h, win, name, ride=None):
    def body(h_ref, w_ref, z_ref):
        hv = h_ref[...]
        for j in range(NDEV):
            z_ref[:, 256 * j:256 * (j + 1)] = _mm(hv, w_ref[j])

    return _call(
        body, name=name, grid=(S // TMM,),
        in_specs=[_row_spec(D, TMM), _full_spec((NDEV, D, 256))],
        out_specs=[_row_spec(D_IN, TMM)], out_shape=[_sds((S, D_IN), F32)],
        cp=_cp(40, ("arbitrary",)), args=(h, win), ride=ride)


def win_bwd(dparts, h, win, name):
    ni = S // TMM

    def body(dq_ref, dk_ref, dv_ref, dus_ref, dup_ref, h_ref, w_ref, dh_ref, dw_ref, acc_ref):
        i = pl.program_id(0)
        hv = h_ref[...]
        cols = [dq_ref[:, 0:256], dq_ref[:, 256:512], dk_ref[:, 0:256], dk_ref[:, 256:512],
                dv_ref[:, 0:256], dv_ref[:, 256:512], dus_ref[...], dup_ref[...]]
        dh = jnp.zeros((TMM, D), F32)
        for j in range(NDEV):
            dz = cols[j].astype(MXU)
            dh = dh + _mm_nt(dz, w_ref[j])
            p = _mm_tn(hv, dz)

            @pl.when(i == 0)
            def _():
                acc_ref[j] = p

            @pl.when(i > 0)
            def _():
                acc_ref[j] += p

        dh_ref[...] = dh

        @pl.when(i == ni - 1)
        def _():
            dw_ref[...] = acc_ref[...].astype(BF16)

    return _call(
        body, name=name, grid=(ni,),
        in_specs=[_row_spec(512, TMM), _row_spec(512, TMM), _row_spec(512, TMM), _row_spec(256, TMM), _row_spec(256, TMM),
                  _row_spec(D, TMM), _full_spec((NDEV, D, 256))],
        out_specs=(_row_spec(D, TMM), _full_spec((NDEV, D, 256))),
        out_shape=(_sds((S, D), F32), _sds((NDEV, D, 256), BF16)),
        scratch=[pltpu.VMEM((NDEV, D, 256), F32)],
        cp=_cp(48, ("arbitrary",)), args=(*dparts, h, win))[0]


def wout_fwd(ya, ys, yp, wout, name, ride=None):
    def body(ya_ref, ys_ref, yp_ref, w_ref, o_ref):
        w = w_ref[...].reshape(D, D)
        o_ref[...] = _mm(ya_ref[...], w[0:512]) + _mm(ys_ref[...], w[512:768]) + _mm(yp_ref[...], w[768:1024])

    return _call(
        body, name=name, grid=(S // TMM,),
        in_specs=[_row_spec(512, TMM), _row_spec(256, TMM), _row_spec(256, TMM), _full_spec((NDEV, 128, D))],
        out_specs=[_row_spec(D, TMM)], out_shape=[_sds((S, D), F32)],
        cp=_cp(40, ("arbitrary",)), args=(ya, ys, yp, wout), ride=ride)


def wout_bwd(do, ya, ys, yp, wout, name):
    ni = S // TMM

    def body(do_ref, ya_ref, ys_ref, yp_ref, w_ref, dya_ref, dys_ref, dyp_ref, dw_ref, acc_ref):
        i = pl.program_id(0)
        w = w_ref[...].reshape(D, D)
        dov = do_ref[...]
        dya_ref[...] = _mm_nt(dov, w[0:512])
        dys_ref[...] = _mm_nt(dov, w[512:768])
        dyp_ref[...] = _mm_nt(dov, w[768:1024])
        parts = [(0, 512, _mm_tn(ya_ref[...], dov)), (512, 768, _mm_tn(ys_ref[...], dov)),
                 (768, 1024, _mm_tn(yp_ref[...], dov))]
        for lo, hi, p in parts:
            @pl.when(i == 0)
            def _():
                acc_ref[lo:hi, :] = p

            @pl.when(i > 0)
            def _():
                acc_ref[lo:hi, :] += p

        @pl.when(i == ni - 1)
        def _():
            dw_ref[...] = acc_ref[...].astype(BF16).reshape(NDEV, 128, D)

    return _call(
        body, name=name, grid=(ni,),
        in_specs=[_row_spec(D, TMM), _row_spec(512, TMM), _row_spec(256, TMM), _row_spec(256, TMM),
                  _full_spec((NDEV, 128, D))],
        out_specs=(_row_spec(512, TMM), _row_spec(256, TMM), _row_spec(256, TMM), _full_spec((NDEV, 128, D))),
        out_shape=(_sds((S, 512), F32), _sds((S, 256), F32), _sds((S, 256), F32), _sds((NDEV, 128, D), BF16)),
        scratch=[pltpu.VMEM((D, D), F32)],
        cp=_cp(40, ("arbitrary",)), args=(do, ya, ys, yp, wout))[0]


def _t5_bucket(dist):
    max_exact = N_BUCKETS // 2
    d = np.maximum(dist, 1).astype(np.float32)
    large = max_exact + (np.log(d / max_exact) / math.log(MAX_DISTANCE / max_exact)
                         * (N_BUCKETS - max_exact)).astype(np.int32)
    large = np.minimum(large, N_BUCKETS - 1)
    return np.where(dist < max_exact, dist, large).astype(np.int32)


def _att_static():
    i = np.arange(QB)[:, None]
    j = np.arange(2 * QB)[None, :]
    r = i + QB - j
    buckets, bands = [], []
    for window, dil in PATTERNS:
        bands.append((r >= 0) & (r <= window // dil))
        buckets.append(_t5_bucket(np.clip(r, 0, None) * dil))
    return np.stack(buckets), np.stack(bands), np.broadcast_to(j >= QB, (QB, 2 * QB))


def att_bias(rel_bias):
    m = np.arange(2 * QB)
    rows = []
    for window, dil in PATTERNS:
        r = QB - m
        ok = (r >= 0) & (r <= window // dil)
        b = rel_bias[_t5_bucket(np.clip(r, 0, None) * dil)]
        rows.append(jnp.where(ok[:, None], b, NEG).T)
    return jnp.broadcast_to(jnp.stack(rows)[:, :, None, :], (3, N_HEADS, 8, 2 * QB))


def _bias_tiles(t_ref, tiles):
    col = lax.broadcasted_iota(jnp.int32, (QB, 2 * QB), 1)
    for p in range(3):
        for hh in range(2):
            t = pltpu.roll(jnp.broadcast_to(t_ref[p, hh, 0:1, :], (QB, 2 * QB)), 0, 1, stride=1, stride_axis=0)
            tiles[p, hh, 0] = t
            tiles[p, hh, 1] = jnp.where(col >= QB, t, NEG)


def _permute_in(dst_ref, src_ref, d, scale=None, pad=QB):
    L = S // d
    for r in range(d):
        v = src_ref[pl.ds(r, L, stride=d), :] if d > 1 else src_ref[...]
        if scale is not None:
            v = v * scale
        dst_ref[pad + r * L:pad + (r + 1) * L, :] = v.astype(dst_ref.dtype)


def att_fwd(z, bias, name, ride=None):
    def body(q_ref, k_ref, v_ref, t_ref, y_ref, l_ref, qs, ks, vs, o_perm, l_perm, o_nat, l_nat, b_ref):
        _bias_tiles(t_ref, b_ref)
        zero_pad = jnp.zeros((QB, 128), MXU)
        ks[0:QB, :] = zero_pad
        vs[0:QB, :] = zero_pad
        lane = lax.broadcasted_iota(jnp.int32, (QB, 128), 1)
        for p, (_, d) in enumerate(PATTERNS):
            L = S // d
            nb = L // QB
            _permute_in(qs, q_ref, d, scale=0.125, pad=0)
            _permute_in(ks, k_ref, d)
            _permute_in(vs, v_ref, d)

            def blk(b, carry):
                r0 = pl.multiple_of(b * QB, QB)
                q = qs[pl.ds(r0, QB), :]
                kb = ks[pl.ds(r0, 2 * QB), :]
                vb = vs[pl.ds(r0, 2 * QB), :]
                first = ((b % nb) == 0).astype(jnp.int32)
                res = []
                for hh in range(2):
                    sel = (lane < 64) if hh == 0 else (lane >= 64)
                    qm = jnp.where(sel, q, jnp.zeros_like(q))
                    s = _mm_nt(qm, kb) + b_ref[p, hh, first]
                    m = jnp.max(s, axis=1, keepdims=True)
                    pe = jnp.exp(s - m)
                    den = jnp.sum(pe, axis=1, keepdims=True)
                    res.append((_mm(pe, vb) / den, m + jnp.log(den)))
                o_perm[pl.ds(r0, QB), :] = jnp.where(lane < 64, res[0][0], res[1][0])
                l_perm[pl.ds(r0, QB), :] = jnp.where(lane < 64, res[0][1], res[1][1])
                return carry

            lax.fori_loop(0, S // QB, blk, 0, unroll=8)
            for r in range(d):
                if d > 1:
                    o_nat[p, pl.ds(r, L, stride=d), :] = o_perm[r * L:(r + 1) * L, :]
                    l_nat[p, pl.ds(r, L, stride=d), :] = l_perm[r * L:(r + 1) * L, :]
                else:
                    o_nat[p] = o_perm[...]
                    l_nat[p] = l_perm[...]
        l0, l1, l2 = l_nat[0], l_nat[1], l_nat[2]
        m = jnp.maximum(jnp.maximum(l0, l1), l2)
        e0, e1, e2 = jnp.exp(l0 - m), jnp.exp(l1 - m), jnp.exp(l2 - m)
        den = e0 + e1 + e2
        y_ref[...] = (e0 * o_nat[0] + e1 * o_nat[1] + e2 * o_nat[2]) / den
        l_ref[...] = m + jnp.log(den)

    col = lambda c0: pl.BlockSpec((S, 128), lambda hp: (0, c0 + hp))
    return _call(
        body, name=name, grid=(N_HEADS // 2,),
        in_specs=[col(0), col(4), col(8), pl.BlockSpec((3, 2, 8, 2 * QB), lambda hp: (0, hp, 0, 0))],
        out_specs=(col(0), col(0)),
        out_shape=(_sds((S, D_ATT), F32), _sds((S, D_ATT), F32)),
        scratch=[pltpu.VMEM((S, 128), MXU), pltpu.VMEM((S + QB, 128), MXU), pltpu.VMEM((S + QB, 128), MXU),
                 pltpu.VMEM((S, 128), F32), pltpu.VMEM((S, 128), F32),
                 pltpu.VMEM((3, S, 128), F32), pltpu.VMEM((3, S, 128), F32),
                 pltpu.VMEM((3, 2, 2, QB, 2 * QB), F32)],
        cp=_cp(40, ("arbitrary",)), args=(z, z, z, bias), ride=ride)


def att_bwd(z, bias, y, lse, dy, name):
    def body(q_ref, k_ref, v_ref, t_ref, y_ref, l_ref, dy_ref, dq_ref, dk_ref, dv_ref, db_ref,
             qs, ks, vs, dys, ls, dds, dn_nat, dq_perm, dk_perm, dv_perm, b_ref):
        _bias_tiles(t_ref, b_ref)
        zero_pad = jnp.zeros((QB, 128), MXU)
        ks[0:QB, :] = zero_pad
        vs[0:QB, :] = zero_pad
        lane = lax.broadcasted_iota(jnp.int32, (QB, 128), 1)
        lane_s = lax.broadcasted_iota(jnp.int32, (S, 128), 1)
        t = dy_ref[...] * y_ref[...]
        sa = jnp.sum(jnp.where(lane_s < 64, t, 0.0), axis=1, keepdims=True)
        sb = jnp.sum(jnp.where(lane_s >= 64, t, 0.0), axis=1, keepdims=True)
        dn_nat[...] = jnp.where(lane_s < 64, sa, sb)
        dq_ref[...] = jnp.zeros((S, 128), F32)
        dk_ref[...] = jnp.zeros((S, 128), F32)
        dv_ref[...] = jnp.zeros((S, 128), F32)
        db_ref[...] = jnp.zeros((3, 2, QB, 2 * QB), F32)
        for p, (_, d) in enumerate(PATTERNS):
            L = S // d
            nb = L // QB
            _permute_in(qs, q_ref, d, scale=0.125, pad=0)
            _permute_in(ks, k_ref, d)
            _permute_in(vs, v_ref, d)
            _permute_in(dys, dy_ref, d, pad=0)
            _permute_in(ls, l_ref, d, pad=0)
            _permute_in(dds, dn_nat, d, pad=0)
            dk_perm[...] = jnp.zeros((S + QB, 128), F32)
            dv_perm[...] = jnp.zeros((S + QB, 128), F32)

            def blk(b, carry):
                r0 = pl.multiple_of(b * QB, QB)
                q = qs[pl.ds(r0, QB), :]
                kb = ks[pl.ds(r0, 2 * QB), :]
                vb = vs[pl.ds(r0, 2 * QB), :]
                dyb = dys[pl.ds(r0, QB), :]
                lb = ls[pl.ds(r0, QB), :]
                db = dds[pl.ds(r0, QB), :]
                first = ((b % nb) == 0).astype(jnp.int32)
                lane2 = jnp.concatenate([lane, lane], axis=0)
                own = (lane2 >> 6) == (lax.broadcasted_iota(jnp.int32, (2 * QB, 128), 0) >> 7)
                qm = jnp.where(own, jnp.concatenate([q, q], axis=0), jnp.zeros((2 * QB, 128), q.dtype))
                dym = jnp.where(own, jnp.concatenate([dyb, dyb], axis=0), jnp.zeros((2 * QB, 128), dyb.dtype))
                wide = lambda t: jnp.concatenate([jnp.broadcast_to(t[:, 0:1], (QB, 2 * QB)), jnp.broadcast_to(t[:, 64:65], (QB, 2 * QB))], axis=0)
                lse2, dd2 = wide(lb), wide(db)
                bias2 = jnp.concatenate([b_ref[p, 0, first], b_ref[p, 1, first]], axis=0)
                pr = jnp.exp(_mm_nt(qm, kb) + bias2 - lse2)
                ds = pr * (_mm_nt(dym, vb) - dd2)
                db_ref[p, 0] += ds[0:QB]
                db_ref[p, 1] += ds[QB:2 * QB]
                dq2 = _mm(ds, kb)
                dqs = [dq2[0:QB], dq2[QB:2 * QB]]
                dkb = _mm_tn(ds, qm)
                dvb = _mm_tn(pr, dym)
                dq_perm[pl.ds(r0, QB), :] = jnp.where(lane < 64, dqs[0], dqs[1])
                dk_perm[pl.ds(r0, 2 * QB), :] += dkb
                dv_perm[pl.ds(r0, 2 * QB), :] += dvb
                return carry

            lax.fori_loop(0, S // QB, blk, 0, unroll=4)
            for r in range(d):
                idx = pl.ds(r, L, stride=d) if d > 1 else pl.ds(0, S)
                dq_ref[idx, :] += dq_perm[r * L:(r + 1) * L, :] * 0.125
                dk_ref[idx, :] += dk_perm[QB + r * L:QB + (r + 1) * L, :]
                dv_ref[idx, :] += dv_perm[QB + r * L:QB + (r + 1) * L, :]

    col = lambda c0: pl.BlockSpec((S, 128), lambda hp: (0, c0 + hp))
    bspec = pl.BlockSpec((3, 2, 8, 2 * QB), lambda hp: (0, hp, 0, 0))
    return _call(
        body, name=name, grid=(N_HEADS // 2,),
        in_specs=[col(0), col(4), col(8), bspec, col(0), col(0), col(0)],
        out_specs=(col(0), col(0), col(0), pl.BlockSpec((3, 2, QB, 2 * QB), lambda hp: (0, hp, 0, 0))),
        out_shape=(_sds((S, D_ATT), F32), _sds((S, D_ATT), F32), _sds((S, D_ATT), F32),
                   _sds((3, N_HEADS, QB, 2 * QB), F32)),
        scratch=[pltpu.VMEM((S, 128), MXU), pltpu.VMEM((S + QB, 128), MXU), pltpu.VMEM((S + QB, 128), MXU),
                 pltpu.VMEM((S, 128), MXU), pltpu.VMEM((S, 128), F32), pltpu.VMEM((S, 128), F32),
                 pltpu.VMEM((S, 128), F32), pltpu.VMEM((S, 128), F32),
                 pltpu.VMEM((S + QB, 128), F32), pltpu.VMEM((S + QB, 128), F32),
                 pltpu.VMEM((3, 2, 2, QB, 2 * QB), F32)],
        cp=_cp(48, ("arbitrary",)), args=(z, z, z, bias, y, lse, dy))[0]


def relbias_grad(dbiases):
    bucket, band, _ = _att_static()
    onehot = (bucket[:, None] == np.arange(N_BUCKETS)[None, :, None, None]) & band[:, None]
    onehot = jnp.asarray(onehot.reshape(3, N_BUCKETS, QB * 2 * QB), BF16)

    def body(db0_ref, db1_ref, oh_ref, o_ref):
        acc = jnp.zeros((N_HEADS, N_BUCKETS), F32)
        for p in range(3):
            acc = acc + lax.dot_general(db0_ref[p] + db1_ref[p], oh_ref[p].astype(F32), (((1,), (1,)), ((), ())),
                                        preferred_element_type=F32, precision=lax.Precision.HIGHEST)
        o_ref[...] = acc

    vm = pl.BlockSpec(memory_space=pltpu.VMEM)
    out = _pallas_call(body, name="relbias_grad", in_specs=[vm, vm, vm], out_specs=vm,
                         out_shape=_sds((N_HEADS, N_BUCKETS), F32), compiler_params=_cp(40))(
        *[d.reshape(3, N_HEADS, QB * 2 * QB) for d in dbiases], onehot)
    return out.T


def _panel(t_ref, ri, j):
    return t_ref[ri, pl.ds(j, S, stride=8), :]


def _gelu(x):
    c = math.sqrt(2.0 / math.pi)
    th = jnp.tanh(c * (x + 0.044715 * x * x * x))
    return 0.5 * x * (1.0 + th), th


def ssm_fwd(z, a, bre, bim, cre, cim, dsk, gluw, glub, name, ride=None):
    def body(u_ref, a_ref, bre_ref, bim_ref, cre_ref, cim_ref, d_ref, gw_ref, gb_ref, y_ref, yp_ref, st_hbm, st_ref):
        u = u_ref[...]
        for j in range(8):
            st_ref[0, pl.ds(j, S, stride=8), :] = _mm(u, bre_ref[:, 128 * j:128 * (j + 1)])
            st_ref[1, pl.ds(j, S, stride=8), :] = _mm(u, bim_ref[:, 128 * j:128 * (j + 1)])
        ar, ai = a_ref[0], a_ref[1]

        def step(t, c):
            re, im = c
            i = pl.multiple_of(t * 8, 8)
            nre = ar * re - ai * im + st_ref[0, pl.ds(i, 8), :]
            nim = ar * im + ai * re + st_ref[1, pl.ds(i, 8), :]
            st_ref[0, pl.ds(i, 8), :] = nre
            st_ref[1, pl.ds(i, 8), :] = nim
            return nre, nim

        zero = jnp.zeros((8, 128), F32)
        lax.fori_loop(0, S, step, (zero, zero), unroll=8)
        y = d_ref[...] * u
        for j in range(8):
            y = y + _mm(_panel(st_ref, 0, j), cre_ref[128 * j:128 * (j + 1), :])
            y = y - _mm(_panel(st_ref, 1, j), cim_ref[128 * j:128 * (j + 1), :])
        pltpu.sync_copy(st_ref, st_hbm)
        yp_ref[...] = y
        gl, _ = _gelu(y)
        tt = _mm(gl, gw_ref[...].reshape(D_SSM, D_SSM)) + gb_ref[...]
        y_ref[...] = y * jax.nn.sigmoid(tt)

    vm = lambda shape: pl.BlockSpec(shape, lambda i: (0,) * len(shape))
    return _call(
        body, name=name, grid=(1,),
        in_specs=[pl.BlockSpec((S, 256), lambda i: (0, 6)), vm((2, 8, 128)), vm((256, 1024)), vm((256, 1024)),
                  vm((1024, 256)), vm((1024, 256)), vm((1, 256)),
                  vm((NDEV, 32, 256)), vm((1, 256))],
        out_specs=(vm((S, 256)), vm((S, 256)), pl.BlockSpec(memory_space=pl.ANY)),
        out_shape=(_sds((S, 256), F32), _sds((S, 256), F32), _sds((2, S * 8, 128), F32)),
        scratch=[pltpu.VMEM((2, S * 8, 128), F32)],
        cp=_cp(40, ("arbitrary",)), args=(z, a, bre, bim, cre, cim, dsk, gluw, glub), ride=ride)


def ssm_bwd(dy, z, ypre, st, a, bre, bim, cre, cim, dsk, gluw, glub, name):
    def body(dy_ref, u_ref, yp_ref, st_hbm, a_ref, bre_ref, bim_ref, cre_ref, cim_ref, d_ref, gw_ref, gb_ref,
             du_ref, dbre_ref, dbim_ref, dcre_ref, dcim_ref, da_ref, dd_ref, dgw_ref, dgb_ref, g_ref, st_ref):
        pltpu.sync_copy(st_hbm, st_ref)
        u = u_ref[...]
        y = yp_ref[...]
        dout = dy_ref[...]
        gw = gw_ref[...].reshape(D_SSM, D_SSM)
        gl, th = _gelu(y)
        sig = jax.nn.sigmoid(_mm(gl, gw) + gb_ref[...])
        dt = dout * y * sig * (1.0 - sig)
        dgw_ref[...] = _mm_tn(gl, dt)
        dgb_ref[...] = jnp.sum(dt, axis=0, keepdims=True)
        c = math.sqrt(2.0 / math.pi)
        dgelu = 0.5 * (1.0 + th) + 0.5 * y * (1.0 - th * th) * c * (1.0 + 3.0 * 0.044715 * y * y)
        dyv = dout * sig + _mm_nt(dt, gw) * dgelu
        dd_ref[...] = jnp.sum(dyv * u, axis=0, keepdims=True)
        for j in range(8):
            rows = slice(128 * j, 128 * (j + 1))
            g_ref[0, pl.ds(j, S, stride=8), :] = _mm_nt(dyv, cre_ref[rows, :])
            g_ref[1, pl.ds(j, S, stride=8), :] = -_mm_nt(dyv, cim_ref[rows, :])
            dcre_ref[rows, :] = _mm_tn(_panel(st_ref, 0, j), dyv)
            dcim_ref[rows, :] = -_mm_tn(_panel(st_ref, 1, j), dyv)
        ar, ai = a_ref[0], a_ref[1]

        def step(k, c4):
            gre, gim, dar, dai = c4
            i = pl.multiple_of((S - 1 - k) * 8, 8)
            nre = g_ref[0, pl.ds(i, 8), :] + ar * gre + ai * gim
            nim = g_ref[1, pl.ds(i, 8), :] + ar * gim - ai * gre
            g_ref[0, pl.ds(i, 8), :] = nre
            g_ref[1, pl.ds(i, 8), :] = nim
            sre = st_ref[0, pl.ds(i - 8, 8), :]
            sim = st_ref[1, pl.ds(i - 8, 8), :]
            return nre, nim, dar + nre * sre + nim * sim, dai + nim * sre - nre * sim

        zero = jnp.zeros((8, 128), F32)
        gre, gim, dar, dai = lax.fori_loop(0, S - 1, step, (zero, zero, zero, zero), unroll=8)
        g_ref[0, 0:8, :] = g_ref[0, 0:8, :] + ar * gre + ai * gim
        g_ref[1, 0:8, :] = g_ref[1, 0:8, :] + ar * gim - ai * gre
        da_ref[0] = dar
        da_ref[1] = dai
        du = dyv * d_ref[...]
        for j in range(8):
            cols = slice(128 * j, 128 * (j + 1))
            gr, gi = _panel(g_ref, 0, j), _panel(g_ref, 1, j)
            dbre_ref[:, cols] = _mm_tn(u, gr)
            dbim_ref[:, cols] = _mm_tn(u, gi)
            du = du + _mm_nt(gr, bre_ref[:, cols]) + _mm_nt(gi, bim_ref[:, cols])
        du_ref[...] = du

    vm = lambda shape: pl.BlockSpec(shape, lambda i: (0,) * len(shape))
    return _call(
        body, name=name, grid=(1,),
        in_specs=[vm((S, 256)), pl.BlockSpec((S, 256), lambda i: (0, 6)), vm((S, 256)), pl.BlockSpec(memory_space=pl.ANY),
                  vm((2, 8, 128)), vm((256, 1024)), vm((256, 1024)), vm((1024, 256)), vm((1024, 256)), vm((1, 256)),
                  vm((NDEV, 32, 256)), vm((1, 256))],
        out_specs=(vm((S, 256)), vm((256, 1024)), vm((256, 1024)), vm((1024, 256)), vm((1024, 256)),
                   vm((2, 8, 128)), vm((1, 256)), vm((256, 256)), vm((1, 256))),
        out_shape=(_sds((S, 256), F32), _sds((256, 1024), F32), _sds((256, 1024), F32), _sds((1024, 256), F32),
                   _sds((1024, 256), F32), _sds((2, 8, 128), F32), _sds((1, 256), F32), _sds((256, 256), F32),
                   _sds((1, 256), F32)),
        scratch=[pltpu.VMEM((2, S * 8, 128), F32), pltpu.VMEM((2, S * 8, 128), F32)],
        cp=_cp(56, ("arbitrary",)), args=(dy, z, ypre, st, a, bre, bim, cre, cim, dsk, gluw, glub))[0]


def _ssm_discretise(a_re, a_im, log_dt, b_re, b_im):
    dt = jnp.exp(log_dt)[:, None]
    er = jnp.exp(a_re * dt)
    abr, abi = er * jnp.cos(a_im * dt), er * jnp.sin(a_im * dt)
    den = a_re * a_re + a_im * a_im
    fr = ((abr - 1.0) * a_re + abi * a_im) / den
    fi = (abi * a_re - (abr - 1.0) * a_im) / den
    bbr = fr[:, :, None] * b_re - fi[:, :, None] * b_im
    bbi = fr[:, :, None] * b_im + fi[:, :, None] * b_re
    return abr, abi, bbr, bbi


def _blockdiag(t):
    g, r, c = t.shape
    eye = jnp.eye(g, dtype=t.dtype)
    return (t[:, :, None, :] * eye[:, None, :, None]).reshape(g * r, g * c)


def _blockdiag_take(m, r, c):
    g = m.shape[0] // r
    idx = jnp.arange(g)
    return m.reshape(g, r, g, c)[idx, :, idx, :]


PAD = 16


def _pool_lane_select(vals):
    lane = lax.broadcasted_iota(jnp.int32, vals[0].shape, 1)
    out = vals[3]
    for g in (2, 1, 0):
        out = jnp.where(lane < 64 * (g + 1), vals[g], out)
    return out


def _pool_counts():
    row = lax.broadcasted_iota(jnp.int32, (S, D_POOL), 0).astype(F32) + 1.0
    return _pool_lane_select([jnp.minimum(row, float(w)) for w in POOL_WINDOWS])


def _pooled(u, sa, sb):
    sums = []
    cur = u
    bufs = (sa, sb)
    for k, sh in enumerate((1, 2, 4, 8)):
        buf = bufs[k % 2]
        buf[PAD:PAD + S, :] = cur
        cur = cur + buf[PAD - sh:PAD - sh + S, :]
        sums.append(cur)
    return _pool_lane_select(sums) / _pool_counts() - u


def pool_fwd(z, pw, psc, name):
    def body(u_ref, w_ref, s_ref, y_ref, sa, sb):
        for buf in (sa, sb):
            buf[0:PAD, :] = jnp.zeros((PAD, D_POOL), F32)
        pooled = _pooled(u_ref[...], sa, sb)
        y_ref[...] = _mm(pooled, w_ref[...]) * s_ref[...]

    vm = lambda shape: pl.BlockSpec(shape, lambda i: (0,) * len(shape))
    return _pallas_call(
        body, name=name, grid=(1,),
        in_specs=[pl.BlockSpec((S, 256), lambda i: (0, 7)), vm((256, 256)), vm((1, 256))],
        out_specs=vm((S, 256)), out_shape=_sds((S, 256), F32),
        scratch_shapes=[pltpu.VMEM((S + 2 * PAD, D_POOL), F32)] * 2,
        compiler_params=_cp(40, ("arbitrary",)))(z, pw, psc)


def pool_bwd(dy, z, pw, psc, name):
    def body(dy_ref, u_ref, w_ref, s_ref, du_ref, dw_ref, ds_ref, sa, sb):
        for buf in (sa, sb):
            buf[0:PAD, :] = jnp.zeros((PAD, D_POOL), F32)
            buf[PAD + S:PAD + S + PAD, :] = jnp.zeros((PAD, D_POOL), F32)
        pooled = _pooled(u_ref[...], sa, sb)
        dyv = dy_ref[...]
        w = w_ref[...]
        ds_ref[...] = jnp.sum(dyv * _mm(pooled, w), axis=0, keepdims=True)
        dyl = dyv * s_ref[...]
        dw_ref[...] = _mm_tn(pooled, dyl)
        dpool = _mm_nt(dyl, w)
        cur = dpool / _pool_counts()
        sums = []
        bufs = (sa, sb)
        for k, sh in enumerate((1, 2, 4, 8)):
            buf = bufs[k % 2]
            buf[PAD:PAD + S, :] = cur
            cur = cur + buf[PAD + sh:PAD + sh + S, :]
            sums.append(cur)
        du_ref[...] = _pool_lane_select(sums) - dpool

    vm = lambda shape: pl.BlockSpec(shape, lambda i: (0,) * len(shape))
    return _pallas_call(
        body, name=name, grid=(1,),
        in_specs=[vm((S, 256)), pl.BlockSpec((S, 256), lambda i: (0, 7)), vm((256, 256)), vm((1, 256))],
        out_specs=(vm((S, 256)), vm((256, 256)), vm((1, 256))),
        out_shape=(_sds((S, 256), F32), _sds((256, 256), F32), _sds((1, 256), F32)),
        scratch_shapes=[pltpu.VMEM((S + 2 * PAD, D_POOL), F32)] * 2,
        compiler_params=_cp(40, ("arbitrary",)))(dy, z, pw, psc)


def ada_fwd(c_all, ada_w, ada_b_cols):
    def body(c_ref, w_ref, b_ref, o_ref):
        c = c_ref[...]
        cond = c * jax.nn.sigmoid(c)
        o_ref[...] = jnp.dot(cond, w_ref[...], preferred_element_type=F32, precision=lax.Precision.HIGHEST) + b_ref[...]

    return _pallas_call(
        body, name="ada_fwd", grid=(DEPTH,),
        in_specs=[pl.BlockSpec((NDEV, D), lambda l: (0, 0)), pl.BlockSpec((None, D, 1152), lambda l: (l, 0, 0)),
                  pl.BlockSpec((None, 1, 1152), lambda l: (l, 0, 0))],
        out_specs=pl.BlockSpec((None, NDEV, 1152), lambda l: (l, 0, 0)), out_shape=_sds((DEPTH, NDEV, 1152), F32),
        compiler_params=_cp(40, ("arbitrary",)))(c_all, ada_w, ada_b_cols)


def ada_bwd(c_all, dmod_cols):
    def body(c_ref, dm_ref, o_ref):
        c = c_ref[...]
        cond = c * jax.nn.sigmoid(c)
        o_ref[...] = lax.dot_general(cond, dm_ref[...], (((0,), (0,)), ((), ())), preferred_element_type=F32,
                                     precision=lax.Precision.HIGHEST)

    return _pallas_call(
        body, name="ada_bwd", grid=(DEPTH,),
        in_specs=[pl.BlockSpec((NDEV, D), lambda l: (0, 0)), pl.BlockSpec((None, NDEV, 1152), lambda l: (l, 0, 0))],
        out_specs=pl.BlockSpec((None, D, 1152), lambda l: (l, 0, 0)), out_shape=_sds((DEPTH, D, 1152), F32),
        compiler_params=_cp(40, ("arbitrary",)))(c_all, dmod_cols)


def _adamw(w, g, m, v):
    m2 = B1 * m + (1.0 - B1) * g
    v2 = B2 * v + (1.0 - B2) * (g * g)
    m_hat = m2 / (1.0 - B1 ** STEP)
    v_hat = v2 / (1.0 - B2 ** STEP)
    return -LR * (m_hat / (jnp.sqrt(v_hat) + EPS) + WD * w), m2, v2


def _sum8(ref):
    g = ref[0].astype(F32)
    for s in range(1, ref.shape[0]):
        g = g + ref[s].astype(F32)
    return g


def adam_rs(recv, w, m, v, tr, name):
    lead, (r, cdim) = w.shape[:-2], w.shape[-2:]
    cp = recv.shape[-1]
    nl = len(lead)

    def body(rc_ref, w_ref, m_ref, v_ref, g_ref, d_ref, m2_ref, v2_ref):
        g = _sum8(rc_ref)[:, :cdim]
        g_ref[...] = g
        d_ref[...], m2_ref[...], v2_ref[...] = _adamw(w_ref[...], g, m_ref[...], v_ref[...])

    rs = pl.BlockSpec((None,) * nl + (tr, cdim), lambda *i: (*i, 0))
    return _call(
        body, name=name, grid=lead + (r // tr,),
        in_specs=[pl.BlockSpec((NDEV,) + (None,) * nl + (tr, cp), lambda *i: (0, *i, 0)), rs, rs, rs],
        out_specs=(rs, rs, rs, rs), out_shape=tuple(_sds(w.shape, F32) for _ in range(4)),
        cp=_cp(48, ("arbitrary",) * (nl + 1)), args=(recv, w, m, v))[0]


def adam_block(recv, w, m, v, lf, prev, name):
    half = FB // 2

    def body(*refs):
        rc_ref, w_ref, m_ref, v_ref = refs[:4]
        g_ref, d_ref, m2_ref, v2_ref = refs[-4:]
        g = _sum8(rc_ref)
        g_ref[...] = g
        d_ref[...], m2_ref[...], v2_ref[...] = _adamw(w_ref[...], g, m_ref[...], v_ref[...])

    rs = pl.BlockSpec((None, None, half, D), lambda i: (lf // 2, lf % 2, i, 0))
    prev = list(prev) if prev is not None else []
    return list(_pallas_call(
        body, name=name, grid=(2,), in_specs=[pl.BlockSpec((recv.shape[0], half, D), lambda i: (0, i, 0)), rs, rs, rs] + [ANY] * len(prev),
        out_specs=(rs, rs, rs, rs), out_shape=tuple(_sds((DEPTH, 2, FB, D), F32) for _ in range(4)),
        input_output_aliases={4 + k: k for k in range(len(prev))},
        compiler_params=_cp(48, ("arbitrary",)))(recv, w, m, v, *prev))


def adam_plain(g, w, m, v, tr, name):
    lead, (r, cdim) = w.shape[:-2], w.shape[-2:]
    nl = len(lead)

    def body(g_ref, w_ref, m_ref, v_ref, d_ref, m2_ref, v2_ref):
        d_ref[...], m2_ref[...], v2_ref[...] = _adamw(w_ref[...], g_ref[...], m_ref[...], v_ref[...])

    rs = pl.BlockSpec((None,) * nl + (tr, cdim), lambda *i: (*i, 0))
    return _call(
        body, name=name, grid=lead + (r // tr,), in_specs=[rs, rs, rs, rs], out_specs=(rs, rs, rs),
        out_shape=tuple(_sds(w.shape, F32) for _ in range(3)),
        cp=_cp(48, ("arbitrary",) * (nl + 1)), args=(g, w, m, v))[0]


def adam_native(gs, ws, ms, vs, name):
    n = len(ws)

    def body(*refs):
        g_refs, w_refs, m_refs, v_refs = (refs[k * n:(k + 1) * n] for k in range(4))
        d_refs, m2_refs, v2_refs = (refs[(4 + k) * n:(5 + k) * n] for k in range(3))
        for a in range(n):
            d_refs[a][...], m2_refs[a][...], v2_refs[a][...] = _adamw(w_refs[a][...], g_refs[a][...], m_refs[a][...], v_refs[a][...])

    vm = pl.BlockSpec(memory_space=pltpu.VMEM)
    outs = _pallas_call(body, name=name, in_specs=[vm] * (4 * n), out_specs=tuple([vm] * (3 * n)),
                        out_shape=tuple(_sds(w.shape, F32) for w in ws) * 3, compiler_params=_cp(40))(*gs, *ws, *ms, *vs)
    return outs[:n], outs[n:2 * n], outs[2 * n:]


def sum_sources(recv, name):
    r = recv.shape[1]

    def body(rc_ref, o_ref):
        o_ref[...] = _sum8(rc_ref)

    vm = pl.BlockSpec(memory_space=pltpu.VMEM)
    return _pallas_call(body, name=name, in_specs=[vm], out_specs=vm, out_shape=_sds((r, 128), F32),
                          compiler_params=_cp(40))(recv)


def _pack(arrs, dtype=F32):
    flat = jnp.concatenate([a.reshape(-1) for a in arrs]).astype(dtype)
    n = flat.shape[0]
    tile = 128 * (32 // jnp.dtype(dtype).itemsize)
    rows = -(-n // tile) * (tile // 128)
    return jnp.pad(flat, (0, rows * 128 - n)).reshape(rows, 128)


def _unpack(vec, shapes):
    flat = vec.reshape(-1)
    out, o = [], 0
    for sh in shapes:
        n = int(np.prod(sh))
        out.append(flat[o:o + n].reshape(sh))
        o += n
    return out


WEIGHTS = ['rel_bias', 'ada_w', 'ada_b', 'ln_g', 'ln_b', 'ffn_w_gate', 'ffn_w_up', 'ffn_w_down', 'w_in', 'w_out',
           'ssm_a_re', 'ssm_a_im', 'ssm_log_dt', 'ssm_b_re', 'ssm_b_im', 'ssm_c_re', 'ssm_c_im', 'ssm_d', 'glu_w',
           'glu_b', 'pool_w', 'pool_scale']
SMALL = ['rel_bias', 'ada_b', 'ln_g', 'ln_b', 'ssm_a_re', 'ssm_a_im', 'ssm_log_dt', 'ssm_b_re', 'ssm_b_im',
         'ssm_c_re', 'ssm_c_im', 'ssm_d', 'glu_b', 'pool_w', 'pool_scale']
SMALL_EXACT = ['rel_bias', 'ln_g', 'ln_b', 'ssm_a_re', 'ssm_a_im', 'ssm_log_dt']
SMALL_ROUNDED = ['ada_b', 'ssm_b_re', 'ssm_b_im', 'ssm_c_re', 'ssm_c_im', 'ssm_d', 'glu_b', 'pool_w', 'pool_scale']
SMALL_FULL_SHAPES = {'rel_bias': (32, 8), 'ada_b': (2, 9216), 'ln_g': (2, 3, 1024), 'ln_b': (2, 3, 1024),
                     'ssm_a_re': (2, 16, 64), 'ssm_a_im': (2, 16, 64), 'ssm_log_dt': (2, 16),
                     'ssm_b_re': (2, 16, 64, 16), 'ssm_b_im': (2, 16, 64, 16), 'ssm_c_re': (2, 16, 16, 64),
                     'ssm_c_im': (2, 16, 16, 64), 'ssm_d': (2, 256), 'glu_b': (2, 256), 'pool_w': (2, 4, 64, 64),
                     'pool_scale': (2, 256)}


def _step(P):
    me = _me()
    x0 = P['x'][0]
    target = P['loss_target'][0]

    bf = lambda a: a.astype(BF16)
    padr = lambda a: jnp.pad(bf(a), ((0, 0), (0, 0), (0, FBP - FB), (0, 0)))
    ffn_b = [padr(jnp.swapaxes(P['ffn_w_gate'], 2, 3)), padr(jnp.swapaxes(P['ffn_w_up'], 2, 3)), padr(P['ffn_w_down'])]
    mix_b = [bf(P['w_in']), bf(P['w_out']), bf(P['glu_w'])]

    def shards(l, sub):
        return [t[l] for t in mix_b] if sub == 1 else [t[l, sub // 2] for t in ffn_b]

    order = [(l, sub) for l in range(DEPTH) for sub in range(3)]
    nxt = dict(zip(order[:-1], order[1:]))
    W = {key: [None] * 3 for key in order}
    c_all, lng_all, lnb_all, *W[order[0]] = _exchange(Gather([P['c'], P['ln_g'], P['ln_b']] + shards(*order[0])), "gather_first")
    gather_queue = [(key, pos, a) for key in order[1:] for pos, a in enumerate(shards(*key))]

    def gather_ride(cap_us, must=None):
        units, used = [], 0.0
        while gather_queue:
            key, _, a = gather_queue[0]
            cost = a.size * a.dtype.itemsize * GATHER_US_PER_BYTE
            if key != must and used + cost / 2 > cap_us:
                break
            units.append(gather_queue.pop(0))
            used += cost
        return (Gather([a for _, _, a in units]) if units else None), units

    def gathered(units, outs):
        for (key, pos, _), o in zip(units, outs):
            W[key][pos] = o

    c_all = c_all.reshape(NDEV, D)
    ln_g = jnp.transpose(lng_all, (1, 2, 0, 3)).reshape(DEPTH, 3, D)
    ln_b = jnp.transpose(lnb_all, (1, 2, 0, 3)).reshape(DEPTH, 3, D)

    ada_b_cols = lax.dynamic_slice_in_dim(P['ada_b'], me * 1152, 1152, axis=1).reshape(DEPTH, 1, 1152)
    modc = ada_fwd(c_all, P['ada_w'], ada_b_cols)
    (mod_all,) = _exchange(Gather([modc]), "gather_mod")
    mod_me = lax.dynamic_index_in_dim(mod_all, me, axis=2, keepdims=False)
    mod = jnp.transpose(mod_me, (1, 0, 2)).reshape(DEPTH, 9, D)

    bias = att_bias(P['rel_bias'])
    ssm = []
    for l in range(DEPTH):
        prm = (P['ssm_a_re'][l], P['ssm_a_im'][l], P['ssm_log_dt'][l], P['ssm_b_re'][l], P['ssm_b_im'][l])
        (abr, abi, bbr, bbi), disc_vjp = jax.vjp(_ssm_discretise, *prm)
        ssm.append(dict(
            vjp=disc_vjp, a=jnp.stack([abr.reshape(8, 128), abi.reshape(8, 128)]),
            bre=_blockdiag(jnp.transpose(bbr, (0, 2, 1))).astype(MXU), bim=_blockdiag(jnp.transpose(bbi, (0, 2, 1))).astype(MXU),
            cre=_blockdiag(jnp.transpose(P['ssm_c_re'][l], (0, 2, 1))).astype(MXU),
            cim=_blockdiag(jnp.transpose(P['ssm_c_im'][l], (0, 2, 1))).astype(MXU),
            d=P['ssm_d'][l].reshape(1, 256), gb=P['glu_b'][l].reshape(1, 256),
            pw=_blockdiag(P['pool_w'][l]).astype(MXU), psc=P['pool_scale'][l].reshape(1, 256)))

    saved = []
    x = x0
    h = ln_mod_fwd(x, mod[0], 0, "ln_mod_fwd_l0s0")
    for l, sub in order:
        tag = f"l{l}s{sub}"
        after = (mod[nxt[(l, sub)][0]], nxt[(l, sub)][1]) if (l, sub) in nxt else None
        if sub != 1:
            wg, wu, wd = (t.reshape(NDEV * FBP, D) for t in W[(l, sub)])
            ride, units = gather_ride(60, nxt.get((l, sub)))
            (G, U, fo), got = ffn_fwd(h, wg, wu, wd, "ffn_fwd_" + tag, ride)
            gathered(units, got)
            saved.append(dict(x=x, h=h, G=G, U=U, f=fo))
            x, *hn = res_ln_fwd(x, fo, mod[l], sub, ln_g[l], ln_b[l], 0.5, "res_ln_fwd_" + tag, after)
        else:
            sp = ssm[l]
            win, wout, gluw = W[(l, sub)]
            ride, units = gather_ride(15)
            (z,), got = win_fwd(h, win, "win_fwd_" + tag, ride)
            gathered(units, got)
            ride, units = gather_ride(55)
            (ya, lse), got = att_fwd(z, bias, "att_fwd_" + tag, ride)
            gathered(units, got)
            ride, units = gather_ride(35)
            (ys, ypre, st), got = ssm_fwd(z, sp['a'], sp['bre'], sp['bim'], sp['cre'], sp['cim'], sp['d'], gluw, sp['gb'],
                                          "ssm_fwd_" + tag, ride)
            gathered(units, got)
            yp = pool_fwd(z, sp['pw'], sp['psc'], "pool_fwd_" + tag)
            ride, units = gather_ride(12, nxt.get((l, sub)))
            (o,), got = wout_fwd(ya, ys, yp, wout, "wout_fwd_" + tag, ride)
            gathered(units, got)
            saved.append(dict(x=x, h=h, z=z, ya=ya, lse=lse, ys=ys, ypre=ypre, st=st, yp=yp, f=o))
            x, *hn = res_ln_fwd(x, o, mod[l], sub, ln_g[l], ln_b[l], 1.0, "res_ln_fwd_" + tag, after)
        h = hn[0] if hn else None
    assert not gather_queue

    loss_tile, dx = loss_fwd_bwd(x, target, "loss")
    loss = lax.psum(loss_tile[0, 0], ("x", "y", "c"))

    flights = []

    dmod = [[None] * 9 for _ in range(DEPTH)]
    dlng = [[None] * 3 for _ in range(DEPTH)]
    dlnb = [[None] * 3 for _ in range(DEPTH)]
    dbiases = [None] * DEPTH
    small_l = [dict() for _ in range(DEPTH)]
    for l, sub in reversed(order):
        tag = f"l{l}s{sub}"
        sv = saved[3 * l + sub]
        if (l, sub) == order[-1]:
            dxa, df, sums = res_ln_bwd(sv['x'], sv['f'], mod[l], sub, ln_g[l], dx, 0.5, "res_ln_bwd_" + tag)
        dlng[l][sub], dlnb[l][sub], dmod[l][3 * sub + 2] = sums[0], sums[1], sums[2]
        if sub != 1:
            f = sub // 2
            wg, wu, wd = (t.reshape(NDEV * FBP, D) for t in W[(l, sub)])
            dwg, dwu, dwd, dh = ffn_bwd(df, sv['h'], sv['G'], sv['U'], wg, wu, wd, "ffn_bwd_" + tag)
            handle, zero = scatter_start([t.reshape(NDEV, FBP, D) for t in (dwg, dwu, dwd)], "scatter_start_" + tag, rows=FB)
            flights.append(((l, sub), handle))
        else:
            sp = ssm[l]
            win, wout, gluw = W[(l, sub)]
            dya, dys, dyp, dwout = wout_bwd(df, sv['ya'], sv['ys'], sv['yp'], wout, "wout_bwd_" + tag)
            dq, dk, dv, dbiases[l] = att_bwd(sv['z'], bias, sv['ya'], sv['lse'], dya, "att_bwd_" + tag)
            dus, dbre, dbim, dcre, dcim, da, dd, dgw, dgb = ssm_bwd(
                dys, sv['z'], sv['ypre'], sv['st'], sp['a'], sp['bre'], sp['bim'], sp['cre'], sp['cim'], sp['d'],
                gluw, sp['gb'], "ssm_bwd_" + tag)
            dup, dpw, dpsc = pool_bwd(dyp, sv['z'], sp['pw'], sp['psc'], "pool_bwd_" + tag)
            dh, dwin = win_bwd((dq, dk, dv, dus, dup), sv['h'], win, "win_bwd_" + tag)
            handle, zero = scatter_start([dwin, dwout, dgw.astype(BF16).reshape(NDEV, 32, 256)], "scatter_start_" + tag)
            flights.append(((l, sub), handle))
            d_are, d_aim, d_ldt, d_bre, d_bim = sp['vjp']((
                da[0].reshape(16, 64), da[1].reshape(16, 64),
                jnp.transpose(_blockdiag_take(dbre, 16, 64), (0, 2, 1)), jnp.transpose(_blockdiag_take(dbim, 16, 64), (0, 2, 1))))
            small_l[l] = dict(
                ssm_a_re=d_are, ssm_a_im=d_aim, ssm_log_dt=d_ldt, ssm_b_re=d_bre, ssm_b_im=d_bim,
                ssm_c_re=jnp.transpose(_blockdiag_take(dcre, 64, 16), (0, 2, 1)),
                ssm_c_im=jnp.transpose(_blockdiag_take(dcim, 64, 16), (0, 2, 1)),
                ssm_d=dd.reshape(256), glu_b=dgb.reshape(256), pool_w=_blockdiag_take(dpw, 64, 64), pool_scale=dpsc.reshape(256))
        if (l, sub) == order[0]:
            dx, sums2 = ln_mod_bwd(sv['x'], dh, mod[l] + zero, sub, dxa, "ln_mod_bwd_" + tag)
        else:
            lp, sp_ = order[order.index((l, sub)) - 1]
            svp = saved[3 * lp + sp_]
            dxa, df, sums, sums2 = ln_join_bwd(svp['x'], svp['f'], mod[lp], sp_, ln_g[lp], ln_b[lp], 1.0 if sp_ == 1 else 0.5,
                                               dh, mod[l] + zero, sub, dxa, "ln_join_bwd_" + tag)
        dmod[l][3 * sub], dmod[l][3 * sub + 1] = sums2[0], sums2[1]
    grad_x = dx[None]

    small = {k: jnp.stack([small_l[l][k] for l in range(DEPTH)]) for k in small_l[0]}
    small['rel_bias'] = relbias_grad(dbiases)
    small['ada_b'] = jnp.stack([jnp.stack(dmod[l]).reshape(9 * D) for l in range(DEPTH)])
    small['ln_g'] = jnp.stack([jnp.stack(dlng[l]) for l in range(DEPTH)])
    small['ln_b'] = jnp.stack([jnp.stack(dlnb[l]) for l in range(DEPTH)])
    small_flight, _ = scatter_start([_pack([small[k] for k in SMALL_EXACT]), _pack([small[k] for k in SMALL_ROUNDED], BF16)],
                                    "gather_small_start", whole=True)

    out = {}

    def put(name, g, d, m2, v2, shape):
        out['grad_' + name], out['delta_' + name] = g.reshape(shape), d.reshape(shape)
        out['new_m_' + name], out['new_v_' + name] = m2.reshape(shape), v2.reshape(shape)

    def wmv(name):
        return [P[pre + name] for pre in ('', 'm_', 'v_')]

    recv = {}
    started_last = small_flight[1][0]
    for key, handle in flights[:-1]:
        recv[key] = scatter_wait(handle, started_last, "scatter_wait_l%ds%d" % key)
    for pos, (name, tr) in enumerate((('w_in', 512), ('w_out', 128), ('glu_w', 32))):
        both = jnp.stack([recv[(l, 1)][pos] for l in range(DEPTH)], axis=1)
        put(name, *adam_rs(both, *wmv(name), tr, "adam_" + name), P[name].shape)
    ffn = (('ffn_w_gate', [jnp.swapaxes(t, 2, 3) for t in wmv('ffn_w_gate')]),
           ('ffn_w_up', [jnp.swapaxes(t, 2, 3) for t in wmv('ffn_w_up')]), ('ffn_w_down', wmv('ffn_w_down')))
    part = [None] * 3
    for l, sub in [key for key, _ in flights[:-1] if key[1] != 1]:
        for pos, (name, ops) in enumerate(ffn):
            part[pos] = adam_block(recv[(l, sub)][pos], *ops, 2 * l + sub // 2, part[pos], f"adam_{name}_l{l}s{sub}")

    (l, sub), handle = flights[-1]
    last = scatter_wait(handle, [p[0] for p in part] + [out['new_v_w_in']], "scatter_wait_l%ds%d" % (l, sub))
    for pos, (name, ops) in enumerate(ffn):
        part[pos] = adam_block(last[pos], *ops, 2 * l + sub // 2, part[pos], f"adam_{name}_l{l}s{sub}")
        put(name, *([jnp.swapaxes(t, 2, 3) for t in part[pos]] if pos < 2 else part[pos]), P[name].shape)
    exact_all, rounded_all = scatter_wait(small_flight, [p[0] for p in part], "gather_small_wait")
    gsum = dict(zip(SMALL_EXACT, _unpack(sum_sources(exact_all, "sum_small_exact"), [SMALL_FULL_SHAPES[k] for k in SMALL_EXACT])))
    gsum.update(zip(SMALL_ROUNDED, _unpack(sum_sources(rounded_all, "sum_small_rounded"), [SMALL_FULL_SHAPES[k] for k in SMALL_ROUNDED])))
    dmod_all = rounded_all.reshape(NDEV, -1)[:, :DEPTH * 9 * D].astype(F32).reshape(NDEV, DEPTH, 9 * D)
    dmod_cols = jnp.transpose(lax.dynamic_slice_in_dim(dmod_all, me * 1152, 1152, axis=2), (1, 0, 2))
    g_ada_w = ada_bwd(c_all, dmod_cols)

    put('ada_w', g_ada_w, *adam_plain(g_ada_w, *wmv('ada_w'), 256, "adam_ada_w"), P['ada_w'].shape)

    for k in ('ln_g', 'ln_b'):
        gsum[k] = lax.dynamic_slice_in_dim(gsum[k], me * 128, 128, axis=2)
    swaps = {'rel_bias': (0, 1), 'ln_g': (0, 1), 'ln_b': (0, 1), 'ssm_b_re': (2, 3), 'ssm_b_im': (2, 3)}
    view = lambda k, t: jnp.swapaxes(t, *swaps[k]) if k in swaps else t
    ds_, m2s, v2s = adam_native(*[[view(k, src(k)) for k in SMALL] for src in
                                  (lambda k: gsum[k], lambda k: P[k], lambda k: P['m_' + k], lambda k: P['v_' + k])],
                                "adam_small")
    for k, d, m2, v2 in zip(SMALL, ds_, m2s, v2s):
        put(k, gsum[k], view(k, d), view(k, m2), view(k, v2), P[k].shape)

    res = [loss, grad_x]
    for pre in ('grad_', 'delta_', 'new_m_', 'new_v_'):
        res += [out[pre + k] for k in WEIGHTS]
    return tuple(res)


def kernel(x, c, rel_bias, ada_w, ada_b, ln_g, ln_b, ffn_w_gate, ffn_w_up, ffn_w_down, w_in, w_out, ssm_a_re, ssm_a_im, ssm_log_dt, ssm_b_re, ssm_b_im, ssm_c_re, ssm_c_im, ssm_d, glu_w, glu_b, pool_w, pool_scale, loss_target, m_rel_bias, m_ada_w, m_ada_b, m_ln_g, m_ln_b, m_ffn_w_gate, m_ffn_w_up, m_ffn_w_down, m_w_in, m_w_out, m_ssm_a_re, m_ssm_a_im, m_ssm_log_dt, m_ssm_b_re, m_ssm_b_im, m_ssm_c_re, m_ssm_c_im, m_ssm_d, m_glu_w, m_glu_b, m_pool_w, m_pool_scale, v_rel_bias, v_ada_w, v_ada_b, v_ln_g, v_ln_b, v_ffn_w_gate, v_ffn_w_up, v_ffn_w_down, v_w_in, v_w_out, v_ssm_a_re, v_ssm_a_im, v_ssm_log_dt, v_ssm_b_re, v_ssm_b_im, v_ssm_c_re, v_ssm_c_im, v_ssm_d, v_glu_w, v_glu_b, v_pool_w, v_pool_scale):
    return _step(dict(locals()))
```

```python
import functools
import math

import numpy as np
import jax
import jax.numpy as jnp
from jax import lax
from jax.experimental import pallas as pl
from jax.experimental.pallas import tpu as pltpu

F32 = jnp.float32
BF16 = jnp.bfloat16
MXU = jnp.bfloat16

S = 2048
D = 1024
NDEV = 8
DEPTH = 2
D_ATT, D_SSM, D_POOL, D_IN = 512, 256, 256, 2048
N_HEADS = 8
FB = 352
FBP = 384
QB = 128
PATTERNS = ((128, 1), (512, 4), (2048, 16))
POOL_WINDOWS = (2, 4, 8, 16)
N_BUCKETS, MAX_DISTANCE = 32, 2048
ALPHA = (2 * DEPTH) ** 0.25
LN_EPS = 1e-5
NEG = -1e30
GATHER_US_PER_BYTE = 43e-6
LR, B1, B2, EPS, WD, STEP = 0.001, 0.9, 0.999, 1e-08, 0.01, 10

TM = 256
TMM = 512
MIB = 1024 * 1024


def _cp(vmem_mib, sem=None):
    kw = dict(vmem_limit_bytes=vmem_mib * MIB)
    if sem is not None:
        kw["dimension_semantics"] = sem
    return pltpu.CompilerParams(**kw)


def _sds(shape, dtype):
    return jax.ShapeDtypeStruct(shape, dtype)


def _mm(a, b):
    return jnp.dot(a.astype(MXU), b.astype(MXU), preferred_element_type=F32)


def _mm_nt(a, b):
    return lax.dot_general(a.astype(MXU), b.astype(MXU), (((1,), (1,)), ((), ())), preferred_element_type=F32)


def _mm_tn(a, b):
    return lax.dot_general(a.astype(MXU), b.astype(MXU), (((0,), (0,)), ((), ())), preferred_element_type=F32)


def _ln_stats(x):
    mu = jnp.mean(x, axis=-1, keepdims=True)
    xc = x - mu
    var = jnp.mean(xc * xc, axis=-1, keepdims=True)
    rstd = lax.rsqrt(var + LN_EPS)
    return xc * rstd, rstd


def _ln_bwd(dn, n, rstd):
    return rstd * (dn - jnp.mean(dn, axis=-1, keepdims=True) - n * jnp.mean(dn * n, axis=-1, keepdims=True))


def _me():
    return 4 * lax.axis_index("x") + 2 * lax.axis_index("y") + lax.axis_index("c")


ANY = pl.BlockSpec(memory_space=pl.ANY)
PIN_BYTES = 1 << 19


def _pallas_call(*a, **k):
    big = lambda o: math.prod(o.shape) * o.dtype.itemsize >= PIN_BYTES
    pin = lambda o: pltpu.HBM(o.shape, o.dtype) if isinstance(o, jax.ShapeDtypeStruct) and big(o) else o
    osh = k["out_shape"]
    k["out_shape"] = tuple(pin(o) for o in osh) if isinstance(osh, (tuple, list)) else pin(osh)
    fn = pl.pallas_call(*a, **k)

    def run(*args):
        return fn(*[pltpu.with_memory_space_constraint(x, pltpu.HBM) if big(x) else x for x in args])
    return run


class Gather:
    def __init__(self, srcs):
        self.srcs = list(srcs)
        self.n = len(self.srcs)
        self.bufs = []
        self.out_shapes = [_sds((NDEV,) + a.shape, a.dtype) for a in self.srcs]
        self.sems = [pltpu.SemaphoreType.DMA((7 * self.n,)), pltpu.SemaphoreType.DMA((7 * self.n,)),
                     pltpu.SemaphoreType.DMA((self.n,))]

    def _parts(self, srcs, outs, sems):
        send_sems, recv_sems, loc_sems = sems
        x, y, c = lax.axis_index("x"), lax.axis_index("y"), lax.axis_index("c")
        me, sib = (x, y, c), (x, y, 1 - c)
        chips = [(1 - x, y), (x, 1 - y), (1 - x, 1 - y)]
        slot = lambda d: 4 * d[0] + 2 * d[1] + d[2]

        def copy(a, k, block, to, src=None):
            dst = outs[a].at[slot(block)]
            return pltpu.make_async_remote_copy(
                src_ref=dst if src is None else src, dst_ref=dst,
                send_sem=send_sems.at[7 * a + k], recv_sem=recv_sems.at[7 * a + k],
                device_id=to, device_id_type=pl.DeviceIdType.MESH)

        local = [pltpu.make_async_copy(srcs[a], outs[a].at[slot(me)], loc_sems.at[a]) for a in range(self.n)]
        return me, sib, chips, c, copy, local

    def start(self, srcs, bufs, outs, sems):
        me, sib, chips, c, copy, local = self._parts(srcs, outs, sems)
        for a in range(self.n):
            local[a].start()
            copy(a, 0, me, sib, src=srcs[a]).start()
            for j, chip in enumerate(chips):
                copy(a, 1 + j, me, (*chip, c), src=srcs[a]).start()

    def finish(self, srcs, bufs, outs, sems):
        me, sib, chips, c, copy, local = self._parts(srcs, outs, sems)
        for a in range(self.n):
            for j, chip in enumerate(chips):
                copy(a, 1 + j, (*chip, c), me).wait_recv()
                copy(a, 4 + j, (*chip, c), sib).start()
        for a in range(self.n):
            copy(a, 0, sib, me).wait_recv()
            copy(a, 0, me, sib, src=srcs[a]).wait_send()
            for j, chip in enumerate(chips):
                copy(a, 4 + j, (*chip, 1 - c), me).wait_recv()
                copy(a, 1 + j, me, (*chip, c), src=srcs[a]).wait_send()
                copy(a, 4 + j, (*chip, c), sib).wait_send()
            local[a].wait()


def _call(body, *, name, grid, in_specs, out_specs, out_shape, args, scratch=(), cp=None, ride=None):
    out_specs, out_shape, scratch = list(out_specs), list(out_shape), list(scratch)
    if ride is None:
        outs = _pallas_call(body, name=name, grid=grid, in_specs=list(in_specs), out_specs=tuple(out_specs),
                              out_shape=tuple(out_shape), scratch_shapes=scratch, compiler_params=cp)(*args)
        return list(outs), []
    nin, nout, nscr, n, nb, no = len(in_specs), len(out_specs), len(scratch), ride.n, len(ride.bufs), len(ride.out_shapes)
    steps = list(grid)

    def wrapped(*refs):
        h_in, r_src, r_buf = refs[:nin], refs[nin:nin + n], refs[nin + n:nin + n + nb]
        o0 = nin + n + nb
        h_out, r_out = refs[o0:o0 + nout], refs[o0 + nout:o0 + nout + no]
        s0 = o0 + nout + no
        h_scr, sems = refs[s0:s0 + nscr], refs[s0 + nscr:]
        ids = [pl.program_id(a) for a in range(len(steps))]
        first = functools.reduce(jnp.logical_and, [i == 0 for i in ids])
        last = functools.reduce(jnp.logical_and, [i == s - 1 for i, s in zip(ids, steps)])

        @pl.when(first)
        def _():
            ride.start(r_src, r_buf, r_out, sems)

        body(*h_in, *h_out, *h_scr)

        @pl.when(last)
        def _():
            ride.finish(r_src, r_buf, r_out, sems)

    aliases = {nin + n + k: nout + k for k in range(nb)}
    outs = _pallas_call(
        wrapped, name=name, grid=grid, in_specs=list(in_specs) + [ANY] * (n + nb),
        out_specs=tuple(out_specs + [ANY] * no), out_shape=tuple(out_shape + ride.out_shapes),
        scratch_shapes=scratch + ride.sems, input_output_aliases=aliases, compiler_params=cp,
    )(*args, *ride.srcs, *ride.bufs)
    return list(outs[:nout]), list(outs[nout:])


def _exchange(ride, name):
    def body(dummy_ref, o_ref):
        o_ref[...] = dummy_ref[...]

    one = pl.BlockSpec((8, 128), lambda i: (0, 0))
    _, outs = _call(body, name=name, grid=(1,), in_specs=[one], out_specs=[one], out_shape=[_sds((8, 128), F32)],
                    args=(jnp.zeros((8, 128), F32),), ride=ride)
    return outs


HBM = pl.BlockSpec(memory_space=pltpu.HBM)
SEM = pl.BlockSpec(memory_space=pltpu.SEMAPHORE)


def routes_all(me):
    return [(k, me ^ k, me, me ^ k) for k in range(NDEV)]


def _scatter_copies(srcs, lands, sems, sending, whole, rows):
    send_sems, recv_sems, loc_sems = sems
    me = _me()
    rts = routes_all(me)
    part = lambda ref, slab: ref if whole else ref.at[slab] if rows is None else ref.at[slab, pl.ds(0, rows)]
    remote_ix = [r for r, (k, _, _, _) in enumerate(rts) if k != 0]
    local_ix = [r for r, (k, _, _, _) in enumerate(rts) if k == 0]
    remote, local = [], []
    for a in range(len(srcs)):
        for n, r in enumerate(remote_ix):
            k, slab, there, here = rts[r]
            t = me ^ k
            sem = len(remote_ix) * a + n
            remote.append(pltpu.make_async_remote_copy(
                src_ref=part(srcs[a], slab), dst_ref=lands[a].at[there if sending else here], send_sem=send_sems.at[sem],
                recv_sem=recv_sems.at[sem], device_id=(t // 4, (t // 2) % 2, t % 2), device_id_type=pl.DeviceIdType.MESH))
        for n, r in enumerate(local_ix):
            _, slab, there, _ = rts[r]
            local.append(pltpu.make_async_copy(part(srcs[a], slab), lands[a].at[there], loc_sems.at[len(local_ix) * a + n]))
    return remote, local


def scatter_start(payloads, name, whole=False, rows=None):
    n = len(payloads)
    nr = NDEV - 1

    def body(*refs):
        srcs, lands, sems = refs[:n], refs[n:2 * n], refs[2 * n:2 * n + 3]
        remote, local = _scatter_copies(srcs, lands, sems, True, whole, rows)
        for cp in local + remote:
            cp.start()
        refs[-1][...] = jnp.zeros((8, 128), F32)

    thru = [pltpu.HBM(p.shape, p.dtype) for p in payloads]
    land_shapes = [(NDEV,) + p.shape if whole else p.shape if rows is None else (NDEV, rows) + p.shape[2:] for p in payloads]
    outs = pl.pallas_call(
        body, name=name,
        out_shape=(pltpu.SemaphoreType.DMA((nr * n,)), pltpu.SemaphoreType.DMA((nr * n,)), pltpu.SemaphoreType.DMA((n,)),
                   *thru, *[pltpu.HBM(sh, p.dtype) for sh, p in zip(land_shapes, payloads)], _sds((8, 128), F32)),
        in_specs=[HBM] * (2 * n),
        out_specs=(SEM, SEM, SEM, *[HBM] * (2 * n), pl.BlockSpec(memory_space=pltpu.VMEM)),
        input_output_aliases={i: 3 + i for i in range(2 * n)},
        compiler_params=pltpu.CompilerParams(has_side_effects=pltpu.SideEffectType.DATAFLOW_SIDE_EFFECTING),
    )(*[pltpu.with_memory_space_constraint(p, pltpu.HBM) for p in payloads],
      *[pltpu.with_memory_space_constraint(lax.empty(sh, p.dtype), pltpu.HBM) for sh, p in zip(land_shapes, payloads)])
    return (outs[:3], outs[3:3 + n], outs[3 + n:3 + 2 * n], whole, rows), outs[-1][0, 0]


def scatter_wait(handle, after, name):
    sems, srcs_thru, lands_thru, whole, rows = handle
    n = len(srcs_thru)
    after = list(after) if isinstance(after, (list, tuple)) else [after]

    def body(*refs):
        srcs, lands, sems_ = refs[:n], refs[n:2 * n], refs[2 * n:2 * n + 3]
        remote, local = _scatter_copies(srcs, lands, sems_, False, whole, rows)
        for cp in remote:
            cp.wait_send()
            cp.wait_recv()
        for cp in local:
            cp.wait()

    outs = pl.pallas_call(
        body, name=name, out_shape=tuple(pltpu.HBM(p.shape, p.dtype) for p in (*srcs_thru, *lands_thru)),
        in_specs=[HBM] * (2 * n) + [SEM] * 3 + [HBM] * len(after), out_specs=tuple([HBM] * (2 * n)),
        input_output_aliases={i: i for i in range(2 * n)},
        compiler_params=pltpu.CompilerParams(has_side_effects=pltpu.SideEffectType.DATAFLOW_SIDE_EFFECTING),
    )(*srcs_thru, *lands_thru, *sems, *[pltpu.with_memory_space_constraint(a, pltpu.HBM) for a in after])
    return list(outs[n:])


def _row_spec(cols, tm=TM):
    return pl.BlockSpec((tm, cols), lambda i: (i, 0))


def _full_spec(shape):
    nd = len(shape)
    return pl.BlockSpec(shape, lambda i: (0,) * nd)


def ln_mod_fwd(x, mod, sub, name):
    def body(x_ref, mod_ref, h_ref):
        n, _ = _ln_stats(x_ref[...])
        shift = mod_ref[3 * sub:3 * sub + 1, :]
        scale = mod_ref[3 * sub + 1:3 * sub + 2, :]
        h_ref[...] = (n * (1.0 + scale) + shift).astype(MXU)

    return _pallas_call(
        body, name=name, grid=(S // TM,),
        in_specs=[_row_spec(D), _full_spec((9, D))], out_specs=_row_spec(D),
        out_shape=_sds((S, D), MXU), compiler_params=_cp(32, ("arbitrary",)))(x, mod)


def res_ln_fwd(x, f, mod, sub, lng, lnb, w, name, nxt=None):
    def body(x_ref, f_ref, mod_ref, g_ref, b_ref, *rest):
        gate = mod_ref[3 * sub + 2:3 * sub + 3, :]
        r = ALPHA * x_ref[...] + (w * gate) * f_ref[...]
        n, _ = _ln_stats(r)
        xo = n * g_ref[sub:sub + 1, :] + b_ref[sub:sub + 1, :]
        rest[-1 if nxt is None else -2][...] = xo
        if nxt is not None:
            nmod_ref, h_ref = rest[0], rest[-1]
            n2, _ = _ln_stats(xo)
            s2 = nxt[1]
            h_ref[...] = (n2 * (1.0 + nmod_ref[3 * s2 + 1:3 * s2 + 2, :]) + nmod_ref[3 * s2:3 * s2 + 1, :]).astype(MXU)

    more = nxt is not None
    return _pallas_call(
        body, name=name, grid=(S // TM,),
        in_specs=[_row_spec(D), _row_spec(D), _full_spec((9, D)), _full_spec((3, D)), _full_spec((3, D))] + [_full_spec((9, D))] * more,
        out_specs=(_row_spec(D),) + (_row_spec(D),) * more, out_shape=(_sds((S, D), F32),) + (_sds((S, D), MXU),) * more,
        compiler_params=_cp(32, ("arbitrary",)))(x, f, mod, lng, lnb, *([nxt[0]] if more else []))


def res_ln_bwd(x, f, mod, sub, lng, dxo, w, name):
    def body(x_ref, f_ref, mod_ref, g_ref, dxo_ref, dxa_ref, df_ref, sums_ref):
        i = pl.program_id(0)
        gate = mod_ref[3 * sub + 2:3 * sub + 3, :]
        fv = f_ref[...]
        r = ALPHA * x_ref[...] + (w * gate) * fv
        n, rstd = _ln_stats(r)
        dxo = dxo_ref[...]
        dr = _ln_bwd(dxo * g_ref[sub:sub + 1, :], n, rstd)
        dxa_ref[...] = ALPHA * dr
        df_ref[...] = ((w * gate) * dr).astype(MXU)
        part = jnp.concatenate([
            jnp.sum(dxo * n, axis=0, keepdims=True),
            jnp.sum(dxo, axis=0, keepdims=True),
            jnp.sum(dr * fv, axis=0, keepdims=True) * w,
            jnp.zeros((5, D), F32)], axis=0)

        @pl.when(i == 0)
        def _():
            sums_ref[...] = part

        @pl.when(i > 0)
        def _():
            sums_ref[...] += part

    return _call(
        body, name=name, grid=(S // TM,),
        in_specs=[_row_spec(D), _row_spec(D), _full_spec((9, D)), _full_spec((3, D)), _row_spec(D)],
        out_specs=(_row_spec(D), _row_spec(D), _full_spec((8, D))),
        out_shape=(_sds((S, D), F32), _sds((S, D), MXU), _sds((8, D), F32)),
        cp=_cp(32, ("arbitrary",)), args=(x, f, mod, lng, dxo))[0]


def ln_mod_bwd(x, dh, mod, sub, dxa, name):
    def body(x_ref, dh_ref, mod_ref, dxa_ref, dx_ref, sums_ref):
        i = pl.program_id(0)
        scale = mod_ref[3 * sub + 1:3 * sub + 2, :]
        n, rstd = _ln_stats(x_ref[...])
        dh = dh_ref[...]
        dx_ref[...] = dxa_ref[...] + _ln_bwd(dh * (1.0 + scale), n, rstd)
        part = jnp.concatenate([
            jnp.sum(dh, axis=0, keepdims=True),
            jnp.sum(dh * n, axis=0, keepdims=True),
            jnp.zeros((6, D), F32)], axis=0)

        @pl.when(i == 0)
        def _():
            sums_ref[...] = part

        @pl.when(i > 0)
        def _():
            sums_ref[...] += part

    return _call(
        body, name=name, grid=(S // TM,),
        in_specs=[_row_spec(D), _row_spec(D), _full_spec((9, D)), _row_spec(D)],
        out_specs=(_row_spec(D), _full_spec((8, D))),
        out_shape=(_sds((S, D), F32), _sds((8, D), F32)),
        cp=_cp(32, ("arbitrary",)), args=(x, dh, mod, dxa))[0]


def ln_join_bwd(xp, fp, modp, subp, lngp, lnbp, wp, dh, mod, sub, dxa, name):
    def body(xp_ref, fp_ref, modp_ref, g_ref, b_ref, dh_ref, mod_ref, dxa_ref, dxap_ref, dfp_ref, sumsp_ref, sums_ref):
        i = pl.program_id(0)
        gate = modp_ref[3 * subp + 2:3 * subp + 3, :]
        fv = fp_ref[...]
        n, rstd = _ln_stats(ALPHA * xp_ref[...] + (wp * gate) * fv)
        gain = g_ref[subp:subp + 1, :]
        n2, rstd2 = _ln_stats(n * gain + b_ref[subp:subp + 1, :])
        dh = dh_ref[...]
        dx = dxa_ref[...] + _ln_bwd(dh * (1.0 + mod_ref[3 * sub + 1:3 * sub + 2, :]), n2, rstd2)
        dr = _ln_bwd(dx * gain, n, rstd)
        dxap_ref[...] = ALPHA * dr
        dfp_ref[...] = ((wp * gate) * dr).astype(MXU)
        partp = jnp.concatenate([
            jnp.sum(dx * n, axis=0, keepdims=True), jnp.sum(dx, axis=0, keepdims=True),
            jnp.sum(dr * fv, axis=0, keepdims=True) * wp, jnp.zeros((5, D), F32)], axis=0)
        part = jnp.concatenate([
            jnp.sum(dh, axis=0, keepdims=True), jnp.sum(dh * n2, axis=0, keepdims=True), jnp.zeros((6, D), F32)], axis=0)

        @pl.when(i == 0)
        def _():
            sumsp_ref[...] = partp
            sums_ref[...] = part

        @pl.when(i > 0)
        def _():
            sumsp_ref[...] += partp
            sums_ref[...] += part

    return _pallas_call(
        body, name=name, grid=(S // TM,),
        in_specs=[_row_spec(D), _row_spec(D), _full_spec((9, D)), _full_spec((3, D)), _full_spec((3, D)), _row_spec(D),
                  _full_spec((9, D)), _row_spec(D)],
        out_specs=(_row_spec(D), _row_spec(D), _full_spec((8, D)), _full_spec((8, D))),
        out_shape=(_sds((S, D), F32), _sds((S, D), MXU), _sds((8, D), F32), _sds((8, D), F32)),
        compiler_params=_cp(40, ("arbitrary",)))(xp, fp, modp, lngp, lnbp, dh, mod, dxa)


def loss_fwd_bwd(y, target, name):
    def body(y_ref, t_ref, l_ref, dy_ref):
        i = pl.program_id(0)
        e = y_ref[...] - t_ref[...]
        dy_ref[...] = e * (1.0 / D)
        part = jnp.zeros((8, 128), F32) + (0.5 / D) * jnp.sum(e * e)

        @pl.when(i == 0)
        def _():
            l_ref[...] = part

        @pl.when(i > 0)
        def _():
            l_ref[...] += part

    return _pallas_call(
        body, name=name, grid=(S // TM,),
        in_specs=[_row_spec(D), _row_spec(D)], out_specs=(_full_spec((8, 128)), _row_spec(D)),
        out_shape=(_sds((8, 128), F32), _sds((S, D), F32)),
        compiler_params=_cp(32, ("arbitrary",)))(y, target)


HB = 2 * FBP
NHB = NDEV * FBP // HB
TMB = 1024


def _wrows(buffers=2):
    return pl.BlockSpec((HB, D), lambda j, i: (j, 0), pipeline_mode=pl.Buffered(buffers))


def _resident(shape):
    return pl.BlockSpec(shape, lambda j, i: (0, 0), pipeline_mode=pl.Buffered(1))


def ffn_fwd(h, wgt, wut, wd, name, ride=None):
    def body(h_ref, wg_ref, wu_ref, wd_ref, g_ref, u_ref, f_ref):
        j, i = pl.program_id(0), pl.program_id(1)
        hv = h_ref[...]
        g = _mm_nt(hv, wg_ref[...])
        u = _mm_nt(hv, wu_ref[...])
        g_ref[...] = g.astype(MXU)
        u_ref[...] = u.astype(MXU)
        a = g * jax.nn.sigmoid(g) * u
        part = _mm(a, wd_ref[...])
        rows = pl.ds(pl.multiple_of(i * TMB, TMB), TMB)

        @pl.when(j == 0)
        def _():
            f_ref[rows, :] = part

        @pl.when(j > 0)
        def _():
            f_ref[rows, :] += part

    gu = pl.BlockSpec((TMB, HB), lambda j, i: (i, j))
    return _call(
        body, name=name, grid=(NHB, S // TMB),
        in_specs=[pl.BlockSpec((TMB, D), lambda j, i: (i, 0)), _wrows(), _wrows(), _wrows()],
        out_specs=(gu, gu, _resident((S, D))),
        out_shape=(_sds((S, NDEV * FBP), MXU), _sds((S, NDEV * FBP), MXU), _sds((S, D), F32)),
        cp=_cp(52, ("arbitrary", "arbitrary")), args=(h, wgt, wut, wd), ride=ride)


def ffn_bwd(df, h, g, u, wgt, wut, wd, name):
    ni = S // TMB

    def body(df_ref, h_ref, g_ref, u_ref, wg_ref, wu_ref, wd_ref, dwg_ref, dwu_ref, dwd_ref, dh_ref,
             ag_ref, au_ref, ad_ref):
        j, i = pl.program_id(0), pl.program_id(1)
        dfv, hv = df_ref[...], h_ref[...]
        gv, uv = g_ref[...].astype(F32), u_ref[...].astype(F32)
        da = _mm_nt(dfv, wd_ref[...])
        sg = jax.nn.sigmoid(gv)
        silu = gv * sg
        du = da * silu
        dg = da * uv * (sg * (1.0 + gv * (1.0 - sg)))
        p_d = _mm_tn(silu * uv, dfv)
        p_g = _mm_tn(dg, hv)
        p_u = _mm_tn(du, hv)

        @pl.when(i == 0)
        def _():
            ad_ref[...] = p_d
            ag_ref[...] = p_g
            au_ref[...] = p_u

        @pl.when(i > 0)
        def _():
            ad_ref[...] += p_d
            ag_ref[...] += p_g
            au_ref[...] += p_u

        @pl.when(i == ni - 1)
        def _():
            dwd_ref[...] = ad_ref[...].astype(BF16)
            dwg_ref[...] = ag_ref[...].astype(BF16)
            dwu_ref[...] = au_ref[...].astype(BF16)

        part = _mm(dg, wg_ref[...]) + _mm(du, wu_ref[...])
        rows = pl.ds(pl.multiple_of(i * TMB, TMB), TMB)

        @pl.when(j == 0)
        def _():
            dh_ref[rows, :] = part

        @pl.when(j > 0)
        def _():
            dh_ref[rows, :] += part

    gu = pl.BlockSpec((TMB, HB), lambda j, i: (i, j))
    rowt = pl.BlockSpec((TMB, D), lambda j, i: (i, 0))
    return _call(
        body, name=name, grid=(NHB, ni),
        in_specs=[rowt, rowt, gu, gu, _wrows(1), _wrows(1), _wrows(1)],
        out_specs=(_wrows(1), _wrows(1), _wrows(1), _resident((S, D))),
        out_shape=(_sds((NDEV * FBP, D), BF16), _sds((NDEV * FBP, D), BF16), _sds((NDEV * FBP, D), BF16), _sds((S, D), F32)),
        scratch=[pltpu.VMEM((HB, D), F32), pltpu.VMEM((HB, D), F32), pltpu.VMEM((HB, D), F32)],
        cp=_cp(60, ("arbitrary", "arbitrary")), args=(df, h, g, u, wgt, wut, wd))[0]


def win_fwd(h, win, name, ride=None):
    def body(h_ref, w_ref, z_ref):
        hv = h_ref[...]
        for j in range(NDEV):
            z_ref[:, 256 * j:256 * (j + 1)] = _mm(hv, w_ref[j])

    return _call(
        body, name=name, grid=(S // TMM,),
        in_specs=[_row_spec(D, TMM), _full_spec((NDEV, D, 256))],
        out_specs=[_row_spec(D_IN, TMM)], out_shape=[_sds((S, D_IN), F32)],
        cp=_cp(40, ("arbitrary",)), args=(h, win), ride=ride)


def win_bwd(dparts, h, win, name):
    ni = S // TMM

    def body(dq_ref, dk_ref, dv_ref, dus_ref, dup_ref, h_ref, w_ref, dh_ref, dw_ref, acc_ref):
        i = pl.program_id(0)
        hv = h_ref[...]
        cols = [dq_ref[:, 0:256], dq_ref[:, 256:512], dk_ref[:, 0:256], dk_ref[:, 256:512],
                dv_ref[:, 0:256], dv_ref[:, 256:512], dus_ref[...], dup_ref[...]]
        dh = jnp.zeros((TMM, D), F32)
        for j in range(NDEV):
            dz = cols[j].astype(MXU)
            dh = dh + _mm_nt(dz, w_ref[j])
            p = _mm_tn(hv, dz)

            @pl.when(i == 0)
            def _():
                acc_ref[j] = p

            @pl.when(i > 0)
            def _():
                acc_ref[j] += p

        dh_ref[...] = dh

        @pl.when(i == ni - 1)
        def _():
            dw_ref[...] = acc_ref[...].astype(BF16)

    return _call(
        body, name=name, grid=(ni,),
        in_specs=[_row_spec(512, TMM), _row_spec(512, TMM), _row_spec(512, TMM), _row_spec(256, TMM), _row_spec(256, TMM),
                  _row_spec(D, TMM), _full_spec((NDEV, D, 256))],
        out_specs=(_row_spec(D, TMM), _full_spec((NDEV, D, 256))),
        out_shape=(_sds((S, D), F32), _sds((NDEV, D, 256), BF16)),
        scratch=[pltpu.VMEM((NDEV, D, 256), F32)],
        cp=_cp(48, ("arbitrary",)), args=(*dparts, h, win))[0]


def wout_fwd(ya, ys, yp, wout, name, ride=None):
    def body(ya_ref, ys_ref, yp_ref, w_ref, o_ref):
        w = w_ref[...].reshape(D, D)
        o_ref[...] = _mm(ya_ref[...], w[0:512]) + _mm(ys_ref[...], w[512:768]) + _mm(yp_ref[...], w[768:1024])

    return _call(
        body, name=name, grid=(S // TMM,),
        in_specs=[_row_spec(512, TMM), _row_spec(256, TMM), _row_spec(256, TMM), _full_spec((NDEV, 128, D))],
        out_specs=[_row_spec(D, TMM)], out_shape=[_sds((S, D), F32)],
        cp=_cp(40, ("arbitrary",)), args=(ya, ys, yp, wout), ride=ride)


def wout_bwd(do, ya, ys, yp, wout, name):
    ni = S // TMM

    def body(do_ref, ya_ref, ys_ref, yp_ref, w_ref, dya_ref, dys_ref, dyp_ref, dw_ref, acc_ref):
        i = pl.program_id(0)
        w = w_ref[...].reshape(D, D)
        dov = do_ref[...]
        dya_ref[...] = _mm_nt(dov, w[0:512])
        dys_ref[...] = _mm_nt(dov, w[512:768])
        dyp_ref[...] = _mm_nt(dov, w[768:1024])
        parts = [(0, 512, _mm_tn(ya_ref[...], dov)), (512, 768, _mm_tn(ys_ref[...], dov)),
                 (768, 1024, _mm_tn(yp_ref[...], dov))]
        for lo, hi, p in parts:
            @pl.when(i == 0)
            def _():
                acc_ref[lo:hi, :] = p

            @pl.when(i > 0)
            def _():
                acc_ref[lo:hi, :] += p

        @pl.when(i == ni - 1)
        def _():
            dw_ref[...] = acc_ref[...].astype(BF16).reshape(NDEV, 128, D)

    return _call(
        body, name=name, grid=(ni,),
        in_specs=[_row_spec(D, TMM), _row_spec(512, TMM), _row_spec(256, TMM), _row_spec(256, TMM),
                  _full_spec((NDEV, 128, D))],
        out_specs=(_row_spec(512, TMM), _row_spec(256, TMM), _row_spec(256, TMM), _full_spec((NDEV, 128, D))),
        out_shape=(_sds((S, 512), F32), _sds((S, 256), F32), _sds((S, 256), F32), _sds((NDEV, 128, D), BF16)),
        scratch=[pltpu.VMEM((D, D), F32)],
        cp=_cp(40, ("arbitrary",)), args=(do, ya, ys, yp, wout))[0]


def _t5_bucket(dist):
    max_exact = N_BUCKETS // 2
    d = np.maximum(dist, 1).astype(np.float32)
    large = max_exact + (np.log(d / max_exact) / math.log(MAX_DISTANCE / max_exact)
                         * (N_BUCKETS - max_exact)).astype(np.int32)
    large = np.minimum(large, N_BUCKETS - 1)
    return np.where(dist < max_exact, dist, large).astype(np.int32)


def _att_static():
    i = np.arange(QB)[:, None]
    j = np.arange(2 * QB)[None, :]
    r = i + QB - j
    buckets, bands = [], []
    for window, dil in PATTERNS:
        bands.append((r >= 0) & (r <= window // dil))
        buckets.append(_t5_bucket(np.clip(r, 0, None) * dil))
    return np.stack(buckets), np.stack(bands), np.broadcast_to(j >= QB, (QB, 2 * QB))


def att_bias(rel_bias):
    m = np.arange(2 * QB)
    rows = []
    for window, dil in PATTERNS:
        r = QB - m
        ok = (r >= 0) & (r <= window // dil)
        b = rel_bias[_t5_bucket(np.clip(r, 0, None) * dil)]
        rows.append(jnp.where(ok[:, None], b, NEG).T)
    return jnp.broadcast_to(jnp.stack(rows)[:, :, None, :], (3, N_HEADS, 8, 2 * QB))


def _bias_tiles(t_ref, tiles):
    col = lax.broadcasted_iota(jnp.int32, (QB, 2 * QB), 1)
    for p in range(3):
        for hh in range(2):
            t = pltpu.roll(jnp.broadcast_to(t_ref[p, hh, 0:1, :], (QB, 2 * QB)), 0, 1, stride=1, stride_axis=0)
            tiles[p, hh, 0] = t
            tiles[p, hh, 1] = jnp.where(col >= QB, t, NEG)


def _permute_in(dst_ref, src_ref, d, scale=None, pad=QB):
    L = S // d
    for r in range(d):
        v = src_ref[pl.ds(r, L, stride=d), :] if d > 1 else src_ref[...]
        if scale is not None:
            v = v * scale
        dst_ref[pad + r * L:pad + (r + 1) * L, :] = v.astype(dst_ref.dtype)


def att_fwd(z, bias, name, ride=None):
    def body(q_ref, k_ref, v_ref, t_ref, y_ref, l_ref, qs, ks, vs, o_perm, l_perm, o_nat, l_nat, b_ref):
        _bias_tiles(t_ref, b_ref)
        zero_pad = jnp.zeros((QB, 128), MXU)
        ks[0:QB, :] = zero_pad
        vs[0:QB, :] = zero_pad
        lane = lax.broadcasted_iota(jnp.int32, (QB, 128), 1)
        for p, (_, d) in enumerate(PATTERNS):
            L = S // d
            nb = L // QB
            _permute_in(qs, q_ref, d, scale=0.125, pad=0)
            _permute_in(ks, k_ref, d)
            _permute_in(vs, v_ref, d)

            def blk(b, carry):
                r0 = pl.multiple_of(b * QB, QB)
                q = qs[pl.ds(r0, QB), :]
                kb = ks[pl.ds(r0, 2 * QB), :]
                vb = vs[pl.ds(r0, 2 * QB), :]
                first = ((b % nb) == 0).astype(jnp.int32)
                res = []
                for hh in range(2):
                    sel = (lane < 64) if hh == 0 else (lane >= 64)
                    qm = jnp.where(sel, q, jnp.zeros_like(q))
                    s = _mm_nt(qm, kb) + b_ref[p, hh, first]
                    m = jnp.max(s, axis=1, keepdims=True)
                    pe = jnp.exp(s - m)
                    den = jnp.sum(pe, axis=1, keepdims=True)
                    res.append((_mm(pe, vb) / den, m + jnp.log(den)))
                o_perm[pl.ds(r0, QB), :] = jnp.where(lane < 64, res[0][0], res[1][0])
                l_perm[pl.ds(r0, QB), :] = jnp.where(lane < 64, res[0][1], res[1][1])
                return carry

            lax.fori_loop(0, S // QB, blk, 0, unroll=8)
            for r in range(d):
                if d > 1:
                    o_nat[p, pl.ds(r, L, stride=d), :] = o_perm[r * L:(r + 1) * L, :]
                    l_nat[p, pl.ds(r, L, stride=d), :] = l_perm[r * L:(r + 1) * L, :]
                else:
                    o_nat[p] = o_perm[...]
                    l_nat[p] = l_perm[...]
        l0, l1, l2 = l_nat[0], l_nat[1], l_nat[2]
        m = jnp.maximum(jnp.maximum(l0, l1), l2)
        e0, e1, e2 = jnp.exp(l0 - m), jnp.exp(l1 - m), jnp.exp(l2 - m)
        den = e0 + e1 + e2
        y_ref[...] = (e0 * o_nat[0] + e1 * o_nat[1] + e2 * o_nat[2]) / den
        l_ref[...] = m + jnp.log(den)

    col = lambda c0: pl.BlockSpec((S, 128), lambda hp: (0, c0 + hp))
    return _call(
        body, name=name, grid=(N_HEADS // 2,),
        in_specs=[col(0), col(4), col(8), pl.BlockSpec((3, 2, 8, 2 * QB), lambda hp: (0, hp, 0, 0))],
        out_specs=(col(0), col(0)),
        out_shape=(_sds((S, D_ATT), F32), _sds((S, D_ATT), F32)),
        scratch=[pltpu.VMEM((S, 128), MXU), pltpu.VMEM((S + QB, 128), MXU), pltpu.VMEM((S + QB, 128), MXU),
                 pltpu.VMEM((S, 128), F32), pltpu.VMEM((S, 128), F32),
                 pltpu.VMEM((3, S, 128), F32), pltpu.VMEM((3, S, 128), F32),
                 pltpu.VMEM((3, 2, 2, QB, 2 * QB), F32)],
        cp=_cp(40, ("arbitrary",)), args=(z, z, z, bias), ride=ride)


def att_bwd(z, bias, y, lse, dy, name):
    def body(q_ref, k_ref, v_ref, t_ref, y_ref, l_ref, dy_ref, dq_ref, dk_ref, dv_ref, db_ref,
             qs, ks, vs, dys, ls, dds, dn_nat, dq_perm, dk_perm, dv_perm, b_ref):
        _bias_tiles(t_ref, b_ref)
        zero_pad = jnp.zeros((QB, 128), MXU)
        ks[0:QB, :] = zero_pad
        vs[0:QB, :] = zero_pad
        lane = lax.broadcasted_iota(jnp.int32, (QB, 128), 1)
        lane_s = lax.broadcasted_iota(jnp.int32, (S, 128), 1)
        t = dy_ref[...] * y_ref[...]
        sa = jnp.sum(jnp.where(lane_s < 64, t, 0.0), axis=1, keepdims=True)
        sb = jnp.sum(jnp.where(lane_s >= 64, t, 0.0), axis=1, keepdims=True)
        dn_nat[...] = jnp.where(lane_s < 64, sa, sb)
        dq_ref[...] = jnp.zeros((S, 128), F32)
        dk_ref[...] = jnp.zeros((S, 128), F32)
        dv_ref[...] = jnp.zeros((S, 128), F32)
        db_ref[...] = jnp.zeros((3, 2, QB, 2 * QB), F32)
        for p, (_, d) in enumerate(PATTERNS):
            L = S // d
            nb = L // QB
            _permute_in(qs, q_ref, d, scale=0.125, pad=0)
            _permute_in(ks, k_ref, d)
            _permute_in(vs, v_ref, d)
            _permute_in(dys, dy_ref, d, pad=0)
            _permute_in(ls, l_ref, d, pad=0)
            _permute_in(dds, dn_nat, d, pad=0)
            dk_perm[...] = jnp.zeros((S + QB, 128), F32)
            dv_perm[...] = jnp.zeros((S + QB, 128), F32)

            def blk(b, carry):
                r0 = pl.multiple_of(b * QB, QB)
                q = qs[pl.ds(r0, QB), :]
                kb = ks[pl.ds(r0, 2 * QB), :]
                vb = vs[pl.ds(r0, 2 * QB), :]
                dyb = dys[pl.ds(r0, QB), :]
                lb = ls[pl.ds(r0, QB), :]
                db = dds[pl.ds(r0, QB), :]
                first = ((b % nb) == 0).astype(jnp.int32)
                lane2 = jnp.concatenate([lane, lane], axis=0)
                own = (lane2 >> 6) == (lax.broadcasted_iota(jnp.int32, (2 * QB, 128), 0) >> 7)
                qm = jnp.where(own, jnp.concatenate([q, q], axis=0), jnp.zeros((2 * QB, 128), q.dtype))
                dym = jnp.where(own, jnp.concatenate([dyb, dyb], axis=0), jnp.zeros((2 * QB, 128), dyb.dtype))
                wide = lambda t: jnp.concatenate([jnp.broadcast_to(t[:, 0:1], (QB, 2 * QB)), jnp.broadcast_to(t[:, 64:65], (QB, 2 * QB))], axis=0)
                lse2, dd2 = wide(lb), wide(db)
                bias2 = jnp.concatenate([b_ref[p, 0, first], b_ref[p, 1, first]], axis=0)
                pr = jnp.exp(_mm_nt(qm, kb) + bias2 - lse2)
                ds = pr * (_mm_nt(dym, vb) - dd2)
                db_ref[p, 0] += ds[0:QB]
                db_ref[p, 1] += ds[QB:2 * QB]
                dq2 = _mm(ds, kb)
                dqs = [dq2[0:QB], dq2[QB:2 * QB]]
                dkb = _mm_tn(ds, qm)
                dvb = _mm_tn(pr, dym)
                dq_perm[pl.ds(r0, QB), :] = jnp.where(lane < 64, dqs[0], dqs[1])
                dk_perm[pl.ds(r0, 2 * QB), :] += dkb
                dv_perm[pl.ds(r0, 2 * QB), :] += dvb
                return carry

            lax.fori_loop(0, S // QB, blk, 0, unroll=4)
            for r in range(d):
                idx = pl.ds(r, L, stride=d) if d > 1 else pl.ds(0, S)
                dq_ref[idx, :] += dq_perm[r * L:(r + 1) * L, :] * 0.125
                dk_ref[idx, :] += dk_perm[QB + r * L:QB + (r + 1) * L, :]
                dv_ref[idx, :] += dv_perm[QB + r * L:QB + (r + 1) * L, :]

    col = lambda c0: pl.BlockSpec((S, 128), lambda hp: (0, c0 + hp))
    bspec = pl.BlockSpec((3, 2, 8, 2 * QB), lambda hp: (0, hp, 0, 0))
    return _call(
        body, name=name, grid=(N_HEADS // 2,),
        in_specs=[col(0), col(4), col(8), bspec, col(0), col(0), col(0)],
        out_specs=(col(0), col(0), col(0), pl.BlockSpec((3, 2, QB, 2 * QB), lambda hp: (0, hp, 0, 0))),
        out_shape=(_sds((S, D_ATT), F32), _sds((S, D_ATT), F32), _sds((S, D_ATT), F32),
                   _sds((3, N_HEADS, QB, 2 * QB), F32)),
        scratch=[pltpu.VMEM((S, 128), MXU), pltpu.VMEM((S + QB, 128), MXU), pltpu.VMEM((S + QB, 128), MXU),
                 pltpu.VMEM((S, 128), MXU), pltpu.VMEM((S, 128), F32), pltpu.VMEM((S, 128), F32),
                 pltpu.VMEM((S, 128), F32), pltpu.VMEM((S, 128), F32),
                 pltpu.VMEM((S + QB, 128), F32), pltpu.VMEM((S + QB, 128), F32),
                 pltpu.VMEM((3, 2, 2, QB, 2 * QB), F32)],
        cp=_cp(48, ("arbitrary",)), args=(z, z, z, bias, y, lse, dy))[0]


def relbias_grad(dbiases):
    bucket, band, _ = _att_static()
    onehot = (bucket[:, None] == np.arange(N_BUCKETS)[None, :, None, None]) & band[:, None]
    onehot = jnp.asarray(onehot.reshape(3, N_BUCKETS, QB * 2 * QB), BF16)

    def body(db0_ref, db1_ref, oh_ref, o_ref):
        acc = jnp.zeros((N_HEADS, N_BUCKETS), F32)
        for p in range(3):
            acc = acc + lax.dot_general(db0_ref[p] + db1_ref[p], oh_ref[p].astype(F32), (((1,), (1,)), ((), ())),
                                        preferred_element_type=F32, precision=lax.Precision.HIGHEST)
        o_ref[...] = acc

    vm = pl.BlockSpec(memory_space=pltpu.VMEM)
    out = _pallas_call(body, name="relbias_grad", in_specs=[vm, vm, vm], out_specs=vm,
                         out_shape=_sds((N_HEADS, N_BUCKETS), F32), compiler_params=_cp(40))(
        *[d.reshape(3, N_HEADS, QB * 2 * QB) for d in dbiases], onehot)
    return out.T


def _panel(t_ref, ri, j):
    return t_ref[ri, pl.ds(j, S, stride=8), :]


def _gelu(x):
    c = math.sqrt(2.0 / math.pi)
    th = jnp.tanh(c * (x + 0.044715 * x * x * x))
    return 0.5 * x * (1.0 + th), th


def ssm_fwd(z, a, bre, bim, cre, cim, dsk, gluw, glub, name, ride=None):
    def body(u_ref, a_ref, bre_ref, bim_ref, cre_ref, cim_ref, d_ref, gw_ref, gb_ref, y_ref, yp_ref, st_hbm, st_ref):
        u = u_ref[...]
        for j in range(8):
            st_ref[0, pl.ds(j, S, stride=8), :] = _mm(u, bre_ref[:, 128 * j:128 * (j + 1)])
            st_ref[1, pl.ds(j, S, stride=8), :] = _mm(u, bim_ref[:, 128 * j:128 * (j + 1)])
        ar, ai = a_ref[0], a_ref[1]

        def step(t, c):
            re, im = c
            i = pl.multiple_of(t * 8, 8)
            nre = ar * re - ai * im + st_ref[0, pl.ds(i, 8), :]
            nim = ar * im + ai * re + st_ref[1, pl.ds(i, 8), :]
            st_ref[0, pl.ds(i, 8), :] = nre
            st_ref[1, pl.ds(i, 8), :] = nim
            return nre, nim

        zero = jnp.zeros((8, 128), F32)
        lax.fori_loop(0, S, step, (zero, zero), unroll=8)
        y = d_ref[...] * u
        for j in range(8):
            y = y + _mm(_panel(st_ref, 0, j), cre_ref[128 * j:128 * (j + 1), :])
            y = y - _mm(_panel(st_ref, 1, j), cim_ref[128 * j:128 * (j + 1), :])
        pltpu.sync_copy(st_ref, st_hbm)
        yp_ref[...] = y
        gl, _ = _gelu(y)
        tt = _mm(gl, gw_ref[...].reshape(D_SSM, D_SSM)) + gb_ref[...]
        y_ref[...] = y * jax.nn.sigmoid(tt)

    vm = lambda shape: pl.BlockSpec(shape, lambda i: (0,) * len(shape))
    return _call(
        body, name=name, grid=(1,),
        in_specs=[pl.BlockSpec((S, 256), lambda i: (0, 6)), vm((2, 8, 128)), vm((256, 1024)), vm((256, 1024)),
                  vm((1024, 256)), vm((1024, 256)), vm((1, 256)),
                  vm((NDEV, 32, 256)), vm((1, 256))],
        out_specs=(vm((S, 256)), vm((S, 256)), pl.BlockSpec(memory_space=pl.ANY)),
        out_shape=(_sds((S, 256), F32), _sds((S, 256), F32), _sds((2, S * 8, 128), F32)),
        scratch=[pltpu.VMEM((2, S * 8, 128), F32)],
        cp=_cp(40, ("arbitrary",)), args=(z, a, bre, bim, cre, cim, dsk, gluw, glub), ride=ride)


def ssm_bwd(dy, z, ypre, st, a, bre, bim, cre, cim, dsk, gluw, glub, name):
    def body(dy_ref, u_ref, yp_ref, st_hbm, a_ref, bre_ref, bim_ref, cre_ref, cim_ref, d_ref, gw_ref, gb_ref,
             du_ref, dbre_ref, dbim_ref, dcre_ref, dcim_ref, da_ref, dd_ref, dgw_ref, dgb_ref, g_ref, st_ref):
        pltpu.sync_copy(st_hbm, st_ref)
        u = u_ref[...]
        y = yp_ref[...]
        dout = dy_ref[...]
        gw = gw_ref[...].reshape(D_SSM, D_SSM)
        gl, th = _gelu(y)
        sig = jax.nn.sigmoid(_mm(gl, gw) + gb_ref[...])
        dt = dout * y * sig * (1.0 - sig)
        dgw_ref[...] = _mm_tn(gl, dt)
        dgb_ref[...] = jnp.sum(dt, axis=0, keepdims=True)
        c = math.sqrt(2.0 / math.pi)
        dgelu = 0.5 * (1.0 + th) + 0.5 * y * (1.0 - th * th) * c * (1.0 + 3.0 * 0.044715 * y * y)
        dyv = dout * sig + _mm_nt(dt, gw) * dgelu
        dd_ref[...] = jnp.sum(dyv * u, axis=0, keepdims=True)
        for j in range(8):
            rows = slice(128 * j, 128 * (j + 1))
            g_ref[0, pl.ds(j, S, stride=8), :] = _mm_nt(dyv, cre_ref[rows, :])
            g_ref[1, pl.ds(j, S, stride=8), :] = -_mm_nt(dyv, cim_ref[rows, :])
            dcre_ref[rows, :] = _mm_tn(_panel(st_ref, 0, j), dyv)
            dcim_ref[rows, :] = -_mm_tn(_panel(st_ref, 1, j), dyv)
        ar, ai = a_ref[0], a_ref[1]

        def step(k, c4):
            gre, gim, dar, dai = c4
            i = pl.multiple_of((S - 1 - k) * 8, 8)
            nre = g_ref[0, pl.ds(i, 8), :] + ar * gre + ai * gim
            nim = g_ref[1, pl.ds(i, 8), :] + ar * gim - ai * gre
            g_ref[0, pl.ds(i, 8), :] = nre
            g_ref[1, pl.ds(i, 8), :] = nim
            sre = st_ref[0, pl.ds(i - 8, 8), :]
            sim = st_ref[1, pl.ds(i - 8, 8), :]
            return nre, nim, dar + nre * sre + nim * sim, dai + nim * sre - nre * sim

        zero = jnp.zeros((8, 128), F32)
        gre, gim, dar, dai = lax.fori_loop(0, S - 1, step, (zero, zero, zero, zero), unroll=8)
        g_ref[0, 0:8, :] = g_ref[0, 0:8, :] + ar * gre + ai * gim
        g_ref[1, 0:8, :] = g_ref[1, 0:8, :] + ar * gim - ai * gre
        da_ref[0] = dar
        da_ref[1] = dai
        du = dyv * d_ref[...]
        for j in range(8):
            cols = slice(128 * j, 128 * (j + 1))
            gr, gi = _panel(g_ref, 0, j), _panel(g_ref, 1, j)
            dbre_ref[:, cols] = _mm_tn(u, gr)
            dbim_ref[:, cols] = _mm_tn(u, gi)
            du = du + _mm_nt(gr, bre_ref[:, cols]) + _mm_nt(gi, bim_ref[:, cols])
        du_ref[...] = du

    vm = lambda shape: pl.BlockSpec(shape, lambda i: (0,) * len(shape))
    return _call(
        body, name=name, grid=(1,),
        in_specs=[vm((S, 256)), pl.BlockSpec((S, 256), lambda i: (0, 6)), vm((S, 256)), pl.BlockSpec(memory_space=pl.ANY),
                  vm((2, 8, 128)), vm((256, 1024)), vm((256, 1024)), vm((1024, 256)), vm((1024, 256)), vm((1, 256)),
                  vm((NDEV, 32, 256)), vm((1, 256))],
        out_specs=(vm((S, 256)), vm((256, 1024)), vm((256, 1024)), vm((1024, 256)), vm((1024, 256)),
                   vm((2, 8, 128)), vm((1, 256)), vm((256, 256)), vm((1, 256))),
        out_shape=(_sds((S, 256), F32), _sds((256, 1024), F32), _sds((256, 1024), F32), _sds((1024, 256), F32),
                   _sds((1024, 256), F32), _sds((2, 8, 128), F32), _sds((1, 256), F32), _sds((256, 256), F32),
                   _sds((1, 256), F32)),
        scratch=[pltpu.VMEM((2, S * 8, 128), F32), pltpu.VMEM((2, S * 8, 128), F32)],
        cp=_cp(56, ("arbitrary",)), args=(dy, z, ypre, st, a, bre, bim, cre, cim, dsk, gluw, glub))[0]


def _ssm_discretise(a_re, a_im, log_dt, b_re, b_im):
    dt = jnp.exp(log_dt)[:, None]
    er = jnp.exp(a_re * dt)
    abr, abi = er * jnp.cos(a_im * dt), er * jnp.sin(a_im * dt)
    den = a_re * a_re + a_im * a_im
    fr = ((abr - 1.0) * a_re + abi * a_im) / den
    fi = (abi * a_re - (abr - 1.0) * a_im) / den
    bbr = fr[:, :, None] * b_re - fi[:, :, None] * b_im
    bbi = fr[:, :, None] * b_im + fi[:, :, None] * b_re
    return abr, abi, bbr, bbi


def _blockdiag(t):
    g, r, c = t.shape
    eye = jnp.eye(g, dtype=t.dtype)
    return (t[:, :, None, :] * eye[:, None, :, None]).reshape(g * r, g * c)


def _blockdiag_take(m, r, c):
    g = m.shape[0] // r
    idx = jnp.arange(g)
    return m.reshape(g, r, g, c)[idx, :, idx, :]


PAD = 16


def _pool_lane_select(vals):
    lane = lax.broadcasted_iota(jnp.int32, vals[0].shape, 1)
    out = vals[3]
    for g in (2, 1, 0):
        out = jnp.where(lane < 64 * (g + 1), vals[g], out)
    return out


def _pool_counts():
    row = lax.broadcasted_iota(jnp.int32, (S, D_POOL), 0).astype(F32) + 1.0
    return _pool_lane_select([jnp.minimum(row, float(w)) for w in POOL_WINDOWS])


def _pooled(u, sa, sb):
    sums = []
    cur = u
    bufs = (sa, sb)
    for k, sh in enumerate((1, 2, 4, 8)):
        buf = bufs[k % 2]
        buf[PAD:PAD + S, :] = cur
        cur = cur + buf[PAD - sh:PAD - sh + S, :]
        sums.append(cur)
    return _pool_lane_select(sums) / _pool_counts() - u


def pool_fwd(z, pw, psc, name):
    def body(u_ref, w_ref, s_ref, y_ref, sa, sb):
        for buf in (sa, sb):
            buf[0:PAD, :] = jnp.zeros((PAD, D_POOL), F32)
        pooled = _pooled(u_ref[...], sa, sb)
        y_ref[...] = _mm(pooled, w_ref[...]) * s_ref[...]

    vm = lambda shape: pl.BlockSpec(shape, lambda i: (0,) * len(shape))
    return _pallas_call(
        body, name=name, grid=(1,),
        in_specs=[pl.BlockSpec((S, 256), lambda i: (0, 7)), vm((256, 256)), vm((1, 256))],
        out_specs=vm((S, 256)), out_shape=_sds((S, 256), F32),
        scratch_shapes=[pltpu.VMEM((S + 2 * PAD, D_POOL), F32)] * 2,
        compiler_params=_cp(40, ("arbitrary",)))(z, pw, psc)


def pool_bwd(dy, z, pw, psc, name):
    def body(dy_ref, u_ref, w_ref, s_ref, du_ref, dw_ref, ds_ref, sa, sb):
        for buf in (sa, sb):
            buf[0:PAD, :] = jnp.zeros((PAD, D_POOL), F32)
            buf[PAD + S:PAD + S + PAD, :] = jnp.zeros((PAD, D_POOL), F32)
        pooled = _pooled(u_ref[...], sa, sb)
        dyv = dy_ref[...]
        w = w_ref[...]
        ds_ref[...] = jnp.sum(dyv * _mm(pooled, w), axis=0, keepdims=True)
        dyl = dyv * s_ref[...]
        dw_ref[...] = _mm_tn(pooled, dyl)
        dpool = _mm_nt(dyl, w)
        cur = dpool / _pool_counts()
        sums = []
        bufs = (sa, sb)
        for k, sh in enumerate((1, 2, 4, 8)):
            buf = bufs[k % 2]
            buf[PAD:PAD + S, :] = cur
            cur = cur + buf[PAD + sh:PAD + sh + S, :]
            sums.append(cur)
        du_ref[...] = _pool_lane_select(sums) - dpool

    vm = lambda shape: pl.BlockSpec(shape, lambda i: (0,) * len(shape))
    return _pallas_call(
        body, name=name, grid=(1,),
        in_specs=[vm((S, 256)), pl.BlockSpec((S, 256), lambda i: (0, 7)), vm((256, 256)), vm((1, 256))],
        out_specs=(vm((S, 256)), vm((256, 256)), vm((1, 256))),
        out_shape=(_sds((S, 256), F32), _sds((256, 256), F32), _sds((1, 256), F32)),
        scratch_shapes=[pltpu.VMEM((S + 2 * PAD, D_POOL), F32)] * 2,
        compiler_params=_cp(40, ("arbitrary",)))(dy, z, pw, psc)


def ada_fwd(c_all, ada_w, ada_b_cols):
    def body(c_ref, w_ref, b_ref, o_ref):
        c = c_ref[...]
        cond = c * jax.nn.sigmoid(c)
        o_ref[...] = jnp.dot(cond, w_ref[...], preferred_element_type=F32, precision=lax.Precision.HIGHEST) + b_ref[...]

    return _pallas_call(
        body, name="ada_fwd", grid=(DEPTH,),
        in_specs=[pl.BlockSpec((NDEV, D), lambda l: (0, 0)), pl.BlockSpec((None, D, 1152), lambda l: (l, 0, 0)),
                  pl.BlockSpec((None, 1, 1152), lambda l: (l, 0, 0))],
        out_specs=pl.BlockSpec((None, NDEV, 1152), lambda l: (l, 0, 0)), out_shape=_sds((DEPTH, NDEV, 1152), F32),
        compiler_params=_cp(40, ("arbitrary",)))(c_all, ada_w, ada_b_cols)


def ada_bwd(c_all, dmod_cols):
    def body(c_ref, dm_ref, o_ref):
        c = c_ref[...]
        cond = c * jax.nn.sigmoid(c)
        o_ref[...] = lax.dot_general(cond, dm_ref[...], (((0,), (0,)), ((), ())), preferred_element_type=F32,
                                     precision=lax.Precision.HIGHEST)

    return _pallas_call(
        body, name="ada_bwd", grid=(DEPTH,),
        in_specs=[pl.BlockSpec((NDEV, D), lambda l: (0, 0)), pl.BlockSpec((None, NDEV, 1152), lambda l: (l, 0, 0))],
        out_specs=pl.BlockSpec((None, D, 1152), lambda l: (l, 0, 0)), out_shape=_sds((DEPTH, D, 1152), F32),
        compiler_params=_cp(40, ("arbitrary",)))(c_all, dmod_cols)


def _adamw(w, g, m, v):
    m2 = B1 * m + (1.0 - B1) * g
    v2 = B2 * v + (1.0 - B2) * (g * g)
    m_hat = m2 / (1.0 - B1 ** STEP)
    v_hat = v2 / (1.0 - B2 ** STEP)
    return -LR * (m_hat / (jnp.sqrt(v_hat) + EPS) + WD * w), m2, v2


def _sum8(ref):
    g = ref[0].astype(F32)
    for s in range(1, ref.shape[0]):
        g = g + ref[s].astype(F32)
    return g


def adam_rs(recv, w, m, v, tr, name):
    lead, (r, cdim) = w.shape[:-2], w.shape[-2:]
    cp = recv.shape[-1]
    nl = len(lead)

    def body(rc_ref, w_ref, m_ref, v_ref, g_ref, d_ref, m2_ref, v2_ref):
        g = _sum8(rc_ref)[:, :cdim]
        g_ref[...] = g
        d_ref[...], m2_ref[...], v2_ref[...] = _adamw(w_ref[...], g, m_ref[...], v_ref[...])

    rs = pl.BlockSpec((None,) * nl + (tr, cdim), lambda *i: (*i, 0))
    return _call(
        body, name=name, grid=lead + (r // tr,),
        in_specs=[pl.BlockSpec((NDEV,) + (None,) * nl + (tr, cp), lambda *i: (0, *i, 0)), rs, rs, rs],
        out_specs=(rs, rs, rs, rs), out_shape=tuple(_sds(w.shape, F32) for _ in range(4)),
        cp=_cp(48, ("arbitrary",) * (nl + 1)), args=(recv, w, m, v))[0]


def adam_block(recv, w, m, v, lf, prev, name):
    half = FB // 2

    def body(*refs):
        rc_ref, w_ref, m_ref, v_ref = refs[:4]
        g_ref, d_ref, m2_ref, v2_ref = refs[-4:]
        g = _sum8(rc_ref)
        g_ref[...] = g
        d_ref[...], m2_ref[...], v2_ref[...] = _adamw(w_ref[...], g, m_ref[...], v_ref[...])

    rs = pl.BlockSpec((None, None, half, D), lambda i: (lf // 2, lf % 2, i, 0))
    prev = list(prev) if prev is not None else []
    return list(_pallas_call(
        body, name=name, grid=(2,), in_specs=[pl.BlockSpec((recv.shape[0], half, D), lambda i: (0, i, 0)), rs, rs, rs] + [ANY] * len(prev),
        out_specs=(rs, rs, rs, rs), out_shape=tuple(_sds((DEPTH, 2, FB, D), F32) for _ in range(4)),
        input_output_aliases={4 + k: k for k in range(len(prev))},
        compiler_params=_cp(48, ("arbitrary",)))(recv, w, m, v, *prev))


def adam_plain(g, w, m, v, tr, name):
    lead, (r, cdim) = w.shape[:-2], w.shape[-2:]
    nl = len(lead)

    def body(g_ref, w_ref, m_ref, v_ref, d_ref, m2_ref, v2_ref):
        d_ref[...], m2_ref[...], v2_ref[...] = _adamw(w_ref[...], g_ref[...], m_ref[...], v_ref[...])

    rs = pl.BlockSpec((None,) * nl + (tr, cdim), lambda *i: (*i, 0))
    return _call(
        body, name=name, grid=lead + (r // tr,), in_specs=[rs, rs, rs, rs], out_specs=(rs, rs, rs),
        out_shape=tuple(_sds(w.shape, F32) for _ in range(3)),
        cp=_cp(48, ("arbitrary",) * (nl + 1)), args=(g, w, m, v))[0]


def adam_native(gs, ws, ms, vs, name):
    n = len(ws)

    def body(*refs):
        g_refs, w_refs, m_refs, v_refs = (refs[k * n:(k + 1) * n] for k in range(4))
        d_refs, m2_refs, v2_refs = (refs[(4 + k) * n:(5 + k) * n] for k in range(3))
        for a in range(n):
            d_refs[a][...], m2_refs[a][...], v2_refs[a][...] = _adamw(w_refs[a][...], g_refs[a][...], m_refs[a][...], v_refs[a][...])

    whole = lambda a: pl.BlockSpec(a.shape, lambda i, nd=len(a.shape): (0,) * nd)
    outs = _pallas_call(body, name=name, grid=(1,), in_specs=[whole(a) for a in (*gs, *ws, *ms, *vs)],
                        out_specs=tuple(whole(w) for w in ws) * 3,
                        out_shape=tuple(_sds(w.shape, F32) for w in ws) * 3, compiler_params=_cp(40))(*gs, *ws, *ms, *vs)
    return outs[:n], outs[n:2 * n], outs[2 * n:]


def sum_sources(recv, name):
    r = recv.shape[1]

    def body(rc_ref, o_ref):
        o_ref[...] = _sum8(rc_ref)

    vm = pl.BlockSpec(memory_space=pltpu.VMEM)
    return _pallas_call(body, name=name, in_specs=[vm], out_specs=vm, out_shape=_sds((r, 128), F32),
                          compiler_params=_cp(40))(recv)


def _pack(arrs, dtype=F32):
    flat = jnp.concatenate([a.reshape(-1) for a in arrs]).astype(dtype)
    n = flat.shape[0]
    tile = 128 * (32 // jnp.dtype(dtype).itemsize)
    rows = -(-n // tile) * (tile // 128)
    return jnp.pad(flat, (0, rows * 128 - n)).reshape(rows, 128)


def _unpack(vec, shapes):
    flat = vec.reshape(-1)
    out, o = [], 0
    for sh in shapes:
        n = int(np.prod(sh))
        out.append(flat[o:o + n].reshape(sh))
        o += n
    return out


WEIGHTS = ['rel_bias', 'ada_w', 'ada_b', 'ln_g', 'ln_b', 'ffn_w_gate', 'ffn_w_up', 'ffn_w_down', 'w_in', 'w_out',
           'ssm_a_re', 'ssm_a_im', 'ssm_log_dt', 'ssm_b_re', 'ssm_b_im', 'ssm_c_re', 'ssm_c_im', 'ssm_d', 'glu_w',
           'glu_b', 'pool_w', 'pool_scale']
SMALL = ['rel_bias', 'ada_b', 'ln_g', 'ln_b', 'ssm_a_re', 'ssm_a_im', 'ssm_log_dt', 'ssm_b_re', 'ssm_b_im',
         'ssm_c_re', 'ssm_c_im', 'ssm_d', 'glu_b', 'pool_w', 'pool_scale']
SMALL_EXACT = ['rel_bias', 'ln_g', 'ln_b', 'ssm_a_re', 'ssm_a_im', 'ssm_log_dt']
SMALL_ROUNDED = ['ada_b', 'ssm_b_re', 'ssm_b_im', 'ssm_c_re', 'ssm_c_im', 'ssm_d', 'glu_b', 'pool_w', 'pool_scale']
SMALL_FULL_SHAPES = {'rel_bias': (32, 8), 'ada_b': (2, 9216), 'ln_g': (2, 3, 1024), 'ln_b': (2, 3, 1024),
                     'ssm_a_re': (2, 16, 64), 'ssm_a_im': (2, 16, 64), 'ssm_log_dt': (2, 16),
                     'ssm_b_re': (2, 16, 64, 16), 'ssm_b_im': (2, 16, 64, 16), 'ssm_c_re': (2, 16, 16, 64),
                     'ssm_c_im': (2, 16, 16, 64), 'ssm_d': (2, 256), 'glu_b': (2, 256), 'pool_w': (2, 4, 64, 64),
                     'pool_scale': (2, 256)}


def _step(P):
    me = _me()
    x0 = P['x'][0]
    target = P['loss_target'][0]

    bf = lambda a: a.astype(BF16)
    padr = lambda a: jnp.pad(bf(a), ((0, 0), (0, 0), (0, FBP - FB), (0, 0)))
    ffn_b = [padr(jnp.swapaxes(P['ffn_w_gate'], 2, 3)), padr(jnp.swapaxes(P['ffn_w_up'], 2, 3)), padr(P['ffn_w_down'])]
    mix_b = [bf(P['w_in']), bf(P['w_out']), bf(P['glu_w'])]

    def shards(l, sub):
        return [t[l] for t in mix_b] if sub == 1 else [t[l, sub // 2] for t in ffn_b]

    order = [(l, sub) for l in range(DEPTH) for sub in range(3)]
    nxt = dict(zip(order[:-1], order[1:]))
    W = {key: [None] * 3 for key in order}
    c_all, lng_all, lnb_all, *W[order[0]] = _exchange(Gather([P['c'], P['ln_g'], P['ln_b']] + shards(*order[0])), "gather_first")
    gather_queue = [(key, pos, a) for key in order[1:] for pos, a in enumerate(shards(*key))]

    def gather_ride(cap_us, must=None):
        units, used = [], 0.0
        while gather_queue:
            key, _, a = gather_queue[0]
            cost = a.size * a.dtype.itemsize * GATHER_US_PER_BYTE
            if key != must and used + cost / 2 > cap_us:
                break
            units.append(gather_queue.pop(0))
            used += cost
        return (Gather([a for _, _, a in units]) if units else None), units

    def gathered(units, outs):
        for (key, pos, _), o in zip(units, outs):
            W[key][pos] = o

    c_all = c_all.reshape(NDEV, D)
    ln_g = jnp.transpose(lng_all, (1, 2, 0, 3)).reshape(DEPTH, 3, D)
    ln_b = jnp.transpose(lnb_all, (1, 2, 0, 3)).reshape(DEPTH, 3, D)

    ada_b_cols = lax.dynamic_slice_in_dim(P['ada_b'], me * 1152, 1152, axis=1).reshape(DEPTH, 1, 1152)
    modc = ada_fwd(c_all, P['ada_w'], ada_b_cols)
    (mod_all,) = _exchange(Gather([modc]), "gather_mod")
    mod_me = lax.dynamic_index_in_dim(mod_all, me, axis=2, keepdims=False)
    mod = jnp.transpose(mod_me, (1, 0, 2)).reshape(DEPTH, 9, D)

    bias = att_bias(P['rel_bias'])
    ssm = []
    for l in range(DEPTH):
        prm = (P['ssm_a_re'][l], P['ssm_a_im'][l], P['ssm_log_dt'][l], P['ssm_b_re'][l], P['ssm_b_im'][l])
        (abr, abi, bbr, bbi), disc_vjp = jax.vjp(_ssm_discretise, *prm)
        ssm.append(dict(
            vjp=disc_vjp, a=jnp.stack([abr.reshape(8, 128), abi.reshape(8, 128)]),
            bre=_blockdiag(jnp.transpose(bbr, (0, 2, 1))).astype(MXU), bim=_blockdiag(jnp.transpose(bbi, (0, 2, 1))).astype(MXU),
            cre=_blockdiag(jnp.transpose(P['ssm_c_re'][l], (0, 2, 1))).astype(MXU),
            cim=_blockdiag(jnp.transpose(P['ssm_c_im'][l], (0, 2, 1))).astype(MXU),
            d=P['ssm_d'][l].reshape(1, 256), gb=P['glu_b'][l].reshape(1, 256),
            pw=_blockdiag(P['pool_w'][l]).astype(MXU), psc=P['pool_scale'][l].reshape(1, 256)))

    saved = []
    x = x0
    h = ln_mod_fwd(x, mod[0], 0, "ln_mod_fwd_l0s0")
    for l, sub in order:
        tag = f"l{l}s{sub}"
        after = (mod[nxt[(l, sub)][0]], nxt[(l, sub)][1]) if (l, sub) in nxt else None
        if sub != 1:
            wg, wu, wd = (t.reshape(NDEV * FBP, D) for t in W[(l, sub)])
            ride, units = gather_ride(60, nxt.get((l, sub)))
            (G, U, fo), got = ffn_fwd(h, wg, wu, wd, "ffn_fwd_" + tag, ride)
            gathered(units, got)
            saved.append(dict(x=x, h=h, G=G, U=U, f=fo))
            x, *hn = res_ln_fwd(x, fo, mod[l], sub, ln_g[l], ln_b[l], 0.5, "res_ln_fwd_" + tag, after)
        else:
            sp = ssm[l]
            win, wout, gluw = W[(l, sub)]
            ride, units = gather_ride(15)
            (z,), got = win_fwd(h, win, "win_fwd_" + tag, ride)
            gathered(units, got)
            ride, units = gather_ride(55)
            (ya, lse), got = att_fwd(z, bias, "att_fwd_" + tag, ride)
            gathered(units, got)
            ride, units = gather_ride(35)
            (ys, ypre, st), got = ssm_fwd(z, sp['a'], sp['bre'], sp['bim'], sp['cre'], sp['cim'], sp['d'], gluw, sp['gb'],
                                          "ssm_fwd_" + tag, ride)
            gathered(units, got)
            yp = pool_fwd(z, sp['pw'], sp['psc'], "pool_fwd_" + tag)
            ride, units = gather_ride(12, nxt.get((l, sub)))
            (o,), got = wout_fwd(ya, ys, yp, wout, "wout_fwd_" + tag, ride)
            gathered(units, got)
            saved.append(dict(x=x, h=h, z=z, ya=ya, lse=lse, ys=ys, ypre=ypre, st=st, yp=yp, f=o))
            x, *hn = res_ln_fwd(x, o, mod[l], sub, ln_g[l], ln_b[l], 1.0, "res_ln_fwd_" + tag, after)
        h = hn[0] if hn else None
    assert not gather_queue

    loss_tile, dx = loss_fwd_bwd(x, target, "loss")
    loss = lax.psum(loss_tile[0, 0], ("x", "y", "c"))

    flights = []

    dmod = [[None] * 9 for _ in range(DEPTH)]
    dlng = [[None] * 3 for _ in range(DEPTH)]
    dlnb = [[None] * 3 for _ in range(DEPTH)]
    dbiases = [None] * DEPTH
    small_l = [dict() for _ in range(DEPTH)]
    for l, sub in reversed(order):
        tag = f"l{l}s{sub}"
        sv = saved[3 * l + sub]
        if (l, sub) == order[-1]:
            dxa, df, sums = res_ln_bwd(sv['x'], sv['f'], mod[l], sub, ln_g[l], dx, 0.5, "res_ln_bwd_" + tag)
        dlng[l][sub], dlnb[l][sub], dmod[l][3 * sub + 2] = sums[0], sums[1], sums[2]
        if sub != 1:
            f = sub // 2
            wg, wu, wd = (t.reshape(NDEV * FBP, D) for t in W[(l, sub)])
            dwg, dwu, dwd, dh = ffn_bwd(df, sv['h'], sv['G'], sv['U'], wg, wu, wd, "ffn_bwd_" + tag)
            handle, zero = scatter_start([t.reshape(NDEV, FBP, D) for t in (dwg, dwu, dwd)], "scatter_start_" + tag, rows=FB)
            flights.append(((l, sub), handle))
        else:
            sp = ssm[l]
            win, wout, gluw = W[(l, sub)]
            dya, dys, dyp, dwout = wout_bwd(df, sv['ya'], sv['ys'], sv['yp'], wout, "wout_bwd_" + tag)
            dq, dk, dv, dbiases[l] = att_bwd(sv['z'], bias, sv['ya'], sv['lse'], dya, "att_bwd_" + tag)
            dus, dbre, dbim, dcre, dcim, da, dd, dgw, dgb = ssm_bwd(
                dys, sv['z'], sv['ypre'], sv['st'], sp['a'], sp['bre'], sp['bim'], sp['cre'], sp['cim'], sp['d'],
                gluw, sp['gb'], "ssm_bwd_" + tag)
            dup, dpw, dpsc = pool_bwd(dyp, sv['z'], sp['pw'], sp['psc'], "pool_bwd_" + tag)
            dh, dwin = win_bwd((dq, dk, dv, dus, dup), sv['h'], win, "win_bwd_" + tag)
            handle, zero = scatter_start([dwin, dwout, dgw.astype(BF16).reshape(NDEV, 32, 256)], "scatter_start_" + tag)
            flights.append(((l, sub), handle))
            d_are, d_aim, d_ldt, d_bre, d_bim = sp['vjp']((
                da[0].reshape(16, 64), da[1].reshape(16, 64),
                jnp.transpose(_blockdiag_take(dbre, 16, 64), (0, 2, 1)), jnp.transpose(_blockdiag_take(dbim, 16, 64), (0, 2, 1))))
            small_l[l] = dict(
                ssm_a_re=d_are, ssm_a_im=d_aim, ssm_log_dt=d_ldt, ssm_b_re=d_bre, ssm_b_im=d_bim,
                ssm_c_re=jnp.transpose(_blockdiag_take(dcre, 64, 16), (0, 2, 1)),
                ssm_c_im=jnp.transpose(_blockdiag_take(dcim, 64, 16), (0, 2, 1)),
                ssm_d=dd.reshape(256), glu_b=dgb.reshape(256), pool_w=_blockdiag_take(dpw, 64, 64), pool_scale=dpsc.reshape(256))
        if (l, sub) == order[0]:
            dx, sums2 = ln_mod_bwd(sv['x'], dh, mod[l] + zero, sub, dxa, "ln_mod_bwd_" + tag)
        else:
            lp, sp_ = order[order.index((l, sub)) - 1]
            svp = saved[3 * lp + sp_]
            dxa, df, sums, sums2 = ln_join_bwd(svp['x'], svp['f'], mod[lp], sp_, ln_g[lp], ln_b[lp], 1.0 if sp_ == 1 else 0.5,
                                               dh, mod[l] + zero, sub, dxa, "ln_join_bwd_" + tag)
        dmod[l][3 * sub], dmod[l][3 * sub + 1] = sums2[0], sums2[1]
    grad_x = dx[None]

    small = {k: jnp.stack([small_l[l][k] for l in range(DEPTH)]) for k in small_l[0]}
    small['rel_bias'] = relbias_grad(dbiases)
    small['ada_b'] = jnp.stack([jnp.stack(dmod[l]).reshape(9 * D) for l in range(DEPTH)])
    small['ln_g'] = jnp.stack([jnp.stack(dlng[l]) for l in range(DEPTH)])
    small['ln_b'] = jnp.stack([jnp.stack(dlnb[l]) for l in range(DEPTH)])
    swaps = {'rel_bias': (0, 1), 'ln_g': (0, 1), 'ln_b': (0, 1), 'ssm_b_re': (2, 3), 'ssm_b_im': (2, 3)}
    view = lambda k, t: jnp.swapaxes(t, *swaps[k]) if k in swaps else t
    kept_shape = lambda k: np.swapaxes(np.empty(SMALL_FULL_SHAPES[k], np.bool_), *swaps.get(k, (0, 0))).shape
    small_flight, _ = scatter_start([_pack([view(k, small[k]) for k in SMALL_EXACT]),
                                     _pack([view(k, small[k]) for k in SMALL_ROUNDED], BF16)], "gather_small_start", whole=True)

    out = {}

    def put(name, g, d, m2, v2, shape):
        out['grad_' + name], out['delta_' + name] = g.reshape(shape), d.reshape(shape)
        out['new_m_' + name], out['new_v_' + name] = m2.reshape(shape), v2.reshape(shape)

    def wmv(name):
        return [P[pre + name] for pre in ('', 'm_', 'v_')]

    recv = {}
    started_last = small_flight[1][0]
    for key, handle in flights[:-1]:
        recv[key] = scatter_wait(handle, started_last, "scatter_wait_l%ds%d" % key)
    mixer_done = []
    for pos, (name, tr) in enumerate((('w_in', 512), ('w_out', 128), ('glu_w', 32))):
        both = jnp.stack([recv[(l, 1)][pos] for l in range(DEPTH)], axis=1)
        res = adam_rs(both, *wmv(name), tr, "adam_" + name)
        mixer_done.append(res[0])
        put(name, *res, P[name].shape)
    ffn = (('ffn_w_gate', [jnp.swapaxes(t, 2, 3) for t in wmv('ffn_w_gate')]),
           ('ffn_w_up', [jnp.swapaxes(t, 2, 3) for t in wmv('ffn_w_up')]), ('ffn_w_down', wmv('ffn_w_down')))
    part = [None] * 3
    for l, sub in [key for key, _ in flights[:-1] if key[1] != 1]:
        for pos, (name, ops) in enumerate(ffn):
            part[pos] = adam_block(recv[(l, sub)][pos], *ops, 2 * l + sub // 2, part[pos], f"adam_{name}_l{l}s{sub}")

    (l, sub), handle = flights[-1]
    last = scatter_wait(handle, [p[0] for p in part] + mixer_done[:2], "scatter_wait_l%ds%d" % (l, sub))
    for pos, (name, ops) in enumerate(ffn):
        part[pos] = adam_block(last[pos], *ops, 2 * l + sub // 2, part[pos], f"adam_{name}_l{l}s{sub}")
        put(name, *([jnp.swapaxes(t, 2, 3) for t in part[pos]] if pos < 2 else part[pos]), P[name].shape)
    exact_all, rounded_all = scatter_wait(small_flight, [p[0] for p in part], "gather_small_wait")
    gsum = dict(zip(SMALL_EXACT, _unpack(sum_sources(exact_all, "sum_small_exact"), [kept_shape(k) for k in SMALL_EXACT])))
    gsum.update(zip(SMALL_ROUNDED, _unpack(sum_sources(rounded_all, "sum_small_rounded"), [kept_shape(k) for k in SMALL_ROUNDED])))
    dmod_cols = jnp.stack([lax.dynamic_slice_in_dim(rounded_all, 72 * l + 9 * me, 9, axis=1).astype(F32).reshape(NDEV, 1152)
                           for l in range(DEPTH)])
    g_ada_w = ada_bwd(c_all, dmod_cols)

    put('ada_w', g_ada_w, *adam_plain(g_ada_w, *wmv('ada_w'), 256, "adam_ada_w"), P['ada_w'].shape)

    for k in ('ln_g', 'ln_b'):
        gsum[k] = lax.dynamic_slice_in_dim(gsum[k], me * 128, 128, axis=2)
    ds_, m2s, v2s = adam_native([gsum[k] for k in SMALL], *[[view(k, P[pre + k]) for k in SMALL] for pre in ('', 'm_', 'v_')],
                                "adam_small")
    for k, d, m2, v2 in zip(SMALL, ds_, m2s, v2s):
        put(k, view(k, gsum[k]), view(k, d), view(k, m2), view(k, v2), P[k].shape)

    res = [loss, grad_x]
    for pre in ('grad_', 'delta_', 'new_m_', 'new_v_'):
        res += [out[pre + k] for k in WEIGHTS]
    return tuple(res)


def kernel(x, c, rel_bias, ada_w, ada_b, ln_g, ln_b, ffn_w_gate, ffn_w_up, ffn_w_down, w_in, w_out, ssm_a_re, ssm_a_im, ssm_log_dt, ssm_b_re, ssm_b_im, ssm_c_re, ssm_c_im, ssm_d, glu_w, glu_b, pool_w, pool_scale, loss_target, m_rel_bias, m_ada_w, m_ada_b, m_ln_g, m_ln_b, m_ffn_w_gate, m_ffn_w_up, m_ffn_w_down, m_w_in, m_w_out, m_ssm_a_re, m_ssm_a_im, m_ssm_log_dt, m_ssm_b_re, m_ssm_b_im, m_ssm_c_re, m_ssm_c_im, m_ssm_d, m_glu_w, m_glu_b, m_pool_w, m_pool_scale, v_rel_bias, v_ada_w, v_ada_b, v_ln_g, v_ln_b, v_ffn_w_gate, v_ffn_w_up, v_ffn_w_down, v_w_in, v_w_out, v_ssm_a_re, v_ssm_a_im, v_ssm_log_dt, v_ssm_b_re, v_ssm_b_im, v_ssm_c_re, v_ssm_c_im, v_ssm_d, v_glu_w, v_glu_b, v_pool_w, v_pool_scale):
    return _step(dict(locals()))
```

```python
import functools
import math

import numpy as np
import jax
import jax.numpy as jnp
from jax import lax
from jax.experimental import pallas as pl
from jax.experimental.pallas import tpu as pltpu

F32 = jnp.float32
BF16 = jnp.bfloat16
MXU = jnp.bfloat16

S = 2048
D = 1024
NDEV = 8
DEPTH = 2
D_ATT, D_SSM, D_POOL, D_IN = 512, 256, 256, 2048
N_HEADS = 8
FB = 352
FBP = 384
QB = 128
PATTERNS = ((128, 1), (512, 4), (2048, 16))
POOL_WINDOWS = (2, 4, 8, 16)
N_BUCKETS, MAX_DISTANCE = 32, 2048
ALPHA = (2 * DEPTH) ** 0.25
LN_EPS = 1e-5
NEG = -1e30
GATHER_US_PER_BYTE = 43e-6
LR, B1, B2, EPS, WD, STEP = 0.001, 0.9, 0.999, 1e-08, 0.01, 10

TM = 256
TMM = 512
MIB = 1024 * 1024


def _cp(vmem_mib, sem=None):
    kw = dict(vmem_limit_bytes=vmem_mib * MIB)
    if sem is not None:
        kw["dimension_semantics"] = sem
    return pltpu.CompilerParams(**kw)


def _sds(shape, dtype):
    return jax.ShapeDtypeStruct(shape, dtype)


def _mm(a, b):
    return jnp.dot(a.astype(MXU), b.astype(MXU), preferred_element_type=F32)


def _mm_nt(a, b):
    return lax.dot_general(a.astype(MXU), b.astype(MXU), (((1,), (1,)), ((), ())), preferred_element_type=F32)


def _mm_tn(a, b):
    return lax.dot_general(a.astype(MXU), b.astype(MXU), (((0,), (0,)), ((), ())), preferred_element_type=F32)


def _ln_stats(x):
    mu = jnp.mean(x, axis=-1, keepdims=True)
    xc = x - mu
    var = jnp.mean(xc * xc, axis=-1, keepdims=True)
    rstd = lax.rsqrt(var + LN_EPS)
    return xc * rstd, rstd


def _ln_bwd(dn, n, rstd):
    return rstd * (dn - jnp.mean(dn, axis=-1, keepdims=True) - n * jnp.mean(dn * n, axis=-1, keepdims=True))


def _me():
    return 4 * lax.axis_index("x") + 2 * lax.axis_index("y") + lax.axis_index("c")


ANY = pl.BlockSpec(memory_space=pl.ANY)
PIN_BYTES = 1 << 19


def _pallas_call(*a, **k):
    pin_all = k.pop("pin_all", False)
    big = lambda o: pin_all or math.prod(o.shape) * o.dtype.itemsize >= PIN_BYTES
    pin = lambda o: pltpu.HBM(o.shape, o.dtype) if isinstance(o, jax.ShapeDtypeStruct) and big(o) else o
    osh = k["out_shape"]
    k["out_shape"] = tuple(pin(o) for o in osh) if isinstance(osh, (tuple, list)) else pin(osh)
    fn = pl.pallas_call(*a, **k)

    def run(*args):
        return fn(*[pltpu.with_memory_space_constraint(x, pltpu.HBM) if big(x) else x for x in args])
    return run


class Gather:
    def __init__(self, srcs):
        self.srcs = list(srcs)
        self.n = len(self.srcs)
        self.bufs = []
        self.out_shapes = [_sds((NDEV,) + a.shape, a.dtype) for a in self.srcs]
        self.sems = [pltpu.SemaphoreType.DMA((7 * self.n,)), pltpu.SemaphoreType.DMA((7 * self.n,)),
                     pltpu.SemaphoreType.DMA((self.n,))]

    def _parts(self, srcs, outs, sems):
        send_sems, recv_sems, loc_sems = sems
        x, y, c = lax.axis_index("x"), lax.axis_index("y"), lax.axis_index("c")
        me, sib = (x, y, c), (x, y, 1 - c)
        chips = [(1 - x, y), (x, 1 - y), (1 - x, 1 - y)]
        slot = lambda d: 4 * d[0] + 2 * d[1] + d[2]

        def copy(a, k, block, to, src=None):
            dst = outs[a].at[slot(block)]
            return pltpu.make_async_remote_copy(
                src_ref=dst if src is None else src, dst_ref=dst,
                send_sem=send_sems.at[7 * a + k], recv_sem=recv_sems.at[7 * a + k],
                device_id=to, device_id_type=pl.DeviceIdType.MESH)

        local = [pltpu.make_async_copy(srcs[a], outs[a].at[slot(me)], loc_sems.at[a]) for a in range(self.n)]
        return me, sib, chips, c, copy, local

    def start(self, srcs, bufs, outs, sems):
        me, sib, chips, c, copy, local = self._parts(srcs, outs, sems)
        for a in range(self.n):
            local[a].start()
            copy(a, 0, me, sib, src=srcs[a]).start()
            for j, chip in enumerate(chips):
                copy(a, 1 + j, me, (*chip, c), src=srcs[a]).start()

    def finish(self, srcs, bufs, outs, sems):
        me, sib, chips, c, copy, local = self._parts(srcs, outs, sems)
        for a in range(self.n):
            for j, chip in enumerate(chips):
                copy(a, 1 + j, (*chip, c), me).wait_recv()
                copy(a, 4 + j, (*chip, c), sib).start()
        for a in range(self.n):
            copy(a, 0, sib, me).wait_recv()
            copy(a, 0, me, sib, src=srcs[a]).wait_send()
            for j, chip in enumerate(chips):
                copy(a, 4 + j, (*chip, 1 - c), me).wait_recv()
                copy(a, 1 + j, me, (*chip, c), src=srcs[a]).wait_send()
                copy(a, 4 + j, (*chip, c), sib).wait_send()
            local[a].wait()


def _call(body, *, name, grid, in_specs, out_specs, out_shape, args, scratch=(), cp=None, ride=None):
    out_specs, out_shape, scratch = list(out_specs), list(out_shape), list(scratch)
    if ride is None:
        outs = _pallas_call(body, name=name, grid=grid, in_specs=list(in_specs), out_specs=tuple(out_specs),
                              out_shape=tuple(out_shape), scratch_shapes=scratch, compiler_params=cp)(*args)
        return list(outs), []
    nin, nout, nscr, n, nb, no = len(in_specs), len(out_specs), len(scratch), ride.n, len(ride.bufs), len(ride.out_shapes)
    steps = list(grid)

    def wrapped(*refs):
        h_in, r_src, r_buf = refs[:nin], refs[nin:nin + n], refs[nin + n:nin + n + nb]
        o0 = nin + n + nb
        h_out, r_out = refs[o0:o0 + nout], refs[o0 + nout:o0 + nout + no]
        s0 = o0 + nout + no
        h_scr, sems = refs[s0:s0 + nscr], refs[s0 + nscr:]
        ids = [pl.program_id(a) for a in range(len(steps))]
        first = functools.reduce(jnp.logical_and, [i == 0 for i in ids])
        last = functools.reduce(jnp.logical_and, [i == s - 1 for i, s in zip(ids, steps)])

        @pl.when(first)
        def _():
            ride.start(r_src, r_buf, r_out, sems)

        body(*h_in, *h_out, *h_scr)

        @pl.when(last)
        def _():
            ride.finish(r_src, r_buf, r_out, sems)

    aliases = {nin + n + k: nout + k for k in range(nb)}
    outs = _pallas_call(
        wrapped, name=name, grid=grid, in_specs=list(in_specs) + [ANY] * (n + nb),
        out_specs=tuple(out_specs + [ANY] * no), out_shape=tuple(out_shape + ride.out_shapes),
        scratch_shapes=scratch + ride.sems, input_output_aliases=aliases, compiler_params=cp,
    )(*args, *ride.srcs, *ride.bufs)
    return list(outs[:nout]), list(outs[nout:])


def _exchange(ride, name):
    def body(dummy_ref, o_ref):
        o_ref[...] = dummy_ref[...]

    one = pl.BlockSpec((8, 128), lambda i: (0, 0))
    _, outs = _call(body, name=name, grid=(1,), in_specs=[one], out_specs=[one], out_shape=[_sds((8, 128), F32)],
                    args=(jnp.zeros((8, 128), F32),), ride=ride)
    return outs


HBM = pl.BlockSpec(memory_space=pltpu.HBM)
SEM = pl.BlockSpec(memory_space=pltpu.SEMAPHORE)


def routes_all(me):
    return [(k, me ^ k, me, me ^ k) for k in range(NDEV)]


def _scatter_copies(srcs, lands, sems, sending, whole, rows):
    send_sems, recv_sems, loc_sems = sems
    me = _me()
    rts = routes_all(me)
    part = lambda ref, slab: ref if whole else ref.at[slab] if rows is None else ref.at[slab, pl.ds(0, rows)]
    remote_ix = [r for r, (k, _, _, _) in enumerate(rts) if k != 0]
    local_ix = [r for r, (k, _, _, _) in enumerate(rts) if k == 0]
    remote, local = [], []
    for a in range(len(srcs)):
        for n, r in enumerate(remote_ix):
            k, slab, there, here = rts[r]
            t = me ^ k
            sem = len(remote_ix) * a + n
            remote.append(pltpu.make_async_remote_copy(
                src_ref=part(srcs[a], slab), dst_ref=lands[a].at[there if sending else here], send_sem=send_sems.at[sem],
                recv_sem=recv_sems.at[sem], device_id=(t // 4, (t // 2) % 2, t % 2), device_id_type=pl.DeviceIdType.MESH))
        for n, r in enumerate(local_ix):
            _, slab, there, _ = rts[r]
            local.append(pltpu.make_async_copy(part(srcs[a], slab), lands[a].at[there], loc_sems.at[len(local_ix) * a + n]))
    return remote, local


def scatter_start(payloads, name, whole=False, rows=None):
    n = len(payloads)
    nr = NDEV - 1

    def body(*refs):
        srcs, lands, sems = refs[:n], refs[n:2 * n], refs[2 * n:2 * n + 3]
        remote, local = _scatter_copies(srcs, lands, sems, True, whole, rows)
        for cp in local + remote:
            cp.start()
        refs[-1][...] = jnp.zeros((8, 128), F32)

    thru = [pltpu.HBM(p.shape, p.dtype) for p in payloads]
    land_shapes = [(NDEV,) + p.shape if whole else p.shape if rows is None else (NDEV, rows) + p.shape[2:] for p in payloads]
    outs = pl.pallas_call(
        body, name=name,
        out_shape=(pltpu.SemaphoreType.DMA((nr * n,)), pltpu.SemaphoreType.DMA((nr * n,)), pltpu.SemaphoreType.DMA((n,)),
                   *thru, *[pltpu.HBM(sh, p.dtype) for sh, p in zip(land_shapes, payloads)], _sds((8, 128), F32)),
        in_specs=[HBM] * (2 * n),
        out_specs=(SEM, SEM, SEM, *[HBM] * (2 * n), pl.BlockSpec(memory_space=pltpu.VMEM)),
        input_output_aliases={i: 3 + i for i in range(2 * n)},
        compiler_params=pltpu.CompilerParams(has_side_effects=pltpu.SideEffectType.DATAFLOW_SIDE_EFFECTING),
    )(*[pltpu.with_memory_space_constraint(p, pltpu.HBM) for p in payloads],
      *[pltpu.with_memory_space_constraint(lax.empty(sh, p.dtype), pltpu.HBM) for sh, p in zip(land_shapes, payloads)])
    return (outs[:3], outs[3:3 + n], outs[3 + n:3 + 2 * n], whole, rows), outs[-1][0, 0]


def scatter_wait(handle, after, name):
    sems, srcs_thru, lands_thru, whole, rows = handle
    n = len(srcs_thru)
    after = list(after) if isinstance(after, (list, tuple)) else [after]

    def body(*refs):
        srcs, lands, sems_ = refs[:n], refs[n:2 * n], refs[2 * n:2 * n + 3]
        remote, local = _scatter_copies(srcs, lands, sems_, False, whole, rows)
        for cp in remote:
            cp.wait_send()
            cp.wait_recv()
        for cp in local:
            cp.wait()

    outs = pl.pallas_call(
        body, name=name, out_shape=tuple(pltpu.HBM(p.shape, p.dtype) for p in (*srcs_thru, *lands_thru)),
        in_specs=[HBM] * (2 * n) + [SEM] * 3 + [HBM] * len(after), out_specs=tuple([HBM] * (2 * n)),
        input_output_aliases={i: i for i in range(2 * n)},
        compiler_params=pltpu.CompilerParams(has_side_effects=pltpu.SideEffectType.DATAFLOW_SIDE_EFFECTING),
    )(*srcs_thru, *lands_thru, *sems, *[pltpu.with_memory_space_constraint(a, pltpu.HBM) for a in after])
    return list(outs[n:])


def _row_spec(cols, tm=TM):
    return pl.BlockSpec((tm, cols), lambda i: (i, 0))


def _full_spec(shape):
    nd = len(shape)
    return pl.BlockSpec(shape, lambda i: (0,) * nd)


def ln_mod_fwd(x, mod, sub, name):
    def body(x_ref, mod_ref, h_ref):
        n, _ = _ln_stats(x_ref[...])
        shift = mod_ref[3 * sub:3 * sub + 1, :]
        scale = mod_ref[3 * sub + 1:3 * sub + 2, :]
        h_ref[...] = (n * (1.0 + scale) + shift).astype(MXU)

    return _pallas_call(
        body, name=name, grid=(S // TM,),
        in_specs=[_row_spec(D), _full_spec((9, D))], out_specs=_row_spec(D),
        out_shape=_sds((S, D), MXU), compiler_params=_cp(32, ("arbitrary",)))(x, mod)


def res_ln_fwd(x, f, mod, sub, lng, lnb, w, name, nxt=None):
    def body(x_ref, f_ref, mod_ref, g_ref, b_ref, *rest):
        gate = mod_ref[3 * sub + 2:3 * sub + 3, :]
        r = ALPHA * x_ref[...] + (w * gate) * f_ref[...]
        n, _ = _ln_stats(r)
        xo = n * g_ref[sub:sub + 1, :] + b_ref[sub:sub + 1, :]
        rest[-1 if nxt is None else -2][...] = xo
        if nxt is not None:
            nmod_ref, h_ref = rest[0], rest[-1]
            n2, _ = _ln_stats(xo)
            s2 = nxt[1]
            h_ref[...] = (n2 * (1.0 + nmod_ref[3 * s2 + 1:3 * s2 + 2, :]) + nmod_ref[3 * s2:3 * s2 + 1, :]).astype(MXU)

    more = nxt is not None
    return _pallas_call(
        body, name=name, grid=(S // TM,),
        in_specs=[_row_spec(D), _row_spec(D), _full_spec((9, D)), _full_spec((3, D)), _full_spec((3, D))] + [_full_spec((9, D))] * more,
        out_specs=(_row_spec(D),) + (_row_spec(D),) * more, out_shape=(_sds((S, D), F32),) + (_sds((S, D), MXU),) * more,
        compiler_params=_cp(32, ("arbitrary",)))(x, f, mod, lng, lnb, *([nxt[0]] if more else []))


def res_ln_bwd(x, f, mod, sub, lng, dxo, w, name):
    def body(x_ref, f_ref, mod_ref, g_ref, dxo_ref, dxa_ref, df_ref, sums_ref):
        i = pl.program_id(0)
        gate = mod_ref[3 * sub + 2:3 * sub + 3, :]
        fv = f_ref[...]
        r = ALPHA * x_ref[...] + (w * gate) * fv
        n, rstd = _ln_stats(r)
        dxo = dxo_ref[...]
        dr = _ln_bwd(dxo * g_ref[sub:sub + 1, :], n, rstd)
        dxa_ref[...] = ALPHA * dr
        df_ref[...] = ((w * gate) * dr).astype(MXU)
        part = jnp.concatenate([
            jnp.sum(dxo * n, axis=0, keepdims=True),
            jnp.sum(dxo, axis=0, keepdims=True),
            jnp.sum(dr * fv, axis=0, keepdims=True) * w,
            jnp.zeros((5, D), F32)], axis=0)

        @pl.when(i == 0)
        def _():
            sums_ref[...] = part

        @pl.when(i > 0)
        def _():
            sums_ref[...] += part

    return _call(
        body, name=name, grid=(S // TM,),
        in_specs=[_row_spec(D), _row_spec(D), _full_spec((9, D)), _full_spec((3, D)), _row_spec(D)],
        out_specs=(_row_spec(D), _row_spec(D), _full_spec((8, D))),
        out_shape=(_sds((S, D), F32), _sds((S, D), MXU), _sds((8, D), F32)),
        cp=_cp(32, ("arbitrary",)), args=(x, f, mod, lng, dxo))[0]


def ln_mod_bwd(x, dh, mod, sub, dxa, name):
    def body(x_ref, dh_ref, mod_ref, dxa_ref, dx_ref, sums_ref):
        i = pl.program_id(0)
        scale = mod_ref[3 * sub + 1:3 * sub + 2, :]
        n, rstd = _ln_stats(x_ref[...])
        dh = dh_ref[...]
        dx_ref[...] = dxa_ref[...] + _ln_bwd(dh * (1.0 + scale), n, rstd)
        part = jnp.concatenate([
            jnp.sum(dh, axis=0, keepdims=True),
            jnp.sum(dh * n, axis=0, keepdims=True),
            jnp.zeros((6, D), F32)], axis=0)

        @pl.when(i == 0)
        def _():
            sums_ref[...] = part

        @pl.when(i > 0)
        def _():
            sums_ref[...] += part

    return _call(
        body, name=name, grid=(S // TM,),
        in_specs=[_row_spec(D), _row_spec(D), _full_spec((9, D)), _row_spec(D)],
        out_specs=(_row_spec(D), _full_spec((8, D))),
        out_shape=(_sds((S, D), F32), _sds((8, D), F32)),
        cp=_cp(32, ("arbitrary",)), args=(x, dh, mod, dxa))[0]


def ln_join_bwd(xp, fp, modp, subp, lngp, lnbp, wp, dh, mod, sub, dxa, name):
    def body(xp_ref, fp_ref, modp_ref, g_ref, b_ref, dh_ref, mod_ref, dxa_ref, dxap_ref, dfp_ref, sumsp_ref, sums_ref):
        i = pl.program_id(0)
        gate = modp_ref[3 * subp + 2:3 * subp + 3, :]
        fv = fp_ref[...]
        n, rstd = _ln_stats(ALPHA * xp_ref[...] + (wp * gate) * fv)
        gain = g_ref[subp:subp + 1, :]
        n2, rstd2 = _ln_stats(n * gain + b_ref[subp:subp + 1, :])
        dh = dh_ref[...]
        dx = dxa_ref[...] + _ln_bwd(dh * (1.0 + mod_ref[3 * sub + 1:3 * sub + 2, :]), n2, rstd2)
        dr = _ln_bwd(dx * gain, n, rstd)
        dxap_ref[...] = ALPHA * dr
        dfp_ref[...] = ((wp * gate) * dr).astype(MXU)
        partp = jnp.concatenate([
            jnp.sum(dx * n, axis=0, keepdims=True), jnp.sum(dx, axis=0, keepdims=True),
            jnp.sum(dr * fv, axis=0, keepdims=True) * wp, jnp.zeros((5, D), F32)], axis=0)
        part = jnp.concatenate([
            jnp.sum(dh, axis=0, keepdims=True), jnp.sum(dh * n2, axis=0, keepdims=True), jnp.zeros((6, D), F32)], axis=0)

        @pl.when(i == 0)
        def _():
            sumsp_ref[...] = partp
            sums_ref[...] = part

        @pl.when(i > 0)
        def _():
            sumsp_ref[...] += partp
            sums_ref[...] += part

    return _pallas_call(
        body, name=name, grid=(S // TM,),
        in_specs=[_row_spec(D), _row_spec(D), _full_spec((9, D)), _full_spec((3, D)), _full_spec((3, D)), _row_spec(D),
                  _full_spec((9, D)), _row_spec(D)],
        out_specs=(_row_spec(D), _row_spec(D), _full_spec((8, D)), _full_spec((8, D))),
        out_shape=(_sds((S, D), F32), _sds((S, D), MXU), _sds((8, D), F32), _sds((8, D), F32)),
        compiler_params=_cp(40, ("arbitrary",)))(xp, fp, modp, lngp, lnbp, dh, mod, dxa)


def loss_fwd_bwd(y, target, name):
    def body(y_ref, t_ref, l_ref, dy_ref):
        i = pl.program_id(0)
        e = y_ref[...] - t_ref[...]
        dy_ref[...] = e * (1.0 / D)
        part = jnp.zeros((8, 128), F32) + (0.5 / D) * jnp.sum(e * e)

        @pl.when(i == 0)
        def _():
            l_ref[...] = part

        @pl.when(i > 0)
        def _():
            l_ref[...] += part

    return _pallas_call(
        body, name=name, grid=(S // TM,),
        in_specs=[_row_spec(D), _row_spec(D)], out_specs=(_full_spec((8, 128)), _row_spec(D)),
        out_shape=(_sds((8, 128), F32), _sds((S, D), F32)),
        compiler_params=_cp(32, ("arbitrary",)))(y, target)


HB = 2 * FBP
NHB = NDEV * FBP // HB
TMB = 1024


def _wrows(buffers=2):
    return pl.BlockSpec((HB, D), lambda j, i: (j, 0), pipeline_mode=pl.Buffered(buffers))


def _resident(shape):
    return pl.BlockSpec(shape, lambda j, i: (0, 0), pipeline_mode=pl.Buffered(1))


def ffn_fwd(h, wgt, wut, wd, name, ride=None):
    def body(h_ref, wg_ref, wu_ref, wd_ref, g_ref, u_ref, f_ref):
        j, i = pl.program_id(0), pl.program_id(1)
        hv = h_ref[...]
        g = _mm_nt(hv, wg_ref[...])
        u = _mm_nt(hv, wu_ref[...])
        g_ref[...] = g.astype(MXU)
        u_ref[...] = u.astype(MXU)
        a = g * jax.nn.sigmoid(g) * u
        part = _mm(a, wd_ref[...])
        rows = pl.ds(pl.multiple_of(i * TMB, TMB), TMB)

        @pl.when(j == 0)
        def _():
            f_ref[rows, :] = part

        @pl.when(j > 0)
        def _():
            f_ref[rows, :] += part

    gu = pl.BlockSpec((TMB, HB), lambda j, i: (i, j))
    return _call(
        body, name=name, grid=(NHB, S // TMB),
        in_specs=[pl.BlockSpec((TMB, D), lambda j, i: (i, 0)), _wrows(), _wrows(), _wrows()],
        out_specs=(gu, gu, _resident((S, D))),
        out_shape=(_sds((S, NDEV * FBP), MXU), _sds((S, NDEV * FBP), MXU), _sds((S, D), F32)),
        cp=_cp(52, ("arbitrary", "arbitrary")), args=(h, wgt, wut, wd), ride=ride)


def ffn_bwd(df, h, g, u, wgt, wut, wd, name):
    ni = S // TMB

    def body(df_ref, h_ref, g_ref, u_ref, wg_ref, wu_ref, wd_ref, dwg_ref, dwu_ref, dwd_ref, dh_ref,
             ag_ref, au_ref, ad_ref):
        j, i = pl.program_id(0), pl.program_id(1)
        dfv, hv = df_ref[...], h_ref[...]
        gv, uv = g_ref[...].astype(F32), u_ref[...].astype(F32)
        da = _mm_nt(dfv, wd_ref[...])
        sg = jax.nn.sigmoid(gv)
        silu = gv * sg
        du = da * silu
        dg = da * uv * (sg * (1.0 + gv * (1.0 - sg)))
        p_d = _mm_tn(silu * uv, dfv)
        p_g = _mm_tn(dg, hv)
        p_u = _mm_tn(du, hv)

        @pl.when(i == 0)
        def _():
            ad_ref[...] = p_d
            ag_ref[...] = p_g
            au_ref[...] = p_u

        @pl.when(i > 0)
        def _():
            ad_ref[...] += p_d
            ag_ref[...] += p_g
            au_ref[...] += p_u

        @pl.when(i == ni - 1)
        def _():
            dwd_ref[...] = ad_ref[...].astype(BF16)
            dwg_ref[...] = ag_ref[...].astype(BF16)
            dwu_ref[...] = au_ref[...].astype(BF16)

        part = _mm(dg, wg_ref[...]) + _mm(du, wu_ref[...])
        rows = pl.ds(pl.multiple_of(i * TMB, TMB), TMB)

        @pl.when(j == 0)
        def _():
            dh_ref[rows, :] = part

        @pl.when(j > 0)
        def _():
            dh_ref[rows, :] += part

    gu = pl.BlockSpec((TMB, HB), lambda j, i: (i, j))
    rowt = pl.BlockSpec((TMB, D), lambda j, i: (i, 0))
    return _call(
        body, name=name, grid=(NHB, ni),
        in_specs=[rowt, rowt, gu, gu, _wrows(1), _wrows(1), _wrows(1)],
        out_specs=(_wrows(1), _wrows(1), _wrows(1), _resident((S, D))),
        out_shape=(_sds((NDEV * FBP, D), BF16), _sds((NDEV * FBP, D), BF16), _sds((NDEV * FBP, D), BF16), _sds((S, D), F32)),
        scratch=[pltpu.VMEM((HB, D), F32), pltpu.VMEM((HB, D), F32), pltpu.VMEM((HB, D), F32)],
        cp=_cp(60, ("arbitrary", "arbitrary")), args=(df, h, g, u, wgt, wut, wd))[0]


def win_fwd(h, win, name, ride=None):
    def body(h_ref, w_ref, z_ref):
        hv = h_ref[...]
        for j in range(NDEV):
            z_ref[:, 256 * j:256 * (j + 1)] = _mm(hv, w_ref[j])

    return _call(
        body, name=name, grid=(S // TMM,),
        in_specs=[_row_spec(D, TMM), _full_spec((NDEV, D, 256))],
        out_specs=[_row_spec(D_IN, TMM)], out_shape=[_sds((S, D_IN), F32)],
        cp=_cp(40, ("arbitrary",)), args=(h, win), ride=ride)


def win_bwd(dparts, h, win, name):
    ni = S // TMM

    def body(dq_ref, dk_ref, dv_ref, dus_ref, dup_ref, h_ref, w_ref, dh_ref, dw_ref, acc_ref):
        i = pl.program_id(0)
        hv = h_ref[...]
        cols = [dq_ref[:, 0:256], dq_ref[:, 256:512], dk_ref[:, 0:256], dk_ref[:, 256:512],
                dv_ref[:, 0:256], dv_ref[:, 256:512], dus_ref[...], dup_ref[...]]
        dh = jnp.zeros((TMM, D), F32)
        for j in range(NDEV):
            dz = cols[j].astype(MXU)
            dh = dh + _mm_nt(dz, w_ref[j])
            p = _mm_tn(hv, dz)

            @pl.when(i == 0)
            def _():
                acc_ref[j] = p

            @pl.when(i > 0)
            def _():
                acc_ref[j] += p

        dh_ref[...] = dh

        @pl.when(i == ni - 1)
        def _():
            dw_ref[...] = acc_ref[...].astype(BF16)

    return _call(
        body, name=name, grid=(ni,),
        in_specs=[_row_spec(512, TMM), _row_spec(512, TMM), _row_spec(512, TMM), _row_spec(256, TMM), _row_spec(256, TMM),
                  _row_spec(D, TMM), _full_spec((NDEV, D, 256))],
        out_specs=(_row_spec(D, TMM), _full_spec((NDEV, D, 256))),
        out_shape=(_sds((S, D), F32), _sds((NDEV, D, 256), BF16)),
        scratch=[pltpu.VMEM((NDEV, D, 256), F32)],
        cp=_cp(48, ("arbitrary",)), args=(*dparts, h, win))[0]


def wout_fwd(ya, ys, yp, wout, name, ride=None):
    def body(ya_ref, ys_ref, yp_ref, w_ref, o_ref):
        w = w_ref[...].reshape(D, D)
        o_ref[...] = _mm(ya_ref[...], w[0:512]) + _mm(ys_ref[...], w[512:768]) + _mm(yp_ref[...], w[768:1024])

    return _call(
        body, name=name, grid=(S // TMM,),
        in_specs=[_row_spec(512, TMM), _row_spec(256, TMM), _row_spec(256, TMM), _full_spec((NDEV, 128, D))],
        out_specs=[_row_spec(D, TMM)], out_shape=[_sds((S, D), F32)],
        cp=_cp(40, ("arbitrary",)), args=(ya, ys, yp, wout), ride=ride)


def wout_bwd(do, ya, ys, yp, wout, name):
    ni = S // TMM

    def body(do_ref, ya_ref, ys_ref, yp_ref, w_ref, dya_ref, dys_ref, dyp_ref, dw_ref, acc_ref):
        i = pl.program_id(0)
        w = w_ref[...].reshape(D, D)
        dov = do_ref[...]
        dya_ref[...] = _mm_nt(dov, w[0:512])
        dys_ref[...] = _mm_nt(dov, w[512:768])
        dyp_ref[...] = _mm_nt(dov, w[768:1024])
        parts = [(0, 512, _mm_tn(ya_ref[...], dov)), (512, 768, _mm_tn(ys_ref[...], dov)),
                 (768, 1024, _mm_tn(yp_ref[...], dov))]
        for lo, hi, p in parts:
            @pl.when(i == 0)
            def _():
                acc_ref[lo:hi, :] = p

            @pl.when(i > 0)
            def _():
                acc_ref[lo:hi, :] += p

        @pl.when(i == ni - 1)
        def _():
            dw_ref[...] = acc_ref[...].astype(BF16).reshape(NDEV, 128, D)

    return _call(
        body, name=name, grid=(ni,),
        in_specs=[_row_spec(D, TMM), _row_spec(512, TMM), _row_spec(256, TMM), _row_spec(256, TMM),
                  _full_spec((NDEV, 128, D))],
        out_specs=(_row_spec(512, TMM), _row_spec(256, TMM), _row_spec(256, TMM), _full_spec((NDEV, 128, D))),
        out_shape=(_sds((S, 512), F32), _sds((S, 256), F32), _sds((S, 256), F32), _sds((NDEV, 128, D), BF16)),
        scratch=[pltpu.VMEM((D, D), F32)],
        cp=_cp(40, ("arbitrary",)), args=(do, ya, ys, yp, wout))[0]


def _t5_bucket(dist):
    max_exact = N_BUCKETS // 2
    d = np.maximum(dist, 1).astype(np.float32)
    large = max_exact + (np.log(d / max_exact) / math.log(MAX_DISTANCE / max_exact)
                         * (N_BUCKETS - max_exact)).astype(np.int32)
    large = np.minimum(large, N_BUCKETS - 1)
    return np.where(dist < max_exact, dist, large).astype(np.int32)


def _att_static():
    i = np.arange(QB)[:, None]
    j = np.arange(2 * QB)[None, :]
    r = i + QB - j
    buckets, bands = [], []
    for window, dil in PATTERNS:
        bands.append((r >= 0) & (r <= window // dil))
        buckets.append(_t5_bucket(np.clip(r, 0, None) * dil))
    return np.stack(buckets), np.stack(bands), np.broadcast_to(j >= QB, (QB, 2 * QB))


def att_bias(rel_bias):
    m = np.arange(2 * QB)
    rows = []
    for window, dil in PATTERNS:
        r = QB - m
        ok = (r >= 0) & (r <= window // dil)
        b = rel_bias[_t5_bucket(np.clip(r, 0, None) * dil)]
        rows.append(jnp.where(ok[:, None], b, NEG).T)
    return jnp.broadcast_to(jnp.stack(rows)[:, :, None, :], (3, N_HEADS, 8, 2 * QB))


def _bias_tiles(t_ref, tiles):
    col = lax.broadcasted_iota(jnp.int32, (QB, 2 * QB), 1)
    for p in range(3):
        for hh in range(2):
            t = pltpu.roll(jnp.broadcast_to(t_ref[p, hh, 0:1, :], (QB, 2 * QB)), 0, 1, stride=1, stride_axis=0)
            tiles[p, hh, 0] = t
            tiles[p, hh, 1] = jnp.where(col >= QB, t, NEG)


def _permute_in(dst_ref, src_ref, d, scale=None, pad=QB):
    L = S // d
    for r in range(d):
        v = src_ref[pl.ds(r, L, stride=d), :] if d > 1 else src_ref[...]
        if scale is not None:
            v = v * scale
        dst_ref[pad + r * L:pad + (r + 1) * L, :] = v.astype(dst_ref.dtype)


def att_fwd(z, bias, name, ride=None):
    def body(q_ref, k_ref, v_ref, t_ref, y_ref, l_ref, qs, ks, vs, o_perm, l_perm, o_nat, l_nat, b_ref):
        _bias_tiles(t_ref, b_ref)
        zero_pad = jnp.zeros((QB, 128), MXU)
        ks[0:QB, :] = zero_pad
        vs[0:QB, :] = zero_pad
        lane = lax.broadcasted_iota(jnp.int32, (QB, 128), 1)
        for p, (_, d) in enumerate(PATTERNS):
            L = S // d
            nb = L // QB
            _permute_in(qs, q_ref, d, scale=0.125, pad=0)
            _permute_in(ks, k_ref, d)
            _permute_in(vs, v_ref, d)

            def blk(b, carry):
                r0 = pl.multiple_of(b * QB, QB)
                q = qs[pl.ds(r0, QB), :]
                kb = ks[pl.ds(r0, 2 * QB), :]
                vb = vs[pl.ds(r0, 2 * QB), :]
                first = ((b % nb) == 0).astype(jnp.int32)
                res = []
                for hh in range(2):
                    sel = (lane < 64) if hh == 0 else (lane >= 64)
                    qm = jnp.where(sel, q, jnp.zeros_like(q))
                    s = _mm_nt(qm, kb) + b_ref[p, hh, first]
                    m = jnp.max(s, axis=1, keepdims=True)
                    pe = jnp.exp(s - m)
                    den = jnp.sum(pe, axis=1, keepdims=True)
                    res.append((_mm(pe, vb) / den, m + jnp.log(den)))
                o_perm[pl.ds(r0, QB), :] = jnp.where(lane < 64, res[0][0], res[1][0])
                l_perm[pl.ds(r0, QB), :] = jnp.where(lane < 64, res[0][1], res[1][1])
                return carry

            lax.fori_loop(0, S // QB, blk, 0, unroll=8)
            for r in range(d):
                if d > 1:
                    o_nat[p, pl.ds(r, L, stride=d), :] = o_perm[r * L:(r + 1) * L, :]
                    l_nat[p, pl.ds(r, L, stride=d), :] = l_perm[r * L:(r + 1) * L, :]
                else:
                    o_nat[p] = o_perm[...]
                    l_nat[p] = l_perm[...]
        l0, l1, l2 = l_nat[0], l_nat[1], l_nat[2]
        m = jnp.maximum(jnp.maximum(l0, l1), l2)
        e0, e1, e2 = jnp.exp(l0 - m), jnp.exp(l1 - m), jnp.exp(l2 - m)
        den = e0 + e1 + e2
        y_ref[...] = (e0 * o_nat[0] + e1 * o_nat[1] + e2 * o_nat[2]) / den
        l_ref[...] = m + jnp.log(den)

    col = lambda c0: pl.BlockSpec((S, 128), lambda hp: (0, c0 + hp))
    return _call(
        body, name=name, grid=(N_HEADS // 2,),
        in_specs=[col(0), col(4), col(8), pl.BlockSpec((3, 2, 8, 2 * QB), lambda hp: (0, hp, 0, 0))],
        out_specs=(col(0), col(0)),
        out_shape=(_sds((S, D_ATT), F32), _sds((S, D_ATT), F32)),
        scratch=[pltpu.VMEM((S, 128), MXU), pltpu.VMEM((S + QB, 128), MXU), pltpu.VMEM((S + QB, 128), MXU),
                 pltpu.VMEM((S, 128), F32), pltpu.VMEM((S, 128), F32),
                 pltpu.VMEM((3, S, 128), F32), pltpu.VMEM((3, S, 128), F32),
                 pltpu.VMEM((3, 2, 2, QB, 2 * QB), F32)],
        cp=_cp(40, ("arbitrary",)), args=(z, z, z, bias), ride=ride)


def att_bwd(z, bias, y, lse, dy, name):
    def body(q_ref, k_ref, v_ref, t_ref, y_ref, l_ref, dy_ref, dq_ref, dk_ref, dv_ref, db_ref,
             qs, ks, vs, dys, ls, dds, dn_nat, dq_perm, dk_perm, dv_perm, b_ref):
        _bias_tiles(t_ref, b_ref)
        zero_pad = jnp.zeros((QB, 128), MXU)
        ks[0:QB, :] = zero_pad
        vs[0:QB, :] = zero_pad
        lane = lax.broadcasted_iota(jnp.int32, (QB, 128), 1)
        lane_s = lax.broadcasted_iota(jnp.int32, (S, 128), 1)
        t = dy_ref[...] * y_ref[...]
        sa = jnp.sum(jnp.where(lane_s < 64, t, 0.0), axis=1, keepdims=True)
        sb = jnp.sum(jnp.where(lane_s >= 64, t, 0.0), axis=1, keepdims=True)
        dn_nat[...] = jnp.where(lane_s < 64, sa, sb)
        dq_ref[...] = jnp.zeros((S, 128), F32)
        dk_ref[...] = jnp.zeros((S, 128), F32)
        dv_ref[...] = jnp.zeros((S, 128), F32)
        db_ref[...] = jnp.zeros((3, 2, QB, 2 * QB), F32)
        for p, (_, d) in enumerate(PATTERNS):
            L = S // d
            nb = L // QB
            _permute_in(qs, q_ref, d, scale=0.125, pad=0)
            _permute_in(ks, k_ref, d)
            _permute_in(vs, v_ref, d)
            _permute_in(dys, dy_ref, d, pad=0)
            _permute_in(ls, l_ref, d, pad=0)
            _permute_in(dds, dn_nat, d, pad=0)
            dk_perm[...] = jnp.zeros((S + QB, 128), F32)
            dv_perm[...] = jnp.zeros((S + QB, 128), F32)

            def blk(b, carry):
                r0 = pl.multiple_of(b * QB, QB)
                q = qs[pl.ds(r0, QB), :]
                kb = ks[pl.ds(r0, 2 * QB), :]
                vb = vs[pl.ds(r0, 2 * QB), :]
                dyb = dys[pl.ds(r0, QB), :]
                lb = ls[pl.ds(r0, QB), :]
                db = dds[pl.ds(r0, QB), :]
                first = ((b % nb) == 0).astype(jnp.int32)
                lane2 = jnp.concatenate([lane, lane], axis=0)
                own = (lane2 >> 6) == (lax.broadcasted_iota(jnp.int32, (2 * QB, 128), 0) >> 7)
                qm = jnp.where(own, jnp.concatenate([q, q], axis=0), jnp.zeros((2 * QB, 128), q.dtype))
                dym = jnp.where(own, jnp.concatenate([dyb, dyb], axis=0), jnp.zeros((2 * QB, 128), dyb.dtype))
                wide = lambda t: jnp.concatenate([jnp.broadcast_to(t[:, 0:1], (QB, 2 * QB)), jnp.broadcast_to(t[:, 64:65], (QB, 2 * QB))], axis=0)
                lse2, dd2 = wide(lb), wide(db)
                bias2 = jnp.concatenate([b_ref[p, 0, first], b_ref[p, 1, first]], axis=0)
                pr = jnp.exp(_mm_nt(qm, kb) + bias2 - lse2)
                ds = pr * (_mm_nt(dym, vb) - dd2)
                db_ref[p, 0] += ds[0:QB]
                db_ref[p, 1] += ds[QB:2 * QB]
                dq2 = _mm(ds, kb)
                dqs = [dq2[0:QB], dq2[QB:2 * QB]]
                dkb = _mm_tn(ds, qm)
                dvb = _mm_tn(pr, dym)
                dq_perm[pl.ds(r0, QB), :] = jnp.where(lane < 64, dqs[0], dqs[1])
                dk_perm[pl.ds(r0, 2 * QB), :] += dkb
                dv_perm[pl.ds(r0, 2 * QB), :] += dvb
                return carry

            lax.fori_loop(0, S // QB, blk, 0, unroll=4)
            for r in range(d):
                idx = pl.ds(r, L, stride=d) if d > 1 else pl.ds(0, S)
                dq_ref[idx, :] += dq_perm[r * L:(r + 1) * L, :] * 0.125
                dk_ref[idx, :] += dk_perm[QB + r * L:QB + (r + 1) * L, :]
                dv_ref[idx, :] += dv_perm[QB + r * L:QB + (r + 1) * L, :]

    col = lambda c0: pl.BlockSpec((S, 128), lambda hp: (0, c0 + hp))
    bspec = pl.BlockSpec((3, 2, 8, 2 * QB), lambda hp: (0, hp, 0, 0))
    return _call(
        body, name=name, grid=(N_HEADS // 2,),
        in_specs=[col(0), col(4), col(8), bspec, col(0), col(0), col(0)],
        out_specs=(col(0), col(0), col(0), pl.BlockSpec((3, 2, QB, 2 * QB), lambda hp: (0, hp, 0, 0))),
        out_shape=(_sds((S, D_ATT), F32), _sds((S, D_ATT), F32), _sds((S, D_ATT), F32),
                   _sds((3, N_HEADS, QB, 2 * QB), F32)),
        scratch=[pltpu.VMEM((S, 128), MXU), pltpu.VMEM((S + QB, 128), MXU), pltpu.VMEM((S + QB, 128), MXU),
                 pltpu.VMEM((S, 128), MXU), pltpu.VMEM((S, 128), F32), pltpu.VMEM((S, 128), F32),
                 pltpu.VMEM((S, 128), F32), pltpu.VMEM((S, 128), F32),
                 pltpu.VMEM((S + QB, 128), F32), pltpu.VMEM((S + QB, 128), F32),
                 pltpu.VMEM((3, 2, 2, QB, 2 * QB), F32)],
        cp=_cp(48, ("arbitrary",)), args=(z, z, z, bias, y, lse, dy))[0]


def relbias_grad(dbiases):
    bucket, band, _ = _att_static()
    onehot = (bucket[:, None] == np.arange(N_BUCKETS)[None, :, None, None]) & band[:, None]
    onehot = jnp.asarray(onehot.reshape(3, N_BUCKETS, QB * 2 * QB), BF16)

    def body(db0_ref, db1_ref, oh_ref, o_ref):
        acc = jnp.zeros((N_HEADS, N_BUCKETS), F32)
        for p in range(3):
            acc = acc + lax.dot_general(db0_ref[p] + db1_ref[p], oh_ref[p].astype(F32), (((1,), (1,)), ((), ())),
                                        preferred_element_type=F32, precision=lax.Precision.HIGHEST)
        o_ref[...] = acc

    vm = pl.BlockSpec(memory_space=pltpu.VMEM)
    out = _pallas_call(body, name="relbias_grad", in_specs=[vm, vm, vm], out_specs=vm,
                         out_shape=_sds((N_HEADS, N_BUCKETS), F32), compiler_params=_cp(40))(
        *[d.reshape(3, N_HEADS, QB * 2 * QB) for d in dbiases], onehot)
    return out.T


def _panel(t_ref, ri, j):
    return t_ref[ri, pl.ds(j, S, stride=8), :]


def _gelu(x):
    c = math.sqrt(2.0 / math.pi)
    th = jnp.tanh(c * (x + 0.044715 * x * x * x))
    return 0.5 * x * (1.0 + th), th


def ssm_fwd(z, a, bre, bim, cre, cim, dsk, gluw, glub, name, ride=None):
    def body(u_ref, a_ref, bre_ref, bim_ref, cre_ref, cim_ref, d_ref, gw_ref, gb_ref, y_ref, yp_ref, st_hbm, st_ref):
        u = u_ref[...]
        for j in range(8):
            st_ref[0, pl.ds(j, S, stride=8), :] = _mm(u, bre_ref[:, 128 * j:128 * (j + 1)])
            st_ref[1, pl.ds(j, S, stride=8), :] = _mm(u, bim_ref[:, 128 * j:128 * (j + 1)])
        ar, ai = a_ref[0], a_ref[1]

        def step(t, c):
            re, im = c
            i = pl.multiple_of(t * 8, 8)
            nre = ar * re - ai * im + st_ref[0, pl.ds(i, 8), :]
            nim = ar * im + ai * re + st_ref[1, pl.ds(i, 8), :]
            st_ref[0, pl.ds(i, 8), :] = nre
            st_ref[1, pl.ds(i, 8), :] = nim
            return nre, nim

        zero = jnp.zeros((8, 128), F32)
        lax.fori_loop(0, S, step, (zero, zero), unroll=8)
        y = d_ref[...] * u
        for j in range(8):
            y = y + _mm(_panel(st_ref, 0, j), cre_ref[128 * j:128 * (j + 1), :])
            y = y - _mm(_panel(st_ref, 1, j), cim_ref[128 * j:128 * (j + 1), :])
        pltpu.sync_copy(st_ref, st_hbm)
        yp_ref[...] = y
        gl, _ = _gelu(y)
        tt = _mm(gl, gw_ref[...].reshape(D_SSM, D_SSM)) + gb_ref[...]
        y_ref[...] = y * jax.nn.sigmoid(tt)

    vm = lambda shape: pl.BlockSpec(shape, lambda i: (0,) * len(shape))
    return _call(
        body, name=name, grid=(1,),
        in_specs=[pl.BlockSpec((S, 256), lambda i: (0, 6)), vm((2, 8, 128)), vm((256, 1024)), vm((256, 1024)),
                  vm((1024, 256)), vm((1024, 256)), vm((1, 256)),
                  vm((NDEV, 32, 256)), vm((1, 256))],
        out_specs=(vm((S, 256)), vm((S, 256)), pl.BlockSpec(memory_space=pl.ANY)),
        out_shape=(_sds((S, 256), F32), _sds((S, 256), F32), _sds((2, S * 8, 128), F32)),
        scratch=[pltpu.VMEM((2, S * 8, 128), F32)],
        cp=_cp(40, ("arbitrary",)), args=(z, a, bre, bim, cre, cim, dsk, gluw, glub), ride=ride)


def ssm_bwd(dy, z, ypre, st, a, bre, bim, cre, cim, dsk, gluw, glub, name):
    def body(dy_ref, u_ref, yp_ref, st_hbm, a_ref, bre_ref, bim_ref, cre_ref, cim_ref, d_ref, gw_ref, gb_ref,
             du_ref, dbre_ref, dbim_ref, dcre_ref, dcim_ref, da_ref, dd_ref, dgw_ref, dgb_ref, g_ref, st_ref):
        pltpu.sync_copy(st_hbm, st_ref)
        u = u_ref[...]
        y = yp_ref[...]
        dout = dy_ref[...]
        gw = gw_ref[...].reshape(D_SSM, D_SSM)
        gl, th = _gelu(y)
        sig = jax.nn.sigmoid(_mm(gl, gw) + gb_ref[...])
        dt = dout * y * sig * (1.0 - sig)
        dgw_ref[...] = _mm_tn(gl, dt)
        dgb_ref[...] = jnp.sum(dt, axis=0, keepdims=True)
        c = math.sqrt(2.0 / math.pi)
        dgelu = 0.5 * (1.0 + th) + 0.5 * y * (1.0 - th * th) * c * (1.0 + 3.0 * 0.044715 * y * y)
        dyv = dout * sig + _mm_nt(dt, gw) * dgelu
        dd_ref[...] = jnp.sum(dyv * u, axis=0, keepdims=True)
        for j in range(8):
            rows = slice(128 * j, 128 * (j + 1))
            g_ref[0, pl.ds(j, S, stride=8), :] = _mm_nt(dyv, cre_ref[rows, :])
            g_ref[1, pl.ds(j, S, stride=8), :] = -_mm_nt(dyv, cim_ref[rows, :])
            dcre_ref[rows, :] = _mm_tn(_panel(st_ref, 0, j), dyv)
            dcim_ref[rows, :] = -_mm_tn(_panel(st_ref, 1, j), dyv)
        ar, ai = a_ref[0], a_ref[1]

        def step(k, c4):
            gre, gim, dar, dai = c4
            i = pl.multiple_of((S - 1 - k) * 8, 8)
            nre = g_ref[0, pl.ds(i, 8), :] + ar * gre + ai * gim
            nim = g_ref[1, pl.ds(i, 8), :] + ar * gim - ai * gre
            g_ref[0, pl.ds(i, 8), :] = nre
            g_ref[1, pl.ds(i, 8), :] = nim
            sre = st_ref[0, pl.ds(i - 8, 8), :]
            sim = st_ref[1, pl.ds(i - 8, 8), :]
            return nre, nim, dar + nre * sre + nim * sim, dai + nim * sre - nre * sim

        zero = jnp.zeros((8, 128), F32)
        gre, gim, dar, dai = lax.fori_loop(0, S - 1, step, (zero, zero, zero, zero), unroll=8)
        g_ref[0, 0:8, :] = g_ref[0, 0:8, :] + ar * gre + ai * gim
        g_ref[1, 0:8, :] = g_ref[1, 0:8, :] + ar * gim - ai * gre
        da_ref[0] = dar
        da_ref[1] = dai
        du = dyv * d_ref[...]
        for j in range(8):
            cols = slice(128 * j, 128 * (j + 1))
            gr, gi = _panel(g_ref, 0, j), _panel(g_ref, 1, j)
            dbre_ref[:, cols] = _mm_tn(u, gr)
            dbim_ref[:, cols] = _mm_tn(u, gi)
            du = du + _mm_nt(gr, bre_ref[:, cols]) + _mm_nt(gi, bim_ref[:, cols])
        du_ref[...] = du

    vm = lambda shape: pl.BlockSpec(shape, lambda i: (0,) * len(shape))
    return _call(
        body, name=name, grid=(1,),
        in_specs=[vm((S, 256)), pl.BlockSpec((S, 256), lambda i: (0, 6)), vm((S, 256)), pl.BlockSpec(memory_space=pl.ANY),
                  vm((2, 8, 128)), vm((256, 1024)), vm((256, 1024)), vm((1024, 256)), vm((1024, 256)), vm((1, 256)),
                  vm((NDEV, 32, 256)), vm((1, 256))],
        out_specs=(vm((S, 256)), vm((256, 1024)), vm((256, 1024)), vm((1024, 256)), vm((1024, 256)),
                   vm((2, 8, 128)), vm((1, 256)), vm((256, 256)), vm((1, 256))),
        out_shape=(_sds((S, 256), F32), _sds((256, 1024), F32), _sds((256, 1024), F32), _sds((1024, 256), F32),
                   _sds((1024, 256), F32), _sds((2, 8, 128), F32), _sds((1, 256), F32), _sds((256, 256), F32),
                   _sds((1, 256), F32)),
        scratch=[pltpu.VMEM((2, S * 8, 128), F32), pltpu.VMEM((2, S * 8, 128), F32)],
        cp=_cp(56, ("arbitrary",)), args=(dy, z, ypre, st, a, bre, bim, cre, cim, dsk, gluw, glub))[0]


def _ssm_discretise(a_re, a_im, log_dt, b_re, b_im):
    dt = jnp.exp(log_dt)[:, None]
    er = jnp.exp(a_re * dt)
    abr, abi = er * jnp.cos(a_im * dt), er * jnp.sin(a_im * dt)
    den = a_re * a_re + a_im * a_im
    fr = ((abr - 1.0) * a_re + abi * a_im) / den
    fi = (abi * a_re - (abr - 1.0) * a_im) / den
    bbr = fr[:, :, None] * b_re - fi[:, :, None] * b_im
    bbi = fr[:, :, None] * b_im + fi[:, :, None] * b_re
    return abr, abi, bbr, bbi


def _blockdiag(t):
    g, r, c = t.shape
    eye = jnp.eye(g, dtype=t.dtype)
    return (t[:, :, None, :] * eye[:, None, :, None]).reshape(g * r, g * c)


def _blockdiag_take(m, r, c):
    g = m.shape[0] // r
    idx = jnp.arange(g)
    return m.reshape(g, r, g, c)[idx, :, idx, :]


PAD = 16


def _pool_lane_select(vals):
    lane = lax.broadcasted_iota(jnp.int32, vals[0].shape, 1)
    out = vals[3]
    for g in (2, 1, 0):
        out = jnp.where(lane < 64 * (g + 1), vals[g], out)
    return out


def _pool_counts():
    row = lax.broadcasted_iota(jnp.int32, (S, D_POOL), 0).astype(F32) + 1.0
    return _pool_lane_select([jnp.minimum(row, float(w)) for w in POOL_WINDOWS])


def _pooled(u, sa, sb):
    sums = []
    cur = u
    bufs = (sa, sb)
    for k, sh in enumerate((1, 2, 4, 8)):
        buf = bufs[k % 2]
        buf[PAD:PAD + S, :] = cur
        cur = cur + buf[PAD - sh:PAD - sh + S, :]
        sums.append(cur)
    return _pool_lane_select(sums) / _pool_counts() - u


def pool_fwd(z, pw, psc, name):
    def body(u_ref, w_ref, s_ref, y_ref, sa, sb):
        for buf in (sa, sb):
            buf[0:PAD, :] = jnp.zeros((PAD, D_POOL), F32)
        pooled = _pooled(u_ref[...], sa, sb)
        y_ref[...] = _mm(pooled, w_ref[...]) * s_ref[...]

    vm = lambda shape: pl.BlockSpec(shape, lambda i: (0,) * len(shape))
    return _pallas_call(
        body, name=name, grid=(1,),
        in_specs=[pl.BlockSpec((S, 256), lambda i: (0, 7)), vm((256, 256)), vm((1, 256))],
        out_specs=vm((S, 256)), out_shape=_sds((S, 256), F32),
        scratch_shapes=[pltpu.VMEM((S + 2 * PAD, D_POOL), F32)] * 2,
        compiler_params=_cp(40, ("arbitrary",)))(z, pw, psc)


def pool_bwd(dy, z, pw, psc, name):
    def body(dy_ref, u_ref, w_ref, s_ref, du_ref, dw_ref, ds_ref, sa, sb):
        for buf in (sa, sb):
            buf[0:PAD, :] = jnp.zeros((PAD, D_POOL), F32)
            buf[PAD + S:PAD + S + PAD, :] = jnp.zeros((PAD, D_POOL), F32)
        pooled = _pooled(u_ref[...], sa, sb)
        dyv = dy_ref[...]
        w = w_ref[...]
        ds_ref[...] = jnp.sum(dyv * _mm(pooled, w), axis=0, keepdims=True)
        dyl = dyv * s_ref[...]
        dw_ref[...] = _mm_tn(pooled, dyl)
        dpool = _mm_nt(dyl, w)
        cur = dpool / _pool_counts()
        sums = []
        bufs = (sa, sb)
        for k, sh in enumerate((1, 2, 4, 8)):
            buf = bufs[k % 2]
            buf[PAD:PAD + S, :] = cur
            cur = cur + buf[PAD + sh:PAD + sh + S, :]
            sums.append(cur)
        du_ref[...] = _pool_lane_select(sums) - dpool

    vm = lambda shape: pl.BlockSpec(shape, lambda i: (0,) * len(shape))
    return _pallas_call(
        body, name=name, grid=(1,),
        in_specs=[vm((S, 256)), pl.BlockSpec((S, 256), lambda i: (0, 7)), vm((256, 256)), vm((1, 256))],
        out_specs=(vm((S, 256)), vm((256, 256)), vm((1, 256))),
        out_shape=(_sds((S, 256), F32), _sds((256, 256), F32), _sds((1, 256), F32)),
        scratch_shapes=[pltpu.VMEM((S + 2 * PAD, D_POOL), F32)] * 2,
        compiler_params=_cp(40, ("arbitrary",)))(dy, z, pw, psc)


def ada_fwd(c_all, ada_w, ada_b_cols):
    def body(c_ref, w_ref, b_ref, o_ref):
        c = c_ref[...]
        cond = c * jax.nn.sigmoid(c)
        o_ref[...] = jnp.dot(cond, w_ref[...], preferred_element_type=F32, precision=lax.Precision.HIGHEST) + b_ref[...]

    return _pallas_call(
        body, name="ada_fwd", grid=(DEPTH,),
        in_specs=[pl.BlockSpec((NDEV, D), lambda l: (0, 0)), pl.BlockSpec((None, D, 1152), lambda l: (l, 0, 0)),
                  pl.BlockSpec((None, 1, 1152), lambda l: (l, 0, 0))],
        out_specs=pl.BlockSpec((None, NDEV, 1152), lambda l: (l, 0, 0)), out_shape=_sds((DEPTH, NDEV, 1152), F32),
        compiler_params=_cp(40, ("arbitrary",)))(c_all, ada_w, ada_b_cols)


def ada_bwd_adam(c_all, dmod_cols, w, m, v, tr):
    def body(c_ref, dm_ref, w_ref, m_ref, v_ref, g_ref, d_ref, m2_ref, v2_ref):
        c = c_ref[...]
        cond = c * jax.nn.sigmoid(c)
        g = lax.dot_general(cond, dm_ref[...], (((0,), (0,)), ((), ())), preferred_element_type=F32,
                            precision=lax.Precision.HIGHEST)
        g_ref[...] = g
        d_ref[...], m2_ref[...], v2_ref[...] = _adamw(w_ref[...], g, m_ref[...], v_ref[...])

    rs = pl.BlockSpec((None, tr, 1152), lambda l, i: (l, i, 0))
    return _pallas_call(
        body, name="ada_bwd_adam", grid=(DEPTH, D // tr),
        in_specs=[pl.BlockSpec((NDEV, tr), lambda l, i: (0, i)), pl.BlockSpec((None, NDEV, 1152), lambda l, i: (l, 0, 0)), rs, rs, rs],
        out_specs=(rs, rs, rs, rs), out_shape=tuple(_sds((DEPTH, D, 1152), F32) for _ in range(4)),
        compiler_params=_cp(48, ("arbitrary", "arbitrary")))(c_all, dmod_cols, w, m, v)


def _adamw(w, g, m, v):
    m2 = B1 * m + (1.0 - B1) * g
    v2 = B2 * v + (1.0 - B2) * (g * g)
    m_hat = m2 / (1.0 - B1 ** STEP)
    v_hat = v2 / (1.0 - B2 ** STEP)
    return -LR * (m_hat / (jnp.sqrt(v_hat) + EPS) + WD * w), m2, v2


def _sum8(ref):
    g = ref[0].astype(F32)
    for s in range(1, ref.shape[0]):
        g = g + ref[s].astype(F32)
    return g


def adam_rs(recv, w, m, v, tr, name):
    lead, (r, cdim) = w.shape[:-2], w.shape[-2:]
    cp = recv.shape[-1]
    nl = len(lead)

    def body(rc_ref, w_ref, m_ref, v_ref, g_ref, d_ref, m2_ref, v2_ref):
        g = _sum8(rc_ref)[:, :cdim]
        g_ref[...] = g
        d_ref[...], m2_ref[...], v2_ref[...] = _adamw(w_ref[...], g, m_ref[...], v_ref[...])

    rs = pl.BlockSpec((None,) * nl + (tr, cdim), lambda *i: (*i, 0))
    return _call(
        body, name=name, grid=lead + (r // tr,),
        in_specs=[pl.BlockSpec((NDEV,) + (None,) * nl + (tr, cp), lambda *i: (0, *i, 0)), rs, rs, rs],
        out_specs=(rs, rs, rs, rs), out_shape=tuple(_sds(w.shape, F32) for _ in range(4)),
        cp=_cp(48, ("arbitrary",) * (nl + 1)), args=(recv, w, m, v))[0]


def adam_block(recv, w, m, v, lf, prev, name):
    half = FB // 2

    def body(*refs):
        rc_ref, w_ref, m_ref, v_ref = refs[:4]
        g_ref, d_ref, m2_ref, v2_ref = refs[-4:]
        g = _sum8(rc_ref)
        g_ref[...] = g
        d_ref[...], m2_ref[...], v2_ref[...] = _adamw(w_ref[...], g, m_ref[...], v_ref[...])

    rs = pl.BlockSpec((None, None, half, D), lambda i: (lf // 2, lf % 2, i, 0))
    prev = list(prev) if prev is not None else []
    return list(_pallas_call(
        body, name=name, grid=(2,), in_specs=[pl.BlockSpec((recv.shape[0], half, D), lambda i: (0, i, 0)), rs, rs, rs] + [ANY] * len(prev),
        out_specs=(rs, rs, rs, rs), out_shape=tuple(_sds((DEPTH, 2, FB, D), F32) for _ in range(4)),
        input_output_aliases={4 + k: k for k in range(len(prev))},
        compiler_params=_cp(48, ("arbitrary",)))(recv, w, m, v, *prev))


def adam_native(gs, ws, ms, vs, name):
    n = len(ws)

    def body(*refs):
        g_refs, w_refs, m_refs, v_refs = (refs[k * n:(k + 1) * n] for k in range(4))
        d_refs, m2_refs, v2_refs = (refs[(4 + k) * n:(5 + k) * n] for k in range(3))
        for a in range(n):
            d_refs[a][...], m2_refs[a][...], v2_refs[a][...] = _adamw(w_refs[a][...], g_refs[a][...], m_refs[a][...], v_refs[a][...])

    whole = lambda a: pl.BlockSpec(a.shape, lambda i, nd=len(a.shape): (0,) * nd)
    outs = _pallas_call(body, name=name, grid=(1,), in_specs=[whole(a) for a in (*gs, *ws, *ms, *vs)],
                        out_specs=tuple(whole(w) for w in ws) * 3,
                        out_shape=tuple(_sds(w.shape, F32) for w in ws) * 3, compiler_params=_cp(40),
                        pin_all=True)(*gs, *ws, *ms, *vs)
    return outs[:n], outs[n:2 * n], outs[2 * n:]


def sum_sources(recv, name):
    r = recv.shape[1]

    def body(rc_ref, o_ref):
        o_ref[...] = _sum8(rc_ref)

    vm = pl.BlockSpec(memory_space=pltpu.VMEM)
    return _pallas_call(body, name=name, in_specs=[vm], out_specs=vm, out_shape=_sds((r, 128), F32),
                          compiler_params=_cp(40))(recv)


def _pack(arrs, dtype=F32):
    flat = jnp.concatenate([a.reshape(-1) for a in arrs]).astype(dtype)
    n = flat.shape[0]
    tile = 128 * (32 // jnp.dtype(dtype).itemsize)
    rows = -(-n // tile) * (tile // 128)
    return jnp.pad(flat, (0, rows * 128 - n)).reshape(rows, 128)


def _unpack(vec, shapes):
    flat = vec.reshape(-1)
    out, o = [], 0
    for sh in shapes:
        n = int(np.prod(sh))
        out.append(flat[o:o + n].reshape(sh))
        o += n
    return out


WEIGHTS = ['rel_bias', 'ada_w', 'ada_b', 'ln_g', 'ln_b', 'ffn_w_gate', 'ffn_w_up', 'ffn_w_down', 'w_in', 'w_out',
           'ssm_a_re', 'ssm_a_im', 'ssm_log_dt', 'ssm_b_re', 'ssm_b_im', 'ssm_c_re', 'ssm_c_im', 'ssm_d', 'glu_w',
           'glu_b', 'pool_w', 'pool_scale']
SMALL = ['rel_bias', 'ada_b', 'ln_g', 'ln_b', 'ssm_a_re', 'ssm_a_im', 'ssm_log_dt', 'ssm_b_re', 'ssm_b_im',
         'ssm_c_re', 'ssm_c_im', 'ssm_d', 'glu_b', 'pool_w', 'pool_scale']
SMALL_EXACT = ['rel_bias', 'ln_g', 'ln_b', 'ssm_a_re', 'ssm_a_im', 'ssm_log_dt']
SMALL_ROUNDED = ['ada_b', 'ssm_b_re', 'ssm_b_im', 'ssm_c_re', 'ssm_c_im', 'ssm_d', 'glu_b', 'pool_w', 'pool_scale']
SMALL_FULL_SHAPES = {'rel_bias': (32, 8), 'ada_b': (2, 9216), 'ln_g': (2, 3, 1024), 'ln_b': (2, 3, 1024),
                     'ssm_a_re': (2, 16, 64), 'ssm_a_im': (2, 16, 64), 'ssm_log_dt': (2, 16),
                     'ssm_b_re': (2, 16, 64, 16), 'ssm_b_im': (2, 16, 64, 16), 'ssm_c_re': (2, 16, 16, 64),
                     'ssm_c_im': (2, 16, 16, 64), 'ssm_d': (2, 256), 'glu_b': (2, 256), 'pool_w': (2, 4, 64, 64),
                     'pool_scale': (2, 256)}


def _step(P):
    me = _me()
    x0 = P['x'][0]
    target = P['loss_target'][0]

    bf = lambda a: a.astype(BF16)
    padr = lambda a: jnp.pad(bf(a), ((0, 0), (0, 0), (0, FBP - FB), (0, 0)))
    ffn_b = [padr(jnp.swapaxes(P['ffn_w_gate'], 2, 3)), padr(jnp.swapaxes(P['ffn_w_up'], 2, 3)), padr(P['ffn_w_down'])]
    mix_b = [bf(P['w_in']), bf(P['w_out']), bf(P['glu_w'])]

    def shards(l, sub):
        return [t[l] for t in mix_b] if sub == 1 else [t[l, sub // 2] for t in ffn_b]

    order = [(l, sub) for l in range(DEPTH) for sub in range(3)]
    nxt = dict(zip(order[:-1], order[1:]))
    W = {key: [None] * 3 for key in order}
    c_all, lng_all, lnb_all, *W[order[0]] = _exchange(Gather([P['c'], P['ln_g'], P['ln_b']] + shards(*order[0])), "gather_first")
    gather_queue = [(key, pos, a) for key in order[1:] for pos, a in enumerate(shards(*key))]

    def gather_ride(cap_us, must=None):
        units, used = [], 0.0
        while gather_queue:
            key, _, a = gather_queue[0]
            cost = a.size * a.dtype.itemsize * GATHER_US_PER_BYTE
            if key != must and used + cost / 2 > cap_us:
                break
            units.append(gather_queue.pop(0))
            used += cost
        return (Gather([a for _, _, a in units]) if units else None), units

    def gathered(units, outs):
        for (key, pos, _), o in zip(units, outs):
            W[key][pos] = o

    c_all = c_all.reshape(NDEV, D)
    ln_g = jnp.transpose(lng_all, (1, 2, 0, 3)).reshape(DEPTH, 3, D)
    ln_b = jnp.transpose(lnb_all, (1, 2, 0, 3)).reshape(DEPTH, 3, D)

    ada_b_cols = lax.dynamic_slice_in_dim(P['ada_b'], me * 1152, 1152, axis=1).reshape(DEPTH, 1, 1152)
    modc = ada_fwd(c_all, P['ada_w'], ada_b_cols)
    (mod_all,) = _exchange(Gather([modc]), "gather_mod")
    mod_me = lax.dynamic_index_in_dim(mod_all, me, axis=2, keepdims=False)
    mod = jnp.transpose(mod_me, (1, 0, 2)).reshape(DEPTH, 9, D)

    bias = att_bias(P['rel_bias'])
    ssm = []
    for l in range(DEPTH):
        prm = (P['ssm_a_re'][l], P['ssm_a_im'][l], P['ssm_log_dt'][l], P['ssm_b_re'][l], P['ssm_b_im'][l])
        (abr, abi, bbr, bbi), disc_vjp = jax.vjp(_ssm_discretise, *prm)
        ssm.append(dict(
            vjp=disc_vjp, a=jnp.stack([abr.reshape(8, 128), abi.reshape(8, 128)]),
            bre=_blockdiag(jnp.transpose(bbr, (0, 2, 1))).astype(MXU), bim=_blockdiag(jnp.transpose(bbi, (0, 2, 1))).astype(MXU),
            cre=_blockdiag(jnp.transpose(P['ssm_c_re'][l], (0, 2, 1))).astype(MXU),
            cim=_blockdiag(jnp.transpose(P['ssm_c_im'][l], (0, 2, 1))).astype(MXU),
            d=P['ssm_d'][l].reshape(1, 256), gb=P['glu_b'][l].reshape(1, 256),
            pw=_blockdiag(P['pool_w'][l]).astype(MXU), psc=P['pool_scale'][l].reshape(1, 256)))

    saved = []
    x = x0
    h = ln_mod_fwd(x, mod[0], 0, "ln_mod_fwd_l0s0")
    for l, sub in order:
        tag = f"l{l}s{sub}"
        after = (mod[nxt[(l, sub)][0]], nxt[(l, sub)][1]) if (l, sub) in nxt else None
        if sub != 1:
            wg, wu, wd = (t.reshape(NDEV * FBP, D) for t in W[(l, sub)])
            ride, units = gather_ride(60, nxt.get((l, sub)))
            (G, U, fo), got = ffn_fwd(h, wg, wu, wd, "ffn_fwd_" + tag, ride)
            gathered(units, got)
            saved.append(dict(x=x, h=h, G=G, U=U, f=fo))
            x, *hn = res_ln_fwd(x, fo, mod[l], sub, ln_g[l], ln_b[l], 0.5, "res_ln_fwd_" + tag, after)
        else:
            sp = ssm[l]
            win, wout, gluw = W[(l, sub)]
            ride, units = gather_ride(15)
            (z,), got = win_fwd(h, win, "win_fwd_" + tag, ride)
            gathered(units, got)
            ride, units = gather_ride(55)
            (ya, lse), got = att_fwd(z, bias, "att_fwd_" + tag, ride)
            gathered(units, got)
            ride, units = gather_ride(35)
            (ys, ypre, st), got = ssm_fwd(z, sp['a'], sp['bre'], sp['bim'], sp['cre'], sp['cim'], sp['d'], gluw, sp['gb'],
                                          "ssm_fwd_" + tag, ride)
            gathered(units, got)
            yp = pool_fwd(z, sp['pw'], sp['psc'], "pool_fwd_" + tag)
            ride, units = gather_ride(12, nxt.get((l, sub)))
            (o,), got = wout_fwd(ya, ys, yp, wout, "wout_fwd_" + tag, ride)
            gathered(units, got)
            saved.append(dict(x=x, h=h, z=z, ya=ya, lse=lse, ys=ys, ypre=ypre, st=st, yp=yp, f=o))
            x, *hn = res_ln_fwd(x, o, mod[l], sub, ln_g[l], ln_b[l], 1.0, "res_ln_fwd_" + tag, after)
        h = hn[0] if hn else None
    assert not gather_queue

    loss_tile, dx = loss_fwd_bwd(x, target, "loss")
    loss = lax.psum(loss_tile[0, 0], ("x", "y", "c"))

    flights = []

    dmod = [[None] * 9 for _ in range(DEPTH)]
    dlng = [[None] * 3 for _ in range(DEPTH)]
    dlnb = [[None] * 3 for _ in range(DEPTH)]
    dbiases = [None] * DEPTH
    small_l = [dict() for _ in range(DEPTH)]
    for l, sub in reversed(order):
        tag = f"l{l}s{sub}"
        sv = saved[3 * l + sub]
        if (l, sub) == order[-1]:
            dxa, df, sums = res_ln_bwd(sv['x'], sv['f'], mod[l], sub, ln_g[l], dx, 0.5, "res_ln_bwd_" + tag)
        dlng[l][sub], dlnb[l][sub], dmod[l][3 * sub + 2] = sums[0], sums[1], sums[2]
        if sub != 1:
            f = sub // 2
            wg, wu, wd = (t.reshape(NDEV * FBP, D) for t in W[(l, sub)])
            dwg, dwu, dwd, dh = ffn_bwd(df, sv['h'], sv['G'], sv['U'], wg, wu, wd, "ffn_bwd_" + tag)
            handle, zero = scatter_start([t.reshape(NDEV, FBP, D) for t in (dwg, dwu, dwd)], "scatter_start_" + tag, rows=FB)
            flights.append(((l, sub), handle))
        else:
            sp = ssm[l]
            win, wout, gluw = W[(l, sub)]
            dya, dys, dyp, dwout = wout_bwd(df, sv['ya'], sv['ys'], sv['yp'], wout, "wout_bwd_" + tag)
            dq, dk, dv, dbiases[l] = att_bwd(sv['z'], bias, sv['ya'], sv['lse'], dya, "att_bwd_" + tag)
            dus, dbre, dbim, dcre, dcim, da, dd, dgw, dgb = ssm_bwd(
                dys, sv['z'], sv['ypre'], sv['st'], sp['a'], sp['bre'], sp['bim'], sp['cre'], sp['cim'], sp['d'],
                gluw, sp['gb'], "ssm_bwd_" + tag)
            dup, dpw, dpsc = pool_bwd(dyp, sv['z'], sp['pw'], sp['psc'], "pool_bwd_" + tag)
            dh, dwin = win_bwd((dq, dk, dv, dus, dup), sv['h'], win, "win_bwd_" + tag)
            handle, zero = scatter_start([dwin, dwout, dgw.astype(BF16).reshape(NDEV, 32, 256)], "scatter_start_" + tag)
            flights.append(((l, sub), handle))
            d_are, d_aim, d_ldt, d_bre, d_bim = sp['vjp']((
                da[0].reshape(16, 64), da[1].reshape(16, 64),
                jnp.transpose(_blockdiag_take(dbre, 16, 64), (0, 2, 1)), jnp.transpose(_blockdiag_take(dbim, 16, 64), (0, 2, 1))))
            small_l[l] = dict(
                ssm_a_re=d_are, ssm_a_im=d_aim, ssm_log_dt=d_ldt, ssm_b_re=d_bre, ssm_b_im=d_bim,
                ssm_c_re=jnp.transpose(_blockdiag_take(dcre, 64, 16), (0, 2, 1)),
                ssm_c_im=jnp.transpose(_blockdiag_take(dcim, 64, 16), (0, 2, 1)),
                ssm_d=dd.reshape(256), glu_b=dgb.reshape(256), pool_w=_blockdiag_take(dpw, 64, 64), pool_scale=dpsc.reshape(256))
        if (l, sub) == order[0]:
            dx, sums2 = ln_mod_bwd(sv['x'], dh, mod[l] + zero, sub, dxa, "ln_mod_bwd_" + tag)
        else:
            lp, sp_ = order[order.index((l, sub)) - 1]
            svp = saved[3 * lp + sp_]
            dxa, df, sums, sums2 = ln_join_bwd(svp['x'], svp['f'], mod[lp], sp_, ln_g[lp], ln_b[lp], 1.0 if sp_ == 1 else 0.5,
                                               dh, mod[l] + zero, sub, dxa, "ln_join_bwd_" + tag)
        dmod[l][3 * sub], dmod[l][3 * sub + 1] = sums2[0], sums2[1]
    grad_x = dx[None]

    small = {k: jnp.stack([small_l[l][k] for l in range(DEPTH)]) for k in small_l[0]}
    small['rel_bias'] = relbias_grad(dbiases)
    small['ada_b'] = jnp.stack([jnp.stack(dmod[l]).reshape(9 * D) for l in range(DEPTH)])
    small['ln_g'] = jnp.stack([jnp.stack(dlng[l]) for l in range(DEPTH)])
    small['ln_b'] = jnp.stack([jnp.stack(dlnb[l]) for l in range(DEPTH)])
    swaps = {'rel_bias': (0, 1), 'ln_g': (0, 1), 'ln_b': (0, 1), 'ssm_b_re': (2, 3), 'ssm_b_im': (2, 3)}
    view = lambda k, t: jnp.swapaxes(t, *swaps[k]) if k in swaps else t
    kept_shape = lambda k: np.swapaxes(np.empty(SMALL_FULL_SHAPES[k], np.bool_), *swaps.get(k, (0, 0))).shape
    small_flight, _ = scatter_start([_pack([view(k, small[k]) for k in SMALL_EXACT]),
                                     _pack([view(k, small[k]) for k in SMALL_ROUNDED], BF16)], "gather_small_start", whole=True)

    out = {}

    def put(name, g, d, m2, v2, shape):
        out['grad_' + name], out['delta_' + name] = g.reshape(shape), d.reshape(shape)
        out['new_m_' + name], out['new_v_' + name] = m2.reshape(shape), v2.reshape(shape)

    def wmv(name):
        return [P[pre + name] for pre in ('', 'm_', 'v_')]

    recv = {}
    started_last = small_flight[1][0]
    for key, handle in flights[:-1]:
        recv[key] = scatter_wait(handle, started_last, "scatter_wait_l%ds%d" % key)
    mixer_done = []
    for pos, (name, tr) in enumerate((('w_in', 512), ('w_out', 128), ('glu_w', 32))):
        both = jnp.stack([recv[(l, 1)][pos] for l in range(DEPTH)], axis=1)
        res = adam_rs(both, *wmv(name), tr, "adam_" + name)
        mixer_done.append(res[0])
        put(name, *res, P[name].shape)
    ffn = (('ffn_w_gate', [jnp.swapaxes(t, 2, 3) for t in wmv('ffn_w_gate')]),
           ('ffn_w_up', [jnp.swapaxes(t, 2, 3) for t in wmv('ffn_w_up')]), ('ffn_w_down', wmv('ffn_w_down')))
    part = [None] * 3
    for l, sub in [key for key, _ in flights[:-1] if key[1] != 1]:
        for pos, (name, ops) in enumerate(ffn):
            part[pos] = adam_block(recv[(l, sub)][pos], *ops, 2 * l + sub // 2, part[pos], f"adam_{name}_l{l}s{sub}")

    (l, sub), handle = flights[-1]
    last = scatter_wait(handle, [p[0] for p in part] + mixer_done[:2], "scatter_wait_l%ds%d" % (l, sub))
    for pos, (name, ops) in enumerate(ffn):
        part[pos] = adam_block(last[pos], *ops, 2 * l + sub // 2, part[pos], f"adam_{name}_l{l}s{sub}")
        put(name, *([jnp.swapaxes(t, 2, 3) for t in part[pos]] if pos < 2 else part[pos]), P[name].shape)
    exact_all, rounded_all = scatter_wait(small_flight, [p[0] for p in part], "gather_small_wait")
    gsum = dict(zip(SMALL_EXACT, _unpack(sum_sources(exact_all, "sum_small_exact"), [kept_shape(k) for k in SMALL_EXACT])))
    gsum.update(zip(SMALL_ROUNDED, _unpack(sum_sources(rounded_all, "sum_small_rounded"), [kept_shape(k) for k in SMALL_ROUNDED])))
    dmod_cols = jnp.stack([lax.dynamic_slice_in_dim(rounded_all, 72 * l + 9 * me, 9, axis=1).astype(F32).reshape(NDEV, 1152)
                           for l in range(DEPTH)])
    put('ada_w', *ada_bwd_adam(c_all, dmod_cols, *wmv('ada_w'), 256), P['ada_w'].shape)

    for k in ('ln_g', 'ln_b'):
        gsum[k] = lax.dynamic_slice_in_dim(gsum[k], me * 128, 128, axis=2)
    ds_, m2s, v2s = adam_native([gsum[k] for k in SMALL], *[[view(k, P[pre + k]) for k in SMALL] for pre in ('', 'm_', 'v_')],
                                "adam_small")
    for k, d, m2, v2 in zip(SMALL, ds_, m2s, v2s):
        put(k, view(k, gsum[k]), view(k, d), view(k, m2), view(k, v2), P[k].shape)

    res = [loss, grad_x]
    for pre in ('grad_', 'delta_', 'new_m_', 'new_v_'):
        res += [out[pre + k] for k in WEIGHTS]
    return tuple(res)


def kernel(x, c, rel_bias, ada_w, ada_b, ln_g, ln_b, ffn_w_gate, ffn_w_up, ffn_w_down, w_in, w_out, ssm_a_re, ssm_a_im, ssm_log_dt, ssm_b_re, ssm_b_im, ssm_c_re, ssm_c_im, ssm_d, glu_w, glu_b, pool_w, pool_scale, loss_target, m_rel_bias, m_ada_w, m_ada_b, m_ln_g, m_ln_b, m_ffn_w_gate, m_ffn_w_up, m_ffn_w_down, m_w_in, m_w_out, m_ssm_a_re, m_ssm_a_im, m_ssm_log_dt, m_ssm_b_re, m_ssm_b_im, m_ssm_c_re, m_ssm_c_im, m_ssm_d, m_glu_w, m_glu_b, m_pool_w, m_pool_scale, v_rel_bias, v_ada_w, v_ada_b, v_ln_g, v_ln_b, v_ffn_w_gate, v_ffn_w_up, v_ffn_w_down, v_w_in, v_w_out, v_ssm_a_re, v_ssm_a_im, v_ssm_log_dt, v_ssm_b_re, v_ssm_b_im, v_ssm_c_re, v_ssm_c_im, v_ssm_d, v_glu_w, v_glu_b, v_pool_w, v_pool_scale):
    return _step(dict(locals()))
```

```python
import functools
import math

import numpy as np
import jax
import jax.numpy as jnp
from jax import lax
from jax.experimental import pallas as pl
from jax.experimental.pallas import tpu as pltpu

F32 = jnp.float32
BF16 = jnp.bfloat16
MXU = jnp.bfloat16

S = 2048
D = 1024
NDEV = 8
DEPTH = 2
D_ATT, D_SSM, D_POOL, D_IN = 512, 256, 256, 2048
N_HEADS = 8
FB = 352
FBP = 384
QB = 128
PATTERNS = ((128, 1), (512, 4), (2048, 16))
POOL_WINDOWS = (2, 4, 8, 16)
N_BUCKETS, MAX_DISTANCE = 32, 2048
ALPHA = (2 * DEPTH) ** 0.25
LN_EPS = 1e-5
NEG = -1e30
GATHER_US_PER_BYTE = 43e-6
LR, B1, B2, EPS, WD, STEP = 0.001, 0.9, 0.999, 1e-08, 0.01, 10

TM = 256
TMM = 512
MIB = 1024 * 1024


def _cp(vmem_mib, sem=None):
    kw = dict(vmem_limit_bytes=vmem_mib * MIB)
    if sem is not None:
        kw["dimension_semantics"] = sem
    return pltpu.CompilerParams(**kw)


def _sds(shape, dtype):
    return jax.ShapeDtypeStruct(shape, dtype)


def _mm(a, b):
    return jnp.dot(a.astype(MXU), b.astype(MXU), preferred_element_type=F32)


def _mm_nt(a, b):
    return lax.dot_general(a.astype(MXU), b.astype(MXU), (((1,), (1,)), ((), ())), preferred_element_type=F32)


def _mm_tn(a, b):
    return lax.dot_general(a.astype(MXU), b.astype(MXU), (((0,), (0,)), ((), ())), preferred_element_type=F32)


def _ln_stats(x):
    mu = jnp.mean(x, axis=-1, keepdims=True)
    xc = x - mu
    var = jnp.mean(xc * xc, axis=-1, keepdims=True)
    rstd = lax.rsqrt(var + LN_EPS)
    return xc * rstd, rstd


def _ln_bwd(dn, n, rstd):
    return rstd * (dn - jnp.mean(dn, axis=-1, keepdims=True) - n * jnp.mean(dn * n, axis=-1, keepdims=True))


def _me():
    return 4 * lax.axis_index("x") + 2 * lax.axis_index("y") + lax.axis_index("c")


ANY = pl.BlockSpec(memory_space=pl.ANY)
PIN_BYTES = 1 << 19


def _pallas_call(*a, **k):
    pin_all = k.pop("pin_all", False)
    big = lambda o: pin_all or math.prod(o.shape) * o.dtype.itemsize >= PIN_BYTES
    pin = lambda o: pltpu.HBM(o.shape, o.dtype) if isinstance(o, jax.ShapeDtypeStruct) and big(o) else o
    osh = k["out_shape"]
    k["out_shape"] = tuple(pin(o) for o in osh) if isinstance(osh, (tuple, list)) else pin(osh)
    fn = pl.pallas_call(*a, **k)

    def run(*args):
        return fn(*[pltpu.with_memory_space_constraint(x, pltpu.HBM) if big(x) else x for x in args])
    return run


class Gather:
    def __init__(self, srcs):
        self.srcs = list(srcs)
        self.n = len(self.srcs)
        self.bufs = []
        self.out_shapes = [_sds((NDEV,) + a.shape, a.dtype) for a in self.srcs]
        self.sems = [pltpu.SemaphoreType.DMA((7 * self.n,)), pltpu.SemaphoreType.DMA((7 * self.n,)),
                     pltpu.SemaphoreType.DMA((self.n,))]

    def _parts(self, srcs, outs, sems):
        send_sems, recv_sems, loc_sems = sems
        x, y, c = lax.axis_index("x"), lax.axis_index("y"), lax.axis_index("c")
        me, sib = (x, y, c), (x, y, 1 - c)
        chips = [(1 - x, y), (x, 1 - y), (1 - x, 1 - y)]
        slot = lambda d: 4 * d[0] + 2 * d[1] + d[2]

        def copy(a, k, block, to, src=None):
            dst = outs[a].at[slot(block)]
            return pltpu.make_async_remote_copy(
                src_ref=dst if src is None else src, dst_ref=dst,
                send_sem=send_sems.at[7 * a + k], recv_sem=recv_sems.at[7 * a + k],
                device_id=to, device_id_type=pl.DeviceIdType.MESH)

        local = [pltpu.make_async_copy(srcs[a], outs[a].at[slot(me)], loc_sems.at[a]) for a in range(self.n)]
        return me, sib, chips, c, copy, local

    def start(self, srcs, bufs, outs, sems):
        me, sib, chips, c, copy, local = self._parts(srcs, outs, sems)
        for a in range(self.n):
            local[a].start()
            copy(a, 0, me, sib, src=srcs[a]).start()
            for j, chip in enumerate(chips):
                copy(a, 1 + j, me, (*chip, c), src=srcs[a]).start()

    def finish(self, srcs, bufs, outs, sems):
        me, sib, chips, c, copy, local = self._parts(srcs, outs, sems)
        for a in range(self.n):
            for j, chip in enumerate(chips):
                copy(a, 1 + j, (*chip, c), me).wait_recv()
                copy(a, 4 + j, (*chip, c), sib).start()
        for a in range(self.n):
            copy(a, 0, sib, me).wait_recv()
            copy(a, 0, me, sib, src=srcs[a]).wait_send()
            for j, chip in enumerate(chips):
                copy(a, 4 + j, (*chip, 1 - c), me).wait_recv()
                copy(a, 1 + j, me, (*chip, c), src=srcs[a]).wait_send()
                copy(a, 4 + j, (*chip, c), sib).wait_send()
            local[a].wait()


def _call(body, *, name, grid, in_specs, out_specs, out_shape, args, scratch=(), cp=None, ride=None):
    out_specs, out_shape, scratch = list(out_specs), list(out_shape), list(scratch)
    if ride is None:
        outs = _pallas_call(body, name=name, grid=grid, in_specs=list(in_specs), out_specs=tuple(out_specs),
                              out_shape=tuple(out_shape), scratch_shapes=scratch, compiler_params=cp)(*args)
        return list(outs), []
    nin, nout, nscr, n, nb, no = len(in_specs), len(out_specs), len(scratch), ride.n, len(ride.bufs), len(ride.out_shapes)
    steps = list(grid)

    def wrapped(*refs):
        h_in, r_src, r_buf = refs[:nin], refs[nin:nin + n], refs[nin + n:nin + n + nb]
        o0 = nin + n + nb
        h_out, r_out = refs[o0:o0 + nout], refs[o0 + nout:o0 + nout + no]
        s0 = o0 + nout + no
        h_scr, sems = refs[s0:s0 + nscr], refs[s0 + nscr:]
        ids = [pl.program_id(a) for a in range(len(steps))]
        first = functools.reduce(jnp.logical_and, [i == 0 for i in ids])
        last = functools.reduce(jnp.logical_and, [i == s - 1 for i, s in zip(ids, steps)])

        @pl.when(first)
        def _():
            ride.start(r_src, r_buf, r_out, sems)

        body(*h_in, *h_out, *h_scr)

        @pl.when(last)
        def _():
            ride.finish(r_src, r_buf, r_out, sems)

    aliases = {nin + n + k: nout + k for k in range(nb)}
    outs = _pallas_call(
        wrapped, name=name, grid=grid, in_specs=list(in_specs) + [ANY] * (n + nb),
        out_specs=tuple(out_specs + [ANY] * no), out_shape=tuple(out_shape + ride.out_shapes),
        scratch_shapes=scratch + ride.sems, input_output_aliases=aliases, compiler_params=cp,
    )(*args, *ride.srcs, *ride.bufs)
    return list(outs[:nout]), list(outs[nout:])


def _exchange(ride, name):
    def body(dummy_ref, o_ref):
        o_ref[...] = dummy_ref[...]

    one = pl.BlockSpec((8, 128), lambda i: (0, 0))
    _, outs = _call(body, name=name, grid=(1,), in_specs=[one], out_specs=[one], out_shape=[_sds((8, 128), F32)],
                    args=(jnp.zeros((8, 128), F32),), ride=ride)
    return outs


HBM = pl.BlockSpec(memory_space=pltpu.HBM)
SEM = pl.BlockSpec(memory_space=pltpu.SEMAPHORE)


def routes_all(me):
    return [(k, me ^ k, me, me ^ k) for k in range(NDEV)]


def _scatter_copies(srcs, lands, sems, sending, whole, rows):
    send_sems, recv_sems, loc_sems = sems
    me = _me()
    rts = routes_all(me)
    part = lambda ref, slab: ref if whole else ref.at[slab] if rows is None else ref.at[slab, pl.ds(0, rows)]
    remote_ix = [r for r, (k, _, _, _) in enumerate(rts) if k != 0]
    local_ix = [r for r, (k, _, _, _) in enumerate(rts) if k == 0]
    remote, local = [], []
    for a in range(len(srcs)):
        for n, r in enumerate(remote_ix):
            k, slab, there, here = rts[r]
            t = me ^ k
            sem = len(remote_ix) * a + n
            remote.append(pltpu.make_async_remote_copy(
                src_ref=part(srcs[a], slab), dst_ref=lands[a].at[there if sending else here], send_sem=send_sems.at[sem],
                recv_sem=recv_sems.at[sem], device_id=(t // 4, (t // 2) % 2, t % 2), device_id_type=pl.DeviceIdType.MESH))
        for n, r in enumerate(local_ix):
            _, slab, there, _ = rts[r]
            local.append(pltpu.make_async_copy(part(srcs[a], slab), lands[a].at[there], loc_sems.at[len(local_ix) * a + n]))
    return remote, local


def scatter_start(payloads, name, whole=False, rows=None, after=()):
    n = len(payloads)
    nr = NDEV - 1
    after = list(after)
    na = len(after)

    def body(*refs):
        srcs, lands, sems = refs[:n], refs[n:2 * n], refs[2 * n + na:2 * n + na + 3]
        remote, local = _scatter_copies(srcs, lands, sems, True, whole, rows)
        for cp in local + remote:
            cp.start()
        refs[-1][...] = jnp.zeros((8, 128), F32)

    thru = [pltpu.HBM(p.shape, p.dtype) for p in payloads]
    land_shapes = [(NDEV,) + p.shape if whole else p.shape if rows is None else (NDEV, rows) + p.shape[2:] for p in payloads]
    outs = pl.pallas_call(
        body, name=name,
        out_shape=(pltpu.SemaphoreType.DMA((nr * n,)), pltpu.SemaphoreType.DMA((nr * n,)), pltpu.SemaphoreType.DMA((n,)),
                   *thru, *[pltpu.HBM(sh, p.dtype) for sh, p in zip(land_shapes, payloads)], _sds((8, 128), F32)),
        in_specs=[HBM] * (2 * n + na),
        out_specs=(SEM, SEM, SEM, *[HBM] * (2 * n), pl.BlockSpec(memory_space=pltpu.VMEM)),
        input_output_aliases={i: 3 + i for i in range(2 * n)},
        compiler_params=pltpu.CompilerParams(has_side_effects=pltpu.SideEffectType.DATAFLOW_SIDE_EFFECTING),
    )(*[pltpu.with_memory_space_constraint(p, pltpu.HBM) for p in payloads],
      *[pltpu.with_memory_space_constraint(lax.empty(sh, p.dtype), pltpu.HBM) for sh, p in zip(land_shapes, payloads)],
      *[pltpu.with_memory_space_constraint(a, pltpu.HBM) for a in after])
    return (outs[:3], outs[3:3 + n], outs[3 + n:3 + 2 * n], whole, rows), outs[-1][0, 0]


def scatter_wait(handle, after, name):
    sems, srcs_thru, lands_thru, whole, rows = handle
    n = len(srcs_thru)
    after = list(after) if isinstance(after, (list, tuple)) else [after]

    def body(*refs):
        srcs, lands, sems_ = refs[:n], refs[n:2 * n], refs[2 * n:2 * n + 3]
        remote, local = _scatter_copies(srcs, lands, sems_, False, whole, rows)
        for cp in remote:
            cp.wait_send()
            cp.wait_recv()
        for cp in local:
            cp.wait()

    outs = pl.pallas_call(
        body, name=name, out_shape=tuple(pltpu.HBM(p.shape, p.dtype) for p in (*srcs_thru, *lands_thru)),
        in_specs=[HBM] * (2 * n) + [SEM] * 3 + [HBM] * len(after), out_specs=tuple([HBM] * (2 * n)),
        input_output_aliases={i: i for i in range(2 * n)},
        compiler_params=pltpu.CompilerParams(has_side_effects=pltpu.SideEffectType.DATAFLOW_SIDE_EFFECTING),
    )(*srcs_thru, *lands_thru, *sems, *[pltpu.with_memory_space_constraint(a, pltpu.HBM) for a in after])
    return list(outs[n:])


def _row_spec(cols, tm=TM):
    return pl.BlockSpec((tm, cols), lambda i: (i, 0))


def _full_spec(shape):
    nd = len(shape)
    return pl.BlockSpec(shape, lambda i: (0,) * nd)


def ln_mod_fwd(x, mod, sub, name):
    def body(x_ref, mod_ref, h_ref):
        n, _ = _ln_stats(x_ref[...])
        shift = mod_ref[3 * sub:3 * sub + 1, :]
        scale = mod_ref[3 * sub + 1:3 * sub + 2, :]
        h_ref[...] = (n * (1.0 + scale) + shift).astype(MXU)

    return _pallas_call(
        body, name=name, grid=(S // TM,),
        in_specs=[_row_spec(D), _full_spec((9, D))], out_specs=_row_spec(D),
        out_shape=_sds((S, D), MXU), compiler_params=_cp(32, ("arbitrary",)))(x, mod)


def res_ln_fwd(x, f, mod, sub, lng, lnb, w, name, nxt=None):
    def body(x_ref, f_ref, mod_ref, g_ref, b_ref, *rest):
        gate = mod_ref[3 * sub + 2:3 * sub + 3, :]
        r = ALPHA * x_ref[...] + (w * gate) * f_ref[...]
        n, _ = _ln_stats(r)
        xo = n * g_ref[sub:sub + 1, :] + b_ref[sub:sub + 1, :]
        rest[-1 if nxt is None else -2][...] = xo
        if nxt is not None:
            nmod_ref, h_ref = rest[0], rest[-1]
            n2, _ = _ln_stats(xo)
            s2 = nxt[1]
            h_ref[...] = (n2 * (1.0 + nmod_ref[3 * s2 + 1:3 * s2 + 2, :]) + nmod_ref[3 * s2:3 * s2 + 1, :]).astype(MXU)

    more = nxt is not None
    return _pallas_call(
        body, name=name, grid=(S // TM,),
        in_specs=[_row_spec(D), _row_spec(D), _full_spec((9, D)), _full_spec((3, D)), _full_spec((3, D))] + [_full_spec((9, D))] * more,
        out_specs=(_row_spec(D),) + (_row_spec(D),) * more, out_shape=(_sds((S, D), F32),) + (_sds((S, D), MXU),) * more,
        compiler_params=_cp(32, ("arbitrary",)))(x, f, mod, lng, lnb, *([nxt[0]] if more else []))


def res_ln_bwd(x, f, mod, sub, lng, dxo, w, name):
    def body(x_ref, f_ref, mod_ref, g_ref, dxo_ref, dxa_ref, df_ref, sums_ref):
        i = pl.program_id(0)
        gate = mod_ref[3 * sub + 2:3 * sub + 3, :]
        fv = f_ref[...]
        r = ALPHA * x_ref[...] + (w * gate) * fv
        n, rstd = _ln_stats(r)
        dxo = dxo_ref[...]
        dr = _ln_bwd(dxo * g_ref[sub:sub + 1, :], n, rstd)
        dxa_ref[...] = ALPHA * dr
        df_ref[...] = ((w * gate) * dr).astype(MXU)
        part = jnp.concatenate([
            jnp.sum(dxo * n, axis=0, keepdims=True),
            jnp.sum(dxo, axis=0, keepdims=True),
            jnp.sum(dr * fv, axis=0, keepdims=True) * w,
            jnp.zeros((5, D), F32)], axis=0)

        @pl.when(i == 0)
        def _():
            sums_ref[...] = part

        @pl.when(i > 0)
        def _():
            sums_ref[...] += part

    return _call(
        body, name=name, grid=(S // TM,),
        in_specs=[_row_spec(D), _row_spec(D), _full_spec((9, D)), _full_spec((3, D)), _row_spec(D)],
        out_specs=(_row_spec(D), _row_spec(D), _full_spec((8, D))),
        out_shape=(_sds((S, D), F32), _sds((S, D), MXU), _sds((8, D), F32)),
        cp=_cp(32, ("arbitrary",)), args=(x, f, mod, lng, dxo))[0]


def ln_mod_bwd(x, dh, mod, sub, dxa, name):
    def body(x_ref, dh_ref, mod_ref, dxa_ref, dx_ref, sums_ref):
        i = pl.program_id(0)
        scale = mod_ref[3 * sub + 1:3 * sub + 2, :]
        n, rstd = _ln_stats(x_ref[...])
        dh = dh_ref[...]
        dx_ref[...] = dxa_ref[...] + _ln_bwd(dh * (1.0 + scale), n, rstd)
        part = jnp.concatenate([
            jnp.sum(dh, axis=0, keepdims=True),
            jnp.sum(dh * n, axis=0, keepdims=True),
            jnp.zeros((6, D), F32)], axis=0)

        @pl.when(i == 0)
        def _():
            sums_ref[...] = part

        @pl.when(i > 0)
        def _():
            sums_ref[...] += part

    return _call(
        body, name=name, grid=(S // TM,),
        in_specs=[_row_spec(D), _row_spec(D), _full_spec((9, D)), _row_spec(D)],
        out_specs=(_row_spec(D), _full_spec((8, D))),
        out_shape=(_sds((S, D), F32), _sds((8, D), F32)),
        cp=_cp(32, ("arbitrary",)), args=(x, dh, mod, dxa))[0]


def ln_join_bwd(xp, fp, modp, subp, lngp, lnbp, wp, dh, mod, sub, dxa, name):
    def body(xp_ref, fp_ref, modp_ref, g_ref, b_ref, dh_ref, mod_ref, dxa_ref, dxap_ref, dfp_ref, sumsp_ref, sums_ref):
        i = pl.program_id(0)
        gate = modp_ref[3 * subp + 2:3 * subp + 3, :]
        fv = fp_ref[...]
        n, rstd = _ln_stats(ALPHA * xp_ref[...] + (wp * gate) * fv)
        gain = g_ref[subp:subp + 1, :]
        n2, rstd2 = _ln_stats(n * gain + b_ref[subp:subp + 1, :])
        dh = dh_ref[...]
        dx = dxa_ref[...] + _ln_bwd(dh * (1.0 + mod_ref[3 * sub + 1:3 * sub + 2, :]), n2, rstd2)
        dr = _ln_bwd(dx * gain, n, rstd)
        dxap_ref[...] = ALPHA * dr
        dfp_ref[...] = ((wp * gate) * dr).astype(MXU)
        partp = jnp.concatenate([
            jnp.sum(dx * n, axis=0, keepdims=True), jnp.sum(dx, axis=0, keepdims=True),
            jnp.sum(dr * fv, axis=0, keepdims=True) * wp, jnp.zeros((5, D), F32)], axis=0)
        part = jnp.concatenate([
            jnp.sum(dh, axis=0, keepdims=True), jnp.sum(dh * n2, axis=0, keepdims=True), jnp.zeros((6, D), F32)], axis=0)

        @pl.when(i == 0)
        def _():
            sumsp_ref[...] = partp
            sums_ref[...] = part

        @pl.when(i > 0)
        def _():
            sumsp_ref[...] += partp
            sums_ref[...] += part

    return _pallas_call(
        body, name=name, grid=(S // TM,),
        in_specs=[_row_spec(D), _row_spec(D), _full_spec((9, D)), _full_spec((3, D)), _full_spec((3, D)), _row_spec(D),
                  _full_spec((9, D)), _row_spec(D)],
        out_specs=(_row_spec(D), _row_spec(D), _full_spec((8, D)), _full_spec((8, D))),
        out_shape=(_sds((S, D), F32), _sds((S, D), MXU), _sds((8, D), F32), _sds((8, D), F32)),
        compiler_params=_cp(40, ("arbitrary",)))(xp, fp, modp, lngp, lnbp, dh, mod, dxa)


def loss_fwd_bwd(y, target, name):
    def body(y_ref, t_ref, l_ref, dy_ref):
        i = pl.program_id(0)
        e = y_ref[...] - t_ref[...]
        dy_ref[...] = e * (1.0 / D)
        part = jnp.zeros((8, 128), F32) + (0.5 / D) * jnp.sum(e * e)

        @pl.when(i == 0)
        def _():
            l_ref[...] = part

        @pl.when(i > 0)
        def _():
            l_ref[...] += part

    return _pallas_call(
        body, name=name, grid=(S // TM,),
        in_specs=[_row_spec(D), _row_spec(D)], out_specs=(_full_spec((8, 128)), _row_spec(D)),
        out_shape=(_sds((8, 128), F32), _sds((S, D), F32)),
        compiler_params=_cp(32, ("arbitrary",)))(y, target)


HB = 2 * FBP
NHB = NDEV * FBP // HB
TMB = 1024


def _wrows(buffers=2):
    return pl.BlockSpec((HB, D), lambda j, i: (j, 0), pipeline_mode=pl.Buffered(buffers))


def _resident(shape):
    return pl.BlockSpec(shape, lambda j, i: (0, 0), pipeline_mode=pl.Buffered(1))


def ffn_fwd(h, wgt, wut, wd, name, ride=None):
    def body(h_ref, wg_ref, wu_ref, wd_ref, g_ref, u_ref, f_ref):
        j, i = pl.program_id(0), pl.program_id(1)
        hv = h_ref[...]
        g = _mm_nt(hv, wg_ref[...])
        u = _mm_nt(hv, wu_ref[...])
        g_ref[...] = g.astype(MXU)
        u_ref[...] = u.astype(MXU)
        a = g * jax.nn.sigmoid(g) * u
        part = _mm(a, wd_ref[...])
        rows = pl.ds(pl.multiple_of(i * TMB, TMB), TMB)

        @pl.when(j == 0)
        def _():
            f_ref[rows, :] = part

        @pl.when(j > 0)
        def _():
            f_ref[rows, :] += part

    gu = pl.BlockSpec((TMB, HB), lambda j, i: (i, j))
    return _call(
        body, name=name, grid=(NHB, S // TMB),
        in_specs=[pl.BlockSpec((TMB, D), lambda j, i: (i, 0)), _wrows(), _wrows(), _wrows()],
        out_specs=(gu, gu, _resident((S, D))),
        out_shape=(_sds((S, NDEV * FBP), MXU), _sds((S, NDEV * FBP), MXU), _sds((S, D), F32)),
        cp=_cp(52, ("arbitrary", "arbitrary")), args=(h, wgt, wut, wd), ride=ride)


def ffn_bwd(df, h, g, u, wgt, wut, wd, name):
    ni = S // TMB

    def body(df_ref, h_ref, g_ref, u_ref, wg_ref, wu_ref, wd_ref, dwg_ref, dwu_ref, dwd_ref, dh_ref,
             ag_ref, au_ref, ad_ref):
        j, i = pl.program_id(0), pl.program_id(1)
        dfv, hv = df_ref[...], h_ref[...]
        gv, uv = g_ref[...].astype(F32), u_ref[...].astype(F32)
        da = _mm_nt(dfv, wd_ref[...])
        sg = jax.nn.sigmoid(gv)
        silu = gv * sg
        du = da * silu
        dg = da * uv * (sg * (1.0 + gv * (1.0 - sg)))
        p_d = _mm_tn(silu * uv, dfv)
        p_g = _mm_tn(dg, hv)
        p_u = _mm_tn(du, hv)

        @pl.when(i == 0)
        def _():
            ad_ref[...] = p_d
            ag_ref[...] = p_g
            au_ref[...] = p_u

        @pl.when(i > 0)
        def _():
            ad_ref[...] += p_d
            ag_ref[...] += p_g
            au_ref[...] += p_u

        @pl.when(i == ni - 1)
        def _():
            dwd_ref[...] = ad_ref[...].astype(BF16)
            dwg_ref[...] = ag_ref[...].astype(BF16)
            dwu_ref[...] = au_ref[...].astype(BF16)

        part = _mm(dg, wg_ref[...]) + _mm(du, wu_ref[...])
        rows = pl.ds(pl.multiple_of(i * TMB, TMB), TMB)

        @pl.when(j == 0)
        def _():
            dh_ref[rows, :] = part

        @pl.when(j > 0)
        def _():
            dh_ref[rows, :] += part

    gu = pl.BlockSpec((TMB, HB), lambda j, i: (i, j))
    rowt = pl.BlockSpec((TMB, D), lambda j, i: (i, 0))
    return _call(
        body, name=name, grid=(NHB, ni),
        in_specs=[rowt, rowt, gu, gu, _wrows(1), _wrows(1), _wrows(1)],
        out_specs=(_wrows(1), _wrows(1), _wrows(1), _resident((S, D))),
        out_shape=(_sds((NDEV * FBP, D), BF16), _sds((NDEV * FBP, D), BF16), _sds((NDEV * FBP, D), BF16), _sds((S, D), F32)),
        scratch=[pltpu.VMEM((HB, D), F32), pltpu.VMEM((HB, D), F32), pltpu.VMEM((HB, D), F32)],
        cp=_cp(60, ("arbitrary", "arbitrary")), args=(df, h, g, u, wgt, wut, wd))[0]


def win_fwd(h, win, name, ride=None):
    def body(h_ref, w_ref, z_ref):
        hv = h_ref[...]
        for j in range(NDEV):
            z_ref[:, 256 * j:256 * (j + 1)] = _mm(hv, w_ref[j])

    return _call(
        body, name=name, grid=(S // TMM,),
        in_specs=[_row_spec(D, TMM), _full_spec((NDEV, D, 256))],
        out_specs=[_row_spec(D_IN, TMM)], out_shape=[_sds((S, D_IN), F32)],
        cp=_cp(40, ("arbitrary",)), args=(h, win), ride=ride)


def win_bwd(dparts, h, win, name):
    ni = S // TMM

    def body(dq_ref, dk_ref, dv_ref, dus_ref, dup_ref, h_ref, w_ref, dh_ref, dw_ref, acc_ref):
        i = pl.program_id(0)
        hv = h_ref[...]
        cols = [dq_ref[:, 0:256], dq_ref[:, 256:512], dk_ref[:, 0:256], dk_ref[:, 256:512],
                dv_ref[:, 0:256], dv_ref[:, 256:512], dus_ref[...], dup_ref[...]]
        dh = jnp.zeros((TMM, D), F32)
        for j in range(NDEV):
            dz = cols[j].astype(MXU)
            dh = dh + _mm_nt(dz, w_ref[j])
            p = _mm_tn(hv, dz)

            @pl.when(i == 0)
            def _():
                acc_ref[j] = p

            @pl.when(i > 0)
            def _():
                acc_ref[j] += p

        dh_ref[...] = dh

        @pl.when(i == ni - 1)
        def _():
            dw_ref[...] = acc_ref[...].astype(BF16)

    return _call(
        body, name=name, grid=(ni,),
        in_specs=[_row_spec(512, TMM), _row_spec(512, TMM), _row_spec(512, TMM), _row_spec(256, TMM), _row_spec(256, TMM),
                  _row_spec(D, TMM), _full_spec((NDEV, D, 256))],
        out_specs=(_row_spec(D, TMM), _full_spec((NDEV, D, 256))),
        out_shape=(_sds((S, D), F32), _sds((NDEV, D, 256), BF16)),
        scratch=[pltpu.VMEM((NDEV, D, 256), F32)],
        cp=_cp(48, ("arbitrary",)), args=(*dparts, h, win))[0]


def wout_fwd(ya, ys, yp, wout, name, ride=None):
    def body(ya_ref, ys_ref, yp_ref, w_ref, o_ref):
        w = w_ref[...].reshape(D, D)
        o_ref[...] = _mm(ya_ref[...], w[0:512]) + _mm(ys_ref[...], w[512:768]) + _mm(yp_ref[...], w[768:1024])

    return _call(
        body, name=name, grid=(S // TMM,),
        in_specs=[_row_spec(512, TMM), _row_spec(256, TMM), _row_spec(256, TMM), _full_spec((NDEV, 128, D))],
        out_specs=[_row_spec(D, TMM)], out_shape=[_sds((S, D), F32)],
        cp=_cp(40, ("arbitrary",)), args=(ya, ys, yp, wout), ride=ride)


def wout_bwd(do, ya, ys, yp, wout, name):
    ni = S // TMM

    def body(do_ref, ya_ref, ys_ref, yp_ref, w_ref, dya_ref, dys_ref, dyp_ref, dw_ref, acc_ref):
        i = pl.program_id(0)
        w = w_ref[...].reshape(D, D)
        dov = do_ref[...]
        dya_ref[...] = _mm_nt(dov, w[0:512])
        dys_ref[...] = _mm_nt(dov, w[512:768])
        dyp_ref[...] = _mm_nt(dov, w[768:1024])
        parts = [(0, 512, _mm_tn(ya_ref[...], dov)), (512, 768, _mm_tn(ys_ref[...], dov)),
                 (768, 1024, _mm_tn(yp_ref[...], dov))]
        for lo, hi, p in parts:
            @pl.when(i == 0)
            def _():
                acc_ref[lo:hi, :] = p

            @pl.when(i > 0)
            def _():
                acc_ref[lo:hi, :] += p

        @pl.when(i == ni - 1)
        def _():
            dw_ref[...] = acc_ref[...].astype(BF16).reshape(NDEV, 128, D)

    return _call(
        body, name=name, grid=(ni,),
        in_specs=[_row_spec(D, TMM), _row_spec(512, TMM), _row_spec(256, TMM), _row_spec(256, TMM),
                  _full_spec((NDEV, 128, D))],
        out_specs=(_row_spec(512, TMM), _row_spec(256, TMM), _row_spec(256, TMM), _full_spec((NDEV, 128, D))),
        out_shape=(_sds((S, 512), F32), _sds((S, 256), F32), _sds((S, 256), F32), _sds((NDEV, 128, D), BF16)),
        scratch=[pltpu.VMEM((D, D), F32)],
        cp=_cp(40, ("arbitrary",)), args=(do, ya, ys, yp, wout))[0]


def _t5_bucket(dist):
    max_exact = N_BUCKETS // 2
    d = np.maximum(dist, 1).astype(np.float32)
    large = max_exact + (np.log(d / max_exact) / math.log(MAX_DISTANCE / max_exact)
                         * (N_BUCKETS - max_exact)).astype(np.int32)
    large = np.minimum(large, N_BUCKETS - 1)
    return np.where(dist < max_exact, dist, large).astype(np.int32)


def _att_static():
    i = np.arange(QB)[:, None]
    j = np.arange(2 * QB)[None, :]
    r = i + QB - j
    buckets, bands = [], []
    for window, dil in PATTERNS:
        bands.append((r >= 0) & (r <= window // dil))
        buckets.append(_t5_bucket(np.clip(r, 0, None) * dil))
    return np.stack(buckets), np.stack(bands), np.broadcast_to(j >= QB, (QB, 2 * QB))


def att_bias(rel_bias):
    m = np.arange(2 * QB)
    rows = []
    for window, dil in PATTERNS:
        r = QB - m
        ok = (r >= 0) & (r <= window // dil)
        b = rel_bias[_t5_bucket(np.clip(r, 0, None) * dil)]
        rows.append(jnp.where(ok[:, None], b, NEG).T)
    return jnp.broadcast_to(jnp.stack(rows)[:, :, None, :], (3, N_HEADS, 8, 2 * QB))


def _bias_tiles(t_ref, tiles):
    col = lax.broadcasted_iota(jnp.int32, (QB, 2 * QB), 1)
    for p in range(3):
        for hh in range(2):
            t = pltpu.roll(jnp.broadcast_to(t_ref[p, hh, 0:1, :], (QB, 2 * QB)), 0, 1, stride=1, stride_axis=0)
            tiles[p, hh, 0] = t
            tiles[p, hh, 1] = jnp.where(col >= QB, t, NEG)


def _permute_in(dst_ref, src_ref, d, scale=None, pad=QB):
    L = S // d
    for r in range(d):
        v = src_ref[pl.ds(r, L, stride=d), :] if d > 1 else src_ref[...]
        if scale is not None:
            v = v * scale
        dst_ref[pad + r * L:pad + (r + 1) * L, :] = v.astype(dst_ref.dtype)


def att_fwd(z, bias, name, ride=None):
    def body(q_ref, k_ref, v_ref, t_ref, y_ref, l_ref, qs, ks, vs, o_perm, l_perm, o_nat, l_nat, b_ref):
        _bias_tiles(t_ref, b_ref)
        zero_pad = jnp.zeros((QB, 128), MXU)
        ks[0:QB, :] = zero_pad
        vs[0:QB, :] = zero_pad
        lane = lax.broadcasted_iota(jnp.int32, (QB, 128), 1)
        for p, (_, d) in enumerate(PATTERNS):
            L = S // d
            nb = L // QB
            _permute_in(qs, q_ref, d, scale=0.125, pad=0)
            _permute_in(ks, k_ref, d)
            _permute_in(vs, v_ref, d)

            def blk(b, carry):
                r0 = pl.multiple_of(b * QB, QB)
                q = qs[pl.ds(r0, QB), :]
                kb = ks[pl.ds(r0, 2 * QB), :]
                vb = vs[pl.ds(r0, 2 * QB), :]
                first = ((b % nb) == 0).astype(jnp.int32)
                res = []
                for hh in range(2):
                    sel = (lane < 64) if hh == 0 else (lane >= 64)
                    qm = jnp.where(sel, q, jnp.zeros_like(q))
                    s = _mm_nt(qm, kb) + b_ref[p, hh, first]
                    m = jnp.max(s, axis=1, keepdims=True)
                    pe = jnp.exp(s - m)
                    den = jnp.sum(pe, axis=1, keepdims=True)
                    res.append((_mm(pe, vb) / den, m + jnp.log(den)))
                o_perm[pl.ds(r0, QB), :] = jnp.where(lane < 64, res[0][0], res[1][0])
                l_perm[pl.ds(r0, QB), :] = jnp.where(lane < 64, res[0][1], res[1][1])
                return carry

            lax.fori_loop(0, S // QB, blk, 0, unroll=8)
            for r in range(d):
                if d > 1:
                    o_nat[p, pl.ds(r, L, stride=d), :] = o_perm[r * L:(r + 1) * L, :]
                    l_nat[p, pl.ds(r, L, stride=d), :] = l_perm[r * L:(r + 1) * L, :]
                else:
                    o_nat[p] = o_perm[...]
                    l_nat[p] = l_perm[...]
        l0, l1, l2 = l_nat[0], l_nat[1], l_nat[2]
        m = jnp.maximum(jnp.maximum(l0, l1), l2)
        e0, e1, e2 = jnp.exp(l0 - m), jnp.exp(l1 - m), jnp.exp(l2 - m)
        den = e0 + e1 + e2
        y_ref[...] = (e0 * o_nat[0] + e1 * o_nat[1] + e2 * o_nat[2]) / den
        l_ref[...] = m + jnp.log(den)

    col = lambda c0: pl.BlockSpec((S, 128), lambda hp: (0, c0 + hp))
    return _call(
        body, name=name, grid=(N_HEADS // 2,),
        in_specs=[col(0), col(4), col(8), pl.BlockSpec((3, 2, 8, 2 * QB), lambda hp: (0, hp, 0, 0))],
        out_specs=(col(0), col(0)),
        out_shape=(_sds((S, D_ATT), F32), _sds((S, D_ATT), F32)),
        scratch=[pltpu.VMEM((S, 128), MXU), pltpu.VMEM((S + QB, 128), MXU), pltpu.VMEM((S + QB, 128), MXU),
                 pltpu.VMEM((S, 128), F32), pltpu.VMEM((S, 128), F32),
                 pltpu.VMEM((3, S, 128), F32), pltpu.VMEM((3, S, 128), F32),
                 pltpu.VMEM((3, 2, 2, QB, 2 * QB), F32)],
        cp=_cp(40, ("arbitrary",)), args=(z, z, z, bias), ride=ride)


def att_bwd(z, bias, y, lse, dy, name):
    def body(q_ref, k_ref, v_ref, t_ref, y_ref, l_ref, dy_ref, dq_ref, dk_ref, dv_ref, db_ref,
             qs, ks, vs, dys, ls, dds, dn_nat, dq_perm, dk_perm, dv_perm, b_ref):
        _bias_tiles(t_ref, b_ref)
        zero_pad = jnp.zeros((QB, 128), MXU)
        ks[0:QB, :] = zero_pad
        vs[0:QB, :] = zero_pad
        lane = lax.broadcasted_iota(jnp.int32, (QB, 128), 1)
        lane_s = lax.broadcasted_iota(jnp.int32, (S, 128), 1)
        t = dy_ref[...] * y_ref[...]
        sa = jnp.sum(jnp.where(lane_s < 64, t, 0.0), axis=1, keepdims=True)
        sb = jnp.sum(jnp.where(lane_s >= 64, t, 0.0), axis=1, keepdims=True)
        dn_nat[...] = jnp.where(lane_s < 64, sa, sb)
        dq_ref[...] = jnp.zeros((S, 128), F32)
        dk_ref[...] = jnp.zeros((S, 128), F32)
        dv_ref[...] = jnp.zeros((S, 128), F32)
        db_ref[...] = jnp.zeros((3, 2, QB, 2 * QB), F32)
        for p, (_, d) in enumerate(PATTERNS):
            L = S // d
            nb = L // QB
            _permute_in(qs, q_ref, d, scale=0.125, pad=0)
            _permute_in(ks, k_ref, d)
            _permute_in(vs, v_ref, d)
            _permute_in(dys, dy_ref, d, pad=0)
            _permute_in(ls, l_ref, d, pad=0)
            _permute_in(dds, dn_nat, d, pad=0)
            dk_perm[...] = jnp.zeros((S + QB, 128), F32)
            dv_perm[...] = jnp.zeros((S + QB, 128), F32)

            def blk(b, carry):
                r0 = pl.multiple_of(b * QB, QB)
                q = qs[pl.ds(r0, QB), :]
                kb = ks[pl.ds(r0, 2 * QB), :]
                vb = vs[pl.ds(r0, 2 * QB), :]
                dyb = dys[pl.ds(r0, QB), :]
                lb = ls[pl.ds(r0, QB), :]
                db = dds[pl.ds(r0, QB), :]
                first = ((b % nb) == 0).astype(jnp.int32)
                lane2 = jnp.concatenate([lane, lane], axis=0)
                own = (lane2 >> 6) == (lax.broadcasted_iota(jnp.int32, (2 * QB, 128), 0) >> 7)
                qm = jnp.where(own, jnp.concatenate([q, q], axis=0), jnp.zeros((2 * QB, 128), q.dtype))
                dym = jnp.where(own, jnp.concatenate([dyb, dyb], axis=0), jnp.zeros((2 * QB, 128), dyb.dtype))
                wide = lambda t: jnp.concatenate([jnp.broadcast_to(t[:, 0:1], (QB, 2 * QB)), jnp.broadcast_to(t[:, 64:65], (QB, 2 * QB))], axis=0)
                lse2, dd2 = wide(lb), wide(db)
                bias2 = jnp.concatenate([b_ref[p, 0, first], b_ref[p, 1, first]], axis=0)
                pr = jnp.exp(_mm_nt(qm, kb) + bias2 - lse2)
                ds = pr * (_mm_nt(dym, vb) - dd2)
                db_ref[p, 0] += ds[0:QB]
                db_ref[p, 1] += ds[QB:2 * QB]
                dq2 = _mm(ds, kb)
                dqs = [dq2[0:QB], dq2[QB:2 * QB]]
                dkb = _mm_tn(ds, qm)
                dvb = _mm_tn(pr, dym)
                dq_perm[pl.ds(r0, QB), :] = jnp.where(lane < 64, dqs[0], dqs[1])
                dk_perm[pl.ds(r0, 2 * QB), :] += dkb
                dv_perm[pl.ds(r0, 2 * QB), :] += dvb
                return carry

            lax.fori_loop(0, S // QB, blk, 0, unroll=4)
            for r in range(d):
                idx = pl.ds(r, L, stride=d) if d > 1 else pl.ds(0, S)
                dq_ref[idx, :] += dq_perm[r * L:(r + 1) * L, :] * 0.125
                dk_ref[idx, :] += dk_perm[QB + r * L:QB + (r + 1) * L, :]
                dv_ref[idx, :] += dv_perm[QB + r * L:QB + (r + 1) * L, :]

    col = lambda c0: pl.BlockSpec((S, 128), lambda hp: (0, c0 + hp))
    bspec = pl.BlockSpec((3, 2, 8, 2 * QB), lambda hp: (0, hp, 0, 0))
    return _call(
        body, name=name, grid=(N_HEADS // 2,),
        in_specs=[col(0), col(4), col(8), bspec, col(0), col(0), col(0)],
        out_specs=(col(0), col(0), col(0), pl.BlockSpec((3, 2, QB, 2 * QB), lambda hp: (0, hp, 0, 0))),
        out_shape=(_sds((S, D_ATT), F32), _sds((S, D_ATT), F32), _sds((S, D_ATT), F32),
                   _sds((3, N_HEADS, QB, 2 * QB), F32)),
        scratch=[pltpu.VMEM((S, 128), MXU), pltpu.VMEM((S + QB, 128), MXU), pltpu.VMEM((S + QB, 128), MXU),
                 pltpu.VMEM((S, 128), MXU), pltpu.VMEM((S, 128), F32), pltpu.VMEM((S, 128), F32),
                 pltpu.VMEM((S, 128), F32), pltpu.VMEM((S, 128), F32),
                 pltpu.VMEM((S + QB, 128), F32), pltpu.VMEM((S + QB, 128), F32),
                 pltpu.VMEM((3, 2, 2, QB, 2 * QB), F32)],
        cp=_cp(48, ("arbitrary",)), args=(z, z, z, bias, y, lse, dy))[0]


def relbias_grad(dbiases):
    bucket, band, _ = _att_static()
    onehot = (bucket[:, None] == np.arange(N_BUCKETS)[None, :, None, None]) & band[:, None]
    onehot = jnp.asarray(onehot.reshape(3, N_BUCKETS, QB * 2 * QB), BF16)

    def body(db0_ref, db1_ref, oh_ref, o_ref):
        acc = jnp.zeros((N_HEADS, N_BUCKETS), F32)
        for p in range(3):
            acc = acc + lax.dot_general(db0_ref[p] + db1_ref[p], oh_ref[p].astype(F32), (((1,), (1,)), ((), ())),
                                        preferred_element_type=F32, precision=lax.Precision.HIGHEST)
        o_ref[...] = acc

    vm = pl.BlockSpec(memory_space=pltpu.VMEM)
    out = _pallas_call(body, name="relbias_grad", in_specs=[vm, vm, vm], out_specs=vm,
                         out_shape=_sds((N_HEADS, N_BUCKETS), F32), compiler_params=_cp(40))(
        *[d.reshape(3, N_HEADS, QB * 2 * QB) for d in dbiases], onehot)
    return out.T


def _panel(t_ref, ri, j):
    return t_ref[ri, pl.ds(j, S, stride=8), :]


def _gelu(x):
    c = math.sqrt(2.0 / math.pi)
    th = jnp.tanh(c * (x + 0.044715 * x * x * x))
    return 0.5 * x * (1.0 + th), th


def ssm_fwd(z, a, bre, bim, cre, cim, dsk, gluw, glub, name, ride=None):
    def body(u_ref, a_ref, bre_ref, bim_ref, cre_ref, cim_ref, d_ref, gw_ref, gb_ref, y_ref, yp_ref, st_hbm, st_ref):
        u = u_ref[...]
        for j in range(8):
            st_ref[0, pl.ds(j, S, stride=8), :] = _mm(u, bre_ref[:, 128 * j:128 * (j + 1)])
            st_ref[1, pl.ds(j, S, stride=8), :] = _mm(u, bim_ref[:, 128 * j:128 * (j + 1)])
        ar, ai = a_ref[0], a_ref[1]

        def step(t, c):
            re, im = c
            i = pl.multiple_of(t * 8, 8)
            nre = ar * re - ai * im + st_ref[0, pl.ds(i, 8), :]
            nim = ar * im + ai * re + st_ref[1, pl.ds(i, 8), :]
            st_ref[0, pl.ds(i, 8), :] = nre
            st_ref[1, pl.ds(i, 8), :] = nim
            return nre, nim

        zero = jnp.zeros((8, 128), F32)
        lax.fori_loop(0, S, step, (zero, zero), unroll=8)
        y = d_ref[...] * u
        for j in range(8):
            y = y + _mm(_panel(st_ref, 0, j), cre_ref[128 * j:128 * (j + 1), :])
            y = y - _mm(_panel(st_ref, 1, j), cim_ref[128 * j:128 * (j + 1), :])
        pltpu.sync_copy(st_ref, st_hbm)
        yp_ref[...] = y
        gl, _ = _gelu(y)
        tt = _mm(gl, gw_ref[...].reshape(D_SSM, D_SSM)) + gb_ref[...]
        y_ref[...] = y * jax.nn.sigmoid(tt)

    vm = lambda shape: pl.BlockSpec(shape, lambda i: (0,) * len(shape))
    return _call(
        body, name=name, grid=(1,),
        in_specs=[pl.BlockSpec((S, 256), lambda i: (0, 6)), vm((2, 8, 128)), vm((256, 1024)), vm((256, 1024)),
                  vm((1024, 256)), vm((1024, 256)), vm((1, 256)),
                  vm((NDEV, 32, 256)), vm((1, 256))],
        out_specs=(vm((S, 256)), vm((S, 256)), pl.BlockSpec(memory_space=pl.ANY)),
        out_shape=(_sds((S, 256), F32), _sds((S, 256), F32), _sds((2, S * 8, 128), F32)),
        scratch=[pltpu.VMEM((2, S * 8, 128), F32)],
        cp=_cp(40, ("arbitrary",)), args=(z, a, bre, bim, cre, cim, dsk, gluw, glub), ride=ride)


def ssm_bwd(dy, z, ypre, st, a, bre, bim, cre, cim, dsk, gluw, glub, name):
    def body(dy_ref, u_ref, yp_ref, st_hbm, a_ref, bre_ref, bim_ref, cre_ref, cim_ref, d_ref, gw_ref, gb_ref,
             du_ref, dbre_ref, dbim_ref, dcre_ref, dcim_ref, da_ref, dd_ref, dgw_ref, dgb_ref, g_ref, st_ref):
        pltpu.sync_copy(st_hbm, st_ref)
        u = u_ref[...]
        y = yp_ref[...]
        dout = dy_ref[...]
        gw = gw_ref[...].reshape(D_SSM, D_SSM)
        gl, th = _gelu(y)
        sig = jax.nn.sigmoid(_mm(gl, gw) + gb_ref[...])
        dt = dout * y * sig * (1.0 - sig)
        dgw_ref[...] = _mm_tn(gl, dt)
        dgb_ref[...] = jnp.sum(dt, axis=0, keepdims=True)
        c = math.sqrt(2.0 / math.pi)
        dgelu = 0.5 * (1.0 + th) + 0.5 * y * (1.0 - th * th) * c * (1.0 + 3.0 * 0.044715 * y * y)
        dyv = dout * sig + _mm_nt(dt, gw) * dgelu
        dd_ref[...] = jnp.sum(dyv * u, axis=0, keepdims=True)
        for j in range(8):
            rows = slice(128 * j, 128 * (j + 1))
            g_ref[0, pl.ds(j, S, stride=8), :] = _mm_nt(dyv, cre_ref[rows, :])
            g_ref[1, pl.ds(j, S, stride=8), :] = -_mm_nt(dyv, cim_ref[rows, :])
            dcre_ref[rows, :] = _mm_tn(_panel(st_ref, 0, j), dyv)
            dcim_ref[rows, :] = -_mm_tn(_panel(st_ref, 1, j), dyv)
        ar, ai = a_ref[0], a_ref[1]

        def step(k, c4):
            gre, gim, dar, dai = c4
            i = pl.multiple_of((S - 1 - k) * 8, 8)
            nre = g_ref[0, pl.ds(i, 8), :] + ar * gre + ai * gim
            nim = g_ref[1, pl.ds(i, 8), :] + ar * gim - ai * gre
            g_ref[0, pl.ds(i, 8), :] = nre
            g_ref[1, pl.ds(i, 8), :] = nim
            sre = st_ref[0, pl.ds(i - 8, 8), :]
            sim = st_ref[1, pl.ds(i - 8, 8), :]
            return nre, nim, dar + nre * sre + nim * sim, dai + nim * sre - nre * sim

        zero = jnp.zeros((8, 128), F32)
        gre, gim, dar, dai = lax.fori_loop(0, S - 1, step, (zero, zero, zero, zero), unroll=8)
        g_ref[0, 0:8, :] = g_ref[0, 0:8, :] + ar * gre + ai * gim
        g_ref[1, 0:8, :] = g_ref[1, 0:8, :] + ar * gim - ai * gre
        da_ref[0] = dar
        da_ref[1] = dai
        du = dyv * d_ref[...]
        for j in range(8):
            cols = slice(128 * j, 128 * (j + 1))
            gr, gi = _panel(g_ref, 0, j), _panel(g_ref, 1, j)
            dbre_ref[:, cols] = _mm_tn(u, gr)
            dbim_ref[:, cols] = _mm_tn(u, gi)
            du = du + _mm_nt(gr, bre_ref[:, cols]) + _mm_nt(gi, bim_ref[:, cols])
        du_ref[...] = du

    vm = lambda shape: pl.BlockSpec(shape, lambda i: (0,) * len(shape))
    return _call(
        body, name=name, grid=(1,),
        in_specs=[vm((S, 256)), pl.BlockSpec((S, 256), lambda i: (0, 6)), vm((S, 256)), pl.BlockSpec(memory_space=pl.ANY),
                  vm((2, 8, 128)), vm((256, 1024)), vm((256, 1024)), vm((1024, 256)), vm((1024, 256)), vm((1, 256)),
                  vm((NDEV, 32, 256)), vm((1, 256))],
        out_specs=(vm((S, 256)), vm((256, 1024)), vm((256, 1024)), vm((1024, 256)), vm((1024, 256)),
                   vm((2, 8, 128)), vm((1, 256)), vm((256, 256)), vm((1, 256))),
        out_shape=(_sds((S, 256), F32), _sds((256, 1024), F32), _sds((256, 1024), F32), _sds((1024, 256), F32),
                   _sds((1024, 256), F32), _sds((2, 8, 128), F32), _sds((1, 256), F32), _sds((256, 256), F32),
                   _sds((1, 256), F32)),
        scratch=[pltpu.VMEM((2, S * 8, 128), F32), pltpu.VMEM((2, S * 8, 128), F32)],
        cp=_cp(56, ("arbitrary",)), args=(dy, z, ypre, st, a, bre, bim, cre, cim, dsk, gluw, glub))[0]


def _ssm_discretise(a_re, a_im, log_dt, b_re, b_im):
    dt = jnp.exp(log_dt)[:, None]
    er = jnp.exp(a_re * dt)
    abr, abi = er * jnp.cos(a_im * dt), er * jnp.sin(a_im * dt)
    den = a_re * a_re + a_im * a_im
    fr = ((abr - 1.0) * a_re + abi * a_im) / den
    fi = (abi * a_re - (abr - 1.0) * a_im) / den
    bbr = fr[:, :, None] * b_re - fi[:, :, None] * b_im
    bbi = fr[:, :, None] * b_im + fi[:, :, None] * b_re
    return abr, abi, bbr, bbi


def _blockdiag(t):
    g, r, c = t.shape
    eye = jnp.eye(g, dtype=t.dtype)
    return (t[:, :, None, :] * eye[:, None, :, None]).reshape(g * r, g * c)


def _blockdiag_take(m, r, c):
    g = m.shape[0] // r
    idx = jnp.arange(g)
    return m.reshape(g, r, g, c)[idx, :, idx, :]


PAD = 16


def _pool_lane_select(vals):
    lane = lax.broadcasted_iota(jnp.int32, vals[0].shape, 1)
    out = vals[3]
    for g in (2, 1, 0):
        out = jnp.where(lane < 64 * (g + 1), vals[g], out)
    return out


def _pool_counts():
    row = lax.broadcasted_iota(jnp.int32, (S, D_POOL), 0).astype(F32) + 1.0
    return _pool_lane_select([jnp.minimum(row, float(w)) for w in POOL_WINDOWS])


def _pooled(u, sa, sb):
    sums = []
    cur = u
    bufs = (sa, sb)
    for k, sh in enumerate((1, 2, 4, 8)):
        buf = bufs[k % 2]
        buf[PAD:PAD + S, :] = cur
        cur = cur + buf[PAD - sh:PAD - sh + S, :]
        sums.append(cur)
    return _pool_lane_select(sums) / _pool_counts() - u


def pool_fwd(z, pw, psc, name):
    def body(u_ref, w_ref, s_ref, y_ref, sa, sb):
        for buf in (sa, sb):
            buf[0:PAD, :] = jnp.zeros((PAD, D_POOL), F32)
        pooled = _pooled(u_ref[...], sa, sb)
        y_ref[...] = _mm(pooled, w_ref[...]) * s_ref[...]

    vm = lambda shape: pl.BlockSpec(shape, lambda i: (0,) * len(shape))
    return _pallas_call(
        body, name=name, grid=(1,),
        in_specs=[pl.BlockSpec((S, 256), lambda i: (0, 7)), vm((256, 256)), vm((1, 256))],
        out_specs=vm((S, 256)), out_shape=_sds((S, 256), F32),
        scratch_shapes=[pltpu.VMEM((S + 2 * PAD, D_POOL), F32)] * 2,
        compiler_params=_cp(40, ("arbitrary",)))(z, pw, psc)


def pool_bwd(dy, z, pw, psc, name):
    def body(dy_ref, u_ref, w_ref, s_ref, du_ref, dw_ref, ds_ref, sa, sb):
        for buf in (sa, sb):
            buf[0:PAD, :] = jnp.zeros((PAD, D_POOL), F32)
            buf[PAD + S:PAD + S + PAD, :] = jnp.zeros((PAD, D_POOL), F32)
        pooled = _pooled(u_ref[...], sa, sb)
        dyv = dy_ref[...]
        w = w_ref[...]
        ds_ref[...] = jnp.sum(dyv * _mm(pooled, w), axis=0, keepdims=True)
        dyl = dyv * s_ref[...]
        dw_ref[...] = _mm_tn(pooled, dyl)
        dpool = _mm_nt(dyl, w)
        cur = dpool / _pool_counts()
        sums = []
        bufs = (sa, sb)
        for k, sh in enumerate((1, 2, 4, 8)):
            buf = bufs[k % 2]
            buf[PAD:PAD + S, :] = cur
            cur = cur + buf[PAD + sh:PAD + sh + S, :]
            sums.append(cur)
        du_ref[...] = _pool_lane_select(sums) - dpool

    vm = lambda shape: pl.BlockSpec(shape, lambda i: (0,) * len(shape))
    return _pallas_call(
        body, name=name, grid=(1,),
        in_specs=[vm((S, 256)), pl.BlockSpec((S, 256), lambda i: (0, 7)), vm((256, 256)), vm((1, 256))],
        out_specs=(vm((S, 256)), vm((256, 256)), vm((1, 256))),
        out_shape=(_sds((S, 256), F32), _sds((256, 256), F32), _sds((1, 256), F32)),
        scratch_shapes=[pltpu.VMEM((S + 2 * PAD, D_POOL), F32)] * 2,
        compiler_params=_cp(40, ("arbitrary",)))(dy, z, pw, psc)


def ada_fwd(c_all, ada_w, ada_b_cols):
    def body(c_ref, w_ref, b_ref, o_ref):
        c = c_ref[...]
        cond = c * jax.nn.sigmoid(c)
        o_ref[...] = jnp.dot(cond, w_ref[...], preferred_element_type=F32, precision=lax.Precision.HIGHEST) + b_ref[...]

    return _pallas_call(
        body, name="ada_fwd", grid=(DEPTH,),
        in_specs=[pl.BlockSpec((NDEV, D), lambda l: (0, 0)), pl.BlockSpec((None, D, 1152), lambda l: (l, 0, 0)),
                  pl.BlockSpec((None, 1, 1152), lambda l: (l, 0, 0))],
        out_specs=pl.BlockSpec((None, NDEV, 1152), lambda l: (l, 0, 0)), out_shape=_sds((DEPTH, NDEV, 1152), F32),
        compiler_params=_cp(40, ("arbitrary",)))(c_all, ada_w, ada_b_cols)


def ada_bwd_adam(c_all, dmod_cols, w, m, v, tr):
    def body(c_ref, dm_ref, w_ref, m_ref, v_ref, g_ref, d_ref, m2_ref, v2_ref):
        c = c_ref[...]
        cond = c * jax.nn.sigmoid(c)
        g = lax.dot_general(cond, dm_ref[...], (((0,), (0,)), ((), ())), preferred_element_type=F32,
                            precision=lax.Precision.HIGHEST)
        g_ref[...] = g
        d_ref[...], m2_ref[...], v2_ref[...] = _adamw(w_ref[...], g, m_ref[...], v_ref[...])

    rs = pl.BlockSpec((None, tr, 1152), lambda l, i: (l, i, 0))
    return _pallas_call(
        body, name="ada_bwd_adam", grid=(DEPTH, D // tr),
        in_specs=[pl.BlockSpec((NDEV, tr), lambda l, i: (0, i)), pl.BlockSpec((None, NDEV, 1152), lambda l, i: (l, 0, 0)), rs, rs, rs],
        out_specs=(rs, rs, rs, rs), out_shape=tuple(_sds((DEPTH, D, 1152), F32) for _ in range(4)),
        compiler_params=_cp(48, ("arbitrary", "arbitrary")))(c_all, dmod_cols, w, m, v)


def _adamw(w, g, m, v):
    m2 = B1 * m + (1.0 - B1) * g
    v2 = B2 * v + (1.0 - B2) * (g * g)
    m_hat = m2 / (1.0 - B1 ** STEP)
    v_hat = v2 / (1.0 - B2 ** STEP)
    return -LR * (m_hat / (jnp.sqrt(v_hat) + EPS) + WD * w), m2, v2


def _sum8(ref):
    g = ref[0].astype(F32)
    for s in range(1, ref.shape[0]):
        g = g + ref[s].astype(F32)
    return g


def adam_rs(recv, w, m, v, tr, name):
    lead, (r, cdim) = w.shape[:-2], w.shape[-2:]
    cp = recv.shape[-1]
    nl = len(lead)

    def body(rc_ref, w_ref, m_ref, v_ref, g_ref, d_ref, m2_ref, v2_ref):
        g = _sum8(rc_ref)[:, :cdim]
        g_ref[...] = g
        d_ref[...], m2_ref[...], v2_ref[...] = _adamw(w_ref[...], g, m_ref[...], v_ref[...])

    rs = pl.BlockSpec((None,) * nl + (tr, cdim), lambda *i: (*i, 0))
    return _call(
        body, name=name, grid=lead + (r // tr,),
        in_specs=[pl.BlockSpec((NDEV,) + (None,) * nl + (tr, cp), lambda *i: (0, *i, 0)), rs, rs, rs],
        out_specs=(rs, rs, rs, rs), out_shape=tuple(_sds(w.shape, F32) for _ in range(4)),
        cp=_cp(48, ("arbitrary",) * (nl + 1)), args=(recv, w, m, v))[0]


def adam_block(recv, w, m, v, lf, prev, name):
    half = FB // 2

    def body(*refs):
        rc_ref, w_ref, m_ref, v_ref = refs[:4]
        g_ref, d_ref, m2_ref, v2_ref = refs[-4:]
        g = _sum8(rc_ref)
        g_ref[...] = g
        d_ref[...], m2_ref[...], v2_ref[...] = _adamw(w_ref[...], g, m_ref[...], v_ref[...])

    rs = pl.BlockSpec((None, None, half, D), lambda i: (lf // 2, lf % 2, i, 0))
    prev = list(prev) if prev is not None else []
    return list(_pallas_call(
        body, name=name, grid=(2,), in_specs=[pl.BlockSpec((recv.shape[0], half, D), lambda i: (0, i, 0)), rs, rs, rs] + [ANY] * len(prev),
        out_specs=(rs, rs, rs, rs), out_shape=tuple(_sds((DEPTH, 2, FB, D), F32) for _ in range(4)),
        input_output_aliases={4 + k: k for k in range(len(prev))},
        compiler_params=_cp(48, ("arbitrary",)))(recv, w, m, v, *prev))


def adam_native(gs, ws, ms, vs, name):
    n = len(ws)

    def body(*refs):
        g_refs, w_refs, m_refs, v_refs = (refs[k * n:(k + 1) * n] for k in range(4))
        d_refs, m2_refs, v2_refs = (refs[(4 + k) * n:(5 + k) * n] for k in range(3))
        for a in range(n):
            d_refs[a][...], m2_refs[a][...], v2_refs[a][...] = _adamw(w_refs[a][...], g_refs[a][...], m_refs[a][...], v_refs[a][...])

    whole = lambda a: pl.BlockSpec(a.shape, lambda i, nd=len(a.shape): (0,) * nd)
    outs = _pallas_call(body, name=name, grid=(1,), in_specs=[whole(a) for a in (*gs, *ws, *ms, *vs)],
                        out_specs=tuple(whole(w) for w in ws) * 3,
                        out_shape=tuple(_sds(w.shape, F32) for w in ws) * 3, compiler_params=_cp(40),
                        pin_all=True)(*gs, *ws, *ms, *vs)
    return outs[:n], outs[n:2 * n], outs[2 * n:]


def sum_sources(recv, name):
    r = recv.shape[1]

    def body(rc_ref, o_ref):
        o_ref[...] = _sum8(rc_ref)

    vm = pl.BlockSpec(memory_space=pltpu.VMEM)
    return _pallas_call(body, name=name, in_specs=[vm], out_specs=vm, out_shape=_sds((r, 128), F32),
                          compiler_params=_cp(40))(recv)


def _pack(arrs, dtype=F32):
    flat = jnp.concatenate([a.reshape(-1) for a in arrs]).astype(dtype)
    n = flat.shape[0]
    tile = 128 * (32 // jnp.dtype(dtype).itemsize)
    rows = -(-n // tile) * (tile // 128)
    return jnp.pad(flat, (0, rows * 128 - n)).reshape(rows, 128)


def _unpack(vec, shapes):
    flat = vec.reshape(-1)
    out, o = [], 0
    for sh in shapes:
        n = int(np.prod(sh))
        out.append(flat[o:o + n].reshape(sh))
        o += n
    return out


WEIGHTS = ['rel_bias', 'ada_w', 'ada_b', 'ln_g', 'ln_b', 'ffn_w_gate', 'ffn_w_up', 'ffn_w_down', 'w_in', 'w_out',
           'ssm_a_re', 'ssm_a_im', 'ssm_log_dt', 'ssm_b_re', 'ssm_b_im', 'ssm_c_re', 'ssm_c_im', 'ssm_d', 'glu_w',
           'glu_b', 'pool_w', 'pool_scale']
SMALL = ['rel_bias', 'ada_b', 'ln_g', 'ln_b', 'ssm_a_re', 'ssm_a_im', 'ssm_log_dt', 'ssm_b_re', 'ssm_b_im',
         'ssm_c_re', 'ssm_c_im', 'ssm_d', 'glu_b', 'pool_w', 'pool_scale']
SMALL_EXACT = ['rel_bias', 'ln_g', 'ln_b', 'ssm_a_re', 'ssm_a_im', 'ssm_log_dt']
SMALL_ROUNDED = ['ada_b', 'ssm_b_re', 'ssm_b_im', 'ssm_c_re', 'ssm_c_im', 'ssm_d', 'glu_b', 'pool_w', 'pool_scale']
SMALL_FULL_SHAPES = {'rel_bias': (32, 8), 'ada_b': (2, 9216), 'ln_g': (2, 3, 1024), 'ln_b': (2, 3, 1024),
                     'ssm_a_re': (2, 16, 64), 'ssm_a_im': (2, 16, 64), 'ssm_log_dt': (2, 16),
                     'ssm_b_re': (2, 16, 64, 16), 'ssm_b_im': (2, 16, 64, 16), 'ssm_c_re': (2, 16, 16, 64),
                     'ssm_c_im': (2, 16, 16, 64), 'ssm_d': (2, 256), 'glu_b': (2, 256), 'pool_w': (2, 4, 64, 64),
                     'pool_scale': (2, 256)}


def _step(P):
    me = _me()
    x0 = P['x'][0]
    target = P['loss_target'][0]

    bf = lambda a: a.astype(BF16)
    padr = lambda a: jnp.pad(bf(a), ((0, 0), (0, 0), (0, FBP - FB), (0, 0)))
    ffn_b = [padr(jnp.swapaxes(P['ffn_w_gate'], 2, 3)), padr(jnp.swapaxes(P['ffn_w_up'], 2, 3)), padr(P['ffn_w_down'])]
    mix_b = [bf(P['w_in']), bf(P['w_out']), bf(P['glu_w'])]

    def shards(l, sub):
        return [t[l] for t in mix_b] if sub == 1 else [t[l, sub // 2] for t in ffn_b]

    order = [(l, sub) for l in range(DEPTH) for sub in range(3)]
    nxt = dict(zip(order[:-1], order[1:]))
    W = {key: [None] * 3 for key in order}
    c_all, lng_all, lnb_all, *W[order[0]] = _exchange(Gather([P['c'], P['ln_g'], P['ln_b']] + shards(*order[0])), "gather_first")
    gather_queue = [(key, pos, a) for key in order[1:] for pos, a in enumerate(shards(*key))]

    def gather_ride(cap_us, must=None):
        units, used = [], 0.0
        while gather_queue:
            key, _, a = gather_queue[0]
            cost = a.size * a.dtype.itemsize * GATHER_US_PER_BYTE
            if key != must and used + cost / 2 > cap_us:
                break
            units.append(gather_queue.pop(0))
            used += cost
        return (Gather([a for _, _, a in units]) if units else None), units

    def gathered(units, outs):
        for (key, pos, _), o in zip(units, outs):
            W[key][pos] = o

    c_all = c_all.reshape(NDEV, D)
    ln_g = jnp.transpose(lng_all, (1, 2, 0, 3)).reshape(DEPTH, 3, D)
    ln_b = jnp.transpose(lnb_all, (1, 2, 0, 3)).reshape(DEPTH, 3, D)

    ada_b_cols = lax.dynamic_slice_in_dim(P['ada_b'], me * 1152, 1152, axis=1).reshape(DEPTH, 1, 1152)
    modc = ada_fwd(c_all, P['ada_w'], ada_b_cols)
    (mod_all,) = _exchange(Gather([modc]), "gather_mod")
    mod_me = lax.dynamic_index_in_dim(mod_all, me, axis=2, keepdims=False)
    mod = jnp.transpose(mod_me, (1, 0, 2)).reshape(DEPTH, 9, D)

    bias = att_bias(P['rel_bias'])
    ssm = []
    for l in range(DEPTH):
        prm = (P['ssm_a_re'][l], P['ssm_a_im'][l], P['ssm_log_dt'][l], P['ssm_b_re'][l], P['ssm_b_im'][l])
        (abr, abi, bbr, bbi), disc_vjp = jax.vjp(_ssm_discretise, *prm)
        ssm.append(dict(
            vjp=disc_vjp, a=jnp.stack([abr.reshape(8, 128), abi.reshape(8, 128)]),
            bre=_blockdiag(jnp.transpose(bbr, (0, 2, 1))).astype(MXU), bim=_blockdiag(jnp.transpose(bbi, (0, 2, 1))).astype(MXU),
            cre=_blockdiag(jnp.transpose(P['ssm_c_re'][l], (0, 2, 1))).astype(MXU),
            cim=_blockdiag(jnp.transpose(P['ssm_c_im'][l], (0, 2, 1))).astype(MXU),
            d=P['ssm_d'][l].reshape(1, 256), gb=P['glu_b'][l].reshape(1, 256),
            pw=_blockdiag(P['pool_w'][l]).astype(MXU), psc=P['pool_scale'][l].reshape(1, 256)))

    saved = []
    x = x0
    h = ln_mod_fwd(x, mod[0], 0, "ln_mod_fwd_l0s0")
    for l, sub in order:
        tag = f"l{l}s{sub}"
        after = (mod[nxt[(l, sub)][0]], nxt[(l, sub)][1]) if (l, sub) in nxt else None
        if sub != 1:
            wg, wu, wd = (t.reshape(NDEV * FBP, D) for t in W[(l, sub)])
            ride, units = gather_ride(60, nxt.get((l, sub)))
            (G, U, fo), got = ffn_fwd(h, wg, wu, wd, "ffn_fwd_" + tag, ride)
            gathered(units, got)
            saved.append(dict(x=x, h=h, G=G, U=U, f=fo))
            x, *hn = res_ln_fwd(x, fo, mod[l], sub, ln_g[l], ln_b[l], 0.5, "res_ln_fwd_" + tag, after)
        else:
            sp = ssm[l]
            win, wout, gluw = W[(l, sub)]
            ride, units = gather_ride(15)
            (z,), got = win_fwd(h, win, "win_fwd_" + tag, ride)
            gathered(units, got)
            ride, units = gather_ride(55)
            (ya, lse), got = att_fwd(z, bias, "att_fwd_" + tag, ride)
            gathered(units, got)
            ride, units = gather_ride(35)
            (ys, ypre, st), got = ssm_fwd(z, sp['a'], sp['bre'], sp['bim'], sp['cre'], sp['cim'], sp['d'], gluw, sp['gb'],
                                          "ssm_fwd_" + tag, ride)
            gathered(units, got)
            yp = pool_fwd(z, sp['pw'], sp['psc'], "pool_fwd_" + tag)
            ride, units = gather_ride(12, nxt.get((l, sub)))
            (o,), got = wout_fwd(ya, ys, yp, wout, "wout_fwd_" + tag, ride)
            gathered(units, got)
            saved.append(dict(x=x, h=h, z=z, ya=ya, lse=lse, ys=ys, ypre=ypre, st=st, yp=yp, f=o))
            x, *hn = res_ln_fwd(x, o, mod[l], sub, ln_g[l], ln_b[l], 1.0, "res_ln_fwd_" + tag, after)
        h = hn[0] if hn else None
    assert not gather_queue

    loss_tile, dx = loss_fwd_bwd(x, target, "loss")
    loss = lax.psum(loss_tile[0, 0], ("x", "y", "c"))

    flights = []

    dmod = [[None] * 9 for _ in range(DEPTH)]
    dlng = [[None] * 3 for _ in range(DEPTH)]
    dlnb = [[None] * 3 for _ in range(DEPTH)]
    dbiases = [None] * DEPTH
    small_l = [dict() for _ in range(DEPTH)]
    swaps = {'rel_bias': (0, 1), 'ln_g': (0, 1), 'ln_b': (0, 1), 'ssm_b_re': (2, 3), 'ssm_b_im': (2, 3)}
    view = lambda k, t: jnp.swapaxes(t, *swaps[k]) if k in swaps else t
    kept_shape = lambda k: np.swapaxes(np.empty(SMALL_FULL_SHAPES[k], np.bool_), *swaps.get(k, (0, 0))).shape

    def small_start():
        small = {k: jnp.stack([small_l[l][k] for l in range(DEPTH)]) for k in small_l[0]}
        small['rel_bias'] = relbias_grad(dbiases)
        small['ada_b'] = jnp.stack([jnp.stack(dmod[l]).reshape(9 * D) for l in range(DEPTH)])
        small['ln_g'] = jnp.stack([jnp.stack(dlng[l]) for l in range(DEPTH)])
        small['ln_b'] = jnp.stack([jnp.stack(dlnb[l]) for l in range(DEPTH)])
        return scatter_start([_pack([view(k, small[k]) for k in SMALL_EXACT]),
                              _pack([view(k, small[k]) for k in SMALL_ROUNDED], BF16)], "gather_small_start", whole=True)

    for l, sub in reversed(order):
        tag = f"l{l}s{sub}"
        sv = saved[3 * l + sub]
        if (l, sub) == order[-1]:
            dxa, df, sums = res_ln_bwd(sv['x'], sv['f'], mod[l], sub, ln_g[l], dx, 0.5, "res_ln_bwd_" + tag)
        dlng[l][sub], dlnb[l][sub], dmod[l][3 * sub + 2] = sums[0], sums[1], sums[2]
        if sub != 1:
            f = sub // 2
            wg, wu, wd = (t.reshape(NDEV * FBP, D) for t in W[(l, sub)])
            if (l, sub) == order[0]:
                dmod[l][0] = dmod[l][1] = jnp.zeros_like(dmod[l][2])
                small_flight, zero = small_start()
                df = df + zero.astype(df.dtype)
            dwg, dwu, dwd, dh = ffn_bwd(df, sv['h'], sv['G'], sv['U'], wg, wu, wd, "ffn_bwd_" + tag)
            if (l, sub) == order[0]:
                dx, sums2 = ln_mod_bwd(sv['x'], dh, mod[l], sub, dxa, "ln_mod_bwd_" + tag)
                late_flight, _ = scatter_start([_pack([sums2[0], sums2[1]])], "gather_late_start", whole=True)
                handle, _ = scatter_start([t.reshape(NDEV, FBP, D) for t in (dwg, dwu, dwd)], "scatter_start_" + tag, rows=FB,
                                          after=[late_flight[1][0]])
            else:
                handle, zero = scatter_start([t.reshape(NDEV, FBP, D) for t in (dwg, dwu, dwd)], "scatter_start_" + tag, rows=FB)
            flights.append(((l, sub), handle))
        else:
            sp = ssm[l]
            win, wout, gluw = W[(l, sub)]
            dya, dys, dyp, dwout = wout_bwd(df, sv['ya'], sv['ys'], sv['yp'], wout, "wout_bwd_" + tag)
            dq, dk, dv, dbiases[l] = att_bwd(sv['z'], bias, sv['ya'], sv['lse'], dya, "att_bwd_" + tag)
            dus, dbre, dbim, dcre, dcim, da, dd, dgw, dgb = ssm_bwd(
                dys, sv['z'], sv['ypre'], sv['st'], sp['a'], sp['bre'], sp['bim'], sp['cre'], sp['cim'], sp['d'],
                gluw, sp['gb'], "ssm_bwd_" + tag)
            dup, dpw, dpsc = pool_bwd(dyp, sv['z'], sp['pw'], sp['psc'], "pool_bwd_" + tag)
            dh, dwin = win_bwd((dq, dk, dv, dus, dup), sv['h'], win, "win_bwd_" + tag)
            handle, zero = scatter_start([dwin, dwout, dgw.astype(BF16).reshape(NDEV, 32, 256)], "scatter_start_" + tag)
            flights.append(((l, sub), handle))
            d_are, d_aim, d_ldt, d_bre, d_bim = sp['vjp']((
                da[0].reshape(16, 64), da[1].reshape(16, 64),
                jnp.transpose(_blockdiag_take(dbre, 16, 64), (0, 2, 1)), jnp.transpose(_blockdiag_take(dbim, 16, 64), (0, 2, 1))))
            small_l[l] = dict(
                ssm_a_re=d_are, ssm_a_im=d_aim, ssm_log_dt=d_ldt, ssm_b_re=d_bre, ssm_b_im=d_bim,
                ssm_c_re=jnp.transpose(_blockdiag_take(dcre, 64, 16), (0, 2, 1)),
                ssm_c_im=jnp.transpose(_blockdiag_take(dcim, 64, 16), (0, 2, 1)),
                ssm_d=dd.reshape(256), glu_b=dgb.reshape(256), pool_w=_blockdiag_take(dpw, 64, 64), pool_scale=dpsc.reshape(256))
        if (l, sub) == order[0]:
            break
        lp, sp_ = order[order.index((l, sub)) - 1]
        svp = saved[3 * lp + sp_]
        dxa, df, sums, sums2 = ln_join_bwd(svp['x'], svp['f'], mod[lp], sp_, ln_g[lp], ln_b[lp], 1.0 if sp_ == 1 else 0.5,
                                           dh, mod[l] + zero, sub, dxa, "ln_join_bwd_" + tag)
        dmod[l][3 * sub], dmod[l][3 * sub + 1] = sums2[0], sums2[1]
    grad_x = dx[None]

    out = {}

    def put(name, g, d, m2, v2, shape):
        out['grad_' + name], out['delta_' + name] = g.reshape(shape), d.reshape(shape)
        out['new_m_' + name], out['new_v_' + name] = m2.reshape(shape), v2.reshape(shape)

    def wmv(name):
        return [P[pre + name] for pre in ('', 'm_', 'v_')]

    recv = {}
    started_last = flights[-1][1][1][0]
    for key, handle in flights[:-1]:
        recv[key] = scatter_wait(handle, started_last, "scatter_wait_l%ds%d" % key)
    mixer_done = []
    for pos, (name, tr) in enumerate((('w_in', 512), ('w_out', 128), ('glu_w', 32))):
        both = jnp.stack([recv[(l, 1)][pos] for l in range(DEPTH)], axis=1)
        res = adam_rs(both, *wmv(name), tr, "adam_" + name)
        mixer_done.append(res[0])
        put(name, *res, P[name].shape)
    ffn = (('ffn_w_gate', [jnp.swapaxes(t, 2, 3) for t in wmv('ffn_w_gate')]),
           ('ffn_w_up', [jnp.swapaxes(t, 2, 3) for t in wmv('ffn_w_up')]), ('ffn_w_down', wmv('ffn_w_down')))
    part = [None] * 3
    for l, sub in [key for key, _ in flights[:-1] if key[1] != 1]:
        for pos, (name, ops) in enumerate(ffn):
            part[pos] = adam_block(recv[(l, sub)][pos], *ops, 2 * l + sub // 2, part[pos], f"adam_{name}_l{l}s{sub}")

    done = [p[0] for p in part] + mixer_done[:2]
    exact_all, rounded_all = scatter_wait(small_flight, done, "gather_small_wait")
    late_all, = scatter_wait(late_flight, done, "gather_late_wait")
    rounded_all = rounded_all.at[:, :late_all.shape[1]].set(late_all.astype(BF16))
    gsum = dict(zip(SMALL_EXACT, _unpack(sum_sources(exact_all, "sum_small_exact"), [kept_shape(k) for k in SMALL_EXACT])))
    gsum.update(zip(SMALL_ROUNDED, _unpack(sum_sources(rounded_all, "sum_small_rounded"), [kept_shape(k) for k in SMALL_ROUNDED])))
    dmod_cols = jnp.stack([lax.dynamic_slice_in_dim(rounded_all, 72 * l + 9 * me, 9, axis=1).astype(F32).reshape(NDEV, 1152)
                           for l in range(DEPTH)])
    ada_res = ada_bwd_adam(c_all, dmod_cols, *wmv('ada_w'), 256)
    put('ada_w', *ada_res, P['ada_w'].shape)

    for k in ('ln_g', 'ln_b'):
        gsum[k] = lax.dynamic_slice_in_dim(gsum[k], me * 128, 128, axis=2)
    ds_, m2s, v2s = adam_native([gsum[k] for k in SMALL], *[[view(k, P[pre + k]) for k in SMALL] for pre in ('', 'm_', 'v_')],
                                "adam_small")
    for k, d, m2, v2 in zip(SMALL, ds_, m2s, v2s):
        put(k, view(k, gsum[k]), view(k, d), view(k, m2), view(k, v2), P[k].shape)

    (l, sub), handle = flights[-1]
    last = scatter_wait(handle, [ada_res[1], ds_[SMALL.index('ada_b')]], "scatter_wait_l%ds%d" % (l, sub))
    for pos, (name, ops) in enumerate(ffn):
        res = adam_block(last[pos], *ops, 2 * l + sub // 2, part[pos], f"adam_{name}_l{l}s{sub}")
        put(name, *([jnp.swapaxes(t, 2, 3) for t in res] if pos < 2 else res), P[name].shape)

    res = [loss, grad_x]
    for pre in ('grad_', 'delta_', 'new_m_', 'new_v_'):
        res += [out[pre + k] for k in WEIGHTS]
    return tuple(res)


def kernel(x, c, rel_bias, ada_w, ada_b, ln_g, ln_b, ffn_w_gate, ffn_w_up, ffn_w_down, w_in, w_out, ssm_a_re, ssm_a_im, ssm_log_dt, ssm_b_re, ssm_b_im, ssm_c_re, ssm_c_im, ssm_d, glu_w, glu_b, pool_w, pool_scale, loss_target, m_rel_bias, m_ada_w, m_ada_b, m_ln_g, m_ln_b, m_ffn_w_gate, m_ffn_w_up, m_ffn_w_down, m_w_in, m_w_out, m_ssm_a_re, m_ssm_a_im, m_ssm_log_dt, m_ssm_b_re, m_ssm_b_im, m_ssm_c_re, m_ssm_c_im, m_ssm_d, m_glu_w, m_glu_b, m_pool_w, m_pool_scale, v_rel_bias, v_ada_w, v_ada_b, v_ln_g, v_ln_b, v_ffn_w_gate, v_ffn_w_up, v_ffn_w_down, v_w_in, v_w_out, v_ssm_a_re, v_ssm_a_im, v_ssm_log_dt, v_ssm_b_re, v_ssm_b_im, v_ssm_c_re, v_ssm_c_im, v_ssm_d, v_glu_w, v_glu_b, v_pool_w, v_pool_scale):
    return _step(dict(locals()))
```

```python
import functools
import math

import numpy as np
import jax
import jax.numpy as jnp
from jax import lax
from jax.experimental import pallas as pl
from jax.experimental.pallas import tpu as pltpu

F32 = jnp.float32
BF16 = jnp.bfloat16
MXU = jnp.bfloat16

S = 2048
D = 1024
NDEV = 8
DEPTH = 2
D_ATT, D_SSM, D_POOL, D_IN = 512, 256, 256, 2048
N_HEADS = 8
FB = 352
FBP = 384
QB = 128
PATTERNS = ((128, 1), (512, 4), (2048, 16))
POOL_WINDOWS = (2, 4, 8, 16)
N_BUCKETS, MAX_DISTANCE = 32, 2048
ALPHA = (2 * DEPTH) ** 0.25
LN_EPS = 1e-5
NEG = -1e30
GATHER_US_PER_BYTE = 43e-6
LR, B1, B2, EPS, WD, STEP = 0.001, 0.9, 0.999, 1e-08, 0.01, 10

TM = 256
TMM = 512
MIB = 1024 * 1024


def _cp(vmem_mib, sem=None):
    kw = dict(vmem_limit_bytes=vmem_mib * MIB)
    if sem is not None:
        kw["dimension_semantics"] = sem
    return pltpu.CompilerParams(**kw)


def _sds(shape, dtype):
    return jax.ShapeDtypeStruct(shape, dtype)


def _mm(a, b):
    return jnp.dot(a.astype(MXU), b.astype(MXU), preferred_element_type=F32)


def _mm_nt(a, b):
    return lax.dot_general(a.astype(MXU), b.astype(MXU), (((1,), (1,)), ((), ())), preferred_element_type=F32)


def _mm_tn(a, b):
    return lax.dot_general(a.astype(MXU), b.astype(MXU), (((0,), (0,)), ((), ())), preferred_element_type=F32)


def _ln_stats(x):
    mu = jnp.mean(x, axis=-1, keepdims=True)
    xc = x - mu
    var = jnp.mean(xc * xc, axis=-1, keepdims=True)
    rstd = lax.rsqrt(var + LN_EPS)
    return xc * rstd, rstd


def _ln_bwd(dn, n, rstd):
    return rstd * (dn - jnp.mean(dn, axis=-1, keepdims=True) - n * jnp.mean(dn * n, axis=-1, keepdims=True))


def _me():
    return 4 * lax.axis_index("x") + 2 * lax.axis_index("y") + lax.axis_index("c")


ANY = pl.BlockSpec(memory_space=pl.ANY)
PIN_BYTES = 1 << 19


def _pallas_call(*a, **k):
    pin_all = k.pop("pin_all", False)
    big = lambda o: pin_all or math.prod(o.shape) * o.dtype.itemsize >= PIN_BYTES
    pin = lambda o: pltpu.HBM(o.shape, o.dtype) if isinstance(o, jax.ShapeDtypeStruct) and big(o) else o
    osh = k["out_shape"]
    k["out_shape"] = tuple(pin(o) for o in osh) if isinstance(osh, (tuple, list)) else pin(osh)
    fn = pl.pallas_call(*a, **k)

    def run(*args):
        return fn(*[pltpu.with_memory_space_constraint(x, pltpu.HBM) if big(x) else x for x in args])
    return run


class Gather:
    def __init__(self, srcs):
        self.srcs = list(srcs)
        self.n = len(self.srcs)
        self.bufs = []
        self.out_shapes = [_sds((NDEV,) + a.shape, a.dtype) for a in self.srcs]
        self.sems = [pltpu.SemaphoreType.DMA((7 * self.n,)), pltpu.SemaphoreType.DMA((7 * self.n,)),
                     pltpu.SemaphoreType.DMA((self.n,))]

    def _parts(self, srcs, outs, sems):
        send_sems, recv_sems, loc_sems = sems
        x, y, c = lax.axis_index("x"), lax.axis_index("y"), lax.axis_index("c")
        me, sib = (x, y, c), (x, y, 1 - c)
        chips = [(1 - x, y), (x, 1 - y), (1 - x, 1 - y)]
        slot = lambda d: 4 * d[0] + 2 * d[1] + d[2]

        def copy(a, k, block, to, src=None):
            dst = outs[a].at[slot(block)]
            return pltpu.make_async_remote_copy(
                src_ref=dst if src is None else src, dst_ref=dst,
                send_sem=send_sems.at[7 * a + k], recv_sem=recv_sems.at[7 * a + k],
                device_id=to, device_id_type=pl.DeviceIdType.MESH)

        local = [pltpu.make_async_copy(srcs[a], outs[a].at[slot(me)], loc_sems.at[a]) for a in range(self.n)]
        return me, sib, chips, c, copy, local

    def start(self, srcs, bufs, outs, sems):
        me, sib, chips, c, copy, local = self._parts(srcs, outs, sems)
        for a in range(self.n):
            local[a].start()
            copy(a, 0, me, sib, src=srcs[a]).start()
            for j, chip in enumerate(chips):
                copy(a, 1 + j, me, (*chip, c), src=srcs[a]).start()

    def finish(self, srcs, bufs, outs, sems):
        me, sib, chips, c, copy, local = self._parts(srcs, outs, sems)
        for a in range(self.n):
            for j, chip in enumerate(chips):
                copy(a, 1 + j, (*chip, c), me).wait_recv()
                copy(a, 4 + j, (*chip, c), sib).start()
        for a in range(self.n):
            copy(a, 0, sib, me).wait_recv()
            copy(a, 0, me, sib, src=srcs[a]).wait_send()
            for j, chip in enumerate(chips):
                copy(a, 4 + j, (*chip, 1 - c), me).wait_recv()
                copy(a, 1 + j, me, (*chip, c), src=srcs[a]).wait_send()
                copy(a, 4 + j, (*chip, c), sib).wait_send()
            local[a].wait()


def _call(body, *, name, grid, in_specs, out_specs, out_shape, args, scratch=(), cp=None, ride=None):
    out_specs, out_shape, scratch = list(out_specs), list(out_shape), list(scratch)
    if ride is None:
        outs = _pallas_call(body, name=name, grid=grid, in_specs=list(in_specs), out_specs=tuple(out_specs),
                              out_shape=tuple(out_shape), scratch_shapes=scratch, compiler_params=cp)(*args)
        return list(outs), []
    nin, nout, nscr, n, nb, no = len(in_specs), len(out_specs), len(scratch), ride.n, len(ride.bufs), len(ride.out_shapes)
    steps = list(grid)

    def wrapped(*refs):
        h_in, r_src, r_buf = refs[:nin], refs[nin:nin + n], refs[nin + n:nin + n + nb]
        o0 = nin + n + nb
        h_out, r_out = refs[o0:o0 + nout], refs[o0 + nout:o0 + nout + no]
        s0 = o0 + nout + no
        h_scr, sems = refs[s0:s0 + nscr], refs[s0 + nscr:]
        ids = [pl.program_id(a) for a in range(len(steps))]
        first = functools.reduce(jnp.logical_and, [i == 0 for i in ids])
        last = functools.reduce(jnp.logical_and, [i == s - 1 for i, s in zip(ids, steps)])

        @pl.when(first)
        def _():
            ride.start(r_src, r_buf, r_out, sems)

        body(*h_in, *h_out, *h_scr)

        @pl.when(last)
        def _():
            ride.finish(r_src, r_buf, r_out, sems)

    aliases = {nin + n + k: nout + k for k in range(nb)}
    outs = _pallas_call(
        wrapped, name=name, grid=grid, in_specs=list(in_specs) + [ANY] * (n + nb),
        out_specs=tuple(out_specs + [ANY] * no), out_shape=tuple(out_shape + ride.out_shapes),
        scratch_shapes=scratch + ride.sems, input_output_aliases=aliases, compiler_params=cp,
    )(*args, *ride.srcs, *ride.bufs)
    return list(outs[:nout]), list(outs[nout:])


def _exchange(ride, name):
    def body(dummy_ref, o_ref):
        o_ref[...] = dummy_ref[...]

    one = pl.BlockSpec((8, 128), lambda i: (0, 0))
    _, outs = _call(body, name=name, grid=(1,), in_specs=[one], out_specs=[one], out_shape=[_sds((8, 128), F32)],
                    args=(jnp.zeros((8, 128), F32),), ride=ride)
    return outs


HBM = pl.BlockSpec(memory_space=pltpu.HBM)
SEM = pl.BlockSpec(memory_space=pltpu.SEMAPHORE)


def routes_all(me):
    return [(k, me ^ k, me, me ^ k) for k in range(NDEV)]


def _scatter_copies(srcs, lands, sems, sending, whole, rows):
    send_sems, recv_sems, loc_sems = sems
    me = _me()
    rts = routes_all(me)
    part = lambda ref, slab: ref if whole else ref.at[slab] if rows is None else ref.at[slab, pl.ds(0, rows)]
    remote_ix = [r for r, (k, _, _, _) in enumerate(rts) if k != 0]
    local_ix = [r for r, (k, _, _, _) in enumerate(rts) if k == 0]
    remote, local = [], []
    for a in range(len(srcs)):
        for n, r in enumerate(remote_ix):
            k, slab, there, here = rts[r]
            t = me ^ k
            sem = len(remote_ix) * a + n
            remote.append(pltpu.make_async_remote_copy(
                src_ref=part(srcs[a], slab), dst_ref=lands[a].at[there if sending else here], send_sem=send_sems.at[sem],
                recv_sem=recv_sems.at[sem], device_id=(t // 4, (t // 2) % 2, t % 2), device_id_type=pl.DeviceIdType.MESH))
        for n, r in enumerate(local_ix):
            _, slab, there, _ = rts[r]
            local.append(pltpu.make_async_copy(part(srcs[a], slab), lands[a].at[there], loc_sems.at[len(local_ix) * a + n]))
    return remote, local


def scatter_start(payloads, name, whole=False, rows=None):
    n = len(payloads)
    nr = NDEV - 1

    def body(*refs):
        srcs, lands, sems = refs[:n], refs[n:2 * n], refs[2 * n:2 * n + 3]
        remote, local = _scatter_copies(srcs, lands, sems, True, whole, rows)
        for cp in local + remote:
            cp.start()
        refs[-1][...] = jnp.zeros((8, 128), F32)

    thru = [pltpu.HBM(p.shape, p.dtype) for p in payloads]
    land_shapes = [(NDEV,) + p.shape if whole else p.shape if rows is None else (NDEV, rows) + p.shape[2:] for p in payloads]
    outs = pl.pallas_call(
        body, name=name,
        out_shape=(pltpu.SemaphoreType.DMA((nr * n,)), pltpu.SemaphoreType.DMA((nr * n,)), pltpu.SemaphoreType.DMA((n,)),
                   *thru, *[pltpu.HBM(sh, p.dtype) for sh, p in zip(land_shapes, payloads)], _sds((8, 128), F32)),
        in_specs=[HBM] * (2 * n),
        out_specs=(SEM, SEM, SEM, *[HBM] * (2 * n), pl.BlockSpec(memory_space=pltpu.VMEM)),
        input_output_aliases={i: 3 + i for i in range(2 * n)},
        compiler_params=pltpu.CompilerParams(has_side_effects=pltpu.SideEffectType.DATAFLOW_SIDE_EFFECTING),
    )(*[pltpu.with_memory_space_constraint(p, pltpu.HBM) for p in payloads],
      *[pltpu.with_memory_space_constraint(lax.empty(sh, p.dtype), pltpu.HBM) for sh, p in zip(land_shapes, payloads)])
    return (outs[:3], outs[3:3 + n], outs[3 + n:3 + 2 * n], whole, rows), outs[-1][0, 0]


def scatter_wait(handle, after, name):
    sems, srcs_thru, lands_thru, whole, rows = handle
    n = len(srcs_thru)
    after = list(after) if isinstance(after, (list, tuple)) else [after]

    def body(*refs):
        srcs, lands, sems_ = refs[:n], refs[n:2 * n], refs[2 * n:2 * n + 3]
        remote, local = _scatter_copies(srcs, lands, sems_, False, whole, rows)
        for cp in remote:
            cp.wait_send()
            cp.wait_recv()
        for cp in local:
            cp.wait()

    outs = pl.pallas_call(
        body, name=name, out_shape=tuple(pltpu.HBM(p.shape, p.dtype) for p in (*srcs_thru, *lands_thru)),
        in_specs=[HBM] * (2 * n) + [SEM] * 3 + [HBM] * len(after), out_specs=tuple([HBM] * (2 * n)),
        input_output_aliases={i: i for i in range(2 * n)},
        compiler_params=pltpu.CompilerParams(has_side_effects=pltpu.SideEffectType.DATAFLOW_SIDE_EFFECTING),
    )(*srcs_thru, *lands_thru, *sems, *[pltpu.with_memory_space_constraint(a, pltpu.HBM) for a in after])
    return list(outs[n:])


def _row_spec(cols, tm=TM):
    return pl.BlockSpec((tm, cols), lambda i: (i, 0))


def _full_spec(shape):
    nd = len(shape)
    return pl.BlockSpec(shape, lambda i: (0,) * nd)


def ln_mod_fwd(x, mod, sub, name):
    def body(x_ref, mod_ref, h_ref):
        n, _ = _ln_stats(x_ref[...])
        shift = mod_ref[3 * sub:3 * sub + 1, :]
        scale = mod_ref[3 * sub + 1:3 * sub + 2, :]
        h_ref[...] = (n * (1.0 + scale) + shift).astype(MXU)

    return _pallas_call(
        body, name=name, grid=(S // TM,),
        in_specs=[_row_spec(D), _full_spec((9, D))], out_specs=_row_spec(D),
        out_shape=_sds((S, D), MXU), compiler_params=_cp(32, ("arbitrary",)))(x, mod)


def res_ln_fwd(x, f, mod, sub, lng, lnb, w, name, nxt=None):
    def body(x_ref, f_ref, mod_ref, g_ref, b_ref, *rest):
        gate = mod_ref[3 * sub + 2:3 * sub + 3, :]
        r = ALPHA * x_ref[...] + (w * gate) * f_ref[...]
        n, _ = _ln_stats(r)
        xo = n * g_ref[sub:sub + 1, :] + b_ref[sub:sub + 1, :]
        rest[-1 if nxt is None else -2][...] = xo
        if nxt is not None:
            nmod_ref, h_ref = rest[0], rest[-1]
            n2, _ = _ln_stats(xo)
            s2 = nxt[1]
            h_ref[...] = (n2 * (1.0 + nmod_ref[3 * s2 + 1:3 * s2 + 2, :]) + nmod_ref[3 * s2:3 * s2 + 1, :]).astype(MXU)

    more = nxt is not None
    return _pallas_call(
        body, name=name, grid=(S // TM,),
        in_specs=[_row_spec(D), _row_spec(D), _full_spec((9, D)), _full_spec((3, D)), _full_spec((3, D))] + [_full_spec((9, D))] * more,
        out_specs=(_row_spec(D),) + (_row_spec(D),) * more, out_shape=(_sds((S, D), F32),) + (_sds((S, D), MXU),) * more,
        compiler_params=_cp(32, ("arbitrary",)))(x, f, mod, lng, lnb, *([nxt[0]] if more else []))


def res_ln_bwd(x, f, mod, sub, lng, dxo, w, name):
    def body(x_ref, f_ref, mod_ref, g_ref, dxo_ref, dxa_ref, df_ref, sums_ref):
        i = pl.program_id(0)
        gate = mod_ref[3 * sub + 2:3 * sub + 3, :]
        fv = f_ref[...]
        r = ALPHA * x_ref[...] + (w * gate) * fv
        n, rstd = _ln_stats(r)
        dxo = dxo_ref[...]
        dr = _ln_bwd(dxo * g_ref[sub:sub + 1, :], n, rstd)
        dxa_ref[...] = ALPHA * dr
        df_ref[...] = ((w * gate) * dr).astype(MXU)
        part = jnp.concatenate([
            jnp.sum(dxo * n, axis=0, keepdims=True),
            jnp.sum(dxo, axis=0, keepdims=True),
            jnp.sum(dr * fv, axis=0, keepdims=True) * w,
            jnp.zeros((5, D), F32)], axis=0)

        @pl.when(i == 0)
        def _():
            sums_ref[...] = part

        @pl.when(i > 0)
        def _():
            sums_ref[...] += part

    return _call(
        body, name=name, grid=(S // TM,),
        in_specs=[_row_spec(D), _row_spec(D), _full_spec((9, D)), _full_spec((3, D)), _row_spec(D)],
        out_specs=(_row_spec(D), _row_spec(D), _full_spec((8, D))),
        out_shape=(_sds((S, D), F32), _sds((S, D), MXU), _sds((8, D), F32)),
        cp=_cp(32, ("arbitrary",)), args=(x, f, mod, lng, dxo))[0]


def ln_mod_bwd(x, dh, mod, sub, dxa, name):
    def body(x_ref, dh_ref, mod_ref, dxa_ref, dx_ref, sums_ref):
        i = pl.program_id(0)
        scale = mod_ref[3 * sub + 1:3 * sub + 2, :]
        n, rstd = _ln_stats(x_ref[...])
        dh = dh_ref[...]
        dx_ref[...] = dxa_ref[...] + _ln_bwd(dh * (1.0 + scale), n, rstd)
        part = jnp.concatenate([
            jnp.sum(dh, axis=0, keepdims=True),
            jnp.sum(dh * n, axis=0, keepdims=True),
            jnp.zeros((6, D), F32)], axis=0)

        @pl.when(i == 0)
        def _():
            sums_ref[...] = part

        @pl.when(i > 0)
        def _():
            sums_ref[...] += part

    return _call(
        body, name=name, grid=(S // TM,),
        in_specs=[_row_spec(D), _row_spec(D), _full_spec((9, D)), _row_spec(D)],
        out_specs=(_row_spec(D), _full_spec((8, D))),
        out_shape=(_sds((S, D), F32), _sds((8, D), F32)),
        cp=_cp(32, ("arbitrary",)), args=(x, dh, mod, dxa))[0]


def ln_join_bwd(xp, fp, modp, subp, lngp, lnbp, wp, dh, mod, sub, dxa, name):
    def body(xp_ref, fp_ref, modp_ref, g_ref, b_ref, dh_ref, mod_ref, dxa_ref, dxap_ref, dfp_ref, sumsp_ref, sums_ref):
        i = pl.program_id(0)
        gate = modp_ref[3 * subp + 2:3 * subp + 3, :]
        fv = fp_ref[...]
        n, rstd = _ln_stats(ALPHA * xp_ref[...] + (wp * gate) * fv)
        gain = g_ref[subp:subp + 1, :]
        n2, rstd2 = _ln_stats(n * gain + b_ref[subp:subp + 1, :])
        dh = dh_ref[...]
        dx = dxa_ref[...] + _ln_bwd(dh * (1.0 + mod_ref[3 * sub + 1:3 * sub + 2, :]), n2, rstd2)
        dr = _ln_bwd(dx * gain, n, rstd)
        dxap_ref[...] = ALPHA * dr
        dfp_ref[...] = ((wp * gate) * dr).astype(MXU)
        partp = jnp.concatenate([
            jnp.sum(dx * n, axis=0, keepdims=True), jnp.sum(dx, axis=0, keepdims=True),
            jnp.sum(dr * fv, axis=0, keepdims=True) * wp, jnp.zeros((5, D), F32)], axis=0)
        part = jnp.concatenate([
            jnp.sum(dh, axis=0, keepdims=True), jnp.sum(dh * n2, axis=0, keepdims=True), jnp.zeros((6, D), F32)], axis=0)

        @pl.when(i == 0)
        def _():
            sumsp_ref[...] = partp
            sums_ref[...] = part

        @pl.when(i > 0)
        def _():
            sumsp_ref[...] += partp
            sums_ref[...] += part

    return _pallas_call(
        body, name=name, grid=(S // TM,),
        in_specs=[_row_spec(D), _row_spec(D), _full_spec((9, D)), _full_spec((3, D)), _full_spec((3, D)), _row_spec(D),
                  _full_spec((9, D)), _row_spec(D)],
        out_specs=(_row_spec(D), _row_spec(D), _full_spec((8, D)), _full_spec((8, D))),
        out_shape=(_sds((S, D), F32), _sds((S, D), MXU), _sds((8, D), F32), _sds((8, D), F32)),
        compiler_params=_cp(40, ("arbitrary",)))(xp, fp, modp, lngp, lnbp, dh, mod, dxa)


def loss_fwd_bwd(y, target, name):
    def body(y_ref, t_ref, l_ref, dy_ref):
        i = pl.program_id(0)
        e = y_ref[...] - t_ref[...]
        dy_ref[...] = e * (1.0 / D)
        part = jnp.zeros((8, 128), F32) + (0.5 / D) * jnp.sum(e * e)

        @pl.when(i == 0)
        def _():
            l_ref[...] = part

        @pl.when(i > 0)
        def _():
            l_ref[...] += part

    return _pallas_call(
        body, name=name, grid=(S // TM,),
        in_specs=[_row_spec(D), _row_spec(D)], out_specs=(_full_spec((8, 128)), _row_spec(D)),
        out_shape=(_sds((8, 128), F32), _sds((S, D), F32)),
        compiler_params=_cp(32, ("arbitrary",)))(y, target)


HB = 2 * FBP
NHB = NDEV * FBP // HB
TMB = 1024


def _wrows(buffers=2):
    return pl.BlockSpec((HB, D), lambda j, i: (j, 0), pipeline_mode=pl.Buffered(buffers))


def _resident(shape):
    return pl.BlockSpec(shape, lambda j, i: (0, 0), pipeline_mode=pl.Buffered(1))


def ffn_fwd(h, wgt, wut, wd, name, ride=None):
    def body(h_ref, wg_ref, wu_ref, wd_ref, g_ref, u_ref, f_ref):
        j, i = pl.program_id(0), pl.program_id(1)
        hv = h_ref[...]
        g = _mm_nt(hv, wg_ref[...])
        u = _mm_nt(hv, wu_ref[...])
        g_ref[...] = g.astype(MXU)
        u_ref[...] = u.astype(MXU)
        a = g * jax.nn.sigmoid(g) * u
        part = _mm(a, wd_ref[...])
        rows = pl.ds(pl.multiple_of(i * TMB, TMB), TMB)

        @pl.when(j == 0)
        def _():
            f_ref[rows, :] = part

        @pl.when(j > 0)
        def _():
            f_ref[rows, :] += part

    gu = pl.BlockSpec((TMB, HB), lambda j, i: (i, j))
    return _call(
        body, name=name, grid=(NHB, S // TMB),
        in_specs=[pl.BlockSpec((TMB, D), lambda j, i: (i, 0)), _wrows(), _wrows(), _wrows()],
        out_specs=(gu, gu, _resident((S, D))),
        out_shape=(_sds((S, NDEV * FBP), MXU), _sds((S, NDEV * FBP), MXU), _sds((S, D), F32)),
        cp=_cp(52, ("arbitrary", "arbitrary")), args=(h, wgt, wut, wd), ride=ride)


def ffn_bwd(df, h, g, u, wgt, wut, wd, name):
    ni = S // TMB

    def body(df_ref, h_ref, g_ref, u_ref, wg_ref, wu_ref, wd_ref, dwg_ref, dwu_ref, dwd_ref, dh_ref,
             ag_ref, au_ref, ad_ref):
        j, i = pl.program_id(0), pl.program_id(1)
        dfv, hv = df_ref[...], h_ref[...]
        gv, uv = g_ref[...].astype(F32), u_ref[...].astype(F32)
        da = _mm_nt(dfv, wd_ref[...])
        sg = jax.nn.sigmoid(gv)
        silu = gv * sg
        du = da * silu
        dg = da * uv * (sg * (1.0 + gv * (1.0 - sg)))
        p_d = _mm_tn(silu * uv, dfv)
        p_g = _mm_tn(dg, hv)
        p_u = _mm_tn(du, hv)

        @pl.when(i == 0)
        def _():
            ad_ref[...] = p_d
            ag_ref[...] = p_g
            au_ref[...] = p_u

        @pl.when(i > 0)
        def _():
            ad_ref[...] += p_d
            ag_ref[...] += p_g
            au_ref[...] += p_u

        @pl.when(i == ni - 1)
        def _():
            dwd_ref[...] = ad_ref[...].astype(BF16)
            dwg_ref[...] = ag_ref[...].astype(BF16)
            dwu_ref[...] = au_ref[...].astype(BF16)

        part = _mm(dg, wg_ref[...]) + _mm(du, wu_ref[...])
        rows = pl.ds(pl.multiple_of(i * TMB, TMB), TMB)

        @pl.when(j == 0)
        def _():
            dh_ref[rows, :] = part

        @pl.when(j > 0)
        def _():
            dh_ref[rows, :] += part

    gu = pl.BlockSpec((TMB, HB), lambda j, i: (i, j))
    rowt = pl.BlockSpec((TMB, D), lambda j, i: (i, 0))
    return _call(
        body, name=name, grid=(NHB, ni),
        in_specs=[rowt, rowt, gu, gu, _wrows(1), _wrows(1), _wrows(1)],
        out_specs=(_wrows(1), _wrows(1), _wrows(1), _resident((S, D))),
        out_shape=(_sds((NDEV * FBP, D), BF16), _sds((NDEV * FBP, D), BF16), _sds((NDEV * FBP, D), BF16), _sds((S, D), F32)),
        scratch=[pltpu.VMEM((HB, D), F32), pltpu.VMEM((HB, D), F32), pltpu.VMEM((HB, D), F32)],
        cp=_cp(60, ("arbitrary", "arbitrary")), args=(df, h, g, u, wgt, wut, wd))[0]


def win_fwd(h, win, name, ride=None):
    def body(h_ref, w_ref, z_ref):
        hv = h_ref[...]
        for j in range(NDEV):
            z_ref[:, 256 * j:256 * (j + 1)] = _mm(hv, w_ref[j])

    return _call(
        body, name=name, grid=(S // TMM,),
        in_specs=[_row_spec(D, TMM), _full_spec((NDEV, D, 256))],
        out_specs=[_row_spec(D_IN, TMM)], out_shape=[_sds((S, D_IN), F32)],
        cp=_cp(40, ("arbitrary",)), args=(h, win), ride=ride)


def win_bwd(dparts, h, win, name):
    ni = S // TMM

    def body(dq_ref, dk_ref, dv_ref, dus_ref, dup_ref, h_ref, w_ref, dh_ref, dw_ref, acc_ref):
        i = pl.program_id(0)
        hv = h_ref[...]
        cols = [dq_ref[:, 0:256], dq_ref[:, 256:512], dk_ref[:, 0:256], dk_ref[:, 256:512],
                dv_ref[:, 0:256], dv_ref[:, 256:512], dus_ref[...], dup_ref[...]]
        dh = jnp.zeros((TMM, D), F32)
        for j in range(NDEV):
            dz = cols[j].astype(MXU)
            dh = dh + _mm_nt(dz, w_ref[j])
            p = _mm_tn(hv, dz)

            @pl.when(i == 0)
            def _():
                acc_ref[j] = p

            @pl.when(i > 0)
            def _():
                acc_ref[j] += p

        dh_ref[...] = dh

        @pl.when(i == ni - 1)
        def _():
            dw_ref[...] = acc_ref[...].astype(BF16)

    return _call(
        body, name=name, grid=(ni,),
        in_specs=[_row_spec(512, TMM), _row_spec(512, TMM), _row_spec(512, TMM), _row_spec(256, TMM), _row_spec(256, TMM),
                  _row_spec(D, TMM), _full_spec((NDEV, D, 256))],
        out_specs=(_row_spec(D, TMM), _full_spec((NDEV, D, 256))),
        out_shape=(_sds((S, D), F32), _sds((NDEV, D, 256), BF16)),
        scratch=[pltpu.VMEM((NDEV, D, 256), F32)],
        cp=_cp(48, ("arbitrary",)), args=(*dparts, h, win))[0]


def wout_fwd(ya, ys, yp, wout, name, ride=None):
    def body(ya_ref, ys_ref, yp_ref, w_ref, o_ref):
        w = w_ref[...].reshape(D, D)
        o_ref[...] = _mm(ya_ref[...], w[0:512]) + _mm(ys_ref[...], w[512:768]) + _mm(yp_ref[...], w[768:1024])

    return _call(
        body, name=name, grid=(S // TMM,),
        in_specs=[_row_spec(512, TMM), _row_spec(256, TMM), _row_spec(256, TMM), _full_spec((NDEV, 128, D))],
        out_specs=[_row_spec(D, TMM)], out_shape=[_sds((S, D), F32)],
        cp=_cp(40, ("arbitrary",)), args=(ya, ys, yp, wout), ride=ride)


def wout_bwd(do, ya, ys, yp, wout, name):
    ni = S // TMM

    def body(do_ref, ya_ref, ys_ref, yp_ref, w_ref, dya_ref, dys_ref, dyp_ref, dw_ref, acc_ref):
        i = pl.program_id(0)
        w = w_ref[...].reshape(D, D)
        dov = do_ref[...]
        dya_ref[...] = _mm_nt(dov, w[0:512])
        dys_ref[...] = _mm_nt(dov, w[512:768])
        dyp_ref[...] = _mm_nt(dov, w[768:1024])
        parts = [(0, 512, _mm_tn(ya_ref[...], dov)), (512, 768, _mm_tn(ys_ref[...], dov)),
                 (768, 1024, _mm_tn(yp_ref[...], dov))]
        for lo, hi, p in parts:
            @pl.when(i == 0)
            def _():
                acc_ref[lo:hi, :] = p

            @pl.when(i > 0)
            def _():
                acc_ref[lo:hi, :] += p

        @pl.when(i == ni - 1)
        def _():
            dw_ref[...] = acc_ref[...].astype(BF16).reshape(NDEV, 128, D)

    return _call(
        body, name=name, grid=(ni,),
        in_specs=[_row_spec(D, TMM), _row_spec(512, TMM), _row_spec(256, TMM), _row_spec(256, TMM),
                  _full_spec((NDEV, 128, D))],
        out_specs=(_row_spec(512, TMM), _row_spec(256, TMM), _row_spec(256, TMM), _full_spec((NDEV, 128, D))),
        out_shape=(_sds((S, 512), F32), _sds((S, 256), F32), _sds((S, 256), F32), _sds((NDEV, 128, D), BF16)),
        scratch=[pltpu.VMEM((D, D), F32)],
        cp=_cp(40, ("arbitrary",)), args=(do, ya, ys, yp, wout))[0]


def _t5_bucket(dist):
    max_exact = N_BUCKETS // 2
    d = np.maximum(dist, 1).astype(np.float32)
    large = max_exact + (np.log(d / max_exact) / math.log(MAX_DISTANCE / max_exact)
                         * (N_BUCKETS - max_exact)).astype(np.int32)
    large = np.minimum(large, N_BUCKETS - 1)
    return np.where(dist < max_exact, dist, large).astype(np.int32)


def _att_static():
    i = np.arange(QB)[:, None]
    j = np.arange(2 * QB)[None, :]
    r = i + QB - j
    buckets, bands = [], []
    for window, dil in PATTERNS:
        bands.append((r >= 0) & (r <= window // dil))
        buckets.append(_t5_bucket(np.clip(r, 0, None) * dil))
    return np.stack(buckets), np.stack(bands), np.broadcast_to(j >= QB, (QB, 2 * QB))


def att_bias(rel_bias):
    m = np.arange(2 * QB)
    rows = []
    for window, dil in PATTERNS:
        r = QB - m
        ok = (r >= 0) & (r <= window // dil)
        b = rel_bias[_t5_bucket(np.clip(r, 0, None) * dil)]
        rows.append(jnp.where(ok[:, None], b, NEG).T)
    return jnp.broadcast_to(jnp.stack(rows)[:, :, None, :], (3, N_HEADS, 8, 2 * QB))


def _bias_tiles(t_ref, tiles):
    col = lax.broadcasted_iota(jnp.int32, (QB, 2 * QB), 1)
    for p in range(3):
        for hh in range(2):
            t = pltpu.roll(jnp.broadcast_to(t_ref[p, hh, 0:1, :], (QB, 2 * QB)), 0, 1, stride=1, stride_axis=0)
            tiles[p, hh, 0] = t
            tiles[p, hh, 1] = jnp.where(col >= QB, t, NEG)


def _permute_in(dst_ref, src_ref, d, scale=None, pad=QB):
    L = S // d
    for r in range(d):
        v = src_ref[pl.ds(r, L, stride=d), :] if d > 1 else src_ref[...]
        if scale is not None:
            v = v * scale
        dst_ref[pad + r * L:pad + (r + 1) * L, :] = v.astype(dst_ref.dtype)


def att_fwd(z, bias, name, ride=None):
    def body(q_ref, k_ref, v_ref, t_ref, y_ref, l_ref, qs, ks, vs, o_perm, l_perm, o_nat, l_nat, b_ref):
        _bias_tiles(t_ref, b_ref)
        zero_pad = jnp.zeros((QB, 128), MXU)
        ks[0:QB, :] = zero_pad
        vs[0:QB, :] = zero_pad
        lane = lax.broadcasted_iota(jnp.int32, (QB, 128), 1)
        for p, (_, d) in enumerate(PATTERNS):
            L = S // d
            nb = L // QB
            _permute_in(qs, q_ref, d, scale=0.125, pad=0)
            _permute_in(ks, k_ref, d)
            _permute_in(vs, v_ref, d)

            def blk(b, carry):
                r0 = pl.multiple_of(b * QB, QB)
                q = qs[pl.ds(r0, QB), :]
                kb = ks[pl.ds(r0, 2 * QB), :]
                vb = vs[pl.ds(r0, 2 * QB), :]
                first = ((b % nb) == 0).astype(jnp.int32)
                res = []
                for hh in range(2):
                    sel = (lane < 64) if hh == 0 else (lane >= 64)
                    qm = jnp.where(sel, q, jnp.zeros_like(q))
                    s = _mm_nt(qm, kb) + b_ref[p, hh, first]
                    m = jnp.max(s, axis=1, keepdims=True)
                    pe = jnp.exp(s - m)
                    den = jnp.sum(pe, axis=1, keepdims=True)
                    res.append((_mm(pe, vb) / den, m + jnp.log(den)))
                o_perm[pl.ds(r0, QB), :] = jnp.where(lane < 64, res[0][0], res[1][0])
                l_perm[pl.ds(r0, QB), :] = jnp.where(lane < 64, res[0][1], res[1][1])
                return carry

            lax.fori_loop(0, S // QB, blk, 0, unroll=8)
            for r in range(d):
                if d > 1:
                    o_nat[p, pl.ds(r, L, stride=d), :] = o_perm[r * L:(r + 1) * L, :]
                    l_nat[p, pl.ds(r, L, stride=d), :] = l_perm[r * L:(r + 1) * L, :]
                else:
                    o_nat[p] = o_perm[...]
                    l_nat[p] = l_perm[...]
        l0, l1, l2 = l_nat[0], l_nat[1], l_nat[2]
        m = jnp.maximum(jnp.maximum(l0, l1), l2)
        e0, e1, e2 = jnp.exp(l0 - m), jnp.exp(l1 - m), jnp.exp(l2 - m)
        den = e0 + e1 + e2
        y_ref[...] = (e0 * o_nat[0] + e1 * o_nat[1] + e2 * o_nat[2]) / den
        l_ref[...] = m + jnp.log(den)

    col = lambda c0: pl.BlockSpec((S, 128), lambda hp: (0, c0 + hp))
    return _call(
        body, name=name, grid=(N_HEADS // 2,),
        in_specs=[col(0), col(4), col(8), pl.BlockSpec((3, 2, 8, 2 * QB), lambda hp: (0, hp, 0, 0))],
        out_specs=(col(0), col(0)),
        out_shape=(_sds((S, D_ATT), F32), _sds((S, D_ATT), F32)),
        scratch=[pltpu.VMEM((S, 128), MXU), pltpu.VMEM((S + QB, 128), MXU), pltpu.VMEM((S + QB, 128), MXU),
                 pltpu.VMEM((S, 128), F32), pltpu.VMEM((S, 128), F32),
                 pltpu.VMEM((3, S, 128), F32), pltpu.VMEM((3, S, 128), F32),
                 pltpu.VMEM((3, 2, 2, QB, 2 * QB), F32)],
        cp=_cp(40, ("arbitrary",)), args=(z, z, z, bias), ride=ride)


def att_bwd(z, bias, y, lse, dy, name):
    def body(q_ref, k_ref, v_ref, t_ref, y_ref, l_ref, dy_ref, dq_ref, dk_ref, dv_ref, db_ref,
             qs, ks, vs, dys, ls, dds, dn_nat, dq_perm, dk_perm, dv_perm, b_ref):
        _bias_tiles(t_ref, b_ref)
        zero_pad = jnp.zeros((QB, 128), MXU)
        ks[0:QB, :] = zero_pad
        vs[0:QB, :] = zero_pad
        lane = lax.broadcasted_iota(jnp.int32, (QB, 128), 1)
        lane_s = lax.broadcasted_iota(jnp.int32, (S, 128), 1)
        t = dy_ref[...] * y_ref[...]
        sa = jnp.sum(jnp.where(lane_s < 64, t, 0.0), axis=1, keepdims=True)
        sb = jnp.sum(jnp.where(lane_s >= 64, t, 0.0), axis=1, keepdims=True)
        dn_nat[...] = jnp.where(lane_s < 64, sa, sb)
        dq_ref[...] = jnp.zeros((S, 128), F32)
        dk_ref[...] = jnp.zeros((S, 128), F32)
        dv_ref[...] = jnp.zeros((S, 128), F32)
        db_ref[...] = jnp.zeros((3, 2, QB, 2 * QB), F32)
        for p, (_, d) in enumerate(PATTERNS):
            L = S // d
            nb = L // QB
            _permute_in(qs, q_ref, d, scale=0.125, pad=0)
            _permute_in(ks, k_ref, d)
            _permute_in(vs, v_ref, d)
            _permute_in(dys, dy_ref, d, pad=0)
            _permute_in(ls, l_ref, d, pad=0)
            _permute_in(dds, dn_nat, d, pad=0)
            dk_perm[...] = jnp.zeros((S + QB, 128), F32)
            dv_perm[...] = jnp.zeros((S + QB, 128), F32)

            def blk(b, carry):
                r0 = pl.multiple_of(b * QB, QB)
                q = qs[pl.ds(r0, QB), :]
                kb = ks[pl.ds(r0, 2 * QB), :]
                vb = vs[pl.ds(r0, 2 * QB), :]
                dyb = dys[pl.ds(r0, QB), :]
                lb = ls[pl.ds(r0, QB), :]
                db = dds[pl.ds(r0, QB), :]
                first = ((b % nb) == 0).astype(jnp.int32)
                lane2 = jnp.concatenate([lane, lane], axis=0)
                own = (lane2 >> 6) == (lax.broadcasted_iota(jnp.int32, (2 * QB, 128), 0) >> 7)
                qm = jnp.where(own, jnp.concatenate([q, q], axis=0), jnp.zeros((2 * QB, 128), q.dtype))
                dym = jnp.where(own, jnp.concatenate([dyb, dyb], axis=0), jnp.zeros((2 * QB, 128), dyb.dtype))
                wide = lambda t: jnp.concatenate([jnp.broadcast_to(t[:, 0:1], (QB, 2 * QB)), jnp.broadcast_to(t[:, 64:65], (QB, 2 * QB))], axis=0)
                lse2, dd2 = wide(lb), wide(db)
                bias2 = jnp.concatenate([b_ref[p, 0, first], b_ref[p, 1, first]], axis=0)
                pr = jnp.exp(_mm_nt(qm, kb) + bias2 - lse2)
                ds = pr * (_mm_nt(dym, vb) - dd2)
                db_ref[p, 0] += ds[0:QB]
                db_ref[p, 1] += ds[QB:2 * QB]
                dq2 = _mm(ds, kb)
                dqs = [dq2[0:QB], dq2[QB:2 * QB]]
                dkb = _mm_tn(ds, qm)
                dvb = _mm_tn(pr, dym)
                dq_perm[pl.ds(r0, QB), :] = jnp.where(lane < 64, dqs[0], dqs[1])
                dk_perm[pl.ds(r0, 2 * QB), :] += dkb
                dv_perm[pl.ds(r0, 2 * QB), :] += dvb
                return carry

            lax.fori_loop(0, S // QB, blk, 0, unroll=4)
            for r in range(d):
                idx = pl.ds(r, L, stride=d) if d > 1 else pl.ds(0, S)
                dq_ref[idx, :] += dq_perm[r * L:(r + 1) * L, :] * 0.125
                dk_ref[idx, :] += dk_perm[QB + r * L:QB + (r + 1) * L, :]
                dv_ref[idx, :] += dv_perm[QB + r * L:QB + (r + 1) * L, :]

    col = lambda c0: pl.BlockSpec((S, 128), lambda hp: (0, c0 + hp))
    bspec = pl.BlockSpec((3, 2, 8, 2 * QB), lambda hp: (0, hp, 0, 0))
    return _call(
        body, name=name, grid=(N_HEADS // 2,),
        in_specs=[col(0), col(4), col(8), bspec, col(0), col(0), col(0)],
        out_specs=(col(0), col(0), col(0), pl.BlockSpec((3, 2, QB, 2 * QB), lambda hp: (0, hp, 0, 0))),
        out_shape=(_sds((S, D_ATT), F32), _sds((S, D_ATT), F32), _sds((S, D_ATT), F32),
                   _sds((3, N_HEADS, QB, 2 * QB), F32)),
        scratch=[pltpu.VMEM((S, 128), MXU), pltpu.VMEM((S + QB, 128), MXU), pltpu.VMEM((S + QB, 128), MXU),
                 pltpu.VMEM((S, 128), MXU), pltpu.VMEM((S, 128), F32), pltpu.VMEM((S, 128), F32),
                 pltpu.VMEM((S, 128), F32), pltpu.VMEM((S, 128), F32),
                 pltpu.VMEM((S + QB, 128), F32), pltpu.VMEM((S + QB, 128), F32),
                 pltpu.VMEM((3, 2, 2, QB, 2 * QB), F32)],
        cp=_cp(48, ("arbitrary",)), args=(z, z, z, bias, y, lse, dy))[0]


def relbias_grad(dbiases):
    bucket, band, _ = _att_static()
    onehot = (bucket[:, None] == np.arange(N_BUCKETS)[None, :, None, None]) & band[:, None]
    onehot = jnp.asarray(onehot.reshape(3, N_BUCKETS, QB * 2 * QB), BF16)

    def body(db0_ref, db1_ref, oh_ref, o_ref):
        acc = jnp.zeros((N_HEADS, N_BUCKETS), F32)
        for p in range(3):
            acc = acc + lax.dot_general(db0_ref[p] + db1_ref[p], oh_ref[p].astype(F32), (((1,), (1,)), ((), ())),
                                        preferred_element_type=F32, precision=lax.Precision.HIGHEST)
        o_ref[...] = acc

    vm = pl.BlockSpec(memory_space=pltpu.VMEM)
    out = _pallas_call(body, name="relbias_grad", in_specs=[vm, vm, vm], out_specs=vm,
                         out_shape=_sds((N_HEADS, N_BUCKETS), F32), compiler_params=_cp(40))(
        *[d.reshape(3, N_HEADS, QB * 2 * QB) for d in dbiases], onehot)
    return out.T


def _panel(t_ref, ri, j):
    return t_ref[ri, pl.ds(j, S, stride=8), :]


def _gelu(x):
    c = math.sqrt(2.0 / math.pi)
    th = jnp.tanh(c * (x + 0.044715 * x * x * x))
    return 0.5 * x * (1.0 + th), th


def ssm_fwd(z, a, bre, bim, cre, cim, dsk, gluw, glub, name, ride=None):
    def body(u_ref, a_ref, bre_ref, bim_ref, cre_ref, cim_ref, d_ref, gw_ref, gb_ref, y_ref, yp_ref, st_hbm, st_ref):
        u = u_ref[...]
        for j in range(8):
            st_ref[0, pl.ds(j, S, stride=8), :] = _mm(u, bre_ref[:, 128 * j:128 * (j + 1)])
            st_ref[1, pl.ds(j, S, stride=8), :] = _mm(u, bim_ref[:, 128 * j:128 * (j + 1)])
        ar, ai = a_ref[0], a_ref[1]

        def step(t, c):
            re, im = c
            i = pl.multiple_of(t * 8, 8)
            nre = ar * re - ai * im + st_ref[0, pl.ds(i, 8), :]
            nim = ar * im + ai * re + st_ref[1, pl.ds(i, 8), :]
            st_ref[0, pl.ds(i, 8), :] = nre
            st_ref[1, pl.ds(i, 8), :] = nim
            return nre, nim

        zero = jnp.zeros((8, 128), F32)
        lax.fori_loop(0, S, step, (zero, zero), unroll=8)
        y = d_ref[...] * u
        for j in range(8):
            y = y + _mm(_panel(st_ref, 0, j), cre_ref[128 * j:128 * (j + 1), :])
            y = y - _mm(_panel(st_ref, 1, j), cim_ref[128 * j:128 * (j + 1), :])
        pltpu.sync_copy(st_ref, st_hbm)
        yp_ref[...] = y
        gl, _ = _gelu(y)
        tt = _mm(gl, gw_ref[...].reshape(D_SSM, D_SSM)) + gb_ref[...]
        y_ref[...] = y * jax.nn.sigmoid(tt)

    vm = lambda shape: pl.BlockSpec(shape, lambda i: (0,) * len(shape))
    return _call(
        body, name=name, grid=(1,),
        in_specs=[pl.BlockSpec((S, 256), lambda i: (0, 6)), vm((2, 8, 128)), vm((256, 1024)), vm((256, 1024)),
                  vm((1024, 256)), vm((1024, 256)), vm((1, 256)),
                  vm((NDEV, 32, 256)), vm((1, 256))],
        out_specs=(vm((S, 256)), vm((S, 256)), pl.BlockSpec(memory_space=pl.ANY)),
        out_shape=(_sds((S, 256), F32), _sds((S, 256), F32), _sds((2, S * 8, 128), F32)),
        scratch=[pltpu.VMEM((2, S * 8, 128), F32)],
        cp=_cp(40, ("arbitrary",)), args=(z, a, bre, bim, cre, cim, dsk, gluw, glub), ride=ride)


def ssm_bwd(dy, z, ypre, st, a, bre, bim, cre, cim, dsk, gluw, glub, name):
    def body(dy_ref, u_ref, yp_ref, st_hbm, a_ref, bre_ref, bim_ref, cre_ref, cim_ref, d_ref, gw_ref, gb_ref,
             du_ref, dbre_ref, dbim_ref, dcre_ref, dcim_ref, da_ref, dd_ref, dgw_ref, dgb_ref, g_ref, st_ref):
        pltpu.sync_copy(st_hbm, st_ref)
        u = u_ref[...]
        y = yp_ref[...]
        dout = dy_ref[...]
        gw = gw_ref[...].reshape(D_SSM, D_SSM)
        gl, th = _gelu(y)
        sig = jax.nn.sigmoid(_mm(gl, gw) + gb_ref[...])
        dt = dout * y * sig * (1.0 - sig)
        dgw_ref[...] = _mm_tn(gl, dt)
        dgb_ref[...] = jnp.sum(dt, axis=0, keepdims=True)
        c = math.sqrt(2.0 / math.pi)
        dgelu = 0.5 * (1.0 + th) + 0.5 * y * (1.0 - th * th) * c * (1.0 + 3.0 * 0.044715 * y * y)
        dyv = dout * sig + _mm_nt(dt, gw) * dgelu
        dd_ref[...] = jnp.sum(dyv * u, axis=0, keepdims=True)
        for j in range(8):
            rows = slice(128 * j, 128 * (j + 1))
            g_ref[0, pl.ds(j, S, stride=8), :] = _mm_nt(dyv, cre_ref[rows, :])
            g_ref[1, pl.ds(j, S, stride=8), :] = -_mm_nt(dyv, cim_ref[rows, :])
            dcre_ref[rows, :] = _mm_tn(_panel(st_ref, 0, j), dyv)
            dcim_ref[rows, :] = -_mm_tn(_panel(st_ref, 1, j), dyv)
        ar, ai = a_ref[0], a_ref[1]

        def step(k, c4):
            gre, gim, dar, dai = c4
            i = pl.multiple_of((S - 1 - k) * 8, 8)
            nre = g_ref[0, pl.ds(i, 8), :] + ar * gre + ai * gim
            nim = g_ref[1, pl.ds(i, 8), :] + ar * gim - ai * gre
            g_ref[0, pl.ds(i, 8), :] = nre
            g_ref[1, pl.ds(i, 8), :] = nim
            sre = st_ref[0, pl.ds(i - 8, 8), :]
            sim = st_ref[1, pl.ds(i - 8, 8), :]
            return nre, nim, dar + nre * sre + nim * sim, dai + nim * sre - nre * sim

        zero = jnp.zeros((8, 128), F32)
        gre, gim, dar, dai = lax.fori_loop(0, S - 1, step, (zero, zero, zero, zero), unroll=8)
        g_ref[0, 0:8, :] = g_ref[0, 0:8, :] + ar * gre + ai * gim
        g_ref[1, 0:8, :] = g_ref[1, 0:8, :] + ar * gim - ai * gre
        da_ref[0] = dar
        da_ref[1] = dai
        du = dyv * d_ref[...]
        for j in range(8):
            cols = slice(128 * j, 128 * (j + 1))
            gr, gi = _panel(g_ref, 0, j), _panel(g_ref, 1, j)
            dbre_ref[:, cols] = _mm_tn(u, gr)
            dbim_ref[:, cols] = _mm_tn(u, gi)
            du = du + _mm_nt(gr, bre_ref[:, cols]) + _mm_nt(gi, bim_ref[:, cols])
        du_ref[...] = du

    vm = lambda shape: pl.BlockSpec(shape, lambda i: (0,) * len(shape))
    return _call(
        body, name=name, grid=(1,),
        in_specs=[vm((S, 256)), pl.BlockSpec((S, 256), lambda i: (0, 6)), vm((S, 256)), pl.BlockSpec(memory_space=pl.ANY),
                  vm((2, 8, 128)), vm((256, 1024)), vm((256, 1024)), vm((1024, 256)), vm((1024, 256)), vm((1, 256)),
                  vm((NDEV, 32, 256)), vm((1, 256))],
        out_specs=(vm((S, 256)), vm((256, 1024)), vm((256, 1024)), vm((1024, 256)), vm((1024, 256)),
                   vm((2, 8, 128)), vm((1, 256)), vm((256, 256)), vm((1, 256))),
        out_shape=(_sds((S, 256), F32), _sds((256, 1024), F32), _sds((256, 1024), F32), _sds((1024, 256), F32),
                   _sds((1024, 256), F32), _sds((2, 8, 128), F32), _sds((1, 256), F32), _sds((256, 256), F32),
                   _sds((1, 256), F32)),
        scratch=[pltpu.VMEM((2, S * 8, 128), F32), pltpu.VMEM((2, S * 8, 128), F32)],
        cp=_cp(56, ("arbitrary",)), args=(dy, z, ypre, st, a, bre, bim, cre, cim, dsk, gluw, glub))[0]


def _ssm_discretise(a_re, a_im, log_dt, b_re, b_im):
    dt = jnp.exp(log_dt)[:, None]
    er = jnp.exp(a_re * dt)
    abr, abi = er * jnp.cos(a_im * dt), er * jnp.sin(a_im * dt)
    den = a_re * a_re + a_im * a_im
    fr = ((abr - 1.0) * a_re + abi * a_im) / den
    fi = (abi * a_re - (abr - 1.0) * a_im) / den
    bbr = fr[:, :, None] * b_re - fi[:, :, None] * b_im
    bbi = fr[:, :, None] * b_im + fi[:, :, None] * b_re
    return abr, abi, bbr, bbi


def _blockdiag(t):
    g, r, c = t.shape
    eye = jnp.eye(g, dtype=t.dtype)
    return (t[:, :, None, :] * eye[:, None, :, None]).reshape(g * r, g * c)


def _blockdiag_take(m, r, c):
    g = m.shape[0] // r
    idx = jnp.arange(g)
    return m.reshape(g, r, g, c)[idx, :, idx, :]


PAD = 16


def _pool_lane_select(vals):
    lane = lax.broadcasted_iota(jnp.int32, vals[0].shape, 1)
    out = vals[3]
    for g in (2, 1, 0):
        out = jnp.where(lane < 64 * (g + 1), vals[g], out)
    return out


def _pool_counts():
    row = lax.broadcasted_iota(jnp.int32, (S, D_POOL), 0).astype(F32) + 1.0
    return _pool_lane_select([jnp.minimum(row, float(w)) for w in POOL_WINDOWS])


def _pooled(u, sa, sb):
    sums = []
    cur = u
    bufs = (sa, sb)
    for k, sh in enumerate((1, 2, 4, 8)):
        buf = bufs[k % 2]
        buf[PAD:PAD + S, :] = cur
        cur = cur + buf[PAD - sh:PAD - sh + S, :]
        sums.append(cur)
    return _pool_lane_select(sums) / _pool_counts() - u


def pool_fwd(z, pw, psc, name):
    def body(u_ref, w_ref, s_ref, y_ref, sa, sb):
        for buf in (sa, sb):
            buf[0:PAD, :] = jnp.zeros((PAD, D_POOL), F32)
        pooled = _pooled(u_ref[...], sa, sb)
        y_ref[...] = _mm(pooled, w_ref[...]) * s_ref[...]

    vm = lambda shape: pl.BlockSpec(shape, lambda i: (0,) * len(shape))
    return _pallas_call(
        body, name=name, grid=(1,),
        in_specs=[pl.BlockSpec((S, 256), lambda i: (0, 7)), vm((256, 256)), vm((1, 256))],
        out_specs=vm((S, 256)), out_shape=_sds((S, 256), F32),
        scratch_shapes=[pltpu.VMEM((S + 2 * PAD, D_POOL), F32)] * 2,
        compiler_params=_cp(40, ("arbitrary",)))(z, pw, psc)


def pool_bwd(dy, z, pw, psc, name):
    def body(dy_ref, u_ref, w_ref, s_ref, du_ref, dw_ref, ds_ref, sa, sb):
        for buf in (sa, sb):
            buf[0:PAD, :] = jnp.zeros((PAD, D_POOL), F32)
            buf[PAD + S:PAD + S + PAD, :] = jnp.zeros((PAD, D_POOL), F32)
        pooled = _pooled(u_ref[...], sa, sb)
        dyv = dy_ref[...]
        w = w_ref[...]
        ds_ref[...] = jnp.sum(dyv * _mm(pooled, w), axis=0, keepdims=True)
        dyl = dyv * s_ref[...]
        dw_ref[...] = _mm_tn(pooled, dyl)
        dpool = _mm_nt(dyl, w)
        cur = dpool / _pool_counts()
        sums = []
        bufs = (sa, sb)
        for k, sh in enumerate((1, 2, 4, 8)):
            buf = bufs[k % 2]
            buf[PAD:PAD + S, :] = cur
            cur = cur + buf[PAD + sh:PAD + sh + S, :]
            sums.append(cur)
        du_ref[...] = _pool_lane_select(sums) - dpool

    vm = lambda shape: pl.BlockSpec(shape, lambda i: (0,) * len(shape))
    return _pallas_call(
        body, name=name, grid=(1,),
        in_specs=[vm((S, 256)), pl.BlockSpec((S, 256), lambda i: (0, 7)), vm((256, 256)), vm((1, 256))],
        out_specs=(vm((S, 256)), vm((256, 256)), vm((1, 256))),
        out_shape=(_sds((S, 256), F32), _sds((256, 256), F32), _sds((1, 256), F32)),
        scratch_shapes=[pltpu.VMEM((S + 2 * PAD, D_POOL), F32)] * 2,
        compiler_params=_cp(40, ("arbitrary",)))(dy, z, pw, psc)


def ada_fwd(c_all, ada_w, ada_b_cols):
    def body(c_ref, w_ref, b_ref, o_ref):
        c = c_ref[...]
        cond = c * jax.nn.sigmoid(c)
        o_ref[...] = jnp.dot(cond, w_ref[...], preferred_element_type=F32, precision=lax.Precision.HIGHEST) + b_ref[...]

    return _pallas_call(
        body, name="ada_fwd", grid=(DEPTH,),
        in_specs=[pl.BlockSpec((NDEV, D), lambda l: (0, 0)), pl.BlockSpec((None, D, 1152), lambda l: (l, 0, 0)),
                  pl.BlockSpec((None, 1, 1152), lambda l: (l, 0, 0))],
        out_specs=pl.BlockSpec((None, NDEV, 1152), lambda l: (l, 0, 0)), out_shape=_sds((DEPTH, NDEV, 1152), F32),
        compiler_params=_cp(40, ("arbitrary",)))(c_all, ada_w, ada_b_cols)


def ada_bwd_adam(c_all, dmod_cols, w, m, v, tr):
    def body(c_ref, dm_ref, w_ref, m_ref, v_ref, g_ref, d_ref, m2_ref, v2_ref):
        c = c_ref[...]
        cond = c * jax.nn.sigmoid(c)
        g = lax.dot_general(cond, dm_ref[...], (((0,), (0,)), ((), ())), preferred_element_type=F32,
                            precision=lax.Precision.HIGHEST)
        g_ref[...] = g
        d_ref[...], m2_ref[...], v2_ref[...] = _adamw(w_ref[...], g, m_ref[...], v_ref[...])

    rs = pl.BlockSpec((None, tr, 1152), lambda l, i: (l, i, 0))
    return _pallas_call(
        body, name="ada_bwd_adam", grid=(DEPTH, D // tr),
        in_specs=[pl.BlockSpec((NDEV, tr), lambda l, i: (0, i)), pl.BlockSpec((None, NDEV, 1152), lambda l, i: (l, 0, 0)), rs, rs, rs],
        out_specs=(rs, rs, rs, rs), out_shape=tuple(_sds((DEPTH, D, 1152), F32) for _ in range(4)),
        compiler_params=_cp(48, ("arbitrary", "arbitrary")))(c_all, dmod_cols, w, m, v)


def _adamw(w, g, m, v):
    m2 = B1 * m + (1.0 - B1) * g
    v2 = B2 * v + (1.0 - B2) * (g * g)
    m_hat = m2 / (1.0 - B1 ** STEP)
    v_hat = v2 / (1.0 - B2 ** STEP)
    return -LR * (m_hat / (jnp.sqrt(v_hat) + EPS) + WD * w), m2, v2


def _sum8(ref):
    g = ref[0].astype(F32)
    for s in range(1, ref.shape[0]):
        g = g + ref[s].astype(F32)
    return g


def adam_rs(recvs, w, m, v, tr, name):
    nlay, r, cdim = w.shape
    cp = recvs[0].shape[-1]
    steps = r // tr

    def body(*refs):
        rc_refs, (w_ref, m_ref, v_ref, g_ref, d_ref, m2_ref, v2_ref) = refs[:nlay], refs[nlay:]
        for k in range(nlay):
            @pl.when(pl.program_id(0) == k)
            def _(k=k):
                g = _sum8(rc_refs[k])[:, :cdim]
                g_ref[...] = g
                d_ref[...], m2_ref[...], v2_ref[...] = _adamw(w_ref[...], g, m_ref[...], v_ref[...])

    rc = lambda k: pl.BlockSpec((NDEV, tr, cp), lambda l, i: (0, jnp.where(l == k, i, jnp.where(l < k, 0, steps - 1)), 0))
    rs = pl.BlockSpec((None, tr, cdim), lambda l, i: (l, i, 0))
    return _call(
        body, name=name, grid=(nlay, steps), in_specs=[rc(k) for k in range(nlay)] + [rs, rs, rs],
        out_specs=(rs, rs, rs, rs), out_shape=tuple(_sds(w.shape, F32) for _ in range(4)),
        cp=_cp(48, ("arbitrary", "arbitrary")), args=(*recvs, w, m, v))[0]


def adam_block(recv, w, m, v, lf, prev, name):
    half = FB // 2

    def body(*refs):
        rc_ref, w_ref, m_ref, v_ref = refs[:4]
        g_ref, d_ref, m2_ref, v2_ref = refs[-4:]
        g = _sum8(rc_ref)
        g_ref[...] = g
        d_ref[...], m2_ref[...], v2_ref[...] = _adamw(w_ref[...], g, m_ref[...], v_ref[...])

    rs = pl.BlockSpec((None, None, half, D), lambda i: (lf // 2, lf % 2, i, 0))
    prev = list(prev) if prev is not None else []
    return list(_pallas_call(
        body, name=name, grid=(2,), in_specs=[pl.BlockSpec((recv.shape[0], half, D), lambda i: (0, i, 0)), rs, rs, rs] + [ANY] * len(prev),
        out_specs=(rs, rs, rs, rs), out_shape=tuple(_sds((DEPTH, 2, FB, D), F32) for _ in range(4)),
        input_output_aliases={4 + k: k for k in range(len(prev))},
        compiler_params=_cp(48, ("arbitrary",)))(recv, w, m, v, *prev))


def adam_native(gs, ws, ms, vs, name):
    n = len(ws)

    def body(*refs):
        g_refs, w_refs, m_refs, v_refs = (refs[k * n:(k + 1) * n] for k in range(4))
        d_refs, m2_refs, v2_refs = (refs[(4 + k) * n:(5 + k) * n] for k in range(3))
        for a in range(n):
            d_refs[a][...], m2_refs[a][...], v2_refs[a][...] = _adamw(w_refs[a][...], g_refs[a][...], m_refs[a][...], v_refs[a][...])

    whole = lambda a: pl.BlockSpec(a.shape, lambda i, nd=len(a.shape): (0,) * nd)
    outs = _pallas_call(body, name=name, grid=(1,), in_specs=[whole(a) for a in (*gs, *ws, *ms, *vs)],
                        out_specs=tuple(whole(w) for w in ws) * 3,
                        out_shape=tuple(_sds(w.shape, F32) for w in ws) * 3, compiler_params=_cp(40),
                        pin_all=True)(*gs, *ws, *ms, *vs)
    return outs[:n], outs[n:2 * n], outs[2 * n:]


def sum_sources(recv, name):
    r = recv.shape[1]

    def body(rc_ref, o_ref):
        o_ref[...] = _sum8(rc_ref)

    vm = pl.BlockSpec(memory_space=pltpu.VMEM)
    return _pallas_call(body, name=name, in_specs=[vm], out_specs=vm, out_shape=_sds((r, 128), F32),
                          compiler_params=_cp(40))(recv)


def _pack(arrs, dtype=F32):
    flat = jnp.concatenate([a.reshape(-1) for a in arrs]).astype(dtype)
    n = flat.shape[0]
    tile = 128 * (32 // jnp.dtype(dtype).itemsize)
    rows = -(-n // tile) * (tile // 128)
    return jnp.pad(flat, (0, rows * 128 - n)).reshape(rows, 128)


def _unpack(vec, shapes):
    flat = vec.reshape(-1)
    out, o = [], 0
    for sh in shapes:
        n = int(np.prod(sh))
        out.append(flat[o:o + n].reshape(sh))
        o += n
    return out


WEIGHTS = ['rel_bias', 'ada_w', 'ada_b', 'ln_g', 'ln_b', 'ffn_w_gate', 'ffn_w_up', 'ffn_w_down', 'w_in', 'w_out',
           'ssm_a_re', 'ssm_a_im', 'ssm_log_dt', 'ssm_b_re', 'ssm_b_im', 'ssm_c_re', 'ssm_c_im', 'ssm_d', 'glu_w',
           'glu_b', 'pool_w', 'pool_scale']
SMALL = ['rel_bias', 'ada_b', 'ln_g', 'ln_b', 'ssm_a_re', 'ssm_a_im', 'ssm_log_dt', 'ssm_b_re', 'ssm_b_im',
         'ssm_c_re', 'ssm_c_im', 'ssm_d', 'glu_b', 'pool_w', 'pool_scale']
SMALL_EXACT = ['rel_bias', 'ln_g', 'ln_b', 'ssm_a_re', 'ssm_a_im', 'ssm_log_dt']
SMALL_ROUNDED = ['ada_b', 'ssm_b_re', 'ssm_b_im', 'ssm_c_re', 'ssm_c_im', 'ssm_d', 'glu_b', 'pool_w', 'pool_scale']
SMALL_FULL_SHAPES = {'rel_bias': (32, 8), 'ada_b': (2, 9216), 'ln_g': (2, 3, 1024), 'ln_b': (2, 3, 1024),
                     'ssm_a_re': (2, 16, 64), 'ssm_a_im': (2, 16, 64), 'ssm_log_dt': (2, 16),
                     'ssm_b_re': (2, 16, 64, 16), 'ssm_b_im': (2, 16, 64, 16), 'ssm_c_re': (2, 16, 16, 64),
                     'ssm_c_im': (2, 16, 16, 64), 'ssm_d': (2, 256), 'glu_b': (2, 256), 'pool_w': (2, 4, 64, 64),
                     'pool_scale': (2, 256)}


def _step(P):
    me = _me()
    x0 = P['x'][0]
    target = P['loss_target'][0]

    bf = lambda a: a.astype(BF16)
    padr = lambda a: jnp.pad(bf(a), ((0, 0), (0, 0), (0, FBP - FB), (0, 0)))
    ffn_b = [padr(jnp.swapaxes(P['ffn_w_gate'], 2, 3)), padr(jnp.swapaxes(P['ffn_w_up'], 2, 3)), padr(P['ffn_w_down'])]
    mix_b = [bf(P['w_in']), bf(P['w_out']), bf(P['glu_w'])]

    def shards(l, sub):
        return [t[l] for t in mix_b] if sub == 1 else [t[l, sub // 2] for t in ffn_b]

    order = [(l, sub) for l in range(DEPTH) for sub in range(3)]
    nxt = dict(zip(order[:-1], order[1:]))
    W = {key: [None] * 3 for key in order}
    c_all, lng_all, lnb_all, *W[order[0]] = _exchange(Gather([P['c'], P['ln_g'], P['ln_b']] + shards(*order[0])), "gather_first")
    gather_queue = [(key, pos, a) for key in order[1:] for pos, a in enumerate(shards(*key))]

    def gather_ride(cap_us, must=None):
        units, used = [], 0.0
        while gather_queue:
            key, _, a = gather_queue[0]
            cost = a.size * a.dtype.itemsize * GATHER_US_PER_BYTE
            if key != must and used + cost / 2 > cap_us:
                break
            units.append(gather_queue.pop(0))
            used += cost
        return (Gather([a for _, _, a in units]) if units else None), units

    def gathered(units, outs):
        for (key, pos, _), o in zip(units, outs):
            W[key][pos] = o

    c_all = c_all.reshape(NDEV, D)
    ln_g = jnp.transpose(lng_all, (1, 2, 0, 3)).reshape(DEPTH, 3, D)
    ln_b = jnp.transpose(lnb_all, (1, 2, 0, 3)).reshape(DEPTH, 3, D)

    ada_b_cols = lax.dynamic_slice_in_dim(P['ada_b'], me * 1152, 1152, axis=1).reshape(DEPTH, 1, 1152)
    modc = ada_fwd(c_all, P['ada_w'], ada_b_cols)
    (mod_all,) = _exchange(Gather([modc]), "gather_mod")
    mod_me = lax.dynamic_index_in_dim(mod_all, me, axis=2, keepdims=False)
    mod = jnp.transpose(mod_me, (1, 0, 2)).reshape(DEPTH, 9, D)

    bias = att_bias(P['rel_bias'])
    ssm = []
    for l in range(DEPTH):
        prm = (P['ssm_a_re'][l], P['ssm_a_im'][l], P['ssm_log_dt'][l], P['ssm_b_re'][l], P['ssm_b_im'][l])
        (abr, abi, bbr, bbi), disc_vjp = jax.vjp(_ssm_discretise, *prm)
        ssm.append(dict(
            vjp=disc_vjp, a=jnp.stack([abr.reshape(8, 128), abi.reshape(8, 128)]),
            bre=_blockdiag(jnp.transpose(bbr, (0, 2, 1))).astype(MXU), bim=_blockdiag(jnp.transpose(bbi, (0, 2, 1))).astype(MXU),
            cre=_blockdiag(jnp.transpose(P['ssm_c_re'][l], (0, 2, 1))).astype(MXU),
            cim=_blockdiag(jnp.transpose(P['ssm_c_im'][l], (0, 2, 1))).astype(MXU),
            d=P['ssm_d'][l].reshape(1, 256), gb=P['glu_b'][l].reshape(1, 256),
            pw=_blockdiag(P['pool_w'][l]).astype(MXU), psc=P['pool_scale'][l].reshape(1, 256)))

    saved = []
    x = x0
    h = ln_mod_fwd(x, mod[0], 0, "ln_mod_fwd_l0s0")
    for l, sub in order:
        tag = f"l{l}s{sub}"
        after = (mod[nxt[(l, sub)][0]], nxt[(l, sub)][1]) if (l, sub) in nxt else None
        if sub != 1:
            wg, wu, wd = (t.reshape(NDEV * FBP, D) for t in W[(l, sub)])
            ride, units = gather_ride(60, nxt.get((l, sub)))
            (G, U, fo), got = ffn_fwd(h, wg, wu, wd, "ffn_fwd_" + tag, ride)
            gathered(units, got)
            saved.append(dict(x=x, h=h, G=G, U=U, f=fo))
            x, *hn = res_ln_fwd(x, fo, mod[l], sub, ln_g[l], ln_b[l], 0.5, "res_ln_fwd_" + tag, after)
        else:
            sp = ssm[l]
            win, wout, gluw = W[(l, sub)]
            ride, units = gather_ride(15)
            (z,), got = win_fwd(h, win, "win_fwd_" + tag, ride)
            gathered(units, got)
            ride, units = gather_ride(55)
            (ya, lse), got = att_fwd(z, bias, "att_fwd_" + tag, ride)
            gathered(units, got)
            ride, units = gather_ride(35)
            (ys, ypre, st), got = ssm_fwd(z, sp['a'], sp['bre'], sp['bim'], sp['cre'], sp['cim'], sp['d'], gluw, sp['gb'],
                                          "ssm_fwd_" + tag, ride)
            gathered(units, got)
            yp = pool_fwd(z, sp['pw'], sp['psc'], "pool_fwd_" + tag)
            ride, units = gather_ride(12, nxt.get((l, sub)))
            (o,), got = wout_fwd(ya, ys, yp, wout, "wout_fwd_" + tag, ride)
            gathered(units, got)
            saved.append(dict(x=x, h=h, z=z, ya=ya, lse=lse, ys=ys, ypre=ypre, st=st, yp=yp, f=o))
            x, *hn = res_ln_fwd(x, o, mod[l], sub, ln_g[l], ln_b[l], 1.0, "res_ln_fwd_" + tag, after)
        h = hn[0] if hn else None
    assert not gather_queue

    loss_tile, dx = loss_fwd_bwd(x, target, "loss")
    loss = lax.psum(loss_tile[0, 0], ("x", "y", "c"))

    flights = []

    dmod = [[None] * 9 for _ in range(DEPTH)]
    dlng = [[None] * 3 for _ in range(DEPTH)]
    dlnb = [[None] * 3 for _ in range(DEPTH)]
    dbiases = [None] * DEPTH
    small_l = [dict() for _ in range(DEPTH)]
    for l, sub in reversed(order):
        tag = f"l{l}s{sub}"
        sv = saved[3 * l + sub]
        if (l, sub) == order[-1]:
            dxa, df, sums = res_ln_bwd(sv['x'], sv['f'], mod[l], sub, ln_g[l], dx, 0.5, "res_ln_bwd_" + tag)
        dlng[l][sub], dlnb[l][sub], dmod[l][3 * sub + 2] = sums[0], sums[1], sums[2]
        if sub != 1:
            f = sub // 2
            wg, wu, wd = (t.reshape(NDEV * FBP, D) for t in W[(l, sub)])
            dwg, dwu, dwd, dh = ffn_bwd(df, sv['h'], sv['G'], sv['U'], wg, wu, wd, "ffn_bwd_" + tag)
            handle, zero = scatter_start([t.reshape(NDEV, FBP, D) for t in (dwg, dwu, dwd)], "scatter_start_" + tag, rows=FB)
            flights.append(((l, sub), handle))
        else:
            sp = ssm[l]
            win, wout, gluw = W[(l, sub)]
            dya, dys, dyp, dwout = wout_bwd(df, sv['ya'], sv['ys'], sv['yp'], wout, "wout_bwd_" + tag)
            dq, dk, dv, dbiases[l] = att_bwd(sv['z'], bias, sv['ya'], sv['lse'], dya, "att_bwd_" + tag)
            dus, dbre, dbim, dcre, dcim, da, dd, dgw, dgb = ssm_bwd(
                dys, sv['z'], sv['ypre'], sv['st'], sp['a'], sp['bre'], sp['bim'], sp['cre'], sp['cim'], sp['d'],
                gluw, sp['gb'], "ssm_bwd_" + tag)
            dup, dpw, dpsc = pool_bwd(dyp, sv['z'], sp['pw'], sp['psc'], "pool_bwd_" + tag)
            dh, dwin = win_bwd((dq, dk, dv, dus, dup), sv['h'], win, "win_bwd_" + tag)
            handle, zero = scatter_start([dwin, dwout, dgw.astype(BF16).reshape(NDEV, 32, 256)], "scatter_start_" + tag)
            flights.append(((l, sub), handle))
            d_are, d_aim, d_ldt, d_bre, d_bim = sp['vjp']((
                da[0].reshape(16, 64), da[1].reshape(16, 64),
                jnp.transpose(_blockdiag_take(dbre, 16, 64), (0, 2, 1)), jnp.transpose(_blockdiag_take(dbim, 16, 64), (0, 2, 1))))
            small_l[l] = dict(
                ssm_a_re=d_are, ssm_a_im=d_aim, ssm_log_dt=d_ldt, ssm_b_re=d_bre, ssm_b_im=d_bim,
                ssm_c_re=jnp.transpose(_blockdiag_take(dcre, 64, 16), (0, 2, 1)),
                ssm_c_im=jnp.transpose(_blockdiag_take(dcim, 64, 16), (0, 2, 1)),
                ssm_d=dd.reshape(256), glu_b=dgb.reshape(256), pool_w=_blockdiag_take(dpw, 64, 64), pool_scale=dpsc.reshape(256))
        if (l, sub) == order[0]:
            dx, sums2 = ln_mod_bwd(sv['x'], dh, mod[l] + zero, sub, dxa, "ln_mod_bwd_" + tag)
        else:
            lp, sp_ = order[order.index((l, sub)) - 1]
            svp = saved[3 * lp + sp_]
            dxa, df, sums, sums2 = ln_join_bwd(svp['x'], svp['f'], mod[lp], sp_, ln_g[lp], ln_b[lp], 1.0 if sp_ == 1 else 0.5,
                                               dh, mod[l] + zero, sub, dxa, "ln_join_bwd_" + tag)
        dmod[l][3 * sub], dmod[l][3 * sub + 1] = sums2[0], sums2[1]
    grad_x = dx[None]

    small = {k: jnp.stack([small_l[l][k] for l in range(DEPTH)]) for k in small_l[0]}
    small['rel_bias'] = relbias_grad(dbiases)
    small['ada_b'] = jnp.stack([jnp.stack(dmod[l]).reshape(9 * D) for l in range(DEPTH)])
    small['ln_g'] = jnp.stack([jnp.stack(dlng[l]) for l in range(DEPTH)])
    small['ln_b'] = jnp.stack([jnp.stack(dlnb[l]) for l in range(DEPTH)])
    swaps = {'rel_bias': (0, 1), 'ln_g': (0, 1), 'ln_b': (0, 1), 'ssm_b_re': (2, 3), 'ssm_b_im': (2, 3)}
    view = lambda k, t: jnp.swapaxes(t, *swaps[k]) if k in swaps else t
    kept_shape = lambda k: np.swapaxes(np.empty(SMALL_FULL_SHAPES[k], np.bool_), *swaps.get(k, (0, 0))).shape
    small_flight, _ = scatter_start([_pack([view(k, small[k]) for k in SMALL_EXACT]),
                                     _pack([view(k, small[k]) for k in SMALL_ROUNDED], BF16)], "gather_small_start", whole=True)

    out = {}

    def put(name, g, d, m2, v2, shape):
        out['grad_' + name], out['delta_' + name] = g.reshape(shape), d.reshape(shape)
        out['new_m_' + name], out['new_v_' + name] = m2.reshape(shape), v2.reshape(shape)

    def wmv(name):
        return [P[pre + name] for pre in ('', 'm_', 'v_')]

    recv = {}
    started_last = small_flight[1][0]
    for key, handle in flights[:-1]:
        recv[key] = scatter_wait(handle, started_last, "scatter_wait_l%ds%d" % key)
    mixer_done = []
    for pos, (name, tr) in enumerate((('w_in', 512), ('w_out', 128), ('glu_w', 32))):
        res = adam_rs([recv[(l, 1)][pos] for l in range(DEPTH)], *wmv(name), tr, "adam_" + name)
        mixer_done.append(res[0])
        put(name, *res, P[name].shape)
    ffn = (('ffn_w_gate', [jnp.swapaxes(t, 2, 3) for t in wmv('ffn_w_gate')]),
           ('ffn_w_up', [jnp.swapaxes(t, 2, 3) for t in wmv('ffn_w_up')]), ('ffn_w_down', wmv('ffn_w_down')))
    part = [None] * 3
    for l, sub in [key for key, _ in flights[:-1] if key[1] != 1]:
        for pos, (name, ops) in enumerate(ffn):
            part[pos] = adam_block(recv[(l, sub)][pos], *ops, 2 * l + sub // 2, part[pos], f"adam_{name}_l{l}s{sub}")

    (l, sub), handle = flights[-1]
    last = scatter_wait(handle, [p[0] for p in part] + mixer_done[:2], "scatter_wait_l%ds%d" % (l, sub))
    for pos, (name, ops) in enumerate(ffn):
        part[pos] = adam_block(last[pos], *ops, 2 * l + sub // 2, part[pos], f"adam_{name}_l{l}s{sub}")
        put(name, *([jnp.swapaxes(t, 2, 3) for t in part[pos]] if pos < 2 else part[pos]), P[name].shape)
    exact_all, rounded_all = scatter_wait(small_flight, [p[0] for p in part], "gather_small_wait")
    gsum = dict(zip(SMALL_EXACT, _unpack(sum_sources(exact_all, "sum_small_exact"), [kept_shape(k) for k in SMALL_EXACT])))
    gsum.update(zip(SMALL_ROUNDED, _unpack(sum_sources(rounded_all, "sum_small_rounded"), [kept_shape(k) for k in SMALL_ROUNDED])))
    dmod_cols = jnp.stack([lax.dynamic_slice_in_dim(rounded_all, 72 * l + 9 * me, 9, axis=1).astype(F32).reshape(NDEV, 1152)
                           for l in range(DEPTH)])
    put('ada_w', *ada_bwd_adam(c_all, dmod_cols, *wmv('ada_w'), 256), P['ada_w'].shape)

    for k in ('ln_g', 'ln_b'):
        gsum[k] = lax.dynamic_slice_in_dim(gsum[k], me * 128, 128, axis=2)
    ds_, m2s, v2s = adam_native([gsum[k] for k in SMALL], *[[view(k, P[pre + k]) for k in SMALL] for pre in ('', 'm_', 'v_')],
                                "adam_small")
    for k, d, m2, v2 in zip(SMALL, ds_, m2s, v2s):
        put(k, view(k, gsum[k]), view(k, d), view(k, m2), view(k, v2), P[k].shape)

    res = [loss, grad_x]
    for pre in ('grad_', 'delta_', 'new_m_', 'new_v_'):
        res += [out[pre + k] for k in WEIGHTS]
    return tuple(res)


def kernel(x, c, rel_bias, ada_w, ada_b, ln_g, ln_b, ffn_w_gate, ffn_w_up, ffn_w_down, w_in, w_out, ssm_a_re, ssm_a_im, ssm_log_dt, ssm_b_re, ssm_b_im, ssm_c_re, ssm_c_im, ssm_d, glu_w, glu_b, pool_w, pool_scale, loss_target, m_rel_bias, m_ada_w, m_ada_b, m_ln_g, m_ln_b, m_ffn_w_gate, m_ffn_w_up, m_ffn_w_down, m_w_in, m_w_out, m_ssm_a_re, m_ssm_a_im, m_ssm_log_dt, m_ssm_b_re, m_ssm_b_im, m_ssm_c_re, m_ssm_c_im, m_ssm_d, m_glu_w, m_glu_b, m_pool_w, m_pool_scale, v_rel_bias, v_ada_w, v_ada_b, v_ln_g, v_ln_b, v_ffn_w_gate, v_ffn_w_up, v_ffn_w_down, v_w_in, v_w_out, v_ssm_a_re, v_ssm_a_im, v_ssm_log_dt, v_ssm_b_re, v_ssm_b_im, v_ssm_c_re, v_ssm_c_im, v_ssm_d, v_glu_w, v_glu_b, v_pool_w, v_pool_scale):
    return _step(dict(locals()))
```

```python
import functools
import math

import numpy as np
import jax
import jax.numpy as jnp
from jax import lax
from jax.experimental import pallas as pl
from jax.experimental.pallas import tpu as pltpu

F32 = jnp.float32
BF16 = jnp.bfloat16
MXU = jnp.bfloat16

S = 2048
D = 1024
NDEV = 8
DEPTH = 2
D_ATT, D_SSM, D_POOL, D_IN = 512, 256, 256, 2048
N_HEADS = 8
FB = 352
FBP = 384
QB = 128
PATTERNS = ((128, 1), (512, 4), (2048, 16))
POOL_WINDOWS = (2, 4, 8, 16)
N_BUCKETS, MAX_DISTANCE = 32, 2048
ALPHA = (2 * DEPTH) ** 0.25
LN_EPS = 1e-5
NEG = -1e30
GATHER_US_PER_BYTE = 43e-6
LR, B1, B2, EPS, WD, STEP = 0.001, 0.9, 0.999, 1e-08, 0.01, 10

TM = 256
TMM = 512
MIB = 1024 * 1024


def _cp(vmem_mib, sem=None):
    kw = dict(vmem_limit_bytes=vmem_mib * MIB)
    if sem is not None:
        kw["dimension_semantics"] = sem
    return pltpu.CompilerParams(**kw)


def _sds(shape, dtype):
    return jax.ShapeDtypeStruct(shape, dtype)


def _mm(a, b):
    return jnp.dot(a.astype(MXU), b.astype(MXU), preferred_element_type=F32)


def _mm_nt(a, b):
    return lax.dot_general(a.astype(MXU), b.astype(MXU), (((1,), (1,)), ((), ())), preferred_element_type=F32)


def _mm_tn(a, b):
    return lax.dot_general(a.astype(MXU), b.astype(MXU), (((0,), (0,)), ((), ())), preferred_element_type=F32)


def _ln_stats(x):
    mu = jnp.mean(x, axis=-1, keepdims=True)
    xc = x - mu
    var = jnp.mean(xc * xc, axis=-1, keepdims=True)
    rstd = lax.rsqrt(var + LN_EPS)
    return xc * rstd, rstd


def _ln_bwd(dn, n, rstd):
    return rstd * (dn - jnp.mean(dn, axis=-1, keepdims=True) - n * jnp.mean(dn * n, axis=-1, keepdims=True))


def _me():
    return 4 * lax.axis_index("x") + 2 * lax.axis_index("y") + lax.axis_index("c")


ANY = pl.BlockSpec(memory_space=pl.ANY)
PIN_BYTES = 1 << 19


def _pallas_call(*a, **k):
    pin_all = k.pop("pin_all", False)
    big = lambda o: pin_all or math.prod(o.shape) * o.dtype.itemsize >= PIN_BYTES
    pin = lambda o: pltpu.HBM(o.shape, o.dtype) if isinstance(o, jax.ShapeDtypeStruct) and big(o) else o
    osh = k["out_shape"]
    k["out_shape"] = tuple(pin(o) for o in osh) if isinstance(osh, (tuple, list)) else pin(osh)
    fn = pl.pallas_call(*a, **k)

    def run(*args):
        return fn(*[pltpu.with_memory_space_constraint(x, pltpu.HBM) if big(x) else x for x in args])
    return run


class Gather:
    def __init__(self, srcs):
        self.srcs = list(srcs)
        self.n = len(self.srcs)
        self.bufs = []
        self.out_shapes = [_sds((NDEV,) + a.shape, a.dtype) for a in self.srcs]
        self.sems = [pltpu.SemaphoreType.DMA((7 * self.n,)), pltpu.SemaphoreType.DMA((7 * self.n,)),
                     pltpu.SemaphoreType.DMA((self.n,))]

    def _parts(self, srcs, outs, sems):
        send_sems, recv_sems, loc_sems = sems
        x, y, c = lax.axis_index("x"), lax.axis_index("y"), lax.axis_index("c")
        me, sib = (x, y, c), (x, y, 1 - c)
        chips = [(1 - x, y), (x, 1 - y), (1 - x, 1 - y)]
        slot = lambda d: 4 * d[0] + 2 * d[1] + d[2]

        def copy(a, k, block, to, src=None):
            dst = outs[a].at[slot(block)]
            return pltpu.make_async_remote_copy(
                src_ref=dst if src is None else src, dst_ref=dst,
                send_sem=send_sems.at[7 * a + k], recv_sem=recv_sems.at[7 * a + k],
                device_id=to, device_id_type=pl.DeviceIdType.MESH)

        local = [pltpu.make_async_copy(srcs[a], outs[a].at[slot(me)], loc_sems.at[a]) for a in range(self.n)]
        return me, sib, chips, c, copy, local

    def start(self, srcs, bufs, outs, sems):
        me, sib, chips, c, copy, local = self._parts(srcs, outs, sems)
        for a in range(self.n):
            local[a].start()
            copy(a, 0, me, sib, src=srcs[a]).start()
            for j, chip in enumerate(chips):
                copy(a, 1 + j, me, (*chip, c), src=srcs[a]).start()

    def finish(self, srcs, bufs, outs, sems):
        me, sib, chips, c, copy, local = self._parts(srcs, outs, sems)
        for a in range(self.n):
            for j, chip in enumerate(chips):
                copy(a, 1 + j, (*chip, c), me).wait_recv()
                copy(a, 4 + j, (*chip, c), sib).start()
        for a in range(self.n):
            copy(a, 0, sib, me).wait_recv()
            copy(a, 0, me, sib, src=srcs[a]).wait_send()
            for j, chip in enumerate(chips):
                copy(a, 4 + j, (*chip, 1 - c), me).wait_recv()
                copy(a, 1 + j, me, (*chip, c), src=srcs[a]).wait_send()
                copy(a, 4 + j, (*chip, c), sib).wait_send()
            local[a].wait()


def _call(body, *, name, grid, in_specs, out_specs, out_shape, args, scratch=(), cp=None, ride=None):
    out_specs, out_shape, scratch = list(out_specs), list(out_shape), list(scratch)
    if ride is None:
        outs = _pallas_call(body, name=name, grid=grid, in_specs=list(in_specs), out_specs=tuple(out_specs),
                              out_shape=tuple(out_shape), scratch_shapes=scratch, compiler_params=cp)(*args)
        return list(outs), []
    nin, nout, nscr, n, nb, no = len(in_specs), len(out_specs), len(scratch), ride.n, len(ride.bufs), len(ride.out_shapes)
    steps = list(grid)

    def wrapped(*refs):
        h_in, r_src, r_buf = refs[:nin], refs[nin:nin + n], refs[nin + n:nin + n + nb]
        o0 = nin + n + nb
        h_out, r_out = refs[o0:o0 + nout], refs[o0 + nout:o0 + nout + no]
        s0 = o0 + nout + no
        h_scr, sems = refs[s0:s0 + nscr], refs[s0 + nscr:]
        ids = [pl.program_id(a) for a in range(len(steps))]
        first = functools.reduce(jnp.logical_and, [i == 0 for i in ids])
        last = functools.reduce(jnp.logical_and, [i == s - 1 for i, s in zip(ids, steps)])

        @pl.when(first)
        def _():
            ride.start(r_src, r_buf, r_out, sems)

        body(*h_in, *h_out, *h_scr)

        @pl.when(last)
        def _():
            ride.finish(r_src, r_buf, r_out, sems)

    aliases = {nin + n + k: nout + k for k in range(nb)}
    outs = _pallas_call(
        wrapped, name=name, grid=grid, in_specs=list(in_specs) + [ANY] * (n + nb),
        out_specs=tuple(out_specs + [ANY] * no), out_shape=tuple(out_shape + ride.out_shapes),
        scratch_shapes=scratch + ride.sems, input_output_aliases=aliases, compiler_params=cp,
    )(*args, *ride.srcs, *ride.bufs)
    return list(outs[:nout]), list(outs[nout:])


def _exchange(ride, name):
    def body(dummy_ref, o_ref):
        o_ref[...] = dummy_ref[...]

    one = pl.BlockSpec((8, 128), lambda i: (0, 0))
    _, outs = _call(body, name=name, grid=(1,), in_specs=[one], out_specs=[one], out_shape=[_sds((8, 128), F32)],
                    args=(jnp.zeros((8, 128), F32),), ride=ride)
    return outs


HBM = pl.BlockSpec(memory_space=pltpu.HBM)
SEM = pl.BlockSpec(memory_space=pltpu.SEMAPHORE)


def routes_all(me):
    return [(k, me ^ k, me, me ^ k) for k in range(NDEV)]


def _scatter_copies(srcs, lands, sems, sending, whole, rows):
    send_sems, recv_sems, loc_sems = sems
    me = _me()
    rts = routes_all(me)
    part = lambda ref, slab: ref if whole else ref.at[slab] if rows is None else ref.at[slab, pl.ds(0, rows)]
    remote_ix = [r for r, (k, _, _, _) in enumerate(rts) if k != 0]
    local_ix = [r for r, (k, _, _, _) in enumerate(rts) if k == 0]
    remote, local = [], []
    for a in range(len(srcs)):
        for n, r in enumerate(remote_ix):
            k, slab, there, here = rts[r]
            t = me ^ k
            sem = len(remote_ix) * a + n
            remote.append(pltpu.make_async_remote_copy(
                src_ref=part(srcs[a], slab), dst_ref=lands[a].at[there if sending else here], send_sem=send_sems.at[sem],
                recv_sem=recv_sems.at[sem], device_id=(t // 4, (t // 2) % 2, t % 2), device_id_type=pl.DeviceIdType.MESH))
        for n, r in enumerate(local_ix):
            _, slab, there, _ = rts[r]
            local.append(pltpu.make_async_copy(part(srcs[a], slab), lands[a].at[there], loc_sems.at[len(local_ix) * a + n]))
    return remote, local


def scatter_start(payloads, name, whole=False, rows=None, after=()):
    n = len(payloads)
    nr = NDEV - 1
    after = list(after)
    na = len(after)

    def body(*refs):
        srcs, lands, sems = refs[:n], refs[n:2 * n], refs[2 * n + na:2 * n + na + 3]
        remote, local = _scatter_copies(srcs, lands, sems, True, whole, rows)
        for cp in local + remote:
            cp.start()
        refs[-1][...] = jnp.zeros((8, 128), F32)

    thru = [pltpu.HBM(p.shape, p.dtype) for p in payloads]
    land_shapes = [(NDEV,) + p.shape if whole else p.shape if rows is None else (NDEV, rows) + p.shape[2:] for p in payloads]
    outs = pl.pallas_call(
        body, name=name,
        out_shape=(pltpu.SemaphoreType.DMA((nr * n,)), pltpu.SemaphoreType.DMA((nr * n,)), pltpu.SemaphoreType.DMA((n,)),
                   *thru, *[pltpu.HBM(sh, p.dtype) for sh, p in zip(land_shapes, payloads)], _sds((8, 128), F32)),
        in_specs=[HBM] * (2 * n + na),
        out_specs=(SEM, SEM, SEM, *[HBM] * (2 * n), pl.BlockSpec(memory_space=pltpu.VMEM)),
        input_output_aliases={i: 3 + i for i in range(2 * n)},
        compiler_params=pltpu.CompilerParams(has_side_effects=pltpu.SideEffectType.DATAFLOW_SIDE_EFFECTING),
    )(*[pltpu.with_memory_space_constraint(p, pltpu.HBM) for p in payloads],
      *[pltpu.with_memory_space_constraint(lax.empty(sh, p.dtype), pltpu.HBM) for sh, p in zip(land_shapes, payloads)],
      *[pltpu.with_memory_space_constraint(a, pltpu.HBM) for a in after])
    return (outs[:3], outs[3:3 + n], outs[3 + n:3 + 2 * n], whole, rows), outs[-1][0, 0]


def scatter_wait(handle, after, name):
    sems, srcs_thru, lands_thru, whole, rows = handle
    n = len(srcs_thru)
    after = list(after) if isinstance(after, (list, tuple)) else [after]

    def body(*refs):
        srcs, lands, sems_ = refs[:n], refs[n:2 * n], refs[2 * n:2 * n + 3]
        remote, local = _scatter_copies(srcs, lands, sems_, False, whole, rows)
        for cp in remote:
            cp.wait_send()
            cp.wait_recv()
        for cp in local:
            cp.wait()

    outs = pl.pallas_call(
        body, name=name, out_shape=tuple(pltpu.HBM(p.shape, p.dtype) for p in (*srcs_thru, *lands_thru)),
        in_specs=[HBM] * (2 * n) + [SEM] * 3 + [HBM] * len(after), out_specs=tuple([HBM] * (2 * n)),
        input_output_aliases={i: i for i in range(2 * n)},
        compiler_params=pltpu.CompilerParams(has_side_effects=pltpu.SideEffectType.DATAFLOW_SIDE_EFFECTING),
    )(*srcs_thru, *lands_thru, *sems, *[pltpu.with_memory_space_constraint(a, pltpu.HBM) for a in after])
    return list(outs[n:])


def _row_spec(cols, tm=TM):
    return pl.BlockSpec((tm, cols), lambda i: (i, 0))


def _full_spec(shape):
    nd = len(shape)
    return pl.BlockSpec(shape, lambda i: (0,) * nd)


def ln_mod_fwd(x, mod, sub, name):
    def body(x_ref, mod_ref, h_ref):
        n, _ = _ln_stats(x_ref[...])
        shift = mod_ref[3 * sub:3 * sub + 1, :]
        scale = mod_ref[3 * sub + 1:3 * sub + 2, :]
        h_ref[...] = (n * (1.0 + scale) + shift).astype(MXU)

    return _pallas_call(
        body, name=name, grid=(S // TM,),
        in_specs=[_row_spec(D), _full_spec((9, D))], out_specs=_row_spec(D),
        out_shape=_sds((S, D), MXU), compiler_params=_cp(32, ("arbitrary",)))(x, mod)


def res_ln_fwd(x, f, mod, sub, lng, lnb, w, name, nxt=None):
    def body(x_ref, f_ref, mod_ref, g_ref, b_ref, *rest):
        gate = mod_ref[3 * sub + 2:3 * sub + 3, :]
        r = ALPHA * x_ref[...] + (w * gate) * f_ref[...]
        n, _ = _ln_stats(r)
        xo = n * g_ref[sub:sub + 1, :] + b_ref[sub:sub + 1, :]
        rest[-1 if nxt is None else -2][...] = xo
        if nxt is not None:
            nmod_ref, h_ref = rest[0], rest[-1]
            n2, _ = _ln_stats(xo)
            s2 = nxt[1]
            h_ref[...] = (n2 * (1.0 + nmod_ref[3 * s2 + 1:3 * s2 + 2, :]) + nmod_ref[3 * s2:3 * s2 + 1, :]).astype(MXU)

    more = nxt is not None
    return _pallas_call(
        body, name=name, grid=(S // TM,),
        in_specs=[_row_spec(D), _row_spec(D), _full_spec((9, D)), _full_spec((3, D)), _full_spec((3, D))] + [_full_spec((9, D))] * more,
        out_specs=(_row_spec(D),) + (_row_spec(D),) * more, out_shape=(_sds((S, D), F32),) + (_sds((S, D), MXU),) * more,
        compiler_params=_cp(32, ("arbitrary",)))(x, f, mod, lng, lnb, *([nxt[0]] if more else []))


def res_ln_bwd(x, f, mod, sub, lng, dxo, w, name):
    def body(x_ref, f_ref, mod_ref, g_ref, dxo_ref, dxa_ref, df_ref, sums_ref):
        i = pl.program_id(0)
        gate = mod_ref[3 * sub + 2:3 * sub + 3, :]
        fv = f_ref[...]
        r = ALPHA * x_ref[...] + (w * gate) * fv
        n, rstd = _ln_stats(r)
        dxo = dxo_ref[...]
        dr = _ln_bwd(dxo * g_ref[sub:sub + 1, :], n, rstd)
        dxa_ref[...] = ALPHA * dr
        df_ref[...] = ((w * gate) * dr).astype(MXU)
        part = jnp.concatenate([
            jnp.sum(dxo * n, axis=0, keepdims=True),
            jnp.sum(dxo, axis=0, keepdims=True),
            jnp.sum(dr * fv, axis=0, keepdims=True) * w,
            jnp.zeros((5, D), F32)], axis=0)

        @pl.when(i == 0)
        def _():
            sums_ref[...] = part

        @pl.when(i > 0)
        def _():
            sums_ref[...] += part

    return _call(
        body, name=name, grid=(S // TM,),
        in_specs=[_row_spec(D), _row_spec(D), _full_spec((9, D)), _full_spec((3, D)), _row_spec(D)],
        out_specs=(_row_spec(D), _row_spec(D), _full_spec((8, D))),
        out_shape=(_sds((S, D), F32), _sds((S, D), MXU), _sds((8, D), F32)),
        cp=_cp(32, ("arbitrary",)), args=(x, f, mod, lng, dxo))[0]


def ln_mod_bwd(x, dh, mod, sub, dxa, name):
    def body(x_ref, dh_ref, mod_ref, dxa_ref, dx_ref, sums_ref):
        i = pl.program_id(0)
        scale = mod_ref[3 * sub + 1:3 * sub + 2, :]
        n, rstd = _ln_stats(x_ref[...])
        dh = dh_ref[...]
        dx_ref[...] = dxa_ref[...] + _ln_bwd(dh * (1.0 + scale), n, rstd)
        part = jnp.concatenate([
            jnp.sum(dh, axis=0, keepdims=True),
            jnp.sum(dh * n, axis=0, keepdims=True),
            jnp.zeros((6, D), F32)], axis=0)

        @pl.when(i == 0)
        def _():
            sums_ref[...] = part

        @pl.when(i > 0)
        def _():
            sums_ref[...] += part

    return _call(
        body, name=name, grid=(S // TM,),
        in_specs=[_row_spec(D), _row_spec(D), _full_spec((9, D)), _row_spec(D)],
        out_specs=(_row_spec(D), _full_spec((8, D))),
        out_shape=(_sds((S, D), F32), _sds((8, D), F32)),
        cp=_cp(32, ("arbitrary",)), args=(x, dh, mod, dxa))[0]


def ln_join_bwd(xp, fp, modp, subp, lngp, lnbp, wp, dh, mod, sub, dxa, name):
    def body(xp_ref, fp_ref, modp_ref, g_ref, b_ref, dh_ref, mod_ref, dxa_ref, dxap_ref, dfp_ref, sumsp_ref, sums_ref):
        i = pl.program_id(0)
        gate = modp_ref[3 * subp + 2:3 * subp + 3, :]
        fv = fp_ref[...]
        n, rstd = _ln_stats(ALPHA * xp_ref[...] + (wp * gate) * fv)
        gain = g_ref[subp:subp + 1, :]
        n2, rstd2 = _ln_stats(n * gain + b_ref[subp:subp + 1, :])
        dh = dh_ref[...]
        dx = dxa_ref[...] + _ln_bwd(dh * (1.0 + mod_ref[3 * sub + 1:3 * sub + 2, :]), n2, rstd2)
        dr = _ln_bwd(dx * gain, n, rstd)
        dxap_ref[...] = ALPHA * dr
        dfp_ref[...] = ((wp * gate) * dr).astype(MXU)
        partp = jnp.concatenate([
            jnp.sum(dx * n, axis=0, keepdims=True), jnp.sum(dx, axis=0, keepdims=True),
            jnp.sum(dr * fv, axis=0, keepdims=True) * wp, jnp.zeros((5, D), F32)], axis=0)
        part = jnp.concatenate([
            jnp.sum(dh, axis=0, keepdims=True), jnp.sum(dh * n2, axis=0, keepdims=True), jnp.zeros((6, D), F32)], axis=0)

        @pl.when(i == 0)
        def _():
            sumsp_ref[...] = partp
            sums_ref[...] = part

        @pl.when(i > 0)
        def _():
            sumsp_ref[...] += partp
            sums_ref[...] += part

    return _pallas_call(
        body, name=name, grid=(S // TM,),
        in_specs=[_row_spec(D), _row_spec(D), _full_spec((9, D)), _full_spec((3, D)), _full_spec((3, D)), _row_spec(D),
                  _full_spec((9, D)), _row_spec(D)],
        out_specs=(_row_spec(D), _row_spec(D), _full_spec((8, D)), _full_spec((8, D))),
        out_shape=(_sds((S, D), F32), _sds((S, D), MXU), _sds((8, D), F32), _sds((8, D), F32)),
        compiler_params=_cp(40, ("arbitrary",)))(xp, fp, modp, lngp, lnbp, dh, mod, dxa)


def loss_fwd_bwd(y, target, name):
    def body(y_ref, t_ref, l_ref, dy_ref):
        i = pl.program_id(0)
        e = y_ref[...] - t_ref[...]
        dy_ref[...] = e * (1.0 / D)
        part = jnp.zeros((8, 128), F32) + (0.5 / D) * jnp.sum(e * e)

        @pl.when(i == 0)
        def _():
            l_ref[...] = part

        @pl.when(i > 0)
        def _():
            l_ref[...] += part

    return _pallas_call(
        body, name=name, grid=(S // TM,),
        in_specs=[_row_spec(D), _row_spec(D)], out_specs=(_full_spec((8, 128)), _row_spec(D)),
        out_shape=(_sds((8, 128), F32), _sds((S, D), F32)),
        compiler_params=_cp(32, ("arbitrary",)))(y, target)


HB = 2 * FBP
NHB = NDEV * FBP // HB
TMB = 1024


def _wrows(buffers=2):
    return pl.BlockSpec((HB, D), lambda j, i: (j, 0), pipeline_mode=pl.Buffered(buffers))


def _resident(shape):
    return pl.BlockSpec(shape, lambda j, i: (0, 0), pipeline_mode=pl.Buffered(1))


def ffn_fwd(h, wgt, wut, wd, name, ride=None):
    def body(h_ref, wg_ref, wu_ref, wd_ref, g_ref, u_ref, f_ref):
        j, i = pl.program_id(0), pl.program_id(1)
        hv = h_ref[...]
        g = _mm_nt(hv, wg_ref[...])
        u = _mm_nt(hv, wu_ref[...])
        g_ref[...] = g.astype(MXU)
        u_ref[...] = u.astype(MXU)
        a = g * jax.nn.sigmoid(g) * u
        part = _mm(a, wd_ref[...])
        rows = pl.ds(pl.multiple_of(i * TMB, TMB), TMB)

        @pl.when(j == 0)
        def _():
            f_ref[rows, :] = part

        @pl.when(j > 0)
        def _():
            f_ref[rows, :] += part

    gu = pl.BlockSpec((TMB, HB), lambda j, i: (i, j))
    return _call(
        body, name=name, grid=(NHB, S // TMB),
        in_specs=[pl.BlockSpec((TMB, D), lambda j, i: (i, 0)), _wrows(), _wrows(), _wrows()],
        out_specs=(gu, gu, _resident((S, D))),
        out_shape=(_sds((S, NDEV * FBP), MXU), _sds((S, NDEV * FBP), MXU), _sds((S, D), F32)),
        cp=_cp(52, ("arbitrary", "arbitrary")), args=(h, wgt, wut, wd), ride=ride)


def ffn_bwd(df, h, g, u, wgt, wut, wd, name):
    ni = S // TMB

    def body(df_ref, h_ref, g_ref, u_ref, wg_ref, wu_ref, wd_ref, dwg_ref, dwu_ref, dwd_ref, dh_ref,
             ag_ref, au_ref, ad_ref):
        j, i = pl.program_id(0), pl.program_id(1)
        dfv, hv = df_ref[...], h_ref[...]
        gv, uv = g_ref[...].astype(F32), u_ref[...].astype(F32)
        da = _mm_nt(dfv, wd_ref[...])
        sg = jax.nn.sigmoid(gv)
        silu = gv * sg
        du = da * silu
        dg = da * uv * (sg * (1.0 + gv * (1.0 - sg)))
        p_d = _mm_tn(silu * uv, dfv)
        p_g = _mm_tn(dg, hv)
        p_u = _mm_tn(du, hv)

        @pl.when(i == 0)
        def _():
            ad_ref[...] = p_d
            ag_ref[...] = p_g
            au_ref[...] = p_u

        @pl.when(i > 0)
        def _():
            ad_ref[...] += p_d
            ag_ref[...] += p_g
            au_ref[...] += p_u

        @pl.when(i == ni - 1)
        def _():
            dwd_ref[...] = ad_ref[...].astype(BF16)
            dwg_ref[...] = ag_ref[...].astype(BF16)
            dwu_ref[...] = au_ref[...].astype(BF16)

        part = _mm(dg, wg_ref[...]) + _mm(du, wu_ref[...])
        rows = pl.ds(pl.multiple_of(i * TMB, TMB), TMB)

        @pl.when(j == 0)
        def _():
            dh_ref[rows, :] = part

        @pl.when(j > 0)
        def _():
            dh_ref[rows, :] += part

    gu = pl.BlockSpec((TMB, HB), lambda j, i: (i, j))
    rowt = pl.BlockSpec((TMB, D), lambda j, i: (i, 0))
    return _call(
        body, name=name, grid=(NHB, ni),
        in_specs=[rowt, rowt, gu, gu, _wrows(1), _wrows(1), _wrows(1)],
        out_specs=(_wrows(1), _wrows(1), _wrows(1), _resident((S, D))),
        out_shape=(_sds((NDEV * FBP, D), BF16), _sds((NDEV * FBP, D), BF16), _sds((NDEV * FBP, D), BF16), _sds((S, D), F32)),
        scratch=[pltpu.VMEM((HB, D), F32), pltpu.VMEM((HB, D), F32), pltpu.VMEM((HB, D), F32)],
        cp=_cp(60, ("arbitrary", "arbitrary")), args=(df, h, g, u, wgt, wut, wd))[0]


def win_fwd(h, win, name, ride=None):
    def body(h_ref, w_ref, z_ref):
        hv = h_ref[...]
        for j in range(NDEV):
            z_ref[:, 256 * j:256 * (j + 1)] = _mm(hv, w_ref[j])

    return _call(
        body, name=name, grid=(S // TMM,),
        in_specs=[_row_spec(D, TMM), _full_spec((NDEV, D, 256))],
        out_specs=[_row_spec(D_IN, TMM)], out_shape=[_sds((S, D_IN), F32)],
        cp=_cp(40, ("arbitrary",)), args=(h, win), ride=ride)


def win_bwd(dparts, h, win, name):
    ni = S // TMM

    def body(dq_ref, dk_ref, dv_ref, dus_ref, dup_ref, h_ref, w_ref, dh_ref, dw_ref, acc_ref):
        i = pl.program_id(0)
        hv = h_ref[...]
        cols = [dq_ref[:, 0:256], dq_ref[:, 256:512], dk_ref[:, 0:256], dk_ref[:, 256:512],
                dv_ref[:, 0:256], dv_ref[:, 256:512], dus_ref[...], dup_ref[...]]
        dh = jnp.zeros((TMM, D), F32)
        for j in range(NDEV):
            dz = cols[j].astype(MXU)
            dh = dh + _mm_nt(dz, w_ref[j])
            p = _mm_tn(hv, dz)

            @pl.when(i == 0)
            def _():
                acc_ref[j] = p

            @pl.when(i > 0)
            def _():
                acc_ref[j] += p

        dh_ref[...] = dh

        @pl.when(i == ni - 1)
        def _():
            dw_ref[...] = acc_ref[...].astype(BF16)

    return _call(
        body, name=name, grid=(ni,),
        in_specs=[_row_spec(512, TMM), _row_spec(512, TMM), _row_spec(512, TMM), _row_spec(256, TMM), _row_spec(256, TMM),
                  _row_spec(D, TMM), _full_spec((NDEV, D, 256))],
        out_specs=(_row_spec(D, TMM), _full_spec((NDEV, D, 256))),
        out_shape=(_sds((S, D), F32), _sds((NDEV, D, 256), BF16)),
        scratch=[pltpu.VMEM((NDEV, D, 256), F32)],
        cp=_cp(48, ("arbitrary",)), args=(*dparts, h, win))[0]


def wout_fwd(ya, ys, yp, wout, name, ride=None):
    def body(ya_ref, ys_ref, yp_ref, w_ref, o_ref):
        w = w_ref[...].reshape(D, D)
        o_ref[...] = _mm(ya_ref[...], w[0:512]) + _mm(ys_ref[...], w[512:768]) + _mm(yp_ref[...], w[768:1024])

    return _call(
        body, name=name, grid=(S // TMM,),
        in_specs=[_row_spec(512, TMM), _row_spec(256, TMM), _row_spec(256, TMM), _full_spec((NDEV, 128, D))],
        out_specs=[_row_spec(D, TMM)], out_shape=[_sds((S, D), F32)],
        cp=_cp(40, ("arbitrary",)), args=(ya, ys, yp, wout), ride=ride)


def wout_bwd(do, ya, ys, yp, wout, name):
    ni = S // TMM

    def body(do_ref, ya_ref, ys_ref, yp_ref, w_ref, dya_ref, dys_ref, dyp_ref, dw_ref, acc_ref):
        i = pl.program_id(0)
        w = w_ref[...].reshape(D, D)
        dov = do_ref[...]
        dya_ref[...] = _mm_nt(dov, w[0:512])
        dys_ref[...] = _mm_nt(dov, w[512:768])
        dyp_ref[...] = _mm_nt(dov, w[768:1024])
        parts = [(0, 512, _mm_tn(ya_ref[...], dov)), (512, 768, _mm_tn(ys_ref[...], dov)),
                 (768, 1024, _mm_tn(yp_ref[...], dov))]
        for lo, hi, p in parts:
            @pl.when(i == 0)
            def _():
                acc_ref[lo:hi, :] = p

            @pl.when(i > 0)
            def _():
                acc_ref[lo:hi, :] += p

        @pl.when(i == ni - 1)
        def _():
            dw_ref[...] = acc_ref[...].astype(BF16).reshape(NDEV, 128, D)

    return _call(
        body, name=name, grid=(ni,),
        in_specs=[_row_spec(D, TMM), _row_spec(512, TMM), _row_spec(256, TMM), _row_spec(256, TMM),
                  _full_spec((NDEV, 128, D))],
        out_specs=(_row_spec(512, TMM), _row_spec(256, TMM), _row_spec(256, TMM), _full_spec((NDEV, 128, D))),
        out_shape=(_sds((S, 512), F32), _sds((S, 256), F32), _sds((S, 256), F32), _sds((NDEV, 128, D), BF16)),
        scratch=[pltpu.VMEM((D, D), F32)],
        cp=_cp(40, ("arbitrary",)), args=(do, ya, ys, yp, wout))[0]


def _t5_bucket(dist):
    max_exact = N_BUCKETS // 2
    d = np.maximum(dist, 1).astype(np.float32)
    large = max_exact + (np.log(d / max_exact) / math.log(MAX_DISTANCE / max_exact)
                         * (N_BUCKETS - max_exact)).astype(np.int32)
    large = np.minimum(large, N_BUCKETS - 1)
    return np.where(dist < max_exact, dist, large).astype(np.int32)


def _att_static():
    i = np.arange(QB)[:, None]
    j = np.arange(2 * QB)[None, :]
    r = i + QB - j
    buckets, bands = [], []
    for window, dil in PATTERNS:
        bands.append((r >= 0) & (r <= window // dil))
        buckets.append(_t5_bucket(np.clip(r, 0, None) * dil))
    return np.stack(buckets), np.stack(bands), np.broadcast_to(j >= QB, (QB, 2 * QB))


def att_bias(rel_bias):
    m = np.arange(2 * QB)
    rows = []
    for window, dil in PATTERNS:
        r = QB - m
        ok = (r >= 0) & (r <= window // dil)
        b = rel_bias[_t5_bucket(np.clip(r, 0, None) * dil)]
        rows.append(jnp.where(ok[:, None], b, NEG).T)
    return jnp.broadcast_to(jnp.stack(rows)[:, :, None, :], (3, N_HEADS, 8, 2 * QB))


def _bias_tiles(t_ref, tiles):
    col = lax.broadcasted_iota(jnp.int32, (QB, 2 * QB), 1)
    for p in range(3):
        for hh in range(2):
            t = pltpu.roll(jnp.broadcast_to(t_ref[p, hh, 0:1, :], (QB, 2 * QB)), 0, 1, stride=1, stride_axis=0)
            tiles[p, hh, 0] = t
            tiles[p, hh, 1] = jnp.where(col >= QB, t, NEG)


def _permute_in(dst_ref, src_ref, d, scale=None, pad=QB):
    L = S // d
    for r in range(d):
        v = src_ref[pl.ds(r, L, stride=d), :] if d > 1 else src_ref[...]
        if scale is not None:
            v = v * scale
        dst_ref[pad + r * L:pad + (r + 1) * L, :] = v.astype(dst_ref.dtype)


def att_fwd(z, bias, name, ride=None):
    def body(q_ref, k_ref, v_ref, t_ref, y_ref, l_ref, qs, ks, vs, o_perm, l_perm, o_nat, l_nat, b_ref):
        _bias_tiles(t_ref, b_ref)
        zero_pad = jnp.zeros((QB, 128), MXU)
        ks[0:QB, :] = zero_pad
        vs[0:QB, :] = zero_pad
        lane = lax.broadcasted_iota(jnp.int32, (QB, 128), 1)
        for p, (_, d) in enumerate(PATTERNS):
            L = S // d
            nb = L // QB
            _permute_in(qs, q_ref, d, scale=0.125, pad=0)
            _permute_in(ks, k_ref, d)
            _permute_in(vs, v_ref, d)

            def blk(b, carry):
                r0 = pl.multiple_of(b * QB, QB)
                q = qs[pl.ds(r0, QB), :]
                kb = ks[pl.ds(r0, 2 * QB), :]
                vb = vs[pl.ds(r0, 2 * QB), :]
                first = ((b % nb) == 0).astype(jnp.int32)
                res = []
                for hh in range(2):
                    sel = (lane < 64) if hh == 0 else (lane >= 64)
                    qm = jnp.where(sel, q, jnp.zeros_like(q))
                    s = _mm_nt(qm, kb) + b_ref[p, hh, first]
                    m = jnp.max(s, axis=1, keepdims=True)
                    pe = jnp.exp(s - m)
                    den = jnp.sum(pe, axis=1, keepdims=True)
                    res.append((_mm(pe, vb) / den, m + jnp.log(den)))
                o_perm[pl.ds(r0, QB), :] = jnp.where(lane < 64, res[0][0], res[1][0])
                l_perm[pl.ds(r0, QB), :] = jnp.where(lane < 64, res[0][1], res[1][1])
                return carry

            lax.fori_loop(0, S // QB, blk, 0, unroll=8)
            for r in range(d):
                if d > 1:
                    o_nat[p, pl.ds(r, L, stride=d), :] = o_perm[r * L:(r + 1) * L, :]
                    l_nat[p, pl.ds(r, L, stride=d), :] = l_perm[r * L:(r + 1) * L, :]
                else:
                    o_nat[p] = o_perm[...]
                    l_nat[p] = l_perm[...]
        l0, l1, l2 = l_nat[0], l_nat[1], l_nat[2]
        m = jnp.maximum(jnp.maximum(l0, l1), l2)
        e0, e1, e2 = jnp.exp(l0 - m), jnp.exp(l1 - m), jnp.exp(l2 - m)
        den = e0 + e1 + e2
        y_ref[...] = (e0 * o_nat[0] + e1 * o_nat[1] + e2 * o_nat[2]) / den
        l_ref[...] = m + jnp.log(den)

    col = lambda c0: pl.BlockSpec((S, 128), lambda hp: (0, c0 + hp))
    return _call(
        body, name=name, grid=(N_HEADS // 2,),
        in_specs=[col(0), col(4), col(8), pl.BlockSpec((3, 2, 8, 2 * QB), lambda hp: (0, hp, 0, 0))],
        out_specs=(col(0), col(0)),
        out_shape=(_sds((S, D_ATT), F32), _sds((S, D_ATT), F32)),
        scratch=[pltpu.VMEM((S, 128), MXU), pltpu.VMEM((S + QB, 128), MXU), pltpu.VMEM((S + QB, 128), MXU),
                 pltpu.VMEM((S, 128), F32), pltpu.VMEM((S, 128), F32),
                 pltpu.VMEM((3, S, 128), F32), pltpu.VMEM((3, S, 128), F32),
                 pltpu.VMEM((3, 2, 2, QB, 2 * QB), F32)],
        cp=_cp(40, ("arbitrary",)), args=(z, z, z, bias), ride=ride)


def att_bwd(z, bias, y, lse, dy, name):
    def body(q_ref, k_ref, v_ref, t_ref, y_ref, l_ref, dy_ref, dq_ref, dk_ref, dv_ref, db_ref,
             qs, ks, vs, dys, ls, dds, dn_nat, dq_perm, dk_perm, dv_perm, b_ref):
        _bias_tiles(t_ref, b_ref)
        zero_pad = jnp.zeros((QB, 128), MXU)
        ks[0:QB, :] = zero_pad
        vs[0:QB, :] = zero_pad
        lane = lax.broadcasted_iota(jnp.int32, (QB, 128), 1)
        lane_s = lax.broadcasted_iota(jnp.int32, (S, 128), 1)
        t = dy_ref[...] * y_ref[...]
        sa = jnp.sum(jnp.where(lane_s < 64, t, 0.0), axis=1, keepdims=True)
        sb = jnp.sum(jnp.where(lane_s >= 64, t, 0.0), axis=1, keepdims=True)
        dn_nat[...] = jnp.where(lane_s < 64, sa, sb)
        dq_ref[...] = jnp.zeros((S, 128), F32)
        dk_ref[...] = jnp.zeros((S, 128), F32)
        dv_ref[...] = jnp.zeros((S, 128), F32)
        db_ref[...] = jnp.zeros((3, 2, QB, 2 * QB), F32)
        for p, (_, d) in enumerate(PATTERNS):
            L = S // d
            nb = L // QB
            _permute_in(qs, q_ref, d, scale=0.125, pad=0)
            _permute_in(ks, k_ref, d)
            _permute_in(vs, v_ref, d)
            _permute_in(dys, dy_ref, d, pad=0)
            _permute_in(ls, l_ref, d, pad=0)
            _permute_in(dds, dn_nat, d, pad=0)
            dk_perm[...] = jnp.zeros((S + QB, 128), F32)
            dv_perm[...] = jnp.zeros((S + QB, 128), F32)

            def blk(b, carry):
                r0 = pl.multiple_of(b * QB, QB)
                q = qs[pl.ds(r0, QB), :]
                kb = ks[pl.ds(r0, 2 * QB), :]
                vb = vs[pl.ds(r0, 2 * QB), :]
                dyb = dys[pl.ds(r0, QB), :]
                lb = ls[pl.ds(r0, QB), :]
                db = dds[pl.ds(r0, QB), :]
                first = ((b % nb) == 0).astype(jnp.int32)
                lane2 = jnp.concatenate([lane, lane], axis=0)
                own = (lane2 >> 6) == (lax.broadcasted_iota(jnp.int32, (2 * QB, 128), 0) >> 7)
                qm = jnp.where(own, jnp.concatenate([q, q], axis=0), jnp.zeros((2 * QB, 128), q.dtype))
                dym = jnp.where(own, jnp.concatenate([dyb, dyb], axis=0), jnp.zeros((2 * QB, 128), dyb.dtype))
                wide = lambda t: jnp.concatenate([jnp.broadcast_to(t[:, 0:1], (QB, 2 * QB)), jnp.broadcast_to(t[:, 64:65], (QB, 2 * QB))], axis=0)
                lse2, dd2 = wide(lb), wide(db)
                bias2 = jnp.concatenate([b_ref[p, 0, first], b_ref[p, 1, first]], axis=0)
                pr = jnp.exp(_mm_nt(qm, kb) + bias2 - lse2)
                ds = pr * (_mm_nt(dym, vb) - dd2)
                db_ref[p, 0] += ds[0:QB]
                db_ref[p, 1] += ds[QB:2 * QB]
                dq2 = _mm(ds, kb)
                dqs = [dq2[0:QB], dq2[QB:2 * QB]]
                dkb = _mm_tn(ds, qm)
                dvb = _mm_tn(pr, dym)
                dq_perm[pl.ds(r0, QB), :] = jnp.where(lane < 64, dqs[0], dqs[1])
                dk_perm[pl.ds(r0, 2 * QB), :] += dkb
                dv_perm[pl.ds(r0, 2 * QB), :] += dvb
                return carry

            lax.fori_loop(0, S // QB, blk, 0, unroll=4)
            for r in range(d):
                idx = pl.ds(r, L, stride=d) if d > 1 else pl.ds(0, S)
                dq_ref[idx, :] += dq_perm[r * L:(r + 1) * L, :] * 0.125
                dk_ref[idx, :] += dk_perm[QB + r * L:QB + (r + 1) * L, :]
                dv_ref[idx, :] += dv_perm[QB + r * L:QB + (r + 1) * L, :]

    col = lambda c0: pl.BlockSpec((S, 128), lambda hp: (0, c0 + hp))
    bspec = pl.BlockSpec((3, 2, 8, 2 * QB), lambda hp: (0, hp, 0, 0))
    return _call(
        body, name=name, grid=(N_HEADS // 2,),
        in_specs=[col(0), col(4), col(8), bspec, col(0), col(0), col(0)],
        out_specs=(col(0), col(0), col(0), pl.BlockSpec((3, 2, QB, 2 * QB), lambda hp: (0, hp, 0, 0))),
        out_shape=(_sds((S, D_ATT), F32), _sds((S, D_ATT), F32), _sds((S, D_ATT), F32),
                   _sds((3, N_HEADS, QB, 2 * QB), F32)),
        scratch=[pltpu.VMEM((S, 128), MXU), pltpu.VMEM((S + QB, 128), MXU), pltpu.VMEM((S + QB, 128), MXU),
                 pltpu.VMEM((S, 128), MXU), pltpu.VMEM((S, 128), F32), pltpu.VMEM((S, 128), F32),
                 pltpu.VMEM((S, 128), F32), pltpu.VMEM((S, 128), F32),
                 pltpu.VMEM((S + QB, 128), F32), pltpu.VMEM((S + QB, 128), F32),
                 pltpu.VMEM((3, 2, 2, QB, 2 * QB), F32)],
        cp=_cp(48, ("arbitrary",)), args=(z, z, z, bias, y, lse, dy))[0]


def relbias_grad(dbiases):
    bucket, band, _ = _att_static()
    onehot = (bucket[:, None] == np.arange(N_BUCKETS)[None, :, None, None]) & band[:, None]
    onehot = jnp.asarray(onehot.reshape(3, N_BUCKETS, QB * 2 * QB), BF16)

    def body(db0_ref, db1_ref, oh_ref, o_ref):
        acc = jnp.zeros((N_HEADS, N_BUCKETS), F32)
        for p in range(3):
            acc = acc + lax.dot_general(db0_ref[p] + db1_ref[p], oh_ref[p].astype(F32), (((1,), (1,)), ((), ())),
                                        preferred_element_type=F32, precision=lax.Precision.HIGHEST)
        o_ref[...] = acc

    vm = pl.BlockSpec(memory_space=pltpu.VMEM)
    out = _pallas_call(body, name="relbias_grad", in_specs=[vm, vm, vm], out_specs=vm,
                         out_shape=_sds((N_HEADS, N_BUCKETS), F32), compiler_params=_cp(40))(
        *[d.reshape(3, N_HEADS, QB * 2 * QB) for d in dbiases], onehot)
    return out.T


def _panel(t_ref, ri, j):
    return t_ref[ri, pl.ds(j, S, stride=8), :]


def _gelu(x):
    c = math.sqrt(2.0 / math.pi)
    th = jnp.tanh(c * (x + 0.044715 * x * x * x))
    return 0.5 * x * (1.0 + th), th


def ssm_fwd(z, a, bre, bim, cre, cim, dsk, gluw, glub, name, ride=None):
    def body(u_ref, a_ref, bre_ref, bim_ref, cre_ref, cim_ref, d_ref, gw_ref, gb_ref, y_ref, yp_ref, st_hbm, st_ref):
        u = u_ref[...]
        for j in range(8):
            st_ref[0, pl.ds(j, S, stride=8), :] = _mm(u, bre_ref[:, 128 * j:128 * (j + 1)])
            st_ref[1, pl.ds(j, S, stride=8), :] = _mm(u, bim_ref[:, 128 * j:128 * (j + 1)])
        ar, ai = a_ref[0], a_ref[1]

        def step(t, c):
            re, im = c
            i = pl.multiple_of(t * 8, 8)
            nre = ar * re - ai * im + st_ref[0, pl.ds(i, 8), :]
            nim = ar * im + ai * re + st_ref[1, pl.ds(i, 8), :]
            st_ref[0, pl.ds(i, 8), :] = nre
            st_ref[1, pl.ds(i, 8), :] = nim
            return nre, nim

        zero = jnp.zeros((8, 128), F32)
        lax.fori_loop(0, S, step, (zero, zero), unroll=8)
        y = d_ref[...] * u
        for j in range(8):
            y = y + _mm(_panel(st_ref, 0, j), cre_ref[128 * j:128 * (j + 1), :])
            y = y - _mm(_panel(st_ref, 1, j), cim_ref[128 * j:128 * (j + 1), :])
        pltpu.sync_copy(st_ref, st_hbm)
        yp_ref[...] = y
        gl, _ = _gelu(y)
        tt = _mm(gl, gw_ref[...].reshape(D_SSM, D_SSM)) + gb_ref[...]
        y_ref[...] = y * jax.nn.sigmoid(tt)

    vm = lambda shape: pl.BlockSpec(shape, lambda i: (0,) * len(shape))
    return _call(
        body, name=name, grid=(1,),
        in_specs=[pl.BlockSpec((S, 256), lambda i: (0, 6)), vm((2, 8, 128)), vm((256, 1024)), vm((256, 1024)),
                  vm((1024, 256)), vm((1024, 256)), vm((1, 256)),
                  vm((NDEV, 32, 256)), vm((1, 256))],
        out_specs=(vm((S, 256)), vm((S, 256)), pl.BlockSpec(memory_space=pl.ANY)),
        out_shape=(_sds((S, 256), F32), _sds((S, 256), F32), _sds((2, S * 8, 128), F32)),
        scratch=[pltpu.VMEM((2, S * 8, 128), F32)],
        cp=_cp(40, ("arbitrary",)), args=(z, a, bre, bim, cre, cim, dsk, gluw, glub), ride=ride)


def ssm_bwd(dy, z, ypre, st, a, bre, bim, cre, cim, dsk, gluw, glub, name):
    def body(dy_ref, u_ref, yp_ref, st_hbm, a_ref, bre_ref, bim_ref, cre_ref, cim_ref, d_ref, gw_ref, gb_ref,
             du_ref, dbre_ref, dbim_ref, dcre_ref, dcim_ref, da_ref, dd_ref, dgw_ref, dgb_ref, g_ref, st_ref):
        pltpu.sync_copy(st_hbm, st_ref)
        u = u_ref[...]
        y = yp_ref[...]
        dout = dy_ref[...]
        gw = gw_ref[...].reshape(D_SSM, D_SSM)
        gl, th = _gelu(y)
        sig = jax.nn.sigmoid(_mm(gl, gw) + gb_ref[...])
        dt = dout * y * sig * (1.0 - sig)
        dgw_ref[...] = _mm_tn(gl, dt)
        dgb_ref[...] = jnp.sum(dt, axis=0, keepdims=True)
        c = math.sqrt(2.0 / math.pi)
        dgelu = 0.5 * (1.0 + th) + 0.5 * y * (1.0 - th * th) * c * (1.0 + 3.0 * 0.044715 * y * y)
        dyv = dout * sig + _mm_nt(dt, gw) * dgelu
        dd_ref[...] = jnp.sum(dyv * u, axis=0, keepdims=True)
        for j in range(8):
            rows = slice(128 * j, 128 * (j + 1))
            g_ref[0, pl.ds(j, S, stride=8), :] = _mm_nt(dyv, cre_ref[rows, :])
            g_ref[1, pl.ds(j, S, stride=8), :] = -_mm_nt(dyv, cim_ref[rows, :])
            dcre_ref[rows, :] = _mm_tn(_panel(st_ref, 0, j), dyv)
            dcim_ref[rows, :] = -_mm_tn(_panel(st_ref, 1, j), dyv)
        ar, ai = a_ref[0], a_ref[1]

        def step(k, c4):
            gre, gim, dar, dai = c4
            i = pl.multiple_of((S - 1 - k) * 8, 8)
            nre = g_ref[0, pl.ds(i, 8), :] + ar * gre + ai * gim
            nim = g_ref[1, pl.ds(i, 8), :] + ar * gim - ai * gre
            g_ref[0, pl.ds(i, 8), :] = nre
            g_ref[1, pl.ds(i, 8), :] = nim
            sre = st_ref[0, pl.ds(i - 8, 8), :]
            sim = st_ref[1, pl.ds(i - 8, 8), :]
            return nre, nim, dar + nre * sre + nim * sim, dai + nim * sre - nre * sim

        zero = jnp.zeros((8, 128), F32)
        gre, gim, dar, dai = lax.fori_loop(0, S - 1, step, (zero, zero, zero, zero), unroll=8)
        g_ref[0, 0:8, :] = g_ref[0, 0:8, :] + ar * gre + ai * gim
        g_ref[1, 0:8, :] = g_ref[1, 0:8, :] + ar * gim - ai * gre
        da_ref[0] = dar
        da_ref[1] = dai
        du = dyv * d_ref[...]
        for j in range(8):
            cols = slice(128 * j, 128 * (j + 1))
            gr, gi = _panel(g_ref, 0, j), _panel(g_ref, 1, j)
            dbre_ref[:, cols] = _mm_tn(u, gr)
            dbim_ref[:, cols] = _mm_tn(u, gi)
            du = du + _mm_nt(gr, bre_ref[:, cols]) + _mm_nt(gi, bim_ref[:, cols])
        du_ref[...] = du

    vm = lambda shape: pl.BlockSpec(shape, lambda i: (0,) * len(shape))
    return _call(
        body, name=name, grid=(1,),
        in_specs=[vm((S, 256)), pl.BlockSpec((S, 256), lambda i: (0, 6)), vm((S, 256)), pl.BlockSpec(memory_space=pl.ANY),
                  vm((2, 8, 128)), vm((256, 1024)), vm((256, 1024)), vm((1024, 256)), vm((1024, 256)), vm((1, 256)),
                  vm((NDEV, 32, 256)), vm((1, 256))],
        out_specs=(vm((S, 256)), vm((256, 1024)), vm((256, 1024)), vm((1024, 256)), vm((1024, 256)),
                   vm((2, 8, 128)), vm((1, 256)), vm((256, 256)), vm((1, 256))),
        out_shape=(_sds((S, 256), F32), _sds((256, 1024), F32), _sds((256, 1024), F32), _sds((1024, 256), F32),
                   _sds((1024, 256), F32), _sds((2, 8, 128), F32), _sds((1, 256), F32), _sds((256, 256), F32),
                   _sds((1, 256), F32)),
        scratch=[pltpu.VMEM((2, S * 8, 128), F32), pltpu.VMEM((2, S * 8, 128), F32)],
        cp=_cp(56, ("arbitrary",)), args=(dy, z, ypre, st, a, bre, bim, cre, cim, dsk, gluw, glub))[0]


def _ssm_discretise(a_re, a_im, log_dt, b_re, b_im):
    dt = jnp.exp(log_dt)[:, None]
    er = jnp.exp(a_re * dt)
    abr, abi = er * jnp.cos(a_im * dt), er * jnp.sin(a_im * dt)
    den = a_re * a_re + a_im * a_im
    fr = ((abr - 1.0) * a_re + abi * a_im) / den
    fi = (abi * a_re - (abr - 1.0) * a_im) / den
    bbr = fr[:, :, None] * b_re - fi[:, :, None] * b_im
    bbi = fr[:, :, None] * b_im + fi[:, :, None] * b_re
    return abr, abi, bbr, bbi


def _blockdiag(t):
    g, r, c = t.shape
    eye = jnp.eye(g, dtype=t.dtype)
    return (t[:, :, None, :] * eye[:, None, :, None]).reshape(g * r, g * c)


def _blockdiag_take(m, r, c):
    g = m.shape[0] // r
    idx = jnp.arange(g)
    return m.reshape(g, r, g, c)[idx, :, idx, :]


PAD = 16


def _pool_lane_select(vals):
    lane = lax.broadcasted_iota(jnp.int32, vals[0].shape, 1)
    out = vals[3]
    for g in (2, 1, 0):
        out = jnp.where(lane < 64 * (g + 1), vals[g], out)
    return out


def _pool_counts():
    row = lax.broadcasted_iota(jnp.int32, (S, D_POOL), 0).astype(F32) + 1.0
    return _pool_lane_select([jnp.minimum(row, float(w)) for w in POOL_WINDOWS])


def _pooled(u, sa, sb):
    sums = []
    cur = u
    bufs = (sa, sb)
    for k, sh in enumerate((1, 2, 4, 8)):
        buf = bufs[k % 2]
        buf[PAD:PAD + S, :] = cur
        cur = cur + buf[PAD - sh:PAD - sh + S, :]
        sums.append(cur)
    return _pool_lane_select(sums) / _pool_counts() - u


def pool_fwd(z, pw, psc, name):
    def body(u_ref, w_ref, s_ref, y_ref, sa, sb):
        for buf in (sa, sb):
            buf[0:PAD, :] = jnp.zeros((PAD, D_POOL), F32)
        pooled = _pooled(u_ref[...], sa, sb)
        y_ref[...] = _mm(pooled, w_ref[...]) * s_ref[...]

    vm = lambda shape: pl.BlockSpec(shape, lambda i: (0,) * len(shape))
    return _pallas_call(
        body, name=name, grid=(1,),
        in_specs=[pl.BlockSpec((S, 256), lambda i: (0, 7)), vm((256, 256)), vm((1, 256))],
        out_specs=vm((S, 256)), out_shape=_sds((S, 256), F32),
        scratch_shapes=[pltpu.VMEM((S + 2 * PAD, D_POOL), F32)] * 2,
        compiler_params=_cp(40, ("arbitrary",)))(z, pw, psc)


def pool_bwd(dy, z, pw, psc, name):
    def body(dy_ref, u_ref, w_ref, s_ref, du_ref, dw_ref, ds_ref, sa, sb):
        for buf in (sa, sb):
            buf[0:PAD, :] = jnp.zeros((PAD, D_POOL), F32)
            buf[PAD + S:PAD + S + PAD, :] = jnp.zeros((PAD, D_POOL), F32)
        pooled = _pooled(u_ref[...], sa, sb)
        dyv = dy_ref[...]
        w = w_ref[...]
        ds_ref[...] = jnp.sum(dyv * _mm(pooled, w), axis=0, keepdims=True)
        dyl = dyv * s_ref[...]
        dw_ref[...] = _mm_tn(pooled, dyl)
        dpool = _mm_nt(dyl, w)
        cur = dpool / _pool_counts()
        sums = []
        bufs = (sa, sb)
        for k, sh in enumerate((1, 2, 4, 8)):
            buf = bufs[k % 2]
            buf[PAD:PAD + S, :] = cur
            cur = cur + buf[PAD + sh:PAD + sh + S, :]
            sums.append(cur)
        du_ref[...] = _pool_lane_select(sums) - dpool

    vm = lambda shape: pl.BlockSpec(shape, lambda i: (0,) * len(shape))
    return _pallas_call(
        body, name=name, grid=(1,),
        in_specs=[vm((S, 256)), pl.BlockSpec((S, 256), lambda i: (0, 7)), vm((256, 256)), vm((1, 256))],
        out_specs=(vm((S, 256)), vm((256, 256)), vm((1, 256))),
        out_shape=(_sds((S, 256), F32), _sds((256, 256), F32), _sds((1, 256), F32)),
        scratch_shapes=[pltpu.VMEM((S + 2 * PAD, D_POOL), F32)] * 2,
        compiler_params=_cp(40, ("arbitrary",)))(dy, z, pw, psc)


def ada_fwd(c_all, ada_w, ada_b_cols):
    def body(c_ref, w_ref, b_ref, o_ref):
        c = c_ref[...]
        cond = c * jax.nn.sigmoid(c)
        o_ref[...] = jnp.dot(cond, w_ref[...], preferred_element_type=F32, precision=lax.Precision.HIGHEST) + b_ref[...]

    return _pallas_call(
        body, name="ada_fwd", grid=(DEPTH,),
        in_specs=[pl.BlockSpec((NDEV, D), lambda l: (0, 0)), pl.BlockSpec((None, D, 1152), lambda l: (l, 0, 0)),
                  pl.BlockSpec((None, 1, 1152), lambda l: (l, 0, 0))],
        out_specs=pl.BlockSpec((None, NDEV, 1152), lambda l: (l, 0, 0)), out_shape=_sds((DEPTH, NDEV, 1152), F32),
        compiler_params=_cp(40, ("arbitrary",)))(c_all, ada_w, ada_b_cols)


def ada_bwd_adam(c_all, dmod_cols, w, m, v, tr):
    def body(c_ref, dm_ref, w_ref, m_ref, v_ref, g_ref, d_ref, m2_ref, v2_ref):
        c = c_ref[...]
        cond = c * jax.nn.sigmoid(c)
        g = lax.dot_general(cond, dm_ref[...], (((0,), (0,)), ((), ())), preferred_element_type=F32,
                            precision=lax.Precision.HIGHEST)
        g_ref[...] = g
        d_ref[...], m2_ref[...], v2_ref[...] = _adamw(w_ref[...], g, m_ref[...], v_ref[...])

    rs = pl.BlockSpec((None, tr, 1152), lambda l, i: (l, i, 0))
    return _pallas_call(
        body, name="ada_bwd_adam", grid=(DEPTH, D // tr),
        in_specs=[pl.BlockSpec((NDEV, tr), lambda l, i: (0, i)), pl.BlockSpec((None, NDEV, 1152), lambda l, i: (l, 0, 0)), rs, rs, rs],
        out_specs=(rs, rs, rs, rs), out_shape=tuple(_sds((DEPTH, D, 1152), F32) for _ in range(4)),
        compiler_params=_cp(48, ("arbitrary", "arbitrary")))(c_all, dmod_cols, w, m, v)


def _adamw(w, g, m, v):
    m2 = B1 * m + (1.0 - B1) * g
    v2 = B2 * v + (1.0 - B2) * (g * g)
    m_hat = m2 / (1.0 - B1 ** STEP)
    v_hat = v2 / (1.0 - B2 ** STEP)
    return -LR * (m_hat / (jnp.sqrt(v_hat) + EPS) + WD * w), m2, v2


def _sum8(ref):
    g = ref[0].astype(F32)
    for s in range(1, ref.shape[0]):
        g = g + ref[s].astype(F32)
    return g


def adam_rs(recvs, w, m, v, tr, name):
    nlay, r, cdim = w.shape
    cp = recvs[0].shape[-1]
    steps = r // tr

    def body(*refs):
        rc_refs, (w_ref, m_ref, v_ref, g_ref, d_ref, m2_ref, v2_ref) = refs[:nlay], refs[nlay:]
        for k in range(nlay):
            @pl.when(pl.program_id(0) == k)
            def _(k=k):
                g = _sum8(rc_refs[k])[:, :cdim]
                g_ref[...] = g
                d_ref[...], m2_ref[...], v2_ref[...] = _adamw(w_ref[...], g, m_ref[...], v_ref[...])

    rc = lambda k: pl.BlockSpec((NDEV, tr, cp), lambda l, i: (0, jnp.where(l == k, i, jnp.where(l < k, 0, steps - 1)), 0))
    rs = pl.BlockSpec((None, tr, cdim), lambda l, i: (l, i, 0))
    return _call(
        body, name=name, grid=(nlay, steps), in_specs=[rc(k) for k in range(nlay)] + [rs, rs, rs],
        out_specs=(rs, rs, rs, rs), out_shape=tuple(_sds(w.shape, F32) for _ in range(4)),
        cp=_cp(48, ("arbitrary", "arbitrary")), args=(*recvs, w, m, v))[0]


def adam_block(recv, w, m, v, lf, prev, name):
    half = FB // 2

    def body(*refs):
        rc_ref, w_ref, m_ref, v_ref = refs[:4]
        g_ref, d_ref, m2_ref, v2_ref = refs[-4:]
        g = _sum8(rc_ref)
        g_ref[...] = g
        d_ref[...], m2_ref[...], v2_ref[...] = _adamw(w_ref[...], g, m_ref[...], v_ref[...])

    rs = pl.BlockSpec((None, None, half, D), lambda i: (lf // 2, lf % 2, i, 0))
    prev = list(prev) if prev is not None else []
    return list(_pallas_call(
        body, name=name, grid=(2,), in_specs=[pl.BlockSpec((recv.shape[0], half, D), lambda i: (0, i, 0)), rs, rs, rs] + [ANY] * len(prev),
        out_specs=(rs, rs, rs, rs), out_shape=tuple(_sds((DEPTH, 2, FB, D), F32) for _ in range(4)),
        input_output_aliases={4 + k: k for k in range(len(prev))},
        compiler_params=_cp(48, ("arbitrary",)))(recv, w, m, v, *prev))


def adam_native(gs, ws, ms, vs, name):
    n = len(ws)

    def body(*refs):
        g_refs, w_refs, m_refs, v_refs = (refs[k * n:(k + 1) * n] for k in range(4))
        d_refs, m2_refs, v2_refs = (refs[(4 + k) * n:(5 + k) * n] for k in range(3))
        for a in range(n):
            d_refs[a][...], m2_refs[a][...], v2_refs[a][...] = _adamw(w_refs[a][...], g_refs[a][...], m_refs[a][...], v_refs[a][...])

    whole = lambda a: pl.BlockSpec(a.shape, lambda i, nd=len(a.shape): (0,) * nd)
    outs = _pallas_call(body, name=name, grid=(1,), in_specs=[whole(a) for a in (*gs, *ws, *ms, *vs)],
                        out_specs=tuple(whole(w) for w in ws) * 3,
                        out_shape=tuple(_sds(w.shape, F32) for w in ws) * 3, compiler_params=_cp(40),
                        pin_all=True)(*gs, *ws, *ms, *vs)
    return outs[:n], outs[n:2 * n], outs[2 * n:]


def sum_sources(recv, name):
    r = recv.shape[1]

    def body(rc_ref, o_ref):
        o_ref[...] = _sum8(rc_ref)

    vm = pl.BlockSpec(memory_space=pltpu.VMEM)
    return _pallas_call(body, name=name, in_specs=[vm], out_specs=vm, out_shape=_sds((r, 128), F32),
                          compiler_params=_cp(40))(recv)


def _pack(arrs, dtype=F32):
    flat = jnp.concatenate([a.reshape(-1) for a in arrs]).astype(dtype)
    n = flat.shape[0]
    tile = 128 * (32 // jnp.dtype(dtype).itemsize)
    rows = -(-n // tile) * (tile // 128)
    return jnp.pad(flat, (0, rows * 128 - n)).reshape(rows, 128)


def _unpack(vec, shapes):
    flat = vec.reshape(-1)
    out, o = [], 0
    for sh in shapes:
        n = int(np.prod(sh))
        out.append(flat[o:o + n].reshape(sh))
        o += n
    return out


WEIGHTS = ['rel_bias', 'ada_w', 'ada_b', 'ln_g', 'ln_b', 'ffn_w_gate', 'ffn_w_up', 'ffn_w_down', 'w_in', 'w_out',
           'ssm_a_re', 'ssm_a_im', 'ssm_log_dt', 'ssm_b_re', 'ssm_b_im', 'ssm_c_re', 'ssm_c_im', 'ssm_d', 'glu_w',
           'glu_b', 'pool_w', 'pool_scale']
SMALL = ['rel_bias', 'ada_b', 'ln_g', 'ln_b', 'ssm_a_re', 'ssm_a_im', 'ssm_log_dt', 'ssm_b_re', 'ssm_b_im',
         'ssm_c_re', 'ssm_c_im', 'ssm_d', 'glu_b', 'pool_w', 'pool_scale']
SMALL_EXACT = ['rel_bias', 'ln_g', 'ln_b', 'ssm_a_re', 'ssm_a_im', 'ssm_log_dt']
SMALL_ROUNDED = ['ada_b', 'ssm_b_re', 'ssm_b_im', 'ssm_c_re', 'ssm_c_im', 'ssm_d', 'glu_b', 'pool_w', 'pool_scale']
SMALL_FULL_SHAPES = {'rel_bias': (32, 8), 'ada_b': (2, 9216), 'ln_g': (2, 3, 1024), 'ln_b': (2, 3, 1024),
                     'ssm_a_re': (2, 16, 64), 'ssm_a_im': (2, 16, 64), 'ssm_log_dt': (2, 16),
                     'ssm_b_re': (2, 16, 64, 16), 'ssm_b_im': (2, 16, 64, 16), 'ssm_c_re': (2, 16, 16, 64),
                     'ssm_c_im': (2, 16, 16, 64), 'ssm_d': (2, 256), 'glu_b': (2, 256), 'pool_w': (2, 4, 64, 64),
                     'pool_scale': (2, 256)}


def _step(P):
    me = _me()
    x0 = P['x'][0]
    target = P['loss_target'][0]

    bf = lambda a: a.astype(BF16)
    padr = lambda a: jnp.pad(bf(a), ((0, 0), (0, 0), (0, FBP - FB), (0, 0)))
    ffn_b = [padr(jnp.swapaxes(P['ffn_w_gate'], 2, 3)), padr(jnp.swapaxes(P['ffn_w_up'], 2, 3)), padr(P['ffn_w_down'])]
    mix_b = [bf(P['w_in']), bf(P['w_out']), bf(P['glu_w'])]

    def shards(l, sub):
        return [t[l] for t in mix_b] if sub == 1 else [t[l, sub // 2] for t in ffn_b]

    order = [(l, sub) for l in range(DEPTH) for sub in range(3)]
    nxt = dict(zip(order[:-1], order[1:]))
    W = {key: [None] * 3 for key in order}
    c_all, lng_all, lnb_all, *W[order[0]] = _exchange(Gather([P['c'], P['ln_g'], P['ln_b']] + shards(*order[0])), "gather_first")
    gather_queue = [(key, pos, a) for key in order[1:] for pos, a in enumerate(shards(*key))]

    def gather_ride(cap_us, must=None):
        units, used = [], 0.0
        while gather_queue:
            key, _, a = gather_queue[0]
            cost = a.size * a.dtype.itemsize * GATHER_US_PER_BYTE
            if key != must and used + cost / 2 > cap_us:
                break
            units.append(gather_queue.pop(0))
            used += cost
        return (Gather([a for _, _, a in units]) if units else None), units

    def gathered(units, outs):
        for (key, pos, _), o in zip(units, outs):
            W[key][pos] = o

    c_all = c_all.reshape(NDEV, D)
    ln_g = jnp.transpose(lng_all, (1, 2, 0, 3)).reshape(DEPTH, 3, D)
    ln_b = jnp.transpose(lnb_all, (1, 2, 0, 3)).reshape(DEPTH, 3, D)

    ada_b_cols = lax.dynamic_slice_in_dim(P['ada_b'], me * 1152, 1152, axis=1).reshape(DEPTH, 1, 1152)
    modc = ada_fwd(c_all, P['ada_w'], ada_b_cols)
    (mod_all,) = _exchange(Gather([modc]), "gather_mod")
    mod_me = lax.dynamic_index_in_dim(mod_all, me, axis=2, keepdims=False)
    mod = jnp.transpose(mod_me, (1, 0, 2)).reshape(DEPTH, 9, D)

    bias = att_bias(P['rel_bias'])
    ssm = []
    for l in range(DEPTH):
        prm = (P['ssm_a_re'][l], P['ssm_a_im'][l], P['ssm_log_dt'][l], P['ssm_b_re'][l], P['ssm_b_im'][l])
        (abr, abi, bbr, bbi), disc_vjp = jax.vjp(_ssm_discretise, *prm)
        ssm.append(dict(
            vjp=disc_vjp, a=jnp.stack([abr.reshape(8, 128), abi.reshape(8, 128)]),
            bre=_blockdiag(jnp.transpose(bbr, (0, 2, 1))).astype(MXU), bim=_blockdiag(jnp.transpose(bbi, (0, 2, 1))).astype(MXU),
            cre=_blockdiag(jnp.transpose(P['ssm_c_re'][l], (0, 2, 1))).astype(MXU),
            cim=_blockdiag(jnp.transpose(P['ssm_c_im'][l], (0, 2, 1))).astype(MXU),
            d=P['ssm_d'][l].reshape(1, 256), gb=P['glu_b'][l].reshape(1, 256),
            pw=_blockdiag(P['pool_w'][l]).astype(MXU), psc=P['pool_scale'][l].reshape(1, 256)))

    saved = []
    x = x0
    h = ln_mod_fwd(x, mod[0], 0, "ln_mod_fwd_l0s0")
    for l, sub in order:
        tag = f"l{l}s{sub}"
        after = (mod[nxt[(l, sub)][0]], nxt[(l, sub)][1]) if (l, sub) in nxt else None
        if sub != 1:
            wg, wu, wd = (t.reshape(NDEV * FBP, D) for t in W[(l, sub)])
            ride, units = gather_ride(60, nxt.get((l, sub)))
            (G, U, fo), got = ffn_fwd(h, wg, wu, wd, "ffn_fwd_" + tag, ride)
            gathered(units, got)
            saved.append(dict(x=x, h=h, G=G, U=U, f=fo))
            x, *hn = res_ln_fwd(x, fo, mod[l], sub, ln_g[l], ln_b[l], 0.5, "res_ln_fwd_" + tag, after)
        else:
            sp = ssm[l]
            win, wout, gluw = W[(l, sub)]
            ride, units = gather_ride(15)
            (z,), got = win_fwd(h, win, "win_fwd_" + tag, ride)
            gathered(units, got)
            ride, units = gather_ride(55)
            (ya, lse), got = att_fwd(z, bias, "att_fwd_" + tag, ride)
            gathered(units, got)
            ride, units = gather_ride(35)
            (ys, ypre, st), got = ssm_fwd(z, sp['a'], sp['bre'], sp['bim'], sp['cre'], sp['cim'], sp['d'], gluw, sp['gb'],
                                          "ssm_fwd_" + tag, ride)
            gathered(units, got)
            yp = pool_fwd(z, sp['pw'], sp['psc'], "pool_fwd_" + tag)
            ride, units = gather_ride(12, nxt.get((l, sub)))
            (o,), got = wout_fwd(ya, ys, yp, wout, "wout_fwd_" + tag, ride)
            gathered(units, got)
            saved.append(dict(x=x, h=h, z=z, ya=ya, lse=lse, ys=ys, ypre=ypre, st=st, yp=yp, f=o))
            x, *hn = res_ln_fwd(x, o, mod[l], sub, ln_g[l], ln_b[l], 1.0, "res_ln_fwd_" + tag, after)
        h = hn[0] if hn else None
    assert not gather_queue

    loss_tile, dx = loss_fwd_bwd(x, target, "loss")
    loss = lax.psum(loss_tile[0, 0], ("x", "y", "c"))

    flights = []

    dmod = [[None] * 9 for _ in range(DEPTH)]
    dlng = [[None] * 3 for _ in range(DEPTH)]
    dlnb = [[None] * 3 for _ in range(DEPTH)]
    dbiases = [None] * DEPTH
    small_l = [dict() for _ in range(DEPTH)]
    for l, sub in reversed(order):
        tag = f"l{l}s{sub}"
        sv = saved[3 * l + sub]
        if (l, sub) == order[-1]:
            dxa, df, sums = res_ln_bwd(sv['x'], sv['f'], mod[l], sub, ln_g[l], dx, 0.5, "res_ln_bwd_" + tag)
        dlng[l][sub], dlnb[l][sub], dmod[l][3 * sub + 2] = sums[0], sums[1], sums[2]
        if sub != 1:
            f = sub // 2
            wg, wu, wd = (t.reshape(NDEV * FBP, D) for t in W[(l, sub)])
            dwg, dwu, dwd, dh = ffn_bwd(df, sv['h'], sv['G'], sv['U'], wg, wu, wd, "ffn_bwd_" + tag)
            if (l, sub) == order[0]:
                handle, zero = scatter_start([t.reshape(NDEV, FBP, D) for t in (dwg, dwu)], "scatter_start_" + tag, rows=FB)
            else:
                handle, zero = scatter_start([t.reshape(NDEV, FBP, D) for t in (dwg, dwu, dwd)], "scatter_start_" + tag, rows=FB)
            flights.append(((l, sub), handle))
        else:
            sp = ssm[l]
            win, wout, gluw = W[(l, sub)]
            dya, dys, dyp, dwout = wout_bwd(df, sv['ya'], sv['ys'], sv['yp'], wout, "wout_bwd_" + tag)
            dq, dk, dv, dbiases[l] = att_bwd(sv['z'], bias, sv['ya'], sv['lse'], dya, "att_bwd_" + tag)
            dus, dbre, dbim, dcre, dcim, da, dd, dgw, dgb = ssm_bwd(
                dys, sv['z'], sv['ypre'], sv['st'], sp['a'], sp['bre'], sp['bim'], sp['cre'], sp['cim'], sp['d'],
                gluw, sp['gb'], "ssm_bwd_" + tag)
            dup, dpw, dpsc = pool_bwd(dyp, sv['z'], sp['pw'], sp['psc'], "pool_bwd_" + tag)
            dh, dwin = win_bwd((dq, dk, dv, dus, dup), sv['h'], win, "win_bwd_" + tag)
            handle, zero = scatter_start([dwin, dwout, dgw.astype(BF16).reshape(NDEV, 32, 256)], "scatter_start_" + tag)
            flights.append(((l, sub), handle))
            d_are, d_aim, d_ldt, d_bre, d_bim = sp['vjp']((
                da[0].reshape(16, 64), da[1].reshape(16, 64),
                jnp.transpose(_blockdiag_take(dbre, 16, 64), (0, 2, 1)), jnp.transpose(_blockdiag_take(dbim, 16, 64), (0, 2, 1))))
            small_l[l] = dict(
                ssm_a_re=d_are, ssm_a_im=d_aim, ssm_log_dt=d_ldt, ssm_b_re=d_bre, ssm_b_im=d_bim,
                ssm_c_re=jnp.transpose(_blockdiag_take(dcre, 64, 16), (0, 2, 1)),
                ssm_c_im=jnp.transpose(_blockdiag_take(dcim, 64, 16), (0, 2, 1)),
                ssm_d=dd.reshape(256), glu_b=dgb.reshape(256), pool_w=_blockdiag_take(dpw, 64, 64), pool_scale=dpsc.reshape(256))
        if (l, sub) == order[0]:
            dx, sums2 = ln_mod_bwd(sv['x'], dh, mod[l] + zero, sub, dxa, "ln_mod_bwd_" + tag)
        else:
            lp, sp_ = order[order.index((l, sub)) - 1]
            svp = saved[3 * lp + sp_]
            dxa, df, sums, sums2 = ln_join_bwd(svp['x'], svp['f'], mod[lp], sp_, ln_g[lp], ln_b[lp], 1.0 if sp_ == 1 else 0.5,
                                               dh, mod[l] + zero, sub, dxa, "ln_join_bwd_" + tag)
        dmod[l][3 * sub], dmod[l][3 * sub + 1] = sums2[0], sums2[1]
    grad_x = dx[None]

    small = {k: jnp.stack([small_l[l][k] for l in range(DEPTH)]) for k in small_l[0]}
    small['rel_bias'] = relbias_grad(dbiases)
    small['ada_b'] = jnp.stack([jnp.stack(dmod[l]).reshape(9 * D) for l in range(DEPTH)])
    small['ln_g'] = jnp.stack([jnp.stack(dlng[l]) for l in range(DEPTH)])
    small['ln_b'] = jnp.stack([jnp.stack(dlnb[l]) for l in range(DEPTH)])
    swaps = {'rel_bias': (0, 1), 'ln_g': (0, 1), 'ln_b': (0, 1), 'ssm_b_re': (2, 3), 'ssm_b_im': (2, 3)}
    view = lambda k, t: jnp.swapaxes(t, *swaps[k]) if k in swaps else t
    kept_shape = lambda k: np.swapaxes(np.empty(SMALL_FULL_SHAPES[k], np.bool_), *swaps.get(k, (0, 0))).shape
    small_flight, _ = scatter_start([_pack([view(k, small[k]) for k in SMALL_EXACT]),
                                     _pack([view(k, small[k]) for k in SMALL_ROUNDED], BF16)], "gather_small_start", whole=True)
    down_flight, _ = scatter_start([dwd.reshape(NDEV, FBP, D)], "scatter_start_down", rows=FB, after=[small_flight[1][0]])

    out = {}

    def put(name, g, d, m2, v2, shape):
        out['grad_' + name], out['delta_' + name] = g.reshape(shape), d.reshape(shape)
        out['new_m_' + name], out['new_v_' + name] = m2.reshape(shape), v2.reshape(shape)

    def wmv(name):
        return [P[pre + name] for pre in ('', 'm_', 'v_')]

    recv = {}
    started_last = down_flight[1][0]
    for key, handle in flights[:-1]:
        recv[key] = scatter_wait(handle, started_last, "scatter_wait_l%ds%d" % key)
    mixer_done = []
    for pos, (name, tr) in enumerate((('w_in', 512), ('w_out', 128), ('glu_w', 32))):
        res = adam_rs([recv[(l, 1)][pos] for l in range(DEPTH)], *wmv(name), tr, "adam_" + name)
        mixer_done.append(res[0])
        put(name, *res, P[name].shape)
    ffn = (('ffn_w_gate', [jnp.swapaxes(t, 2, 3) for t in wmv('ffn_w_gate')]),
           ('ffn_w_up', [jnp.swapaxes(t, 2, 3) for t in wmv('ffn_w_up')]), ('ffn_w_down', wmv('ffn_w_down')))
    part = [None] * 3
    for l, sub in [key for key, _ in flights[:-1] if key[1] != 1]:
        for pos, (name, ops) in enumerate(ffn):
            part[pos] = adam_block(recv[(l, sub)][pos], *ops, 2 * l + sub // 2, part[pos], f"adam_{name}_l{l}s{sub}")

    (l, sub), handle = flights[-1]
    last = scatter_wait(handle, [p[0] for p in part] + mixer_done[:2], "scatter_wait_l%ds%d" % (l, sub))
    for pos, (name, ops) in enumerate(ffn[:2]):
        part[pos] = adam_block(last[pos], *ops, 2 * l + sub // 2, part[pos], f"adam_{name}_l{l}s{sub}")
        put(name, *[jnp.swapaxes(t, 2, 3) for t in part[pos]], P[name].shape)
    exact_all, rounded_all = scatter_wait(small_flight, [p[0] for p in part], "gather_small_wait")
    gsum = dict(zip(SMALL_EXACT, _unpack(sum_sources(exact_all, "sum_small_exact"), [kept_shape(k) for k in SMALL_EXACT])))
    gsum.update(zip(SMALL_ROUNDED, _unpack(sum_sources(rounded_all, "sum_small_rounded"), [kept_shape(k) for k in SMALL_ROUNDED])))
    dmod_cols = jnp.stack([lax.dynamic_slice_in_dim(rounded_all, 72 * l + 9 * me, 9, axis=1).astype(F32).reshape(NDEV, 1152)
                           for l in range(DEPTH)])
    ada_res = ada_bwd_adam(c_all, dmod_cols, *wmv('ada_w'), 256)
    put('ada_w', *ada_res, P['ada_w'].shape)

    for k in ('ln_g', 'ln_b'):
        gsum[k] = lax.dynamic_slice_in_dim(gsum[k], me * 128, 128, axis=2)
    ds_, m2s, v2s = adam_native([gsum[k] for k in SMALL], *[[view(k, P[pre + k]) for k in SMALL] for pre in ('', 'm_', 'v_')],
                                "adam_small")
    for k, d, m2, v2 in zip(SMALL, ds_, m2s, v2s):
        put(k, view(k, gsum[k]), view(k, d), view(k, m2), view(k, v2), P[k].shape)

    (down,) = scatter_wait(down_flight, [ada_res[1], ds_[SMALL.index('ada_b')]], "scatter_wait_down")
    name, ops = ffn[2]
    put(name, *adam_block(down, *ops, 2 * l + sub // 2, part[2], f"adam_{name}_l{l}s{sub}"), P[name].shape)

    res = [loss, grad_x]
    for pre in ('grad_', 'delta_', 'new_m_', 'new_v_'):
        res += [out[pre + k] for k in WEIGHTS]
    return tuple(res)


def kernel(x, c, rel_bias, ada_w, ada_b, ln_g, ln_b, ffn_w_gate, ffn_w_up, ffn_w_down, w_in, w_out, ssm_a_re, ssm_a_im, ssm_log_dt, ssm_b_re, ssm_b_im, ssm_c_re, ssm_c_im, ssm_d, glu_w, glu_b, pool_w, pool_scale, loss_target, m_rel_bias, m_ada_w, m_ada_b, m_ln_g, m_ln_b, m_ffn_w_gate, m_ffn_w_up, m_ffn_w_down, m_w_in, m_w_out, m_ssm_a_re, m_ssm_a_im, m_ssm_log_dt, m_ssm_b_re, m_ssm_b_im, m_ssm_c_re, m_ssm_c_im, m_ssm_d, m_glu_w, m_glu_b, m_pool_w, m_pool_scale, v_rel_bias, v_ada_w, v_ada_b, v_ln_g, v_ln_b, v_ffn_w_gate, v_ffn_w_up, v_ffn_w_down, v_w_in, v_w_out, v_ssm_a_re, v_ssm_a_im, v_ssm_log_dt, v_ssm_b_re, v_ssm_b_im, v_ssm_c_re, v_ssm_c_im, v_ssm_d, v_glu_w, v_glu_b, v_pool_w, v_pool_scale):
    return _step(dict(locals()))
```
